```python
import math
import jax, jax.numpy as jnp
from jax import lax
import numpy as np

D_MODEL = 1024
BATCH = 8
SEQ = 4096
DEPTH = 1

DN_HEADS = 4
DN_HEAD_DIM = 128
DN_WIDTH = DN_HEADS * DN_HEAD_DIM
DN_QKV_WIDTH = 3 * DN_WIDTH
DN_CONV = 4
DN_CHUNK = 64
SWA_HEADS = 8
SWA_KV_HEADS = 2
SWA_HEAD_DIM = 64
SWA_GROUP = SWA_HEADS // SWA_KV_HEADS
SWA_WIDTH = SWA_HEADS * SWA_HEAD_DIM
SWA_KV_WIDTH = SWA_KV_HEADS * SWA_HEAD_DIM
WINDOW = 128
SWA_BLOCK = 128
REL_BUCKETS = 32
REL_MAX_DIST = 128
N_BRANCHES = 2
D_FF = ((8 * D_MODEL // 3 + 255) // 256) * 256
D_IN = DN_QKV_WIDTH + DN_WIDTH + 2 * DN_HEADS + SWA_WIDTH + 2 * SWA_KV_WIDTH + N_BRANCHES * D_MODEL
EPS = 1e-6

kernel_name = "hybrid_gdn_swa_gated_merge"


def rms_norm(x, gain):
    xf = x.astype(jnp.float32)
    y = xf * lax.rsqrt(jnp.mean(xf * xf, axis=-1, keepdims=True) + EPS)
    return (y * gain.astype(jnp.float32)).astype(x.dtype)


def l2_norm(x):
    xf = x.astype(jnp.float32)
    return xf * lax.rsqrt(jnp.sum(xf * xf, axis=-1, keepdims=True) + EPS)


def causal_dwconv(x, w):
    k = w.shape[0]
    return lax.conv_general_dilated(
        x, w[:, None, :], window_strides=(1,), padding=((k - 1, 0),),
        dimension_numbers=("NWC", "WIO", "NWC"), feature_group_count=x.shape[-1])


def chunk_gated_delta_rule(q, k, v, g, beta):
    B, S, H, DK = q.shape
    DV = v.shape[-1]
    C = DN_CHUNK
    NC = S // C
    q = q * (DK ** -0.5)

    def to_chunks(t):
        return t.reshape(B, NC, C, H, t.shape[-1]).transpose(0, 3, 1, 2, 4)

    qc, kc, vc = to_chunks(q), to_chunks(k), to_chunks(v)
    gc = jnp.cumsum(g.reshape(B, NC, C, H).transpose(0, 3, 1, 2), axis=-1)
    bc = beta.reshape(B, NC, C, H).transpose(0, 3, 1, 2)[..., None]
    k_beta = kc * bc
    v_beta = vc * bc

    idx = jnp.arange(C)
    lower_incl = idx[:, None] >= idx[None, :]
    strict_lower = idx[:, None] > idx[None, :]
    decay = jnp.exp(jnp.where(lower_incl, gc[..., :, None] - gc[..., None, :], -jnp.inf))

    a_mat = jnp.where(strict_lower, jnp.einsum("bhncd,bhnsd->bhncs", k_beta, kc) * decay, 0.0)
    lhs = a_mat + jnp.eye(C, dtype=a_mat.dtype)
    rhs = jnp.concatenate([v_beta, k_beta * jnp.exp(gc)[..., None]], axis=-1)
    sol = lax.linalg.triangular_solve(lhs, rhs, left_side=True, lower=True, unit_diagonal=True)
    u, w = sol[..., :DV], sol[..., DV:]
    qk = jnp.einsum("bhncd,bhnsd->bhncs", qc, kc) * decay

    def step(state, inp):
        q_i, k_i, u_i, w_i, g_i, qk_i = inp
        v_new = u_i - jnp.einsum("bhck,bhkv->bhcv", w_i, state)
        o_i = (jnp.einsum("bhck,bhkv->bhcv", q_i * jnp.exp(g_i)[..., None], state)
               + jnp.einsum("bhcs,bhsv->bhcv", qk_i, v_new))
        g_last = g_i[..., -1]
        k_dec = k_i * jnp.exp(g_last[..., None] - g_i)[..., None]
        state = state * jnp.exp(g_last)[..., None, None] + jnp.einsum("bhck,bhcv->bhkv", k_dec, v_new)
        return state, o_i

    xs = tuple(jnp.moveaxis(t, 2, 0) for t in (qc, kc, u, w, gc, qk))
    state0 = jnp.zeros((B, H, DK, DV), jnp.float32)
    _, o = lax.scan(step, state0, xs)
    return o.transpose(1, 0, 3, 2, 4).reshape(B, S, H, DV)


def gated_deltanet_branch(qkv, z, b_raw, a_raw, conv_w, a_log, dt_bias, out_gain):
    B, S, _ = qkv.shape
    qkv = jax.nn.silu(causal_dwconv(qkv, conv_w))
    q, k, v = jnp.split(qkv, 3, axis=-1)
    q = l2_norm(q.reshape(B, S, DN_HEADS, DN_HEAD_DIM))
    k = l2_norm(k.reshape(B, S, DN_HEADS, DN_HEAD_DIM))
    v = v.reshape(B, S, DN_HEADS, DN_HEAD_DIM).astype(jnp.float32)
    beta = jax.nn.sigmoid(b_raw.astype(jnp.float32))
    g = -jnp.exp(a_log.astype(jnp.float32)) * jax.nn.softplus(a_raw.astype(jnp.float32) + dt_bias.astype(jnp.float32))
    o = chunk_gated_delta_rule(q, k, v, g, beta)
    o = rms_norm(o, out_gain) * jax.nn.silu(z.reshape(B, S, DN_HEADS, DN_HEAD_DIM).astype(jnp.float32))
    return o.reshape(B, S, DN_WIDTH).astype(qkv.dtype)


def t5_causal_bucket(dist):
    n = jnp.maximum(dist, 0)
    max_exact = REL_BUCKETS // 2
    nf = jnp.maximum(n, 1).astype(jnp.float32)
    large = max_exact + (jnp.log(nf / max_exact) / math.log(REL_MAX_DIST / max_exact)
                         * (REL_BUCKETS - max_exact)).astype(jnp.int32)
    large = jnp.minimum(large, REL_BUCKETS - 1)
    return jnp.where(n < max_exact, n, large)


def sliding_window_branch(q, k, v, q_gain, k_gain, sinks, rel_bias):
    B, S, _ = q.shape
    NB = S // SWA_BLOCK
    q = rms_norm(q.reshape(B, S, SWA_HEADS, SWA_HEAD_DIM), q_gain)
    k = rms_norm(k.reshape(B, S, SWA_KV_HEADS, SWA_HEAD_DIM), k_gain)
    v = v.reshape(B, S, SWA_KV_HEADS, SWA_HEAD_DIM)
    qb = q.reshape(B, NB, SWA_BLOCK, SWA_KV_HEADS, SWA_GROUP, SWA_HEAD_DIM).astype(jnp.float32)

    def band(t):
        tp = jnp.pad(t, ((0, 0), (SWA_BLOCK, 0), (0, 0), (0, 0)))
        tp = tp.reshape(B, NB + 1, SWA_BLOCK, SWA_KV_HEADS, SWA_HEAD_DIM)
        return jnp.concatenate([tp[:, :-1], tp[:, 1:]], axis=2).astype(jnp.float32)

    kb, vb = band(k), band(v)
    logits = jnp.einsum("bnqkgd,bnskd->bnkgqs", qb, kb) * (SWA_HEAD_DIM ** -0.5)

    qi = jnp.arange(SWA_BLOCK)[:, None]
    kj = jnp.arange(2 * SWA_BLOCK)[None, :]
    dist = SWA_BLOCK + qi - kj
    in_window = (dist >= 0) & (dist < WINDOW)
    key_pos = (jnp.arange(NB)[:, None] - 1) * SWA_BLOCK + jnp.arange(2 * SWA_BLOCK)[None, :]
    mask = in_window[None, :, :] & (key_pos >= 0)[:, None, :]

    bias = rel_bias.astype(jnp.float32)[t5_causal_bucket(dist)]
    bias = bias.transpose(2, 0, 1).reshape(SWA_KV_HEADS, SWA_GROUP, SWA_BLOCK, 2 * SWA_BLOCK)
    logits = jnp.where(mask[None, :, None, None], logits + bias, -jnp.inf)

    sink = sinks.astype(jnp.float32).reshape(SWA_KV_HEADS, SWA_GROUP)[None, None, :, :, None, None]
    m = jnp.maximum(jnp.max(logits, axis=-1, keepdims=True), sink)
    p = jnp.exp(logits - m)
    denom = jnp.sum(p, axis=-1, keepdims=True) + jnp.exp(sink - m)
    out = jnp.einsum("bnkgqs,bnskd->bnqkgd", p / denom, vb)
    return out.reshape(B, S, SWA_WIDTH).astype(q.dtype)


def split_combined(p):
    sizes = (DN_QKV_WIDTH, DN_WIDTH, DN_HEADS, DN_HEADS, SWA_WIDTH, SWA_KV_WIDTH, SWA_KV_WIDTH,
             N_BRANCHES * D_MODEL)
    return jnp.split(p, np.cumsum(sizes)[:-1].tolist(), axis=-1)


def _fwd_setup_inputs(seed: int = 0) -> dict:
    key = jax.random.key(seed)
    ks = jax.random.split(key, 20)
    f32 = jnp.float32
    L = DEPTH

    def nrm(k, shape, scale):
        return jax.random.normal(k, shape, f32) * scale

    dt = jnp.exp(jax.random.uniform(ks[5], (L, DN_HEADS), f32, math.log(1e-3), math.log(1e-1)))
    return {
        "x": nrm(ks[0], (BATCH, SEQ, D_MODEL), 1.0),
        "attn_norm": 1.0 + nrm(ks[1], (L, D_MODEL), 0.02),
        "w_in": nrm(ks[2], (L, D_MODEL, D_IN), D_MODEL ** -0.5),
        "dn_conv": nrm(ks[3], (L, DN_CONV, DN_QKV_WIDTH), DN_CONV ** -0.5),
        "dn_a_log": jnp.log(jax.random.uniform(ks[4], (L, DN_HEADS), f32, 1.0, 16.0)),
        "dn_dt_bias": dt + jnp.log(-jnp.expm1(-dt)),
        "dn_out_norm": 1.0 + nrm(ks[6], (L, DN_HEAD_DIM), 0.02),
        "swa_q_norm": 1.0 + nrm(ks[7], (L, SWA_HEAD_DIM), 0.02),
        "swa_k_norm": 1.0 + nrm(ks[8], (L, SWA_HEAD_DIM), 0.02),
        "swa_sinks": nrm(ks[9], (L, SWA_HEADS), 0.5),
        "rel_bias": nrm(ks[10], (REL_BUCKETS, SWA_HEADS), 0.1),
        "w_branch_dn": nrm(ks[11], (L, DN_WIDTH, D_MODEL), DN_WIDTH ** -0.5),
        "w_branch_swa": nrm(ks[12], (L, SWA_WIDTH, D_MODEL), SWA_WIDTH ** -0.5),
        "w_out": nrm(ks[13], (L, D_MODEL, D_MODEL), D_MODEL ** -0.5),
        "ffn_norm": 1.0 + nrm(ks[14], (L, D_MODEL), 0.02),
        "w_gate": nrm(ks[15], (L, D_MODEL, D_FF), D_MODEL ** -0.5),
        "w_up": nrm(ks[16], (L, D_MODEL, D_FF), D_MODEL ** -0.5),
        "w_down": nrm(ks[17], (L, D_FF, D_MODEL), D_FF ** -0.5),
    }


def _fwd_reference(x, attn_norm, w_in, dn_conv, dn_a_log, dn_dt_bias, dn_out_norm, swa_q_norm,
              swa_k_norm, swa_sinks, rel_bias, w_branch_dn, w_branch_swa, w_out, ffn_norm,
              w_gate, w_up, w_down):
    B, S, _ = x.shape
    for l in range(DEPTH):
        h = rms_norm(x, attn_norm[l])
        proj = h @ w_in[l]
        dn_qkv, dn_z, dn_b, dn_a, sq, sk, sv, gate_raw = split_combined(proj)
        y_dn = gated_deltanet_branch(dn_qkv, dn_z, dn_b, dn_a, dn_conv[l], dn_a_log[l],
                                     dn_dt_bias[l], dn_out_norm[l])
        y_swa = sliding_window_branch(sq, sk, sv, swa_q_norm[l], swa_k_norm[l], swa_sinks[l], rel_bias)
        gates = jax.nn.sigmoid(gate_raw.astype(jnp.float32)).astype(x.dtype)
        gates = gates.reshape(B, S, N_BRANCHES, D_MODEL)
        merged = gates[:, :, 0] * (y_dn @ w_branch_dn[l]) + gates[:, :, 1] * (y_swa @ w_branch_swa[l])
        x = x + merged @ w_out[l]
        h2 = rms_norm(x, ffn_norm[l])
        x = x + (jax.nn.silu(h2 @ w_gate[l]) * (h2 @ w_up[l])) @ w_down[l]
    return x


import jax as _jax
import jax.numpy as _jnp

TWIN_FORMAT = 'train_step'
FWD_PARAMS = ['x', 'attn_norm', 'w_in', 'dn_conv', 'dn_a_log', 'dn_dt_bias', 'dn_out_norm', 'swa_q_norm', 'swa_k_norm', 'swa_sinks', 'rel_bias', 'w_branch_dn', 'w_branch_swa', 'w_out', 'ffn_norm', 'w_gate', 'w_up', 'w_down']
TWIN_WEIGHTS = ['attn_norm', 'w_in', 'dn_conv', 'dn_a_log', 'dn_dt_bias', 'dn_out_norm', 'swa_q_norm', 'swa_k_norm', 'swa_sinks', 'rel_bias', 'w_branch_dn', 'w_branch_swa', 'w_out', 'ffn_norm', 'w_gate', 'w_up', 'w_down']
TWIN_DIFF_INPUT = 'x'
TWIN_INPUTS = ['x', 'attn_norm', 'w_in', 'dn_conv', 'dn_a_log', 'dn_dt_bias', 'dn_out_norm', 'swa_q_norm', 'swa_k_norm', 'swa_sinks', 'rel_bias', 'w_branch_dn', 'w_branch_swa', 'w_out', 'ffn_norm', 'w_gate', 'w_up', 'w_down', 'loss_target', 'm_attn_norm', 'm_w_in', 'm_dn_conv', 'm_dn_a_log', 'm_dn_dt_bias', 'm_dn_out_norm', 'm_swa_q_norm', 'm_swa_k_norm', 'm_swa_sinks', 'm_rel_bias', 'm_w_branch_dn', 'm_w_branch_swa', 'm_w_out', 'm_ffn_norm', 'm_w_gate', 'm_w_up', 'm_w_down', 'v_attn_norm', 'v_w_in', 'v_dn_conv', 'v_dn_a_log', 'v_dn_dt_bias', 'v_dn_out_norm', 'v_swa_q_norm', 'v_swa_k_norm', 'v_swa_sinks', 'v_rel_bias', 'v_w_branch_dn', 'v_w_branch_swa', 'v_w_out', 'v_ffn_norm', 'v_w_gate', 'v_w_up', 'v_w_down']
TWIN_OUTPUTS = ['loss', 'grad_x', 'grad_attn_norm', 'grad_w_in', 'grad_dn_conv', 'grad_dn_a_log', 'grad_dn_dt_bias', 'grad_dn_out_norm', 'grad_swa_q_norm', 'grad_swa_k_norm', 'grad_swa_sinks', 'grad_rel_bias', 'grad_w_branch_dn', 'grad_w_branch_swa', 'grad_w_out', 'grad_ffn_norm', 'grad_w_gate', 'grad_w_up', 'grad_w_down', 'delta_attn_norm', 'delta_w_in', 'delta_dn_conv', 'delta_dn_a_log', 'delta_dn_dt_bias', 'delta_dn_out_norm', 'delta_swa_q_norm', 'delta_swa_k_norm', 'delta_swa_sinks', 'delta_rel_bias', 'delta_w_branch_dn', 'delta_w_branch_swa', 'delta_w_out', 'delta_ffn_norm', 'delta_w_gate', 'delta_w_up', 'delta_w_down', 'new_m_attn_norm', 'new_m_w_in', 'new_m_dn_conv', 'new_m_dn_a_log', 'new_m_dn_dt_bias', 'new_m_dn_out_norm', 'new_m_swa_q_norm', 'new_m_swa_k_norm', 'new_m_swa_sinks', 'new_m_rel_bias', 'new_m_w_branch_dn', 'new_m_w_branch_swa', 'new_m_w_out', 'new_m_ffn_norm', 'new_m_w_gate', 'new_m_w_up', 'new_m_w_down', 'new_v_attn_norm', 'new_v_w_in', 'new_v_dn_conv', 'new_v_dn_a_log', 'new_v_dn_dt_bias', 'new_v_dn_out_norm', 'new_v_swa_q_norm', 'new_v_swa_k_norm', 'new_v_swa_sinks', 'new_v_rel_bias', 'new_v_w_branch_dn', 'new_v_w_branch_swa', 'new_v_w_out', 'new_v_ffn_norm', 'new_v_w_gate', 'new_v_w_up', 'new_v_w_down']
TWIN_LEAF_KINDS = {'loss': 'loss', 'grad_x': 'grad_x', 'grad_attn_norm': 'grad_w', 'grad_w_in': 'grad_w', 'grad_dn_conv': 'grad_w', 'grad_dn_a_log': 'grad_w', 'grad_dn_dt_bias': 'grad_w', 'grad_dn_out_norm': 'grad_w', 'grad_swa_q_norm': 'grad_w', 'grad_swa_k_norm': 'grad_w', 'grad_swa_sinks': 'grad_w', 'grad_rel_bias': 'grad_w', 'grad_w_branch_dn': 'grad_w', 'grad_w_branch_swa': 'grad_w', 'grad_w_out': 'grad_w', 'grad_ffn_norm': 'grad_w', 'grad_w_gate': 'grad_w', 'grad_w_up': 'grad_w', 'grad_w_down': 'grad_w', 'delta_attn_norm': 'delta_w', 'delta_w_in': 'delta_w', 'delta_dn_conv': 'delta_w', 'delta_dn_a_log': 'delta_w', 'delta_dn_dt_bias': 'delta_w', 'delta_dn_out_norm': 'delta_w', 'delta_swa_q_norm': 'delta_w', 'delta_swa_k_norm': 'delta_w', 'delta_swa_sinks': 'delta_w', 'delta_rel_bias': 'delta_w', 'delta_w_branch_dn': 'delta_w', 'delta_w_branch_swa': 'delta_w', 'delta_w_out': 'delta_w', 'delta_ffn_norm': 'delta_w', 'delta_w_gate': 'delta_w', 'delta_w_up': 'delta_w', 'delta_w_down': 'delta_w', 'new_m_attn_norm': 'new_m', 'new_m_w_in': 'new_m', 'new_m_dn_conv': 'new_m', 'new_m_dn_a_log': 'new_m', 'new_m_dn_dt_bias': 'new_m', 'new_m_dn_out_norm': 'new_m', 'new_m_swa_q_norm': 'new_m', 'new_m_swa_k_norm': 'new_m', 'new_m_swa_sinks': 'new_m', 'new_m_rel_bias': 'new_m', 'new_m_w_branch_dn': 'new_m', 'new_m_w_branch_swa': 'new_m', 'new_m_w_out': 'new_m', 'new_m_ffn_norm': 'new_m', 'new_m_w_gate': 'new_m', 'new_m_w_up': 'new_m', 'new_m_w_down': 'new_m', 'new_v_attn_norm': 'new_v', 'new_v_w_in': 'new_v', 'new_v_dn_conv': 'new_v', 'new_v_dn_a_log': 'new_v', 'new_v_dn_dt_bias': 'new_v', 'new_v_dn_out_norm': 'new_v', 'new_v_swa_q_norm': 'new_v', 'new_v_swa_k_norm': 'new_v', 'new_v_swa_sinks': 'new_v', 'new_v_rel_bias': 'new_v', 'new_v_w_branch_dn': 'new_v', 'new_v_w_branch_swa': 'new_v', 'new_v_w_out': 'new_v', 'new_v_ffn_norm': 'new_v', 'new_v_w_gate': 'new_v', 'new_v_w_up': 'new_v', 'new_v_w_down': 'new_v'}


def _forward(args):
    return _fwd_reference(*[args[k] for k in FWD_PARAMS])


def _output_shape():
    def fwd():
        inp = _fwd_setup_inputs(0)
        return _fwd_reference(*[inp[k] for k in FWD_PARAMS])
    out = _jax.eval_shape(fwd)
    return out.shape, out.dtype

N_MICROBATCH = 1
ADAM_LR = 0.001
ADAM_B1 = 0.9
ADAM_B2 = 0.999
ADAM_EPS = 1e-08
ADAM_WD = 0.01
ADAM_STEP = 10
PER_EXAMPLE_BATCH_AXIS = {'x': 0, 'loss_target': 0}
SHARED_INPUTS = []
_WEIGHT_DTYPES = {'attn_norm': _jnp.float32, 'w_in': _jnp.float32, 'dn_conv': _jnp.float32, 'dn_a_log': _jnp.float32, 'dn_dt_bias': _jnp.float32, 'dn_out_norm': _jnp.float32, 'swa_q_norm': _jnp.float32, 'swa_k_norm': _jnp.float32, 'swa_sinks': _jnp.float32, 'rel_bias': _jnp.float32, 'w_branch_dn': _jnp.float32, 'w_branch_swa': _jnp.float32, 'w_out': _jnp.float32, 'ffn_norm': _jnp.float32, 'w_gate': _jnp.float32, 'w_up': _jnp.float32, 'w_down': _jnp.float32}
MOMENT_SCALE = {'attn_norm': 4.391320e+00, 'w_in': 1.582964e-01, 'dn_conv': 3.246182e-01, 'dn_a_log': 2.595455e+01, 'dn_dt_bias': 2.438892e+01, 'dn_out_norm': 2.759541e+01, 'swa_q_norm': 2.071559e+00, 'swa_k_norm': 2.073363e+00, 'swa_sinks': 6.554591e-01, 'rel_bias': 9.243407e-02, 'w_branch_dn': 4.088931e-01, 'w_branch_swa': 4.877755e-02, 'w_out': 3.521001e-01, 'ffn_norm': 2.485603e+01, 'w_gate': 2.164896e-01, 'w_up': 1.851041e-01, 'w_down': 2.888060e-01}


def _to_microbatches(a, axis):
    t = _jnp.moveaxis(a, axis, 0)
    t = t.reshape((N_MICROBATCH, t.shape[0] // N_MICROBATCH) + t.shape[1:])
    return _jnp.moveaxis(t, 1, axis + 1)


def setup_inputs(seed: int = 0) -> dict:
    inp = _fwd_setup_inputs(seed)
    key = _jax.random.fold_in(_jax.random.key(seed), 7919)
    shape, _ = _output_shape()
    out = dict(inp)
    out["loss_target"] = _jax.random.normal(_jax.random.fold_in(key, 0), shape, _jnp.float32)
    for i, name in enumerate(TWIN_WEIGHTS):
        w = inp[name].astype(_jnp.float32)
        if MOMENT_SCALE is None:
            s = _jnp.sqrt(_jnp.mean(_jnp.square(w)) + 1e-30)
        else:
            s = MOMENT_SCALE[name]
        km, kv = _jax.random.split(_jax.random.fold_in(key, i + 1))
        out[name] = w
        out["m_" + name] = s * _jax.random.normal(km, w.shape, _jnp.float32)
        out["v_" + name] = (s * s) * _jax.random.uniform(kv, w.shape, _jnp.float32, 0.5, 1.5)
    if N_MICROBATCH > 1:
        for name, axis in PER_EXAMPLE_BATCH_AXIS.items():
            out[name] = _to_microbatches(out[name], axis)
    return {'x': out['x'], 'attn_norm': out['attn_norm'], 'w_in': out['w_in'], 'dn_conv': out['dn_conv'], 'dn_a_log': out['dn_a_log'], 'dn_dt_bias': out['dn_dt_bias'], 'dn_out_norm': out['dn_out_norm'], 'swa_q_norm': out['swa_q_norm'], 'swa_k_norm': out['swa_k_norm'], 'swa_sinks': out['swa_sinks'], 'rel_bias': out['rel_bias'], 'w_branch_dn': out['w_branch_dn'], 'w_branch_swa': out['w_branch_swa'], 'w_out': out['w_out'], 'ffn_norm': out['ffn_norm'], 'w_gate': out['w_gate'], 'w_up': out['w_up'], 'w_down': out['w_down'], 'loss_target': out['loss_target'], 'm_attn_norm': out['m_attn_norm'], 'm_w_in': out['m_w_in'], 'm_dn_conv': out['m_dn_conv'], 'm_dn_a_log': out['m_dn_a_log'], 'm_dn_dt_bias': out['m_dn_dt_bias'], 'm_dn_out_norm': out['m_dn_out_norm'], 'm_swa_q_norm': out['m_swa_q_norm'], 'm_swa_k_norm': out['m_swa_k_norm'], 'm_swa_sinks': out['m_swa_sinks'], 'm_rel_bias': out['m_rel_bias'], 'm_w_branch_dn': out['m_w_branch_dn'], 'm_w_branch_swa': out['m_w_branch_swa'], 'm_w_out': out['m_w_out'], 'm_ffn_norm': out['m_ffn_norm'], 'm_w_gate': out['m_w_gate'], 'm_w_up': out['m_w_up'], 'm_w_down': out['m_w_down'], 'v_attn_norm': out['v_attn_norm'], 'v_w_in': out['v_w_in'], 'v_dn_conv': out['v_dn_conv'], 'v_dn_a_log': out['v_dn_a_log'], 'v_dn_dt_bias': out['v_dn_dt_bias'], 'v_dn_out_norm': out['v_dn_out_norm'], 'v_swa_q_norm': out['v_swa_q_norm'], 'v_swa_k_norm': out['v_swa_k_norm'], 'v_swa_sinks': out['v_swa_sinks'], 'v_rel_bias': out['v_rel_bias'], 'v_w_branch_dn': out['v_w_branch_dn'], 'v_w_branch_swa': out['v_w_branch_swa'], 'v_w_out': out['v_w_out'], 'v_ffn_norm': out['v_ffn_norm'], 'v_w_gate': out['v_w_gate'], 'v_w_up': out['v_w_up'], 'v_w_down': out['v_w_down']}


def _loss(weights, diff, rest, loss_target):
    with _jax.named_scope("forward"):
        args = {**rest, TWIN_DIFF_INPUT: diff, **{k: w.astype(_WEIGHT_DTYPES[k]) for k, w in weights.items()}}
        y = _forward(args)
    with _jax.named_scope("loss_head"):
        err = _jnp.square(y.astype(_jnp.float32) - loss_target)
        return 0.5 * _jnp.sum(_jnp.mean(err, axis=-1)) if err.ndim else 0.5 * err


def _adamw(w, g, m, v):
    m = ADAM_B1 * m + (1.0 - ADAM_B1) * g
    v = ADAM_B2 * v + (1.0 - ADAM_B2) * _jnp.square(g)
    m_hat = m / (1.0 - ADAM_B1 ** ADAM_STEP)
    v_hat = v / (1.0 - ADAM_B2 ** ADAM_STEP)
    delta = -ADAM_LR * (m_hat / (_jnp.sqrt(v_hat) + ADAM_EPS) + ADAM_WD * w)
    return delta, m, v


def reference(x, attn_norm, w_in, dn_conv, dn_a_log, dn_dt_bias, dn_out_norm, swa_q_norm, swa_k_norm, swa_sinks, rel_bias, w_branch_dn, w_branch_swa, w_out, ffn_norm, w_gate, w_up, w_down, loss_target, m_attn_norm, m_w_in, m_dn_conv, m_dn_a_log, m_dn_dt_bias, m_dn_out_norm, m_swa_q_norm, m_swa_k_norm, m_swa_sinks, m_rel_bias, m_w_branch_dn, m_w_branch_swa, m_w_out, m_ffn_norm, m_w_gate, m_w_up, m_w_down, v_attn_norm, v_w_in, v_dn_conv, v_dn_a_log, v_dn_dt_bias, v_dn_out_norm, v_swa_q_norm, v_swa_k_norm, v_swa_sinks, v_rel_bias, v_w_branch_dn, v_w_branch_swa, v_w_out, v_ffn_norm, v_w_gate, v_w_up, v_w_down):
    given = dict(x=x, attn_norm=attn_norm, w_in=w_in, dn_conv=dn_conv, dn_a_log=dn_a_log, dn_dt_bias=dn_dt_bias, dn_out_norm=dn_out_norm, swa_q_norm=swa_q_norm, swa_k_norm=swa_k_norm, swa_sinks=swa_sinks, rel_bias=rel_bias, w_branch_dn=w_branch_dn, w_branch_swa=w_branch_swa, w_out=w_out, ffn_norm=ffn_norm, w_gate=w_gate, w_up=w_up, w_down=w_down, loss_target=loss_target, m_attn_norm=m_attn_norm, m_w_in=m_w_in, m_dn_conv=m_dn_conv, m_dn_a_log=m_dn_a_log, m_dn_dt_bias=m_dn_dt_bias, m_dn_out_norm=m_dn_out_norm, m_swa_q_norm=m_swa_q_norm, m_swa_k_norm=m_swa_k_norm, m_swa_sinks=m_swa_sinks, m_rel_bias=m_rel_bias, m_w_branch_dn=m_w_branch_dn, m_w_branch_swa=m_w_branch_swa, m_w_out=m_w_out, m_ffn_norm=m_ffn_norm, m_w_gate=m_w_gate, m_w_up=m_w_up, m_w_down=m_w_down, v_attn_norm=v_attn_norm, v_w_in=v_w_in, v_dn_conv=v_dn_conv, v_dn_a_log=v_dn_a_log, v_dn_dt_bias=v_dn_dt_bias, v_dn_out_norm=v_dn_out_norm, v_swa_q_norm=v_swa_q_norm, v_swa_k_norm=v_swa_k_norm, v_swa_sinks=v_swa_sinks, v_rel_bias=v_rel_bias, v_w_branch_dn=v_w_branch_dn, v_w_branch_swa=v_w_branch_swa, v_w_out=v_w_out, v_ffn_norm=v_ffn_norm, v_w_gate=v_w_gate, v_w_up=v_w_up, v_w_down=v_w_down)
    weights = {n: given[n] for n in TWIN_WEIGHTS}
    shared = {n: given[n] for n in SHARED_INPUTS}
    per_example = {n: given[n] for n in ['x']}
    grad_fn = _jax.value_and_grad(_loss, argnums=(0, 1))

    def one_microbatch(ex, loss_target):
        ex = dict(ex)
        diff = ex.pop(TWIN_DIFF_INPUT)
        return grad_fn(weights, diff, {**shared, **ex}, loss_target)

    if N_MICROBATCH == 1:
        loss, (grad_w, grad_x) = one_microbatch(per_example, given["loss_target"])
    else:
        def body(carry, xs):
            loss_sum, grad_sum = carry
            l_k, (gw_k, gx_k) = one_microbatch(xs[0], xs[1])
            with _jax.named_scope("update"):
                return (loss_sum + l_k, _jax.tree.map(_jnp.add, grad_sum, gw_k)), gx_k

        init = (_jnp.zeros((), _jnp.float32), _jax.tree.map(_jnp.zeros_like, weights))
        (loss, grad_w), grad_x = _jax.lax.scan(body, init, (per_example, given["loss_target"]))
    with _jax.named_scope("update"):
        delta_w, new_m, new_v = {}, {}, {}
        for n in TWIN_WEIGHTS:
            delta_w[n], new_m[n], new_v[n] = _adamw(weights[n], grad_w[n], given["m_" + n], given["v_" + n])
    return (loss, grad_x, *[grad_w[n] for n in TWIN_WEIGHTS], *[delta_w[n] for n in TWIN_WEIGHTS],
            *[new_m[n] for n in TWIN_WEIGHTS], *[new_v[n] for n in TWIN_WEIGHTS])
```

```python
import functools
import math

import numpy as np
import jax
import jax.numpy as jnp
from jax import lax
from jax.experimental import pallas as pl
from jax.experimental.pallas import tpu as pltpu

F32 = jnp.float32
BF16 = jnp.bfloat16

D = 1024
DN_H = 4
DH = 128
DNW = DN_H * DH
QKVW = 3 * DNW
CONV = 4
CHUNK = 64
SWA_H = 8
SWA_KV = 2
SWA_G = SWA_H // SWA_KV
SWA_D = 64
SWAW = SWA_H * SWA_D
SWAKW = SWA_KV * SWA_D
BLK = 128
NBUCKET = 32
MAXDIST = 128
DFF = 2816
D_IN = QKVW + DNW + 2 * DN_H + SWAW + 2 * SWAKW + 2 * D
EPS = 1e-6
NEG = -1e30

ADAM_LR = 0.001
ADAM_B1 = 0.9
ADAM_B2 = 0.999
ADAM_EPS = 1e-08
ADAM_WD = 0.01
ADAM_STEP = 10

C_QKV, C_Z, C_GATE, C_SQ, C_SK, C_SV, C_BA = 0, 1536, 2048, 4096, 4608, 4736, 4864
PW = 5120
_ORIG_PIECES = (
    (0, QKVW, C_QKV),
    (QKVW, DNW, C_Z),
    (QKVW + DNW, 2 * DN_H, C_BA),
    (QKVW + DNW + 2 * DN_H, SWAW, C_SQ),
    (QKVW + DNW + 2 * DN_H + SWAW, SWAKW, C_SK),
    (QKVW + DNW + 2 * DN_H + SWAW + SWAKW, SWAKW, C_SV),
    (QKVW + DNW + 2 * DN_H + SWAW + 2 * SWAKW, 2 * D, C_GATE),
)

N_CHIPS = 4
VMEM_LIMIT = 48 * 1024 * 1024
MESH = pl.DeviceIdType.MESH

_BIG = (
    ("w_in", (D, D_IN), 1),
    ("dn_conv", (CONV, QKVW), 1),
    ("w_branch_dn", (DNW, D), 1),
    ("w_branch_swa", (SWAW, D), 1),
    ("w_out", (D, D), 0),
    ("w_gate", (D, DFF), 1),
    ("w_up", (D, DFF), 1),
    ("w_down", (DFF, D), 0),
)
_N_LOCAL = sum(s[0] * s[1] // N_CHIPS for _, s, _ in _BIG)
_HALF_ROWS = -(-_N_LOCAL // (2 * 128 * 256)) * 256
_N_PAD = 2 * _HALF_ROWS * 128

_SMALL = (
    ("attn_norm", D), ("ffn_norm", D), ("dn_out_norm", DH), ("swa_q_norm", SWA_D), ("swa_k_norm", SWA_D),
    ("swa_sinks", SWA_H), ("dn_a_log", DN_H), ("dn_dt_bias", DN_H), ("rel_bias", NBUCKET * SWA_H),
)
_SMALL_OFF = {}
_o = 0
for _n, _s in _SMALL:
    _SMALL_OFF[_n] = (_o, _s)
    _o += _s
_LOSS_OFF = _o
_SMALL_ROWS = 24
assert _o + 1 <= _SMALL_ROWS * 128


def _cparams(**kw):
    return pltpu.CompilerParams(vmem_limit_bytes=VMEM_LIMIT, **kw)


_DIMS = {
    "nn": (((1,), (0,)), ((), ())),
    "nt": (((1,), (1,)), ((), ())),
    "tn": (((0,), (0,)), ((), ())),
}


def _raw_dot(a, b, kind, exact):
    if exact:
        return lax.dot_general(a, b, _DIMS[kind], precision=lax.Precision.HIGHEST, preferred_element_type=F32)
    return lax.dot_general(a.astype(BF16), b.astype(BF16), _DIMS[kind], preferred_element_type=F32)


@functools.partial(jax.custom_vjp, nondiff_argnums=(2, 3))
def _dot(a, b, kind, exact):
    return _raw_dot(a, b, kind, exact)


def _dot_fwd(a, b, kind, exact):
    return _raw_dot(a, b, kind, exact), (a, b)


def _dot_bwd(kind, exact, res, g):
    a, b = res
    if kind == "nn":
        return _dot(g, b, "nt", exact), _dot(a, g, "tn", exact)
    if kind == "nt":
        return _dot(g, b, "nn", exact), _dot(g, a, "tn", exact)
    return _dot(b, g, "nt", exact), _dot(a, g, "nn", exact)


_dot.defvjp(_dot_fwd, _dot_bwd)


def _silu(x):
    return x * jax.nn.sigmoid(x)


def _f_rms(x, gain):
    return x * lax.rsqrt(jnp.mean(x * x, axis=-1, keepdims=True) + EPS) * gain


def _f_dn_pre(xs0, xs1, xs2, xs3, ba, cw, alog, dtb):
    rows = xs0.shape[0]
    c = xs0 * cw[0:1] + xs1 * cw[1:2] + xs2 * cw[2:3] + xs3 * cw[3:4]
    qkv = _silu(c)
    qs, ks, bbs, gbs = [], [], [], []
    for h in range(DN_H):
        qh = qkv[:, h * DH:(h + 1) * DH]
        kh = qkv[:, DNW + h * DH:DNW + (h + 1) * DH]
        qs.append(qh * lax.rsqrt(jnp.sum(qh * qh, axis=-1, keepdims=True) + EPS) * (DH ** -0.5))
        ks.append(kh * lax.rsqrt(jnp.sum(kh * kh, axis=-1, keepdims=True) + EPS))
        beta = jax.nn.sigmoid(ba[:, h:h + 1])
        ar = ba[:, DN_H + h:DN_H + h + 1] + dtb[:, h:h + 1]
        softplus = jnp.maximum(ar, 0.0) + jnp.log1p(jnp.exp(-jnp.abs(ar)))
        g = -jnp.exp(alog[:, h:h + 1]) * softplus
        bbs.append(jnp.broadcast_to(beta, (rows, DH)))
        gbs.append(jnp.broadcast_to(g, (rows, DH)))
    return (jnp.concatenate(qs, axis=1), jnp.concatenate(ks, axis=1), qkv[:, 2 * DNW:],
            jnp.concatenate(bbs, axis=1), jnp.concatenate(gbs, axis=1))


def _f_dn_post(o, z, gain):
    ys = []
    for h in range(DN_H):
        oh = o[:, h * DH:(h + 1) * DH]
        zh = z[:, h * DH:(h + 1) * DH]
        ys.append(oh * lax.rsqrt(jnp.mean(oh * oh, axis=-1, keepdims=True) + EPS) * gain * _silu(zh))
    return jnp.concatenate(ys, axis=1)


def _f_merge(pa, pb, gr):
    return jax.nn.sigmoid(gr[:, :D]) * pa + jax.nn.sigmoid(gr[:, D:]) * pb


def _f_swiglu(gu):
    return _silu(gu[:, :DFF]) * gu[:, DFF:]


def _f_chunk(q, k, v, gb, bb, s):
    c = CHUNK
    ii = lax.broadcasted_iota(jnp.int32, (c, c), 0)
    jj = lax.broadcasted_iota(jnp.int32, (c, c), 1)
    incl = ii >= jj
    strict = ii > jj
    eye = (ii == jj).astype(F32)
    gcb = _dot(incl.astype(F32), gb, "nn", True)
    lane0 = (lax.broadcasted_iota(jnp.int32, (c, DH), 1) == 0).astype(F32)
    gcol = gcb[:, :c]
    grow = _dot(lane0, gcb, "nt", True)
    decay = jnp.where(incl, jnp.exp(jnp.where(incl, gcol - grow, 0.0)), 0.0)
    kb = k * bb
    vb = v * bb
    a = jnp.where(strict, _dot(kb, k, "nt", False) * decay, 0.0)
    p = -a
    t = eye + p
    for _ in range(5):
        p = _dot(p, p, "nn", True)
        t = t + _dot(t, p, "nn", True)
    eg = jnp.exp(gcb)
    u = _dot(t, vb, "nn", True)
    w = _dot(t, kb * eg, "nn", True)
    qk = jnp.where(incl, _dot(q, k, "nt", False) * decay, 0.0)
    v_new = u - _dot(w, s, "nn", False)
    o = _dot(q * eg, s, "nn", False) + _dot(qk, v_new, "nn", False)
    glast = gcb[c - 1:c, :]
    k_dec = k * jnp.exp(glast - gcb)
    s_new = s * jnp.exp(glast) + _dot(k_dec, v_new, "tn", False)
    return o, s_new


def _f_swa(q4, kp, kc, vp, vc, bias4, qg, kg, sink, mask):
    kb = jnp.concatenate([kp, kc], axis=0)
    vb = jnp.concatenate([vp, vc], axis=0)
    kn = kb * lax.rsqrt(jnp.mean(kb * kb, axis=-1, keepdims=True) + EPS) * kg
    outs = []
    for g in range(SWA_G):
        qq = q4[g]
        qn = qq * lax.rsqrt(jnp.mean(qq * qq, axis=-1, keepdims=True) + EPS) * qg
        lg = _dot(qn, kn, "nt", False) * (SWA_D ** -0.5) + bias4[g]
        lg = jnp.where(mask, lg, NEG)
        sk = sink[:, g:g + 1]
        m = lax.stop_gradient(jnp.maximum(jnp.max(lg, axis=-1, keepdims=True), sk))
        p = jnp.exp(lg - m)
        den = jnp.sum(p, axis=-1, keepdims=True) + jnp.exp(sk - m)
        outs.append(_dot(p / den, vb, "nn", False))
    return jnp.stack(outs, axis=0)


def _mm(a, b, out_dtype, tm, tn, name, res=None):
    m, k = a.shape
    k2, n = b.shape
    assert k == k2 and m % tm == 0 and n % tn == 0, (name, a.shape, b.shape, tm, tn)
    has_res = res is not None

    def kern(*refs):
        if has_res:
            a_ref, b_ref, r_ref, o_ref = refs
        else:
            a_ref, b_ref, o_ref = refs
        acc = jnp.dot(a_ref[...].astype(BF16), b_ref[...].astype(BF16), preferred_element_type=F32)
        if has_res:
            acc = acc + r_ref[...]
        o_ref[...] = acc.astype(o_ref.dtype)

    in_specs = [pl.BlockSpec((tm, k), lambda i, j: (i, 0)), pl.BlockSpec((k, tn), lambda i, j: (0, j))]
    args = [a, b]
    if has_res:
        in_specs.append(pl.BlockSpec((tm, tn), lambda i, j: (i, j)))
        args.append(res)
    return pl.pallas_call(
        kern, name=name, grid=(m // tm, n // tn), in_specs=in_specs,
        out_specs=pl.BlockSpec((tm, tn), lambda i, j: (i, j)),
        out_shape=jax.ShapeDtypeStruct((m, n), out_dtype),
        compiler_params=_cparams(dimension_semantics=("arbitrary", "arbitrary")),
    )(*args)


def _rows(body, name, m, tm, row_ins, full_ins, row_outs, acc_outs=()):
    n_r, n_f, n_o, n_a = len(row_ins), len(full_ins), len(row_outs), len(acc_outs)
    assert m % tm == 0

    def kern(*refs):
        r = refs[:n_r]
        f = refs[n_r:n_r + n_f]
        o = refs[n_r + n_f:n_r + n_f + n_o]
        acc = refs[n_r + n_f + n_o:]
        outs, sums = body([x[...] for x in r], [x[...] for x in f])
        for ref, val in zip(o, outs, strict=True):
            ref[...] = val.astype(ref.dtype)
        if n_a:
            @pl.when(pl.program_id(0) == 0)
            def _():
                for ref in acc:
                    ref[...] = jnp.zeros(ref.shape, F32)

            for ref, val in zip(acc, sums, strict=True):
                ref[...] += val

    in_specs = [pl.BlockSpec((tm, w), functools.partial(lambda i, cb: (i, cb), cb=cb)) for _, w, cb in row_ins]
    in_specs += [pl.BlockSpec(x.shape, lambda i: (0, 0)) for x in full_ins]
    out_specs = [pl.BlockSpec((tm, w), lambda i: (i, 0)) for w, _ in row_outs]
    out_specs += [pl.BlockSpec(s, lambda i: (0, 0)) for s in acc_outs]
    out_shape = [jax.ShapeDtypeStruct((m, w), dt) for w, dt in row_outs]
    out_shape += [jax.ShapeDtypeStruct(s, F32) for s in acc_outs]
    res = pl.pallas_call(
        kern, name=name, grid=(m // tm,), in_specs=in_specs, out_specs=out_specs, out_shape=out_shape,
        compiler_params=_cparams(dimension_semantics=("arbitrary",)),
    )(*[x for x, _, _ in row_ins], *full_ins)
    return res


def _whole(x):
    return (x, x.shape[1], 0)


def _dn_chunks_fwd(q, k, v, gb, bb):
    s_len = q.shape[0]
    nc = s_len // CHUNK

    def kern(q_ref, k_ref, v_ref, g_ref, b_ref, o_ref, sall_ref, state):
        @pl.when(pl.program_id(1) == 0)
        def _():
            state[...] = jnp.zeros(state.shape, F32)

        s = state[...]
        sall_ref[0, 0] = s
        o, s_new = _f_chunk(q_ref[...], k_ref[...], v_ref[...], g_ref[...], b_ref[...], s)
        o_ref[...] = o
        state[...] = s_new

    blk = pl.BlockSpec((CHUNK, DH), lambda h, c: (c, h))
    return pl.pallas_call(
        kern, name="dn_chunks_fwd", grid=(DN_H, nc), in_specs=[blk] * 5,
        out_specs=[blk, pl.BlockSpec((1, 1, DH, DH), lambda h, c: (c, h, 0, 0))],
        out_shape=[jax.ShapeDtypeStruct((s_len, DNW), F32), jax.ShapeDtypeStruct((nc, DN_H, DH, DH), F32)],
        scratch_shapes=[pltpu.VMEM((DH, DH), F32)],
        compiler_params=_cparams(dimension_semantics=("arbitrary", "arbitrary")),
    )(q, k, v, gb, bb)


def _dn_chunks_bwd(q, k, v, gb, bb, s_all, d_o):
    s_len = q.shape[0]
    nc = s_len // CHUNK

    def kern(q_ref, k_ref, v_ref, g_ref, b_ref, sall_ref, do_ref, dq_ref, dk_ref, dv_ref, dg_ref, db_ref, dstate):
        @pl.when(pl.program_id(1) == 0)
        def _():
            dstate[...] = jnp.zeros(dstate.shape, F32)

        _, vjp = jax.vjp(_f_chunk, q_ref[...], k_ref[...], v_ref[...], g_ref[...], b_ref[...], sall_ref[0, 0])
        dq, dk, dv, dg, db, ds = vjp((do_ref[...], dstate[...]))
        dq_ref[...] = dq
        dk_ref[...] = dk
        dv_ref[...] = dv
        dg_ref[...] = dg
        db_ref[...] = db
        dstate[...] = ds

    blk = pl.BlockSpec((CHUNK, DH), lambda h, c: (nc - 1 - c, h))
    return pl.pallas_call(
        kern, name="dn_chunks_bwd", grid=(DN_H, nc),
        in_specs=[blk] * 5 + [pl.BlockSpec((1, 1, DH, DH), lambda h, c: (nc - 1 - c, h, 0, 0)), blk],
        out_specs=[blk] * 5,
        out_shape=[jax.ShapeDtypeStruct((s_len, DNW), F32)] * 5,
        scratch_shapes=[pltpu.VMEM((DH, DH), F32)],
        compiler_params=_cparams(dimension_semantics=("arbitrary", "arbitrary")),
    )(q, k, v, gb, bb, s_all, d_o)


def _t5_bucket_table():
    qi = np.arange(BLK)[:, None]
    kj = np.arange(2 * BLK)[None, :]
    dist = BLK + qi - kj
    n = np.maximum(dist, 0)
    max_exact = NBUCKET // 2
    nf = np.maximum(n, 1).astype(np.float32)
    large = max_exact + (np.log(nf / np.float32(max_exact)) / np.float32(math.log(MAXDIST / max_exact))
                         * np.float32(NBUCKET - max_exact)).astype(np.int32)
    large = np.minimum(large, NBUCKET - 1)
    return np.where(n < max_exact, n, large)


def _bucket_onehot_t():
    table = _t5_bucket_table().reshape(-1)
    return (np.arange(NBUCKET)[:, None] == table[None, :]).astype(np.float32)


def _swa_mask(first):
    qi = lax.broadcasted_iota(jnp.int32, (BLK, 2 * BLK), 0)
    kj = lax.broadcasted_iota(jnp.int32, (BLK, 2 * BLK), 1)
    dist = BLK + qi - kj
    window = (dist >= 0) & (dist < BLK)
    return window & ((kj >= BLK) | jnp.logical_not(first))


def _bias_expand(rel_bias_t):
    onehot = jnp.asarray(_bucket_onehot_t())

    def kern(r_ref, oh_ref, o_ref):
        o_ref[...] = _raw_dot(r_ref[...], oh_ref[...], "nn", True)

    return pl.pallas_call(
        kern, name="bias_expand", out_shape=jax.ShapeDtypeStruct((SWA_H, BLK * 2 * BLK), F32),
        compiler_params=_cparams(),
    )(rel_bias_t, onehot)


def _bias_reduce(d_bias_flat):
    onehot = jnp.asarray(_bucket_onehot_t())

    def kern(d_ref, oh_ref, o_ref):
        o_ref[...] = _raw_dot(d_ref[...], oh_ref[...], "nt", True)

    return pl.pallas_call(
        kern, name="bias_reduce", out_shape=jax.ShapeDtypeStruct((SWA_H, NBUCKET), F32),
        compiler_params=_cparams(),
    )(d_bias_flat, onehot)


def _swa_specs(nb, rev):
    def blk(n):
        return (nb - 1 - n) if rev else n

    q_spec = pl.BlockSpec((SWA_G, BLK, SWA_D), lambda kv, n: (kv, blk(n), 0))
    cur = pl.BlockSpec((1, BLK, SWA_D), lambda kv, n: (kv, blk(n), 0))
    prev = pl.BlockSpec((1, BLK, SWA_D), lambda kv, n: (kv, jnp.maximum(blk(n) - 1, 0), 0))
    bias = pl.BlockSpec((SWA_G, BLK, 2 * BLK), lambda kv, n: (kv, 0, 0))
    gain = pl.BlockSpec((1, SWA_D), lambda kv, n: (0, 0))
    sink = pl.BlockSpec((1, 1, SWA_G), lambda kv, n: (kv, 0, 0))
    return q_spec, cur, prev, bias, gain, sink


def _swa_fwd(q, k, v, bias, qg, kg, sinks):
    s_len = q.shape[1]
    nb = s_len // BLK
    q_spec, cur, prev, bias_spec, gain, sink = _swa_specs(nb, False)

    def kern(q_ref, kp_ref, kc_ref, vp_ref, vc_ref, b_ref, qg_ref, kg_ref, s_ref, o_ref):
        mask = _swa_mask(pl.program_id(1) == 0)
        o_ref[...] = _f_swa(q_ref[...], kp_ref[0], kc_ref[0], vp_ref[0], vc_ref[0], b_ref[...], qg_ref[...],
                            kg_ref[...], s_ref[0], mask)

    return pl.pallas_call(
        kern, name="swa_fwd", grid=(SWA_KV, nb),
        in_specs=[q_spec, prev, cur, prev, cur, bias_spec, gain, gain, sink],
        out_specs=q_spec, out_shape=jax.ShapeDtypeStruct((SWA_H, s_len, SWA_D), F32),
        compiler_params=_cparams(dimension_semantics=("arbitrary", "arbitrary")),
    )(q, k, k, v, v, bias, qg, kg, sinks)


def _swa_bwd(q, k, v, bias, qg, kg, sinks, d_out):
    s_len = q.shape[1]
    nb = s_len // BLK
    q_spec, cur, prev, bias_spec, gain, sink = _swa_specs(nb, True)

    def kern(q_ref, kp_ref, kc_ref, vp_ref, vc_ref, b_ref, qg_ref, kg_ref, s_ref, do_ref,
             dq_ref, dk_ref, dv_ref, db_ref, dqg_ref, dkg_ref, ds_ref, carry_k, carry_v):
        kv = pl.program_id(0)
        n = pl.program_id(1)
        mask = _swa_mask(n == nb - 1)

        @pl.when(n == 0)
        def _():
            carry_k[...] = jnp.zeros(carry_k.shape, F32)
            carry_v[...] = jnp.zeros(carry_v.shape, F32)
            db_ref[...] = jnp.zeros(db_ref.shape, F32)
            ds_ref[...] = jnp.zeros(ds_ref.shape, F32)

        @pl.when((n == 0) & (kv == 0))
        def _():
            dqg_ref[...] = jnp.zeros(dqg_ref.shape, F32)
            dkg_ref[...] = jnp.zeros(dkg_ref.shape, F32)

        fn = functools.partial(_f_swa, mask=mask)
        _, vjp = jax.vjp(fn, q_ref[...], kp_ref[0], kc_ref[0], vp_ref[0], vc_ref[0], b_ref[...], qg_ref[...],
                         kg_ref[...], s_ref[0])
        dq, dkp, dkc, dvp, dvc, dbias, dqg, dkg, dsink = vjp(do_ref[...])
        dq_ref[...] = dq
        dk_ref[0] = dkc + carry_k[...]
        dv_ref[0] = dvc + carry_v[...]
        carry_k[...] = dkp
        carry_v[...] = dvp
        db_ref[...] += dbias
        dqg_ref[...] += dqg
        dkg_ref[...] += dkg
        ds_ref[0] += dsink

    return pl.pallas_call(
        kern, name="swa_bwd", grid=(SWA_KV, nb),
        in_specs=[q_spec, prev, cur, prev, cur, bias_spec, gain, gain, sink, q_spec],
        out_specs=[q_spec, cur, cur, bias_spec, gain, gain, sink],
        out_shape=[
            jax.ShapeDtypeStruct((SWA_H, s_len, SWA_D), F32),
            jax.ShapeDtypeStruct((SWA_KV, s_len, SWA_D), F32),
            jax.ShapeDtypeStruct((SWA_KV, s_len, SWA_D), F32),
            jax.ShapeDtypeStruct((SWA_H, BLK, 2 * BLK), F32),
            jax.ShapeDtypeStruct((1, SWA_D), F32),
            jax.ShapeDtypeStruct((1, SWA_D), F32),
            jax.ShapeDtypeStruct((SWA_KV, 1, SWA_G), F32),
        ],
        scratch_shapes=[pltpu.VMEM((BLK, SWA_D), F32), pltpu.VMEM((BLK, SWA_D), F32)],
        compiler_params=_cparams(dimension_semantics=("arbitrary", "arbitrary")),
    )(q, k, k, v, v, bias, qg, kg, sinks, d_out)


def _shift_down(a, n):
    if n == 0:
        return a
    return jnp.pad(a, ((n, 0), (0, 0)))[:a.shape[0]]


def _shift_up(a, n):
    if n == 0:
        return a
    return jnp.pad(a, ((0, n), (0, 0)))[n:]


def _pad_w_in(w_in):
    pieces = [w_in[:, o0:o0 + w] for o0, w, _ in sorted(_ORIG_PIECES, key=lambda t: t[2])]
    pieces.append(jnp.zeros((w_in.shape[0], PW - D_IN), w_in.dtype))
    return jnp.concatenate(pieces, axis=1)


def _unpad_w_in(g):
    return jnp.concatenate([g[:, p0:p0 + w] for _, w, p0 in _ORIG_PIECES], axis=1)


def _local_step(x, target, wts):
    s_len = x.shape[0]
    tm = min(256, s_len)
    tmh = min(128, s_len)
    w_in_p = _pad_w_in(wts["w_in"])
    w_gu = jnp.concatenate([wts["w_gate"], wts["w_up"]], axis=1)
    attn_gain = wts["attn_norm"]
    ffn_gain = wts["ffn_norm"]
    conv_w = wts["dn_conv"]
    alog, dtb, out_gain = wts["dn_a_log"], wts["dn_dt_bias"], wts["dn_out_norm"]
    qg, kg = wts["swa_q_norm"], wts["swa_k_norm"]
    sinks = wts["swa_sinks"].reshape(SWA_KV, 1, SWA_G)

    (h,) = _rows(lambda r, f: ([_f_rms(r[0], f[0])], []), "rms1_fwd", s_len, tm, [_whole(x)], [attn_gain],
                 [(D, BF16)])
    proj = _mm(h, w_in_p, F32, tm, 1024, "mm_proj")
    qkv_pre = proj[:, :QKVW]
    xs = [_shift_down(qkv_pre, CONV - 1 - j) for j in range(CONV - 1)]
    pre_ins = [_whole(xs[0]), _whole(xs[1]), _whole(xs[2]), (proj, QKVW, 0), (proj, 128, C_BA // 128)]
    pre_full = [conv_w, alog, dtb]
    q_dn, k_dn, v_dn, bb, gb = _rows(lambda r, f: (list(_f_dn_pre(*r, *f)), []), "dn_pre_fwd", s_len, tmh, pre_ins,
                                  pre_full, [(DNW, F32)] * 5)
    o_dn, s_all = _dn_chunks_fwd(q_dn, k_dn, v_dn, gb, bb)
    post_ins = [_whole(o_dn), (proj, DNW, C_Z // DNW)]
    (y_dn,) = _rows(lambda r, f: ([_f_dn_post(r[0], r[1], f[0])], []), "dn_post_fwd", s_len, tm, post_ins,
                    [out_gain], [(DNW, BF16)])

    sq = proj[:, C_SQ:C_SQ + SWAW].reshape(s_len, SWA_H, SWA_D).transpose(1, 0, 2)
    sk = proj[:, C_SK:C_SK + SWAKW].reshape(s_len, SWA_KV, SWA_D).transpose(1, 0, 2)
    sv = proj[:, C_SV:C_SV + SWAKW].reshape(s_len, SWA_KV, SWA_D).transpose(1, 0, 2)
    bias = _bias_expand(wts["rel_bias"].T).reshape(SWA_H, BLK, 2 * BLK)
    o_swa = _swa_fwd(sq, sk, sv, bias, qg, kg, sinks)
    y_swa = o_swa.transpose(1, 0, 2).reshape(s_len, SWAW).astype(BF16)

    p_a = _mm(y_dn, wts["w_branch_dn"], F32, tm, 1024, "mm_pa")
    p_b = _mm(y_swa, wts["w_branch_swa"], F32, tm, 1024, "mm_pb")
    gate_in = (proj, 2 * D, C_GATE // (2 * D))
    (merged,) = _rows(lambda r, f: ([_f_merge(*r)], []), "merge_fwd", s_len, tm, [_whole(p_a), _whole(p_b), gate_in],
                      [], [(D, BF16)])
    x1 = _mm(merged, wts["w_out"], F32, tm, 1024, "mm_out", res=x)
    (h2,) = _rows(lambda r, f: ([_f_rms(r[0], f[0])], []), "rms2_fwd", s_len, tm, [_whole(x1)], [ffn_gain],
                  [(D, BF16)])
    gu = _mm(h2, w_gu, F32, tm, 1408, "mm_gu")
    (act,) = _rows(lambda r, f: ([_f_swiglu(r[0])], []), "swiglu_fwd", s_len, tm, [_whole(gu)], [], [(DFF, BF16)])
    y = _mm(act, wts["w_down"], F32, tm, 1024, "mm_down", res=x1)

    def loss_body(r, f):
        d = r[0] - r[1]
        return [d * (1.0 / D), d * (1.0 / D)], [jnp.sum(d * d).reshape(1, 1) * (0.5 / D)]

    dy, dy_b, loss = _rows(loss_body, "loss", s_len, tm, [_whole(y), _whole(target)], [], [(D, F32), (D, BF16)],
                           [(1, 1)])

    grads = {}
    d_act = _mm(dy_b, wts["w_down"].T, F32, tm, 1408, "mm_dact")
    grads["w_down"] = _mm(act.T, dy_b, BF16, 704, 1024, "mm_gwdown")

    def swiglu_bwd(r, f):
        _, vjp = jax.vjp(_f_swiglu, r[0])
        return [vjp(r[1])[0]], []

    (d_gu,) = _rows(swiglu_bwd, "swiglu_bwd", s_len, tm, [_whole(gu), _whole(d_act)], [], [(2 * DFF, BF16)])
    g_gu = _mm(h2.T, d_gu, BF16, 512, 1408, "mm_gwgu")
    grads["w_gate"], grads["w_up"] = g_gu[:, :DFF], g_gu[:, DFF:]
    dh2 = _mm(d_gu, w_gu.T, F32, tm, 1024, "mm_dh2")

    def rms_bwd(r, f):
        _, vjp = jax.vjp(_f_rms, r[0], f[0])
        dx, dgain = vjp(r[1])
        return [dx + r[2], dx + r[2]], [dgain]

    dx1, dx1_b, grads["ffn_norm"] = _rows(rms_bwd, "rms2_bwd", s_len, tm, [_whole(x1), _whole(dh2), _whole(dy)],
                                          [ffn_gain], [(D, F32), (D, BF16)], [(1, D)])
    d_merged = _mm(dx1_b, wts["w_out"].T, F32, tm, 1024, "mm_dmerged")
    grads["w_out"] = _mm(merged.T, dx1_b, BF16, 512, 1024, "mm_gwout")

    def merge_bwd(r, f):
        _, vjp = jax.vjp(_f_merge, r[0], r[1], r[2])
        return list(vjp(r[3])), []

    d_pa, d_pb, d_gr = _rows(merge_bwd, "merge_bwd", s_len, tm,
                             [_whole(p_a), _whole(p_b), gate_in, _whole(d_merged)], [],
                             [(D, BF16), (D, BF16), (2 * D, BF16)])
    d_ydn = _mm(d_pa, wts["w_branch_dn"].T, F32, tm, DNW, "mm_dydn")
    d_yswa = _mm(d_pb, wts["w_branch_swa"].T, F32, tm, SWAW, "mm_dyswa")
    grads["w_branch_dn"] = _mm(y_dn.T, d_pa, BF16, DNW, 1024, "mm_gwa")
    grads["w_branch_swa"] = _mm(y_swa.T, d_pb, BF16, SWAW, 1024, "mm_gwb")

    d_oswa = d_yswa.reshape(s_len, SWA_H, SWA_D).transpose(1, 0, 2)
    d_sq, d_sk, d_sv, d_bias, grads["swa_q_norm"], grads["swa_k_norm"], d_sinks = _swa_bwd(
        sq, sk, sv, bias, qg, kg, sinks, d_oswa)
    grads["swa_sinks"] = d_sinks.reshape(1, SWA_H)
    grads["rel_bias"] = _bias_reduce(d_bias.reshape(SWA_H, BLK * 2 * BLK)).T
    d_sq = d_sq.transpose(1, 0, 2).reshape(s_len, SWAW).astype(BF16)
    d_sk = d_sk.transpose(1, 0, 2).reshape(s_len, SWAKW).astype(BF16)
    d_sv = d_sv.transpose(1, 0, 2).reshape(s_len, SWAKW).astype(BF16)

    def post_bwd(r, f):
        _, vjp = jax.vjp(_f_dn_post, r[0], r[1], f[0])
        d_o, d_z, d_gain = vjp(r[2])
        return [d_o, d_z], [d_gain]

    d_o, d_z, grads["dn_out_norm"] = _rows(post_bwd, "dn_post_bwd", s_len, tm, post_ins + [_whole(d_ydn)], [out_gain],
                                           [(DNW, F32), (DNW, BF16)], [(1, DH)])
    d_q, d_k, d_v, d_gb, d_bb = _dn_chunks_bwd(q_dn, k_dn, v_dn, gb, bb, s_all, d_o)

    def pre_bwd(r, f):
        _, vjp = jax.vjp(_f_dn_pre, *r[:5], *f)
        dxs0, dxs1, dxs2, dxs3, dba, dcw, dalog, ddtb = vjp((r[5], r[6], r[7], r[8], r[9]))
        return [dxs0, dxs1, dxs2, dxs3, dba], [dcw, dalog, ddtb]

    dxs0, dxs1, dxs2, dxs3, d_ba, grads["dn_conv"], grads["dn_a_log"], grads["dn_dt_bias"] = _rows(
        pre_bwd, "dn_pre_bwd", s_len, tmh, pre_ins + [_whole(t) for t in (d_q, d_k, d_v, d_bb, d_gb)], pre_full,
        [(QKVW, F32)] * 4 + [(128, BF16)], [(CONV, QKVW), (1, DN_H), (1, DN_H)])
    shifted = [_shift_up(t, CONV - 1 - j) for j, t in enumerate((dxs0, dxs1, dxs2, dxs3))]
    (d_qkv,) = _rows(lambda r, f: ([r[0] + r[1] + r[2] + r[3]], []), "conv_bwd_sum", s_len, tm,
                     [_whole(t) for t in shifted], [], [(QKVW, BF16)])

    d_proj = jnp.concatenate(
        [d_qkv, d_z, d_gr, d_sq, d_sk, d_sv, d_ba, jnp.zeros((s_len, PW - C_BA - 128), BF16)], axis=1)
    grads["w_in"] = _unpad_w_in(_mm(h.T, d_proj, BF16, 512, 1024, "mm_gwin"))
    dh = _mm(d_proj, w_in_p.T, F32, tm, 1024, "mm_dh")
    grad_x, _, grads["attn_norm"] = _rows(rms_bwd, "rms1_bwd", s_len, tm, [_whole(x), _whole(dh), _whole(dx1)],
                                          [attn_gain], [(D, F32), (D, BF16)], [(1, D)])
    return loss, grad_x, grads


_HBM = pl.BlockSpec(memory_space=pl.ANY)


def _place():
    return lax.axis_index("x"), lax.axis_index("y"), lax.axis_index("c")


def _other_chips(x, y):
    return [(1 - x, y), (x, 1 - y), (1 - x, 1 - y)]


def _rcopy(src, dst, send_sems, recv_sems, k, to):
    return pltpu.make_async_remote_copy(src_ref=src, dst_ref=dst, send_sem=send_sems.at[k], recv_sem=recv_sems.at[k],
                                        device_id=to, device_id_type=MESH)


def _gather_weights(w_loc):
    def body(w_ref, out_ref, send_sems, recv_sems, loc_sem):
        x, y, c = _place()
        s = 2 * x + y
        sib = (x, y, 1 - c)
        chips = _other_chips(x, y)
        mine = pltpu.make_async_copy(w_ref, out_ref.at[s], loc_sem)
        mine.start()
        first = [_rcopy(w_ref.at[c], out_ref.at[s, c], send_sems, recv_sems, j, (cx, cy, c))
                 for j, (cx, cy) in enumerate(chips)]
        for cp in first:
            cp.start()
        passed = []
        for j, (cx, cy) in enumerate(chips):
            sj = 2 * cx + cy
            _rcopy(w_ref.at[c], out_ref.at[sj, c], send_sems, recv_sems, j, (cx, cy, c)).wait_recv()
            cp = _rcopy(out_ref.at[sj, c], out_ref.at[sj, c], send_sems, recv_sems, 3 + j, sib)
            cp.start()
            passed.append(cp)
        for j, (cx, cy) in enumerate(chips):
            sj = 2 * cx + cy
            _rcopy(out_ref.at[sj, 1 - c], out_ref.at[sj, 1 - c], send_sems, recv_sems, 3 + j, sib).wait_recv()
        for cp in first + passed:
            cp.wait_send()
        mine.wait()

    return pl.pallas_call(
        body, name="gather_weights", in_specs=[_HBM], out_specs=_HBM,
        out_shape=jax.ShapeDtypeStruct((N_CHIPS,) + w_loc.shape, w_loc.dtype),
        scratch_shapes=[pltpu.SemaphoreType.DMA((6,)), pltpu.SemaphoreType.DMA((6,)), pltpu.SemaphoreType.DMA],
        compiler_params=_cparams(has_side_effects=True),
    )(w_loc)


def _sibling_swap(a, name):
    def body(a_ref, out_ref, send_sems, recv_sems):
        x, y, c = _place()
        cp = _rcopy(a_ref, out_ref, send_sems, recv_sems, 0, (x, y, 1 - c))
        cp.start()
        cp.wait()

    return pl.pallas_call(
        body, name=name, in_specs=[_HBM], out_specs=_HBM, out_shape=jax.ShapeDtypeStruct(a.shape, a.dtype),
        scratch_shapes=[pltpu.SemaphoreType.DMA((1,)), pltpu.SemaphoreType.DMA((1,))],
        compiler_params=_cparams(has_side_effects=True),
    )(a)


def _chip_exchange(hsum):
    def body(h_ref, out_ref, send_sems, recv_sems, loc_sem):
        x, y, c = _place()
        s = 2 * x + y
        chips = _other_chips(x, y)
        mine = pltpu.make_async_copy(h_ref.at[s], out_ref.at[s], loc_sem)
        mine.start()
        sent = []
        for j, (cx, cy) in enumerate(chips):
            cp = _rcopy(h_ref.at[2 * cx + cy], out_ref.at[s], send_sems, recv_sems, j, (cx, cy, c))
            cp.start()
            sent.append(cp)
        for j, (cx, cy) in enumerate(chips):
            sj = 2 * cx + cy
            _rcopy(h_ref.at[sj], out_ref.at[sj], send_sems, recv_sems, j, (cx, cy, c)).wait_recv()
        for cp in sent:
            cp.wait_send()
        mine.wait()

    return pl.pallas_call(
        body, name="chip_exchange", in_specs=[_HBM], out_specs=_HBM,
        out_shape=jax.ShapeDtypeStruct(hsum.shape, hsum.dtype),
        scratch_shapes=[pltpu.SemaphoreType.DMA((3,)), pltpu.SemaphoreType.DMA((3,)), pltpu.SemaphoreType.DMA],
        compiler_params=_cparams(has_side_effects=True),
    )(hsum)


def _all_sum_small(vec):
    n_dev = 8
    flips = [(bx, by, bc) for bx in (0, 1) for by in (0, 1) for bc in (0, 1)][1:]

    def body(v_ref, out_ref, gath, send_sems, recv_sems):
        x, y, c = _place()
        me = 4 * x + 2 * y + c
        gath[me] = v_ref[...]
        sent = []
        for k, (bx, by, bc) in enumerate(flips):
            peer = (x ^ bx, y ^ by, c ^ bc)
            cp = _rcopy(v_ref, gath.at[me], send_sems, recv_sems, k, peer)
            cp.start()
            sent.append(cp)
        for k, (bx, by, bc) in enumerate(flips):
            peer = (x ^ bx, y ^ by, c ^ bc)
            _rcopy(v_ref, gath.at[4 * peer[0] + 2 * peer[1] + peer[2]], send_sems, recv_sems, k, peer).wait_recv()
        for cp in sent:
            cp.wait_send()
        acc = gath[0]
        for d in range(1, n_dev):
            acc = acc + gath[d]
        out_ref[...] = acc

    vm = pl.BlockSpec(memory_space=pltpu.VMEM)
    return pl.pallas_call(
        body, name="all_sum_small", in_specs=[vm], out_specs=vm, out_shape=jax.ShapeDtypeStruct(vec.shape, F32),
        scratch_shapes=[pltpu.VMEM((n_dev,) + vec.shape, F32), pltpu.SemaphoreType.DMA((7,)),
                        pltpu.SemaphoreType.DMA((7,))],
        compiler_params=_cparams(has_side_effects=True),
    )(vec)


def _shard_shape(shape, axis):
    return tuple(d // N_CHIPS if i == axis else d for i, d in enumerate(shape))


def _flat_local(blocks, dtype):
    flat = jnp.concatenate([blocks[n].reshape(-1).astype(dtype) for n, _, _ in _BIG])
    flat = jnp.concatenate([flat, jnp.zeros((_N_PAD - _N_LOCAL,), dtype)])
    return flat.reshape(2, _HALF_ROWS, 128)


def _full_from_shards(flat4):
    out = {}
    off = 0
    for n, shape, axis in _BIG:
        r, c = _shard_shape(shape, axis)
        piece = flat4[:, off:off + r * c].reshape(N_CHIPS, r, c)
        off += r * c
        if axis == 1:
            out[n] = piece.transpose(1, 0, 2).reshape(shape)
        else:
            out[n] = piece.reshape(shape)
    return out


def _shards_from_full(full, dtype):
    rows = []
    for s in range(N_CHIPS):
        parts = []
        for n, shape, axis in _BIG:
            r, c = _shard_shape(shape, axis)
            blk = full[n][:, s * c:(s + 1) * c] if axis == 1 else full[n][s * r:(s + 1) * r, :]
            parts.append(blk.reshape(-1).astype(dtype))
        parts.append(jnp.zeros((_N_PAD - _N_LOCAL,), dtype))
        rows.append(jnp.concatenate(parts))
    return jnp.stack(rows)


def _blocks_from_flat(flat):
    out = {}
    off = 0
    for n, shape, axis in _BIG:
        r, c = _shard_shape(shape, axis)
        out[n] = flat[off:off + r * c].reshape(r, c)
        off += r * c
    return out


def _pack_small(vals, extra=None):
    parts = [vals[n].reshape(-1).astype(F32) for n, _ in _SMALL]
    parts.append(jnp.zeros((1,), F32) if extra is None else extra.reshape(1).astype(F32))
    flat = jnp.concatenate(parts)
    flat = jnp.concatenate([flat, jnp.zeros((_SMALL_ROWS * 128 - flat.shape[0],), F32)])
    return flat.reshape(_SMALL_ROWS, 128)


def _unpack_small(packed, shapes):
    flat = packed.reshape(-1)
    return {n: flat[_SMALL_OFF[n][0]:_SMALL_OFF[n][0] + _SMALL_OFF[n][1]].reshape(shapes[n]) for n, _ in _SMALL}


def _pair_sum(a, b):
    n, r, l = a.shape
    tr = r // 4

    def kern(a_ref, b_ref, o_ref):
        o_ref[...] = (a_ref[...].astype(F32) + b_ref[...].astype(F32)).astype(o_ref.dtype)

    spec = pl.BlockSpec((1, tr, l), lambda i, j: (i, j, 0))
    return pl.pallas_call(
        kern, name="pair_sum", grid=(n, r // tr), in_specs=[spec, spec], out_specs=spec,
        out_shape=jax.ShapeDtypeStruct(a.shape, a.dtype),
        compiler_params=_cparams(dimension_semantics=("arbitrary", "arbitrary")),
    )(a, b)


def _chip_sum(q):
    n, r, l = q.shape
    tr = r // 4

    def kern(q_ref, o_ref):
        acc = q_ref[0].astype(F32)
        for s in range(1, n):
            acc = acc + q_ref[s].astype(F32)
        o_ref[...] = acc

    return pl.pallas_call(
        kern, name="chip_sum", grid=(r // tr,), in_specs=[pl.BlockSpec((n, tr, l), lambda j: (0, j, 0))],
        out_specs=pl.BlockSpec((tr, l), lambda j: (j, 0)), out_shape=jax.ShapeDtypeStruct((r, l), F32),
        compiler_params=_cparams(dimension_semantics=("arbitrary",)),
    )(q)


def _adamw(w, g, m, v, name):
    rows, cols = w.shape
    tr = rows
    for cand in (256, 128, 64, 32, 16, 8):
        if rows % cand == 0 and rows > cand:
            tr = cand
            break

    def kern(w_ref, g_ref, m_ref, v_ref, d_ref, nm_ref, nv_ref):
        g_ = g_ref[...]
        m_ = ADAM_B1 * m_ref[...] + (1.0 - ADAM_B1) * g_
        v_ = ADAM_B2 * v_ref[...] + (1.0 - ADAM_B2) * jnp.square(g_)
        m_hat = m_ / (1.0 - ADAM_B1 ** ADAM_STEP)
        v_hat = v_ / (1.0 - ADAM_B2 ** ADAM_STEP)
        d_ref[...] = -ADAM_LR * (m_hat / (jnp.sqrt(v_hat) + ADAM_EPS) + ADAM_WD * w_ref[...])
        nm_ref[...] = m_
        nv_ref[...] = v_

    spec = pl.BlockSpec((tr, cols), lambda i: (i, 0))
    return pl.pallas_call(
        kern, name=name, grid=(rows // tr,), in_specs=[spec] * 4, out_specs=[spec] * 3,
        out_shape=[jax.ShapeDtypeStruct(w.shape, F32)] * 3,
        compiler_params=_cparams(dimension_semantics=("arbitrary",)),
    )(w, g, m, v)


_WEIGHT_NAMES = ("attn_norm", "w_in", "dn_conv", "dn_a_log", "dn_dt_bias", "dn_out_norm", "swa_q_norm", "swa_k_norm",
                 "swa_sinks", "rel_bias", "w_branch_dn", "w_branch_swa", "w_out", "ffn_norm", "w_gate", "w_up",
                 "w_down")
_BIG_NAMES = tuple(n for n, _, _ in _BIG)


def kernel(x, attn_norm, w_in, dn_conv, dn_a_log, dn_dt_bias, dn_out_norm, swa_q_norm, swa_k_norm, swa_sinks, rel_bias, w_branch_dn, w_branch_swa, w_out, ffn_norm, w_gate, w_up, w_down, loss_target, m_attn_norm, m_w_in, m_dn_conv, m_dn_a_log, m_dn_dt_bias, m_dn_out_norm, m_swa_q_norm, m_swa_k_norm, m_swa_sinks, m_rel_bias, m_w_branch_dn, m_w_branch_swa, m_w_out, m_ffn_norm, m_w_gate, m_w_up, m_w_down, v_attn_norm, v_w_in, v_dn_conv, v_dn_a_log, v_dn_dt_bias, v_dn_out_norm, v_swa_q_norm, v_swa_k_norm, v_swa_sinks, v_rel_bias, v_w_branch_dn, v_w_branch_swa, v_w_out, v_ffn_norm, v_w_gate, v_w_up, v_w_down):
    w = dict(attn_norm=attn_norm, w_in=w_in, dn_conv=dn_conv, dn_a_log=dn_a_log, dn_dt_bias=dn_dt_bias,
             dn_out_norm=dn_out_norm, swa_q_norm=swa_q_norm, swa_k_norm=swa_k_norm, swa_sinks=swa_sinks,
             rel_bias=rel_bias, w_branch_dn=w_branch_dn, w_branch_swa=w_branch_swa, w_out=w_out, ffn_norm=ffn_norm,
             w_gate=w_gate, w_up=w_up, w_down=w_down)
    m = dict(attn_norm=m_attn_norm, w_in=m_w_in, dn_conv=m_dn_conv, dn_a_log=m_dn_a_log, dn_dt_bias=m_dn_dt_bias,
             dn_out_norm=m_dn_out_norm, swa_q_norm=m_swa_q_norm, swa_k_norm=m_swa_k_norm, swa_sinks=m_swa_sinks,
             rel_bias=m_rel_bias, w_branch_dn=m_w_branch_dn, w_branch_swa=m_w_branch_swa, w_out=m_w_out,
             ffn_norm=m_ffn_norm, w_gate=m_w_gate, w_up=m_w_up, w_down=m_w_down)
    v = dict(attn_norm=v_attn_norm, w_in=v_w_in, dn_conv=v_dn_conv, dn_a_log=v_dn_a_log, dn_dt_bias=v_dn_dt_bias,
             dn_out_norm=v_dn_out_norm, swa_q_norm=v_swa_q_norm, swa_k_norm=v_swa_k_norm, swa_sinks=v_swa_sinks,
             rel_bias=v_rel_bias, w_branch_dn=v_w_branch_dn, w_branch_swa=v_w_branch_swa, w_out=v_w_out,
             ffn_norm=v_ffn_norm, w_gate=v_w_gate, w_up=v_w_up, w_down=v_w_down)
    shapes = {n: w[n].shape for n in _WEIGHT_NAMES}

    def two_d(a):
        return a.reshape(a.shape[-2], a.shape[-1]) if a.ndim == 3 else a

    core = lax.axis_index("c")

    local_blocks = {n: two_d(w[n]) for n in _BIG_NAMES}
    gathered = _gather_weights(_flat_local(local_blocks, BF16))
    full = _full_from_shards(gathered.reshape(N_CHIPS, _N_PAD))
    wts = dict(full)
    wts["dn_conv"] = full["dn_conv"].astype(F32)
    for n, _ in _SMALL:
        wts[n] = two_d(w[n])

    loss_sum, grad_x, grads = _local_step(x[0], loss_target[0], wts)

    slabs = _shards_from_full(grads, BF16).reshape(N_CHIPS, 2, _HALF_ROWS, 128)
    keep = lax.dynamic_index_in_dim(slabs, core, axis=1, keepdims=False)
    give = lax.dynamic_index_in_dim(slabs, 1 - core, axis=1, keepdims=False)
    got = _sibling_swap(give, "swap_halves")
    chip_part = _pair_sum(keep, got)
    reduced_half = _chip_sum(_chip_exchange(chip_part))
    other_half = _sibling_swap(reduced_half, "swap_reduced")
    lo = jnp.where(core == 0, reduced_half, other_half)
    hi = jnp.where(core == 0, other_half, reduced_half)
    g_blocks = _blocks_from_flat(jnp.concatenate([lo.reshape(-1), hi.reshape(-1)]))

    small_sum = _all_sum_small(_pack_small(grads, loss_sum))
    loss = small_sum.reshape(-1)[_LOSS_OFF]
    g_small = _unpack_small(small_sum, {n: two_d(w[n]).shape for n, _ in _SMALL})

    g_out, d_out, m_out, v_out = {}, {}, {}, {}
    for n in _BIG_NAMES:
        g_out[n] = g_blocks[n].reshape(shapes[n])
        d_, m_, v_ = _adamw(local_blocks[n], g_blocks[n], two_d(m[n]), two_d(v[n]), "adamw_" + n)
        d_out[n], m_out[n], v_out[n] = d_.reshape(shapes[n]), m_.reshape(shapes[n]), v_.reshape(shapes[n])
    small_shapes = {n: two_d(w[n]).shape for n, _ in _SMALL}
    d_s, m_s, v_s = _adamw(_pack_small({n: w[n] for n, _ in _SMALL}), small_sum,
                           _pack_small({n: m[n] for n, _ in _SMALL}), _pack_small({n: v[n] for n, _ in _SMALL}),
                           "adamw_small")
    d_small, m_small, v_small = (_unpack_small(t, small_shapes) for t in (d_s, m_s, v_s))
    for n, _ in _SMALL:
        g_out[n] = g_small[n].reshape(shapes[n])
        d_out[n], m_out[n], v_out[n] = (t[n].reshape(shapes[n]) for t in (d_small, m_small, v_small))

    return (loss, grad_x[None], *[g_out[n] for n in _WEIGHT_NAMES], *[d_out[n] for n in _WEIGHT_NAMES],
            *[m_out[n] for n in _WEIGHT_NAMES], *[v_out[n] for n in _WEIGHT_NAMES])
```

```python
import functools
import math

import numpy as np
import jax
import jax.numpy as jnp
from jax import lax
from jax.experimental import pallas as pl
from jax.experimental.pallas import tpu as pltpu

F32 = jnp.float32
BF16 = jnp.bfloat16
SDS = jax.ShapeDtypeStruct

D = 1024
DN_H = 4
DH = 128
DNW = DN_H * DH
QKVW = 3 * DNW
CONV = 4
CHUNK = 64
SWA_H = 8
SWA_KV = 2
SWA_G = SWA_H // SWA_KV
SWA_D = 64
SWAW = SWA_H * SWA_D
SWAKW = SWA_KV * SWA_D
BLK = 128
NBUCKET = 32
MAXDIST = 128
DFF = 2816
D_IN = QKVW + DNW + 2 * DN_H + SWAW + 2 * SWAKW + 2 * D
EPS = 1e-6
NEG = -1e30

ADAM_LR = 0.001
ADAM_B1 = 0.9
ADAM_B2 = 0.999
ADAM_EPS = 1e-08
ADAM_WD = 0.01
ADAM_STEP = 10

C_QKV, C_Z, C_GATE, C_SQ, C_SK, C_SV, C_BA = 0, 1536, 2048, 4096, 4608, 4736, 4864
PW = 5120
_ORIG_PIECES = (
    (0, QKVW, C_QKV),
    (QKVW, DNW, C_Z),
    (QKVW + DNW, 2 * DN_H, C_BA),
    (QKVW + DNW + 2 * DN_H, SWAW, C_SQ),
    (QKVW + DNW + 2 * DN_H + SWAW, SWAKW, C_SK),
    (QKVW + DNW + 2 * DN_H + SWAW + SWAKW, SWAKW, C_SV),
    (QKVW + DNW + 2 * DN_H + SWAW + 2 * SWAKW, 2 * D, C_GATE),
)

N_CHIPS = 4
FSH = DFF // N_CHIPS
CSH = D // N_CHIPS
VMEM_LIMIT = 48 * 1024 * 1024
MESH = pl.DeviceIdType.MESH

_BIG = (
    ("w_in", D, D_IN // N_CHIPS),
    ("w_branch_dn", DNW, CSH),
    ("w_branch_swa", SWAW, CSH),
    ("w_out", CSH, D),
    ("w_gate", D, FSH),
    ("w_up", D, FSH),
    ("w_down", FSH, D),
)
_BIG_NAMES = tuple(n for n, _, _ in _BIG)

_SMALL = (
    ("attn_norm", D), ("ffn_norm", D), ("dn_out_norm", DH), ("swa_q_norm", SWA_D), ("swa_k_norm", SWA_D),
    ("swa_sinks", SWA_H), ("dn_a_log", DN_H), ("dn_dt_bias", DN_H), ("rel_bias", NBUCKET * SWA_H),
    ("dn_conv", CONV * QKVW),
)
_SMALL_OFF = {}
_o = 0
for _n, _s in _SMALL:
    _SMALL_OFF[_n] = (_o, _s)
    _o += _s
_LOSS_OFF = _o
_SMALL_ROWS = -(-(_o + 1) // (8 * 128)) * 8


def _cparams(**kw):
    return pltpu.CompilerParams(vmem_limit_bytes=VMEM_LIMIT, **kw)


_DIMS = {
    "nn": (((1,), (0,)), ((), ())),
    "nt": (((1,), (1,)), ((), ())),
    "tn": (((0,), (0,)), ((), ())),
    "bnn": (((2,), (1,)), ((0,), (0,))),
    "bnt": (((2,), (2,)), ((0,), (0,))),
    "btn": (((1,), (1,)), ((0,), (0,))),
}


def _raw_dot(a, b, kind, exact):
    if exact:
        return lax.dot_general(a, b, _DIMS[kind], precision=lax.Precision.HIGHEST, preferred_element_type=F32)
    return lax.dot_general(a.astype(BF16), b.astype(BF16), _DIMS[kind], preferred_element_type=F32)


@functools.partial(jax.custom_vjp, nondiff_argnums=(2, 3))
def _dot(a, b, kind, exact):
    return _raw_dot(a, b, kind, exact)


def _dot_fwd(a, b, kind, exact):
    return _raw_dot(a, b, kind, exact), (a, b)


def _dot_bwd(kind, exact, res, g):
    a, b = res
    pre = kind[:-2]
    nn, nt, tn = pre + "nn", pre + "nt", pre + "tn"
    if kind == nn:
        return _dot(g, b, nt, exact), _dot(a, g, tn, exact)
    if kind == nt:
        return _dot(g, b, nn, exact), _dot(g, a, tn, exact)
    return _dot(b, g, nt, exact), _dot(a, g, nn, exact)


_dot.defvjp(_dot_fwd, _dot_bwd)


def _silu(x):
    return x * jax.nn.sigmoid(x)


def _f_rms(x, gain):
    return x * lax.rsqrt(jnp.mean(x * x, axis=-1, keepdims=True) + EPS) * gain


def _f_dn_pre(xs0, xs1, xs2, xs3, ba, cw, alog, dtb):
    rows = xs0.shape[0]
    c = xs0 * cw[0:1] + xs1 * cw[1:2] + xs2 * cw[2:3] + xs3 * cw[3:4]
    qkv = _silu(c)
    qs, ks, bbs, gbs = [], [], [], []
    for h in range(DN_H):
        qh = qkv[:, h * DH:(h + 1) * DH]
        kh = qkv[:, DNW + h * DH:DNW + (h + 1) * DH]
        qs.append(qh * lax.rsqrt(jnp.sum(qh * qh, axis=-1, keepdims=True) + EPS) * (DH ** -0.5))
        ks.append(kh * lax.rsqrt(jnp.sum(kh * kh, axis=-1, keepdims=True) + EPS))
        beta = jax.nn.sigmoid(ba[:, h:h + 1])
        ar = ba[:, DN_H + h:DN_H + h + 1] + dtb[:, h:h + 1]
        softplus = jnp.maximum(ar, 0.0) + jnp.log1p(jnp.exp(-jnp.abs(ar)))
        g = -jnp.exp(alog[:, h:h + 1]) * softplus
        bbs.append(jnp.broadcast_to(beta, (rows, DH)))
        gbs.append(jnp.broadcast_to(g, (rows, DH)))
    return (jnp.concatenate(qs, axis=1), jnp.concatenate(ks, axis=1), qkv[:, 2 * DNW:],
            jnp.concatenate(bbs, axis=1), jnp.concatenate(gbs, axis=1))


def _f_dn_post(o, z, gain):
    ys = []
    for h in range(DN_H):
        oh = o[:, h * DH:(h + 1) * DH]
        zh = z[:, h * DH:(h + 1) * DH]
        ys.append(oh * lax.rsqrt(jnp.mean(oh * oh, axis=-1, keepdims=True) + EPS) * gain * _silu(zh))
    return jnp.concatenate(ys, axis=1)


def _f_merge(pa, pb, ga, gb):
    return jax.nn.sigmoid(ga) * pa + jax.nn.sigmoid(gb) * pb


def _f_swiglu(g, u):
    return _silu(g) * u


def _f_chunk(q, k, v, gb, bb, s):
    c = CHUNK
    nh = q.shape[0]
    ii = lax.broadcasted_iota(jnp.int32, (nh, c, c), 1)
    jj = lax.broadcasted_iota(jnp.int32, (nh, c, c), 2)
    incl = ii >= jj
    strict = ii > jj
    eye = (ii == jj).astype(F32)
    gcb = _dot(incl.astype(F32), gb, "bnn", True)
    lane0 = (lax.broadcasted_iota(jnp.int32, (nh, c, DH), 2) == 0).astype(F32)
    gcol = gcb[:, :, :c]
    grow = _dot(lane0, gcb, "bnt", True)
    decay = jnp.where(incl, jnp.exp(jnp.where(incl, gcol - grow, 0.0)), 0.0)
    kb = k * bb
    vb = v * bb
    a = jnp.where(strict, _dot(kb, k, "bnt", False) * decay, 0.0)
    p = -a
    t = eye + p
    for _ in range(5):
        p = _dot(p, p, "bnn", True)
        t = t + _dot(t, p, "bnn", True)
    eg = jnp.exp(gcb)
    u = _dot(t, vb, "bnn", True)
    w = _dot(t, kb * eg, "bnn", True)
    qk = jnp.where(incl, _dot(q, k, "bnt", False) * decay, 0.0)
    v_new = u - _dot(w, s, "bnn", False)
    o = _dot(q * eg, s, "bnn", False) + _dot(qk, v_new, "bnn", False)
    glast = gcb[:, c - 1:c, :]
    k_dec = k * jnp.exp(glast - gcb)
    s_new = s * jnp.exp(glast) + _dot(k_dec, v_new, "btn", False)
    return o, s_new


def _f_swa(q4, kp, kc, vp, vc, bias4, qg, kg, sink, mask):
    kb = jnp.concatenate([kp, kc], axis=0)
    vb = jnp.concatenate([vp, vc], axis=0)
    kn = kb * lax.rsqrt(jnp.mean(kb * kb, axis=-1, keepdims=True) + EPS) * kg
    outs = []
    for g in range(SWA_G):
        qq = q4[g]
        qn = qq * lax.rsqrt(jnp.mean(qq * qq, axis=-1, keepdims=True) + EPS) * qg
        lg = _dot(qn, kn, "nt", False) * (SWA_D ** -0.5) + bias4[g]
        lg = jnp.where(mask, lg, NEG)
        sk = sink[:, g:g + 1]
        m = lax.stop_gradient(jnp.maximum(jnp.max(lg, axis=-1, keepdims=True), sk))
        p = jnp.exp(lg - m)
        den = jnp.sum(p, axis=-1, keepdims=True) + jnp.exp(sk - m)
        outs.append(_dot(p / den, vb, "nn", False))
    return jnp.stack(outs, axis=0)


def _bdot(a, b, kind="nn"):
    return lax.dot_general(a.astype(BF16), b.astype(BF16), _DIMS[kind], preferred_element_type=F32)


def _pc(kern, name, grid, in_specs, out_specs, out_shape, scratch=()):
    return pl.pallas_call(
        kern, name=name, grid=grid, in_specs=in_specs, out_specs=out_specs, out_shape=out_shape,
        scratch_shapes=list(scratch), compiler_params=_cparams(dimension_semantics=("arbitrary",) * len(grid)))


def _mm(a, b, kind, out_dtype, tm, tn, name):
    if kind == "tn":
        k, m = a.shape
    else:
        m, k = a.shape
    n = b.shape[0] if kind == "nt" else b.shape[1]
    tm, tn = min(tm, m), min(tn, n)
    assert m % tm == 0 and n % tn == 0, (name, a.shape, b.shape, tm, tn)

    def kern(a_ref, b_ref, o_ref):
        o_ref[...] = _bdot(a_ref[...], b_ref[...], kind).astype(o_ref.dtype)

    a_spec = pl.BlockSpec((k, tm), lambda i, j: (0, i)) if kind == "tn" else pl.BlockSpec((tm, k), lambda i, j: (i, 0))
    b_spec = pl.BlockSpec((tn, k), lambda i, j: (j, 0)) if kind == "nt" else pl.BlockSpec((k, tn), lambda i, j: (0, j))
    return _pc(kern, name, (m // tm, n // tn), [a_spec, b_spec], pl.BlockSpec((tm, tn), lambda i, j: (i, j)),
               SDS((m, n), out_dtype))(a, b)


def _rows(body, name, m, tm, row_ins, full_ins, row_outs, acc_outs=()):
    n_r, n_f, n_o, n_a = len(row_ins), len(full_ins), len(row_outs), len(acc_outs)
    assert m % tm == 0

    def kern(*refs):
        r = refs[:n_r]
        f = refs[n_r:n_r + n_f]
        o = refs[n_r + n_f:n_r + n_f + n_o]
        acc = refs[n_r + n_f + n_o:]
        outs, sums = body([x[...] for x in r], [x[...] for x in f])
        for ref, val in zip(o, outs, strict=True):
            ref[...] = val.astype(ref.dtype)
        if n_a:
            @pl.when(pl.program_id(0) == 0)
            def _():
                for ref in acc:
                    ref[...] = jnp.zeros(ref.shape, F32)

            for ref, val in zip(acc, sums, strict=True):
                ref[...] += val

    in_specs = [pl.BlockSpec((tm, w), functools.partial(lambda i, cb: (i, cb), cb=cb)) for _, w, cb in row_ins]
    in_specs += [pl.BlockSpec(x.shape, lambda i: (0, 0)) for x in full_ins]
    out_specs = [pl.BlockSpec((tm, w), lambda i: (i, 0)) for w, _ in row_outs]
    out_specs += [pl.BlockSpec(s, lambda i: (0, 0)) for s in acc_outs]
    out_shape = [SDS((m, w), dt) for w, dt in row_outs]
    out_shape += [SDS(s, F32) for s in acc_outs]
    return _pc(kern, name, (m // tm,), in_specs, out_specs, out_shape)(*[x for x, _, _ in row_ins], *full_ins)


def _whole(x):
    return (x, x.shape[1], 0)


def _zero_first(refs):
    @pl.when(pl.program_id(0) == 0)
    def _():
        for ref in refs:
            ref[...] = jnp.zeros(ref.shape, F32)


def _heads(ref):
    return jnp.stack([ref[:, h * DH:(h + 1) * DH] for h in range(DN_H)], axis=0)


def _dn_chunks_fwd(q, k, v, gb, bb):
    s_len = q.shape[0]
    nc = s_len // CHUNK

    def kern(q_ref, k_ref, v_ref, g_ref, b_ref, o_ref, sall_ref, state):
        _zero_first([state])
        s = state[...]
        sall_ref[0] = s
        o, s_new = _f_chunk(*[_heads(r) for r in (q_ref, k_ref, v_ref, g_ref, b_ref)], s)
        for h in range(DN_H):
            o_ref[:, h * DH:(h + 1) * DH] = o[h]
        state[...] = s_new

    blk = pl.BlockSpec((CHUNK, DNW), lambda c: (c, 0))
    return _pc(kern, "dn_chunks_fwd", (nc,), [blk] * 5,
               [blk, pl.BlockSpec((1, DN_H, DH, DH), lambda c: (c, 0, 0, 0))],
               [SDS((s_len, DNW), F32), SDS((nc, DN_H, DH, DH), F32)],
               scratch=[pltpu.VMEM((DN_H, DH, DH), F32)])(q, k, v, gb, bb)


def _dn_chunks_bwd(q, k, v, gb, bb, s_all, d_o):
    s_len = q.shape[0]
    nc = s_len // CHUNK

    def kern(q_ref, k_ref, v_ref, g_ref, b_ref, sall_ref, do_ref, dq_ref, dk_ref, dv_ref, dg_ref, db_ref, dstate):
        _zero_first([dstate])
        _, vjp = jax.vjp(_f_chunk, *[_heads(r) for r in (q_ref, k_ref, v_ref, g_ref, b_ref)], sall_ref[0])
        *d_ins, ds = vjp((_heads(do_ref), dstate[...]))
        for ref, val in zip((dq_ref, dk_ref, dv_ref, dg_ref, db_ref), d_ins, strict=True):
            for h in range(DN_H):
                ref[:, h * DH:(h + 1) * DH] = val[h]
        dstate[...] = ds

    blk = pl.BlockSpec((CHUNK, DNW), lambda c: (nc - 1 - c, 0))
    return _pc(kern, "dn_chunks_bwd", (nc,),
               [blk] * 5 + [pl.BlockSpec((1, DN_H, DH, DH), lambda c: (nc - 1 - c, 0, 0, 0)), blk],
               [blk] * 5, [SDS((s_len, DNW), F32)] * 5,
               scratch=[pltpu.VMEM((DN_H, DH, DH), F32)])(q, k, v, gb, bb, s_all, d_o)


def _t5_bucket_table():
    qi = np.arange(BLK)[:, None]
    kj = np.arange(2 * BLK)[None, :]
    dist = BLK + qi - kj
    n = np.maximum(dist, 0)
    max_exact = NBUCKET // 2
    nf = np.maximum(n, 1).astype(np.float32)
    large = max_exact + (np.log(nf / np.float32(max_exact)) / np.float32(math.log(MAXDIST / max_exact))
                         * np.float32(NBUCKET - max_exact)).astype(np.int32)
    large = np.minimum(large, NBUCKET - 1)
    return np.where(n < max_exact, n, large)


def _bucket_onehot_t():
    table = _t5_bucket_table().reshape(-1)
    return (np.arange(NBUCKET)[:, None] == table[None, :]).astype(np.float32)


def _swa_mask(first):
    qi = lax.broadcasted_iota(jnp.int32, (BLK, 2 * BLK), 0)
    kj = lax.broadcasted_iota(jnp.int32, (BLK, 2 * BLK), 1)
    dist = BLK + qi - kj
    window = (dist >= 0) & (dist < BLK)
    return window & ((kj >= BLK) | jnp.logical_not(first))


def _bias_expand(rel_bias_t):
    onehot = jnp.asarray(_bucket_onehot_t())

    def kern(r_ref, oh_ref, o_ref):
        o_ref[...] = _raw_dot(r_ref[...], oh_ref[...], "nn", True)

    return pl.pallas_call(
        kern, name="bias_expand", out_shape=SDS((SWA_H, BLK * 2 * BLK), F32), compiler_params=_cparams(),
    )(rel_bias_t, onehot)


def _bias_reduce(d_bias_flat):
    onehot = jnp.asarray(_bucket_onehot_t())

    def kern(d_ref, oh_ref, o_ref):
        o_ref[...] = _raw_dot(d_ref[...], oh_ref[...], "nt", True)

    return pl.pallas_call(
        kern, name="bias_reduce", out_shape=SDS((SWA_H, NBUCKET), F32), compiler_params=_cparams(),
    )(d_bias_flat, onehot)


def _swa_specs(nb, rev):
    def blk(n):
        return (nb - 1 - n) if rev else n

    q_spec = pl.BlockSpec((SWA_G, BLK, SWA_D), lambda kv, n: (kv, blk(n), 0))
    cur = pl.BlockSpec((1, BLK, SWA_D), lambda kv, n: (kv, blk(n), 0))
    prev = pl.BlockSpec((1, BLK, SWA_D), lambda kv, n: (kv, jnp.maximum(blk(n) - 1, 0), 0))
    bias = pl.BlockSpec((SWA_G, BLK, 2 * BLK), lambda kv, n: (kv, 0, 0))
    gain = pl.BlockSpec((1, SWA_D), lambda kv, n: (0, 0))
    sink = pl.BlockSpec((1, 1, SWA_G), lambda kv, n: (kv, 0, 0))
    return q_spec, cur, prev, bias, gain, sink


def _swa_fwd(q, k, v, bias, qg, kg, sinks):
    s_len = q.shape[1]
    nb = s_len // BLK
    q_spec, cur, prev, bias_spec, gain, sink = _swa_specs(nb, False)

    def kern(q_ref, kp_ref, kc_ref, vp_ref, vc_ref, b_ref, qg_ref, kg_ref, s_ref, o_ref):
        mask = _swa_mask(pl.program_id(1) == 0)
        o_ref[...] = _f_swa(q_ref[...], kp_ref[0], kc_ref[0], vp_ref[0], vc_ref[0], b_ref[...], qg_ref[...],
                            kg_ref[...], s_ref[0], mask)

    return _pc(kern, "swa_fwd", (SWA_KV, nb), [q_spec, prev, cur, prev, cur, bias_spec, gain, gain, sink],
               q_spec, SDS((SWA_H, s_len, SWA_D), F32))(q, k, k, v, v, bias, qg, kg, sinks)


def _swa_bwd(q, k, v, bias, qg, kg, sinks, d_out):
    s_len = q.shape[1]
    nb = s_len // BLK
    q_spec, cur, prev, bias_spec, gain, sink = _swa_specs(nb, True)

    def kern(q_ref, kp_ref, kc_ref, vp_ref, vc_ref, b_ref, qg_ref, kg_ref, s_ref, do_ref,
             dq_ref, dk_ref, dv_ref, db_ref, dqg_ref, dkg_ref, ds_ref, carry_k, carry_v):
        kv = pl.program_id(0)
        n = pl.program_id(1)
        mask = _swa_mask(n == nb - 1)

        @pl.when(n == 0)
        def _():
            carry_k[...] = jnp.zeros(carry_k.shape, F32)
            carry_v[...] = jnp.zeros(carry_v.shape, F32)
            db_ref[...] = jnp.zeros(db_ref.shape, F32)
            ds_ref[...] = jnp.zeros(ds_ref.shape, F32)

        @pl.when((n == 0) & (kv == 0))
        def _():
            dqg_ref[...] = jnp.zeros(dqg_ref.shape, F32)
            dkg_ref[...] = jnp.zeros(dkg_ref.shape, F32)

        fn = functools.partial(_f_swa, mask=mask)
        _, vjp = jax.vjp(fn, q_ref[...], kp_ref[0], kc_ref[0], vp_ref[0], vc_ref[0], b_ref[...], qg_ref[...],
                         kg_ref[...], s_ref[0])
        dq, dkp, dkc, dvp, dvc, dbias, dqg, dkg, dsink = vjp(do_ref[...])
        dq_ref[...] = dq
        dk_ref[0] = dkc + carry_k[...]
        dv_ref[0] = dvc + carry_v[...]
        carry_k[...] = dkp
        carry_v[...] = dvp
        db_ref[...] += dbias
        dqg_ref[...] += dqg
        dkg_ref[...] += dkg
        ds_ref[0] += dsink

    return _pc(
        kern, "swa_bwd", (SWA_KV, nb),
        [q_spec, prev, cur, prev, cur, bias_spec, gain, gain, sink, q_spec],
        [q_spec, cur, cur, bias_spec, gain, gain, sink],
        [SDS((SWA_H, s_len, SWA_D), F32), SDS((SWA_KV, s_len, SWA_D), F32), SDS((SWA_KV, s_len, SWA_D), F32),
         SDS((SWA_H, BLK, 2 * BLK), F32), SDS((1, SWA_D), F32), SDS((1, SWA_D), F32), SDS((SWA_KV, 1, SWA_G), F32)],
        scratch=[pltpu.VMEM((BLK, SWA_D), F32), pltpu.VMEM((BLK, SWA_D), F32)],
    )(q, k, k, v, v, bias, qg, kg, sinks, d_out)


def _branch_merge(y_dn, y_swa, wa, wb, proj):
    s_len = y_dn.shape[0]
    tm = min(512, s_len)

    def kern(ya_ref, yb_ref, wa_ref, wb_ref, ga_ref, gb_ref, pa_ref, pb_ref, m_ref):
        pa = _bdot(ya_ref[...], wa_ref[0])
        pb = _bdot(yb_ref[...], wb_ref[0])
        pa_ref[...] = pa
        pb_ref[...] = pb
        m_ref[...] = _f_merge(pa, pb, ga_ref[...], gb_ref[...]).astype(BF16)

    y_spec = pl.BlockSpec((tm, DNW), lambda i, s: (i, 0))
    w_spec = pl.BlockSpec((1, DNW, CSH), lambda i, s: (s, 0, 0))
    o_spec = pl.BlockSpec((tm, CSH), lambda i, s: (i, s))
    ga_spec = pl.BlockSpec((tm, CSH), lambda i, s: (i, C_GATE // CSH + s))
    gb_spec = pl.BlockSpec((tm, CSH), lambda i, s: (i, (C_GATE + D) // CSH + s))
    return _pc(kern, "branch_merge", (s_len // tm, N_CHIPS), [y_spec, y_spec, w_spec, w_spec, ga_spec, gb_spec],
               [o_spec] * 3, [SDS((s_len, D), F32), SDS((s_len, D), F32), SDS((s_len, D), BF16)],
               )(y_dn, y_swa, wa, wb, proj, proj)


def _out_proj(merged, w_out, x, gain):
    s_len = x.shape[0]
    tm = min(256, s_len)

    def kern(m_ref, w_ref, x_ref, g_ref, x1_ref, h2_ref):
        x1 = x_ref[...] + _bdot(m_ref[...], w_ref[...])
        x1_ref[...] = x1
        h2_ref[...] = _f_rms(x1, g_ref[...]).astype(BF16)

    row = pl.BlockSpec((tm, D), lambda i: (i, 0))
    return _pc(kern, "out_proj", (s_len // tm,),
               [row, pl.BlockSpec((D, D), lambda i: (0, 0)), row, pl.BlockSpec((1, D), lambda i: (0, 0))],
               [row, row], [SDS((s_len, D), F32), SDS((s_len, D), BF16)])(merged, w_out, x, gain)


def _ffn_up(h2, wg, wu):
    s_len = h2.shape[0]
    tm = min(512, s_len)

    def kern(h_ref, g_ref, u_ref, gt_ref, up_ref, act_ref):
        h = h_ref[...]
        g = _bdot(h, g_ref[0])
        u = _bdot(h, u_ref[0])
        gt_ref[0] = g
        up_ref[0] = u
        act_ref[0] = _f_swiglu(g, u).astype(BF16)

    w_spec = pl.BlockSpec((1, D, FSH), lambda s, i: (s, 0, 0))
    o_spec = pl.BlockSpec((1, tm, FSH), lambda s, i: (s, i, 0))
    shape = (N_CHIPS, s_len, FSH)
    return _pc(kern, "ffn_up", (N_CHIPS, s_len // tm), [pl.BlockSpec((tm, D), lambda s, i: (i, 0)), w_spec, w_spec],
               [o_spec] * 3, [SDS(shape, F32), SDS(shape, F32), SDS(shape, BF16)])(h2, wg, wu)


def _ffn_down_loss(act, wd, x1, target):
    s_len = x1.shape[0]
    tm = min(256, s_len)

    def kern(a_ref, w_ref, x_ref, t_ref, dy_ref, dyb_ref, loss_ref):
        _zero_first([loss_ref])
        y = x_ref[...]
        for s in range(N_CHIPS):
            y = y + _bdot(a_ref[s], w_ref[s])
        d = y - t_ref[...]
        dy = d * (1.0 / D)
        dy_ref[...] = dy
        dyb_ref[...] = dy.astype(BF16)
        loss_ref[...] += jnp.sum(d * d).reshape(1, 1) * (0.5 / D)

    row = pl.BlockSpec((tm, D), lambda i: (i, 0))
    return _pc(kern, "ffn_down_loss", (s_len // tm,),
               [pl.BlockSpec((N_CHIPS, tm, FSH), lambda i: (0, i, 0)),
                pl.BlockSpec((N_CHIPS, FSH, D), lambda i: (0, 0, 0)), row, row],
               [row, row, pl.BlockSpec((1, 1), lambda i: (0, 0))],
               [SDS((s_len, D), F32), SDS((s_len, D), BF16), SDS((1, 1), F32)])(act, wd, x1, target)


def _ffn_dact(dy_b, wd, gt, up):
    s_len = dy_b.shape[0]
    tm = min(512, s_len)

    def kern(dy_ref, w_ref, gt_ref, up_ref, dg_ref, du_ref):
        d_act = _bdot(dy_ref[...], w_ref[0], "nt")
        _, vjp = jax.vjp(_f_swiglu, gt_ref[0], up_ref[0])
        dg, du = vjp(d_act)
        dg_ref[0] = dg.astype(BF16)
        du_ref[0] = du.astype(BF16)

    a_spec = pl.BlockSpec((1, tm, FSH), lambda s, i: (s, i, 0))
    shape = (N_CHIPS, s_len, FSH)
    return _pc(kern, "ffn_dact", (N_CHIPS, s_len // tm),
               [pl.BlockSpec((tm, D), lambda s, i: (i, 0)), pl.BlockSpec((1, FSH, D), lambda s, i: (s, 0, 0)),
                a_spec, a_spec],
               [a_spec, a_spec], [SDS(shape, BF16), SDS(shape, BF16)])(dy_b, wd, gt, up)


def _gw_down(act, dy_b):
    s_len = dy_b.shape[0]
    tn = 512

    def kern(a_ref, g_ref, o_ref):
        o_ref[0] = _bdot(a_ref[0], g_ref[...], "tn").astype(BF16)

    return _pc(kern, "gw_down", (N_CHIPS, D // tn),
               [pl.BlockSpec((1, s_len, FSH), lambda s, j: (s, 0, 0)), pl.BlockSpec((s_len, tn), lambda s, j: (0, j))],
               pl.BlockSpec((1, FSH, tn), lambda s, j: (s, 0, j)), SDS((N_CHIPS, FSH, D), BF16))(act, dy_b)


def _gw_gate_up(h2, d_gt, d_up):
    s_len = h2.shape[0]
    tk = 256

    def kern(h_ref, dg_ref, du_ref, og_ref, ou_ref):
        h = h_ref[...]
        og_ref[0] = _bdot(h, dg_ref[0], "tn").astype(BF16)
        ou_ref[0] = _bdot(h, du_ref[0], "tn").astype(BF16)

    d_spec = pl.BlockSpec((1, s_len, FSH), lambda s, j: (s, 0, 0))
    o_spec = pl.BlockSpec((1, tk, FSH), lambda s, j: (s, j, 0))
    shape = (N_CHIPS, D, FSH)
    return _pc(kern, "gw_gate_up", (N_CHIPS, D // tk), [pl.BlockSpec((s_len, tk), lambda s, j: (0, j)), d_spec, d_spec],
               [o_spec, o_spec], [SDS(shape, BF16), SDS(shape, BF16)])(h2, d_gt, d_up)


def _ffn_dh2(d_gt, d_up, wg, wu, x1, dy, gain):
    s_len = x1.shape[0]
    tm = min(256, s_len)

    def kern(dg_ref, du_ref, wg_ref, wu_ref, x_ref, dy_ref, g_ref, dx_ref, dxb_ref, dgain_ref):
        _zero_first([dgain_ref])
        dh2 = jnp.zeros((tm, D), F32)
        for s in range(N_CHIPS):
            dh2 = dh2 + _bdot(dg_ref[s], wg_ref[s], "nt") + _bdot(du_ref[s], wu_ref[s], "nt")
        _, vjp = jax.vjp(_f_rms, x_ref[...], g_ref[...])
        dx, dgain = vjp(dh2)
        dx1 = dx + dy_ref[...]
        dx_ref[...] = dx1
        dxb_ref[...] = dx1.astype(BF16)
        dgain_ref[...] += dgain

    row = pl.BlockSpec((tm, D), lambda i: (i, 0))
    d_spec = pl.BlockSpec((N_CHIPS, tm, FSH), lambda i: (0, i, 0))
    w_spec = pl.BlockSpec((N_CHIPS, D, FSH), lambda i: (0, 0, 0))
    vec = pl.BlockSpec((1, D), lambda i: (0, 0))
    return _pc(kern, "ffn_dh2", (s_len // tm,), [d_spec, d_spec, w_spec, w_spec, row, row, vec],
               [row, row, vec], [SDS((s_len, D), F32), SDS((s_len, D), BF16), SDS((1, D), F32)],
               )(d_gt, d_up, wg, wu, x1, dy, gain)


def _merge_bwd(dx1_b, w_out, pa, pb, proj):
    s_len = dx1_b.shape[0]
    tm = min(256, s_len)

    def kern(dx_ref, w_ref, pa_ref, pb_ref, g_ref, dpa_ref, dpb_ref, dg_ref):
        dm = _bdot(dx_ref[...], w_ref[...], "nt")
        gates = g_ref[...]
        _, vjp = jax.vjp(_f_merge, pa_ref[...], pb_ref[...], gates[:, :D], gates[:, D:])
        dpa, dpb, dga, dgb = vjp(dm)
        dpa_ref[...] = dpa.astype(BF16)
        dpb_ref[...] = dpb.astype(BF16)
        dg_ref[:, :D] = dga.astype(BF16)
        dg_ref[:, D:] = dgb.astype(BF16)

    row = pl.BlockSpec((tm, D), lambda i: (i, 0))
    return _pc(kern, "merge_bwd", (s_len // tm,),
               [row, pl.BlockSpec((D, D), lambda i: (0, 0)), row, row,
                pl.BlockSpec((tm, 2 * D), lambda i: (i, C_GATE // (2 * D)))],
               [row, row, pl.BlockSpec((tm, 2 * D), lambda i: (i, 0))],
               [SDS((s_len, D), BF16), SDS((s_len, D), BF16), SDS((s_len, 2 * D), BF16)],
               )(dx1_b, w_out, pa, pb, proj)


def _d_branch(d_pa, d_pb, wa, wb):
    s_len = d_pa.shape[0]
    tm = min(512, s_len)

    def kern(da_ref, db_ref, wa_ref, wb_ref, oa_ref, ob_ref):
        acc_a = jnp.zeros((tm, DNW), F32)
        acc_b = jnp.zeros((tm, SWAW), F32)
        for s in range(N_CHIPS):
            acc_a = acc_a + _bdot(da_ref[:, s * CSH:(s + 1) * CSH], wa_ref[s], "nt")
            acc_b = acc_b + _bdot(db_ref[:, s * CSH:(s + 1) * CSH], wb_ref[s], "nt")
        oa_ref[...] = acc_a
        ob_ref[...] = acc_b

    row = pl.BlockSpec((tm, D), lambda i: (i, 0))
    w_spec = pl.BlockSpec((N_CHIPS, DNW, CSH), lambda i: (0, 0, 0))
    out = pl.BlockSpec((tm, DNW), lambda i: (i, 0))
    return _pc(kern, "d_branch", (s_len // tm,), [row, row, w_spec, w_spec], [out, out],
               [SDS((s_len, DNW), F32), SDS((s_len, SWAW), F32)])(d_pa, d_pb, wa, wb)


def _gw_branch(y_dn, y_swa, d_pa, d_pb):
    s_len = y_dn.shape[0]

    def kern(ya_ref, yb_ref, da_ref, db_ref, oa_ref, ob_ref):
        oa_ref[0] = _bdot(ya_ref[...], da_ref[...], "tn").astype(BF16)
        ob_ref[0] = _bdot(yb_ref[...], db_ref[...], "tn").astype(BF16)

    y_spec = pl.BlockSpec((s_len, DNW), lambda s: (0, 0))
    d_spec = pl.BlockSpec((s_len, CSH), lambda s: (0, s))
    o_spec = pl.BlockSpec((1, DNW, CSH), lambda s: (s, 0, 0))
    shape = (N_CHIPS, DNW, CSH)
    return _pc(kern, "gw_branch", (N_CHIPS,), [y_spec, y_spec, d_spec, d_spec], [o_spec, o_spec],
               [SDS(shape, BF16), SDS(shape, BF16)])(y_dn, y_swa, d_pa, d_pb)


def _dh_rms(d_proj, w_in_p, x, dx1, gain):
    s_len = x.shape[0]
    tm = min(256, s_len)

    def kern(dp_ref, w_ref, x_ref, r_ref, g_ref, gx_ref, dgain_ref):
        _zero_first([dgain_ref])
        dh = _bdot(dp_ref[...], w_ref[...], "nt")
        _, vjp = jax.vjp(_f_rms, x_ref[...], g_ref[...])
        dx, dgain = vjp(dh)
        gx_ref[...] = dx + r_ref[...]
        dgain_ref[...] += dgain

    row = pl.BlockSpec((tm, D), lambda i: (i, 0))
    vec = pl.BlockSpec((1, D), lambda i: (0, 0))
    return _pc(kern, "dh_rms", (s_len // tm,),
               [pl.BlockSpec((tm, PW), lambda i: (i, 0)), pl.BlockSpec((D, PW), lambda i: (0, 0)), row, row, vec],
               [row, vec], [SDS((s_len, D), F32), SDS((1, D), F32)])(d_proj, w_in_p, x, dx1, gain)


def _shift_down(a, n):
    if n == 0:
        return a
    return jnp.pad(a, ((n, 0), (0, 0)))[:a.shape[0]]


def _shift_up(a, n):
    if n == 0:
        return a
    return jnp.pad(a, ((0, n), (0, 0)))[n:]


def _pad_w_in(w_in):
    pieces = [w_in[:, o0:o0 + w] for o0, w, _ in sorted(_ORIG_PIECES, key=lambda t: t[2])]
    pieces.append(jnp.zeros((w_in.shape[0], PW - D_IN), w_in.dtype))
    return jnp.concatenate(pieces, axis=1)


def _unpad_w_in(g):
    return jnp.concatenate([g[:, p0:p0 + w] for _, w, p0 in _ORIG_PIECES], axis=1)


def _local_step(x, target, wts):
    s_len = x.shape[0]
    tm = min(256, s_len)
    tmh = min(128, s_len)
    w_in_p = wts["w_in_p"]
    attn_gain = wts["attn_norm"]
    ffn_gain = wts["ffn_norm"]
    conv_w = wts["dn_conv"]
    alog, dtb, out_gain = wts["dn_a_log"], wts["dn_dt_bias"], wts["dn_out_norm"]
    qg, kg = wts["swa_q_norm"], wts["swa_k_norm"]
    sinks = wts["swa_sinks"].reshape(SWA_KV, 1, SWA_G)

    (h,) = _rows(lambda r, f: ([_f_rms(r[0], f[0])], []), "rms1_fwd", s_len, tm, [_whole(x)], [attn_gain],
                 [(D, BF16)])
    proj = _mm(h, w_in_p, "nn", F32, 512, 1024, "mm_proj")
    qkv_pre = proj[:, :QKVW]
    xs = [_shift_down(qkv_pre, CONV - 1 - j) for j in range(CONV - 1)]
    pre_ins = [_whole(xs[0]), _whole(xs[1]), _whole(xs[2]), (proj, QKVW, 0), (proj, 128, C_BA // 128)]
    pre_full = [conv_w, alog, dtb]
    q_dn, k_dn, v_dn, bb, gb = _rows(lambda r, f: (list(_f_dn_pre(*r, *f)), []), "dn_pre_fwd", s_len, tmh, pre_ins,
                                     pre_full, [(DNW, F32)] * 5)
    o_dn, s_all = _dn_chunks_fwd(q_dn, k_dn, v_dn, gb, bb)
    post_ins = [_whole(o_dn), (proj, DNW, C_Z // DNW)]
    (y_dn,) = _rows(lambda r, f: ([_f_dn_post(r[0], r[1], f[0])], []), "dn_post_fwd", s_len, tm, post_ins,
                    [out_gain], [(DNW, BF16)])

    sq = proj[:, C_SQ:C_SQ + SWAW].reshape(s_len, SWA_H, SWA_D).transpose(1, 0, 2)
    sk = proj[:, C_SK:C_SK + SWAKW].reshape(s_len, SWA_KV, SWA_D).transpose(1, 0, 2)
    sv = proj[:, C_SV:C_SV + SWAKW].reshape(s_len, SWA_KV, SWA_D).transpose(1, 0, 2)
    bias = _bias_expand(wts["rel_bias"].T).reshape(SWA_H, BLK, 2 * BLK)
    o_swa = _swa_fwd(sq, sk, sv, bias, qg, kg, sinks)
    y_swa = o_swa.transpose(1, 0, 2).reshape(s_len, SWAW).astype(BF16)

    p_a, p_b, merged = _branch_merge(y_dn, y_swa, wts["wa"], wts["wb"], proj)
    x1, h2 = _out_proj(merged, wts["w_out"], x, ffn_gain)
    gt, up, act = _ffn_up(h2, wts["wg"], wts["wu"])
    dy, dy_b, loss = _ffn_down_loss(act, wts["wd"], x1, target)

    grads = {}
    d_gt, d_up = _ffn_dact(dy_b, wts["wd"], gt, up)
    grads["w_down"] = _gw_down(act, dy_b)
    grads["w_gate"], grads["w_up"] = _gw_gate_up(h2, d_gt, d_up)
    dx1, dx1_b, grads["ffn_norm"] = _ffn_dh2(d_gt, d_up, wts["wg"], wts["wu"], x1, dy, ffn_gain)
    grads["w_out"] = _mm(merged, dx1_b, "tn", BF16, 512, 512, "gw_out")
    d_pa, d_pb, d_gr = _merge_bwd(dx1_b, wts["w_out"], p_a, p_b, proj)
    d_ydn, d_yswa = _d_branch(d_pa, d_pb, wts["wa"], wts["wb"])
    grads["w_branch_dn"], grads["w_branch_swa"] = _gw_branch(y_dn, y_swa, d_pa, d_pb)

    d_oswa = d_yswa.reshape(s_len, SWA_H, SWA_D).transpose(1, 0, 2)
    d_sq, d_sk, d_sv, d_bias, grads["swa_q_norm"], grads["swa_k_norm"], d_sinks = _swa_bwd(
        sq, sk, sv, bias, qg, kg, sinks, d_oswa)
    grads["swa_sinks"] = d_sinks.reshape(1, SWA_H)
    grads["rel_bias"] = _bias_reduce(d_bias.reshape(SWA_H, BLK * 2 * BLK)).T
    d_sq = d_sq.transpose(1, 0, 2).reshape(s_len, SWAW).astype(BF16)
    d_sk = d_sk.transpose(1, 0, 2).reshape(s_len, SWAKW).astype(BF16)
    d_sv = d_sv.transpose(1, 0, 2).reshape(s_len, SWAKW).astype(BF16)

    def post_bwd(r, f):
        _, vjp = jax.vjp(_f_dn_post, r[0], r[1], f[0])
        d_o, d_z, d_gain = vjp(r[2])
        return [d_o, d_z], [d_gain]

    d_o, d_z, grads["dn_out_norm"] = _rows(post_bwd, "dn_post_bwd", s_len, tm, post_ins + [_whole(d_ydn)], [out_gain],
                                           [(DNW, F32), (DNW, BF16)], [(1, DH)])
    d_q, d_k, d_v, d_gb, d_bb = _dn_chunks_bwd(q_dn, k_dn, v_dn, gb, bb, s_all, d_o)

    def pre_bwd(r, f):
        _, vjp = jax.vjp(_f_dn_pre, *r[:5], *f)
        dxs0, dxs1, dxs2, dxs3, dba, dcw, dalog, ddtb = vjp((r[5], r[6], r[7], r[8], r[9]))
        return [dxs0, dxs1, dxs2, dxs3, dba], [dcw, dalog, ddtb]

    dxs0, dxs1, dxs2, dxs3, d_ba, grads["dn_conv"], grads["dn_a_log"], grads["dn_dt_bias"] = _rows(
        pre_bwd, "dn_pre_bwd", s_len, tmh, pre_ins + [_whole(t) for t in (d_q, d_k, d_v, d_bb, d_gb)], pre_full,
        [(QKVW, F32)] * 4 + [(128, BF16)], [(CONV, QKVW), (1, DN_H), (1, DN_H)])
    shifted = [_shift_up(t, CONV - 1 - j) for j, t in enumerate((dxs0, dxs1, dxs2, dxs3))]
    (d_qkv,) = _rows(lambda r, f: ([r[0] + r[1] + r[2] + r[3]], []), "conv_bwd_sum", s_len, tm,
                     [_whole(t) for t in shifted], [], [(QKVW, BF16)])

    d_proj = jnp.concatenate(
        [d_qkv, d_z, d_gr, d_sq, d_sk, d_sv, d_ba, jnp.zeros((s_len, PW - C_BA - 128), BF16)], axis=1)
    grads["w_in_p"] = _mm(h, d_proj, "tn", BF16, 512, 1024, "gw_in")
    grad_x, grads["attn_norm"] = _dh_rms(d_proj, w_in_p, x, dx1, attn_gain)
    return loss, grad_x, grads


_HBM = pl.BlockSpec(memory_space=pl.ANY)


def _place():
    return lax.axis_index("x"), lax.axis_index("y"), lax.axis_index("c")


def _other_chips(x, y):
    return [(1 - x, y), (x, 1 - y), (1 - x, 1 - y)]


def _rcopy(src, dst, send_sems, recv_sems, k, to):
    return pltpu.make_async_remote_copy(src_ref=src, dst_ref=dst, send_sem=send_sems.at[k], recv_sem=recv_sems.at[k],
                                        device_id=to, device_id_type=MESH)


def _comm_call(body, name, ins, out_shapes, n_remote, n_local):
    return pl.pallas_call(
        body, name=name, in_specs=[_HBM] * len(ins), out_specs=[_HBM] * len(out_shapes), out_shape=out_shapes,
        scratch_shapes=[pltpu.SemaphoreType.DMA((n_remote,)), pltpu.SemaphoreType.DMA((n_remote,)),
                        pltpu.SemaphoreType.DMA((max(n_local, 1),))],
        compiler_params=_cparams(has_side_effects=True),
    )(*ins)


def _gather_weights(ws):
    n = len(ws)
    halves = [w.shape[0] // 2 for w in ws]

    def body(*refs):
        w_refs, o_refs = refs[:n], refs[n:2 * n]
        send_sems, recv_sems, loc_sems = refs[2 * n:]
        x, y, c = _place()
        s = 2 * x + y
        sib = (x, y, 1 - c)
        chips = _other_chips(x, y)
        local = [pltpu.make_async_copy(w_refs[i], o_refs[i].at[s], loc_sems.at[i]) for i in range(n)]
        for cp in local:
            cp.start()

        def rows(i, half):
            return pl.ds(half * halves[i], halves[i])

        first = []
        for j, (cx, cy) in enumerate(chips):
            for i in range(n):
                cp = _rcopy(w_refs[i].at[rows(i, c), :], o_refs[i].at[s, rows(i, c), :], send_sems, recv_sems,
                            j * n + i, (cx, cy, c))
                cp.start()
                first.append(cp)
        passed = []
        for j, (cx, cy) in enumerate(chips):
            sj = 2 * cx + cy
            for i in range(n):
                blk = o_refs[i].at[sj, rows(i, c), :]
                _rcopy(blk, blk, send_sems, recv_sems, j * n + i, (cx, cy, c)).wait_recv()
                cp = _rcopy(blk, blk, send_sems, recv_sems, (3 + j) * n + i, sib)
                cp.start()
                passed.append(cp)
        for j, (cx, cy) in enumerate(chips):
            sj = 2 * cx + cy
            for i in range(n):
                blk = o_refs[i].at[sj, rows(i, 1 - c), :]
                _rcopy(blk, blk, send_sems, recv_sems, (3 + j) * n + i, sib).wait_recv()
        for cp in first + passed:
            cp.wait_send()
        for cp in local:
            cp.wait()

    return _comm_call(body, "gather_weights", ws, [SDS((N_CHIPS,) + w.shape, w.dtype) for w in ws], 6 * n, n)


def _swap_halves(gs):
    n = len(gs)
    halves = [g.shape[1] // 2 for g in gs]

    def body(*refs):
        g_refs, o_refs = refs[:n], refs[n:2 * n]
        send_sems, recv_sems, _ = refs[2 * n:]
        x, y, c = _place()
        cps = [_rcopy(g_refs[i].at[:, pl.ds((1 - c) * halves[i], halves[i]), :], o_refs[i], send_sems, recv_sems, i,
                      (x, y, 1 - c)) for i in range(n)]
        for cp in cps:
            cp.start()
        for cp in cps:
            cp.wait()

    return _comm_call(body, "swap_halves", gs, [SDS((N_CHIPS, h, g.shape[2]), g.dtype) for g, h in zip(gs, halves)],
                      n, 0)


def _chip_exchange(ps):
    n = len(ps)

    def body(*refs):
        p_refs, o_refs = refs[:n], refs[n:2 * n]
        send_sems, recv_sems, loc_sems = refs[2 * n:]
        x, y, c = _place()
        s = 2 * x + y
        chips = _other_chips(x, y)
        local = [pltpu.make_async_copy(p_refs[i].at[s], o_refs[i].at[s], loc_sems.at[i]) for i in range(n)]
        for cp in local:
            cp.start()
        sent = []
        for j, (cx, cy) in enumerate(chips):
            for i in range(n):
                cp = _rcopy(p_refs[i].at[2 * cx + cy], o_refs[i].at[s], send_sems, recv_sems, j * n + i, (cx, cy, c))
                cp.start()
                sent.append(cp)
        for j, (cx, cy) in enumerate(chips):
            sj = 2 * cx + cy
            for i in range(n):
                _rcopy(p_refs[i].at[sj], o_refs[i].at[sj], send_sems, recv_sems, j * n + i, (cx, cy, c)).wait_recv()
        for cp in sent:
            cp.wait_send()
        for cp in local:
            cp.wait()

    return _comm_call(body, "chip_exchange", ps, [SDS(p.shape, p.dtype) for p in ps], 3 * n, n)


def _swap_reduced(rs):
    n = len(rs)

    def body(*refs):
        r_refs, o_refs = refs[:n], refs[n:2 * n]
        send_sems, recv_sems, loc_sems = refs[2 * n:]
        x, y, c = _place()
        local, remote = [], []
        for i in range(n):
            h = rs[i].shape[0]
            mine = o_refs[i].at[pl.ds(c * h, h), :]
            local.append(pltpu.make_async_copy(r_refs[i], mine, loc_sems.at[i]))
            remote.append(_rcopy(r_refs[i], mine, send_sems, recv_sems, i, (x, y, 1 - c)))
        for cp in local + remote:
            cp.start()
        for i in range(n):
            h = rs[i].shape[0]
            theirs = o_refs[i].at[pl.ds((1 - c) * h, h), :]
            _rcopy(r_refs[i], theirs, send_sems, recv_sems, i, (x, y, 1 - c)).wait_recv()
        for cp in remote:
            cp.wait_send()
        for cp in local:
            cp.wait()

    return _comm_call(body, "swap_reduced", rs, [SDS((2 * r.shape[0], r.shape[1]), r.dtype) for r in rs], n, n)


def _all_sum_small(vec, name):
    n_dev = 8
    flips = [(bx, by, bc) for bx in (0, 1) for by in (0, 1) for bc in (0, 1)][1:]

    def body(v_ref, out_ref, gath, send_sems, recv_sems):
        x, y, c = _place()
        me = 4 * x + 2 * y + c
        gath[me] = v_ref[...]
        sent = []
        for k, (bx, by, bc) in enumerate(flips):
            peer = (x ^ bx, y ^ by, c ^ bc)
            cp = _rcopy(v_ref, gath.at[me], send_sems, recv_sems, k, peer)
            cp.start()
            sent.append(cp)
        for k, (bx, by, bc) in enumerate(flips):
            peer = (x ^ bx, y ^ by, c ^ bc)
            _rcopy(v_ref, gath.at[4 * peer[0] + 2 * peer[1] + peer[2]], send_sems, recv_sems, k, peer).wait_recv()
        for cp in sent:
            cp.wait_send()
        acc = gath[0]
        for d in range(1, n_dev):
            acc = acc + gath[d]
        out_ref[...] = acc

    vm = pl.BlockSpec(memory_space=pltpu.VMEM)
    return pl.pallas_call(
        body, name=name, in_specs=[vm], out_specs=vm, out_shape=SDS(vec.shape, F32),
        scratch_shapes=[pltpu.VMEM((n_dev,) + vec.shape, F32), pltpu.SemaphoreType.DMA((7,)),
                        pltpu.SemaphoreType.DMA((7,))],
        compiler_params=_cparams(has_side_effects=True),
    )(vec)


def _pack_small(vals, extra=None):
    parts = [vals[n].reshape(-1).astype(F32) for n, _ in _SMALL]
    parts.append(jnp.zeros((1,), F32) if extra is None else extra.reshape(1).astype(F32))
    flat = jnp.concatenate(parts)
    flat = jnp.concatenate([flat, jnp.zeros((_SMALL_ROWS * 128 - flat.shape[0],), F32)])
    return flat.reshape(_SMALL_ROWS, 128)


def _unpack_small(packed, shapes):
    flat = packed.reshape(-1)
    return {n: flat[_SMALL_OFF[n][0]:_SMALL_OFF[n][0] + _SMALL_OFF[n][1]].reshape(shapes[n]) for n, _ in _SMALL}


def _pair_sum(gs, gots, core):
    n = len(gs)

    def kern(c_ref, *refs):
        for i in range(n):
            refs[2 * n + i][...] = (refs[i][...].astype(F32) + refs[n + i][...].astype(F32)).astype(BF16)

    in_specs = [pl.BlockSpec((1, t.shape[1], t.shape[2]), lambda s, c_ref: (s, c_ref[0], 0)) for t in gots]
    in_specs += [pl.BlockSpec((1, t.shape[1], t.shape[2]), lambda s, c_ref: (s, 0, 0)) for t in gots]
    out_specs = [pl.BlockSpec((1, t.shape[1], t.shape[2]), lambda s, c_ref: (s, 0, 0)) for t in gots]
    return pl.pallas_call(
        kern, name="pair_sum",
        grid_spec=pltpu.PrefetchScalarGridSpec(num_scalar_prefetch=1, grid=(N_CHIPS,), in_specs=in_specs,
                                               out_specs=out_specs),
        out_shape=[SDS(t.shape, BF16) for t in gots],
        compiler_params=_cparams(dimension_semantics=("arbitrary",)),
    )(core.reshape(1).astype(jnp.int32), *gs, *gots)


def _chip_sum(qs):
    n = len(qs)

    def kern(*refs):
        for i in range(n):
            acc = refs[i][0].astype(F32)
            for s in range(1, N_CHIPS):
                acc = acc + refs[i][s].astype(F32)
            refs[n + i][...] = acc

    in_specs = [pl.BlockSpec((N_CHIPS, q.shape[1] // 2, q.shape[2]), lambda j: (0, j, 0)) for q in qs]
    out_specs = [pl.BlockSpec((q.shape[1] // 2, q.shape[2]), lambda j: (j, 0)) for q in qs]
    return _pc(kern, "chip_sum", (2,), in_specs, out_specs, [SDS(q.shape[1:], F32) for q in qs])(*qs)


def _adamw(w, g, m, v, name):
    rows, cols = w.shape
    tr = rows
    for cand in (256, 128, 64, 32, 16, 8):
        if rows % cand == 0 and rows > cand:
            tr = cand
            break

    def kern(w_ref, g_ref, m_ref, v_ref, d_ref, nm_ref, nv_ref):
        g_ = g_ref[...]
        m_ = ADAM_B1 * m_ref[...] + (1.0 - ADAM_B1) * g_
        v_ = ADAM_B2 * v_ref[...] + (1.0 - ADAM_B2) * jnp.square(g_)
        m_hat = m_ / (1.0 - ADAM_B1 ** ADAM_STEP)
        v_hat = v_ / (1.0 - ADAM_B2 ** ADAM_STEP)
        d_ref[...] = -ADAM_LR * (m_hat / (jnp.sqrt(v_hat) + ADAM_EPS) + ADAM_WD * w_ref[...])
        nm_ref[...] = m_
        nv_ref[...] = v_

    spec = pl.BlockSpec((tr, cols), lambda i: (i, 0))
    return _pc(kern, name, (rows // tr,), [spec] * 4, [spec] * 3, [SDS(w.shape, F32)] * 3)(w, g, m, v)


_WEIGHT_NAMES = ("attn_norm", "w_in", "dn_conv", "dn_a_log", "dn_dt_bias", "dn_out_norm", "swa_q_norm", "swa_k_norm",
                 "swa_sinks", "rel_bias", "w_branch_dn", "w_branch_swa", "w_out", "ffn_norm", "w_gate", "w_up",
                 "w_down")
_CONV_SH = QKVW // N_CHIPS


def kernel(x, attn_norm, w_in, dn_conv, dn_a_log, dn_dt_bias, dn_out_norm, swa_q_norm, swa_k_norm, swa_sinks, rel_bias, w_branch_dn, w_branch_swa, w_out, ffn_norm, w_gate, w_up, w_down, loss_target, m_attn_norm, m_w_in, m_dn_conv, m_dn_a_log, m_dn_dt_bias, m_dn_out_norm, m_swa_q_norm, m_swa_k_norm, m_swa_sinks, m_rel_bias, m_w_branch_dn, m_w_branch_swa, m_w_out, m_ffn_norm, m_w_gate, m_w_up, m_w_down, v_attn_norm, v_w_in, v_dn_conv, v_dn_a_log, v_dn_dt_bias, v_dn_out_norm, v_swa_q_norm, v_swa_k_norm, v_swa_sinks, v_rel_bias, v_w_branch_dn, v_w_branch_swa, v_w_out, v_ffn_norm, v_w_gate, v_w_up, v_w_down):
    w = dict(attn_norm=attn_norm, w_in=w_in, dn_conv=dn_conv, dn_a_log=dn_a_log, dn_dt_bias=dn_dt_bias,
             dn_out_norm=dn_out_norm, swa_q_norm=swa_q_norm, swa_k_norm=swa_k_norm, swa_sinks=swa_sinks,
             rel_bias=rel_bias, w_branch_dn=w_branch_dn, w_branch_swa=w_branch_swa, w_out=w_out, ffn_norm=ffn_norm,
             w_gate=w_gate, w_up=w_up, w_down=w_down)
    m = dict(attn_norm=m_attn_norm, w_in=m_w_in, dn_conv=m_dn_conv, dn_a_log=m_dn_a_log, dn_dt_bias=m_dn_dt_bias,
             dn_out_norm=m_dn_out_norm, swa_q_norm=m_swa_q_norm, swa_k_norm=m_swa_k_norm, swa_sinks=m_swa_sinks,
             rel_bias=m_rel_bias, w_branch_dn=m_w_branch_dn, w_branch_swa=m_w_branch_swa, w_out=m_w_out,
             ffn_norm=m_ffn_norm, w_gate=m_w_gate, w_up=m_w_up, w_down=m_w_down)
    v = dict(attn_norm=v_attn_norm, w_in=v_w_in, dn_conv=v_dn_conv, dn_a_log=v_dn_a_log, dn_dt_bias=v_dn_dt_bias,
             dn_out_norm=v_dn_out_norm, swa_q_norm=v_swa_q_norm, swa_k_norm=v_swa_k_norm, swa_sinks=v_swa_sinks,
             rel_bias=v_rel_bias, w_branch_dn=v_w_branch_dn, w_branch_swa=v_w_branch_swa, w_out=v_w_out,
             ffn_norm=v_ffn_norm, w_gate=v_w_gate, w_up=v_w_up, w_down=v_w_down)
    shapes = {n: w[n].shape for n in _WEIGHT_NAMES}

    def two_d(a):
        return a.reshape(a.shape[-2], a.shape[-1]) if a.ndim == 3 else a

    core = lax.axis_index("c")
    chip = 2 * lax.axis_index("x") + lax.axis_index("y")
    small_shapes = {n: two_d(w[n]).shape for n, _ in _SMALL}
    small_shapes["dn_conv"] = (CONV, QKVW)

    conv_loc = two_d(w["dn_conv"])
    conv_part = lax.dynamic_update_slice(jnp.zeros((CONV, QKVW), F32), jnp.where(core == 0, conv_loc, 0.0),
                                         (0, chip * _CONV_SH))
    conv_full = _all_sum_small(conv_part.reshape(CONV * QKVW // 128, 128), "gather_conv").reshape(CONV, QKVW)

    local_blocks = {n: two_d(w[n]) for n in _BIG_NAMES}
    gathered = dict(zip(_BIG_NAMES, _gather_weights([local_blocks[n].astype(BF16) for n in _BIG_NAMES])))
    w_in_full = gathered["w_in"].transpose(1, 0, 2).reshape(D, D_IN)
    wts = dict(w_in_p=_pad_w_in(w_in_full), wa=gathered["w_branch_dn"], wb=gathered["w_branch_swa"],
               w_out=gathered["w_out"].reshape(D, D), wg=gathered["w_gate"], wu=gathered["w_up"],
               wd=gathered["w_down"], dn_conv=conv_full)
    for n, _ in _SMALL[:-1]:
        wts[n] = two_d(w[n])

    loss_sum, grad_x, grads = _local_step(x[0], loss_target[0], wts)

    g_in = _unpad_w_in(grads["w_in_p"]).reshape(D, N_CHIPS, D_IN // N_CHIPS).transpose(1, 0, 2)
    gs = [g_in, grads["w_branch_dn"], grads["w_branch_swa"], grads["w_out"].reshape(N_CHIPS, CSH, D),
          grads["w_gate"], grads["w_up"], grads["w_down"]]
    gots = _swap_halves(gs)
    parts = _pair_sum(gs, gots, core)
    reduced = _chip_sum(_chip_exchange(parts))
    g_blocks = dict(zip(_BIG_NAMES, _swap_reduced(reduced)))

    small_sum = _all_sum_small(_pack_small(grads, loss_sum), "all_sum_small")
    loss = small_sum.reshape(-1)[_LOSS_OFF]
    g_small = _unpack_small(small_sum, small_shapes)

    g_out, d_out, m_out, v_out = {}, {}, {}, {}
    for n in _BIG_NAMES:
        g_out[n] = g_blocks[n].reshape(shapes[n])
        d_, m_, v_ = _adamw(local_blocks[n], g_blocks[n], two_d(m[n]), two_d(v[n]), "adamw_" + n)
        d_out[n], m_out[n], v_out[n] = d_.reshape(shapes[n]), m_.reshape(shapes[n]), v_.reshape(shapes[n])
    g_conv = lax.dynamic_slice(g_small["dn_conv"], (0, chip * _CONV_SH), (CONV, _CONV_SH))
    g_out["dn_conv"] = g_conv.reshape(shapes["dn_conv"])
    d_, m_, v_ = _adamw(conv_loc, g_conv, two_d(m["dn_conv"]), two_d(v["dn_conv"]), "adamw_dn_conv")
    d_out["dn_conv"], m_out["dn_conv"], v_out["dn_conv"] = (t.reshape(shapes["dn_conv"]) for t in (d_, m_, v_))

    def packed(src):
        vals = {n: src[n] for n, _ in _SMALL[:-1]}
        vals["dn_conv"] = jnp.zeros((CONV * QKVW,), F32)
        return _pack_small(vals)

    d_s, m_s, v_s = _adamw(packed(w), small_sum, packed(m), packed(v), "adamw_small")
    d_small, m_small, v_small = (_unpack_small(t, small_shapes) for t in (d_s, m_s, v_s))
    for n, _ in _SMALL[:-1]:
        g_out[n] = g_small[n].reshape(shapes[n])
        d_out[n], m_out[n], v_out[n] = (t[n].reshape(shapes[n]) for t in (d_small, m_small, v_small))

    return (loss, grad_x[None], *[g_out[n] for n in _WEIGHT_NAMES], *[d_out[n] for n in _WEIGHT_NAMES],
            *[m_out[n] for n in _WEIGHT_NAMES], *[v_out[n] for n in _WEIGHT_NAMES])
```

```python
import functools
import math

import numpy as np
import jax
import jax.numpy as jnp
from jax import lax
from jax.experimental import pallas as pl
from jax.experimental.pallas import tpu as pltpu

F32 = jnp.float32
BF16 = jnp.bfloat16
SDS = jax.ShapeDtypeStruct

D = 1024
DN_H = 4
DH = 128
DNW = DN_H * DH
QKVW = 3 * DNW
CONV = 4
CHUNK = 64
SWA_H = 8
SWA_KV = 2
SWA_G = SWA_H // SWA_KV
SWA_D = 64
SWAW = SWA_H * SWA_D
SWAKW = SWA_KV * SWA_D
BLK = 128
NBUCKET = 32
MAXDIST = 128
DFF = 2816
D_IN = QKVW + DNW + 2 * DN_H + SWAW + 2 * SWAKW + 2 * D
EPS = 1e-6
NEG = -1e30

ADAM_LR = 0.001
ADAM_B1 = 0.9
ADAM_B2 = 0.999
ADAM_EPS = 1e-08
ADAM_WD = 0.01
ADAM_STEP = 10

C_QKV, C_Z, C_GATE, C_SQ, C_SK, C_SV, C_BA = 0, 1536, 2048, 4096, 4608, 4736, 4864
PW = 5120
_ORIG_PIECES = (
    (0, QKVW, C_QKV),
    (QKVW, DNW, C_Z),
    (QKVW + DNW, 2 * DN_H, C_BA),
    (QKVW + DNW + 2 * DN_H, SWAW, C_SQ),
    (QKVW + DNW + 2 * DN_H + SWAW, SWAKW, C_SK),
    (QKVW + DNW + 2 * DN_H + SWAW + SWAKW, SWAKW, C_SV),
    (QKVW + DNW + 2 * DN_H + SWAW + 2 * SWAKW, 2 * D, C_GATE),
)

N_CHIPS = 4
FSH = DFF // N_CHIPS
CSH = D // N_CHIPS
VMEM_LIMIT = 48 * 1024 * 1024
MESH = pl.DeviceIdType.MESH

_BIG = (
    ("w_in", D, D_IN // N_CHIPS),
    ("w_branch_dn", DNW, CSH),
    ("w_branch_swa", SWAW, CSH),
    ("w_out", CSH, D),
    ("w_gate", D, FSH),
    ("w_up", D, FSH),
    ("w_down", FSH, D),
)
_BIG_NAMES = tuple(n for n, _, _ in _BIG)

_SMALL = (
    ("attn_norm", D), ("ffn_norm", D), ("dn_out_norm", DH), ("swa_q_norm", SWA_D), ("swa_k_norm", SWA_D),
    ("swa_sinks", SWA_H), ("dn_a_log", DN_H), ("dn_dt_bias", DN_H), ("rel_bias", NBUCKET * SWA_H),
    ("dn_conv", CONV * QKVW),
)
_SMALL_OFF = {}
_o = 0
for _n, _s in _SMALL:
    _SMALL_OFF[_n] = (_o, _s)
    _o += _s
_LOSS_OFF = _o
_SMALL_ROWS = -(-(_o + 1) // (8 * 128)) * 8


def _cparams(**kw):
    return pltpu.CompilerParams(vmem_limit_bytes=VMEM_LIMIT, **kw)


_DIMS = {
    "nn": (((1,), (0,)), ((), ())),
    "nt": (((1,), (1,)), ((), ())),
    "tn": (((0,), (0,)), ((), ())),
    "bnn": (((2,), (1,)), ((0,), (0,))),
    "bnt": (((2,), (2,)), ((0,), (0,))),
    "btn": (((1,), (1,)), ((0,), (0,))),
}


def _raw_dot(a, b, kind, exact):
    if exact:
        prec = lax.Precision.HIGH if exact == "x3" else lax.Precision.HIGHEST
        return lax.dot_general(a, b, _DIMS[kind], precision=prec, preferred_element_type=F32)
    return lax.dot_general(a.astype(BF16), b.astype(BF16), _DIMS[kind], preferred_element_type=F32)


@functools.partial(jax.custom_vjp, nondiff_argnums=(2, 3))
def _dot(a, b, kind, exact):
    return _raw_dot(a, b, kind, exact)


def _dot_fwd(a, b, kind, exact):
    return _raw_dot(a, b, kind, exact), (a, b)


def _dot_bwd(kind, exact, res, g):
    a, b = res
    pre = kind[:-2]
    nn, nt, tn = pre + "nn", pre + "nt", pre + "tn"
    if kind == nn:
        return _dot(g, b, nt, exact), _dot(a, g, tn, exact)
    if kind == nt:
        return _dot(g, b, nn, exact), _dot(g, a, tn, exact)
    return _dot(b, g, nt, exact), _dot(a, g, nn, exact)


_dot.defvjp(_dot_fwd, _dot_bwd)


def _silu(x):
    return x * jax.nn.sigmoid(x)


def _f_rms(x, gain):
    return x * lax.rsqrt(jnp.mean(x * x, axis=-1, keepdims=True) + EPS) * gain


def _f_dn_pre(xs0, xs1, xs2, xs3, ba, cw, alog, dtb):
    rows = xs0.shape[0]
    c = xs0 * cw[0:1] + xs1 * cw[1:2] + xs2 * cw[2:3] + xs3 * cw[3:4]
    qkv = _silu(c)
    qs, ks, bbs, gbs = [], [], [], []
    for h in range(DN_H):
        qh = qkv[:, h * DH:(h + 1) * DH]
        kh = qkv[:, DNW + h * DH:DNW + (h + 1) * DH]
        qs.append(qh * lax.rsqrt(jnp.sum(qh * qh, axis=-1, keepdims=True) + EPS) * (DH ** -0.5))
        ks.append(kh * lax.rsqrt(jnp.sum(kh * kh, axis=-1, keepdims=True) + EPS))
        beta = jax.nn.sigmoid(ba[:, h:h + 1])
        ar = ba[:, DN_H + h:DN_H + h + 1] + dtb[:, h:h + 1]
        softplus = jnp.maximum(ar, 0.0) + jnp.log1p(jnp.exp(-jnp.abs(ar)))
        g = -jnp.exp(alog[:, h:h + 1]) * softplus
        bbs.append(jnp.broadcast_to(beta, (rows, DH)))
        gbs.append(jnp.broadcast_to(g, (rows, DH)))
    return (jnp.concatenate(qs, axis=1), jnp.concatenate(ks, axis=1), qkv[:, 2 * DNW:],
            jnp.concatenate(bbs, axis=1), jnp.concatenate(gbs, axis=1))


def _f_dn_post(o, z, gain):
    ys = []
    for h in range(DN_H):
        oh = o[:, h * DH:(h + 1) * DH]
        zh = z[:, h * DH:(h + 1) * DH]
        ys.append(oh * lax.rsqrt(jnp.mean(oh * oh, axis=-1, keepdims=True) + EPS) * gain * _silu(zh))
    return jnp.concatenate(ys, axis=1)


def _f_merge(pa, pb, ga, gb):
    return jax.nn.sigmoid(ga) * pa + jax.nn.sigmoid(gb) * pb


def _f_swiglu(g, u):
    return _silu(g) * u


def _f_chunk(q, k, v, gb, bb, s):
    c = CHUNK
    nh = q.shape[0]
    ii = lax.broadcasted_iota(jnp.int32, (nh, c, c), 1)
    jj = lax.broadcasted_iota(jnp.int32, (nh, c, c), 2)
    incl = ii >= jj
    strict = ii > jj
    eye = (ii == jj).astype(F32)
    gcb = _dot(incl.astype(F32), gb, "bnn", True)
    lane0 = (lax.broadcasted_iota(jnp.int32, (nh, c, DH), 2) == 0).astype(F32)
    gcol = gcb[:, :, :c]
    grow = _dot(lane0, gcb, "bnt", True)
    decay = jnp.where(incl, jnp.exp(jnp.where(incl, gcol - grow, 0.0)), 0.0)
    kb = k * bb
    vb = v * bb
    a = jnp.where(strict, _dot(kb, k, "bnt", False) * decay, 0.0)
    p = -a
    t = eye + p
    for _ in range(5):
        p = _dot(p, p, "bnn", "x3")
        t = t + _dot(t, p, "bnn", "x3")
    eg = jnp.exp(gcb)
    u = _dot(t, vb, "bnn", "x3")
    w = _dot(t, kb * eg, "bnn", "x3")
    qk = jnp.where(incl, _dot(q, k, "bnt", False) * decay, 0.0)
    qe = q * eg
    glast = gcb[:, c - 1:c, :]
    k_dec = k * jnp.exp(glast - gcb)
    e_last = jnp.exp(glast)
    outs = []
    for g in range(nh // DN_H):
        sl = slice(g * DN_H, (g + 1) * DN_H)
        v_new = u[sl] - _dot(w[sl], s, "bnn", False)
        outs.append(_dot(qe[sl], s, "bnn", False) + _dot(qk[sl], v_new, "bnn", False))
        s = s * e_last[sl] + _dot(k_dec[sl], v_new, "btn", False)
    return jnp.concatenate(outs, axis=0), s


def _f_swa(q4, kp, kc, vp, vc, bias4, qg, kg, sink, mask):
    kb = jnp.concatenate([kp, kc], axis=0)
    vb = jnp.concatenate([vp, vc], axis=0)
    kn = kb * lax.rsqrt(jnp.mean(kb * kb, axis=-1, keepdims=True) + EPS) * kg
    outs = []
    for g in range(SWA_G):
        qq = q4[g]
        qn = qq * lax.rsqrt(jnp.mean(qq * qq, axis=-1, keepdims=True) + EPS) * qg
        lg = _dot(qn, kn, "nt", False) * (SWA_D ** -0.5) + bias4[g]
        lg = jnp.where(mask, lg, NEG)
        sk = sink[:, g:g + 1]
        m = lax.stop_gradient(jnp.maximum(jnp.max(lg, axis=-1, keepdims=True), sk))
        p = jnp.exp(lg - m)
        den = jnp.sum(p, axis=-1, keepdims=True) + jnp.exp(sk - m)
        outs.append(_dot(p / den, vb, "nn", False))
    return jnp.stack(outs, axis=0)


def _bdot(a, b, kind="nn"):
    return lax.dot_general(a.astype(BF16), b.astype(BF16), _DIMS[kind], preferred_element_type=F32)


def _pc(kern, name, grid, in_specs, out_specs, out_shape, scratch=()):
    return pl.pallas_call(
        kern, name=name, grid=grid, in_specs=in_specs, out_specs=out_specs, out_shape=out_shape,
        scratch_shapes=list(scratch), compiler_params=_cparams(dimension_semantics=("arbitrary",) * len(grid)))


def _mm(a, b, kind, out_dtype, tm, tn, name):
    if kind == "tn":
        k, m = a.shape
    else:
        m, k = a.shape
    n = b.shape[0] if kind == "nt" else b.shape[1]
    tm, tn = min(tm, m), min(tn, n)
    assert m % tm == 0 and n % tn == 0, (name, a.shape, b.shape, tm, tn)

    def kern(a_ref, b_ref, o_ref):
        o_ref[...] = _bdot(a_ref[...], b_ref[...], kind).astype(o_ref.dtype)

    a_spec = pl.BlockSpec((k, tm), lambda i, j: (0, i)) if kind == "tn" else pl.BlockSpec((tm, k), lambda i, j: (i, 0))
    b_spec = pl.BlockSpec((tn, k), lambda i, j: (j, 0)) if kind == "nt" else pl.BlockSpec((k, tn), lambda i, j: (0, j))
    return _pc(kern, name, (m // tm, n // tn), [a_spec, b_spec], pl.BlockSpec((tm, tn), lambda i, j: (i, j)),
               SDS((m, n), out_dtype))(a, b)


def _rows(body, name, m, tm, row_ins, full_ins, row_outs, acc_outs=()):
    n_r, n_f, n_o, n_a = len(row_ins), len(full_ins), len(row_outs), len(acc_outs)
    assert m % tm == 0

    def kern(*refs):
        r = refs[:n_r]
        f = refs[n_r:n_r + n_f]
        o = refs[n_r + n_f:n_r + n_f + n_o]
        acc = refs[n_r + n_f + n_o:]
        outs, sums = body([x[...] for x in r], [x[...] for x in f])
        for ref, val in zip(o, outs, strict=True):
            ref[...] = val.astype(ref.dtype)
        if n_a:
            @pl.when(pl.program_id(0) == 0)
            def _():
                for ref in acc:
                    ref[...] = jnp.zeros(ref.shape, F32)

            for ref, val in zip(acc, sums, strict=True):
                ref[...] += val

    in_specs = [pl.BlockSpec((tm, w), functools.partial(lambda i, cb: (i, cb), cb=cb)) for _, w, cb in row_ins]
    in_specs += [pl.BlockSpec(x.shape, lambda i: (0, 0)) for x in full_ins]
    out_specs = [pl.BlockSpec((tm, w), lambda i: (i, 0)) for w, _ in row_outs]
    out_specs += [pl.BlockSpec(s, lambda i: (0, 0)) for s in acc_outs]
    out_shape = [SDS((m, w), dt) for w, dt in row_outs]
    out_shape += [SDS(s, F32) for s in acc_outs]
    return _pc(kern, name, (m // tm,), in_specs, out_specs, out_shape)(*[x for x, _, _ in row_ins], *full_ins)


def _whole(x):
    return (x, x.shape[1], 0)


def _zero_first(refs):
    @pl.when(pl.program_id(0) == 0)
    def _():
        for ref in refs:
            ref[...] = jnp.zeros(ref.shape, F32)


GROUP = 4


def _heads(ref):
    return jnp.stack([ref[g * CHUNK:(g + 1) * CHUNK, h * DH:(h + 1) * DH]
                      for g in range(GROUP) for h in range(DN_H)], axis=0)


def _unheads(ref, val):
    for g in range(GROUP):
        for h in range(DN_H):
            ref[g * CHUNK:(g + 1) * CHUNK, h * DH:(h + 1) * DH] = val[g * DN_H + h]


def _dn_chunks_fwd(q, k, v, gb, bb):
    s_len = q.shape[0]
    ng = s_len // (GROUP * CHUNK)

    def kern(q_ref, k_ref, v_ref, g_ref, b_ref, o_ref, sall_ref, state):
        _zero_first([state])
        s = state[...]
        sall_ref[0] = s
        o, s_new = _f_chunk(*[_heads(r) for r in (q_ref, k_ref, v_ref, g_ref, b_ref)], s)
        _unheads(o_ref, o)
        state[...] = s_new

    blk = pl.BlockSpec((GROUP * CHUNK, DNW), lambda c: (c, 0))
    return _pc(kern, "dn_chunks_fwd", (ng,), [blk] * 5,
               [blk, pl.BlockSpec((1, DN_H, DH, DH), lambda c: (c, 0, 0, 0))],
               [SDS((s_len, DNW), F32), SDS((ng, DN_H, DH, DH), F32)],
               scratch=[pltpu.VMEM((DN_H, DH, DH), F32)])(q, k, v, gb, bb)


def _dn_chunks_bwd(q, k, v, gb, bb, s_all, d_o):
    s_len = q.shape[0]
    ng = s_len // (GROUP * CHUNK)

    def kern(q_ref, k_ref, v_ref, g_ref, b_ref, sall_ref, do_ref, dq_ref, dk_ref, dv_ref, dg_ref, db_ref, dstate):
        _zero_first([dstate])
        _, vjp = jax.vjp(_f_chunk, *[_heads(r) for r in (q_ref, k_ref, v_ref, g_ref, b_ref)], sall_ref[0])
        *d_ins, ds = vjp((_heads(do_ref), dstate[...]))
        for ref, val in zip((dq_ref, dk_ref, dv_ref, dg_ref, db_ref), d_ins, strict=True):
            _unheads(ref, val)
        dstate[...] = ds

    blk = pl.BlockSpec((GROUP * CHUNK, DNW), lambda c: (ng - 1 - c, 0))
    return _pc(kern, "dn_chunks_bwd", (ng,),
               [blk] * 5 + [pl.BlockSpec((1, DN_H, DH, DH), lambda c: (ng - 1 - c, 0, 0, 0)), blk],
               [blk] * 5, [SDS((s_len, DNW), F32)] * 5,
               scratch=[pltpu.VMEM((DN_H, DH, DH), F32)])(q, k, v, gb, bb, s_all, d_o)


def _t5_bucket_table():
    qi = np.arange(BLK)[:, None]
    kj = np.arange(2 * BLK)[None, :]
    dist = BLK + qi - kj
    n = np.maximum(dist, 0)
    max_exact = NBUCKET // 2
    nf = np.maximum(n, 1).astype(np.float32)
    large = max_exact + (np.log(nf / np.float32(max_exact)) / np.float32(math.log(MAXDIST / max_exact))
                         * np.float32(NBUCKET - max_exact)).astype(np.int32)
    large = np.minimum(large, NBUCKET - 1)
    return np.where(n < max_exact, n, large)


def _bucket_onehot_t():
    table = _t5_bucket_table().reshape(-1)
    return (np.arange(NBUCKET)[:, None] == table[None, :]).astype(np.float32)


def _swa_mask(first):
    qi = lax.broadcasted_iota(jnp.int32, (BLK, 2 * BLK), 0)
    kj = lax.broadcasted_iota(jnp.int32, (BLK, 2 * BLK), 1)
    dist = BLK + qi - kj
    window = (dist >= 0) & (dist < BLK)
    return window & ((kj >= BLK) | jnp.logical_not(first))


def _bias_expand(rel_bias_t):
    onehot = jnp.asarray(_bucket_onehot_t())

    def kern(r_ref, oh_ref, o_ref):
        o_ref[...] = _raw_dot(r_ref[...], oh_ref[...], "nn", True)

    return pl.pallas_call(
        kern, name="bias_expand", out_shape=SDS((SWA_H, BLK * 2 * BLK), F32), compiler_params=_cparams(),
    )(rel_bias_t, onehot)


def _bias_reduce(d_bias_flat):
    onehot = jnp.asarray(_bucket_onehot_t())

    def kern(d_ref, oh_ref, o_ref):
        o_ref[...] = _raw_dot(d_ref[...], oh_ref[...], "nt", True)

    return pl.pallas_call(
        kern, name="bias_reduce", out_shape=SDS((SWA_H, NBUCKET), F32), compiler_params=_cparams(),
    )(d_bias_flat, onehot)


def _swa_specs(nb, rev):
    def blk(n):
        return (nb - 1 - n) if rev else n

    q_spec = pl.BlockSpec((SWA_G, BLK, SWA_D), lambda kv, n: (kv, blk(n), 0))
    cur = pl.BlockSpec((1, BLK, SWA_D), lambda kv, n: (kv, blk(n), 0))
    prev = pl.BlockSpec((1, BLK, SWA_D), lambda kv, n: (kv, jnp.maximum(blk(n) - 1, 0), 0))
    bias = pl.BlockSpec((SWA_G, BLK, 2 * BLK), lambda kv, n: (kv, 0, 0))
    gain = pl.BlockSpec((1, SWA_D), lambda kv, n: (0, 0))
    sink = pl.BlockSpec((1, 1, SWA_G), lambda kv, n: (kv, 0, 0))
    return q_spec, cur, prev, bias, gain, sink


def _swa_fwd(q, k, v, bias, qg, kg, sinks):
    s_len = q.shape[1]
    nb = s_len // BLK
    q_spec, cur, prev, bias_spec, gain, sink = _swa_specs(nb, False)

    def kern(q_ref, kp_ref, kc_ref, vp_ref, vc_ref, b_ref, qg_ref, kg_ref, s_ref, o_ref):
        mask = _swa_mask(pl.program_id(1) == 0)
        o_ref[...] = _f_swa(q_ref[...], kp_ref[0], kc_ref[0], vp_ref[0], vc_ref[0], b_ref[...], qg_ref[...],
                            kg_ref[...], s_ref[0], mask)

    return _pc(kern, "swa_fwd", (SWA_KV, nb), [q_spec, prev, cur, prev, cur, bias_spec, gain, gain, sink],
               q_spec, SDS((SWA_H, s_len, SWA_D), F32))(q, k, k, v, v, bias, qg, kg, sinks)


def _swa_bwd(q, k, v, bias, qg, kg, sinks, d_out):
    s_len = q.shape[1]
    nb = s_len // BLK
    q_spec, cur, prev, bias_spec, gain, sink = _swa_specs(nb, True)

    def kern(q_ref, kp_ref, kc_ref, vp_ref, vc_ref, b_ref, qg_ref, kg_ref, s_ref, do_ref,
             dq_ref, dk_ref, dv_ref, db_ref, dqg_ref, dkg_ref, ds_ref, carry_k, carry_v):
        kv = pl.program_id(0)
        n = pl.program_id(1)
        mask = _swa_mask(n == nb - 1)

        @pl.when(n == 0)
        def _():
            carry_k[...] = jnp.zeros(carry_k.shape, F32)
            carry_v[...] = jnp.zeros(carry_v.shape, F32)
            db_ref[...] = jnp.zeros(db_ref.shape, F32)
            ds_ref[...] = jnp.zeros(ds_ref.shape, F32)

        @pl.when((n == 0) & (kv == 0))
        def _():
            dqg_ref[...] = jnp.zeros(dqg_ref.shape, F32)
            dkg_ref[...] = jnp.zeros(dkg_ref.shape, F32)

        fn = functools.partial(_f_swa, mask=mask)
        _, vjp = jax.vjp(fn, q_ref[...], kp_ref[0], kc_ref[0], vp_ref[0], vc_ref[0], b_ref[...], qg_ref[...],
                         kg_ref[...], s_ref[0])
        dq, dkp, dkc, dvp, dvc, dbias, dqg, dkg, dsink = vjp(do_ref[...])
        dq_ref[...] = dq
        dk_ref[0] = dkc + carry_k[...]
        dv_ref[0] = dvc + carry_v[...]
        carry_k[...] = dkp
        carry_v[...] = dvp
        db_ref[...] += dbias
        dqg_ref[...] += dqg
        dkg_ref[...] += dkg
        ds_ref[0] += dsink

    return _pc(
        kern, "swa_bwd", (SWA_KV, nb),
        [q_spec, prev, cur, prev, cur, bias_spec, gain, gain, sink, q_spec],
        [q_spec, cur, cur, bias_spec, gain, gain, sink],
        [SDS((SWA_H, s_len, SWA_D), F32), SDS((SWA_KV, s_len, SWA_D), F32), SDS((SWA_KV, s_len, SWA_D), F32),
         SDS((SWA_H, BLK, 2 * BLK), F32), SDS((1, SWA_D), F32), SDS((1, SWA_D), F32), SDS((SWA_KV, 1, SWA_G), F32)],
        scratch=[pltpu.VMEM((BLK, SWA_D), F32), pltpu.VMEM((BLK, SWA_D), F32)],
    )(q, k, k, v, v, bias, qg, kg, sinks, d_out)


def _branch_merge(y_dn, y_swa, wa, wb, proj):
    s_len = y_dn.shape[0]
    tm = min(512, s_len)

    def kern(ya_ref, yb_ref, wa_ref, wb_ref, ga_ref, gb_ref, pa_ref, pb_ref, m_ref):
        pa = _bdot(ya_ref[...], wa_ref[0])
        pb = _bdot(yb_ref[...], wb_ref[0])
        pa_ref[...] = pa
        pb_ref[...] = pb
        m_ref[...] = _f_merge(pa, pb, ga_ref[...], gb_ref[...]).astype(BF16)

    y_spec = pl.BlockSpec((tm, DNW), lambda i, s: (i, 0))
    w_spec = pl.BlockSpec((1, DNW, CSH), lambda i, s: (s, 0, 0))
    o_spec = pl.BlockSpec((tm, CSH), lambda i, s: (i, s))
    ga_spec = pl.BlockSpec((tm, CSH), lambda i, s: (i, C_GATE // CSH + s))
    gb_spec = pl.BlockSpec((tm, CSH), lambda i, s: (i, (C_GATE + D) // CSH + s))
    return _pc(kern, "branch_merge", (s_len // tm, N_CHIPS), [y_spec, y_spec, w_spec, w_spec, ga_spec, gb_spec],
               [o_spec] * 3, [SDS((s_len, D), F32), SDS((s_len, D), F32), SDS((s_len, D), BF16)],
               )(y_dn, y_swa, wa, wb, proj, proj)


def _out_proj(merged, w_out, x, gain):
    s_len = x.shape[0]
    tm = min(256, s_len)

    def kern(m_ref, w_ref, x_ref, g_ref, x1_ref, h2_ref):
        x1 = x_ref[...] + _bdot(m_ref[...], w_ref[...])
        x1_ref[...] = x1
        h2_ref[...] = _f_rms(x1, g_ref[...]).astype(BF16)

    row = pl.BlockSpec((tm, D), lambda i: (i, 0))
    return _pc(kern, "out_proj", (s_len // tm,),
               [row, pl.BlockSpec((D, D), lambda i: (0, 0)), row, pl.BlockSpec((1, D), lambda i: (0, 0))],
               [row, row], [SDS((s_len, D), F32), SDS((s_len, D), BF16)])(merged, w_out, x, gain)


def _ffn_up(h2, wg, wu):
    s_len = h2.shape[0]
    tm = min(512, s_len)

    def kern(h_ref, g_ref, u_ref, gt_ref, up_ref, act_ref):
        h = h_ref[...]
        g = _bdot(h, g_ref[0])
        u = _bdot(h, u_ref[0])
        gt_ref[0] = g
        up_ref[0] = u
        act_ref[0] = _f_swiglu(g, u).astype(BF16)

    w_spec = pl.BlockSpec((1, D, FSH), lambda s, i: (s, 0, 0))
    o_spec = pl.BlockSpec((1, tm, FSH), lambda s, i: (s, i, 0))
    shape = (N_CHIPS, s_len, FSH)
    return _pc(kern, "ffn_up", (N_CHIPS, s_len // tm), [pl.BlockSpec((tm, D), lambda s, i: (i, 0)), w_spec, w_spec],
               [o_spec] * 3, [SDS(shape, F32), SDS(shape, F32), SDS(shape, BF16)])(h2, wg, wu)


def _ffn_down_loss(act, wd, x1, target):
    s_len = x1.shape[0]
    tm = min(256, s_len)

    def kern(a_ref, w_ref, x_ref, t_ref, dy_ref, dyb_ref, loss_ref):
        _zero_first([loss_ref])
        y = x_ref[...]
        for s in range(N_CHIPS):
            y = y + _bdot(a_ref[s], w_ref[s])
        d = y - t_ref[...]
        dy = d * (1.0 / D)
        dy_ref[...] = dy
        dyb_ref[...] = dy.astype(BF16)
        loss_ref[...] += jnp.sum(d * d).reshape(1, 1) * (0.5 / D)

    row = pl.BlockSpec((tm, D), lambda i: (i, 0))
    return _pc(kern, "ffn_down_loss", (s_len // tm,),
               [pl.BlockSpec((N_CHIPS, tm, FSH), lambda i: (0, i, 0)),
                pl.BlockSpec((N_CHIPS, FSH, D), lambda i: (0, 0, 0)), row, row],
               [row, row, pl.BlockSpec((1, 1), lambda i: (0, 0))],
               [SDS((s_len, D), F32), SDS((s_len, D), BF16), SDS((1, 1), F32)])(act, wd, x1, target)


def _ffn_dact(dy_b, wd, gt, up):
    s_len = dy_b.shape[0]
    tm = min(512, s_len)

    def kern(dy_ref, w_ref, gt_ref, up_ref, dg_ref, du_ref):
        d_act = _bdot(dy_ref[...], w_ref[0], "nt")
        _, vjp = jax.vjp(_f_swiglu, gt_ref[0], up_ref[0])
        dg, du = vjp(d_act)
        dg_ref[0] = dg.astype(BF16)
        du_ref[0] = du.astype(BF16)

    a_spec = pl.BlockSpec((1, tm, FSH), lambda s, i: (s, i, 0))
    shape = (N_CHIPS, s_len, FSH)
    return _pc(kern, "ffn_dact", (N_CHIPS, s_len // tm),
               [pl.BlockSpec((tm, D), lambda s, i: (i, 0)), pl.BlockSpec((1, FSH, D), lambda s, i: (s, 0, 0)),
                a_spec, a_spec],
               [a_spec, a_spec], [SDS(shape, BF16), SDS(shape, BF16)])(dy_b, wd, gt, up)


def _gw_down(act, dy_b):
    s_len = dy_b.shape[0]
    tn = 512

    def kern(a_ref, g_ref, o_ref):
        o_ref[0] = _bdot(a_ref[0], g_ref[...], "tn").astype(BF16)

    return _pc(kern, "gw_down", (N_CHIPS, D // tn),
               [pl.BlockSpec((1, s_len, FSH), lambda s, j: (s, 0, 0)), pl.BlockSpec((s_len, tn), lambda s, j: (0, j))],
               pl.BlockSpec((1, FSH, tn), lambda s, j: (s, 0, j)), SDS((N_CHIPS, FSH, D), BF16))(act, dy_b)


def _gw_gate_up(h2, d_gt, d_up):
    s_len = h2.shape[0]
    tk = 256

    def kern(h_ref, dg_ref, du_ref, og_ref, ou_ref):
        h = h_ref[...]
        og_ref[0] = _bdot(h, dg_ref[0], "tn").astype(BF16)
        ou_ref[0] = _bdot(h, du_ref[0], "tn").astype(BF16)

    d_spec = pl.BlockSpec((1, s_len, FSH), lambda s, j: (s, 0, 0))
    o_spec = pl.BlockSpec((1, tk, FSH), lambda s, j: (s, j, 0))
    shape = (N_CHIPS, D, FSH)
    return _pc(kern, "gw_gate_up", (N_CHIPS, D // tk), [pl.BlockSpec((s_len, tk), lambda s, j: (0, j)), d_spec, d_spec],
               [o_spec, o_spec], [SDS(shape, BF16), SDS(shape, BF16)])(h2, d_gt, d_up)


def _ffn_dh2(d_gt, d_up, wg, wu, x1, dy, gain):
    s_len = x1.shape[0]
    tm = min(256, s_len)

    def kern(dg_ref, du_ref, wg_ref, wu_ref, x_ref, dy_ref, g_ref, dx_ref, dxb_ref, dgain_ref):
        _zero_first([dgain_ref])
        dh2 = jnp.zeros((tm, D), F32)
        for s in range(N_CHIPS):
            dh2 = dh2 + _bdot(dg_ref[s], wg_ref[s], "nt") + _bdot(du_ref[s], wu_ref[s], "nt")
        _, vjp = jax.vjp(_f_rms, x_ref[...], g_ref[...])
        dx, dgain = vjp(dh2)
        dx1 = dx + dy_ref[...]
        dx_ref[...] = dx1
        dxb_ref[...] = dx1.astype(BF16)
        dgain_ref[...] += dgain

    row = pl.BlockSpec((tm, D), lambda i: (i, 0))
    d_spec = pl.BlockSpec((N_CHIPS, tm, FSH), lambda i: (0, i, 0))
    w_spec = pl.BlockSpec((N_CHIPS, D, FSH), lambda i: (0, 0, 0))
    vec = pl.BlockSpec((1, D), lambda i: (0, 0))
    return _pc(kern, "ffn_dh2", (s_len // tm,), [d_spec, d_spec, w_spec, w_spec, row, row, vec],
               [row, row, vec], [SDS((s_len, D), F32), SDS((s_len, D), BF16), SDS((1, D), F32)],
               )(d_gt, d_up, wg, wu, x1, dy, gain)


def _merge_bwd(dx1_b, w_out, pa, pb, proj):
    s_len = dx1_b.shape[0]
    tm = min(256, s_len)

    def kern(dx_ref, w_ref, pa_ref, pb_ref, g_ref, dpa_ref, dpb_ref, dg_ref):
        dm = _bdot(dx_ref[...], w_ref[...], "nt")
        gates = g_ref[...]
        _, vjp = jax.vjp(_f_merge, pa_ref[...], pb_ref[...], gates[:, :D], gates[:, D:])
        dpa, dpb, dga, dgb = vjp(dm)
        dpa_ref[...] = dpa.astype(BF16)
        dpb_ref[...] = dpb.astype(BF16)
        dg_ref[:, :D] = dga.astype(BF16)
        dg_ref[:, D:] = dgb.astype(BF16)

    row = pl.BlockSpec((tm, D), lambda i: (i, 0))
    return _pc(kern, "merge_bwd", (s_len // tm,),
               [row, pl.BlockSpec((D, D), lambda i: (0, 0)), row, row,
                pl.BlockSpec((tm, 2 * D), lambda i: (i, C_GATE // (2 * D)))],
               [row, row, pl.BlockSpec((tm, 2 * D), lambda i: (i, 0))],
               [SDS((s_len, D), BF16), SDS((s_len, D), BF16), SDS((s_len, 2 * D), BF16)],
               )(dx1_b, w_out, pa, pb, proj)


def _d_branch(d_pa, d_pb, wa, wb):
    s_len = d_pa.shape[0]
    tm = min(512, s_len)

    def kern(da_ref, db_ref, wa_ref, wb_ref, oa_ref, ob_ref):
        acc_a = jnp.zeros((tm, DNW), F32)
        acc_b = jnp.zeros((tm, SWAW), F32)
        for s in range(N_CHIPS):
            acc_a = acc_a + _bdot(da_ref[:, s * CSH:(s + 1) * CSH], wa_ref[s], "nt")
            acc_b = acc_b + _bdot(db_ref[:, s * CSH:(s + 1) * CSH], wb_ref[s], "nt")
        oa_ref[...] = acc_a
        ob_ref[...] = acc_b

    row = pl.BlockSpec((tm, D), lambda i: (i, 0))
    w_spec = pl.BlockSpec((N_CHIPS, DNW, CSH), lambda i: (0, 0, 0))
    out = pl.BlockSpec((tm, DNW), lambda i: (i, 0))
    return _pc(kern, "d_branch", (s_len // tm,), [row, row, w_spec, w_spec], [out, out],
               [SDS((s_len, DNW), F32), SDS((s_len, SWAW), F32)])(d_pa, d_pb, wa, wb)


def _gw_branch(y_dn, y_swa, d_pa, d_pb):
    s_len = y_dn.shape[0]

    def kern(ya_ref, yb_ref, da_ref, db_ref, oa_ref, ob_ref):
        oa_ref[0] = _bdot(ya_ref[...], da_ref[...], "tn").astype(BF16)
        ob_ref[0] = _bdot(yb_ref[...], db_ref[...], "tn").astype(BF16)

    y_spec = pl.BlockSpec((s_len, DNW), lambda s: (0, 0))
    d_spec = pl.BlockSpec((s_len, CSH), lambda s: (0, s))
    o_spec = pl.BlockSpec((1, DNW, CSH), lambda s: (s, 0, 0))
    shape = (N_CHIPS, DNW, CSH)
    return _pc(kern, "gw_branch", (N_CHIPS,), [y_spec, y_spec, d_spec, d_spec], [o_spec, o_spec],
               [SDS(shape, BF16), SDS(shape, BF16)])(y_dn, y_swa, d_pa, d_pb)


def _dh_rms(d_proj, w_in_p, x, dx1, gain):
    s_len = x.shape[0]
    tm = min(256, s_len)

    def kern(dp_ref, w_ref, x_ref, r_ref, g_ref, gx_ref, dgain_ref):
        _zero_first([dgain_ref])
        dh = _bdot(dp_ref[...], w_ref[...], "nt")
        _, vjp = jax.vjp(_f_rms, x_ref[...], g_ref[...])
        dx, dgain = vjp(dh)
        gx_ref[...] = dx + r_ref[...]
        dgain_ref[...] += dgain

    row = pl.BlockSpec((tm, D), lambda i: (i, 0))
    vec = pl.BlockSpec((1, D), lambda i: (0, 0))
    return _pc(kern, "dh_rms", (s_len // tm,),
               [pl.BlockSpec((tm, PW), lambda i: (i, 0)), pl.BlockSpec((D, PW), lambda i: (0, 0)), row, row, vec],
               [row, vec], [SDS((s_len, D), F32), SDS((1, D), F32)])(d_proj, w_in_p, x, dx1, gain)


HALO = 8


def _conv_taps(cur_ref, prev_ref, halo, first):
    tm = cur_ref.shape[0]
    halo[0:HALO, :] = jnp.where(first, 0.0, prev_ref[...])
    halo[HALO:, :] = cur_ref[...]
    return [halo[HALO - n:HALO - n + tm, :] for n in range(CONV - 1, 0, -1)] + [cur_ref[...]]


def _dn_pre_specs(s_len, tm, blk):
    cur = pl.BlockSpec((tm, QKVW), lambda i: (blk(i), 0))
    prev = pl.BlockSpec((HALO, QKVW), lambda i: (jnp.maximum(blk(i) * (tm // HALO) - 1, 0), 0))
    ba = pl.BlockSpec((tm, 128), lambda i: (blk(i), C_BA // 128))
    row = pl.BlockSpec((tm, DNW), lambda i: (blk(i), 0))
    full = [pl.BlockSpec((CONV, QKVW), lambda i: (0, 0)), pl.BlockSpec((1, DN_H), lambda i: (0, 0)),
            pl.BlockSpec((1, DN_H), lambda i: (0, 0))]
    return cur, prev, ba, row, full


def _dn_pre_fwd(proj, conv_w, alog, dtb):
    s_len = proj.shape[0]
    tm = min(128, s_len)
    cur, prev, ba, row, full = _dn_pre_specs(s_len, tm, lambda i: i)

    def kern(cur_ref, prev_ref, ba_ref, cw_ref, al_ref, dt_ref, q_ref, k_ref, v_ref, bb_ref, gb_ref, halo):
        xs = _conv_taps(cur_ref, prev_ref, halo, pl.program_id(0) == 0)
        outs = _f_dn_pre(*xs, ba_ref[...], cw_ref[...], al_ref[...], dt_ref[...])
        for ref, val in zip((q_ref, k_ref, v_ref, bb_ref, gb_ref), outs, strict=True):
            ref[...] = val

    return _pc(kern, "dn_pre_fwd", (s_len // tm,), [cur, prev, ba] + full, [row] * 5, [SDS((s_len, DNW), F32)] * 5,
               scratch=[pltpu.VMEM((tm + HALO, QKVW), F32)])(proj, proj, proj, conv_w, alog, dtb)


def _dn_pre_bwd(proj, conv_w, alog, dtb, cots):
    s_len = proj.shape[0]
    tm = min(128, s_len)
    nb = s_len // tm
    cur, prev, ba, row, full = _dn_pre_specs(s_len, tm, lambda i: nb - 1 - i)

    def kern(cur_ref, prev_ref, ba_ref, cw_ref, al_ref, dt_ref, dq_ref, dk_ref, dv_ref, dbb_ref, dgb_ref,
             dqkv_ref, dba_ref, dcw_ref, dal_ref, ddt_ref, halo, *tails):
        i = pl.program_id(0)
        _zero_first([dcw_ref, dal_ref, ddt_ref])

        @pl.when(i == 0)
        def _():
            for t in tails:
                t[tm:, :] = jnp.zeros((HALO, QKVW), F32)

        xs = _conv_taps(cur_ref, prev_ref, halo, i == nb - 1)
        _, vjp = jax.vjp(_f_dn_pre, *xs, ba_ref[...], cw_ref[...], al_ref[...], dt_ref[...])
        *dxs, dba, dcw, dal, ddt = vjp((dq_ref[...], dk_ref[...], dv_ref[...], dbb_ref[...], dgb_ref[...]))
        total = dxs[CONV - 1]
        for j, t in enumerate(tails):
            n = CONV - 1 - j
            t[0:tm, :] = dxs[j]
            total = total + t[n:n + tm, :]
            t[tm:, :] = dxs[j][0:HALO, :]
        dqkv_ref[...] = total.astype(BF16)
        dba_ref[...] = dba.astype(BF16)
        dcw_ref[...] += dcw
        dal_ref[...] += dal
        ddt_ref[...] += ddt

    return _pc(kern, "dn_pre_bwd", (nb,), [cur, prev, ba] + full + [row] * 5,
               [cur, pl.BlockSpec((tm, 128), lambda i: (nb - 1 - i, 0))] + full,
               [SDS((s_len, QKVW), BF16), SDS((s_len, 128), BF16), SDS((CONV, QKVW), F32), SDS((1, DN_H), F32),
                SDS((1, DN_H), F32)],
               scratch=[pltpu.VMEM((tm + HALO, QKVW), F32)] * CONV)(proj, proj, proj, conv_w, alog, dtb, *cots)


def _pad_w_in(w_in):
    pieces = [w_in[:, o0:o0 + w] for o0, w, _ in sorted(_ORIG_PIECES, key=lambda t: t[2])]
    pieces.append(jnp.zeros((w_in.shape[0], PW - D_IN), w_in.dtype))
    return jnp.concatenate(pieces, axis=1)


def _unpad_w_in(g):
    return jnp.concatenate([g[:, p0:p0 + w] for _, w, p0 in _ORIG_PIECES], axis=1)


def _local_step(x, target, wts):
    s_len = x.shape[0]
    tm = min(256, s_len)
    tmh = min(128, s_len)
    w_in_p = wts["w_in_p"]
    attn_gain = wts["attn_norm"]
    ffn_gain = wts["ffn_norm"]
    conv_w = wts["dn_conv"]
    alog, dtb, out_gain = wts["dn_a_log"], wts["dn_dt_bias"], wts["dn_out_norm"]
    qg, kg = wts["swa_q_norm"], wts["swa_k_norm"]
    sinks = wts["swa_sinks"].reshape(SWA_KV, 1, SWA_G)

    (h,) = _rows(lambda r, f: ([_f_rms(r[0], f[0])], []), "rms1_fwd", s_len, tm, [_whole(x)], [attn_gain],
                 [(D, BF16)])
    proj = _mm(h, w_in_p, "nn", F32, 512, 1024, "mm_proj")
    q_dn, k_dn, v_dn, bb, gb = _dn_pre_fwd(proj, conv_w, alog, dtb)
    o_dn, s_all = _dn_chunks_fwd(q_dn, k_dn, v_dn, gb, bb)
    post_ins = [_whole(o_dn), (proj, DNW, C_Z // DNW)]
    (y_dn,) = _rows(lambda r, f: ([_f_dn_post(r[0], r[1], f[0])], []), "dn_post_fwd", s_len, tm, post_ins,
                    [out_gain], [(DNW, BF16)])

    sq = proj[:, C_SQ:C_SQ + SWAW].reshape(s_len, SWA_H, SWA_D).transpose(1, 0, 2)
    sk = proj[:, C_SK:C_SK + SWAKW].reshape(s_len, SWA_KV, SWA_D).transpose(1, 0, 2)
    sv = proj[:, C_SV:C_SV + SWAKW].reshape(s_len, SWA_KV, SWA_D).transpose(1, 0, 2)
    bias = _bias_expand(wts["rel_bias"].T).reshape(SWA_H, BLK, 2 * BLK)
    o_swa = _swa_fwd(sq, sk, sv, bias, qg, kg, sinks)
    y_swa = o_swa.transpose(1, 0, 2).reshape(s_len, SWAW).astype(BF16)

    p_a, p_b, merged = _branch_merge(y_dn, y_swa, wts["wa"], wts["wb"], proj)
    x1, h2 = _out_proj(merged, wts["w_out"], x, ffn_gain)
    gt, up, act = _ffn_up(h2, wts["wg"], wts["wu"])
    dy, dy_b, loss = _ffn_down_loss(act, wts["wd"], x1, target)

    grads = {}
    d_gt, d_up = _ffn_dact(dy_b, wts["wd"], gt, up)
    grads["w_down"] = _gw_down(act, dy_b)
    grads["w_gate"], grads["w_up"] = _gw_gate_up(h2, d_gt, d_up)
    dx1, dx1_b, grads["ffn_norm"] = _ffn_dh2(d_gt, d_up, wts["wg"], wts["wu"], x1, dy, ffn_gain)
    grads["w_out"] = _mm(merged, dx1_b, "tn", BF16, 512, 512, "gw_out")
    d_pa, d_pb, d_gr = _merge_bwd(dx1_b, wts["w_out"], p_a, p_b, proj)
    d_ydn, d_yswa = _d_branch(d_pa, d_pb, wts["wa"], wts["wb"])
    grads["w_branch_dn"], grads["w_branch_swa"] = _gw_branch(y_dn, y_swa, d_pa, d_pb)

    d_oswa = d_yswa.reshape(s_len, SWA_H, SWA_D).transpose(1, 0, 2)
    d_sq, d_sk, d_sv, d_bias, grads["swa_q_norm"], grads["swa_k_norm"], d_sinks = _swa_bwd(
        sq, sk, sv, bias, qg, kg, sinks, d_oswa)
    grads["swa_sinks"] = d_sinks.reshape(1, SWA_H)
    grads["rel_bias"] = _bias_reduce(d_bias.reshape(SWA_H, BLK * 2 * BLK)).T
    d_sq = d_sq.transpose(1, 0, 2).reshape(s_len, SWAW).astype(BF16)
    d_sk = d_sk.transpose(1, 0, 2).reshape(s_len, SWAKW).astype(BF16)
    d_sv = d_sv.transpose(1, 0, 2).reshape(s_len, SWAKW).astype(BF16)

    def post_bwd(r, f):
        _, vjp = jax.vjp(_f_dn_post, r[0], r[1], f[0])
        d_o, d_z, d_gain = vjp(r[2])
        return [d_o, d_z], [d_gain]

    d_o, d_z, grads["dn_out_norm"] = _rows(post_bwd, "dn_post_bwd", s_len, tm, post_ins + [_whole(d_ydn)], [out_gain],
                                           [(DNW, F32), (DNW, BF16)], [(1, DH)])
    d_q, d_k, d_v, d_gb, d_bb = _dn_chunks_bwd(q_dn, k_dn, v_dn, gb, bb, s_all, d_o)

    d_qkv, d_ba, grads["dn_conv"], grads["dn_a_log"], grads["dn_dt_bias"] = _dn_pre_bwd(
        proj, conv_w, alog, dtb, (d_q, d_k, d_v, d_bb, d_gb))

    d_proj = jnp.concatenate(
        [d_qkv, d_z, d_gr, d_sq, d_sk, d_sv, d_ba, jnp.zeros((s_len, PW - C_BA - 128), BF16)], axis=1)
    grads["w_in_p"] = _mm(h, d_proj, "tn", BF16, 512, 1024, "gw_in")
    grad_x, grads["attn_norm"] = _dh_rms(d_proj, w_in_p, x, dx1, attn_gain)
    return loss, grad_x, grads


_HBM = pl.BlockSpec(memory_space=pl.ANY)


def _place():
    return lax.axis_index("x"), lax.axis_index("y"), lax.axis_index("c")


def _other_chips(x, y):
    return [(1 - x, y), (x, 1 - y), (1 - x, 1 - y)]


def _rcopy(src, dst, send_sems, recv_sems, k, to):
    return pltpu.make_async_remote_copy(src_ref=src, dst_ref=dst, send_sem=send_sems.at[k], recv_sem=recv_sems.at[k],
                                        device_id=to, device_id_type=MESH)


def _comm_call(body, name, ins, out_shapes, n_remote, n_local):
    return pl.pallas_call(
        body, name=name, in_specs=[_HBM] * len(ins), out_specs=[_HBM] * len(out_shapes), out_shape=out_shapes,
        scratch_shapes=[pltpu.SemaphoreType.DMA((n_remote,)), pltpu.SemaphoreType.DMA((n_remote,)),
                        pltpu.SemaphoreType.DMA((max(n_local, 1),))],
        compiler_params=_cparams(has_side_effects=True),
    )(*ins)


def _gather_weights(ws):
    n = len(ws)
    halves = [w.shape[0] // 2 for w in ws]

    def body(*refs):
        w_refs, o_refs = refs[:n], refs[n:2 * n]
        send_sems, recv_sems, loc_sems = refs[2 * n:]
        x, y, c = _place()
        s = 2 * x + y
        sib = (x, y, 1 - c)
        chips = _other_chips(x, y)
        local = [pltpu.make_async_copy(w_refs[i], o_refs[i].at[s], loc_sems.at[i]) for i in range(n)]
        for cp in local:
            cp.start()

        def rows(i, half):
            return pl.ds(half * halves[i], halves[i])

        first = []
        for j, (cx, cy) in enumerate(chips):
            for i in range(n):
                cp = _rcopy(w_refs[i].at[rows(i, c), :], o_refs[i].at[s, rows(i, c), :], send_sems, recv_sems,
                            j * n + i, (cx, cy, c))
                cp.start()
                first.append(cp)
        passed = []
        for j, (cx, cy) in enumerate(chips):
            sj = 2 * cx + cy
            for i in range(n):
                blk = o_refs[i].at[sj, rows(i, c), :]
                _rcopy(blk, blk, send_sems, recv_sems, j * n + i, (cx, cy, c)).wait_recv()
                cp = _rcopy(blk, blk, send_sems, recv_sems, (3 + j) * n + i, sib)
                cp.start()
                passed.append(cp)
        for j, (cx, cy) in enumerate(chips):
            sj = 2 * cx + cy
            for i in range(n):
                blk = o_refs[i].at[sj, rows(i, 1 - c), :]
                _rcopy(blk, blk, send_sems, recv_sems, (3 + j) * n + i, sib).wait_recv()
        for cp in first + passed:
            cp.wait_send()
        for cp in local:
            cp.wait()

    return _comm_call(body, "gather_weights", ws, [SDS((N_CHIPS,) + w.shape, w.dtype) for w in ws], 6 * n, n)


def _swap_halves(gs):
    n = len(gs)
    halves = [g.shape[1] // 2 for g in gs]

    def body(*refs):
        g_refs, o_refs = refs[:n], refs[n:2 * n]
        send_sems, recv_sems, _ = refs[2 * n:]
        x, y, c = _place()
        cps = [_rcopy(g_refs[i].at[:, pl.ds((1 - c) * halves[i], halves[i]), :], o_refs[i], send_sems, recv_sems, i,
                      (x, y, 1 - c)) for i in range(n)]
        for cp in cps:
            cp.start()
        for cp in cps:
            cp.wait()

    return _comm_call(body, "swap_halves", gs, [SDS((N_CHIPS, h, g.shape[2]), g.dtype) for g, h in zip(gs, halves)],
                      n, 0)


def _chip_exchange(ps):
    n = len(ps)

    def body(*refs):
        p_refs, o_refs = refs[:n], refs[n:2 * n]
        send_sems, recv_sems, loc_sems = refs[2 * n:]
        x, y, c = _place()
        s = 2 * x + y
        chips = _other_chips(x, y)
        local = [pltpu.make_async_copy(p_refs[i].at[s], o_refs[i].at[s], loc_sems.at[i]) for i in range(n)]
        for cp in local:
            cp.start()
        sent = []
        for j, (cx, cy) in enumerate(chips):
            for i in range(n):
                cp = _rcopy(p_refs[i].at[2 * cx + cy], o_refs[i].at[s], send_sems, recv_sems, j * n + i, (cx, cy, c))
                cp.start()
                sent.append(cp)
        for j, (cx, cy) in enumerate(chips):
            sj = 2 * cx + cy
            for i in range(n):
                _rcopy(p_refs[i].at[sj], o_refs[i].at[sj], send_sems, recv_sems, j * n + i, (cx, cy, c)).wait_recv()
        for cp in sent:
            cp.wait_send()
        for cp in local:
            cp.wait()

    return _comm_call(body, "chip_exchange", ps, [SDS(p.shape, p.dtype) for p in ps], 3 * n, n)


def _swap_reduced(rs):
    n = len(rs)

    def body(*refs):
        r_refs, o_refs = refs[:n], refs[n:2 * n]
        send_sems, recv_sems, loc_sems = refs[2 * n:]
        x, y, c = _place()
        local, remote = [], []
        for i in range(n):
            h = rs[i].shape[0]
            mine = o_refs[i].at[pl.ds(c * h, h), :]
            local.append(pltpu.make_async_copy(r_refs[i], mine, loc_sems.at[i]))
            remote.append(_rcopy(r_refs[i], mine, send_sems, recv_sems, i, (x, y, 1 - c)))
        for cp in local + remote:
            cp.start()
        for i in range(n):
            h = rs[i].shape[0]
            theirs = o_refs[i].at[pl.ds((1 - c) * h, h), :]
            _rcopy(r_refs[i], theirs, send_sems, recv_sems, i, (x, y, 1 - c)).wait_recv()
        for cp in remote:
            cp.wait_send()
        for cp in local:
            cp.wait()

    return _comm_call(body, "swap_reduced", rs, [SDS((2 * r.shape[0], r.shape[1]), r.dtype) for r in rs], n, n)


def _all_sum_small(vec, name):
    n_dev = 8
    flips = [(bx, by, bc) for bx in (0, 1) for by in (0, 1) for bc in (0, 1)][1:]

    def body(v_ref, out_ref, gath, send_sems, recv_sems):
        x, y, c = _place()
        me = 4 * x + 2 * y + c
        gath[me] = v_ref[...]
        sent = []
        for k, (bx, by, bc) in enumerate(flips):
            peer = (x ^ bx, y ^ by, c ^ bc)
            cp = _rcopy(v_ref, gath.at[me], send_sems, recv_sems, k, peer)
            cp.start()
            sent.append(cp)
        for k, (bx, by, bc) in enumerate(flips):
            peer = (x ^ bx, y ^ by, c ^ bc)
            _rcopy(v_ref, gath.at[4 * peer[0] + 2 * peer[1] + peer[2]], send_sems, recv_sems, k, peer).wait_recv()
        for cp in sent:
            cp.wait_send()
        acc = gath[0]
        for d in range(1, n_dev):
            acc = acc + gath[d]
        out_ref[...] = acc

    vm = pl.BlockSpec(memory_space=pltpu.VMEM)
    return pl.pallas_call(
        body, name=name, in_specs=[vm], out_specs=vm, out_shape=SDS(vec.shape, F32),
        scratch_shapes=[pltpu.VMEM((n_dev,) + vec.shape, F32), pltpu.SemaphoreType.DMA((7,)),
                        pltpu.SemaphoreType.DMA((7,))],
        compiler_params=_cparams(has_side_effects=True),
    )(vec)


def _pack_small(vals, extra=None):
    parts = [vals[n].reshape(-1).astype(F32) for n, _ in _SMALL]
    parts.append(jnp.zeros((1,), F32) if extra is None else extra.reshape(1).astype(F32))
    flat = jnp.concatenate(parts)
    flat = jnp.concatenate([flat, jnp.zeros((_SMALL_ROWS * 128 - flat.shape[0],), F32)])
    return flat.reshape(_SMALL_ROWS, 128)


def _unpack_small(packed, shapes):
    flat = packed.reshape(-1)
    return {n: flat[_SMALL_OFF[n][0]:_SMALL_OFF[n][0] + _SMALL_OFF[n][1]].reshape(shapes[n]) for n, _ in _SMALL}


def _pair_sum(gs, gots, core):
    n = len(gs)

    def kern(c_ref, *refs):
        for i in range(n):
            refs[2 * n + i][...] = (refs[i][...].astype(F32) + refs[n + i][...].astype(F32)).astype(BF16)

    in_specs = [pl.BlockSpec((1, t.shape[1], t.shape[2]), lambda s, c_ref: (s, c_ref[0], 0)) for t in gots]
    in_specs += [pl.BlockSpec((1, t.shape[1], t.shape[2]), lambda s, c_ref: (s, 0, 0)) for t in gots]
    out_specs = [pl.BlockSpec((1, t.shape[1], t.shape[2]), lambda s, c_ref: (s, 0, 0)) for t in gots]
    return pl.pallas_call(
        kern, name="pair_sum",
        grid_spec=pltpu.PrefetchScalarGridSpec(num_scalar_prefetch=1, grid=(N_CHIPS,), in_specs=in_specs,
                                               out_specs=out_specs),
        out_shape=[SDS(t.shape, BF16) for t in gots],
        compiler_params=_cparams(dimension_semantics=("arbitrary",)),
    )(core.reshape(1).astype(jnp.int32), *gs, *gots)


def _chip_sum(qs):
    n = len(qs)

    def kern(*refs):
        for i in range(n):
            acc = refs[i][0].astype(F32)
            for s in range(1, N_CHIPS):
                acc = acc + refs[i][s].astype(F32)
            refs[n + i][...] = acc

    in_specs = [pl.BlockSpec((N_CHIPS, q.shape[1] // 2, q.shape[2]), lambda j: (0, j, 0)) for q in qs]
    out_specs = [pl.BlockSpec((q.shape[1] // 2, q.shape[2]), lambda j: (j, 0)) for q in qs]
    return _pc(kern, "chip_sum", (2,), in_specs, out_specs, [SDS(q.shape[1:], F32) for q in qs])(*qs)


def _adamw(w, g, m, v, name):
    rows, cols = w.shape
    tr = rows
    for cand in (256, 128, 64, 32, 16, 8):
        if rows % cand == 0 and rows > cand:
            tr = cand
            break

    def kern(w_ref, g_ref, m_ref, v_ref, d_ref, nm_ref, nv_ref):
        g_ = g_ref[...]
        m_ = ADAM_B1 * m_ref[...] + (1.0 - ADAM_B1) * g_
        v_ = ADAM_B2 * v_ref[...] + (1.0 - ADAM_B2) * jnp.square(g_)
        m_hat = m_ / (1.0 - ADAM_B1 ** ADAM_STEP)
        v_hat = v_ / (1.0 - ADAM_B2 ** ADAM_STEP)
        d_ref[...] = -ADAM_LR * (m_hat / (jnp.sqrt(v_hat) + ADAM_EPS) + ADAM_WD * w_ref[...])
        nm_ref[...] = m_
        nv_ref[...] = v_

    spec = pl.BlockSpec((tr, cols), lambda i: (i, 0))
    return _pc(kern, name, (rows // tr,), [spec] * 4, [spec] * 3, [SDS(w.shape, F32)] * 3)(w, g, m, v)


_WEIGHT_NAMES = ("attn_norm", "w_in", "dn_conv", "dn_a_log", "dn_dt_bias", "dn_out_norm", "swa_q_norm", "swa_k_norm",
                 "swa_sinks", "rel_bias", "w_branch_dn", "w_branch_swa", "w_out", "ffn_norm", "w_gate", "w_up",
                 "w_down")
_CONV_SH = QKVW // N_CHIPS


def kernel(x, attn_norm, w_in, dn_conv, dn_a_log, dn_dt_bias, dn_out_norm, swa_q_norm, swa_k_norm, swa_sinks, rel_bias, w_branch_dn, w_branch_swa, w_out, ffn_norm, w_gate, w_up, w_down, loss_target, m_attn_norm, m_w_in, m_dn_conv, m_dn_a_log, m_dn_dt_bias, m_dn_out_norm, m_swa_q_norm, m_swa_k_norm, m_swa_sinks, m_rel_bias, m_w_branch_dn, m_w_branch_swa, m_w_out, m_ffn_norm, m_w_gate, m_w_up, m_w_down, v_attn_norm, v_w_in, v_dn_conv, v_dn_a_log, v_dn_dt_bias, v_dn_out_norm, v_swa_q_norm, v_swa_k_norm, v_swa_sinks, v_rel_bias, v_w_branch_dn, v_w_branch_swa, v_w_out, v_ffn_norm, v_w_gate, v_w_up, v_w_down):
    w = dict(attn_norm=attn_norm, w_in=w_in, dn_conv=dn_conv, dn_a_log=dn_a_log, dn_dt_bias=dn_dt_bias,
             dn_out_norm=dn_out_norm, swa_q_norm=swa_q_norm, swa_k_norm=swa_k_norm, swa_sinks=swa_sinks,
             rel_bias=rel_bias, w_branch_dn=w_branch_dn, w_branch_swa=w_branch_swa, w_out=w_out, ffn_norm=ffn_norm,
             w_gate=w_gate, w_up=w_up, w_down=w_down)
    m = dict(attn_norm=m_attn_norm, w_in=m_w_in, dn_conv=m_dn_conv, dn_a_log=m_dn_a_log, dn_dt_bias=m_dn_dt_bias,
             dn_out_norm=m_dn_out_norm, swa_q_norm=m_swa_q_norm, swa_k_norm=m_swa_k_norm, swa_sinks=m_swa_sinks,
             rel_bias=m_rel_bias, w_branch_dn=m_w_branch_dn, w_branch_swa=m_w_branch_swa, w_out=m_w_out,
             ffn_norm=m_ffn_norm, w_gate=m_w_gate, w_up=m_w_up, w_down=m_w_down)
    v = dict(attn_norm=v_attn_norm, w_in=v_w_in, dn_conv=v_dn_conv, dn_a_log=v_dn_a_log, dn_dt_bias=v_dn_dt_bias,
             dn_out_norm=v_dn_out_norm, swa_q_norm=v_swa_q_norm, swa_k_norm=v_swa_k_norm, swa_sinks=v_swa_sinks,
             rel_bias=v_rel_bias, w_branch_dn=v_w_branch_dn, w_branch_swa=v_w_branch_swa, w_out=v_w_out,
             ffn_norm=v_ffn_norm, w_gate=v_w_gate, w_up=v_w_up, w_down=v_w_down)
    shapes = {n: w[n].shape for n in _WEIGHT_NAMES}

    def two_d(a):
        return a.reshape(a.shape[-2], a.shape[-1]) if a.ndim == 3 else a

    core = lax.axis_index("c")
    chip = 2 * lax.axis_index("x") + lax.axis_index("y")
    small_shapes = {n: two_d(w[n]).shape for n, _ in _SMALL}
    small_shapes["dn_conv"] = (CONV, QKVW)

    conv_loc = two_d(w["dn_conv"])
    conv_part = lax.dynamic_update_slice(jnp.zeros((CONV, QKVW), F32), jnp.where(core == 0, conv_loc, 0.0),
                                         (0, chip * _CONV_SH))
    conv_full = _all_sum_small(conv_part.reshape(CONV * QKVW // 128, 128), "gather_conv").reshape(CONV, QKVW)

    local_blocks = {n: two_d(w[n]) for n in _BIG_NAMES}
    gathered = dict(zip(_BIG_NAMES, _gather_weights([local_blocks[n].astype(BF16) for n in _BIG_NAMES])))
    w_in_full = gathered["w_in"].transpose(1, 0, 2).reshape(D, D_IN)
    wts = dict(w_in_p=_pad_w_in(w_in_full), wa=gathered["w_branch_dn"], wb=gathered["w_branch_swa"],
               w_out=gathered["w_out"].reshape(D, D), wg=gathered["w_gate"], wu=gathered["w_up"],
               wd=gathered["w_down"], dn_conv=conv_full)
    for n, _ in _SMALL[:-1]:
        wts[n] = two_d(w[n])

    loss_sum, grad_x, grads = _local_step(x[0], loss_target[0], wts)

    g_in = _unpad_w_in(grads["w_in_p"]).reshape(D, N_CHIPS, D_IN // N_CHIPS).transpose(1, 0, 2)
    gs = [g_in, grads["w_branch_dn"], grads["w_branch_swa"], grads["w_out"].reshape(N_CHIPS, CSH, D),
          grads["w_gate"], grads["w_up"], grads["w_down"]]
    gots = _swap_halves(gs)
    parts = _pair_sum(gs, gots, core)
    reduced = _chip_sum(_chip_exchange(parts))
    g_blocks = dict(zip(_BIG_NAMES, _swap_reduced(reduced)))

    small_sum = _all_sum_small(_pack_small(grads, loss_sum), "all_sum_small")
    loss = small_sum.reshape(-1)[_LOSS_OFF]
    g_small = _unpack_small(small_sum, small_shapes)

    g_out, d_out, m_out, v_out = {}, {}, {}, {}
    for n in _BIG_NAMES:
        g_out[n] = g_blocks[n].reshape(shapes[n])
        d_, m_, v_ = _adamw(local_blocks[n], g_blocks[n], two_d(m[n]), two_d(v[n]), "adamw_" + n)
        d_out[n], m_out[n], v_out[n] = d_.reshape(shapes[n]), m_.reshape(shapes[n]), v_.reshape(shapes[n])
    g_conv = lax.dynamic_slice(g_small["dn_conv"], (0, chip * _CONV_SH), (CONV, _CONV_SH))
    g_out["dn_conv"] = g_conv.reshape(shapes["dn_conv"])
    d_, m_, v_ = _adamw(conv_loc, g_conv, two_d(m["dn_conv"]), two_d(v["dn_conv"]), "adamw_dn_conv")
    d_out["dn_conv"], m_out["dn_conv"], v_out["dn_conv"] = (t.reshape(shapes["dn_conv"]) for t in (d_, m_, v_))

    def packed(src):
        vals = {n: src[n] for n, _ in _SMALL[:-1]}
        vals["dn_conv"] = jnp.zeros((CONV * QKVW,), F32)
        return _pack_small(vals)

    d_s, m_s, v_s = _adamw(packed(w), small_sum, packed(m), packed(v), "adamw_small")
    d_small, m_small, v_small = (_unpack_small(t, small_shapes) for t in (d_s, m_s, v_s))
    for n, _ in _SMALL[:-1]:
        g_out[n] = g_small[n].reshape(shapes[n])
        d_out[n], m_out[n], v_out[n] = (t[n].reshape(shapes[n]) for t in (d_small, m_small, v_small))

    return (loss, grad_x[None], *[g_out[n] for n in _WEIGHT_NAMES], *[d_out[n] for n in _WEIGHT_NAMES],
            *[m_out[n] for n in _WEIGHT_NAMES], *[v_out[n] for n in _WEIGHT_NAMES])
```

```python
import functools
import math

import numpy as np
import jax
import jax.numpy as jnp
from jax import lax
from jax.experimental import pallas as pl
from jax.experimental.pallas import tpu as pltpu

F32 = jnp.float32
BF16 = jnp.bfloat16
SDS = jax.ShapeDtypeStruct

D = 1024
DN_H = 4
DH = 128
DNW = DN_H * DH
QKVW = 3 * DNW
CONV = 4
CHUNK = 64
SWA_H = 8
SWA_KV = 2
SWA_G = SWA_H // SWA_KV
SWA_D = 64
SWAW = SWA_H * SWA_D
SWAKW = SWA_KV * SWA_D
BLK = 128
NBUCKET = 32
MAXDIST = 128
DFF = 2816
D_IN = QKVW + DNW + 2 * DN_H + SWAW + 2 * SWAKW + 2 * D
EPS = 1e-6
NEG = -1e30

ADAM_LR = 0.001
ADAM_B1 = 0.9
ADAM_B2 = 0.999
ADAM_EPS = 1e-08
ADAM_WD = 0.01
ADAM_STEP = 10

C_QKV, C_Z, C_GATE, C_SQ, C_SK, C_SV, C_BA = 0, 1536, 2048, 4096, 4608, 4736, 4864
PW = 5120
_ORIG_PIECES = (
    (0, QKVW, C_QKV),
    (QKVW, DNW, C_Z),
    (QKVW + DNW, 2 * DN_H, C_BA),
    (QKVW + DNW + 2 * DN_H, SWAW, C_SQ),
    (QKVW + DNW + 2 * DN_H + SWAW, SWAKW, C_SK),
    (QKVW + DNW + 2 * DN_H + SWAW + SWAKW, SWAKW, C_SV),
    (QKVW + DNW + 2 * DN_H + SWAW + 2 * SWAKW, 2 * D, C_GATE),
)

N_CHIPS = 4
FSH = DFF // N_CHIPS
CSH = D // N_CHIPS
VMEM_LIMIT = 48 * 1024 * 1024
MESH = pl.DeviceIdType.MESH

_BIG = (
    ("w_in", D, D_IN // N_CHIPS),
    ("w_branch_dn", DNW, CSH),
    ("w_branch_swa", SWAW, CSH),
    ("w_out", CSH, D),
    ("w_gate", D, FSH),
    ("w_up", D, FSH),
    ("w_down", FSH, D),
)
_BIG_NAMES = tuple(n for n, _, _ in _BIG)

_SMALL = (
    ("attn_norm", D), ("ffn_norm", D), ("dn_out_norm", DH), ("swa_q_norm", SWA_D), ("swa_k_norm", SWA_D),
    ("swa_sinks", SWA_H), ("dn_a_log", DN_H), ("dn_dt_bias", DN_H), ("rel_bias", NBUCKET * SWA_H),
    ("dn_conv", CONV * QKVW),
)
_SMALL_OFF = {}
_o = 0
for _n, _s in _SMALL:
    _SMALL_OFF[_n] = (_o, _s)
    _o += _s
_LOSS_OFF = _o
_SMALL_ROWS = -(-(_o + 1) // (8 * 128)) * 8


def _cparams(**kw):
    return pltpu.CompilerParams(vmem_limit_bytes=VMEM_LIMIT, **kw)


_DIMS = {
    "nn": (((1,), (0,)), ((), ())),
    "nt": (((1,), (1,)), ((), ())),
    "tn": (((0,), (0,)), ((), ())),
    "bnn": (((2,), (1,)), ((0,), (0,))),
    "bnt": (((2,), (2,)), ((0,), (0,))),
    "btn": (((1,), (1,)), ((0,), (0,))),
}


def _raw_dot(a, b, kind, exact):
    if exact:
        prec = lax.Precision.HIGH if exact == "x3" else lax.Precision.HIGHEST
        return lax.dot_general(a, b, _DIMS[kind], precision=prec, preferred_element_type=F32)
    return lax.dot_general(a.astype(BF16), b.astype(BF16), _DIMS[kind], preferred_element_type=F32)


@functools.partial(jax.custom_vjp, nondiff_argnums=(2, 3))
def _dot(a, b, kind, exact):
    return _raw_dot(a, b, kind, exact)


def _dot_fwd(a, b, kind, exact):
    return _raw_dot(a, b, kind, exact), (a, b)


def _dot_bwd(kind, exact, res, g):
    a, b = res
    pre = kind[:-2]
    nn, nt, tn = pre + "nn", pre + "nt", pre + "tn"
    if kind == nn:
        return _dot(g, b, nt, exact), _dot(a, g, tn, exact)
    if kind == nt:
        return _dot(g, b, nn, exact), _dot(g, a, tn, exact)
    return _dot(b, g, nt, exact), _dot(a, g, nn, exact)


_dot.defvjp(_dot_fwd, _dot_bwd)


def _silu(x):
    return x * jax.nn.sigmoid(x)


def _f_rms(x, gain):
    return x * lax.rsqrt(jnp.mean(x * x, axis=-1, keepdims=True) + EPS) * gain


def _f_dn_pre(xs0, xs1, xs2, xs3, ba, cw, alog, dtb):
    rows = xs0.shape[0]
    c = xs0 * cw[0:1] + xs1 * cw[1:2] + xs2 * cw[2:3] + xs3 * cw[3:4]
    qkv = _silu(c)
    qs, ks, bbs, gbs = [], [], [], []
    for h in range(DN_H):
        qh = qkv[:, h * DH:(h + 1) * DH]
        kh = qkv[:, DNW + h * DH:DNW + (h + 1) * DH]
        qs.append(qh * lax.rsqrt(jnp.sum(qh * qh, axis=-1, keepdims=True) + EPS) * (DH ** -0.5))
        ks.append(kh * lax.rsqrt(jnp.sum(kh * kh, axis=-1, keepdims=True) + EPS))
        beta = jax.nn.sigmoid(ba[:, h:h + 1])
        ar = ba[:, DN_H + h:DN_H + h + 1] + dtb[:, h:h + 1]
        softplus = jnp.maximum(ar, 0.0) + jnp.log1p(jnp.exp(-jnp.abs(ar)))
        g = -jnp.exp(alog[:, h:h + 1]) * softplus
        bbs.append(jnp.broadcast_to(beta, (rows, DH)))
        gbs.append(jnp.broadcast_to(g, (rows, DH)))
    return (jnp.concatenate(qs, axis=1), jnp.concatenate(ks, axis=1), qkv[:, 2 * DNW:],
            jnp.concatenate(bbs, axis=1), jnp.concatenate(gbs, axis=1))


def _f_dn_post(o, z, gain):
    ys = []
    for h in range(DN_H):
        oh = o[:, h * DH:(h + 1) * DH]
        zh = z[:, h * DH:(h + 1) * DH]
        ys.append(oh * lax.rsqrt(jnp.mean(oh * oh, axis=-1, keepdims=True) + EPS) * gain * _silu(zh))
    return jnp.concatenate(ys, axis=1)


def _f_merge(pa, pb, ga, gb):
    return jax.nn.sigmoid(ga) * pa + jax.nn.sigmoid(gb) * pb


def _f_swiglu(g, u):
    return _silu(g) * u


def _f_chunk(q, k, v, gb, bb, s):
    c = CHUNK
    nh = q.shape[0]
    ii = lax.broadcasted_iota(jnp.int32, (nh, c, c), 1)
    jj = lax.broadcasted_iota(jnp.int32, (nh, c, c), 2)
    incl = ii >= jj
    strict = ii > jj
    eye = (ii == jj).astype(F32)
    gcb = _dot(incl.astype(F32), gb, "bnn", True)
    lane0 = (lax.broadcasted_iota(jnp.int32, (nh, c, DH), 2) == 0).astype(F32)
    gcol = gcb[:, :, :c]
    grow = _dot(lane0, gcb, "bnt", True)
    decay = jnp.where(incl, jnp.exp(jnp.where(incl, gcol - grow, 0.0)), 0.0)
    kb = k * bb
    vb = v * bb
    a = jnp.where(strict, _dot(kb, k, "bnt", False) * decay, 0.0)
    p = -a
    t = eye + p
    for _ in range(5):
        p = _dot(p, p, "bnn", "x3")
        t = t + _dot(t, p, "bnn", "x3")
    eg = jnp.exp(gcb)
    u = _dot(t, vb, "bnn", "x3")
    w = _dot(t, kb * eg, "bnn", "x3")
    qk = jnp.where(incl, _dot(q, k, "bnt", False) * decay, 0.0)
    qe = q * eg
    glast = gcb[:, c - 1:c, :]
    k_dec = k * jnp.exp(glast - gcb)
    e_last = jnp.exp(glast)
    outs = []
    for g in range(nh // DN_H):
        sl = slice(g * DN_H, (g + 1) * DN_H)
        v_new = u[sl] - _dot(w[sl], s, "bnn", False)
        outs.append(_dot(qe[sl], s, "bnn", False) + _dot(qk[sl], v_new, "bnn", False))
        s = s * e_last[sl] + _dot(k_dec[sl], v_new, "btn", False)
    return jnp.concatenate(outs, axis=0), s


def _f_swa(q4, kp, kc, vp, vc, bias4, qg, kg, sink, mask):
    kb = jnp.concatenate([kp, kc], axis=0)
    vb = jnp.concatenate([vp, vc], axis=0)
    kn = kb * lax.rsqrt(jnp.mean(kb * kb, axis=-1, keepdims=True) + EPS) * kg
    outs = []
    for g in range(SWA_G):
        qq = q4[g]
        qn = qq * lax.rsqrt(jnp.mean(qq * qq, axis=-1, keepdims=True) + EPS) * qg
        lg = _dot(qn, kn, "nt", False) * (SWA_D ** -0.5) + bias4[g]
        lg = jnp.where(mask, lg, NEG)
        sk = sink[:, g:g + 1]
        m = lax.stop_gradient(jnp.maximum(jnp.max(lg, axis=-1, keepdims=True), sk))
        p = jnp.exp(lg - m)
        den = jnp.sum(p, axis=-1, keepdims=True) + jnp.exp(sk - m)
        outs.append(_dot(p / den, vb, "nn", False))
    return jnp.stack(outs, axis=0)


def _bdot(a, b, kind="nn"):
    return lax.dot_general(a.astype(BF16), b.astype(BF16), _DIMS[kind], preferred_element_type=F32)


def _pc(kern, name, grid, in_specs, out_specs, out_shape, scratch=()):
    return pl.pallas_call(
        kern, name=name, grid=grid, in_specs=in_specs, out_specs=out_specs, out_shape=out_shape,
        scratch_shapes=list(scratch), compiler_params=_cparams(dimension_semantics=("arbitrary",) * len(grid)))


def _mm(a, b, kind, out_dtype, tm, tn, name):
    if kind == "tn":
        k, m = a.shape
    else:
        m, k = a.shape
    n = b.shape[0] if kind == "nt" else b.shape[1]
    tm, tn = min(tm, m), min(tn, n)
    assert m % tm == 0 and n % tn == 0, (name, a.shape, b.shape, tm, tn)

    def kern(a_ref, b_ref, o_ref):
        o_ref[...] = _bdot(a_ref[...], b_ref[...], kind).astype(o_ref.dtype)

    a_spec = pl.BlockSpec((k, tm), lambda i, j: (0, i)) if kind == "tn" else pl.BlockSpec((tm, k), lambda i, j: (i, 0))
    b_spec = pl.BlockSpec((tn, k), lambda i, j: (j, 0)) if kind == "nt" else pl.BlockSpec((k, tn), lambda i, j: (0, j))
    return _pc(kern, name, (m // tm, n // tn), [a_spec, b_spec], pl.BlockSpec((tm, tn), lambda i, j: (i, j)),
               SDS((m, n), out_dtype))(a, b)


def _rows(body, name, m, tm, row_ins, full_ins, row_outs, acc_outs=()):
    n_r, n_f, n_o, n_a = len(row_ins), len(full_ins), len(row_outs), len(acc_outs)
    assert m % tm == 0

    def kern(*refs):
        r = refs[:n_r]
        f = refs[n_r:n_r + n_f]
        o = refs[n_r + n_f:n_r + n_f + n_o]
        acc = refs[n_r + n_f + n_o:]
        outs, sums = body([x[...] for x in r], [x[...] for x in f])
        for ref, val in zip(o, outs, strict=True):
            ref[...] = val.astype(ref.dtype)
        if n_a:
            @pl.when(pl.program_id(0) == 0)
            def _():
                for ref in acc:
                    ref[...] = jnp.zeros(ref.shape, F32)

            for ref, val in zip(acc, sums, strict=True):
                ref[...] += val

    in_specs = [pl.BlockSpec((tm, w), functools.partial(lambda i, cb: (i, cb), cb=cb)) for _, w, cb in row_ins]
    in_specs += [pl.BlockSpec(x.shape, lambda i: (0, 0)) for x in full_ins]
    out_specs = [pl.BlockSpec((tm, w), lambda i: (i, 0)) for w, _ in row_outs]
    out_specs += [pl.BlockSpec(s, lambda i: (0, 0)) for s in acc_outs]
    out_shape = [SDS((m, w), dt) for w, dt in row_outs]
    out_shape += [SDS(s, F32) for s in acc_outs]
    return _pc(kern, name, (m // tm,), in_specs, out_specs, out_shape)(*[x for x, _, _ in row_ins], *full_ins)


def _whole(x):
    return (x, x.shape[1], 0)


def _zero_first(refs):
    @pl.when(pl.program_id(0) == 0)
    def _():
        for ref in refs:
            ref[...] = jnp.zeros(ref.shape, F32)


GROUP = 4


def _heads(ref):
    return jnp.stack([ref[g * CHUNK:(g + 1) * CHUNK, h * DH:(h + 1) * DH]
                      for g in range(GROUP) for h in range(DN_H)], axis=0)


def _unheads(ref, val):
    for g in range(GROUP):
        for h in range(DN_H):
            ref[g * CHUNK:(g + 1) * CHUNK, h * DH:(h + 1) * DH] = val[g * DN_H + h]


def _dn_chunks_fwd(q, k, v, gb, bb):
    s_len = q.shape[0]
    ng = s_len // (GROUP * CHUNK)

    def kern(q_ref, k_ref, v_ref, g_ref, b_ref, o_ref, sall_ref, state):
        _zero_first([state])
        s = state[...]
        sall_ref[0] = s
        o, s_new = _f_chunk(*[_heads(r) for r in (q_ref, k_ref, v_ref, g_ref, b_ref)], s)
        _unheads(o_ref, o)
        state[...] = s_new

    blk = pl.BlockSpec((GROUP * CHUNK, DNW), lambda c: (c, 0))
    return _pc(kern, "dn_chunks_fwd", (ng,), [blk] * 5,
               [blk, pl.BlockSpec((1, DN_H, DH, DH), lambda c: (c, 0, 0, 0))],
               [SDS((s_len, DNW), F32), SDS((ng, DN_H, DH, DH), F32)],
               scratch=[pltpu.VMEM((DN_H, DH, DH), F32)])(q, k, v, gb, bb)


def _dn_chunks_bwd(q, k, v, gb, bb, s_all, d_o):
    s_len = q.shape[0]
    ng = s_len // (GROUP * CHUNK)

    def kern(q_ref, k_ref, v_ref, g_ref, b_ref, sall_ref, do_ref, dq_ref, dk_ref, dv_ref, dg_ref, db_ref, dstate):
        _zero_first([dstate])
        _, vjp = jax.vjp(_f_chunk, *[_heads(r) for r in (q_ref, k_ref, v_ref, g_ref, b_ref)], sall_ref[0])
        *d_ins, ds = vjp((_heads(do_ref), dstate[...]))
        for ref, val in zip((dq_ref, dk_ref, dv_ref, dg_ref, db_ref), d_ins, strict=True):
            _unheads(ref, val)
        dstate[...] = ds

    blk = pl.BlockSpec((GROUP * CHUNK, DNW), lambda c: (ng - 1 - c, 0))
    return _pc(kern, "dn_chunks_bwd", (ng,),
               [blk] * 5 + [pl.BlockSpec((1, DN_H, DH, DH), lambda c: (ng - 1 - c, 0, 0, 0)), blk],
               [blk] * 5, [SDS((s_len, DNW), F32)] * 5,
               scratch=[pltpu.VMEM((DN_H, DH, DH), F32)])(q, k, v, gb, bb, s_all, d_o)


def _t5_bucket_table():
    qi = np.arange(BLK)[:, None]
    kj = np.arange(2 * BLK)[None, :]
    dist = BLK + qi - kj
    n = np.maximum(dist, 0)
    max_exact = NBUCKET // 2
    nf = np.maximum(n, 1).astype(np.float32)
    large = max_exact + (np.log(nf / np.float32(max_exact)) / np.float32(math.log(MAXDIST / max_exact))
                         * np.float32(NBUCKET - max_exact)).astype(np.int32)
    large = np.minimum(large, NBUCKET - 1)
    return np.where(n < max_exact, n, large)


def _bucket_onehot_t():
    table = _t5_bucket_table().reshape(-1)
    return (np.arange(NBUCKET)[:, None] == table[None, :]).astype(np.float32)


def _swa_mask(first):
    qi = lax.broadcasted_iota(jnp.int32, (BLK, 2 * BLK), 0)
    kj = lax.broadcasted_iota(jnp.int32, (BLK, 2 * BLK), 1)
    dist = BLK + qi - kj
    window = (dist >= 0) & (dist < BLK)
    return window & ((kj >= BLK) | jnp.logical_not(first))


def _bias_expand(rel_bias_t):
    onehot = jnp.asarray(_bucket_onehot_t())

    def kern(r_ref, oh_ref, o_ref):
        o_ref[...] = _raw_dot(r_ref[...], oh_ref[...], "nn", True)

    return pl.pallas_call(
        kern, name="bias_expand", out_shape=SDS((SWA_H, BLK * 2 * BLK), F32), compiler_params=_cparams(),
    )(rel_bias_t, onehot)


def _bias_reduce(d_bias_flat):
    onehot = jnp.asarray(_bucket_onehot_t())

    def kern(d_ref, oh_ref, o_ref):
        o_ref[...] = _raw_dot(d_ref[...], oh_ref[...], "nt", True)

    return pl.pallas_call(
        kern, name="bias_reduce", out_shape=SDS((SWA_H, NBUCKET), F32), compiler_params=_cparams(),
    )(d_bias_flat, onehot)


def _swa_specs(nb, rev):
    def blk(n):
        return (nb - 1 - n) if rev else n

    q_spec = pl.BlockSpec((SWA_G, BLK, SWA_D), lambda kv, n: (kv, blk(n), 0))
    cur = pl.BlockSpec((1, BLK, SWA_D), lambda kv, n: (kv, blk(n), 0))
    prev = pl.BlockSpec((1, BLK, SWA_D), lambda kv, n: (kv, jnp.maximum(blk(n) - 1, 0), 0))
    bias = pl.BlockSpec((SWA_G, BLK, 2 * BLK), lambda kv, n: (kv, 0, 0))
    gain = pl.BlockSpec((1, SWA_D), lambda kv, n: (0, 0))
    sink = pl.BlockSpec((1, 1, SWA_G), lambda kv, n: (kv, 0, 0))
    return q_spec, cur, prev, bias, gain, sink


def _swa_fwd(q, k, v, bias, qg, kg, sinks):
    s_len = q.shape[1]
    nb = s_len // BLK
    q_spec, cur, prev, bias_spec, gain, sink = _swa_specs(nb, False)

    def kern(q_ref, kp_ref, kc_ref, vp_ref, vc_ref, b_ref, qg_ref, kg_ref, s_ref, o_ref):
        mask = _swa_mask(pl.program_id(1) == 0)
        o_ref[...] = _f_swa(q_ref[...], kp_ref[0], kc_ref[0], vp_ref[0], vc_ref[0], b_ref[...], qg_ref[...],
                            kg_ref[...], s_ref[0], mask)

    return _pc(kern, "swa_fwd", (SWA_KV, nb), [q_spec, prev, cur, prev, cur, bias_spec, gain, gain, sink],
               q_spec, SDS((SWA_H, s_len, SWA_D), F32))(q, k, k, v, v, bias, qg, kg, sinks)


def _swa_bwd(q, k, v, bias, qg, kg, sinks, d_out):
    s_len = q.shape[1]
    nb = s_len // BLK
    q_spec, cur, prev, bias_spec, gain, sink = _swa_specs(nb, True)

    def kern(q_ref, kp_ref, kc_ref, vp_ref, vc_ref, b_ref, qg_ref, kg_ref, s_ref, do_ref,
             dq_ref, dk_ref, dv_ref, db_ref, dqg_ref, dkg_ref, ds_ref, carry_k, carry_v):
        kv = pl.program_id(0)
        n = pl.program_id(1)
        mask = _swa_mask(n == nb - 1)

        @pl.when(n == 0)
        def _():
            carry_k[...] = jnp.zeros(carry_k.shape, F32)
            carry_v[...] = jnp.zeros(carry_v.shape, F32)
            db_ref[...] = jnp.zeros(db_ref.shape, F32)
            ds_ref[...] = jnp.zeros(ds_ref.shape, F32)

        @pl.when((n == 0) & (kv == 0))
        def _():
            dqg_ref[...] = jnp.zeros(dqg_ref.shape, F32)
            dkg_ref[...] = jnp.zeros(dkg_ref.shape, F32)

        fn = functools.partial(_f_swa, mask=mask)
        _, vjp = jax.vjp(fn, q_ref[...], kp_ref[0], kc_ref[0], vp_ref[0], vc_ref[0], b_ref[...], qg_ref[...],
                         kg_ref[...], s_ref[0])
        dq, dkp, dkc, dvp, dvc, dbias, dqg, dkg, dsink = vjp(do_ref[...])
        dq_ref[...] = dq
        dk_ref[0] = dkc + carry_k[...]
        dv_ref[0] = dvc + carry_v[...]
        carry_k[...] = dkp
        carry_v[...] = dvp
        db_ref[...] += dbias
        dqg_ref[...] += dqg
        dkg_ref[...] += dkg
        ds_ref[0] += dsink

    return _pc(
        kern, "swa_bwd", (SWA_KV, nb),
        [q_spec, prev, cur, prev, cur, bias_spec, gain, gain, sink, q_spec],
        [q_spec, cur, cur, bias_spec, gain, gain, sink],
        [SDS((SWA_H, s_len, SWA_D), F32), SDS((SWA_KV, s_len, SWA_D), F32), SDS((SWA_KV, s_len, SWA_D), F32),
         SDS((SWA_H, BLK, 2 * BLK), F32), SDS((1, SWA_D), F32), SDS((1, SWA_D), F32), SDS((SWA_KV, 1, SWA_G), F32)],
        scratch=[pltpu.VMEM((BLK, SWA_D), F32), pltpu.VMEM((BLK, SWA_D), F32)],
    )(q, k, k, v, v, bias, qg, kg, sinks, d_out)


def _branch_merge(y_dn, y_swa, wa, wb, proj):
    s_len = y_dn.shape[0]
    tm = min(512, s_len)

    def kern(ya_ref, yb_ref, wa_ref, wb_ref, ga_ref, gb_ref, pa_ref, pb_ref, m_ref):
        pa = _bdot(ya_ref[...], wa_ref[0])
        pb = _bdot(yb_ref[...], wb_ref[0])
        pa_ref[...] = pa
        pb_ref[...] = pb
        m_ref[...] = _f_merge(pa, pb, ga_ref[...], gb_ref[...]).astype(BF16)

    y_spec = pl.BlockSpec((tm, DNW), lambda i, s: (i, 0))
    w_spec = pl.BlockSpec((1, DNW, CSH), lambda i, s: (s, 0, 0))
    o_spec = pl.BlockSpec((tm, CSH), lambda i, s: (i, s))
    ga_spec = pl.BlockSpec((tm, CSH), lambda i, s: (i, C_GATE // CSH + s))
    gb_spec = pl.BlockSpec((tm, CSH), lambda i, s: (i, (C_GATE + D) // CSH + s))
    return _pc(kern, "branch_merge", (s_len // tm, N_CHIPS), [y_spec, y_spec, w_spec, w_spec, ga_spec, gb_spec],
               [o_spec] * 3, [SDS((s_len, D), F32), SDS((s_len, D), F32), SDS((s_len, D), BF16)],
               )(y_dn, y_swa, wa, wb, proj, proj)


def _out_proj(merged, w_out, x, gain):
    s_len = x.shape[0]
    tm = min(256, s_len)

    def kern(m_ref, w_ref, x_ref, g_ref, x1_ref, h2_ref):
        x1 = x_ref[...] + _bdot(m_ref[...], w_ref[...])
        x1_ref[...] = x1
        h2_ref[...] = _f_rms(x1, g_ref[...]).astype(BF16)

    row = pl.BlockSpec((tm, D), lambda i: (i, 0))
    return _pc(kern, "out_proj", (s_len // tm,),
               [row, pl.BlockSpec((D, D), lambda i: (0, 0)), row, pl.BlockSpec((1, D), lambda i: (0, 0))],
               [row, row], [SDS((s_len, D), F32), SDS((s_len, D), BF16)])(merged, w_out, x, gain)


def _ffn_up(h2, wg, wu):
    s_len = h2.shape[0]
    tm = min(512, s_len)

    def kern(h_ref, g_ref, u_ref, gt_ref, up_ref, act_ref):
        h = h_ref[...]
        g = _bdot(h, g_ref[0])
        u = _bdot(h, u_ref[0])
        gt_ref[0] = g
        up_ref[0] = u
        act_ref[0] = _f_swiglu(g, u).astype(BF16)

    w_spec = pl.BlockSpec((1, D, FSH), lambda s, i: (s, 0, 0))
    o_spec = pl.BlockSpec((1, tm, FSH), lambda s, i: (s, i, 0))
    shape = (N_CHIPS, s_len, FSH)
    return _pc(kern, "ffn_up", (N_CHIPS, s_len // tm), [pl.BlockSpec((tm, D), lambda s, i: (i, 0)), w_spec, w_spec],
               [o_spec] * 3, [SDS(shape, F32), SDS(shape, F32), SDS(shape, BF16)])(h2, wg, wu)


def _ffn_down_loss(act, wd, x1, target):
    s_len = x1.shape[0]
    tm = min(256, s_len)

    def kern(a_ref, w_ref, x_ref, t_ref, dy_ref, dyb_ref, loss_ref):
        _zero_first([loss_ref])
        y = x_ref[...]
        for s in range(N_CHIPS):
            y = y + _bdot(a_ref[s], w_ref[s])
        d = y - t_ref[...]
        dy = d * (1.0 / D)
        dy_ref[...] = dy
        dyb_ref[...] = dy.astype(BF16)
        loss_ref[...] += jnp.sum(d * d).reshape(1, 1) * (0.5 / D)

    row = pl.BlockSpec((tm, D), lambda i: (i, 0))
    return _pc(kern, "ffn_down_loss", (s_len // tm,),
               [pl.BlockSpec((N_CHIPS, tm, FSH), lambda i: (0, i, 0)),
                pl.BlockSpec((N_CHIPS, FSH, D), lambda i: (0, 0, 0)), row, row],
               [row, row, pl.BlockSpec((1, 1), lambda i: (0, 0))],
               [SDS((s_len, D), F32), SDS((s_len, D), BF16), SDS((1, 1), F32)])(act, wd, x1, target)


def _ffn_dact(dy_b, wd, gt, up):
    s_len = dy_b.shape[0]
    tm = min(512, s_len)

    def kern(dy_ref, w_ref, gt_ref, up_ref, dg_ref, du_ref):
        d_act = _bdot(dy_ref[...], w_ref[0], "nt")
        _, vjp = jax.vjp(_f_swiglu, gt_ref[0], up_ref[0])
        dg, du = vjp(d_act)
        dg_ref[0] = dg.astype(BF16)
        du_ref[0] = du.astype(BF16)

    a_spec = pl.BlockSpec((1, tm, FSH), lambda s, i: (s, i, 0))
    shape = (N_CHIPS, s_len, FSH)
    return _pc(kern, "ffn_dact", (N_CHIPS, s_len // tm),
               [pl.BlockSpec((tm, D), lambda s, i: (i, 0)), pl.BlockSpec((1, FSH, D), lambda s, i: (s, 0, 0)),
                a_spec, a_spec],
               [a_spec, a_spec], [SDS(shape, BF16), SDS(shape, BF16)])(dy_b, wd, gt, up)


def _gw_down(act, dy_b):
    s_len = dy_b.shape[0]
    tn = 512

    def kern(a_ref, g_ref, o_ref):
        o_ref[0] = _bdot(a_ref[0], g_ref[...], "tn").astype(BF16)

    return _pc(kern, "gw_down", (N_CHIPS, D // tn),
               [pl.BlockSpec((1, s_len, FSH), lambda s, j: (s, 0, 0)), pl.BlockSpec((s_len, tn), lambda s, j: (0, j))],
               pl.BlockSpec((1, FSH, tn), lambda s, j: (s, 0, j)), SDS((N_CHIPS, FSH, D), BF16))(act, dy_b)


def _gw_gate_up(h2, d_gt, d_up):
    s_len = h2.shape[0]
    tk = 256

    def kern(h_ref, dg_ref, du_ref, og_ref, ou_ref):
        h = h_ref[...]
        og_ref[0] = _bdot(h, dg_ref[0], "tn").astype(BF16)
        ou_ref[0] = _bdot(h, du_ref[0], "tn").astype(BF16)

    d_spec = pl.BlockSpec((1, s_len, FSH), lambda s, j: (s, 0, 0))
    o_spec = pl.BlockSpec((1, tk, FSH), lambda s, j: (s, j, 0))
    shape = (N_CHIPS, D, FSH)
    return _pc(kern, "gw_gate_up", (N_CHIPS, D // tk), [pl.BlockSpec((s_len, tk), lambda s, j: (0, j)), d_spec, d_spec],
               [o_spec, o_spec], [SDS(shape, BF16), SDS(shape, BF16)])(h2, d_gt, d_up)


def _ffn_dh2(d_gt, d_up, wg, wu, x1, dy, gain):
    s_len = x1.shape[0]
    tm = min(256, s_len)

    def kern(dg_ref, du_ref, wg_ref, wu_ref, x_ref, dy_ref, g_ref, dx_ref, dxb_ref, dgain_ref):
        _zero_first([dgain_ref])
        dh2 = jnp.zeros((tm, D), F32)
        for s in range(N_CHIPS):
            dh2 = dh2 + _bdot(dg_ref[s], wg_ref[s], "nt") + _bdot(du_ref[s], wu_ref[s], "nt")
        _, vjp = jax.vjp(_f_rms, x_ref[...], g_ref[...])
        dx, dgain = vjp(dh2)
        dx1 = dx + dy_ref[...]
        dx_ref[...] = dx1
        dxb_ref[...] = dx1.astype(BF16)
        dgain_ref[...] += dgain

    row = pl.BlockSpec((tm, D), lambda i: (i, 0))
    d_spec = pl.BlockSpec((N_CHIPS, tm, FSH), lambda i: (0, i, 0))
    w_spec = pl.BlockSpec((N_CHIPS, D, FSH), lambda i: (0, 0, 0))
    vec = pl.BlockSpec((1, D), lambda i: (0, 0))
    return _pc(kern, "ffn_dh2", (s_len // tm,), [d_spec, d_spec, w_spec, w_spec, row, row, vec],
               [row, row, vec], [SDS((s_len, D), F32), SDS((s_len, D), BF16), SDS((1, D), F32)],
               )(d_gt, d_up, wg, wu, x1, dy, gain)


def _merge_bwd(dx1_b, w_out, pa, pb, proj):
    s_len = dx1_b.shape[0]
    tm = min(256, s_len)

    def kern(dx_ref, w_ref, pa_ref, pb_ref, g_ref, dpa_ref, dpb_ref, dg_ref):
        dm = _bdot(dx_ref[...], w_ref[...], "nt")
        gates = g_ref[...]
        _, vjp = jax.vjp(_f_merge, pa_ref[...], pb_ref[...], gates[:, :D], gates[:, D:])
        dpa, dpb, dga, dgb = vjp(dm)
        dpa_ref[...] = dpa.astype(BF16)
        dpb_ref[...] = dpb.astype(BF16)
        dg_ref[:, :D] = dga.astype(BF16)
        dg_ref[:, D:] = dgb.astype(BF16)

    row = pl.BlockSpec((tm, D), lambda i: (i, 0))
    return _pc(kern, "merge_bwd", (s_len // tm,),
               [row, pl.BlockSpec((D, D), lambda i: (0, 0)), row, row,
                pl.BlockSpec((tm, 2 * D), lambda i: (i, C_GATE // (2 * D)))],
               [row, row, pl.BlockSpec((tm, 2 * D), lambda i: (i, 0))],
               [SDS((s_len, D), BF16), SDS((s_len, D), BF16), SDS((s_len, 2 * D), BF16)],
               )(dx1_b, w_out, pa, pb, proj)


def _d_branch(d_pa, d_pb, wa, wb):
    s_len = d_pa.shape[0]
    tm = min(512, s_len)

    def kern(da_ref, db_ref, wa_ref, wb_ref, oa_ref, ob_ref):
        acc_a = jnp.zeros((tm, DNW), F32)
        acc_b = jnp.zeros((tm, SWAW), F32)
        for s in range(N_CHIPS):
            acc_a = acc_a + _bdot(da_ref[:, s * CSH:(s + 1) * CSH], wa_ref[s], "nt")
            acc_b = acc_b + _bdot(db_ref[:, s * CSH:(s + 1) * CSH], wb_ref[s], "nt")
        oa_ref[...] = acc_a
        ob_ref[...] = acc_b

    row = pl.BlockSpec((tm, D), lambda i: (i, 0))
    w_spec = pl.BlockSpec((N_CHIPS, DNW, CSH), lambda i: (0, 0, 0))
    out = pl.BlockSpec((tm, DNW), lambda i: (i, 0))
    return _pc(kern, "d_branch", (s_len // tm,), [row, row, w_spec, w_spec], [out, out],
               [SDS((s_len, DNW), F32), SDS((s_len, SWAW), F32)])(d_pa, d_pb, wa, wb)


def _gw_branch(y_dn, y_swa, d_pa, d_pb):
    s_len = y_dn.shape[0]

    def kern(ya_ref, yb_ref, da_ref, db_ref, oa_ref, ob_ref):
        oa_ref[0] = _bdot(ya_ref[...], da_ref[...], "tn").astype(BF16)
        ob_ref[0] = _bdot(yb_ref[...], db_ref[...], "tn").astype(BF16)

    y_spec = pl.BlockSpec((s_len, DNW), lambda s: (0, 0))
    d_spec = pl.BlockSpec((s_len, CSH), lambda s: (0, s))
    o_spec = pl.BlockSpec((1, DNW, CSH), lambda s: (s, 0, 0))
    shape = (N_CHIPS, DNW, CSH)
    return _pc(kern, "gw_branch", (N_CHIPS,), [y_spec, y_spec, d_spec, d_spec], [o_spec, o_spec],
               [SDS(shape, BF16), SDS(shape, BF16)])(y_dn, y_swa, d_pa, d_pb)


def _dh_rms(d_proj, w_in_p, x, dx1, gain):
    s_len = x.shape[0]
    tm = min(256, s_len)

    def kern(dp_ref, w_ref, x_ref, r_ref, g_ref, gx_ref, dgain_ref):
        _zero_first([dgain_ref])
        dh = _bdot(dp_ref[...], w_ref[...], "nt")
        _, vjp = jax.vjp(_f_rms, x_ref[...], g_ref[...])
        dx, dgain = vjp(dh)
        gx_ref[...] = dx + r_ref[...]
        dgain_ref[...] += dgain

    row = pl.BlockSpec((tm, D), lambda i: (i, 0))
    vec = pl.BlockSpec((1, D), lambda i: (0, 0))
    return _pc(kern, "dh_rms", (s_len // tm,),
               [pl.BlockSpec((tm, PW), lambda i: (i, 0)), pl.BlockSpec((D, PW), lambda i: (0, 0)), row, row, vec],
               [row, vec], [SDS((s_len, D), F32), SDS((1, D), F32)])(d_proj, w_in_p, x, dx1, gain)


HALO = 8


def _conv_taps(cur_ref, prev_ref, halo, first):
    tm = cur_ref.shape[0]
    halo[0:HALO, :] = jnp.where(first, 0.0, prev_ref[...])
    halo[HALO:, :] = cur_ref[...]
    return [halo[HALO - n:HALO - n + tm, :] for n in range(CONV - 1, 0, -1)] + [cur_ref[...]]


def _dn_pre_specs(s_len, tm, blk):
    cur = pl.BlockSpec((tm, QKVW), lambda i: (blk(i), 0))
    prev = pl.BlockSpec((HALO, QKVW), lambda i: (jnp.maximum(blk(i) * (tm // HALO) - 1, 0), 0))
    ba = pl.BlockSpec((tm, 128), lambda i: (blk(i), C_BA // 128))
    row = pl.BlockSpec((tm, DNW), lambda i: (blk(i), 0))
    full = [pl.BlockSpec((CONV, QKVW), lambda i: (0, 0)), pl.BlockSpec((1, DN_H), lambda i: (0, 0)),
            pl.BlockSpec((1, DN_H), lambda i: (0, 0))]
    return cur, prev, ba, row, full


def _dn_pre_fwd(proj, conv_w, alog, dtb):
    s_len = proj.shape[0]
    tm = min(128, s_len)
    cur, prev, ba, row, full = _dn_pre_specs(s_len, tm, lambda i: i)

    def kern(cur_ref, prev_ref, ba_ref, cw_ref, al_ref, dt_ref, q_ref, k_ref, v_ref, bb_ref, gb_ref, halo):
        xs = _conv_taps(cur_ref, prev_ref, halo, pl.program_id(0) == 0)
        outs = _f_dn_pre(*xs, ba_ref[...], cw_ref[...], al_ref[...], dt_ref[...])
        for ref, val in zip((q_ref, k_ref, v_ref, bb_ref, gb_ref), outs, strict=True):
            ref[...] = val

    return _pc(kern, "dn_pre_fwd", (s_len // tm,), [cur, prev, ba] + full, [row] * 5, [SDS((s_len, DNW), F32)] * 5,
               scratch=[pltpu.VMEM((tm + HALO, QKVW), F32)])(proj, proj, proj, conv_w, alog, dtb)


def _dn_pre_bwd(proj, conv_w, alog, dtb, cots):
    s_len = proj.shape[0]
    tm = min(128, s_len)
    nb = s_len // tm
    cur, prev, ba, row, full = _dn_pre_specs(s_len, tm, lambda i: nb - 1 - i)

    def kern(cur_ref, prev_ref, ba_ref, cw_ref, al_ref, dt_ref, dq_ref, dk_ref, dv_ref, dbb_ref, dgb_ref,
             dqkv_ref, dba_ref, dcw_ref, dal_ref, ddt_ref, halo, *tails):
        i = pl.program_id(0)
        _zero_first([dcw_ref, dal_ref, ddt_ref])

        @pl.when(i == 0)
        def _():
            for t in tails:
                t[tm:, :] = jnp.zeros((HALO, QKVW), F32)

        xs = _conv_taps(cur_ref, prev_ref, halo, i == nb - 1)
        _, vjp = jax.vjp(_f_dn_pre, *xs, ba_ref[...], cw_ref[...], al_ref[...], dt_ref[...])
        *dxs, dba, dcw, dal, ddt = vjp((dq_ref[...], dk_ref[...], dv_ref[...], dbb_ref[...], dgb_ref[...]))
        total = dxs[CONV - 1]
        for j, t in enumerate(tails):
            n = CONV - 1 - j
            t[0:tm, :] = dxs[j]
            total = total + t[n:n + tm, :]
            t[tm:, :] = dxs[j][0:HALO, :]
        dqkv_ref[...] = total.astype(BF16)
        dba_ref[...] = dba.astype(BF16)
        dcw_ref[...] += dcw
        dal_ref[...] += dal
        ddt_ref[...] += ddt

    return _pc(kern, "dn_pre_bwd", (nb,), [cur, prev, ba] + full + [row] * 5,
               [cur, pl.BlockSpec((tm, 128), lambda i: (nb - 1 - i, 0))] + full,
               [SDS((s_len, QKVW), BF16), SDS((s_len, 128), BF16), SDS((CONV, QKVW), F32), SDS((1, DN_H), F32),
                SDS((1, DN_H), F32)],
               scratch=[pltpu.VMEM((tm + HALO, QKVW), F32)] * CONV)(proj, proj, proj, conv_w, alog, dtb, *cots)


def _pad_w_in(w_in):
    pieces = [w_in[:, o0:o0 + w] for o0, w, _ in sorted(_ORIG_PIECES, key=lambda t: t[2])]
    pieces.append(jnp.zeros((w_in.shape[0], PW - D_IN), w_in.dtype))
    return jnp.concatenate(pieces, axis=1)


def _unpad_w_in(g):
    return jnp.concatenate([g[:, p0:p0 + w] for _, w, p0 in _ORIG_PIECES], axis=1)


def _local_step(x, target, wts):
    s_len = x.shape[0]
    tm = min(256, s_len)
    tmh = min(128, s_len)
    w_in_p = wts["w_in_p"]
    attn_gain = wts["attn_norm"]
    ffn_gain = wts["ffn_norm"]
    conv_w = wts["dn_conv"]
    alog, dtb, out_gain = wts["dn_a_log"], wts["dn_dt_bias"], wts["dn_out_norm"]
    qg, kg = wts["swa_q_norm"], wts["swa_k_norm"]
    sinks = wts["swa_sinks"].reshape(SWA_KV, 1, SWA_G)

    (h,) = _rows(lambda r, f: ([_f_rms(r[0], f[0])], []), "rms1_fwd", s_len, tm, [_whole(x)], [attn_gain],
                 [(D, BF16)])
    proj = _mm(h, w_in_p, "nn", F32, 512, 1024, "mm_proj")
    q_dn, k_dn, v_dn, bb, gb = _dn_pre_fwd(proj, conv_w, alog, dtb)
    o_dn, s_all = _dn_chunks_fwd(q_dn, k_dn, v_dn, gb, bb)
    post_ins = [_whole(o_dn), (proj, DNW, C_Z // DNW)]
    (y_dn,) = _rows(lambda r, f: ([_f_dn_post(r[0], r[1], f[0])], []), "dn_post_fwd", s_len, tm, post_ins,
                    [out_gain], [(DNW, BF16)])

    sq = proj[:, C_SQ:C_SQ + SWAW].reshape(s_len, SWA_H, SWA_D).transpose(1, 0, 2)
    sk = proj[:, C_SK:C_SK + SWAKW].reshape(s_len, SWA_KV, SWA_D).transpose(1, 0, 2)
    sv = proj[:, C_SV:C_SV + SWAKW].reshape(s_len, SWA_KV, SWA_D).transpose(1, 0, 2)
    bias = _bias_expand(wts["rel_bias"].T).reshape(SWA_H, BLK, 2 * BLK)
    o_swa = _swa_fwd(sq, sk, sv, bias, qg, kg, sinks)
    y_swa = o_swa.transpose(1, 0, 2).reshape(s_len, SWAW).astype(BF16)

    p_a, p_b, merged = _branch_merge(y_dn, y_swa, wts["wa"], wts["wb"], proj)
    x1, h2 = _out_proj(merged, wts["w_out"], x, ffn_gain)
    gt, up, act = _ffn_up(h2, wts["wg"], wts["wu"])
    dy, dy_b, loss = _ffn_down_loss(act, wts["wd"], x1, target)

    grads = {}
    d_gt, d_up = _ffn_dact(dy_b, wts["wd"], gt, up)
    grads["w_down"] = _gw_down(act, dy_b)
    grads["w_gate"], grads["w_up"] = _gw_gate_up(h2, d_gt, d_up)
    dx1, dx1_b, grads["ffn_norm"] = _ffn_dh2(d_gt, d_up, wts["wg"], wts["wu"], x1, dy, ffn_gain)
    grads["w_out"] = _mm(merged, dx1_b, "tn", BF16, 512, 512, "gw_out")
    d_pa, d_pb, d_gr = _merge_bwd(dx1_b, wts["w_out"], p_a, p_b, proj)
    d_ydn, d_yswa = _d_branch(d_pa, d_pb, wts["wa"], wts["wb"])
    grads["w_branch_dn"], grads["w_branch_swa"] = _gw_branch(y_dn, y_swa, d_pa, d_pb)

    d_oswa = d_yswa.reshape(s_len, SWA_H, SWA_D).transpose(1, 0, 2)
    d_sq, d_sk, d_sv, d_bias, grads["swa_q_norm"], grads["swa_k_norm"], d_sinks = _swa_bwd(
        sq, sk, sv, bias, qg, kg, sinks, d_oswa)
    grads["swa_sinks"] = d_sinks.reshape(1, SWA_H)
    grads["rel_bias"] = _bias_reduce(d_bias.reshape(SWA_H, BLK * 2 * BLK)).T
    d_sq = d_sq.transpose(1, 0, 2).reshape(s_len, SWAW).astype(BF16)
    d_sk = d_sk.transpose(1, 0, 2).reshape(s_len, SWAKW).astype(BF16)
    d_sv = d_sv.transpose(1, 0, 2).reshape(s_len, SWAKW).astype(BF16)

    def post_bwd(r, f):
        _, vjp = jax.vjp(_f_dn_post, r[0], r[1], f[0])
        d_o, d_z, d_gain = vjp(r[2])
        return [d_o, d_z], [d_gain]

    d_o, d_z, grads["dn_out_norm"] = _rows(post_bwd, "dn_post_bwd", s_len, tm, post_ins + [_whole(d_ydn)], [out_gain],
                                           [(DNW, F32), (DNW, BF16)], [(1, DH)])
    d_q, d_k, d_v, d_gb, d_bb = _dn_chunks_bwd(q_dn, k_dn, v_dn, gb, bb, s_all, d_o)

    d_qkv, d_ba, grads["dn_conv"], grads["dn_a_log"], grads["dn_dt_bias"] = _dn_pre_bwd(
        proj, conv_w, alog, dtb, (d_q, d_k, d_v, d_bb, d_gb))

    d_proj = jnp.concatenate(
        [d_qkv, d_z, d_gr, d_sq, d_sk, d_sv, d_ba, jnp.zeros((s_len, PW - C_BA - 128), BF16)], axis=1)
    grads["w_in_p"] = _mm(h, d_proj, "tn", BF16, 512, 1024, "gw_in")
    grad_x, grads["attn_norm"] = _dh_rms(d_proj, w_in_p, x, dx1, attn_gain)
    return loss, grad_x, grads


_HBM = pl.BlockSpec(memory_space=pl.ANY)


def _place():
    return lax.axis_index("x"), lax.axis_index("y"), lax.axis_index("c")


def _other_chips(x, y):
    return [(1 - x, y), (x, 1 - y), (1 - x, 1 - y)]


def _rcopy(src, dst, send_sems, recv_sems, k, to):
    return pltpu.make_async_remote_copy(src_ref=src, dst_ref=dst, send_sem=send_sems.at[k], recv_sem=recv_sems.at[k],
                                        device_id=to, device_id_type=MESH)


def _comm_call(body, name, ins, out_shapes, n_remote, landing=0):
    first = len(ins) - landing
    return pl.pallas_call(
        body, name=name, in_specs=[_HBM] * len(ins), out_specs=[_HBM] * len(out_shapes), out_shape=out_shapes,
        scratch_shapes=[pltpu.SemaphoreType.DMA((n_remote,)), pltpu.SemaphoreType.DMA((n_remote,))],
        input_output_aliases={first + i: i for i in range(landing)},
        compiler_params=_cparams(has_side_effects=True),
    )(*ins)


def _own_slot(blocks, chip):
    return [lax.dynamic_update_slice(jnp.zeros((N_CHIPS,) + b.shape, b.dtype), b[None], (chip, 0, 0)) for b in blocks]


def _gather_weights(ws, chip):
    n = len(ws)
    halves = [w.shape[0] // 2 for w in ws]

    def body(*refs):
        w_refs, o_refs = refs[:n], refs[2 * n:3 * n]
        send_sems, recv_sems = refs[3 * n:]
        x, y, c = _place()
        s = 2 * x + y
        sib = (x, y, 1 - c)
        chips = _other_chips(x, y)

        def rows(i, half):
            return pl.ds(half * halves[i], halves[i])

        first = []
        for j, (cx, cy) in enumerate(chips):
            for i in range(n):
                cp = _rcopy(w_refs[i].at[rows(i, c), :], o_refs[i].at[s, rows(i, c), :], send_sems, recv_sems,
                            j * n + i, (cx, cy, c))
                cp.start()
                first.append(cp)
        passed = []
        for j, (cx, cy) in enumerate(chips):
            sj = 2 * cx + cy
            for i in range(n):
                blk = o_refs[i].at[sj, rows(i, c), :]
                _rcopy(blk, blk, send_sems, recv_sems, j * n + i, (cx, cy, c)).wait_recv()
                cp = _rcopy(blk, blk, send_sems, recv_sems, (3 + j) * n + i, sib)
                cp.start()
                passed.append(cp)
        for j, (cx, cy) in enumerate(chips):
            sj = 2 * cx + cy
            for i in range(n):
                blk = o_refs[i].at[sj, rows(i, 1 - c), :]
                _rcopy(blk, blk, send_sems, recv_sems, (3 + j) * n + i, sib).wait_recv()
        for cp in first + passed:
            cp.wait_send()

    return _comm_call(body, "gather_weights", list(ws) + _own_slot(ws, chip),
                      [SDS((N_CHIPS,) + w.shape, w.dtype) for w in ws], 6 * n, landing=n)


def _swap_halves(gs):
    n = len(gs)
    halves = [g.shape[1] // 2 for g in gs]

    def body(*refs):
        g_refs, o_refs = refs[:n], refs[n:2 * n]
        send_sems, recv_sems = refs[2 * n:]
        x, y, c = _place()
        cps = [_rcopy(g_refs[i].at[:, pl.ds((1 - c) * halves[i], halves[i]), :], o_refs[i], send_sems, recv_sems, i,
                      (x, y, 1 - c)) for i in range(n)]
        for cp in cps:
            cp.start()
        for cp in cps:
            cp.wait()

    return _comm_call(body, "swap_halves", gs, [SDS((N_CHIPS, h, g.shape[2]), g.dtype) for g, h in zip(gs, halves)], n)


def _chip_exchange(ps, chip):
    n = len(ps)

    def body(*refs):
        p_refs, o_refs = refs[:n], refs[2 * n:3 * n]
        send_sems, recv_sems = refs[3 * n:]
        x, y, c = _place()
        s = 2 * x + y
        chips = _other_chips(x, y)
        sent = []
        for j, (cx, cy) in enumerate(chips):
            for i in range(n):
                cp = _rcopy(p_refs[i].at[2 * cx + cy], o_refs[i].at[s], send_sems, recv_sems, j * n + i, (cx, cy, c))
                cp.start()
                sent.append(cp)
        for j, (cx, cy) in enumerate(chips):
            sj = 2 * cx + cy
            for i in range(n):
                _rcopy(p_refs[i].at[sj], o_refs[i].at[sj], send_sems, recv_sems, j * n + i, (cx, cy, c)).wait_recv()
        for cp in sent:
            cp.wait_send()

    own = [lax.dynamic_index_in_dim(p, chip, axis=0, keepdims=False) for p in ps]
    return _comm_call(body, "chip_exchange", list(ps) + _own_slot(own, chip), [SDS(p.shape, p.dtype) for p in ps],
                      3 * n, landing=n)


def _swap_reduced(rs):
    n = len(rs)

    def body(*refs):
        r_refs, o_refs = refs[:n], refs[n:2 * n]
        send_sems, recv_sems = refs[2 * n:]
        x, y, c = _place()
        cps = [_rcopy(r_refs[i], o_refs[i], send_sems, recv_sems, i, (x, y, 1 - c)) for i in range(n)]
        for cp in cps:
            cp.start()
        for cp in cps:
            cp.wait()

    return _comm_call(body, "swap_reduced", rs, [SDS(r.shape, r.dtype) for r in rs], n)


def _all_sum_small(vec, name):
    n_dev = 8
    flips = [(bx, by, bc) for bx in (0, 1) for by in (0, 1) for bc in (0, 1)][1:]

    def body(v_ref, out_ref, gath, send_sems, recv_sems):
        x, y, c = _place()
        me = 4 * x + 2 * y + c
        gath[me] = v_ref[...]
        sent = []
        for k, (bx, by, bc) in enumerate(flips):
            peer = (x ^ bx, y ^ by, c ^ bc)
            cp = _rcopy(v_ref, gath.at[me], send_sems, recv_sems, k, peer)
            cp.start()
            sent.append(cp)
        for k, (bx, by, bc) in enumerate(flips):
            peer = (x ^ bx, y ^ by, c ^ bc)
            _rcopy(v_ref, gath.at[4 * peer[0] + 2 * peer[1] + peer[2]], send_sems, recv_sems, k, peer).wait_recv()
        for cp in sent:
            cp.wait_send()
        acc = gath[0]
        for d in range(1, n_dev):
            acc = acc + gath[d]
        out_ref[...] = acc

    vm = pl.BlockSpec(memory_space=pltpu.VMEM)
    return pl.pallas_call(
        body, name=name, in_specs=[vm], out_specs=vm, out_shape=SDS(vec.shape, F32),
        scratch_shapes=[pltpu.VMEM((n_dev,) + vec.shape, F32), pltpu.SemaphoreType.DMA((7,)),
                        pltpu.SemaphoreType.DMA((7,))],
        compiler_params=_cparams(has_side_effects=True),
    )(vec)


def _pack_small(vals, extra=None):
    parts = [vals[n].reshape(-1).astype(F32) for n, _ in _SMALL]
    parts.append(jnp.zeros((1,), F32) if extra is None else extra.reshape(1).astype(F32))
    flat = jnp.concatenate(parts)
    flat = jnp.concatenate([flat, jnp.zeros((_SMALL_ROWS * 128 - flat.shape[0],), F32)])
    return flat.reshape(_SMALL_ROWS, 128)


def _unpack_small(packed, shapes):
    flat = packed.reshape(-1)
    return {n: flat[_SMALL_OFF[n][0]:_SMALL_OFF[n][0] + _SMALL_OFF[n][1]].reshape(shapes[n]) for n, _ in _SMALL}


def _pair_sum(gs, gots, core):
    n = len(gs)

    def kern(c_ref, *refs):
        for i in range(n):
            refs[2 * n + i][...] = (refs[i][...].astype(F32) + refs[n + i][...].astype(F32)).astype(BF16)

    in_specs = [pl.BlockSpec((1, t.shape[1], t.shape[2]), lambda s, c_ref: (s, c_ref[0], 0)) for t in gots]
    in_specs += [pl.BlockSpec((1, t.shape[1], t.shape[2]), lambda s, c_ref: (s, 0, 0)) for t in gots]
    out_specs = [pl.BlockSpec((1, t.shape[1], t.shape[2]), lambda s, c_ref: (s, 0, 0)) for t in gots]
    return pl.pallas_call(
        kern, name="pair_sum",
        grid_spec=pltpu.PrefetchScalarGridSpec(num_scalar_prefetch=1, grid=(N_CHIPS,), in_specs=in_specs,
                                               out_specs=out_specs),
        out_shape=[SDS(t.shape, BF16) for t in gots],
        compiler_params=_cparams(dimension_semantics=("arbitrary",)),
    )(core.reshape(1).astype(jnp.int32), *gs, *gots)


def _chip_sum(qs):
    n = len(qs)

    def kern(*refs):
        for i in range(n):
            acc = refs[i][0].astype(F32)
            for s in range(1, N_CHIPS):
                acc = acc + refs[i][s].astype(F32)
            refs[n + i][...] = acc

    in_specs = [pl.BlockSpec((N_CHIPS, q.shape[1] // 2, q.shape[2]), lambda j: (0, j, 0)) for q in qs]
    out_specs = [pl.BlockSpec((q.shape[1] // 2, q.shape[2]), lambda j: (j, 0)) for q in qs]
    return _pc(kern, "chip_sum", (2,), in_specs, out_specs, [SDS(q.shape[1:], F32) for q in qs])(*qs)


def _adamw(w, g, m, v, name):
    rows, cols = w.shape
    tr = rows
    for cand in (256, 128, 64, 32, 16, 8):
        if rows % cand == 0 and rows > cand:
            tr = cand
            break

    def kern(w_ref, g_ref, m_ref, v_ref, d_ref, nm_ref, nv_ref):
        g_ = g_ref[...]
        m_ = ADAM_B1 * m_ref[...] + (1.0 - ADAM_B1) * g_
        v_ = ADAM_B2 * v_ref[...] + (1.0 - ADAM_B2) * jnp.square(g_)
        m_hat = m_ / (1.0 - ADAM_B1 ** ADAM_STEP)
        v_hat = v_ / (1.0 - ADAM_B2 ** ADAM_STEP)
        d_ref[...] = -ADAM_LR * (m_hat / (jnp.sqrt(v_hat) + ADAM_EPS) + ADAM_WD * w_ref[...])
        nm_ref[...] = m_
        nv_ref[...] = v_

    spec = pl.BlockSpec((tr, cols), lambda i: (i, 0))
    return _pc(kern, name, (rows // tr,), [spec] * 4, [spec] * 3, [SDS(w.shape, F32)] * 3)(w, g, m, v)


_WEIGHT_NAMES = ("attn_norm", "w_in", "dn_conv", "dn_a_log", "dn_dt_bias", "dn_out_norm", "swa_q_norm", "swa_k_norm",
                 "swa_sinks", "rel_bias", "w_branch_dn", "w_branch_swa", "w_out", "ffn_norm", "w_gate", "w_up",
                 "w_down")
_CONV_SH = QKVW // N_CHIPS


def kernel(x, attn_norm, w_in, dn_conv, dn_a_log, dn_dt_bias, dn_out_norm, swa_q_norm, swa_k_norm, swa_sinks, rel_bias, w_branch_dn, w_branch_swa, w_out, ffn_norm, w_gate, w_up, w_down, loss_target, m_attn_norm, m_w_in, m_dn_conv, m_dn_a_log, m_dn_dt_bias, m_dn_out_norm, m_swa_q_norm, m_swa_k_norm, m_swa_sinks, m_rel_bias, m_w_branch_dn, m_w_branch_swa, m_w_out, m_ffn_norm, m_w_gate, m_w_up, m_w_down, v_attn_norm, v_w_in, v_dn_conv, v_dn_a_log, v_dn_dt_bias, v_dn_out_norm, v_swa_q_norm, v_swa_k_norm, v_swa_sinks, v_rel_bias, v_w_branch_dn, v_w_branch_swa, v_w_out, v_ffn_norm, v_w_gate, v_w_up, v_w_down):
    w = dict(attn_norm=attn_norm, w_in=w_in, dn_conv=dn_conv, dn_a_log=dn_a_log, dn_dt_bias=dn_dt_bias,
             dn_out_norm=dn_out_norm, swa_q_norm=swa_q_norm, swa_k_norm=swa_k_norm, swa_sinks=swa_sinks,
             rel_bias=rel_bias, w_branch_dn=w_branch_dn, w_branch_swa=w_branch_swa, w_out=w_out, ffn_norm=ffn_norm,
             w_gate=w_gate, w_up=w_up, w_down=w_down)
    m = dict(attn_norm=m_attn_norm, w_in=m_w_in, dn_conv=m_dn_conv, dn_a_log=m_dn_a_log, dn_dt_bias=m_dn_dt_bias,
             dn_out_norm=m_dn_out_norm, swa_q_norm=m_swa_q_norm, swa_k_norm=m_swa_k_norm, swa_sinks=m_swa_sinks,
             rel_bias=m_rel_bias, w_branch_dn=m_w_branch_dn, w_branch_swa=m_w_branch_swa, w_out=m_w_out,
             ffn_norm=m_ffn_norm, w_gate=m_w_gate, w_up=m_w_up, w_down=m_w_down)
    v = dict(attn_norm=v_attn_norm, w_in=v_w_in, dn_conv=v_dn_conv, dn_a_log=v_dn_a_log, dn_dt_bias=v_dn_dt_bias,
             dn_out_norm=v_dn_out_norm, swa_q_norm=v_swa_q_norm, swa_k_norm=v_swa_k_norm, swa_sinks=v_swa_sinks,
             rel_bias=v_rel_bias, w_branch_dn=v_w_branch_dn, w_branch_swa=v_w_branch_swa, w_out=v_w_out,
             ffn_norm=v_ffn_norm, w_gate=v_w_gate, w_up=v_w_up, w_down=v_w_down)
    shapes = {n: w[n].shape for n in _WEIGHT_NAMES}

    def two_d(a):
        return a.reshape(a.shape[-2], a.shape[-1]) if a.ndim == 3 else a

    core = lax.axis_index("c")
    chip = 2 * lax.axis_index("x") + lax.axis_index("y")
    small_shapes = {n: two_d(w[n]).shape for n, _ in _SMALL}
    small_shapes["dn_conv"] = (CONV, QKVW)

    conv_loc = two_d(w["dn_conv"])
    conv_part = lax.dynamic_update_slice(jnp.zeros((CONV, QKVW), F32), jnp.where(core == 0, conv_loc, 0.0),
                                         (0, chip * _CONV_SH))
    conv_full = _all_sum_small(conv_part.reshape(CONV * QKVW // 128, 128), "gather_conv").reshape(CONV, QKVW)

    local_blocks = {n: two_d(w[n]) for n in _BIG_NAMES}
    gathered = dict(zip(_BIG_NAMES, _gather_weights([local_blocks[n].astype(BF16) for n in _BIG_NAMES], chip)))
    w_in_full = gathered["w_in"].transpose(1, 0, 2).reshape(D, D_IN)
    wts = dict(w_in_p=_pad_w_in(w_in_full), wa=gathered["w_branch_dn"], wb=gathered["w_branch_swa"],
               w_out=gathered["w_out"].reshape(D, D), wg=gathered["w_gate"], wu=gathered["w_up"],
               wd=gathered["w_down"], dn_conv=conv_full)
    for n, _ in _SMALL[:-1]:
        wts[n] = two_d(w[n])

    loss_sum, grad_x, grads = _local_step(x[0], loss_target[0], wts)

    g_in = _unpad_w_in(grads["w_in_p"]).reshape(D, N_CHIPS, D_IN // N_CHIPS).transpose(1, 0, 2)
    gs = [g_in, grads["w_branch_dn"], grads["w_branch_swa"], grads["w_out"].reshape(N_CHIPS, CSH, D),
          grads["w_gate"], grads["w_up"], grads["w_down"]]
    gots = _swap_halves(gs)
    parts = _pair_sum(gs, gots, core)
    reduced = _chip_sum(_chip_exchange(parts, chip))
    theirs = _swap_reduced(reduced)
    g_blocks = {n: jnp.concatenate([jnp.where(core == 0, mine, other), jnp.where(core == 0, other, mine)], axis=0)
                for n, mine, other in zip(_BIG_NAMES, reduced, theirs)}

    small_sum = _all_sum_small(_pack_small(grads, loss_sum), "all_sum_small")
    loss = small_sum.reshape(-1)[_LOSS_OFF]
    g_small = _unpack_small(small_sum, small_shapes)

    g_out, d_out, m_out, v_out = {}, {}, {}, {}
    for n in _BIG_NAMES:
        g_out[n] = g_blocks[n].reshape(shapes[n])
        d_, m_, v_ = _adamw(local_blocks[n], g_blocks[n], two_d(m[n]), two_d(v[n]), "adamw_" + n)
        d_out[n], m_out[n], v_out[n] = d_.reshape(shapes[n]), m_.reshape(shapes[n]), v_.reshape(shapes[n])
    g_conv = lax.dynamic_slice(g_small["dn_conv"], (0, chip * _CONV_SH), (CONV, _CONV_SH))
    g_out["dn_conv"] = g_conv.reshape(shapes["dn_conv"])
    d_, m_, v_ = _adamw(conv_loc, g_conv, two_d(m["dn_conv"]), two_d(v["dn_conv"]), "adamw_dn_conv")
    d_out["dn_conv"], m_out["dn_conv"], v_out["dn_conv"] = (t.reshape(shapes["dn_conv"]) for t in (d_, m_, v_))

    def packed(src):
        vals = {n: src[n] for n, _ in _SMALL[:-1]}
        vals["dn_conv"] = jnp.zeros((CONV * QKVW,), F32)
        return _pack_small(vals)

    d_s, m_s, v_s = _adamw(packed(w), small_sum, packed(m), packed(v), "adamw_small")
    d_small, m_small, v_small = (_unpack_small(t, small_shapes) for t in (d_s, m_s, v_s))
    for n, _ in _SMALL[:-1]:
        g_out[n] = g_small[n].reshape(shapes[n])
        d_out[n], m_out[n], v_out[n] = (t[n].reshape(shapes[n]) for t in (d_small, m_small, v_small))

    return (loss, grad_x[None], *[g_out[n] for n in _WEIGHT_NAMES], *[d_out[n] for n in _WEIGHT_NAMES],
            *[m_out[n] for n in _WEIGHT_NAMES], *[v_out[n] for n in _WEIGHT_NAMES])
```

```python
import functools
import math

import numpy as np
import jax
import jax.numpy as jnp
from jax import lax
from jax.experimental import pallas as pl
from jax.experimental.pallas import tpu as pltpu

F32 = jnp.float32
BF16 = jnp.bfloat16
SDS = jax.ShapeDtypeStruct

D = 1024
DN_H = 4
DH = 128
DNW = DN_H * DH
QKVW = 3 * DNW
CONV = 4
CHUNK = 64
SWA_H = 8
SWA_KV = 2
SWA_G = SWA_H // SWA_KV
SWA_D = 64
SWAW = SWA_H * SWA_D
SWAKW = SWA_KV * SWA_D
BLK = 128
NBUCKET = 32
MAXDIST = 128
DFF = 2816
D_IN = QKVW + DNW + 2 * DN_H + SWAW + 2 * SWAKW + 2 * D
EPS = 1e-6
NEG = -1e30

ADAM_LR = 0.001
ADAM_B1 = 0.9
ADAM_B2 = 0.999
ADAM_EPS = 1e-08
ADAM_WD = 0.01
ADAM_STEP = 10

C_QKV, C_Z, C_GATE, C_SQ, C_SK, C_SV, C_BA = 0, 1536, 2048, 4096, 4608, 4736, 4864
PW = 5120
_ORIG_PIECES = (
    (0, QKVW, C_QKV),
    (QKVW, DNW, C_Z),
    (QKVW + DNW, 2 * DN_H, C_BA),
    (QKVW + DNW + 2 * DN_H, SWAW, C_SQ),
    (QKVW + DNW + 2 * DN_H + SWAW, SWAKW, C_SK),
    (QKVW + DNW + 2 * DN_H + SWAW + SWAKW, SWAKW, C_SV),
    (QKVW + DNW + 2 * DN_H + SWAW + 2 * SWAKW, 2 * D, C_GATE),
)

N_CHIPS = 4
FSH = DFF // N_CHIPS
CSH = D // N_CHIPS
VMEM_LIMIT = 48 * 1024 * 1024
MESH = pl.DeviceIdType.MESH

_BIG = (
    ("w_in", D, D_IN // N_CHIPS),
    ("w_branch_dn", DNW, CSH),
    ("w_branch_swa", SWAW, CSH),
    ("w_out", CSH, D),
    ("w_gate", D, FSH),
    ("w_up", D, FSH),
    ("w_down", FSH, D),
)
_BIG_NAMES = tuple(n for n, _, _ in _BIG)

_SMALL = (
    ("attn_norm", D), ("ffn_norm", D), ("dn_out_norm", DH), ("swa_q_norm", SWA_D), ("swa_k_norm", SWA_D),
    ("swa_sinks", SWA_H), ("dn_a_log", DN_H), ("dn_dt_bias", DN_H), ("rel_bias", NBUCKET * SWA_H),
    ("dn_conv", CONV * QKVW),
)
_SMALL_OFF = {}
_o = 0
for _n, _s in _SMALL:
    _SMALL_OFF[_n] = (_o, _s)
    _o += _s
_LOSS_OFF = _o
_SMALL_ROWS = -(-(_o + 1) // (8 * 128)) * 8


def _cparams(**kw):
    return pltpu.CompilerParams(vmem_limit_bytes=VMEM_LIMIT, **kw)


_DIMS = {
    "nn": (((1,), (0,)), ((), ())),
    "nt": (((1,), (1,)), ((), ())),
    "tn": (((0,), (0,)), ((), ())),
    "bnn": (((2,), (1,)), ((0,), (0,))),
    "bnt": (((2,), (2,)), ((0,), (0,))),
    "btn": (((1,), (1,)), ((0,), (0,))),
}


def _raw_dot(a, b, kind, exact):
    if exact:
        prec = lax.Precision.HIGH if exact == "x3" else lax.Precision.HIGHEST
        return lax.dot_general(a, b, _DIMS[kind], precision=prec, preferred_element_type=F32)
    return lax.dot_general(a.astype(BF16), b.astype(BF16), _DIMS[kind], preferred_element_type=F32)


@functools.partial(jax.custom_vjp, nondiff_argnums=(2, 3))
def _dot(a, b, kind, exact):
    return _raw_dot(a, b, kind, exact)


def _dot_fwd(a, b, kind, exact):
    return _raw_dot(a, b, kind, exact), (a, b)


def _dot_bwd(kind, exact, res, g):
    a, b = res
    pre = kind[:-2]
    nn, nt, tn = pre + "nn", pre + "nt", pre + "tn"
    if kind == nn:
        return _dot(g, b, nt, exact), _dot(a, g, tn, exact)
    if kind == nt:
        return _dot(g, b, nn, exact), _dot(g, a, tn, exact)
    return _dot(b, g, nt, exact), _dot(a, g, nn, exact)


_dot.defvjp(_dot_fwd, _dot_bwd)


def _silu(x):
    return x * jax.nn.sigmoid(x)


def _f_rms(x, gain):
    return x * lax.rsqrt(jnp.mean(x * x, axis=-1, keepdims=True) + EPS) * gain


def _f_dn_pre(xs0, xs1, xs2, xs3, ba, cw, alog, dtb):
    rows = xs0.shape[0]
    c = xs0 * cw[0:1] + xs1 * cw[1:2] + xs2 * cw[2:3] + xs3 * cw[3:4]
    qkv = _silu(c)
    qs, ks, bbs, gbs = [], [], [], []
    for h in range(DN_H):
        qh = qkv[:, h * DH:(h + 1) * DH]
        kh = qkv[:, DNW + h * DH:DNW + (h + 1) * DH]
        qs.append(qh * lax.rsqrt(jnp.sum(qh * qh, axis=-1, keepdims=True) + EPS) * (DH ** -0.5))
        ks.append(kh * lax.rsqrt(jnp.sum(kh * kh, axis=-1, keepdims=True) + EPS))
        beta = jax.nn.sigmoid(ba[:, h:h + 1])
        ar = ba[:, DN_H + h:DN_H + h + 1] + dtb[:, h:h + 1]
        softplus = jnp.maximum(ar, 0.0) + jnp.log1p(jnp.exp(-jnp.abs(ar)))
        g = -jnp.exp(alog[:, h:h + 1]) * softplus
        bbs.append(jnp.broadcast_to(beta, (rows, DH)))
        gbs.append(jnp.broadcast_to(g, (rows, DH)))
    return (jnp.concatenate(qs, axis=1), jnp.concatenate(ks, axis=1), qkv[:, 2 * DNW:],
            jnp.concatenate(bbs, axis=1), jnp.concatenate(gbs, axis=1))


def _f_dn_post(o, z, gain):
    ys = []
    for h in range(DN_H):
        oh = o[:, h * DH:(h + 1) * DH]
        zh = z[:, h * DH:(h + 1) * DH]
        ys.append(oh * lax.rsqrt(jnp.mean(oh * oh, axis=-1, keepdims=True) + EPS) * gain * _silu(zh))
    return jnp.concatenate(ys, axis=1)


def _f_merge(pa, pb, ga, gb):
    return jax.nn.sigmoid(ga) * pa + jax.nn.sigmoid(gb) * pb


def _f_swiglu(g, u):
    return _silu(g) * u


def _f_chunk(q, k, v, gb, bb, s):
    c = CHUNK
    nh = q.shape[0]
    ii = lax.broadcasted_iota(jnp.int32, (nh, c, c), 1)
    jj = lax.broadcasted_iota(jnp.int32, (nh, c, c), 2)
    incl = ii >= jj
    strict = ii > jj
    eye = (ii == jj).astype(F32)
    gcb = _dot(incl.astype(F32), gb, "bnn", True)
    lane0 = (lax.broadcasted_iota(jnp.int32, (nh, c, DH), 2) == 0).astype(F32)
    gcol = gcb[:, :, :c]
    grow = _dot(lane0, gcb, "bnt", True)
    decay = jnp.where(incl, jnp.exp(jnp.where(incl, gcol - grow, 0.0)), 0.0)
    kb = k * bb
    vb = v * bb
    a = jnp.where(strict, _dot(kb, k, "bnt", False) * decay, 0.0)
    p = -a
    t = eye + p
    for _ in range(5):
        p = _dot(p, p, "bnn", "x3")
        t = t + _dot(t, p, "bnn", "x3")
    eg = jnp.exp(gcb)
    u = _dot(t, vb, "bnn", "x3")
    w = _dot(t, kb * eg, "bnn", "x3")
    qk = jnp.where(incl, _dot(q, k, "bnt", False) * decay, 0.0)
    qe = q * eg
    glast = gcb[:, c - 1:c, :]
    k_dec = k * jnp.exp(glast - gcb)
    e_last = jnp.exp(glast)
    outs = []
    for g in range(nh // DN_H):
        sl = slice(g * DN_H, (g + 1) * DN_H)
        v_new = u[sl] - _dot(w[sl], s, "bnn", False)
        outs.append(_dot(qe[sl], s, "bnn", False) + _dot(qk[sl], v_new, "bnn", False))
        s = s * e_last[sl] + _dot(k_dec[sl], v_new, "btn", False)
    return jnp.concatenate(outs, axis=0), s


def _f_swa(q4, kp, kc, vp, vc, bias4, qg, kg, sink, mask):
    kb = jnp.concatenate([kp, kc], axis=0)
    vb = jnp.concatenate([vp, vc], axis=0)
    kn = kb * lax.rsqrt(jnp.mean(kb * kb, axis=-1, keepdims=True) + EPS) * kg
    outs = []
    for g in range(SWA_G):
        qq = q4[g]
        qn = qq * lax.rsqrt(jnp.mean(qq * qq, axis=-1, keepdims=True) + EPS) * qg
        lg = _dot(qn, kn, "nt", False) * (SWA_D ** -0.5) + bias4[g]
        lg = jnp.where(mask, lg, NEG)
        sk = sink[:, g:g + 1]
        m = lax.stop_gradient(jnp.maximum(jnp.max(lg, axis=-1, keepdims=True), sk))
        p = jnp.exp(lg - m)
        den = jnp.sum(p, axis=-1, keepdims=True) + jnp.exp(sk - m)
        outs.append(_dot(p / den, vb, "nn", False))
    return jnp.stack(outs, axis=0)


def _bdot(a, b, kind="nn"):
    return lax.dot_general(a.astype(BF16), b.astype(BF16), _DIMS[kind], preferred_element_type=F32)


def _pc(kern, name, grid, in_specs, out_specs, out_shape, scratch=()):
    return pl.pallas_call(
        kern, name=name, grid=grid, in_specs=in_specs, out_specs=out_specs, out_shape=out_shape,
        scratch_shapes=list(scratch), compiler_params=_cparams(dimension_semantics=("arbitrary",) * len(grid)))


def _mm(a, b, kind, out_dtype, tm, tn, name):
    if kind == "tn":
        k, m = a.shape
    else:
        m, k = a.shape
    n = b.shape[0] if kind == "nt" else b.shape[1]
    tm, tn = min(tm, m), min(tn, n)
    assert m % tm == 0 and n % tn == 0, (name, a.shape, b.shape, tm, tn)

    def kern(a_ref, b_ref, o_ref):
        o_ref[...] = _bdot(a_ref[...], b_ref[...], kind).astype(o_ref.dtype)

    a_spec = pl.BlockSpec((k, tm), lambda i, j: (0, i)) if kind == "tn" else pl.BlockSpec((tm, k), lambda i, j: (i, 0))
    b_spec = pl.BlockSpec((tn, k), lambda i, j: (j, 0)) if kind == "nt" else pl.BlockSpec((k, tn), lambda i, j: (0, j))
    return _pc(kern, name, (m // tm, n // tn), [a_spec, b_spec], pl.BlockSpec((tm, tn), lambda i, j: (i, j)),
               SDS((m, n), out_dtype))(a, b)


def _rows(body, name, m, tm, row_ins, full_ins, row_outs, acc_outs=()):
    n_r, n_f, n_o, n_a = len(row_ins), len(full_ins), len(row_outs), len(acc_outs)
    assert m % tm == 0

    def kern(*refs):
        r = refs[:n_r]
        f = refs[n_r:n_r + n_f]
        o = refs[n_r + n_f:n_r + n_f + n_o]
        acc = refs[n_r + n_f + n_o:]
        outs, sums = body([x[...] for x in r], [x[...] for x in f])
        for ref, val in zip(o, outs, strict=True):
            ref[...] = val.astype(ref.dtype)
        if n_a:
            @pl.when(pl.program_id(0) == 0)
            def _():
                for ref in acc:
                    ref[...] = jnp.zeros(ref.shape, F32)

            for ref, val in zip(acc, sums, strict=True):
                ref[...] += val

    in_specs = [pl.BlockSpec((tm, w), functools.partial(lambda i, cb: (i, cb), cb=cb)) for _, w, cb in row_ins]
    in_specs += [pl.BlockSpec(x.shape, lambda i: (0, 0)) for x in full_ins]
    out_specs = [pl.BlockSpec((tm, w), lambda i: (i, 0)) for w, _ in row_outs]
    out_specs += [pl.BlockSpec(s, lambda i: (0, 0)) for s in acc_outs]
    out_shape = [SDS((m, w), dt) for w, dt in row_outs]
    out_shape += [SDS(s, F32) for s in acc_outs]
    return _pc(kern, name, (m // tm,), in_specs, out_specs, out_shape)(*[x for x, _, _ in row_ins], *full_ins)


def _whole(x):
    return (x, x.shape[1], 0)


def _zero_first(refs):
    @pl.when(pl.program_id(0) == 0)
    def _():
        for ref in refs:
            ref[...] = jnp.zeros(ref.shape, F32)


GROUP = 4


def _heads(ref):
    return jnp.stack([ref[g * CHUNK:(g + 1) * CHUNK, h * DH:(h + 1) * DH]
                      for g in range(GROUP) for h in range(DN_H)], axis=0)


def _unheads(ref, val):
    for g in range(GROUP):
        for h in range(DN_H):
            ref[g * CHUNK:(g + 1) * CHUNK, h * DH:(h + 1) * DH] = val[g * DN_H + h]


def _dn_chunks_fwd(q, k, v, gb, bb):
    s_len = q.shape[0]
    ng = s_len // (GROUP * CHUNK)

    def kern(q_ref, k_ref, v_ref, g_ref, b_ref, o_ref, sall_ref, state):
        _zero_first([state])
        s = state[...]
        sall_ref[0] = s
        o, s_new = _f_chunk(*[_heads(r) for r in (q_ref, k_ref, v_ref, g_ref, b_ref)], s)
        _unheads(o_ref, o)
        state[...] = s_new

    blk = pl.BlockSpec((GROUP * CHUNK, DNW), lambda c: (c, 0))
    return _pc(kern, "dn_chunks_fwd", (ng,), [blk] * 5,
               [blk, pl.BlockSpec((1, DN_H, DH, DH), lambda c: (c, 0, 0, 0))],
               [SDS((s_len, DNW), F32), SDS((ng, DN_H, DH, DH), F32)],
               scratch=[pltpu.VMEM((DN_H, DH, DH), F32)])(q, k, v, gb, bb)


def _dn_chunks_bwd(q, k, v, gb, bb, s_all, d_o):
    s_len = q.shape[0]
    ng = s_len // (GROUP * CHUNK)

    def kern(q_ref, k_ref, v_ref, g_ref, b_ref, sall_ref, do_ref, dq_ref, dk_ref, dv_ref, dg_ref, db_ref, dstate):
        _zero_first([dstate])
        _, vjp = jax.vjp(_f_chunk, *[_heads(r) for r in (q_ref, k_ref, v_ref, g_ref, b_ref)], sall_ref[0])
        *d_ins, ds = vjp((_heads(do_ref), dstate[...]))
        for ref, val in zip((dq_ref, dk_ref, dv_ref, dg_ref, db_ref), d_ins, strict=True):
            _unheads(ref, val)
        dstate[...] = ds

    blk = pl.BlockSpec((GROUP * CHUNK, DNW), lambda c: (ng - 1 - c, 0))
    return _pc(kern, "dn_chunks_bwd", (ng,),
               [blk] * 5 + [pl.BlockSpec((1, DN_H, DH, DH), lambda c: (ng - 1 - c, 0, 0, 0)), blk],
               [blk] * 5, [SDS((s_len, DNW), F32)] * 5,
               scratch=[pltpu.VMEM((DN_H, DH, DH), F32)])(q, k, v, gb, bb, s_all, d_o)


def _t5_bucket_table():
    qi = np.arange(BLK)[:, None]
    kj = np.arange(2 * BLK)[None, :]
    dist = BLK + qi - kj
    n = np.maximum(dist, 0)
    max_exact = NBUCKET // 2
    nf = np.maximum(n, 1).astype(np.float32)
    large = max_exact + (np.log(nf / np.float32(max_exact)) / np.float32(math.log(MAXDIST / max_exact))
                         * np.float32(NBUCKET - max_exact)).astype(np.int32)
    large = np.minimum(large, NBUCKET - 1)
    return np.where(n < max_exact, n, large)


def _bucket_onehot_t():
    table = _t5_bucket_table().reshape(-1)
    return (np.arange(NBUCKET)[:, None] == table[None, :]).astype(np.float32)


def _swa_mask(first):
    qi = lax.broadcasted_iota(jnp.int32, (BLK, 2 * BLK), 0)
    kj = lax.broadcasted_iota(jnp.int32, (BLK, 2 * BLK), 1)
    dist = BLK + qi - kj
    window = (dist >= 0) & (dist < BLK)
    return window & ((kj >= BLK) | jnp.logical_not(first))


def _bias_expand(rel_bias_t):
    onehot = jnp.asarray(_bucket_onehot_t())

    def kern(r_ref, oh_ref, o_ref):
        o_ref[...] = _raw_dot(r_ref[...], oh_ref[...], "nn", True)

    return pl.pallas_call(
        kern, name="bias_expand", out_shape=SDS((SWA_H, BLK * 2 * BLK), F32), compiler_params=_cparams(),
    )(rel_bias_t, onehot)


def _bias_reduce(d_bias_flat):
    onehot = jnp.asarray(_bucket_onehot_t())

    def kern(d_ref, oh_ref, o_ref):
        o_ref[...] = _raw_dot(d_ref[...], oh_ref[...], "nt", True)

    return pl.pallas_call(
        kern, name="bias_reduce", out_shape=SDS((SWA_H, NBUCKET), F32), compiler_params=_cparams(),
    )(d_bias_flat, onehot)


def _swa_specs(nb, rev):
    def blk(n):
        return (nb - 1 - n) if rev else n

    q_spec = pl.BlockSpec((SWA_G, BLK, SWA_D), lambda kv, n: (kv, blk(n), 0))
    cur = pl.BlockSpec((1, BLK, SWA_D), lambda kv, n: (kv, blk(n), 0))
    prev = pl.BlockSpec((1, BLK, SWA_D), lambda kv, n: (kv, jnp.maximum(blk(n) - 1, 0), 0))
    bias = pl.BlockSpec((SWA_G, BLK, 2 * BLK), lambda kv, n: (kv, 0, 0))
    gain = pl.BlockSpec((1, SWA_D), lambda kv, n: (0, 0))
    sink = pl.BlockSpec((1, 1, SWA_G), lambda kv, n: (kv, 0, 0))
    return q_spec, cur, prev, bias, gain, sink


def _swa_fwd(q, k, v, bias, qg, kg, sinks):
    s_len = q.shape[1]
    nb = s_len // BLK
    q_spec, cur, prev, bias_spec, gain, sink = _swa_specs(nb, False)

    def kern(q_ref, kp_ref, kc_ref, vp_ref, vc_ref, b_ref, qg_ref, kg_ref, s_ref, o_ref):
        mask = _swa_mask(pl.program_id(1) == 0)
        o_ref[...] = _f_swa(q_ref[...], kp_ref[0], kc_ref[0], vp_ref[0], vc_ref[0], b_ref[...], qg_ref[...],
                            kg_ref[...], s_ref[0], mask)

    return _pc(kern, "swa_fwd", (SWA_KV, nb), [q_spec, prev, cur, prev, cur, bias_spec, gain, gain, sink],
               q_spec, SDS((SWA_H, s_len, SWA_D), F32))(q, k, k, v, v, bias, qg, kg, sinks)


def _swa_bwd(q, k, v, bias, qg, kg, sinks, d_out):
    s_len = q.shape[1]
    nb = s_len // BLK
    q_spec, cur, prev, bias_spec, gain, sink = _swa_specs(nb, True)

    def kern(q_ref, kp_ref, kc_ref, vp_ref, vc_ref, b_ref, qg_ref, kg_ref, s_ref, do_ref,
             dq_ref, dk_ref, dv_ref, db_ref, dqg_ref, dkg_ref, ds_ref, carry_k, carry_v):
        kv = pl.program_id(0)
        n = pl.program_id(1)
        mask = _swa_mask(n == nb - 1)

        @pl.when(n == 0)
        def _():
            carry_k[...] = jnp.zeros(carry_k.shape, F32)
            carry_v[...] = jnp.zeros(carry_v.shape, F32)
            db_ref[...] = jnp.zeros(db_ref.shape, F32)
            ds_ref[...] = jnp.zeros(ds_ref.shape, F32)

        @pl.when((n == 0) & (kv == 0))
        def _():
            dqg_ref[...] = jnp.zeros(dqg_ref.shape, F32)
            dkg_ref[...] = jnp.zeros(dkg_ref.shape, F32)

        fn = functools.partial(_f_swa, mask=mask)
        _, vjp = jax.vjp(fn, q_ref[...], kp_ref[0], kc_ref[0], vp_ref[0], vc_ref[0], b_ref[...], qg_ref[...],
                         kg_ref[...], s_ref[0])
        dq, dkp, dkc, dvp, dvc, dbias, dqg, dkg, dsink = vjp(do_ref[...])
        dq_ref[...] = dq
        dk_ref[0] = dkc + carry_k[...]
        dv_ref[0] = dvc + carry_v[...]
        carry_k[...] = dkp
        carry_v[...] = dvp
        db_ref[...] += dbias
        dqg_ref[...] += dqg
        dkg_ref[...] += dkg
        ds_ref[0] += dsink

    return _pc(
        kern, "swa_bwd", (SWA_KV, nb),
        [q_spec, prev, cur, prev, cur, bias_spec, gain, gain, sink, q_spec],
        [q_spec, cur, cur, bias_spec, gain, gain, sink],
        [SDS((SWA_H, s_len, SWA_D), F32), SDS((SWA_KV, s_len, SWA_D), F32), SDS((SWA_KV, s_len, SWA_D), F32),
         SDS((SWA_H, BLK, 2 * BLK), F32), SDS((1, SWA_D), F32), SDS((1, SWA_D), F32), SDS((SWA_KV, 1, SWA_G), F32)],
        scratch=[pltpu.VMEM((BLK, SWA_D), F32), pltpu.VMEM((BLK, SWA_D), F32)],
    )(q, k, k, v, v, bias, qg, kg, sinks, d_out)


def _branch_merge(y_dn, y_swa, wa, wb, proj):
    s_len = y_dn.shape[0]
    tm = min(512, s_len)

    def kern(ya_ref, yb_ref, wa_ref, wb_ref, ga_ref, gb_ref, pa_ref, pb_ref, m_ref):
        pa = _bdot(ya_ref[...], wa_ref[0])
        pb = _bdot(yb_ref[...], wb_ref[0])
        pa_ref[...] = pa
        pb_ref[...] = pb
        m_ref[...] = _f_merge(pa, pb, ga_ref[...], gb_ref[...]).astype(BF16)

    y_spec = pl.BlockSpec((tm, DNW), lambda i, s: (i, 0))
    w_spec = pl.BlockSpec((1, DNW, CSH), lambda i, s: (s, 0, 0))
    o_spec = pl.BlockSpec((tm, CSH), lambda i, s: (i, s))
    ga_spec = pl.BlockSpec((tm, CSH), lambda i, s: (i, C_GATE // CSH + s))
    gb_spec = pl.BlockSpec((tm, CSH), lambda i, s: (i, (C_GATE + D) // CSH + s))
    return _pc(kern, "branch_merge", (s_len // tm, N_CHIPS), [y_spec, y_spec, w_spec, w_spec, ga_spec, gb_spec],
               [o_spec] * 3, [SDS((s_len, D), F32), SDS((s_len, D), F32), SDS((s_len, D), BF16)],
               )(y_dn, y_swa, wa, wb, proj, proj)


def _out_proj(merged, w_out, x, gain):
    s_len = x.shape[0]
    tm = min(256, s_len)

    def kern(m_ref, w_ref, x_ref, g_ref, x1_ref, h2_ref):
        x1 = x_ref[...] + _bdot(m_ref[...], w_ref[...])
        x1_ref[...] = x1
        h2_ref[...] = _f_rms(x1, g_ref[...]).astype(BF16)

    row = pl.BlockSpec((tm, D), lambda i: (i, 0))
    return _pc(kern, "out_proj", (s_len // tm,),
               [row, pl.BlockSpec((D, D), lambda i: (0, 0)), row, pl.BlockSpec((1, D), lambda i: (0, 0))],
               [row, row], [SDS((s_len, D), F32), SDS((s_len, D), BF16)])(merged, w_out, x, gain)


def _ffn_up(h2, wg, wu):
    s_len = h2.shape[0]
    tm = min(512, s_len)

    def kern(h_ref, g_ref, u_ref, gt_ref, up_ref, act_ref):
        h = h_ref[...]
        g = _bdot(h, g_ref[0])
        u = _bdot(h, u_ref[0])
        gt_ref[0] = g
        up_ref[0] = u
        act_ref[0] = _f_swiglu(g, u).astype(BF16)

    w_spec = pl.BlockSpec((1, D, FSH), lambda s, i: (s, 0, 0))
    o_spec = pl.BlockSpec((1, tm, FSH), lambda s, i: (s, i, 0))
    shape = (N_CHIPS, s_len, FSH)
    return _pc(kern, "ffn_up", (N_CHIPS, s_len // tm), [pl.BlockSpec((tm, D), lambda s, i: (i, 0)), w_spec, w_spec],
               [o_spec] * 3, [SDS(shape, F32), SDS(shape, F32), SDS(shape, BF16)])(h2, wg, wu)


def _ffn_down_loss(act, wd, x1, target):
    s_len = x1.shape[0]
    tm = min(256, s_len)

    def kern(a_ref, w_ref, x_ref, t_ref, dy_ref, dyb_ref, loss_ref):
        _zero_first([loss_ref])
        y = x_ref[...]
        for s in range(N_CHIPS):
            y = y + _bdot(a_ref[s], w_ref[s])
        d = y - t_ref[...]
        dy = d * (1.0 / D)
        dy_ref[...] = dy
        dyb_ref[...] = dy.astype(BF16)
        loss_ref[...] += jnp.sum(d * d).reshape(1, 1) * (0.5 / D)

    row = pl.BlockSpec((tm, D), lambda i: (i, 0))
    return _pc(kern, "ffn_down_loss", (s_len // tm,),
               [pl.BlockSpec((N_CHIPS, tm, FSH), lambda i: (0, i, 0)),
                pl.BlockSpec((N_CHIPS, FSH, D), lambda i: (0, 0, 0)), row, row],
               [row, row, pl.BlockSpec((1, 1), lambda i: (0, 0))],
               [SDS((s_len, D), F32), SDS((s_len, D), BF16), SDS((1, 1), F32)])(act, wd, x1, target)


def _ffn_dact(dy_b, wd, gt, up):
    s_len = dy_b.shape[0]
    tm = min(512, s_len)

    def kern(dy_ref, w_ref, gt_ref, up_ref, dg_ref, du_ref):
        d_act = _bdot(dy_ref[...], w_ref[0], "nt")
        _, vjp = jax.vjp(_f_swiglu, gt_ref[0], up_ref[0])
        dg, du = vjp(d_act)
        dg_ref[0] = dg.astype(BF16)
        du_ref[0] = du.astype(BF16)

    a_spec = pl.BlockSpec((1, tm, FSH), lambda s, i: (s, i, 0))
    shape = (N_CHIPS, s_len, FSH)
    return _pc(kern, "ffn_dact", (N_CHIPS, s_len // tm),
               [pl.BlockSpec((tm, D), lambda s, i: (i, 0)), pl.BlockSpec((1, FSH, D), lambda s, i: (s, 0, 0)),
                a_spec, a_spec],
               [a_spec, a_spec], [SDS(shape, BF16), SDS(shape, BF16)])(dy_b, wd, gt, up)


def _gw_down(act, dy_b):
    s_len = dy_b.shape[0]
    tn = 512

    def kern(a_ref, g_ref, o_ref):
        o_ref[0] = _bdot(a_ref[0], g_ref[...], "tn").astype(BF16)

    return _pc(kern, "gw_down", (N_CHIPS, D // tn),
               [pl.BlockSpec((1, s_len, FSH), lambda s, j: (s, 0, 0)), pl.BlockSpec((s_len, tn), lambda s, j: (0, j))],
               pl.BlockSpec((1, FSH, tn), lambda s, j: (s, 0, j)), SDS((N_CHIPS, FSH, D), BF16))(act, dy_b)


def _gw_gate_up(h2, d_gt, d_up):
    s_len = h2.shape[0]
    tk = 256

    def kern(h_ref, dg_ref, du_ref, og_ref, ou_ref):
        h = h_ref[...]
        og_ref[0] = _bdot(h, dg_ref[0], "tn").astype(BF16)
        ou_ref[0] = _bdot(h, du_ref[0], "tn").astype(BF16)

    d_spec = pl.BlockSpec((1, s_len, FSH), lambda s, j: (s, 0, 0))
    o_spec = pl.BlockSpec((1, tk, FSH), lambda s, j: (s, j, 0))
    shape = (N_CHIPS, D, FSH)
    return _pc(kern, "gw_gate_up", (N_CHIPS, D // tk), [pl.BlockSpec((s_len, tk), lambda s, j: (0, j)), d_spec, d_spec],
               [o_spec, o_spec], [SDS(shape, BF16), SDS(shape, BF16)])(h2, d_gt, d_up)


def _ffn_dh2(d_gt, d_up, wg, wu, x1, dy, gain):
    s_len = x1.shape[0]
    tm = min(256, s_len)

    def kern(dg_ref, du_ref, wg_ref, wu_ref, x_ref, dy_ref, g_ref, dx_ref, dxb_ref, dgain_ref):
        _zero_first([dgain_ref])
        dh2 = jnp.zeros((tm, D), F32)
        for s in range(N_CHIPS):
            dh2 = dh2 + _bdot(dg_ref[s], wg_ref[s], "nt") + _bdot(du_ref[s], wu_ref[s], "nt")
        _, vjp = jax.vjp(_f_rms, x_ref[...], g_ref[...])
        dx, dgain = vjp(dh2)
        dx1 = dx + dy_ref[...]
        dx_ref[...] = dx1
        dxb_ref[...] = dx1.astype(BF16)
        dgain_ref[...] += dgain

    row = pl.BlockSpec((tm, D), lambda i: (i, 0))
    d_spec = pl.BlockSpec((N_CHIPS, tm, FSH), lambda i: (0, i, 0))
    w_spec = pl.BlockSpec((N_CHIPS, D, FSH), lambda i: (0, 0, 0))
    vec = pl.BlockSpec((1, D), lambda i: (0, 0))
    return _pc(kern, "ffn_dh2", (s_len // tm,), [d_spec, d_spec, w_spec, w_spec, row, row, vec],
               [row, row, vec], [SDS((s_len, D), F32), SDS((s_len, D), BF16), SDS((1, D), F32)],
               )(d_gt, d_up, wg, wu, x1, dy, gain)


def _merge_bwd(dx1_b, w_out, pa, pb, proj):
    s_len = dx1_b.shape[0]
    tm = min(256, s_len)

    def kern(dx_ref, w_ref, pa_ref, pb_ref, g_ref, dpa_ref, dpb_ref, dg_ref):
        dm = _bdot(dx_ref[...], w_ref[...], "nt")
        gates = g_ref[...]
        _, vjp = jax.vjp(_f_merge, pa_ref[...], pb_ref[...], gates[:, :D], gates[:, D:])
        dpa, dpb, dga, dgb = vjp(dm)
        dpa_ref[...] = dpa.astype(BF16)
        dpb_ref[...] = dpb.astype(BF16)
        dg_ref[:, :D] = dga.astype(BF16)
        dg_ref[:, D:] = dgb.astype(BF16)

    row = pl.BlockSpec((tm, D), lambda i: (i, 0))
    return _pc(kern, "merge_bwd", (s_len // tm,),
               [row, pl.BlockSpec((D, D), lambda i: (0, 0)), row, row,
                pl.BlockSpec((tm, 2 * D), lambda i: (i, C_GATE // (2 * D)))],
               [row, row, pl.BlockSpec((tm, 2 * D), lambda i: (i, 0))],
               [SDS((s_len, D), BF16), SDS((s_len, D), BF16), SDS((s_len, 2 * D), BF16)],
               )(dx1_b, w_out, pa, pb, proj)


def _d_branch(d_pa, d_pb, wa, wb):
    s_len = d_pa.shape[0]
    tm = min(512, s_len)

    def kern(da_ref, db_ref, wa_ref, wb_ref, oa_ref, ob_ref):
        acc_a = jnp.zeros((tm, DNW), F32)
        acc_b = jnp.zeros((tm, SWAW), F32)
        for s in range(N_CHIPS):
            acc_a = acc_a + _bdot(da_ref[:, s * CSH:(s + 1) * CSH], wa_ref[s], "nt")
            acc_b = acc_b + _bdot(db_ref[:, s * CSH:(s + 1) * CSH], wb_ref[s], "nt")
        oa_ref[...] = acc_a
        ob_ref[...] = acc_b

    row = pl.BlockSpec((tm, D), lambda i: (i, 0))
    w_spec = pl.BlockSpec((N_CHIPS, DNW, CSH), lambda i: (0, 0, 0))
    out = pl.BlockSpec((tm, DNW), lambda i: (i, 0))
    return _pc(kern, "d_branch", (s_len // tm,), [row, row, w_spec, w_spec], [out, out],
               [SDS((s_len, DNW), F32), SDS((s_len, SWAW), F32)])(d_pa, d_pb, wa, wb)


def _gw_branch(y_dn, y_swa, d_pa, d_pb):
    s_len = y_dn.shape[0]

    def kern(ya_ref, yb_ref, da_ref, db_ref, oa_ref, ob_ref):
        oa_ref[0] = _bdot(ya_ref[...], da_ref[...], "tn").astype(BF16)
        ob_ref[0] = _bdot(yb_ref[...], db_ref[...], "tn").astype(BF16)

    y_spec = pl.BlockSpec((s_len, DNW), lambda s: (0, 0))
    d_spec = pl.BlockSpec((s_len, CSH), lambda s: (0, s))
    o_spec = pl.BlockSpec((1, DNW, CSH), lambda s: (s, 0, 0))
    shape = (N_CHIPS, DNW, CSH)
    return _pc(kern, "gw_branch", (N_CHIPS,), [y_spec, y_spec, d_spec, d_spec], [o_spec, o_spec],
               [SDS(shape, BF16), SDS(shape, BF16)])(y_dn, y_swa, d_pa, d_pb)


def _dh_rms(d_proj, w_in_p, x, dx1, gain):
    s_len = x.shape[0]
    tm = min(256, s_len)

    def kern(dp_ref, w_ref, x_ref, r_ref, g_ref, gx_ref, dgain_ref):
        _zero_first([dgain_ref])
        dh = _bdot(dp_ref[...], w_ref[...], "nt")
        _, vjp = jax.vjp(_f_rms, x_ref[...], g_ref[...])
        dx, dgain = vjp(dh)
        gx_ref[...] = dx + r_ref[...]
        dgain_ref[...] += dgain

    row = pl.BlockSpec((tm, D), lambda i: (i, 0))
    vec = pl.BlockSpec((1, D), lambda i: (0, 0))
    return _pc(kern, "dh_rms", (s_len // tm,),
               [pl.BlockSpec((tm, PW), lambda i: (i, 0)), pl.BlockSpec((D, PW), lambda i: (0, 0)), row, row, vec],
               [row, vec], [SDS((s_len, D), F32), SDS((1, D), F32)])(d_proj, w_in_p, x, dx1, gain)


HALO = 8


def _conv_taps(cur_ref, prev_ref, halo, first):
    tm = cur_ref.shape[0]
    halo[0:HALO, :] = jnp.where(first, 0.0, prev_ref[...])
    halo[HALO:, :] = cur_ref[...]
    return [halo[HALO - n:HALO - n + tm, :] for n in range(CONV - 1, 0, -1)] + [cur_ref[...]]


def _dn_pre_specs(s_len, tm, blk):
    cur = pl.BlockSpec((tm, QKVW), lambda i: (blk(i), 0))
    prev = pl.BlockSpec((HALO, QKVW), lambda i: (jnp.maximum(blk(i) * (tm // HALO) - 1, 0), 0))
    ba = pl.BlockSpec((tm, 128), lambda i: (blk(i), C_BA // 128))
    row = pl.BlockSpec((tm, DNW), lambda i: (blk(i), 0))
    full = [pl.BlockSpec((CONV, QKVW), lambda i: (0, 0)), pl.BlockSpec((1, DN_H), lambda i: (0, 0)),
            pl.BlockSpec((1, DN_H), lambda i: (0, 0))]
    return cur, prev, ba, row, full


def _dn_pre_fwd(proj, conv_w, alog, dtb):
    s_len = proj.shape[0]
    tm = min(128, s_len)
    cur, prev, ba, row, full = _dn_pre_specs(s_len, tm, lambda i: i)

    def kern(cur_ref, prev_ref, ba_ref, cw_ref, al_ref, dt_ref, q_ref, k_ref, v_ref, bb_ref, gb_ref, halo):
        xs = _conv_taps(cur_ref, prev_ref, halo, pl.program_id(0) == 0)
        outs = _f_dn_pre(*xs, ba_ref[...], cw_ref[...], al_ref[...], dt_ref[...])
        for ref, val in zip((q_ref, k_ref, v_ref, bb_ref, gb_ref), outs, strict=True):
            ref[...] = val

    return _pc(kern, "dn_pre_fwd", (s_len // tm,), [cur, prev, ba] + full, [row] * 5, [SDS((s_len, DNW), F32)] * 5,
               scratch=[pltpu.VMEM((tm + HALO, QKVW), F32)])(proj, proj, proj, conv_w, alog, dtb)


def _dn_pre_bwd(proj, conv_w, alog, dtb, cots):
    s_len = proj.shape[0]
    tm = min(128, s_len)
    nb = s_len // tm
    cur, prev, ba, row, full = _dn_pre_specs(s_len, tm, lambda i: nb - 1 - i)

    def kern(cur_ref, prev_ref, ba_ref, cw_ref, al_ref, dt_ref, dq_ref, dk_ref, dv_ref, dbb_ref, dgb_ref,
             dqkv_ref, dba_ref, dcw_ref, dal_ref, ddt_ref, halo, *tails):
        i = pl.program_id(0)
        _zero_first([dcw_ref, dal_ref, ddt_ref])

        @pl.when(i == 0)
        def _():
            for t in tails:
                t[tm:, :] = jnp.zeros((HALO, QKVW), F32)

        xs = _conv_taps(cur_ref, prev_ref, halo, i == nb - 1)
        _, vjp = jax.vjp(_f_dn_pre, *xs, ba_ref[...], cw_ref[...], al_ref[...], dt_ref[...])
        *dxs, dba, dcw, dal, ddt = vjp((dq_ref[...], dk_ref[...], dv_ref[...], dbb_ref[...], dgb_ref[...]))
        total = dxs[CONV - 1]
        for j, t in enumerate(tails):
            n = CONV - 1 - j
            t[0:tm, :] = dxs[j]
            total = total + t[n:n + tm, :]
            t[tm:, :] = dxs[j][0:HALO, :]
        dqkv_ref[...] = total.astype(BF16)
        dba_ref[...] = dba.astype(BF16)
        dcw_ref[...] += dcw
        dal_ref[...] += dal
        ddt_ref[...] += ddt

    return _pc(kern, "dn_pre_bwd", (nb,), [cur, prev, ba] + full + [row] * 5,
               [cur, pl.BlockSpec((tm, 128), lambda i: (nb - 1 - i, 0))] + full,
               [SDS((s_len, QKVW), BF16), SDS((s_len, 128), BF16), SDS((CONV, QKVW), F32), SDS((1, DN_H), F32),
                SDS((1, DN_H), F32)],
               scratch=[pltpu.VMEM((tm + HALO, QKVW), F32)] * CONV)(proj, proj, proj, conv_w, alog, dtb, *cots)


def _pad_w_in(w_in):
    pieces = [w_in[:, o0:o0 + w] for o0, w, _ in sorted(_ORIG_PIECES, key=lambda t: t[2])]
    pieces.append(jnp.zeros((w_in.shape[0], PW - D_IN), w_in.dtype))
    return jnp.concatenate(pieces, axis=1)


def _unpad_w_in(g):
    return jnp.concatenate([g[:, p0:p0 + w] for _, w, p0 in _ORIG_PIECES], axis=1)


def _local_step(x, target, wts):
    s_len = x.shape[0]
    tm = min(256, s_len)
    tmh = min(128, s_len)
    w_in_p = wts["w_in_p"]
    attn_gain = wts["attn_norm"]
    ffn_gain = wts["ffn_norm"]
    conv_w = wts["dn_conv"]
    alog, dtb, out_gain = wts["dn_a_log"], wts["dn_dt_bias"], wts["dn_out_norm"]
    qg, kg = wts["swa_q_norm"], wts["swa_k_norm"]
    sinks = wts["swa_sinks"].reshape(SWA_KV, 1, SWA_G)

    (h,) = _rows(lambda r, f: ([_f_rms(r[0], f[0])], []), "rms1_fwd", s_len, tm, [_whole(x)], [attn_gain],
                 [(D, BF16)])
    proj = _mm(h, w_in_p, "nn", F32, 512, 1024, "mm_proj")
    q_dn, k_dn, v_dn, bb, gb = _dn_pre_fwd(proj, conv_w, alog, dtb)
    o_dn, s_all = _dn_chunks_fwd(q_dn, k_dn, v_dn, gb, bb)
    post_ins = [_whole(o_dn), (proj, DNW, C_Z // DNW)]
    (y_dn,) = _rows(lambda r, f: ([_f_dn_post(r[0], r[1], f[0])], []), "dn_post_fwd", s_len, tm, post_ins,
                    [out_gain], [(DNW, BF16)])

    sq = proj[:, C_SQ:C_SQ + SWAW].reshape(s_len, SWA_H, SWA_D).transpose(1, 0, 2)
    sk = proj[:, C_SK:C_SK + SWAKW].reshape(s_len, SWA_KV, SWA_D).transpose(1, 0, 2)
    sv = proj[:, C_SV:C_SV + SWAKW].reshape(s_len, SWA_KV, SWA_D).transpose(1, 0, 2)
    bias = _bias_expand(wts["rel_bias"].T).reshape(SWA_H, BLK, 2 * BLK)
    o_swa = _swa_fwd(sq, sk, sv, bias, qg, kg, sinks)
    y_swa = o_swa.transpose(1, 0, 2).reshape(s_len, SWAW).astype(BF16)

    wts = {**wts, **wts["late"](y_swa)}
    p_a, p_b, merged = _branch_merge(y_dn, y_swa, wts["wa"], wts["wb"], proj)
    x1, h2 = _out_proj(merged, wts["w_out"], x, ffn_gain)
    gt, up, act = _ffn_up(h2, wts["wg"], wts["wu"])
    dy, dy_b, loss = _ffn_down_loss(act, wts["wd"], x1, target)

    grads = {}
    d_gt, d_up = _ffn_dact(dy_b, wts["wd"], gt, up)
    grads["w_down"] = _gw_down(act, dy_b)
    grads["w_gate"], grads["w_up"] = _gw_gate_up(h2, d_gt, d_up)
    dx1, dx1_b, grads["ffn_norm"] = _ffn_dh2(d_gt, d_up, wts["wg"], wts["wu"], x1, dy, ffn_gain)
    grads["w_out"] = _mm(merged, dx1_b, "tn", BF16, 512, 512, "gw_out")
    d_pa, d_pb, d_gr = _merge_bwd(dx1_b, wts["w_out"], p_a, p_b, proj)
    d_ydn, d_yswa = _d_branch(d_pa, d_pb, wts["wa"], wts["wb"])
    grads["w_branch_dn"], grads["w_branch_swa"] = _gw_branch(y_dn, y_swa, d_pa, d_pb)
    token = wts["send_early"](grads)
    qg_t = qg + token[0:1, 0:1]
    out_gain_t = out_gain + token[0:1, 0:1]

    d_oswa = d_yswa.reshape(s_len, SWA_H, SWA_D).transpose(1, 0, 2)
    d_sq, d_sk, d_sv, d_bias, grads["swa_q_norm"], grads["swa_k_norm"], d_sinks = _swa_bwd(
        sq, sk, sv, bias, qg_t, kg, sinks, d_oswa)
    grads["swa_sinks"] = d_sinks.reshape(1, SWA_H)
    grads["rel_bias"] = _bias_reduce(d_bias.reshape(SWA_H, BLK * 2 * BLK)).T
    d_sq = d_sq.transpose(1, 0, 2).reshape(s_len, SWAW).astype(BF16)
    d_sk = d_sk.transpose(1, 0, 2).reshape(s_len, SWAKW).astype(BF16)
    d_sv = d_sv.transpose(1, 0, 2).reshape(s_len, SWAKW).astype(BF16)

    def post_bwd(r, f):
        _, vjp = jax.vjp(_f_dn_post, r[0], r[1], f[0])
        d_o, d_z, d_gain = vjp(r[2])
        return [d_o, d_z], [d_gain]

    d_o, d_z, grads["dn_out_norm"] = _rows(post_bwd, "dn_post_bwd", s_len, tm, post_ins + [_whole(d_ydn)], [out_gain_t],
                                           [(DNW, F32), (DNW, BF16)], [(1, DH)])
    d_q, d_k, d_v, d_gb, d_bb = _dn_chunks_bwd(q_dn, k_dn, v_dn, gb, bb, s_all, d_o)

    d_qkv, d_ba, grads["dn_conv"], grads["dn_a_log"], grads["dn_dt_bias"] = _dn_pre_bwd(
        proj, conv_w, alog, dtb, (d_q, d_k, d_v, d_bb, d_gb))

    d_proj = jnp.concatenate(
        [d_qkv, d_z, d_gr, d_sq, d_sk, d_sv, d_ba, jnp.zeros((s_len, PW - C_BA - 128), BF16)], axis=1)
    grads["w_in_p"] = _mm(h, d_proj, "tn", BF16, 512, 1024, "gw_in")
    grad_x, grads["attn_norm"] = _dh_rms(d_proj, w_in_p, x, dx1, attn_gain)
    return loss, grad_x, grads


_HBM = pl.BlockSpec(memory_space=pl.ANY)


def _place():
    return lax.axis_index("x"), lax.axis_index("y"), lax.axis_index("c")


def _other_chips(x, y):
    return [(1 - x, y), (x, 1 - y), (1 - x, 1 - y)]


def _rcopy(src, dst, send_sems, recv_sems, k, to):
    return pltpu.make_async_remote_copy(src_ref=src, dst_ref=dst, send_sem=send_sems.at[k], recv_sem=recv_sems.at[k],
                                        device_id=to, device_id_type=MESH)


def _comm_call(body, name, ins, out_shapes, n_remote, landing=0):
    first = len(ins) - landing
    return pl.pallas_call(
        body, name=name, in_specs=[_HBM] * len(ins), out_specs=[_HBM] * len(out_shapes), out_shape=out_shapes,
        scratch_shapes=[pltpu.SemaphoreType.DMA((n_remote,)), pltpu.SemaphoreType.DMA((n_remote,))],
        input_output_aliases={first + i: i for i in range(landing)},
        compiler_params=_cparams(has_side_effects=True),
    )(*ins)


def _own_slot(blocks, chip):
    return [lax.dynamic_update_slice(jnp.zeros((N_CHIPS,) + b.shape, b.dtype), b[None], (chip, 0, 0)) for b in blocks]


def _gather_weights(ws, chip):
    n = len(ws)
    halves = [w.shape[0] // 2 for w in ws]

    def body(*refs):
        w_refs, o_refs = refs[:n], refs[2 * n:3 * n]
        send_sems, recv_sems = refs[3 * n:]
        x, y, c = _place()
        s = 2 * x + y
        sib = (x, y, 1 - c)
        chips = _other_chips(x, y)

        def rows(i, half):
            return pl.ds(half * halves[i], halves[i])

        first = []
        for j, (cx, cy) in enumerate(chips):
            for i in range(n):
                cp = _rcopy(w_refs[i].at[rows(i, c), :], o_refs[i].at[s, rows(i, c), :], send_sems, recv_sems,
                            j * n + i, (cx, cy, c))
                cp.start()
                first.append(cp)
        passed = []
        for j, (cx, cy) in enumerate(chips):
            sj = 2 * cx + cy
            for i in range(n):
                blk = o_refs[i].at[sj, rows(i, c), :]
                _rcopy(blk, blk, send_sems, recv_sems, j * n + i, (cx, cy, c)).wait_recv()
                cp = _rcopy(blk, blk, send_sems, recv_sems, (3 + j) * n + i, sib)
                cp.start()
                passed.append(cp)
        for j, (cx, cy) in enumerate(chips):
            sj = 2 * cx + cy
            for i in range(n):
                blk = o_refs[i].at[sj, rows(i, 1 - c), :]
                _rcopy(blk, blk, send_sems, recv_sems, (3 + j) * n + i, sib).wait_recv()
        for cp in first + passed:
            cp.wait_send()

    return _comm_call(body, "gather_weights", list(ws) + _own_slot(ws, chip),
                      [SDS((N_CHIPS,) + w.shape, w.dtype) for w in ws], 6 * n, landing=n)


_HBM_ONLY = pl.BlockSpec(memory_space=pltpu.HBM)
_SEM = pl.BlockSpec(memory_space=pltpu.SEMAPHORE)
_DATAFLOW = pltpu.SideEffectType.DATAFLOW_SIDE_EFFECTING


def _in_hbm(a):
    return pltpu.with_memory_space_constraint(a, pltpu.HBM)


def _gather_windows(blocks):
    halves = [b.shape[0] // 2 for b in blocks]

    def src_at(ref, i, c, sj):
        return ref.at[pl.ds(c * halves[i], halves[i]), :]

    def dst_at(ref, i, c, s_from):
        return ref.at[s_from, pl.ds(c * halves[i], halves[i]), :]

    return src_at, dst_at


def _exchange_windows():
    return (lambda ref, i, c, sj: ref.at[sj]), (lambda ref, i, c, s_from: ref.at[s_from])


def _split_start(name, ws, lands, dep, windows):
    n = len(ws)
    src_at, dst_at = windows

    def body(*refs):
        w_refs, l_refs = refs[:n], refs[n:2 * n]
        send_sems, recv_sems = refs[2 * n + 1], refs[2 * n + 2]
        token = refs[-1]
        x, y, c = _place()
        s = 2 * x + y
        for j, (cx, cy) in enumerate(_other_chips(x, y)):
            for i in range(n):
                _rcopy(src_at(w_refs[i], i, c, 2 * cx + cy), dst_at(l_refs[i], i, c, s), send_sems, recv_sems,
                       j * n + i, (cx, cy, c)).start()
        token[...] = jnp.zeros_like(token)

    outs = pl.pallas_call(
        body, name=name,
        out_shape=(pltpu.SemaphoreType.DMA((3 * n,)), pltpu.SemaphoreType.DMA((3 * n,)),
                   *[pltpu.HBM(w.shape, w.dtype) for w in ws], *[pltpu.HBM(t.shape, t.dtype) for t in lands],
                   SDS((8, 128), F32)),
        in_specs=[_HBM_ONLY] * (2 * n) + [pl.BlockSpec(memory_space=pl.ANY)],
        out_specs=(_SEM, _SEM, *[_HBM_ONLY] * (2 * n), pl.BlockSpec(memory_space=pltpu.VMEM)),
        input_output_aliases={i: 2 + i for i in range(2 * n)},
        compiler_params=pltpu.CompilerParams(has_side_effects=_DATAFLOW),
    )(*[_in_hbm(w) for w in ws], *[_in_hbm(t) for t in lands], dep)
    return outs[0], outs[1], outs[2:2 + n], outs[2 + n:2 + 2 * n], outs[-1]


def _split_wait(name, w_thru, l_thru, send_sems, recv_sems, after, windows):
    n = len(w_thru)
    src_at, dst_at = windows

    def body(*refs):
        w_refs, l_refs = refs[:n], refs[n:2 * n]
        send_sems, recv_sems = refs[2 * n], refs[2 * n + 1]
        x, y, c = _place()
        for j, (cx, cy) in enumerate(_other_chips(x, y)):
            sj = 2 * cx + cy
            for i in range(n):
                cp = _rcopy(src_at(w_refs[i], i, c, sj), dst_at(l_refs[i], i, c, sj), send_sems, recv_sems, j * n + i,
                            (cx, cy, c))
                cp.wait_send()
                cp.wait_recv()

    outs = pl.pallas_call(
        body, name=name,
        out_shape=[pltpu.HBM(w.shape, w.dtype) for w in w_thru] + [pltpu.HBM(t.shape, t.dtype) for t in l_thru],
        in_specs=[_HBM_ONLY] * (2 * n) + [_SEM, _SEM, pl.BlockSpec(memory_space=pl.ANY)],
        out_specs=[_HBM_ONLY] * (2 * n),
        input_output_aliases={i: i for i in range(2 * n)},
        compiler_params=pltpu.CompilerParams(has_side_effects=_DATAFLOW),
    )(*w_thru, *l_thru, send_sems, recv_sems, after)
    return outs[n:]


def _sibling_fill(lands):
    n = len(lands)
    halves = [t.shape[1] // 2 for t in lands]

    def body(*refs):
        o_refs = refs[n:2 * n]
        send_sems, recv_sems = refs[2 * n:]
        x, y, c = _place()
        sib = (x, y, 1 - c)
        chips = _other_chips(x, y)
        sent = []
        for j, (cx, cy) in enumerate(chips):
            for i in range(n):
                blk = o_refs[i].at[2 * cx + cy, pl.ds(c * halves[i], halves[i]), :]
                cp = _rcopy(blk, blk, send_sems, recv_sems, j * n + i, sib)
                cp.start()
                sent.append(cp)
        for j, (cx, cy) in enumerate(chips):
            for i in range(n):
                blk = o_refs[i].at[2 * cx + cy, pl.ds((1 - c) * halves[i], halves[i]), :]
                _rcopy(blk, blk, send_sems, recv_sems, j * n + i, sib).wait_recv()
        for cp in sent:
            cp.wait_send()

    return _comm_call(body, "sibling_fill", list(lands), [SDS(t.shape, t.dtype) for t in lands], 3 * n, landing=n)


def _swap_halves(gs, name):
    n = len(gs)
    halves = [g.shape[1] // 2 for g in gs]

    def body(*refs):
        g_refs, o_refs = refs[:n], refs[n:2 * n]
        send_sems, recv_sems = refs[2 * n:]
        x, y, c = _place()
        cps = [_rcopy(g_refs[i].at[:, pl.ds((1 - c) * halves[i], halves[i]), :], o_refs[i], send_sems, recv_sems, i,
                      (x, y, 1 - c)) for i in range(n)]
        for cp in cps:
            cp.start()
        for cp in cps:
            cp.wait()

    return _comm_call(body, name, gs, [SDS((N_CHIPS, h, g.shape[2]), g.dtype) for g, h in zip(gs, halves)], n)


def _chip_exchange(ps, chip):
    n = len(ps)

    def body(*refs):
        p_refs, o_refs = refs[:n], refs[2 * n:3 * n]
        send_sems, recv_sems = refs[3 * n:]
        x, y, c = _place()
        s = 2 * x + y
        chips = _other_chips(x, y)
        sent = []
        for j, (cx, cy) in enumerate(chips):
            for i in range(n):
                cp = _rcopy(p_refs[i].at[2 * cx + cy], o_refs[i].at[s], send_sems, recv_sems, j * n + i, (cx, cy, c))
                cp.start()
                sent.append(cp)
        for j, (cx, cy) in enumerate(chips):
            sj = 2 * cx + cy
            for i in range(n):
                _rcopy(p_refs[i].at[sj], o_refs[i].at[sj], send_sems, recv_sems, j * n + i, (cx, cy, c)).wait_recv()
        for cp in sent:
            cp.wait_send()

    own = [lax.dynamic_index_in_dim(p, chip, axis=0, keepdims=False) for p in ps]
    return _comm_call(body, "chip_exchange", list(ps) + _own_slot(own, chip), [SDS(p.shape, p.dtype) for p in ps],
                      3 * n, landing=n)


def _swap_reduced(rs):
    n = len(rs)

    def body(*refs):
        r_refs, o_refs = refs[:n], refs[n:2 * n]
        send_sems, recv_sems = refs[2 * n:]
        x, y, c = _place()
        cps = [_rcopy(r_refs[i], o_refs[i], send_sems, recv_sems, i, (x, y, 1 - c)) for i in range(n)]
        for cp in cps:
            cp.start()
        for cp in cps:
            cp.wait()

    return _comm_call(body, "swap_reduced", rs, [SDS(r.shape, r.dtype) for r in rs], n)


def _all_sum_small(vec, name):
    n_dev = 8
    flips = [(bx, by, bc) for bx in (0, 1) for by in (0, 1) for bc in (0, 1)][1:]

    def body(v_ref, out_ref, gath, send_sems, recv_sems):
        x, y, c = _place()
        me = 4 * x + 2 * y + c
        gath[me] = v_ref[...]
        sent = []
        for k, (bx, by, bc) in enumerate(flips):
            peer = (x ^ bx, y ^ by, c ^ bc)
            cp = _rcopy(v_ref, gath.at[me], send_sems, recv_sems, k, peer)
            cp.start()
            sent.append(cp)
        for k, (bx, by, bc) in enumerate(flips):
            peer = (x ^ bx, y ^ by, c ^ bc)
            _rcopy(v_ref, gath.at[4 * peer[0] + 2 * peer[1] + peer[2]], send_sems, recv_sems, k, peer).wait_recv()
        for cp in sent:
            cp.wait_send()
        acc = gath[0]
        for d in range(1, n_dev):
            acc = acc + gath[d]
        out_ref[...] = acc

    vm = pl.BlockSpec(memory_space=pltpu.VMEM)
    return pl.pallas_call(
        body, name=name, in_specs=[vm], out_specs=vm, out_shape=SDS(vec.shape, F32),
        scratch_shapes=[pltpu.VMEM((n_dev,) + vec.shape, F32), pltpu.SemaphoreType.DMA((7,)),
                        pltpu.SemaphoreType.DMA((7,))],
        compiler_params=_cparams(has_side_effects=True),
    )(vec)


def _pack_small(vals, extra=None):
    parts = [vals[n].reshape(-1).astype(F32) for n, _ in _SMALL]
    parts.append(jnp.zeros((1,), F32) if extra is None else extra.reshape(1).astype(F32))
    flat = jnp.concatenate(parts)
    flat = jnp.concatenate([flat, jnp.zeros((_SMALL_ROWS * 128 - flat.shape[0],), F32)])
    return flat.reshape(_SMALL_ROWS, 128)


def _unpack_small(packed, shapes):
    flat = packed.reshape(-1)
    return {n: flat[_SMALL_OFF[n][0]:_SMALL_OFF[n][0] + _SMALL_OFF[n][1]].reshape(shapes[n]) for n, _ in _SMALL}


def _pair_sum(gs, gots, core, name):
    n = len(gs)

    def kern(c_ref, *refs):
        for i in range(n):
            refs[2 * n + i][...] = (refs[i][...].astype(F32) + refs[n + i][...].astype(F32)).astype(BF16)

    in_specs = [pl.BlockSpec((1, t.shape[1], t.shape[2]), lambda s, c_ref: (s, c_ref[0], 0)) for t in gots]
    in_specs += [pl.BlockSpec((1, t.shape[1], t.shape[2]), lambda s, c_ref: (s, 0, 0)) for t in gots]
    out_specs = [pl.BlockSpec((1, t.shape[1], t.shape[2]), lambda s, c_ref: (s, 0, 0)) for t in gots]
    return pl.pallas_call(
        kern, name=name,
        grid_spec=pltpu.PrefetchScalarGridSpec(num_scalar_prefetch=1, grid=(N_CHIPS,), in_specs=in_specs,
                                               out_specs=out_specs),
        out_shape=[SDS(t.shape, BF16) for t in gots],
        compiler_params=_cparams(dimension_semantics=("arbitrary",)),
    )(core.reshape(1).astype(jnp.int32), *gs, *gots)


def _chip_sum(qs):
    n = len(qs)

    def kern(*refs):
        for i in range(n):
            acc = refs[i][0].astype(F32)
            for s in range(1, N_CHIPS):
                acc = acc + refs[i][s].astype(F32)
            refs[n + i][...] = acc

    in_specs = [pl.BlockSpec((N_CHIPS, q.shape[1] // 2, q.shape[2]), lambda j: (0, j, 0)) for q in qs]
    out_specs = [pl.BlockSpec((q.shape[1] // 2, q.shape[2]), lambda j: (j, 0)) for q in qs]
    return _pc(kern, "chip_sum", (2,), in_specs, out_specs, [SDS(q.shape[1:], F32) for q in qs])(*qs)


def _adamw(w, g, m, v, name):
    rows, cols = w.shape
    tr = rows
    for cand in (256, 128, 64, 32, 16, 8):
        if rows % cand == 0 and rows > cand:
            tr = cand
            break

    def kern(w_ref, g_ref, m_ref, v_ref, d_ref, nm_ref, nv_ref):
        g_ = g_ref[...]
        m_ = ADAM_B1 * m_ref[...] + (1.0 - ADAM_B1) * g_
        v_ = ADAM_B2 * v_ref[...] + (1.0 - ADAM_B2) * jnp.square(g_)
        m_hat = m_ / (1.0 - ADAM_B1 ** ADAM_STEP)
        v_hat = v_ / (1.0 - ADAM_B2 ** ADAM_STEP)
        d_ref[...] = -ADAM_LR * (m_hat / (jnp.sqrt(v_hat) + ADAM_EPS) + ADAM_WD * w_ref[...])
        nm_ref[...] = m_
        nv_ref[...] = v_

    spec = pl.BlockSpec((tr, cols), lambda i: (i, 0))
    return _pc(kern, name, (rows // tr,), [spec] * 4, [spec] * 3, [SDS(w.shape, F32)] * 3)(w, g, m, v)


_WEIGHT_NAMES = ("attn_norm", "w_in", "dn_conv", "dn_a_log", "dn_dt_bias", "dn_out_norm", "swa_q_norm", "swa_k_norm",
                 "swa_sinks", "rel_bias", "w_branch_dn", "w_branch_swa", "w_out", "ffn_norm", "w_gate", "w_up",
                 "w_down")
_CONV_SH = QKVW // N_CHIPS


def kernel(x, attn_norm, w_in, dn_conv, dn_a_log, dn_dt_bias, dn_out_norm, swa_q_norm, swa_k_norm, swa_sinks, rel_bias, w_branch_dn, w_branch_swa, w_out, ffn_norm, w_gate, w_up, w_down, loss_target, m_attn_norm, m_w_in, m_dn_conv, m_dn_a_log, m_dn_dt_bias, m_dn_out_norm, m_swa_q_norm, m_swa_k_norm, m_swa_sinks, m_rel_bias, m_w_branch_dn, m_w_branch_swa, m_w_out, m_ffn_norm, m_w_gate, m_w_up, m_w_down, v_attn_norm, v_w_in, v_dn_conv, v_dn_a_log, v_dn_dt_bias, v_dn_out_norm, v_swa_q_norm, v_swa_k_norm, v_swa_sinks, v_rel_bias, v_w_branch_dn, v_w_branch_swa, v_w_out, v_ffn_norm, v_w_gate, v_w_up, v_w_down):
    w = dict(attn_norm=attn_norm, w_in=w_in, dn_conv=dn_conv, dn_a_log=dn_a_log, dn_dt_bias=dn_dt_bias,
             dn_out_norm=dn_out_norm, swa_q_norm=swa_q_norm, swa_k_norm=swa_k_norm, swa_sinks=swa_sinks,
             rel_bias=rel_bias, w_branch_dn=w_branch_dn, w_branch_swa=w_branch_swa, w_out=w_out, ffn_norm=ffn_norm,
             w_gate=w_gate, w_up=w_up, w_down=w_down)
    m = dict(attn_norm=m_attn_norm, w_in=m_w_in, dn_conv=m_dn_conv, dn_a_log=m_dn_a_log, dn_dt_bias=m_dn_dt_bias,
             dn_out_norm=m_dn_out_norm, swa_q_norm=m_swa_q_norm, swa_k_norm=m_swa_k_norm, swa_sinks=m_swa_sinks,
             rel_bias=m_rel_bias, w_branch_dn=m_w_branch_dn, w_branch_swa=m_w_branch_swa, w_out=m_w_out,
             ffn_norm=m_ffn_norm, w_gate=m_w_gate, w_up=m_w_up, w_down=m_w_down)
    v = dict(attn_norm=v_attn_norm, w_in=v_w_in, dn_conv=v_dn_conv, dn_a_log=v_dn_a_log, dn_dt_bias=v_dn_dt_bias,
             dn_out_norm=v_dn_out_norm, swa_q_norm=v_swa_q_norm, swa_k_norm=v_swa_k_norm, swa_sinks=v_swa_sinks,
             rel_bias=v_rel_bias, w_branch_dn=v_w_branch_dn, w_branch_swa=v_w_branch_swa, w_out=v_w_out,
             ffn_norm=v_ffn_norm, w_gate=v_w_gate, w_up=v_w_up, w_down=v_w_down)
    shapes = {n: w[n].shape for n in _WEIGHT_NAMES}

    def two_d(a):
        return a.reshape(a.shape[-2], a.shape[-1]) if a.ndim == 3 else a

    core = lax.axis_index("c")
    chip = 2 * lax.axis_index("x") + lax.axis_index("y")
    small_shapes = {n: two_d(w[n]).shape for n, _ in _SMALL}
    small_shapes["dn_conv"] = (CONV, QKVW)

    conv_loc = two_d(w["dn_conv"])
    conv_part = lax.dynamic_update_slice(jnp.zeros((CONV, QKVW), F32), jnp.where(core == 0, conv_loc, 0.0),
                                         (0, chip * _CONV_SH))
    conv_full = _all_sum_small(conv_part.reshape(CONV * QKVW // 128, 128), "gather_conv").reshape(CONV, QKVW)

    local_blocks = {n: two_d(w[n]) for n in _BIG_NAMES}
    w_bf = [local_blocks[n].astype(BF16) for n in _BIG_NAMES]
    (w_in_g,) = _gather_weights(w_bf[:1], chip)
    windows = _gather_windows(w_bf[1:])
    send_sems, recv_sems, w_thru, l_thru, token = _split_start(
        "gather_start", w_bf[1:], _own_slot(w_bf[1:], chip), w_in_g[0, :8, :128], windows)

    def late(after):
        lands = _split_wait("gather_wait", w_thru, l_thru, send_sems, recv_sems, after, windows)
        g = dict(zip(_BIG_NAMES[1:], _sibling_fill(lands)))
        return dict(wa=g["w_branch_dn"], wb=g["w_branch_swa"], w_out=g["w_out"].reshape(D, D), wg=g["w_gate"],
                    wu=g["w_up"], wd=g["w_down"])

    w_in_full = w_in_g.transpose(1, 0, 2).reshape(D, D_IN)
    wts = dict(w_in_p=_pad_w_in(w_in_full), dn_conv=conv_full, late=late)
    for n, _ in _SMALL[:-1]:
        wts[n] = two_d(w[n])
    wts["attn_norm"] = wts["attn_norm"] + token[0:1, 0:1]

    early = {}

    def send_early(grads):
        gs = [grads["w_branch_dn"], grads["w_branch_swa"], grads["w_out"].reshape(N_CHIPS, CSH, D), grads["w_gate"],
              grads["w_up"], grads["w_down"]]
        parts = _pair_sum(gs, _swap_halves(gs, "swap_halves_early"), core, "pair_sum_early")
        own = [lax.dynamic_index_in_dim(p, chip, axis=0, keepdims=False) for p in parts]
        early["sems"], early["recv"], early["src"], early["land"], tok = _split_start(
            "exchange_start", parts, _own_slot(own, chip), parts[0][0, :8, :128], _exchange_windows())
        return tok

    wts["send_early"] = send_early
    loss_sum, grad_x, grads = _local_step(x[0], loss_target[0], wts)

    q_early = _split_wait("exchange_wait", early["src"], early["land"], early["sems"], early["recv"],
                          grads["w_in_p"], _exchange_windows())
    g_in = [_unpad_w_in(grads["w_in_p"]).reshape(D, N_CHIPS, D_IN // N_CHIPS).transpose(1, 0, 2)]
    parts_in = _pair_sum(g_in, _swap_halves(g_in, "swap_halves_in"), core, "pair_sum_in")
    reduced = _chip_sum(list(_chip_exchange(parts_in, chip)) + list(q_early))
    theirs = _swap_reduced(reduced)
    g_blocks = {n: jnp.concatenate([jnp.where(core == 0, mine, other), jnp.where(core == 0, other, mine)], axis=0)
                for n, mine, other in zip(_BIG_NAMES, reduced, theirs)}

    small_sum = _all_sum_small(_pack_small(grads, loss_sum), "all_sum_small")
    loss = small_sum.reshape(-1)[_LOSS_OFF]
    g_small = _unpack_small(small_sum, small_shapes)

    g_out, d_out, m_out, v_out = {}, {}, {}, {}
    for n in _BIG_NAMES:
        g_out[n] = g_blocks[n].reshape(shapes[n])
        d_, m_, v_ = _adamw(local_blocks[n], g_blocks[n], two_d(m[n]), two_d(v[n]), "adamw_" + n)
        d_out[n], m_out[n], v_out[n] = d_.reshape(shapes[n]), m_.reshape(shapes[n]), v_.reshape(shapes[n])
    g_conv = lax.dynamic_slice(g_small["dn_conv"], (0, chip * _CONV_SH), (CONV, _CONV_SH))
    g_out["dn_conv"] = g_conv.reshape(shapes["dn_conv"])
    d_, m_, v_ = _adamw(conv_loc, g_conv, two_d(m["dn_conv"]), two_d(v["dn_conv"]), "adamw_dn_conv")
    d_out["dn_conv"], m_out["dn_conv"], v_out["dn_conv"] = (t.reshape(shapes["dn_conv"]) for t in (d_, m_, v_))

    def packed(src):
        vals = {n: src[n] for n, _ in _SMALL[:-1]}
        vals["dn_conv"] = jnp.zeros((CONV * QKVW,), F32)
        return _pack_small(vals)

    d_s, m_s, v_s = _adamw(packed(w), small_sum, packed(m), packed(v), "adamw_small")
    d_small, m_small, v_small = (_unpack_small(t, small_shapes) for t in (d_s, m_s, v_s))
    for n, _ in _SMALL[:-1]:
        g_out[n] = g_small[n].reshape(shapes[n])
        d_out[n], m_out[n], v_out[n] = (t[n].reshape(shapes[n]) for t in (d_small, m_small, v_small))

    return (loss, grad_x[None], *[g_out[n] for n in _WEIGHT_NAMES], *[d_out[n] for n in _WEIGHT_NAMES],
            *[m_out[n] for n in _WEIGHT_NAMES], *[v_out[n] for n in _WEIGHT_NAMES])
```

```python
import functools
import math

import numpy as np
import jax
import jax.numpy as jnp
from jax import lax
from jax.experimental import pallas as pl
from jax.experimental.pallas import tpu as pltpu

F32 = jnp.float32
BF16 = jnp.bfloat16
SDS = jax.ShapeDtypeStruct

D = 1024
DN_H = 4
DH = 128
DNW = DN_H * DH
QKVW = 3 * DNW
CONV = 4
CHUNK = 64
SWA_H = 8
SWA_KV = 2
SWA_G = SWA_H // SWA_KV
SWA_D = 64
SWAW = SWA_H * SWA_D
SWAKW = SWA_KV * SWA_D
BLK = 128
NBUCKET = 32
MAXDIST = 128
DFF = 2816
D_IN = QKVW + DNW + 2 * DN_H + SWAW + 2 * SWAKW + 2 * D
EPS = 1e-6
NEG = -1e30

ADAM_LR = 0.001
ADAM_B1 = 0.9
ADAM_B2 = 0.999
ADAM_EPS = 1e-08
ADAM_WD = 0.01
ADAM_STEP = 10

C_QKV, C_Z, C_GATE, C_SQ, C_SK, C_SV, C_BA = 0, 1536, 2048, 4096, 4608, 4736, 4864
PW = 5120
_ORIG_PIECES = (
    (0, QKVW, C_QKV),
    (QKVW, DNW, C_Z),
    (QKVW + DNW, 2 * DN_H, C_BA),
    (QKVW + DNW + 2 * DN_H, SWAW, C_SQ),
    (QKVW + DNW + 2 * DN_H + SWAW, SWAKW, C_SK),
    (QKVW + DNW + 2 * DN_H + SWAW + SWAKW, SWAKW, C_SV),
    (QKVW + DNW + 2 * DN_H + SWAW + 2 * SWAKW, 2 * D, C_GATE),
)

N_CHIPS = 4
FSH = DFF // N_CHIPS
CSH = D // N_CHIPS
VMEM_LIMIT = 48 * 1024 * 1024
MESH = pl.DeviceIdType.MESH

_BIG = (
    ("w_in", D, D_IN // N_CHIPS),
    ("w_branch_dn", DNW, CSH),
    ("w_branch_swa", SWAW, CSH),
    ("w_out", CSH, D),
    ("w_gate", D, FSH),
    ("w_up", D, FSH),
    ("w_down", FSH, D),
)
_BIG_NAMES = tuple(n for n, _, _ in _BIG)

_SMALL = (
    ("attn_norm", D), ("ffn_norm", D), ("dn_out_norm", DH), ("swa_q_norm", SWA_D), ("swa_k_norm", SWA_D),
    ("swa_sinks", SWA_H), ("dn_a_log", DN_H), ("dn_dt_bias", DN_H), ("rel_bias", NBUCKET * SWA_H),
    ("dn_conv", CONV * QKVW),
)
_SMALL_OFF = {}
_o = 0
for _n, _s in _SMALL:
    _SMALL_OFF[_n] = (_o, _s)
    _o += _s
_LOSS_OFF = _o
_SMALL_ROWS = -(-(_o + 1) // (8 * 128)) * 8


def _cparams(**kw):
    return pltpu.CompilerParams(vmem_limit_bytes=VMEM_LIMIT, **kw)


_DIMS = {
    "nn": (((1,), (0,)), ((), ())),
    "nt": (((1,), (1,)), ((), ())),
    "tn": (((0,), (0,)), ((), ())),
    "bnn": (((2,), (1,)), ((0,), (0,))),
    "bnt": (((2,), (2,)), ((0,), (0,))),
    "btn": (((1,), (1,)), ((0,), (0,))),
}


def _raw_dot(a, b, kind, exact):
    if exact:
        prec = lax.Precision.HIGH if exact == "x3" else lax.Precision.HIGHEST
        return lax.dot_general(a, b, _DIMS[kind], precision=prec, preferred_element_type=F32)
    return lax.dot_general(a.astype(BF16), b.astype(BF16), _DIMS[kind], preferred_element_type=F32)


@functools.partial(jax.custom_vjp, nondiff_argnums=(2, 3))
def _dot(a, b, kind, exact):
    return _raw_dot(a, b, kind, exact)


def _dot_fwd(a, b, kind, exact):
    return _raw_dot(a, b, kind, exact), (a, b)


def _dot_bwd(kind, exact, res, g):
    a, b = res
    pre = kind[:-2]
    nn, nt, tn = pre + "nn", pre + "nt", pre + "tn"
    if kind == nn:
        return _dot(g, b, nt, exact), _dot(a, g, tn, exact)
    if kind == nt:
        return _dot(g, b, nn, exact), _dot(g, a, tn, exact)
    return _dot(b, g, nt, exact), _dot(a, g, nn, exact)


_dot.defvjp(_dot_fwd, _dot_bwd)


def _silu(x):
    return x * jax.nn.sigmoid(x)


def _f_rms(x, gain):
    return x * lax.rsqrt(jnp.mean(x * x, axis=-1, keepdims=True) + EPS) * gain


def _f_dn_pre(xs0, xs1, xs2, xs3, ba, cw, alog, dtb):
    rows = xs0.shape[0]
    c = xs0 * cw[0:1] + xs1 * cw[1:2] + xs2 * cw[2:3] + xs3 * cw[3:4]
    qkv = _silu(c)
    qs, ks, bbs, gbs = [], [], [], []
    for h in range(DN_H):
        qh = qkv[:, h * DH:(h + 1) * DH]
        kh = qkv[:, DNW + h * DH:DNW + (h + 1) * DH]
        qs.append(qh * lax.rsqrt(jnp.sum(qh * qh, axis=-1, keepdims=True) + EPS) * (DH ** -0.5))
        ks.append(kh * lax.rsqrt(jnp.sum(kh * kh, axis=-1, keepdims=True) + EPS))
        beta = jax.nn.sigmoid(ba[:, h:h + 1])
        ar = ba[:, DN_H + h:DN_H + h + 1] + dtb[:, h:h + 1]
        softplus = jnp.maximum(ar, 0.0) + jnp.log1p(jnp.exp(-jnp.abs(ar)))
        g = -jnp.exp(alog[:, h:h + 1]) * softplus
        bbs.append(jnp.broadcast_to(beta, (rows, DH)))
        gbs.append(jnp.broadcast_to(g, (rows, DH)))
    return (jnp.concatenate(qs, axis=1), jnp.concatenate(ks, axis=1), qkv[:, 2 * DNW:],
            jnp.concatenate(bbs, axis=1), jnp.concatenate(gbs, axis=1))


def _f_dn_post(o, z, gain):
    ys = []
    for h in range(DN_H):
        oh = o[:, h * DH:(h + 1) * DH]
        zh = z[:, h * DH:(h + 1) * DH]
        ys.append(oh * lax.rsqrt(jnp.mean(oh * oh, axis=-1, keepdims=True) + EPS) * gain * _silu(zh))
    return jnp.concatenate(ys, axis=1)


def _f_merge(pa, pb, ga, gb):
    return jax.nn.sigmoid(ga) * pa + jax.nn.sigmoid(gb) * pb


def _f_swiglu(g, u):
    return _silu(g) * u


def _f_chunk(q, k, v, gb, bb, s):
    c = CHUNK
    nh = q.shape[0]
    ii = lax.broadcasted_iota(jnp.int32, (nh, c, c), 1)
    jj = lax.broadcasted_iota(jnp.int32, (nh, c, c), 2)
    incl = ii >= jj
    strict = ii > jj
    eye = (ii == jj).astype(F32)
    gcb = _dot(incl.astype(F32), gb, "bnn", True)
    lane0 = (lax.broadcasted_iota(jnp.int32, (nh, c, DH), 2) == 0).astype(F32)
    gcol = gcb[:, :, :c]
    grow = _dot(lane0, gcb, "bnt", True)
    decay = jnp.where(incl, jnp.exp(jnp.where(incl, gcol - grow, 0.0)), 0.0)
    kb = k * bb
    vb = v * bb
    a = jnp.where(strict, _dot(kb, k, "bnt", False) * decay, 0.0)
    p = -a
    t = eye + p
    for _ in range(5):
        p = _dot(p, p, "bnn", "x3")
        t = t + _dot(t, p, "bnn", "x3")
    eg = jnp.exp(gcb)
    u = _dot(t, vb, "bnn", "x3")
    w = _dot(t, kb * eg, "bnn", "x3")
    qk = jnp.where(incl, _dot(q, k, "bnt", False) * decay, 0.0)
    qe = q * eg
    glast = gcb[:, c - 1:c, :]
    k_dec = k * jnp.exp(glast - gcb)
    e_last = jnp.exp(glast)
    outs = []
    for g in range(nh // DN_H):
        sl = slice(g * DN_H, (g + 1) * DN_H)
        v_new = u[sl] - _dot(w[sl], s, "bnn", False)
        outs.append(_dot(qe[sl], s, "bnn", False) + _dot(qk[sl], v_new, "bnn", False))
        s = s * e_last[sl] + _dot(k_dec[sl], v_new, "btn", False)
    return jnp.concatenate(outs, axis=0), s


def _f_swa(q8, kp, kc, vp, vc, bias8, qg, kg, sink, mask):
    kb = jnp.concatenate([kp, kc], axis=1)
    vb = jnp.concatenate([vp, vc], axis=1)
    kn = kb * lax.rsqrt(jnp.mean(kb * kb, axis=-1, keepdims=True) + EPS) * kg

    def rows(per_head):
        return jnp.stack([jnp.concatenate([per_head(kv, g) for g in range(SWA_G)], axis=0)
                          for kv in range(SWA_KV)], axis=0)

    qq = rows(lambda kv, g: q8[kv * SWA_G + g])
    qn = qq * lax.rsqrt(jnp.mean(qq * qq, axis=-1, keepdims=True) + EPS) * qg
    lg = _dot(qn, kn, "bnt", False) * (SWA_D ** -0.5) + rows(lambda kv, g: bias8[kv * SWA_G + g])
    lg = jnp.where(rows(lambda kv, g: mask), lg, NEG)
    sk = rows(lambda kv, g: jnp.broadcast_to(sink[kv][:, g:g + 1], (BLK, 1)))
    m = lax.stop_gradient(jnp.maximum(jnp.max(lg, axis=-1, keepdims=True), sk))
    p = jnp.exp(lg - m)
    den = jnp.sum(p, axis=-1, keepdims=True) + jnp.exp(sk - m)
    out = _dot(p / den, vb, "bnn", False)
    return jnp.stack([out[kv, g * BLK:(g + 1) * BLK] for kv in range(SWA_KV) for g in range(SWA_G)], axis=0)


def _bdot(a, b, kind="nn"):
    return lax.dot_general(a.astype(BF16), b.astype(BF16), _DIMS[kind], preferred_element_type=F32)


def _pc(kern, name, grid, in_specs, out_specs, out_shape, scratch=()):
    return pl.pallas_call(
        kern, name=name, grid=grid, in_specs=in_specs, out_specs=out_specs, out_shape=out_shape,
        scratch_shapes=list(scratch), compiler_params=_cparams(dimension_semantics=("arbitrary",) * len(grid)))


def _mm(a, b, kind, out_dtype, tm, tn, name):
    if kind == "tn":
        k, m = a.shape
    else:
        m, k = a.shape
    n = b.shape[0] if kind == "nt" else b.shape[1]
    tm, tn = min(tm, m), min(tn, n)
    assert m % tm == 0 and n % tn == 0, (name, a.shape, b.shape, tm, tn)

    def kern(a_ref, b_ref, o_ref):
        o_ref[...] = _bdot(a_ref[...], b_ref[...], kind).astype(o_ref.dtype)

    a_spec = pl.BlockSpec((k, tm), lambda i, j: (0, i)) if kind == "tn" else pl.BlockSpec((tm, k), lambda i, j: (i, 0))
    b_spec = pl.BlockSpec((tn, k), lambda i, j: (j, 0)) if kind == "nt" else pl.BlockSpec((k, tn), lambda i, j: (0, j))
    return _pc(kern, name, (m // tm, n // tn), [a_spec, b_spec], pl.BlockSpec((tm, tn), lambda i, j: (i, j)),
               SDS((m, n), out_dtype))(a, b)


def _rows(body, name, m, tm, row_ins, full_ins, row_outs, acc_outs=()):
    n_r, n_f, n_o, n_a = len(row_ins), len(full_ins), len(row_outs), len(acc_outs)
    assert m % tm == 0

    def kern(*refs):
        r = refs[:n_r]
        f = refs[n_r:n_r + n_f]
        o = refs[n_r + n_f:n_r + n_f + n_o]
        acc = refs[n_r + n_f + n_o:]
        outs, sums = body([x[...] for x in r], [x[...] for x in f])
        for ref, val in zip(o, outs, strict=True):
            ref[...] = val.astype(ref.dtype)
        if n_a:
            @pl.when(pl.program_id(0) == 0)
            def _():
                for ref in acc:
                    ref[...] = jnp.zeros(ref.shape, F32)

            for ref, val in zip(acc, sums, strict=True):
                ref[...] += val

    in_specs = [pl.BlockSpec((tm, w), functools.partial(lambda i, cb: (i, cb), cb=cb)) for _, w, cb in row_ins]
    in_specs += [pl.BlockSpec(x.shape, lambda i: (0, 0)) for x in full_ins]
    out_specs = [pl.BlockSpec((tm, w), lambda i: (i, 0)) for w, _ in row_outs]
    out_specs += [pl.BlockSpec(s, lambda i: (0, 0)) for s in acc_outs]
    out_shape = [SDS((m, w), dt) for w, dt in row_outs]
    out_shape += [SDS(s, F32) for s in acc_outs]
    return _pc(kern, name, (m // tm,), in_specs, out_specs, out_shape)(*[x for x, _, _ in row_ins], *full_ins)


def _whole(x):
    return (x, x.shape[1], 0)


def _zero_first(refs):
    @pl.when(pl.program_id(0) == 0)
    def _():
        for ref in refs:
            ref[...] = jnp.zeros(ref.shape, F32)


GROUP = 4


def _heads(ref):
    return jnp.stack([ref[g * CHUNK:(g + 1) * CHUNK, h * DH:(h + 1) * DH]
                      for g in range(GROUP) for h in range(DN_H)], axis=0)


def _unheads(ref, val):
    for g in range(GROUP):
        for h in range(DN_H):
            ref[g * CHUNK:(g + 1) * CHUNK, h * DH:(h + 1) * DH] = val[g * DN_H + h]


def _dn_chunks_fwd(q, k, v, gb, bb):
    s_len = q.shape[0]
    ng = s_len // (GROUP * CHUNK)

    def kern(q_ref, k_ref, v_ref, g_ref, b_ref, o_ref, sall_ref, state):
        _zero_first([state])
        s = state[...]
        sall_ref[0] = s
        o, s_new = _f_chunk(*[_heads(r) for r in (q_ref, k_ref, v_ref, g_ref, b_ref)], s)
        _unheads(o_ref, o)
        state[...] = s_new

    blk = pl.BlockSpec((GROUP * CHUNK, DNW), lambda c: (c, 0))
    return _pc(kern, "dn_chunks_fwd", (ng,), [blk] * 5,
               [blk, pl.BlockSpec((1, DN_H, DH, DH), lambda c: (c, 0, 0, 0))],
               [SDS((s_len, DNW), F32), SDS((ng, DN_H, DH, DH), F32)],
               scratch=[pltpu.VMEM((DN_H, DH, DH), F32)])(q, k, v, gb, bb)


def _dn_chunks_bwd(q, k, v, gb, bb, s_all, d_o):
    s_len = q.shape[0]
    ng = s_len // (GROUP * CHUNK)

    def kern(q_ref, k_ref, v_ref, g_ref, b_ref, sall_ref, do_ref, dq_ref, dk_ref, dv_ref, dg_ref, db_ref, dstate):
        _zero_first([dstate])
        _, vjp = jax.vjp(_f_chunk, *[_heads(r) for r in (q_ref, k_ref, v_ref, g_ref, b_ref)], sall_ref[0])
        *d_ins, ds = vjp((_heads(do_ref), dstate[...]))
        for ref, val in zip((dq_ref, dk_ref, dv_ref, dg_ref, db_ref), d_ins, strict=True):
            _unheads(ref, val)
        dstate[...] = ds

    blk = pl.BlockSpec((GROUP * CHUNK, DNW), lambda c: (ng - 1 - c, 0))
    return _pc(kern, "dn_chunks_bwd", (ng,),
               [blk] * 5 + [pl.BlockSpec((1, DN_H, DH, DH), lambda c: (ng - 1 - c, 0, 0, 0)), blk],
               [blk] * 5, [SDS((s_len, DNW), F32)] * 5,
               scratch=[pltpu.VMEM((DN_H, DH, DH), F32)])(q, k, v, gb, bb, s_all, d_o)


def _t5_bucket_table():
    qi = np.arange(BLK)[:, None]
    kj = np.arange(2 * BLK)[None, :]
    dist = BLK + qi - kj
    n = np.maximum(dist, 0)
    max_exact = NBUCKET // 2
    nf = np.maximum(n, 1).astype(np.float32)
    large = max_exact + (np.log(nf / np.float32(max_exact)) / np.float32(math.log(MAXDIST / max_exact))
                         * np.float32(NBUCKET - max_exact)).astype(np.int32)
    large = np.minimum(large, NBUCKET - 1)
    return np.where(n < max_exact, n, large)


def _bucket_onehot_t():
    table = _t5_bucket_table().reshape(-1)
    return (np.arange(NBUCKET)[:, None] == table[None, :]).astype(np.float32)


def _swa_mask(first):
    qi = lax.broadcasted_iota(jnp.int32, (BLK, 2 * BLK), 0)
    kj = lax.broadcasted_iota(jnp.int32, (BLK, 2 * BLK), 1)
    dist = BLK + qi - kj
    window = (dist >= 0) & (dist < BLK)
    return window & ((kj >= BLK) | jnp.logical_not(first))


def _bias_expand(rel_bias_t):
    onehot = jnp.asarray(_bucket_onehot_t())

    def kern(r_ref, oh_ref, o_ref):
        o_ref[...] = _raw_dot(r_ref[...], oh_ref[...], "nn", True)

    return pl.pallas_call(
        kern, name="bias_expand", out_shape=SDS((SWA_H, BLK * 2 * BLK), F32), compiler_params=_cparams(),
    )(rel_bias_t, onehot)


def _bias_reduce(d_bias_flat):
    onehot = jnp.asarray(_bucket_onehot_t())

    def kern(d_ref, oh_ref, o_ref):
        o_ref[...] = _raw_dot(d_ref[...], oh_ref[...], "nt", True)

    return pl.pallas_call(
        kern, name="bias_reduce", out_shape=SDS((SWA_H, NBUCKET), F32), compiler_params=_cparams(),
    )(d_bias_flat, onehot)


def _swa_specs(nb, rev):
    def blk(n):
        return (nb - 1 - n) if rev else n

    def before(n):
        return jnp.maximum(blk(n) - 1, 0)

    q_spec = pl.BlockSpec((BLK, SWAW), lambda n: (blk(n), C_SQ // SWAW))
    k_cur = pl.BlockSpec((BLK, SWAKW), lambda n: (blk(n), C_SK // SWAKW))
    k_prev = pl.BlockSpec((BLK, SWAKW), lambda n: (before(n), C_SK // SWAKW))
    v_cur = pl.BlockSpec((BLK, SWAKW), lambda n: (blk(n), C_SV // SWAKW))
    v_prev = pl.BlockSpec((BLK, SWAKW), lambda n: (before(n), C_SV // SWAKW))
    bias = pl.BlockSpec((SWA_H, BLK, 2 * BLK), lambda n: (0, 0, 0))
    gain = pl.BlockSpec((1, SWA_D), lambda n: (0, 0))
    sink = pl.BlockSpec((SWA_KV, 1, SWA_G), lambda n: (0, 0, 0))
    wide = pl.BlockSpec((BLK, SWAW), lambda n: (blk(n), 0))
    narrow = pl.BlockSpec((BLK, SWAKW), lambda n: (blk(n), 0))
    return [q_spec, k_prev, k_cur, v_prev, v_cur, bias, gain, gain, sink], wide, narrow


def _split_heads(x):
    return jnp.stack([x[:, h * SWA_D:(h + 1) * SWA_D] for h in range(x.shape[1] // SWA_D)], axis=0)


def _join_heads(x):
    return jnp.concatenate([x[h] for h in range(x.shape[0])], axis=1)


def _swa_fwd(proj, bias, qg, kg, sinks):
    s_len = proj.shape[0]
    nb = s_len // BLK
    in_specs, wide, _ = _swa_specs(nb, False)

    def kern(q_ref, kp_ref, kc_ref, vp_ref, vc_ref, b_ref, qg_ref, kg_ref, s_ref, o_ref):
        mask = _swa_mask(pl.program_id(0) == 0)
        o8 = _f_swa(*[_split_heads(r[...]) for r in (q_ref, kp_ref, kc_ref, vp_ref, vc_ref)], b_ref[...], qg_ref[...],
                    kg_ref[...], s_ref[...], mask)
        o_ref[...] = _join_heads(o8).astype(BF16)

    return _pc(kern, "swa_fwd", (nb,), in_specs, wide, SDS((s_len, SWAW), BF16))(
        proj, proj, proj, proj, proj, bias, qg, kg, sinks)


def _swa_bwd(proj, bias, qg, kg, sinks, d_out):
    s_len = proj.shape[0]
    nb = s_len // BLK
    in_specs, wide, narrow = _swa_specs(nb, True)

    def kern(q_ref, kp_ref, kc_ref, vp_ref, vc_ref, b_ref, qg_ref, kg_ref, s_ref, do_ref,
             dq_ref, dk_ref, dv_ref, db_ref, dqg_ref, dkg_ref, ds_ref, carry_k, carry_v):
        n = pl.program_id(0)
        mask = _swa_mask(n == nb - 1)
        _zero_first([carry_k, carry_v, db_ref, ds_ref, dqg_ref, dkg_ref])
        fn = functools.partial(_f_swa, mask=mask)
        _, vjp = jax.vjp(fn, *[_split_heads(r[...]) for r in (q_ref, kp_ref, kc_ref, vp_ref, vc_ref)], b_ref[...],
                         qg_ref[...], kg_ref[...], s_ref[...])
        dq, dkp, dkc, dvp, dvc, dbias, dqg, dkg, dsink = vjp(_split_heads(do_ref[...]))
        dq_ref[...] = _join_heads(dq).astype(BF16)
        dk_ref[...] = (_join_heads(dkc) + carry_k[...]).astype(BF16)
        dv_ref[...] = (_join_heads(dvc) + carry_v[...]).astype(BF16)
        carry_k[...] = _join_heads(dkp)
        carry_v[...] = _join_heads(dvp)
        db_ref[...] += dbias
        dqg_ref[...] += dqg
        dkg_ref[...] += dkg
        ds_ref[...] += dsink

    bias_spec, gain, sink = in_specs[5], in_specs[6], in_specs[8]
    return _pc(
        kern, "swa_bwd", (nb,), in_specs + [wide], [wide, narrow, narrow, bias_spec, gain, gain, sink],
        [SDS((s_len, SWAW), BF16), SDS((s_len, SWAKW), BF16), SDS((s_len, SWAKW), BF16),
         SDS((SWA_H, BLK, 2 * BLK), F32), SDS((1, SWA_D), F32), SDS((1, SWA_D), F32), SDS((SWA_KV, 1, SWA_G), F32)],
        scratch=[pltpu.VMEM((BLK, SWAKW), F32), pltpu.VMEM((BLK, SWAKW), F32)],
    )(proj, proj, proj, proj, proj, bias, qg, kg, sinks, d_out)


def _branch_merge(y_dn, y_swa, wa, wb, proj):
    s_len = y_dn.shape[0]
    tm = min(512, s_len)

    def kern(ya_ref, yb_ref, wa_ref, wb_ref, ga_ref, gb_ref, pa_ref, pb_ref, m_ref):
        pa = _bdot(ya_ref[...], wa_ref[0])
        pb = _bdot(yb_ref[...], wb_ref[0])
        pa_ref[...] = pa
        pb_ref[...] = pb
        m_ref[...] = _f_merge(pa, pb, ga_ref[...], gb_ref[...]).astype(BF16)

    y_spec = pl.BlockSpec((tm, DNW), lambda i, s: (i, 0))
    w_spec = pl.BlockSpec((1, DNW, CSH), lambda i, s: (s, 0, 0))
    o_spec = pl.BlockSpec((tm, CSH), lambda i, s: (i, s))
    ga_spec = pl.BlockSpec((tm, CSH), lambda i, s: (i, C_GATE // CSH + s))
    gb_spec = pl.BlockSpec((tm, CSH), lambda i, s: (i, (C_GATE + D) // CSH + s))
    return _pc(kern, "branch_merge", (s_len // tm, N_CHIPS), [y_spec, y_spec, w_spec, w_spec, ga_spec, gb_spec],
               [o_spec] * 3, [SDS((s_len, D), F32), SDS((s_len, D), F32), SDS((s_len, D), BF16)],
               )(y_dn, y_swa, wa, wb, proj, proj)


def _out_proj(merged, w_out, x, gain):
    s_len = x.shape[0]
    tm = min(256, s_len)

    def kern(m_ref, w_ref, x_ref, g_ref, x1_ref, h2_ref):
        x1 = x_ref[...] + _bdot(m_ref[...], w_ref[...])
        x1_ref[...] = x1
        h2_ref[...] = _f_rms(x1, g_ref[...]).astype(BF16)

    row = pl.BlockSpec((tm, D), lambda i: (i, 0))
    return _pc(kern, "out_proj", (s_len // tm,),
               [row, pl.BlockSpec((D, D), lambda i: (0, 0)), row, pl.BlockSpec((1, D), lambda i: (0, 0))],
               [row, row], [SDS((s_len, D), F32), SDS((s_len, D), BF16)])(merged, w_out, x, gain)


def _ffn_up(h2, wg, wu):
    s_len = h2.shape[0]
    tm = min(512, s_len)

    def kern(h_ref, g_ref, u_ref, gt_ref, up_ref, act_ref):
        h = h_ref[...]
        g = _bdot(h, g_ref[0])
        u = _bdot(h, u_ref[0])
        gt_ref[0] = g
        up_ref[0] = u
        act_ref[0] = _f_swiglu(g, u).astype(BF16)

    w_spec = pl.BlockSpec((1, D, FSH), lambda s, i: (s, 0, 0))
    o_spec = pl.BlockSpec((1, tm, FSH), lambda s, i: (s, i, 0))
    shape = (N_CHIPS, s_len, FSH)
    return _pc(kern, "ffn_up", (N_CHIPS, s_len // tm), [pl.BlockSpec((tm, D), lambda s, i: (i, 0)), w_spec, w_spec],
               [o_spec] * 3, [SDS(shape, F32), SDS(shape, F32), SDS(shape, BF16)])(h2, wg, wu)


def _ffn_down_loss(act, wd, x1, target):
    s_len = x1.shape[0]
    tm = min(256, s_len)

    def kern(a_ref, w_ref, x_ref, t_ref, dy_ref, dyb_ref, loss_ref):
        _zero_first([loss_ref])
        y = x_ref[...]
        for s in range(N_CHIPS):
            y = y + _bdot(a_ref[s], w_ref[s])
        d = y - t_ref[...]
        dy = d * (1.0 / D)
        dy_ref[...] = dy
        dyb_ref[...] = dy.astype(BF16)
        loss_ref[...] += jnp.sum(d * d).reshape(1, 1) * (0.5 / D)

    row = pl.BlockSpec((tm, D), lambda i: (i, 0))
    return _pc(kern, "ffn_down_loss", (s_len // tm,),
               [pl.BlockSpec((N_CHIPS, tm, FSH), lambda i: (0, i, 0)),
                pl.BlockSpec((N_CHIPS, FSH, D), lambda i: (0, 0, 0)), row, row],
               [row, row, pl.BlockSpec((1, 1), lambda i: (0, 0))],
               [SDS((s_len, D), F32), SDS((s_len, D), BF16), SDS((1, 1), F32)])(act, wd, x1, target)


def _ffn_dact(dy_b, wd, gt, up):
    s_len = dy_b.shape[0]
    tm = min(512, s_len)

    def kern(dy_ref, w_ref, gt_ref, up_ref, dg_ref, du_ref):
        d_act = _bdot(dy_ref[...], w_ref[0], "nt")
        _, vjp = jax.vjp(_f_swiglu, gt_ref[0], up_ref[0])
        dg, du = vjp(d_act)
        dg_ref[0] = dg.astype(BF16)
        du_ref[0] = du.astype(BF16)

    a_spec = pl.BlockSpec((1, tm, FSH), lambda s, i: (s, i, 0))
    shape = (N_CHIPS, s_len, FSH)
    return _pc(kern, "ffn_dact", (N_CHIPS, s_len // tm),
               [pl.BlockSpec((tm, D), lambda s, i: (i, 0)), pl.BlockSpec((1, FSH, D), lambda s, i: (s, 0, 0)),
                a_spec, a_spec],
               [a_spec, a_spec], [SDS(shape, BF16), SDS(shape, BF16)])(dy_b, wd, gt, up)


def _gw_down(act, dy_b):
    s_len = dy_b.shape[0]
    tn = 512

    def kern(a_ref, g_ref, o_ref):
        o_ref[0] = _bdot(a_ref[0], g_ref[...], "tn").astype(BF16)

    return _pc(kern, "gw_down", (N_CHIPS, D // tn),
               [pl.BlockSpec((1, s_len, FSH), lambda s, j: (s, 0, 0)), pl.BlockSpec((s_len, tn), lambda s, j: (0, j))],
               pl.BlockSpec((1, FSH, tn), lambda s, j: (s, 0, j)), SDS((N_CHIPS, FSH, D), BF16))(act, dy_b)


def _gw_gate_up(h2, d_gt, d_up):
    s_len = h2.shape[0]
    tk = 256

    def kern(h_ref, dg_ref, du_ref, og_ref, ou_ref):
        h = h_ref[...]
        og_ref[0] = _bdot(h, dg_ref[0], "tn").astype(BF16)
        ou_ref[0] = _bdot(h, du_ref[0], "tn").astype(BF16)

    d_spec = pl.BlockSpec((1, s_len, FSH), lambda s, j: (s, 0, 0))
    o_spec = pl.BlockSpec((1, tk, FSH), lambda s, j: (s, j, 0))
    shape = (N_CHIPS, D, FSH)
    return _pc(kern, "gw_gate_up", (N_CHIPS, D // tk), [pl.BlockSpec((s_len, tk), lambda s, j: (0, j)), d_spec, d_spec],
               [o_spec, o_spec], [SDS(shape, BF16), SDS(shape, BF16)])(h2, d_gt, d_up)


def _ffn_dh2(d_gt, d_up, wg, wu, x1, dy, gain):
    s_len = x1.shape[0]
    tm = min(256, s_len)

    def kern(dg_ref, du_ref, wg_ref, wu_ref, x_ref, dy_ref, g_ref, dx_ref, dxb_ref, dgain_ref):
        _zero_first([dgain_ref])
        dh2 = jnp.zeros((tm, D), F32)
        for s in range(N_CHIPS):
            dh2 = dh2 + _bdot(dg_ref[s], wg_ref[s], "nt") + _bdot(du_ref[s], wu_ref[s], "nt")
        _, vjp = jax.vjp(_f_rms, x_ref[...], g_ref[...])
        dx, dgain = vjp(dh2)
        dx1 = dx + dy_ref[...]
        dx_ref[...] = dx1
        dxb_ref[...] = dx1.astype(BF16)
        dgain_ref[...] += dgain

    row = pl.BlockSpec((tm, D), lambda i: (i, 0))
    d_spec = pl.BlockSpec((N_CHIPS, tm, FSH), lambda i: (0, i, 0))
    w_spec = pl.BlockSpec((N_CHIPS, D, FSH), lambda i: (0, 0, 0))
    vec = pl.BlockSpec((1, D), lambda i: (0, 0))
    return _pc(kern, "ffn_dh2", (s_len // tm,), [d_spec, d_spec, w_spec, w_spec, row, row, vec],
               [row, row, vec], [SDS((s_len, D), F32), SDS((s_len, D), BF16), SDS((1, D), F32)],
               )(d_gt, d_up, wg, wu, x1, dy, gain)


def _merge_bwd(dx1_b, w_out, pa, pb, proj):
    s_len = dx1_b.shape[0]
    tm = min(256, s_len)

    def kern(dx_ref, w_ref, pa_ref, pb_ref, g_ref, dpa_ref, dpb_ref, dg_ref):
        dm = _bdot(dx_ref[...], w_ref[...], "nt")
        gates = g_ref[...]
        _, vjp = jax.vjp(_f_merge, pa_ref[...], pb_ref[...], gates[:, :D], gates[:, D:])
        dpa, dpb, dga, dgb = vjp(dm)
        dpa_ref[...] = dpa.astype(BF16)
        dpb_ref[...] = dpb.astype(BF16)
        dg_ref[:, :D] = dga.astype(BF16)
        dg_ref[:, D:] = dgb.astype(BF16)

    row = pl.BlockSpec((tm, D), lambda i: (i, 0))
    return _pc(kern, "merge_bwd", (s_len // tm,),
               [row, pl.BlockSpec((D, D), lambda i: (0, 0)), row, row,
                pl.BlockSpec((tm, 2 * D), lambda i: (i, C_GATE // (2 * D)))],
               [row, row, pl.BlockSpec((tm, 2 * D), lambda i: (i, 0))],
               [SDS((s_len, D), BF16), SDS((s_len, D), BF16), SDS((s_len, 2 * D), BF16)],
               )(dx1_b, w_out, pa, pb, proj)


def _d_branch(d_pa, d_pb, wa, wb):
    s_len = d_pa.shape[0]
    tm = min(512, s_len)

    def kern(da_ref, db_ref, wa_ref, wb_ref, oa_ref, ob_ref):
        acc_a = jnp.zeros((tm, DNW), F32)
        acc_b = jnp.zeros((tm, SWAW), F32)
        for s in range(N_CHIPS):
            acc_a = acc_a + _bdot(da_ref[:, s * CSH:(s + 1) * CSH], wa_ref[s], "nt")
            acc_b = acc_b + _bdot(db_ref[:, s * CSH:(s + 1) * CSH], wb_ref[s], "nt")
        oa_ref[...] = acc_a
        ob_ref[...] = acc_b

    row = pl.BlockSpec((tm, D), lambda i: (i, 0))
    w_spec = pl.BlockSpec((N_CHIPS, DNW, CSH), lambda i: (0, 0, 0))
    out = pl.BlockSpec((tm, DNW), lambda i: (i, 0))
    return _pc(kern, "d_branch", (s_len // tm,), [row, row, w_spec, w_spec], [out, out],
               [SDS((s_len, DNW), F32), SDS((s_len, SWAW), F32)])(d_pa, d_pb, wa, wb)


def _gw_branch(y_dn, y_swa, d_pa, d_pb):
    s_len = y_dn.shape[0]

    def kern(ya_ref, yb_ref, da_ref, db_ref, oa_ref, ob_ref):
        oa_ref[0] = _bdot(ya_ref[...], da_ref[...], "tn").astype(BF16)
        ob_ref[0] = _bdot(yb_ref[...], db_ref[...], "tn").astype(BF16)

    y_spec = pl.BlockSpec((s_len, DNW), lambda s: (0, 0))
    d_spec = pl.BlockSpec((s_len, CSH), lambda s: (0, s))
    o_spec = pl.BlockSpec((1, DNW, CSH), lambda s: (s, 0, 0))
    shape = (N_CHIPS, DNW, CSH)
    return _pc(kern, "gw_branch", (N_CHIPS,), [y_spec, y_spec, d_spec, d_spec], [o_spec, o_spec],
               [SDS(shape, BF16), SDS(shape, BF16)])(y_dn, y_swa, d_pa, d_pb)


def _dh_rms(d_proj, w_in_p, x, dx1, gain):
    s_len = x.shape[0]
    tm = min(256, s_len)

    def kern(dp_ref, w_ref, x_ref, r_ref, g_ref, gx_ref, dgain_ref):
        _zero_first([dgain_ref])
        dh = _bdot(dp_ref[...], w_ref[...], "nt")
        _, vjp = jax.vjp(_f_rms, x_ref[...], g_ref[...])
        dx, dgain = vjp(dh)
        gx_ref[...] = dx + r_ref[...]
        dgain_ref[...] += dgain

    row = pl.BlockSpec((tm, D), lambda i: (i, 0))
    vec = pl.BlockSpec((1, D), lambda i: (0, 0))
    return _pc(kern, "dh_rms", (s_len // tm,),
               [pl.BlockSpec((tm, PW), lambda i: (i, 0)), pl.BlockSpec((D, PW), lambda i: (0, 0)), row, row, vec],
               [row, vec], [SDS((s_len, D), F32), SDS((1, D), F32)])(d_proj, w_in_p, x, dx1, gain)


HALO = 8


def _conv_taps(cur_ref, prev_ref, halo, first):
    tm = cur_ref.shape[0]
    halo[0:HALO, :] = jnp.where(first, 0.0, prev_ref[...])
    halo[HALO:, :] = cur_ref[...]
    return [halo[HALO - n:HALO - n + tm, :] for n in range(CONV - 1, 0, -1)] + [cur_ref[...]]


def _dn_pre_specs(s_len, tm, blk):
    cur = pl.BlockSpec((tm, QKVW), lambda i: (blk(i), 0))
    prev = pl.BlockSpec((HALO, QKVW), lambda i: (jnp.maximum(blk(i) * (tm // HALO) - 1, 0), 0))
    ba = pl.BlockSpec((tm, 128), lambda i: (blk(i), C_BA // 128))
    row = pl.BlockSpec((tm, DNW), lambda i: (blk(i), 0))
    full = [pl.BlockSpec((CONV, QKVW), lambda i: (0, 0)), pl.BlockSpec((1, DN_H), lambda i: (0, 0)),
            pl.BlockSpec((1, DN_H), lambda i: (0, 0))]
    return cur, prev, ba, row, full


def _dn_pre_fwd(proj, conv_w, alog, dtb):
    s_len = proj.shape[0]
    tm = min(128, s_len)
    cur, prev, ba, row, full = _dn_pre_specs(s_len, tm, lambda i: i)

    def kern(cur_ref, prev_ref, ba_ref, cw_ref, al_ref, dt_ref, q_ref, k_ref, v_ref, bb_ref, gb_ref, halo):
        xs = _conv_taps(cur_ref, prev_ref, halo, pl.program_id(0) == 0)
        outs = _f_dn_pre(*xs, ba_ref[...], cw_ref[...], al_ref[...], dt_ref[...])
        for ref, val in zip((q_ref, k_ref, v_ref, bb_ref, gb_ref), outs, strict=True):
            ref[...] = val

    return _pc(kern, "dn_pre_fwd", (s_len // tm,), [cur, prev, ba] + full, [row] * 5, [SDS((s_len, DNW), F32)] * 5,
               scratch=[pltpu.VMEM((tm + HALO, QKVW), F32)])(proj, proj, proj, conv_w, alog, dtb)


def _dn_pre_bwd(proj, conv_w, alog, dtb, cots):
    s_len = proj.shape[0]
    tm = min(128, s_len)
    nb = s_len // tm
    cur, prev, ba, row, full = _dn_pre_specs(s_len, tm, lambda i: nb - 1 - i)

    def kern(cur_ref, prev_ref, ba_ref, cw_ref, al_ref, dt_ref, dq_ref, dk_ref, dv_ref, dbb_ref, dgb_ref,
             dqkv_ref, dba_ref, dcw_ref, dal_ref, ddt_ref, halo, *tails):
        i = pl.program_id(0)
        _zero_first([dcw_ref, dal_ref, ddt_ref])

        @pl.when(i == 0)
        def _():
            for t in tails:
                t[tm:, :] = jnp.zeros((HALO, QKVW), F32)

        xs = _conv_taps(cur_ref, prev_ref, halo, i == nb - 1)
        _, vjp = jax.vjp(_f_dn_pre, *xs, ba_ref[...], cw_ref[...], al_ref[...], dt_ref[...])
        *dxs, dba, dcw, dal, ddt = vjp((dq_ref[...], dk_ref[...], dv_ref[...], dbb_ref[...], dgb_ref[...]))
        total = dxs[CONV - 1]
        for j, t in enumerate(tails):
            n = CONV - 1 - j
            t[0:tm, :] = dxs[j]
            total = total + t[n:n + tm, :]
            t[tm:, :] = dxs[j][0:HALO, :]
        dqkv_ref[...] = total.astype(BF16)
        dba_ref[...] = dba.astype(BF16)
        dcw_ref[...] += dcw
        dal_ref[...] += dal
        ddt_ref[...] += ddt

    return _pc(kern, "dn_pre_bwd", (nb,), [cur, prev, ba] + full + [row] * 5,
               [cur, pl.BlockSpec((tm, 128), lambda i: (nb - 1 - i, 0))] + full,
               [SDS((s_len, QKVW), BF16), SDS((s_len, 128), BF16), SDS((CONV, QKVW), F32), SDS((1, DN_H), F32),
                SDS((1, DN_H), F32)],
               scratch=[pltpu.VMEM((tm + HALO, QKVW), F32)] * CONV)(proj, proj, proj, conv_w, alog, dtb, *cots)


def _pad_w_in(w_in):
    pieces = [w_in[:, o0:o0 + w] for o0, w, _ in sorted(_ORIG_PIECES, key=lambda t: t[2])]
    pieces.append(jnp.zeros((w_in.shape[0], PW - D_IN), w_in.dtype))
    return jnp.concatenate(pieces, axis=1)


def _unpad_w_in(g):
    return jnp.concatenate([g[:, p0:p0 + w] for _, w, p0 in _ORIG_PIECES], axis=1)


def _local_step(x, target, wts):
    s_len = x.shape[0]
    tm = min(256, s_len)
    tmh = min(128, s_len)
    w_in_p = wts["w_in_p"]
    attn_gain = wts["attn_norm"]
    ffn_gain = wts["ffn_norm"]
    conv_w = wts["dn_conv"]
    alog, dtb, out_gain = wts["dn_a_log"], wts["dn_dt_bias"], wts["dn_out_norm"]
    qg, kg = wts["swa_q_norm"], wts["swa_k_norm"]
    sinks = wts["swa_sinks"].reshape(SWA_KV, 1, SWA_G)

    (h,) = _rows(lambda r, f: ([_f_rms(r[0], f[0])], []), "rms1_fwd", s_len, tm, [_whole(x)], [attn_gain],
                 [(D, BF16)])
    proj = _mm(h, w_in_p, "nn", F32, 512, 1024, "mm_proj")
    q_dn, k_dn, v_dn, bb, gb = _dn_pre_fwd(proj, conv_w, alog, dtb)
    o_dn, s_all = _dn_chunks_fwd(q_dn, k_dn, v_dn, gb, bb)
    post_ins = [_whole(o_dn), (proj, DNW, C_Z // DNW)]
    (y_dn,) = _rows(lambda r, f: ([_f_dn_post(r[0], r[1], f[0])], []), "dn_post_fwd", s_len, tm, post_ins,
                    [out_gain], [(DNW, BF16)])

    bias = _bias_expand(wts["rel_bias"].T).reshape(SWA_H, BLK, 2 * BLK)
    y_swa = _swa_fwd(proj, bias, qg, kg, sinks)

    wts = {**wts, **wts["late"](y_swa)}
    p_a, p_b, merged = _branch_merge(y_dn, y_swa, wts["wa"], wts["wb"], proj)
    x1, h2 = _out_proj(merged, wts["w_out"], x, ffn_gain)
    gt, up, act = _ffn_up(h2, wts["wg"], wts["wu"])
    dy, dy_b, loss = _ffn_down_loss(act, wts["wd"], x1, target)

    grads = {}
    d_gt, d_up = _ffn_dact(dy_b, wts["wd"], gt, up)
    grads["w_down"] = _gw_down(act, dy_b)
    grads["w_gate"], grads["w_up"] = _gw_gate_up(h2, d_gt, d_up)
    dx1, dx1_b, grads["ffn_norm"] = _ffn_dh2(d_gt, d_up, wts["wg"], wts["wu"], x1, dy, ffn_gain)
    grads["w_out"] = _mm(merged, dx1_b, "tn", BF16, 512, 512, "gw_out")
    d_pa, d_pb, d_gr = _merge_bwd(dx1_b, wts["w_out"], p_a, p_b, proj)
    d_ydn, d_yswa = _d_branch(d_pa, d_pb, wts["wa"], wts["wb"])
    grads["w_branch_dn"], grads["w_branch_swa"] = _gw_branch(y_dn, y_swa, d_pa, d_pb)
    token = wts["send_early"](grads)
    qg_t = qg + token[0:1, 0:1]
    out_gain_t = out_gain + token[0:1, 0:1]

    d_sq, d_sk, d_sv, d_bias, grads["swa_q_norm"], grads["swa_k_norm"], d_sinks = _swa_bwd(
        proj, bias, qg_t, kg, sinks, d_yswa)
    grads["swa_sinks"] = d_sinks.reshape(1, SWA_H)
    grads["rel_bias"] = _bias_reduce(d_bias.reshape(SWA_H, BLK * 2 * BLK)).T

    def post_bwd(r, f):
        _, vjp = jax.vjp(_f_dn_post, r[0], r[1], f[0])
        d_o, d_z, d_gain = vjp(r[2])
        return [d_o, d_z], [d_gain]

    d_o, d_z, grads["dn_out_norm"] = _rows(post_bwd, "dn_post_bwd", s_len, tm, post_ins + [_whole(d_ydn)], [out_gain_t],
                                           [(DNW, F32), (DNW, BF16)], [(1, DH)])
    d_q, d_k, d_v, d_gb, d_bb = _dn_chunks_bwd(q_dn, k_dn, v_dn, gb, bb, s_all, d_o)

    d_qkv, d_ba, grads["dn_conv"], grads["dn_a_log"], grads["dn_dt_bias"] = _dn_pre_bwd(
        proj, conv_w, alog, dtb, (d_q, d_k, d_v, d_bb, d_gb))

    d_proj = jnp.concatenate(
        [d_qkv, d_z, d_gr, d_sq, d_sk, d_sv, d_ba, jnp.zeros((s_len, PW - C_BA - 128), BF16)], axis=1)
    grads["w_in_p"] = _mm(h, d_proj, "tn", BF16, 512, 1024, "gw_in")
    grad_x, grads["attn_norm"] = _dh_rms(d_proj, w_in_p, x, dx1, attn_gain)
    return loss, grad_x, grads


_HBM = pl.BlockSpec(memory_space=pl.ANY)


def _place():
    return lax.axis_index("x"), lax.axis_index("y"), lax.axis_index("c")


def _other_chips(x, y):
    return [(1 - x, y), (x, 1 - y), (1 - x, 1 - y)]


def _rcopy(src, dst, send_sems, recv_sems, k, to):
    return pltpu.make_async_remote_copy(src_ref=src, dst_ref=dst, send_sem=send_sems.at[k], recv_sem=recv_sems.at[k],
                                        device_id=to, device_id_type=MESH)


def _comm_call(body, name, ins, out_shapes, n_remote, landing=0):
    first = len(ins) - landing
    return pl.pallas_call(
        body, name=name, in_specs=[_HBM] * len(ins), out_specs=[_HBM] * len(out_shapes), out_shape=out_shapes,
        scratch_shapes=[pltpu.SemaphoreType.DMA((n_remote,)), pltpu.SemaphoreType.DMA((n_remote,))],
        input_output_aliases={first + i: i for i in range(landing)},
        compiler_params=_cparams(has_side_effects=True),
    )(*ins)


def _own_slot(blocks, chip):
    return [lax.dynamic_update_slice(jnp.zeros((N_CHIPS,) + b.shape, b.dtype), b[None], (chip, 0, 0)) for b in blocks]


def _gather_weights(ws, chip):
    n = len(ws)
    halves = [w.shape[0] // 2 for w in ws]

    def body(*refs):
        w_refs, o_refs = refs[:n], refs[2 * n:3 * n]
        send_sems, recv_sems = refs[3 * n:]
        x, y, c = _place()
        s = 2 * x + y
        sib = (x, y, 1 - c)
        chips = _other_chips(x, y)

        def rows(i, half):
            return pl.ds(half * halves[i], halves[i])

        first = []
        for j, (cx, cy) in enumerate(chips):
            for i in range(n):
                cp = _rcopy(w_refs[i].at[rows(i, c), :], o_refs[i].at[s, rows(i, c), :], send_sems, recv_sems,
                            j * n + i, (cx, cy, c))
                cp.start()
                first.append(cp)
        passed = []
        for j, (cx, cy) in enumerate(chips):
            sj = 2 * cx + cy
            for i in range(n):
                blk = o_refs[i].at[sj, rows(i, c), :]
                _rcopy(blk, blk, send_sems, recv_sems, j * n + i, (cx, cy, c)).wait_recv()
                cp = _rcopy(blk, blk, send_sems, recv_sems, (3 + j) * n + i, sib)
                cp.start()
                passed.append(cp)
        for j, (cx, cy) in enumerate(chips):
            sj = 2 * cx + cy
            for i in range(n):
                blk = o_refs[i].at[sj, rows(i, 1 - c), :]
                _rcopy(blk, blk, send_sems, recv_sems, (3 + j) * n + i, sib).wait_recv()
        for cp in first + passed:
            cp.wait_send()

    return _comm_call(body, "gather_weights", list(ws) + _own_slot(ws, chip),
                      [SDS((N_CHIPS,) + w.shape, w.dtype) for w in ws], 6 * n, landing=n)


_HBM_ONLY = pl.BlockSpec(memory_space=pltpu.HBM)
_SEM = pl.BlockSpec(memory_space=pltpu.SEMAPHORE)
_DATAFLOW = pltpu.SideEffectType.DATAFLOW_SIDE_EFFECTING


def _in_hbm(a):
    return pltpu.with_memory_space_constraint(a, pltpu.HBM)


def _gather_windows(blocks):
    halves = [b.shape[0] // 2 for b in blocks]

    def src_at(ref, i, c, sj):
        return ref.at[pl.ds(c * halves[i], halves[i]), :]

    def dst_at(ref, i, c, s_from):
        return ref.at[s_from, pl.ds(c * halves[i], halves[i]), :]

    return src_at, dst_at


def _exchange_windows():
    return (lambda ref, i, c, sj: ref.at[sj]), (lambda ref, i, c, s_from: ref.at[s_from])


def _split_start(name, ws, lands, dep, windows):
    n = len(ws)
    src_at, dst_at = windows

    def body(*refs):
        w_refs, l_refs = refs[:n], refs[n:2 * n]
        send_sems, recv_sems = refs[2 * n + 1], refs[2 * n + 2]
        token = refs[-1]
        x, y, c = _place()
        s = 2 * x + y
        for j, (cx, cy) in enumerate(_other_chips(x, y)):
            for i in range(n):
                _rcopy(src_at(w_refs[i], i, c, 2 * cx + cy), dst_at(l_refs[i], i, c, s), send_sems, recv_sems,
                       j * n + i, (cx, cy, c)).start()
        token[...] = jnp.zeros_like(token)

    outs = pl.pallas_call(
        body, name=name,
        out_shape=(pltpu.SemaphoreType.DMA((3 * n,)), pltpu.SemaphoreType.DMA((3 * n,)),
                   *[pltpu.HBM(w.shape, w.dtype) for w in ws], *[pltpu.HBM(t.shape, t.dtype) for t in lands],
                   SDS((8, 128), F32)),
        in_specs=[_HBM_ONLY] * (2 * n) + [pl.BlockSpec(memory_space=pl.ANY)],
        out_specs=(_SEM, _SEM, *[_HBM_ONLY] * (2 * n), pl.BlockSpec(memory_space=pltpu.VMEM)),
        input_output_aliases={i: 2 + i for i in range(2 * n)},
        compiler_params=pltpu.CompilerParams(has_side_effects=_DATAFLOW),
    )(*[_in_hbm(w) for w in ws], *[_in_hbm(t) for t in lands], dep)
    return outs[0], outs[1], outs[2:2 + n], outs[2 + n:2 + 2 * n], outs[-1]


def _split_wait(name, w_thru, l_thru, send_sems, recv_sems, after, windows):
    n = len(w_thru)
    src_at, dst_at = windows

    def body(*refs):
        w_refs, l_refs = refs[:n], refs[n:2 * n]
        send_sems, recv_sems = refs[2 * n], refs[2 * n + 1]
        x, y, c = _place()
        for j, (cx, cy) in enumerate(_other_chips(x, y)):
            sj = 2 * cx + cy
            for i in range(n):
                cp = _rcopy(src_at(w_refs[i], i, c, sj), dst_at(l_refs[i], i, c, sj), send_sems, recv_sems, j * n + i,
                            (cx, cy, c))
                cp.wait_send()
                cp.wait_recv()

    outs = pl.pallas_call(
        body, name=name,
        out_shape=[pltpu.HBM(w.shape, w.dtype) for w in w_thru] + [pltpu.HBM(t.shape, t.dtype) for t in l_thru],
        in_specs=[_HBM_ONLY] * (2 * n) + [_SEM, _SEM, pl.BlockSpec(memory_space=pl.ANY)],
        out_specs=[_HBM_ONLY] * (2 * n),
        input_output_aliases={i: i for i in range(2 * n)},
        compiler_params=pltpu.CompilerParams(has_side_effects=_DATAFLOW),
    )(*w_thru, *l_thru, send_sems, recv_sems, after)
    return outs[n:]


def _sibling_fill(lands):
    n = len(lands)
    halves = [t.shape[1] // 2 for t in lands]

    def body(*refs):
        o_refs = refs[n:2 * n]
        send_sems, recv_sems = refs[2 * n:]
        x, y, c = _place()
        sib = (x, y, 1 - c)
        chips = _other_chips(x, y)
        sent = []
        for j, (cx, cy) in enumerate(chips):
            for i in range(n):
                blk = o_refs[i].at[2 * cx + cy, pl.ds(c * halves[i], halves[i]), :]
                cp = _rcopy(blk, blk, send_sems, recv_sems, j * n + i, sib)
                cp.start()
                sent.append(cp)
        for j, (cx, cy) in enumerate(chips):
            for i in range(n):
                blk = o_refs[i].at[2 * cx + cy, pl.ds((1 - c) * halves[i], halves[i]), :]
                _rcopy(blk, blk, send_sems, recv_sems, j * n + i, sib).wait_recv()
        for cp in sent:
            cp.wait_send()

    return _comm_call(body, "sibling_fill", list(lands), [SDS(t.shape, t.dtype) for t in lands], 3 * n, landing=n)


def _swap_halves(gs, name):
    n = len(gs)
    halves = [g.shape[1] // 2 for g in gs]

    def body(*refs):
        g_refs, o_refs = refs[:n], refs[n:2 * n]
        send_sems, recv_sems = refs[2 * n:]
        x, y, c = _place()
        cps = [_rcopy(g_refs[i].at[:, pl.ds((1 - c) * halves[i], halves[i]), :], o_refs[i], send_sems, recv_sems, i,
                      (x, y, 1 - c)) for i in range(n)]
        for cp in cps:
            cp.start()
        for cp in cps:
            cp.wait()

    return _comm_call(body, name, gs, [SDS((N_CHIPS, h, g.shape[2]), g.dtype) for g, h in zip(gs, halves)], n)


def _chip_exchange(ps, chip):
    n = len(ps)

    def body(*refs):
        p_refs, o_refs = refs[:n], refs[2 * n:3 * n]
        send_sems, recv_sems = refs[3 * n:]
        x, y, c = _place()
        s = 2 * x + y
        chips = _other_chips(x, y)
        sent = []
        for j, (cx, cy) in enumerate(chips):
            for i in range(n):
                cp = _rcopy(p_refs[i].at[2 * cx + cy], o_refs[i].at[s], send_sems, recv_sems, j * n + i, (cx, cy, c))
                cp.start()
                sent.append(cp)
        for j, (cx, cy) in enumerate(chips):
            sj = 2 * cx + cy
            for i in range(n):
                _rcopy(p_refs[i].at[sj], o_refs[i].at[sj], send_sems, recv_sems, j * n + i, (cx, cy, c)).wait_recv()
        for cp in sent:
            cp.wait_send()

    own = [lax.dynamic_index_in_dim(p, chip, axis=0, keepdims=False) for p in ps]
    return _comm_call(body, "chip_exchange", list(ps) + _own_slot(own, chip), [SDS(p.shape, p.dtype) for p in ps],
                      3 * n, landing=n)


def _swap_reduced(rs):
    n = len(rs)

    def body(*refs):
        r_refs, o_refs = refs[:n], refs[n:2 * n]
        send_sems, recv_sems = refs[2 * n:]
        x, y, c = _place()
        cps = [_rcopy(r_refs[i], o_refs[i], send_sems, recv_sems, i, (x, y, 1 - c)) for i in range(n)]
        for cp in cps:
            cp.start()
        for cp in cps:
            cp.wait()

    return _comm_call(body, "swap_reduced", rs, [SDS(r.shape, r.dtype) for r in rs], n)


def _all_sum_small(vec, name):
    n_dev = 8
    flips = [(bx, by, bc) for bx in (0, 1) for by in (0, 1) for bc in (0, 1)][1:]

    def body(v_ref, out_ref, gath, send_sems, recv_sems):
        x, y, c = _place()
        me = 4 * x + 2 * y + c
        gath[me] = v_ref[...]
        sent = []
        for k, (bx, by, bc) in enumerate(flips):
            peer = (x ^ bx, y ^ by, c ^ bc)
            cp = _rcopy(v_ref, gath.at[me], send_sems, recv_sems, k, peer)
            cp.start()
            sent.append(cp)
        for k, (bx, by, bc) in enumerate(flips):
            peer = (x ^ bx, y ^ by, c ^ bc)
            _rcopy(v_ref, gath.at[4 * peer[0] + 2 * peer[1] + peer[2]], send_sems, recv_sems, k, peer).wait_recv()
        for cp in sent:
            cp.wait_send()
        acc = gath[0]
        for d in range(1, n_dev):
            acc = acc + gath[d]
        out_ref[...] = acc

    vm = pl.BlockSpec(memory_space=pltpu.VMEM)
    return pl.pallas_call(
        body, name=name, in_specs=[vm], out_specs=vm, out_shape=SDS(vec.shape, F32),
        scratch_shapes=[pltpu.VMEM((n_dev,) + vec.shape, F32), pltpu.SemaphoreType.DMA((7,)),
                        pltpu.SemaphoreType.DMA((7,))],
        compiler_params=_cparams(has_side_effects=True),
    )(vec)


def _pack_small(vals, extra=None):
    parts = [vals[n].reshape(-1).astype(F32) for n, _ in _SMALL]
    parts.append(jnp.zeros((1,), F32) if extra is None else extra.reshape(1).astype(F32))
    flat = jnp.concatenate(parts)
    flat = jnp.concatenate([flat, jnp.zeros((_SMALL_ROWS * 128 - flat.shape[0],), F32)])
    return flat.reshape(_SMALL_ROWS, 128)


def _unpack_small(packed, shapes):
    flat = packed.reshape(-1)
    return {n: flat[_SMALL_OFF[n][0]:_SMALL_OFF[n][0] + _SMALL_OFF[n][1]].reshape(shapes[n]) for n, _ in _SMALL}


def _pair_sum(gs, gots, core, name):
    n = len(gs)

    def kern(c_ref, *refs):
        for i in range(n):
            refs[2 * n + i][...] = (refs[i][...].astype(F32) + refs[n + i][...].astype(F32)).astype(BF16)

    in_specs = [pl.BlockSpec((1, t.shape[1], t.shape[2]), lambda s, c_ref: (s, c_ref[0], 0)) for t in gots]
    in_specs += [pl.BlockSpec((1, t.shape[1], t.shape[2]), lambda s, c_ref: (s, 0, 0)) for t in gots]
    out_specs = [pl.BlockSpec((1, t.shape[1], t.shape[2]), lambda s, c_ref: (s, 0, 0)) for t in gots]
    return pl.pallas_call(
        kern, name=name,
        grid_spec=pltpu.PrefetchScalarGridSpec(num_scalar_prefetch=1, grid=(N_CHIPS,), in_specs=in_specs,
                                               out_specs=out_specs),
        out_shape=[SDS(t.shape, BF16) for t in gots],
        compiler_params=_cparams(dimension_semantics=("arbitrary",)),
    )(core.reshape(1).astype(jnp.int32), *gs, *gots)


def _chip_sum(qs):
    n = len(qs)

    def kern(*refs):
        for i in range(n):
            acc = refs[i][0].astype(F32)
            for s in range(1, N_CHIPS):
                acc = acc + refs[i][s].astype(F32)
            refs[n + i][...] = acc

    in_specs = [pl.BlockSpec((N_CHIPS, q.shape[1] // 2, q.shape[2]), lambda j: (0, j, 0)) for q in qs]
    out_specs = [pl.BlockSpec((q.shape[1] // 2, q.shape[2]), lambda j: (j, 0)) for q in qs]
    return _pc(kern, "chip_sum", (2,), in_specs, out_specs, [SDS(q.shape[1:], F32) for q in qs])(*qs)


def _adam_math(w_, g_, m_, v_):
    m_ = ADAM_B1 * m_ + (1.0 - ADAM_B1) * g_
    v_ = ADAM_B2 * v_ + (1.0 - ADAM_B2) * jnp.square(g_)
    m_hat = m_ / (1.0 - ADAM_B1 ** ADAM_STEP)
    v_hat = v_ / (1.0 - ADAM_B2 ** ADAM_STEP)
    return -ADAM_LR * (m_hat / (jnp.sqrt(v_hat) + ADAM_EPS) + ADAM_WD * w_), m_, v_


def _adamw(w, g, m, v, name):
    rows, cols = w.shape
    tr = rows
    for cand in (256, 128, 64, 32, 16, 8):
        if rows % cand == 0 and rows > cand:
            tr = cand
            break

    def kern(w_ref, g_ref, m_ref, v_ref, d_ref, nm_ref, nv_ref):
        d_ref[...], nm_ref[...], nv_ref[...] = _adam_math(w_ref[...], g_ref[...], m_ref[...], v_ref[...])

    spec = pl.BlockSpec((tr, cols), lambda i: (i, 0))
    return _pc(kern, name, (rows // tr,), [spec] * 4, [spec] * 3, [SDS(w.shape, F32)] * 3)(w, g, m, v)


def _adamw_big(w, mine, theirs, m, v, core, name):
    _, rows, cols = w.shape
    half = rows // 2
    tr = next(t for t in (256, 176, 128, 64, 32, 16, 8) if half % t == 0)
    nbh = half // tr

    def kern(c_ref, w_ref, a_ref, b_ref, m_ref, v_ref, g_ref, d_ref, nm_ref, nv_ref):
        g_ = jnp.where(pl.program_id(0) // nbh == c_ref[0], a_ref[...], b_ref[...])
        g_ref[0] = g_
        d_ref[0], nm_ref[0], nv_ref[0] = _adam_math(w_ref[0], g_, m_ref[0], v_ref[0])

    full = pl.BlockSpec((1, tr, cols), lambda i, c_ref: (0, i, 0))
    part = pl.BlockSpec((tr, cols), lambda i, c_ref: (i % nbh, 0))
    return pl.pallas_call(
        kern, name=name,
        grid_spec=pltpu.PrefetchScalarGridSpec(num_scalar_prefetch=1, grid=(rows // tr,),
                                               in_specs=[full, part, part, full, full], out_specs=[full] * 4),
        out_shape=[SDS(w.shape, F32)] * 4,
        compiler_params=_cparams(dimension_semantics=("arbitrary",)),
    )(core.reshape(1).astype(jnp.int32), w, mine, theirs, m, v)


_WEIGHT_NAMES = ("attn_norm", "w_in", "dn_conv", "dn_a_log", "dn_dt_bias", "dn_out_norm", "swa_q_norm", "swa_k_norm",
                 "swa_sinks", "rel_bias", "w_branch_dn", "w_branch_swa", "w_out", "ffn_norm", "w_gate", "w_up",
                 "w_down")
_CONV_SH = QKVW // N_CHIPS


def kernel(x, attn_norm, w_in, dn_conv, dn_a_log, dn_dt_bias, dn_out_norm, swa_q_norm, swa_k_norm, swa_sinks, rel_bias, w_branch_dn, w_branch_swa, w_out, ffn_norm, w_gate, w_up, w_down, loss_target, m_attn_norm, m_w_in, m_dn_conv, m_dn_a_log, m_dn_dt_bias, m_dn_out_norm, m_swa_q_norm, m_swa_k_norm, m_swa_sinks, m_rel_bias, m_w_branch_dn, m_w_branch_swa, m_w_out, m_ffn_norm, m_w_gate, m_w_up, m_w_down, v_attn_norm, v_w_in, v_dn_conv, v_dn_a_log, v_dn_dt_bias, v_dn_out_norm, v_swa_q_norm, v_swa_k_norm, v_swa_sinks, v_rel_bias, v_w_branch_dn, v_w_branch_swa, v_w_out, v_ffn_norm, v_w_gate, v_w_up, v_w_down):
    w = dict(attn_norm=attn_norm, w_in=w_in, dn_conv=dn_conv, dn_a_log=dn_a_log, dn_dt_bias=dn_dt_bias,
             dn_out_norm=dn_out_norm, swa_q_norm=swa_q_norm, swa_k_norm=swa_k_norm, swa_sinks=swa_sinks,
             rel_bias=rel_bias, w_branch_dn=w_branch_dn, w_branch_swa=w_branch_swa, w_out=w_out, ffn_norm=ffn_norm,
             w_gate=w_gate, w_up=w_up, w_down=w_down)
    m = dict(attn_norm=m_attn_norm, w_in=m_w_in, dn_conv=m_dn_conv, dn_a_log=m_dn_a_log, dn_dt_bias=m_dn_dt_bias,
             dn_out_norm=m_dn_out_norm, swa_q_norm=m_swa_q_norm, swa_k_norm=m_swa_k_norm, swa_sinks=m_swa_sinks,
             rel_bias=m_rel_bias, w_branch_dn=m_w_branch_dn, w_branch_swa=m_w_branch_swa, w_out=m_w_out,
             ffn_norm=m_ffn_norm, w_gate=m_w_gate, w_up=m_w_up, w_down=m_w_down)
    v = dict(attn_norm=v_attn_norm, w_in=v_w_in, dn_conv=v_dn_conv, dn_a_log=v_dn_a_log, dn_dt_bias=v_dn_dt_bias,
             dn_out_norm=v_dn_out_norm, swa_q_norm=v_swa_q_norm, swa_k_norm=v_swa_k_norm, swa_sinks=v_swa_sinks,
             rel_bias=v_rel_bias, w_branch_dn=v_w_branch_dn, w_branch_swa=v_w_branch_swa, w_out=v_w_out,
             ffn_norm=v_ffn_norm, w_gate=v_w_gate, w_up=v_w_up, w_down=v_w_down)
    shapes = {n: w[n].shape for n in _WEIGHT_NAMES}

    def two_d(a):
        return a.reshape(a.shape[-2], a.shape[-1]) if a.ndim == 3 else a

    core = lax.axis_index("c")
    chip = 2 * lax.axis_index("x") + lax.axis_index("y")
    small_shapes = {n: two_d(w[n]).shape for n, _ in _SMALL}
    small_shapes["dn_conv"] = (CONV, QKVW)

    conv_loc = two_d(w["dn_conv"])
    conv_part = lax.dynamic_update_slice(jnp.zeros((CONV, QKVW), F32), jnp.where(core == 0, conv_loc, 0.0),
                                         (0, chip * _CONV_SH))
    conv_full = _all_sum_small(conv_part.reshape(CONV * QKVW // 128, 128), "gather_conv").reshape(CONV, QKVW)

    w_bf = [two_d(w[n].astype(BF16)) for n in _BIG_NAMES]
    (w_in_g,) = _gather_weights(w_bf[:1], chip)
    windows = _gather_windows(w_bf[1:])
    send_sems, recv_sems, w_thru, l_thru, token = _split_start(
        "gather_start", w_bf[1:], _own_slot(w_bf[1:], chip), w_in_g[0, :8, :128], windows)

    def late(after):
        lands = _split_wait("gather_wait", w_thru, l_thru, send_sems, recv_sems, after, windows)
        g = dict(zip(_BIG_NAMES[1:], _sibling_fill(lands)))
        return dict(wa=g["w_branch_dn"], wb=g["w_branch_swa"], w_out=g["w_out"].reshape(D, D), wg=g["w_gate"],
                    wu=g["w_up"], wd=g["w_down"])

    w_in_full = w_in_g.transpose(1, 0, 2).reshape(D, D_IN)
    wts = dict(w_in_p=_pad_w_in(w_in_full), dn_conv=conv_full, late=late)
    for n, _ in _SMALL[:-1]:
        wts[n] = two_d(w[n])
    wts["attn_norm"] = wts["attn_norm"] + token[0:1, 0:1]

    early = {}

    def send_early(grads):
        gs = [grads["w_branch_dn"], grads["w_branch_swa"], grads["w_out"].reshape(N_CHIPS, CSH, D), grads["w_gate"],
              grads["w_up"], grads["w_down"]]
        parts = _pair_sum(gs, _swap_halves(gs, "swap_halves_early"), core, "pair_sum_early")
        own = [lax.dynamic_index_in_dim(p, chip, axis=0, keepdims=False) for p in parts]
        early["sems"], early["recv"], early["src"], early["land"], tok = _split_start(
            "exchange_start", parts, _own_slot(own, chip), parts[0][0, :8, :128], _exchange_windows())
        return tok

    wts["send_early"] = send_early
    loss_sum, grad_x, grads = _local_step(x[0], loss_target[0], wts)

    q_early = _split_wait("exchange_wait", early["src"], early["land"], early["sems"], early["recv"],
                          grads["w_in_p"], _exchange_windows())
    g_in = [_unpad_w_in(grads["w_in_p"]).reshape(D, N_CHIPS, D_IN // N_CHIPS).transpose(1, 0, 2)]
    parts_in = _pair_sum(g_in, _swap_halves(g_in, "swap_halves_in"), core, "pair_sum_in")
    reduced = _chip_sum(list(_chip_exchange(parts_in, chip)) + list(q_early))
    theirs = _swap_reduced(reduced)

    small_sum = _all_sum_small(_pack_small(grads, loss_sum), "all_sum_small")
    loss = small_sum.reshape(-1)[_LOSS_OFF]
    g_small = _unpack_small(small_sum, small_shapes)

    g_out, d_out, m_out, v_out = {}, {}, {}, {}
    for n, mine, other in zip(_BIG_NAMES, reduced, theirs):
        g_out[n], d_out[n], m_out[n], v_out[n] = _adamw_big(w[n], mine, other, m[n], v[n], core, "adamw_" + n)
    g_conv = lax.dynamic_slice(g_small["dn_conv"], (0, chip * _CONV_SH), (CONV, _CONV_SH))
    g_out["dn_conv"] = g_conv.reshape(shapes["dn_conv"])
    d_, m_, v_ = _adamw(conv_loc, g_conv, two_d(m["dn_conv"]), two_d(v["dn_conv"]), "adamw_dn_conv")
    d_out["dn_conv"], m_out["dn_conv"], v_out["dn_conv"] = (t.reshape(shapes["dn_conv"]) for t in (d_, m_, v_))

    def packed(src):
        vals = {n: src[n] for n, _ in _SMALL[:-1]}
        vals["dn_conv"] = jnp.zeros((CONV * QKVW,), F32)
        return _pack_small(vals)

    d_s, m_s, v_s = _adamw(packed(w), small_sum, packed(m), packed(v), "adamw_small")
    d_small, m_small, v_small = (_unpack_small(t, small_shapes) for t in (d_s, m_s, v_s))
    for n, _ in _SMALL[:-1]:
        g_out[n] = g_small[n].reshape(shapes[n])
        d_out[n], m_out[n], v_out[n] = (t[n].reshape(shapes[n]) for t in (d_small, m_small, v_small))

    return (loss, grad_x[None], *[g_out[n] for n in _WEIGHT_NAMES], *[d_out[n] for n in _WEIGHT_NAMES],
            *[m_out[n] for n in _WEIGHT_NAMES], *[v_out[n] for n in _WEIGHT_NAMES])
```

```python
import functools
import math

import numpy as np
import jax
import jax.numpy as jnp
from jax import lax
from jax.experimental import pallas as pl
from jax.experimental.pallas import tpu as pltpu

F32 = jnp.float32
BF16 = jnp.bfloat16
SDS = jax.ShapeDtypeStruct

D = 1024
DN_H = 4
DH = 128
DNW = DN_H * DH
QKVW = 3 * DNW
CONV = 4
CHUNK = 64
SWA_H = 8
SWA_KV = 2
SWA_G = SWA_H // SWA_KV
SWA_D = 64
SWAW = SWA_H * SWA_D
SWAKW = SWA_KV * SWA_D
BLK = 128
NBUCKET = 32
MAXDIST = 128
DFF = 2816
D_IN = QKVW + DNW + 2 * DN_H + SWAW + 2 * SWAKW + 2 * D
EPS = 1e-6
NEG = -1e30

ADAM_LR = 0.001
ADAM_B1 = 0.9
ADAM_B2 = 0.999
ADAM_EPS = 1e-08
ADAM_WD = 0.01
ADAM_STEP = 10

C_QKV, C_Z, C_GATE, C_SQ, C_SK, C_SV, C_BA = 0, 1536, 2048, 4096, 4608, 4736, 4864
PW = 5120
_ORIG_PIECES = (
    (0, QKVW, C_QKV),
    (QKVW, DNW, C_Z),
    (QKVW + DNW, 2 * DN_H, C_BA),
    (QKVW + DNW + 2 * DN_H, SWAW, C_SQ),
    (QKVW + DNW + 2 * DN_H + SWAW, SWAKW, C_SK),
    (QKVW + DNW + 2 * DN_H + SWAW + SWAKW, SWAKW, C_SV),
    (QKVW + DNW + 2 * DN_H + SWAW + 2 * SWAKW, 2 * D, C_GATE),
)

N_CHIPS = 4
FSH = DFF // N_CHIPS
CSH = D // N_CHIPS
VMEM_LIMIT = 48 * 1024 * 1024
MESH = pl.DeviceIdType.MESH

_BIG = (
    ("w_in", D, D_IN // N_CHIPS),
    ("w_branch_dn", DNW, CSH),
    ("w_branch_swa", SWAW, CSH),
    ("w_out", CSH, D),
    ("w_gate", FSH, D),
    ("w_up", FSH, D),
    ("w_down", FSH, D),
)
_BIG_NAMES = tuple(n for n, _, _ in _BIG)

_SMALL = (
    ("attn_norm", D), ("ffn_norm", D), ("dn_out_norm", DH), ("swa_q_norm", SWA_D), ("swa_k_norm", SWA_D),
    ("swa_sinks", SWA_H), ("dn_a_log", DN_H), ("dn_dt_bias", DN_H), ("rel_bias", NBUCKET * SWA_H),
    ("dn_conv", CONV * QKVW),
)
_SMALL_OFF = {}
_o = 0
for _n, _s in _SMALL:
    _SMALL_OFF[_n] = (_o, _s)
    _o += _s
_LOSS_OFF = _o
_SMALL_ROWS = -(-(_o + 1) // (8 * 128)) * 8


def _cparams(**kw):
    return pltpu.CompilerParams(vmem_limit_bytes=VMEM_LIMIT, **kw)


_DIMS = {
    "nn": (((1,), (0,)), ((), ())),
    "nt": (((1,), (1,)), ((), ())),
    "tn": (((0,), (0,)), ((), ())),
    "bnn": (((2,), (1,)), ((0,), (0,))),
    "bnt": (((2,), (2,)), ((0,), (0,))),
    "btn": (((1,), (1,)), ((0,), (0,))),
}


def _raw_dot(a, b, kind, exact):
    if exact:
        prec = lax.Precision.HIGH if exact == "x3" else lax.Precision.HIGHEST
        return lax.dot_general(a, b, _DIMS[kind], precision=prec, preferred_element_type=F32)
    return lax.dot_general(a.astype(BF16), b.astype(BF16), _DIMS[kind], preferred_element_type=F32)


@functools.partial(jax.custom_vjp, nondiff_argnums=(2, 3))
def _dot(a, b, kind, exact):
    return _raw_dot(a, b, kind, exact)


def _dot_fwd(a, b, kind, exact):
    return _raw_dot(a, b, kind, exact), (a, b)


def _dot_bwd(kind, exact, res, g):
    a, b = res
    pre = kind[:-2]
    nn, nt, tn = pre + "nn", pre + "nt", pre + "tn"
    if kind == nn:
        return _dot(g, b, nt, exact), _dot(a, g, tn, exact)
    if kind == nt:
        return _dot(g, b, nn, exact), _dot(g, a, tn, exact)
    return _dot(b, g, nt, exact), _dot(a, g, nn, exact)


_dot.defvjp(_dot_fwd, _dot_bwd)


def _silu(x):
    return x * jax.nn.sigmoid(x)


def _f_rms(x, gain):
    return x * lax.rsqrt(jnp.mean(x * x, axis=-1, keepdims=True) + EPS) * gain


def _f_dn_pre(xs0, xs1, xs2, xs3, ba, cw, alog, dtb):
    rows = xs0.shape[0]
    c = xs0 * cw[0:1] + xs1 * cw[1:2] + xs2 * cw[2:3] + xs3 * cw[3:4]
    qkv = _silu(c)
    qs, ks, bbs, gbs = [], [], [], []
    for h in range(DN_H):
        qh = qkv[:, h * DH:(h + 1) * DH]
        kh = qkv[:, DNW + h * DH:DNW + (h + 1) * DH]
        qs.append(qh * lax.rsqrt(jnp.sum(qh * qh, axis=-1, keepdims=True) + EPS) * (DH ** -0.5))
        ks.append(kh * lax.rsqrt(jnp.sum(kh * kh, axis=-1, keepdims=True) + EPS))
        beta = jax.nn.sigmoid(ba[:, h:h + 1])
        ar = ba[:, DN_H + h:DN_H + h + 1] + dtb[:, h:h + 1]
        softplus = jnp.maximum(ar, 0.0) + jnp.log1p(jnp.exp(-jnp.abs(ar)))
        g = -jnp.exp(alog[:, h:h + 1]) * softplus
        bbs.append(jnp.broadcast_to(beta, (rows, DH)))
        gbs.append(jnp.broadcast_to(g, (rows, DH)))
    return (jnp.concatenate(qs, axis=1), jnp.concatenate(ks, axis=1), qkv[:, 2 * DNW:],
            jnp.concatenate(bbs, axis=1), jnp.concatenate(gbs, axis=1))


def _f_dn_post(o, z, gain):
    ys = []
    for h in range(DN_H):
        oh = o[:, h * DH:(h + 1) * DH]
        zh = z[:, h * DH:(h + 1) * DH]
        ys.append(oh * lax.rsqrt(jnp.mean(oh * oh, axis=-1, keepdims=True) + EPS) * gain * _silu(zh))
    return jnp.concatenate(ys, axis=1)


def _f_merge(pa, pb, ga, gb):
    return jax.nn.sigmoid(ga) * pa + jax.nn.sigmoid(gb) * pb


def _f_swiglu(g, u):
    return _silu(g) * u


@jax.custom_vjp
def _unit_lower_inverse(a):
    c = a.shape[-1]
    eye = (lax.broadcasted_iota(jnp.int32, a.shape, 1) == lax.broadcasted_iota(jnp.int32, a.shape, 2)).astype(F32)
    p = -a
    t = eye + p
    for _ in range(max(c.bit_length() - 2, 0)):
        p = _raw_dot(p, p, "bnn", "x3")
        t = t + _raw_dot(t, p, "bnn", "x3")
    return t


def _unit_lower_inverse_fwd(a):
    t = _unit_lower_inverse(a)
    return t, t


def _unit_lower_inverse_bwd(t, g):
    return (-_raw_dot(_raw_dot(t, g, "btn", "x3"), t, "bnt", "x3"),)


_unit_lower_inverse.defvjp(_unit_lower_inverse_fwd, _unit_lower_inverse_bwd)


@jax.custom_vjp
def _known_inverse(a, t):
    return t


def _known_inverse_fwd(a, t):
    return t, t


def _known_inverse_bwd(t, g):
    return _unit_lower_inverse_bwd(t, g)[0], jnp.zeros_like(t)


_known_inverse.defvjp(_known_inverse_fwd, _known_inverse_bwd)


def _f_chunk(q, k, v, gb, bb, s, t_known=None, with_t=False):
    c = CHUNK
    nh = q.shape[0]
    ii = lax.broadcasted_iota(jnp.int32, (nh, c, c), 1)
    jj = lax.broadcasted_iota(jnp.int32, (nh, c, c), 2)
    incl = ii >= jj
    strict = ii > jj
    eye = (ii == jj).astype(F32)
    gcb = _dot(incl.astype(F32), gb, "bnn", True)
    lane0 = (lax.broadcasted_iota(jnp.int32, (nh, c, DH), 2) == 0).astype(F32)
    gcol = gcb[:, :, :c]
    grow = _dot(lane0, gcb, "bnt", True)
    decay = jnp.where(incl, jnp.exp(jnp.where(incl, gcol - grow, 0.0)), 0.0)
    kb = k * bb
    vb = v * bb
    a = jnp.where(strict, _dot(kb, k, "bnt", False) * decay, 0.0)
    t = _unit_lower_inverse(a) if t_known is None else _known_inverse(a, t_known)
    eg = jnp.exp(gcb)
    u = _dot(t, vb, "bnn", "x3")
    w = _dot(t, kb * eg, "bnn", "x3")
    qk = jnp.where(incl, _dot(q, k, "bnt", False) * decay, 0.0)
    qe = q * eg
    glast = gcb[:, c - 1:c, :]
    k_dec = k * jnp.exp(glast - gcb)
    e_last = jnp.exp(glast)
    outs = []
    for g in range(nh // DN_H):
        sl = slice(g * DN_H, (g + 1) * DN_H)
        v_new = u[sl] - _dot(w[sl], s, "bnn", False)
        outs.append(_dot(qe[sl], s, "bnn", False) + _dot(qk[sl], v_new, "bnn", False))
        s = s * e_last[sl] + _dot(k_dec[sl], v_new, "btn", False)
    o = jnp.concatenate(outs, axis=0)
    return (o, s, t) if with_t else (o, s)


def _f_swa(q8, kp, kc, vp, vc, bias8, qg, kg, sink, mask):
    kb = jnp.concatenate([kp, kc], axis=1)
    vb = jnp.concatenate([vp, vc], axis=1)
    kn = kb * lax.rsqrt(jnp.mean(kb * kb, axis=-1, keepdims=True) + EPS) * kg

    def rows(per_head):
        return jnp.stack([jnp.concatenate([per_head(kv, g) for g in range(SWA_G)], axis=0)
                          for kv in range(SWA_KV)], axis=0)

    qq = rows(lambda kv, g: q8[kv * SWA_G + g])
    qn = qq * lax.rsqrt(jnp.mean(qq * qq, axis=-1, keepdims=True) + EPS) * qg
    lg = _dot(qn, kn, "bnt", False) * (SWA_D ** -0.5) + rows(lambda kv, g: bias8[kv * SWA_G + g])
    lg = jnp.where(rows(lambda kv, g: mask), lg, NEG)
    sk = rows(lambda kv, g: jnp.broadcast_to(sink[kv][:, g:g + 1], (BLK, 1)))
    m = lax.stop_gradient(jnp.maximum(jnp.max(lg, axis=-1, keepdims=True), sk))
    p = jnp.exp(lg - m)
    den = jnp.sum(p, axis=-1, keepdims=True) + jnp.exp(sk - m)
    out = _dot(p / den, vb, "bnn", False)
    return jnp.stack([out[kv, g * BLK:(g + 1) * BLK] for kv in range(SWA_KV) for g in range(SWA_G)], axis=0)


def _bdot(a, b, kind="nn"):
    return lax.dot_general(a.astype(BF16), b.astype(BF16), _DIMS[kind], preferred_element_type=F32)


def _pc(kern, name, grid, in_specs, out_specs, out_shape, scratch=()):
    return pl.pallas_call(
        kern, name=name, grid=grid, in_specs=in_specs, out_specs=out_specs, out_shape=out_shape,
        scratch_shapes=list(scratch), compiler_params=_cparams(dimension_semantics=("arbitrary",) * len(grid)))


def _mm(a, b, kind, out_dtype, tm, tn, name):
    if kind == "tn":
        k, m = a.shape
    else:
        m, k = a.shape
    n = b.shape[0] if kind == "nt" else b.shape[1]
    tm, tn = min(tm, m), min(tn, n)
    assert m % tm == 0 and n % tn == 0, (name, a.shape, b.shape, tm, tn)

    def kern(a_ref, b_ref, o_ref):
        o_ref[...] = _bdot(a_ref[...], b_ref[...], kind).astype(o_ref.dtype)

    a_spec = pl.BlockSpec((k, tm), lambda i, j: (0, i)) if kind == "tn" else pl.BlockSpec((tm, k), lambda i, j: (i, 0))
    b_spec = pl.BlockSpec((tn, k), lambda i, j: (j, 0)) if kind == "nt" else pl.BlockSpec((k, tn), lambda i, j: (0, j))
    return _pc(kern, name, (m // tm, n // tn), [a_spec, b_spec], pl.BlockSpec((tm, tn), lambda i, j: (i, j)),
               SDS((m, n), out_dtype))(a, b)


def _rows(body, name, m, tm, row_ins, full_ins, row_outs, acc_outs=()):
    n_r, n_f, n_o, n_a = len(row_ins), len(full_ins), len(row_outs), len(acc_outs)
    assert m % tm == 0

    def kern(*refs):
        r = refs[:n_r]
        f = refs[n_r:n_r + n_f]
        o = refs[n_r + n_f:n_r + n_f + n_o]
        acc = refs[n_r + n_f + n_o:]
        outs, sums = body([x[...] for x in r], [x[...] for x in f])
        for ref, val in zip(o, outs, strict=True):
            ref[...] = val.astype(ref.dtype)
        if n_a:
            @pl.when(pl.program_id(0) == 0)
            def _():
                for ref in acc:
                    ref[...] = jnp.zeros(ref.shape, F32)

            for ref, val in zip(acc, sums, strict=True):
                ref[...] += val

    in_specs = [pl.BlockSpec((tm, w), functools.partial(lambda i, cb: (i, cb), cb=cb)) for _, w, cb in row_ins]
    in_specs += [pl.BlockSpec(x.shape, lambda i: (0, 0)) for x in full_ins]
    out_specs = [pl.BlockSpec((tm, w), lambda i: (i, 0)) for w, _ in row_outs]
    out_specs += [pl.BlockSpec(s, lambda i: (0, 0)) for s in acc_outs]
    out_shape = [SDS((m, w), dt) for w, dt in row_outs]
    out_shape += [SDS(s, F32) for s in acc_outs]
    return _pc(kern, name, (m // tm,), in_specs, out_specs, out_shape)(*[x for x, _, _ in row_ins], *full_ins)


def _whole(x):
    return (x, x.shape[1], 0)


def _zero_first(refs):
    @pl.when(pl.program_id(0) == 0)
    def _():
        for ref in refs:
            ref[...] = jnp.zeros(ref.shape, F32)


GROUP = 4


def _heads(ref):
    return jnp.stack([ref[g * CHUNK:(g + 1) * CHUNK, h * DH:(h + 1) * DH]
                      for g in range(GROUP) for h in range(DN_H)], axis=0)


def _unheads(ref, val):
    for g in range(GROUP):
        for h in range(DN_H):
            ref[g * CHUNK:(g + 1) * CHUNK, h * DH:(h + 1) * DH] = val[g * DN_H + h]


def _dn_chunks_fwd(q, k, v, gb, bb):
    s_len = q.shape[0]
    ng = s_len // (GROUP * CHUNK)

    def kern(q_ref, k_ref, v_ref, g_ref, b_ref, o_ref, sall_ref, t_ref, state):
        _zero_first([state])
        s = state[...]
        sall_ref[0] = s
        o, s_new, t = _f_chunk(*[_heads(r) for r in (q_ref, k_ref, v_ref, g_ref, b_ref)], s, with_t=True)
        _unheads(o_ref, o)
        t_ref[0] = t
        state[...] = s_new

    blk = pl.BlockSpec((GROUP * CHUNK, DNW), lambda c: (c, 0))
    return _pc(kern, "dn_chunks_fwd", (ng,), [blk] * 5,
               [blk, pl.BlockSpec((1, DN_H, DH, DH), lambda c: (c, 0, 0, 0)),
                pl.BlockSpec((1, GROUP * DN_H, CHUNK, CHUNK), lambda c: (c, 0, 0, 0))],
               [SDS((s_len, DNW), F32), SDS((ng, DN_H, DH, DH), F32), SDS((ng, GROUP * DN_H, CHUNK, CHUNK), F32)],
               scratch=[pltpu.VMEM((DN_H, DH, DH), F32)])(q, k, v, gb, bb)


def _dn_chunks_bwd(q, k, v, gb, bb, s_all, t_all, d_o):
    s_len = q.shape[0]
    ng = s_len // (GROUP * CHUNK)

    def kern(q_ref, k_ref, v_ref, g_ref, b_ref, sall_ref, t_ref, do_ref, dq_ref, dk_ref, dv_ref, dg_ref, db_ref,
             dstate):
        _zero_first([dstate])
        fn = functools.partial(_f_chunk, t_known=t_ref[0])
        _, vjp = jax.vjp(fn, *[_heads(r) for r in (q_ref, k_ref, v_ref, g_ref, b_ref)], sall_ref[0])
        *d_ins, ds = vjp((_heads(do_ref), dstate[...]))
        for ref, val in zip((dq_ref, dk_ref, dv_ref, dg_ref, db_ref), d_ins, strict=True):
            _unheads(ref, val)
        dstate[...] = ds

    blk = pl.BlockSpec((GROUP * CHUNK, DNW), lambda c: (ng - 1 - c, 0))
    return _pc(kern, "dn_chunks_bwd", (ng,),
               [blk] * 5 + [pl.BlockSpec((1, DN_H, DH, DH), lambda c: (ng - 1 - c, 0, 0, 0)),
                            pl.BlockSpec((1, GROUP * DN_H, CHUNK, CHUNK), lambda c: (ng - 1 - c, 0, 0, 0)), blk],
               [blk] * 5, [SDS((s_len, DNW), F32)] * 5,
               scratch=[pltpu.VMEM((DN_H, DH, DH), F32)])(q, k, v, gb, bb, s_all, t_all, d_o)


def _t5_bucket_table():
    qi = np.arange(BLK)[:, None]
    kj = np.arange(2 * BLK)[None, :]
    dist = BLK + qi - kj
    n = np.maximum(dist, 0)
    max_exact = NBUCKET // 2
    nf = np.maximum(n, 1).astype(np.float32)
    large = max_exact + (np.log(nf / np.float32(max_exact)) / np.float32(math.log(MAXDIST / max_exact))
                         * np.float32(NBUCKET - max_exact)).astype(np.int32)
    large = np.minimum(large, NBUCKET - 1)
    return np.where(n < max_exact, n, large)


def _bucket_onehot_t():
    table = _t5_bucket_table().reshape(-1)
    return (np.arange(NBUCKET)[:, None] == table[None, :]).astype(np.float32)


def _swa_mask(first):
    qi = lax.broadcasted_iota(jnp.int32, (BLK, 2 * BLK), 0)
    kj = lax.broadcasted_iota(jnp.int32, (BLK, 2 * BLK), 1)
    dist = BLK + qi - kj
    window = (dist >= 0) & (dist < BLK)
    return window & ((kj >= BLK) | jnp.logical_not(first))


def _bias_expand(rel_bias_t):
    onehot = jnp.asarray(_bucket_onehot_t())

    def kern(r_ref, oh_ref, o_ref):
        o_ref[...] = _raw_dot(r_ref[...], oh_ref[...], "nn", True)

    return pl.pallas_call(
        kern, name="bias_expand", out_shape=SDS((SWA_H, BLK * 2 * BLK), F32), compiler_params=_cparams(),
    )(rel_bias_t, onehot)


def _bias_reduce(d_bias_flat):
    onehot = jnp.asarray(_bucket_onehot_t())

    def kern(d_ref, oh_ref, o_ref):
        o_ref[...] = _raw_dot(d_ref[...], oh_ref[...], "nt", True)

    return pl.pallas_call(
        kern, name="bias_reduce", out_shape=SDS((SWA_H, NBUCKET), F32), compiler_params=_cparams(),
    )(d_bias_flat, onehot)


def _swa_specs(nb, rev):
    def blk(n):
        return (nb - 1 - n) if rev else n

    def before(n):
        return jnp.maximum(blk(n) - 1, 0)

    q_spec = pl.BlockSpec((BLK, SWAW), lambda n: (blk(n), C_SQ // SWAW))
    k_cur = pl.BlockSpec((BLK, SWAKW), lambda n: (blk(n), C_SK // SWAKW))
    k_prev = pl.BlockSpec((BLK, SWAKW), lambda n: (before(n), C_SK // SWAKW))
    v_cur = pl.BlockSpec((BLK, SWAKW), lambda n: (blk(n), C_SV // SWAKW))
    v_prev = pl.BlockSpec((BLK, SWAKW), lambda n: (before(n), C_SV // SWAKW))
    bias = pl.BlockSpec((SWA_H, BLK, 2 * BLK), lambda n: (0, 0, 0))
    gain = pl.BlockSpec((1, SWA_D), lambda n: (0, 0))
    sink = pl.BlockSpec((SWA_KV, 1, SWA_G), lambda n: (0, 0, 0))
    wide = pl.BlockSpec((BLK, SWAW), lambda n: (blk(n), 0))
    narrow = pl.BlockSpec((BLK, SWAKW), lambda n: (blk(n), 0))
    return [q_spec, k_prev, k_cur, v_prev, v_cur, bias, gain, gain, sink], wide, narrow


def _split_heads(x):
    return jnp.stack([x[:, h * SWA_D:(h + 1) * SWA_D] for h in range(x.shape[1] // SWA_D)], axis=0)


def _join_heads(x):
    return jnp.concatenate([x[h] for h in range(x.shape[0])], axis=1)


def _swa_fwd(proj, bias, qg, kg, sinks):
    s_len = proj.shape[0]
    nb = s_len // BLK
    in_specs, wide, _ = _swa_specs(nb, False)

    def kern(q_ref, kp_ref, kc_ref, vp_ref, vc_ref, b_ref, qg_ref, kg_ref, s_ref, o_ref):
        mask = _swa_mask(pl.program_id(0) == 0)
        o8 = _f_swa(*[_split_heads(r[...]) for r in (q_ref, kp_ref, kc_ref, vp_ref, vc_ref)], b_ref[...], qg_ref[...],
                    kg_ref[...], s_ref[...], mask)
        o_ref[...] = _join_heads(o8).astype(BF16)

    return _pc(kern, "swa_fwd", (nb,), in_specs, wide, SDS((s_len, SWAW), BF16))(
        proj, proj, proj, proj, proj, bias, qg, kg, sinks)


def _swa_bwd(proj, bias, qg, kg, sinks, d_out):
    s_len = proj.shape[0]
    nb = s_len // BLK
    in_specs, wide, narrow = _swa_specs(nb, True)

    def kern(q_ref, kp_ref, kc_ref, vp_ref, vc_ref, b_ref, qg_ref, kg_ref, s_ref, do_ref,
             dq_ref, dk_ref, dv_ref, db_ref, dqg_ref, dkg_ref, ds_ref, carry_k, carry_v):
        n = pl.program_id(0)
        mask = _swa_mask(n == nb - 1)
        _zero_first([carry_k, carry_v, db_ref, ds_ref, dqg_ref, dkg_ref])
        fn = functools.partial(_f_swa, mask=mask)
        _, vjp = jax.vjp(fn, *[_split_heads(r[...]) for r in (q_ref, kp_ref, kc_ref, vp_ref, vc_ref)], b_ref[...],
                         qg_ref[...], kg_ref[...], s_ref[...])
        dq, dkp, dkc, dvp, dvc, dbias, dqg, dkg, dsink = vjp(_split_heads(do_ref[...]))
        dq_ref[...] = _join_heads(dq).astype(BF16)
        dk_ref[...] = (_join_heads(dkc) + carry_k[...]).astype(BF16)
        dv_ref[...] = (_join_heads(dvc) + carry_v[...]).astype(BF16)
        carry_k[...] = _join_heads(dkp)
        carry_v[...] = _join_heads(dvp)
        db_ref[...] += dbias
        dqg_ref[...] += dqg
        dkg_ref[...] += dkg
        ds_ref[...] += dsink

    bias_spec, gain, sink = in_specs[5], in_specs[6], in_specs[8]
    return _pc(
        kern, "swa_bwd", (nb,), in_specs + [wide], [wide, narrow, narrow, bias_spec, gain, gain, sink],
        [SDS((s_len, SWAW), BF16), SDS((s_len, SWAKW), BF16), SDS((s_len, SWAKW), BF16),
         SDS((SWA_H, BLK, 2 * BLK), F32), SDS((1, SWA_D), F32), SDS((1, SWA_D), F32), SDS((SWA_KV, 1, SWA_G), F32)],
        scratch=[pltpu.VMEM((BLK, SWAKW), F32), pltpu.VMEM((BLK, SWAKW), F32)],
    )(proj, proj, proj, proj, proj, bias, qg, kg, sinks, d_out)


def _branch_merge(y_dn, y_swa, wa, wb, proj):
    s_len = y_dn.shape[0]
    tm = min(512, s_len)

    def kern(ya_ref, yb_ref, wa_ref, wb_ref, ga_ref, gb_ref, pa_ref, pb_ref, m_ref):
        pa = _bdot(ya_ref[...], wa_ref[0])
        pb = _bdot(yb_ref[...], wb_ref[0])
        pa_ref[...] = pa
        pb_ref[...] = pb
        m_ref[...] = _f_merge(pa, pb, ga_ref[...], gb_ref[...]).astype(BF16)

    y_spec = pl.BlockSpec((tm, DNW), lambda i, s: (i, 0))
    w_spec = pl.BlockSpec((1, DNW, CSH), lambda i, s: (s, 0, 0))
    o_spec = pl.BlockSpec((tm, CSH), lambda i, s: (i, s))
    ga_spec = pl.BlockSpec((tm, CSH), lambda i, s: (i, C_GATE // CSH + s))
    gb_spec = pl.BlockSpec((tm, CSH), lambda i, s: (i, (C_GATE + D) // CSH + s))
    return _pc(kern, "branch_merge", (s_len // tm, N_CHIPS), [y_spec, y_spec, w_spec, w_spec, ga_spec, gb_spec],
               [o_spec] * 3, [SDS((s_len, D), F32), SDS((s_len, D), F32), SDS((s_len, D), BF16)],
               )(y_dn, y_swa, wa, wb, proj, proj)


def _out_proj(merged, w_out, x, gain):
    s_len = x.shape[0]
    tm = min(256, s_len)

    def kern(m_ref, w_ref, x_ref, g_ref, x1_ref, h2_ref):
        x1 = x_ref[...] + _bdot(m_ref[...], w_ref[...])
        x1_ref[...] = x1
        h2_ref[...] = _f_rms(x1, g_ref[...]).astype(BF16)

    row = pl.BlockSpec((tm, D), lambda i: (i, 0))
    return _pc(kern, "out_proj", (s_len // tm,),
               [row, pl.BlockSpec((D, D), lambda i: (0, 0)), row, pl.BlockSpec((1, D), lambda i: (0, 0))],
               [row, row], [SDS((s_len, D), F32), SDS((s_len, D), BF16)])(merged, w_out, x, gain)


def _ffn_up(h2, wg, wu):
    s_len = h2.shape[0]
    tm = min(512, s_len)

    def kern(h_ref, g_ref, u_ref, gt_ref, up_ref, act_ref):
        h = h_ref[...]
        g = _bdot(h, g_ref[0], "nt")
        u = _bdot(h, u_ref[0], "nt")
        gt_ref[0] = g
        up_ref[0] = u
        act_ref[0] = _f_swiglu(g, u).astype(BF16)

    w_spec = pl.BlockSpec((1, FSH, D), lambda s, i: (s, 0, 0))
    o_spec = pl.BlockSpec((1, tm, FSH), lambda s, i: (s, i, 0))
    shape = (N_CHIPS, s_len, FSH)
    return _pc(kern, "ffn_up", (N_CHIPS, s_len // tm), [pl.BlockSpec((tm, D), lambda s, i: (i, 0)), w_spec, w_spec],
               [o_spec] * 3, [SDS(shape, F32), SDS(shape, F32), SDS(shape, BF16)])(h2, wg, wu)


def _ffn_down_loss(act, wd, x1, target):
    s_len = x1.shape[0]
    tm = min(256, s_len)

    def kern(a_ref, w_ref, x_ref, t_ref, dy_ref, dyb_ref, loss_ref):
        _zero_first([loss_ref])
        y = x_ref[...]
        for s in range(N_CHIPS):
            y = y + _bdot(a_ref[s], w_ref[s])
        d = y - t_ref[...]
        dy = d * (1.0 / D)
        dy_ref[...] = dy
        dyb_ref[...] = dy.astype(BF16)
        loss_ref[...] += jnp.sum(d * d).reshape(1, 1) * (0.5 / D)

    row = pl.BlockSpec((tm, D), lambda i: (i, 0))
    return _pc(kern, "ffn_down_loss", (s_len // tm,),
               [pl.BlockSpec((N_CHIPS, tm, FSH), lambda i: (0, i, 0)),
                pl.BlockSpec((N_CHIPS, FSH, D), lambda i: (0, 0, 0)), row, row],
               [row, row, pl.BlockSpec((1, 1), lambda i: (0, 0))],
               [SDS((s_len, D), F32), SDS((s_len, D), BF16), SDS((1, 1), F32)])(act, wd, x1, target)


def _ffn_dact(dy_b, wd, gt, up):
    s_len = dy_b.shape[0]
    tm = min(512, s_len)

    def kern(dy_ref, w_ref, gt_ref, up_ref, dg_ref, du_ref):
        d_act = _bdot(dy_ref[...], w_ref[0], "nt")
        _, vjp = jax.vjp(_f_swiglu, gt_ref[0], up_ref[0])
        dg, du = vjp(d_act)
        dg_ref[0] = dg.astype(BF16)
        du_ref[0] = du.astype(BF16)

    a_spec = pl.BlockSpec((1, tm, FSH), lambda s, i: (s, i, 0))
    shape = (N_CHIPS, s_len, FSH)
    return _pc(kern, "ffn_dact", (N_CHIPS, s_len // tm),
               [pl.BlockSpec((tm, D), lambda s, i: (i, 0)), pl.BlockSpec((1, FSH, D), lambda s, i: (s, 0, 0)),
                a_spec, a_spec],
               [a_spec, a_spec], [SDS(shape, BF16), SDS(shape, BF16)])(dy_b, wd, gt, up)


def _gw_ffn(lhs, rhs, name):
    s_len = rhs.shape[0]
    n = len(lhs)
    tn = 512

    def kern(*refs):
        g = refs[n][...]
        for i in range(n):
            refs[n + 1 + i][0] = _bdot(refs[i][0], g, "tn").astype(BF16)

    a_spec = pl.BlockSpec((1, s_len, FSH), lambda s, j: (s, 0, 0))
    o_spec = pl.BlockSpec((1, FSH, tn), lambda s, j: (s, 0, j))
    return _pc(kern, name, (N_CHIPS, D // tn), [a_spec] * n + [pl.BlockSpec((s_len, tn), lambda s, j: (0, j))],
               [o_spec] * n, [SDS((N_CHIPS, FSH, D), BF16)] * n)(*lhs, rhs)


def _ffn_dh2(d_gt, d_up, wg, wu, x1, dy, gain):
    s_len = x1.shape[0]
    tm = min(256, s_len)

    def kern(dg_ref, du_ref, wg_ref, wu_ref, x_ref, dy_ref, g_ref, dx_ref, dxb_ref, dgain_ref):
        _zero_first([dgain_ref])
        dh2 = jnp.zeros((tm, D), F32)
        for s in range(N_CHIPS):
            dh2 = dh2 + _bdot(dg_ref[s], wg_ref[s]) + _bdot(du_ref[s], wu_ref[s])
        _, vjp = jax.vjp(_f_rms, x_ref[...], g_ref[...])
        dx, dgain = vjp(dh2)
        dx1 = dx + dy_ref[...]
        dx_ref[...] = dx1
        dxb_ref[...] = dx1.astype(BF16)
        dgain_ref[...] += dgain

    row = pl.BlockSpec((tm, D), lambda i: (i, 0))
    d_spec = pl.BlockSpec((N_CHIPS, tm, FSH), lambda i: (0, i, 0))
    w_spec = pl.BlockSpec((N_CHIPS, FSH, D), lambda i: (0, 0, 0))
    vec = pl.BlockSpec((1, D), lambda i: (0, 0))
    return _pc(kern, "ffn_dh2", (s_len // tm,), [d_spec, d_spec, w_spec, w_spec, row, row, vec],
               [row, row, vec], [SDS((s_len, D), F32), SDS((s_len, D), BF16), SDS((1, D), F32)],
               )(d_gt, d_up, wg, wu, x1, dy, gain)


def _merge_bwd(dx1_b, w_out, pa, pb, proj):
    s_len = dx1_b.shape[0]
    tm = min(256, s_len)

    def kern(dx_ref, w_ref, pa_ref, pb_ref, g_ref, dpa_ref, dpb_ref, dg_ref):
        dm = _bdot(dx_ref[...], w_ref[...], "nt")
        gates = g_ref[...]
        _, vjp = jax.vjp(_f_merge, pa_ref[...], pb_ref[...], gates[:, :D], gates[:, D:])
        dpa, dpb, dga, dgb = vjp(dm)
        dpa_ref[...] = dpa.astype(BF16)
        dpb_ref[...] = dpb.astype(BF16)
        dg_ref[:, :D] = dga.astype(BF16)
        dg_ref[:, D:] = dgb.astype(BF16)

    row = pl.BlockSpec((tm, D), lambda i: (i, 0))
    return _pc(kern, "merge_bwd", (s_len // tm,),
               [row, pl.BlockSpec((D, D), lambda i: (0, 0)), row, row,
                pl.BlockSpec((tm, 2 * D), lambda i: (i, C_GATE // (2 * D)))],
               [row, row, pl.BlockSpec((tm, 2 * D), lambda i: (i, 0))],
               [SDS((s_len, D), BF16), SDS((s_len, D), BF16), SDS((s_len, 2 * D), BF16)],
               )(dx1_b, w_out, pa, pb, proj)


def _d_branch(d_pa, d_pb, wa, wb):
    s_len = d_pa.shape[0]
    tm = min(512, s_len)

    def kern(da_ref, db_ref, wa_ref, wb_ref, oa_ref, ob_ref):
        acc_a = jnp.zeros((tm, DNW), F32)
        acc_b = jnp.zeros((tm, SWAW), F32)
        for s in range(N_CHIPS):
            acc_a = acc_a + _bdot(da_ref[:, s * CSH:(s + 1) * CSH], wa_ref[s], "nt")
            acc_b = acc_b + _bdot(db_ref[:, s * CSH:(s + 1) * CSH], wb_ref[s], "nt")
        oa_ref[...] = acc_a
        ob_ref[...] = acc_b

    row = pl.BlockSpec((tm, D), lambda i: (i, 0))
    w_spec = pl.BlockSpec((N_CHIPS, DNW, CSH), lambda i: (0, 0, 0))
    out = pl.BlockSpec((tm, DNW), lambda i: (i, 0))
    return _pc(kern, "d_branch", (s_len // tm,), [row, row, w_spec, w_spec], [out, out],
               [SDS((s_len, DNW), F32), SDS((s_len, SWAW), F32)])(d_pa, d_pb, wa, wb)


def _gw_branch(y_dn, y_swa, d_pa, d_pb):
    s_len = y_dn.shape[0]

    def kern(ya_ref, yb_ref, da_ref, db_ref, oa_ref, ob_ref):
        oa_ref[0] = _bdot(ya_ref[...], da_ref[...], "tn").astype(BF16)
        ob_ref[0] = _bdot(yb_ref[...], db_ref[...], "tn").astype(BF16)

    y_spec = pl.BlockSpec((s_len, DNW), lambda s: (0, 0))
    d_spec = pl.BlockSpec((s_len, CSH), lambda s: (0, s))
    o_spec = pl.BlockSpec((1, DNW, CSH), lambda s: (s, 0, 0))
    shape = (N_CHIPS, DNW, CSH)
    return _pc(kern, "gw_branch", (N_CHIPS,), [y_spec, y_spec, d_spec, d_spec], [o_spec, o_spec],
               [SDS(shape, BF16), SDS(shape, BF16)])(y_dn, y_swa, d_pa, d_pb)


def _dh_rms(d_proj, w_in_p, x, dx1, gain):
    s_len = x.shape[0]
    tm = min(256, s_len)

    def kern(dp_ref, w_ref, x_ref, r_ref, g_ref, gx_ref, dgain_ref):
        _zero_first([dgain_ref])
        dh = _bdot(dp_ref[...], w_ref[...], "nt")
        _, vjp = jax.vjp(_f_rms, x_ref[...], g_ref[...])
        dx, dgain = vjp(dh)
        gx_ref[...] = dx + r_ref[...]
        dgain_ref[...] += dgain

    row = pl.BlockSpec((tm, D), lambda i: (i, 0))
    vec = pl.BlockSpec((1, D), lambda i: (0, 0))
    return _pc(kern, "dh_rms", (s_len // tm,),
               [pl.BlockSpec((tm, PW), lambda i: (i, 0)), pl.BlockSpec((D, PW), lambda i: (0, 0)), row, row, vec],
               [row, vec], [SDS((s_len, D), F32), SDS((1, D), F32)])(d_proj, w_in_p, x, dx1, gain)


HALO = 8


def _conv_taps(cur_ref, prev_ref, halo, first):
    tm = cur_ref.shape[0]
    halo[0:HALO, :] = jnp.where(first, 0.0, prev_ref[...])
    halo[HALO:, :] = cur_ref[...]
    return [halo[HALO - n:HALO - n + tm, :] for n in range(CONV - 1, 0, -1)] + [cur_ref[...]]


def _dn_pre_specs(s_len, tm, blk):
    cur = pl.BlockSpec((tm, QKVW), lambda i: (blk(i), 0))
    prev = pl.BlockSpec((HALO, QKVW), lambda i: (jnp.maximum(blk(i) * (tm // HALO) - 1, 0), 0))
    ba = pl.BlockSpec((tm, 128), lambda i: (blk(i), C_BA // 128))
    row = pl.BlockSpec((tm, DNW), lambda i: (blk(i), 0))
    full = [pl.BlockSpec((CONV, QKVW), lambda i: (0, 0)), pl.BlockSpec((1, DN_H), lambda i: (0, 0)),
            pl.BlockSpec((1, DN_H), lambda i: (0, 0))]
    return cur, prev, ba, row, full


def _dn_pre_fwd(proj, conv_w, alog, dtb):
    s_len = proj.shape[0]
    tm = min(128, s_len)
    cur, prev, ba, row, full = _dn_pre_specs(s_len, tm, lambda i: i)

    def kern(cur_ref, prev_ref, ba_ref, cw_ref, al_ref, dt_ref, q_ref, k_ref, v_ref, bb_ref, gb_ref, halo):
        xs = _conv_taps(cur_ref, prev_ref, halo, pl.program_id(0) == 0)
        outs = _f_dn_pre(*xs, ba_ref[...], cw_ref[...], al_ref[...], dt_ref[...])
        for ref, val in zip((q_ref, k_ref, v_ref, bb_ref, gb_ref), outs, strict=True):
            ref[...] = val

    return _pc(kern, "dn_pre_fwd", (s_len // tm,), [cur, prev, ba] + full, [row] * 5, [SDS((s_len, DNW), F32)] * 5,
               scratch=[pltpu.VMEM((tm + HALO, QKVW), F32)])(proj, proj, proj, conv_w, alog, dtb)


def _dn_pre_bwd(proj, conv_w, alog, dtb, cots):
    s_len = proj.shape[0]
    tm = min(128, s_len)
    nb = s_len // tm
    cur, prev, ba, row, full = _dn_pre_specs(s_len, tm, lambda i: nb - 1 - i)

    def kern(cur_ref, prev_ref, ba_ref, cw_ref, al_ref, dt_ref, dq_ref, dk_ref, dv_ref, dbb_ref, dgb_ref,
             dqkv_ref, dba_ref, dcw_ref, dal_ref, ddt_ref, halo, *tails):
        i = pl.program_id(0)
        _zero_first([dcw_ref, dal_ref, ddt_ref])

        @pl.when(i == 0)
        def _():
            for t in tails:
                t[tm:, :] = jnp.zeros((HALO, QKVW), F32)

        xs = _conv_taps(cur_ref, prev_ref, halo, i == nb - 1)
        _, vjp = jax.vjp(_f_dn_pre, *xs, ba_ref[...], cw_ref[...], al_ref[...], dt_ref[...])
        *dxs, dba, dcw, dal, ddt = vjp((dq_ref[...], dk_ref[...], dv_ref[...], dbb_ref[...], dgb_ref[...]))
        total = dxs[CONV - 1]
        for j, t in enumerate(tails):
            n = CONV - 1 - j
            t[0:tm, :] = dxs[j]
            total = total + t[n:n + tm, :]
            t[tm:, :] = dxs[j][0:HALO, :]
        dqkv_ref[...] = total.astype(BF16)
        dba_ref[...] = dba.astype(BF16)
        dcw_ref[...] += dcw
        dal_ref[...] += dal
        ddt_ref[...] += ddt

    return _pc(kern, "dn_pre_bwd", (nb,), [cur, prev, ba] + full + [row] * 5,
               [cur, pl.BlockSpec((tm, 128), lambda i: (nb - 1 - i, 0))] + full,
               [SDS((s_len, QKVW), BF16), SDS((s_len, 128), BF16), SDS((CONV, QKVW), F32), SDS((1, DN_H), F32),
                SDS((1, DN_H), F32)],
               scratch=[pltpu.VMEM((tm + HALO, QKVW), F32)] * CONV)(proj, proj, proj, conv_w, alog, dtb, *cots)


def _pad_w_in(w_in):
    pieces = [w_in[:, o0:o0 + w] for o0, w, _ in sorted(_ORIG_PIECES, key=lambda t: t[2])]
    pieces.append(jnp.zeros((w_in.shape[0], PW - D_IN), w_in.dtype))
    return jnp.concatenate(pieces, axis=1)


def _unpad_w_in(g):
    return jnp.concatenate([g[:, p0:p0 + w] for _, w, p0 in _ORIG_PIECES], axis=1)


def _local_step(x, target, wts):
    s_len = x.shape[0]
    tm = min(256, s_len)
    tmh = min(128, s_len)
    w_in_p = wts["w_in_p"]
    attn_gain = wts["attn_norm"]
    ffn_gain = wts["ffn_norm"]
    conv_w = wts["dn_conv"]
    alog, dtb, out_gain = wts["dn_a_log"], wts["dn_dt_bias"], wts["dn_out_norm"]
    qg, kg = wts["swa_q_norm"], wts["swa_k_norm"]
    sinks = wts["swa_sinks"].reshape(SWA_KV, 1, SWA_G)

    (h,) = _rows(lambda r, f: ([_f_rms(r[0], f[0])], []), "rms1_fwd", s_len, tm, [_whole(x)], [attn_gain],
                 [(D, BF16)])
    proj = _mm(h, w_in_p, "nn", F32, 512, 1024, "mm_proj")
    q_dn, k_dn, v_dn, bb, gb = _dn_pre_fwd(proj, conv_w, alog, dtb)
    o_dn, s_all, t_all = _dn_chunks_fwd(q_dn, k_dn, v_dn, gb, bb)
    post_ins = [_whole(o_dn), (proj, DNW, C_Z // DNW)]
    (y_dn,) = _rows(lambda r, f: ([_f_dn_post(r[0], r[1], f[0])], []), "dn_post_fwd", s_len, tm, post_ins,
                    [out_gain], [(DNW, BF16)])

    bias = _bias_expand(wts["rel_bias"].T).reshape(SWA_H, BLK, 2 * BLK)
    y_swa = _swa_fwd(proj, bias, qg, kg, sinks)

    wts = {**wts, **wts["late"](y_swa)}
    p_a, p_b, merged = _branch_merge(y_dn, y_swa, wts["wa"], wts["wb"], proj)
    x1, h2 = _out_proj(merged, wts["w_out"], x, ffn_gain)
    gt, up, act = _ffn_up(h2, wts["wg"], wts["wu"])
    dy, dy_b, loss = _ffn_down_loss(act, wts["wd"], x1, target)

    grads = {}
    d_gt, d_up = _ffn_dact(dy_b, wts["wd"], gt, up)
    (grads["w_down"],) = _gw_ffn([act], dy_b, "gw_down")
    grads["w_gate"], grads["w_up"] = _gw_ffn([d_gt, d_up], h2, "gw_gate_up")
    dx1, dx1_b, grads["ffn_norm"] = _ffn_dh2(d_gt, d_up, wts["wg"], wts["wu"], x1, dy, ffn_gain)
    grads["w_out"] = _mm(merged, dx1_b, "tn", BF16, 512, 512, "gw_out")
    d_pa, d_pb, d_gr = _merge_bwd(dx1_b, wts["w_out"], p_a, p_b, proj)
    d_ydn, d_yswa = _d_branch(d_pa, d_pb, wts["wa"], wts["wb"])
    grads["w_branch_dn"], grads["w_branch_swa"] = _gw_branch(y_dn, y_swa, d_pa, d_pb)
    token = wts["send_early"](grads)
    qg_t = qg + token[0:1, 0:1]
    out_gain_t = out_gain + token[0:1, 0:1]

    d_sq, d_sk, d_sv, d_bias, grads["swa_q_norm"], grads["swa_k_norm"], d_sinks = _swa_bwd(
        proj, bias, qg_t, kg, sinks, d_yswa)
    grads["swa_sinks"] = d_sinks.reshape(1, SWA_H)
    grads["rel_bias"] = _bias_reduce(d_bias.reshape(SWA_H, BLK * 2 * BLK)).T

    def post_bwd(r, f):
        _, vjp = jax.vjp(_f_dn_post, r[0], r[1], f[0])
        d_o, d_z, d_gain = vjp(r[2])
        return [d_o, d_z], [d_gain]

    d_o, d_z, grads["dn_out_norm"] = _rows(post_bwd, "dn_post_bwd", s_len, tm, post_ins + [_whole(d_ydn)], [out_gain_t],
                                           [(DNW, F32), (DNW, BF16)], [(1, DH)])
    d_q, d_k, d_v, d_gb, d_bb = _dn_chunks_bwd(q_dn, k_dn, v_dn, gb, bb, s_all, t_all, d_o)

    d_qkv, d_ba, grads["dn_conv"], grads["dn_a_log"], grads["dn_dt_bias"] = _dn_pre_bwd(
        proj, conv_w, alog, dtb, (d_q, d_k, d_v, d_bb, d_gb))

    d_proj = jnp.concatenate(
        [d_qkv, d_z, d_gr, d_sq, d_sk, d_sv, d_ba, jnp.zeros((s_len, PW - C_BA - 128), BF16)], axis=1)
    grads["w_in_p"] = _mm(h, d_proj, "tn", BF16, 512, 1024, "gw_in")
    grad_x, grads["attn_norm"] = _dh_rms(d_proj, w_in_p, x, dx1, attn_gain)
    return loss, grad_x, grads


_HBM = pl.BlockSpec(memory_space=pl.ANY)


def _place():
    return lax.axis_index("x"), lax.axis_index("y"), lax.axis_index("c")


def _other_chips(x, y):
    return [(1 - x, y), (x, 1 - y), (1 - x, 1 - y)]


def _rcopy(src, dst, send_sems, recv_sems, k, to):
    return pltpu.make_async_remote_copy(src_ref=src, dst_ref=dst, send_sem=send_sems.at[k], recv_sem=recv_sems.at[k],
                                        device_id=to, device_id_type=MESH)


def _comm_call(body, name, ins, out_shapes, n_remote, landing=0):
    first = len(ins) - landing
    return pl.pallas_call(
        body, name=name, in_specs=[_HBM] * len(ins), out_specs=[_HBM] * len(out_shapes), out_shape=out_shapes,
        scratch_shapes=[pltpu.SemaphoreType.DMA((n_remote,)), pltpu.SemaphoreType.DMA((n_remote,))],
        input_output_aliases={first + i: i for i in range(landing)},
        compiler_params=_cparams(has_side_effects=True),
    )(*ins)


def _own_slot(blocks, chip):
    return [lax.dynamic_update_slice(jnp.zeros((N_CHIPS,) + b.shape, b.dtype), b[None], (chip, 0, 0)) for b in blocks]


def _gather_weights(ws, chip):
    n = len(ws)
    halves = [w.shape[0] // 2 for w in ws]

    def body(*refs):
        w_refs, o_refs = refs[:n], refs[2 * n:3 * n]
        send_sems, recv_sems = refs[3 * n:]
        x, y, c = _place()
        s = 2 * x + y
        sib = (x, y, 1 - c)
        chips = _other_chips(x, y)

        def rows(i, half):
            return pl.ds(half * halves[i], halves[i])

        first = []
        for j, (cx, cy) in enumerate(chips):
            for i in range(n):
                cp = _rcopy(w_refs[i].at[rows(i, c), :], o_refs[i].at[s, rows(i, c), :], send_sems, recv_sems,
                            j * n + i, (cx, cy, c))
                cp.start()
                first.append(cp)
        passed = []
        for j, (cx, cy) in enumerate(chips):
            sj = 2 * cx + cy
            for i in range(n):
                blk = o_refs[i].at[sj, rows(i, c), :]
                _rcopy(blk, blk, send_sems, recv_sems, j * n + i, (cx, cy, c)).wait_recv()
                cp = _rcopy(blk, blk, send_sems, recv_sems, (3 + j) * n + i, sib)
                cp.start()
                passed.append(cp)
        for j, (cx, cy) in enumerate(chips):
            sj = 2 * cx + cy
            for i in range(n):
                blk = o_refs[i].at[sj, rows(i, 1 - c), :]
                _rcopy(blk, blk, send_sems, recv_sems, (3 + j) * n + i, sib).wait_recv()
        for cp in first + passed:
            cp.wait_send()

    return _comm_call(body, "gather_weights", list(ws) + _own_slot(ws, chip),
                      [SDS((N_CHIPS,) + w.shape, w.dtype) for w in ws], 6 * n, landing=n)


_HBM_ONLY = pl.BlockSpec(memory_space=pltpu.HBM)
_SEM = pl.BlockSpec(memory_space=pltpu.SEMAPHORE)
_DATAFLOW = pltpu.SideEffectType.DATAFLOW_SIDE_EFFECTING


def _in_hbm(a):
    return pltpu.with_memory_space_constraint(a, pltpu.HBM)


def _gather_windows(blocks):
    halves = [b.shape[0] // 2 for b in blocks]

    def src_at(ref, i, c, sj):
        return ref.at[pl.ds(c * halves[i], halves[i]), :]

    def dst_at(ref, i, c, s_from):
        return ref.at[s_from, pl.ds(c * halves[i], halves[i]), :]

    return src_at, dst_at


def _exchange_windows():
    return (lambda ref, i, c, sj: ref.at[sj]), (lambda ref, i, c, s_from: ref.at[s_from])


def _split_start(name, ws, lands, dep, windows):
    n = len(ws)
    src_at, dst_at = windows

    def body(*refs):
        w_refs, l_refs = refs[:n], refs[n:2 * n]
        send_sems, recv_sems = refs[2 * n + 1], refs[2 * n + 2]
        token = refs[-1]
        x, y, c = _place()
        s = 2 * x + y
        for j, (cx, cy) in enumerate(_other_chips(x, y)):
            for i in range(n):
                _rcopy(src_at(w_refs[i], i, c, 2 * cx + cy), dst_at(l_refs[i], i, c, s), send_sems, recv_sems,
                       j * n + i, (cx, cy, c)).start()
        token[...] = jnp.zeros_like(token)

    outs = pl.pallas_call(
        body, name=name,
        out_shape=(pltpu.SemaphoreType.DMA((3 * n,)), pltpu.SemaphoreType.DMA((3 * n,)),
                   *[pltpu.HBM(w.shape, w.dtype) for w in ws], *[pltpu.HBM(t.shape, t.dtype) for t in lands],
                   SDS((8, 128), F32)),
        in_specs=[_HBM_ONLY] * (2 * n) + [pl.BlockSpec(memory_space=pl.ANY)],
        out_specs=(_SEM, _SEM, *[_HBM_ONLY] * (2 * n), pl.BlockSpec(memory_space=pltpu.VMEM)),
        input_output_aliases={i: 2 + i for i in range(2 * n)},
        compiler_params=pltpu.CompilerParams(has_side_effects=_DATAFLOW),
    )(*[_in_hbm(w) for w in ws], *[_in_hbm(t) for t in lands], dep)
    return outs[0], outs[1], outs[2:2 + n], outs[2 + n:2 + 2 * n], outs[-1]


def _split_wait(name, w_thru, l_thru, send_sems, recv_sems, after, windows):
    n = len(w_thru)
    src_at, dst_at = windows

    def body(*refs):
        w_refs, l_refs = refs[:n], refs[n:2 * n]
        send_sems, recv_sems = refs[2 * n], refs[2 * n + 1]
        x, y, c = _place()
        for j, (cx, cy) in enumerate(_other_chips(x, y)):
            sj = 2 * cx + cy
            for i in range(n):
                cp = _rcopy(src_at(w_refs[i], i, c, sj), dst_at(l_refs[i], i, c, sj), send_sems, recv_sems, j * n + i,
                            (cx, cy, c))
                cp.wait_send()
                cp.wait_recv()

    outs = pl.pallas_call(
        body, name=name,
        out_shape=[pltpu.HBM(w.shape, w.dtype) for w in w_thru] + [pltpu.HBM(t.shape, t.dtype) for t in l_thru],
        in_specs=[_HBM_ONLY] * (2 * n) + [_SEM, _SEM, pl.BlockSpec(memory_space=pl.ANY)],
        out_specs=[_HBM_ONLY] * (2 * n),
        input_output_aliases={i: i for i in range(2 * n)},
        compiler_params=pltpu.CompilerParams(has_side_effects=_DATAFLOW),
    )(*w_thru, *l_thru, send_sems, recv_sems, after)
    return outs[n:]


def _sibling_fill(lands):
    n = len(lands)
    halves = [t.shape[1] // 2 for t in lands]

    def body(*refs):
        o_refs = refs[n:2 * n]
        send_sems, recv_sems = refs[2 * n:]
        x, y, c = _place()
        sib = (x, y, 1 - c)
        chips = _other_chips(x, y)
        sent = []
        for j, (cx, cy) in enumerate(chips):
            for i in range(n):
                blk = o_refs[i].at[2 * cx + cy, pl.ds(c * halves[i], halves[i]), :]
                cp = _rcopy(blk, blk, send_sems, recv_sems, j * n + i, sib)
                cp.start()
                sent.append(cp)
        for j, (cx, cy) in enumerate(chips):
            for i in range(n):
                blk = o_refs[i].at[2 * cx + cy, pl.ds((1 - c) * halves[i], halves[i]), :]
                _rcopy(blk, blk, send_sems, recv_sems, j * n + i, sib).wait_recv()
        for cp in sent:
            cp.wait_send()

    return _comm_call(body, "sibling_fill", list(lands), [SDS(t.shape, t.dtype) for t in lands], 3 * n, landing=n)


def _swap_halves(gs, name):
    n = len(gs)
    halves = [g.shape[1] // 2 for g in gs]

    def body(*refs):
        g_refs, o_refs = refs[:n], refs[n:2 * n]
        send_sems, recv_sems = refs[2 * n:]
        x, y, c = _place()
        cps = [_rcopy(g_refs[i].at[:, pl.ds((1 - c) * halves[i], halves[i]), :], o_refs[i], send_sems, recv_sems, i,
                      (x, y, 1 - c)) for i in range(n)]
        for cp in cps:
            cp.start()
        for cp in cps:
            cp.wait()

    return _comm_call(body, name, gs, [SDS((N_CHIPS, h, g.shape[2]), g.dtype) for g, h in zip(gs, halves)], n)


def _chip_exchange(ps, chip):
    n = len(ps)

    def body(*refs):
        p_refs, o_refs = refs[:n], refs[2 * n:3 * n]
        send_sems, recv_sems = refs[3 * n:]
        x, y, c = _place()
        s = 2 * x + y
        chips = _other_chips(x, y)
        sent = []
        for j, (cx, cy) in enumerate(chips):
            for i in range(n):
                cp = _rcopy(p_refs[i].at[2 * cx + cy], o_refs[i].at[s], send_sems, recv_sems, j * n + i, (cx, cy, c))
                cp.start()
                sent.append(cp)
        for j, (cx, cy) in enumerate(chips):
            sj = 2 * cx + cy
            for i in range(n):
                _rcopy(p_refs[i].at[sj], o_refs[i].at[sj], send_sems, recv_sems, j * n + i, (cx, cy, c)).wait_recv()
        for cp in sent:
            cp.wait_send()

    own = [lax.dynamic_index_in_dim(p, chip, axis=0, keepdims=False) for p in ps]
    return _comm_call(body, "chip_exchange", list(ps) + _own_slot(own, chip), [SDS(p.shape, p.dtype) for p in ps],
                      3 * n, landing=n)


def _swap_reduced(rs):
    n = len(rs)

    def body(*refs):
        r_refs, o_refs = refs[:n], refs[n:2 * n]
        send_sems, recv_sems = refs[2 * n:]
        x, y, c = _place()
        cps = [_rcopy(r_refs[i], o_refs[i], send_sems, recv_sems, i, (x, y, 1 - c)) for i in range(n)]
        for cp in cps:
            cp.start()
        for cp in cps:
            cp.wait()

    return _comm_call(body, "swap_reduced", rs, [SDS(r.shape, r.dtype) for r in rs], n)


def _all_sum_small(vec, name):
    n_dev = 8
    flips = [(bx, by, bc) for bx in (0, 1) for by in (0, 1) for bc in (0, 1)][1:]

    def body(v_ref, out_ref, gath, send_sems, recv_sems):
        x, y, c = _place()
        me = 4 * x + 2 * y + c
        gath[me] = v_ref[...]
        sent = []
        for k, (bx, by, bc) in enumerate(flips):
            peer = (x ^ bx, y ^ by, c ^ bc)
            cp = _rcopy(v_ref, gath.at[me], send_sems, recv_sems, k, peer)
            cp.start()
            sent.append(cp)
        for k, (bx, by, bc) in enumerate(flips):
            peer = (x ^ bx, y ^ by, c ^ bc)
            _rcopy(v_ref, gath.at[4 * peer[0] + 2 * peer[1] + peer[2]], send_sems, recv_sems, k, peer).wait_recv()
        for cp in sent:
            cp.wait_send()
        acc = gath[0]
        for d in range(1, n_dev):
            acc = acc + gath[d]
        out_ref[...] = acc

    vm = pl.BlockSpec(memory_space=pltpu.VMEM)
    return pl.pallas_call(
        body, name=name, in_specs=[vm], out_specs=vm, out_shape=SDS(vec.shape, F32),
        scratch_shapes=[pltpu.VMEM((n_dev,) + vec.shape, F32), pltpu.SemaphoreType.DMA((7,)),
                        pltpu.SemaphoreType.DMA((7,))],
        compiler_params=_cparams(has_side_effects=True),
    )(vec)


def _pack_small(vals, extra=None):
    parts = [vals[n].reshape(-1).astype(F32) for n, _ in _SMALL]
    parts.append(jnp.zeros((1,), F32) if extra is None else extra.reshape(1).astype(F32))
    flat = jnp.concatenate(parts)
    flat = jnp.concatenate([flat, jnp.zeros((_SMALL_ROWS * 128 - flat.shape[0],), F32)])
    return flat.reshape(_SMALL_ROWS, 128)


def _unpack_small(packed, shapes):
    flat = packed.reshape(-1)
    return {n: flat[_SMALL_OFF[n][0]:_SMALL_OFF[n][0] + _SMALL_OFF[n][1]].reshape(shapes[n]) for n, _ in _SMALL}


def _pair_sum(gs, gots, core, name):
    n = len(gs)

    def kern(c_ref, *refs):
        for i in range(n):
            refs[2 * n + i][...] = (refs[i][...].astype(F32) + refs[n + i][...].astype(F32)).astype(BF16)

    in_specs = [pl.BlockSpec((1, t.shape[1], t.shape[2]), lambda s, c_ref: (s, c_ref[0], 0)) for t in gots]
    in_specs += [pl.BlockSpec((1, t.shape[1], t.shape[2]), lambda s, c_ref: (s, 0, 0)) for t in gots]
    out_specs = [pl.BlockSpec((1, t.shape[1], t.shape[2]), lambda s, c_ref: (s, 0, 0)) for t in gots]
    return pl.pallas_call(
        kern, name=name,
        grid_spec=pltpu.PrefetchScalarGridSpec(num_scalar_prefetch=1, grid=(N_CHIPS,), in_specs=in_specs,
                                               out_specs=out_specs),
        out_shape=[SDS(t.shape, BF16) for t in gots],
        compiler_params=_cparams(dimension_semantics=("arbitrary",)),
    )(core.reshape(1).astype(jnp.int32), *gs, *gots)


def _chip_sum(qs):
    n = len(qs)

    def kern(*refs):
        for i in range(n):
            acc = refs[i][0].astype(F32)
            for s in range(1, N_CHIPS):
                acc = acc + refs[i][s].astype(F32)
            refs[n + i][...] = acc

    in_specs = [pl.BlockSpec((N_CHIPS, q.shape[1] // 2, q.shape[2]), lambda j: (0, j, 0)) for q in qs]
    out_specs = [pl.BlockSpec((q.shape[1] // 2, q.shape[2]), lambda j: (j, 0)) for q in qs]
    return _pc(kern, "chip_sum", (2,), in_specs, out_specs, [SDS(q.shape[1:], F32) for q in qs])(*qs)


def _adam_math(w_, g_, m_, v_):
    m_ = ADAM_B1 * m_ + (1.0 - ADAM_B1) * g_
    v_ = ADAM_B2 * v_ + (1.0 - ADAM_B2) * jnp.square(g_)
    m_hat = m_ / (1.0 - ADAM_B1 ** ADAM_STEP)
    v_hat = v_ / (1.0 - ADAM_B2 ** ADAM_STEP)
    return -ADAM_LR * (m_hat / (jnp.sqrt(v_hat) + ADAM_EPS) + ADAM_WD * w_), m_, v_


def _adamw(w, g, m, v, name):
    rows, cols = w.shape
    tr = rows
    for cand in (256, 128, 64, 32, 16, 8):
        if rows % cand == 0 and rows > cand:
            tr = cand
            break

    def kern(w_ref, g_ref, m_ref, v_ref, d_ref, nm_ref, nv_ref):
        d_ref[...], nm_ref[...], nv_ref[...] = _adam_math(w_ref[...], g_ref[...], m_ref[...], v_ref[...])

    spec = pl.BlockSpec((tr, cols), lambda i: (i, 0))
    return _pc(kern, name, (rows // tr,), [spec] * 4, [spec] * 3, [SDS(w.shape, F32)] * 3)(w, g, m, v)


def _adamw_big(w, mine, theirs, m, v, core, name):
    _, rows, cols = w.shape
    half = rows // 2
    tr = next(t for t in (256, 176, 128, 64, 32, 16, 8) if half % t == 0)
    nbh = half // tr

    def kern(c_ref, w_ref, a_ref, b_ref, m_ref, v_ref, g_ref, d_ref, nm_ref, nv_ref):
        g_ = jnp.where(pl.program_id(0) // nbh == c_ref[0], a_ref[...], b_ref[...])
        g_ref[0] = g_
        d_ref[0], nm_ref[0], nv_ref[0] = _adam_math(w_ref[0], g_, m_ref[0], v_ref[0])

    full = pl.BlockSpec((1, tr, cols), lambda i, c_ref: (0, i, 0))
    part = pl.BlockSpec((tr, cols), lambda i, c_ref: (i % nbh, 0))
    return pl.pallas_call(
        kern, name=name,
        grid_spec=pltpu.PrefetchScalarGridSpec(num_scalar_prefetch=1, grid=(rows // tr,),
                                               in_specs=[full, part, part, full, full], out_specs=[full] * 4),
        out_shape=[SDS(w.shape, F32)] * 4,
        compiler_params=_cparams(dimension_semantics=("arbitrary",)),
    )(core.reshape(1).astype(jnp.int32), w, mine, theirs, m, v)


_WEIGHT_NAMES = ("attn_norm", "w_in", "dn_conv", "dn_a_log", "dn_dt_bias", "dn_out_norm", "swa_q_norm", "swa_k_norm",
                 "swa_sinks", "rel_bias", "w_branch_dn", "w_branch_swa", "w_out", "ffn_norm", "w_gate", "w_up",
                 "w_down")
_CONV_SH = QKVW // N_CHIPS


def kernel(x, attn_norm, w_in, dn_conv, dn_a_log, dn_dt_bias, dn_out_norm, swa_q_norm, swa_k_norm, swa_sinks, rel_bias, w_branch_dn, w_branch_swa, w_out, ffn_norm, w_gate, w_up, w_down, loss_target, m_attn_norm, m_w_in, m_dn_conv, m_dn_a_log, m_dn_dt_bias, m_dn_out_norm, m_swa_q_norm, m_swa_k_norm, m_swa_sinks, m_rel_bias, m_w_branch_dn, m_w_branch_swa, m_w_out, m_ffn_norm, m_w_gate, m_w_up, m_w_down, v_attn_norm, v_w_in, v_dn_conv, v_dn_a_log, v_dn_dt_bias, v_dn_out_norm, v_swa_q_norm, v_swa_k_norm, v_swa_sinks, v_rel_bias, v_w_branch_dn, v_w_branch_swa, v_w_out, v_ffn_norm, v_w_gate, v_w_up, v_w_down):
    w = dict(attn_norm=attn_norm, w_in=w_in, dn_conv=dn_conv, dn_a_log=dn_a_log, dn_dt_bias=dn_dt_bias,
             dn_out_norm=dn_out_norm, swa_q_norm=swa_q_norm, swa_k_norm=swa_k_norm, swa_sinks=swa_sinks,
             rel_bias=rel_bias, w_branch_dn=w_branch_dn, w_branch_swa=w_branch_swa, w_out=w_out, ffn_norm=ffn_norm,
             w_gate=w_gate, w_up=w_up, w_down=w_down)
    m = dict(attn_norm=m_attn_norm, w_in=m_w_in, dn_conv=m_dn_conv, dn_a_log=m_dn_a_log, dn_dt_bias=m_dn_dt_bias,
             dn_out_norm=m_dn_out_norm, swa_q_norm=m_swa_q_norm, swa_k_norm=m_swa_k_norm, swa_sinks=m_swa_sinks,
             rel_bias=m_rel_bias, w_branch_dn=m_w_branch_dn, w_branch_swa=m_w_branch_swa, w_out=m_w_out,
             ffn_norm=m_ffn_norm, w_gate=m_w_gate, w_up=m_w_up, w_down=m_w_down)
    v = dict(attn_norm=v_attn_norm, w_in=v_w_in, dn_conv=v_dn_conv, dn_a_log=v_dn_a_log, dn_dt_bias=v_dn_dt_bias,
             dn_out_norm=v_dn_out_norm, swa_q_norm=v_swa_q_norm, swa_k_norm=v_swa_k_norm, swa_sinks=v_swa_sinks,
             rel_bias=v_rel_bias, w_branch_dn=v_w_branch_dn, w_branch_swa=v_w_branch_swa, w_out=v_w_out,
             ffn_norm=v_ffn_norm, w_gate=v_w_gate, w_up=v_w_up, w_down=v_w_down)
    shapes = {n: w[n].shape for n in _WEIGHT_NAMES}

    def two_d(a):
        return a.reshape(a.shape[-2], a.shape[-1]) if a.ndim == 3 else a

    core = lax.axis_index("c")
    chip = 2 * lax.axis_index("x") + lax.axis_index("y")
    small_shapes = {n: two_d(w[n]).shape for n, _ in _SMALL}
    small_shapes["dn_conv"] = (CONV, QKVW)

    conv_loc = two_d(w["dn_conv"])
    conv_part = lax.dynamic_update_slice(jnp.zeros((CONV, QKVW), F32), jnp.where(core == 0, conv_loc, 0.0),
                                         (0, chip * _CONV_SH))
    conv_full = _all_sum_small(conv_part.reshape(CONV * QKVW // 128, 128), "gather_conv").reshape(CONV, QKVW)

    flipped = ("w_gate", "w_up")

    def natural(a, n):
        return a.transpose(0, 2, 1) if n in flipped else a

    w_bf = [two_d(natural(w[n], n).astype(BF16)) for n in _BIG_NAMES]
    (w_in_g,) = _gather_weights(w_bf[:1], chip)
    windows = _gather_windows(w_bf[1:])
    send_sems, recv_sems, w_thru, l_thru, token = _split_start(
        "gather_start", w_bf[1:], _own_slot(w_bf[1:], chip), w_in_g[0, :8, :128], windows)

    def late(after):
        lands = _split_wait("gather_wait", w_thru, l_thru, send_sems, recv_sems, after, windows)
        g = dict(zip(_BIG_NAMES[1:], _sibling_fill(lands)))
        return dict(wa=g["w_branch_dn"], wb=g["w_branch_swa"], w_out=g["w_out"].reshape(D, D), wg=g["w_gate"],
                    wu=g["w_up"], wd=g["w_down"])

    w_in_full = w_in_g.transpose(1, 0, 2).reshape(D, D_IN)
    wts = dict(w_in_p=_pad_w_in(w_in_full), dn_conv=conv_full, late=late)
    for n, _ in _SMALL[:-1]:
        wts[n] = two_d(w[n])
    wts["attn_norm"] = wts["attn_norm"] + token[0:1, 0:1]

    early = {}

    def send_early(grads):
        gs = [grads["w_branch_dn"], grads["w_branch_swa"], grads["w_out"].reshape(N_CHIPS, CSH, D), grads["w_gate"],
              grads["w_up"], grads["w_down"]]
        parts = _pair_sum(gs, _swap_halves(gs, "swap_halves_early"), core, "pair_sum_early")
        own = [lax.dynamic_index_in_dim(p, chip, axis=0, keepdims=False) for p in parts]
        early["sems"], early["recv"], early["src"], early["land"], tok = _split_start(
            "exchange_start", parts, _own_slot(own, chip), parts[0][0, :8, :128], _exchange_windows())
        return tok

    wts["send_early"] = send_early
    loss_sum, grad_x, grads = _local_step(x[0], loss_target[0], wts)

    q_early = _split_wait("exchange_wait", early["src"], early["land"], early["sems"], early["recv"],
                          grads["w_in_p"], _exchange_windows())
    g_in = [_unpad_w_in(grads["w_in_p"]).reshape(D, N_CHIPS, D_IN // N_CHIPS).transpose(1, 0, 2)]
    parts_in = _pair_sum(g_in, _swap_halves(g_in, "swap_halves_in"), core, "pair_sum_in")
    reduced = _chip_sum(list(_chip_exchange(parts_in, chip)) + list(q_early))
    theirs = _swap_reduced(reduced)

    small_sum = _all_sum_small(_pack_small(grads, loss_sum), "all_sum_small")
    loss = small_sum.reshape(-1)[_LOSS_OFF]
    g_small = _unpack_small(small_sum, small_shapes)

    g_out, d_out, m_out, v_out = {}, {}, {}, {}
    for n, mine, other in zip(_BIG_NAMES, reduced, theirs):
        res = _adamw_big(natural(w[n], n), mine, other, natural(m[n], n), natural(v[n], n), core, "adamw_" + n)
        g_out[n], d_out[n], m_out[n], v_out[n] = (natural(t, n) for t in res)
    g_conv = lax.dynamic_slice(g_small["dn_conv"], (0, chip * _CONV_SH), (CONV, _CONV_SH))
    g_out["dn_conv"] = g_conv.reshape(shapes["dn_conv"])
    d_, m_, v_ = _adamw(conv_loc, g_conv, two_d(m["dn_conv"]), two_d(v["dn_conv"]), "adamw_dn_conv")
    d_out["dn_conv"], m_out["dn_conv"], v_out["dn_conv"] = (t.reshape(shapes["dn_conv"]) for t in (d_, m_, v_))

    def packed(src):
        vals = {n: src[n] for n, _ in _SMALL[:-1]}
        vals["dn_conv"] = jnp.zeros((CONV * QKVW,), F32)
        return _pack_small(vals)

    d_s, m_s, v_s = _adamw(packed(w), small_sum, packed(m), packed(v), "adamw_small")
    d_small, m_small, v_small = (_unpack_small(t, small_shapes) for t in (d_s, m_s, v_s))
    for n, _ in _SMALL[:-1]:
        g_out[n] = g_small[n].reshape(shapes[n])
        d_out[n], m_out[n], v_out[n] = (t[n].reshape(shapes[n]) for t in (d_small, m_small, v_small))

    return (loss, grad_x[None], *[g_out[n] for n in _WEIGHT_NAMES], *[d_out[n] for n in _WEIGHT_NAMES],
            *[m_out[n] for n in _WEIGHT_NAMES], *[v_out[n] for n in _WEIGHT_NAMES])
```

```python
import functools
import math

import numpy as np
import jax
import jax.numpy as jnp
from jax import lax
from jax.experimental import pallas as pl
from jax.experimental.pallas import tpu as pltpu

F32 = jnp.float32
BF16 = jnp.bfloat16
SDS = jax.ShapeDtypeStruct

D = 1024
DN_H = 4
DH = 128
DNW = DN_H * DH
QKVW = 3 * DNW
CONV = 4
CHUNK = 64
SWA_H = 8
SWA_KV = 2
SWA_G = SWA_H // SWA_KV
SWA_D = 64
SWAW = SWA_H * SWA_D
SWAKW = SWA_KV * SWA_D
BLK = 128
NBUCKET = 32
MAXDIST = 128
DFF = 2816
D_IN = QKVW + DNW + 2 * DN_H + SWAW + 2 * SWAKW + 2 * D
EPS = 1e-6
NEG = -1e30

ADAM_LR = 0.001
ADAM_B1 = 0.9
ADAM_B2 = 0.999
ADAM_EPS = 1e-08
ADAM_WD = 0.01
ADAM_STEP = 10

C_QKV, C_Z, C_GATE, C_SQ, C_SK, C_SV, C_BA = 0, 1536, 2048, 4096, 4608, 4736, 4864
PW = 5120
_ORIG_PIECES = (
    (0, QKVW, C_QKV),
    (QKVW, DNW, C_Z),
    (QKVW + DNW, 2 * DN_H, C_BA),
    (QKVW + DNW + 2 * DN_H, SWAW, C_SQ),
    (QKVW + DNW + 2 * DN_H + SWAW, SWAKW, C_SK),
    (QKVW + DNW + 2 * DN_H + SWAW + SWAKW, SWAKW, C_SV),
    (QKVW + DNW + 2 * DN_H + SWAW + 2 * SWAKW, 2 * D, C_GATE),
)

N_CHIPS = 4
FSH = DFF // N_CHIPS
CSH = D // N_CHIPS
VMEM_LIMIT = 48 * 1024 * 1024
MESH = pl.DeviceIdType.MESH

_BIG = (
    ("w_in", D, D_IN // N_CHIPS),
    ("w_branch_dn", DNW, CSH),
    ("w_branch_swa", SWAW, CSH),
    ("w_out", CSH, D),
    ("w_gate", FSH, D),
    ("w_up", FSH, D),
    ("w_down", FSH, D),
)
_BIG_NAMES = tuple(n for n, _, _ in _BIG)

_SMALL = (
    ("attn_norm", D), ("ffn_norm", D), ("dn_out_norm", DH), ("swa_q_norm", SWA_D), ("swa_k_norm", SWA_D),
    ("swa_sinks", SWA_H), ("dn_a_log", DN_H), ("dn_dt_bias", DN_H), ("rel_bias", NBUCKET * SWA_H),
    ("dn_conv", CONV * QKVW),
)
_SMALL_OFF = {}
_o = 0
for _n, _s in _SMALL:
    _SMALL_OFF[_n] = (_o, _s)
    _o += _s
_LOSS_OFF = _o
_SMALL_ROWS = -(-(_o + 1) // (8 * 128)) * 8


def _cparams(**kw):
    return pltpu.CompilerParams(vmem_limit_bytes=VMEM_LIMIT, **kw)


_DIMS = {
    "nn": (((1,), (0,)), ((), ())),
    "nt": (((1,), (1,)), ((), ())),
    "tn": (((0,), (0,)), ((), ())),
    "bnn": (((2,), (1,)), ((0,), (0,))),
    "bnt": (((2,), (2,)), ((0,), (0,))),
    "btn": (((1,), (1,)), ((0,), (0,))),
}


def _raw_dot(a, b, kind, exact):
    if exact:
        prec = lax.Precision.HIGH if exact == "x3" else lax.Precision.HIGHEST
        return lax.dot_general(a, b, _DIMS[kind], precision=prec, preferred_element_type=F32)
    return lax.dot_general(a.astype(BF16), b.astype(BF16), _DIMS[kind], preferred_element_type=F32)


@functools.partial(jax.custom_vjp, nondiff_argnums=(2, 3))
def _dot(a, b, kind, exact):
    return _raw_dot(a, b, kind, exact)


def _dot_fwd(a, b, kind, exact):
    return _raw_dot(a, b, kind, exact), (a, b)


def _dot_bwd(kind, exact, res, g):
    a, b = res
    pre = kind[:-2]
    nn, nt, tn = pre + "nn", pre + "nt", pre + "tn"
    if kind == nn:
        return _dot(g, b, nt, exact), _dot(a, g, tn, exact)
    if kind == nt:
        return _dot(g, b, nn, exact), _dot(g, a, tn, exact)
    return _dot(b, g, nt, exact), _dot(a, g, nn, exact)


_dot.defvjp(_dot_fwd, _dot_bwd)


def _silu(x):
    return x * jax.nn.sigmoid(x)


def _f_rms(x, gain):
    return x * lax.rsqrt(jnp.mean(x * x, axis=-1, keepdims=True) + EPS) * gain


def _f_dn_pre(xs0, xs1, xs2, xs3, ba, cw, alog, dtb):
    rows = xs0.shape[0]
    c = xs0 * cw[0:1] + xs1 * cw[1:2] + xs2 * cw[2:3] + xs3 * cw[3:4]
    qkv = _silu(c)
    qs, ks, bbs, gbs = [], [], [], []
    for h in range(DN_H):
        qh = qkv[:, h * DH:(h + 1) * DH]
        kh = qkv[:, DNW + h * DH:DNW + (h + 1) * DH]
        qs.append(qh * lax.rsqrt(jnp.sum(qh * qh, axis=-1, keepdims=True) + EPS) * (DH ** -0.5))
        ks.append(kh * lax.rsqrt(jnp.sum(kh * kh, axis=-1, keepdims=True) + EPS))
        beta = jax.nn.sigmoid(ba[:, h:h + 1])
        ar = ba[:, DN_H + h:DN_H + h + 1] + dtb[:, h:h + 1]
        softplus = jnp.maximum(ar, 0.0) + jnp.log1p(jnp.exp(-jnp.abs(ar)))
        g = -jnp.exp(alog[:, h:h + 1]) * softplus
        bbs.append(jnp.broadcast_to(beta, (rows, DH)))
        gbs.append(jnp.broadcast_to(g, (rows, DH)))
    return (jnp.concatenate(qs, axis=1), jnp.concatenate(ks, axis=1), qkv[:, 2 * DNW:],
            jnp.concatenate(bbs, axis=1), jnp.concatenate(gbs, axis=1))


def _f_dn_post(o, z, gain):
    ys = []
    for h in range(DN_H):
        oh = o[:, h * DH:(h + 1) * DH]
        zh = z[:, h * DH:(h + 1) * DH]
        ys.append(oh * lax.rsqrt(jnp.mean(oh * oh, axis=-1, keepdims=True) + EPS) * gain * _silu(zh))
    return jnp.concatenate(ys, axis=1)


def _f_merge(pa, pb, ga, gb):
    return jax.nn.sigmoid(ga) * pa + jax.nn.sigmoid(gb) * pb


def _f_swiglu(g, u):
    return _silu(g) * u


@jax.custom_vjp
def _unit_lower_inverse(a):
    c = a.shape[-1]
    eye = (lax.broadcasted_iota(jnp.int32, a.shape, 1) == lax.broadcasted_iota(jnp.int32, a.shape, 2)).astype(F32)
    p = -a
    t = eye + p
    for _ in range(max(c.bit_length() - 2, 0)):
        p = _raw_dot(p, p, "bnn", "x3")
        t = t + _raw_dot(t, p, "bnn", "x3")
    return t


def _unit_lower_inverse_fwd(a):
    t = _unit_lower_inverse(a)
    return t, t


def _unit_lower_inverse_bwd(t, g):
    return (-_raw_dot(_raw_dot(t, g, "btn", "x3"), t, "bnt", "x3"),)


_unit_lower_inverse.defvjp(_unit_lower_inverse_fwd, _unit_lower_inverse_bwd)


@jax.custom_vjp
def _known_inverse(a, t):
    return t


def _known_inverse_fwd(a, t):
    return t, t


def _known_inverse_bwd(t, g):
    return _unit_lower_inverse_bwd(t, g)[0], jnp.zeros_like(t)


_known_inverse.defvjp(_known_inverse_fwd, _known_inverse_bwd)


def _f_chunk(q, k, v, gb, bb, s, t_known=None, with_t=False):
    c = CHUNK
    nh = q.shape[0]
    ii = lax.broadcasted_iota(jnp.int32, (nh, c, c), 1)
    jj = lax.broadcasted_iota(jnp.int32, (nh, c, c), 2)
    incl = ii >= jj
    strict = ii > jj
    eye = (ii == jj).astype(F32)
    gcb = _dot(incl.astype(F32), gb, "bnn", True)
    lane0 = (lax.broadcasted_iota(jnp.int32, (nh, c, DH), 2) == 0).astype(F32)
    gcol = gcb[:, :, :c]
    grow = _dot(lane0, gcb, "bnt", True)
    decay = jnp.where(incl, jnp.exp(jnp.where(incl, gcol - grow, 0.0)), 0.0)
    kb = k * bb
    vb = v * bb
    a = jnp.where(strict, _dot(kb, k, "bnt", False) * decay, 0.0)
    t = _unit_lower_inverse(a) if t_known is None else _known_inverse(a, t_known)
    eg = jnp.exp(gcb)
    u = _dot(t, vb, "bnn", "x3")
    w = _dot(t, kb * eg, "bnn", "x3")
    qk = jnp.where(incl, _dot(q, k, "bnt", False) * decay, 0.0)
    qe = q * eg
    glast = gcb[:, c - 1:c, :]
    k_dec = k * jnp.exp(glast - gcb)
    e_last = jnp.exp(glast)
    outs = []
    for g in range(nh // DN_H):
        sl = slice(g * DN_H, (g + 1) * DN_H)
        v_new = u[sl] - _dot(w[sl], s, "bnn", False)
        outs.append(_dot(qe[sl], s, "bnn", False) + _dot(qk[sl], v_new, "bnn", False))
        s = s * e_last[sl] + _dot(k_dec[sl], v_new, "btn", False)
    o = jnp.concatenate(outs, axis=0)
    return (o, s, t) if with_t else (o, s)


def _f_swa(q8, kp, kc, vp, vc, bias8, qg, kg, sink, mask):
    kb = jnp.concatenate([kp, kc], axis=1)
    vb = jnp.concatenate([vp, vc], axis=1)
    kn = kb * lax.rsqrt(jnp.mean(kb * kb, axis=-1, keepdims=True) + EPS) * kg

    def rows(per_head):
        return jnp.stack([jnp.concatenate([per_head(kv, g) for g in range(SWA_G)], axis=0)
                          for kv in range(SWA_KV)], axis=0)

    qq = rows(lambda kv, g: q8[kv * SWA_G + g])
    qn = qq * lax.rsqrt(jnp.mean(qq * qq, axis=-1, keepdims=True) + EPS) * qg
    lg = _dot(qn, kn, "bnt", False) * (SWA_D ** -0.5) + rows(lambda kv, g: bias8[kv * SWA_G + g])
    lg = jnp.where(rows(lambda kv, g: mask), lg, NEG)
    sk = rows(lambda kv, g: jnp.broadcast_to(sink[kv][:, g:g + 1], (BLK, 1)))
    m = lax.stop_gradient(jnp.maximum(jnp.max(lg, axis=-1, keepdims=True), sk))
    p = jnp.exp(lg - m)
    den = jnp.sum(p, axis=-1, keepdims=True) + jnp.exp(sk - m)
    out = _dot(p / den, vb, "bnn", False)
    return jnp.stack([out[kv, g * BLK:(g + 1) * BLK] for kv in range(SWA_KV) for g in range(SWA_G)], axis=0)


def _bdot(a, b, kind="nn"):
    return lax.dot_general(a.astype(BF16), b.astype(BF16), _DIMS[kind], preferred_element_type=F32)


def _pc(kern, name, grid, in_specs, out_specs, out_shape, scratch=()):
    return pl.pallas_call(
        kern, name=name, grid=grid, in_specs=in_specs, out_specs=out_specs, out_shape=out_shape,
        scratch_shapes=list(scratch), compiler_params=_cparams(dimension_semantics=("arbitrary",) * len(grid)))


def _mm(a, b, kind, out_dtype, tm, tn, name):
    if kind == "tn":
        k, m = a.shape
    else:
        m, k = a.shape
    n = b.shape[0] if kind == "nt" else b.shape[1]
    tm, tn = min(tm, m), min(tn, n)
    assert m % tm == 0 and n % tn == 0, (name, a.shape, b.shape, tm, tn)

    def kern(a_ref, b_ref, o_ref):
        o_ref[...] = _bdot(a_ref[...], b_ref[...], kind).astype(o_ref.dtype)

    a_spec = pl.BlockSpec((k, tm), lambda i, j: (0, i)) if kind == "tn" else pl.BlockSpec((tm, k), lambda i, j: (i, 0))
    b_spec = pl.BlockSpec((tn, k), lambda i, j: (j, 0)) if kind == "nt" else pl.BlockSpec((k, tn), lambda i, j: (0, j))
    return _pc(kern, name, (m // tm, n // tn), [a_spec, b_spec], pl.BlockSpec((tm, tn), lambda i, j: (i, j)),
               SDS((m, n), out_dtype))(a, b)


def _rows(body, name, m, tm, row_ins, full_ins, row_outs, acc_outs=()):
    n_r, n_f, n_o, n_a = len(row_ins), len(full_ins), len(row_outs), len(acc_outs)
    assert m % tm == 0

    def kern(*refs):
        r = refs[:n_r]
        f = refs[n_r:n_r + n_f]
        o = refs[n_r + n_f:n_r + n_f + n_o]
        acc = refs[n_r + n_f + n_o:]
        outs, sums = body([x[...] for x in r], [x[...] for x in f])
        for ref, val in zip(o, outs, strict=True):
            ref[...] = val.astype(ref.dtype)
        if n_a:
            @pl.when(pl.program_id(0) == 0)
            def _():
                for ref in acc:
                    ref[...] = jnp.zeros(ref.shape, F32)

            for ref, val in zip(acc, sums, strict=True):
                ref[...] += val

    in_specs = [pl.BlockSpec((tm, w), functools.partial(lambda i, cb: (i, cb), cb=cb)) for _, w, cb in row_ins]
    in_specs += [pl.BlockSpec(x.shape, lambda i: (0, 0)) for x in full_ins]
    out_specs = [pl.BlockSpec((tm, w), lambda i: (i, 0)) for w, _ in row_outs]
    out_specs += [pl.BlockSpec(s, lambda i: (0, 0)) for s in acc_outs]
    out_shape = [SDS((m, w), dt) for w, dt in row_outs]
    out_shape += [SDS(s, F32) for s in acc_outs]
    return _pc(kern, name, (m // tm,), in_specs, out_specs, out_shape)(*[x for x, _, _ in row_ins], *full_ins)


def _whole(x):
    return (x, x.shape[1], 0)


def _zero_first(refs):
    @pl.when(pl.program_id(0) == 0)
    def _():
        for ref in refs:
            ref[...] = jnp.zeros(ref.shape, F32)


GROUP = 4


def _heads(ref):
    return jnp.stack([ref[g * CHUNK:(g + 1) * CHUNK, h * DH:(h + 1) * DH]
                      for g in range(GROUP) for h in range(DN_H)], axis=0)


def _unheads(ref, val):
    for g in range(GROUP):
        for h in range(DN_H):
            ref[g * CHUNK:(g + 1) * CHUNK, h * DH:(h + 1) * DH] = val[g * DN_H + h]


def _dn_chunks_fwd(q, k, v, gb, bb):
    s_len = q.shape[0]
    ng = s_len // (GROUP * CHUNK)

    def kern(q_ref, k_ref, v_ref, g_ref, b_ref, o_ref, sall_ref, t_ref, state):
        _zero_first([state])
        s = state[...]
        sall_ref[0] = s
        o, s_new, t = _f_chunk(*[_heads(r) for r in (q_ref, k_ref, v_ref, g_ref, b_ref)], s, with_t=True)
        _unheads(o_ref, o)
        t_ref[0] = t
        state[...] = s_new

    blk = pl.BlockSpec((GROUP * CHUNK, DNW), lambda c: (c, 0))
    return _pc(kern, "dn_chunks_fwd", (ng,), [blk] * 5,
               [blk, pl.BlockSpec((1, DN_H, DH, DH), lambda c: (c, 0, 0, 0)),
                pl.BlockSpec((1, GROUP * DN_H, CHUNK, CHUNK), lambda c: (c, 0, 0, 0))],
               [SDS((s_len, DNW), F32), SDS((ng, DN_H, DH, DH), F32), SDS((ng, GROUP * DN_H, CHUNK, CHUNK), F32)],
               scratch=[pltpu.VMEM((DN_H, DH, DH), F32)])(q, k, v, gb, bb)


def _dn_chunks_bwd(q, k, v, gb, bb, s_all, t_all, d_o):
    s_len = q.shape[0]
    ng = s_len // (GROUP * CHUNK)

    def kern(q_ref, k_ref, v_ref, g_ref, b_ref, sall_ref, t_ref, do_ref, dq_ref, dk_ref, dv_ref, dg_ref, db_ref,
             dstate):
        _zero_first([dstate])
        fn = functools.partial(_f_chunk, t_known=t_ref[0])
        _, vjp = jax.vjp(fn, *[_heads(r) for r in (q_ref, k_ref, v_ref, g_ref, b_ref)], sall_ref[0])
        *d_ins, ds = vjp((_heads(do_ref), dstate[...]))
        for ref, val in zip((dq_ref, dk_ref, dv_ref, dg_ref, db_ref), d_ins, strict=True):
            _unheads(ref, val)
        dstate[...] = ds

    blk = pl.BlockSpec((GROUP * CHUNK, DNW), lambda c: (ng - 1 - c, 0))
    return _pc(kern, "dn_chunks_bwd", (ng,),
               [blk] * 5 + [pl.BlockSpec((1, DN_H, DH, DH), lambda c: (ng - 1 - c, 0, 0, 0)),
                            pl.BlockSpec((1, GROUP * DN_H, CHUNK, CHUNK), lambda c: (ng - 1 - c, 0, 0, 0)), blk],
               [blk] * 5, [SDS((s_len, DNW), F32)] * 5,
               scratch=[pltpu.VMEM((DN_H, DH, DH), F32)])(q, k, v, gb, bb, s_all, t_all, d_o)


def _t5_bucket_table():
    qi = np.arange(BLK)[:, None]
    kj = np.arange(2 * BLK)[None, :]
    dist = BLK + qi - kj
    n = np.maximum(dist, 0)
    max_exact = NBUCKET // 2
    nf = np.maximum(n, 1).astype(np.float32)
    large = max_exact + (np.log(nf / np.float32(max_exact)) / np.float32(math.log(MAXDIST / max_exact))
                         * np.float32(NBUCKET - max_exact)).astype(np.int32)
    large = np.minimum(large, NBUCKET - 1)
    return np.where(n < max_exact, n, large)


def _bucket_onehot_t():
    table = _t5_bucket_table().reshape(-1)
    return (np.arange(NBUCKET)[:, None] == table[None, :]).astype(np.float32)


def _swa_mask(first):
    qi = lax.broadcasted_iota(jnp.int32, (BLK, 2 * BLK), 0)
    kj = lax.broadcasted_iota(jnp.int32, (BLK, 2 * BLK), 1)
    dist = BLK + qi - kj
    window = (dist >= 0) & (dist < BLK)
    return window & ((kj >= BLK) | jnp.logical_not(first))


def _bias_expand(rel_bias_t):
    onehot = jnp.asarray(_bucket_onehot_t())

    def kern(r_ref, oh_ref, o_ref):
        o_ref[...] = _raw_dot(r_ref[...], oh_ref[...], "nn", True)

    return pl.pallas_call(
        kern, name="bias_expand", out_shape=SDS((SWA_H, BLK * 2 * BLK), F32), compiler_params=_cparams(),
    )(rel_bias_t, onehot)


def _bias_reduce(d_bias_flat):
    onehot = jnp.asarray(_bucket_onehot_t())

    def kern(d_ref, oh_ref, o_ref):
        o_ref[...] = _raw_dot(d_ref[...], oh_ref[...], "nt", True)

    return pl.pallas_call(
        kern, name="bias_reduce", out_shape=SDS((SWA_H, NBUCKET), F32), compiler_params=_cparams(),
    )(d_bias_flat, onehot)


def _swa_specs(nb, rev):
    def blk(n):
        return (nb - 1 - n) if rev else n

    def before(n):
        return jnp.maximum(blk(n) - 1, 0)

    q_spec = pl.BlockSpec((BLK, SWAW), lambda n: (blk(n), C_SQ // SWAW))
    k_cur = pl.BlockSpec((BLK, SWAKW), lambda n: (blk(n), C_SK // SWAKW))
    k_prev = pl.BlockSpec((BLK, SWAKW), lambda n: (before(n), C_SK // SWAKW))
    v_cur = pl.BlockSpec((BLK, SWAKW), lambda n: (blk(n), C_SV // SWAKW))
    v_prev = pl.BlockSpec((BLK, SWAKW), lambda n: (before(n), C_SV // SWAKW))
    bias = pl.BlockSpec((SWA_H, BLK, 2 * BLK), lambda n: (0, 0, 0))
    gain = pl.BlockSpec((1, SWA_D), lambda n: (0, 0))
    sink = pl.BlockSpec((SWA_KV, 1, SWA_G), lambda n: (0, 0, 0))
    wide = pl.BlockSpec((BLK, SWAW), lambda n: (blk(n), 0))
    narrow = pl.BlockSpec((BLK, SWAKW), lambda n: (blk(n), 0))
    return [q_spec, k_prev, k_cur, v_prev, v_cur, bias, gain, gain, sink], wide, narrow


def _split_heads(x):
    return jnp.stack([x[:, h * SWA_D:(h + 1) * SWA_D] for h in range(x.shape[1] // SWA_D)], axis=0)


def _join_heads(x):
    return jnp.concatenate([x[h] for h in range(x.shape[0])], axis=1)


def _swa_fwd(proj, bias, qg, kg, sinks):
    s_len = proj.shape[0]
    nb = s_len // BLK
    in_specs, wide, _ = _swa_specs(nb, False)

    def kern(q_ref, kp_ref, kc_ref, vp_ref, vc_ref, b_ref, qg_ref, kg_ref, s_ref, o_ref):
        mask = _swa_mask(pl.program_id(0) == 0)
        o8 = _f_swa(*[_split_heads(r[...]) for r in (q_ref, kp_ref, kc_ref, vp_ref, vc_ref)], b_ref[...], qg_ref[...],
                    kg_ref[...], s_ref[...], mask)
        o_ref[...] = _join_heads(o8).astype(BF16)

    return _pc(kern, "swa_fwd", (nb,), in_specs, wide, SDS((s_len, SWAW), BF16))(
        proj, proj, proj, proj, proj, bias, qg, kg, sinks)


def _swa_bwd(proj, bias, qg, kg, sinks, d_out):
    s_len = proj.shape[0]
    nb = s_len // BLK
    in_specs, wide, narrow = _swa_specs(nb, True)

    def kern(q_ref, kp_ref, kc_ref, vp_ref, vc_ref, b_ref, qg_ref, kg_ref, s_ref, do_ref,
             dq_ref, dk_ref, dv_ref, db_ref, dqg_ref, dkg_ref, ds_ref, carry_k, carry_v):
        n = pl.program_id(0)
        mask = _swa_mask(n == nb - 1)
        _zero_first([carry_k, carry_v, db_ref, ds_ref, dqg_ref, dkg_ref])
        fn = functools.partial(_f_swa, mask=mask)
        _, vjp = jax.vjp(fn, *[_split_heads(r[...]) for r in (q_ref, kp_ref, kc_ref, vp_ref, vc_ref)], b_ref[...],
                         qg_ref[...], kg_ref[...], s_ref[...])
        dq, dkp, dkc, dvp, dvc, dbias, dqg, dkg, dsink = vjp(_split_heads(do_ref[...]))
        dq_ref[...] = _join_heads(dq).astype(BF16)
        dk_ref[...] = (_join_heads(dkc) + carry_k[...]).astype(BF16)
        dv_ref[...] = (_join_heads(dvc) + carry_v[...]).astype(BF16)
        carry_k[...] = _join_heads(dkp)
        carry_v[...] = _join_heads(dvp)
        db_ref[...] += dbias
        dqg_ref[...] += dqg
        dkg_ref[...] += dkg
        ds_ref[...] += dsink

    bias_spec, gain, sink = in_specs[5], in_specs[6], in_specs[8]
    return _pc(
        kern, "swa_bwd", (nb,), in_specs + [wide], [wide, narrow, narrow, bias_spec, gain, gain, sink],
        [SDS((s_len, SWAW), BF16), SDS((s_len, SWAKW), BF16), SDS((s_len, SWAKW), BF16),
         SDS((SWA_H, BLK, 2 * BLK), F32), SDS((1, SWA_D), F32), SDS((1, SWA_D), F32), SDS((SWA_KV, 1, SWA_G), F32)],
        scratch=[pltpu.VMEM((BLK, SWAKW), F32), pltpu.VMEM((BLK, SWAKW), F32)],
    )(proj, proj, proj, proj, proj, bias, qg, kg, sinks, d_out)


def _branch_merge(y_dn, y_swa, wa, wb, proj):
    s_len = y_dn.shape[0]
    tm = min(512, s_len)

    def kern(ya_ref, yb_ref, wa_ref, wb_ref, ga_ref, gb_ref, pa_ref, pb_ref, m_ref):
        pa = _bdot(ya_ref[...], wa_ref[0])
        pb = _bdot(yb_ref[...], wb_ref[0])
        pa_ref[...] = pa
        pb_ref[...] = pb
        m_ref[...] = _f_merge(pa, pb, ga_ref[...], gb_ref[...]).astype(BF16)

    y_spec = pl.BlockSpec((tm, DNW), lambda i, s: (i, 0))
    w_spec = pl.BlockSpec((1, DNW, CSH), lambda i, s: (s, 0, 0))
    o_spec = pl.BlockSpec((tm, CSH), lambda i, s: (i, s))
    ga_spec = pl.BlockSpec((tm, CSH), lambda i, s: (i, C_GATE // CSH + s))
    gb_spec = pl.BlockSpec((tm, CSH), lambda i, s: (i, (C_GATE + D) // CSH + s))
    return _pc(kern, "branch_merge", (s_len // tm, N_CHIPS), [y_spec, y_spec, w_spec, w_spec, ga_spec, gb_spec],
               [o_spec] * 3, [SDS((s_len, D), F32), SDS((s_len, D), F32), SDS((s_len, D), BF16)],
               )(y_dn, y_swa, wa, wb, proj, proj)


def _out_proj(merged, w_out, x, gain):
    s_len = x.shape[0]
    tm = min(256, s_len)

    def kern(m_ref, w_ref, x_ref, g_ref, x1_ref, h2_ref):
        x1 = x_ref[...] + _bdot(m_ref[...], w_ref[...])
        x1_ref[...] = x1
        h2_ref[...] = _f_rms(x1, g_ref[...]).astype(BF16)

    row = pl.BlockSpec((tm, D), lambda i: (i, 0))
    return _pc(kern, "out_proj", (s_len // tm,),
               [row, pl.BlockSpec((D, D), lambda i: (0, 0)), row, pl.BlockSpec((1, D), lambda i: (0, 0))],
               [row, row], [SDS((s_len, D), F32), SDS((s_len, D), BF16)])(merged, w_out, x, gain)


def _ffn_up(h2, wg, wu):
    s_len = h2.shape[0]
    tm = min(512, s_len)

    def kern(h_ref, g_ref, u_ref, gt_ref, up_ref, act_ref):
        h = h_ref[...]
        g = _bdot(h, g_ref[0], "nt")
        u = _bdot(h, u_ref[0], "nt")
        gt_ref[0] = g
        up_ref[0] = u
        act_ref[0] = _f_swiglu(g, u).astype(BF16)

    w_spec = pl.BlockSpec((1, FSH, D), lambda s, i: (s, 0, 0))
    o_spec = pl.BlockSpec((1, tm, FSH), lambda s, i: (s, i, 0))
    shape = (N_CHIPS, s_len, FSH)
    return _pc(kern, "ffn_up", (N_CHIPS, s_len // tm), [pl.BlockSpec((tm, D), lambda s, i: (i, 0)), w_spec, w_spec],
               [o_spec] * 3, [SDS(shape, F32), SDS(shape, F32), SDS(shape, BF16)])(h2, wg, wu)


def _ffn_down_loss(act, wd, x1, target):
    s_len = x1.shape[0]
    tm = min(256, s_len)

    def kern(a_ref, w_ref, x_ref, t_ref, dy_ref, dyb_ref, loss_ref):
        _zero_first([loss_ref])
        y = x_ref[...]
        for s in range(N_CHIPS):
            y = y + _bdot(a_ref[s], w_ref[s])
        d = y - t_ref[...]
        dy = d * (1.0 / D)
        dy_ref[...] = dy
        dyb_ref[...] = dy.astype(BF16)
        loss_ref[...] += jnp.sum(d * d).reshape(1, 1) * (0.5 / D)

    row = pl.BlockSpec((tm, D), lambda i: (i, 0))
    return _pc(kern, "ffn_down_loss", (s_len // tm,),
               [pl.BlockSpec((N_CHIPS, tm, FSH), lambda i: (0, i, 0)),
                pl.BlockSpec((N_CHIPS, FSH, D), lambda i: (0, 0, 0)), row, row],
               [row, row, pl.BlockSpec((1, 1), lambda i: (0, 0))],
               [SDS((s_len, D), F32), SDS((s_len, D), BF16), SDS((1, 1), F32)])(act, wd, x1, target)


def _ffn_dact(dy_b, wd, gt, up):
    s_len = dy_b.shape[0]
    tm = min(512, s_len)

    def kern(dy_ref, w_ref, gt_ref, up_ref, dg_ref, du_ref):
        d_act = _bdot(dy_ref[...], w_ref[0], "nt")
        _, vjp = jax.vjp(_f_swiglu, gt_ref[0], up_ref[0])
        dg, du = vjp(d_act)
        dg_ref[0] = dg.astype(BF16)
        du_ref[0] = du.astype(BF16)

    a_spec = pl.BlockSpec((1, tm, FSH), lambda s, i: (s, i, 0))
    shape = (N_CHIPS, s_len, FSH)
    return _pc(kern, "ffn_dact", (N_CHIPS, s_len // tm),
               [pl.BlockSpec((tm, D), lambda s, i: (i, 0)), pl.BlockSpec((1, FSH, D), lambda s, i: (s, 0, 0)),
                a_spec, a_spec],
               [a_spec, a_spec], [SDS(shape, BF16), SDS(shape, BF16)])(dy_b, wd, gt, up)


def _gw_ffn(lhs, rhs, name):
    s_len = rhs.shape[0]
    n = len(lhs)
    tn = 512

    def kern(*refs):
        g = refs[n][...]
        for i in range(n):
            refs[n + 1 + i][0] = _bdot(refs[i][0], g, "tn").astype(BF16)

    a_spec = pl.BlockSpec((1, s_len, FSH), lambda s, j: (s, 0, 0))
    o_spec = pl.BlockSpec((1, FSH, tn), lambda s, j: (s, 0, j))
    return _pc(kern, name, (N_CHIPS, D // tn), [a_spec] * n + [pl.BlockSpec((s_len, tn), lambda s, j: (0, j))],
               [o_spec] * n, [SDS((N_CHIPS, FSH, D), BF16)] * n)(*lhs, rhs)


def _ffn_dh2(d_gt, d_up, wg, wu, x1, dy, gain):
    s_len = x1.shape[0]
    tm = min(256, s_len)

    def kern(dg_ref, du_ref, wg_ref, wu_ref, x_ref, dy_ref, g_ref, dx_ref, dxb_ref, dgain_ref):
        _zero_first([dgain_ref])
        dh2 = jnp.zeros((tm, D), F32)
        for s in range(N_CHIPS):
            dh2 = dh2 + _bdot(dg_ref[s], wg_ref[s]) + _bdot(du_ref[s], wu_ref[s])
        _, vjp = jax.vjp(_f_rms, x_ref[...], g_ref[...])
        dx, dgain = vjp(dh2)
        dx1 = dx + dy_ref[...]
        dx_ref[...] = dx1
        dxb_ref[...] = dx1.astype(BF16)
        dgain_ref[...] += dgain

    row = pl.BlockSpec((tm, D), lambda i: (i, 0))
    d_spec = pl.BlockSpec((N_CHIPS, tm, FSH), lambda i: (0, i, 0))
    w_spec = pl.BlockSpec((N_CHIPS, FSH, D), lambda i: (0, 0, 0))
    vec = pl.BlockSpec((1, D), lambda i: (0, 0))
    return _pc(kern, "ffn_dh2", (s_len // tm,), [d_spec, d_spec, w_spec, w_spec, row, row, vec],
               [row, row, vec], [SDS((s_len, D), F32), SDS((s_len, D), BF16), SDS((1, D), F32)],
               )(d_gt, d_up, wg, wu, x1, dy, gain)


def _merge_bwd(dx1_b, w_out, pa, pb, proj):
    s_len = dx1_b.shape[0]
    tm = min(256, s_len)

    def kern(dx_ref, w_ref, pa_ref, pb_ref, g_ref, dpa_ref, dpb_ref, dg_ref):
        dm = _bdot(dx_ref[...], w_ref[...], "nt")
        gates = g_ref[...]
        _, vjp = jax.vjp(_f_merge, pa_ref[...], pb_ref[...], gates[:, :D], gates[:, D:])
        dpa, dpb, dga, dgb = vjp(dm)
        dpa_ref[...] = dpa.astype(BF16)
        dpb_ref[...] = dpb.astype(BF16)
        dg_ref[:, :D] = dga.astype(BF16)
        dg_ref[:, D:] = dgb.astype(BF16)

    row = pl.BlockSpec((tm, D), lambda i: (i, 0))
    return _pc(kern, "merge_bwd", (s_len // tm,),
               [row, pl.BlockSpec((D, D), lambda i: (0, 0)), row, row,
                pl.BlockSpec((tm, 2 * D), lambda i: (i, C_GATE // (2 * D)))],
               [row, row, pl.BlockSpec((tm, 2 * D), lambda i: (i, 0))],
               [SDS((s_len, D), BF16), SDS((s_len, D), BF16), SDS((s_len, 2 * D), BF16)],
               )(dx1_b, w_out, pa, pb, proj)


def _d_branch(d_pa, d_pb, wa, wb):
    s_len = d_pa.shape[0]
    tm = min(512, s_len)

    def kern(da_ref, db_ref, wa_ref, wb_ref, oa_ref, ob_ref):
        acc_a = jnp.zeros((tm, DNW), F32)
        acc_b = jnp.zeros((tm, SWAW), F32)
        for s in range(N_CHIPS):
            acc_a = acc_a + _bdot(da_ref[:, s * CSH:(s + 1) * CSH], wa_ref[s], "nt")
            acc_b = acc_b + _bdot(db_ref[:, s * CSH:(s + 1) * CSH], wb_ref[s], "nt")
        oa_ref[...] = acc_a
        ob_ref[...] = acc_b

    row = pl.BlockSpec((tm, D), lambda i: (i, 0))
    w_spec = pl.BlockSpec((N_CHIPS, DNW, CSH), lambda i: (0, 0, 0))
    out = pl.BlockSpec((tm, DNW), lambda i: (i, 0))
    return _pc(kern, "d_branch", (s_len // tm,), [row, row, w_spec, w_spec], [out, out],
               [SDS((s_len, DNW), F32), SDS((s_len, SWAW), F32)])(d_pa, d_pb, wa, wb)


def _gw_branch(y_dn, y_swa, d_pa, d_pb):
    s_len = y_dn.shape[0]

    def kern(ya_ref, yb_ref, da_ref, db_ref, oa_ref, ob_ref):
        oa_ref[0] = _bdot(ya_ref[...], da_ref[...], "tn").astype(BF16)
        ob_ref[0] = _bdot(yb_ref[...], db_ref[...], "tn").astype(BF16)

    y_spec = pl.BlockSpec((s_len, DNW), lambda s: (0, 0))
    d_spec = pl.BlockSpec((s_len, CSH), lambda s: (0, s))
    o_spec = pl.BlockSpec((1, DNW, CSH), lambda s: (s, 0, 0))
    shape = (N_CHIPS, DNW, CSH)
    return _pc(kern, "gw_branch", (N_CHIPS,), [y_spec, y_spec, d_spec, d_spec], [o_spec, o_spec],
               [SDS(shape, BF16), SDS(shape, BF16)])(y_dn, y_swa, d_pa, d_pb)


def _dh_rms(d_proj, w_in_p, x, dx1, gain):
    s_len = x.shape[0]
    tm = min(256, s_len)

    def kern(dp_ref, w_ref, x_ref, r_ref, g_ref, gx_ref, dgain_ref):
        _zero_first([dgain_ref])
        dh = _bdot(dp_ref[...], w_ref[...], "nt")
        _, vjp = jax.vjp(_f_rms, x_ref[...], g_ref[...])
        dx, dgain = vjp(dh)
        gx_ref[...] = dx + r_ref[...]
        dgain_ref[...] += dgain

    row = pl.BlockSpec((tm, D), lambda i: (i, 0))
    vec = pl.BlockSpec((1, D), lambda i: (0, 0))
    return _pc(kern, "dh_rms", (s_len // tm,),
               [pl.BlockSpec((tm, PW), lambda i: (i, 0)), pl.BlockSpec((D, PW), lambda i: (0, 0)), row, row, vec],
               [row, vec], [SDS((s_len, D), F32), SDS((1, D), F32)])(d_proj, w_in_p, x, dx1, gain)


HALO = 8


def _conv_taps(cur_ref, prev_ref, halo, first):
    tm = cur_ref.shape[0]
    halo[0:HALO, :] = jnp.where(first, 0.0, prev_ref[...])
    halo[HALO:, :] = cur_ref[...]
    return [halo[HALO - n:HALO - n + tm, :] for n in range(CONV - 1, 0, -1)] + [cur_ref[...]]


def _dn_pre_specs(s_len, tm, blk):
    cur = pl.BlockSpec((tm, QKVW), lambda i: (blk(i), 0))
    prev = pl.BlockSpec((HALO, QKVW), lambda i: (jnp.maximum(blk(i) * (tm // HALO) - 1, 0), 0))
    ba = pl.BlockSpec((tm, 128), lambda i: (blk(i), C_BA // 128))
    row = pl.BlockSpec((tm, DNW), lambda i: (blk(i), 0))
    full = [pl.BlockSpec((CONV, QKVW), lambda i: (0, 0)), pl.BlockSpec((1, DN_H), lambda i: (0, 0)),
            pl.BlockSpec((1, DN_H), lambda i: (0, 0))]
    return cur, prev, ba, row, full


def _dn_pre_fwd(proj, conv_w, alog, dtb):
    s_len = proj.shape[0]
    tm = min(128, s_len)
    cur, prev, ba, row, full = _dn_pre_specs(s_len, tm, lambda i: i)

    def kern(cur_ref, prev_ref, ba_ref, cw_ref, al_ref, dt_ref, q_ref, k_ref, v_ref, bb_ref, gb_ref, halo):
        xs = _conv_taps(cur_ref, prev_ref, halo, pl.program_id(0) == 0)
        outs = _f_dn_pre(*xs, ba_ref[...], cw_ref[...], al_ref[...], dt_ref[...])
        for ref, val in zip((q_ref, k_ref, v_ref, bb_ref, gb_ref), outs, strict=True):
            ref[...] = val

    return _pc(kern, "dn_pre_fwd", (s_len // tm,), [cur, prev, ba] + full, [row] * 5, [SDS((s_len, DNW), F32)] * 5,
               scratch=[pltpu.VMEM((tm + HALO, QKVW), F32)])(proj, proj, proj, conv_w, alog, dtb)


def _dn_pre_bwd(proj, conv_w, alog, dtb, cots):
    s_len = proj.shape[0]
    tm = min(128, s_len)
    nb = s_len // tm
    cur, prev, ba, row, full = _dn_pre_specs(s_len, tm, lambda i: nb - 1 - i)

    def kern(cur_ref, prev_ref, ba_ref, cw_ref, al_ref, dt_ref, dq_ref, dk_ref, dv_ref, dbb_ref, dgb_ref,
             dqkv_ref, dba_ref, dcw_ref, dal_ref, ddt_ref, halo, *tails):
        i = pl.program_id(0)
        _zero_first([dcw_ref, dal_ref, ddt_ref])

        @pl.when(i == 0)
        def _():
            for t in tails:
                t[tm:, :] = jnp.zeros((HALO, QKVW), F32)

        xs = _conv_taps(cur_ref, prev_ref, halo, i == nb - 1)
        _, vjp = jax.vjp(_f_dn_pre, *xs, ba_ref[...], cw_ref[...], al_ref[...], dt_ref[...])
        *dxs, dba, dcw, dal, ddt = vjp((dq_ref[...], dk_ref[...], dv_ref[...], dbb_ref[...], dgb_ref[...]))
        total = dxs[CONV - 1]
        for j, t in enumerate(tails):
            n = CONV - 1 - j
            t[0:tm, :] = dxs[j]
            total = total + t[n:n + tm, :]
            t[tm:, :] = dxs[j][0:HALO, :]
        dqkv_ref[...] = total.astype(BF16)
        dba_ref[...] = dba.astype(BF16)
        dcw_ref[...] += dcw
        dal_ref[...] += dal
        ddt_ref[...] += ddt

    return _pc(kern, "dn_pre_bwd", (nb,), [cur, prev, ba] + full + [row] * 5,
               [cur, pl.BlockSpec((tm, 128), lambda i: (nb - 1 - i, 0))] + full,
               [SDS((s_len, QKVW), BF16), SDS((s_len, 128), BF16), SDS((CONV, QKVW), F32), SDS((1, DN_H), F32),
                SDS((1, DN_H), F32)],
               scratch=[pltpu.VMEM((tm + HALO, QKVW), F32)] * CONV)(proj, proj, proj, conv_w, alog, dtb, *cots)


def _pad_w_in(w_in):
    pieces = [w_in[:, o0:o0 + w] for o0, w, _ in sorted(_ORIG_PIECES, key=lambda t: t[2])]
    pieces.append(jnp.zeros((w_in.shape[0], PW - D_IN), w_in.dtype))
    return jnp.concatenate(pieces, axis=1)


def _unpad_w_in(g):
    return jnp.concatenate([g[:, p0:p0 + w] for _, w, p0 in _ORIG_PIECES], axis=1)


def _local_step(x, target, wts):
    s_len = x.shape[0]
    tm = min(256, s_len)
    tmh = min(128, s_len)
    w_in_p = wts["w_in_p"]
    attn_gain = wts["attn_norm"]
    ffn_gain = wts["ffn_norm"]
    conv_w = wts["dn_conv"]
    alog, dtb, out_gain = wts["dn_a_log"], wts["dn_dt_bias"], wts["dn_out_norm"]
    qg, kg = wts["swa_q_norm"], wts["swa_k_norm"]
    sinks = wts["swa_sinks"].reshape(SWA_KV, 1, SWA_G)

    (h,) = _rows(lambda r, f: ([_f_rms(r[0], f[0])], []), "rms1_fwd", s_len, tm, [_whole(x)], [attn_gain],
                 [(D, BF16)])
    proj = _mm(h, w_in_p, "nn", F32, 512, 1024, "mm_proj")
    q_dn, k_dn, v_dn, bb, gb = _dn_pre_fwd(proj, conv_w, alog, dtb)
    o_dn, s_all, t_all = _dn_chunks_fwd(q_dn, k_dn, v_dn, gb, bb)
    post_ins = [_whole(o_dn), (proj, DNW, C_Z // DNW)]
    (y_dn,) = _rows(lambda r, f: ([_f_dn_post(r[0], r[1], f[0])], []), "dn_post_fwd", s_len, tm, post_ins,
                    [out_gain], [(DNW, BF16)])

    bias = _bias_expand(wts["rel_bias"].T).reshape(SWA_H, BLK, 2 * BLK)
    y_swa = _swa_fwd(proj, bias, qg, kg, sinks)

    wts = {**wts, **wts["late"](y_swa)}
    p_a, p_b, merged = _branch_merge(y_dn, y_swa, wts["wa"], wts["wb"], proj)
    x1, h2 = _out_proj(merged, wts["w_out"], x, ffn_gain)
    gt, up, act = _ffn_up(h2, wts["wg"], wts["wu"])
    dy, dy_b, loss = _ffn_down_loss(act, wts["wd"], x1, target)

    grads = {}
    d_gt, d_up = _ffn_dact(dy_b, wts["wd"], gt, up)
    (grads["w_down"],) = _gw_ffn([act], dy_b, "gw_down")
    grads["w_gate"], grads["w_up"] = _gw_ffn([d_gt, d_up], h2, "gw_gate_up")
    dx1, dx1_b, grads["ffn_norm"] = _ffn_dh2(d_gt, d_up, wts["wg"], wts["wu"], x1, dy, ffn_gain)
    grads["w_out"] = _mm(merged, dx1_b, "tn", BF16, 512, 512, "gw_out")
    d_pa, d_pb, d_gr = _merge_bwd(dx1_b, wts["w_out"], p_a, p_b, proj)
    d_ydn, d_yswa = _d_branch(d_pa, d_pb, wts["wa"], wts["wb"])
    grads["w_branch_dn"], grads["w_branch_swa"] = _gw_branch(y_dn, y_swa, d_pa, d_pb)
    token = wts["send_early"](grads)
    qg_t = qg + token[0:1, 0:1]
    out_gain_t = out_gain + token[0:1, 0:1]

    d_sq, d_sk, d_sv, d_bias, grads["swa_q_norm"], grads["swa_k_norm"], d_sinks = _swa_bwd(
        proj, bias, qg_t, kg, sinks, d_yswa)
    grads["swa_sinks"] = d_sinks.reshape(1, SWA_H)
    grads["rel_bias"] = _bias_reduce(d_bias.reshape(SWA_H, BLK * 2 * BLK)).T

    def post_bwd(r, f):
        _, vjp = jax.vjp(_f_dn_post, r[0], r[1], f[0])
        d_o, d_z, d_gain = vjp(r[2])
        return [d_o, d_z], [d_gain]

    d_o, d_z, grads["dn_out_norm"] = _rows(post_bwd, "dn_post_bwd", s_len, tm, post_ins + [_whole(d_ydn)], [out_gain_t],
                                           [(DNW, F32), (DNW, BF16)], [(1, DH)])
    d_q, d_k, d_v, d_gb, d_bb = _dn_chunks_bwd(q_dn, k_dn, v_dn, gb, bb, s_all, t_all, d_o)

    d_qkv, d_ba, grads["dn_conv"], grads["dn_a_log"], grads["dn_dt_bias"] = _dn_pre_bwd(
        proj, conv_w, alog, dtb, (d_q, d_k, d_v, d_bb, d_gb))

    d_proj = jnp.concatenate(
        [d_qkv, d_z, d_gr, d_sq, d_sk, d_sv, d_ba, jnp.zeros((s_len, PW - C_BA - 128), BF16)], axis=1)
    grads["w_in_p"] = _mm(h, d_proj, "tn", BF16, 512, 1024, "gw_in")
    grad_x, grads["attn_norm"] = _dh_rms(d_proj, w_in_p, x, dx1, attn_gain)
    return loss, grad_x, grads


_HBM = pl.BlockSpec(memory_space=pl.ANY)


def _place():
    return lax.axis_index("x"), lax.axis_index("y"), lax.axis_index("c")


def _other_chips(x, y):
    return [(1 - x, y), (x, 1 - y), (1 - x, 1 - y)]


def _rcopy(src, dst, send_sems, recv_sems, k, to):
    return pltpu.make_async_remote_copy(src_ref=src, dst_ref=dst, send_sem=send_sems.at[k], recv_sem=recv_sems.at[k],
                                        device_id=to, device_id_type=MESH)


def _comm_call(body, name, ins, out_shapes, n_remote, landing=0):
    first = len(ins) - landing
    return pl.pallas_call(
        body, name=name, in_specs=[_HBM] * len(ins), out_specs=[_HBM] * len(out_shapes), out_shape=out_shapes,
        scratch_shapes=[pltpu.SemaphoreType.DMA((n_remote,)), pltpu.SemaphoreType.DMA((n_remote,))],
        input_output_aliases={first + i: i for i in range(landing)},
        compiler_params=_cparams(has_side_effects=True),
    )(*ins)


def _own_slot(blocks, chip):
    return [lax.dynamic_update_slice(jnp.zeros((N_CHIPS,) + b.shape, b.dtype), b[None], (chip, 0, 0)) for b in blocks]


def _gather_weights(ws, chip):
    n = len(ws)
    halves = [w.shape[0] // 2 for w in ws]

    def body(*refs):
        w_refs, o_refs = refs[:n], refs[2 * n:3 * n]
        send_sems, recv_sems = refs[3 * n:]
        x, y, c = _place()
        s = 2 * x + y
        sib = (x, y, 1 - c)
        chips = _other_chips(x, y)

        def rows(i, half):
            return pl.ds(half * halves[i], halves[i])

        first = []
        for j, (cx, cy) in enumerate(chips):
            for i in range(n):
                cp = _rcopy(w_refs[i].at[rows(i, c), :], o_refs[i].at[s, rows(i, c), :], send_sems, recv_sems,
                            j * n + i, (cx, cy, c))
                cp.start()
                first.append(cp)
        passed = []
        for j, (cx, cy) in enumerate(chips):
            sj = 2 * cx + cy
            for i in range(n):
                blk = o_refs[i].at[sj, rows(i, c), :]
                _rcopy(blk, blk, send_sems, recv_sems, j * n + i, (cx, cy, c)).wait_recv()
                cp = _rcopy(blk, blk, send_sems, recv_sems, (3 + j) * n + i, sib)
                cp.start()
                passed.append(cp)
        for j, (cx, cy) in enumerate(chips):
            sj = 2 * cx + cy
            for i in range(n):
                blk = o_refs[i].at[sj, rows(i, 1 - c), :]
                _rcopy(blk, blk, send_sems, recv_sems, (3 + j) * n + i, sib).wait_recv()
        for cp in first + passed:
            cp.wait_send()

    return _comm_call(body, "gather_weights", list(ws) + _own_slot(ws, chip),
                      [SDS((N_CHIPS,) + w.shape, w.dtype) for w in ws], 6 * n, landing=n)


_HBM_ONLY = pl.BlockSpec(memory_space=pltpu.HBM)
_SEM = pl.BlockSpec(memory_space=pltpu.SEMAPHORE)
_DATAFLOW = pltpu.SideEffectType.DATAFLOW_SIDE_EFFECTING


def _in_hbm(a):
    return pltpu.with_memory_space_constraint(a, pltpu.HBM)


def _gather_windows(blocks):
    halves = [b.shape[0] // 2 for b in blocks]

    def src_at(ref, i, c, sj):
        return ref.at[pl.ds(c * halves[i], halves[i]), :]

    def dst_at(ref, i, c, s_from):
        return ref.at[s_from, pl.ds(c * halves[i], halves[i]), :]

    return src_at, dst_at


def _exchange_windows():
    return (lambda ref, i, c, sj: ref.at[sj]), (lambda ref, i, c, s_from: ref.at[s_from])


def _split_start(name, ws, lands, dep, windows):
    n = len(ws)
    src_at, dst_at = windows

    def body(*refs):
        w_refs, l_refs = refs[:n], refs[n:2 * n]
        send_sems, recv_sems = refs[2 * n + 1], refs[2 * n + 2]
        token = refs[-1]
        x, y, c = _place()
        s = 2 * x + y
        for j, (cx, cy) in enumerate(_other_chips(x, y)):
            for i in range(n):
                _rcopy(src_at(w_refs[i], i, c, 2 * cx + cy), dst_at(l_refs[i], i, c, s), send_sems, recv_sems,
                       j * n + i, (cx, cy, c)).start()
        token[...] = jnp.zeros_like(token)

    outs = pl.pallas_call(
        body, name=name,
        out_shape=(pltpu.SemaphoreType.DMA((3 * n,)), pltpu.SemaphoreType.DMA((3 * n,)),
                   *[pltpu.HBM(w.shape, w.dtype) for w in ws], *[pltpu.HBM(t.shape, t.dtype) for t in lands],
                   SDS((8, 128), F32)),
        in_specs=[_HBM_ONLY] * (2 * n) + [pl.BlockSpec(memory_space=pl.ANY)],
        out_specs=(_SEM, _SEM, *[_HBM_ONLY] * (2 * n), pl.BlockSpec(memory_space=pltpu.VMEM)),
        input_output_aliases={i: 2 + i for i in range(2 * n)},
        compiler_params=pltpu.CompilerParams(has_side_effects=_DATAFLOW),
    )(*[_in_hbm(w) for w in ws], *[_in_hbm(t) for t in lands], dep)
    return outs[0], outs[1], outs[2:2 + n], outs[2 + n:2 + 2 * n], outs[-1]


def _split_wait(name, w_thru, l_thru, send_sems, recv_sems, after, windows):
    n = len(w_thru)
    src_at, dst_at = windows

    def body(*refs):
        w_refs, l_refs = refs[:n], refs[n:2 * n]
        send_sems, recv_sems = refs[2 * n], refs[2 * n + 1]
        x, y, c = _place()
        for j, (cx, cy) in enumerate(_other_chips(x, y)):
            sj = 2 * cx + cy
            for i in range(n):
                cp = _rcopy(src_at(w_refs[i], i, c, sj), dst_at(l_refs[i], i, c, sj), send_sems, recv_sems, j * n + i,
                            (cx, cy, c))
                cp.wait_send()
                cp.wait_recv()

    outs = pl.pallas_call(
        body, name=name,
        out_shape=[pltpu.HBM(w.shape, w.dtype) for w in w_thru] + [pltpu.HBM(t.shape, t.dtype) for t in l_thru],
        in_specs=[_HBM_ONLY] * (2 * n) + [_SEM, _SEM, pl.BlockSpec(memory_space=pl.ANY)],
        out_specs=[_HBM_ONLY] * (2 * n),
        input_output_aliases={i: i for i in range(2 * n)},
        compiler_params=pltpu.CompilerParams(has_side_effects=_DATAFLOW),
    )(*w_thru, *l_thru, send_sems, recv_sems, after)
    return outs[n:]


def _sibling_fill(lands):
    n = len(lands)
    halves = [t.shape[1] // 2 for t in lands]

    def body(*refs):
        o_refs = refs[n:2 * n]
        send_sems, recv_sems = refs[2 * n:]
        x, y, c = _place()
        sib = (x, y, 1 - c)
        chips = _other_chips(x, y)
        sent = []
        for j, (cx, cy) in enumerate(chips):
            for i in range(n):
                blk = o_refs[i].at[2 * cx + cy, pl.ds(c * halves[i], halves[i]), :]
                cp = _rcopy(blk, blk, send_sems, recv_sems, j * n + i, sib)
                cp.start()
                sent.append(cp)
        for j, (cx, cy) in enumerate(chips):
            for i in range(n):
                blk = o_refs[i].at[2 * cx + cy, pl.ds((1 - c) * halves[i], halves[i]), :]
                _rcopy(blk, blk, send_sems, recv_sems, j * n + i, sib).wait_recv()
        for cp in sent:
            cp.wait_send()

    return _comm_call(body, "sibling_fill", list(lands), [SDS(t.shape, t.dtype) for t in lands], 3 * n, landing=n)


def _swap_halves(gs, name):
    n = len(gs)
    halves = [g.shape[1] // 2 for g in gs]

    def body(*refs):
        g_refs, o_refs = refs[:n], refs[n:2 * n]
        send_sems, recv_sems = refs[2 * n:]
        x, y, c = _place()
        cps = [_rcopy(g_refs[i].at[:, pl.ds((1 - c) * halves[i], halves[i]), :], o_refs[i], send_sems, recv_sems, i,
                      (x, y, 1 - c)) for i in range(n)]
        for cp in cps:
            cp.start()
        for cp in cps:
            cp.wait()

    return _comm_call(body, name, gs, [SDS((N_CHIPS, h, g.shape[2]), g.dtype) for g, h in zip(gs, halves)], n)


def _chip_exchange(ps, chip):
    n = len(ps)

    def body(*refs):
        p_refs, o_refs = refs[:n], refs[2 * n:3 * n]
        send_sems, recv_sems = refs[3 * n:]
        x, y, c = _place()
        s = 2 * x + y
        chips = _other_chips(x, y)
        sent = []
        for j, (cx, cy) in enumerate(chips):
            for i in range(n):
                cp = _rcopy(p_refs[i].at[2 * cx + cy], o_refs[i].at[s], send_sems, recv_sems, j * n + i, (cx, cy, c))
                cp.start()
                sent.append(cp)
        for j, (cx, cy) in enumerate(chips):
            sj = 2 * cx + cy
            for i in range(n):
                _rcopy(p_refs[i].at[sj], o_refs[i].at[sj], send_sems, recv_sems, j * n + i, (cx, cy, c)).wait_recv()
        for cp in sent:
            cp.wait_send()

    own = [lax.dynamic_index_in_dim(p, chip, axis=0, keepdims=False) for p in ps]
    return _comm_call(body, "chip_exchange", list(ps) + _own_slot(own, chip), [SDS(p.shape, p.dtype) for p in ps],
                      3 * n, landing=n)


def _swap_reduced(rs):
    n = len(rs)

    def body(*refs):
        r_refs, o_refs = refs[:n], refs[n:2 * n]
        send_sems, recv_sems = refs[2 * n:]
        x, y, c = _place()
        cps = [_rcopy(r_refs[i], o_refs[i], send_sems, recv_sems, i, (x, y, 1 - c)) for i in range(n)]
        for cp in cps:
            cp.start()
        for cp in cps:
            cp.wait()

    return _comm_call(body, "swap_reduced", rs, [SDS(r.shape, r.dtype) for r in rs], n)


def _all_sum_small(vec, name):
    n_dev = 8
    flips = [(bx, by, bc) for bx in (0, 1) for by in (0, 1) for bc in (0, 1)][1:]

    def body(v_ref, out_ref, gath, send_sems, recv_sems):
        x, y, c = _place()
        me = 4 * x + 2 * y + c
        gath[me] = v_ref[...]
        sent = []
        for k, (bx, by, bc) in enumerate(flips):
            peer = (x ^ bx, y ^ by, c ^ bc)
            cp = _rcopy(v_ref, gath.at[me], send_sems, recv_sems, k, peer)
            cp.start()
            sent.append(cp)
        for k, (bx, by, bc) in enumerate(flips):
            peer = (x ^ bx, y ^ by, c ^ bc)
            _rcopy(v_ref, gath.at[4 * peer[0] + 2 * peer[1] + peer[2]], send_sems, recv_sems, k, peer).wait_recv()
        for cp in sent:
            cp.wait_send()
        acc = gath[0]
        for d in range(1, n_dev):
            acc = acc + gath[d]
        out_ref[...] = acc

    vm = pl.BlockSpec(memory_space=pltpu.VMEM)
    return pl.pallas_call(
        body, name=name, in_specs=[vm], out_specs=vm, out_shape=SDS(vec.shape, F32),
        scratch_shapes=[pltpu.VMEM((n_dev,) + vec.shape, F32), pltpu.SemaphoreType.DMA((7,)),
                        pltpu.SemaphoreType.DMA((7,))],
        compiler_params=_cparams(has_side_effects=True),
    )(vec)


def _pack_small(vals, extra=None):
    parts = [vals[n].reshape(-1).astype(F32) for n, _ in _SMALL]
    parts.append(jnp.zeros((1,), F32) if extra is None else extra.reshape(1).astype(F32))
    flat = jnp.concatenate(parts)
    flat = jnp.concatenate([flat, jnp.zeros((_SMALL_ROWS * 128 - flat.shape[0],), F32)])
    return flat.reshape(_SMALL_ROWS, 128)


def _unpack_small(packed, shapes):
    flat = packed.reshape(-1)
    return {n: flat[_SMALL_OFF[n][0]:_SMALL_OFF[n][0] + _SMALL_OFF[n][1]].reshape(shapes[n]) for n, _ in _SMALL}


def _pair_sum(gs, gots, core, name):
    n = len(gs)

    def kern(c_ref, *refs):
        for i in range(n):
            refs[2 * n + i][...] = (refs[i][...].astype(F32) + refs[n + i][...].astype(F32)).astype(BF16)

    in_specs = [pl.BlockSpec((1, t.shape[1], t.shape[2]), lambda s, c_ref: (s, c_ref[0], 0)) for t in gots]
    in_specs += [pl.BlockSpec((1, t.shape[1], t.shape[2]), lambda s, c_ref: (s, 0, 0)) for t in gots]
    out_specs = [pl.BlockSpec((1, t.shape[1], t.shape[2]), lambda s, c_ref: (s, 0, 0)) for t in gots]
    return pl.pallas_call(
        kern, name=name,
        grid_spec=pltpu.PrefetchScalarGridSpec(num_scalar_prefetch=1, grid=(N_CHIPS,), in_specs=in_specs,
                                               out_specs=out_specs),
        out_shape=[SDS(t.shape, BF16) for t in gots],
        compiler_params=_cparams(dimension_semantics=("arbitrary",)),
    )(core.reshape(1).astype(jnp.int32), *gs, *gots)


def _chip_sum(qs):
    n = len(qs)

    def kern(*refs):
        for i in range(n):
            acc = refs[i][0].astype(F32)
            for s in range(1, N_CHIPS):
                acc = acc + refs[i][s].astype(F32)
            refs[n + i][...] = acc

    in_specs = [pl.BlockSpec((N_CHIPS, q.shape[1] // 2, q.shape[2]), lambda j: (0, j, 0)) for q in qs]
    out_specs = [pl.BlockSpec((q.shape[1] // 2, q.shape[2]), lambda j: (j, 0)) for q in qs]
    return _pc(kern, "chip_sum", (2,), in_specs, out_specs, [SDS(q.shape[1:], F32) for q in qs])(*qs)


def _adam_math(w_, g_, m_, v_):
    m_ = ADAM_B1 * m_ + (1.0 - ADAM_B1) * g_
    v_ = ADAM_B2 * v_ + (1.0 - ADAM_B2) * jnp.square(g_)
    m_hat = m_ / (1.0 - ADAM_B1 ** ADAM_STEP)
    v_hat = v_ / (1.0 - ADAM_B2 ** ADAM_STEP)
    return -ADAM_LR * (m_hat / (jnp.sqrt(v_hat) + ADAM_EPS) + ADAM_WD * w_), m_, v_


def _adamw(w, g, m, v, name):
    rows, cols = w.shape
    tr = rows
    for cand in (256, 128, 64, 32, 16, 8):
        if rows % cand == 0 and rows > cand:
            tr = cand
            break

    def kern(w_ref, g_ref, m_ref, v_ref, d_ref, nm_ref, nv_ref):
        d_ref[...], nm_ref[...], nv_ref[...] = _adam_math(w_ref[...], g_ref[...], m_ref[...], v_ref[...])

    spec = pl.BlockSpec((tr, cols), lambda i: (i, 0))
    return _pc(kern, name, (rows // tr,), [spec] * 4, [spec] * 3, [SDS(w.shape, F32)] * 3)(w, g, m, v)


def _adamw_rows1(w, g, m, v, name):
    rows, _, cols = w.shape
    tr = next(t for t in (42, 32, 29, 16, 8, 7, 6, 4, 3, 2, 1) if rows % t == 0)

    def kern(w_ref, g_ref, m_ref, v_ref, d_ref, nm_ref, nv_ref):
        d_ref[...], nm_ref[...], nv_ref[...] = _adam_math(w_ref[...], g_ref[...], m_ref[...], v_ref[...])

    spec = pl.BlockSpec((tr, 1, cols), lambda i: (i, 0, 0))
    return _pc(kern, name, (rows // tr,), [spec] * 4, [spec] * 3, [SDS(w.shape, F32)] * 3)(w, g, m, v)


def _adamw_big(w, mine, theirs, m, v, core, name):
    _, rows, cols = w.shape
    half = rows // 2
    tr = next(t for t in (256, 176, 128, 64, 32, 16, 8) if half % t == 0)
    nbh = half // tr

    def kern(c_ref, w_ref, a_ref, b_ref, m_ref, v_ref, g_ref, d_ref, nm_ref, nv_ref):
        g_ = jnp.where(pl.program_id(0) // nbh == c_ref[0], a_ref[...], b_ref[...])
        g_ref[0] = g_
        d_ref[0], nm_ref[0], nv_ref[0] = _adam_math(w_ref[0], g_, m_ref[0], v_ref[0])

    full = pl.BlockSpec((1, tr, cols), lambda i, c_ref: (0, i, 0))
    part = pl.BlockSpec((tr, cols), lambda i, c_ref: (i % nbh, 0))
    return pl.pallas_call(
        kern, name=name,
        grid_spec=pltpu.PrefetchScalarGridSpec(num_scalar_prefetch=1, grid=(rows // tr,),
                                               in_specs=[full, part, part, full, full], out_specs=[full] * 4),
        out_shape=[SDS(w.shape, F32)] * 4,
        compiler_params=_cparams(dimension_semantics=("arbitrary",)),
    )(core.reshape(1).astype(jnp.int32), w, mine, theirs, m, v)


_WEIGHT_NAMES = ("attn_norm", "w_in", "dn_conv", "dn_a_log", "dn_dt_bias", "dn_out_norm", "swa_q_norm", "swa_k_norm",
                 "swa_sinks", "rel_bias", "w_branch_dn", "w_branch_swa", "w_out", "ffn_norm", "w_gate", "w_up",
                 "w_down")
_CONV_SH = QKVW // N_CHIPS


def kernel(x, attn_norm, w_in, dn_conv, dn_a_log, dn_dt_bias, dn_out_norm, swa_q_norm, swa_k_norm, swa_sinks, rel_bias, w_branch_dn, w_branch_swa, w_out, ffn_norm, w_gate, w_up, w_down, loss_target, m_attn_norm, m_w_in, m_dn_conv, m_dn_a_log, m_dn_dt_bias, m_dn_out_norm, m_swa_q_norm, m_swa_k_norm, m_swa_sinks, m_rel_bias, m_w_branch_dn, m_w_branch_swa, m_w_out, m_ffn_norm, m_w_gate, m_w_up, m_w_down, v_attn_norm, v_w_in, v_dn_conv, v_dn_a_log, v_dn_dt_bias, v_dn_out_norm, v_swa_q_norm, v_swa_k_norm, v_swa_sinks, v_rel_bias, v_w_branch_dn, v_w_branch_swa, v_w_out, v_ffn_norm, v_w_gate, v_w_up, v_w_down):
    w = dict(attn_norm=attn_norm, w_in=w_in, dn_conv=dn_conv, dn_a_log=dn_a_log, dn_dt_bias=dn_dt_bias,
             dn_out_norm=dn_out_norm, swa_q_norm=swa_q_norm, swa_k_norm=swa_k_norm, swa_sinks=swa_sinks,
             rel_bias=rel_bias, w_branch_dn=w_branch_dn, w_branch_swa=w_branch_swa, w_out=w_out, ffn_norm=ffn_norm,
             w_gate=w_gate, w_up=w_up, w_down=w_down)
    m = dict(attn_norm=m_attn_norm, w_in=m_w_in, dn_conv=m_dn_conv, dn_a_log=m_dn_a_log, dn_dt_bias=m_dn_dt_bias,
             dn_out_norm=m_dn_out_norm, swa_q_norm=m_swa_q_norm, swa_k_norm=m_swa_k_norm, swa_sinks=m_swa_sinks,
             rel_bias=m_rel_bias, w_branch_dn=m_w_branch_dn, w_branch_swa=m_w_branch_swa, w_out=m_w_out,
             ffn_norm=m_ffn_norm, w_gate=m_w_gate, w_up=m_w_up, w_down=m_w_down)
    v = dict(attn_norm=v_attn_norm, w_in=v_w_in, dn_conv=v_dn_conv, dn_a_log=v_dn_a_log, dn_dt_bias=v_dn_dt_bias,
             dn_out_norm=v_dn_out_norm, swa_q_norm=v_swa_q_norm, swa_k_norm=v_swa_k_norm, swa_sinks=v_swa_sinks,
             rel_bias=v_rel_bias, w_branch_dn=v_w_branch_dn, w_branch_swa=v_w_branch_swa, w_out=v_w_out,
             ffn_norm=v_ffn_norm, w_gate=v_w_gate, w_up=v_w_up, w_down=v_w_down)
    shapes = {n: w[n].shape for n in _WEIGHT_NAMES}

    def two_d(a):
        return a.reshape(a.shape[-2], a.shape[-1]) if a.ndim == 3 else a

    core = lax.axis_index("c")
    chip = 2 * lax.axis_index("x") + lax.axis_index("y")
    small_shapes = {n: two_d(w[n]).shape for n, _ in _SMALL}
    small_shapes["dn_conv"] = (CONV, QKVW)

    conv_loc = two_d(w["dn_conv"])
    conv_part = lax.dynamic_update_slice(jnp.zeros((CONV, QKVW), F32), jnp.where(core == 0, conv_loc, 0.0),
                                         (0, chip * _CONV_SH))
    conv_full = _all_sum_small(conv_part.reshape(CONV * QKVW // 128, 128), "gather_conv").reshape(CONV, QKVW)

    flipped = ("w_gate", "w_up")

    def natural(a, n):
        return a.transpose(0, 2, 1) if n in flipped else a

    w_bf = [two_d(natural(w[n], n).astype(BF16)) for n in _BIG_NAMES]
    (w_in_g,) = _gather_weights(w_bf[:1], chip)
    windows = _gather_windows(w_bf[1:])
    after_sync = w_in_g[0, :8, :128].astype(F32) + conv_full[0:1, :128]
    send_sems, recv_sems, w_thru, l_thru, token = _split_start(
        "gather_start", w_bf[1:], _own_slot(w_bf[1:], chip), after_sync, windows)

    def late(after):
        lands = _split_wait("gather_wait", w_thru, l_thru, send_sems, recv_sems, after, windows)
        g = dict(zip(_BIG_NAMES[1:], _sibling_fill(lands)))
        return dict(wa=g["w_branch_dn"], wb=g["w_branch_swa"], w_out=g["w_out"].reshape(D, D), wg=g["w_gate"],
                    wu=g["w_up"], wd=g["w_down"])

    w_in_full = w_in_g.transpose(1, 0, 2).reshape(D, D_IN)
    wts = dict(w_in_p=_pad_w_in(w_in_full), dn_conv=conv_full, late=late)
    for n, _ in _SMALL[:-1]:
        wts[n] = two_d(w[n])
    wts["attn_norm"] = wts["attn_norm"] + token[0:1, 0:1]

    early = {}

    def send_early(grads):
        gs = [grads["w_branch_dn"], grads["w_branch_swa"], grads["w_out"].reshape(N_CHIPS, CSH, D), grads["w_gate"],
              grads["w_up"], grads["w_down"]]
        parts = _pair_sum(gs, _swap_halves(gs, "swap_halves_early"), core, "pair_sum_early")
        own = [lax.dynamic_index_in_dim(p, chip, axis=0, keepdims=False) for p in parts]
        early["sems"], early["recv"], early["src"], early["land"], tok = _split_start(
            "exchange_start", parts, _own_slot(own, chip), parts[0][0, :8, :128], _exchange_windows())
        return tok

    wts["send_early"] = send_early
    loss_sum, grad_x, grads = _local_step(x[0], loss_target[0], wts)

    q_early = _split_wait("exchange_wait", early["src"], early["land"], early["sems"], early["recv"],
                          grads["w_in_p"], _exchange_windows())
    g_in = [_unpad_w_in(grads["w_in_p"]).reshape(D, N_CHIPS, D_IN // N_CHIPS).transpose(1, 0, 2)]
    parts_in = _pair_sum(g_in, _swap_halves(g_in, "swap_halves_in"), core, "pair_sum_in")
    reduced = _chip_sum(list(_chip_exchange(parts_in, chip)) + list(q_early))
    theirs = _swap_reduced(reduced)

    small_sum = _all_sum_small(_pack_small(grads, loss_sum), "all_sum_small")
    loss = small_sum.reshape(-1)[_LOSS_OFF]
    g_small = _unpack_small(small_sum, small_shapes)

    g_out, d_out, m_out, v_out = {}, {}, {}, {}
    for n, mine, other in zip(_BIG_NAMES[1:], reduced[1:], theirs[1:]):
        res = _adamw_big(natural(w[n], n), mine, other, natural(m[n], n), natural(v[n], n), core, "adamw_" + n)
        g_out[n], d_out[n], m_out[n], v_out[n] = (natural(t, n) for t in res)

    def rows1(a):
        return a.transpose(2, 0, 1)

    def unrows1(a):
        return a.transpose(1, 2, 0)

    g_in_blk = jnp.concatenate([jnp.where(core == 0, reduced[0], theirs[0]),
                                jnp.where(core == 0, theirs[0], reduced[0])], axis=0)
    g_in_r = rows1(g_in_blk[None])
    d_, m_, v_ = _adamw_rows1(rows1(w["w_in"]), g_in_r, rows1(m["w_in"]), rows1(v["w_in"]), "adamw_w_in")
    g_out["w_in"], d_out["w_in"], m_out["w_in"], v_out["w_in"] = (unrows1(t) for t in (g_in_r, d_, m_, v_))
    g_conv = lax.dynamic_slice(g_small["dn_conv"], (0, chip * _CONV_SH), (CONV, _CONV_SH))
    g_out["dn_conv"] = g_conv.reshape(shapes["dn_conv"])
    d_, m_, v_ = _adamw(conv_loc, g_conv, two_d(m["dn_conv"]), two_d(v["dn_conv"]), "adamw_dn_conv")
    d_out["dn_conv"], m_out["dn_conv"], v_out["dn_conv"] = (t.reshape(shapes["dn_conv"]) for t in (d_, m_, v_))

    def packed(src):
        vals = {n: src[n] for n, _ in _SMALL[:-1]}
        vals["dn_conv"] = jnp.zeros((CONV * QKVW,), F32)
        return _pack_small(vals)

    d_s, m_s, v_s = _adamw(packed(w), small_sum, packed(m), packed(v), "adamw_small")
    d_small, m_small, v_small = (_unpack_small(t, small_shapes) for t in (d_s, m_s, v_s))
    for n, _ in _SMALL[:-1]:
        g_out[n] = g_small[n].reshape(shapes[n])
        d_out[n], m_out[n], v_out[n] = (t[n].reshape(shapes[n]) for t in (d_small, m_small, v_small))

    return (loss, grad_x[None], *[g_out[n] for n in _WEIGHT_NAMES], *[d_out[n] for n in _WEIGHT_NAMES],
            *[m_out[n] for n in _WEIGHT_NAMES], *[v_out[n] for n in _WEIGHT_NAMES])
```

```python
import functools
import math

import numpy as np
import jax
import jax.numpy as jnp
from jax import lax
from jax.experimental import pallas as pl
from jax.experimental.pallas import tpu as pltpu

F32 = jnp.float32
BF16 = jnp.bfloat16
SDS = jax.ShapeDtypeStruct

D = 1024
DN_H = 4
DH = 128
DNW = DN_H * DH
QKVW = 3 * DNW
CONV = 4
CHUNK = 64
SWA_H = 8
SWA_KV = 2
SWA_G = SWA_H // SWA_KV
SWA_D = 64
SWAW = SWA_H * SWA_D
SWAKW = SWA_KV * SWA_D
BLK = 128
NBUCKET = 32
MAXDIST = 128
DFF = 2816
D_IN = QKVW + DNW + 2 * DN_H + SWAW + 2 * SWAKW + 2 * D
EPS = 1e-6
NEG = -1e30

ADAM_LR = 0.001
ADAM_B1 = 0.9
ADAM_B2 = 0.999
ADAM_EPS = 1e-08
ADAM_WD = 0.01
ADAM_STEP = 10

C_QKV, C_Z, C_GATE, C_SQ, C_SK, C_SV, C_BA = 0, 1536, 2048, 4096, 4608, 4736, 4864
PW = 5120
_ORIG_PIECES = (
    (0, QKVW, C_QKV),
    (QKVW, DNW, C_Z),
    (QKVW + DNW, 2 * DN_H, C_BA),
    (QKVW + DNW + 2 * DN_H, SWAW, C_SQ),
    (QKVW + DNW + 2 * DN_H + SWAW, SWAKW, C_SK),
    (QKVW + DNW + 2 * DN_H + SWAW + SWAKW, SWAKW, C_SV),
    (QKVW + DNW + 2 * DN_H + SWAW + 2 * SWAKW, 2 * D, C_GATE),
)

N_CHIPS = 4
FSH = DFF // N_CHIPS
CSH = D // N_CHIPS
VMEM_LIMIT = 48 * 1024 * 1024
MESH = pl.DeviceIdType.MESH

_BIG = (
    ("w_in", D, D_IN // N_CHIPS),
    ("w_branch_dn", DNW, CSH),
    ("w_branch_swa", SWAW, CSH),
    ("w_out", CSH, D),
    ("w_gate", FSH, D),
    ("w_up", FSH, D),
    ("w_down", FSH, D),
)
_BIG_NAMES = tuple(n for n, _, _ in _BIG)

_SMALL = (
    ("attn_norm", D), ("ffn_norm", D), ("dn_out_norm", DH), ("swa_q_norm", SWA_D), ("swa_k_norm", SWA_D),
    ("swa_sinks", SWA_H), ("dn_a_log", DN_H), ("dn_dt_bias", DN_H), ("rel_bias", NBUCKET * SWA_H),
    ("dn_conv", CONV * QKVW),
)
_SMALL_OFF = {}
_o = 0
for _n, _s in _SMALL:
    _SMALL_OFF[_n] = (_o, _s)
    _o += _s
_LOSS_OFF = _o
_SMALL_ROWS = -(-(_o + 1) // (8 * 128)) * 8


def _cparams(**kw):
    return pltpu.CompilerParams(vmem_limit_bytes=VMEM_LIMIT, **kw)


_DIMS = {
    "nn": (((1,), (0,)), ((), ())),
    "nt": (((1,), (1,)), ((), ())),
    "tn": (((0,), (0,)), ((), ())),
    "bnn": (((2,), (1,)), ((0,), (0,))),
    "bnt": (((2,), (2,)), ((0,), (0,))),
    "btn": (((1,), (1,)), ((0,), (0,))),
}


def _raw_dot(a, b, kind, exact):
    if exact:
        prec = lax.Precision.HIGH if exact == "x3" else lax.Precision.HIGHEST
        return lax.dot_general(a, b, _DIMS[kind], precision=prec, preferred_element_type=F32)
    return lax.dot_general(a.astype(BF16), b.astype(BF16), _DIMS[kind], preferred_element_type=F32)


@functools.partial(jax.custom_vjp, nondiff_argnums=(2, 3))
def _dot(a, b, kind, exact):
    return _raw_dot(a, b, kind, exact)


def _dot_fwd(a, b, kind, exact):
    return _raw_dot(a, b, kind, exact), (a, b)


def _dot_bwd(kind, exact, res, g):
    a, b = res
    pre = kind[:-2]
    nn, nt, tn = pre + "nn", pre + "nt", pre + "tn"
    if kind == nn:
        return _dot(g, b, nt, exact), _dot(a, g, tn, exact)
    if kind == nt:
        return _dot(g, b, nn, exact), _dot(g, a, tn, exact)
    return _dot(b, g, nt, exact), _dot(a, g, nn, exact)


_dot.defvjp(_dot_fwd, _dot_bwd)


def _silu(x):
    return x * jax.nn.sigmoid(x)


def _f_rms(x, gain):
    return x * lax.rsqrt(jnp.mean(x * x, axis=-1, keepdims=True) + EPS) * gain


def _f_dn_pre(xs0, xs1, xs2, xs3, ba, cw, alog, dtb):
    rows = xs0.shape[0]
    c = xs0 * cw[0:1] + xs1 * cw[1:2] + xs2 * cw[2:3] + xs3 * cw[3:4]
    qkv = _silu(c)
    qs, ks, bbs, gbs = [], [], [], []
    for h in range(DN_H):
        qh = qkv[:, h * DH:(h + 1) * DH]
        kh = qkv[:, DNW + h * DH:DNW + (h + 1) * DH]
        qs.append(qh * lax.rsqrt(jnp.sum(qh * qh, axis=-1, keepdims=True) + EPS) * (DH ** -0.5))
        ks.append(kh * lax.rsqrt(jnp.sum(kh * kh, axis=-1, keepdims=True) + EPS))
        beta = jax.nn.sigmoid(ba[:, h:h + 1])
        ar = ba[:, DN_H + h:DN_H + h + 1] + dtb[:, h:h + 1]
        softplus = jnp.maximum(ar, 0.0) + jnp.log1p(jnp.exp(-jnp.abs(ar)))
        g = -jnp.exp(alog[:, h:h + 1]) * softplus
        bbs.append(jnp.broadcast_to(beta, (rows, DH)))
        gbs.append(jnp.broadcast_to(g, (rows, DH)))
    return (jnp.concatenate(qs, axis=1), jnp.concatenate(ks, axis=1), qkv[:, 2 * DNW:],
            jnp.concatenate(bbs, axis=1), jnp.concatenate(gbs, axis=1))


def _f_dn_post(o, z, gain):
    ys = []
    for h in range(DN_H):
        oh = o[:, h * DH:(h + 1) * DH]
        zh = z[:, h * DH:(h + 1) * DH]
        ys.append(oh * lax.rsqrt(jnp.mean(oh * oh, axis=-1, keepdims=True) + EPS) * gain * _silu(zh))
    return jnp.concatenate(ys, axis=1)


def _f_merge(pa, pb, ga, gb):
    return jax.nn.sigmoid(ga) * pa + jax.nn.sigmoid(gb) * pb


def _f_swiglu(g, u):
    return _silu(g) * u


@jax.custom_vjp
def _unit_lower_inverse(a):
    c = a.shape[-1]
    eye = (lax.broadcasted_iota(jnp.int32, a.shape, 1) == lax.broadcasted_iota(jnp.int32, a.shape, 2)).astype(F32)
    p = -a
    t = eye + p
    for _ in range(max(c.bit_length() - 2, 0)):
        p = _raw_dot(p, p, "bnn", "x3")
        t = t + _raw_dot(t, p, "bnn", "x3")
    return t


def _unit_lower_inverse_fwd(a):
    t = _unit_lower_inverse(a)
    return t, t


def _unit_lower_inverse_bwd(t, g):
    return (-_raw_dot(_raw_dot(t, g, "btn", "x3"), t, "bnt", "x3"),)


_unit_lower_inverse.defvjp(_unit_lower_inverse_fwd, _unit_lower_inverse_bwd)


@jax.custom_vjp
def _known_inverse(a, t):
    return t


def _known_inverse_fwd(a, t):
    return t, t


def _known_inverse_bwd(t, g):
    return _unit_lower_inverse_bwd(t, g)[0], jnp.zeros_like(t)


_known_inverse.defvjp(_known_inverse_fwd, _known_inverse_bwd)


def _f_chunk(q, k, v, gb, bb, s, t_known=None, with_t=False):
    c = CHUNK
    nh = q.shape[0]
    ii = lax.broadcasted_iota(jnp.int32, (nh, c, c), 1)
    jj = lax.broadcasted_iota(jnp.int32, (nh, c, c), 2)
    incl = ii >= jj
    strict = ii > jj
    eye = (ii == jj).astype(F32)
    gcb = _dot(incl.astype(F32), gb, "bnn", True)
    lane0 = (lax.broadcasted_iota(jnp.int32, (nh, c, DH), 2) == 0).astype(F32)
    gcol = gcb[:, :, :c]
    grow = _dot(lane0, gcb, "bnt", True)
    decay = jnp.where(incl, jnp.exp(jnp.where(incl, gcol - grow, 0.0)), 0.0)
    kb = k * bb
    vb = v * bb
    a = jnp.where(strict, _dot(kb, k, "bnt", False) * decay, 0.0)
    t = _unit_lower_inverse(a) if t_known is None else _known_inverse(a, t_known)
    eg = jnp.exp(gcb)
    u = _dot(t, vb, "bnn", "x3")
    w = _dot(t, kb * eg, "bnn", "x3")
    qk = jnp.where(incl, _dot(q, k, "bnt", False) * decay, 0.0)
    qe = q * eg
    glast = gcb[:, c - 1:c, :]
    k_dec = k * jnp.exp(glast - gcb)
    e_last = jnp.exp(glast)
    outs = []
    for g in range(nh // DN_H):
        sl = slice(g * DN_H, (g + 1) * DN_H)
        v_new = u[sl] - _dot(w[sl], s, "bnn", False)
        outs.append(_dot(qe[sl], s, "bnn", False) + _dot(qk[sl], v_new, "bnn", False))
        s = s * e_last[sl] + _dot(k_dec[sl], v_new, "btn", False)
    o = jnp.concatenate(outs, axis=0)
    return (o, s, t) if with_t else (o, s)


def _f_swa(q8, kp, kc, vp, vc, bias8, qg, kg, sink, mask):
    kb = jnp.concatenate([kp, kc], axis=1)
    vb = jnp.concatenate([vp, vc], axis=1)
    kn = kb * lax.rsqrt(jnp.mean(kb * kb, axis=-1, keepdims=True) + EPS) * kg

    def rows(per_head):
        return jnp.stack([jnp.concatenate([per_head(kv, g) for g in range(SWA_G)], axis=0)
                          for kv in range(SWA_KV)], axis=0)

    qq = rows(lambda kv, g: q8[kv * SWA_G + g])
    qn = qq * lax.rsqrt(jnp.mean(qq * qq, axis=-1, keepdims=True) + EPS) * qg
    lg = _dot(qn, kn, "bnt", False) * (SWA_D ** -0.5) + rows(lambda kv, g: bias8[kv * SWA_G + g])
    lg = jnp.where(rows(lambda kv, g: mask), lg, NEG)
    sk = rows(lambda kv, g: jnp.broadcast_to(sink[kv][:, g:g + 1], (BLK, 1)))
    m = lax.stop_gradient(jnp.maximum(jnp.max(lg, axis=-1, keepdims=True), sk))
    p = jnp.exp(lg - m)
    den = jnp.sum(p, axis=-1, keepdims=True) + jnp.exp(sk - m)
    out = _dot(p / den, vb, "bnn", False)
    return jnp.stack([out[kv, g * BLK:(g + 1) * BLK] for kv in range(SWA_KV) for g in range(SWA_G)], axis=0)


def _bdot(a, b, kind="nn"):
    return lax.dot_general(a.astype(BF16), b.astype(BF16), _DIMS[kind], preferred_element_type=F32)


def _pc(kern, name, grid, in_specs, out_specs, out_shape, scratch=()):
    return pl.pallas_call(
        kern, name=name, grid=grid, in_specs=in_specs, out_specs=out_specs, out_shape=out_shape,
        scratch_shapes=list(scratch), compiler_params=_cparams(dimension_semantics=("arbitrary",) * len(grid)))


def _mm(a, b, kind, out_dtype, tm, tn, name):
    if kind == "tn":
        k, m = a.shape
    else:
        m, k = a.shape
    n = b.shape[0] if kind == "nt" else b.shape[1]
    tm, tn = min(tm, m), min(tn, n)
    assert m % tm == 0 and n % tn == 0, (name, a.shape, b.shape, tm, tn)

    def kern(a_ref, b_ref, o_ref):
        o_ref[...] = _bdot(a_ref[...], b_ref[...], kind).astype(o_ref.dtype)

    a_spec = pl.BlockSpec((k, tm), lambda i, j: (0, i)) if kind == "tn" else pl.BlockSpec((tm, k), lambda i, j: (i, 0))
    b_spec = pl.BlockSpec((tn, k), lambda i, j: (j, 0)) if kind == "nt" else pl.BlockSpec((k, tn), lambda i, j: (0, j))
    return _pc(kern, name, (m // tm, n // tn), [a_spec, b_spec], pl.BlockSpec((tm, tn), lambda i, j: (i, j)),
               SDS((m, n), out_dtype))(a, b)


def _rows(body, name, m, tm, row_ins, full_ins, row_outs, acc_outs=()):
    n_r, n_f, n_o, n_a = len(row_ins), len(full_ins), len(row_outs), len(acc_outs)
    assert m % tm == 0

    def kern(*refs):
        r = refs[:n_r]
        f = refs[n_r:n_r + n_f]
        o = refs[n_r + n_f:n_r + n_f + n_o]
        acc = refs[n_r + n_f + n_o:]
        outs, sums = body([x[...] for x in r], [x[...] for x in f])
        for ref, val in zip(o, outs, strict=True):
            ref[...] = val.astype(ref.dtype)
        if n_a:
            @pl.when(pl.program_id(0) == 0)
            def _():
                for ref in acc:
                    ref[...] = jnp.zeros(ref.shape, F32)

            for ref, val in zip(acc, sums, strict=True):
                ref[...] += val

    in_specs = [pl.BlockSpec((tm, w), functools.partial(lambda i, cb: (i, cb), cb=cb)) for _, w, cb in row_ins]
    in_specs += [pl.BlockSpec(x.shape, lambda i: (0, 0)) for x in full_ins]
    out_specs = [pl.BlockSpec((tm, w), lambda i: (i, 0)) for w, _ in row_outs]
    out_specs += [pl.BlockSpec(s, lambda i: (0, 0)) for s in acc_outs]
    out_shape = [SDS((m, w), dt) for w, dt in row_outs]
    out_shape += [SDS(s, F32) for s in acc_outs]
    return _pc(kern, name, (m // tm,), in_specs, out_specs, out_shape)(*[x for x, _, _ in row_ins], *full_ins)


def _whole(x):
    return (x, x.shape[1], 0)


def _zero_first(refs):
    @pl.when(pl.program_id(0) == 0)
    def _():
        for ref in refs:
            ref[...] = jnp.zeros(ref.shape, F32)


GROUP = 4


def _heads(ref):
    return jnp.stack([ref[g * CHUNK:(g + 1) * CHUNK, h * DH:(h + 1) * DH]
                      for g in range(GROUP) for h in range(DN_H)], axis=0)


def _unheads(ref, val):
    for g in range(GROUP):
        for h in range(DN_H):
            ref[g * CHUNK:(g + 1) * CHUNK, h * DH:(h + 1) * DH] = val[g * DN_H + h]


def _dn_chunks_fwd(q, k, v, gb, bb):
    s_len = q.shape[0]
    ng = s_len // (GROUP * CHUNK)

    def kern(q_ref, k_ref, v_ref, g_ref, b_ref, o_ref, sall_ref, t_ref, state):
        _zero_first([state])
        s = state[...]
        sall_ref[0] = s
        o, s_new, t = _f_chunk(*[_heads(r) for r in (q_ref, k_ref, v_ref, g_ref, b_ref)], s, with_t=True)
        _unheads(o_ref, o)
        t_ref[0] = t
        state[...] = s_new

    blk = pl.BlockSpec((GROUP * CHUNK, DNW), lambda c: (c, 0))
    return _pc(kern, "dn_chunks_fwd", (ng,), [blk] * 5,
               [blk, pl.BlockSpec((1, DN_H, DH, DH), lambda c: (c, 0, 0, 0)),
                pl.BlockSpec((1, GROUP * DN_H, CHUNK, CHUNK), lambda c: (c, 0, 0, 0))],
               [SDS((s_len, DNW), F32), SDS((ng, DN_H, DH, DH), F32), SDS((ng, GROUP * DN_H, CHUNK, CHUNK), F32)],
               scratch=[pltpu.VMEM((DN_H, DH, DH), F32)])(q, k, v, gb, bb)


def _dn_chunks_bwd(q, k, v, gb, bb, s_all, t_all, d_o):
    s_len = q.shape[0]
    ng = s_len // (GROUP * CHUNK)

    def kern(q_ref, k_ref, v_ref, g_ref, b_ref, sall_ref, t_ref, do_ref, dq_ref, dk_ref, dv_ref, dg_ref, db_ref,
             dstate):
        _zero_first([dstate])
        fn = functools.partial(_f_chunk, t_known=t_ref[0])
        _, vjp = jax.vjp(fn, *[_heads(r) for r in (q_ref, k_ref, v_ref, g_ref, b_ref)], sall_ref[0])
        *d_ins, ds = vjp((_heads(do_ref), dstate[...]))
        for ref, val in zip((dq_ref, dk_ref, dv_ref, dg_ref, db_ref), d_ins, strict=True):
            _unheads(ref, val)
        dstate[...] = ds

    blk = pl.BlockSpec((GROUP * CHUNK, DNW), lambda c: (ng - 1 - c, 0))
    return _pc(kern, "dn_chunks_bwd", (ng,),
               [blk] * 5 + [pl.BlockSpec((1, DN_H, DH, DH), lambda c: (ng - 1 - c, 0, 0, 0)),
                            pl.BlockSpec((1, GROUP * DN_H, CHUNK, CHUNK), lambda c: (ng - 1 - c, 0, 0, 0)), blk],
               [blk] * 5, [SDS((s_len, DNW), F32)] * 5,
               scratch=[pltpu.VMEM((DN_H, DH, DH), F32)])(q, k, v, gb, bb, s_all, t_all, d_o)


def _t5_bucket_table():
    qi = np.arange(BLK)[:, None]
    kj = np.arange(2 * BLK)[None, :]
    dist = BLK + qi - kj
    n = np.maximum(dist, 0)
    max_exact = NBUCKET // 2
    nf = np.maximum(n, 1).astype(np.float32)
    large = max_exact + (np.log(nf / np.float32(max_exact)) / np.float32(math.log(MAXDIST / max_exact))
                         * np.float32(NBUCKET - max_exact)).astype(np.int32)
    large = np.minimum(large, NBUCKET - 1)
    return np.where(n < max_exact, n, large)


def _bucket_onehot_t():
    table = _t5_bucket_table().reshape(-1)
    return (np.arange(NBUCKET)[:, None] == table[None, :]).astype(np.float32)


def _swa_mask(first):
    qi = lax.broadcasted_iota(jnp.int32, (BLK, 2 * BLK), 0)
    kj = lax.broadcasted_iota(jnp.int32, (BLK, 2 * BLK), 1)
    dist = BLK + qi - kj
    window = (dist >= 0) & (dist < BLK)
    return window & ((kj >= BLK) | jnp.logical_not(first))


def _bias_expand(rel_bias_t):
    onehot = jnp.asarray(_bucket_onehot_t())

    def kern(r_ref, oh_ref, o_ref):
        o_ref[...] = _raw_dot(r_ref[...], oh_ref[...], "nn", True)

    return pl.pallas_call(
        kern, name="bias_expand", out_shape=SDS((SWA_H, BLK * 2 * BLK), F32), compiler_params=_cparams(),
    )(rel_bias_t, onehot)


def _bias_reduce(d_bias_flat):
    onehot = jnp.asarray(_bucket_onehot_t())

    def kern(d_ref, oh_ref, o_ref):
        o_ref[...] = _raw_dot(d_ref[...], oh_ref[...], "nt", True)

    return pl.pallas_call(
        kern, name="bias_reduce", out_shape=SDS((SWA_H, NBUCKET), F32), compiler_params=_cparams(),
    )(d_bias_flat, onehot)


def _swa_specs(nb, rev):
    def blk(n):
        return (nb - 1 - n) if rev else n

    def before(n):
        return jnp.maximum(blk(n) - 1, 0)

    q_spec = pl.BlockSpec((BLK, SWAW), lambda n: (blk(n), C_SQ // SWAW))
    k_cur = pl.BlockSpec((BLK, SWAKW), lambda n: (blk(n), C_SK // SWAKW))
    k_prev = pl.BlockSpec((BLK, SWAKW), lambda n: (before(n), C_SK // SWAKW))
    v_cur = pl.BlockSpec((BLK, SWAKW), lambda n: (blk(n), C_SV // SWAKW))
    v_prev = pl.BlockSpec((BLK, SWAKW), lambda n: (before(n), C_SV // SWAKW))
    bias = pl.BlockSpec((SWA_H, BLK, 2 * BLK), lambda n: (0, 0, 0))
    gain = pl.BlockSpec((1, SWA_D), lambda n: (0, 0))
    sink = pl.BlockSpec((SWA_KV, 1, SWA_G), lambda n: (0, 0, 0))
    wide = pl.BlockSpec((BLK, SWAW), lambda n: (blk(n), 0))
    narrow = pl.BlockSpec((BLK, SWAKW), lambda n: (blk(n), 0))
    return [q_spec, k_prev, k_cur, v_prev, v_cur, bias, gain, gain, sink], wide, narrow


def _split_heads(x):
    return jnp.stack([x[:, h * SWA_D:(h + 1) * SWA_D] for h in range(x.shape[1] // SWA_D)], axis=0)


def _join_heads(x):
    return jnp.concatenate([x[h] for h in range(x.shape[0])], axis=1)


def _swa_fwd(proj, bias, qg, kg, sinks):
    s_len = proj.shape[0]
    nb = s_len // BLK
    in_specs, wide, _ = _swa_specs(nb, False)

    def kern(q_ref, kp_ref, kc_ref, vp_ref, vc_ref, b_ref, qg_ref, kg_ref, s_ref, o_ref):
        mask = _swa_mask(pl.program_id(0) == 0)
        o8 = _f_swa(*[_split_heads(r[...]) for r in (q_ref, kp_ref, kc_ref, vp_ref, vc_ref)], b_ref[...], qg_ref[...],
                    kg_ref[...], s_ref[...], mask)
        o_ref[...] = _join_heads(o8).astype(BF16)

    return _pc(kern, "swa_fwd", (nb,), in_specs, wide, SDS((s_len, SWAW), BF16))(
        proj, proj, proj, proj, proj, bias, qg, kg, sinks)


def _swa_bwd(proj, bias, qg, kg, sinks, d_out):
    s_len = proj.shape[0]
    nb = s_len // BLK
    in_specs, wide, narrow = _swa_specs(nb, True)

    def kern(q_ref, kp_ref, kc_ref, vp_ref, vc_ref, b_ref, qg_ref, kg_ref, s_ref, do_ref,
             dq_ref, dk_ref, dv_ref, db_ref, dqg_ref, dkg_ref, ds_ref, carry_k, carry_v):
        n = pl.program_id(0)
        mask = _swa_mask(n == nb - 1)
        _zero_first([carry_k, carry_v, db_ref, ds_ref, dqg_ref, dkg_ref])
        fn = functools.partial(_f_swa, mask=mask)
        _, vjp = jax.vjp(fn, *[_split_heads(r[...]) for r in (q_ref, kp_ref, kc_ref, vp_ref, vc_ref)], b_ref[...],
                         qg_ref[...], kg_ref[...], s_ref[...])
        dq, dkp, dkc, dvp, dvc, dbias, dqg, dkg, dsink = vjp(_split_heads(do_ref[...]))
        dq_ref[...] = _join_heads(dq).astype(BF16)
        dk_ref[...] = (_join_heads(dkc) + carry_k[...]).astype(BF16)
        dv_ref[...] = (_join_heads(dvc) + carry_v[...]).astype(BF16)
        carry_k[...] = _join_heads(dkp)
        carry_v[...] = _join_heads(dvp)
        db_ref[...] += dbias
        dqg_ref[...] += dqg
        dkg_ref[...] += dkg
        ds_ref[...] += dsink

    bias_spec, gain, sink = in_specs[5], in_specs[6], in_specs[8]
    return _pc(
        kern, "swa_bwd", (nb,), in_specs + [wide], [wide, narrow, narrow, bias_spec, gain, gain, sink],
        [SDS((s_len, SWAW), BF16), SDS((s_len, SWAKW), BF16), SDS((s_len, SWAKW), BF16),
         SDS((SWA_H, BLK, 2 * BLK), F32), SDS((1, SWA_D), F32), SDS((1, SWA_D), F32), SDS((SWA_KV, 1, SWA_G), F32)],
        scratch=[pltpu.VMEM((BLK, SWAKW), F32), pltpu.VMEM((BLK, SWAKW), F32)],
    )(proj, proj, proj, proj, proj, bias, qg, kg, sinks, d_out)


def _branch_merge(y_dn, y_swa, wa, wb, proj):
    s_len = y_dn.shape[0]
    tm = min(512, s_len)

    def kern(ya_ref, yb_ref, wa_ref, wb_ref, ga_ref, gb_ref, pa_ref, pb_ref, m_ref):
        pa = _bdot(ya_ref[...], wa_ref[0])
        pb = _bdot(yb_ref[...], wb_ref[0])
        pa_ref[...] = pa
        pb_ref[...] = pb
        m_ref[...] = _f_merge(pa, pb, ga_ref[...], gb_ref[...]).astype(BF16)

    y_spec = pl.BlockSpec((tm, DNW), lambda i, s: (i, 0))
    w_spec = pl.BlockSpec((1, DNW, CSH), lambda i, s: (s, 0, 0))
    o_spec = pl.BlockSpec((tm, CSH), lambda i, s: (i, s))
    ga_spec = pl.BlockSpec((tm, CSH), lambda i, s: (i, C_GATE // CSH + s))
    gb_spec = pl.BlockSpec((tm, CSH), lambda i, s: (i, (C_GATE + D) // CSH + s))
    return _pc(kern, "branch_merge", (s_len // tm, N_CHIPS), [y_spec, y_spec, w_spec, w_spec, ga_spec, gb_spec],
               [o_spec] * 3, [SDS((s_len, D), F32), SDS((s_len, D), F32), SDS((s_len, D), BF16)],
               )(y_dn, y_swa, wa, wb, proj, proj)


def _out_proj(merged, w_out, x, gain):
    s_len = x.shape[0]
    tm = min(256, s_len)

    def kern(m_ref, w_ref, x_ref, g_ref, x1_ref, h2_ref):
        x1 = x_ref[...] + _bdot(m_ref[...], w_ref[...])
        x1_ref[...] = x1
        h2_ref[...] = _f_rms(x1, g_ref[...]).astype(BF16)

    row = pl.BlockSpec((tm, D), lambda i: (i, 0))
    return _pc(kern, "out_proj", (s_len // tm,),
               [row, pl.BlockSpec((D, D), lambda i: (0, 0)), row, pl.BlockSpec((1, D), lambda i: (0, 0))],
               [row, row], [SDS((s_len, D), F32), SDS((s_len, D), BF16)])(merged, w_out, x, gain)


def _ffn_up(h2, wg, wu):
    s_len = h2.shape[0]
    tm = min(512, s_len)

    def kern(h_ref, g_ref, u_ref, gt_ref, up_ref, act_ref):
        h = h_ref[...]
        g = _bdot(h, g_ref[0], "nt")
        u = _bdot(h, u_ref[0], "nt")
        gt_ref[0] = g.astype(BF16)
        up_ref[0] = u.astype(BF16)
        act_ref[0] = _f_swiglu(g, u).astype(BF16)

    w_spec = pl.BlockSpec((1, FSH, D), lambda s, i: (s, 0, 0))
    o_spec = pl.BlockSpec((1, tm, FSH), lambda s, i: (s, i, 0))
    shape = (N_CHIPS, s_len, FSH)
    return _pc(kern, "ffn_up", (N_CHIPS, s_len // tm), [pl.BlockSpec((tm, D), lambda s, i: (i, 0)), w_spec, w_spec],
               [o_spec] * 3, [SDS(shape, BF16)] * 3)(h2, wg, wu)


def _ffn_down_loss(act, wd, x1, target):
    s_len = x1.shape[0]
    tm = min(256, s_len)

    def kern(a_ref, w_ref, x_ref, t_ref, dy_ref, dyb_ref, loss_ref):
        _zero_first([loss_ref])
        y = x_ref[...]
        for s in range(N_CHIPS):
            y = y + _bdot(a_ref[s], w_ref[s])
        d = y - t_ref[...]
        dy = d * (1.0 / D)
        dy_ref[...] = dy
        dyb_ref[...] = dy.astype(BF16)
        loss_ref[...] += jnp.sum(d * d).reshape(1, 1) * (0.5 / D)

    row = pl.BlockSpec((tm, D), lambda i: (i, 0))
    return _pc(kern, "ffn_down_loss", (s_len // tm,),
               [pl.BlockSpec((N_CHIPS, tm, FSH), lambda i: (0, i, 0)),
                pl.BlockSpec((N_CHIPS, FSH, D), lambda i: (0, 0, 0)), row, row],
               [row, row, pl.BlockSpec((1, 1), lambda i: (0, 0))],
               [SDS((s_len, D), F32), SDS((s_len, D), BF16), SDS((1, 1), F32)])(act, wd, x1, target)


def _ffn_dact(dy_b, wd, gt, up):
    s_len = dy_b.shape[0]
    tm = min(512, s_len)

    def kern(dy_ref, w_ref, gt_ref, up_ref, dg_ref, du_ref):
        d_act = _bdot(dy_ref[...], w_ref[0], "nt")
        _, vjp = jax.vjp(_f_swiglu, gt_ref[0].astype(F32), up_ref[0].astype(F32))
        dg, du = vjp(d_act)
        dg_ref[0] = dg.astype(BF16)
        du_ref[0] = du.astype(BF16)

    a_spec = pl.BlockSpec((1, tm, FSH), lambda s, i: (s, i, 0))
    shape = (N_CHIPS, s_len, FSH)
    return _pc(kern, "ffn_dact", (N_CHIPS, s_len // tm),
               [pl.BlockSpec((tm, D), lambda s, i: (i, 0)), pl.BlockSpec((1, FSH, D), lambda s, i: (s, 0, 0)),
                a_spec, a_spec],
               [a_spec, a_spec], [SDS(shape, BF16), SDS(shape, BF16)])(dy_b, wd, gt, up)


def _gw_ffn(lhs, rhs, name):
    s_len = rhs.shape[0]
    n = len(lhs)
    tn = 512

    def kern(*refs):
        g = refs[n][...]
        for i in range(n):
            refs[n + 1 + i][0] = _bdot(refs[i][0], g, "tn").astype(BF16)

    a_spec = pl.BlockSpec((1, s_len, FSH), lambda s, j: (s, 0, 0))
    o_spec = pl.BlockSpec((1, FSH, tn), lambda s, j: (s, 0, j))
    return _pc(kern, name, (N_CHIPS, D // tn), [a_spec] * n + [pl.BlockSpec((s_len, tn), lambda s, j: (0, j))],
               [o_spec] * n, [SDS((N_CHIPS, FSH, D), BF16)] * n)(*lhs, rhs)


def _ffn_dh2(d_gt, d_up, wg, wu, x1, dy, gain):
    s_len = x1.shape[0]
    tm = min(256, s_len)

    def kern(dg_ref, du_ref, wg_ref, wu_ref, x_ref, dy_ref, g_ref, dx_ref, dxb_ref, dgain_ref):
        _zero_first([dgain_ref])
        dh2 = jnp.zeros((tm, D), F32)
        for s in range(N_CHIPS):
            dh2 = dh2 + _bdot(dg_ref[s], wg_ref[s]) + _bdot(du_ref[s], wu_ref[s])
        _, vjp = jax.vjp(_f_rms, x_ref[...], g_ref[...])
        dx, dgain = vjp(dh2)
        dx1 = dx + dy_ref[...]
        dx_ref[...] = dx1
        dxb_ref[...] = dx1.astype(BF16)
        dgain_ref[...] += dgain

    row = pl.BlockSpec((tm, D), lambda i: (i, 0))
    d_spec = pl.BlockSpec((N_CHIPS, tm, FSH), lambda i: (0, i, 0))
    w_spec = pl.BlockSpec((N_CHIPS, FSH, D), lambda i: (0, 0, 0))
    vec = pl.BlockSpec((1, D), lambda i: (0, 0))
    return _pc(kern, "ffn_dh2", (s_len // tm,), [d_spec, d_spec, w_spec, w_spec, row, row, vec],
               [row, row, vec], [SDS((s_len, D), F32), SDS((s_len, D), BF16), SDS((1, D), F32)],
               )(d_gt, d_up, wg, wu, x1, dy, gain)


def _merge_bwd(dx1_b, w_out, pa, pb, proj):
    s_len = dx1_b.shape[0]
    tm = min(256, s_len)

    def kern(dx_ref, w_ref, pa_ref, pb_ref, g_ref, dpa_ref, dpb_ref, dg_ref):
        dm = _bdot(dx_ref[...], w_ref[...], "nt")
        gates = g_ref[...]
        _, vjp = jax.vjp(_f_merge, pa_ref[...], pb_ref[...], gates[:, :D], gates[:, D:])
        dpa, dpb, dga, dgb = vjp(dm)
        dpa_ref[...] = dpa.astype(BF16)
        dpb_ref[...] = dpb.astype(BF16)
        dg_ref[:, :D] = dga.astype(BF16)
        dg_ref[:, D:] = dgb.astype(BF16)

    row = pl.BlockSpec((tm, D), lambda i: (i, 0))
    return _pc(kern, "merge_bwd", (s_len // tm,),
               [row, pl.BlockSpec((D, D), lambda i: (0, 0)), row, row,
                pl.BlockSpec((tm, 2 * D), lambda i: (i, C_GATE // (2 * D)))],
               [row, row, pl.BlockSpec((tm, 2 * D), lambda i: (i, 0))],
               [SDS((s_len, D), BF16), SDS((s_len, D), BF16), SDS((s_len, 2 * D), BF16)],
               )(dx1_b, w_out, pa, pb, proj)


def _d_branch(d_pa, d_pb, wa, wb):
    s_len = d_pa.shape[0]
    tm = min(512, s_len)

    def kern(da_ref, db_ref, wa_ref, wb_ref, oa_ref, ob_ref):
        acc_a = jnp.zeros((tm, DNW), F32)
        acc_b = jnp.zeros((tm, SWAW), F32)
        for s in range(N_CHIPS):
            acc_a = acc_a + _bdot(da_ref[:, s * CSH:(s + 1) * CSH], wa_ref[s], "nt")
            acc_b = acc_b + _bdot(db_ref[:, s * CSH:(s + 1) * CSH], wb_ref[s], "nt")
        oa_ref[...] = acc_a
        ob_ref[...] = acc_b

    row = pl.BlockSpec((tm, D), lambda i: (i, 0))
    w_spec = pl.BlockSpec((N_CHIPS, DNW, CSH), lambda i: (0, 0, 0))
    out = pl.BlockSpec((tm, DNW), lambda i: (i, 0))
    return _pc(kern, "d_branch", (s_len // tm,), [row, row, w_spec, w_spec], [out, out],
               [SDS((s_len, DNW), F32), SDS((s_len, SWAW), F32)])(d_pa, d_pb, wa, wb)


def _gw_branch(y_dn, y_swa, d_pa, d_pb):
    s_len = y_dn.shape[0]

    def kern(ya_ref, yb_ref, da_ref, db_ref, oa_ref, ob_ref):
        oa_ref[0] = _bdot(ya_ref[...], da_ref[...], "tn").astype(BF16)
        ob_ref[0] = _bdot(yb_ref[...], db_ref[...], "tn").astype(BF16)

    y_spec = pl.BlockSpec((s_len, DNW), lambda s: (0, 0))
    d_spec = pl.BlockSpec((s_len, CSH), lambda s: (0, s))
    o_spec = pl.BlockSpec((1, DNW, CSH), lambda s: (s, 0, 0))
    shape = (N_CHIPS, DNW, CSH)
    return _pc(kern, "gw_branch", (N_CHIPS,), [y_spec, y_spec, d_spec, d_spec], [o_spec, o_spec],
               [SDS(shape, BF16), SDS(shape, BF16)])(y_dn, y_swa, d_pa, d_pb)


def _dh_rms(d_proj, w_in_p, x, dx1, gain):
    s_len = x.shape[0]
    tm = min(256, s_len)

    def kern(dp_ref, w_ref, x_ref, r_ref, g_ref, gx_ref, dgain_ref):
        _zero_first([dgain_ref])
        dh = _bdot(dp_ref[...], w_ref[...], "nt")
        _, vjp = jax.vjp(_f_rms, x_ref[...], g_ref[...])
        dx, dgain = vjp(dh)
        gx_ref[...] = dx + r_ref[...]
        dgain_ref[...] += dgain

    row = pl.BlockSpec((tm, D), lambda i: (i, 0))
    vec = pl.BlockSpec((1, D), lambda i: (0, 0))
    return _pc(kern, "dh_rms", (s_len // tm,),
               [pl.BlockSpec((tm, PW), lambda i: (i, 0)), pl.BlockSpec((D, PW), lambda i: (0, 0)), row, row, vec],
               [row, vec], [SDS((s_len, D), F32), SDS((1, D), F32)])(d_proj, w_in_p, x, dx1, gain)


HALO = 8


def _conv_taps(cur_ref, prev_ref, halo, first):
    tm = cur_ref.shape[0]
    halo[0:HALO, :] = jnp.where(first, 0.0, prev_ref[...])
    halo[HALO:, :] = cur_ref[...]
    return [halo[HALO - n:HALO - n + tm, :] for n in range(CONV - 1, 0, -1)] + [cur_ref[...]]


def _dn_pre_specs(s_len, tm, blk):
    cur = pl.BlockSpec((tm, QKVW), lambda i: (blk(i), 0))
    prev = pl.BlockSpec((HALO, QKVW), lambda i: (jnp.maximum(blk(i) * (tm // HALO) - 1, 0), 0))
    ba = pl.BlockSpec((tm, 128), lambda i: (blk(i), C_BA // 128))
    row = pl.BlockSpec((tm, DNW), lambda i: (blk(i), 0))
    full = [pl.BlockSpec((CONV, QKVW), lambda i: (0, 0)), pl.BlockSpec((1, DN_H), lambda i: (0, 0)),
            pl.BlockSpec((1, DN_H), lambda i: (0, 0))]
    return cur, prev, ba, row, full


def _dn_pre_fwd(proj, conv_w, alog, dtb):
    s_len = proj.shape[0]
    tm = min(128, s_len)
    cur, prev, ba, row, full = _dn_pre_specs(s_len, tm, lambda i: i)

    def kern(cur_ref, prev_ref, ba_ref, cw_ref, al_ref, dt_ref, q_ref, k_ref, v_ref, bb_ref, gb_ref, halo):
        xs = _conv_taps(cur_ref, prev_ref, halo, pl.program_id(0) == 0)
        outs = _f_dn_pre(*xs, ba_ref[...], cw_ref[...], al_ref[...], dt_ref[...])
        for ref, val in zip((q_ref, k_ref, v_ref, bb_ref, gb_ref), outs, strict=True):
            ref[...] = val

    return _pc(kern, "dn_pre_fwd", (s_len // tm,), [cur, prev, ba] + full, [row] * 5, [SDS((s_len, DNW), F32)] * 5,
               scratch=[pltpu.VMEM((tm + HALO, QKVW), F32)])(proj, proj, proj, conv_w, alog, dtb)


def _dn_pre_bwd(proj, conv_w, alog, dtb, cots):
    s_len = proj.shape[0]
    tm = min(128, s_len)
    nb = s_len // tm
    cur, prev, ba, row, full = _dn_pre_specs(s_len, tm, lambda i: nb - 1 - i)

    def kern(cur_ref, prev_ref, ba_ref, cw_ref, al_ref, dt_ref, dq_ref, dk_ref, dv_ref, dbb_ref, dgb_ref,
             dqkv_ref, dba_ref, dcw_ref, dal_ref, ddt_ref, halo, *tails):
        i = pl.program_id(0)
        _zero_first([dcw_ref, dal_ref, ddt_ref])

        @pl.when(i == 0)
        def _():
            for t in tails:
                t[tm:, :] = jnp.zeros((HALO, QKVW), F32)

        xs = _conv_taps(cur_ref, prev_ref, halo, i == nb - 1)
        _, vjp = jax.vjp(_f_dn_pre, *xs, ba_ref[...], cw_ref[...], al_ref[...], dt_ref[...])
        *dxs, dba, dcw, dal, ddt = vjp((dq_ref[...], dk_ref[...], dv_ref[...], dbb_ref[...], dgb_ref[...]))
        total = dxs[CONV - 1]
        for j, t in enumerate(tails):
            n = CONV - 1 - j
            t[0:tm, :] = dxs[j]
            total = total + t[n:n + tm, :]
            t[tm:, :] = dxs[j][0:HALO, :]
        dqkv_ref[...] = total.astype(BF16)
        dba_ref[...] = dba.astype(BF16)
        dcw_ref[...] += dcw
        dal_ref[...] += dal
        ddt_ref[...] += ddt

    return _pc(kern, "dn_pre_bwd", (nb,), [cur, prev, ba] + full + [row] * 5,
               [cur, pl.BlockSpec((tm, 128), lambda i: (nb - 1 - i, 0))] + full,
               [SDS((s_len, QKVW), BF16), SDS((s_len, 128), BF16), SDS((CONV, QKVW), F32), SDS((1, DN_H), F32),
                SDS((1, DN_H), F32)],
               scratch=[pltpu.VMEM((tm + HALO, QKVW), F32)] * CONV)(proj, proj, proj, conv_w, alog, dtb, *cots)


def _pad_w_in(w_in):
    pieces = [w_in[:, o0:o0 + w] for o0, w, _ in sorted(_ORIG_PIECES, key=lambda t: t[2])]
    pieces.append(jnp.zeros((w_in.shape[0], PW - D_IN), w_in.dtype))
    return jnp.concatenate(pieces, axis=1)


def _unpad_w_in(g):
    return jnp.concatenate([g[:, p0:p0 + w] for _, w, p0 in _ORIG_PIECES], axis=1)


def _local_step(x, target, wts):
    s_len = x.shape[0]
    tm = min(256, s_len)
    tmh = min(128, s_len)
    w_in_p = wts["w_in_p"]
    attn_gain = wts["attn_norm"]
    ffn_gain = wts["ffn_norm"]
    conv_w = wts["dn_conv"]
    alog, dtb, out_gain = wts["dn_a_log"], wts["dn_dt_bias"], wts["dn_out_norm"]
    qg, kg = wts["swa_q_norm"], wts["swa_k_norm"]
    sinks = wts["swa_sinks"].reshape(SWA_KV, 1, SWA_G)

    (h,) = _rows(lambda r, f: ([_f_rms(r[0], f[0])], []), "rms1_fwd", s_len, tm, [_whole(x)], [attn_gain],
                 [(D, BF16)])
    proj = _mm(h, w_in_p, "nn", F32, 256, PW, "mm_proj")
    q_dn, k_dn, v_dn, bb, gb = _dn_pre_fwd(proj, conv_w, alog, dtb)
    o_dn, s_all, t_all = _dn_chunks_fwd(q_dn, k_dn, v_dn, gb, bb)
    post_ins = [_whole(o_dn), (proj, DNW, C_Z // DNW)]
    (y_dn,) = _rows(lambda r, f: ([_f_dn_post(r[0], r[1], f[0])], []), "dn_post_fwd", s_len, tm, post_ins,
                    [out_gain], [(DNW, BF16)])

    bias = _bias_expand(wts["rel_bias"].T).reshape(SWA_H, BLK, 2 * BLK)
    y_swa = _swa_fwd(proj, bias, qg, kg, sinks)

    wts = {**wts, **wts["late"](y_swa)}
    p_a, p_b, merged = _branch_merge(y_dn, y_swa, wts["wa"], wts["wb"], proj)
    x1, h2 = _out_proj(merged, wts["w_out"], x, ffn_gain)
    gt, up, act = _ffn_up(h2, wts["wg"], wts["wu"])
    dy, dy_b, loss = _ffn_down_loss(act, wts["wd"], x1, target)

    grads = {}
    d_gt, d_up = _ffn_dact(dy_b, wts["wd"], gt, up)
    (grads["w_down"],) = _gw_ffn([act], dy_b, "gw_down")
    grads["w_gate"], grads["w_up"] = _gw_ffn([d_gt, d_up], h2, "gw_gate_up")
    dx1, dx1_b, grads["ffn_norm"] = _ffn_dh2(d_gt, d_up, wts["wg"], wts["wu"], x1, dy, ffn_gain)
    grads["w_out"] = _mm(merged, dx1_b, "tn", BF16, 512, 512, "gw_out")
    d_pa, d_pb, d_gr = _merge_bwd(dx1_b, wts["w_out"], p_a, p_b, proj)
    d_ydn, d_yswa = _d_branch(d_pa, d_pb, wts["wa"], wts["wb"])
    grads["w_branch_dn"], grads["w_branch_swa"] = _gw_branch(y_dn, y_swa, d_pa, d_pb)
    token = wts["send_early"](grads)
    qg_t = qg + token[0:1, 0:1]
    out_gain_t = out_gain + token[0:1, 0:1]

    d_sq, d_sk, d_sv, d_bias, grads["swa_q_norm"], grads["swa_k_norm"], d_sinks = _swa_bwd(
        proj, bias, qg_t, kg, sinks, d_yswa)
    grads["swa_sinks"] = d_sinks.reshape(1, SWA_H)
    grads["rel_bias"] = _bias_reduce(d_bias.reshape(SWA_H, BLK * 2 * BLK)).T

    def post_bwd(r, f):
        _, vjp = jax.vjp(_f_dn_post, r[0], r[1], f[0])
        d_o, d_z, d_gain = vjp(r[2])
        return [d_o, d_z], [d_gain]

    d_o, d_z, grads["dn_out_norm"] = _rows(post_bwd, "dn_post_bwd", s_len, tm, post_ins + [_whole(d_ydn)], [out_gain_t],
                                           [(DNW, F32), (DNW, BF16)], [(1, DH)])
    d_q, d_k, d_v, d_gb, d_bb = _dn_chunks_bwd(q_dn, k_dn, v_dn, gb, bb, s_all, t_all, d_o)

    d_qkv, d_ba, grads["dn_conv"], grads["dn_a_log"], grads["dn_dt_bias"] = _dn_pre_bwd(
        proj, conv_w, alog, dtb, (d_q, d_k, d_v, d_bb, d_gb))

    d_proj = jnp.concatenate(
        [d_qkv, d_z, d_gr, d_sq, d_sk, d_sv, d_ba, jnp.zeros((s_len, PW - C_BA - 128), BF16)], axis=1)
    grads["w_in_p"] = _mm(h, d_proj, "tn", BF16, 512, 1024, "gw_in")
    token = wts["send_in"](grads["w_in_p"])
    grad_x, grads["attn_norm"] = _dh_rms(d_proj, w_in_p, x, dx1, attn_gain + token[0:1, 0:1])
    return loss, grad_x, grads


_HBM = pl.BlockSpec(memory_space=pl.ANY)


def _place():
    return lax.axis_index("x"), lax.axis_index("y"), lax.axis_index("c")


def _other_chips(x, y):
    return [(1 - x, y), (x, 1 - y), (1 - x, 1 - y)]


def _rcopy(src, dst, send_sems, recv_sems, k, to):
    return pltpu.make_async_remote_copy(src_ref=src, dst_ref=dst, send_sem=send_sems.at[k], recv_sem=recv_sems.at[k],
                                        device_id=to, device_id_type=MESH)


def _comm_call(body, name, ins, out_shapes, n_remote, landing=0):
    first = len(ins) - landing
    return pl.pallas_call(
        body, name=name, in_specs=[_HBM] * len(ins), out_specs=[_HBM] * len(out_shapes), out_shape=out_shapes,
        scratch_shapes=[pltpu.SemaphoreType.DMA((n_remote,)), pltpu.SemaphoreType.DMA((n_remote,))],
        input_output_aliases={first + i: i for i in range(landing)},
        compiler_params=_cparams(has_side_effects=True),
    )(*ins)


def _own_slot(blocks, chip):
    return [lax.dynamic_update_slice(jnp.zeros((N_CHIPS,) + b.shape, b.dtype), b[None], (chip, 0, 0)) for b in blocks]


def _gather_weights(ws, chip):
    n = len(ws)
    halves = [w.shape[0] // 2 for w in ws]

    def body(*refs):
        w_refs, o_refs = refs[:n], refs[2 * n:3 * n]
        send_sems, recv_sems = refs[3 * n:]
        x, y, c = _place()
        s = 2 * x + y
        sib = (x, y, 1 - c)
        chips = _other_chips(x, y)

        def rows(i, half):
            return pl.ds(half * halves[i], halves[i])

        first = []
        for j, (cx, cy) in enumerate(chips):
            for i in range(n):
                cp = _rcopy(w_refs[i].at[rows(i, c), :], o_refs[i].at[s, rows(i, c), :], send_sems, recv_sems,
                            j * n + i, (cx, cy, c))
                cp.start()
                first.append(cp)
        passed = []
        for j, (cx, cy) in enumerate(chips):
            sj = 2 * cx + cy
            for i in range(n):
                blk = o_refs[i].at[sj, rows(i, c), :]
                _rcopy(blk, blk, send_sems, recv_sems, j * n + i, (cx, cy, c)).wait_recv()
                cp = _rcopy(blk, blk, send_sems, recv_sems, (3 + j) * n + i, sib)
                cp.start()
                passed.append(cp)
        for j, (cx, cy) in enumerate(chips):
            sj = 2 * cx + cy
            for i in range(n):
                blk = o_refs[i].at[sj, rows(i, 1 - c), :]
                _rcopy(blk, blk, send_sems, recv_sems, (3 + j) * n + i, sib).wait_recv()
        for cp in first + passed:
            cp.wait_send()

    return _comm_call(body, "gather_weights", list(ws) + _own_slot(ws, chip),
                      [SDS((N_CHIPS,) + w.shape, w.dtype) for w in ws], 6 * n, landing=n)


_HBM_ONLY = pl.BlockSpec(memory_space=pltpu.HBM)
_SEM = pl.BlockSpec(memory_space=pltpu.SEMAPHORE)
_DATAFLOW = pltpu.SideEffectType.DATAFLOW_SIDE_EFFECTING


def _in_hbm(a):
    return pltpu.with_memory_space_constraint(a, pltpu.HBM)


def _gather_windows(blocks):
    halves = [b.shape[0] // 2 for b in blocks]

    def src_at(ref, i, c, sj):
        return ref.at[pl.ds(c * halves[i], halves[i]), :]

    def dst_at(ref, i, c, s_from):
        return ref.at[s_from, pl.ds(c * halves[i], halves[i]), :]

    return src_at, dst_at


def _exchange_windows():
    return (lambda ref, i, c, sj: ref.at[sj]), (lambda ref, i, c, s_from: ref.at[s_from])


def _split_start(name, ws, lands, dep, windows):
    n = len(ws)
    src_at, dst_at = windows

    def body(*refs):
        w_refs, l_refs = refs[:n], refs[n:2 * n]
        send_sems, recv_sems = refs[2 * n + 1], refs[2 * n + 2]
        token = refs[-1]
        x, y, c = _place()
        s = 2 * x + y
        for j, (cx, cy) in enumerate(_other_chips(x, y)):
            for i in range(n):
                _rcopy(src_at(w_refs[i], i, c, 2 * cx + cy), dst_at(l_refs[i], i, c, s), send_sems, recv_sems,
                       j * n + i, (cx, cy, c)).start()
        token[...] = jnp.zeros_like(token)

    outs = pl.pallas_call(
        body, name=name,
        out_shape=(pltpu.SemaphoreType.DMA((3 * n,)), pltpu.SemaphoreType.DMA((3 * n,)),
                   *[pltpu.HBM(w.shape, w.dtype) for w in ws], *[pltpu.HBM(t.shape, t.dtype) for t in lands],
                   SDS((8, 128), F32)),
        in_specs=[_HBM_ONLY] * (2 * n) + [pl.BlockSpec(memory_space=pl.ANY)],
        out_specs=(_SEM, _SEM, *[_HBM_ONLY] * (2 * n), pl.BlockSpec(memory_space=pltpu.VMEM)),
        input_output_aliases={i: 2 + i for i in range(2 * n)},
        compiler_params=pltpu.CompilerParams(has_side_effects=_DATAFLOW),
    )(*[_in_hbm(w) for w in ws], *[_in_hbm(t) for t in lands], dep)
    return outs[0], outs[1], outs[2:2 + n], outs[2 + n:2 + 2 * n], outs[-1]


def _split_wait(name, w_thru, l_thru, send_sems, recv_sems, after, windows):
    n = len(w_thru)
    src_at, dst_at = windows

    def body(*refs):
        w_refs, l_refs = refs[:n], refs[n:2 * n]
        send_sems, recv_sems = refs[2 * n], refs[2 * n + 1]
        x, y, c = _place()
        for j, (cx, cy) in enumerate(_other_chips(x, y)):
            sj = 2 * cx + cy
            for i in range(n):
                cp = _rcopy(src_at(w_refs[i], i, c, sj), dst_at(l_refs[i], i, c, sj), send_sems, recv_sems, j * n + i,
                            (cx, cy, c))
                cp.wait_send()
                cp.wait_recv()

    outs = pl.pallas_call(
        body, name=name,
        out_shape=[pltpu.HBM(w.shape, w.dtype) for w in w_thru] + [pltpu.HBM(t.shape, t.dtype) for t in l_thru],
        in_specs=[_HBM_ONLY] * (2 * n) + [_SEM, _SEM, pl.BlockSpec(memory_space=pl.ANY)],
        out_specs=[_HBM_ONLY] * (2 * n),
        input_output_aliases={i: i for i in range(2 * n)},
        compiler_params=pltpu.CompilerParams(has_side_effects=_DATAFLOW),
    )(*w_thru, *l_thru, send_sems, recv_sems, after)
    return outs[n:]


def _sibling_fill(lands):
    n = len(lands)
    halves = [t.shape[1] // 2 for t in lands]

    def body(*refs):
        o_refs = refs[n:2 * n]
        send_sems, recv_sems = refs[2 * n:]
        x, y, c = _place()
        sib = (x, y, 1 - c)
        chips = _other_chips(x, y)
        sent = []
        for j, (cx, cy) in enumerate(chips):
            for i in range(n):
                blk = o_refs[i].at[2 * cx + cy, pl.ds(c * halves[i], halves[i]), :]
                cp = _rcopy(blk, blk, send_sems, recv_sems, j * n + i, sib)
                cp.start()
                sent.append(cp)
        for j, (cx, cy) in enumerate(chips):
            for i in range(n):
                blk = o_refs[i].at[2 * cx + cy, pl.ds((1 - c) * halves[i], halves[i]), :]
                _rcopy(blk, blk, send_sems, recv_sems, j * n + i, sib).wait_recv()
        for cp in sent:
            cp.wait_send()

    return _comm_call(body, "sibling_fill", list(lands), [SDS(t.shape, t.dtype) for t in lands], 3 * n, landing=n)


def _swap_halves(gs, name):
    n = len(gs)
    halves = [g.shape[1] // 2 for g in gs]

    def body(*refs):
        g_refs, o_refs = refs[:n], refs[n:2 * n]
        send_sems, recv_sems = refs[2 * n:]
        x, y, c = _place()
        cps = [_rcopy(g_refs[i].at[:, pl.ds((1 - c) * halves[i], halves[i]), :], o_refs[i], send_sems, recv_sems, i,
                      (x, y, 1 - c)) for i in range(n)]
        for cp in cps:
            cp.start()
        for cp in cps:
            cp.wait()

    return _comm_call(body, name, gs, [SDS((N_CHIPS, h, g.shape[2]), g.dtype) for g, h in zip(gs, halves)], n)


def _chip_exchange(ps, chip):
    n = len(ps)

    def body(*refs):
        p_refs, o_refs = refs[:n], refs[2 * n:3 * n]
        send_sems, recv_sems = refs[3 * n:]
        x, y, c = _place()
        s = 2 * x + y
        chips = _other_chips(x, y)
        sent = []
        for j, (cx, cy) in enumerate(chips):
            for i in range(n):
                cp = _rcopy(p_refs[i].at[2 * cx + cy], o_refs[i].at[s], send_sems, recv_sems, j * n + i, (cx, cy, c))
                cp.start()
                sent.append(cp)
        for j, (cx, cy) in enumerate(chips):
            sj = 2 * cx + cy
            for i in range(n):
                _rcopy(p_refs[i].at[sj], o_refs[i].at[sj], send_sems, recv_sems, j * n + i, (cx, cy, c)).wait_recv()
        for cp in sent:
            cp.wait_send()

    own = [lax.dynamic_index_in_dim(p, chip, axis=0, keepdims=False) for p in ps]
    return _comm_call(body, "chip_exchange", list(ps) + _own_slot(own, chip), [SDS(p.shape, p.dtype) for p in ps],
                      3 * n, landing=n)


def _swap_reduced(rs, name):
    n = len(rs)

    def body(*refs):
        r_refs, o_refs = refs[:n], refs[n:2 * n]
        send_sems, recv_sems = refs[2 * n:]
        x, y, c = _place()
        cps = [_rcopy(r_refs[i], o_refs[i], send_sems, recv_sems, i, (x, y, 1 - c)) for i in range(n)]
        for cp in cps:
            cp.start()
        for cp in cps:
            cp.wait()

    return _comm_call(body, name, rs, [SDS(r.shape, r.dtype) for r in rs], n)


def _all_sum_small(vec, name):
    n_dev = 8
    flips = [(bx, by, bc) for bx in (0, 1) for by in (0, 1) for bc in (0, 1)][1:]

    def body(v_ref, out_ref, gath, send_sems, recv_sems):
        x, y, c = _place()
        me = 4 * x + 2 * y + c
        gath[me] = v_ref[...]
        sent = []
        for k, (bx, by, bc) in enumerate(flips):
            peer = (x ^ bx, y ^ by, c ^ bc)
            cp = _rcopy(v_ref, gath.at[me], send_sems, recv_sems, k, peer)
            cp.start()
            sent.append(cp)
        for k, (bx, by, bc) in enumerate(flips):
            peer = (x ^ bx, y ^ by, c ^ bc)
            _rcopy(v_ref, gath.at[4 * peer[0] + 2 * peer[1] + peer[2]], send_sems, recv_sems, k, peer).wait_recv()
        for cp in sent:
            cp.wait_send()
        acc = gath[0]
        for d in range(1, n_dev):
            acc = acc + gath[d]
        out_ref[...] = acc

    vm = pl.BlockSpec(memory_space=pltpu.VMEM)
    return pl.pallas_call(
        body, name=name, in_specs=[vm], out_specs=vm, out_shape=SDS(vec.shape, F32),
        scratch_shapes=[pltpu.VMEM((n_dev,) + vec.shape, F32), pltpu.SemaphoreType.DMA((7,)),
                        pltpu.SemaphoreType.DMA((7,))],
        compiler_params=_cparams(has_side_effects=True),
    )(vec)


def _pack_small(vals, extra=None):
    parts = [vals[n].reshape(-1).astype(F32) for n, _ in _SMALL]
    parts.append(jnp.zeros((1,), F32) if extra is None else extra.reshape(1).astype(F32))
    flat = jnp.concatenate(parts)
    flat = jnp.concatenate([flat, jnp.zeros((_SMALL_ROWS * 128 - flat.shape[0],), F32)])
    return flat.reshape(_SMALL_ROWS, 128)


def _unpack_small(packed, shapes):
    flat = packed.reshape(-1)
    return {n: flat[_SMALL_OFF[n][0]:_SMALL_OFF[n][0] + _SMALL_OFF[n][1]].reshape(shapes[n]) for n, _ in _SMALL}


def _pair_sum(gs, gots, core, name):
    n = len(gs)

    def kern(c_ref, *refs):
        for i in range(n):
            refs[2 * n + i][...] = (refs[i][...].astype(F32) + refs[n + i][...].astype(F32)).astype(BF16)

    in_specs = [pl.BlockSpec((1, t.shape[1], t.shape[2]), lambda s, c_ref: (s, c_ref[0], 0)) for t in gots]
    in_specs += [pl.BlockSpec((1, t.shape[1], t.shape[2]), lambda s, c_ref: (s, 0, 0)) for t in gots]
    out_specs = [pl.BlockSpec((1, t.shape[1], t.shape[2]), lambda s, c_ref: (s, 0, 0)) for t in gots]
    return pl.pallas_call(
        kern, name=name,
        grid_spec=pltpu.PrefetchScalarGridSpec(num_scalar_prefetch=1, grid=(N_CHIPS,), in_specs=in_specs,
                                               out_specs=out_specs),
        out_shape=[SDS(t.shape, BF16) for t in gots],
        compiler_params=_cparams(dimension_semantics=("arbitrary",)),
    )(core.reshape(1).astype(jnp.int32), *gs, *gots)


def _chip_sum(qs, name):
    n = len(qs)

    def kern(*refs):
        for i in range(n):
            acc = refs[i][0].astype(F32)
            for s in range(1, N_CHIPS):
                acc = acc + refs[i][s].astype(F32)
            refs[n + i][...] = acc

    in_specs = [pl.BlockSpec((N_CHIPS, q.shape[1] // 2, q.shape[2]), lambda j: (0, j, 0)) for q in qs]
    out_specs = [pl.BlockSpec((q.shape[1] // 2, q.shape[2]), lambda j: (j, 0)) for q in qs]
    return _pc(kern, name, (2,), in_specs, out_specs, [SDS(q.shape[1:], F32) for q in qs])(*qs)


def _adam_math(w_, g_, m_, v_):
    m_ = ADAM_B1 * m_ + (1.0 - ADAM_B1) * g_
    v_ = ADAM_B2 * v_ + (1.0 - ADAM_B2) * jnp.square(g_)
    m_hat = m_ / (1.0 - ADAM_B1 ** ADAM_STEP)
    v_hat = v_ / (1.0 - ADAM_B2 ** ADAM_STEP)
    return -ADAM_LR * (m_hat / (jnp.sqrt(v_hat) + ADAM_EPS) + ADAM_WD * w_), m_, v_


def _adamw(w, g, m, v, name):
    rows, cols = w.shape
    tr = rows
    for cand in (256, 128, 64, 32, 16, 8):
        if rows % cand == 0 and rows > cand:
            tr = cand
            break

    def kern(w_ref, g_ref, m_ref, v_ref, d_ref, nm_ref, nv_ref):
        d_ref[...], nm_ref[...], nv_ref[...] = _adam_math(w_ref[...], g_ref[...], m_ref[...], v_ref[...])

    spec = pl.BlockSpec((tr, cols), lambda i: (i, 0))
    return _pc(kern, name, (rows // tr,), [spec] * 4, [spec] * 3, [SDS(w.shape, F32)] * 3)(w, g, m, v)


def _adamw_rows1(w, g, m, v, name):
    rows, _, cols = w.shape
    tr = next(t for t in (42, 32, 29, 16, 8, 7, 6, 4, 3, 2, 1) if rows % t == 0)

    def kern(w_ref, g_ref, m_ref, v_ref, d_ref, nm_ref, nv_ref):
        d_ref[...], nm_ref[...], nv_ref[...] = _adam_math(w_ref[...], g_ref[...], m_ref[...], v_ref[...])

    spec = pl.BlockSpec((tr, 1, cols), lambda i: (i, 0, 0))
    return _pc(kern, name, (rows // tr,), [spec] * 4, [spec] * 3, [SDS(w.shape, F32)] * 3)(w, g, m, v)


def _adamw_big(w, mine, theirs, m, v, core, name):
    _, rows, cols = w.shape
    half = rows // 2
    tr = next(t for t in (256, 176, 128, 64, 32, 16, 8) if half % t == 0)
    nbh = half // tr

    def kern(c_ref, w_ref, a_ref, b_ref, m_ref, v_ref, g_ref, d_ref, nm_ref, nv_ref):
        g_ = jnp.where(pl.program_id(0) // nbh == c_ref[0], a_ref[...], b_ref[...])
        g_ref[0] = g_
        d_ref[0], nm_ref[0], nv_ref[0] = _adam_math(w_ref[0], g_, m_ref[0], v_ref[0])

    full = pl.BlockSpec((1, tr, cols), lambda i, c_ref: (0, i, 0))
    part = pl.BlockSpec((tr, cols), lambda i, c_ref: (i % nbh, 0))
    return pl.pallas_call(
        kern, name=name,
        grid_spec=pltpu.PrefetchScalarGridSpec(num_scalar_prefetch=1, grid=(rows // tr,),
                                               in_specs=[full, part, part, full, full], out_specs=[full] * 4),
        out_shape=[SDS(w.shape, F32)] * 4,
        compiler_params=_cparams(dimension_semantics=("arbitrary",)),
    )(core.reshape(1).astype(jnp.int32), w, mine, theirs, m, v)


_WEIGHT_NAMES = ("attn_norm", "w_in", "dn_conv", "dn_a_log", "dn_dt_bias", "dn_out_norm", "swa_q_norm", "swa_k_norm",
                 "swa_sinks", "rel_bias", "w_branch_dn", "w_branch_swa", "w_out", "ffn_norm", "w_gate", "w_up",
                 "w_down")
_CONV_SH = QKVW // N_CHIPS


def kernel(x, attn_norm, w_in, dn_conv, dn_a_log, dn_dt_bias, dn_out_norm, swa_q_norm, swa_k_norm, swa_sinks, rel_bias, w_branch_dn, w_branch_swa, w_out, ffn_norm, w_gate, w_up, w_down, loss_target, m_attn_norm, m_w_in, m_dn_conv, m_dn_a_log, m_dn_dt_bias, m_dn_out_norm, m_swa_q_norm, m_swa_k_norm, m_swa_sinks, m_rel_bias, m_w_branch_dn, m_w_branch_swa, m_w_out, m_ffn_norm, m_w_gate, m_w_up, m_w_down, v_attn_norm, v_w_in, v_dn_conv, v_dn_a_log, v_dn_dt_bias, v_dn_out_norm, v_swa_q_norm, v_swa_k_norm, v_swa_sinks, v_rel_bias, v_w_branch_dn, v_w_branch_swa, v_w_out, v_ffn_norm, v_w_gate, v_w_up, v_w_down):
    w = dict(attn_norm=attn_norm, w_in=w_in, dn_conv=dn_conv, dn_a_log=dn_a_log, dn_dt_bias=dn_dt_bias,
             dn_out_norm=dn_out_norm, swa_q_norm=swa_q_norm, swa_k_norm=swa_k_norm, swa_sinks=swa_sinks,
             rel_bias=rel_bias, w_branch_dn=w_branch_dn, w_branch_swa=w_branch_swa, w_out=w_out, ffn_norm=ffn_norm,
             w_gate=w_gate, w_up=w_up, w_down=w_down)
    m = dict(attn_norm=m_attn_norm, w_in=m_w_in, dn_conv=m_dn_conv, dn_a_log=m_dn_a_log, dn_dt_bias=m_dn_dt_bias,
             dn_out_norm=m_dn_out_norm, swa_q_norm=m_swa_q_norm, swa_k_norm=m_swa_k_norm, swa_sinks=m_swa_sinks,
             rel_bias=m_rel_bias, w_branch_dn=m_w_branch_dn, w_branch_swa=m_w_branch_swa, w_out=m_w_out,
             ffn_norm=m_ffn_norm, w_gate=m_w_gate, w_up=m_w_up, w_down=m_w_down)
    v = dict(attn_norm=v_attn_norm, w_in=v_w_in, dn_conv=v_dn_conv, dn_a_log=v_dn_a_log, dn_dt_bias=v_dn_dt_bias,
             dn_out_norm=v_dn_out_norm, swa_q_norm=v_swa_q_norm, swa_k_norm=v_swa_k_norm, swa_sinks=v_swa_sinks,
             rel_bias=v_rel_bias, w_branch_dn=v_w_branch_dn, w_branch_swa=v_w_branch_swa, w_out=v_w_out,
             ffn_norm=v_ffn_norm, w_gate=v_w_gate, w_up=v_w_up, w_down=v_w_down)
    shapes = {n: w[n].shape for n in _WEIGHT_NAMES}

    def two_d(a):
        return a.reshape(a.shape[-2], a.shape[-1]) if a.ndim == 3 else a

    core = lax.axis_index("c")
    chip = 2 * lax.axis_index("x") + lax.axis_index("y")
    small_shapes = {n: two_d(w[n]).shape for n, _ in _SMALL}
    small_shapes["dn_conv"] = (CONV, QKVW)

    conv_loc = two_d(w["dn_conv"])
    conv_part = lax.dynamic_update_slice(jnp.zeros((CONV, QKVW), F32), jnp.where(core == 0, conv_loc, 0.0),
                                         (0, chip * _CONV_SH))
    conv_full = _all_sum_small(conv_part.reshape(CONV * QKVW // 128, 128), "gather_conv").reshape(CONV, QKVW)

    flipped = ("w_gate", "w_up")

    def natural(a, n):
        return a.transpose(0, 2, 1) if n in flipped else a

    w_bf = [two_d(natural(w[n], n).astype(BF16)) for n in _BIG_NAMES]
    (w_in_g,) = _gather_weights(w_bf[:1], chip)
    windows = _gather_windows(w_bf[1:])
    after_sync = w_in_g[0, :8, :128].astype(F32) + conv_full[0:1, :128]
    send_sems, recv_sems, w_thru, l_thru, token = _split_start(
        "gather_start", w_bf[1:], _own_slot(w_bf[1:], chip), after_sync, windows)

    def late(after):
        lands = _split_wait("gather_wait", w_thru, l_thru, send_sems, recv_sems, after, windows)
        g = dict(zip(_BIG_NAMES[1:], _sibling_fill(lands)))
        return dict(wa=g["w_branch_dn"], wb=g["w_branch_swa"], w_out=g["w_out"].reshape(D, D), wg=g["w_gate"],
                    wu=g["w_up"], wd=g["w_down"])

    w_in_full = w_in_g.transpose(1, 0, 2).reshape(D, D_IN)
    wts = dict(w_in_p=_pad_w_in(w_in_full), dn_conv=conv_full, late=late)
    for n, _ in _SMALL[:-1]:
        wts[n] = two_d(w[n])
    wts["attn_norm"] = wts["attn_norm"] + token[0:1, 0:1]

    early = {}

    def send_early(grads):
        gs = [grads["w_branch_dn"], grads["w_branch_swa"], grads["w_out"].reshape(N_CHIPS, CSH, D), grads["w_gate"],
              grads["w_up"], grads["w_down"]]
        parts = _pair_sum(gs, _swap_halves(gs, "swap_halves_early"), core, "pair_sum_early")
        own = [lax.dynamic_index_in_dim(p, chip, axis=0, keepdims=False) for p in parts]
        early["sems"], early["recv"], early["src"], early["land"], tok = _split_start(
            "exchange_start", parts, _own_slot(own, chip), parts[0][0, :8, :128], _exchange_windows())
        return tok

    last = {}

    def send_in(g_in_p):
        g_in = [_unpad_w_in(g_in_p).reshape(D, N_CHIPS, D_IN // N_CHIPS).transpose(1, 0, 2)]
        parts = _pair_sum(g_in, _swap_halves(g_in, "swap_halves_in"), core, "pair_sum_in")
        own = [lax.dynamic_index_in_dim(p, chip, axis=0, keepdims=False) for p in parts]
        last["sems"], last["recv"], last["src"], last["land"], tok = _split_start(
            "exchange_in_start", parts, _own_slot(own, chip), parts[0][0, :8, :128], _exchange_windows())
        return tok

    wts["send_early"] = send_early
    wts["send_in"] = send_in
    loss_sum, grad_x, grads = _local_step(x[0], loss_target[0], wts)

    small_sum = _all_sum_small(_pack_small(grads, loss_sum), "all_sum_small")
    loss = small_sum.reshape(-1)[_LOSS_OFF]
    g_small = _unpack_small(small_sum, small_shapes)

    q_early = _split_wait("exchange_wait", early["src"], early["land"], early["sems"], early["recv"], small_sum,
                          _exchange_windows())
    red_early = _chip_sum(list(q_early), "chip_sum_early")
    their_early = _swap_reduced(red_early, "swap_reduced_early")
    g_out, d_out, m_out, v_out = {}, {}, {}, {}
    for n, mine, other in zip(_BIG_NAMES[1:], red_early, their_early):
        res = _adamw_big(natural(w[n], n), mine, other, natural(m[n], n), natural(v[n], n), core, "adamw_" + n)
        g_out[n], d_out[n], m_out[n], v_out[n] = (natural(t, n) for t in res)

    q_in = _split_wait("exchange_in_wait", last["src"], last["land"], last["sems"], last["recv"],
                       d_out[_BIG_NAMES[-1]], _exchange_windows())
    reduced = _chip_sum(list(q_in), "chip_sum_in")
    theirs = _swap_reduced(reduced, "swap_reduced_in")

    def rows1(a):
        return a.transpose(2, 0, 1)

    def unrows1(a):
        return a.transpose(1, 2, 0)

    g_in_blk = jnp.concatenate([jnp.where(core == 0, reduced[0], theirs[0]),
                                jnp.where(core == 0, theirs[0], reduced[0])], axis=0)
    g_in_r = rows1(g_in_blk[None])
    d_, m_, v_ = _adamw_rows1(rows1(w["w_in"]), g_in_r, rows1(m["w_in"]), rows1(v["w_in"]), "adamw_w_in")
    g_out["w_in"], d_out["w_in"], m_out["w_in"], v_out["w_in"] = (unrows1(t) for t in (g_in_r, d_, m_, v_))
    g_conv = lax.dynamic_slice(g_small["dn_conv"], (0, chip * _CONV_SH), (CONV, _CONV_SH))
    g_out["dn_conv"] = g_conv.reshape(shapes["dn_conv"])
    d_, m_, v_ = _adamw(conv_loc, g_conv, two_d(m["dn_conv"]), two_d(v["dn_conv"]), "adamw_dn_conv")
    d_out["dn_conv"], m_out["dn_conv"], v_out["dn_conv"] = (t.reshape(shapes["dn_conv"]) for t in (d_, m_, v_))

    def packed(src):
        vals = {n: src[n] for n, _ in _SMALL[:-1]}
        vals["dn_conv"] = jnp.zeros((CONV * QKVW,), F32)
        return _pack_small(vals)

    d_s, m_s, v_s = _adamw(packed(w), small_sum, packed(m), packed(v), "adamw_small")
    d_small, m_small, v_small = (_unpack_small(t, small_shapes) for t in (d_s, m_s, v_s))
    for n, _ in _SMALL[:-1]:
        g_out[n] = g_small[n].reshape(shapes[n])
        d_out[n], m_out[n], v_out[n] = (t[n].reshape(shapes[n]) for t in (d_small, m_small, v_small))

    return (loss, grad_x[None], *[g_out[n] for n in _WEIGHT_NAMES], *[d_out[n] for n in _WEIGHT_NAMES],
            *[m_out[n] for n in _WEIGHT_NAMES], *[v_out[n] for n in _WEIGHT_NAMES])
```

```python
import functools
import math

import numpy as np
import jax
import jax.numpy as jnp
from jax import lax
from jax.experimental import pallas as pl
from jax.experimental.pallas import tpu as pltpu

F32 = jnp.float32
BF16 = jnp.bfloat16
SDS = jax.ShapeDtypeStruct

D = 1024
DN_H = 4
DH = 128
DNW = DN_H * DH
QKVW = 3 * DNW
CONV = 4
CHUNK = 64
SWA_H = 8
SWA_KV = 2
SWA_G = SWA_H // SWA_KV
SWA_D = 64
SWAW = SWA_H * SWA_D
SWAKW = SWA_KV * SWA_D
BLK = 128
NBUCKET = 32
MAXDIST = 128
DFF = 2816
D_IN = QKVW + DNW + 2 * DN_H + SWAW + 2 * SWAKW + 2 * D
EPS = 1e-6
NEG = -1e30

ADAM_LR = 0.001
ADAM_B1 = 0.9
ADAM_B2 = 0.999
ADAM_EPS = 1e-08
ADAM_WD = 0.01
ADAM_STEP = 10

C_QKV, C_Z, C_GATE, C_SQ, C_SK, C_SV, C_BA = 0, 1536, 2048, 4096, 4608, 4736, 4864
PW = 5120
_ORIG_PIECES = (
    (0, QKVW, C_QKV),
    (QKVW, DNW, C_Z),
    (QKVW + DNW, 2 * DN_H, C_BA),
    (QKVW + DNW + 2 * DN_H, SWAW, C_SQ),
    (QKVW + DNW + 2 * DN_H + SWAW, SWAKW, C_SK),
    (QKVW + DNW + 2 * DN_H + SWAW + SWAKW, SWAKW, C_SV),
    (QKVW + DNW + 2 * DN_H + SWAW + 2 * SWAKW, 2 * D, C_GATE),
)

N_CHIPS = 4
FSH = DFF // N_CHIPS
CSH = D // N_CHIPS
VMEM_LIMIT = 48 * 1024 * 1024
MESH = pl.DeviceIdType.MESH

_BIG = (
    ("w_in", D, D_IN // N_CHIPS),
    ("w_branch_dn", DNW, CSH),
    ("w_branch_swa", SWAW, CSH),
    ("w_out", CSH, D),
    ("w_gate", FSH, D),
    ("w_up", FSH, D),
    ("w_down", FSH, D),
)
_BIG_NAMES = tuple(n for n, _, _ in _BIG)

_SMALL = (
    ("attn_norm", D), ("ffn_norm", D), ("dn_out_norm", DH), ("swa_q_norm", SWA_D), ("swa_k_norm", SWA_D),
    ("swa_sinks", SWA_H), ("dn_a_log", DN_H), ("dn_dt_bias", DN_H), ("rel_bias", NBUCKET * SWA_H),
    ("dn_conv", CONV * QKVW),
)
_SMALL_OFF = {}
_o = 0
for _n, _s in _SMALL:
    _SMALL_OFF[_n] = (_o, _s)
    _o += _s
_LOSS_OFF = _o
_SMALL_ROWS = -(-(_o + 1) // (8 * 128)) * 8


def _cparams(**kw):
    return pltpu.CompilerParams(vmem_limit_bytes=VMEM_LIMIT, **kw)


_DIMS = {
    "nn": (((1,), (0,)), ((), ())),
    "nt": (((1,), (1,)), ((), ())),
    "tn": (((0,), (0,)), ((), ())),
    "bnn": (((2,), (1,)), ((0,), (0,))),
    "bnt": (((2,), (2,)), ((0,), (0,))),
    "btn": (((1,), (1,)), ((0,), (0,))),
}


def _raw_dot(a, b, kind, exact):
    if exact:
        prec = lax.Precision.HIGH if exact == "x3" else lax.Precision.HIGHEST
        return lax.dot_general(a, b, _DIMS[kind], precision=prec, preferred_element_type=F32)
    return lax.dot_general(a.astype(BF16), b.astype(BF16), _DIMS[kind], preferred_element_type=F32)


@functools.partial(jax.custom_vjp, nondiff_argnums=(2, 3))
def _dot(a, b, kind, exact):
    return _raw_dot(a, b, kind, exact)


def _dot_fwd(a, b, kind, exact):
    return _raw_dot(a, b, kind, exact), (a, b)


def _dot_bwd(kind, exact, res, g):
    a, b = res
    pre = kind[:-2]
    nn, nt, tn = pre + "nn", pre + "nt", pre + "tn"
    if kind == nn:
        return _dot(g, b, nt, exact), _dot(a, g, tn, exact)
    if kind == nt:
        return _dot(g, b, nn, exact), _dot(g, a, tn, exact)
    return _dot(b, g, nt, exact), _dot(a, g, nn, exact)


_dot.defvjp(_dot_fwd, _dot_bwd)


def _silu(x):
    return x * jax.nn.sigmoid(x)


def _f_rms(x, gain):
    return x * lax.rsqrt(jnp.mean(x * x, axis=-1, keepdims=True) + EPS) * gain


def _f_dn_pre(xs0, xs1, xs2, xs3, ba, cw, alog, dtb):
    rows = xs0.shape[0]
    c = xs0 * cw[0:1] + xs1 * cw[1:2] + xs2 * cw[2:3] + xs3 * cw[3:4]
    qkv = _silu(c)
    qs, ks, bbs, gbs = [], [], [], []
    for h in range(DN_H):
        qh = qkv[:, h * DH:(h + 1) * DH]
        kh = qkv[:, DNW + h * DH:DNW + (h + 1) * DH]
        qs.append(qh * lax.rsqrt(jnp.sum(qh * qh, axis=-1, keepdims=True) + EPS) * (DH ** -0.5))
        ks.append(kh * lax.rsqrt(jnp.sum(kh * kh, axis=-1, keepdims=True) + EPS))
        beta = jax.nn.sigmoid(ba[:, h:h + 1])
        ar = ba[:, DN_H + h:DN_H + h + 1] + dtb[:, h:h + 1]
        softplus = jnp.maximum(ar, 0.0) + jnp.log1p(jnp.exp(-jnp.abs(ar)))
        g = -jnp.exp(alog[:, h:h + 1]) * softplus
        bbs.append(jnp.broadcast_to(beta, (rows, DH)))
        gbs.append(jnp.broadcast_to(g, (rows, DH)))
    return (jnp.concatenate(qs, axis=1), jnp.concatenate(ks, axis=1), qkv[:, 2 * DNW:],
            jnp.concatenate(bbs, axis=1), jnp.concatenate(gbs, axis=1))


def _f_dn_post(o, z, gain):
    ys = []
    for h in range(DN_H):
        oh = o[:, h * DH:(h + 1) * DH]
        zh = z[:, h * DH:(h + 1) * DH]
        ys.append(oh * lax.rsqrt(jnp.mean(oh * oh, axis=-1, keepdims=True) + EPS) * gain * _silu(zh))
    return jnp.concatenate(ys, axis=1)


def _f_merge(pa, pb, ga, gb):
    return jax.nn.sigmoid(ga) * pa + jax.nn.sigmoid(gb) * pb


def _f_swiglu(g, u):
    return _silu(g) * u


@jax.custom_vjp
def _unit_lower_inverse(a):
    c = a.shape[-1]
    eye = (lax.broadcasted_iota(jnp.int32, a.shape, 1) == lax.broadcasted_iota(jnp.int32, a.shape, 2)).astype(F32)
    p = -a
    t = eye + p
    for _ in range(max(c.bit_length() - 2, 0)):
        p = _raw_dot(p, p, "bnn", "x3")
        t = t + _raw_dot(t, p, "bnn", "x3")
    return t


def _unit_lower_inverse_fwd(a):
    t = _unit_lower_inverse(a)
    return t, t


def _unit_lower_inverse_bwd(t, g):
    return (-_raw_dot(_raw_dot(t, g, "btn", "x3"), t, "bnt", "x3"),)


_unit_lower_inverse.defvjp(_unit_lower_inverse_fwd, _unit_lower_inverse_bwd)


@jax.custom_vjp
def _known_inverse(a, t):
    return t


def _known_inverse_fwd(a, t):
    return t, t


def _known_inverse_bwd(t, g):
    return _unit_lower_inverse_bwd(t, g)[0], jnp.zeros_like(t)


_known_inverse.defvjp(_known_inverse_fwd, _known_inverse_bwd)


def _f_chunk(q, k, v, gb, bb, s, t_known=None, with_t=False):
    c = CHUNK
    nh = q.shape[0]
    ii = lax.broadcasted_iota(jnp.int32, (nh, c, c), 1)
    jj = lax.broadcasted_iota(jnp.int32, (nh, c, c), 2)
    incl = ii >= jj
    strict = ii > jj
    eye = (ii == jj).astype(F32)
    gcb = _dot(incl.astype(F32), gb, "bnn", True)
    lane0 = (lax.broadcasted_iota(jnp.int32, (nh, c, DH), 2) == 0).astype(F32)
    gcol = gcb[:, :, :c]
    grow = _dot(lane0, gcb, "bnt", True)
    decay = jnp.where(incl, jnp.exp(jnp.where(incl, gcol - grow, 0.0)), 0.0)
    kb = k * bb
    vb = v * bb
    a = jnp.where(strict, _dot(kb, k, "bnt", False) * decay, 0.0)
    t = _unit_lower_inverse(a) if t_known is None else _known_inverse(a, t_known)
    eg = jnp.exp(gcb)
    u = _dot(t, vb, "bnn", "x3")
    w = _dot(t, kb * eg, "bnn", "x3")
    qk = jnp.where(incl, _dot(q, k, "bnt", False) * decay, 0.0)
    qe = q * eg
    glast = gcb[:, c - 1:c, :]
    k_dec = k * jnp.exp(glast - gcb)
    e_last = jnp.exp(glast)
    outs = []
    for g in range(nh // DN_H):
        sl = slice(g * DN_H, (g + 1) * DN_H)
        v_new = u[sl] - _dot(w[sl], s, "bnn", False)
        outs.append(_dot(qe[sl], s, "bnn", False) + _dot(qk[sl], v_new, "bnn", False))
        s = s * e_last[sl] + _dot(k_dec[sl], v_new, "btn", False)
    o = jnp.concatenate(outs, axis=0)
    return (o, s, t) if with_t else (o, s)


def _f_swa(q8, kp, kc, vp, vc, bias8, qg, kg, sink, mask):
    kb = jnp.concatenate([kp, kc], axis=1)
    vb = jnp.concatenate([vp, vc], axis=1)
    kn = kb * lax.rsqrt(jnp.mean(kb * kb, axis=-1, keepdims=True) + EPS) * kg

    def rows(per_head):
        return jnp.stack([jnp.concatenate([per_head(kv, g) for g in range(SWA_G)], axis=0)
                          for kv in range(SWA_KV)], axis=0)

    qq = rows(lambda kv, g: q8[kv * SWA_G + g])
    qn = qq * lax.rsqrt(jnp.mean(qq * qq, axis=-1, keepdims=True) + EPS) * qg
    lg = _dot(qn, kn, "bnt", False) * (SWA_D ** -0.5) + rows(lambda kv, g: bias8[kv * SWA_G + g])
    lg = jnp.where(rows(lambda kv, g: mask), lg, NEG)
    sk = rows(lambda kv, g: jnp.broadcast_to(sink[kv][:, g:g + 1], (BLK, 1)))
    m = lax.stop_gradient(jnp.maximum(jnp.max(lg, axis=-1, keepdims=True), sk))
    p = jnp.exp(lg - m)
    den = jnp.sum(p, axis=-1, keepdims=True) + jnp.exp(sk - m)
    out = _dot(p / den, vb, "bnn", False)
    return jnp.stack([out[kv, g * BLK:(g + 1) * BLK] for kv in range(SWA_KV) for g in range(SWA_G)], axis=0)


def _bdot(a, b, kind="nn"):
    return lax.dot_general(a.astype(BF16), b.astype(BF16), _DIMS[kind], preferred_element_type=F32)


def _pc(kern, name, grid, in_specs, out_specs, out_shape, scratch=()):
    return pl.pallas_call(
        kern, name=name, grid=grid, in_specs=in_specs, out_specs=out_specs, out_shape=out_shape,
        scratch_shapes=list(scratch), compiler_params=_cparams(dimension_semantics=("arbitrary",) * len(grid)))


def _mm(a, b, kind, out_dtype, tm, tn, name):
    if kind == "tn":
        k, m = a.shape
    else:
        m, k = a.shape
    n = b.shape[0] if kind == "nt" else b.shape[1]
    tm, tn = min(tm, m), min(tn, n)
    assert m % tm == 0 and n % tn == 0, (name, a.shape, b.shape, tm, tn)

    def kern(a_ref, b_ref, o_ref):
        o_ref[...] = _bdot(a_ref[...], b_ref[...], kind).astype(o_ref.dtype)

    a_spec = pl.BlockSpec((k, tm), lambda i, j: (0, i)) if kind == "tn" else pl.BlockSpec((tm, k), lambda i, j: (i, 0))
    b_spec = pl.BlockSpec((tn, k), lambda i, j: (j, 0)) if kind == "nt" else pl.BlockSpec((k, tn), lambda i, j: (0, j))
    return _pc(kern, name, (m // tm, n // tn), [a_spec, b_spec], pl.BlockSpec((tm, tn), lambda i, j: (i, j)),
               SDS((m, n), out_dtype))(a, b)


def _rows(body, name, m, tm, row_ins, full_ins, row_outs, acc_outs=()):
    n_r, n_f, n_o, n_a = len(row_ins), len(full_ins), len(row_outs), len(acc_outs)
    assert m % tm == 0

    def kern(*refs):
        r = refs[:n_r]
        f = refs[n_r:n_r + n_f]
        o = refs[n_r + n_f:n_r + n_f + n_o]
        acc = refs[n_r + n_f + n_o:]
        outs, sums = body([x[...] for x in r], [x[...] for x in f])
        for ref, val in zip(o, outs, strict=True):
            ref[...] = val.astype(ref.dtype)
        if n_a:
            @pl.when(pl.program_id(0) == 0)
            def _():
                for ref in acc:
                    ref[...] = jnp.zeros(ref.shape, F32)

            for ref, val in zip(acc, sums, strict=True):
                ref[...] += val

    in_specs = [pl.BlockSpec((tm, w), functools.partial(lambda i, cb: (i, cb), cb=cb)) for _, w, cb in row_ins]
    in_specs += [pl.BlockSpec(x.shape, lambda i: (0, 0)) for x in full_ins]
    out_specs = [pl.BlockSpec((tm, w), lambda i: (i, 0)) for w, _ in row_outs]
    out_specs += [pl.BlockSpec(s, lambda i: (0, 0)) for s in acc_outs]
    out_shape = [SDS((m, w), dt) for w, dt in row_outs]
    out_shape += [SDS(s, F32) for s in acc_outs]
    return _pc(kern, name, (m // tm,), in_specs, out_specs, out_shape)(*[x for x, _, _ in row_ins], *full_ins)


def _whole(x):
    return (x, x.shape[1], 0)


def _zero_first(refs):
    @pl.when(pl.program_id(0) == 0)
    def _():
        for ref in refs:
            ref[...] = jnp.zeros(ref.shape, F32)


GROUP = 4


def _heads(ref):
    return jnp.stack([ref[g * CHUNK:(g + 1) * CHUNK, h * DH:(h + 1) * DH]
                      for g in range(GROUP) for h in range(DN_H)], axis=0)


def _unheads(ref, val):
    for g in range(GROUP):
        for h in range(DN_H):
            ref[g * CHUNK:(g + 1) * CHUNK, h * DH:(h + 1) * DH] = val[g * DN_H + h]


def _dn_chunks_fwd(q, k, v, gb, bb):
    s_len = q.shape[0]
    ng = s_len // (GROUP * CHUNK)

    def kern(q_ref, k_ref, v_ref, g_ref, b_ref, o_ref, sall_ref, t_ref, state):
        _zero_first([state])
        s = state[...]
        sall_ref[0] = s
        o, s_new, t = _f_chunk(*[_heads(r) for r in (q_ref, k_ref, v_ref, g_ref, b_ref)], s, with_t=True)
        _unheads(o_ref, o)
        t_ref[0] = t
        state[...] = s_new

    blk = pl.BlockSpec((GROUP * CHUNK, DNW), lambda c: (c, 0))
    return _pc(kern, "dn_chunks_fwd", (ng,), [blk] * 5,
               [blk, pl.BlockSpec((1, DN_H, DH, DH), lambda c: (c, 0, 0, 0)),
                pl.BlockSpec((1, GROUP * DN_H, CHUNK, CHUNK), lambda c: (c, 0, 0, 0))],
               [SDS((s_len, DNW), F32), SDS((ng, DN_H, DH, DH), F32), SDS((ng, GROUP * DN_H, CHUNK, CHUNK), F32)],
               scratch=[pltpu.VMEM((DN_H, DH, DH), F32)])(q, k, v, gb, bb)


def _dn_chunks_bwd(q, k, v, gb, bb, s_all, t_all, d_o):
    s_len = q.shape[0]
    ng = s_len // (GROUP * CHUNK)

    def kern(q_ref, k_ref, v_ref, g_ref, b_ref, sall_ref, t_ref, do_ref, dq_ref, dk_ref, dv_ref, dg_ref, db_ref,
             dstate):
        _zero_first([dstate])
        fn = functools.partial(_f_chunk, t_known=t_ref[0])
        _, vjp = jax.vjp(fn, *[_heads(r) for r in (q_ref, k_ref, v_ref, g_ref, b_ref)], sall_ref[0])
        *d_ins, ds = vjp((_heads(do_ref), dstate[...]))
        for ref, val in zip((dq_ref, dk_ref, dv_ref, dg_ref, db_ref), d_ins, strict=True):
            _unheads(ref, val)
        dstate[...] = ds

    blk = pl.BlockSpec((GROUP * CHUNK, DNW), lambda c: (ng - 1 - c, 0))
    return _pc(kern, "dn_chunks_bwd", (ng,),
               [blk] * 5 + [pl.BlockSpec((1, DN_H, DH, DH), lambda c: (ng - 1 - c, 0, 0, 0)),
                            pl.BlockSpec((1, GROUP * DN_H, CHUNK, CHUNK), lambda c: (ng - 1 - c, 0, 0, 0)), blk],
               [blk] * 5, [SDS((s_len, DNW), F32)] * 5,
               scratch=[pltpu.VMEM((DN_H, DH, DH), F32)])(q, k, v, gb, bb, s_all, t_all, d_o)


def _t5_bucket_table():
    qi = np.arange(BLK)[:, None]
    kj = np.arange(2 * BLK)[None, :]
    dist = BLK + qi - kj
    n = np.maximum(dist, 0)
    max_exact = NBUCKET // 2
    nf = np.maximum(n, 1).astype(np.float32)
    large = max_exact + (np.log(nf / np.float32(max_exact)) / np.float32(math.log(MAXDIST / max_exact))
                         * np.float32(NBUCKET - max_exact)).astype(np.int32)
    large = np.minimum(large, NBUCKET - 1)
    return np.where(n < max_exact, n, large)


def _bucket_onehot_t():
    table = _t5_bucket_table().reshape(-1)
    return (np.arange(NBUCKET)[:, None] == table[None, :]).astype(np.float32)


def _swa_mask(first):
    qi = lax.broadcasted_iota(jnp.int32, (BLK, 2 * BLK), 0)
    kj = lax.broadcasted_iota(jnp.int32, (BLK, 2 * BLK), 1)
    dist = BLK + qi - kj
    window = (dist >= 0) & (dist < BLK)
    return window & ((kj >= BLK) | jnp.logical_not(first))


def _bias_expand(rel_bias_t):
    onehot = jnp.asarray(_bucket_onehot_t())

    def kern(r_ref, oh_ref, o_ref):
        o_ref[...] = _raw_dot(r_ref[...], oh_ref[...], "nn", True)

    return pl.pallas_call(
        kern, name="bias_expand", out_shape=SDS((SWA_H, BLK * 2 * BLK), F32), compiler_params=_cparams(),
    )(rel_bias_t, onehot)


def _bias_reduce(d_bias_flat):
    onehot = jnp.asarray(_bucket_onehot_t())

    def kern(d_ref, oh_ref, o_ref):
        o_ref[...] = _raw_dot(d_ref[...], oh_ref[...], "nt", True)

    return pl.pallas_call(
        kern, name="bias_reduce", out_shape=SDS((SWA_H, NBUCKET), F32), compiler_params=_cparams(),
    )(d_bias_flat, onehot)


def _swa_specs(nb, rev):
    def blk(n):
        return (nb - 1 - n) if rev else n

    def before(n):
        return jnp.maximum(blk(n) - 1, 0)

    q_spec = pl.BlockSpec((BLK, SWAW), lambda n: (blk(n), C_SQ // SWAW))
    k_cur = pl.BlockSpec((BLK, SWAKW), lambda n: (blk(n), C_SK // SWAKW))
    k_prev = pl.BlockSpec((BLK, SWAKW), lambda n: (before(n), C_SK // SWAKW))
    v_cur = pl.BlockSpec((BLK, SWAKW), lambda n: (blk(n), C_SV // SWAKW))
    v_prev = pl.BlockSpec((BLK, SWAKW), lambda n: (before(n), C_SV // SWAKW))
    bias = pl.BlockSpec((SWA_H, BLK, 2 * BLK), lambda n: (0, 0, 0))
    gain = pl.BlockSpec((1, SWA_D), lambda n: (0, 0))
    sink = pl.BlockSpec((SWA_KV, 1, SWA_G), lambda n: (0, 0, 0))
    wide = pl.BlockSpec((BLK, SWAW), lambda n: (blk(n), 0))
    narrow = pl.BlockSpec((BLK, SWAKW), lambda n: (blk(n), 0))
    return [q_spec, k_prev, k_cur, v_prev, v_cur, bias, gain, gain, sink], wide, narrow


def _split_heads(x):
    return jnp.stack([x[:, h * SWA_D:(h + 1) * SWA_D] for h in range(x.shape[1] // SWA_D)], axis=0)


def _join_heads(x):
    return jnp.concatenate([x[h] for h in range(x.shape[0])], axis=1)


def _swa_fwd(proj, bias, qg, kg, sinks):
    s_len = proj.shape[0]
    nb = s_len // BLK
    in_specs, wide, _ = _swa_specs(nb, False)

    def kern(q_ref, kp_ref, kc_ref, vp_ref, vc_ref, b_ref, qg_ref, kg_ref, s_ref, o_ref):
        mask = _swa_mask(pl.program_id(0) == 0)
        o8 = _f_swa(*[_split_heads(r[...]) for r in (q_ref, kp_ref, kc_ref, vp_ref, vc_ref)], b_ref[...], qg_ref[...],
                    kg_ref[...], s_ref[...], mask)
        o_ref[...] = _join_heads(o8).astype(BF16)

    return _pc(kern, "swa_fwd", (nb,), in_specs, wide, SDS((s_len, SWAW), BF16))(
        proj, proj, proj, proj, proj, bias, qg, kg, sinks)


def _swa_bwd(proj, bias, qg, kg, sinks, d_out):
    s_len = proj.shape[0]
    nb = s_len // BLK
    in_specs, wide, narrow = _swa_specs(nb, True)

    def kern(q_ref, kp_ref, kc_ref, vp_ref, vc_ref, b_ref, qg_ref, kg_ref, s_ref, do_ref,
             dq_ref, dk_ref, dv_ref, db_ref, dqg_ref, dkg_ref, ds_ref, carry_k, carry_v):
        n = pl.program_id(0)
        mask = _swa_mask(n == nb - 1)
        _zero_first([carry_k, carry_v, db_ref, ds_ref, dqg_ref, dkg_ref])
        fn = functools.partial(_f_swa, mask=mask)
        _, vjp = jax.vjp(fn, *[_split_heads(r[...]) for r in (q_ref, kp_ref, kc_ref, vp_ref, vc_ref)], b_ref[...],
                         qg_ref[...], kg_ref[...], s_ref[...])
        dq, dkp, dkc, dvp, dvc, dbias, dqg, dkg, dsink = vjp(_split_heads(do_ref[...]))
        dq_ref[...] = _join_heads(dq).astype(BF16)
        dk_ref[...] = (_join_heads(dkc) + carry_k[...]).astype(BF16)
        dv_ref[...] = (_join_heads(dvc) + carry_v[...]).astype(BF16)
        carry_k[...] = _join_heads(dkp)
        carry_v[...] = _join_heads(dvp)
        db_ref[...] += dbias
        dqg_ref[...] += dqg
        dkg_ref[...] += dkg
        ds_ref[...] += dsink

    bias_spec, gain, sink = in_specs[5], in_specs[6], in_specs[8]
    return _pc(
        kern, "swa_bwd", (nb,), in_specs + [wide], [wide, narrow, narrow, bias_spec, gain, gain, sink],
        [SDS((s_len, SWAW), BF16), SDS((s_len, SWAKW), BF16), SDS((s_len, SWAKW), BF16),
         SDS((SWA_H, BLK, 2 * BLK), F32), SDS((1, SWA_D), F32), SDS((1, SWA_D), F32), SDS((SWA_KV, 1, SWA_G), F32)],
        scratch=[pltpu.VMEM((BLK, SWAKW), F32), pltpu.VMEM((BLK, SWAKW), F32)],
    )(proj, proj, proj, proj, proj, bias, qg, kg, sinks, d_out)


def _branch_merge(y_dn, y_swa, wa, wb, proj):
    s_len = y_dn.shape[0]
    tm = min(512, s_len)

    def kern(ya_ref, yb_ref, wa_ref, wb_ref, ga_ref, gb_ref, pa_ref, pb_ref, m_ref):
        pa = _bdot(ya_ref[...], wa_ref[0])
        pb = _bdot(yb_ref[...], wb_ref[0])
        pa_ref[...] = pa.astype(BF16)
        pb_ref[...] = pb.astype(BF16)
        m_ref[...] = _f_merge(pa, pb, ga_ref[...], gb_ref[...]).astype(BF16)

    y_spec = pl.BlockSpec((tm, DNW), lambda i, s: (i, 0))
    w_spec = pl.BlockSpec((1, DNW, CSH), lambda i, s: (s, 0, 0))
    o_spec = pl.BlockSpec((tm, CSH), lambda i, s: (i, s))
    ga_spec = pl.BlockSpec((tm, CSH), lambda i, s: (i, C_GATE // CSH + s))
    gb_spec = pl.BlockSpec((tm, CSH), lambda i, s: (i, (C_GATE + D) // CSH + s))
    return _pc(kern, "branch_merge", (s_len // tm, N_CHIPS), [y_spec, y_spec, w_spec, w_spec, ga_spec, gb_spec],
               [o_spec] * 3, [SDS((s_len, D), BF16)] * 3,
               )(y_dn, y_swa, wa, wb, proj, proj)


def _out_proj(merged, w_out, x, gain):
    s_len = x.shape[0]
    tm = min(256, s_len)

    def kern(m_ref, w_ref, x_ref, g_ref, x1_ref, h2_ref):
        x1 = x_ref[...] + _bdot(m_ref[...], w_ref[...])
        x1_ref[...] = x1
        h2_ref[...] = _f_rms(x1, g_ref[...]).astype(BF16)

    row = pl.BlockSpec((tm, D), lambda i: (i, 0))
    return _pc(kern, "out_proj", (s_len // tm,),
               [row, pl.BlockSpec((D, D), lambda i: (0, 0)), row, pl.BlockSpec((1, D), lambda i: (0, 0))],
               [row, row], [SDS((s_len, D), F32), SDS((s_len, D), BF16)])(merged, w_out, x, gain)


def _ffn_up(h2, wg, wu):
    s_len = h2.shape[0]
    tm = min(512, s_len)

    def kern(h_ref, g_ref, u_ref, gt_ref, up_ref, act_ref):
        h = h_ref[...]
        g = _bdot(h, g_ref[0], "nt")
        u = _bdot(h, u_ref[0], "nt")
        gt_ref[0] = g.astype(BF16)
        up_ref[0] = u.astype(BF16)
        act_ref[0] = _f_swiglu(g, u).astype(BF16)

    w_spec = pl.BlockSpec((1, FSH, D), lambda s, i: (s, 0, 0))
    o_spec = pl.BlockSpec((1, tm, FSH), lambda s, i: (s, i, 0))
    shape = (N_CHIPS, s_len, FSH)
    return _pc(kern, "ffn_up", (N_CHIPS, s_len // tm), [pl.BlockSpec((tm, D), lambda s, i: (i, 0)), w_spec, w_spec],
               [o_spec] * 3, [SDS(shape, BF16)] * 3)(h2, wg, wu)


def _ffn_down_loss(act, wd, x1, target):
    s_len = x1.shape[0]
    tm = min(256, s_len)

    def kern(a_ref, w_ref, x_ref, t_ref, dy_ref, dyb_ref, loss_ref):
        _zero_first([loss_ref])
        y = x_ref[...]
        for s in range(N_CHIPS):
            y = y + _bdot(a_ref[s], w_ref[s])
        d = y - t_ref[...]
        dy = d * (1.0 / D)
        dy_ref[...] = dy
        dyb_ref[...] = dy.astype(BF16)
        loss_ref[...] += jnp.sum(d * d).reshape(1, 1) * (0.5 / D)

    row = pl.BlockSpec((tm, D), lambda i: (i, 0))
    return _pc(kern, "ffn_down_loss", (s_len // tm,),
               [pl.BlockSpec((N_CHIPS, tm, FSH), lambda i: (0, i, 0)),
                pl.BlockSpec((N_CHIPS, FSH, D), lambda i: (0, 0, 0)), row, row],
               [row, row, pl.BlockSpec((1, 1), lambda i: (0, 0))],
               [SDS((s_len, D), F32), SDS((s_len, D), BF16), SDS((1, 1), F32)])(act, wd, x1, target)


def _ffn_dact(dy_b, wd, gt, up):
    s_len = dy_b.shape[0]
    tm = min(512, s_len)

    def kern(dy_ref, w_ref, gt_ref, up_ref, dg_ref, du_ref):
        d_act = _bdot(dy_ref[...], w_ref[0], "nt")
        _, vjp = jax.vjp(_f_swiglu, gt_ref[0].astype(F32), up_ref[0].astype(F32))
        dg, du = vjp(d_act)
        dg_ref[0] = dg.astype(BF16)
        du_ref[0] = du.astype(BF16)

    a_spec = pl.BlockSpec((1, tm, FSH), lambda s, i: (s, i, 0))
    shape = (N_CHIPS, s_len, FSH)
    return _pc(kern, "ffn_dact", (N_CHIPS, s_len // tm),
               [pl.BlockSpec((tm, D), lambda s, i: (i, 0)), pl.BlockSpec((1, FSH, D), lambda s, i: (s, 0, 0)),
                a_spec, a_spec],
               [a_spec, a_spec], [SDS(shape, BF16), SDS(shape, BF16)])(dy_b, wd, gt, up)


def _gw_ffn(lhs, rhs, name):
    s_len = rhs.shape[0]
    n = len(lhs)
    tn = 512

    def kern(*refs):
        g = refs[n][...]
        for i in range(n):
            refs[n + 1 + i][0] = _bdot(refs[i][0], g, "tn").astype(BF16)

    a_spec = pl.BlockSpec((1, s_len, FSH), lambda s, j: (s, 0, 0))
    o_spec = pl.BlockSpec((1, FSH, tn), lambda s, j: (s, 0, j))
    return _pc(kern, name, (N_CHIPS, D // tn), [a_spec] * n + [pl.BlockSpec((s_len, tn), lambda s, j: (0, j))],
               [o_spec] * n, [SDS((N_CHIPS, FSH, D), BF16)] * n)(*lhs, rhs)


def _ffn_dh2(d_gt, d_up, wg, wu, x1, dy, gain):
    s_len = x1.shape[0]
    tm = min(256, s_len)

    def kern(dg_ref, du_ref, wg_ref, wu_ref, x_ref, dy_ref, g_ref, dx_ref, dxb_ref, dgain_ref):
        _zero_first([dgain_ref])
        dh2 = jnp.zeros((tm, D), F32)
        for s in range(N_CHIPS):
            dh2 = dh2 + _bdot(dg_ref[s], wg_ref[s]) + _bdot(du_ref[s], wu_ref[s])
        _, vjp = jax.vjp(_f_rms, x_ref[...], g_ref[...])
        dx, dgain = vjp(dh2)
        dx1 = dx + dy_ref[...]
        dx_ref[...] = dx1
        dxb_ref[...] = dx1.astype(BF16)
        dgain_ref[...] += dgain

    row = pl.BlockSpec((tm, D), lambda i: (i, 0))
    d_spec = pl.BlockSpec((N_CHIPS, tm, FSH), lambda i: (0, i, 0))
    w_spec = pl.BlockSpec((N_CHIPS, FSH, D), lambda i: (0, 0, 0))
    vec = pl.BlockSpec((1, D), lambda i: (0, 0))
    return _pc(kern, "ffn_dh2", (s_len // tm,), [d_spec, d_spec, w_spec, w_spec, row, row, vec],
               [row, row, vec], [SDS((s_len, D), F32), SDS((s_len, D), BF16), SDS((1, D), F32)],
               )(d_gt, d_up, wg, wu, x1, dy, gain)


def _merge_bwd(dx1_b, w_out, pa, pb, proj):
    s_len = dx1_b.shape[0]
    tm = min(256, s_len)

    def kern(dx_ref, w_ref, pa_ref, pb_ref, g_ref, dpa_ref, dpb_ref, dg_ref):
        dm = _bdot(dx_ref[...], w_ref[...], "nt")
        gates = g_ref[...]
        _, vjp = jax.vjp(_f_merge, pa_ref[...].astype(F32), pb_ref[...].astype(F32), gates[:, :D], gates[:, D:])
        dpa, dpb, dga, dgb = vjp(dm)
        dpa_ref[...] = dpa.astype(BF16)
        dpb_ref[...] = dpb.astype(BF16)
        dg_ref[:, :D] = dga.astype(BF16)
        dg_ref[:, D:] = dgb.astype(BF16)

    row = pl.BlockSpec((tm, D), lambda i: (i, 0))
    return _pc(kern, "merge_bwd", (s_len // tm,),
               [row, pl.BlockSpec((D, D), lambda i: (0, 0)), row, row,
                pl.BlockSpec((tm, 2 * D), lambda i: (i, C_GATE // (2 * D)))],
               [row, row, pl.BlockSpec((tm, 2 * D), lambda i: (i, 0))],
               [SDS((s_len, D), BF16), SDS((s_len, D), BF16), SDS((s_len, 2 * D), BF16)],
               )(dx1_b, w_out, pa, pb, proj)


def _d_branch(d_pa, d_pb, wa, wb):
    s_len = d_pa.shape[0]
    tm = min(512, s_len)

    def kern(da_ref, db_ref, wa_ref, wb_ref, oa_ref, ob_ref):
        acc_a = jnp.zeros((tm, DNW), F32)
        acc_b = jnp.zeros((tm, SWAW), F32)
        for s in range(N_CHIPS):
            acc_a = acc_a + _bdot(da_ref[:, s * CSH:(s + 1) * CSH], wa_ref[s], "nt")
            acc_b = acc_b + _bdot(db_ref[:, s * CSH:(s + 1) * CSH], wb_ref[s], "nt")
        oa_ref[...] = acc_a
        ob_ref[...] = acc_b

    row = pl.BlockSpec((tm, D), lambda i: (i, 0))
    w_spec = pl.BlockSpec((N_CHIPS, DNW, CSH), lambda i: (0, 0, 0))
    out = pl.BlockSpec((tm, DNW), lambda i: (i, 0))
    return _pc(kern, "d_branch", (s_len // tm,), [row, row, w_spec, w_spec], [out, out],
               [SDS((s_len, DNW), F32), SDS((s_len, SWAW), F32)])(d_pa, d_pb, wa, wb)


def _gw_branch(y_dn, y_swa, d_pa, d_pb):
    s_len = y_dn.shape[0]

    def kern(ya_ref, yb_ref, da_ref, db_ref, oa_ref, ob_ref):
        oa_ref[0] = _bdot(ya_ref[...], da_ref[...], "tn").astype(BF16)
        ob_ref[0] = _bdot(yb_ref[...], db_ref[...], "tn").astype(BF16)

    y_spec = pl.BlockSpec((s_len, DNW), lambda s: (0, 0))
    d_spec = pl.BlockSpec((s_len, CSH), lambda s: (0, s))
    o_spec = pl.BlockSpec((1, DNW, CSH), lambda s: (s, 0, 0))
    shape = (N_CHIPS, DNW, CSH)
    return _pc(kern, "gw_branch", (N_CHIPS,), [y_spec, y_spec, d_spec, d_spec], [o_spec, o_spec],
               [SDS(shape, BF16), SDS(shape, BF16)])(y_dn, y_swa, d_pa, d_pb)


def _dh_rms(d_proj, w_in_p, x, dx1, gain):
    s_len = x.shape[0]
    tm = min(256, s_len)

    def kern(dp_ref, w_ref, x_ref, r_ref, g_ref, gx_ref, dgain_ref):
        _zero_first([dgain_ref])
        dh = _bdot(dp_ref[...], w_ref[...], "nt")
        _, vjp = jax.vjp(_f_rms, x_ref[...], g_ref[...])
        dx, dgain = vjp(dh)
        gx_ref[...] = dx + r_ref[...]
        dgain_ref[...] += dgain

    row = pl.BlockSpec((tm, D), lambda i: (i, 0))
    vec = pl.BlockSpec((1, D), lambda i: (0, 0))
    return _pc(kern, "dh_rms", (s_len // tm,),
               [pl.BlockSpec((tm, PW), lambda i: (i, 0)), pl.BlockSpec((D, PW), lambda i: (0, 0)), row, row, vec],
               [row, vec], [SDS((s_len, D), F32), SDS((1, D), F32)])(d_proj, w_in_p, x, dx1, gain)


HALO = 8


def _conv_taps(cur_ref, prev_ref, halo, first):
    tm = cur_ref.shape[0]
    halo[0:HALO, :] = jnp.where(first, 0.0, prev_ref[...])
    halo[HALO:, :] = cur_ref[...]
    return [halo[HALO - n:HALO - n + tm, :] for n in range(CONV - 1, 0, -1)] + [cur_ref[...]]


def _dn_pre_specs(s_len, tm, blk):
    cur = pl.BlockSpec((tm, QKVW), lambda i: (blk(i), 0))
    prev = pl.BlockSpec((HALO, QKVW), lambda i: (jnp.maximum(blk(i) * (tm // HALO) - 1, 0), 0))
    ba = pl.BlockSpec((tm, 128), lambda i: (blk(i), C_BA // 128))
    row = pl.BlockSpec((tm, DNW), lambda i: (blk(i), 0))
    full = [pl.BlockSpec((CONV, QKVW), lambda i: (0, 0)), pl.BlockSpec((1, DN_H), lambda i: (0, 0)),
            pl.BlockSpec((1, DN_H), lambda i: (0, 0))]
    return cur, prev, ba, row, full


def _dn_pre_fwd(proj, conv_w, alog, dtb):
    s_len = proj.shape[0]
    tm = min(128, s_len)
    cur, prev, ba, row, full = _dn_pre_specs(s_len, tm, lambda i: i)

    def kern(cur_ref, prev_ref, ba_ref, cw_ref, al_ref, dt_ref, q_ref, k_ref, v_ref, bb_ref, gb_ref, halo):
        xs = _conv_taps(cur_ref, prev_ref, halo, pl.program_id(0) == 0)
        outs = _f_dn_pre(*xs, ba_ref[...], cw_ref[...], al_ref[...], dt_ref[...])
        for ref, val in zip((q_ref, k_ref, v_ref, bb_ref, gb_ref), outs, strict=True):
            ref[...] = val

    return _pc(kern, "dn_pre_fwd", (s_len // tm,), [cur, prev, ba] + full, [row] * 5, [SDS((s_len, DNW), F32)] * 5,
               scratch=[pltpu.VMEM((tm + HALO, QKVW), F32)])(proj, proj, proj, conv_w, alog, dtb)


def _dn_pre_bwd(proj, conv_w, alog, dtb, cots):
    s_len = proj.shape[0]
    tm = min(128, s_len)
    nb = s_len // tm
    cur, prev, ba, row, full = _dn_pre_specs(s_len, tm, lambda i: nb - 1 - i)

    def kern(cur_ref, prev_ref, ba_ref, cw_ref, al_ref, dt_ref, dq_ref, dk_ref, dv_ref, dbb_ref, dgb_ref,
             dqkv_ref, dba_ref, dcw_ref, dal_ref, ddt_ref, halo, *tails):
        i = pl.program_id(0)
        _zero_first([dcw_ref, dal_ref, ddt_ref])

        @pl.when(i == 0)
        def _():
            for t in tails:
                t[tm:, :] = jnp.zeros((HALO, QKVW), F32)

        xs = _conv_taps(cur_ref, prev_ref, halo, i == nb - 1)
        _, vjp = jax.vjp(_f_dn_pre, *xs, ba_ref[...], cw_ref[...], al_ref[...], dt_ref[...])
        *dxs, dba, dcw, dal, ddt = vjp((dq_ref[...], dk_ref[...], dv_ref[...], dbb_ref[...], dgb_ref[...]))
        total = dxs[CONV - 1]
        for j, t in enumerate(tails):
            n = CONV - 1 - j
            t[0:tm, :] = dxs[j]
            total = total + t[n:n + tm, :]
            t[tm:, :] = dxs[j][0:HALO, :]
        dqkv_ref[...] = total.astype(BF16)
        dba_ref[...] = dba.astype(BF16)
        dcw_ref[...] += dcw
        dal_ref[...] += dal
        ddt_ref[...] += ddt

    return _pc(kern, "dn_pre_bwd", (nb,), [cur, prev, ba] + full + [row] * 5,
               [cur, pl.BlockSpec((tm, 128), lambda i: (nb - 1 - i, 0))] + full,
               [SDS((s_len, QKVW), BF16), SDS((s_len, 128), BF16), SDS((CONV, QKVW), F32), SDS((1, DN_H), F32),
                SDS((1, DN_H), F32)],
               scratch=[pltpu.VMEM((tm + HALO, QKVW), F32)] * CONV)(proj, proj, proj, conv_w, alog, dtb, *cots)


def _pad_w_in(w_in):
    pieces = [w_in[:, o0:o0 + w] for o0, w, _ in sorted(_ORIG_PIECES, key=lambda t: t[2])]
    pieces.append(jnp.zeros((w_in.shape[0], PW - D_IN), w_in.dtype))
    return jnp.concatenate(pieces, axis=1)


def _unpad_w_in(g):
    return jnp.concatenate([g[:, p0:p0 + w] for _, w, p0 in _ORIG_PIECES], axis=1)


def _local_step(x, target, wts):
    s_len = x.shape[0]
    tm = min(256, s_len)
    w_in_p = wts["w_in_p"]
    attn_gain = wts["attn_norm"]
    ffn_gain = wts["ffn_norm"]
    conv_w = wts["dn_conv"]
    alog, dtb, out_gain = wts["dn_a_log"], wts["dn_dt_bias"], wts["dn_out_norm"]
    qg, kg = wts["swa_q_norm"], wts["swa_k_norm"]
    sinks = wts["swa_sinks"].reshape(SWA_KV, 1, SWA_G)

    (h,) = _rows(lambda r, f: ([_f_rms(r[0], f[0])], []), "rms1_fwd", s_len, tm, [_whole(x)], [attn_gain],
                 [(D, BF16)])
    proj = _mm(h, w_in_p, "nn", F32, 256, PW, "mm_proj")
    q_dn, k_dn, v_dn, bb, gb = _dn_pre_fwd(proj, conv_w, alog, dtb)
    o_dn, s_all, t_all = _dn_chunks_fwd(q_dn, k_dn, v_dn, gb, bb)
    post_ins = [_whole(o_dn), (proj, DNW, C_Z // DNW)]
    (y_dn,) = _rows(lambda r, f: ([_f_dn_post(r[0], r[1], f[0])], []), "dn_post_fwd", s_len, tm, post_ins,
                    [out_gain], [(DNW, BF16)])

    bias = _bias_expand(wts["rel_bias"].T).reshape(SWA_H, BLK, 2 * BLK)
    y_swa = _swa_fwd(proj, bias, qg, kg, sinks)

    wts = {**wts, **wts["late"](y_swa)}
    p_a, p_b, merged = _branch_merge(y_dn, y_swa, wts["wa"], wts["wb"], proj)
    x1, h2 = _out_proj(merged, wts["w_out"], x, ffn_gain)
    gt, up, act = _ffn_up(h2, wts["wg"], wts["wu"])
    dy, dy_b, loss = _ffn_down_loss(act, wts["wd"], x1, target)

    grads = {}
    d_gt, d_up = _ffn_dact(dy_b, wts["wd"], gt, up)
    (grads["w_down"],) = _gw_ffn([act], dy_b, "gw_down")
    grads["w_gate"], grads["w_up"] = _gw_ffn([d_gt, d_up], h2, "gw_gate_up")
    dx1, dx1_b, grads["ffn_norm"] = _ffn_dh2(d_gt, d_up, wts["wg"], wts["wu"], x1, dy, ffn_gain)
    grads["w_out"] = _mm(merged, dx1_b, "tn", BF16, 512, 512, "gw_out")
    d_pa, d_pb, d_gr = _merge_bwd(dx1_b, wts["w_out"], p_a, p_b, proj)
    d_ydn, d_yswa = _d_branch(d_pa, d_pb, wts["wa"], wts["wb"])
    grads["w_branch_dn"], grads["w_branch_swa"] = _gw_branch(y_dn, y_swa, d_pa, d_pb)
    token = wts["send_early"](grads)
    qg_t = qg + token[0:1, 0:1]
    out_gain_t = out_gain + token[0:1, 0:1]

    d_sq, d_sk, d_sv, d_bias, grads["swa_q_norm"], grads["swa_k_norm"], d_sinks = _swa_bwd(
        proj, bias, qg_t, kg, sinks, d_yswa)
    grads["swa_sinks"] = d_sinks.reshape(1, SWA_H)
    grads["rel_bias"] = _bias_reduce(d_bias.reshape(SWA_H, BLK * 2 * BLK)).T

    def post_bwd(r, f):
        _, vjp = jax.vjp(_f_dn_post, r[0], r[1], f[0])
        d_o, d_z, d_gain = vjp(r[2])
        return [d_o, d_z], [d_gain]

    d_o, d_z, grads["dn_out_norm"] = _rows(post_bwd, "dn_post_bwd", s_len, tm, post_ins + [_whole(d_ydn)], [out_gain_t],
                                           [(DNW, F32), (DNW, BF16)], [(1, DH)])
    d_q, d_k, d_v, d_gb, d_bb = _dn_chunks_bwd(q_dn, k_dn, v_dn, gb, bb, s_all, t_all, d_o)

    d_qkv, d_ba, grads["dn_conv"], grads["dn_a_log"], grads["dn_dt_bias"] = _dn_pre_bwd(
        proj, conv_w, alog, dtb, (d_q, d_k, d_v, d_bb, d_gb))

    d_proj = jnp.concatenate(
        [d_qkv, d_z, d_gr, d_sq, d_sk, d_sv, d_ba, jnp.zeros((s_len, PW - C_BA - 128), BF16)], axis=1)
    grads["w_in_p"] = _mm(h, d_proj, "tn", BF16, 512, 1024, "gw_in")
    token = wts["send_in"](grads["w_in_p"])
    grad_x, grads["attn_norm"] = _dh_rms(d_proj, w_in_p, x, dx1, attn_gain + token[0:1, 0:1])
    return loss, grad_x, grads


_HBM = pl.BlockSpec(memory_space=pl.ANY)


def _place():
    return lax.axis_index("x"), lax.axis_index("y"), lax.axis_index("c")


def _other_chips(x, y):
    return [(1 - x, y), (x, 1 - y), (1 - x, 1 - y)]


def _rcopy(src, dst, send_sems, recv_sems, k, to):
    return pltpu.make_async_remote_copy(src_ref=src, dst_ref=dst, send_sem=send_sems.at[k], recv_sem=recv_sems.at[k],
                                        device_id=to, device_id_type=MESH)


def _comm_call(body, name, ins, out_shapes, n_remote, landing=0):
    first = len(ins) - landing
    return pl.pallas_call(
        body, name=name, in_specs=[_HBM] * len(ins), out_specs=[_HBM] * len(out_shapes), out_shape=out_shapes,
        scratch_shapes=[pltpu.SemaphoreType.DMA((n_remote,)), pltpu.SemaphoreType.DMA((n_remote,))],
        input_output_aliases={first + i: i for i in range(landing)},
        compiler_params=_cparams(has_side_effects=True),
    )(*ins)


def _own_slot(blocks, chip):
    return [lax.dynamic_update_slice(lax.empty((N_CHIPS,) + b.shape, b.dtype), b[None], (chip, 0, 0)) for b in blocks]


def _gather_weights(ws, chip):
    n = len(ws)
    halves = [w.shape[0] // 2 for w in ws]

    def body(*refs):
        w_refs, o_refs = refs[:n], refs[2 * n:3 * n]
        send_sems, recv_sems = refs[3 * n:]
        x, y, c = _place()
        s = 2 * x + y
        sib = (x, y, 1 - c)
        chips = _other_chips(x, y)

        def rows(i, half):
            return pl.ds(half * halves[i], halves[i])

        first = []
        for j, (cx, cy) in enumerate(chips):
            for i in range(n):
                cp = _rcopy(w_refs[i].at[rows(i, c), :], o_refs[i].at[s, rows(i, c), :], send_sems, recv_sems,
                            j * n + i, (cx, cy, c))
                cp.start()
                first.append(cp)
        passed = []
        for j, (cx, cy) in enumerate(chips):
            sj = 2 * cx + cy
            for i in range(n):
                blk = o_refs[i].at[sj, rows(i, c), :]
                _rcopy(blk, blk, send_sems, recv_sems, j * n + i, (cx, cy, c)).wait_recv()
                cp = _rcopy(blk, blk, send_sems, recv_sems, (3 + j) * n + i, sib)
                cp.start()
                passed.append(cp)
        for j, (cx, cy) in enumerate(chips):
            sj = 2 * cx + cy
            for i in range(n):
                blk = o_refs[i].at[sj, rows(i, 1 - c), :]
                _rcopy(blk, blk, send_sems, recv_sems, (3 + j) * n + i, sib).wait_recv()
        for cp in first + passed:
            cp.wait_send()

    return _comm_call(body, "gather_weights", list(ws) + _own_slot(ws, chip),
                      [SDS((N_CHIPS,) + w.shape, w.dtype) for w in ws], 6 * n, landing=n)


_HBM_ONLY = pl.BlockSpec(memory_space=pltpu.HBM)
_SEM = pl.BlockSpec(memory_space=pltpu.SEMAPHORE)
_DATAFLOW = pltpu.SideEffectType.DATAFLOW_SIDE_EFFECTING


def _in_hbm(a):
    return pltpu.with_memory_space_constraint(a, pltpu.HBM)


def _gather_windows(blocks):
    halves = [b.shape[0] // 2 for b in blocks]

    def src_at(ref, i, c, sj):
        return ref.at[pl.ds(c * halves[i], halves[i]), :]

    def dst_at(ref, i, c, s_from):
        return ref.at[s_from, pl.ds(c * halves[i], halves[i]), :]

    return src_at, dst_at


def _exchange_windows():
    return (lambda ref, i, c, sj: ref.at[sj]), (lambda ref, i, c, s_from: ref.at[s_from])


def _split_start(name, ws, lands, dep, windows):
    n = len(ws)
    src_at, dst_at = windows

    def body(*refs):
        w_refs, l_refs = refs[:n], refs[n:2 * n]
        send_sems, recv_sems = refs[2 * n + 1], refs[2 * n + 2]
        token = refs[-1]
        x, y, c = _place()
        s = 2 * x + y
        for j, (cx, cy) in enumerate(_other_chips(x, y)):
            for i in range(n):
                _rcopy(src_at(w_refs[i], i, c, 2 * cx + cy), dst_at(l_refs[i], i, c, s), send_sems, recv_sems,
                       j * n + i, (cx, cy, c)).start()
        token[...] = jnp.zeros_like(token)

    outs = pl.pallas_call(
        body, name=name,
        out_shape=(pltpu.SemaphoreType.DMA((3 * n,)), pltpu.SemaphoreType.DMA((3 * n,)),
                   *[pltpu.HBM(w.shape, w.dtype) for w in ws], *[pltpu.HBM(t.shape, t.dtype) for t in lands],
                   SDS((8, 128), F32)),
        in_specs=[_HBM_ONLY] * (2 * n) + [pl.BlockSpec(memory_space=pl.ANY)],
        out_specs=(_SEM, _SEM, *[_HBM_ONLY] * (2 * n), pl.BlockSpec(memory_space=pltpu.VMEM)),
        input_output_aliases={i: 2 + i for i in range(2 * n)},
        compiler_params=pltpu.CompilerParams(has_side_effects=_DATAFLOW),
    )(*[_in_hbm(w) for w in ws], *[_in_hbm(t) for t in lands], dep)
    return outs[0], outs[1], outs[2:2 + n], outs[2 + n:2 + 2 * n], outs[-1]


def _split_wait(name, w_thru, l_thru, send_sems, recv_sems, after, windows):
    n = len(w_thru)
    src_at, dst_at = windows

    def body(*refs):
        w_refs, l_refs = refs[:n], refs[n:2 * n]
        send_sems, recv_sems = refs[2 * n], refs[2 * n + 1]
        x, y, c = _place()
        for j, (cx, cy) in enumerate(_other_chips(x, y)):
            sj = 2 * cx + cy
            for i in range(n):
                cp = _rcopy(src_at(w_refs[i], i, c, sj), dst_at(l_refs[i], i, c, sj), send_sems, recv_sems, j * n + i,
                            (cx, cy, c))
                cp.wait_send()
                cp.wait_recv()

    outs = pl.pallas_call(
        body, name=name,
        out_shape=[pltpu.HBM(w.shape, w.dtype) for w in w_thru] + [pltpu.HBM(t.shape, t.dtype) for t in l_thru],
        in_specs=[_HBM_ONLY] * (2 * n) + [_SEM, _SEM, pl.BlockSpec(memory_space=pl.ANY)],
        out_specs=[_HBM_ONLY] * (2 * n),
        input_output_aliases={i: i for i in range(2 * n)},
        compiler_params=pltpu.CompilerParams(has_side_effects=_DATAFLOW),
    )(*w_thru, *l_thru, send_sems, recv_sems, after)
    return outs[n:]


def _sibling_fill(lands):
    n = len(lands)
    halves = [t.shape[1] // 2 for t in lands]

    def body(*refs):
        o_refs = refs[n:2 * n]
        send_sems, recv_sems = refs[2 * n:]
        x, y, c = _place()
        sib = (x, y, 1 - c)
        chips = _other_chips(x, y)
        sent = []
        for j, (cx, cy) in enumerate(chips):
            for i in range(n):
                blk = o_refs[i].at[2 * cx + cy, pl.ds(c * halves[i], halves[i]), :]
                cp = _rcopy(blk, blk, send_sems, recv_sems, j * n + i, sib)
                cp.start()
                sent.append(cp)
        for j, (cx, cy) in enumerate(chips):
            for i in range(n):
                blk = o_refs[i].at[2 * cx + cy, pl.ds((1 - c) * halves[i], halves[i]), :]
                _rcopy(blk, blk, send_sems, recv_sems, j * n + i, sib).wait_recv()
        for cp in sent:
            cp.wait_send()

    return _comm_call(body, "sibling_fill", list(lands), [SDS(t.shape, t.dtype) for t in lands], 3 * n, landing=n)


def _swap_halves(gs, name):
    n = len(gs)
    halves = [g.shape[1] // 2 for g in gs]

    def body(*refs):
        g_refs, o_refs = refs[:n], refs[n:2 * n]
        send_sems, recv_sems = refs[2 * n:]
        x, y, c = _place()
        cps = [_rcopy(g_refs[i].at[:, pl.ds((1 - c) * halves[i], halves[i]), :], o_refs[i], send_sems, recv_sems, i,
                      (x, y, 1 - c)) for i in range(n)]
        for cp in cps:
            cp.start()
        for cp in cps:
            cp.wait()

    return _comm_call(body, name, gs, [SDS((N_CHIPS, h, g.shape[2]), g.dtype) for g, h in zip(gs, halves)], n)


def _swap_reduced(rs, name):
    n = len(rs)

    def body(*refs):
        r_refs, o_refs = refs[:n], refs[n:2 * n]
        send_sems, recv_sems = refs[2 * n:]
        x, y, c = _place()
        cps = [_rcopy(r_refs[i], o_refs[i], send_sems, recv_sems, i, (x, y, 1 - c)) for i in range(n)]
        for cp in cps:
            cp.start()
        for cp in cps:
            cp.wait()

    return _comm_call(body, name, rs, [SDS(r.shape, r.dtype) for r in rs], n)


def _all_sum_small(vec, name):
    n_dev = 8
    flips = [(bx, by, bc) for bx in (0, 1) for by in (0, 1) for bc in (0, 1)][1:]

    def body(v_ref, out_ref, gath, send_sems, recv_sems):
        x, y, c = _place()
        me = 4 * x + 2 * y + c
        gath[me] = v_ref[...]
        sent = []
        for k, (bx, by, bc) in enumerate(flips):
            peer = (x ^ bx, y ^ by, c ^ bc)
            cp = _rcopy(v_ref, gath.at[me], send_sems, recv_sems, k, peer)
            cp.start()
            sent.append(cp)
        for k, (bx, by, bc) in enumerate(flips):
            peer = (x ^ bx, y ^ by, c ^ bc)
            _rcopy(v_ref, gath.at[4 * peer[0] + 2 * peer[1] + peer[2]], send_sems, recv_sems, k, peer).wait_recv()
        for cp in sent:
            cp.wait_send()
        acc = gath[0]
        for d in range(1, n_dev):
            acc = acc + gath[d]
        out_ref[...] = acc

    vm = pl.BlockSpec(memory_space=pltpu.VMEM)
    return pl.pallas_call(
        body, name=name, in_specs=[vm], out_specs=vm, out_shape=SDS(vec.shape, F32),
        scratch_shapes=[pltpu.VMEM((n_dev,) + vec.shape, F32), pltpu.SemaphoreType.DMA((7,)),
                        pltpu.SemaphoreType.DMA((7,))],
        compiler_params=_cparams(has_side_effects=True),
    )(vec)


def _pack_small(vals, extra=None):
    parts = [vals[n].reshape(-1).astype(F32) for n, _ in _SMALL]
    parts.append(jnp.zeros((1,), F32) if extra is None else extra.reshape(1).astype(F32))
    flat = jnp.concatenate(parts)
    flat = jnp.concatenate([flat, jnp.zeros((_SMALL_ROWS * 128 - flat.shape[0],), F32)])
    return flat.reshape(_SMALL_ROWS, 128)


def _unpack_small(packed, shapes):
    flat = packed.reshape(-1)
    return {n: flat[_SMALL_OFF[n][0]:_SMALL_OFF[n][0] + _SMALL_OFF[n][1]].reshape(shapes[n]) for n, _ in _SMALL}


def _pair_sum(gs, gots, core, name):
    n = len(gs)

    def kern(c_ref, *refs):
        for i in range(n):
            refs[2 * n + i][...] = (refs[i][...].astype(F32) + refs[n + i][...].astype(F32)).astype(BF16)

    in_specs = [pl.BlockSpec((1, t.shape[1], t.shape[2]), lambda s, c_ref: (s, c_ref[0], 0)) for t in gots]
    in_specs += [pl.BlockSpec((1, t.shape[1], t.shape[2]), lambda s, c_ref: (s, 0, 0)) for t in gots]
    out_specs = [pl.BlockSpec((1, t.shape[1], t.shape[2]), lambda s, c_ref: (s, 0, 0)) for t in gots]
    return pl.pallas_call(
        kern, name=name,
        grid_spec=pltpu.PrefetchScalarGridSpec(num_scalar_prefetch=1, grid=(N_CHIPS,), in_specs=in_specs,
                                               out_specs=out_specs),
        out_shape=[SDS(t.shape, BF16) for t in gots],
        compiler_params=_cparams(dimension_semantics=("arbitrary",)),
    )(core.reshape(1).astype(jnp.int32), *gs, *gots)


def _chip_sum(qs, name):
    n = len(qs)

    def kern(*refs):
        for i in range(n):
            acc = refs[i][0].astype(F32)
            for s in range(1, N_CHIPS):
                acc = acc + refs[i][s].astype(F32)
            refs[n + i][...] = acc

    in_specs = [pl.BlockSpec((N_CHIPS, q.shape[1] // 2, q.shape[2]), lambda j: (0, j, 0)) for q in qs]
    out_specs = [pl.BlockSpec((q.shape[1] // 2, q.shape[2]), lambda j: (j, 0)) for q in qs]
    return _pc(kern, name, (2,), in_specs, out_specs, [SDS(q.shape[1:], F32) for q in qs])(*qs)


def _adam_math(w_, g_, m_, v_):
    m_ = ADAM_B1 * m_ + (1.0 - ADAM_B1) * g_
    v_ = ADAM_B2 * v_ + (1.0 - ADAM_B2) * jnp.square(g_)
    m_hat = m_ / (1.0 - ADAM_B1 ** ADAM_STEP)
    v_hat = v_ / (1.0 - ADAM_B2 ** ADAM_STEP)
    return -ADAM_LR * (m_hat / (jnp.sqrt(v_hat) + ADAM_EPS) + ADAM_WD * w_), m_, v_


def _adamw(w, g, m, v, name):
    rows, cols = w.shape
    tr = rows
    for cand in (256, 128, 64, 32, 16, 8):
        if rows % cand == 0 and rows > cand:
            tr = cand
            break

    def kern(w_ref, g_ref, m_ref, v_ref, d_ref, nm_ref, nv_ref):
        d_ref[...], nm_ref[...], nv_ref[...] = _adam_math(w_ref[...], g_ref[...], m_ref[...], v_ref[...])

    spec = pl.BlockSpec((tr, cols), lambda i: (i, 0))
    return _pc(kern, name, (rows // tr,), [spec] * 4, [spec] * 3, [SDS(w.shape, F32)] * 3)(w, g, m, v)


def _adamw_rows1(w, g, m, v, name):
    rows, _, cols = w.shape
    tr = next(t for t in (42, 32, 29, 16, 8, 7, 6, 4, 3, 2, 1) if rows % t == 0)

    def kern(w_ref, g_ref, m_ref, v_ref, d_ref, nm_ref, nv_ref):
        d_ref[...], nm_ref[...], nv_ref[...] = _adam_math(w_ref[...], g_ref[...], m_ref[...], v_ref[...])

    spec = pl.BlockSpec((tr, 1, cols), lambda i: (i, 0, 0))
    return _pc(kern, name, (rows // tr,), [spec] * 4, [spec] * 3, [SDS(w.shape, F32)] * 3)(w, g, m, v)


def _adamw_big(w, mine, theirs, m, v, core, name):
    _, rows, cols = w.shape
    half = rows // 2
    tr = next(t for t in (256, 176, 128, 64, 32, 16, 8) if half % t == 0)
    nbh = half // tr

    def kern(c_ref, w_ref, a_ref, b_ref, m_ref, v_ref, g_ref, d_ref, nm_ref, nv_ref):
        g_ = jnp.where(pl.program_id(0) // nbh == c_ref[0], a_ref[...], b_ref[...])
        g_ref[0] = g_
        d_ref[0], nm_ref[0], nv_ref[0] = _adam_math(w_ref[0], g_, m_ref[0], v_ref[0])

    full = pl.BlockSpec((1, tr, cols), lambda i, c_ref: (0, i, 0))
    part = pl.BlockSpec((tr, cols), lambda i, c_ref: (i % nbh, 0))
    return pl.pallas_call(
        kern, name=name,
        grid_spec=pltpu.PrefetchScalarGridSpec(num_scalar_prefetch=1, grid=(rows // tr,),
                                               in_specs=[full, part, part, full, full], out_specs=[full] * 4),
        out_shape=[SDS(w.shape, F32)] * 4,
        compiler_params=_cparams(dimension_semantics=("arbitrary",)),
    )(core.reshape(1).astype(jnp.int32), w, mine, theirs, m, v)


_WEIGHT_NAMES = ("attn_norm", "w_in", "dn_conv", "dn_a_log", "dn_dt_bias", "dn_out_norm", "swa_q_norm", "swa_k_norm",
                 "swa_sinks", "rel_bias", "w_branch_dn", "w_branch_swa", "w_out", "ffn_norm", "w_gate", "w_up",
                 "w_down")
_CONV_SH = QKVW // N_CHIPS


def kernel(x, attn_norm, w_in, dn_conv, dn_a_log, dn_dt_bias, dn_out_norm, swa_q_norm, swa_k_norm, swa_sinks, rel_bias, w_branch_dn, w_branch_swa, w_out, ffn_norm, w_gate, w_up, w_down, loss_target, m_attn_norm, m_w_in, m_dn_conv, m_dn_a_log, m_dn_dt_bias, m_dn_out_norm, m_swa_q_norm, m_swa_k_norm, m_swa_sinks, m_rel_bias, m_w_branch_dn, m_w_branch_swa, m_w_out, m_ffn_norm, m_w_gate, m_w_up, m_w_down, v_attn_norm, v_w_in, v_dn_conv, v_dn_a_log, v_dn_dt_bias, v_dn_out_norm, v_swa_q_norm, v_swa_k_norm, v_swa_sinks, v_rel_bias, v_w_branch_dn, v_w_branch_swa, v_w_out, v_ffn_norm, v_w_gate, v_w_up, v_w_down):
    w = dict(attn_norm=attn_norm, w_in=w_in, dn_conv=dn_conv, dn_a_log=dn_a_log, dn_dt_bias=dn_dt_bias,
             dn_out_norm=dn_out_norm, swa_q_norm=swa_q_norm, swa_k_norm=swa_k_norm, swa_sinks=swa_sinks,
             rel_bias=rel_bias, w_branch_dn=w_branch_dn, w_branch_swa=w_branch_swa, w_out=w_out, ffn_norm=ffn_norm,
             w_gate=w_gate, w_up=w_up, w_down=w_down)
    m = dict(attn_norm=m_attn_norm, w_in=m_w_in, dn_conv=m_dn_conv, dn_a_log=m_dn_a_log, dn_dt_bias=m_dn_dt_bias,
             dn_out_norm=m_dn_out_norm, swa_q_norm=m_swa_q_norm, swa_k_norm=m_swa_k_norm, swa_sinks=m_swa_sinks,
             rel_bias=m_rel_bias, w_branch_dn=m_w_branch_dn, w_branch_swa=m_w_branch_swa, w_out=m_w_out,
             ffn_norm=m_ffn_norm, w_gate=m_w_gate, w_up=m_w_up, w_down=m_w_down)
    v = dict(attn_norm=v_attn_norm, w_in=v_w_in, dn_conv=v_dn_conv, dn_a_log=v_dn_a_log, dn_dt_bias=v_dn_dt_bias,
             dn_out_norm=v_dn_out_norm, swa_q_norm=v_swa_q_norm, swa_k_norm=v_swa_k_norm, swa_sinks=v_swa_sinks,
             rel_bias=v_rel_bias, w_branch_dn=v_w_branch_dn, w_branch_swa=v_w_branch_swa, w_out=v_w_out,
             ffn_norm=v_ffn_norm, w_gate=v_w_gate, w_up=v_w_up, w_down=v_w_down)
    shapes = {n: w[n].shape for n in _WEIGHT_NAMES}

    def two_d(a):
        return a.reshape(a.shape[-2], a.shape[-1]) if a.ndim == 3 else a

    core = lax.axis_index("c")
    chip = 2 * lax.axis_index("x") + lax.axis_index("y")
    small_shapes = {n: two_d(w[n]).shape for n, _ in _SMALL}
    small_shapes["dn_conv"] = (CONV, QKVW)

    conv_loc = two_d(w["dn_conv"])
    conv_part = lax.dynamic_update_slice(jnp.zeros((CONV, QKVW), F32), jnp.where(core == 0, conv_loc, 0.0),
                                         (0, chip * _CONV_SH))
    conv_full = _all_sum_small(conv_part.reshape(CONV * QKVW // 128, 128), "gather_conv").reshape(CONV, QKVW)

    flipped = ("w_gate", "w_up")

    def natural(a, n):
        return a.transpose(0, 2, 1) if n in flipped else a

    w_bf = [two_d(natural(w[n], n).astype(BF16)) for n in _BIG_NAMES]
    (w_in_g,) = _gather_weights(w_bf[:1], chip)
    windows = _gather_windows(w_bf[1:])
    after_sync = w_in_g[0, :8, :128].astype(F32) + conv_full[0:1, :128]
    send_sems, recv_sems, w_thru, l_thru, token = _split_start(
        "gather_start", w_bf[1:], _own_slot(w_bf[1:], chip), after_sync, windows)

    def late(after):
        lands = _split_wait("gather_wait", w_thru, l_thru, send_sems, recv_sems, after, windows)
        g = dict(zip(_BIG_NAMES[1:], _sibling_fill(lands)))
        return dict(wa=g["w_branch_dn"], wb=g["w_branch_swa"], w_out=g["w_out"].reshape(D, D), wg=g["w_gate"],
                    wu=g["w_up"], wd=g["w_down"])

    w_in_full = w_in_g.transpose(1, 0, 2).reshape(D, D_IN)
    wts = dict(w_in_p=_pad_w_in(w_in_full), dn_conv=conv_full, late=late)
    for n, _ in _SMALL[:-1]:
        wts[n] = two_d(w[n])
    wts["attn_norm"] = wts["attn_norm"] + token[0:1, 0:1]

    early = {}

    def send_early(grads):
        gs = [grads["w_branch_dn"], grads["w_branch_swa"], grads["w_out"].reshape(N_CHIPS, CSH, D), grads["w_gate"],
              grads["w_up"], grads["w_down"]]
        parts = _pair_sum(gs, _swap_halves(gs, "swap_halves_early"), core, "pair_sum_early")
        own = [lax.dynamic_index_in_dim(p, chip, axis=0, keepdims=False) for p in parts]
        early["sems"], early["recv"], early["src"], early["land"], tok = _split_start(
            "exchange_start", parts, _own_slot(own, chip), parts[0][0, :8, :128], _exchange_windows())
        return tok

    last = {}

    def send_in(g_in_p):
        g_in = [_unpad_w_in(g_in_p).reshape(D, N_CHIPS, D_IN // N_CHIPS).transpose(1, 0, 2)]
        parts = _pair_sum(g_in, _swap_halves(g_in, "swap_halves_in"), core, "pair_sum_in")
        own = [lax.dynamic_index_in_dim(p, chip, axis=0, keepdims=False) for p in parts]
        last["sems"], last["recv"], last["src"], last["land"], tok = _split_start(
            "exchange_in_start", parts, _own_slot(own, chip), parts[0][0, :8, :128], _exchange_windows())
        return tok

    wts["send_early"] = send_early
    wts["send_in"] = send_in
    loss_sum, grad_x, grads = _local_step(x[0], loss_target[0], wts)

    small_sum = _all_sum_small(_pack_small(grads, loss_sum), "all_sum_small")
    loss = small_sum.reshape(-1)[_LOSS_OFF]
    g_small = _unpack_small(small_sum, small_shapes)

    q_early = _split_wait("exchange_wait", early["src"], early["land"], early["sems"], early["recv"], small_sum,
                          _exchange_windows())
    red_early = _chip_sum(list(q_early), "chip_sum_early")
    their_early = _swap_reduced(red_early, "swap_reduced_early")
    g_out, d_out, m_out, v_out = {}, {}, {}, {}
    for n, mine, other in zip(_BIG_NAMES[1:], red_early, their_early):
        res = _adamw_big(natural(w[n], n), mine, other, natural(m[n], n), natural(v[n], n), core, "adamw_" + n)
        g_out[n], d_out[n], m_out[n], v_out[n] = (natural(t, n) for t in res)

    q_in = _split_wait("exchange_in_wait", last["src"], last["land"], last["sems"], last["recv"],
                       d_out[_BIG_NAMES[-1]], _exchange_windows())
    reduced = _chip_sum(list(q_in), "chip_sum_in")
    theirs = _swap_reduced(reduced, "swap_reduced_in")

    def rows1(a):
        return a.transpose(2, 0, 1)

    def unrows1(a):
        return a.transpose(1, 2, 0)

    g_in_blk = jnp.concatenate([jnp.where(core == 0, reduced[0], theirs[0]),
                                jnp.where(core == 0, theirs[0], reduced[0])], axis=0)
    g_in_r = rows1(g_in_blk[None])
    d_, m_, v_ = _adamw_rows1(rows1(w["w_in"]), g_in_r, rows1(m["w_in"]), rows1(v["w_in"]), "adamw_w_in")
    g_out["w_in"], d_out["w_in"], m_out["w_in"], v_out["w_in"] = (unrows1(t) for t in (g_in_r, d_, m_, v_))
    g_conv = lax.dynamic_slice(g_small["dn_conv"], (0, chip * _CONV_SH), (CONV, _CONV_SH))
    g_out["dn_conv"] = g_conv.reshape(shapes["dn_conv"])
    d_, m_, v_ = _adamw(conv_loc, g_conv, two_d(m["dn_conv"]), two_d(v["dn_conv"]), "adamw_dn_conv")
    d_out["dn_conv"], m_out["dn_conv"], v_out["dn_conv"] = (t.reshape(shapes["dn_conv"]) for t in (d_, m_, v_))

    def packed(src):
        vals = {n: src[n] for n, _ in _SMALL[:-1]}
        vals["dn_conv"] = jnp.zeros((CONV * QKVW,), F32)
        return _pack_small(vals)

    d_s, m_s, v_s = _adamw(packed(w), small_sum, packed(m), packed(v), "adamw_small")
    d_small, m_small, v_small = (_unpack_small(t, small_shapes) for t in (d_s, m_s, v_s))
    for n, _ in _SMALL[:-1]:
        g_out[n] = g_small[n].reshape(shapes[n])
        d_out[n], m_out[n], v_out[n] = (t[n].reshape(shapes[n]) for t in (d_small, m_small, v_small))

    return (loss, grad_x[None], *[g_out[n] for n in _WEIGHT_NAMES], *[d_out[n] for n in _WEIGHT_NAMES],
            *[m_out[n] for n in _WEIGHT_NAMES], *[v_out[n] for n in _WEIGHT_NAMES])
```

```python
import functools
import math

import numpy as np
import jax
import jax.numpy as jnp
from jax import lax
from jax.experimental import pallas as pl
from jax.experimental.pallas import tpu as pltpu

F32 = jnp.float32
BF16 = jnp.bfloat16
SDS = jax.ShapeDtypeStruct

D = 1024
DN_H = 4
DH = 128
DNW = DN_H * DH
QKVW = 3 * DNW
CONV = 4
CHUNK = 64
SWA_H = 8
SWA_KV = 2
SWA_G = SWA_H // SWA_KV
SWA_D = 64
SWAW = SWA_H * SWA_D
SWAKW = SWA_KV * SWA_D
BLK = 128
NBUCKET = 32
MAXDIST = 128
DFF = 2816
D_IN = QKVW + DNW + 2 * DN_H + SWAW + 2 * SWAKW + 2 * D
EPS = 1e-6
NEG = -1e30

ADAM_LR = 0.001
ADAM_B1 = 0.9
ADAM_B2 = 0.999
ADAM_EPS = 1e-08
ADAM_WD = 0.01
ADAM_STEP = 10

C_QKV, C_Z, C_GATE, C_SQ, C_SK, C_SV, C_BA = 0, 1536, 2048, 4096, 4608, 4736, 4864
PW = 5120
_ORIG_PIECES = (
    (0, QKVW, C_QKV),
    (QKVW, DNW, C_Z),
    (QKVW + DNW, 2 * DN_H, C_BA),
    (QKVW + DNW + 2 * DN_H, SWAW, C_SQ),
    (QKVW + DNW + 2 * DN_H + SWAW, SWAKW, C_SK),
    (QKVW + DNW + 2 * DN_H + SWAW + SWAKW, SWAKW, C_SV),
    (QKVW + DNW + 2 * DN_H + SWAW + 2 * SWAKW, 2 * D, C_GATE),
)

N_CHIPS = 4
FSH = DFF // N_CHIPS
CSH = D // N_CHIPS
VMEM_LIMIT = 48 * 1024 * 1024
MESH = pl.DeviceIdType.MESH

_BIG = (
    ("w_in", D, D_IN // N_CHIPS),
    ("w_branch_dn", DNW, CSH),
    ("w_branch_swa", SWAW, CSH),
    ("w_out", CSH, D),
    ("w_gate", FSH, D),
    ("w_up", FSH, D),
    ("w_down", FSH, D),
)
_BIG_NAMES = tuple(n for n, _, _ in _BIG)

_SMALL = (
    ("attn_norm", D), ("ffn_norm", D), ("dn_out_norm", DH), ("swa_q_norm", SWA_D), ("swa_k_norm", SWA_D),
    ("swa_sinks", SWA_H), ("dn_a_log", DN_H), ("dn_dt_bias", DN_H), ("rel_bias", NBUCKET * SWA_H),
    ("dn_conv", CONV * QKVW),
)
_SMALL_OFF = {}
_o = 0
for _n, _s in _SMALL:
    _SMALL_OFF[_n] = (_o, _s)
    _o += _s
_LOSS_OFF = _o
_SMALL_ROWS = -(-(_o + 1) // (8 * 128)) * 8


def _cparams(**kw):
    return pltpu.CompilerParams(vmem_limit_bytes=VMEM_LIMIT, **kw)


_DIMS = {
    "nn": (((1,), (0,)), ((), ())),
    "nt": (((1,), (1,)), ((), ())),
    "tn": (((0,), (0,)), ((), ())),
    "bnn": (((2,), (1,)), ((0,), (0,))),
    "bnt": (((2,), (2,)), ((0,), (0,))),
    "btn": (((1,), (1,)), ((0,), (0,))),
}


def _raw_dot(a, b, kind, exact):
    if exact:
        prec = lax.Precision.HIGH if exact == "x3" else lax.Precision.HIGHEST
        return lax.dot_general(a, b, _DIMS[kind], precision=prec, preferred_element_type=F32)
    return lax.dot_general(a.astype(BF16), b.astype(BF16), _DIMS[kind], preferred_element_type=F32)


@functools.partial(jax.custom_vjp, nondiff_argnums=(2, 3))
def _dot(a, b, kind, exact):
    return _raw_dot(a, b, kind, exact)


def _dot_fwd(a, b, kind, exact):
    return _raw_dot(a, b, kind, exact), (a, b)


def _dot_bwd(kind, exact, res, g):
    a, b = res
    pre = kind[:-2]
    nn, nt, tn = pre + "nn", pre + "nt", pre + "tn"
    if kind == nn:
        return _dot(g, b, nt, exact), _dot(a, g, tn, exact)
    if kind == nt:
        return _dot(g, b, nn, exact), _dot(g, a, tn, exact)
    return _dot(b, g, nt, exact), _dot(a, g, nn, exact)


_dot.defvjp(_dot_fwd, _dot_bwd)


def _silu(x):
    return x * jax.nn.sigmoid(x)


def _f_rms(x, gain):
    return x * lax.rsqrt(jnp.mean(x * x, axis=-1, keepdims=True) + EPS) * gain


def _f_dn_pre(xs0, xs1, xs2, xs3, ba, cw, alog, dtb):
    rows = xs0.shape[0]
    c = xs0 * cw[0:1] + xs1 * cw[1:2] + xs2 * cw[2:3] + xs3 * cw[3:4]
    qkv = _silu(c)
    qs, ks, bbs, gbs = [], [], [], []
    for h in range(DN_H):
        qh = qkv[:, h * DH:(h + 1) * DH]
        kh = qkv[:, DNW + h * DH:DNW + (h + 1) * DH]
        qs.append(qh * lax.rsqrt(jnp.sum(qh * qh, axis=-1, keepdims=True) + EPS) * (DH ** -0.5))
        ks.append(kh * lax.rsqrt(jnp.sum(kh * kh, axis=-1, keepdims=True) + EPS))
        beta = jax.nn.sigmoid(ba[:, h:h + 1])
        ar = ba[:, DN_H + h:DN_H + h + 1] + dtb[:, h:h + 1]
        softplus = jnp.maximum(ar, 0.0) + jnp.log1p(jnp.exp(-jnp.abs(ar)))
        g = -jnp.exp(alog[:, h:h + 1]) * softplus
        bbs.append(jnp.broadcast_to(beta, (rows, DH)))
        gbs.append(jnp.broadcast_to(g, (rows, DH)))
    return (jnp.concatenate(qs, axis=1), jnp.concatenate(ks, axis=1), qkv[:, 2 * DNW:],
            jnp.concatenate(bbs, axis=1), jnp.concatenate(gbs, axis=1))


def _f_dn_post(o, z, gain):
    ys = []
    for h in range(DN_H):
        oh = o[:, h * DH:(h + 1) * DH]
        zh = z[:, h * DH:(h + 1) * DH]
        ys.append(oh * lax.rsqrt(jnp.mean(oh * oh, axis=-1, keepdims=True) + EPS) * gain * _silu(zh))
    return jnp.concatenate(ys, axis=1)


def _f_merge(pa, pb, ga, gb):
    return jax.nn.sigmoid(ga) * pa + jax.nn.sigmoid(gb) * pb


def _f_swiglu(g, u):
    return _silu(g) * u


@jax.custom_vjp
def _unit_lower_inverse(a):
    c = a.shape[-1]
    eye = (lax.broadcasted_iota(jnp.int32, a.shape, 1) == lax.broadcasted_iota(jnp.int32, a.shape, 2)).astype(F32)
    p = -a
    t = eye + p
    for _ in range(max(c.bit_length() - 2, 0)):
        p = _raw_dot(p, p, "bnn", "x3")
        t = t + _raw_dot(t, p, "bnn", "x3")
    return t


def _unit_lower_inverse_fwd(a):
    t = _unit_lower_inverse(a)
    return t, t


def _unit_lower_inverse_bwd(t, g):
    return (-_raw_dot(_raw_dot(t, g, "btn", "x3"), t, "bnt", "x3"),)


_unit_lower_inverse.defvjp(_unit_lower_inverse_fwd, _unit_lower_inverse_bwd)


@jax.custom_vjp
def _known_inverse(a, t):
    return t


def _known_inverse_fwd(a, t):
    return t, t


def _known_inverse_bwd(t, g):
    return _unit_lower_inverse_bwd(t, g)[0], jnp.zeros_like(t)


_known_inverse.defvjp(_known_inverse_fwd, _known_inverse_bwd)


def _f_chunk(q, k, v, gb, bb, s, t_known=None, with_t=False):
    c = CHUNK
    nh = q.shape[0]
    ii = lax.broadcasted_iota(jnp.int32, (nh, c, c), 1)
    jj = lax.broadcasted_iota(jnp.int32, (nh, c, c), 2)
    incl = ii >= jj
    strict = ii > jj
    eye = (ii == jj).astype(F32)
    gcb = _dot(incl.astype(F32), gb, "bnn", True)
    lane0 = (lax.broadcasted_iota(jnp.int32, (nh, c, DH), 2) == 0).astype(F32)
    gcol = gcb[:, :, :c]
    grow = _dot(lane0, gcb, "bnt", True)
    decay = jnp.where(incl, jnp.exp(jnp.where(incl, gcol - grow, 0.0)), 0.0)
    kb = k * bb
    vb = v * bb
    a = jnp.where(strict, _dot(kb, k, "bnt", False) * decay, 0.0)
    t = _unit_lower_inverse(a) if t_known is None else _known_inverse(a, t_known)
    eg = jnp.exp(gcb)
    u = _dot(t, vb, "bnn", "x3")
    w = _dot(t, kb * eg, "bnn", "x3")
    qk = jnp.where(incl, _dot(q, k, "bnt", False) * decay, 0.0)
    qe = q * eg
    glast = gcb[:, c - 1:c, :]
    k_dec = k * jnp.exp(glast - gcb)
    e_last = jnp.exp(glast)
    outs = []
    for g in range(nh // DN_H):
        sl = slice(g * DN_H, (g + 1) * DN_H)
        v_new = u[sl] - _dot(w[sl], s, "bnn", False)
        outs.append(_dot(qe[sl], s, "bnn", False) + _dot(qk[sl], v_new, "bnn", False))
        s = s * e_last[sl] + _dot(k_dec[sl], v_new, "btn", False)
    o = jnp.concatenate(outs, axis=0)
    return (o, s, t) if with_t else (o, s)


def _f_swa(q8, kp, kc, vp, vc, bias8, qg, kg, sink, mask):
    kb = jnp.concatenate([kp, kc], axis=1)
    vb = jnp.concatenate([vp, vc], axis=1)
    kn = kb * lax.rsqrt(jnp.mean(kb * kb, axis=-1, keepdims=True) + EPS) * kg

    def rows(per_head):
        return jnp.stack([jnp.concatenate([per_head(kv, g) for g in range(SWA_G)], axis=0)
                          for kv in range(SWA_KV)], axis=0)

    qq = rows(lambda kv, g: q8[kv * SWA_G + g])
    qn = qq * lax.rsqrt(jnp.mean(qq * qq, axis=-1, keepdims=True) + EPS) * qg
    lg = _dot(qn, kn, "bnt", False) * (SWA_D ** -0.5) + rows(lambda kv, g: bias8[kv * SWA_G + g])
    lg = jnp.where(rows(lambda kv, g: mask), lg, NEG)
    sk = rows(lambda kv, g: jnp.broadcast_to(sink[kv][:, g:g + 1], (BLK, 1)))
    m = lax.stop_gradient(jnp.maximum(jnp.max(lg, axis=-1, keepdims=True), sk))
    p = jnp.exp(lg - m)
    den = jnp.sum(p, axis=-1, keepdims=True) + jnp.exp(sk - m)
    out = _dot(p / den, vb, "bnn", False)
    return jnp.stack([out[kv, g * BLK:(g + 1) * BLK] for kv in range(SWA_KV) for g in range(SWA_G)], axis=0)


def _bdot(a, b, kind="nn"):
    return lax.dot_general(a.astype(BF16), b.astype(BF16), _DIMS[kind], preferred_element_type=F32)


def _pc(kern, name, grid, in_specs, out_specs, out_shape, scratch=()):
    return pl.pallas_call(
        kern, name=name, grid=grid, in_specs=in_specs, out_specs=out_specs, out_shape=out_shape,
        scratch_shapes=list(scratch), compiler_params=_cparams(dimension_semantics=("arbitrary",) * len(grid)))


def _mm(a, b, kind, out_dtype, tm, tn, name):
    if kind == "tn":
        k, m = a.shape
    else:
        m, k = a.shape
    n = b.shape[0] if kind == "nt" else b.shape[1]
    tm, tn = min(tm, m), min(tn, n)
    assert m % tm == 0 and n % tn == 0, (name, a.shape, b.shape, tm, tn)

    def kern(a_ref, b_ref, o_ref):
        o_ref[...] = _bdot(a_ref[...], b_ref[...], kind).astype(o_ref.dtype)

    a_spec = pl.BlockSpec((k, tm), lambda i, j: (0, i)) if kind == "tn" else pl.BlockSpec((tm, k), lambda i, j: (i, 0))
    b_spec = pl.BlockSpec((tn, k), lambda i, j: (j, 0)) if kind == "nt" else pl.BlockSpec((k, tn), lambda i, j: (0, j))
    return _pc(kern, name, (m // tm, n // tn), [a_spec, b_spec], pl.BlockSpec((tm, tn), lambda i, j: (i, j)),
               SDS((m, n), out_dtype))(a, b)


def _rows(body, name, m, tm, row_ins, full_ins, row_outs, acc_outs=()):
    n_r, n_f, n_o, n_a = len(row_ins), len(full_ins), len(row_outs), len(acc_outs)
    assert m % tm == 0

    def kern(*refs):
        r = refs[:n_r]
        f = refs[n_r:n_r + n_f]
        o = refs[n_r + n_f:n_r + n_f + n_o]
        acc = refs[n_r + n_f + n_o:]
        outs, sums = body([x[...] for x in r], [x[...] for x in f])
        for ref, val in zip(o, outs, strict=True):
            ref[...] = val.astype(ref.dtype)
        if n_a:
            @pl.when(pl.program_id(0) == 0)
            def _():
                for ref in acc:
                    ref[...] = jnp.zeros(ref.shape, F32)

            for ref, val in zip(acc, sums, strict=True):
                ref[...] += val

    in_specs = [pl.BlockSpec((tm, w), functools.partial(lambda i, cb: (i, cb), cb=cb)) for _, w, cb in row_ins]
    in_specs += [pl.BlockSpec(x.shape, lambda i: (0, 0)) for x in full_ins]
    out_specs = [pl.BlockSpec((tm, w), lambda i: (i, 0)) for w, _ in row_outs]
    out_specs += [pl.BlockSpec(s, lambda i: (0, 0)) for s in acc_outs]
    out_shape = [SDS((m, w), dt) for w, dt in row_outs]
    out_shape += [SDS(s, F32) for s in acc_outs]
    return _pc(kern, name, (m // tm,), in_specs, out_specs, out_shape)(*[x for x, _, _ in row_ins], *full_ins)


def _whole(x):
    return (x, x.shape[1], 0)


def _zero_first(refs):
    @pl.when(pl.program_id(0) == 0)
    def _():
        for ref in refs:
            ref[...] = jnp.zeros(ref.shape, F32)


GROUP = 4


def _heads(ref):
    return jnp.stack([ref[g * CHUNK:(g + 1) * CHUNK, h * DH:(h + 1) * DH]
                      for g in range(GROUP) for h in range(DN_H)], axis=0)


def _unheads(ref, val):
    for g in range(GROUP):
        for h in range(DN_H):
            ref[g * CHUNK:(g + 1) * CHUNK, h * DH:(h + 1) * DH] = val[g * DN_H + h]


def _dn_chunks_fwd(q, k, v, gb, bb):
    s_len = q.shape[0]
    ng = s_len // (GROUP * CHUNK)

    def kern(q_ref, k_ref, v_ref, g_ref, b_ref, o_ref, sall_ref, t_ref, state):
        _zero_first([state])
        s = state[...]
        sall_ref[0] = s
        o, s_new, t = _f_chunk(*[_heads(r) for r in (q_ref, k_ref, v_ref, g_ref, b_ref)], s, with_t=True)
        _unheads(o_ref, o)
        t_ref[0] = t
        state[...] = s_new

    blk = pl.BlockSpec((GROUP * CHUNK, DNW), lambda c: (c, 0))
    return _pc(kern, "dn_chunks_fwd", (ng,), [blk] * 5,
               [blk, pl.BlockSpec((1, DN_H, DH, DH), lambda c: (c, 0, 0, 0)),
                pl.BlockSpec((1, GROUP * DN_H, CHUNK, CHUNK), lambda c: (c, 0, 0, 0))],
               [SDS((s_len, DNW), F32), SDS((ng, DN_H, DH, DH), F32), SDS((ng, GROUP * DN_H, CHUNK, CHUNK), F32)],
               scratch=[pltpu.VMEM((DN_H, DH, DH), F32)])(q, k, v, gb, bb)


def _dn_chunks_bwd(q, k, v, gb, bb, s_all, t_all, d_o):
    s_len = q.shape[0]
    ng = s_len // (GROUP * CHUNK)

    def kern(q_ref, k_ref, v_ref, g_ref, b_ref, sall_ref, t_ref, do_ref, dq_ref, dk_ref, dv_ref, dg_ref, db_ref,
             dstate):
        _zero_first([dstate])
        fn = functools.partial(_f_chunk, t_known=t_ref[0])
        _, vjp = jax.vjp(fn, *[_heads(r) for r in (q_ref, k_ref, v_ref, g_ref, b_ref)], sall_ref[0])
        *d_ins, ds = vjp((_heads(do_ref), dstate[...]))
        for ref, val in zip((dq_ref, dk_ref, dv_ref, dg_ref, db_ref), d_ins, strict=True):
            _unheads(ref, val)
        dstate[...] = ds

    blk = pl.BlockSpec((GROUP * CHUNK, DNW), lambda c: (ng - 1 - c, 0))
    return _pc(kern, "dn_chunks_bwd", (ng,),
               [blk] * 5 + [pl.BlockSpec((1, DN_H, DH, DH), lambda c: (ng - 1 - c, 0, 0, 0)),
                            pl.BlockSpec((1, GROUP * DN_H, CHUNK, CHUNK), lambda c: (ng - 1 - c, 0, 0, 0)), blk],
               [blk] * 5, [SDS((s_len, DNW), F32)] * 5,
               scratch=[pltpu.VMEM((DN_H, DH, DH), F32)])(q, k, v, gb, bb, s_all, t_all, d_o)


def _t5_bucket_table():
    qi = np.arange(BLK)[:, None]
    kj = np.arange(2 * BLK)[None, :]
    dist = BLK + qi - kj
    n = np.maximum(dist, 0)
    max_exact = NBUCKET // 2
    nf = np.maximum(n, 1).astype(np.float32)
    large = max_exact + (np.log(nf / np.float32(max_exact)) / np.float32(math.log(MAXDIST / max_exact))
                         * np.float32(NBUCKET - max_exact)).astype(np.int32)
    large = np.minimum(large, NBUCKET - 1)
    return np.where(n < max_exact, n, large)


def _bucket_onehot_t():
    table = _t5_bucket_table().reshape(-1)
    return (np.arange(NBUCKET)[:, None] == table[None, :]).astype(np.float32)


def _swa_mask(first):
    qi = lax.broadcasted_iota(jnp.int32, (BLK, 2 * BLK), 0)
    kj = lax.broadcasted_iota(jnp.int32, (BLK, 2 * BLK), 1)
    dist = BLK + qi - kj
    window = (dist >= 0) & (dist < BLK)
    return window & ((kj >= BLK) | jnp.logical_not(first))


def _bias_expand(rel_bias_t):
    onehot = jnp.asarray(_bucket_onehot_t())

    def kern(r_ref, oh_ref, o_ref):
        o_ref[...] = _raw_dot(r_ref[...], oh_ref[...], "nn", True)

    return pl.pallas_call(
        kern, name="bias_expand", out_shape=SDS((SWA_H, BLK * 2 * BLK), F32), compiler_params=_cparams(),
    )(rel_bias_t, onehot)


def _bias_reduce(d_bias_flat):
    onehot = jnp.asarray(_bucket_onehot_t())

    def kern(d_ref, oh_ref, o_ref):
        o_ref[...] = _raw_dot(d_ref[...], oh_ref[...], "nt", True)

    return pl.pallas_call(
        kern, name="bias_reduce", out_shape=SDS((SWA_H, NBUCKET), F32), compiler_params=_cparams(),
    )(d_bias_flat, onehot)


def _swa_specs(nb, rev):
    def blk(n):
        return (nb - 1 - n) if rev else n

    def before(n):
        return jnp.maximum(blk(n) - 1, 0)

    q_spec = pl.BlockSpec((BLK, SWAW), lambda n: (blk(n), C_SQ // SWAW))
    k_cur = pl.BlockSpec((BLK, SWAKW), lambda n: (blk(n), C_SK // SWAKW))
    k_prev = pl.BlockSpec((BLK, SWAKW), lambda n: (before(n), C_SK // SWAKW))
    v_cur = pl.BlockSpec((BLK, SWAKW), lambda n: (blk(n), C_SV // SWAKW))
    v_prev = pl.BlockSpec((BLK, SWAKW), lambda n: (before(n), C_SV // SWAKW))
    bias = pl.BlockSpec((SWA_H, BLK, 2 * BLK), lambda n: (0, 0, 0))
    gain = pl.BlockSpec((1, SWA_D), lambda n: (0, 0))
    sink = pl.BlockSpec((SWA_KV, 1, SWA_G), lambda n: (0, 0, 0))
    wide = pl.BlockSpec((BLK, SWAW), lambda n: (blk(n), 0))
    narrow = pl.BlockSpec((BLK, SWAKW), lambda n: (blk(n), 0))
    return [q_spec, k_prev, k_cur, v_prev, v_cur, bias, gain, gain, sink], wide, narrow


def _split_heads(x):
    return jnp.stack([x[:, h * SWA_D:(h + 1) * SWA_D] for h in range(x.shape[1] // SWA_D)], axis=0)


def _join_heads(x):
    return jnp.concatenate([x[h] for h in range(x.shape[0])], axis=1)


def _swa_fwd(proj, bias, qg, kg, sinks):
    s_len = proj.shape[0]
    nb = s_len // BLK
    in_specs, wide, _ = _swa_specs(nb, False)

    def kern(q_ref, kp_ref, kc_ref, vp_ref, vc_ref, b_ref, qg_ref, kg_ref, s_ref, o_ref):
        mask = _swa_mask(pl.program_id(0) == 0)
        o8 = _f_swa(*[_split_heads(r[...]) for r in (q_ref, kp_ref, kc_ref, vp_ref, vc_ref)], b_ref[...], qg_ref[...],
                    kg_ref[...], s_ref[...], mask)
        o_ref[...] = _join_heads(o8).astype(BF16)

    return _pc(kern, "swa_fwd", (nb,), in_specs, wide, SDS((s_len, SWAW), BF16))(
        proj, proj, proj, proj, proj, bias, qg, kg, sinks)


def _swa_bwd(proj, bias, qg, kg, sinks, d_out):
    s_len = proj.shape[0]
    nb = s_len // BLK
    in_specs, wide, narrow = _swa_specs(nb, True)

    def kern(q_ref, kp_ref, kc_ref, vp_ref, vc_ref, b_ref, qg_ref, kg_ref, s_ref, do_ref,
             dq_ref, dk_ref, dv_ref, db_ref, dqg_ref, dkg_ref, ds_ref, carry_k, carry_v):
        n = pl.program_id(0)
        mask = _swa_mask(n == nb - 1)
        _zero_first([carry_k, carry_v, db_ref, ds_ref, dqg_ref, dkg_ref])
        fn = functools.partial(_f_swa, mask=mask)
        _, vjp = jax.vjp(fn, *[_split_heads(r[...]) for r in (q_ref, kp_ref, kc_ref, vp_ref, vc_ref)], b_ref[...],
                         qg_ref[...], kg_ref[...], s_ref[...])
        dq, dkp, dkc, dvp, dvc, dbias, dqg, dkg, dsink = vjp(_split_heads(do_ref[...]))
        dq_ref[...] = _join_heads(dq).astype(BF16)
        dk_ref[...] = (_join_heads(dkc) + carry_k[...]).astype(BF16)
        dv_ref[...] = (_join_heads(dvc) + carry_v[...]).astype(BF16)
        carry_k[...] = _join_heads(dkp)
        carry_v[...] = _join_heads(dvp)
        db_ref[...] += dbias
        dqg_ref[...] += dqg
        dkg_ref[...] += dkg
        ds_ref[...] += dsink

    bias_spec, gain, sink = in_specs[5], in_specs[6], in_specs[8]
    return _pc(
        kern, "swa_bwd", (nb,), in_specs + [wide], [wide, narrow, narrow, bias_spec, gain, gain, sink],
        [SDS((s_len, SWAW), BF16), SDS((s_len, SWAKW), BF16), SDS((s_len, SWAKW), BF16),
         SDS((SWA_H, BLK, 2 * BLK), F32), SDS((1, SWA_D), F32), SDS((1, SWA_D), F32), SDS((SWA_KV, 1, SWA_G), F32)],
        scratch=[pltpu.VMEM((BLK, SWAKW), F32), pltpu.VMEM((BLK, SWAKW), F32)],
    )(proj, proj, proj, proj, proj, bias, qg, kg, sinks, d_out)


def _branch_merge(y_dn, y_swa, wa, wb, proj):
    s_len = y_dn.shape[0]
    tm = min(512, s_len)

    def kern(ya_ref, yb_ref, wa_ref, wb_ref, ga_ref, gb_ref, pa_ref, pb_ref, m_ref):
        pa = _bdot(ya_ref[...], wa_ref[0])
        pb = _bdot(yb_ref[...], wb_ref[0])
        pa_ref[...] = pa.astype(BF16)
        pb_ref[...] = pb.astype(BF16)
        m_ref[...] = _f_merge(pa, pb, ga_ref[...], gb_ref[...]).astype(BF16)

    y_spec = pl.BlockSpec((tm, DNW), lambda i, s: (i, 0))
    w_spec = pl.BlockSpec((1, DNW, CSH), lambda i, s: (s, 0, 0))
    o_spec = pl.BlockSpec((tm, CSH), lambda i, s: (i, s))
    ga_spec = pl.BlockSpec((tm, CSH), lambda i, s: (i, C_GATE // CSH + s))
    gb_spec = pl.BlockSpec((tm, CSH), lambda i, s: (i, (C_GATE + D) // CSH + s))
    return _pc(kern, "branch_merge", (s_len // tm, N_CHIPS), [y_spec, y_spec, w_spec, w_spec, ga_spec, gb_spec],
               [o_spec] * 3, [SDS((s_len, D), BF16)] * 3,
               )(y_dn, y_swa, wa, wb, proj, proj)


def _out_proj(merged, w_out, x, gain):
    s_len = x.shape[0]
    tm = min(256, s_len)

    def kern(m_ref, w_ref, x_ref, g_ref, x1_ref, h2_ref):
        x1 = x_ref[...] + _bdot(m_ref[...], w_ref[...])
        x1_ref[...] = x1
        h2_ref[...] = _f_rms(x1, g_ref[...]).astype(BF16)

    row = pl.BlockSpec((tm, D), lambda i: (i, 0))
    return _pc(kern, "out_proj", (s_len // tm,),
               [row, pl.BlockSpec((D, D), lambda i: (0, 0)), row, pl.BlockSpec((1, D), lambda i: (0, 0))],
               [row, row], [SDS((s_len, D), F32), SDS((s_len, D), BF16)])(merged, w_out, x, gain)


def _ffn_up(h2, wg, wu):
    s_len = h2.shape[0]
    tm = min(512, s_len)

    def kern(h_ref, g_ref, u_ref, gt_ref, up_ref, act_ref):
        h = h_ref[...]
        g = _bdot(h, g_ref[0], "nt")
        u = _bdot(h, u_ref[0], "nt")
        gt_ref[0] = g.astype(BF16)
        up_ref[0] = u.astype(BF16)
        act_ref[0] = _f_swiglu(g, u).astype(BF16)

    w_spec = pl.BlockSpec((1, FSH, D), lambda s, i: (s, 0, 0))
    o_spec = pl.BlockSpec((1, tm, FSH), lambda s, i: (s, i, 0))
    shape = (N_CHIPS, s_len, FSH)
    return _pc(kern, "ffn_up", (N_CHIPS, s_len // tm), [pl.BlockSpec((tm, D), lambda s, i: (i, 0)), w_spec, w_spec],
               [o_spec] * 3, [SDS(shape, BF16)] * 3)(h2, wg, wu)


def _ffn_down_loss(act, wd, x1, target):
    s_len = x1.shape[0]
    tm = min(256, s_len)

    def kern(a_ref, w_ref, x_ref, t_ref, dy_ref, dyb_ref, loss_ref):
        _zero_first([loss_ref])
        y = x_ref[...]
        for s in range(N_CHIPS):
            y = y + _bdot(a_ref[s], w_ref[s])
        d = y - t_ref[...]
        dy = d * (1.0 / D)
        dy_ref[...] = dy
        dyb_ref[...] = dy.astype(BF16)
        loss_ref[...] += jnp.sum(d * d).reshape(1, 1) * (0.5 / D)

    row = pl.BlockSpec((tm, D), lambda i: (i, 0))
    return _pc(kern, "ffn_down_loss", (s_len // tm,),
               [pl.BlockSpec((N_CHIPS, tm, FSH), lambda i: (0, i, 0)),
                pl.BlockSpec((N_CHIPS, FSH, D), lambda i: (0, 0, 0)), row, row],
               [row, row, pl.BlockSpec((1, 1), lambda i: (0, 0))],
               [SDS((s_len, D), F32), SDS((s_len, D), BF16), SDS((1, 1), F32)])(act, wd, x1, target)


def _ffn_dact(dy_b, wd, gt, up):
    s_len = dy_b.shape[0]
    tm = min(512, s_len)

    def kern(dy_ref, w_ref, gt_ref, up_ref, dg_ref, du_ref):
        d_act = _bdot(dy_ref[...], w_ref[0], "nt")
        _, vjp = jax.vjp(_f_swiglu, gt_ref[0].astype(F32), up_ref[0].astype(F32))
        dg, du = vjp(d_act)
        dg_ref[0] = dg.astype(BF16)
        du_ref[0] = du.astype(BF16)

    a_spec = pl.BlockSpec((1, tm, FSH), lambda s, i: (s, i, 0))
    shape = (N_CHIPS, s_len, FSH)
    return _pc(kern, "ffn_dact", (N_CHIPS, s_len // tm),
               [pl.BlockSpec((tm, D), lambda s, i: (i, 0)), pl.BlockSpec((1, FSH, D), lambda s, i: (s, 0, 0)),
                a_spec, a_spec],
               [a_spec, a_spec], [SDS(shape, BF16), SDS(shape, BF16)])(dy_b, wd, gt, up)


def _gw_ffn(lhs, rhs, name):
    s_len = rhs.shape[0]
    n = len(lhs)
    tn = 512

    def kern(*refs):
        g = refs[n][...]
        for i in range(n):
            refs[n + 1 + i][0] = _bdot(refs[i][0], g, "tn").astype(BF16)

    a_spec = pl.BlockSpec((1, s_len, FSH), lambda s, j: (s, 0, 0))
    o_spec = pl.BlockSpec((1, FSH, tn), lambda s, j: (s, 0, j))
    return _pc(kern, name, (N_CHIPS, D // tn), [a_spec] * n + [pl.BlockSpec((s_len, tn), lambda s, j: (0, j))],
               [o_spec] * n, [SDS((N_CHIPS, FSH, D), BF16)] * n)(*lhs, rhs)


def _ffn_dh2(d_gt, d_up, wg, wu, x1, dy, gain):
    s_len = x1.shape[0]
    tm = min(256, s_len)

    def kern(dg_ref, du_ref, wg_ref, wu_ref, x_ref, dy_ref, g_ref, dx_ref, dxb_ref, dgain_ref):
        _zero_first([dgain_ref])
        dh2 = jnp.zeros((tm, D), F32)
        for s in range(N_CHIPS):
            dh2 = dh2 + _bdot(dg_ref[s], wg_ref[s]) + _bdot(du_ref[s], wu_ref[s])
        _, vjp = jax.vjp(_f_rms, x_ref[...], g_ref[...])
        dx, dgain = vjp(dh2)
        dx1 = dx + dy_ref[...]
        dx_ref[...] = dx1
        dxb_ref[...] = dx1.astype(BF16)
        dgain_ref[...] += dgain

    row = pl.BlockSpec((tm, D), lambda i: (i, 0))
    d_spec = pl.BlockSpec((N_CHIPS, tm, FSH), lambda i: (0, i, 0))
    w_spec = pl.BlockSpec((N_CHIPS, FSH, D), lambda i: (0, 0, 0))
    vec = pl.BlockSpec((1, D), lambda i: (0, 0))
    return _pc(kern, "ffn_dh2", (s_len // tm,), [d_spec, d_spec, w_spec, w_spec, row, row, vec],
               [row, row, vec], [SDS((s_len, D), F32), SDS((s_len, D), BF16), SDS((1, D), F32)],
               )(d_gt, d_up, wg, wu, x1, dy, gain)


def _merge_bwd(dx1_b, w_out, pa, pb, proj):
    s_len = dx1_b.shape[0]
    tm = min(256, s_len)

    def kern(dx_ref, w_ref, pa_ref, pb_ref, g_ref, dpa_ref, dpb_ref, dg_ref):
        dm = _bdot(dx_ref[...], w_ref[...], "nt")
        gates = g_ref[...]
        _, vjp = jax.vjp(_f_merge, pa_ref[...].astype(F32), pb_ref[...].astype(F32), gates[:, :D], gates[:, D:])
        dpa, dpb, dga, dgb = vjp(dm)
        dpa_ref[...] = dpa.astype(BF16)
        dpb_ref[...] = dpb.astype(BF16)
        dg_ref[:, :D] = dga.astype(BF16)
        dg_ref[:, D:] = dgb.astype(BF16)

    row = pl.BlockSpec((tm, D), lambda i: (i, 0))
    return _pc(kern, "merge_bwd", (s_len // tm,),
               [row, pl.BlockSpec((D, D), lambda i: (0, 0)), row, row,
                pl.BlockSpec((tm, 2 * D), lambda i: (i, C_GATE // (2 * D)))],
               [row, row, pl.BlockSpec((tm, 2 * D), lambda i: (i, 0))],
               [SDS((s_len, D), BF16), SDS((s_len, D), BF16), SDS((s_len, 2 * D), BF16)],
               )(dx1_b, w_out, pa, pb, proj)


def _d_branch(d_pa, d_pb, wa, wb):
    s_len = d_pa.shape[0]
    tm = min(512, s_len)

    def kern(da_ref, db_ref, wa_ref, wb_ref, oa_ref, ob_ref):
        acc_a = jnp.zeros((tm, DNW), F32)
        acc_b = jnp.zeros((tm, SWAW), F32)
        for s in range(N_CHIPS):
            acc_a = acc_a + _bdot(da_ref[:, s * CSH:(s + 1) * CSH], wa_ref[s], "nt")
            acc_b = acc_b + _bdot(db_ref[:, s * CSH:(s + 1) * CSH], wb_ref[s], "nt")
        oa_ref[...] = acc_a
        ob_ref[...] = acc_b

    row = pl.BlockSpec((tm, D), lambda i: (i, 0))
    w_spec = pl.BlockSpec((N_CHIPS, DNW, CSH), lambda i: (0, 0, 0))
    out = pl.BlockSpec((tm, DNW), lambda i: (i, 0))
    return _pc(kern, "d_branch", (s_len // tm,), [row, row, w_spec, w_spec], [out, out],
               [SDS((s_len, DNW), F32), SDS((s_len, SWAW), F32)])(d_pa, d_pb, wa, wb)


def _gw_branch(y_dn, y_swa, d_pa, d_pb):
    s_len = y_dn.shape[0]

    def kern(ya_ref, yb_ref, da_ref, db_ref, oa_ref, ob_ref):
        oa_ref[0] = _bdot(ya_ref[...], da_ref[...], "tn").astype(BF16)
        ob_ref[0] = _bdot(yb_ref[...], db_ref[...], "tn").astype(BF16)

    y_spec = pl.BlockSpec((s_len, DNW), lambda s: (0, 0))
    d_spec = pl.BlockSpec((s_len, CSH), lambda s: (0, s))
    o_spec = pl.BlockSpec((1, DNW, CSH), lambda s: (s, 0, 0))
    shape = (N_CHIPS, DNW, CSH)
    return _pc(kern, "gw_branch", (N_CHIPS,), [y_spec, y_spec, d_spec, d_spec], [o_spec, o_spec],
               [SDS(shape, BF16), SDS(shape, BF16)])(y_dn, y_swa, d_pa, d_pb)


def _dh_rms(d_proj, w_in_p, x, dx1, gain):
    s_len = x.shape[0]
    tm = min(256, s_len)

    def kern(dp_ref, w_ref, x_ref, r_ref, g_ref, gx_ref, dgain_ref):
        _zero_first([dgain_ref])
        dh = _bdot(dp_ref[...], w_ref[...], "nt")
        _, vjp = jax.vjp(_f_rms, x_ref[...], g_ref[...])
        dx, dgain = vjp(dh)
        gx_ref[...] = dx + r_ref[...]
        dgain_ref[...] += dgain

    row = pl.BlockSpec((tm, D), lambda i: (i, 0))
    vec = pl.BlockSpec((1, D), lambda i: (0, 0))
    return _pc(kern, "dh_rms", (s_len // tm,),
               [pl.BlockSpec((tm, PW), lambda i: (i, 0)), pl.BlockSpec((D, PW), lambda i: (0, 0)), row, row, vec],
               [row, vec], [SDS((s_len, D), F32), SDS((1, D), F32)])(d_proj, w_in_p, x, dx1, gain)


HALO = 8


def _conv_taps(cur_ref, prev_ref, halo, first):
    tm = cur_ref.shape[0]
    halo[0:HALO, :] = jnp.where(first, 0.0, prev_ref[...])
    halo[HALO:, :] = cur_ref[...]
    return [halo[HALO - n:HALO - n + tm, :] for n in range(CONV - 1, 0, -1)] + [cur_ref[...]]


def _dn_pre_specs(s_len, tm, blk):
    cur = pl.BlockSpec((tm, QKVW), lambda i: (blk(i), 0))
    prev = pl.BlockSpec((HALO, QKVW), lambda i: (jnp.maximum(blk(i) * (tm // HALO) - 1, 0), 0))
    ba = pl.BlockSpec((tm, 128), lambda i: (blk(i), C_BA // 128))
    row = pl.BlockSpec((tm, DNW), lambda i: (blk(i), 0))
    full = [pl.BlockSpec((CONV, QKVW), lambda i: (0, 0)), pl.BlockSpec((1, DN_H), lambda i: (0, 0)),
            pl.BlockSpec((1, DN_H), lambda i: (0, 0))]
    return cur, prev, ba, row, full


def _dn_pre_fwd(proj, conv_w, alog, dtb):
    s_len = proj.shape[0]
    tm = min(128, s_len)
    cur, prev, ba, row, full = _dn_pre_specs(s_len, tm, lambda i: i)

    def kern(cur_ref, prev_ref, ba_ref, cw_ref, al_ref, dt_ref, q_ref, k_ref, v_ref, bb_ref, gb_ref, halo):
        xs = _conv_taps(cur_ref, prev_ref, halo, pl.program_id(0) == 0)
        outs = _f_dn_pre(*xs, ba_ref[...], cw_ref[...], al_ref[...], dt_ref[...])
        for ref, val in zip((q_ref, k_ref, v_ref, bb_ref, gb_ref), outs, strict=True):
            ref[...] = val

    return _pc(kern, "dn_pre_fwd", (s_len // tm,), [cur, prev, ba] + full, [row] * 5, [SDS((s_len, DNW), F32)] * 5,
               scratch=[pltpu.VMEM((tm + HALO, QKVW), F32)])(proj, proj, proj, conv_w, alog, dtb)


def _dn_pre_bwd(proj, conv_w, alog, dtb, cots, others):
    s_len = proj.shape[0]
    tm = min(128, s_len)
    nb = s_len // tm
    cur, prev, ba, row, full = _dn_pre_specs(s_len, tm, lambda i: nb - 1 - i)
    n_o = len(others)
    assert QKVW + sum(t.shape[1] for t in others) + 128 == C_BA + 128

    def kern(cur_ref, prev_ref, ba_ref, cw_ref, al_ref, dt_ref, dq_ref, dk_ref, dv_ref, dbb_ref, dgb_ref, *rest):
        o_refs = rest[:n_o]
        dproj_ref, dcw_ref, dal_ref, ddt_ref, halo, *tails = rest[n_o:]
        i = pl.program_id(0)
        _zero_first([dcw_ref, dal_ref, ddt_ref])

        @pl.when(i == 0)
        def _():
            for t in tails:
                t[tm:, :] = jnp.zeros((HALO, QKVW), F32)

        xs = _conv_taps(cur_ref, prev_ref, halo, i == nb - 1)
        _, vjp = jax.vjp(_f_dn_pre, *xs, ba_ref[...], cw_ref[...], al_ref[...], dt_ref[...])
        *dxs, dba, dcw, dal, ddt = vjp((dq_ref[...], dk_ref[...], dv_ref[...], dbb_ref[...], dgb_ref[...]))
        total = dxs[CONV - 1]
        for j, t in enumerate(tails):
            n = CONV - 1 - j
            t[0:tm, :] = dxs[j]
            total = total + t[n:n + tm, :]
            t[tm:, :] = dxs[j][0:HALO, :]
        dproj_ref[...] = jnp.concatenate(
            [total.astype(BF16)] + [r[...] for r in o_refs] + [dba.astype(BF16), jnp.zeros((tm, PW - C_BA - 128), BF16)],
            axis=1)
        dcw_ref[...] += dcw
        dal_ref[...] += dal
        ddt_ref[...] += ddt

    o_specs = [pl.BlockSpec((tm, t.shape[1]), lambda i: (nb - 1 - i, 0)) for t in others]
    return _pc(kern, "dn_pre_bwd", (nb,), [cur, prev, ba] + full + [row] * 5 + o_specs,
               [pl.BlockSpec((tm, PW), lambda i: (nb - 1 - i, 0))] + full,
               [SDS((s_len, PW), BF16), SDS((CONV, QKVW), F32), SDS((1, DN_H), F32), SDS((1, DN_H), F32)],
               scratch=[pltpu.VMEM((tm + HALO, QKVW), F32)] * CONV)(proj, proj, proj, conv_w, alog, dtb, *cots, *others)


def _w_in_to_padded(w_sh):
    tr = 256

    def kern(w_ref, o_ref):
        full = jnp.concatenate([w_ref[s] for s in range(N_CHIPS)], axis=1)
        pieces = [full[:, o0:o0 + w] for o0, w, _ in sorted(_ORIG_PIECES, key=lambda t: t[2])]
        o_ref[...] = jnp.concatenate(pieces + [jnp.zeros((tr, PW - D_IN), w_ref.dtype)], axis=1)

    return _pc(kern, "w_in_to_padded", (D // tr,), [pl.BlockSpec((N_CHIPS, tr, D_IN // N_CHIPS), lambda i: (0, i, 0))],
               pl.BlockSpec((tr, PW), lambda i: (i, 0)), SDS((D, PW), w_sh.dtype))(w_sh)


def _padded_to_w_in(g):
    tr = 256
    csh = D_IN // N_CHIPS

    def kern(g_ref, o_ref):
        x = g_ref[...]
        full = jnp.concatenate([x[:, p0:p0 + w] for _, w, p0 in _ORIG_PIECES], axis=1)
        for s in range(N_CHIPS):
            o_ref[s] = full[:, s * csh:(s + 1) * csh]

    return _pc(kern, "padded_to_w_in", (D // tr,), [pl.BlockSpec((tr, PW), lambda i: (i, 0))],
               pl.BlockSpec((N_CHIPS, tr, csh), lambda i: (0, i, 0)), SDS((N_CHIPS, D, csh), g.dtype))(g)


def _pad_w_in(w_in):
    pieces = [w_in[:, o0:o0 + w] for o0, w, _ in sorted(_ORIG_PIECES, key=lambda t: t[2])]
    pieces.append(jnp.zeros((w_in.shape[0], PW - D_IN), w_in.dtype))
    return jnp.concatenate(pieces, axis=1)


def _unpad_w_in(g):
    return jnp.concatenate([g[:, p0:p0 + w] for _, w, p0 in _ORIG_PIECES], axis=1)


def _local_step(x, target, wts):
    s_len = x.shape[0]
    tm = min(256, s_len)
    w_in_p = wts["w_in_p"]
    attn_gain = wts["attn_norm"]
    ffn_gain = wts["ffn_norm"]
    conv_w = wts["dn_conv"]
    alog, dtb, out_gain = wts["dn_a_log"], wts["dn_dt_bias"], wts["dn_out_norm"]
    qg, kg = wts["swa_q_norm"], wts["swa_k_norm"]
    sinks = wts["swa_sinks"].reshape(SWA_KV, 1, SWA_G)

    (h,) = _rows(lambda r, f: ([_f_rms(r[0], f[0])], []), "rms1_fwd", s_len, tm, [_whole(x)], [attn_gain],
                 [(D, BF16)])
    proj = _mm(h, w_in_p, "nn", F32, 256, PW, "mm_proj")
    q_dn, k_dn, v_dn, bb, gb = _dn_pre_fwd(proj, conv_w, alog, dtb)
    o_dn, s_all, t_all = _dn_chunks_fwd(q_dn, k_dn, v_dn, gb, bb)
    post_ins = [_whole(o_dn), (proj, DNW, C_Z // DNW)]
    (y_dn,) = _rows(lambda r, f: ([_f_dn_post(r[0], r[1], f[0])], []), "dn_post_fwd", s_len, tm, post_ins,
                    [out_gain], [(DNW, BF16)])

    bias = _bias_expand(wts["rel_bias"].T).reshape(SWA_H, BLK, 2 * BLK)
    y_swa = _swa_fwd(proj, bias, qg, kg, sinks)

    wts = {**wts, **wts["late"](y_swa)}
    p_a, p_b, merged = _branch_merge(y_dn, y_swa, wts["wa"], wts["wb"], proj)
    x1, h2 = _out_proj(merged, wts["w_out"], x, ffn_gain)
    gt, up, act = _ffn_up(h2, wts["wg"], wts["wu"])
    dy, dy_b, loss = _ffn_down_loss(act, wts["wd"], x1, target)

    grads = {}
    d_gt, d_up = _ffn_dact(dy_b, wts["wd"], gt, up)
    (grads["w_down"],) = _gw_ffn([act], dy_b, "gw_down")
    grads["w_gate"], grads["w_up"] = _gw_ffn([d_gt, d_up], h2, "gw_gate_up")
    dx1, dx1_b, grads["ffn_norm"] = _ffn_dh2(d_gt, d_up, wts["wg"], wts["wu"], x1, dy, ffn_gain)
    grads["w_out"] = _mm(merged, dx1_b, "tn", BF16, 512, 512, "gw_out")
    d_pa, d_pb, d_gr = _merge_bwd(dx1_b, wts["w_out"], p_a, p_b, proj)
    d_ydn, d_yswa = _d_branch(d_pa, d_pb, wts["wa"], wts["wb"])
    grads["w_branch_dn"], grads["w_branch_swa"] = _gw_branch(y_dn, y_swa, d_pa, d_pb)
    token = wts["send_early"](grads)
    qg_t = qg + token[0:1, 0:1]
    out_gain_t = out_gain + token[0:1, 0:1]

    d_sq, d_sk, d_sv, d_bias, grads["swa_q_norm"], grads["swa_k_norm"], d_sinks = _swa_bwd(
        proj, bias, qg_t, kg, sinks, d_yswa)
    grads["swa_sinks"] = d_sinks.reshape(1, SWA_H)
    grads["rel_bias"] = _bias_reduce(d_bias.reshape(SWA_H, BLK * 2 * BLK)).T

    def post_bwd(r, f):
        _, vjp = jax.vjp(_f_dn_post, r[0], r[1], f[0])
        d_o, d_z, d_gain = vjp(r[2])
        return [d_o, d_z], [d_gain]

    d_o, d_z, grads["dn_out_norm"] = _rows(post_bwd, "dn_post_bwd", s_len, tm, post_ins + [_whole(d_ydn)], [out_gain_t],
                                           [(DNW, F32), (DNW, BF16)], [(1, DH)])
    d_q, d_k, d_v, d_gb, d_bb = _dn_chunks_bwd(q_dn, k_dn, v_dn, gb, bb, s_all, t_all, d_o)

    d_proj, grads["dn_conv"], grads["dn_a_log"], grads["dn_dt_bias"] = _dn_pre_bwd(
        proj, conv_w, alog, dtb, (d_q, d_k, d_v, d_bb, d_gb), (d_z, d_gr, d_sq, d_sk, d_sv))
    grads["w_in_p"] = _mm(h, d_proj, "tn", BF16, 512, 1024, "gw_in")
    token = wts["send_in"](grads["w_in_p"])
    grad_x, grads["attn_norm"] = _dh_rms(d_proj, w_in_p, x, dx1, attn_gain + token[0:1, 0:1])
    return loss, grad_x, grads


_HBM = pl.BlockSpec(memory_space=pl.ANY)


def _place():
    return lax.axis_index("x"), lax.axis_index("y"), lax.axis_index("c")


def _other_chips(x, y):
    return [(1 - x, y), (x, 1 - y), (1 - x, 1 - y)]


def _rcopy(src, dst, send_sems, recv_sems, k, to):
    return pltpu.make_async_remote_copy(src_ref=src, dst_ref=dst, send_sem=send_sems.at[k], recv_sem=recv_sems.at[k],
                                        device_id=to, device_id_type=MESH)


def _comm_call(body, name, ins, out_shapes, n_remote, landing=0):
    first = len(ins) - landing
    return pl.pallas_call(
        body, name=name, in_specs=[_HBM] * len(ins), out_specs=[_HBM] * len(out_shapes), out_shape=out_shapes,
        scratch_shapes=[pltpu.SemaphoreType.DMA((n_remote,)), pltpu.SemaphoreType.DMA((n_remote,))],
        input_output_aliases={first + i: i for i in range(landing)},
        compiler_params=_cparams(has_side_effects=True),
    )(*ins)


def _own_slot(blocks, chip):
    return [lax.dynamic_update_slice(lax.empty((N_CHIPS,) + b.shape, b.dtype), b[None], (chip, 0, 0)) for b in blocks]


def _gather_weights(ws, chip):
    n = len(ws)
    halves = [w.shape[0] // 2 for w in ws]

    def body(*refs):
        w_refs, o_refs = refs[:n], refs[2 * n:3 * n]
        send_sems, recv_sems = refs[3 * n:]
        x, y, c = _place()
        s = 2 * x + y
        sib = (x, y, 1 - c)
        chips = _other_chips(x, y)

        def rows(i, half):
            return pl.ds(half * halves[i], halves[i])

        first = []
        for j, (cx, cy) in enumerate(chips):
            for i in range(n):
                cp = _rcopy(w_refs[i].at[rows(i, c), :], o_refs[i].at[s, rows(i, c), :], send_sems, recv_sems,
                            j * n + i, (cx, cy, c))
                cp.start()
                first.append(cp)
        passed = []
        for j, (cx, cy) in enumerate(chips):
            sj = 2 * cx + cy
            for i in range(n):
                blk = o_refs[i].at[sj, rows(i, c), :]
                _rcopy(blk, blk, send_sems, recv_sems, j * n + i, (cx, cy, c)).wait_recv()
                cp = _rcopy(blk, blk, send_sems, recv_sems, (3 + j) * n + i, sib)
                cp.start()
                passed.append(cp)
        for j, (cx, cy) in enumerate(chips):
            sj = 2 * cx + cy
            for i in range(n):
                blk = o_refs[i].at[sj, rows(i, 1 - c), :]
                _rcopy(blk, blk, send_sems, recv_sems, (3 + j) * n + i, sib).wait_recv()
        for cp in first + passed:
            cp.wait_send()

    return _comm_call(body, "gather_weights", list(ws) + _own_slot(ws, chip),
                      [SDS((N_CHIPS,) + w.shape, w.dtype) for w in ws], 6 * n, landing=n)


_HBM_ONLY = pl.BlockSpec(memory_space=pltpu.HBM)
_SEM = pl.BlockSpec(memory_space=pltpu.SEMAPHORE)
_DATAFLOW = pltpu.SideEffectType.DATAFLOW_SIDE_EFFECTING


def _in_hbm(a):
    return pltpu.with_memory_space_constraint(a, pltpu.HBM)


def _gather_windows(blocks):
    halves = [b.shape[0] // 2 for b in blocks]

    def src_at(ref, i, c, sj):
        return ref.at[pl.ds(c * halves[i], halves[i]), :]

    def dst_at(ref, i, c, s_from):
        return ref.at[s_from, pl.ds(c * halves[i], halves[i]), :]

    return src_at, dst_at


def _exchange_windows():
    return (lambda ref, i, c, sj: ref.at[sj]), (lambda ref, i, c, s_from: ref.at[s_from])


def _split_start(name, ws, lands, dep, windows):
    n = len(ws)
    src_at, dst_at = windows

    def body(*refs):
        w_refs, l_refs = refs[:n], refs[n:2 * n]
        send_sems, recv_sems = refs[2 * n + 1], refs[2 * n + 2]
        token = refs[-1]
        x, y, c = _place()
        s = 2 * x + y
        for j, (cx, cy) in enumerate(_other_chips(x, y)):
            for i in range(n):
                _rcopy(src_at(w_refs[i], i, c, 2 * cx + cy), dst_at(l_refs[i], i, c, s), send_sems, recv_sems,
                       j * n + i, (cx, cy, c)).start()
        token[...] = jnp.zeros_like(token)

    outs = pl.pallas_call(
        body, name=name,
        out_shape=(pltpu.SemaphoreType.DMA((3 * n,)), pltpu.SemaphoreType.DMA((3 * n,)),
                   *[pltpu.HBM(w.shape, w.dtype) for w in ws], *[pltpu.HBM(t.shape, t.dtype) for t in lands],
                   SDS((8, 128), F32)),
        in_specs=[_HBM_ONLY] * (2 * n) + [pl.BlockSpec(memory_space=pl.ANY)],
        out_specs=(_SEM, _SEM, *[_HBM_ONLY] * (2 * n), pl.BlockSpec(memory_space=pltpu.VMEM)),
        input_output_aliases={i: 2 + i for i in range(2 * n)},
        compiler_params=pltpu.CompilerParams(has_side_effects=_DATAFLOW),
    )(*[_in_hbm(w) for w in ws], *[_in_hbm(t) for t in lands], dep)
    return outs[0], outs[1], outs[2:2 + n], outs[2 + n:2 + 2 * n], outs[-1]


def _split_wait(name, w_thru, l_thru, send_sems, recv_sems, after, windows):
    n = len(w_thru)
    src_at, dst_at = windows

    def body(*refs):
        w_refs, l_refs = refs[:n], refs[n:2 * n]
        send_sems, recv_sems = refs[2 * n], refs[2 * n + 1]
        x, y, c = _place()
        for j, (cx, cy) in enumerate(_other_chips(x, y)):
            sj = 2 * cx + cy
            for i in range(n):
                cp = _rcopy(src_at(w_refs[i], i, c, sj), dst_at(l_refs[i], i, c, sj), send_sems, recv_sems, j * n + i,
                            (cx, cy, c))
                cp.wait_send()
                cp.wait_recv()

    outs = pl.pallas_call(
        body, name=name,
        out_shape=[pltpu.HBM(w.shape, w.dtype) for w in w_thru] + [pltpu.HBM(t.shape, t.dtype) for t in l_thru],
        in_specs=[_HBM_ONLY] * (2 * n) + [_SEM, _SEM, pl.BlockSpec(memory_space=pl.ANY)],
        out_specs=[_HBM_ONLY] * (2 * n),
        input_output_aliases={i: i for i in range(2 * n)},
        compiler_params=pltpu.CompilerParams(has_side_effects=_DATAFLOW),
    )(*w_thru, *l_thru, send_sems, recv_sems, after)
    return outs[n:]


def _sibling_fill(lands):
    n = len(lands)
    halves = [t.shape[1] // 2 for t in lands]

    def body(*refs):
        o_refs = refs[n:2 * n]
        send_sems, recv_sems = refs[2 * n:]
        x, y, c = _place()
        sib = (x, y, 1 - c)
        chips = _other_chips(x, y)
        sent = []
        for j, (cx, cy) in enumerate(chips):
            for i in range(n):
                blk = o_refs[i].at[2 * cx + cy, pl.ds(c * halves[i], halves[i]), :]
                cp = _rcopy(blk, blk, send_sems, recv_sems, j * n + i, sib)
                cp.start()
                sent.append(cp)
        for j, (cx, cy) in enumerate(chips):
            for i in range(n):
                blk = o_refs[i].at[2 * cx + cy, pl.ds((1 - c) * halves[i], halves[i]), :]
                _rcopy(blk, blk, send_sems, recv_sems, j * n + i, sib).wait_recv()
        for cp in sent:
            cp.wait_send()

    return _comm_call(body, "sibling_fill", list(lands), [SDS(t.shape, t.dtype) for t in lands], 3 * n, landing=n)


def _swap_halves(gs, name):
    n = len(gs)
    halves = [g.shape[1] // 2 for g in gs]

    def body(*refs):
        g_refs, o_refs = refs[:n], refs[n:2 * n]
        send_sems, recv_sems = refs[2 * n:]
        x, y, c = _place()
        cps = [_rcopy(g_refs[i].at[:, pl.ds((1 - c) * halves[i], halves[i]), :], o_refs[i], send_sems, recv_sems, i,
                      (x, y, 1 - c)) for i in range(n)]
        for cp in cps:
            cp.start()
        for cp in cps:
            cp.wait()

    return _comm_call(body, name, gs, [SDS((N_CHIPS, h, g.shape[2]), g.dtype) for g, h in zip(gs, halves)], n)


def _swap_reduced(rs, name):
    n = len(rs)

    def body(*refs):
        r_refs, o_refs = refs[:n], refs[n:2 * n]
        send_sems, recv_sems = refs[2 * n:]
        x, y, c = _place()
        cps = [_rcopy(r_refs[i], o_refs[i], send_sems, recv_sems, i, (x, y, 1 - c)) for i in range(n)]
        for cp in cps:
            cp.start()
        for cp in cps:
            cp.wait()

    return _comm_call(body, name, rs, [SDS(r.shape, r.dtype) for r in rs], n)


def _all_sum_small(vec, name):
    n_dev = 8
    flips = [(bx, by, bc) for bx in (0, 1) for by in (0, 1) for bc in (0, 1)][1:]

    def body(v_ref, out_ref, gath, send_sems, recv_sems):
        x, y, c = _place()
        me = 4 * x + 2 * y + c
        gath[me] = v_ref[...]
        sent = []
        for k, (bx, by, bc) in enumerate(flips):
            peer = (x ^ bx, y ^ by, c ^ bc)
            cp = _rcopy(v_ref, gath.at[me], send_sems, recv_sems, k, peer)
            cp.start()
            sent.append(cp)
        for k, (bx, by, bc) in enumerate(flips):
            peer = (x ^ bx, y ^ by, c ^ bc)
            _rcopy(v_ref, gath.at[4 * peer[0] + 2 * peer[1] + peer[2]], send_sems, recv_sems, k, peer).wait_recv()
        for cp in sent:
            cp.wait_send()
        acc = gath[0]
        for d in range(1, n_dev):
            acc = acc + gath[d]
        out_ref[...] = acc

    vm = pl.BlockSpec(memory_space=pltpu.VMEM)
    return pl.pallas_call(
        body, name=name, in_specs=[vm], out_specs=vm, out_shape=SDS(vec.shape, F32),
        scratch_shapes=[pltpu.VMEM((n_dev,) + vec.shape, F32), pltpu.SemaphoreType.DMA((7,)),
                        pltpu.SemaphoreType.DMA((7,))],
        compiler_params=_cparams(has_side_effects=True),
    )(vec)


def _pack_small(vals, extra=None):
    parts = [vals[n].reshape(-1).astype(F32) for n, _ in _SMALL]
    parts.append(jnp.zeros((1,), F32) if extra is None else extra.reshape(1).astype(F32))
    flat = jnp.concatenate(parts)
    flat = jnp.concatenate([flat, jnp.zeros((_SMALL_ROWS * 128 - flat.shape[0],), F32)])
    return flat.reshape(_SMALL_ROWS, 128)


def _unpack_small(packed, shapes):
    flat = packed.reshape(-1)
    return {n: flat[_SMALL_OFF[n][0]:_SMALL_OFF[n][0] + _SMALL_OFF[n][1]].reshape(shapes[n]) for n, _ in _SMALL}


def _pair_sum(gs, gots, core, name):
    n = len(gs)

    def kern(c_ref, *refs):
        for i in range(n):
            refs[2 * n + i][...] = (refs[i][...].astype(F32) + refs[n + i][...].astype(F32)).astype(BF16)

    in_specs = [pl.BlockSpec((1, t.shape[1], t.shape[2]), lambda s, c_ref: (s, c_ref[0], 0)) for t in gots]
    in_specs += [pl.BlockSpec((1, t.shape[1], t.shape[2]), lambda s, c_ref: (s, 0, 0)) for t in gots]
    out_specs = [pl.BlockSpec((1, t.shape[1], t.shape[2]), lambda s, c_ref: (s, 0, 0)) for t in gots]
    return pl.pallas_call(
        kern, name=name,
        grid_spec=pltpu.PrefetchScalarGridSpec(num_scalar_prefetch=1, grid=(N_CHIPS,), in_specs=in_specs,
                                               out_specs=out_specs),
        out_shape=[SDS(t.shape, BF16) for t in gots],
        compiler_params=_cparams(dimension_semantics=("arbitrary",)),
    )(core.reshape(1).astype(jnp.int32), *gs, *gots)


def _chip_sum(qs, name):
    n = len(qs)

    def kern(*refs):
        for i in range(n):
            acc = refs[i][0].astype(F32)
            for s in range(1, N_CHIPS):
                acc = acc + refs[i][s].astype(F32)
            refs[n + i][...] = acc

    in_specs = [pl.BlockSpec((N_CHIPS, q.shape[1] // 2, q.shape[2]), lambda j: (0, j, 0)) for q in qs]
    out_specs = [pl.BlockSpec((q.shape[1] // 2, q.shape[2]), lambda j: (j, 0)) for q in qs]
    return _pc(kern, name, (2,), in_specs, out_specs, [SDS(q.shape[1:], F32) for q in qs])(*qs)


def _adam_math(w_, g_, m_, v_):
    m_ = ADAM_B1 * m_ + (1.0 - ADAM_B1) * g_
    v_ = ADAM_B2 * v_ + (1.0 - ADAM_B2) * jnp.square(g_)
    m_hat = m_ / (1.0 - ADAM_B1 ** ADAM_STEP)
    v_hat = v_ / (1.0 - ADAM_B2 ** ADAM_STEP)
    return -ADAM_LR * (m_hat / (jnp.sqrt(v_hat) + ADAM_EPS) + ADAM_WD * w_), m_, v_


def _adamw(w, g, m, v, name):
    rows, cols = w.shape
    tr = rows
    for cand in (256, 128, 64, 32, 16, 8):
        if rows % cand == 0 and rows > cand:
            tr = cand
            break

    def kern(w_ref, g_ref, m_ref, v_ref, d_ref, nm_ref, nv_ref):
        d_ref[...], nm_ref[...], nv_ref[...] = _adam_math(w_ref[...], g_ref[...], m_ref[...], v_ref[...])

    spec = pl.BlockSpec((tr, cols), lambda i: (i, 0))
    return _pc(kern, name, (rows // tr,), [spec] * 4, [spec] * 3, [SDS(w.shape, F32)] * 3)(w, g, m, v)


def _adamw_rows1(w, g, m, v, name):
    rows, _, cols = w.shape
    tr = next(t for t in (42, 32, 29, 16, 8, 7, 6, 4, 3, 2, 1) if rows % t == 0)

    def kern(w_ref, g_ref, m_ref, v_ref, go_ref, d_ref, nm_ref, nv_ref):
        g_ = g_ref[...]
        go_ref[...] = g_
        d_ref[...], nm_ref[...], nv_ref[...] = _adam_math(w_ref[...], g_, m_ref[...], v_ref[...])

    spec = pl.BlockSpec((tr, 1, cols), lambda i: (i, 0, 0))
    return _pc(kern, name, (rows // tr,), [spec] * 4, [spec] * 4, [SDS(w.shape, F32)] * 4)(w, g, m, v)


def _adamw_big(w, mine, theirs, m, v, core, name):
    _, rows, cols = w.shape
    half = rows // 2
    tr = next(t for t in (256, 176, 128, 64, 32, 16, 8) if half % t == 0)
    nbh = half // tr

    def kern(c_ref, w_ref, a_ref, b_ref, m_ref, v_ref, g_ref, d_ref, nm_ref, nv_ref):
        g_ = jnp.where(pl.program_id(0) // nbh == c_ref[0], a_ref[...], b_ref[...])
        g_ref[0] = g_
        d_ref[0], nm_ref[0], nv_ref[0] = _adam_math(w_ref[0], g_, m_ref[0], v_ref[0])

    full = pl.BlockSpec((1, tr, cols), lambda i, c_ref: (0, i, 0))
    part = pl.BlockSpec((tr, cols), lambda i, c_ref: (i % nbh, 0))
    return pl.pallas_call(
        kern, name=name,
        grid_spec=pltpu.PrefetchScalarGridSpec(num_scalar_prefetch=1, grid=(rows // tr,),
                                               in_specs=[full, part, part, full, full], out_specs=[full] * 4),
        out_shape=[SDS(w.shape, F32)] * 4,
        compiler_params=_cparams(dimension_semantics=("arbitrary",)),
    )(core.reshape(1).astype(jnp.int32), w, mine, theirs, m, v)


_WEIGHT_NAMES = ("attn_norm", "w_in", "dn_conv", "dn_a_log", "dn_dt_bias", "dn_out_norm", "swa_q_norm", "swa_k_norm",
                 "swa_sinks", "rel_bias", "w_branch_dn", "w_branch_swa", "w_out", "ffn_norm", "w_gate", "w_up",
                 "w_down")
_CONV_SH = QKVW // N_CHIPS


def kernel(x, attn_norm, w_in, dn_conv, dn_a_log, dn_dt_bias, dn_out_norm, swa_q_norm, swa_k_norm, swa_sinks, rel_bias, w_branch_dn, w_branch_swa, w_out, ffn_norm, w_gate, w_up, w_down, loss_target, m_attn_norm, m_w_in, m_dn_conv, m_dn_a_log, m_dn_dt_bias, m_dn_out_norm, m_swa_q_norm, m_swa_k_norm, m_swa_sinks, m_rel_bias, m_w_branch_dn, m_w_branch_swa, m_w_out, m_ffn_norm, m_w_gate, m_w_up, m_w_down, v_attn_norm, v_w_in, v_dn_conv, v_dn_a_log, v_dn_dt_bias, v_dn_out_norm, v_swa_q_norm, v_swa_k_norm, v_swa_sinks, v_rel_bias, v_w_branch_dn, v_w_branch_swa, v_w_out, v_ffn_norm, v_w_gate, v_w_up, v_w_down):
    w = dict(attn_norm=attn_norm, w_in=w_in, dn_conv=dn_conv, dn_a_log=dn_a_log, dn_dt_bias=dn_dt_bias,
             dn_out_norm=dn_out_norm, swa_q_norm=swa_q_norm, swa_k_norm=swa_k_norm, swa_sinks=swa_sinks,
             rel_bias=rel_bias, w_branch_dn=w_branch_dn, w_branch_swa=w_branch_swa, w_out=w_out, ffn_norm=ffn_norm,
             w_gate=w_gate, w_up=w_up, w_down=w_down)
    m = dict(attn_norm=m_attn_norm, w_in=m_w_in, dn_conv=m_dn_conv, dn_a_log=m_dn_a_log, dn_dt_bias=m_dn_dt_bias,
             dn_out_norm=m_dn_out_norm, swa_q_norm=m_swa_q_norm, swa_k_norm=m_swa_k_norm, swa_sinks=m_swa_sinks,
             rel_bias=m_rel_bias, w_branch_dn=m_w_branch_dn, w_branch_swa=m_w_branch_swa, w_out=m_w_out,
             ffn_norm=m_ffn_norm, w_gate=m_w_gate, w_up=m_w_up, w_down=m_w_down)
    v = dict(attn_norm=v_attn_norm, w_in=v_w_in, dn_conv=v_dn_conv, dn_a_log=v_dn_a_log, dn_dt_bias=v_dn_dt_bias,
             dn_out_norm=v_dn_out_norm, swa_q_norm=v_swa_q_norm, swa_k_norm=v_swa_k_norm, swa_sinks=v_swa_sinks,
             rel_bias=v_rel_bias, w_branch_dn=v_w_branch_dn, w_branch_swa=v_w_branch_swa, w_out=v_w_out,
             ffn_norm=v_ffn_norm, w_gate=v_w_gate, w_up=v_w_up, w_down=v_w_down)
    shapes = {n: w[n].shape for n in _WEIGHT_NAMES}

    def two_d(a):
        return a.reshape(a.shape[-2], a.shape[-1]) if a.ndim == 3 else a

    core = lax.axis_index("c")
    chip = 2 * lax.axis_index("x") + lax.axis_index("y")
    small_shapes = {n: two_d(w[n]).shape for n, _ in _SMALL}
    small_shapes["dn_conv"] = (CONV, QKVW)

    conv_loc = two_d(w["dn_conv"])
    conv_part = lax.dynamic_update_slice(jnp.zeros((CONV, QKVW), F32), jnp.where(core == 0, conv_loc, 0.0),
                                         (0, chip * _CONV_SH))
    conv_full = _all_sum_small(conv_part.reshape(CONV * QKVW // 128, 128), "gather_conv").reshape(CONV, QKVW)

    flipped = ("w_gate", "w_up")

    def natural(a, n):
        return a.transpose(0, 2, 1) if n in flipped else a

    w_bf = [two_d(natural(w[n], n).astype(BF16)) for n in _BIG_NAMES]
    (w_in_g,) = _gather_weights(w_bf[:1], chip)
    windows = _gather_windows(w_bf[1:])
    after_sync = w_in_g[0, :8, :128].astype(F32) + conv_full[0:1, :128]
    send_sems, recv_sems, w_thru, l_thru, token = _split_start(
        "gather_start", w_bf[1:], _own_slot(w_bf[1:], chip), after_sync, windows)

    def late(after):
        lands = _split_wait("gather_wait", w_thru, l_thru, send_sems, recv_sems, after, windows)
        g = dict(zip(_BIG_NAMES[1:], _sibling_fill(lands)))
        return dict(wa=g["w_branch_dn"], wb=g["w_branch_swa"], w_out=g["w_out"].reshape(D, D), wg=g["w_gate"],
                    wu=g["w_up"], wd=g["w_down"])

    wts = dict(w_in_p=_w_in_to_padded(w_in_g), dn_conv=conv_full, late=late)
    for n, _ in _SMALL[:-1]:
        wts[n] = two_d(w[n])
    wts["attn_norm"] = wts["attn_norm"] + token[0:1, 0:1]

    early = {}

    def send_early(grads):
        gs = [grads["w_branch_dn"], grads["w_branch_swa"], grads["w_out"].reshape(N_CHIPS, CSH, D), grads["w_gate"],
              grads["w_up"], grads["w_down"]]
        parts = _pair_sum(gs, _swap_halves(gs, "swap_halves_early"), core, "pair_sum_early")
        own = [lax.dynamic_index_in_dim(p, chip, axis=0, keepdims=False) for p in parts]
        early["sems"], early["recv"], early["src"], early["land"], tok = _split_start(
            "exchange_start", parts, _own_slot(own, chip), parts[0][0, :8, :128], _exchange_windows())
        return tok

    last = {}

    def send_in(g_in_p):
        g_in = [_padded_to_w_in(g_in_p)]
        parts = _pair_sum(g_in, _swap_halves(g_in, "swap_halves_in"), core, "pair_sum_in")
        own = [lax.dynamic_index_in_dim(p, chip, axis=0, keepdims=False) for p in parts]
        last["sems"], last["recv"], last["src"], last["land"], tok = _split_start(
            "exchange_in_start", parts, _own_slot(own, chip), parts[0][0, :8, :128], _exchange_windows())
        return tok

    wts["send_early"] = send_early
    wts["send_in"] = send_in
    loss_sum, grad_x, grads = _local_step(x[0], loss_target[0], wts)

    small_sum = _all_sum_small(_pack_small(grads, loss_sum), "all_sum_small")
    loss = small_sum.reshape(-1)[_LOSS_OFF]
    g_small = _unpack_small(small_sum, small_shapes)

    q_early = _split_wait("exchange_wait", early["src"], early["land"], early["sems"], early["recv"], small_sum,
                          _exchange_windows())
    red_early = _chip_sum(list(q_early), "chip_sum_early")
    their_early = _swap_reduced(red_early, "swap_reduced_early")
    g_out, d_out, m_out, v_out = {}, {}, {}, {}
    for n, mine, other in zip(_BIG_NAMES[1:], red_early, their_early):
        res = _adamw_big(natural(w[n], n), mine, other, natural(m[n], n), natural(v[n], n), core, "adamw_" + n)
        g_out[n], d_out[n], m_out[n], v_out[n] = (natural(t, n) for t in res)

    q_in = _split_wait("exchange_in_wait", last["src"], last["land"], last["sems"], last["recv"],
                       d_out[_BIG_NAMES[-1]], _exchange_windows())
    reduced = _chip_sum(list(q_in), "chip_sum_in")
    theirs = _swap_reduced(reduced, "swap_reduced_in")

    def rows1(a):
        return a.transpose(2, 0, 1)

    def unrows1(a):
        return a.transpose(1, 2, 0)

    g_in_blk = jnp.concatenate([jnp.where(core == 0, reduced[0], theirs[0]),
                                jnp.where(core == 0, theirs[0], reduced[0])], axis=0)
    g_in_r = rows1(g_in_blk[None])
    res = _adamw_rows1(rows1(w["w_in"]), g_in_r, rows1(m["w_in"]), rows1(v["w_in"]), "adamw_w_in")
    g_out["w_in"], d_out["w_in"], m_out["w_in"], v_out["w_in"] = (unrows1(t) for t in res)
    g_conv = lax.dynamic_slice(g_small["dn_conv"], (0, chip * _CONV_SH), (CONV, _CONV_SH))
    g_out["dn_conv"] = g_conv.reshape(shapes["dn_conv"])
    d_, m_, v_ = _adamw(conv_loc, g_conv, two_d(m["dn_conv"]), two_d(v["dn_conv"]), "adamw_dn_conv")
    d_out["dn_conv"], m_out["dn_conv"], v_out["dn_conv"] = (t.reshape(shapes["dn_conv"]) for t in (d_, m_, v_))

    def packed(src):
        vals = {n: src[n] for n, _ in _SMALL[:-1]}
        vals["dn_conv"] = jnp.zeros((CONV * QKVW,), F32)
        return _pack_small(vals)

    d_s, m_s, v_s = _adamw(packed(w), small_sum, packed(m), packed(v), "adamw_small")
    d_small, m_small, v_small = (_unpack_small(t, small_shapes) for t in (d_s, m_s, v_s))
    for n, _ in _SMALL[:-1]:
        g_out[n] = g_small[n].reshape(shapes[n])
        d_out[n], m_out[n], v_out[n] = (t[n].reshape(shapes[n]) for t in (d_small, m_small, v_small))

    return (loss, grad_x[None], *[g_out[n] for n in _WEIGHT_NAMES], *[d_out[n] for n in _WEIGHT_NAMES],
            *[m_out[n] for n in _WEIGHT_NAMES], *[v_out[n] for n in _WEIGHT_NAMES])
```

```python
import functools
import math

import numpy as np
import jax
import jax.numpy as jnp
from jax import lax
from jax.experimental import pallas as pl
from jax.experimental.pallas import tpu as pltpu

F32 = jnp.float32
BF16 = jnp.bfloat16
SDS = jax.ShapeDtypeStruct

D = 1024
DN_H = 4
DH = 128
DNW = DN_H * DH
QKVW = 3 * DNW
CONV = 4
CHUNK = 64
SWA_H = 8
SWA_KV = 2
SWA_G = SWA_H // SWA_KV
SWA_D = 64
SWAW = SWA_H * SWA_D
SWAKW = SWA_KV * SWA_D
BLK = 128
NBUCKET = 32
MAXDIST = 128
DFF = 2816
D_IN = QKVW + DNW + 2 * DN_H + SWAW + 2 * SWAKW + 2 * D
EPS = 1e-6
NEG = -1e30

ADAM_LR = 0.001
ADAM_B1 = 0.9
ADAM_B2 = 0.999
ADAM_EPS = 1e-08
ADAM_WD = 0.01
ADAM_STEP = 10

C_QKV, C_Z, C_GATE, C_SQ, C_SK, C_SV, C_BA = 0, 1536, 2048, 4096, 4608, 4736, 4864
PW = 5120
_ORIG_PIECES = (
    (0, QKVW, C_QKV),
    (QKVW, DNW, C_Z),
    (QKVW + DNW, 2 * DN_H, C_BA),
    (QKVW + DNW + 2 * DN_H, SWAW, C_SQ),
    (QKVW + DNW + 2 * DN_H + SWAW, SWAKW, C_SK),
    (QKVW + DNW + 2 * DN_H + SWAW + SWAKW, SWAKW, C_SV),
    (QKVW + DNW + 2 * DN_H + SWAW + 2 * SWAKW, 2 * D, C_GATE),
)

N_CHIPS = 4
FSH = DFF // N_CHIPS
CSH = D // N_CHIPS
VMEM_LIMIT = 48 * 1024 * 1024
MESH = pl.DeviceIdType.MESH

_BIG = (
    ("w_in", D, D_IN // N_CHIPS),
    ("w_branch_dn", DNW, CSH),
    ("w_branch_swa", SWAW, CSH),
    ("w_out", CSH, D),
    ("w_gate", FSH, D),
    ("w_up", FSH, D),
    ("w_down", FSH, D),
)
_BIG_NAMES = tuple(n for n, _, _ in _BIG)

_SMALL = (
    ("attn_norm", D), ("ffn_norm", D), ("dn_out_norm", DH), ("swa_q_norm", SWA_D), ("swa_k_norm", SWA_D),
    ("swa_sinks", SWA_H), ("dn_a_log", DN_H), ("dn_dt_bias", DN_H), ("rel_bias", NBUCKET * SWA_H),
    ("dn_conv", CONV * QKVW),
)
_SMALL_OFF = {}
_o = 0
for _n, _s in _SMALL:
    _SMALL_OFF[_n] = (_o, _s)
    _o += _s
_LOSS_OFF = _o
_SMALL_ROWS = -(-(_o + 1) // (8 * 128)) * 8


def _cparams(**kw):
    return pltpu.CompilerParams(vmem_limit_bytes=VMEM_LIMIT, **kw)


_DIMS = {
    "nn": (((1,), (0,)), ((), ())),
    "nt": (((1,), (1,)), ((), ())),
    "tn": (((0,), (0,)), ((), ())),
    "bnn": (((2,), (1,)), ((0,), (0,))),
    "bnt": (((2,), (2,)), ((0,), (0,))),
    "btn": (((1,), (1,)), ((0,), (0,))),
}


def _raw_dot(a, b, kind, exact):
    if exact:
        prec = lax.Precision.HIGH if exact == "x3" else lax.Precision.HIGHEST
        return lax.dot_general(a, b, _DIMS[kind], precision=prec, preferred_element_type=F32)
    return lax.dot_general(a.astype(BF16), b.astype(BF16), _DIMS[kind], preferred_element_type=F32)


@functools.partial(jax.custom_vjp, nondiff_argnums=(2, 3))
def _dot(a, b, kind, exact):
    return _raw_dot(a, b, kind, exact)


def _dot_fwd(a, b, kind, exact):
    return _raw_dot(a, b, kind, exact), (a, b)


def _dot_bwd(kind, exact, res, g):
    a, b = res
    pre = kind[:-2]
    nn, nt, tn = pre + "nn", pre + "nt", pre + "tn"
    if kind == nn:
        return _dot(g, b, nt, exact), _dot(a, g, tn, exact)
    if kind == nt:
        return _dot(g, b, nn, exact), _dot(g, a, tn, exact)
    return _dot(b, g, nt, exact), _dot(a, g, nn, exact)


_dot.defvjp(_dot_fwd, _dot_bwd)


def _silu(x):
    return x * jax.nn.sigmoid(x)


def _f_rms(x, gain):
    return x * lax.rsqrt(jnp.mean(x * x, axis=-1, keepdims=True) + EPS) * gain


def _f_dn_pre(xs0, xs1, xs2, xs3, ba, cw, alog, dtb):
    rows = xs0.shape[0]
    c = xs0 * cw[0:1] + xs1 * cw[1:2] + xs2 * cw[2:3] + xs3 * cw[3:4]
    qkv = _silu(c)
    qs, ks, bbs, gbs = [], [], [], []
    for h in range(DN_H):
        qh = qkv[:, h * DH:(h + 1) * DH]
        kh = qkv[:, DNW + h * DH:DNW + (h + 1) * DH]
        qs.append(qh * lax.rsqrt(jnp.sum(qh * qh, axis=-1, keepdims=True) + EPS) * (DH ** -0.5))
        ks.append(kh * lax.rsqrt(jnp.sum(kh * kh, axis=-1, keepdims=True) + EPS))
        beta = jax.nn.sigmoid(ba[:, h:h + 1])
        ar = ba[:, DN_H + h:DN_H + h + 1] + dtb[:, h:h + 1]
        softplus = jnp.maximum(ar, 0.0) + jnp.log1p(jnp.exp(-jnp.abs(ar)))
        g = -jnp.exp(alog[:, h:h + 1]) * softplus
        bbs.append(jnp.broadcast_to(beta, (rows, DH)))
        gbs.append(jnp.broadcast_to(g, (rows, DH)))
    return (jnp.concatenate(qs, axis=1), jnp.concatenate(ks, axis=1), qkv[:, 2 * DNW:],
            jnp.concatenate(bbs, axis=1), jnp.concatenate(gbs, axis=1))


def _f_dn_post(o, z, gain):
    ys = []
    for h in range(DN_H):
        oh = o[:, h * DH:(h + 1) * DH]
        zh = z[:, h * DH:(h + 1) * DH]
        ys.append(oh * lax.rsqrt(jnp.mean(oh * oh, axis=-1, keepdims=True) + EPS) * gain * _silu(zh))
    return jnp.concatenate(ys, axis=1)


def _f_merge(pa, pb, ga, gb):
    return jax.nn.sigmoid(ga) * pa + jax.nn.sigmoid(gb) * pb


def _f_swiglu(g, u):
    return _silu(g) * u


@jax.custom_vjp
def _unit_lower_inverse(a):
    c = a.shape[-1]
    eye = (lax.broadcasted_iota(jnp.int32, a.shape, 1) == lax.broadcasted_iota(jnp.int32, a.shape, 2)).astype(F32)
    p = -a
    t = eye + p
    for _ in range(max(c.bit_length() - 2, 0)):
        p = _raw_dot(p, p, "bnn", "x3")
        t = t + _raw_dot(t, p, "bnn", "x3")
    return t


def _unit_lower_inverse_fwd(a):
    t = _unit_lower_inverse(a)
    return t, t


def _unit_lower_inverse_bwd(t, g):
    return (-_raw_dot(_raw_dot(t, g, "btn", "x3"), t, "bnt", "x3"),)


_unit_lower_inverse.defvjp(_unit_lower_inverse_fwd, _unit_lower_inverse_bwd)


@jax.custom_vjp
def _known_inverse(a, t):
    return t


def _known_inverse_fwd(a, t):
    return t, t


def _known_inverse_bwd(t, g):
    return _unit_lower_inverse_bwd(t, g)[0], jnp.zeros_like(t)


_known_inverse.defvjp(_known_inverse_fwd, _known_inverse_bwd)


def _f_chunk(q, k, v, gb, bb, s, t_known=None, with_t=False):
    c = CHUNK
    nh = q.shape[0]
    ii = lax.broadcasted_iota(jnp.int32, (nh, c, c), 1)
    jj = lax.broadcasted_iota(jnp.int32, (nh, c, c), 2)
    incl = ii >= jj
    strict = ii > jj
    eye = (ii == jj).astype(F32)
    gcb = _dot(incl.astype(F32), gb, "bnn", "x3")
    lane0 = (lax.broadcasted_iota(jnp.int32, (nh, c, DH), 2) == 0).astype(F32)
    gcol = gcb[:, :, :c]
    grow = _dot(lane0, gcb, "bnt", "x3")
    decay = jnp.where(incl, jnp.exp(jnp.where(incl, gcol - grow, 0.0)), 0.0)
    kb = k * bb
    vb = v * bb
    a = jnp.where(strict, _dot(kb, k, "bnt", False) * decay, 0.0)
    t = _unit_lower_inverse(a) if t_known is None else _known_inverse(a, t_known)
    eg = jnp.exp(gcb)
    u = _dot(t, vb, "bnn", "x3")
    w = _dot(t, kb * eg, "bnn", "x3")
    qk = jnp.where(incl, _dot(q, k, "bnt", False) * decay, 0.0)
    qe = q * eg
    glast = gcb[:, c - 1:c, :]
    k_dec = k * jnp.exp(glast - gcb)
    e_last = jnp.exp(glast)
    outs = []
    for g in range(nh // DN_H):
        sl = slice(g * DN_H, (g + 1) * DN_H)
        v_new = u[sl] - _dot(w[sl], s, "bnn", False)
        outs.append(_dot(qe[sl], s, "bnn", False) + _dot(qk[sl], v_new, "bnn", False))
        s = s * e_last[sl] + _dot(k_dec[sl], v_new, "btn", False)
    o = jnp.concatenate(outs, axis=0)
    return (o, s, t) if with_t else (o, s)


def _f_swa(q8, kp, kc, vp, vc, bias8, qg, kg, sink, mask):
    kb = jnp.concatenate([kp, kc], axis=1)
    vb = jnp.concatenate([vp, vc], axis=1)
    kn = kb * lax.rsqrt(jnp.mean(kb * kb, axis=-1, keepdims=True) + EPS) * kg

    def rows(per_head):
        return jnp.stack([jnp.concatenate([per_head(kv, g) for g in range(SWA_G)], axis=0)
                          for kv in range(SWA_KV)], axis=0)

    qq = rows(lambda kv, g: q8[kv * SWA_G + g])
    qn = qq * lax.rsqrt(jnp.mean(qq * qq, axis=-1, keepdims=True) + EPS) * qg
    lg = _dot(qn, kn, "bnt", False) * (SWA_D ** -0.5) + rows(lambda kv, g: bias8[kv * SWA_G + g])
    lg = jnp.where(rows(lambda kv, g: mask), lg, NEG)
    sk = rows(lambda kv, g: jnp.broadcast_to(sink[kv][:, g:g + 1], (BLK, 1)))
    m = lax.stop_gradient(jnp.maximum(jnp.max(lg, axis=-1, keepdims=True), sk))
    p = jnp.exp(lg - m)
    den = jnp.sum(p, axis=-1, keepdims=True) + jnp.exp(sk - m)
    out = _dot(p * (1.0 / den), vb, "bnn", False)
    return jnp.stack([out[kv, g * BLK:(g + 1) * BLK] for kv in range(SWA_KV) for g in range(SWA_G)], axis=0)


def _bdot(a, b, kind="nn"):
    return lax.dot_general(a.astype(BF16), b.astype(BF16), _DIMS[kind], preferred_element_type=F32)


def _pc(kern, name, grid, in_specs, out_specs, out_shape, scratch=()):
    return pl.pallas_call(
        kern, name=name, grid=grid, in_specs=in_specs, out_specs=out_specs, out_shape=out_shape,
        scratch_shapes=list(scratch), compiler_params=_cparams(dimension_semantics=("arbitrary",) * len(grid)))


def _mm(a, b, kind, out_dtype, tm, tn, name):
    if kind == "tn":
        k, m = a.shape
    else:
        m, k = a.shape
    n = b.shape[0] if kind == "nt" else b.shape[1]
    tm, tn = min(tm, m), min(tn, n)
    assert m % tm == 0 and n % tn == 0, (name, a.shape, b.shape, tm, tn)

    def kern(a_ref, b_ref, o_ref):
        o_ref[...] = _bdot(a_ref[...], b_ref[...], kind).astype(o_ref.dtype)

    a_spec = pl.BlockSpec((k, tm), lambda i, j: (0, i)) if kind == "tn" else pl.BlockSpec((tm, k), lambda i, j: (i, 0))
    b_spec = pl.BlockSpec((tn, k), lambda i, j: (j, 0)) if kind == "nt" else pl.BlockSpec((k, tn), lambda i, j: (0, j))
    return _pc(kern, name, (m // tm, n // tn), [a_spec, b_spec], pl.BlockSpec((tm, tn), lambda i, j: (i, j)),
               SDS((m, n), out_dtype))(a, b)


def _rows(body, name, m, tm, row_ins, full_ins, row_outs, acc_outs=()):
    n_r, n_f, n_o, n_a = len(row_ins), len(full_ins), len(row_outs), len(acc_outs)
    assert m % tm == 0

    def kern(*refs):
        r = refs[:n_r]
        f = refs[n_r:n_r + n_f]
        o = refs[n_r + n_f:n_r + n_f + n_o]
        acc = refs[n_r + n_f + n_o:]
        outs, sums = body([x[...] for x in r], [x[...] for x in f])
        for ref, val in zip(o, outs, strict=True):
            ref[...] = val.astype(ref.dtype)
        if n_a:
            @pl.when(pl.program_id(0) == 0)
            def _():
                for ref in acc:
                    ref[...] = jnp.zeros(ref.shape, F32)

            for ref, val in zip(acc, sums, strict=True):
                ref[...] += val

    in_specs = [pl.BlockSpec((tm, w), functools.partial(lambda i, cb: (i, cb), cb=cb)) for _, w, cb in row_ins]
    in_specs += [pl.BlockSpec(x.shape, lambda i: (0, 0)) for x in full_ins]
    out_specs = [pl.BlockSpec((tm, w), lambda i: (i, 0)) for w, _ in row_outs]
    out_specs += [pl.BlockSpec(s, lambda i: (0, 0)) for s in acc_outs]
    out_shape = [SDS((m, w), dt) for w, dt in row_outs]
    out_shape += [SDS(s, F32) for s in acc_outs]
    return _pc(kern, name, (m // tm,), in_specs, out_specs, out_shape)(*[x for x, _, _ in row_ins], *full_ins)


def _whole(x):
    return (x, x.shape[1], 0)


def _zero_first(refs):
    @pl.when(pl.program_id(0) == 0)
    def _():
        for ref in refs:
            ref[...] = jnp.zeros(ref.shape, F32)


GROUP = 4


def _heads(ref):
    return jnp.stack([ref[g * CHUNK:(g + 1) * CHUNK, h * DH:(h + 1) * DH]
                      for g in range(GROUP) for h in range(DN_H)], axis=0)


def _unheads(ref, val):
    for g in range(GROUP):
        for h in range(DN_H):
            ref[g * CHUNK:(g + 1) * CHUNK, h * DH:(h + 1) * DH] = val[g * DN_H + h]


def _dn_chunks_fwd(q, k, v, gb, bb):
    s_len = q.shape[0]
    ng = s_len // (GROUP * CHUNK)

    def kern(q_ref, k_ref, v_ref, g_ref, b_ref, o_ref, sall_ref, t_ref, state):
        _zero_first([state])
        s = state[...]
        sall_ref[0] = s
        o, s_new, t = _f_chunk(*[_heads(r) for r in (q_ref, k_ref, v_ref, g_ref, b_ref)], s, with_t=True)
        _unheads(o_ref, o)
        t_ref[0] = t
        state[...] = s_new

    blk = pl.BlockSpec((GROUP * CHUNK, DNW), lambda c: (c, 0))
    return _pc(kern, "dn_chunks_fwd", (ng,), [blk] * 5,
               [blk, pl.BlockSpec((1, DN_H, DH, DH), lambda c: (c, 0, 0, 0)),
                pl.BlockSpec((1, GROUP * DN_H, CHUNK, CHUNK), lambda c: (c, 0, 0, 0))],
               [SDS((s_len, DNW), F32), SDS((ng, DN_H, DH, DH), F32), SDS((ng, GROUP * DN_H, CHUNK, CHUNK), F32)],
               scratch=[pltpu.VMEM((DN_H, DH, DH), F32)])(q, k, v, gb, bb)


def _dn_chunks_bwd(q, k, v, gb, bb, s_all, t_all, d_o):
    s_len = q.shape[0]
    ng = s_len // (GROUP * CHUNK)

    def kern(q_ref, k_ref, v_ref, g_ref, b_ref, sall_ref, t_ref, do_ref, dq_ref, dk_ref, dv_ref, dg_ref, db_ref,
             dstate):
        _zero_first([dstate])
        fn = functools.partial(_f_chunk, t_known=t_ref[0])
        _, vjp = jax.vjp(fn, *[_heads(r) for r in (q_ref, k_ref, v_ref, g_ref, b_ref)], sall_ref[0])
        *d_ins, ds = vjp((_heads(do_ref), dstate[...]))
        for ref, val in zip((dq_ref, dk_ref, dv_ref, dg_ref, db_ref), d_ins, strict=True):
            _unheads(ref, val)
        dstate[...] = ds

    blk = pl.BlockSpec((GROUP * CHUNK, DNW), lambda c: (ng - 1 - c, 0))
    return _pc(kern, "dn_chunks_bwd", (ng,),
               [blk] * 5 + [pl.BlockSpec((1, DN_H, DH, DH), lambda c: (ng - 1 - c, 0, 0, 0)),
                            pl.BlockSpec((1, GROUP * DN_H, CHUNK, CHUNK), lambda c: (ng - 1 - c, 0, 0, 0)), blk],
               [blk] * 5, [SDS((s_len, DNW), F32)] * 5,
               scratch=[pltpu.VMEM((DN_H, DH, DH), F32)])(q, k, v, gb, bb, s_all, t_all, d_o)


def _t5_bucket_table():
    qi = np.arange(BLK)[:, None]
    kj = np.arange(2 * BLK)[None, :]
    dist = BLK + qi - kj
    n = np.maximum(dist, 0)
    max_exact = NBUCKET // 2
    nf = np.maximum(n, 1).astype(np.float32)
    large = max_exact + (np.log(nf / np.float32(max_exact)) / np.float32(math.log(MAXDIST / max_exact))
                         * np.float32(NBUCKET - max_exact)).astype(np.int32)
    large = np.minimum(large, NBUCKET - 1)
    return np.where(n < max_exact, n, large)


def _bucket_onehot_t():
    table = _t5_bucket_table().reshape(-1)
    return (np.arange(NBUCKET)[:, None] == table[None, :]).astype(np.float32)


def _swa_mask(first):
    qi = lax.broadcasted_iota(jnp.int32, (BLK, 2 * BLK), 0)
    kj = lax.broadcasted_iota(jnp.int32, (BLK, 2 * BLK), 1)
    dist = BLK + qi - kj
    window = (dist >= 0) & (dist < BLK)
    return window & ((kj >= BLK) | jnp.logical_not(first))


def _bias_expand(rel_bias_t):
    onehot = jnp.asarray(_bucket_onehot_t())

    def kern(r_ref, oh_ref, o_ref):
        o_ref[...] = _raw_dot(r_ref[...], oh_ref[...], "nn", True)

    return pl.pallas_call(
        kern, name="bias_expand", out_shape=SDS((SWA_H, BLK * 2 * BLK), F32), compiler_params=_cparams(),
    )(rel_bias_t, onehot)


def _bias_reduce(d_bias_flat):
    onehot = jnp.asarray(_bucket_onehot_t())

    def kern(d_ref, oh_ref, o_ref):
        o_ref[...] = _raw_dot(d_ref[...], oh_ref[...], "nt", True)

    return pl.pallas_call(
        kern, name="bias_reduce", out_shape=SDS((SWA_H, NBUCKET), F32), compiler_params=_cparams(),
    )(d_bias_flat, onehot)


def _swa_specs(nb, rev):
    def blk(n):
        return (nb - 1 - n) if rev else n

    def before(n):
        return jnp.maximum(blk(n) - 1, 0)

    q_spec = pl.BlockSpec((BLK, SWAW), lambda n: (blk(n), C_SQ // SWAW))
    k_cur = pl.BlockSpec((BLK, SWAKW), lambda n: (blk(n), C_SK // SWAKW))
    k_prev = pl.BlockSpec((BLK, SWAKW), lambda n: (before(n), C_SK // SWAKW))
    v_cur = pl.BlockSpec((BLK, SWAKW), lambda n: (blk(n), C_SV // SWAKW))
    v_prev = pl.BlockSpec((BLK, SWAKW), lambda n: (before(n), C_SV // SWAKW))
    bias = pl.BlockSpec((SWA_H, BLK, 2 * BLK), lambda n: (0, 0, 0))
    gain = pl.BlockSpec((1, SWA_D), lambda n: (0, 0))
    sink = pl.BlockSpec((SWA_KV, 1, SWA_G), lambda n: (0, 0, 0))
    wide = pl.BlockSpec((BLK, SWAW), lambda n: (blk(n), 0))
    narrow = pl.BlockSpec((BLK, SWAKW), lambda n: (blk(n), 0))
    return [q_spec, k_prev, k_cur, v_prev, v_cur, bias, gain, gain, sink], wide, narrow


def _split_heads(x):
    return jnp.stack([x[:, h * SWA_D:(h + 1) * SWA_D] for h in range(x.shape[1] // SWA_D)], axis=0)


def _join_heads(x):
    return jnp.concatenate([x[h] for h in range(x.shape[0])], axis=1)


def _swa_fwd(proj, bias, qg, kg, sinks):
    s_len = proj.shape[0]
    nb = s_len // BLK
    in_specs, wide, _ = _swa_specs(nb, False)

    def kern(q_ref, kp_ref, kc_ref, vp_ref, vc_ref, b_ref, qg_ref, kg_ref, s_ref, o_ref):
        mask = _swa_mask(pl.program_id(0) == 0)
        o8 = _f_swa(*[_split_heads(r[...]) for r in (q_ref, kp_ref, kc_ref, vp_ref, vc_ref)], b_ref[...], qg_ref[...],
                    kg_ref[...], s_ref[...], mask)
        o_ref[...] = _join_heads(o8).astype(BF16)

    return _pc(kern, "swa_fwd", (nb,), in_specs, wide, SDS((s_len, SWAW), BF16))(
        proj, proj, proj, proj, proj, bias, qg, kg, sinks)


def _swa_bwd(proj, bias, qg, kg, sinks, d_out):
    s_len = proj.shape[0]
    nb = s_len // BLK
    in_specs, wide, narrow = _swa_specs(nb, True)

    def kern(q_ref, kp_ref, kc_ref, vp_ref, vc_ref, b_ref, qg_ref, kg_ref, s_ref, do_ref,
             dq_ref, dk_ref, dv_ref, db_ref, dqg_ref, dkg_ref, ds_ref, carry_k, carry_v):
        n = pl.program_id(0)
        mask = _swa_mask(n == nb - 1)
        _zero_first([carry_k, carry_v, db_ref, ds_ref, dqg_ref, dkg_ref])
        fn = functools.partial(_f_swa, mask=mask)
        _, vjp = jax.vjp(fn, *[_split_heads(r[...]) for r in (q_ref, kp_ref, kc_ref, vp_ref, vc_ref)], b_ref[...],
                         qg_ref[...], kg_ref[...], s_ref[...])
        dq, dkp, dkc, dvp, dvc, dbias, dqg, dkg, dsink = vjp(_split_heads(do_ref[...]))
        dq_ref[...] = _join_heads(dq).astype(BF16)
        dk_ref[...] = (_join_heads(dkc) + carry_k[...]).astype(BF16)
        dv_ref[...] = (_join_heads(dvc) + carry_v[...]).astype(BF16)
        carry_k[...] = _join_heads(dkp)
        carry_v[...] = _join_heads(dvp)
        db_ref[...] += dbias
        dqg_ref[...] += dqg
        dkg_ref[...] += dkg
        ds_ref[...] += dsink

    bias_spec, gain, sink = in_specs[5], in_specs[6], in_specs[8]
    return _pc(
        kern, "swa_bwd", (nb,), in_specs + [wide], [wide, narrow, narrow, bias_spec, gain, gain, sink],
        [SDS((s_len, SWAW), BF16), SDS((s_len, SWAKW), BF16), SDS((s_len, SWAKW), BF16),
         SDS((SWA_H, BLK, 2 * BLK), F32), SDS((1, SWA_D), F32), SDS((1, SWA_D), F32), SDS((SWA_KV, 1, SWA_G), F32)],
        scratch=[pltpu.VMEM((BLK, SWAKW), F32), pltpu.VMEM((BLK, SWAKW), F32)],
    )(proj, proj, proj, proj, proj, bias, qg, kg, sinks, d_out)


def _branch_merge(y_dn, y_swa, wa, wb, proj):
    s_len = y_dn.shape[0]
    tm = min(512, s_len)

    def kern(ya_ref, yb_ref, wa_ref, wb_ref, ga_ref, gb_ref, pa_ref, pb_ref, m_ref):
        pa = _bdot(ya_ref[...], wa_ref[0])
        pb = _bdot(yb_ref[...], wb_ref[0])
        pa_ref[...] = pa.astype(BF16)
        pb_ref[...] = pb.astype(BF16)
        m_ref[...] = _f_merge(pa, pb, ga_ref[...], gb_ref[...]).astype(BF16)

    y_spec = pl.BlockSpec((tm, DNW), lambda i, s: (i, 0))
    w_spec = pl.BlockSpec((1, DNW, CSH), lambda i, s: (s, 0, 0))
    o_spec = pl.BlockSpec((tm, CSH), lambda i, s: (i, s))
    ga_spec = pl.BlockSpec((tm, CSH), lambda i, s: (i, C_GATE // CSH + s))
    gb_spec = pl.BlockSpec((tm, CSH), lambda i, s: (i, (C_GATE + D) // CSH + s))
    return _pc(kern, "branch_merge", (s_len // tm, N_CHIPS), [y_spec, y_spec, w_spec, w_spec, ga_spec, gb_spec],
               [o_spec] * 3, [SDS((s_len, D), BF16)] * 3,
               )(y_dn, y_swa, wa, wb, proj, proj)


def _in_proj(x, gain, w_in_p):
    s_len = x.shape[0]
    tm = min(256, s_len)

    def kern(x_ref, g_ref, w_ref, h_ref, p_ref):
        h = _f_rms(x_ref[...], g_ref[...]).astype(BF16)
        h_ref[...] = h
        p_ref[...] = _bdot(h, w_ref[...])

    row = pl.BlockSpec((tm, D), lambda i: (i, 0))
    return _pc(kern, "in_proj", (s_len // tm,),
               [row, pl.BlockSpec((1, D), lambda i: (0, 0)), pl.BlockSpec((D, PW), lambda i: (0, 0))],
               [row, pl.BlockSpec((tm, PW), lambda i: (i, 0))],
               [SDS((s_len, D), BF16), SDS((s_len, PW), F32)])(x, gain, w_in_p)


def _out_proj(merged, w_out, x, gain):
    s_len = x.shape[0]
    tm = min(256, s_len)

    def kern(m_ref, w_ref, x_ref, g_ref, x1_ref, h2_ref):
        x1 = x_ref[...] + _bdot(m_ref[...], w_ref[...])
        x1_ref[...] = x1
        h2_ref[...] = _f_rms(x1, g_ref[...]).astype(BF16)

    row = pl.BlockSpec((tm, D), lambda i: (i, 0))
    return _pc(kern, "out_proj", (s_len // tm,),
               [row, pl.BlockSpec((D, D), lambda i: (0, 0)), row, pl.BlockSpec((1, D), lambda i: (0, 0))],
               [row, row], [SDS((s_len, D), F32), SDS((s_len, D), BF16)])(merged, w_out, x, gain)


def _ffn_up(h2, wg, wu):
    s_len = h2.shape[0]
    tm = min(512, s_len)

    def kern(h_ref, g_ref, u_ref, gt_ref, up_ref, act_ref):
        h = h_ref[...]
        g = _bdot(h, g_ref[0], "nt")
        u = _bdot(h, u_ref[0], "nt")
        gt_ref[0] = g.astype(BF16)
        up_ref[0] = u.astype(BF16)
        act_ref[0] = _f_swiglu(g, u).astype(BF16)

    w_spec = pl.BlockSpec((1, FSH, D), lambda s, i: (s, 0, 0))
    o_spec = pl.BlockSpec((1, tm, FSH), lambda s, i: (s, i, 0))
    shape = (N_CHIPS, s_len, FSH)
    return _pc(kern, "ffn_up", (N_CHIPS, s_len // tm), [pl.BlockSpec((tm, D), lambda s, i: (i, 0)), w_spec, w_spec],
               [o_spec] * 3, [SDS(shape, BF16)] * 3)(h2, wg, wu)


def _ffn_down_loss(act, wd, x1, target):
    s_len = x1.shape[0]
    tm = min(256, s_len)

    def kern(a_ref, w_ref, x_ref, t_ref, dy_ref, dyb_ref, loss_ref):
        _zero_first([loss_ref])
        y = x_ref[...]
        for s in range(N_CHIPS):
            y = y + _bdot(a_ref[s], w_ref[s])
        d = y - t_ref[...]
        dy = d * (1.0 / D)
        dy_ref[...] = dy
        dyb_ref[...] = dy.astype(BF16)
        loss_ref[...] += jnp.sum(d * d).reshape(1, 1) * (0.5 / D)

    row = pl.BlockSpec((tm, D), lambda i: (i, 0))
    return _pc(kern, "ffn_down_loss", (s_len // tm,),
               [pl.BlockSpec((N_CHIPS, tm, FSH), lambda i: (0, i, 0)),
                pl.BlockSpec((N_CHIPS, FSH, D), lambda i: (0, 0, 0)), row, row],
               [row, row, pl.BlockSpec((1, 1), lambda i: (0, 0))],
               [SDS((s_len, D), F32), SDS((s_len, D), BF16), SDS((1, 1), F32)])(act, wd, x1, target)


def _ffn_dact(dy_b, wd, gt, up):
    s_len = dy_b.shape[0]
    tm = min(512, s_len)

    def kern(dy_ref, w_ref, gt_ref, up_ref, dg_ref, du_ref):
        d_act = _bdot(dy_ref[...], w_ref[0], "nt")
        _, vjp = jax.vjp(_f_swiglu, gt_ref[0].astype(F32), up_ref[0].astype(F32))
        dg, du = vjp(d_act)
        dg_ref[0] = dg.astype(BF16)
        du_ref[0] = du.astype(BF16)

    a_spec = pl.BlockSpec((1, tm, FSH), lambda s, i: (s, i, 0))
    shape = (N_CHIPS, s_len, FSH)
    return _pc(kern, "ffn_dact", (N_CHIPS, s_len // tm),
               [pl.BlockSpec((tm, D), lambda s, i: (i, 0)), pl.BlockSpec((1, FSH, D), lambda s, i: (s, 0, 0)),
                a_spec, a_spec],
               [a_spec, a_spec], [SDS(shape, BF16), SDS(shape, BF16)])(dy_b, wd, gt, up)


def _gw_ffn(lhs, rhs, name):
    s_len = rhs.shape[0]
    n = len(lhs)
    tn = 512

    def kern(*refs):
        g = refs[n][...]
        for i in range(n):
            refs[n + 1 + i][0] = _bdot(refs[i][0], g, "tn").astype(BF16)

    a_spec = pl.BlockSpec((1, s_len, FSH), lambda s, j: (s, 0, 0))
    o_spec = pl.BlockSpec((1, FSH, tn), lambda s, j: (s, 0, j))
    return _pc(kern, name, (N_CHIPS, D // tn), [a_spec] * n + [pl.BlockSpec((s_len, tn), lambda s, j: (0, j))],
               [o_spec] * n, [SDS((N_CHIPS, FSH, D), BF16)] * n)(*lhs, rhs)


def _ffn_dh2(d_gt, d_up, wg, wu, x1, dy, gain):
    s_len = x1.shape[0]
    tm = min(256, s_len)

    def kern(dg_ref, du_ref, wg_ref, wu_ref, x_ref, dy_ref, g_ref, dx_ref, dxb_ref, dgain_ref):
        _zero_first([dgain_ref])
        dh2 = jnp.zeros((tm, D), F32)
        for s in range(N_CHIPS):
            dh2 = dh2 + _bdot(dg_ref[s], wg_ref[s]) + _bdot(du_ref[s], wu_ref[s])
        _, vjp = jax.vjp(_f_rms, x_ref[...], g_ref[...])
        dx, dgain = vjp(dh2)
        dx1 = dx + dy_ref[...]
        dx_ref[...] = dx1
        dxb_ref[...] = dx1.astype(BF16)
        dgain_ref[...] += dgain

    row = pl.BlockSpec((tm, D), lambda i: (i, 0))
    d_spec = pl.BlockSpec((N_CHIPS, tm, FSH), lambda i: (0, i, 0))
    w_spec = pl.BlockSpec((N_CHIPS, FSH, D), lambda i: (0, 0, 0))
    vec = pl.BlockSpec((1, D), lambda i: (0, 0))
    return _pc(kern, "ffn_dh2", (s_len // tm,), [d_spec, d_spec, w_spec, w_spec, row, row, vec],
               [row, row, vec], [SDS((s_len, D), F32), SDS((s_len, D), BF16), SDS((1, D), F32)],
               )(d_gt, d_up, wg, wu, x1, dy, gain)


def _merge_bwd(dx1_b, w_out, pa, pb, proj):
    s_len = dx1_b.shape[0]
    tm = min(256, s_len)

    def kern(dx_ref, w_ref, pa_ref, pb_ref, g_ref, dpa_ref, dpb_ref, dg_ref):
        dm = _bdot(dx_ref[...], w_ref[...], "nt")
        gates = g_ref[...]
        _, vjp = jax.vjp(_f_merge, pa_ref[...].astype(F32), pb_ref[...].astype(F32), gates[:, :D], gates[:, D:])
        dpa, dpb, dga, dgb = vjp(dm)
        dpa_ref[...] = dpa.astype(BF16)
        dpb_ref[...] = dpb.astype(BF16)
        dg_ref[:, :D] = dga.astype(BF16)
        dg_ref[:, D:] = dgb.astype(BF16)

    row = pl.BlockSpec((tm, D), lambda i: (i, 0))
    return _pc(kern, "merge_bwd", (s_len // tm,),
               [row, pl.BlockSpec((D, D), lambda i: (0, 0)), row, row,
                pl.BlockSpec((tm, 2 * D), lambda i: (i, C_GATE // (2 * D)))],
               [row, row, pl.BlockSpec((tm, 2 * D), lambda i: (i, 0))],
               [SDS((s_len, D), BF16), SDS((s_len, D), BF16), SDS((s_len, 2 * D), BF16)],
               )(dx1_b, w_out, pa, pb, proj)


def _d_branch(d_pa, d_pb, wa, wb):
    s_len = d_pa.shape[0]
    tm = min(512, s_len)

    def kern(da_ref, db_ref, wa_ref, wb_ref, oa_ref, ob_ref):
        acc_a = jnp.zeros((tm, DNW), F32)
        acc_b = jnp.zeros((tm, SWAW), F32)
        for s in range(N_CHIPS):
            acc_a = acc_a + _bdot(da_ref[:, s * CSH:(s + 1) * CSH], wa_ref[s], "nt")
            acc_b = acc_b + _bdot(db_ref[:, s * CSH:(s + 1) * CSH], wb_ref[s], "nt")
        oa_ref[...] = acc_a
        ob_ref[...] = acc_b

    row = pl.BlockSpec((tm, D), lambda i: (i, 0))
    w_spec = pl.BlockSpec((N_CHIPS, DNW, CSH), lambda i: (0, 0, 0))
    out = pl.BlockSpec((tm, DNW), lambda i: (i, 0))
    return _pc(kern, "d_branch", (s_len // tm,), [row, row, w_spec, w_spec], [out, out],
               [SDS((s_len, DNW), F32), SDS((s_len, SWAW), F32)])(d_pa, d_pb, wa, wb)


def _gw_branch(y_dn, y_swa, d_pa, d_pb):
    s_len = y_dn.shape[0]

    def kern(ya_ref, yb_ref, da_ref, db_ref, oa_ref, ob_ref):
        oa_ref[0] = _bdot(ya_ref[...], da_ref[...], "tn").astype(BF16)
        ob_ref[0] = _bdot(yb_ref[...], db_ref[...], "tn").astype(BF16)

    y_spec = pl.BlockSpec((s_len, DNW), lambda s: (0, 0))
    d_spec = pl.BlockSpec((s_len, CSH), lambda s: (0, s))
    o_spec = pl.BlockSpec((1, DNW, CSH), lambda s: (s, 0, 0))
    shape = (N_CHIPS, DNW, CSH)
    return _pc(kern, "gw_branch", (N_CHIPS,), [y_spec, y_spec, d_spec, d_spec], [o_spec, o_spec],
               [SDS(shape, BF16), SDS(shape, BF16)])(y_dn, y_swa, d_pa, d_pb)


def _dh_rms(d_proj, w_in_p, x, dx1, gain):
    s_len = x.shape[0]
    tm = min(256, s_len)

    def kern(dp_ref, w_ref, x_ref, r_ref, g_ref, gx_ref, dgain_ref):
        _zero_first([dgain_ref])
        dh = _bdot(dp_ref[...], w_ref[...], "nt")
        _, vjp = jax.vjp(_f_rms, x_ref[...], g_ref[...])
        dx, dgain = vjp(dh)
        gx_ref[...] = dx + r_ref[...]
        dgain_ref[...] += dgain

    row = pl.BlockSpec((tm, D), lambda i: (i, 0))
    vec = pl.BlockSpec((1, D), lambda i: (0, 0))
    return _pc(kern, "dh_rms", (s_len // tm,),
               [pl.BlockSpec((tm, PW), lambda i: (i, 0)), pl.BlockSpec((D, PW), lambda i: (0, 0)), row, row, vec],
               [row, vec], [SDS((s_len, D), F32), SDS((1, D), F32)])(d_proj, w_in_p, x, dx1, gain)


HALO = 8


def _conv_taps(cur_ref, prev_ref, halo, first):
    tm = cur_ref.shape[0]
    halo[0:HALO, :] = jnp.where(first, 0.0, prev_ref[...])
    halo[HALO:, :] = cur_ref[...]
    return [halo[HALO - n:HALO - n + tm, :] for n in range(CONV - 1, 0, -1)] + [cur_ref[...]]


def _dn_pre_specs(s_len, tm, blk):
    cur = pl.BlockSpec((tm, QKVW), lambda i: (blk(i), 0))
    prev = pl.BlockSpec((HALO, QKVW), lambda i: (jnp.maximum(blk(i) * (tm // HALO) - 1, 0), 0))
    ba = pl.BlockSpec((tm, 128), lambda i: (blk(i), C_BA // 128))
    row = pl.BlockSpec((tm, DNW), lambda i: (blk(i), 0))
    full = [pl.BlockSpec((CONV, QKVW), lambda i: (0, 0)), pl.BlockSpec((1, DN_H), lambda i: (0, 0)),
            pl.BlockSpec((1, DN_H), lambda i: (0, 0))]
    return cur, prev, ba, row, full


def _dn_pre_fwd(proj, conv_w, alog, dtb):
    s_len = proj.shape[0]
    tm = min(128, s_len)
    cur, prev, ba, row, full = _dn_pre_specs(s_len, tm, lambda i: i)

    def kern(cur_ref, prev_ref, ba_ref, cw_ref, al_ref, dt_ref, q_ref, k_ref, v_ref, bb_ref, gb_ref, halo):
        xs = _conv_taps(cur_ref, prev_ref, halo, pl.program_id(0) == 0)
        outs = _f_dn_pre(*xs, ba_ref[...], cw_ref[...], al_ref[...], dt_ref[...])
        for ref, val in zip((q_ref, k_ref, v_ref, bb_ref, gb_ref), outs, strict=True):
            ref[...] = val

    return _pc(kern, "dn_pre_fwd", (s_len // tm,), [cur, prev, ba] + full, [row] * 5, [SDS((s_len, DNW), F32)] * 5,
               scratch=[pltpu.VMEM((tm + HALO, QKVW), F32)])(proj, proj, proj, conv_w, alog, dtb)


def _dn_pre_bwd(proj, conv_w, alog, dtb, cots, others):
    s_len = proj.shape[0]
    tm = min(128, s_len)
    nb = s_len // tm
    cur, prev, ba, row, full = _dn_pre_specs(s_len, tm, lambda i: nb - 1 - i)
    n_o = len(others)
    assert QKVW + sum(t.shape[1] for t in others) + 128 == C_BA + 128

    def kern(cur_ref, prev_ref, ba_ref, cw_ref, al_ref, dt_ref, dq_ref, dk_ref, dv_ref, dbb_ref, dgb_ref, *rest):
        o_refs = rest[:n_o]
        dproj_ref, dcw_ref, dal_ref, ddt_ref, halo, *tails = rest[n_o:]
        i = pl.program_id(0)
        _zero_first([dcw_ref, dal_ref, ddt_ref])

        @pl.when(i == 0)
        def _():
            for t in tails:
                t[tm:, :] = jnp.zeros((HALO, QKVW), F32)

        xs = _conv_taps(cur_ref, prev_ref, halo, i == nb - 1)
        _, vjp = jax.vjp(_f_dn_pre, *xs, ba_ref[...], cw_ref[...], al_ref[...], dt_ref[...])
        *dxs, dba, dcw, dal, ddt = vjp((dq_ref[...], dk_ref[...], dv_ref[...], dbb_ref[...], dgb_ref[...]))
        total = dxs[CONV - 1]
        for j, t in enumerate(tails):
            n = CONV - 1 - j
            t[0:tm, :] = dxs[j]
            total = total + t[n:n + tm, :]
            t[tm:, :] = dxs[j][0:HALO, :]
        dproj_ref[...] = jnp.concatenate(
            [total.astype(BF16)] + [r[...] for r in o_refs] + [dba.astype(BF16), jnp.zeros((tm, PW - C_BA - 128), BF16)],
            axis=1)
        dcw_ref[...] += dcw
        dal_ref[...] += dal
        ddt_ref[...] += ddt

    o_specs = [pl.BlockSpec((tm, t.shape[1]), lambda i: (nb - 1 - i, 0)) for t in others]
    return _pc(kern, "dn_pre_bwd", (nb,), [cur, prev, ba] + full + [row] * 5 + o_specs,
               [pl.BlockSpec((tm, PW), lambda i: (nb - 1 - i, 0))] + full,
               [SDS((s_len, PW), BF16), SDS((CONV, QKVW), F32), SDS((1, DN_H), F32), SDS((1, DN_H), F32)],
               scratch=[pltpu.VMEM((tm + HALO, QKVW), F32)] * CONV)(proj, proj, proj, conv_w, alog, dtb, *cots, *others)


def _w_in_to_padded(w_sh):
    tr = 256

    def kern(w_ref, o_ref):
        full = jnp.concatenate([w_ref[s] for s in range(N_CHIPS)], axis=1)
        pieces = [full[:, o0:o0 + w] for o0, w, _ in sorted(_ORIG_PIECES, key=lambda t: t[2])]
        o_ref[...] = jnp.concatenate(pieces + [jnp.zeros((tr, PW - D_IN), w_ref.dtype)], axis=1)

    return _pc(kern, "w_in_to_padded", (D // tr,), [pl.BlockSpec((N_CHIPS, tr, D_IN // N_CHIPS), lambda i: (0, i, 0))],
               pl.BlockSpec((tr, PW), lambda i: (i, 0)), SDS((D, PW), w_sh.dtype))(w_sh)


def _padded_to_w_in(g):
    tr = 256
    csh = D_IN // N_CHIPS

    def kern(g_ref, o_ref):
        x = g_ref[...]
        full = jnp.concatenate([x[:, p0:p0 + w] for _, w, p0 in _ORIG_PIECES], axis=1)
        for s in range(N_CHIPS):
            o_ref[s] = full[:, s * csh:(s + 1) * csh]

    return _pc(kern, "padded_to_w_in", (D // tr,), [pl.BlockSpec((tr, PW), lambda i: (i, 0))],
               pl.BlockSpec((N_CHIPS, tr, csh), lambda i: (0, i, 0)), SDS((N_CHIPS, D, csh), g.dtype))(g)


def _pad_w_in(w_in):
    pieces = [w_in[:, o0:o0 + w] for o0, w, _ in sorted(_ORIG_PIECES, key=lambda t: t[2])]
    pieces.append(jnp.zeros((w_in.shape[0], PW - D_IN), w_in.dtype))
    return jnp.concatenate(pieces, axis=1)


def _unpad_w_in(g):
    return jnp.concatenate([g[:, p0:p0 + w] for _, w, p0 in _ORIG_PIECES], axis=1)


def _local_step(x, target, wts):
    s_len = x.shape[0]
    tm = min(256, s_len)
    w_in_p = wts["w_in_p"]
    attn_gain = wts["attn_norm"]
    ffn_gain = wts["ffn_norm"]
    conv_w = wts["dn_conv"]
    alog, dtb, out_gain = wts["dn_a_log"], wts["dn_dt_bias"], wts["dn_out_norm"]
    qg, kg = wts["swa_q_norm"], wts["swa_k_norm"]
    sinks = wts["swa_sinks"].reshape(SWA_KV, 1, SWA_G)

    h, proj = _in_proj(x, attn_gain, w_in_p)
    q_dn, k_dn, v_dn, bb, gb = _dn_pre_fwd(proj, conv_w, alog, dtb)
    o_dn, s_all, t_all = _dn_chunks_fwd(q_dn, k_dn, v_dn, gb, bb)
    post_ins = [_whole(o_dn), (proj, DNW, C_Z // DNW)]
    (y_dn,) = _rows(lambda r, f: ([_f_dn_post(r[0], r[1], f[0])], []), "dn_post_fwd", s_len, tm, post_ins,
                    [out_gain], [(DNW, BF16)])

    bias = _bias_expand(wts["rel_bias"].T).reshape(SWA_H, BLK, 2 * BLK)
    y_swa = _swa_fwd(proj, bias, qg, kg, sinks)

    wts = {**wts, **wts["late"](y_swa)}
    p_a, p_b, merged = _branch_merge(y_dn, y_swa, wts["wa"], wts["wb"], proj)
    x1, h2 = _out_proj(merged, wts["w_out"], x, ffn_gain)
    gt, up, act = _ffn_up(h2, wts["wg"], wts["wu"])
    dy, dy_b, loss = _ffn_down_loss(act, wts["wd"], x1, target)

    grads = {}
    d_gt, d_up = _ffn_dact(dy_b, wts["wd"], gt, up)
    (grads["w_down"],) = _gw_ffn([act], dy_b, "gw_down")
    grads["w_gate"], grads["w_up"] = _gw_ffn([d_gt, d_up], h2, "gw_gate_up")
    dx1, dx1_b, grads["ffn_norm"] = _ffn_dh2(d_gt, d_up, wts["wg"], wts["wu"], x1, dy, ffn_gain)
    grads["w_out"] = _mm(merged, dx1_b, "tn", BF16, 512, 512, "gw_out")
    d_pa, d_pb, d_gr = _merge_bwd(dx1_b, wts["w_out"], p_a, p_b, proj)
    d_ydn, d_yswa = _d_branch(d_pa, d_pb, wts["wa"], wts["wb"])
    grads["w_branch_dn"], grads["w_branch_swa"] = _gw_branch(y_dn, y_swa, d_pa, d_pb)
    token = wts["send_early"](grads)
    qg_t = qg + token[0:1, 0:1]
    out_gain_t = out_gain + token[0:1, 0:1]

    d_sq, d_sk, d_sv, d_bias, grads["swa_q_norm"], grads["swa_k_norm"], d_sinks = _swa_bwd(
        proj, bias, qg_t, kg, sinks, d_yswa)
    grads["swa_sinks"] = d_sinks.reshape(1, SWA_H)
    grads["rel_bias"] = _bias_reduce(d_bias.reshape(SWA_H, BLK * 2 * BLK)).T

    def post_bwd(r, f):
        _, vjp = jax.vjp(_f_dn_post, r[0], r[1], f[0])
        d_o, d_z, d_gain = vjp(r[2])
        return [d_o, d_z], [d_gain]

    d_o, d_z, grads["dn_out_norm"] = _rows(post_bwd, "dn_post_bwd", s_len, tm, post_ins + [_whole(d_ydn)], [out_gain_t],
                                           [(DNW, F32), (DNW, BF16)], [(1, DH)])
    d_q, d_k, d_v, d_gb, d_bb = _dn_chunks_bwd(q_dn, k_dn, v_dn, gb, bb, s_all, t_all, d_o)

    d_proj, grads["dn_conv"], grads["dn_a_log"], grads["dn_dt_bias"] = _dn_pre_bwd(
        proj, conv_w, alog, dtb, (d_q, d_k, d_v, d_bb, d_gb), (d_z, d_gr, d_sq, d_sk, d_sv))
    grads["w_in_p"] = _mm(h, d_proj, "tn", BF16, 512, 1024, "gw_in")
    token = wts["send_in"](grads["w_in_p"])
    grad_x, grads["attn_norm"] = _dh_rms(d_proj, w_in_p, x, dx1, attn_gain + token[0:1, 0:1])
    return loss, grad_x, grads


_HBM = pl.BlockSpec(memory_space=pl.ANY)


def _place():
    return lax.axis_index("x"), lax.axis_index("y"), lax.axis_index("c")


def _other_chips(x, y):
    return [(1 - x, y), (x, 1 - y), (1 - x, 1 - y)]


def _rcopy(src, dst, send_sems, recv_sems, k, to):
    return pltpu.make_async_remote_copy(src_ref=src, dst_ref=dst, send_sem=send_sems.at[k], recv_sem=recv_sems.at[k],
                                        device_id=to, device_id_type=MESH)


def _comm_call(body, name, ins, out_shapes, n_remote, landing=0):
    first = len(ins) - landing
    return pl.pallas_call(
        body, name=name, in_specs=[_HBM] * len(ins), out_specs=[_HBM] * len(out_shapes), out_shape=out_shapes,
        scratch_shapes=[pltpu.SemaphoreType.DMA((n_remote,)), pltpu.SemaphoreType.DMA((n_remote,))],
        input_output_aliases={first + i: i for i in range(landing)},
        compiler_params=_cparams(has_side_effects=True),
    )(*ins)


def _own_slot(blocks, chip):
    return [lax.dynamic_update_slice(lax.empty((N_CHIPS,) + b.shape, b.dtype), b[None], (chip, 0, 0)) for b in blocks]


def _gather_weights(ws, chip):
    n = len(ws)
    halves = [w.shape[0] // 2 for w in ws]

    def body(*refs):
        w_refs, o_refs = refs[:n], refs[2 * n:3 * n]
        send_sems, recv_sems = refs[3 * n:]
        x, y, c = _place()
        s = 2 * x + y
        sib = (x, y, 1 - c)
        chips = _other_chips(x, y)

        def rows(i, half):
            return pl.ds(half * halves[i], halves[i])

        first = []
        for j, (cx, cy) in enumerate(chips):
            for i in range(n):
                cp = _rcopy(w_refs[i].at[rows(i, c), :], o_refs[i].at[s, rows(i, c), :], send_sems, recv_sems,
                            j * n + i, (cx, cy, c))
                cp.start()
                first.append(cp)
        passed = []
        for j, (cx, cy) in enumerate(chips):
            sj = 2 * cx + cy
            for i in range(n):
                blk = o_refs[i].at[sj, rows(i, c), :]
                _rcopy(blk, blk, send_sems, recv_sems, j * n + i, (cx, cy, c)).wait_recv()
                cp = _rcopy(blk, blk, send_sems, recv_sems, (3 + j) * n + i, sib)
                cp.start()
                passed.append(cp)
        for j, (cx, cy) in enumerate(chips):
            sj = 2 * cx + cy
            for i in range(n):
                blk = o_refs[i].at[sj, rows(i, 1 - c), :]
                _rcopy(blk, blk, send_sems, recv_sems, (3 + j) * n + i, sib).wait_recv()
        for cp in first + passed:
            cp.wait_send()

    return _comm_call(body, "gather_weights", list(ws) + _own_slot(ws, chip),
                      [SDS((N_CHIPS,) + w.shape, w.dtype) for w in ws], 6 * n, landing=n)


_HBM_ONLY = pl.BlockSpec(memory_space=pltpu.HBM)
_SEM = pl.BlockSpec(memory_space=pltpu.SEMAPHORE)
_DATAFLOW = pltpu.SideEffectType.DATAFLOW_SIDE_EFFECTING


def _in_hbm(a):
    return pltpu.with_memory_space_constraint(a, pltpu.HBM)


def _gather_windows(blocks):
    halves = [b.shape[0] // 2 for b in blocks]

    def src_at(ref, i, c, sj):
        return ref.at[pl.ds(c * halves[i], halves[i]), :]

    def dst_at(ref, i, c, s_from):
        return ref.at[s_from, pl.ds(c * halves[i], halves[i]), :]

    return src_at, dst_at


def _exchange_windows():
    return (lambda ref, i, c, sj: ref.at[sj]), (lambda ref, i, c, s_from: ref.at[s_from])


def _split_start(name, ws, lands, dep, windows):
    n = len(ws)
    src_at, dst_at = windows

    def body(*refs):
        w_refs, l_refs = refs[:n], refs[n:2 * n]
        send_sems, recv_sems = refs[2 * n + 1], refs[2 * n + 2]
        token = refs[-1]
        x, y, c = _place()
        s = 2 * x + y
        for j, (cx, cy) in enumerate(_other_chips(x, y)):
            for i in range(n):
                _rcopy(src_at(w_refs[i], i, c, 2 * cx + cy), dst_at(l_refs[i], i, c, s), send_sems, recv_sems,
                       j * n + i, (cx, cy, c)).start()
        token[...] = jnp.zeros_like(token)

    outs = pl.pallas_call(
        body, name=name,
        out_shape=(pltpu.SemaphoreType.DMA((3 * n,)), pltpu.SemaphoreType.DMA((3 * n,)),
                   *[pltpu.HBM(w.shape, w.dtype) for w in ws], *[pltpu.HBM(t.shape, t.dtype) for t in lands],
                   SDS((8, 128), F32)),
        in_specs=[_HBM_ONLY] * (2 * n) + [pl.BlockSpec(memory_space=pl.ANY)],
        out_specs=(_SEM, _SEM, *[_HBM_ONLY] * (2 * n), pl.BlockSpec(memory_space=pltpu.VMEM)),
        input_output_aliases={i: 2 + i for i in range(2 * n)},
        compiler_params=pltpu.CompilerParams(has_side_effects=_DATAFLOW),
    )(*[_in_hbm(w) for w in ws], *[_in_hbm(t) for t in lands], dep)
    return outs[0], outs[1], outs[2:2 + n], outs[2 + n:2 + 2 * n], outs[-1]


def _split_wait(name, w_thru, l_thru, send_sems, recv_sems, after, windows):
    n = len(w_thru)
    src_at, dst_at = windows

    def body(*refs):
        w_refs, l_refs = refs[:n], refs[n:2 * n]
        send_sems, recv_sems = refs[2 * n], refs[2 * n + 1]
        x, y, c = _place()
        for j, (cx, cy) in enumerate(_other_chips(x, y)):
            sj = 2 * cx + cy
            for i in range(n):
                cp = _rcopy(src_at(w_refs[i], i, c, sj), dst_at(l_refs[i], i, c, sj), send_sems, recv_sems, j * n + i,
                            (cx, cy, c))
                cp.wait_send()
                cp.wait_recv()

    outs = pl.pallas_call(
        body, name=name,
        out_shape=[pltpu.HBM(w.shape, w.dtype) for w in w_thru] + [pltpu.HBM(t.shape, t.dtype) for t in l_thru],
        in_specs=[_HBM_ONLY] * (2 * n) + [_SEM, _SEM, pl.BlockSpec(memory_space=pl.ANY)],
        out_specs=[_HBM_ONLY] * (2 * n),
        input_output_aliases={i: i for i in range(2 * n)},
        compiler_params=pltpu.CompilerParams(has_side_effects=_DATAFLOW),
    )(*w_thru, *l_thru, send_sems, recv_sems, after)
    return outs[n:]


def _sibling_fill(lands):
    n = len(lands)
    halves = [t.shape[1] // 2 for t in lands]

    def body(*refs):
        o_refs = refs[n:2 * n]
        send_sems, recv_sems = refs[2 * n:]
        x, y, c = _place()
        sib = (x, y, 1 - c)
        chips = _other_chips(x, y)
        sent = []
        for j, (cx, cy) in enumerate(chips):
            for i in range(n):
                blk = o_refs[i].at[2 * cx + cy, pl.ds(c * halves[i], halves[i]), :]
                cp = _rcopy(blk, blk, send_sems, recv_sems, j * n + i, sib)
                cp.start()
                sent.append(cp)
        for j, (cx, cy) in enumerate(chips):
            for i in range(n):
                blk = o_refs[i].at[2 * cx + cy, pl.ds((1 - c) * halves[i], halves[i]), :]
                _rcopy(blk, blk, send_sems, recv_sems, j * n + i, sib).wait_recv()
        for cp in sent:
            cp.wait_send()

    return _comm_call(body, "sibling_fill", list(lands), [SDS(t.shape, t.dtype) for t in lands], 3 * n, landing=n)


def _swap_halves(gs, name):
    n = len(gs)
    halves = [g.shape[1] // 2 for g in gs]

    def body(*refs):
        g_refs, o_refs = refs[:n], refs[n:2 * n]
        send_sems, recv_sems = refs[2 * n:]
        x, y, c = _place()
        cps = [_rcopy(g_refs[i].at[:, pl.ds((1 - c) * halves[i], halves[i]), :], o_refs[i], send_sems, recv_sems, i,
                      (x, y, 1 - c)) for i in range(n)]
        for cp in cps:
            cp.start()
        for cp in cps:
            cp.wait()

    return _comm_call(body, name, gs, [SDS((N_CHIPS, h, g.shape[2]), g.dtype) for g, h in zip(gs, halves)], n)


def _swap_reduced(rs, name):
    n = len(rs)

    def body(*refs):
        r_refs, o_refs = refs[:n], refs[n:2 * n]
        send_sems, recv_sems = refs[2 * n:]
        x, y, c = _place()
        cps = [_rcopy(r_refs[i], o_refs[i], send_sems, recv_sems, i, (x, y, 1 - c)) for i in range(n)]
        for cp in cps:
            cp.start()
        for cp in cps:
            cp.wait()

    return _comm_call(body, name, rs, [SDS(r.shape, r.dtype) for r in rs], n)


def _all_sum_small(vec, name):
    n_dev = 8
    flips = [(bx, by, bc) for bx in (0, 1) for by in (0, 1) for bc in (0, 1)][1:]

    def body(v_ref, out_ref, gath, send_sems, recv_sems):
        x, y, c = _place()
        me = 4 * x + 2 * y + c
        gath[me] = v_ref[...]
        sent = []
        for k, (bx, by, bc) in enumerate(flips):
            peer = (x ^ bx, y ^ by, c ^ bc)
            cp = _rcopy(v_ref, gath.at[me], send_sems, recv_sems, k, peer)
            cp.start()
            sent.append(cp)
        for k, (bx, by, bc) in enumerate(flips):
            peer = (x ^ bx, y ^ by, c ^ bc)
            _rcopy(v_ref, gath.at[4 * peer[0] + 2 * peer[1] + peer[2]], send_sems, recv_sems, k, peer).wait_recv()
        for cp in sent:
            cp.wait_send()
        acc = gath[0]
        for d in range(1, n_dev):
            acc = acc + gath[d]
        out_ref[...] = acc

    vm = pl.BlockSpec(memory_space=pltpu.VMEM)
    return pl.pallas_call(
        body, name=name, in_specs=[vm], out_specs=vm, out_shape=SDS(vec.shape, F32),
        scratch_shapes=[pltpu.VMEM((n_dev,) + vec.shape, F32), pltpu.SemaphoreType.DMA((7,)),
                        pltpu.SemaphoreType.DMA((7,))],
        compiler_params=_cparams(has_side_effects=True),
    )(vec)


def _pack_small(vals, extra=None):
    parts = [vals[n].reshape(-1).astype(F32) for n, _ in _SMALL]
    parts.append(jnp.zeros((1,), F32) if extra is None else extra.reshape(1).astype(F32))
    flat = jnp.concatenate(parts)
    flat = jnp.concatenate([flat, jnp.zeros((_SMALL_ROWS * 128 - flat.shape[0],), F32)])
    return flat.reshape(_SMALL_ROWS, 128)


def _unpack_small(packed, shapes):
    flat = packed.reshape(-1)
    return {n: flat[_SMALL_OFF[n][0]:_SMALL_OFF[n][0] + _SMALL_OFF[n][1]].reshape(shapes[n]) for n, _ in _SMALL}


def _pair_sum(gs, gots, core, name):
    n = len(gs)

    def kern(c_ref, *refs):
        for i in range(n):
            refs[2 * n + i][...] = (refs[i][...].astype(F32) + refs[n + i][...].astype(F32)).astype(BF16)

    in_specs = [pl.BlockSpec((1, t.shape[1], t.shape[2]), lambda s, c_ref: (s, c_ref[0], 0)) for t in gots]
    in_specs += [pl.BlockSpec((1, t.shape[1], t.shape[2]), lambda s, c_ref: (s, 0, 0)) for t in gots]
    out_specs = [pl.BlockSpec((1, t.shape[1], t.shape[2]), lambda s, c_ref: (s, 0, 0)) for t in gots]
    return pl.pallas_call(
        kern, name=name,
        grid_spec=pltpu.PrefetchScalarGridSpec(num_scalar_prefetch=1, grid=(N_CHIPS,), in_specs=in_specs,
                                               out_specs=out_specs),
        out_shape=[SDS(t.shape, BF16) for t in gots],
        compiler_params=_cparams(dimension_semantics=("arbitrary",)),
    )(core.reshape(1).astype(jnp.int32), *gs, *gots)


def _chip_sum(qs, name):
    n = len(qs)

    def kern(*refs):
        for i in range(n):
            acc = refs[i][0].astype(F32)
            for s in range(1, N_CHIPS):
                acc = acc + refs[i][s].astype(F32)
            refs[n + i][...] = acc

    in_specs = [pl.BlockSpec((N_CHIPS, q.shape[1] // 2, q.shape[2]), lambda j: (0, j, 0)) for q in qs]
    out_specs = [pl.BlockSpec((q.shape[1] // 2, q.shape[2]), lambda j: (j, 0)) for q in qs]
    return _pc(kern, name, (2,), in_specs, out_specs, [SDS(q.shape[1:], F32) for q in qs])(*qs)


def _adam_math(w_, g_, m_, v_):
    m_ = ADAM_B1 * m_ + (1.0 - ADAM_B1) * g_
    v_ = ADAM_B2 * v_ + (1.0 - ADAM_B2) * jnp.square(g_)
    m_hat = m_ / (1.0 - ADAM_B1 ** ADAM_STEP)
    v_hat = v_ / (1.0 - ADAM_B2 ** ADAM_STEP)
    return -ADAM_LR * (m_hat / (jnp.sqrt(v_hat) + ADAM_EPS) + ADAM_WD * w_), m_, v_


def _adamw(w, g, m, v, name):
    rows, cols = w.shape
    tr = rows
    for cand in (256, 128, 64, 32, 16, 8):
        if rows % cand == 0 and rows > cand:
            tr = cand
            break

    def kern(w_ref, g_ref, m_ref, v_ref, d_ref, nm_ref, nv_ref):
        d_ref[...], nm_ref[...], nv_ref[...] = _adam_math(w_ref[...], g_ref[...], m_ref[...], v_ref[...])

    spec = pl.BlockSpec((tr, cols), lambda i: (i, 0))
    return _pc(kern, name, (rows // tr,), [spec] * 4, [spec] * 3, [SDS(w.shape, F32)] * 3)(w, g, m, v)


def _adamw_rows1(w, g, m, v, name):
    rows, _, cols = w.shape
    tr = next(t for t in (42, 32, 29, 16, 8, 7, 6, 4, 3, 2, 1) if rows % t == 0)

    def kern(w_ref, g_ref, m_ref, v_ref, go_ref, d_ref, nm_ref, nv_ref):
        g_ = g_ref[...]
        go_ref[...] = g_
        d_ref[...], nm_ref[...], nv_ref[...] = _adam_math(w_ref[...], g_, m_ref[...], v_ref[...])

    spec = pl.BlockSpec((tr, 1, cols), lambda i: (i, 0, 0))
    return _pc(kern, name, (rows // tr,), [spec] * 4, [spec] * 4, [SDS(w.shape, F32)] * 4)(w, g, m, v)


def _adamw_big(w, mine, theirs, m, v, core, name):
    _, rows, cols = w.shape
    half = rows // 2
    tr = next(t for t in (256, 176, 128, 64, 32, 16, 8) if half % t == 0)
    nbh = half // tr

    def kern(c_ref, w_ref, a_ref, b_ref, m_ref, v_ref, g_ref, d_ref, nm_ref, nv_ref):
        g_ = jnp.where(pl.program_id(0) // nbh == c_ref[0], a_ref[...], b_ref[...])
        g_ref[0] = g_
        d_ref[0], nm_ref[0], nv_ref[0] = _adam_math(w_ref[0], g_, m_ref[0], v_ref[0])

    full = pl.BlockSpec((1, tr, cols), lambda i, c_ref: (0, i, 0))
    part = pl.BlockSpec((tr, cols), lambda i, c_ref: (i % nbh, 0))
    return pl.pallas_call(
        kern, name=name,
        grid_spec=pltpu.PrefetchScalarGridSpec(num_scalar_prefetch=1, grid=(rows // tr,),
                                               in_specs=[full, part, part, full, full], out_specs=[full] * 4),
        out_shape=[SDS(w.shape, F32)] * 4,
        compiler_params=_cparams(dimension_semantics=("arbitrary",)),
    )(core.reshape(1).astype(jnp.int32), w, mine, theirs, m, v)


_WEIGHT_NAMES = ("attn_norm", "w_in", "dn_conv", "dn_a_log", "dn_dt_bias", "dn_out_norm", "swa_q_norm", "swa_k_norm",
                 "swa_sinks", "rel_bias", "w_branch_dn", "w_branch_swa", "w_out", "ffn_norm", "w_gate", "w_up",
                 "w_down")
_CONV_SH = QKVW // N_CHIPS


def kernel(x, attn_norm, w_in, dn_conv, dn_a_log, dn_dt_bias, dn_out_norm, swa_q_norm, swa_k_norm, swa_sinks, rel_bias, w_branch_dn, w_branch_swa, w_out, ffn_norm, w_gate, w_up, w_down, loss_target, m_attn_norm, m_w_in, m_dn_conv, m_dn_a_log, m_dn_dt_bias, m_dn_out_norm, m_swa_q_norm, m_swa_k_norm, m_swa_sinks, m_rel_bias, m_w_branch_dn, m_w_branch_swa, m_w_out, m_ffn_norm, m_w_gate, m_w_up, m_w_down, v_attn_norm, v_w_in, v_dn_conv, v_dn_a_log, v_dn_dt_bias, v_dn_out_norm, v_swa_q_norm, v_swa_k_norm, v_swa_sinks, v_rel_bias, v_w_branch_dn, v_w_branch_swa, v_w_out, v_ffn_norm, v_w_gate, v_w_up, v_w_down):
    w = dict(attn_norm=attn_norm, w_in=w_in, dn_conv=dn_conv, dn_a_log=dn_a_log, dn_dt_bias=dn_dt_bias,
             dn_out_norm=dn_out_norm, swa_q_norm=swa_q_norm, swa_k_norm=swa_k_norm, swa_sinks=swa_sinks,
             rel_bias=rel_bias, w_branch_dn=w_branch_dn, w_branch_swa=w_branch_swa, w_out=w_out, ffn_norm=ffn_norm,
             w_gate=w_gate, w_up=w_up, w_down=w_down)
    m = dict(attn_norm=m_attn_norm, w_in=m_w_in, dn_conv=m_dn_conv, dn_a_log=m_dn_a_log, dn_dt_bias=m_dn_dt_bias,
             dn_out_norm=m_dn_out_norm, swa_q_norm=m_swa_q_norm, swa_k_norm=m_swa_k_norm, swa_sinks=m_swa_sinks,
             rel_bias=m_rel_bias, w_branch_dn=m_w_branch_dn, w_branch_swa=m_w_branch_swa, w_out=m_w_out,
             ffn_norm=m_ffn_norm, w_gate=m_w_gate, w_up=m_w_up, w_down=m_w_down)
    v = dict(attn_norm=v_attn_norm, w_in=v_w_in, dn_conv=v_dn_conv, dn_a_log=v_dn_a_log, dn_dt_bias=v_dn_dt_bias,
             dn_out_norm=v_dn_out_norm, swa_q_norm=v_swa_q_norm, swa_k_norm=v_swa_k_norm, swa_sinks=v_swa_sinks,
             rel_bias=v_rel_bias, w_branch_dn=v_w_branch_dn, w_branch_swa=v_w_branch_swa, w_out=v_w_out,
             ffn_norm=v_ffn_norm, w_gate=v_w_gate, w_up=v_w_up, w_down=v_w_down)
    shapes = {n: w[n].shape for n in _WEIGHT_NAMES}

    def two_d(a):
        return a.reshape(a.shape[-2], a.shape[-1]) if a.ndim == 3 else a

    core = lax.axis_index("c")
    chip = 2 * lax.axis_index("x") + lax.axis_index("y")
    small_shapes = {n: two_d(w[n]).shape for n, _ in _SMALL}
    small_shapes["dn_conv"] = (CONV, QKVW)

    conv_loc = two_d(w["dn_conv"])
    conv_part = lax.dynamic_update_slice(jnp.zeros((CONV, QKVW), F32), jnp.where(core == 0, conv_loc, 0.0),
                                         (0, chip * _CONV_SH))
    conv_full = _all_sum_small(conv_part.reshape(CONV * QKVW // 128, 128), "gather_conv").reshape(CONV, QKVW)

    flipped = ("w_gate", "w_up")

    def natural(a, n):
        return a.transpose(0, 2, 1) if n in flipped else a

    w_bf = [two_d(natural(w[n], n).astype(BF16)) for n in _BIG_NAMES]
    (w_in_g,) = _gather_weights(w_bf[:1], chip)
    windows = _gather_windows(w_bf[1:])
    after_sync = w_in_g[0, :8, :128].astype(F32) + conv_full[0:1, :128]
    send_sems, recv_sems, w_thru, l_thru, token = _split_start(
        "gather_start", w_bf[1:], _own_slot(w_bf[1:], chip), after_sync, windows)

    def late(after):
        lands = _split_wait("gather_wait", w_thru, l_thru, send_sems, recv_sems, after, windows)
        g = dict(zip(_BIG_NAMES[1:], _sibling_fill(lands)))
        return dict(wa=g["w_branch_dn"], wb=g["w_branch_swa"], w_out=g["w_out"].reshape(D, D), wg=g["w_gate"],
                    wu=g["w_up"], wd=g["w_down"])

    wts = dict(w_in_p=_w_in_to_padded(w_in_g), dn_conv=conv_full, late=late)
    for n, _ in _SMALL[:-1]:
        wts[n] = two_d(w[n])
    wts["attn_norm"] = wts["attn_norm"] + token[0:1, 0:1]

    early = {}

    def send_early(grads):
        gs = [grads["w_branch_dn"], grads["w_branch_swa"], grads["w_out"].reshape(N_CHIPS, CSH, D), grads["w_gate"],
              grads["w_up"], grads["w_down"]]
        parts = _pair_sum(gs, _swap_halves(gs, "swap_halves_early"), core, "pair_sum_early")
        own = [lax.dynamic_index_in_dim(p, chip, axis=0, keepdims=False) for p in parts]
        early["sems"], early["recv"], early["src"], early["land"], tok = _split_start(
            "exchange_start", parts, _own_slot(own, chip), parts[0][0, :8, :128], _exchange_windows())
        return tok

    last = {}

    def send_in(g_in_p):
        g_in = [_padded_to_w_in(g_in_p)]
        parts = _pair_sum(g_in, _swap_halves(g_in, "swap_halves_in"), core, "pair_sum_in")
        own = [lax.dynamic_index_in_dim(p, chip, axis=0, keepdims=False) for p in parts]
        last["sems"], last["recv"], last["src"], last["land"], tok = _split_start(
            "exchange_in_start", parts, _own_slot(own, chip), parts[0][0, :8, :128], _exchange_windows())
        return tok

    wts["send_early"] = send_early
    wts["send_in"] = send_in
    loss_sum, grad_x, grads = _local_step(x[0], loss_target[0], wts)

    small_sum = _all_sum_small(_pack_small(grads, loss_sum), "all_sum_small")
    loss = small_sum.reshape(-1)[_LOSS_OFF]
    g_small = _unpack_small(small_sum, small_shapes)

    q_early = _split_wait("exchange_wait", early["src"], early["land"], early["sems"], early["recv"], small_sum,
                          _exchange_windows())
    red_early = _chip_sum(list(q_early), "chip_sum_early")
    their_early = _swap_reduced(red_early, "swap_reduced_early")
    g_out, d_out, m_out, v_out = {}, {}, {}, {}
    for n, mine, other in zip(_BIG_NAMES[1:], red_early, their_early):
        res = _adamw_big(natural(w[n], n), mine, other, natural(m[n], n), natural(v[n], n), core, "adamw_" + n)
        g_out[n], d_out[n], m_out[n], v_out[n] = (natural(t, n) for t in res)

    q_in = _split_wait("exchange_in_wait", last["src"], last["land"], last["sems"], last["recv"],
                       d_out[_BIG_NAMES[-1]], _exchange_windows())
    reduced = _chip_sum(list(q_in), "chip_sum_in")
    theirs = _swap_reduced(reduced, "swap_reduced_in")

    def rows1(a):
        return a.transpose(2, 0, 1)

    def unrows1(a):
        return a.transpose(1, 2, 0)

    g_in_blk = jnp.concatenate([jnp.where(core == 0, reduced[0], theirs[0]),
                                jnp.where(core == 0, theirs[0], reduced[0])], axis=0)
    g_in_r = rows1(g_in_blk[None])
    res = _adamw_rows1(rows1(w["w_in"]), g_in_r, rows1(m["w_in"]), rows1(v["w_in"]), "adamw_w_in")
    g_out["w_in"], d_out["w_in"], m_out["w_in"], v_out["w_in"] = (unrows1(t) for t in res)
    g_conv = lax.dynamic_slice(g_small["dn_conv"], (0, chip * _CONV_SH), (CONV, _CONV_SH))
    g_out["dn_conv"] = g_conv.reshape(shapes["dn_conv"])
    d_, m_, v_ = _adamw(conv_loc, g_conv, two_d(m["dn_conv"]), two_d(v["dn_conv"]), "adamw_dn_conv")
    d_out["dn_conv"], m_out["dn_conv"], v_out["dn_conv"] = (t.reshape(shapes["dn_conv"]) for t in (d_, m_, v_))

    def packed(src):
        vals = {n: src[n] for n, _ in _SMALL[:-1]}
        vals["dn_conv"] = jnp.zeros((CONV * QKVW,), F32)
        return _pack_small(vals)

    d_s, m_s, v_s = _adamw(packed(w), small_sum, packed(m), packed(v), "adamw_small")
    d_small, m_small, v_small = (_unpack_small(t, small_shapes) for t in (d_s, m_s, v_s))
    for n, _ in _SMALL[:-1]:
        g_out[n] = g_small[n].reshape(shapes[n])
        d_out[n], m_out[n], v_out[n] = (t[n].reshape(shapes[n]) for t in (d_small, m_small, v_small))

    return (loss, grad_x[None], *[g_out[n] for n in _WEIGHT_NAMES], *[d_out[n] for n in _WEIGHT_NAMES],
            *[m_out[n] for n in _WEIGHT_NAMES], *[v_out[n] for n in _WEIGHT_NAMES])
```

```python
import functools
import math

import numpy as np
import jax
import jax.numpy as jnp
from jax import lax
from jax.experimental import pallas as pl
from jax.experimental.pallas import tpu as pltpu

F32 = jnp.float32
BF16 = jnp.bfloat16
SDS = jax.ShapeDtypeStruct

D = 1024
DN_H = 4
DH = 128
DNW = DN_H * DH
QKVW = 3 * DNW
CONV = 4
CHUNK = 64
SWA_H = 8
SWA_KV = 2
SWA_G = SWA_H // SWA_KV
SWA_D = 64
SWAW = SWA_H * SWA_D
SWAKW = SWA_KV * SWA_D
BLK = 128
NBUCKET = 32
MAXDIST = 128
DFF = 2816
D_IN = QKVW + DNW + 2 * DN_H + SWAW + 2 * SWAKW + 2 * D
EPS = 1e-6
NEG = -1e30

ADAM_LR = 0.001
ADAM_B1 = 0.9
ADAM_B2 = 0.999
ADAM_EPS = 1e-08
ADAM_WD = 0.01
ADAM_STEP = 10

C_QKV, C_Z, C_GATE, C_SQ, C_SK, C_SV, C_BA = 0, 1536, 2048, 4096, 4608, 4736, 4864
PW = 5120
_ORIG_PIECES = (
    (0, QKVW, C_QKV),
    (QKVW, DNW, C_Z),
    (QKVW + DNW, 2 * DN_H, C_BA),
    (QKVW + DNW + 2 * DN_H, SWAW, C_SQ),
    (QKVW + DNW + 2 * DN_H + SWAW, SWAKW, C_SK),
    (QKVW + DNW + 2 * DN_H + SWAW + SWAKW, SWAKW, C_SV),
    (QKVW + DNW + 2 * DN_H + SWAW + 2 * SWAKW, 2 * D, C_GATE),
)

N_CHIPS = 4
FSH = DFF // N_CHIPS
CSH = D // N_CHIPS
VMEM_LIMIT = 48 * 1024 * 1024
MESH = pl.DeviceIdType.MESH

_BIG = (
    ("w_in", D, D_IN // N_CHIPS),
    ("w_branch_dn", DNW, CSH),
    ("w_branch_swa", SWAW, CSH),
    ("w_out", CSH, D),
    ("w_gate", FSH, D),
    ("w_up", FSH, D),
    ("w_down", FSH, D),
)
_BIG_NAMES = tuple(n for n, _, _ in _BIG)

_SMALL = (
    ("attn_norm", D), ("ffn_norm", D), ("dn_out_norm", DH), ("swa_q_norm", SWA_D), ("swa_k_norm", SWA_D),
    ("swa_sinks", SWA_H), ("dn_a_log", DN_H), ("dn_dt_bias", DN_H), ("rel_bias", NBUCKET * SWA_H),
    ("dn_conv", CONV * QKVW),
)
_SMALL_OFF = {}
_o = 0
for _n, _s in _SMALL:
    _SMALL_OFF[_n] = (_o, _s)
    _o += _s
_LOSS_OFF = _o
_SMALL_ROWS = -(-(_o + 1) // (8 * 128)) * 8


def _cparams(**kw):
    return pltpu.CompilerParams(vmem_limit_bytes=VMEM_LIMIT, **kw)


_DIMS = {
    "nn": (((1,), (0,)), ((), ())),
    "nt": (((1,), (1,)), ((), ())),
    "tn": (((0,), (0,)), ((), ())),
    "bnn": (((2,), (1,)), ((0,), (0,))),
    "bnt": (((2,), (2,)), ((0,), (0,))),
    "btn": (((1,), (1,)), ((0,), (0,))),
}


def _raw_dot(a, b, kind, exact):
    if exact:
        prec = lax.Precision.HIGH if exact == "x3" else lax.Precision.HIGHEST
        return lax.dot_general(a, b, _DIMS[kind], precision=prec, preferred_element_type=F32)
    return lax.dot_general(a.astype(BF16), b.astype(BF16), _DIMS[kind], preferred_element_type=F32)


@functools.partial(jax.custom_vjp, nondiff_argnums=(2, 3))
def _dot(a, b, kind, exact):
    return _raw_dot(a, b, kind, exact)


def _dot_fwd(a, b, kind, exact):
    return _raw_dot(a, b, kind, exact), (a, b)


def _dot_bwd(kind, exact, res, g):
    a, b = res
    pre = kind[:-2]
    nn, nt, tn = pre + "nn", pre + "nt", pre + "tn"
    if kind == nn:
        return _dot(g, b, nt, exact), _dot(a, g, tn, exact)
    if kind == nt:
        return _dot(g, b, nn, exact), _dot(g, a, tn, exact)
    return _dot(b, g, nt, exact), _dot(a, g, nn, exact)


_dot.defvjp(_dot_fwd, _dot_bwd)


def _silu(x):
    return x * jax.nn.sigmoid(x)


def _f_rms(x, gain):
    return x * lax.rsqrt(jnp.mean(x * x, axis=-1, keepdims=True) + EPS) * gain


def _f_dn_pre(xs0, xs1, xs2, xs3, ba, cw, alog, dtb):
    rows = xs0.shape[0]
    c = xs0 * cw[0:1] + xs1 * cw[1:2] + xs2 * cw[2:3] + xs3 * cw[3:4]
    qkv = _silu(c)
    qs, ks, bbs, gbs = [], [], [], []
    for h in range(DN_H):
        qh = qkv[:, h * DH:(h + 1) * DH]
        kh = qkv[:, DNW + h * DH:DNW + (h + 1) * DH]
        qs.append(qh * lax.rsqrt(jnp.sum(qh * qh, axis=-1, keepdims=True) + EPS) * (DH ** -0.5))
        ks.append(kh * lax.rsqrt(jnp.sum(kh * kh, axis=-1, keepdims=True) + EPS))
        beta = jax.nn.sigmoid(ba[:, h:h + 1])
        ar = ba[:, DN_H + h:DN_H + h + 1] + dtb[:, h:h + 1]
        softplus = jnp.maximum(ar, 0.0) + jnp.log1p(jnp.exp(-jnp.abs(ar)))
        g = -jnp.exp(alog[:, h:h + 1]) * softplus
        bbs.append(jnp.broadcast_to(beta, (rows, DH)))
        gbs.append(jnp.broadcast_to(g, (rows, DH)))
    return (jnp.concatenate(qs, axis=1), jnp.concatenate(ks, axis=1), qkv[:, 2 * DNW:],
            jnp.concatenate(bbs, axis=1), jnp.concatenate(gbs, axis=1))


def _f_dn_post(o, z, gain):
    ys = []
    for h in range(DN_H):
        oh = o[:, h * DH:(h + 1) * DH]
        zh = z[:, h * DH:(h + 1) * DH]
        ys.append(oh * lax.rsqrt(jnp.mean(oh * oh, axis=-1, keepdims=True) + EPS) * gain * _silu(zh))
    return jnp.concatenate(ys, axis=1)


def _f_merge(pa, pb, ga, gb):
    return jax.nn.sigmoid(ga) * pa + jax.nn.sigmoid(gb) * pb


def _f_swiglu(g, u):
    return _silu(g) * u


@jax.custom_vjp
def _unit_lower_inverse(a):
    c = a.shape[-1]
    eye = (lax.broadcasted_iota(jnp.int32, a.shape, 1) == lax.broadcasted_iota(jnp.int32, a.shape, 2)).astype(F32)
    p = -a
    t = eye + p
    for _ in range(max(c.bit_length() - 2, 0)):
        p = _raw_dot(p, p, "bnn", "x3")
        t = t + _raw_dot(t, p, "bnn", "x3")
    return t


def _unit_lower_inverse_fwd(a):
    t = _unit_lower_inverse(a)
    return t, t


def _unit_lower_inverse_bwd(t, g):
    return (-_raw_dot(_raw_dot(t, g, "btn", "x3"), t, "bnt", "x3"),)


_unit_lower_inverse.defvjp(_unit_lower_inverse_fwd, _unit_lower_inverse_bwd)


@jax.custom_vjp
def _known_inverse(a, t):
    return t


def _known_inverse_fwd(a, t):
    return t, t


def _known_inverse_bwd(t, g):
    return _unit_lower_inverse_bwd(t, g)[0], jnp.zeros_like(t)


_known_inverse.defvjp(_known_inverse_fwd, _known_inverse_bwd)


def _f_chunk(q, k, v, gb, bb, s, t_known=None, with_t=False):
    c = CHUNK
    nh = q.shape[0]
    ii = lax.broadcasted_iota(jnp.int32, (nh, c, c), 1)
    jj = lax.broadcasted_iota(jnp.int32, (nh, c, c), 2)
    incl = ii >= jj
    strict = ii > jj
    eye = (ii == jj).astype(F32)
    gcb = _dot(incl.astype(F32), gb, "bnn", "x3")
    lane0 = (lax.broadcasted_iota(jnp.int32, (nh, c, DH), 2) == 0).astype(F32)
    gcol = gcb[:, :, :c]
    grow = _dot(lane0, gcb, "bnt", "x3")
    decay = jnp.where(incl, jnp.exp(jnp.where(incl, gcol - grow, 0.0)), 0.0)
    kb = k * bb
    vb = v * bb
    a = jnp.where(strict, _dot(kb, k, "bnt", False) * decay, 0.0)
    t = _unit_lower_inverse(a) if t_known is None else _known_inverse(a, t_known)
    eg = jnp.exp(gcb)
    u = _dot(t, vb, "bnn", "x3")
    w = _dot(t, kb * eg, "bnn", "x3")
    qk = jnp.where(incl, _dot(q, k, "bnt", False) * decay, 0.0)
    qe = q * eg
    glast = gcb[:, c - 1:c, :]
    k_dec = k * jnp.exp(glast - gcb)
    e_last = jnp.exp(glast)
    outs = []
    for g in range(nh // DN_H):
        sl = slice(g * DN_H, (g + 1) * DN_H)
        v_new = u[sl] - _dot(w[sl], s, "bnn", False)
        outs.append(_dot(qe[sl], s, "bnn", False) + _dot(qk[sl], v_new, "bnn", False))
        s = s * e_last[sl] + _dot(k_dec[sl], v_new, "btn", False)
    o = jnp.concatenate(outs, axis=0)
    return (o, s, t) if with_t else (o, s)


def _f_swa(q8, kp, kc, vp, vc, bias8, qg, kg, sink, mask):
    kb = jnp.concatenate([kp, kc], axis=1)
    vb = jnp.concatenate([vp, vc], axis=1)
    kn = kb * lax.rsqrt(jnp.mean(kb * kb, axis=-1, keepdims=True) + EPS) * kg

    def rows(per_head):
        return jnp.stack([jnp.concatenate([per_head(kv, g) for g in range(SWA_G)], axis=0)
                          for kv in range(SWA_KV)], axis=0)

    qq = rows(lambda kv, g: q8[kv * SWA_G + g])
    qn = qq * lax.rsqrt(jnp.mean(qq * qq, axis=-1, keepdims=True) + EPS) * qg * (SWA_D ** -0.5)
    lg = _dot(qn, kn, "bnt", False) + rows(lambda kv, g: bias8[kv * SWA_G + g])
    lg = jnp.where(rows(lambda kv, g: mask), lg, NEG)
    sk = rows(lambda kv, g: jnp.broadcast_to(sink[kv][:, g:g + 1], (BLK, 1)))
    m = lax.stop_gradient(jnp.maximum(jnp.max(lg, axis=-1, keepdims=True), sk))
    p = jnp.exp(lg - m)
    den = jnp.sum(p, axis=-1, keepdims=True) + jnp.exp(sk - m)
    out = _dot(p * (1.0 / den), vb, "bnn", False)
    return jnp.stack([out[kv, g * BLK:(g + 1) * BLK] for kv in range(SWA_KV) for g in range(SWA_G)], axis=0)


def _bdot(a, b, kind="nn"):
    return lax.dot_general(a.astype(BF16), b.astype(BF16), _DIMS[kind], preferred_element_type=F32)


def _pc(kern, name, grid, in_specs, out_specs, out_shape, scratch=()):
    return pl.pallas_call(
        kern, name=name, grid=grid, in_specs=in_specs, out_specs=out_specs, out_shape=out_shape,
        scratch_shapes=list(scratch), compiler_params=_cparams(dimension_semantics=("arbitrary",) * len(grid)))


def _mm(a, b, kind, out_dtype, tm, tn, name):
    if kind == "tn":
        k, m = a.shape
    else:
        m, k = a.shape
    n = b.shape[0] if kind == "nt" else b.shape[1]
    tm, tn = min(tm, m), min(tn, n)
    assert m % tm == 0 and n % tn == 0, (name, a.shape, b.shape, tm, tn)

    def kern(a_ref, b_ref, o_ref):
        o_ref[...] = _bdot(a_ref[...], b_ref[...], kind).astype(o_ref.dtype)

    a_spec = pl.BlockSpec((k, tm), lambda i, j: (0, i)) if kind == "tn" else pl.BlockSpec((tm, k), lambda i, j: (i, 0))
    b_spec = pl.BlockSpec((tn, k), lambda i, j: (j, 0)) if kind == "nt" else pl.BlockSpec((k, tn), lambda i, j: (0, j))
    return _pc(kern, name, (m // tm, n // tn), [a_spec, b_spec], pl.BlockSpec((tm, tn), lambda i, j: (i, j)),
               SDS((m, n), out_dtype))(a, b)


def _rows(body, name, m, tm, row_ins, full_ins, row_outs, acc_outs=()):
    n_r, n_f, n_o, n_a = len(row_ins), len(full_ins), len(row_outs), len(acc_outs)
    assert m % tm == 0

    def kern(*refs):
        r = refs[:n_r]
        f = refs[n_r:n_r + n_f]
        o = refs[n_r + n_f:n_r + n_f + n_o]
        acc = refs[n_r + n_f + n_o:]
        outs, sums = body([x[...] for x in r], [x[...] for x in f])
        for ref, val in zip(o, outs, strict=True):
            ref[...] = val.astype(ref.dtype)
        if n_a:
            @pl.when(pl.program_id(0) == 0)
            def _():
                for ref in acc:
                    ref[...] = jnp.zeros(ref.shape, F32)

            for ref, val in zip(acc, sums, strict=True):
                ref[...] += val

    in_specs = [pl.BlockSpec((tm, w), functools.partial(lambda i, cb: (i, cb), cb=cb)) for _, w, cb in row_ins]
    in_specs += [pl.BlockSpec(x.shape, lambda i: (0, 0)) for x in full_ins]
    out_specs = [pl.BlockSpec((tm, w), lambda i: (i, 0)) for w, _ in row_outs]
    out_specs += [pl.BlockSpec(s, lambda i: (0, 0)) for s in acc_outs]
    out_shape = [SDS((m, w), dt) for w, dt in row_outs]
    out_shape += [SDS(s, F32) for s in acc_outs]
    return _pc(kern, name, (m // tm,), in_specs, out_specs, out_shape)(*[x for x, _, _ in row_ins], *full_ins)


def _whole(x):
    return (x, x.shape[1], 0)


def _zero_first(refs):
    @pl.when(pl.program_id(0) == 0)
    def _():
        for ref in refs:
            ref[...] = jnp.zeros(ref.shape, F32)


GROUP = 4


def _heads(ref):
    return jnp.stack([ref[g * CHUNK:(g + 1) * CHUNK, h * DH:(h + 1) * DH]
                      for g in range(GROUP) for h in range(DN_H)], axis=0)


def _unheads(ref, val):
    for g in range(GROUP):
        for h in range(DN_H):
            ref[g * CHUNK:(g + 1) * CHUNK, h * DH:(h + 1) * DH] = val[g * DN_H + h]


def _dn_chunks_fwd(q, k, v, gb, bb):
    s_len = q.shape[0]
    ng = s_len // (GROUP * CHUNK)

    def kern(q_ref, k_ref, v_ref, g_ref, b_ref, o_ref, sall_ref, t_ref, state):
        _zero_first([state])
        s = state[...]
        sall_ref[0] = s
        o, s_new, t = _f_chunk(*[_heads(r) for r in (q_ref, k_ref, v_ref, g_ref, b_ref)], s, with_t=True)
        _unheads(o_ref, o)
        t_ref[0] = t
        state[...] = s_new

    blk = pl.BlockSpec((GROUP * CHUNK, DNW), lambda c: (c, 0))
    return _pc(kern, "dn_chunks_fwd", (ng,), [blk] * 5,
               [blk, pl.BlockSpec((1, DN_H, DH, DH), lambda c: (c, 0, 0, 0)),
                pl.BlockSpec((1, GROUP * DN_H, CHUNK, CHUNK), lambda c: (c, 0, 0, 0))],
               [SDS((s_len, DNW), F32), SDS((ng, DN_H, DH, DH), F32), SDS((ng, GROUP * DN_H, CHUNK, CHUNK), F32)],
               scratch=[pltpu.VMEM((DN_H, DH, DH), F32)])(q, k, v, gb, bb)


def _dn_chunks_bwd(q, k, v, gb, bb, s_all, t_all, d_o):
    s_len = q.shape[0]
    ng = s_len // (GROUP * CHUNK)

    def kern(q_ref, k_ref, v_ref, g_ref, b_ref, sall_ref, t_ref, do_ref, dq_ref, dk_ref, dv_ref, dg_ref, db_ref,
             dstate):
        _zero_first([dstate])
        fn = functools.partial(_f_chunk, t_known=t_ref[0])
        _, vjp = jax.vjp(fn, *[_heads(r) for r in (q_ref, k_ref, v_ref, g_ref, b_ref)], sall_ref[0])
        *d_ins, ds = vjp((_heads(do_ref), dstate[...]))
        for ref, val in zip((dq_ref, dk_ref, dv_ref, dg_ref, db_ref), d_ins, strict=True):
            _unheads(ref, val)
        dstate[...] = ds

    blk = pl.BlockSpec((GROUP * CHUNK, DNW), lambda c: (ng - 1 - c, 0))
    return _pc(kern, "dn_chunks_bwd", (ng,),
               [blk] * 5 + [pl.BlockSpec((1, DN_H, DH, DH), lambda c: (ng - 1 - c, 0, 0, 0)),
                            pl.BlockSpec((1, GROUP * DN_H, CHUNK, CHUNK), lambda c: (ng - 1 - c, 0, 0, 0)), blk],
               [blk] * 5, [SDS((s_len, DNW), F32)] * 5,
               scratch=[pltpu.VMEM((DN_H, DH, DH), F32)])(q, k, v, gb, bb, s_all, t_all, d_o)


def _t5_bucket_table():
    qi = np.arange(BLK)[:, None]
    kj = np.arange(2 * BLK)[None, :]
    dist = BLK + qi - kj
    n = np.maximum(dist, 0)
    max_exact = NBUCKET // 2
    nf = np.maximum(n, 1).astype(np.float32)
    large = max_exact + (np.log(nf / np.float32(max_exact)) / np.float32(math.log(MAXDIST / max_exact))
                         * np.float32(NBUCKET - max_exact)).astype(np.int32)
    large = np.minimum(large, NBUCKET - 1)
    return np.where(n < max_exact, n, large)


def _bucket_onehot_t():
    table = _t5_bucket_table().reshape(-1)
    return (np.arange(NBUCKET)[:, None] == table[None, :]).astype(np.float32)


def _swa_mask(first):
    qi = lax.broadcasted_iota(jnp.int32, (BLK, 2 * BLK), 0)
    kj = lax.broadcasted_iota(jnp.int32, (BLK, 2 * BLK), 1)
    dist = BLK + qi - kj
    window = (dist >= 0) & (dist < BLK)
    return window & ((kj >= BLK) | jnp.logical_not(first))


def _bias_expand(rel_bias_t):
    onehot = jnp.asarray(_bucket_onehot_t())

    def kern(r_ref, oh_ref, o_ref):
        o_ref[...] = _raw_dot(r_ref[...], oh_ref[...], "nn", True)

    return pl.pallas_call(
        kern, name="bias_expand", out_shape=SDS((SWA_H, BLK * 2 * BLK), F32), compiler_params=_cparams(),
    )(rel_bias_t, onehot)


def _bias_reduce(d_bias_flat):
    onehot = jnp.asarray(_bucket_onehot_t())

    def kern(d_ref, oh_ref, o_ref):
        o_ref[...] = _raw_dot(d_ref[...], oh_ref[...], "nt", True)

    return pl.pallas_call(
        kern, name="bias_reduce", out_shape=SDS((SWA_H, NBUCKET), F32), compiler_params=_cparams(),
    )(d_bias_flat, onehot)


def _swa_specs(nb, rev):
    def blk(n):
        return (nb - 1 - n) if rev else n

    def before(n):
        return jnp.maximum(blk(n) - 1, 0)

    q_spec = pl.BlockSpec((BLK, SWAW), lambda n: (blk(n), C_SQ // SWAW))
    k_cur = pl.BlockSpec((BLK, SWAKW), lambda n: (blk(n), C_SK // SWAKW))
    k_prev = pl.BlockSpec((BLK, SWAKW), lambda n: (before(n), C_SK // SWAKW))
    v_cur = pl.BlockSpec((BLK, SWAKW), lambda n: (blk(n), C_SV // SWAKW))
    v_prev = pl.BlockSpec((BLK, SWAKW), lambda n: (before(n), C_SV // SWAKW))
    bias = pl.BlockSpec((SWA_H, BLK, 2 * BLK), lambda n: (0, 0, 0))
    gain = pl.BlockSpec((1, SWA_D), lambda n: (0, 0))
    sink = pl.BlockSpec((SWA_KV, 1, SWA_G), lambda n: (0, 0, 0))
    wide = pl.BlockSpec((BLK, SWAW), lambda n: (blk(n), 0))
    narrow = pl.BlockSpec((BLK, SWAKW), lambda n: (blk(n), 0))
    return [q_spec, k_prev, k_cur, v_prev, v_cur, bias, gain, gain, sink], wide, narrow


def _split_heads(x):
    return jnp.stack([x[:, h * SWA_D:(h + 1) * SWA_D] for h in range(x.shape[1] // SWA_D)], axis=0)


def _join_heads(x):
    return jnp.concatenate([x[h] for h in range(x.shape[0])], axis=1)


def _swa_fwd(proj, bias, qg, kg, sinks):
    s_len = proj.shape[0]
    nb = s_len // BLK
    in_specs, wide, _ = _swa_specs(nb, False)

    def kern(q_ref, kp_ref, kc_ref, vp_ref, vc_ref, b_ref, qg_ref, kg_ref, s_ref, o_ref):
        mask = _swa_mask(pl.program_id(0) == 0)
        o8 = _f_swa(*[_split_heads(r[...]) for r in (q_ref, kp_ref, kc_ref, vp_ref, vc_ref)], b_ref[...], qg_ref[...],
                    kg_ref[...], s_ref[...], mask)
        o_ref[...] = _join_heads(o8).astype(BF16)

    return _pc(kern, "swa_fwd", (nb,), in_specs, wide, SDS((s_len, SWAW), BF16))(
        proj, proj, proj, proj, proj, bias, qg, kg, sinks)


def _swa_bwd(proj, bias, qg, kg, sinks, d_out):
    s_len = proj.shape[0]
    nb = s_len // BLK
    in_specs, wide, narrow = _swa_specs(nb, True)

    def kern(q_ref, kp_ref, kc_ref, vp_ref, vc_ref, b_ref, qg_ref, kg_ref, s_ref, do_ref,
             dq_ref, dk_ref, dv_ref, db_ref, dqg_ref, dkg_ref, ds_ref, carry_k, carry_v):
        n = pl.program_id(0)
        mask = _swa_mask(n == nb - 1)
        _zero_first([carry_k, carry_v, db_ref, ds_ref, dqg_ref, dkg_ref])
        fn = functools.partial(_f_swa, mask=mask)
        _, vjp = jax.vjp(fn, *[_split_heads(r[...]) for r in (q_ref, kp_ref, kc_ref, vp_ref, vc_ref)], b_ref[...],
                         qg_ref[...], kg_ref[...], s_ref[...])
        dq, dkp, dkc, dvp, dvc, dbias, dqg, dkg, dsink = vjp(_split_heads(do_ref[...]))
        dq_ref[...] = _join_heads(dq).astype(BF16)
        dk_ref[...] = (_join_heads(dkc) + carry_k[...]).astype(BF16)
        dv_ref[...] = (_join_heads(dvc) + carry_v[...]).astype(BF16)
        carry_k[...] = _join_heads(dkp)
        carry_v[...] = _join_heads(dvp)
        db_ref[...] += dbias
        dqg_ref[...] += dqg
        dkg_ref[...] += dkg
        ds_ref[...] += dsink

    bias_spec, gain, sink = in_specs[5], in_specs[6], in_specs[8]
    return _pc(
        kern, "swa_bwd", (nb,), in_specs + [wide], [wide, narrow, narrow, bias_spec, gain, gain, sink],
        [SDS((s_len, SWAW), BF16), SDS((s_len, SWAKW), BF16), SDS((s_len, SWAKW), BF16),
         SDS((SWA_H, BLK, 2 * BLK), F32), SDS((1, SWA_D), F32), SDS((1, SWA_D), F32), SDS((SWA_KV, 1, SWA_G), F32)],
        scratch=[pltpu.VMEM((BLK, SWAKW), F32), pltpu.VMEM((BLK, SWAKW), F32)],
    )(proj, proj, proj, proj, proj, bias, qg, kg, sinks, d_out)


def _branch_merge(y_dn, y_swa, wa, wb, proj):
    s_len = y_dn.shape[0]
    tm = min(512, s_len)

    def kern(ya_ref, yb_ref, wa_ref, wb_ref, ga_ref, gb_ref, pa_ref, pb_ref, m_ref):
        pa = _bdot(ya_ref[...], wa_ref[0])
        pb = _bdot(yb_ref[...], wb_ref[0])
        pa_ref[...] = pa.astype(BF16)
        pb_ref[...] = pb.astype(BF16)
        m_ref[...] = _f_merge(pa, pb, ga_ref[...], gb_ref[...]).astype(BF16)

    y_spec = pl.BlockSpec((tm, DNW), lambda i, s: (i, 0))
    w_spec = pl.BlockSpec((1, DNW, CSH), lambda i, s: (s, 0, 0))
    o_spec = pl.BlockSpec((tm, CSH), lambda i, s: (i, s))
    ga_spec = pl.BlockSpec((tm, CSH), lambda i, s: (i, C_GATE // CSH + s))
    gb_spec = pl.BlockSpec((tm, CSH), lambda i, s: (i, (C_GATE + D) // CSH + s))
    return _pc(kern, "branch_merge", (s_len // tm, N_CHIPS), [y_spec, y_spec, w_spec, w_spec, ga_spec, gb_spec],
               [o_spec] * 3, [SDS((s_len, D), BF16)] * 3,
               )(y_dn, y_swa, wa, wb, proj, proj)


def _in_proj(x, gain, w_in_p):
    s_len = x.shape[0]
    tm = min(256, s_len)

    def kern(x_ref, g_ref, w_ref, h_ref, p_ref):
        h = _f_rms(x_ref[...], g_ref[...]).astype(BF16)
        h_ref[...] = h
        p_ref[...] = _bdot(h, w_ref[...])

    row = pl.BlockSpec((tm, D), lambda i: (i, 0))
    return _pc(kern, "in_proj", (s_len // tm,),
               [row, pl.BlockSpec((1, D), lambda i: (0, 0)), pl.BlockSpec((D, PW), lambda i: (0, 0))],
               [row, pl.BlockSpec((tm, PW), lambda i: (i, 0))],
               [SDS((s_len, D), BF16), SDS((s_len, PW), F32)])(x, gain, w_in_p)


def _out_proj(merged, w_out, x, gain):
    s_len = x.shape[0]
    tm = min(256, s_len)

    def kern(m_ref, w_ref, x_ref, g_ref, x1_ref, h2_ref):
        x1 = x_ref[...] + _bdot(m_ref[...], w_ref[...])
        x1_ref[...] = x1
        h2_ref[...] = _f_rms(x1, g_ref[...]).astype(BF16)

    row = pl.BlockSpec((tm, D), lambda i: (i, 0))
    return _pc(kern, "out_proj", (s_len // tm,),
               [row, pl.BlockSpec((D, D), lambda i: (0, 0)), row, pl.BlockSpec((1, D), lambda i: (0, 0))],
               [row, row], [SDS((s_len, D), F32), SDS((s_len, D), BF16)])(merged, w_out, x, gain)


def _ffn_up(h2, wg, wu):
    s_len = h2.shape[0]
    tm = min(512, s_len)

    def kern(h_ref, g_ref, u_ref, gt_ref, up_ref, act_ref):
        h = h_ref[...]
        g = _bdot(h, g_ref[0], "nt")
        u = _bdot(h, u_ref[0], "nt")
        gt_ref[0] = g.astype(BF16)
        up_ref[0] = u.astype(BF16)
        act_ref[0] = _f_swiglu(g, u).astype(BF16)

    w_spec = pl.BlockSpec((1, FSH, D), lambda s, i: (s, 0, 0))
    o_spec = pl.BlockSpec((1, tm, FSH), lambda s, i: (s, i, 0))
    shape = (N_CHIPS, s_len, FSH)
    return _pc(kern, "ffn_up", (N_CHIPS, s_len // tm), [pl.BlockSpec((tm, D), lambda s, i: (i, 0)), w_spec, w_spec],
               [o_spec] * 3, [SDS(shape, BF16)] * 3)(h2, wg, wu)


def _ffn_down_loss(act, wd, x1, target):
    s_len = x1.shape[0]
    tm = min(256, s_len)

    def kern(a_ref, w_ref, x_ref, t_ref, dy_ref, dyb_ref, loss_ref):
        _zero_first([loss_ref])
        y = x_ref[...]
        for s in range(N_CHIPS):
            y = y + _bdot(a_ref[s], w_ref[s])
        d = y - t_ref[...]
        dy = d * (1.0 / D)
        dy_ref[...] = dy
        dyb_ref[...] = dy.astype(BF16)
        loss_ref[...] += jnp.sum(d * d).reshape(1, 1) * (0.5 / D)

    row = pl.BlockSpec((tm, D), lambda i: (i, 0))
    return _pc(kern, "ffn_down_loss", (s_len // tm,),
               [pl.BlockSpec((N_CHIPS, tm, FSH), lambda i: (0, i, 0)),
                pl.BlockSpec((N_CHIPS, FSH, D), lambda i: (0, 0, 0)), row, row],
               [row, row, pl.BlockSpec((1, 1), lambda i: (0, 0))],
               [SDS((s_len, D), F32), SDS((s_len, D), BF16), SDS((1, 1), F32)])(act, wd, x1, target)


def _ffn_dact(dy_b, wd, gt, up):
    s_len = dy_b.shape[0]
    tm = min(512, s_len)

    def kern(dy_ref, w_ref, gt_ref, up_ref, dg_ref, du_ref):
        d_act = _bdot(dy_ref[...], w_ref[0], "nt")
        _, vjp = jax.vjp(_f_swiglu, gt_ref[0].astype(F32), up_ref[0].astype(F32))
        dg, du = vjp(d_act)
        dg_ref[0] = dg.astype(BF16)
        du_ref[0] = du.astype(BF16)

    a_spec = pl.BlockSpec((1, tm, FSH), lambda s, i: (s, i, 0))
    shape = (N_CHIPS, s_len, FSH)
    return _pc(kern, "ffn_dact", (N_CHIPS, s_len // tm),
               [pl.BlockSpec((tm, D), lambda s, i: (i, 0)), pl.BlockSpec((1, FSH, D), lambda s, i: (s, 0, 0)),
                a_spec, a_spec],
               [a_spec, a_spec], [SDS(shape, BF16), SDS(shape, BF16)])(dy_b, wd, gt, up)


def _gw_ffn(lhs, rhs, name):
    s_len = rhs.shape[0]
    n = len(lhs)
    tn = 512

    def kern(*refs):
        g = refs[n][...]
        for i in range(n):
            refs[n + 1 + i][0] = _bdot(refs[i][0], g, "tn").astype(BF16)

    a_spec = pl.BlockSpec((1, s_len, FSH), lambda s, j: (s, 0, 0))
    o_spec = pl.BlockSpec((1, FSH, tn), lambda s, j: (s, 0, j))
    return _pc(kern, name, (N_CHIPS, D // tn), [a_spec] * n + [pl.BlockSpec((s_len, tn), lambda s, j: (0, j))],
               [o_spec] * n, [SDS((N_CHIPS, FSH, D), BF16)] * n)(*lhs, rhs)


def _ffn_dh2(d_gt, d_up, wg, wu, x1, dy, gain):
    s_len = x1.shape[0]
    tm = min(256, s_len)

    def kern(dg_ref, du_ref, wg_ref, wu_ref, x_ref, dy_ref, g_ref, dx_ref, dxb_ref, dgain_ref):
        _zero_first([dgain_ref])
        dh2 = jnp.zeros((tm, D), F32)
        for s in range(N_CHIPS):
            dh2 = dh2 + _bdot(dg_ref[s], wg_ref[s]) + _bdot(du_ref[s], wu_ref[s])
        _, vjp = jax.vjp(_f_rms, x_ref[...], g_ref[...])
        dx, dgain = vjp(dh2)
        dx1 = dx + dy_ref[...]
        dx_ref[...] = dx1
        dxb_ref[...] = dx1.astype(BF16)
        dgain_ref[...] += dgain

    row = pl.BlockSpec((tm, D), lambda i: (i, 0))
    d_spec = pl.BlockSpec((N_CHIPS, tm, FSH), lambda i: (0, i, 0))
    w_spec = pl.BlockSpec((N_CHIPS, FSH, D), lambda i: (0, 0, 0))
    vec = pl.BlockSpec((1, D), lambda i: (0, 0))
    return _pc(kern, "ffn_dh2", (s_len // tm,), [d_spec, d_spec, w_spec, w_spec, row, row, vec],
               [row, row, vec], [SDS((s_len, D), F32), SDS((s_len, D), BF16), SDS((1, D), F32)],
               )(d_gt, d_up, wg, wu, x1, dy, gain)


def _merge_bwd(dx1_b, w_out, pa, pb, proj):
    s_len = dx1_b.shape[0]
    tm = min(256, s_len)

    def kern(dx_ref, w_ref, pa_ref, pb_ref, g_ref, dpa_ref, dpb_ref, dg_ref):
        dm = _bdot(dx_ref[...], w_ref[...], "nt")
        gates = g_ref[...]
        _, vjp = jax.vjp(_f_merge, pa_ref[...].astype(F32), pb_ref[...].astype(F32), gates[:, :D], gates[:, D:])
        dpa, dpb, dga, dgb = vjp(dm)
        dpa_ref[...] = dpa.astype(BF16)
        dpb_ref[...] = dpb.astype(BF16)
        dg_ref[:, :D] = dga.astype(BF16)
        dg_ref[:, D:] = dgb.astype(BF16)

    row = pl.BlockSpec((tm, D), lambda i: (i, 0))
    return _pc(kern, "merge_bwd", (s_len // tm,),
               [row, pl.BlockSpec((D, D), lambda i: (0, 0)), row, row,
                pl.BlockSpec((tm, 2 * D), lambda i: (i, C_GATE // (2 * D)))],
               [row, row, pl.BlockSpec((tm, 2 * D), lambda i: (i, 0))],
               [SDS((s_len, D), BF16), SDS((s_len, D), BF16), SDS((s_len, 2 * D), BF16)],
               )(dx1_b, w_out, pa, pb, proj)


def _d_branch(d_pa, d_pb, wa, wb):
    s_len = d_pa.shape[0]
    tm = min(512, s_len)

    def kern(da_ref, db_ref, wa_ref, wb_ref, oa_ref, ob_ref):
        acc_a = jnp.zeros((tm, DNW), F32)
        acc_b = jnp.zeros((tm, SWAW), F32)
        for s in range(N_CHIPS):
            acc_a = acc_a + _bdot(da_ref[:, s * CSH:(s + 1) * CSH], wa_ref[s], "nt")
            acc_b = acc_b + _bdot(db_ref[:, s * CSH:(s + 1) * CSH], wb_ref[s], "nt")
        oa_ref[...] = acc_a
        ob_ref[...] = acc_b

    row = pl.BlockSpec((tm, D), lambda i: (i, 0))
    w_spec = pl.BlockSpec((N_CHIPS, DNW, CSH), lambda i: (0, 0, 0))
    out = pl.BlockSpec((tm, DNW), lambda i: (i, 0))
    return _pc(kern, "d_branch", (s_len // tm,), [row, row, w_spec, w_spec], [out, out],
               [SDS((s_len, DNW), F32), SDS((s_len, SWAW), F32)])(d_pa, d_pb, wa, wb)


def _gw_branch(y_dn, y_swa, d_pa, d_pb):
    s_len = y_dn.shape[0]

    def kern(ya_ref, yb_ref, da_ref, db_ref, oa_ref, ob_ref):
        oa_ref[0] = _bdot(ya_ref[...], da_ref[...], "tn").astype(BF16)
        ob_ref[0] = _bdot(yb_ref[...], db_ref[...], "tn").astype(BF16)

    y_spec = pl.BlockSpec((s_len, DNW), lambda s: (0, 0))
    d_spec = pl.BlockSpec((s_len, CSH), lambda s: (0, s))
    o_spec = pl.BlockSpec((1, DNW, CSH), lambda s: (s, 0, 0))
    shape = (N_CHIPS, DNW, CSH)
    return _pc(kern, "gw_branch", (N_CHIPS,), [y_spec, y_spec, d_spec, d_spec], [o_spec, o_spec],
               [SDS(shape, BF16), SDS(shape, BF16)])(y_dn, y_swa, d_pa, d_pb)


def _dh_rms(d_proj, w_in_p, x, dx1, gain):
    s_len = x.shape[0]
    tm = min(256, s_len)

    def kern(dp_ref, w_ref, x_ref, r_ref, g_ref, gx_ref, dgain_ref):
        _zero_first([dgain_ref])
        dh = _bdot(dp_ref[...], w_ref[...], "nt")
        _, vjp = jax.vjp(_f_rms, x_ref[...], g_ref[...])
        dx, dgain = vjp(dh)
        gx_ref[...] = dx + r_ref[...]
        dgain_ref[...] += dgain

    row = pl.BlockSpec((tm, D), lambda i: (i, 0))
    vec = pl.BlockSpec((1, D), lambda i: (0, 0))
    return _pc(kern, "dh_rms", (s_len // tm,),
               [pl.BlockSpec((tm, PW), lambda i: (i, 0)), pl.BlockSpec((D, PW), lambda i: (0, 0)), row, row, vec],
               [row, vec], [SDS((s_len, D), F32), SDS((1, D), F32)])(d_proj, w_in_p, x, dx1, gain)


HALO = 8


def _rows_down(x, n, above):
    tm = x.shape[0]
    r = pltpu.roll(x, n, 0)
    a = pltpu.roll(above, n, 0)
    top = jnp.where(lax.broadcasted_iota(jnp.int32, above.shape, 0) < n, a, r[0:HALO])
    return jnp.concatenate([top, r[HALO:tm]], axis=0)


def _rows_up(x, n, below):
    tm = x.shape[0]
    r = pltpu.roll(x, tm - n, 0)
    b = pltpu.roll(below, HALO - n, 0)
    bottom = jnp.where(lax.broadcasted_iota(jnp.int32, below.shape, 0) >= HALO - n, b, r[tm - HALO:tm])
    return jnp.concatenate([r[0:tm - HALO], bottom], axis=0)


def _conv_taps(cur_ref, prev_ref, first):
    cur = cur_ref[...]
    above = jnp.where(first, 0.0, prev_ref[...])
    return [_rows_down(cur, n, above) for n in range(CONV - 1, 0, -1)] + [cur]


def _dn_pre_specs(s_len, tm, blk):
    cur = pl.BlockSpec((tm, QKVW), lambda i: (blk(i), 0))
    prev = pl.BlockSpec((HALO, QKVW), lambda i: (jnp.maximum(blk(i) * (tm // HALO) - 1, 0), 0))
    ba = pl.BlockSpec((tm, 128), lambda i: (blk(i), C_BA // 128))
    row = pl.BlockSpec((tm, DNW), lambda i: (blk(i), 0))
    full = [pl.BlockSpec((CONV, QKVW), lambda i: (0, 0)), pl.BlockSpec((1, DN_H), lambda i: (0, 0)),
            pl.BlockSpec((1, DN_H), lambda i: (0, 0))]
    return cur, prev, ba, row, full


def _dn_pre_fwd(proj, conv_w, alog, dtb):
    s_len = proj.shape[0]
    tm = min(128, s_len)
    cur, prev, ba, row, full = _dn_pre_specs(s_len, tm, lambda i: i)

    def kern(cur_ref, prev_ref, ba_ref, cw_ref, al_ref, dt_ref, q_ref, k_ref, v_ref, bb_ref, gb_ref):
        xs = _conv_taps(cur_ref, prev_ref, pl.program_id(0) == 0)
        outs = _f_dn_pre(*xs, ba_ref[...], cw_ref[...], al_ref[...], dt_ref[...])
        for ref, val in zip((q_ref, k_ref, v_ref, bb_ref, gb_ref), outs, strict=True):
            ref[...] = val

    return _pc(kern, "dn_pre_fwd", (s_len // tm,), [cur, prev, ba] + full, [row] * 5,
               [SDS((s_len, DNW), F32)] * 5)(proj, proj, proj, conv_w, alog, dtb)


def _dn_pre_bwd(proj, conv_w, alog, dtb, cots, others):
    s_len = proj.shape[0]
    tm = min(128, s_len)
    nb = s_len // tm
    cur, prev, ba, row, full = _dn_pre_specs(s_len, tm, lambda i: nb - 1 - i)
    n_o = len(others)
    assert QKVW + sum(t.shape[1] for t in others) + 128 == C_BA + 128

    def kern(cur_ref, prev_ref, ba_ref, cw_ref, al_ref, dt_ref, dq_ref, dk_ref, dv_ref, dbb_ref, dgb_ref, *rest):
        o_refs = rest[:n_o]
        dproj_ref, dcw_ref, dal_ref, ddt_ref, *tails = rest[n_o:]
        i = pl.program_id(0)
        _zero_first([dcw_ref, dal_ref, ddt_ref] + tails)
        xs = _conv_taps(cur_ref, prev_ref, i == nb - 1)
        _, vjp = jax.vjp(_f_dn_pre, *xs, ba_ref[...], cw_ref[...], al_ref[...], dt_ref[...])
        *dxs, dba, dcw, dal, ddt = vjp((dq_ref[...], dk_ref[...], dv_ref[...], dbb_ref[...], dgb_ref[...]))
        total = dxs[CONV - 1]
        for j, t in enumerate(tails):
            n = CONV - 1 - j
            total = total + _rows_up(dxs[j], n, t[...])
            t[...] = dxs[j][0:HALO, :]
        dproj_ref[...] = jnp.concatenate(
            [total.astype(BF16)] + [r[...] for r in o_refs] + [dba.astype(BF16), jnp.zeros((tm, PW - C_BA - 128), BF16)],
            axis=1)
        dcw_ref[...] += dcw
        dal_ref[...] += dal
        ddt_ref[...] += ddt

    o_specs = [pl.BlockSpec((tm, t.shape[1]), lambda i: (nb - 1 - i, 0)) for t in others]
    return _pc(kern, "dn_pre_bwd", (nb,), [cur, prev, ba] + full + [row] * 5 + o_specs,
               [pl.BlockSpec((tm, PW), lambda i: (nb - 1 - i, 0))] + full,
               [SDS((s_len, PW), BF16), SDS((CONV, QKVW), F32), SDS((1, DN_H), F32), SDS((1, DN_H), F32)],
               scratch=[pltpu.VMEM((HALO, QKVW), F32)] * (CONV - 1))(proj, proj, proj, conv_w, alog, dtb, *cots, *others)


def _w_in_to_padded(w_sh):
    tr = 256

    def kern(w_ref, o_ref):
        full = jnp.concatenate([w_ref[s] for s in range(N_CHIPS)], axis=1)
        pieces = [full[:, o0:o0 + w] for o0, w, _ in sorted(_ORIG_PIECES, key=lambda t: t[2])]
        o_ref[...] = jnp.concatenate(pieces + [jnp.zeros((tr, PW - D_IN), w_ref.dtype)], axis=1)

    return _pc(kern, "w_in_to_padded", (D // tr,), [pl.BlockSpec((N_CHIPS, tr, D_IN // N_CHIPS), lambda i: (0, i, 0))],
               pl.BlockSpec((tr, PW), lambda i: (i, 0)), SDS((D, PW), w_sh.dtype))(w_sh)


def _padded_to_w_in(g):
    tr = 256
    csh = D_IN // N_CHIPS

    def kern(g_ref, o_ref):
        x = g_ref[...]
        full = jnp.concatenate([x[:, p0:p0 + w] for _, w, p0 in _ORIG_PIECES], axis=1)
        for s in range(N_CHIPS):
            o_ref[s] = full[:, s * csh:(s + 1) * csh]

    return _pc(kern, "padded_to_w_in", (D // tr,), [pl.BlockSpec((tr, PW), lambda i: (i, 0))],
               pl.BlockSpec((N_CHIPS, tr, csh), lambda i: (0, i, 0)), SDS((N_CHIPS, D, csh), g.dtype))(g)


def _pad_w_in(w_in):
    pieces = [w_in[:, o0:o0 + w] for o0, w, _ in sorted(_ORIG_PIECES, key=lambda t: t[2])]
    pieces.append(jnp.zeros((w_in.shape[0], PW - D_IN), w_in.dtype))
    return jnp.concatenate(pieces, axis=1)


def _unpad_w_in(g):
    return jnp.concatenate([g[:, p0:p0 + w] for _, w, p0 in _ORIG_PIECES], axis=1)


def _local_step(x, target, wts):
    s_len = x.shape[0]
    tm = min(256, s_len)
    w_in_p = wts["w_in_p"]
    attn_gain = wts["attn_norm"]
    ffn_gain = wts["ffn_norm"]
    conv_w = wts["dn_conv"]
    alog, dtb, out_gain = wts["dn_a_log"], wts["dn_dt_bias"], wts["dn_out_norm"]
    qg, kg = wts["swa_q_norm"], wts["swa_k_norm"]
    sinks = wts["swa_sinks"].reshape(SWA_KV, 1, SWA_G)

    h, proj = _in_proj(x, attn_gain, w_in_p)
    q_dn, k_dn, v_dn, bb, gb = _dn_pre_fwd(proj, conv_w, alog, dtb)
    o_dn, s_all, t_all = _dn_chunks_fwd(q_dn, k_dn, v_dn, gb, bb)
    post_ins = [_whole(o_dn), (proj, DNW, C_Z // DNW)]
    (y_dn,) = _rows(lambda r, f: ([_f_dn_post(r[0], r[1], f[0])], []), "dn_post_fwd", s_len, tm, post_ins,
                    [out_gain], [(DNW, BF16)])

    bias = _bias_expand(wts["rel_bias"].T).reshape(SWA_H, BLK, 2 * BLK)
    y_swa = _swa_fwd(proj, bias, qg, kg, sinks)

    wts = {**wts, **wts["late"](y_swa)}
    p_a, p_b, merged = _branch_merge(y_dn, y_swa, wts["wa"], wts["wb"], proj)
    x1, h2 = _out_proj(merged, wts["w_out"], x, ffn_gain)
    gt, up, act = _ffn_up(h2, wts["wg"], wts["wu"])
    dy, dy_b, loss = _ffn_down_loss(act, wts["wd"], x1, target)

    grads = {}
    d_gt, d_up = _ffn_dact(dy_b, wts["wd"], gt, up)
    (grads["w_down"],) = _gw_ffn([act], dy_b, "gw_down")
    grads["w_gate"], grads["w_up"] = _gw_ffn([d_gt, d_up], h2, "gw_gate_up")
    dx1, dx1_b, grads["ffn_norm"] = _ffn_dh2(d_gt, d_up, wts["wg"], wts["wu"], x1, dy, ffn_gain)
    grads["w_out"] = _mm(merged, dx1_b, "tn", BF16, 512, 512, "gw_out")
    d_pa, d_pb, d_gr = _merge_bwd(dx1_b, wts["w_out"], p_a, p_b, proj)
    d_ydn, d_yswa = _d_branch(d_pa, d_pb, wts["wa"], wts["wb"])
    grads["w_branch_dn"], grads["w_branch_swa"] = _gw_branch(y_dn, y_swa, d_pa, d_pb)
    token = wts["send_early"](grads)
    qg_t = qg + token[0:1, 0:1]
    out_gain_t = out_gain + token[0:1, 0:1]

    d_sq, d_sk, d_sv, d_bias, grads["swa_q_norm"], grads["swa_k_norm"], d_sinks = _swa_bwd(
        proj, bias, qg_t, kg, sinks, d_yswa)
    grads["swa_sinks"] = d_sinks.reshape(1, SWA_H)
    grads["rel_bias"] = _bias_reduce(d_bias.reshape(SWA_H, BLK * 2 * BLK)).T

    def post_bwd(r, f):
        _, vjp = jax.vjp(_f_dn_post, r[0], r[1], f[0])
        d_o, d_z, d_gain = vjp(r[2])
        return [d_o, d_z], [d_gain]

    d_o, d_z, grads["dn_out_norm"] = _rows(post_bwd, "dn_post_bwd", s_len, tm, post_ins + [_whole(d_ydn)], [out_gain_t],
                                           [(DNW, F32), (DNW, BF16)], [(1, DH)])
    d_q, d_k, d_v, d_gb, d_bb = _dn_chunks_bwd(q_dn, k_dn, v_dn, gb, bb, s_all, t_all, d_o)

    d_proj, grads["dn_conv"], grads["dn_a_log"], grads["dn_dt_bias"] = _dn_pre_bwd(
        proj, conv_w, alog, dtb, (d_q, d_k, d_v, d_bb, d_gb), (d_z, d_gr, d_sq, d_sk, d_sv))
    grads["w_in_p"] = _mm(h, d_proj, "tn", BF16, 512, 1024, "gw_in")
    token = wts["send_in"](grads["w_in_p"])
    grad_x, grads["attn_norm"] = _dh_rms(d_proj, w_in_p, x, dx1, attn_gain + token[0:1, 0:1])
    return loss, grad_x, grads


_HBM = pl.BlockSpec(memory_space=pl.ANY)


def _place():
    return lax.axis_index("x"), lax.axis_index("y"), lax.axis_index("c")


def _other_chips(x, y):
    return [(1 - x, y), (x, 1 - y), (1 - x, 1 - y)]


def _rcopy(src, dst, send_sems, recv_sems, k, to):
    return pltpu.make_async_remote_copy(src_ref=src, dst_ref=dst, send_sem=send_sems.at[k], recv_sem=recv_sems.at[k],
                                        device_id=to, device_id_type=MESH)


def _comm_call(body, name, ins, out_shapes, n_remote, landing=0):
    first = len(ins) - landing
    return pl.pallas_call(
        body, name=name, in_specs=[_HBM] * len(ins), out_specs=[_HBM] * len(out_shapes), out_shape=out_shapes,
        scratch_shapes=[pltpu.SemaphoreType.DMA((n_remote,)), pltpu.SemaphoreType.DMA((n_remote,))],
        input_output_aliases={first + i: i for i in range(landing)},
        compiler_params=_cparams(has_side_effects=True),
    )(*ins)


def _own_slot(blocks, chip):
    return [lax.dynamic_update_slice(lax.empty((N_CHIPS,) + b.shape, b.dtype), b[None], (chip, 0, 0)) for b in blocks]


def _gather_weights(ws, chip):
    n = len(ws)
    halves = [w.shape[0] // 2 for w in ws]

    def body(*refs):
        w_refs, o_refs = refs[:n], refs[2 * n:3 * n]
        send_sems, recv_sems = refs[3 * n:]
        x, y, c = _place()
        s = 2 * x + y
        sib = (x, y, 1 - c)
        chips = _other_chips(x, y)

        def rows(i, half):
            return pl.ds(half * halves[i], halves[i])

        first = []
        for j, (cx, cy) in enumerate(chips):
            for i in range(n):
                cp = _rcopy(w_refs[i].at[rows(i, c), :], o_refs[i].at[s, rows(i, c), :], send_sems, recv_sems,
                            j * n + i, (cx, cy, c))
                cp.start()
                first.append(cp)
        passed = []
        for j, (cx, cy) in enumerate(chips):
            sj = 2 * cx + cy
            for i in range(n):
                blk = o_refs[i].at[sj, rows(i, c), :]
                _rcopy(blk, blk, send_sems, recv_sems, j * n + i, (cx, cy, c)).wait_recv()
                cp = _rcopy(blk, blk, send_sems, recv_sems, (3 + j) * n + i, sib)
                cp.start()
                passed.append(cp)
        for j, (cx, cy) in enumerate(chips):
            sj = 2 * cx + cy
            for i in range(n):
                blk = o_refs[i].at[sj, rows(i, 1 - c), :]
                _rcopy(blk, blk, send_sems, recv_sems, (3 + j) * n + i, sib).wait_recv()
        for cp in first + passed:
            cp.wait_send()

    return _comm_call(body, "gather_weights", list(ws) + _own_slot(ws, chip),
                      [SDS((N_CHIPS,) + w.shape, w.dtype) for w in ws], 6 * n, landing=n)


_HBM_ONLY = pl.BlockSpec(memory_space=pltpu.HBM)
_SEM = pl.BlockSpec(memory_space=pltpu.SEMAPHORE)
_DATAFLOW = pltpu.SideEffectType.DATAFLOW_SIDE_EFFECTING


def _in_hbm(a):
    return pltpu.with_memory_space_constraint(a, pltpu.HBM)


def _gather_windows(blocks):
    halves = [b.shape[0] // 2 for b in blocks]

    def src_at(ref, i, c, sj):
        return ref.at[pl.ds(c * halves[i], halves[i]), :]

    def dst_at(ref, i, c, s_from):
        return ref.at[s_from, pl.ds(c * halves[i], halves[i]), :]

    return src_at, dst_at


def _exchange_windows():
    return (lambda ref, i, c, sj: ref.at[sj]), (lambda ref, i, c, s_from: ref.at[s_from])


def _split_start(name, ws, lands, dep, windows):
    n = len(ws)
    src_at, dst_at = windows

    def body(*refs):
        w_refs, l_refs = refs[:n], refs[n:2 * n]
        send_sems, recv_sems = refs[2 * n + 1], refs[2 * n + 2]
        token = refs[-1]
        x, y, c = _place()
        s = 2 * x + y
        for j, (cx, cy) in enumerate(_other_chips(x, y)):
            for i in range(n):
                _rcopy(src_at(w_refs[i], i, c, 2 * cx + cy), dst_at(l_refs[i], i, c, s), send_sems, recv_sems,
                       j * n + i, (cx, cy, c)).start()
        token[...] = jnp.zeros_like(token)

    outs = pl.pallas_call(
        body, name=name,
        out_shape=(pltpu.SemaphoreType.DMA((3 * n,)), pltpu.SemaphoreType.DMA((3 * n,)),
                   *[pltpu.HBM(w.shape, w.dtype) for w in ws], *[pltpu.HBM(t.shape, t.dtype) for t in lands],
                   SDS((8, 128), F32)),
        in_specs=[_HBM_ONLY] * (2 * n) + [pl.BlockSpec(memory_space=pl.ANY)],
        out_specs=(_SEM, _SEM, *[_HBM_ONLY] * (2 * n), pl.BlockSpec(memory_space=pltpu.VMEM)),
        input_output_aliases={i: 2 + i for i in range(2 * n)},
        compiler_params=pltpu.CompilerParams(has_side_effects=_DATAFLOW),
    )(*[_in_hbm(w) for w in ws], *[_in_hbm(t) for t in lands], dep)
    return outs[0], outs[1], outs[2:2 + n], outs[2 + n:2 + 2 * n], outs[-1]


def _split_wait(name, w_thru, l_thru, send_sems, recv_sems, after, windows):
    n = len(w_thru)
    src_at, dst_at = windows

    def body(*refs):
        w_refs, l_refs = refs[:n], refs[n:2 * n]
        send_sems, recv_sems = refs[2 * n], refs[2 * n + 1]
        x, y, c = _place()
        for j, (cx, cy) in enumerate(_other_chips(x, y)):
            sj = 2 * cx + cy
            for i in range(n):
                cp = _rcopy(src_at(w_refs[i], i, c, sj), dst_at(l_refs[i], i, c, sj), send_sems, recv_sems, j * n + i,
                            (cx, cy, c))
                cp.wait_send()
                cp.wait_recv()

    outs = pl.pallas_call(
        body, name=name,
        out_shape=[pltpu.HBM(w.shape, w.dtype) for w in w_thru] + [pltpu.HBM(t.shape, t.dtype) for t in l_thru],
        in_specs=[_HBM_ONLY] * (2 * n) + [_SEM, _SEM, pl.BlockSpec(memory_space=pl.ANY)],
        out_specs=[_HBM_ONLY] * (2 * n),
        input_output_aliases={i: i for i in range(2 * n)},
        compiler_params=pltpu.CompilerParams(has_side_effects=_DATAFLOW),
    )(*w_thru, *l_thru, send_sems, recv_sems, after)
    return outs[n:]


def _sibling_fill(lands):
    n = len(lands)
    halves = [t.shape[1] // 2 for t in lands]

    def body(*refs):
        o_refs = refs[n:2 * n]
        send_sems, recv_sems = refs[2 * n:]
        x, y, c = _place()
        sib = (x, y, 1 - c)
        chips = _other_chips(x, y)
        sent = []
        for j, (cx, cy) in enumerate(chips):
            for i in range(n):
                blk = o_refs[i].at[2 * cx + cy, pl.ds(c * halves[i], halves[i]), :]
                cp = _rcopy(blk, blk, send_sems, recv_sems, j * n + i, sib)
                cp.start()
                sent.append(cp)
        for j, (cx, cy) in enumerate(chips):
            for i in range(n):
                blk = o_refs[i].at[2 * cx + cy, pl.ds((1 - c) * halves[i], halves[i]), :]
                _rcopy(blk, blk, send_sems, recv_sems, j * n + i, sib).wait_recv()
        for cp in sent:
            cp.wait_send()

    return _comm_call(body, "sibling_fill", list(lands), [SDS(t.shape, t.dtype) for t in lands], 3 * n, landing=n)


def _swap_halves(gs, name):
    n = len(gs)
    halves = [g.shape[1] // 2 for g in gs]

    def body(*refs):
        g_refs, o_refs = refs[:n], refs[n:2 * n]
        send_sems, recv_sems = refs[2 * n:]
        x, y, c = _place()
        cps = [_rcopy(g_refs[i].at[:, pl.ds((1 - c) * halves[i], halves[i]), :], o_refs[i], send_sems, recv_sems, i,
                      (x, y, 1 - c)) for i in range(n)]
        for cp in cps:
            cp.start()
        for cp in cps:
            cp.wait()

    return _comm_call(body, name, gs, [SDS((N_CHIPS, h, g.shape[2]), g.dtype) for g, h in zip(gs, halves)], n)


def _swap_reduced(rs, name):
    n = len(rs)

    def body(*refs):
        r_refs, o_refs = refs[:n], refs[n:2 * n]
        send_sems, recv_sems = refs[2 * n:]
        x, y, c = _place()
        cps = [_rcopy(r_refs[i], o_refs[i], send_sems, recv_sems, i, (x, y, 1 - c)) for i in range(n)]
        for cp in cps:
            cp.start()
        for cp in cps:
            cp.wait()

    return _comm_call(body, name, rs, [SDS(r.shape, r.dtype) for r in rs], n)


def _all_sum_small(vec, name):
    n_dev = 8
    flips = [(bx, by, bc) for bx in (0, 1) for by in (0, 1) for bc in (0, 1)][1:]

    def body(v_ref, out_ref, gath, send_sems, recv_sems):
        x, y, c = _place()
        me = 4 * x + 2 * y + c
        gath[me] = v_ref[...]
        sent = []
        for k, (bx, by, bc) in enumerate(flips):
            peer = (x ^ bx, y ^ by, c ^ bc)
            cp = _rcopy(v_ref, gath.at[me], send_sems, recv_sems, k, peer)
            cp.start()
            sent.append(cp)
        for k, (bx, by, bc) in enumerate(flips):
            peer = (x ^ bx, y ^ by, c ^ bc)
            _rcopy(v_ref, gath.at[4 * peer[0] + 2 * peer[1] + peer[2]], send_sems, recv_sems, k, peer).wait_recv()
        for cp in sent:
            cp.wait_send()
        acc = gath[0]
        for d in range(1, n_dev):
            acc = acc + gath[d]
        out_ref[...] = acc

    vm = pl.BlockSpec(memory_space=pltpu.VMEM)
    return pl.pallas_call(
        body, name=name, in_specs=[vm], out_specs=vm, out_shape=SDS(vec.shape, F32),
        scratch_shapes=[pltpu.VMEM((n_dev,) + vec.shape, F32), pltpu.SemaphoreType.DMA((7,)),
                        pltpu.SemaphoreType.DMA((7,))],
        compiler_params=_cparams(has_side_effects=True),
    )(vec)


def _pack_small(vals, extra=None):
    parts = [vals[n].reshape(-1).astype(F32) for n, _ in _SMALL]
    parts.append(jnp.zeros((1,), F32) if extra is None else extra.reshape(1).astype(F32))
    flat = jnp.concatenate(parts)
    flat = jnp.concatenate([flat, jnp.zeros((_SMALL_ROWS * 128 - flat.shape[0],), F32)])
    return flat.reshape(_SMALL_ROWS, 128)


def _unpack_small(packed, shapes):
    flat = packed.reshape(-1)
    return {n: flat[_SMALL_OFF[n][0]:_SMALL_OFF[n][0] + _SMALL_OFF[n][1]].reshape(shapes[n]) for n, _ in _SMALL}


def _pair_sum(gs, gots, core, name):
    n = len(gs)

    def kern(c_ref, *refs):
        for i in range(n):
            refs[2 * n + i][...] = (refs[i][...].astype(F32) + refs[n + i][...].astype(F32)).astype(BF16)

    in_specs = [pl.BlockSpec((1, t.shape[1], t.shape[2]), lambda s, c_ref: (s, c_ref[0], 0)) for t in gots]
    in_specs += [pl.BlockSpec((1, t.shape[1], t.shape[2]), lambda s, c_ref: (s, 0, 0)) for t in gots]
    out_specs = [pl.BlockSpec((1, t.shape[1], t.shape[2]), lambda s, c_ref: (s, 0, 0)) for t in gots]
    return pl.pallas_call(
        kern, name=name,
        grid_spec=pltpu.PrefetchScalarGridSpec(num_scalar_prefetch=1, grid=(N_CHIPS,), in_specs=in_specs,
                                               out_specs=out_specs),
        out_shape=[SDS(t.shape, BF16) for t in gots],
        compiler_params=_cparams(dimension_semantics=("arbitrary",)),
    )(core.reshape(1).astype(jnp.int32), *gs, *gots)


def _chip_sum(qs, name):
    n = len(qs)

    def kern(*refs):
        for i in range(n):
            acc = refs[i][0].astype(F32)
            for s in range(1, N_CHIPS):
                acc = acc + refs[i][s].astype(F32)
            refs[n + i][...] = acc

    in_specs = [pl.BlockSpec((N_CHIPS, q.shape[1] // 2, q.shape[2]), lambda j: (0, j, 0)) for q in qs]
    out_specs = [pl.BlockSpec((q.shape[1] // 2, q.shape[2]), lambda j: (j, 0)) for q in qs]
    return _pc(kern, name, (2,), in_specs, out_specs, [SDS(q.shape[1:], F32) for q in qs])(*qs)


def _adam_math(w_, g_, m_, v_):
    m_ = ADAM_B1 * m_ + (1.0 - ADAM_B1) * g_
    v_ = ADAM_B2 * v_ + (1.0 - ADAM_B2) * jnp.square(g_)
    m_hat = m_ / (1.0 - ADAM_B1 ** ADAM_STEP)
    v_hat = v_ / (1.0 - ADAM_B2 ** ADAM_STEP)
    return -ADAM_LR * (m_hat / (jnp.sqrt(v_hat) + ADAM_EPS) + ADAM_WD * w_), m_, v_


def _adamw(w, g, m, v, name):
    rows, cols = w.shape
    tr = rows
    for cand in (256, 128, 64, 32, 16, 8):
        if rows % cand == 0 and rows > cand:
            tr = cand
            break

    def kern(w_ref, g_ref, m_ref, v_ref, d_ref, nm_ref, nv_ref):
        d_ref[...], nm_ref[...], nv_ref[...] = _adam_math(w_ref[...], g_ref[...], m_ref[...], v_ref[...])

    spec = pl.BlockSpec((tr, cols), lambda i: (i, 0))
    return _pc(kern, name, (rows // tr,), [spec] * 4, [spec] * 3, [SDS(w.shape, F32)] * 3)(w, g, m, v)


def _adamw_rows1(w, g, m, v, name):
    rows, _, cols = w.shape
    tr = next(t for t in (42, 32, 29, 16, 8, 7, 6, 4, 3, 2, 1) if rows % t == 0)

    def kern(w_ref, g_ref, m_ref, v_ref, go_ref, d_ref, nm_ref, nv_ref):
        g_ = g_ref[...]
        go_ref[...] = g_
        d_ref[...], nm_ref[...], nv_ref[...] = _adam_math(w_ref[...], g_, m_ref[...], v_ref[...])

    spec = pl.BlockSpec((tr, 1, cols), lambda i: (i, 0, 0))
    return _pc(kern, name, (rows // tr,), [spec] * 4, [spec] * 4, [SDS(w.shape, F32)] * 4)(w, g, m, v)


def _adamw_big(w, mine, theirs, m, v, core, name):
    _, rows, cols = w.shape
    half = rows // 2
    tr = next(t for t in (256, 176, 128, 64, 32, 16, 8) if half % t == 0)
    nbh = half // tr

    def kern(c_ref, w_ref, a_ref, b_ref, m_ref, v_ref, g_ref, d_ref, nm_ref, nv_ref):
        g_ = jnp.where(pl.program_id(0) // nbh == c_ref[0], a_ref[...], b_ref[...])
        g_ref[0] = g_
        d_ref[0], nm_ref[0], nv_ref[0] = _adam_math(w_ref[0], g_, m_ref[0], v_ref[0])

    full = pl.BlockSpec((1, tr, cols), lambda i, c_ref: (0, i, 0))
    part = pl.BlockSpec((tr, cols), lambda i, c_ref: (i % nbh, 0))
    return pl.pallas_call(
        kern, name=name,
        grid_spec=pltpu.PrefetchScalarGridSpec(num_scalar_prefetch=1, grid=(rows // tr,),
                                               in_specs=[full, part, part, full, full], out_specs=[full] * 4),
        out_shape=[SDS(w.shape, F32)] * 4,
        compiler_params=_cparams(dimension_semantics=("arbitrary",)),
    )(core.reshape(1).astype(jnp.int32), w, mine, theirs, m, v)


_WEIGHT_NAMES = ("attn_norm", "w_in", "dn_conv", "dn_a_log", "dn_dt_bias", "dn_out_norm", "swa_q_norm", "swa_k_norm",
                 "swa_sinks", "rel_bias", "w_branch_dn", "w_branch_swa", "w_out", "ffn_norm", "w_gate", "w_up",
                 "w_down")
_CONV_SH = QKVW // N_CHIPS


def kernel(x, attn_norm, w_in, dn_conv, dn_a_log, dn_dt_bias, dn_out_norm, swa_q_norm, swa_k_norm, swa_sinks, rel_bias, w_branch_dn, w_branch_swa, w_out, ffn_norm, w_gate, w_up, w_down, loss_target, m_attn_norm, m_w_in, m_dn_conv, m_dn_a_log, m_dn_dt_bias, m_dn_out_norm, m_swa_q_norm, m_swa_k_norm, m_swa_sinks, m_rel_bias, m_w_branch_dn, m_w_branch_swa, m_w_out, m_ffn_norm, m_w_gate, m_w_up, m_w_down, v_attn_norm, v_w_in, v_dn_conv, v_dn_a_log, v_dn_dt_bias, v_dn_out_norm, v_swa_q_norm, v_swa_k_norm, v_swa_sinks, v_rel_bias, v_w_branch_dn, v_w_branch_swa, v_w_out, v_ffn_norm, v_w_gate, v_w_up, v_w_down):
    w = dict(attn_norm=attn_norm, w_in=w_in, dn_conv=dn_conv, dn_a_log=dn_a_log, dn_dt_bias=dn_dt_bias,
             dn_out_norm=dn_out_norm, swa_q_norm=swa_q_norm, swa_k_norm=swa_k_norm, swa_sinks=swa_sinks,
             rel_bias=rel_bias, w_branch_dn=w_branch_dn, w_branch_swa=w_branch_swa, w_out=w_out, ffn_norm=ffn_norm,
             w_gate=w_gate, w_up=w_up, w_down=w_down)
    m = dict(attn_norm=m_attn_norm, w_in=m_w_in, dn_conv=m_dn_conv, dn_a_log=m_dn_a_log, dn_dt_bias=m_dn_dt_bias,
             dn_out_norm=m_dn_out_norm, swa_q_norm=m_swa_q_norm, swa_k_norm=m_swa_k_norm, swa_sinks=m_swa_sinks,
             rel_bias=m_rel_bias, w_branch_dn=m_w_branch_dn, w_branch_swa=m_w_branch_swa, w_out=m_w_out,
             ffn_norm=m_ffn_norm, w_gate=m_w_gate, w_up=m_w_up, w_down=m_w_down)
    v = dict(attn_norm=v_attn_norm, w_in=v_w_in, dn_conv=v_dn_conv, dn_a_log=v_dn_a_log, dn_dt_bias=v_dn_dt_bias,
             dn_out_norm=v_dn_out_norm, swa_q_norm=v_swa_q_norm, swa_k_norm=v_swa_k_norm, swa_sinks=v_swa_sinks,
             rel_bias=v_rel_bias, w_branch_dn=v_w_branch_dn, w_branch_swa=v_w_branch_swa, w_out=v_w_out,
             ffn_norm=v_ffn_norm, w_gate=v_w_gate, w_up=v_w_up, w_down=v_w_down)
    shapes = {n: w[n].shape for n in _WEIGHT_NAMES}

    def two_d(a):
        return a.reshape(a.shape[-2], a.shape[-1]) if a.ndim == 3 else a

    core = lax.axis_index("c")
    chip = 2 * lax.axis_index("x") + lax.axis_index("y")
    small_shapes = {n: two_d(w[n]).shape for n, _ in _SMALL}
    small_shapes["dn_conv"] = (CONV, QKVW)

    conv_loc = two_d(w["dn_conv"])
    conv_part = lax.dynamic_update_slice(jnp.zeros((CONV, QKVW), F32), jnp.where(core == 0, conv_loc, 0.0),
                                         (0, chip * _CONV_SH))
    conv_full = _all_sum_small(conv_part.reshape(CONV * QKVW // 128, 128), "gather_conv").reshape(CONV, QKVW)

    flipped = ("w_gate", "w_up")

    def natural(a, n):
        return a.transpose(0, 2, 1) if n in flipped else a

    w_bf = [two_d(natural(w[n], n).astype(BF16)) for n in _BIG_NAMES]
    (w_in_g,) = _gather_weights(w_bf[:1], chip)
    windows = _gather_windows(w_bf[1:])
    after_sync = w_in_g[0, :8, :128].astype(F32) + conv_full[0:1, :128]
    send_sems, recv_sems, w_thru, l_thru, token = _split_start(
        "gather_start", w_bf[1:], _own_slot(w_bf[1:], chip), after_sync, windows)

    def late(after):
        lands = _split_wait("gather_wait", w_thru, l_thru, send_sems, recv_sems, after, windows)
        g = dict(zip(_BIG_NAMES[1:], _sibling_fill(lands)))
        return dict(wa=g["w_branch_dn"], wb=g["w_branch_swa"], w_out=g["w_out"].reshape(D, D), wg=g["w_gate"],
                    wu=g["w_up"], wd=g["w_down"])

    wts = dict(w_in_p=_w_in_to_padded(w_in_g), dn_conv=conv_full, late=late)
    for n, _ in _SMALL[:-1]:
        wts[n] = two_d(w[n])
    wts["attn_norm"] = wts["attn_norm"] + token[0:1, 0:1]

    early = {}

    def send_early(grads):
        gs = [grads["w_branch_dn"], grads["w_branch_swa"], grads["w_out"].reshape(N_CHIPS, CSH, D), grads["w_gate"],
              grads["w_up"], grads["w_down"]]
        parts = _pair_sum(gs, _swap_halves(gs, "swap_halves_early"), core, "pair_sum_early")
        own = [lax.dynamic_index_in_dim(p, chip, axis=0, keepdims=False) for p in parts]
        early["sems"], early["recv"], early["src"], early["land"], tok = _split_start(
            "exchange_start", parts, _own_slot(own, chip), parts[0][0, :8, :128], _exchange_windows())
        return tok

    last = {}

    def send_in(g_in_p):
        g_in = [_padded_to_w_in(g_in_p)]
        parts = _pair_sum(g_in, _swap_halves(g_in, "swap_halves_in"), core, "pair_sum_in")
        own = [lax.dynamic_index_in_dim(p, chip, axis=0, keepdims=False) for p in parts]
        last["sems"], last["recv"], last["src"], last["land"], tok = _split_start(
            "exchange_in_start", parts, _own_slot(own, chip), parts[0][0, :8, :128], _exchange_windows())
        return tok

    wts["send_early"] = send_early
    wts["send_in"] = send_in
    loss_sum, grad_x, grads = _local_step(x[0], loss_target[0], wts)

    small_sum = _all_sum_small(_pack_small(grads, loss_sum), "all_sum_small")
    loss = small_sum.reshape(-1)[_LOSS_OFF]
    g_small = _unpack_small(small_sum, small_shapes)

    q_early = _split_wait("exchange_wait", early["src"], early["land"], early["sems"], early["recv"], small_sum,
                          _exchange_windows())
    red_early = _chip_sum(list(q_early), "chip_sum_early")
    their_early = _swap_reduced(red_early, "swap_reduced_early")
    g_out, d_out, m_out, v_out = {}, {}, {}, {}
    for n, mine, other in zip(_BIG_NAMES[1:], red_early, their_early):
        res = _adamw_big(natural(w[n], n), mine, other, natural(m[n], n), natural(v[n], n), core, "adamw_" + n)
        g_out[n], d_out[n], m_out[n], v_out[n] = (natural(t, n) for t in res)

    q_in = _split_wait("exchange_in_wait", last["src"], last["land"], last["sems"], last["recv"],
                       d_out[_BIG_NAMES[-1]], _exchange_windows())
    reduced = _chip_sum(list(q_in), "chip_sum_in")
    theirs = _swap_reduced(reduced, "swap_reduced_in")

    def rows1(a):
        return a.transpose(2, 0, 1)

    def unrows1(a):
        return a.transpose(1, 2, 0)

    g_in_blk = jnp.concatenate([jnp.where(core == 0, reduced[0], theirs[0]),
                                jnp.where(core == 0, theirs[0], reduced[0])], axis=0)
    g_in_r = rows1(g_in_blk[None])
    res = _adamw_rows1(rows1(w["w_in"]), g_in_r, rows1(m["w_in"]), rows1(v["w_in"]), "adamw_w_in")
    g_out["w_in"], d_out["w_in"], m_out["w_in"], v_out["w_in"] = (unrows1(t) for t in res)
    g_conv = lax.dynamic_slice(g_small["dn_conv"], (0, chip * _CONV_SH), (CONV, _CONV_SH))
    g_out["dn_conv"] = g_conv.reshape(shapes["dn_conv"])
    d_, m_, v_ = _adamw(conv_loc, g_conv, two_d(m["dn_conv"]), two_d(v["dn_conv"]), "adamw_dn_conv")
    d_out["dn_conv"], m_out["dn_conv"], v_out["dn_conv"] = (t.reshape(shapes["dn_conv"]) for t in (d_, m_, v_))

    def packed(src):
        vals = {n: src[n] for n, _ in _SMALL[:-1]}
        vals["dn_conv"] = jnp.zeros((CONV * QKVW,), F32)
        return _pack_small(vals)

    d_s, m_s, v_s = _adamw(packed(w), small_sum, packed(m), packed(v), "adamw_small")
    d_small, m_small, v_small = (_unpack_small(t, small_shapes) for t in (d_s, m_s, v_s))
    for n, _ in _SMALL[:-1]:
        g_out[n] = g_small[n].reshape(shapes[n])
        d_out[n], m_out[n], v_out[n] = (t[n].reshape(shapes[n]) for t in (d_small, m_small, v_small))

    return (loss, grad_x[None], *[g_out[n] for n in _WEIGHT_NAMES], *[d_out[n] for n in _WEIGHT_NAMES],
            *[m_out[n] for n in _WEIGHT_NAMES], *[v_out[n] for n in _WEIGHT_NAMES])
```

```python
import functools
import math

import numpy as np
import jax
import jax.numpy as jnp
from jax import lax
from jax.experimental import pallas as pl
from jax.experimental.pallas import tpu as pltpu

F32 = jnp.float32
BF16 = jnp.bfloat16
SDS = jax.ShapeDtypeStruct

D = 1024
DN_H = 4
DH = 128
DNW = DN_H * DH
QKVW = 3 * DNW
CONV = 4
CHUNK = 64
SWA_H = 8
SWA_KV = 2
SWA_G = SWA_H // SWA_KV
SWA_D = 64
SWAW = SWA_H * SWA_D
SWAKW = SWA_KV * SWA_D
BLK = 128
NBUCKET = 32
MAXDIST = 128
DFF = 2816
D_IN = QKVW + DNW + 2 * DN_H + SWAW + 2 * SWAKW + 2 * D
EPS = 1e-6
NEG = -1e30

ADAM_LR = 0.001
ADAM_B1 = 0.9
ADAM_B2 = 0.999
ADAM_EPS = 1e-08
ADAM_WD = 0.01
ADAM_STEP = 10

C_QKV, C_Z, C_GATE, C_SQ, C_SK, C_SV, C_BA = 0, 1536, 2048, 4096, 4608, 4736, 4864
PW = 5120
_ORIG_PIECES = (
    (0, QKVW, C_QKV),
    (QKVW, DNW, C_Z),
    (QKVW + DNW, 2 * DN_H, C_BA),
    (QKVW + DNW + 2 * DN_H, SWAW, C_SQ),
    (QKVW + DNW + 2 * DN_H + SWAW, SWAKW, C_SK),
    (QKVW + DNW + 2 * DN_H + SWAW + SWAKW, SWAKW, C_SV),
    (QKVW + DNW + 2 * DN_H + SWAW + 2 * SWAKW, 2 * D, C_GATE),
)

N_CHIPS = 4
FSH = DFF // N_CHIPS
CSH = D // N_CHIPS
VMEM_LIMIT = 48 * 1024 * 1024
MESH = pl.DeviceIdType.MESH

_BIG = (
    ("w_in", D, D_IN // N_CHIPS),
    ("w_branch_dn", DNW, CSH),
    ("w_branch_swa", SWAW, CSH),
    ("w_out", CSH, D),
    ("w_gate", FSH, D),
    ("w_up", FSH, D),
    ("w_down", FSH, D),
)
_BIG_NAMES = tuple(n for n, _, _ in _BIG)

_SMALL = (
    ("attn_norm", D), ("ffn_norm", D), ("dn_out_norm", DH), ("swa_q_norm", SWA_D), ("swa_k_norm", SWA_D),
    ("swa_sinks", SWA_H), ("dn_a_log", DN_H), ("dn_dt_bias", DN_H), ("rel_bias", NBUCKET * SWA_H),
    ("dn_conv", CONV * QKVW),
)
_SMALL_OFF = {}
_o = 0
for _n, _s in _SMALL:
    _SMALL_OFF[_n] = (_o, _s)
    _o += _s
_LOSS_OFF = _o
_SMALL_ROWS = -(-(_o + 1) // (8 * 128)) * 8


def _cparams(**kw):
    return pltpu.CompilerParams(vmem_limit_bytes=VMEM_LIMIT, **kw)


_DIMS = {
    "nn": (((1,), (0,)), ((), ())),
    "nt": (((1,), (1,)), ((), ())),
    "tn": (((0,), (0,)), ((), ())),
    "bnn": (((2,), (1,)), ((0,), (0,))),
    "bnt": (((2,), (2,)), ((0,), (0,))),
    "btn": (((1,), (1,)), ((0,), (0,))),
}


def _raw_dot(a, b, kind, exact):
    if exact:
        prec = lax.Precision.HIGH if exact == "x3" else lax.Precision.HIGHEST
        return lax.dot_general(a, b, _DIMS[kind], precision=prec, preferred_element_type=F32)
    return lax.dot_general(a.astype(BF16), b.astype(BF16), _DIMS[kind], preferred_element_type=F32)


@functools.partial(jax.custom_vjp, nondiff_argnums=(2, 3))
def _dot(a, b, kind, exact):
    return _raw_dot(a, b, kind, exact)


def _dot_fwd(a, b, kind, exact):
    return _raw_dot(a, b, kind, exact), (a, b)


def _dot_bwd(kind, exact, res, g):
    a, b = res
    pre = kind[:-2]
    nn, nt, tn = pre + "nn", pre + "nt", pre + "tn"
    if kind == nn:
        return _dot(g, b, nt, exact), _dot(a, g, tn, exact)
    if kind == nt:
        return _dot(g, b, nn, exact), _dot(g, a, tn, exact)
    return _dot(b, g, nt, exact), _dot(a, g, nn, exact)


_dot.defvjp(_dot_fwd, _dot_bwd)


def _silu(x):
    return x * jax.nn.sigmoid(x)


def _f_rms(x, gain):
    return x * lax.rsqrt(jnp.mean(x * x, axis=-1, keepdims=True) + EPS) * gain


def _f_dn_pre(xs0, xs1, xs2, xs3, ba, cw, alog, dtb):
    rows = xs0.shape[0]
    c = xs0 * cw[0:1] + xs1 * cw[1:2] + xs2 * cw[2:3] + xs3 * cw[3:4]
    qkv = _silu(c)
    qs, ks, bbs, gbs = [], [], [], []
    for h in range(DN_H):
        qh = qkv[:, h * DH:(h + 1) * DH]
        kh = qkv[:, DNW + h * DH:DNW + (h + 1) * DH]
        qs.append(qh * lax.rsqrt(jnp.sum(qh * qh, axis=-1, keepdims=True) + EPS) * (DH ** -0.5))
        ks.append(kh * lax.rsqrt(jnp.sum(kh * kh, axis=-1, keepdims=True) + EPS))
        beta = jax.nn.sigmoid(ba[:, h:h + 1])
        ar = ba[:, DN_H + h:DN_H + h + 1] + dtb[:, h:h + 1]
        softplus = jnp.maximum(ar, 0.0) + jnp.log1p(jnp.exp(-jnp.abs(ar)))
        g = -jnp.exp(alog[:, h:h + 1]) * softplus
        bbs.append(jnp.broadcast_to(beta, (rows, DH)))
        gbs.append(jnp.broadcast_to(g, (rows, DH)))
    return (jnp.concatenate(qs, axis=1), jnp.concatenate(ks, axis=1), qkv[:, 2 * DNW:],
            jnp.concatenate(bbs, axis=1), jnp.concatenate(gbs, axis=1))


def _f_dn_post(o, z, gain):
    ys = []
    for h in range(DN_H):
        oh = o[:, h * DH:(h + 1) * DH]
        zh = z[:, h * DH:(h + 1) * DH]
        ys.append(oh * lax.rsqrt(jnp.mean(oh * oh, axis=-1, keepdims=True) + EPS) * gain * _silu(zh))
    return jnp.concatenate(ys, axis=1)


def _f_merge(pa, pb, ga, gb):
    return jax.nn.sigmoid(ga) * pa + jax.nn.sigmoid(gb) * pb


@jax.custom_vjp
def _f_swiglu(g, u):
    return _silu(g) * u


def _f_swiglu_fwd(g, u):
    return _silu(g) * u, (g, u)


def _f_swiglu_bwd(res, d):
    g, u = res
    s = jax.nn.sigmoid(g)
    act = g * s
    return d * u * (s + act * (1.0 - s)), d * act


_f_swiglu.defvjp(_f_swiglu_fwd, _f_swiglu_bwd)


@jax.custom_vjp
def _unit_lower_inverse(a):
    c = a.shape[-1]
    eye = (lax.broadcasted_iota(jnp.int32, a.shape, 1) == lax.broadcasted_iota(jnp.int32, a.shape, 2)).astype(F32)
    p = -a
    t = eye + p
    for _ in range(max(c.bit_length() - 2, 0)):
        p = _raw_dot(p, p, "bnn", "x3")
        t = t + _raw_dot(t, p, "bnn", "x3")
    return t


def _unit_lower_inverse_fwd(a):
    t = _unit_lower_inverse(a)
    return t, t


def _unit_lower_inverse_bwd(t, g):
    return (-_raw_dot(_raw_dot(t, g, "btn", "x3"), t, "bnt", "x3"),)


_unit_lower_inverse.defvjp(_unit_lower_inverse_fwd, _unit_lower_inverse_bwd)


@jax.custom_vjp
def _known_inverse(a, t):
    return t


def _known_inverse_fwd(a, t):
    return t, t


def _known_inverse_bwd(t, g):
    return _unit_lower_inverse_bwd(t, g)[0], jnp.zeros_like(t)


_known_inverse.defvjp(_known_inverse_fwd, _known_inverse_bwd)


def _f_chunk(q, k, v, gb, bb, s, t_known=None, with_t=False):
    c = CHUNK
    nh = q.shape[0]
    ii = lax.broadcasted_iota(jnp.int32, (nh, c, c), 1)
    jj = lax.broadcasted_iota(jnp.int32, (nh, c, c), 2)
    incl = ii >= jj
    strict = ii > jj
    eye = (ii == jj).astype(F32)
    gcb = _dot(incl.astype(F32), gb, "bnn", "x3")
    lane0 = (lax.broadcasted_iota(jnp.int32, (nh, c, DH), 2) == 0).astype(F32)
    gcol = gcb[:, :, :c]
    grow = _dot(lane0, gcb, "bnt", "x3")
    decay = jnp.where(incl, jnp.exp(jnp.where(incl, gcol - grow, 0.0)), 0.0)
    kb = k * bb
    vb = v * bb
    a = jnp.where(strict, _dot(kb, k, "bnt", False) * decay, 0.0)
    t = _unit_lower_inverse(a) if t_known is None else _known_inverse(a, t_known)
    eg = jnp.exp(gcb)
    u = _dot(t, vb, "bnn", "x3")
    w = _dot(t, kb * eg, "bnn", "x3")
    qk = jnp.where(incl, _dot(q, k, "bnt", False) * decay, 0.0)
    qe = q * eg
    glast = gcb[:, c - 1:c, :]
    k_dec = k * jnp.exp(glast - gcb)
    e_last = jnp.exp(glast)
    outs = []
    for g in range(nh // DN_H):
        sl = slice(g * DN_H, (g + 1) * DN_H)
        v_new = u[sl] - _dot(w[sl], s, "bnn", False)
        outs.append(_dot(qe[sl], s, "bnn", False) + _dot(qk[sl], v_new, "bnn", False))
        s = s * e_last[sl] + _dot(k_dec[sl], v_new, "btn", False)
    o = jnp.concatenate(outs, axis=0)
    return (o, s, t) if with_t else (o, s)


def _f_swa(q8, kp, kc, vp, vc, bias8, qg, kg, sink, mask):
    kb = jnp.concatenate([kp, kc], axis=1)
    vb = jnp.concatenate([vp, vc], axis=1)
    kn = kb * lax.rsqrt(jnp.mean(kb * kb, axis=-1, keepdims=True) + EPS) * kg

    def rows(per_head):
        return jnp.stack([jnp.concatenate([per_head(kv, g) for g in range(SWA_G)], axis=0)
                          for kv in range(SWA_KV)], axis=0)

    qq = rows(lambda kv, g: q8[kv * SWA_G + g])
    qn = qq * lax.rsqrt(jnp.mean(qq * qq, axis=-1, keepdims=True) + EPS) * qg * (SWA_D ** -0.5)
    lg = _dot(qn, kn, "bnt", False) + rows(lambda kv, g: bias8[kv * SWA_G + g])
    lg = jnp.where(rows(lambda kv, g: mask), lg, NEG)
    sk = rows(lambda kv, g: jnp.broadcast_to(sink[kv][:, g:g + 1], (BLK, 1)))
    m = lax.stop_gradient(jnp.maximum(jnp.max(lg, axis=-1, keepdims=True), sk))
    p = jnp.exp(lg - m)
    den = jnp.sum(p, axis=-1, keepdims=True) + jnp.exp(sk - m)
    out = _dot(p * (1.0 / den), vb, "bnn", False)
    return jnp.stack([out[kv, g * BLK:(g + 1) * BLK] for kv in range(SWA_KV) for g in range(SWA_G)], axis=0)


def _bdot(a, b, kind="nn"):
    return lax.dot_general(a.astype(BF16), b.astype(BF16), _DIMS[kind], preferred_element_type=F32)


def _pc(kern, name, grid, in_specs, out_specs, out_shape, scratch=()):
    return pl.pallas_call(
        kern, name=name, grid=grid, in_specs=in_specs, out_specs=out_specs, out_shape=out_shape,
        scratch_shapes=list(scratch), compiler_params=_cparams(dimension_semantics=("arbitrary",) * len(grid)))


def _mm(a, b, kind, out_dtype, tm, tn, name):
    if kind == "tn":
        k, m = a.shape
    else:
        m, k = a.shape
    n = b.shape[0] if kind == "nt" else b.shape[1]
    tm, tn = min(tm, m), min(tn, n)
    assert m % tm == 0 and n % tn == 0, (name, a.shape, b.shape, tm, tn)

    def kern(a_ref, b_ref, o_ref):
        o_ref[...] = _bdot(a_ref[...], b_ref[...], kind).astype(o_ref.dtype)

    a_spec = pl.BlockSpec((k, tm), lambda i, j: (0, i)) if kind == "tn" else pl.BlockSpec((tm, k), lambda i, j: (i, 0))
    b_spec = pl.BlockSpec((tn, k), lambda i, j: (j, 0)) if kind == "nt" else pl.BlockSpec((k, tn), lambda i, j: (0, j))
    return _pc(kern, name, (m // tm, n // tn), [a_spec, b_spec], pl.BlockSpec((tm, tn), lambda i, j: (i, j)),
               SDS((m, n), out_dtype))(a, b)


def _rows(body, name, m, tm, row_ins, full_ins, row_outs, acc_outs=()):
    n_r, n_f, n_o, n_a = len(row_ins), len(full_ins), len(row_outs), len(acc_outs)
    assert m % tm == 0

    def kern(*refs):
        r = refs[:n_r]
        f = refs[n_r:n_r + n_f]
        o = refs[n_r + n_f:n_r + n_f + n_o]
        acc = refs[n_r + n_f + n_o:]
        outs, sums = body([x[...] for x in r], [x[...] for x in f])
        for ref, val in zip(o, outs, strict=True):
            ref[...] = val.astype(ref.dtype)
        if n_a:
            @pl.when(pl.program_id(0) == 0)
            def _():
                for ref in acc:
                    ref[...] = jnp.zeros(ref.shape, F32)

            for ref, val in zip(acc, sums, strict=True):
                ref[...] += val

    in_specs = [pl.BlockSpec((tm, w), functools.partial(lambda i, cb: (i, cb), cb=cb)) for _, w, cb in row_ins]
    in_specs += [pl.BlockSpec(x.shape, lambda i: (0, 0)) for x in full_ins]
    out_specs = [pl.BlockSpec((tm, w), lambda i: (i, 0)) for w, _ in row_outs]
    out_specs += [pl.BlockSpec(s, lambda i: (0, 0)) for s in acc_outs]
    out_shape = [SDS((m, w), dt) for w, dt in row_outs]
    out_shape += [SDS(s, F32) for s in acc_outs]
    return _pc(kern, name, (m // tm,), in_specs, out_specs, out_shape)(*[x for x, _, _ in row_ins], *full_ins)


def _whole(x):
    return (x, x.shape[1], 0)


def _zero_first(refs):
    @pl.when(pl.program_id(0) == 0)
    def _():
        for ref in refs:
            ref[...] = jnp.zeros(ref.shape, F32)


GROUP = 4


def _heads(ref):
    return jnp.stack([ref[g * CHUNK:(g + 1) * CHUNK, h * DH:(h + 1) * DH]
                      for g in range(GROUP) for h in range(DN_H)], axis=0)


def _unheads(ref, val):
    for g in range(GROUP):
        for h in range(DN_H):
            ref[g * CHUNK:(g + 1) * CHUNK, h * DH:(h + 1) * DH] = val[g * DN_H + h]


def _dn_chunks_fwd(q, k, v, gb, bb):
    s_len = q.shape[0]
    ng = s_len // (GROUP * CHUNK)

    def kern(q_ref, k_ref, v_ref, g_ref, b_ref, o_ref, sall_ref, t_ref, state):
        _zero_first([state])
        s = state[...]
        sall_ref[0] = s
        o, s_new, t = _f_chunk(*[_heads(r) for r in (q_ref, k_ref, v_ref, g_ref, b_ref)], s, with_t=True)
        _unheads(o_ref, o)
        t_ref[0] = t
        state[...] = s_new

    blk = pl.BlockSpec((GROUP * CHUNK, DNW), lambda c: (c, 0))
    return _pc(kern, "dn_chunks_fwd", (ng,), [blk] * 5,
               [blk, pl.BlockSpec((1, DN_H, DH, DH), lambda c: (c, 0, 0, 0)),
                pl.BlockSpec((1, GROUP * DN_H, CHUNK, CHUNK), lambda c: (c, 0, 0, 0))],
               [SDS((s_len, DNW), F32), SDS((ng, DN_H, DH, DH), F32), SDS((ng, GROUP * DN_H, CHUNK, CHUNK), F32)],
               scratch=[pltpu.VMEM((DN_H, DH, DH), F32)])(q, k, v, gb, bb)


def _dn_chunks_bwd(q, k, v, gb, bb, s_all, t_all, d_o):
    s_len = q.shape[0]
    ng = s_len // (GROUP * CHUNK)

    def kern(q_ref, k_ref, v_ref, g_ref, b_ref, sall_ref, t_ref, do_ref, dq_ref, dk_ref, dv_ref, dg_ref, db_ref,
             dstate):
        _zero_first([dstate])
        fn = functools.partial(_f_chunk, t_known=t_ref[0])
        _, vjp = jax.vjp(fn, *[_heads(r) for r in (q_ref, k_ref, v_ref, g_ref, b_ref)], sall_ref[0])
        *d_ins, ds = vjp((_heads(do_ref), dstate[...]))
        for ref, val in zip((dq_ref, dk_ref, dv_ref, dg_ref, db_ref), d_ins, strict=True):
            _unheads(ref, val)
        dstate[...] = ds

    blk = pl.BlockSpec((GROUP * CHUNK, DNW), lambda c: (ng - 1 - c, 0))
    return _pc(kern, "dn_chunks_bwd", (ng,),
               [blk] * 5 + [pl.BlockSpec((1, DN_H, DH, DH), lambda c: (ng - 1 - c, 0, 0, 0)),
                            pl.BlockSpec((1, GROUP * DN_H, CHUNK, CHUNK), lambda c: (ng - 1 - c, 0, 0, 0)), blk],
               [blk] * 5, [SDS((s_len, DNW), F32)] * 5,
               scratch=[pltpu.VMEM((DN_H, DH, DH), F32)])(q, k, v, gb, bb, s_all, t_all, d_o)


def _t5_bucket_table():
    qi = np.arange(BLK)[:, None]
    kj = np.arange(2 * BLK)[None, :]
    dist = BLK + qi - kj
    n = np.maximum(dist, 0)
    max_exact = NBUCKET // 2
    nf = np.maximum(n, 1).astype(np.float32)
    large = max_exact + (np.log(nf / np.float32(max_exact)) / np.float32(math.log(MAXDIST / max_exact))
                         * np.float32(NBUCKET - max_exact)).astype(np.int32)
    large = np.minimum(large, NBUCKET - 1)
    return np.where(n < max_exact, n, large)


def _bucket_onehot_t():
    table = _t5_bucket_table().reshape(-1)
    return (np.arange(NBUCKET)[:, None] == table[None, :]).astype(np.float32)


def _swa_mask(first):
    qi = lax.broadcasted_iota(jnp.int32, (BLK, 2 * BLK), 0)
    kj = lax.broadcasted_iota(jnp.int32, (BLK, 2 * BLK), 1)
    dist = BLK + qi - kj
    window = (dist >= 0) & (dist < BLK)
    return window & ((kj >= BLK) | jnp.logical_not(first))


def _bias_expand(rel_bias_t):
    onehot = jnp.asarray(_bucket_onehot_t())

    def kern(r_ref, oh_ref, o_ref):
        o_ref[...] = _raw_dot(r_ref[...], oh_ref[...], "nn", True)

    return pl.pallas_call(
        kern, name="bias_expand", out_shape=SDS((SWA_H, BLK * 2 * BLK), F32), compiler_params=_cparams(),
    )(rel_bias_t, onehot)


def _bias_reduce(d_bias_flat):
    onehot = jnp.asarray(_bucket_onehot_t())

    def kern(d_ref, oh_ref, o_ref):
        o_ref[...] = _raw_dot(d_ref[...], oh_ref[...], "nt", True)

    return pl.pallas_call(
        kern, name="bias_reduce", out_shape=SDS((SWA_H, NBUCKET), F32), compiler_params=_cparams(),
    )(d_bias_flat, onehot)


def _swa_specs(nb, rev):
    def blk(n):
        return (nb - 1 - n) if rev else n

    def before(n):
        return jnp.maximum(blk(n) - 1, 0)

    q_spec = pl.BlockSpec((BLK, SWAW), lambda n: (blk(n), C_SQ // SWAW))
    k_cur = pl.BlockSpec((BLK, SWAKW), lambda n: (blk(n), C_SK // SWAKW))
    k_prev = pl.BlockSpec((BLK, SWAKW), lambda n: (before(n), C_SK // SWAKW))
    v_cur = pl.BlockSpec((BLK, SWAKW), lambda n: (blk(n), C_SV // SWAKW))
    v_prev = pl.BlockSpec((BLK, SWAKW), lambda n: (before(n), C_SV // SWAKW))
    bias = pl.BlockSpec((SWA_H, BLK, 2 * BLK), lambda n: (0, 0, 0))
    gain = pl.BlockSpec((1, SWA_D), lambda n: (0, 0))
    sink = pl.BlockSpec((SWA_KV, 1, SWA_G), lambda n: (0, 0, 0))
    wide = pl.BlockSpec((BLK, SWAW), lambda n: (blk(n), 0))
    narrow = pl.BlockSpec((BLK, SWAKW), lambda n: (blk(n), 0))
    return [q_spec, k_prev, k_cur, v_prev, v_cur, bias, gain, gain, sink], wide, narrow


def _split_heads(x):
    return jnp.stack([x[:, h * SWA_D:(h + 1) * SWA_D] for h in range(x.shape[1] // SWA_D)], axis=0)


def _join_heads(x):
    return jnp.concatenate([x[h] for h in range(x.shape[0])], axis=1)


def _swa_fwd(proj, bias, qg, kg, sinks):
    s_len = proj.shape[0]
    nb = s_len // BLK
    in_specs, wide, _ = _swa_specs(nb, False)

    def kern(q_ref, kp_ref, kc_ref, vp_ref, vc_ref, b_ref, qg_ref, kg_ref, s_ref, o_ref):
        mask = _swa_mask(pl.program_id(0) == 0)
        o8 = _f_swa(*[_split_heads(r[...]) for r in (q_ref, kp_ref, kc_ref, vp_ref, vc_ref)], b_ref[...], qg_ref[...],
                    kg_ref[...], s_ref[...], mask)
        o_ref[...] = _join_heads(o8).astype(BF16)

    return _pc(kern, "swa_fwd", (nb,), in_specs, wide, SDS((s_len, SWAW), BF16))(
        proj, proj, proj, proj, proj, bias, qg, kg, sinks)


def _swa_bwd(proj, bias, qg, kg, sinks, d_out):
    s_len = proj.shape[0]
    nb = s_len // BLK
    in_specs, wide, narrow = _swa_specs(nb, True)

    def kern(q_ref, kp_ref, kc_ref, vp_ref, vc_ref, b_ref, qg_ref, kg_ref, s_ref, do_ref,
             dq_ref, dk_ref, dv_ref, db_ref, dqg_ref, dkg_ref, ds_ref, carry_k, carry_v):
        n = pl.program_id(0)
        mask = _swa_mask(n == nb - 1)
        _zero_first([carry_k, carry_v, db_ref, ds_ref, dqg_ref, dkg_ref])
        fn = functools.partial(_f_swa, mask=mask)
        _, vjp = jax.vjp(fn, *[_split_heads(r[...]) for r in (q_ref, kp_ref, kc_ref, vp_ref, vc_ref)], b_ref[...],
                         qg_ref[...], kg_ref[...], s_ref[...])
        dq, dkp, dkc, dvp, dvc, dbias, dqg, dkg, dsink = vjp(_split_heads(do_ref[...]))
        dq_ref[...] = _join_heads(dq).astype(BF16)
        dk_ref[...] = (_join_heads(dkc) + carry_k[...]).astype(BF16)
        dv_ref[...] = (_join_heads(dvc) + carry_v[...]).astype(BF16)
        carry_k[...] = _join_heads(dkp)
        carry_v[...] = _join_heads(dvp)
        db_ref[...] += dbias
        dqg_ref[...] += dqg
        dkg_ref[...] += dkg
        ds_ref[...] += dsink

    bias_spec, gain, sink = in_specs[5], in_specs[6], in_specs[8]
    return _pc(
        kern, "swa_bwd", (nb,), in_specs + [wide], [wide, narrow, narrow, bias_spec, gain, gain, sink],
        [SDS((s_len, SWAW), BF16), SDS((s_len, SWAKW), BF16), SDS((s_len, SWAKW), BF16),
         SDS((SWA_H, BLK, 2 * BLK), F32), SDS((1, SWA_D), F32), SDS((1, SWA_D), F32), SDS((SWA_KV, 1, SWA_G), F32)],
        scratch=[pltpu.VMEM((BLK, SWAKW), F32), pltpu.VMEM((BLK, SWAKW), F32)],
    )(proj, proj, proj, proj, proj, bias, qg, kg, sinks, d_out)


def _branch_merge(y_dn, y_swa, wa, wb, proj):
    s_len = y_dn.shape[0]
    tm = min(512, s_len)

    def kern(ya_ref, yb_ref, wa_ref, wb_ref, ga_ref, gb_ref, pa_ref, pb_ref, m_ref):
        pa = _bdot(ya_ref[...], wa_ref[0])
        pb = _bdot(yb_ref[...], wb_ref[0])
        pa_ref[...] = pa.astype(BF16)
        pb_ref[...] = pb.astype(BF16)
        m_ref[...] = _f_merge(pa, pb, ga_ref[...], gb_ref[...]).astype(BF16)

    y_spec = pl.BlockSpec((tm, DNW), lambda i, s: (i, 0))
    w_spec = pl.BlockSpec((1, DNW, CSH), lambda i, s: (s, 0, 0))
    o_spec = pl.BlockSpec((tm, CSH), lambda i, s: (i, s))
    ga_spec = pl.BlockSpec((tm, CSH), lambda i, s: (i, C_GATE // CSH + s))
    gb_spec = pl.BlockSpec((tm, CSH), lambda i, s: (i, (C_GATE + D) // CSH + s))
    return _pc(kern, "branch_merge", (s_len // tm, N_CHIPS), [y_spec, y_spec, w_spec, w_spec, ga_spec, gb_spec],
               [o_spec] * 3, [SDS((s_len, D), BF16)] * 3,
               )(y_dn, y_swa, wa, wb, proj, proj)


def _in_proj(x, gain, w_in_p):
    s_len = x.shape[0]
    tm = min(256, s_len)

    def kern(x_ref, g_ref, w_ref, h_ref, p_ref):
        h = _f_rms(x_ref[...], g_ref[...]).astype(BF16)
        h_ref[...] = h
        p_ref[...] = _bdot(h, w_ref[...])

    row = pl.BlockSpec((tm, D), lambda i: (i, 0))
    return _pc(kern, "in_proj", (s_len // tm,),
               [row, pl.BlockSpec((1, D), lambda i: (0, 0)), pl.BlockSpec((D, PW), lambda i: (0, 0))],
               [row, pl.BlockSpec((tm, PW), lambda i: (i, 0))],
               [SDS((s_len, D), BF16), SDS((s_len, PW), F32)])(x, gain, w_in_p)


def _out_proj(merged, w_out, x, gain):
    s_len = x.shape[0]
    tm = min(256, s_len)

    def kern(m_ref, w_ref, x_ref, g_ref, x1_ref, h2_ref):
        x1 = x_ref[...] + _bdot(m_ref[...], w_ref[...])
        x1_ref[...] = x1
        h2_ref[...] = _f_rms(x1, g_ref[...]).astype(BF16)

    row = pl.BlockSpec((tm, D), lambda i: (i, 0))
    return _pc(kern, "out_proj", (s_len // tm,),
               [row, pl.BlockSpec((D, D), lambda i: (0, 0)), row, pl.BlockSpec((1, D), lambda i: (0, 0))],
               [row, row], [SDS((s_len, D), F32), SDS((s_len, D), BF16)])(merged, w_out, x, gain)


def _ffn_up(h2, wg, wu):
    s_len = h2.shape[0]
    tm = min(512, s_len)

    def kern(h_ref, g_ref, u_ref, gt_ref, up_ref, act_ref):
        h = h_ref[...]
        g = _bdot(h, g_ref[0], "nt")
        u = _bdot(h, u_ref[0], "nt")
        gt_ref[0] = g.astype(BF16)
        up_ref[0] = u.astype(BF16)
        act_ref[0] = _f_swiglu(g, u).astype(BF16)

    w_spec = pl.BlockSpec((1, FSH, D), lambda s, i: (s, 0, 0))
    o_spec = pl.BlockSpec((1, tm, FSH), lambda s, i: (s, i, 0))
    shape = (N_CHIPS, s_len, FSH)
    return _pc(kern, "ffn_up", (N_CHIPS, s_len // tm), [pl.BlockSpec((tm, D), lambda s, i: (i, 0)), w_spec, w_spec],
               [o_spec] * 3, [SDS(shape, BF16)] * 3)(h2, wg, wu)


def _ffn_down_loss(act, wd, x1, target):
    s_len = x1.shape[0]
    tm = min(256, s_len)

    def kern(a_ref, w_ref, x_ref, t_ref, dy_ref, dyb_ref, loss_ref):
        _zero_first([loss_ref])
        y = x_ref[...]
        for s in range(N_CHIPS):
            y = y + _bdot(a_ref[s], w_ref[s])
        d = y - t_ref[...]
        dy = d * (1.0 / D)
        dy_ref[...] = dy
        dyb_ref[...] = dy.astype(BF16)
        loss_ref[...] += jnp.sum(d * d).reshape(1, 1) * (0.5 / D)

    row = pl.BlockSpec((tm, D), lambda i: (i, 0))
    return _pc(kern, "ffn_down_loss", (s_len // tm,),
               [pl.BlockSpec((N_CHIPS, tm, FSH), lambda i: (0, i, 0)),
                pl.BlockSpec((N_CHIPS, FSH, D), lambda i: (0, 0, 0)), row, row],
               [row, row, pl.BlockSpec((1, 1), lambda i: (0, 0))],
               [SDS((s_len, D), F32), SDS((s_len, D), BF16), SDS((1, 1), F32)])(act, wd, x1, target)


def _ffn_dact(dy_b, wd, gt, up):
    s_len = dy_b.shape[0]
    tm = min(512, s_len)

    def kern(dy_ref, w_ref, gt_ref, up_ref, dg_ref, du_ref):
        d_act = _bdot(dy_ref[...], w_ref[0], "nt")
        _, vjp = jax.vjp(_f_swiglu, gt_ref[0].astype(F32), up_ref[0].astype(F32))
        dg, du = vjp(d_act)
        dg_ref[0] = dg.astype(BF16)
        du_ref[0] = du.astype(BF16)

    a_spec = pl.BlockSpec((1, tm, FSH), lambda s, i: (s, i, 0))
    shape = (N_CHIPS, s_len, FSH)
    return _pc(kern, "ffn_dact", (N_CHIPS, s_len // tm),
               [pl.BlockSpec((tm, D), lambda s, i: (i, 0)), pl.BlockSpec((1, FSH, D), lambda s, i: (s, 0, 0)),
                a_spec, a_spec],
               [a_spec, a_spec], [SDS(shape, BF16), SDS(shape, BF16)])(dy_b, wd, gt, up)


def _gw_ffn(lhs, rhs, name):
    s_len = rhs.shape[0]
    n = len(lhs)
    tn = 512

    def kern(*refs):
        g = refs[n][...]
        for i in range(n):
            refs[n + 1 + i][0] = _bdot(refs[i][0], g, "tn").astype(BF16)

    a_spec = pl.BlockSpec((1, s_len, FSH), lambda s, j: (s, 0, 0))
    o_spec = pl.BlockSpec((1, FSH, tn), lambda s, j: (s, 0, j))
    return _pc(kern, name, (N_CHIPS, D // tn), [a_spec] * n + [pl.BlockSpec((s_len, tn), lambda s, j: (0, j))],
               [o_spec] * n, [SDS((N_CHIPS, FSH, D), BF16)] * n)(*lhs, rhs)


def _ffn_dh2(d_gt, d_up, wg, wu, x1, dy, gain):
    s_len = x1.shape[0]
    tm = min(256, s_len)

    def kern(dg_ref, du_ref, wg_ref, wu_ref, x_ref, dy_ref, g_ref, dx_ref, dxb_ref, dgain_ref):
        _zero_first([dgain_ref])
        dh2 = jnp.zeros((tm, D), F32)
        for s in range(N_CHIPS):
            dh2 = dh2 + _bdot(dg_ref[s], wg_ref[s]) + _bdot(du_ref[s], wu_ref[s])
        _, vjp = jax.vjp(_f_rms, x_ref[...], g_ref[...])
        dx, dgain = vjp(dh2)
        dx1 = dx + dy_ref[...]
        dx_ref[...] = dx1
        dxb_ref[...] = dx1.astype(BF16)
        dgain_ref[...] += dgain

    row = pl.BlockSpec((tm, D), lambda i: (i, 0))
    d_spec = pl.BlockSpec((N_CHIPS, tm, FSH), lambda i: (0, i, 0))
    w_spec = pl.BlockSpec((N_CHIPS, FSH, D), lambda i: (0, 0, 0))
    vec = pl.BlockSpec((1, D), lambda i: (0, 0))
    return _pc(kern, "ffn_dh2", (s_len // tm,), [d_spec, d_spec, w_spec, w_spec, row, row, vec],
               [row, row, vec], [SDS((s_len, D), F32), SDS((s_len, D), BF16), SDS((1, D), F32)],
               )(d_gt, d_up, wg, wu, x1, dy, gain)


def _merge_bwd(dx1_b, w_out, pa, pb, proj):
    s_len = dx1_b.shape[0]
    tm = min(256, s_len)

    def kern(dx_ref, w_ref, pa_ref, pb_ref, g_ref, dpa_ref, dpb_ref, dg_ref):
        dm = _bdot(dx_ref[...], w_ref[...], "nt")
        gates = g_ref[...]
        _, vjp = jax.vjp(_f_merge, pa_ref[...].astype(F32), pb_ref[...].astype(F32), gates[:, :D], gates[:, D:])
        dpa, dpb, dga, dgb = vjp(dm)
        dpa_ref[...] = dpa.astype(BF16)
        dpb_ref[...] = dpb.astype(BF16)
        dg_ref[:, :D] = dga.astype(BF16)
        dg_ref[:, D:] = dgb.astype(BF16)

    row = pl.BlockSpec((tm, D), lambda i: (i, 0))
    return _pc(kern, "merge_bwd", (s_len // tm,),
               [row, pl.BlockSpec((D, D), lambda i: (0, 0)), row, row,
                pl.BlockSpec((tm, 2 * D), lambda i: (i, C_GATE // (2 * D)))],
               [row, row, pl.BlockSpec((tm, 2 * D), lambda i: (i, 0))],
               [SDS((s_len, D), BF16), SDS((s_len, D), BF16), SDS((s_len, 2 * D), BF16)],
               )(dx1_b, w_out, pa, pb, proj)


def _d_branch(d_pa, d_pb, wa, wb):
    s_len = d_pa.shape[0]
    tm = min(512, s_len)

    def kern(da_ref, db_ref, wa_ref, wb_ref, oa_ref, ob_ref):
        acc_a = jnp.zeros((tm, DNW), F32)
        acc_b = jnp.zeros((tm, SWAW), F32)
        for s in range(N_CHIPS):
            acc_a = acc_a + _bdot(da_ref[:, s * CSH:(s + 1) * CSH], wa_ref[s], "nt")
            acc_b = acc_b + _bdot(db_ref[:, s * CSH:(s + 1) * CSH], wb_ref[s], "nt")
        oa_ref[...] = acc_a
        ob_ref[...] = acc_b

    row = pl.BlockSpec((tm, D), lambda i: (i, 0))
    w_spec = pl.BlockSpec((N_CHIPS, DNW, CSH), lambda i: (0, 0, 0))
    out = pl.BlockSpec((tm, DNW), lambda i: (i, 0))
    return _pc(kern, "d_branch", (s_len // tm,), [row, row, w_spec, w_spec], [out, out],
               [SDS((s_len, DNW), F32), SDS((s_len, SWAW), F32)])(d_pa, d_pb, wa, wb)


def _gw_branch(y_dn, y_swa, d_pa, d_pb):
    s_len = y_dn.shape[0]

    def kern(ya_ref, yb_ref, da_ref, db_ref, oa_ref, ob_ref):
        oa_ref[0] = _bdot(ya_ref[...], da_ref[...], "tn").astype(BF16)
        ob_ref[0] = _bdot(yb_ref[...], db_ref[...], "tn").astype(BF16)

    y_spec = pl.BlockSpec((s_len, DNW), lambda s: (0, 0))
    d_spec = pl.BlockSpec((s_len, CSH), lambda s: (0, s))
    o_spec = pl.BlockSpec((1, DNW, CSH), lambda s: (s, 0, 0))
    shape = (N_CHIPS, DNW, CSH)
    return _pc(kern, "gw_branch", (N_CHIPS,), [y_spec, y_spec, d_spec, d_spec], [o_spec, o_spec],
               [SDS(shape, BF16), SDS(shape, BF16)])(y_dn, y_swa, d_pa, d_pb)


def _dh_rms(d_proj, w_in_p, x, dx1, gain):
    s_len = x.shape[0]
    tm = min(256, s_len)

    def kern(dp_ref, w_ref, x_ref, r_ref, g_ref, gx_ref, dgain_ref):
        _zero_first([dgain_ref])
        dh = _bdot(dp_ref[...], w_ref[...], "nt")
        _, vjp = jax.vjp(_f_rms, x_ref[...], g_ref[...])
        dx, dgain = vjp(dh)
        gx_ref[...] = dx + r_ref[...]
        dgain_ref[...] += dgain

    row = pl.BlockSpec((tm, D), lambda i: (i, 0))
    vec = pl.BlockSpec((1, D), lambda i: (0, 0))
    return _pc(kern, "dh_rms", (s_len // tm,),
               [pl.BlockSpec((tm, PW), lambda i: (i, 0)), pl.BlockSpec((D, PW), lambda i: (0, 0)), row, row, vec],
               [row, vec], [SDS((s_len, D), F32), SDS((1, D), F32)])(d_proj, w_in_p, x, dx1, gain)


HALO = 8


def _rows_down(x, n, above):
    tm = x.shape[0]
    r = pltpu.roll(x, n, 0)
    a = pltpu.roll(above, n, 0)
    top = jnp.where(lax.broadcasted_iota(jnp.int32, above.shape, 0) < n, a, r[0:HALO])
    return jnp.concatenate([top, r[HALO:tm]], axis=0)


def _rows_up(x, n, below):
    tm = x.shape[0]
    r = pltpu.roll(x, tm - n, 0)
    b = pltpu.roll(below, HALO - n, 0)
    bottom = jnp.where(lax.broadcasted_iota(jnp.int32, below.shape, 0) >= HALO - n, b, r[tm - HALO:tm])
    return jnp.concatenate([r[0:tm - HALO], bottom], axis=0)


def _conv_taps(cur_ref, prev_ref, first):
    cur = cur_ref[...]
    above = jnp.where(first, 0.0, prev_ref[...])
    return [_rows_down(cur, n, above) for n in range(CONV - 1, 0, -1)] + [cur]


def _dn_pre_specs(s_len, tm, blk):
    cur = pl.BlockSpec((tm, QKVW), lambda i: (blk(i), 0))
    prev = pl.BlockSpec((HALO, QKVW), lambda i: (jnp.maximum(blk(i) * (tm // HALO) - 1, 0), 0))
    ba = pl.BlockSpec((tm, 128), lambda i: (blk(i), C_BA // 128))
    row = pl.BlockSpec((tm, DNW), lambda i: (blk(i), 0))
    full = [pl.BlockSpec((CONV, QKVW), lambda i: (0, 0)), pl.BlockSpec((1, DN_H), lambda i: (0, 0)),
            pl.BlockSpec((1, DN_H), lambda i: (0, 0))]
    return cur, prev, ba, row, full


def _dn_pre_fwd(proj, conv_w, alog, dtb):
    s_len = proj.shape[0]
    tm = min(128, s_len)
    cur, prev, ba, row, full = _dn_pre_specs(s_len, tm, lambda i: i)

    def kern(cur_ref, prev_ref, ba_ref, cw_ref, al_ref, dt_ref, q_ref, k_ref, v_ref, bb_ref, gb_ref):
        xs = _conv_taps(cur_ref, prev_ref, pl.program_id(0) == 0)
        outs = _f_dn_pre(*xs, ba_ref[...], cw_ref[...], al_ref[...], dt_ref[...])
        for ref, val in zip((q_ref, k_ref, v_ref, bb_ref, gb_ref), outs, strict=True):
            ref[...] = val

    return _pc(kern, "dn_pre_fwd", (s_len // tm,), [cur, prev, ba] + full, [row] * 5,
               [SDS((s_len, DNW), F32)] * 5)(proj, proj, proj, conv_w, alog, dtb)


def _dn_pre_bwd(proj, conv_w, alog, dtb, cots, others):
    s_len = proj.shape[0]
    tm = min(128, s_len)
    nb = s_len // tm
    cur, prev, ba, row, full = _dn_pre_specs(s_len, tm, lambda i: nb - 1 - i)
    n_o = len(others)
    assert QKVW + sum(t.shape[1] for t in others) + 128 == C_BA + 128

    def kern(cur_ref, prev_ref, ba_ref, cw_ref, al_ref, dt_ref, dq_ref, dk_ref, dv_ref, dbb_ref, dgb_ref, *rest):
        o_refs = rest[:n_o]
        dproj_ref, dcw_ref, dal_ref, ddt_ref, *tails = rest[n_o:]
        i = pl.program_id(0)
        _zero_first([dcw_ref, dal_ref, ddt_ref] + tails)
        xs = _conv_taps(cur_ref, prev_ref, i == nb - 1)
        _, vjp = jax.vjp(_f_dn_pre, *xs, ba_ref[...], cw_ref[...], al_ref[...], dt_ref[...])
        *dxs, dba, dcw, dal, ddt = vjp((dq_ref[...], dk_ref[...], dv_ref[...], dbb_ref[...], dgb_ref[...]))
        total = dxs[CONV - 1]
        for j, t in enumerate(tails):
            n = CONV - 1 - j
            total = total + _rows_up(dxs[j], n, t[...])
            t[...] = dxs[j][0:HALO, :]
        dproj_ref[...] = jnp.concatenate(
            [total.astype(BF16)] + [r[...] for r in o_refs] + [dba.astype(BF16), jnp.zeros((tm, PW - C_BA - 128), BF16)],
            axis=1)
        dcw_ref[...] += dcw
        dal_ref[...] += dal
        ddt_ref[...] += ddt

    o_specs = [pl.BlockSpec((tm, t.shape[1]), lambda i: (nb - 1 - i, 0)) for t in others]
    return _pc(kern, "dn_pre_bwd", (nb,), [cur, prev, ba] + full + [row] * 5 + o_specs,
               [pl.BlockSpec((tm, PW), lambda i: (nb - 1 - i, 0))] + full,
               [SDS((s_len, PW), BF16), SDS((CONV, QKVW), F32), SDS((1, DN_H), F32), SDS((1, DN_H), F32)],
               scratch=[pltpu.VMEM((HALO, QKVW), F32)] * (CONV - 1))(proj, proj, proj, conv_w, alog, dtb, *cots, *others)


def _w_in_to_padded(w_sh):
    tr = 256

    def kern(w_ref, o_ref):
        full = jnp.concatenate([w_ref[s] for s in range(N_CHIPS)], axis=1)
        pieces = [full[:, o0:o0 + w] for o0, w, _ in sorted(_ORIG_PIECES, key=lambda t: t[2])]
        o_ref[...] = jnp.concatenate(pieces + [jnp.zeros((tr, PW - D_IN), w_ref.dtype)], axis=1)

    return _pc(kern, "w_in_to_padded", (D // tr,), [pl.BlockSpec((N_CHIPS, tr, D_IN // N_CHIPS), lambda i: (0, i, 0))],
               pl.BlockSpec((tr, PW), lambda i: (i, 0)), SDS((D, PW), w_sh.dtype))(w_sh)


def _padded_to_w_in(g):
    tr = 256
    csh = D_IN // N_CHIPS

    def kern(g_ref, o_ref):
        x = g_ref[...]
        full = jnp.concatenate([x[:, p0:p0 + w] for _, w, p0 in _ORIG_PIECES], axis=1)
        for s in range(N_CHIPS):
            o_ref[s] = full[:, s * csh:(s + 1) * csh]

    return _pc(kern, "padded_to_w_in", (D // tr,), [pl.BlockSpec((tr, PW), lambda i: (i, 0))],
               pl.BlockSpec((N_CHIPS, tr, csh), lambda i: (0, i, 0)), SDS((N_CHIPS, D, csh), g.dtype))(g)


def _pad_w_in(w_in):
    pieces = [w_in[:, o0:o0 + w] for o0, w, _ in sorted(_ORIG_PIECES, key=lambda t: t[2])]
    pieces.append(jnp.zeros((w_in.shape[0], PW - D_IN), w_in.dtype))
    return jnp.concatenate(pieces, axis=1)


def _unpad_w_in(g):
    return jnp.concatenate([g[:, p0:p0 + w] for _, w, p0 in _ORIG_PIECES], axis=1)


def _local_step(x, target, wts):
    s_len = x.shape[0]
    tm = min(256, s_len)
    w_in_p = wts["w_in_p"]
    attn_gain = wts["attn_norm"]
    ffn_gain = wts["ffn_norm"]
    conv_w = wts["dn_conv"]
    alog, dtb, out_gain = wts["dn_a_log"], wts["dn_dt_bias"], wts["dn_out_norm"]
    qg, kg = wts["swa_q_norm"], wts["swa_k_norm"]
    sinks = wts["swa_sinks"].reshape(SWA_KV, 1, SWA_G)

    h, proj = _in_proj(x, attn_gain, w_in_p)
    q_dn, k_dn, v_dn, bb, gb = _dn_pre_fwd(proj, conv_w, alog, dtb)
    o_dn, s_all, t_all = _dn_chunks_fwd(q_dn, k_dn, v_dn, gb, bb)
    post_ins = [_whole(o_dn), (proj, DNW, C_Z // DNW)]
    (y_dn,) = _rows(lambda r, f: ([_f_dn_post(r[0], r[1], f[0])], []), "dn_post_fwd", s_len, tm, post_ins,
                    [out_gain], [(DNW, BF16)])

    bias = _bias_expand(wts["rel_bias"].T).reshape(SWA_H, BLK, 2 * BLK)
    y_swa = _swa_fwd(proj, bias, qg, kg, sinks)

    wts = {**wts, **wts["late"](y_swa)}
    p_a, p_b, merged = _branch_merge(y_dn, y_swa, wts["wa"], wts["wb"], proj)
    x1, h2 = _out_proj(merged, wts["w_out"], x, ffn_gain)
    gt, up, act = _ffn_up(h2, wts["wg"], wts["wu"])
    dy, dy_b, loss = _ffn_down_loss(act, wts["wd"], x1, target)

    grads = {}
    d_gt, d_up = _ffn_dact(dy_b, wts["wd"], gt, up)
    (grads["w_down"],) = _gw_ffn([act], dy_b, "gw_down")
    grads["w_gate"], grads["w_up"] = _gw_ffn([d_gt, d_up], h2, "gw_gate_up")
    token = wts["send_ffn"](grads)
    dx1, dx1_b, grads["ffn_norm"] = _ffn_dh2(d_gt, d_up, wts["wg"], wts["wu"], x1, dy,
                                             ffn_gain + token[0:1, 0:1])
    grads["w_out"] = _mm(merged, dx1_b, "tn", BF16, 512, 512, "gw_out")
    d_pa, d_pb, d_gr = _merge_bwd(dx1_b, wts["w_out"], p_a, p_b, proj)
    d_ydn, d_yswa = _d_branch(d_pa, d_pb, wts["wa"], wts["wb"])
    grads["w_branch_dn"], grads["w_branch_swa"] = _gw_branch(y_dn, y_swa, d_pa, d_pb)
    token = wts["send_early"](grads)
    qg_t = qg + token[0:1, 0:1]
    out_gain_t = out_gain + token[0:1, 0:1]

    d_sq, d_sk, d_sv, d_bias, grads["swa_q_norm"], grads["swa_k_norm"], d_sinks = _swa_bwd(
        proj, bias, qg_t, kg, sinks, d_yswa)
    grads["swa_sinks"] = d_sinks.reshape(1, SWA_H)
    grads["rel_bias"] = _bias_reduce(d_bias.reshape(SWA_H, BLK * 2 * BLK)).T

    def post_bwd(r, f):
        _, vjp = jax.vjp(_f_dn_post, r[0], r[1], f[0])
        d_o, d_z, d_gain = vjp(r[2])
        return [d_o, d_z], [d_gain]

    d_o, d_z, grads["dn_out_norm"] = _rows(post_bwd, "dn_post_bwd", s_len, tm, post_ins + [_whole(d_ydn)], [out_gain_t],
                                           [(DNW, F32), (DNW, BF16)], [(1, DH)])
    d_q, d_k, d_v, d_gb, d_bb = _dn_chunks_bwd(q_dn, k_dn, v_dn, gb, bb, s_all, t_all, d_o)

    d_proj, grads["dn_conv"], grads["dn_a_log"], grads["dn_dt_bias"] = _dn_pre_bwd(
        proj, conv_w, alog, dtb, (d_q, d_k, d_v, d_bb, d_gb), (d_z, d_gr, d_sq, d_sk, d_sv))
    grads["w_in_p"] = _mm(h, d_proj, "tn", BF16, 512, 1024, "gw_in")
    token = wts["send_in"](grads["w_in_p"])
    grad_x, grads["attn_norm"] = _dh_rms(d_proj, w_in_p, x, dx1, attn_gain + token[0:1, 0:1])
    return loss, grad_x, grads


_HBM = pl.BlockSpec(memory_space=pl.ANY)


def _place():
    return lax.axis_index("x"), lax.axis_index("y"), lax.axis_index("c")


def _other_chips(x, y):
    return [(1 - x, y), (x, 1 - y), (1 - x, 1 - y)]


def _rcopy(src, dst, send_sems, recv_sems, k, to):
    return pltpu.make_async_remote_copy(src_ref=src, dst_ref=dst, send_sem=send_sems.at[k], recv_sem=recv_sems.at[k],
                                        device_id=to, device_id_type=MESH)


def _comm_call(body, name, ins, out_shapes, n_remote, landing=0):
    first = len(ins) - landing
    return pl.pallas_call(
        body, name=name, in_specs=[_HBM] * len(ins), out_specs=[_HBM] * len(out_shapes), out_shape=out_shapes,
        scratch_shapes=[pltpu.SemaphoreType.DMA((n_remote,)), pltpu.SemaphoreType.DMA((n_remote,))],
        input_output_aliases={first + i: i for i in range(landing)},
        compiler_params=_cparams(has_side_effects=True),
    )(*ins)


def _own_slot(blocks, chip):
    return [lax.dynamic_update_slice(lax.empty((N_CHIPS,) + b.shape, b.dtype), b[None], (chip, 0, 0)) for b in blocks]


def _gather_weights(ws, chip):
    n = len(ws)
    halves = [w.shape[0] // 2 for w in ws]

    def body(*refs):
        w_refs, o_refs = refs[:n], refs[2 * n:3 * n]
        send_sems, recv_sems = refs[3 * n:]
        x, y, c = _place()
        s = 2 * x + y
        sib = (x, y, 1 - c)
        chips = _other_chips(x, y)

        def rows(i, half):
            return pl.ds(half * halves[i], halves[i])

        first = []
        for j, (cx, cy) in enumerate(chips):
            for i in range(n):
                cp = _rcopy(w_refs[i].at[rows(i, c), :], o_refs[i].at[s, rows(i, c), :], send_sems, recv_sems,
                            j * n + i, (cx, cy, c))
                cp.start()
                first.append(cp)
        passed = []
        for j, (cx, cy) in enumerate(chips):
            sj = 2 * cx + cy
            for i in range(n):
                blk = o_refs[i].at[sj, rows(i, c), :]
                _rcopy(blk, blk, send_sems, recv_sems, j * n + i, (cx, cy, c)).wait_recv()
                cp = _rcopy(blk, blk, send_sems, recv_sems, (3 + j) * n + i, sib)
                cp.start()
                passed.append(cp)
        for j, (cx, cy) in enumerate(chips):
            sj = 2 * cx + cy
            for i in range(n):
                blk = o_refs[i].at[sj, rows(i, 1 - c), :]
                _rcopy(blk, blk, send_sems, recv_sems, (3 + j) * n + i, sib).wait_recv()
        for cp in first + passed:
            cp.wait_send()

    return _comm_call(body, "gather_weights", list(ws) + _own_slot(ws, chip),
                      [SDS((N_CHIPS,) + w.shape, w.dtype) for w in ws], 6 * n, landing=n)


_HBM_ONLY = pl.BlockSpec(memory_space=pltpu.HBM)
_SEM = pl.BlockSpec(memory_space=pltpu.SEMAPHORE)
_DATAFLOW = pltpu.SideEffectType.DATAFLOW_SIDE_EFFECTING


def _in_hbm(a):
    return pltpu.with_memory_space_constraint(a, pltpu.HBM)


def _gather_windows(blocks):
    halves = [b.shape[0] // 2 for b in blocks]

    def src_at(ref, i, c, sj):
        return ref.at[pl.ds(c * halves[i], halves[i]), :]

    def dst_at(ref, i, c, s_from):
        return ref.at[s_from, pl.ds(c * halves[i], halves[i]), :]

    return src_at, dst_at


def _exchange_windows():
    return (lambda ref, i, c, sj: ref.at[sj]), (lambda ref, i, c, s_from: ref.at[s_from])


def _swap_windows(gs):
    halves = [g.shape[1] // 2 for g in gs]
    return ((lambda ref, i, c, tag: ref.at[:, pl.ds((1 - c) * halves[i], halves[i]), :]),
            (lambda ref, i, c, slot: ref))


def _chip_peers(x, y, c):
    return [(2 * cx + cy, (cx, cy, c), 2 * x + y, 2 * cx + cy) for cx, cy in _other_chips(x, y)]


def _sibling_peer(x, y, c):
    return [(0, (x, y, 1 - c), 0, 0)]


def _split_start(name, ws, lands, dep, windows, peers=_chip_peers, n_peers=3):
    n = len(ws)
    src_at, dst_at = windows

    def body(*refs):
        w_refs, l_refs = refs[:n], refs[n:2 * n]
        send_sems, recv_sems = refs[2 * n + 1], refs[2 * n + 2]
        token = refs[-1]
        x, y, c = _place()
        for j, (tag, dev, there, _) in enumerate(peers(x, y, c)):
            for i in range(n):
                _rcopy(src_at(w_refs[i], i, c, tag), dst_at(l_refs[i], i, c, there), send_sems, recv_sems,
                       j * n + i, dev).start()
        token[...] = jnp.zeros_like(token)

    outs = pl.pallas_call(
        body, name=name,
        out_shape=(pltpu.SemaphoreType.DMA((n_peers * n,)), pltpu.SemaphoreType.DMA((n_peers * n,)),
                   *[pltpu.HBM(w.shape, w.dtype) for w in ws], *[pltpu.HBM(t.shape, t.dtype) for t in lands],
                   SDS((8, 128), F32)),
        in_specs=[_HBM_ONLY] * (2 * n) + [pl.BlockSpec(memory_space=pl.ANY)],
        out_specs=(_SEM, _SEM, *[_HBM_ONLY] * (2 * n), pl.BlockSpec(memory_space=pltpu.VMEM)),
        input_output_aliases={i: 2 + i for i in range(2 * n)},
        compiler_params=pltpu.CompilerParams(has_side_effects=_DATAFLOW),
    )(*[_in_hbm(w) for w in ws], *[_in_hbm(t) for t in lands], dep)
    return outs[0], outs[1], outs[2:2 + n], outs[2 + n:2 + 2 * n], outs[-1]


def _split_wait(name, w_thru, l_thru, send_sems, recv_sems, after, windows, peers=_chip_peers, with_sources=False):
    n = len(w_thru)
    src_at, dst_at = windows

    def body(*refs):
        w_refs, l_refs = refs[:n], refs[n:2 * n]
        send_sems, recv_sems = refs[2 * n], refs[2 * n + 1]
        x, y, c = _place()
        for j, (tag, dev, _, here) in enumerate(peers(x, y, c)):
            for i in range(n):
                cp = _rcopy(src_at(w_refs[i], i, c, tag), dst_at(l_refs[i], i, c, here), send_sems, recv_sems,
                            j * n + i, dev)
                cp.wait_send()
                cp.wait_recv()

    outs = pl.pallas_call(
        body, name=name,
        out_shape=[pltpu.HBM(w.shape, w.dtype) for w in w_thru] + [pltpu.HBM(t.shape, t.dtype) for t in l_thru],
        in_specs=[_HBM_ONLY] * (2 * n) + [_SEM, _SEM, pl.BlockSpec(memory_space=pl.ANY)],
        out_specs=[_HBM_ONLY] * (2 * n),
        input_output_aliases={i: i for i in range(2 * n)},
        compiler_params=pltpu.CompilerParams(has_side_effects=_DATAFLOW),
    )(*w_thru, *l_thru, send_sems, recv_sems, after)
    return (outs[:n], outs[n:]) if with_sources else outs[n:]


def _sibling_fill(lands):
    n = len(lands)
    halves = [t.shape[1] // 2 for t in lands]

    def body(*refs):
        o_refs = refs[n:2 * n]
        send_sems, recv_sems = refs[2 * n:]
        x, y, c = _place()
        sib = (x, y, 1 - c)
        chips = _other_chips(x, y)
        sent = []
        for j, (cx, cy) in enumerate(chips):
            for i in range(n):
                blk = o_refs[i].at[2 * cx + cy, pl.ds(c * halves[i], halves[i]), :]
                cp = _rcopy(blk, blk, send_sems, recv_sems, j * n + i, sib)
                cp.start()
                sent.append(cp)
        for j, (cx, cy) in enumerate(chips):
            for i in range(n):
                blk = o_refs[i].at[2 * cx + cy, pl.ds((1 - c) * halves[i], halves[i]), :]
                _rcopy(blk, blk, send_sems, recv_sems, j * n + i, sib).wait_recv()
        for cp in sent:
            cp.wait_send()

    return _comm_call(body, "sibling_fill", list(lands), [SDS(t.shape, t.dtype) for t in lands], 3 * n, landing=n)


def _swap_halves(gs, name):
    n = len(gs)
    halves = [g.shape[1] // 2 for g in gs]

    def body(*refs):
        g_refs, o_refs = refs[:n], refs[n:2 * n]
        send_sems, recv_sems = refs[2 * n:]
        x, y, c = _place()
        cps = [_rcopy(g_refs[i].at[:, pl.ds((1 - c) * halves[i], halves[i]), :], o_refs[i], send_sems, recv_sems, i,
                      (x, y, 1 - c)) for i in range(n)]
        for cp in cps:
            cp.start()
        for cp in cps:
            cp.wait()

    return _comm_call(body, name, gs, [SDS((N_CHIPS, h, g.shape[2]), g.dtype) for g, h in zip(gs, halves)], n)


def _swap_reduced(rs, name):
    n = len(rs)

    def body(*refs):
        r_refs, o_refs = refs[:n], refs[n:2 * n]
        send_sems, recv_sems = refs[2 * n:]
        x, y, c = _place()
        cps = [_rcopy(r_refs[i], o_refs[i], send_sems, recv_sems, i, (x, y, 1 - c)) for i in range(n)]
        for cp in cps:
            cp.start()
        for cp in cps:
            cp.wait()

    return _comm_call(body, name, rs, [SDS(r.shape, r.dtype) for r in rs], n)


def _all_sum_small(vec, name):
    n_dev = 8
    flips = [(bx, by, bc) for bx in (0, 1) for by in (0, 1) for bc in (0, 1)][1:]

    def body(v_ref, out_ref, gath, send_sems, recv_sems):
        x, y, c = _place()
        me = 4 * x + 2 * y + c
        gath[me] = v_ref[...]
        sent = []
        for k, (bx, by, bc) in enumerate(flips):
            peer = (x ^ bx, y ^ by, c ^ bc)
            cp = _rcopy(v_ref, gath.at[me], send_sems, recv_sems, k, peer)
            cp.start()
            sent.append(cp)
        for k, (bx, by, bc) in enumerate(flips):
            peer = (x ^ bx, y ^ by, c ^ bc)
            _rcopy(v_ref, gath.at[4 * peer[0] + 2 * peer[1] + peer[2]], send_sems, recv_sems, k, peer).wait_recv()
        for cp in sent:
            cp.wait_send()
        acc = gath[0]
        for d in range(1, n_dev):
            acc = acc + gath[d]
        out_ref[...] = acc

    vm = pl.BlockSpec(memory_space=pltpu.VMEM)
    return pl.pallas_call(
        body, name=name, in_specs=[vm], out_specs=vm, out_shape=SDS(vec.shape, F32),
        scratch_shapes=[pltpu.VMEM((n_dev,) + vec.shape, F32), pltpu.SemaphoreType.DMA((7,)),
                        pltpu.SemaphoreType.DMA((7,))],
        compiler_params=_cparams(has_side_effects=True),
    )(vec)


def _pack_small(vals, extra=None):
    parts = [vals[n].reshape(-1).astype(F32) for n, _ in _SMALL]
    parts.append(jnp.zeros((1,), F32) if extra is None else extra.reshape(1).astype(F32))
    flat = jnp.concatenate(parts)
    flat = jnp.concatenate([flat, jnp.zeros((_SMALL_ROWS * 128 - flat.shape[0],), F32)])
    return flat.reshape(_SMALL_ROWS, 128)


def _unpack_small(packed, shapes):
    flat = packed.reshape(-1)
    return {n: flat[_SMALL_OFF[n][0]:_SMALL_OFF[n][0] + _SMALL_OFF[n][1]].reshape(shapes[n]) for n, _ in _SMALL}


def _pair_sum(gs, gots, core, name):
    n = len(gs)

    def kern(c_ref, *refs):
        for i in range(n):
            refs[2 * n + i][...] = (refs[i][...].astype(F32) + refs[n + i][...].astype(F32)).astype(BF16)

    in_specs = [pl.BlockSpec((1, t.shape[1], t.shape[2]), lambda s, c_ref: (s, c_ref[0], 0)) for t in gots]
    in_specs += [pl.BlockSpec((1, t.shape[1], t.shape[2]), lambda s, c_ref: (s, 0, 0)) for t in gots]
    out_specs = [pl.BlockSpec((1, t.shape[1], t.shape[2]), lambda s, c_ref: (s, 0, 0)) for t in gots]
    return pl.pallas_call(
        kern, name=name,
        grid_spec=pltpu.PrefetchScalarGridSpec(num_scalar_prefetch=1, grid=(N_CHIPS,), in_specs=in_specs,
                                               out_specs=out_specs),
        out_shape=[SDS(t.shape, BF16) for t in gots],
        compiler_params=_cparams(dimension_semantics=("arbitrary",)),
    )(core.reshape(1).astype(jnp.int32), *gs, *gots)


def _chip_sum(qs, name):
    n = len(qs)

    def kern(*refs):
        for i in range(n):
            acc = refs[i][0].astype(F32)
            for s in range(1, N_CHIPS):
                acc = acc + refs[i][s].astype(F32)
            refs[n + i][...] = acc

    in_specs = [pl.BlockSpec((N_CHIPS, q.shape[1] // 2, q.shape[2]), lambda j: (0, j, 0)) for q in qs]
    out_specs = [pl.BlockSpec((q.shape[1] // 2, q.shape[2]), lambda j: (j, 0)) for q in qs]
    return _pc(kern, name, (2,), in_specs, out_specs, [SDS(q.shape[1:], F32) for q in qs])(*qs)


def _adam_math(w_, g_, m_, v_):
    m_ = ADAM_B1 * m_ + (1.0 - ADAM_B1) * g_
    v_ = ADAM_B2 * v_ + (1.0 - ADAM_B2) * jnp.square(g_)
    m_hat = m_ / (1.0 - ADAM_B1 ** ADAM_STEP)
    v_hat = v_ / (1.0 - ADAM_B2 ** ADAM_STEP)
    return -ADAM_LR * (m_hat / (jnp.sqrt(v_hat) + ADAM_EPS) + ADAM_WD * w_), m_, v_


def _adamw(w, g, m, v, name):
    rows, cols = w.shape
    tr = rows
    for cand in (256, 128, 64, 32, 16, 8):
        if rows % cand == 0 and rows > cand:
            tr = cand
            break

    def kern(w_ref, g_ref, m_ref, v_ref, d_ref, nm_ref, nv_ref):
        d_ref[...], nm_ref[...], nv_ref[...] = _adam_math(w_ref[...], g_ref[...], m_ref[...], v_ref[...])

    spec = pl.BlockSpec((tr, cols), lambda i: (i, 0))
    return _pc(kern, name, (rows // tr,), [spec] * 4, [spec] * 3, [SDS(w.shape, F32)] * 3)(w, g, m, v)


def _adamw_rows1(w, g, m, v, name):
    rows, _, cols = w.shape
    tr = next(t for t in (203, 174, 128, 64, 42, 32, 29, 16, 8, 7, 6, 4, 3, 2, 1) if rows % t == 0)

    def kern(w_ref, g_ref, m_ref, v_ref, go_ref, d_ref, nm_ref, nv_ref):
        g_ = g_ref[...]
        go_ref[...] = g_
        d_ref[...], nm_ref[...], nv_ref[...] = _adam_math(w_ref[...], g_, m_ref[...], v_ref[...])

    spec = pl.BlockSpec((tr, 1, cols), lambda i: (i, 0, 0))
    return _pc(kern, name, (rows // tr,), [spec] * 4, [spec] * 4, [SDS(w.shape, F32)] * 4)(w, g, m, v)


def _adamw_big(w, mine, theirs, m, v, core, name):
    _, rows, cols = w.shape
    half = rows // 2
    tr = next(t for t in (256, 176, 128, 64, 32, 16, 8) if half % t == 0)
    nbh = half // tr

    def kern(c_ref, w_ref, a_ref, b_ref, m_ref, v_ref, g_ref, d_ref, nm_ref, nv_ref):
        g_ = jnp.where(pl.program_id(0) // nbh == c_ref[0], a_ref[...], b_ref[...])
        g_ref[0] = g_
        d_ref[0], nm_ref[0], nv_ref[0] = _adam_math(w_ref[0], g_, m_ref[0], v_ref[0])

    full = pl.BlockSpec((1, tr, cols), lambda i, c_ref: (0, i, 0))
    part = pl.BlockSpec((tr, cols), lambda i, c_ref: (i % nbh, 0))
    return pl.pallas_call(
        kern, name=name,
        grid_spec=pltpu.PrefetchScalarGridSpec(num_scalar_prefetch=1, grid=(rows // tr,),
                                               in_specs=[full, part, part, full, full], out_specs=[full] * 4),
        out_shape=[SDS(w.shape, F32)] * 4,
        compiler_params=_cparams(dimension_semantics=("arbitrary",)),
    )(core.reshape(1).astype(jnp.int32), w, mine, theirs, m, v)


_WEIGHT_NAMES = ("attn_norm", "w_in", "dn_conv", "dn_a_log", "dn_dt_bias", "dn_out_norm", "swa_q_norm", "swa_k_norm",
                 "swa_sinks", "rel_bias", "w_branch_dn", "w_branch_swa", "w_out", "ffn_norm", "w_gate", "w_up",
                 "w_down")
_CONV_SH = QKVW // N_CHIPS


def kernel(x, attn_norm, w_in, dn_conv, dn_a_log, dn_dt_bias, dn_out_norm, swa_q_norm, swa_k_norm, swa_sinks, rel_bias, w_branch_dn, w_branch_swa, w_out, ffn_norm, w_gate, w_up, w_down, loss_target, m_attn_norm, m_w_in, m_dn_conv, m_dn_a_log, m_dn_dt_bias, m_dn_out_norm, m_swa_q_norm, m_swa_k_norm, m_swa_sinks, m_rel_bias, m_w_branch_dn, m_w_branch_swa, m_w_out, m_ffn_norm, m_w_gate, m_w_up, m_w_down, v_attn_norm, v_w_in, v_dn_conv, v_dn_a_log, v_dn_dt_bias, v_dn_out_norm, v_swa_q_norm, v_swa_k_norm, v_swa_sinks, v_rel_bias, v_w_branch_dn, v_w_branch_swa, v_w_out, v_ffn_norm, v_w_gate, v_w_up, v_w_down):
    w = dict(attn_norm=attn_norm, w_in=w_in, dn_conv=dn_conv, dn_a_log=dn_a_log, dn_dt_bias=dn_dt_bias,
             dn_out_norm=dn_out_norm, swa_q_norm=swa_q_norm, swa_k_norm=swa_k_norm, swa_sinks=swa_sinks,
             rel_bias=rel_bias, w_branch_dn=w_branch_dn, w_branch_swa=w_branch_swa, w_out=w_out, ffn_norm=ffn_norm,
             w_gate=w_gate, w_up=w_up, w_down=w_down)
    m = dict(attn_norm=m_attn_norm, w_in=m_w_in, dn_conv=m_dn_conv, dn_a_log=m_dn_a_log, dn_dt_bias=m_dn_dt_bias,
             dn_out_norm=m_dn_out_norm, swa_q_norm=m_swa_q_norm, swa_k_norm=m_swa_k_norm, swa_sinks=m_swa_sinks,
             rel_bias=m_rel_bias, w_branch_dn=m_w_branch_dn, w_branch_swa=m_w_branch_swa, w_out=m_w_out,
             ffn_norm=m_ffn_norm, w_gate=m_w_gate, w_up=m_w_up, w_down=m_w_down)
    v = dict(attn_norm=v_attn_norm, w_in=v_w_in, dn_conv=v_dn_conv, dn_a_log=v_dn_a_log, dn_dt_bias=v_dn_dt_bias,
             dn_out_norm=v_dn_out_norm, swa_q_norm=v_swa_q_norm, swa_k_norm=v_swa_k_norm, swa_sinks=v_swa_sinks,
             rel_bias=v_rel_bias, w_branch_dn=v_w_branch_dn, w_branch_swa=v_w_branch_swa, w_out=v_w_out,
             ffn_norm=v_ffn_norm, w_gate=v_w_gate, w_up=v_w_up, w_down=v_w_down)
    shapes = {n: w[n].shape for n in _WEIGHT_NAMES}

    def two_d(a):
        return a.reshape(a.shape[-2], a.shape[-1]) if a.ndim == 3 else a

    core = lax.axis_index("c")
    chip = 2 * lax.axis_index("x") + lax.axis_index("y")
    small_shapes = {n: two_d(w[n]).shape for n, _ in _SMALL}
    small_shapes["dn_conv"] = (CONV, QKVW)

    conv_loc = two_d(w["dn_conv"])
    conv_part = lax.dynamic_update_slice(jnp.zeros((CONV, QKVW), F32), jnp.where(core == 0, conv_loc, 0.0),
                                         (0, chip * _CONV_SH))
    conv_full = _all_sum_small(conv_part.reshape(CONV * QKVW // 128, 128), "gather_conv").reshape(CONV, QKVW)

    flipped = ("w_gate", "w_up")

    def natural(a, n):
        return a.transpose(0, 2, 1) if n in flipped else a

    w_bf = [two_d(natural(w[n], n).astype(BF16)) for n in _BIG_NAMES]
    (w_in_g,) = _gather_weights(w_bf[:1], chip)
    windows = _gather_windows(w_bf[1:])
    after_sync = w_in_g[0, :8, :128].astype(F32) + conv_full[0:1, :128]
    send_sems, recv_sems, w_thru, l_thru, token = _split_start(
        "gather_start", w_bf[1:], _own_slot(w_bf[1:], chip), after_sync, windows)

    def late(after):
        lands = _split_wait("gather_wait", w_thru, l_thru, send_sems, recv_sems, after, windows)
        g = dict(zip(_BIG_NAMES[1:], _sibling_fill(lands)))
        return dict(wa=g["w_branch_dn"], wb=g["w_branch_swa"], w_out=g["w_out"].reshape(D, D), wg=g["w_gate"],
                    wu=g["w_up"], wd=g["w_down"])

    wts = dict(w_in_p=_w_in_to_padded(w_in_g), dn_conv=conv_full, late=late)
    for n, _ in _SMALL[:-1]:
        wts[n] = two_d(w[n])
    wts["attn_norm"] = wts["attn_norm"] + token[0:1, 0:1]

    early = {}

    ffn = {}

    def send_ffn(grads):
        gs = [grads["w_gate"], grads["w_up"], grads["w_down"]]
        lands = [lax.empty((N_CHIPS, g.shape[1] // 2, g.shape[2]), g.dtype) for g in gs]
        ffn["sems"], ffn["recv"], ffn["src"], ffn["land"], tok = _split_start(
            "swap_ffn_start", gs, lands, gs[0][0, :8, :128], _swap_windows(gs), _sibling_peer, 1)
        return tok

    def send_early(grads):
        small = [grads["w_branch_dn"], grads["w_branch_swa"], grads["w_out"].reshape(N_CHIPS, CSH, D)]
        big = [grads["w_gate"], grads["w_up"], grads["w_down"]]
        big, got_big = _split_wait("swap_ffn_wait", ffn["src"], ffn["land"], ffn["sems"], ffn["recv"], small[0],
                                   _swap_windows(big), _sibling_peer, with_sources=True)
        gots = list(_swap_halves(small, "swap_halves_early")) + list(got_big)
        parts = _pair_sum(small + list(big), gots, core, "pair_sum_early")
        own = [lax.dynamic_index_in_dim(p, chip, axis=0, keepdims=False) for p in parts]
        early["sems"], early["recv"], early["src"], early["land"], tok = _split_start(
            "exchange_start", parts, _own_slot(own, chip), parts[0][0, :8, :128], _exchange_windows())
        return tok

    last = {}

    def send_in(g_in_p):
        g_in = [_padded_to_w_in(g_in_p)]
        parts = _pair_sum(g_in, _swap_halves(g_in, "swap_halves_in"), core, "pair_sum_in")
        own = [lax.dynamic_index_in_dim(p, chip, axis=0, keepdims=False) for p in parts]
        last["sems"], last["recv"], last["src"], last["land"], tok = _split_start(
            "exchange_in_start", parts, _own_slot(own, chip), parts[0][0, :8, :128], _exchange_windows())
        return tok

    wts["send_ffn"] = send_ffn
    wts["send_early"] = send_early
    wts["send_in"] = send_in
    loss_sum, grad_x, grads = _local_step(x[0], loss_target[0], wts)

    small_sum = _all_sum_small(_pack_small(grads, loss_sum), "all_sum_small")
    loss = small_sum.reshape(-1)[_LOSS_OFF]
    g_small = _unpack_small(small_sum, small_shapes)

    q_early = _split_wait("exchange_wait", early["src"], early["land"], early["sems"], early["recv"], small_sum,
                          _exchange_windows())
    red_early = _chip_sum(list(q_early), "chip_sum_early")
    their_early = _swap_reduced(red_early, "swap_reduced_early")
    g_out, d_out, m_out, v_out = {}, {}, {}, {}
    for n, mine, other in zip(_BIG_NAMES[1:], red_early, their_early):
        res = _adamw_big(natural(w[n], n), mine, other, natural(m[n], n), natural(v[n], n), core, "adamw_" + n)
        g_out[n], d_out[n], m_out[n], v_out[n] = (natural(t, n) for t in res)

    q_in = _split_wait("exchange_in_wait", last["src"], last["land"], last["sems"], last["recv"],
                       d_out[_BIG_NAMES[-1]], _exchange_windows())
    reduced = _chip_sum(list(q_in), "chip_sum_in")
    theirs = _swap_reduced(reduced, "swap_reduced_in")

    def rows1(a):
        return a.transpose(2, 0, 1)

    def unrows1(a):
        return a.transpose(1, 2, 0)

    g_in_blk = jnp.concatenate([jnp.where(core == 0, reduced[0], theirs[0]),
                                jnp.where(core == 0, theirs[0], reduced[0])], axis=0)
    g_in_r = rows1(g_in_blk[None])
    res = _adamw_rows1(rows1(w["w_in"]), g_in_r, rows1(m["w_in"]), rows1(v["w_in"]), "adamw_w_in")
    g_out["w_in"], d_out["w_in"], m_out["w_in"], v_out["w_in"] = (unrows1(t) for t in res)
    g_conv = lax.dynamic_slice(g_small["dn_conv"], (0, chip * _CONV_SH), (CONV, _CONV_SH))
    g_out["dn_conv"] = g_conv.reshape(shapes["dn_conv"])
    d_, m_, v_ = _adamw(conv_loc, g_conv, two_d(m["dn_conv"]), two_d(v["dn_conv"]), "adamw_dn_conv")
    d_out["dn_conv"], m_out["dn_conv"], v_out["dn_conv"] = (t.reshape(shapes["dn_conv"]) for t in (d_, m_, v_))

    def packed(src):
        vals = {n: src[n] for n, _ in _SMALL[:-1]}
        vals["dn_conv"] = jnp.zeros((CONV * QKVW,), F32)
        return _pack_small(vals)

    d_s, m_s, v_s = _adamw(packed(w), small_sum, packed(m), packed(v), "adamw_small")
    d_small, m_small, v_small = (_unpack_small(t, small_shapes) for t in (d_s, m_s, v_s))
    for n, _ in _SMALL[:-1]:
        g_out[n] = g_small[n].reshape(shapes[n])
        d_out[n], m_out[n], v_out[n] = (t[n].reshape(shapes[n]) for t in (d_small, m_small, v_small))

    return (loss, grad_x[None], *[g_out[n] for n in _WEIGHT_NAMES], *[d_out[n] for n in _WEIGHT_NAMES],
            *[m_out[n] for n in _WEIGHT_NAMES], *[v_out[n] for n in _WEIGHT_NAMES])
```

```python
import functools
import math

import numpy as np
import jax
import jax.numpy as jnp
from jax import lax
from jax.experimental import pallas as pl
from jax.experimental.pallas import tpu as pltpu

F32 = jnp.float32
BF16 = jnp.bfloat16
SDS = jax.ShapeDtypeStruct

D = 1024
DN_H = 4
DH = 128
DNW = DN_H * DH
QKVW = 3 * DNW
CONV = 4
CHUNK = 64
SWA_H = 8
SWA_KV = 2
SWA_G = SWA_H // SWA_KV
SWA_D = 64
SWAW = SWA_H * SWA_D
SWAKW = SWA_KV * SWA_D
BLK = 128
NBUCKET = 32
MAXDIST = 128
DFF = 2816
D_IN = QKVW + DNW + 2 * DN_H + SWAW + 2 * SWAKW + 2 * D
EPS = 1e-6
NEG = -1e30

ADAM_LR = 0.001
ADAM_B1 = 0.9
ADAM_B2 = 0.999
ADAM_EPS = 1e-08
ADAM_WD = 0.01
ADAM_STEP = 10

C_QKV, C_Z, C_GATE, C_SQ, C_SK, C_SV, C_BA = 0, 1536, 2048, 4096, 4608, 4736, 4864
PW = 5120
_ORIG_PIECES = (
    (0, QKVW, C_QKV),
    (QKVW, DNW, C_Z),
    (QKVW + DNW, 2 * DN_H, C_BA),
    (QKVW + DNW + 2 * DN_H, SWAW, C_SQ),
    (QKVW + DNW + 2 * DN_H + SWAW, SWAKW, C_SK),
    (QKVW + DNW + 2 * DN_H + SWAW + SWAKW, SWAKW, C_SV),
    (QKVW + DNW + 2 * DN_H + SWAW + 2 * SWAKW, 2 * D, C_GATE),
)

N_CHIPS = 4
FSH = DFF // N_CHIPS
CSH = D // N_CHIPS
VMEM_LIMIT = 48 * 1024 * 1024
MESH = pl.DeviceIdType.MESH

_BIG = (
    ("w_in", D, D_IN // N_CHIPS),
    ("w_branch_dn", DNW, CSH),
    ("w_branch_swa", SWAW, CSH),
    ("w_out", CSH, D),
    ("w_gate", FSH, D),
    ("w_up", FSH, D),
    ("w_down", FSH, D),
)
_BIG_NAMES = tuple(n for n, _, _ in _BIG)

_SMALL = (
    ("attn_norm", D), ("ffn_norm", D), ("dn_out_norm", DH), ("swa_q_norm", SWA_D), ("swa_k_norm", SWA_D),
    ("swa_sinks", SWA_H), ("dn_a_log", DN_H), ("dn_dt_bias", DN_H), ("rel_bias", NBUCKET * SWA_H),
    ("dn_conv", CONV * QKVW),
)
_SMALL_OFF = {}
_o = 0
for _n, _s in _SMALL:
    _SMALL_OFF[_n] = (_o, _s)
    _o += _s
_LOSS_OFF = _o
_SMALL_ROWS = -(-(_o + 1) // (8 * 128)) * 8


def _cparams(**kw):
    return pltpu.CompilerParams(vmem_limit_bytes=VMEM_LIMIT, **kw)


_DIMS = {
    "nn": (((1,), (0,)), ((), ())),
    "nt": (((1,), (1,)), ((), ())),
    "tn": (((0,), (0,)), ((), ())),
    "bnn": (((2,), (1,)), ((0,), (0,))),
    "bnt": (((2,), (2,)), ((0,), (0,))),
    "btn": (((1,), (1,)), ((0,), (0,))),
}


def _raw_dot(a, b, kind, exact):
    if exact:
        prec = lax.Precision.HIGH if exact == "x3" else lax.Precision.HIGHEST
        return lax.dot_general(a, b, _DIMS[kind], precision=prec, preferred_element_type=F32)
    return lax.dot_general(a.astype(BF16), b.astype(BF16), _DIMS[kind], preferred_element_type=F32)


@functools.partial(jax.custom_vjp, nondiff_argnums=(2, 3))
def _dot(a, b, kind, exact):
    return _raw_dot(a, b, kind, exact)


def _dot_fwd(a, b, kind, exact):
    return _raw_dot(a, b, kind, exact), (a, b)


def _dot_bwd(kind, exact, res, g):
    a, b = res
    pre = kind[:-2]
    nn, nt, tn = pre + "nn", pre + "nt", pre + "tn"
    if kind == nn:
        return _dot(g, b, nt, exact), _dot(a, g, tn, exact)
    if kind == nt:
        return _dot(g, b, nn, exact), _dot(g, a, tn, exact)
    return _dot(b, g, nt, exact), _dot(a, g, nn, exact)


_dot.defvjp(_dot_fwd, _dot_bwd)


def _silu(x):
    return x * jax.nn.sigmoid(x)


def _f_rms(x, gain):
    return x * lax.rsqrt(jnp.mean(x * x, axis=-1, keepdims=True) + EPS) * gain


def _f_dn_pre(xs0, xs1, xs2, xs3, ba, cw, alog, dtb):
    rows = xs0.shape[0]
    c = xs0 * cw[0:1] + xs1 * cw[1:2] + xs2 * cw[2:3] + xs3 * cw[3:4]
    qkv = _silu(c)
    qs, ks, bbs, gbs = [], [], [], []
    for h in range(DN_H):
        qh = qkv[:, h * DH:(h + 1) * DH]
        kh = qkv[:, DNW + h * DH:DNW + (h + 1) * DH]
        qs.append(qh * lax.rsqrt(jnp.sum(qh * qh, axis=-1, keepdims=True) + EPS) * (DH ** -0.5))
        ks.append(kh * lax.rsqrt(jnp.sum(kh * kh, axis=-1, keepdims=True) + EPS))
        beta = jax.nn.sigmoid(ba[:, h:h + 1])
        ar = ba[:, DN_H + h:DN_H + h + 1] + dtb[:, h:h + 1]
        softplus = jnp.maximum(ar, 0.0) + jnp.log1p(jnp.exp(-jnp.abs(ar)))
        g = -jnp.exp(alog[:, h:h + 1]) * softplus
        bbs.append(jnp.broadcast_to(beta, (rows, DH)))
        gbs.append(jnp.broadcast_to(g, (rows, DH)))
    return (jnp.concatenate(qs, axis=1), jnp.concatenate(ks, axis=1), qkv[:, 2 * DNW:],
            jnp.concatenate(bbs, axis=1), jnp.concatenate(gbs, axis=1))


def _f_dn_post(o, z, gain):
    ys = []
    for h in range(DN_H):
        oh = o[:, h * DH:(h + 1) * DH]
        zh = z[:, h * DH:(h + 1) * DH]
        ys.append(oh * lax.rsqrt(jnp.mean(oh * oh, axis=-1, keepdims=True) + EPS) * gain * _silu(zh))
    return jnp.concatenate(ys, axis=1)


def _f_merge(pa, pb, ga, gb):
    return jax.nn.sigmoid(ga) * pa + jax.nn.sigmoid(gb) * pb


@jax.custom_vjp
def _f_swiglu(g, u):
    return _silu(g) * u


def _f_swiglu_fwd(g, u):
    return _silu(g) * u, (g, u)


def _f_swiglu_bwd(res, d):
    g, u = res
    s = jax.nn.sigmoid(g)
    act = g * s
    return d * u * (s + act * (1.0 - s)), d * act


_f_swiglu.defvjp(_f_swiglu_fwd, _f_swiglu_bwd)


@jax.custom_vjp
def _unit_lower_inverse(a):
    c = a.shape[-1]
    eye = (lax.broadcasted_iota(jnp.int32, a.shape, 1) == lax.broadcasted_iota(jnp.int32, a.shape, 2)).astype(F32)
    p = -a
    t = eye + p
    for _ in range(max(c.bit_length() - 2, 0)):
        p = _raw_dot(p, p, "bnn", "x3")
        t = t + _raw_dot(t, p, "bnn", "x3")
    return t


def _unit_lower_inverse_fwd(a):
    t = _unit_lower_inverse(a)
    return t, t


def _unit_lower_inverse_bwd(t, g):
    return (-_raw_dot(_raw_dot(t, g, "btn", "x3"), t, "bnt", "x3"),)


_unit_lower_inverse.defvjp(_unit_lower_inverse_fwd, _unit_lower_inverse_bwd)


@jax.custom_vjp
def _known_inverse(a, t):
    return t


def _known_inverse_fwd(a, t):
    return t, t


def _known_inverse_bwd(t, g):
    return _unit_lower_inverse_bwd(t, g)[0], jnp.zeros_like(t)


_known_inverse.defvjp(_known_inverse_fwd, _known_inverse_bwd)


def _f_chunk(q, k, v, gb, bb, s, t_known=None, with_t=False):
    c = CHUNK
    nh = q.shape[0]
    ii = lax.broadcasted_iota(jnp.int32, (nh, c, c), 1)
    jj = lax.broadcasted_iota(jnp.int32, (nh, c, c), 2)
    incl = ii >= jj
    strict = ii > jj
    eye = (ii == jj).astype(F32)
    gcb = _dot(incl.astype(F32), gb, "bnn", "x3")
    lane0 = (lax.broadcasted_iota(jnp.int32, (nh, c, DH), 2) == 0).astype(F32)
    gcol = gcb[:, :, :c]
    grow = _dot(lane0, gcb, "bnt", "x3")
    decay = jnp.where(incl, jnp.exp(jnp.where(incl, gcol - grow, 0.0)), 0.0)
    kb = k * bb
    vb = v * bb
    a = jnp.where(strict, _dot(kb, k, "bnt", False) * decay, 0.0)
    t = _unit_lower_inverse(a) if t_known is None else _known_inverse(a, t_known)
    eg = jnp.exp(gcb)
    u = _dot(t, vb, "bnn", "x3")
    w = _dot(t, kb * eg, "bnn", "x3")
    qk = jnp.where(incl, _dot(q, k, "bnt", False) * decay, 0.0)
    qe = q * eg
    glast = gcb[:, c - 1:c, :]
    k_dec = k * jnp.exp(glast - gcb)
    e_last = jnp.exp(glast)
    outs = []
    for g in range(nh // DN_H):
        sl = slice(g * DN_H, (g + 1) * DN_H)
        v_new = u[sl] - _dot(w[sl], s, "bnn", False)
        outs.append(_dot(qe[sl], s, "bnn", False) + _dot(qk[sl], v_new, "bnn", False))
        s = s * e_last[sl] + _dot(k_dec[sl], v_new, "btn", False)
    o = jnp.concatenate(outs, axis=0)
    return (o, s, t) if with_t else (o, s)


def _f_swa(q8, kp, kc, vp, vc, bias8, qg, kg, sink, mask):
    kb = jnp.concatenate([kp, kc], axis=1)
    vb = jnp.concatenate([vp, vc], axis=1)
    kn = kb * lax.rsqrt(jnp.mean(kb * kb, axis=-1, keepdims=True) + EPS) * kg

    def rows(per_head):
        return jnp.stack([jnp.concatenate([per_head(kv, g) for g in range(SWA_G)], axis=0)
                          for kv in range(SWA_KV)], axis=0)

    qq = rows(lambda kv, g: q8[kv * SWA_G + g])
    qn = qq * lax.rsqrt(jnp.mean(qq * qq, axis=-1, keepdims=True) + EPS) * qg * (SWA_D ** -0.5)
    lg = _dot(qn, kn, "bnt", False) + rows(lambda kv, g: bias8[kv * SWA_G + g])
    lg = jnp.where(rows(lambda kv, g: mask), lg, NEG)
    sk = rows(lambda kv, g: jnp.broadcast_to(sink[kv][:, g:g + 1], (BLK, 1)))
    m = lax.stop_gradient(jnp.maximum(jnp.max(lg, axis=-1, keepdims=True), sk))
    p = jnp.exp(lg - m)
    den = jnp.sum(p, axis=-1, keepdims=True) + jnp.exp(sk - m)
    out = _dot(p * (1.0 / den), vb, "bnn", False)
    return jnp.stack([out[kv, g * BLK:(g + 1) * BLK] for kv in range(SWA_KV) for g in range(SWA_G)], axis=0)


def _bdot(a, b, kind="nn"):
    return lax.dot_general(a.astype(BF16), b.astype(BF16), _DIMS[kind], preferred_element_type=F32)


def _pc(kern, name, grid, in_specs, out_specs, out_shape, scratch=()):
    return pl.pallas_call(
        kern, name=name, grid=grid, in_specs=in_specs, out_specs=out_specs, out_shape=out_shape,
        scratch_shapes=list(scratch), compiler_params=_cparams(dimension_semantics=("arbitrary",) * len(grid)))


def _mm(a, b, kind, out_dtype, tm, tn, name):
    if kind == "tn":
        k, m = a.shape
    else:
        m, k = a.shape
    n = b.shape[0] if kind == "nt" else b.shape[1]
    tm, tn = min(tm, m), min(tn, n)
    assert m % tm == 0 and n % tn == 0, (name, a.shape, b.shape, tm, tn)

    def kern(a_ref, b_ref, o_ref):
        o_ref[...] = _bdot(a_ref[...], b_ref[...], kind).astype(o_ref.dtype)

    a_spec = pl.BlockSpec((k, tm), lambda i, j: (0, i)) if kind == "tn" else pl.BlockSpec((tm, k), lambda i, j: (i, 0))
    b_spec = pl.BlockSpec((tn, k), lambda i, j: (j, 0)) if kind == "nt" else pl.BlockSpec((k, tn), lambda i, j: (0, j))
    return _pc(kern, name, (m // tm, n // tn), [a_spec, b_spec], pl.BlockSpec((tm, tn), lambda i, j: (i, j)),
               SDS((m, n), out_dtype))(a, b)


def _rows(body, name, m, tm, row_ins, full_ins, row_outs, acc_outs=()):
    n_r, n_f, n_o, n_a = len(row_ins), len(full_ins), len(row_outs), len(acc_outs)
    assert m % tm == 0

    def kern(*refs):
        r = refs[:n_r]
        f = refs[n_r:n_r + n_f]
        o = refs[n_r + n_f:n_r + n_f + n_o]
        acc = refs[n_r + n_f + n_o:]
        outs, sums = body([x[...] for x in r], [x[...] for x in f])
        for ref, val in zip(o, outs, strict=True):
            ref[...] = val.astype(ref.dtype)
        if n_a:
            @pl.when(pl.program_id(0) == 0)
            def _():
                for ref in acc:
                    ref[...] = jnp.zeros(ref.shape, F32)

            for ref, val in zip(acc, sums, strict=True):
                ref[...] += val

    in_specs = [pl.BlockSpec((tm, w), functools.partial(lambda i, cb: (i, cb), cb=cb)) for _, w, cb in row_ins]
    in_specs += [pl.BlockSpec(x.shape, lambda i: (0, 0)) for x in full_ins]
    out_specs = [pl.BlockSpec((tm, w), lambda i: (i, 0)) for w, _ in row_outs]
    out_specs += [pl.BlockSpec(s, lambda i: (0, 0)) for s in acc_outs]
    out_shape = [SDS((m, w), dt) for w, dt in row_outs]
    out_shape += [SDS(s, F32) for s in acc_outs]
    return _pc(kern, name, (m // tm,), in_specs, out_specs, out_shape)(*[x for x, _, _ in row_ins], *full_ins)


def _whole(x):
    return (x, x.shape[1], 0)


def _resident(shape):
    return pl.BlockSpec(shape, lambda i: (0,) * len(shape), pipeline_mode=pl.Buffered(1))


def _row_pieces(tm, piece):
    piece = min(piece, tm)
    return [slice(r, r + piece) for r in range(0, tm, piece)]


def _zero_first(refs):
    @pl.when(pl.program_id(0) == 0)
    def _():
        for ref in refs:
            ref[...] = jnp.zeros(ref.shape, F32)


GROUP = 4


def _heads(ref):
    return jnp.stack([ref[g * CHUNK:(g + 1) * CHUNK, h * DH:(h + 1) * DH]
                      for g in range(GROUP) for h in range(DN_H)], axis=0)


def _unheads(ref, val):
    for g in range(GROUP):
        for h in range(DN_H):
            ref[g * CHUNK:(g + 1) * CHUNK, h * DH:(h + 1) * DH] = val[g * DN_H + h]


def _dn_chunks_fwd(q, k, v, gb, bb):
    s_len = q.shape[0]
    ng = s_len // (GROUP * CHUNK)

    def kern(q_ref, k_ref, v_ref, g_ref, b_ref, o_ref, sall_ref, t_ref, state):
        _zero_first([state])
        s = state[...]
        sall_ref[0] = s
        o, s_new, t = _f_chunk(*[_heads(r) for r in (q_ref, k_ref, v_ref, g_ref, b_ref)], s, with_t=True)
        _unheads(o_ref, o)
        t_ref[0] = t
        state[...] = s_new

    blk = pl.BlockSpec((GROUP * CHUNK, DNW), lambda c: (c, 0))
    return _pc(kern, "dn_chunks_fwd", (ng,), [blk] * 5,
               [blk, pl.BlockSpec((1, DN_H, DH, DH), lambda c: (c, 0, 0, 0)),
                pl.BlockSpec((1, GROUP * DN_H, CHUNK, CHUNK), lambda c: (c, 0, 0, 0))],
               [SDS((s_len, DNW), F32), SDS((ng, DN_H, DH, DH), F32), SDS((ng, GROUP * DN_H, CHUNK, CHUNK), F32)],
               scratch=[pltpu.VMEM((DN_H, DH, DH), F32)])(q, k, v, gb, bb)


def _dn_chunks_bwd(q, k, v, gb, bb, s_all, t_all, d_o):
    s_len = q.shape[0]
    ng = s_len // (GROUP * CHUNK)

    def kern(q_ref, k_ref, v_ref, g_ref, b_ref, sall_ref, t_ref, do_ref, dq_ref, dk_ref, dv_ref, dg_ref, db_ref,
             dstate):
        _zero_first([dstate])
        fn = functools.partial(_f_chunk, t_known=t_ref[0])
        _, vjp = jax.vjp(fn, *[_heads(r) for r in (q_ref, k_ref, v_ref, g_ref, b_ref)], sall_ref[0])
        *d_ins, ds = vjp((_heads(do_ref), dstate[...]))
        for ref, val in zip((dq_ref, dk_ref, dv_ref, dg_ref, db_ref), d_ins, strict=True):
            _unheads(ref, val)
        dstate[...] = ds

    blk = pl.BlockSpec((GROUP * CHUNK, DNW), lambda c: (ng - 1 - c, 0))
    return _pc(kern, "dn_chunks_bwd", (ng,),
               [blk] * 5 + [pl.BlockSpec((1, DN_H, DH, DH), lambda c: (ng - 1 - c, 0, 0, 0)),
                            pl.BlockSpec((1, GROUP * DN_H, CHUNK, CHUNK), lambda c: (ng - 1 - c, 0, 0, 0)), blk],
               [blk] * 5, [SDS((s_len, DNW), F32)] * 5,
               scratch=[pltpu.VMEM((DN_H, DH, DH), F32)])(q, k, v, gb, bb, s_all, t_all, d_o)


def _t5_bucket_table():
    qi = np.arange(BLK)[:, None]
    kj = np.arange(2 * BLK)[None, :]
    dist = BLK + qi - kj
    n = np.maximum(dist, 0)
    max_exact = NBUCKET // 2
    nf = np.maximum(n, 1).astype(np.float32)
    large = max_exact + (np.log(nf / np.float32(max_exact)) / np.float32(math.log(MAXDIST / max_exact))
                         * np.float32(NBUCKET - max_exact)).astype(np.int32)
    large = np.minimum(large, NBUCKET - 1)
    return np.where(n < max_exact, n, large)


def _bucket_onehot_t():
    table = _t5_bucket_table().reshape(-1)
    return (np.arange(NBUCKET)[:, None] == table[None, :]).astype(np.float32)


def _swa_mask(first):
    qi = lax.broadcasted_iota(jnp.int32, (BLK, 2 * BLK), 0)
    kj = lax.broadcasted_iota(jnp.int32, (BLK, 2 * BLK), 1)
    dist = BLK + qi - kj
    window = (dist >= 0) & (dist < BLK)
    return window & ((kj >= BLK) | jnp.logical_not(first))


def _bias_expand(rel_bias_t):
    onehot = jnp.asarray(_bucket_onehot_t())

    def kern(r_ref, oh_ref, o_ref):
        o_ref[...] = _raw_dot(r_ref[...], oh_ref[...], "nn", True)

    return pl.pallas_call(
        kern, name="bias_expand", out_shape=SDS((SWA_H, BLK * 2 * BLK), F32), compiler_params=_cparams(),
    )(rel_bias_t, onehot)


def _bias_reduce(d_bias_flat):
    onehot = jnp.asarray(_bucket_onehot_t())

    def kern(d_ref, oh_ref, o_ref):
        o_ref[...] = _raw_dot(d_ref[...], oh_ref[...], "nt", True)

    return pl.pallas_call(
        kern, name="bias_reduce", out_shape=SDS((SWA_H, NBUCKET), F32), compiler_params=_cparams(),
    )(d_bias_flat, onehot)


def _swa_specs(nb, rev):
    def blk(n):
        return (nb - 1 - n) if rev else n

    def before(n):
        return jnp.maximum(blk(n) - 1, 0)

    q_spec = pl.BlockSpec((BLK, SWAW), lambda n: (blk(n), C_SQ // SWAW))
    k_cur = pl.BlockSpec((BLK, SWAKW), lambda n: (blk(n), C_SK // SWAKW))
    k_prev = pl.BlockSpec((BLK, SWAKW), lambda n: (before(n), C_SK // SWAKW))
    v_cur = pl.BlockSpec((BLK, SWAKW), lambda n: (blk(n), C_SV // SWAKW))
    v_prev = pl.BlockSpec((BLK, SWAKW), lambda n: (before(n), C_SV // SWAKW))
    bias = pl.BlockSpec((SWA_H, BLK, 2 * BLK), lambda n: (0, 0, 0))
    gain = pl.BlockSpec((1, SWA_D), lambda n: (0, 0))
    sink = pl.BlockSpec((SWA_KV, 1, SWA_G), lambda n: (0, 0, 0))
    wide = pl.BlockSpec((BLK, SWAW), lambda n: (blk(n), 0))
    narrow = pl.BlockSpec((BLK, SWAKW), lambda n: (blk(n), 0))
    return [q_spec, k_prev, k_cur, v_prev, v_cur, bias, gain, gain, sink], wide, narrow


def _split_heads(x):
    return jnp.stack([x[:, h * SWA_D:(h + 1) * SWA_D] for h in range(x.shape[1] // SWA_D)], axis=0)


def _join_heads(x):
    return jnp.concatenate([x[h] for h in range(x.shape[0])], axis=1)


def _swa_fwd(proj, bias, qg, kg, sinks):
    s_len = proj.shape[0]
    nb = s_len // BLK
    in_specs, wide, _ = _swa_specs(nb, False)

    def kern(q_ref, kp_ref, kc_ref, vp_ref, vc_ref, b_ref, qg_ref, kg_ref, s_ref, o_ref):
        mask = _swa_mask(pl.program_id(0) == 0)
        o8 = _f_swa(*[_split_heads(r[...]) for r in (q_ref, kp_ref, kc_ref, vp_ref, vc_ref)], b_ref[...], qg_ref[...],
                    kg_ref[...], s_ref[...], mask)
        o_ref[...] = _join_heads(o8).astype(BF16)

    return _pc(kern, "swa_fwd", (nb,), in_specs, wide, SDS((s_len, SWAW), BF16))(
        proj, proj, proj, proj, proj, bias, qg, kg, sinks)


def _swa_bwd(proj, bias, qg, kg, sinks, d_out):
    s_len = proj.shape[0]
    nb = s_len // BLK
    in_specs, wide, narrow = _swa_specs(nb, True)

    def kern(q_ref, kp_ref, kc_ref, vp_ref, vc_ref, b_ref, qg_ref, kg_ref, s_ref, do_ref,
             dq_ref, dk_ref, dv_ref, db_ref, dqg_ref, dkg_ref, ds_ref, carry_k, carry_v):
        n = pl.program_id(0)
        mask = _swa_mask(n == nb - 1)
        _zero_first([carry_k, carry_v, db_ref, ds_ref, dqg_ref, dkg_ref])
        fn = functools.partial(_f_swa, mask=mask)
        _, vjp = jax.vjp(fn, *[_split_heads(r[...]) for r in (q_ref, kp_ref, kc_ref, vp_ref, vc_ref)], b_ref[...],
                         qg_ref[...], kg_ref[...], s_ref[...])
        dq, dkp, dkc, dvp, dvc, dbias, dqg, dkg, dsink = vjp(_split_heads(do_ref[...]))
        dq_ref[...] = _join_heads(dq).astype(BF16)
        dk_ref[...] = (_join_heads(dkc) + carry_k[...]).astype(BF16)
        dv_ref[...] = (_join_heads(dvc) + carry_v[...]).astype(BF16)
        carry_k[...] = _join_heads(dkp)
        carry_v[...] = _join_heads(dvp)
        db_ref[...] += dbias
        dqg_ref[...] += dqg
        dkg_ref[...] += dkg
        ds_ref[...] += dsink

    bias_spec, gain, sink = in_specs[5], in_specs[6], in_specs[8]
    return _pc(
        kern, "swa_bwd", (nb,), in_specs + [wide], [wide, narrow, narrow, bias_spec, gain, gain, sink],
        [SDS((s_len, SWAW), BF16), SDS((s_len, SWAKW), BF16), SDS((s_len, SWAKW), BF16),
         SDS((SWA_H, BLK, 2 * BLK), F32), SDS((1, SWA_D), F32), SDS((1, SWA_D), F32), SDS((SWA_KV, 1, SWA_G), F32)],
        scratch=[pltpu.VMEM((BLK, SWAKW), F32), pltpu.VMEM((BLK, SWAKW), F32)],
    )(proj, proj, proj, proj, proj, bias, qg, kg, sinks, d_out)


def _branch_merge(y_dn, y_swa, wa, wb, proj):
    s_len = y_dn.shape[0]
    tm = min(512, s_len)

    def kern(ya_ref, yb_ref, wa_ref, wb_ref, ga_ref, gb_ref, pa_ref, pb_ref, m_ref):
        pa = _bdot(ya_ref[...], wa_ref[0])
        pb = _bdot(yb_ref[...], wb_ref[0])
        pa_ref[...] = pa.astype(BF16)
        pb_ref[...] = pb.astype(BF16)
        m_ref[...] = _f_merge(pa, pb, ga_ref[...], gb_ref[...]).astype(BF16)

    y_spec = pl.BlockSpec((tm, DNW), lambda i, s: (i, 0))
    w_spec = pl.BlockSpec((1, DNW, CSH), lambda i, s: (s, 0, 0))
    o_spec = pl.BlockSpec((tm, CSH), lambda i, s: (i, s))
    ga_spec = pl.BlockSpec((tm, CSH), lambda i, s: (i, C_GATE // CSH + s))
    gb_spec = pl.BlockSpec((tm, CSH), lambda i, s: (i, (C_GATE + D) // CSH + s))
    return _pc(kern, "branch_merge", (s_len // tm, N_CHIPS), [y_spec, y_spec, w_spec, w_spec, ga_spec, gb_spec],
               [o_spec] * 3, [SDS((s_len, D), BF16)] * 3,
               )(y_dn, y_swa, wa, wb, proj, proj)


def _in_proj(x, gain, w_in_p):
    s_len = x.shape[0]
    tm = min(512, s_len)

    def kern(x_ref, g_ref, w_ref, h_ref, p_ref):
        h = _f_rms(x_ref[...], g_ref[...]).astype(BF16)
        h_ref[...] = h
        p_ref[...] = _bdot(h, w_ref[...])

    row = pl.BlockSpec((tm, D), lambda i: (i, 0))
    return _pc(kern, "in_proj", (s_len // tm,),
               [row, pl.BlockSpec((1, D), lambda i: (0, 0)), _resident((D, PW))],
               [row, pl.BlockSpec((tm, PW), lambda i: (i, 0))],
               [SDS((s_len, D), BF16), SDS((s_len, PW), F32)])(x, gain, w_in_p)


def _out_proj(merged, w_out, x, gain):
    s_len = x.shape[0]
    tm = min(256, s_len)

    def kern(m_ref, w_ref, x_ref, g_ref, x1_ref, h2_ref):
        x1 = x_ref[...] + _bdot(m_ref[...], w_ref[...])
        x1_ref[...] = x1
        h2_ref[...] = _f_rms(x1, g_ref[...]).astype(BF16)

    row = pl.BlockSpec((tm, D), lambda i: (i, 0))
    return _pc(kern, "out_proj", (s_len // tm,),
               [row, pl.BlockSpec((D, D), lambda i: (0, 0)), row, pl.BlockSpec((1, D), lambda i: (0, 0))],
               [row, row], [SDS((s_len, D), F32), SDS((s_len, D), BF16)])(merged, w_out, x, gain)


def _ffn_up(h2, wg, wu):
    s_len = h2.shape[0]
    tm = min(1024, s_len)

    def kern(h_ref, g_ref, u_ref, gt_ref, up_ref, act_ref):
        for rows in _row_pieces(tm, 256):
            h = h_ref[rows, :]
            g = _bdot(h, g_ref[0], "nt")
            u = _bdot(h, u_ref[0], "nt")
            gt_ref[0, rows, :] = g.astype(BF16)
            up_ref[0, rows, :] = u.astype(BF16)
            act_ref[0, rows, :] = _f_swiglu(g, u).astype(BF16)

    w_spec = pl.BlockSpec((1, FSH, D), lambda s, i: (s, 0, 0))
    o_spec = pl.BlockSpec((1, tm, FSH), lambda s, i: (s, i, 0))
    shape = (N_CHIPS, s_len, FSH)
    return _pc(kern, "ffn_up", (N_CHIPS, s_len // tm), [pl.BlockSpec((tm, D), lambda s, i: (i, 0)), w_spec, w_spec],
               [o_spec] * 3, [SDS(shape, BF16)] * 3)(h2, wg, wu)


def _ffn_down_loss(act, wd, x1, target):
    s_len = x1.shape[0]
    tm = min(256, s_len)

    def kern(a_ref, w_ref, x_ref, t_ref, dy_ref, dyb_ref, loss_ref):
        _zero_first([loss_ref])
        for rows in _row_pieces(tm, 128):
            y = x_ref[rows, :]
            for s in range(N_CHIPS):
                y = y + _bdot(a_ref[s, rows, :], w_ref[s])
            d = y - t_ref[rows, :]
            dy = d * (1.0 / D)
            dy_ref[rows, :] = dy
            dyb_ref[rows, :] = dy.astype(BF16)
            loss_ref[...] += jnp.sum(d * d).reshape(1, 1) * (0.5 / D)

    row = pl.BlockSpec((tm, D), lambda i: (i, 0))
    return _pc(kern, "ffn_down_loss", (s_len // tm,),
               [pl.BlockSpec((N_CHIPS, tm, FSH), lambda i: (0, i, 0)),
                pl.BlockSpec((N_CHIPS, FSH, D), lambda i: (0, 0, 0)), row, row],
               [row, row, pl.BlockSpec((1, 1), lambda i: (0, 0))],
               [SDS((s_len, D), F32), SDS((s_len, D), BF16), SDS((1, 1), F32)])(act, wd, x1, target)


def _ffn_dact(dy_b, wd, gt, up):
    s_len = dy_b.shape[0]
    tm = min(1024, s_len)

    def kern(dy_ref, w_ref, gt_ref, up_ref, dg_ref, du_ref):
        w = w_ref[0]
        for rows in _row_pieces(tm, 256):
            d_act = _bdot(dy_ref[rows, :], w, "nt")
            _, vjp = jax.vjp(_f_swiglu, gt_ref[0, rows, :].astype(F32), up_ref[0, rows, :].astype(F32))
            dg, du = vjp(d_act)
            dg_ref[0, rows, :] = dg.astype(BF16)
            du_ref[0, rows, :] = du.astype(BF16)

    a_spec = pl.BlockSpec((1, tm, FSH), lambda s, i: (s, i, 0))
    shape = (N_CHIPS, s_len, FSH)
    return _pc(kern, "ffn_dact", (N_CHIPS, s_len // tm),
               [pl.BlockSpec((tm, D), lambda s, i: (i, 0)), pl.BlockSpec((1, FSH, D), lambda s, i: (s, 0, 0)),
                a_spec, a_spec],
               [a_spec, a_spec], [SDS(shape, BF16), SDS(shape, BF16)])(dy_b, wd, gt, up)


def _gw_ffn(lhs, rhs, name):
    s_len = rhs.shape[0]
    n = len(lhs)
    tn = 512

    def kern(*refs):
        g = refs[n][...]
        for i in range(n):
            refs[n + 1 + i][0] = _bdot(refs[i][0], g, "tn").astype(BF16)

    a_spec = pl.BlockSpec((1, s_len, FSH), lambda s, j: (s, 0, 0))
    o_spec = pl.BlockSpec((1, FSH, tn), lambda s, j: (s, 0, j))
    return _pc(kern, name, (N_CHIPS, D // tn), [a_spec] * n + [pl.BlockSpec((s_len, tn), lambda s, j: (0, j))],
               [o_spec] * n, [SDS((N_CHIPS, FSH, D), BF16)] * n)(*lhs, rhs)


def _ffn_dh2(d_gt, d_up, wg, wu, x1, dy, gain):
    s_len = x1.shape[0]
    tm = min(512, s_len)

    def kern(dg_ref, du_ref, wg_ref, wu_ref, x_ref, dy_ref, g_ref, dx_ref, dxb_ref, dgain_ref):
        _zero_first([dgain_ref])
        dh2 = jnp.zeros((tm, D), F32)
        for s in range(N_CHIPS):
            dh2 = dh2 + _bdot(dg_ref[s], wg_ref[s]) + _bdot(du_ref[s], wu_ref[s])
        _, vjp = jax.vjp(_f_rms, x_ref[...], g_ref[...])
        dx, dgain = vjp(dh2)
        dx1 = dx + dy_ref[...]
        dx_ref[...] = dx1
        dxb_ref[...] = dx1.astype(BF16)
        dgain_ref[...] += dgain

    row = pl.BlockSpec((tm, D), lambda i: (i, 0))
    d_spec = pl.BlockSpec((N_CHIPS, tm, FSH), lambda i: (0, i, 0))
    w_spec = _resident((N_CHIPS, FSH, D))
    vec = pl.BlockSpec((1, D), lambda i: (0, 0))
    return _pc(kern, "ffn_dh2", (s_len // tm,), [d_spec, d_spec, w_spec, w_spec, row, row, vec],
               [row, row, vec], [SDS((s_len, D), F32), SDS((s_len, D), BF16), SDS((1, D), F32)],
               )(d_gt, d_up, wg, wu, x1, dy, gain)


def _merge_bwd(dx1_b, w_out, pa, pb, proj):
    s_len = dx1_b.shape[0]
    tm = min(256, s_len)

    def kern(dx_ref, w_ref, pa_ref, pb_ref, g_ref, dpa_ref, dpb_ref, dg_ref):
        dm = _bdot(dx_ref[...], w_ref[...], "nt")
        gates = g_ref[...]
        _, vjp = jax.vjp(_f_merge, pa_ref[...].astype(F32), pb_ref[...].astype(F32), gates[:, :D], gates[:, D:])
        dpa, dpb, dga, dgb = vjp(dm)
        dpa_ref[...] = dpa.astype(BF16)
        dpb_ref[...] = dpb.astype(BF16)
        dg_ref[:, :D] = dga.astype(BF16)
        dg_ref[:, D:] = dgb.astype(BF16)

    row = pl.BlockSpec((tm, D), lambda i: (i, 0))
    return _pc(kern, "merge_bwd", (s_len // tm,),
               [row, pl.BlockSpec((D, D), lambda i: (0, 0)), row, row,
                pl.BlockSpec((tm, 2 * D), lambda i: (i, C_GATE // (2 * D)))],
               [row, row, pl.BlockSpec((tm, 2 * D), lambda i: (i, 0))],
               [SDS((s_len, D), BF16), SDS((s_len, D), BF16), SDS((s_len, 2 * D), BF16)],
               )(dx1_b, w_out, pa, pb, proj)


def _d_branch(d_pa, d_pb, wa, wb):
    s_len = d_pa.shape[0]
    tm = min(512, s_len)

    def kern(da_ref, db_ref, wa_ref, wb_ref, oa_ref, ob_ref):
        acc_a = jnp.zeros((tm, DNW), F32)
        acc_b = jnp.zeros((tm, SWAW), F32)
        for s in range(N_CHIPS):
            acc_a = acc_a + _bdot(da_ref[:, s * CSH:(s + 1) * CSH], wa_ref[s], "nt")
            acc_b = acc_b + _bdot(db_ref[:, s * CSH:(s + 1) * CSH], wb_ref[s], "nt")
        oa_ref[...] = acc_a
        ob_ref[...] = acc_b

    row = pl.BlockSpec((tm, D), lambda i: (i, 0))
    w_spec = pl.BlockSpec((N_CHIPS, DNW, CSH), lambda i: (0, 0, 0))
    out = pl.BlockSpec((tm, DNW), lambda i: (i, 0))
    return _pc(kern, "d_branch", (s_len // tm,), [row, row, w_spec, w_spec], [out, out],
               [SDS((s_len, DNW), F32), SDS((s_len, SWAW), F32)])(d_pa, d_pb, wa, wb)


def _gw_branch(y_dn, y_swa, d_pa, d_pb):
    s_len = y_dn.shape[0]

    def kern(ya_ref, yb_ref, da_ref, db_ref, oa_ref, ob_ref):
        oa_ref[0] = _bdot(ya_ref[...], da_ref[...], "tn").astype(BF16)
        ob_ref[0] = _bdot(yb_ref[...], db_ref[...], "tn").astype(BF16)

    y_spec = pl.BlockSpec((s_len, DNW), lambda s: (0, 0))
    d_spec = pl.BlockSpec((s_len, CSH), lambda s: (0, s))
    o_spec = pl.BlockSpec((1, DNW, CSH), lambda s: (s, 0, 0))
    shape = (N_CHIPS, DNW, CSH)
    return _pc(kern, "gw_branch", (N_CHIPS,), [y_spec, y_spec, d_spec, d_spec], [o_spec, o_spec],
               [SDS(shape, BF16), SDS(shape, BF16)])(y_dn, y_swa, d_pa, d_pb)


def _dh_rms(d_proj, w_in_p, x, dx1, gain):
    s_len = x.shape[0]
    tm = min(512, s_len)

    def kern(dp_ref, w_ref, x_ref, r_ref, g_ref, gx_ref, dgain_ref):
        _zero_first([dgain_ref])
        dh = _bdot(dp_ref[...], w_ref[...], "nt")
        _, vjp = jax.vjp(_f_rms, x_ref[...], g_ref[...])
        dx, dgain = vjp(dh)
        gx_ref[...] = dx + r_ref[...]
        dgain_ref[...] += dgain

    row = pl.BlockSpec((tm, D), lambda i: (i, 0))
    vec = pl.BlockSpec((1, D), lambda i: (0, 0))
    return _pc(kern, "dh_rms", (s_len // tm,),
               [pl.BlockSpec((tm, PW), lambda i: (i, 0)), _resident((D, PW)), row, row, vec],
               [row, vec], [SDS((s_len, D), F32), SDS((1, D), F32)])(d_proj, w_in_p, x, dx1, gain)


HALO = 8


def _rows_down(x, n, above):
    tm = x.shape[0]
    r = pltpu.roll(x, n, 0)
    a = pltpu.roll(above, n, 0)
    top = jnp.where(lax.broadcasted_iota(jnp.int32, above.shape, 0) < n, a, r[0:HALO])
    return jnp.concatenate([top, r[HALO:tm]], axis=0)


def _rows_up(x, n, below):
    tm = x.shape[0]
    r = pltpu.roll(x, tm - n, 0)
    b = pltpu.roll(below, HALO - n, 0)
    bottom = jnp.where(lax.broadcasted_iota(jnp.int32, below.shape, 0) >= HALO - n, b, r[tm - HALO:tm])
    return jnp.concatenate([r[0:tm - HALO], bottom], axis=0)


def _conv_taps(cur_ref, prev_ref, first):
    cur = cur_ref[...]
    above = jnp.where(first, 0.0, prev_ref[...])
    return [_rows_down(cur, n, above) for n in range(CONV - 1, 0, -1)] + [cur]


def _dn_pre_specs(s_len, tm, blk):
    cur = pl.BlockSpec((tm, QKVW), lambda i: (blk(i), 0))
    prev = pl.BlockSpec((HALO, QKVW), lambda i: (jnp.maximum(blk(i) * (tm // HALO) - 1, 0), 0))
    ba = pl.BlockSpec((tm, 128), lambda i: (blk(i), C_BA // 128))
    row = pl.BlockSpec((tm, DNW), lambda i: (blk(i), 0))
    full = [pl.BlockSpec((CONV, QKVW), lambda i: (0, 0)), pl.BlockSpec((1, DN_H), lambda i: (0, 0)),
            pl.BlockSpec((1, DN_H), lambda i: (0, 0))]
    return cur, prev, ba, row, full


def _dn_pre_fwd(proj, conv_w, alog, dtb):
    s_len = proj.shape[0]
    tm = min(128, s_len)
    cur, prev, ba, row, full = _dn_pre_specs(s_len, tm, lambda i: i)

    def kern(cur_ref, prev_ref, ba_ref, cw_ref, al_ref, dt_ref, q_ref, k_ref, v_ref, bb_ref, gb_ref):
        xs = _conv_taps(cur_ref, prev_ref, pl.program_id(0) == 0)
        outs = _f_dn_pre(*xs, ba_ref[...], cw_ref[...], al_ref[...], dt_ref[...])
        for ref, val in zip((q_ref, k_ref, v_ref, bb_ref, gb_ref), outs, strict=True):
            ref[...] = val

    return _pc(kern, "dn_pre_fwd", (s_len // tm,), [cur, prev, ba] + full, [row] * 5,
               [SDS((s_len, DNW), F32)] * 5)(proj, proj, proj, conv_w, alog, dtb)


def _dn_pre_bwd(proj, conv_w, alog, dtb, cots, others):
    s_len = proj.shape[0]
    tm = min(128, s_len)
    nb = s_len // tm
    cur, prev, ba, row, full = _dn_pre_specs(s_len, tm, lambda i: nb - 1 - i)
    n_o = len(others)
    assert QKVW + sum(t.shape[1] for t in others) + 128 == C_BA + 128

    def kern(cur_ref, prev_ref, ba_ref, cw_ref, al_ref, dt_ref, dq_ref, dk_ref, dv_ref, dbb_ref, dgb_ref, *rest):
        o_refs = rest[:n_o]
        dproj_ref, dcw_ref, dal_ref, ddt_ref, *tails = rest[n_o:]
        i = pl.program_id(0)
        _zero_first([dcw_ref, dal_ref, ddt_ref] + tails)
        xs = _conv_taps(cur_ref, prev_ref, i == nb - 1)
        _, vjp = jax.vjp(_f_dn_pre, *xs, ba_ref[...], cw_ref[...], al_ref[...], dt_ref[...])
        *dxs, dba, dcw, dal, ddt = vjp((dq_ref[...], dk_ref[...], dv_ref[...], dbb_ref[...], dgb_ref[...]))
        total = dxs[CONV - 1]
        for j, t in enumerate(tails):
            n = CONV - 1 - j
            total = total + _rows_up(dxs[j], n, t[...])
            t[...] = dxs[j][0:HALO, :]
        dproj_ref[...] = jnp.concatenate(
            [total.astype(BF16)] + [r[...] for r in o_refs] + [dba.astype(BF16), jnp.zeros((tm, PW - C_BA - 128), BF16)],
            axis=1)
        dcw_ref[...] += dcw
        dal_ref[...] += dal
        ddt_ref[...] += ddt

    o_specs = [pl.BlockSpec((tm, t.shape[1]), lambda i: (nb - 1 - i, 0)) for t in others]
    return _pc(kern, "dn_pre_bwd", (nb,), [cur, prev, ba] + full + [row] * 5 + o_specs,
               [pl.BlockSpec((tm, PW), lambda i: (nb - 1 - i, 0))] + full,
               [SDS((s_len, PW), BF16), SDS((CONV, QKVW), F32), SDS((1, DN_H), F32), SDS((1, DN_H), F32)],
               scratch=[pltpu.VMEM((HALO, QKVW), F32)] * (CONV - 1))(proj, proj, proj, conv_w, alog, dtb, *cots, *others)


def _w_in_to_padded(w_sh):
    tr = 256

    def kern(w_ref, o_ref):
        full = jnp.concatenate([w_ref[s] for s in range(N_CHIPS)], axis=1)
        pieces = [full[:, o0:o0 + w] for o0, w, _ in sorted(_ORIG_PIECES, key=lambda t: t[2])]
        o_ref[...] = jnp.concatenate(pieces + [jnp.zeros((tr, PW - D_IN), w_ref.dtype)], axis=1)

    return _pc(kern, "w_in_to_padded", (D // tr,), [pl.BlockSpec((N_CHIPS, tr, D_IN // N_CHIPS), lambda i: (0, i, 0))],
               pl.BlockSpec((tr, PW), lambda i: (i, 0)), SDS((D, PW), w_sh.dtype))(w_sh)


def _padded_to_w_in(g):
    tr = 256
    csh = D_IN // N_CHIPS

    def kern(g_ref, o_ref):
        x = g_ref[...]
        full = jnp.concatenate([x[:, p0:p0 + w] for _, w, p0 in _ORIG_PIECES], axis=1)
        for s in range(N_CHIPS):
            o_ref[s] = full[:, s * csh:(s + 1) * csh]

    return _pc(kern, "padded_to_w_in", (D // tr,), [pl.BlockSpec((tr, PW), lambda i: (i, 0))],
               pl.BlockSpec((N_CHIPS, tr, csh), lambda i: (0, i, 0)), SDS((N_CHIPS, D, csh), g.dtype))(g)


def _pad_w_in(w_in):
    pieces = [w_in[:, o0:o0 + w] for o0, w, _ in sorted(_ORIG_PIECES, key=lambda t: t[2])]
    pieces.append(jnp.zeros((w_in.shape[0], PW - D_IN), w_in.dtype))
    return jnp.concatenate(pieces, axis=1)


def _unpad_w_in(g):
    return jnp.concatenate([g[:, p0:p0 + w] for _, w, p0 in _ORIG_PIECES], axis=1)


def _local_step(x, target, wts):
    s_len = x.shape[0]
    tm = min(256, s_len)
    w_in_p = wts["w_in_p"]
    attn_gain = wts["attn_norm"]
    ffn_gain = wts["ffn_norm"]
    conv_w = wts["dn_conv"]
    alog, dtb, out_gain = wts["dn_a_log"], wts["dn_dt_bias"], wts["dn_out_norm"]
    qg, kg = wts["swa_q_norm"], wts["swa_k_norm"]
    sinks = wts["swa_sinks"].reshape(SWA_KV, 1, SWA_G)

    h, proj = _in_proj(x, attn_gain, w_in_p)
    q_dn, k_dn, v_dn, bb, gb = _dn_pre_fwd(proj, conv_w, alog, dtb)
    o_dn, s_all, t_all = _dn_chunks_fwd(q_dn, k_dn, v_dn, gb, bb)
    post_ins = [_whole(o_dn), (proj, DNW, C_Z // DNW)]
    (y_dn,) = _rows(lambda r, f: ([_f_dn_post(r[0], r[1], f[0])], []), "dn_post_fwd", s_len, tm, post_ins,
                    [out_gain], [(DNW, BF16)])

    bias = _bias_expand(wts["rel_bias"].T).reshape(SWA_H, BLK, 2 * BLK)
    y_swa = _swa_fwd(proj, bias, qg, kg, sinks)

    wts = {**wts, **wts["late"](y_swa)}
    p_a, p_b, merged = _branch_merge(y_dn, y_swa, wts["wa"], wts["wb"], proj)
    x1, h2 = _out_proj(merged, wts["w_out"], x, ffn_gain)
    gt, up, act = _ffn_up(h2, wts["wg"], wts["wu"])
    dy, dy_b, loss = _ffn_down_loss(act, wts["wd"], x1, target)

    grads = {}
    d_gt, d_up = _ffn_dact(dy_b, wts["wd"], gt, up)
    (grads["w_down"],) = _gw_ffn([act], dy_b, "gw_down")
    grads["w_gate"], grads["w_up"] = _gw_ffn([d_gt, d_up], h2, "gw_gate_up")
    token = wts["send_ffn"](grads)
    dx1, dx1_b, grads["ffn_norm"] = _ffn_dh2(d_gt, d_up, wts["wg"], wts["wu"], x1, dy,
                                             ffn_gain + token[0:1, 0:1])
    grads["w_out"] = _mm(merged, dx1_b, "tn", BF16, 512, 512, "gw_out")
    d_pa, d_pb, d_gr = _merge_bwd(dx1_b, wts["w_out"], p_a, p_b, proj)
    d_ydn, d_yswa = _d_branch(d_pa, d_pb, wts["wa"], wts["wb"])
    grads["w_branch_dn"], grads["w_branch_swa"] = _gw_branch(y_dn, y_swa, d_pa, d_pb)
    token = wts["send_early"](grads)
    qg_t = qg + token[0:1, 0:1]
    out_gain_t = out_gain + token[0:1, 0:1]

    d_sq, d_sk, d_sv, d_bias, grads["swa_q_norm"], grads["swa_k_norm"], d_sinks = _swa_bwd(
        proj, bias, qg_t, kg, sinks, d_yswa)
    grads["swa_sinks"] = d_sinks.reshape(1, SWA_H)
    grads["rel_bias"] = _bias_reduce(d_bias.reshape(SWA_H, BLK * 2 * BLK)).T

    def post_bwd(r, f):
        _, vjp = jax.vjp(_f_dn_post, r[0], r[1], f[0])
        d_o, d_z, d_gain = vjp(r[2])
        return [d_o, d_z], [d_gain]

    d_o, d_z, grads["dn_out_norm"] = _rows(post_bwd, "dn_post_bwd", s_len, tm, post_ins + [_whole(d_ydn)], [out_gain_t],
                                           [(DNW, F32), (DNW, BF16)], [(1, DH)])
    d_q, d_k, d_v, d_gb, d_bb = _dn_chunks_bwd(q_dn, k_dn, v_dn, gb, bb, s_all, t_all, d_o)

    d_proj, grads["dn_conv"], grads["dn_a_log"], grads["dn_dt_bias"] = _dn_pre_bwd(
        proj, conv_w, alog, dtb, (d_q, d_k, d_v, d_bb, d_gb), (d_z, d_gr, d_sq, d_sk, d_sv))
    grads["w_in_p"] = _mm(h, d_proj, "tn", BF16, 512, 1024, "gw_in")
    token = wts["send_in"](grads["w_in_p"])
    grad_x, grads["attn_norm"] = _dh_rms(d_proj, w_in_p, x, dx1, attn_gain + token[0:1, 0:1])
    return loss, grad_x, grads


_HBM = pl.BlockSpec(memory_space=pl.ANY)


def _place():
    return lax.axis_index("x"), lax.axis_index("y"), lax.axis_index("c")


def _other_chips(x, y):
    return [(1 - x, y), (x, 1 - y), (1 - x, 1 - y)]


def _rcopy(src, dst, send_sems, recv_sems, k, to):
    return pltpu.make_async_remote_copy(src_ref=src, dst_ref=dst, send_sem=send_sems.at[k], recv_sem=recv_sems.at[k],
                                        device_id=to, device_id_type=MESH)


def _comm_call(body, name, ins, out_shapes, n_remote, landing=0):
    first = len(ins) - landing
    return pl.pallas_call(
        body, name=name, in_specs=[_HBM] * len(ins), out_specs=[_HBM] * len(out_shapes), out_shape=out_shapes,
        scratch_shapes=[pltpu.SemaphoreType.DMA((n_remote,)), pltpu.SemaphoreType.DMA((n_remote,))],
        input_output_aliases={first + i: i for i in range(landing)},
        compiler_params=_cparams(has_side_effects=True),
    )(*ins)


def _own_slot(blocks, chip):
    return [lax.dynamic_update_slice(lax.empty((N_CHIPS,) + b.shape, b.dtype), b[None], (chip, 0, 0)) for b in blocks]


def _gather_weights(ws, chip):
    n = len(ws)
    halves = [w.shape[0] // 2 for w in ws]

    def body(*refs):
        w_refs, o_refs = refs[:n], refs[2 * n:3 * n]
        send_sems, recv_sems = refs[3 * n:]
        x, y, c = _place()
        s = 2 * x + y
        sib = (x, y, 1 - c)
        chips = _other_chips(x, y)

        def rows(i, half):
            return pl.ds(half * halves[i], halves[i])

        first = []
        for j, (cx, cy) in enumerate(chips):
            for i in range(n):
                cp = _rcopy(w_refs[i].at[rows(i, c), :], o_refs[i].at[s, rows(i, c), :], send_sems, recv_sems,
                            j * n + i, (cx, cy, c))
                cp.start()
                first.append(cp)
        passed = []
        for j, (cx, cy) in enumerate(chips):
            sj = 2 * cx + cy
            for i in range(n):
                blk = o_refs[i].at[sj, rows(i, c), :]
                _rcopy(blk, blk, send_sems, recv_sems, j * n + i, (cx, cy, c)).wait_recv()
                cp = _rcopy(blk, blk, send_sems, recv_sems, (3 + j) * n + i, sib)
                cp.start()
                passed.append(cp)
        for j, (cx, cy) in enumerate(chips):
            sj = 2 * cx + cy
            for i in range(n):
                blk = o_refs[i].at[sj, rows(i, 1 - c), :]
                _rcopy(blk, blk, send_sems, recv_sems, (3 + j) * n + i, sib).wait_recv()
        for cp in first + passed:
            cp.wait_send()

    return _comm_call(body, "gather_weights", list(ws) + _own_slot(ws, chip),
                      [SDS((N_CHIPS,) + w.shape, w.dtype) for w in ws], 6 * n, landing=n)


_HBM_ONLY = pl.BlockSpec(memory_space=pltpu.HBM)
_SEM = pl.BlockSpec(memory_space=pltpu.SEMAPHORE)
_DATAFLOW = pltpu.SideEffectType.DATAFLOW_SIDE_EFFECTING


def _in_hbm(a):
    return pltpu.with_memory_space_constraint(a, pltpu.HBM)


def _gather_windows(blocks):
    halves = [b.shape[0] // 2 for b in blocks]

    def src_at(ref, i, c, sj):
        return ref.at[pl.ds(c * halves[i], halves[i]), :]

    def dst_at(ref, i, c, s_from):
        return ref.at[s_from, pl.ds(c * halves[i], halves[i]), :]

    return src_at, dst_at


def _exchange_windows():
    return (lambda ref, i, c, sj: ref.at[sj]), (lambda ref, i, c, s_from: ref.at[s_from])


def _swap_windows(gs):
    halves = [g.shape[1] // 2 for g in gs]
    return ((lambda ref, i, c, tag: ref.at[:, pl.ds((1 - c) * halves[i], halves[i]), :]),
            (lambda ref, i, c, slot: ref))


def _chip_peers(x, y, c):
    return [(2 * cx + cy, (cx, cy, c), 2 * x + y, 2 * cx + cy) for cx, cy in _other_chips(x, y)]


def _sibling_peer(x, y, c):
    return [(0, (x, y, 1 - c), 0, 0)]


def _split_start(name, ws, lands, dep, windows, peers=_chip_peers, n_peers=3):
    n = len(ws)
    src_at, dst_at = windows

    def body(*refs):
        w_refs, l_refs = refs[:n], refs[n:2 * n]
        send_sems, recv_sems = refs[2 * n + 1], refs[2 * n + 2]
        token = refs[-1]
        x, y, c = _place()
        for j, (tag, dev, there, _) in enumerate(peers(x, y, c)):
            for i in range(n):
                _rcopy(src_at(w_refs[i], i, c, tag), dst_at(l_refs[i], i, c, there), send_sems, recv_sems,
                       j * n + i, dev).start()
        token[...] = jnp.zeros_like(token)

    outs = pl.pallas_call(
        body, name=name,
        out_shape=(pltpu.SemaphoreType.DMA((n_peers * n,)), pltpu.SemaphoreType.DMA((n_peers * n,)),
                   *[pltpu.HBM(w.shape, w.dtype) for w in ws], *[pltpu.HBM(t.shape, t.dtype) for t in lands],
                   SDS((8, 128), F32)),
        in_specs=[_HBM_ONLY] * (2 * n) + [pl.BlockSpec(memory_space=pl.ANY)],
        out_specs=(_SEM, _SEM, *[_HBM_ONLY] * (2 * n), pl.BlockSpec(memory_space=pltpu.VMEM)),
        input_output_aliases={i: 2 + i for i in range(2 * n)},
        compiler_params=pltpu.CompilerParams(has_side_effects=_DATAFLOW),
    )(*[_in_hbm(w) for w in ws], *[_in_hbm(t) for t in lands], dep)
    return outs[0], outs[1], outs[2:2 + n], outs[2 + n:2 + 2 * n], outs[-1]


def _split_wait(name, w_thru, l_thru, send_sems, recv_sems, after, windows, peers=_chip_peers, with_sources=False):
    n = len(w_thru)
    src_at, dst_at = windows

    def body(*refs):
        w_refs, l_refs = refs[:n], refs[n:2 * n]
        send_sems, recv_sems = refs[2 * n], refs[2 * n + 1]
        x, y, c = _place()
        for j, (tag, dev, _, here) in enumerate(peers(x, y, c)):
            for i in range(n):
                cp = _rcopy(src_at(w_refs[i], i, c, tag), dst_at(l_refs[i], i, c, here), send_sems, recv_sems,
                            j * n + i, dev)
                cp.wait_send()
                cp.wait_recv()

    outs = pl.pallas_call(
        body, name=name,
        out_shape=[pltpu.HBM(w.shape, w.dtype) for w in w_thru] + [pltpu.HBM(t.shape, t.dtype) for t in l_thru],
        in_specs=[_HBM_ONLY] * (2 * n) + [_SEM, _SEM, pl.BlockSpec(memory_space=pl.ANY)],
        out_specs=[_HBM_ONLY] * (2 * n),
        input_output_aliases={i: i for i in range(2 * n)},
        compiler_params=pltpu.CompilerParams(has_side_effects=_DATAFLOW),
    )(*w_thru, *l_thru, send_sems, recv_sems, after)
    return (outs[:n], outs[n:]) if with_sources else outs[n:]


def _sibling_fill(lands):
    n = len(lands)
    halves = [t.shape[1] // 2 for t in lands]

    def body(*refs):
        o_refs = refs[n:2 * n]
        send_sems, recv_sems = refs[2 * n:]
        x, y, c = _place()
        sib = (x, y, 1 - c)
        chips = _other_chips(x, y)
        sent = []
        for j, (cx, cy) in enumerate(chips):
            for i in range(n):
                blk = o_refs[i].at[2 * cx + cy, pl.ds(c * halves[i], halves[i]), :]
                cp = _rcopy(blk, blk, send_sems, recv_sems, j * n + i, sib)
                cp.start()
                sent.append(cp)
        for j, (cx, cy) in enumerate(chips):
            for i in range(n):
                blk = o_refs[i].at[2 * cx + cy, pl.ds((1 - c) * halves[i], halves[i]), :]
                _rcopy(blk, blk, send_sems, recv_sems, j * n + i, sib).wait_recv()
        for cp in sent:
            cp.wait_send()

    return _comm_call(body, "sibling_fill", list(lands), [SDS(t.shape, t.dtype) for t in lands], 3 * n, landing=n)


def _swap_halves(gs, name):
    n = len(gs)
    halves = [g.shape[1] // 2 for g in gs]

    def body(*refs):
        g_refs, o_refs = refs[:n], refs[n:2 * n]
        send_sems, recv_sems = refs[2 * n:]
        x, y, c = _place()
        cps = [_rcopy(g_refs[i].at[:, pl.ds((1 - c) * halves[i], halves[i]), :], o_refs[i], send_sems, recv_sems, i,
                      (x, y, 1 - c)) for i in range(n)]
        for cp in cps:
            cp.start()
        for cp in cps:
            cp.wait()

    return _comm_call(body, name, gs, [SDS((N_CHIPS, h, g.shape[2]), g.dtype) for g, h in zip(gs, halves)], n)


def _swap_reduced(rs, name):
    n = len(rs)

    def body(*refs):
        r_refs, o_refs = refs[:n], refs[n:2 * n]
        send_sems, recv_sems = refs[2 * n:]
        x, y, c = _place()
        cps = [_rcopy(r_refs[i], o_refs[i], send_sems, recv_sems, i, (x, y, 1 - c)) for i in range(n)]
        for cp in cps:
            cp.start()
        for cp in cps:
            cp.wait()

    return _comm_call(body, name, rs, [SDS(r.shape, r.dtype) for r in rs], n)


def _all_sum_small(vec, name):
    n_dev = 8
    flips = [(bx, by, bc) for bx in (0, 1) for by in (0, 1) for bc in (0, 1)][1:]

    def body(v_ref, out_ref, gath, send_sems, recv_sems):
        x, y, c = _place()
        me = 4 * x + 2 * y + c
        gath[me] = v_ref[...]
        sent = []
        for k, (bx, by, bc) in enumerate(flips):
            peer = (x ^ bx, y ^ by, c ^ bc)
            cp = _rcopy(v_ref, gath.at[me], send_sems, recv_sems, k, peer)
            cp.start()
            sent.append(cp)
        for k, (bx, by, bc) in enumerate(flips):
            peer = (x ^ bx, y ^ by, c ^ bc)
            _rcopy(v_ref, gath.at[4 * peer[0] + 2 * peer[1] + peer[2]], send_sems, recv_sems, k, peer).wait_recv()
        for cp in sent:
            cp.wait_send()
        acc = gath[0]
        for d in range(1, n_dev):
            acc = acc + gath[d]
        out_ref[...] = acc

    vm = pl.BlockSpec(memory_space=pltpu.VMEM)
    return pl.pallas_call(
        body, name=name, in_specs=[vm], out_specs=vm, out_shape=SDS(vec.shape, F32),
        scratch_shapes=[pltpu.VMEM((n_dev,) + vec.shape, F32), pltpu.SemaphoreType.DMA((7,)),
                        pltpu.SemaphoreType.DMA((7,))],
        compiler_params=_cparams(has_side_effects=True),
    )(vec)


def _pack_small(vals, extra=None):
    parts = [vals[n].reshape(-1).astype(F32) for n, _ in _SMALL]
    parts.append(jnp.zeros((1,), F32) if extra is None else extra.reshape(1).astype(F32))
    flat = jnp.concatenate(parts)
    flat = jnp.concatenate([flat, jnp.zeros((_SMALL_ROWS * 128 - flat.shape[0],), F32)])
    return flat.reshape(_SMALL_ROWS, 128)


def _unpack_small(packed, shapes):
    flat = packed.reshape(-1)
    return {n: flat[_SMALL_OFF[n][0]:_SMALL_OFF[n][0] + _SMALL_OFF[n][1]].reshape(shapes[n]) for n, _ in _SMALL}


def _pair_sum(gs, gots, core, name):
    n = len(gs)

    def kern(c_ref, *refs):
        for i in range(n):
            refs[2 * n + i][...] = (refs[i][...].astype(F32) + refs[n + i][...].astype(F32)).astype(BF16)

    in_specs = [pl.BlockSpec((1, t.shape[1], t.shape[2]), lambda s, c_ref: (s, c_ref[0], 0)) for t in gots]
    in_specs += [pl.BlockSpec((1, t.shape[1], t.shape[2]), lambda s, c_ref: (s, 0, 0)) for t in gots]
    out_specs = [pl.BlockSpec((1, t.shape[1], t.shape[2]), lambda s, c_ref: (s, 0, 0)) for t in gots]
    return pl.pallas_call(
        kern, name=name,
        grid_spec=pltpu.PrefetchScalarGridSpec(num_scalar_prefetch=1, grid=(N_CHIPS,), in_specs=in_specs,
                                               out_specs=out_specs),
        out_shape=[SDS(t.shape, BF16) for t in gots],
        compiler_params=_cparams(dimension_semantics=("arbitrary",)),
    )(core.reshape(1).astype(jnp.int32), *gs, *gots)


def _chip_sum(qs, name):
    n = len(qs)

    def kern(*refs):
        for i in range(n):
            acc = refs[i][0].astype(F32)
            for s in range(1, N_CHIPS):
                acc = acc + refs[i][s].astype(F32)
            refs[n + i][...] = acc

    in_specs = [pl.BlockSpec((N_CHIPS, q.shape[1] // 2, q.shape[2]), lambda j: (0, j, 0)) for q in qs]
    out_specs = [pl.BlockSpec((q.shape[1] // 2, q.shape[2]), lambda j: (j, 0)) for q in qs]
    return _pc(kern, name, (2,), in_specs, out_specs, [SDS(q.shape[1:], F32) for q in qs])(*qs)


def _adam_math(w_, g_, m_, v_):
    m_ = ADAM_B1 * m_ + (1.0 - ADAM_B1) * g_
    v_ = ADAM_B2 * v_ + (1.0 - ADAM_B2) * jnp.square(g_)
    m_hat = m_ / (1.0 - ADAM_B1 ** ADAM_STEP)
    v_hat = v_ / (1.0 - ADAM_B2 ** ADAM_STEP)
    return -ADAM_LR * (m_hat / (jnp.sqrt(v_hat) + ADAM_EPS) + ADAM_WD * w_), m_, v_


def _adamw(w, g, m, v, name):
    rows, cols = w.shape
    tr = rows
    for cand in (256, 128, 64, 32, 16, 8):
        if rows % cand == 0 and rows > cand:
            tr = cand
            break

    def kern(w_ref, g_ref, m_ref, v_ref, d_ref, nm_ref, nv_ref):
        d_ref[...], nm_ref[...], nv_ref[...] = _adam_math(w_ref[...], g_ref[...], m_ref[...], v_ref[...])

    spec = pl.BlockSpec((tr, cols), lambda i: (i, 0))
    return _pc(kern, name, (rows // tr,), [spec] * 4, [spec] * 3, [SDS(w.shape, F32)] * 3)(w, g, m, v)


def _adamw_rows1(w, g, m, v, name):
    rows, _, cols = w.shape
    tr = next(t for t in (203, 174, 128, 64, 42, 32, 29, 16, 8, 7, 6, 4, 3, 2, 1) if rows % t == 0)

    def kern(w_ref, g_ref, m_ref, v_ref, go_ref, d_ref, nm_ref, nv_ref):
        g_ = g_ref[...]
        go_ref[...] = g_
        d_ref[...], nm_ref[...], nv_ref[...] = _adam_math(w_ref[...], g_, m_ref[...], v_ref[...])

    spec = pl.BlockSpec((tr, 1, cols), lambda i: (i, 0, 0))
    return _pc(kern, name, (rows // tr,), [spec] * 4, [spec] * 4, [SDS(w.shape, F32)] * 4)(w, g, m, v)


def _adamw_big(w, mine, theirs, m, v, core, name):
    _, rows, cols = w.shape
    half = rows // 2
    tr = next(t for t in (256, 176, 128, 64, 32, 16, 8) if half % t == 0)
    nbh = half // tr

    def kern(c_ref, w_ref, a_ref, b_ref, m_ref, v_ref, g_ref, d_ref, nm_ref, nv_ref):
        g_ = jnp.where(pl.program_id(0) // nbh == c_ref[0], a_ref[...], b_ref[...])
        g_ref[0] = g_
        d_ref[0], nm_ref[0], nv_ref[0] = _adam_math(w_ref[0], g_, m_ref[0], v_ref[0])

    full = pl.BlockSpec((1, tr, cols), lambda i, c_ref: (0, i, 0))
    part = pl.BlockSpec((tr, cols), lambda i, c_ref: (i % nbh, 0))
    return pl.pallas_call(
        kern, name=name,
        grid_spec=pltpu.PrefetchScalarGridSpec(num_scalar_prefetch=1, grid=(rows // tr,),
                                               in_specs=[full, part, part, full, full], out_specs=[full] * 4),
        out_shape=[SDS(w.shape, F32)] * 4,
        compiler_params=_cparams(dimension_semantics=("arbitrary",)),
    )(core.reshape(1).astype(jnp.int32), w, mine, theirs, m, v)


_WEIGHT_NAMES = ("attn_norm", "w_in", "dn_conv", "dn_a_log", "dn_dt_bias", "dn_out_norm", "swa_q_norm", "swa_k_norm",
                 "swa_sinks", "rel_bias", "w_branch_dn", "w_branch_swa", "w_out", "ffn_norm", "w_gate", "w_up",
                 "w_down")
_CONV_SH = QKVW // N_CHIPS


def kernel(x, attn_norm, w_in, dn_conv, dn_a_log, dn_dt_bias, dn_out_norm, swa_q_norm, swa_k_norm, swa_sinks, rel_bias, w_branch_dn, w_branch_swa, w_out, ffn_norm, w_gate, w_up, w_down, loss_target, m_attn_norm, m_w_in, m_dn_conv, m_dn_a_log, m_dn_dt_bias, m_dn_out_norm, m_swa_q_norm, m_swa_k_norm, m_swa_sinks, m_rel_bias, m_w_branch_dn, m_w_branch_swa, m_w_out, m_ffn_norm, m_w_gate, m_w_up, m_w_down, v_attn_norm, v_w_in, v_dn_conv, v_dn_a_log, v_dn_dt_bias, v_dn_out_norm, v_swa_q_norm, v_swa_k_norm, v_swa_sinks, v_rel_bias, v_w_branch_dn, v_w_branch_swa, v_w_out, v_ffn_norm, v_w_gate, v_w_up, v_w_down):
    w = dict(attn_norm=attn_norm, w_in=w_in, dn_conv=dn_conv, dn_a_log=dn_a_log, dn_dt_bias=dn_dt_bias,
             dn_out_norm=dn_out_norm, swa_q_norm=swa_q_norm, swa_k_norm=swa_k_norm, swa_sinks=swa_sinks,
             rel_bias=rel_bias, w_branch_dn=w_branch_dn, w_branch_swa=w_branch_swa, w_out=w_out, ffn_norm=ffn_norm,
             w_gate=w_gate, w_up=w_up, w_down=w_down)
    m = dict(attn_norm=m_attn_norm, w_in=m_w_in, dn_conv=m_dn_conv, dn_a_log=m_dn_a_log, dn_dt_bias=m_dn_dt_bias,
             dn_out_norm=m_dn_out_norm, swa_q_norm=m_swa_q_norm, swa_k_norm=m_swa_k_norm, swa_sinks=m_swa_sinks,
             rel_bias=m_rel_bias, w_branch_dn=m_w_branch_dn, w_branch_swa=m_w_branch_swa, w_out=m_w_out,
             ffn_norm=m_ffn_norm, w_gate=m_w_gate, w_up=m_w_up, w_down=m_w_down)
    v = dict(attn_norm=v_attn_norm, w_in=v_w_in, dn_conv=v_dn_conv, dn_a_log=v_dn_a_log, dn_dt_bias=v_dn_dt_bias,
             dn_out_norm=v_dn_out_norm, swa_q_norm=v_swa_q_norm, swa_k_norm=v_swa_k_norm, swa_sinks=v_swa_sinks,
             rel_bias=v_rel_bias, w_branch_dn=v_w_branch_dn, w_branch_swa=v_w_branch_swa, w_out=v_w_out,
             ffn_norm=v_ffn_norm, w_gate=v_w_gate, w_up=v_w_up, w_down=v_w_down)
    shapes = {n: w[n].shape for n in _WEIGHT_NAMES}

    def two_d(a):
        return a.reshape(a.shape[-2], a.shape[-1]) if a.ndim == 3 else a

    core = lax.axis_index("c")
    chip = 2 * lax.axis_index("x") + lax.axis_index("y")
    small_shapes = {n: two_d(w[n]).shape for n, _ in _SMALL}
    small_shapes["dn_conv"] = (CONV, QKVW)

    conv_loc = two_d(w["dn_conv"])
    conv_part = lax.dynamic_update_slice(jnp.zeros((CONV, QKVW), F32), jnp.where(core == 0, conv_loc, 0.0),
                                         (0, chip * _CONV_SH))
    conv_full = _all_sum_small(conv_part.reshape(CONV * QKVW // 128, 128), "gather_conv").reshape(CONV, QKVW)

    flipped = ("w_gate", "w_up")

    def natural(a, n):
        return a.transpose(0, 2, 1) if n in flipped else a

    w_bf = [two_d(natural(w[n], n).astype(BF16)) for n in _BIG_NAMES]
    (w_in_g,) = _gather_weights(w_bf[:1], chip)
    windows = _gather_windows(w_bf[1:])
    after_sync = w_in_g[0, :8, :128].astype(F32) + conv_full[0:1, :128]
    send_sems, recv_sems, w_thru, l_thru, token = _split_start(
        "gather_start", w_bf[1:], _own_slot(w_bf[1:], chip), after_sync, windows)

    def late(after):
        lands = _split_wait("gather_wait", w_thru, l_thru, send_sems, recv_sems, after, windows)
        g = dict(zip(_BIG_NAMES[1:], _sibling_fill(lands)))
        return dict(wa=g["w_branch_dn"], wb=g["w_branch_swa"], w_out=g["w_out"].reshape(D, D), wg=g["w_gate"],
                    wu=g["w_up"], wd=g["w_down"])

    wts = dict(w_in_p=_w_in_to_padded(w_in_g), dn_conv=conv_full, late=late)
    for n, _ in _SMALL[:-1]:
        wts[n] = two_d(w[n])
    wts["attn_norm"] = wts["attn_norm"] + token[0:1, 0:1]

    early = {}

    ffn = {}

    def send_ffn(grads):
        gs = [grads["w_gate"], grads["w_up"], grads["w_down"]]
        lands = [lax.empty((N_CHIPS, g.shape[1] // 2, g.shape[2]), g.dtype) for g in gs]
        ffn["sems"], ffn["recv"], ffn["src"], ffn["land"], tok = _split_start(
            "swap_ffn_start", gs, lands, gs[0][0, :8, :128], _swap_windows(gs), _sibling_peer, 1)
        return tok

    def send_early(grads):
        small = [grads["w_branch_dn"], grads["w_branch_swa"], grads["w_out"].reshape(N_CHIPS, CSH, D)]
        big = [grads["w_gate"], grads["w_up"], grads["w_down"]]
        big, got_big = _split_wait("swap_ffn_wait", ffn["src"], ffn["land"], ffn["sems"], ffn["recv"], small[0],
                                   _swap_windows(big), _sibling_peer, with_sources=True)
        gots = list(_swap_halves(small, "swap_halves_early")) + list(got_big)
        parts = _pair_sum(small + list(big), gots, core, "pair_sum_early")
        own = [lax.dynamic_index_in_dim(p, chip, axis=0, keepdims=False) for p in parts]
        early["sems"], early["recv"], early["src"], early["land"], tok = _split_start(
            "exchange_start", parts, _own_slot(own, chip), parts[0][0, :8, :128], _exchange_windows())
        return tok

    last = {}

    def send_in(g_in_p):
        g_in = [_padded_to_w_in(g_in_p)]
        parts = _pair_sum(g_in, _swap_halves(g_in, "swap_halves_in"), core, "pair_sum_in")
        own = [lax.dynamic_index_in_dim(p, chip, axis=0, keepdims=False) for p in parts]
        last["sems"], last["recv"], last["src"], last["land"], tok = _split_start(
            "exchange_in_start", parts, _own_slot(own, chip), parts[0][0, :8, :128], _exchange_windows())
        return tok

    wts["send_ffn"] = send_ffn
    wts["send_early"] = send_early
    wts["send_in"] = send_in
    loss_sum, grad_x, grads = _local_step(x[0], loss_target[0], wts)

    small_sum = _all_sum_small(_pack_small(grads, loss_sum), "all_sum_small")
    loss = small_sum.reshape(-1)[_LOSS_OFF]
    g_small = _unpack_small(small_sum, small_shapes)

    q_early = _split_wait("exchange_wait", early["src"], early["land"], early["sems"], early["recv"], small_sum,
                          _exchange_windows())
    red_early = _chip_sum(list(q_early), "chip_sum_early")
    their_early = _swap_reduced(red_early, "swap_reduced_early")
    g_out, d_out, m_out, v_out = {}, {}, {}, {}
    for n, mine, other in zip(_BIG_NAMES[1:], red_early, their_early):
        res = _adamw_big(natural(w[n], n), mine, other, natural(m[n], n), natural(v[n], n), core, "adamw_" + n)
        g_out[n], d_out[n], m_out[n], v_out[n] = (natural(t, n) for t in res)

    q_in = _split_wait("exchange_in_wait", last["src"], last["land"], last["sems"], last["recv"],
                       d_out[_BIG_NAMES[-1]], _exchange_windows())
    reduced = _chip_sum(list(q_in), "chip_sum_in")
    theirs = _swap_reduced(reduced, "swap_reduced_in")

    def rows1(a):
        return a.transpose(2, 0, 1)

    def unrows1(a):
        return a.transpose(1, 2, 0)

    g_in_blk = jnp.concatenate([jnp.where(core == 0, reduced[0], theirs[0]),
                                jnp.where(core == 0, theirs[0], reduced[0])], axis=0)
    g_in_r = rows1(g_in_blk[None])
    res = _adamw_rows1(rows1(w["w_in"]), g_in_r, rows1(m["w_in"]), rows1(v["w_in"]), "adamw_w_in")
    g_out["w_in"], d_out["w_in"], m_out["w_in"], v_out["w_in"] = (unrows1(t) for t in res)
    g_conv = lax.dynamic_slice(g_small["dn_conv"], (0, chip * _CONV_SH), (CONV, _CONV_SH))
    g_out["dn_conv"] = g_conv.reshape(shapes["dn_conv"])
    d_, m_, v_ = _adamw(conv_loc, g_conv, two_d(m["dn_conv"]), two_d(v["dn_conv"]), "adamw_dn_conv")
    d_out["dn_conv"], m_out["dn_conv"], v_out["dn_conv"] = (t.reshape(shapes["dn_conv"]) for t in (d_, m_, v_))

    def packed(src):
        vals = {n: src[n] for n, _ in _SMALL[:-1]}
        vals["dn_conv"] = jnp.zeros((CONV * QKVW,), F32)
        return _pack_small(vals)

    d_s, m_s, v_s = _adamw(packed(w), small_sum, packed(m), packed(v), "adamw_small")
    d_small, m_small, v_small = (_unpack_small(t, small_shapes) for t in (d_s, m_s, v_s))
    for n, _ in _SMALL[:-1]:
        g_out[n] = g_small[n].reshape(shapes[n])
        d_out[n], m_out[n], v_out[n] = (t[n].reshape(shapes[n]) for t in (d_small, m_small, v_small))

    return (loss, grad_x[None], *[g_out[n] for n in _WEIGHT_NAMES], *[d_out[n] for n in _WEIGHT_NAMES],
            *[m_out[n] for n in _WEIGHT_NAMES], *[v_out[n] for n in _WEIGHT_NAMES])
```

```python
import functools
import math

import numpy as np
import jax
import jax.numpy as jnp
from jax import lax
from jax.experimental import pallas as pl
from jax.experimental.pallas import tpu as pltpu

F32 = jnp.float32
BF16 = jnp.bfloat16
SDS = jax.ShapeDtypeStruct

D = 1024
DN_H = 4
DH = 128
DNW = DN_H * DH
QKVW = 3 * DNW
CONV = 4
CHUNK = 64
SWA_H = 8
SWA_KV = 2
SWA_G = SWA_H // SWA_KV
SWA_D = 64
SWAW = SWA_H * SWA_D
SWAKW = SWA_KV * SWA_D
BLK = 128
NBUCKET = 32
MAXDIST = 128
DFF = 2816
D_IN = QKVW + DNW + 2 * DN_H + SWAW + 2 * SWAKW + 2 * D
EPS = 1e-6
NEG = -1e30

ADAM_LR = 0.001
ADAM_B1 = 0.9
ADAM_B2 = 0.999
ADAM_EPS = 1e-08
ADAM_WD = 0.01
ADAM_STEP = 10

C_QKV, C_Z, C_GATE, C_SQ, C_SK, C_SV, C_BA = 0, 1536, 2048, 4096, 4608, 4736, 4864
PW = 5120
_ORIG_PIECES = (
    (0, QKVW, C_QKV),
    (QKVW, DNW, C_Z),
    (QKVW + DNW, 2 * DN_H, C_BA),
    (QKVW + DNW + 2 * DN_H, SWAW, C_SQ),
    (QKVW + DNW + 2 * DN_H + SWAW, SWAKW, C_SK),
    (QKVW + DNW + 2 * DN_H + SWAW + SWAKW, SWAKW, C_SV),
    (QKVW + DNW + 2 * DN_H + SWAW + 2 * SWAKW, 2 * D, C_GATE),
)

N_CHIPS = 4
FSH = DFF // N_CHIPS
CSH = D // N_CHIPS
VMEM_LIMIT = 48 * 1024 * 1024
MESH = pl.DeviceIdType.MESH

_BIG = (
    ("w_in", D, D_IN // N_CHIPS),
    ("w_branch_dn", DNW, CSH),
    ("w_branch_swa", SWAW, CSH),
    ("w_out", CSH, D),
    ("w_gate", FSH, D),
    ("w_up", FSH, D),
    ("w_down", FSH, D),
)
_BIG_NAMES = tuple(n for n, _, _ in _BIG)

_SMALL = (
    ("attn_norm", D), ("ffn_norm", D), ("dn_out_norm", DH), ("swa_q_norm", SWA_D), ("swa_k_norm", SWA_D),
    ("swa_sinks", SWA_H), ("dn_a_log", DN_H), ("dn_dt_bias", DN_H), ("rel_bias", NBUCKET * SWA_H),
    ("dn_conv", CONV * QKVW),
)
_SMALL_OFF = {}
_o = 0
for _n, _s in _SMALL:
    _SMALL_OFF[_n] = (_o, _s)
    _o += _s
_LOSS_OFF = _o
_SMALL_ROWS = -(-(_o + 1) // (8 * 128)) * 8


def _cparams(**kw):
    return pltpu.CompilerParams(vmem_limit_bytes=VMEM_LIMIT, **kw)


_DIMS = {
    "nn": (((1,), (0,)), ((), ())),
    "nt": (((1,), (1,)), ((), ())),
    "tn": (((0,), (0,)), ((), ())),
    "bnn": (((2,), (1,)), ((0,), (0,))),
    "bnt": (((2,), (2,)), ((0,), (0,))),
    "btn": (((1,), (1,)), ((0,), (0,))),
}


def _raw_dot(a, b, kind, exact):
    if exact:
        prec = lax.Precision.HIGH if exact == "x3" else lax.Precision.HIGHEST
        return lax.dot_general(a, b, _DIMS[kind], precision=prec, preferred_element_type=F32)
    return lax.dot_general(a.astype(BF16), b.astype(BF16), _DIMS[kind], preferred_element_type=F32)


@functools.partial(jax.custom_vjp, nondiff_argnums=(2, 3))
def _dot(a, b, kind, exact):
    return _raw_dot(a, b, kind, exact)


def _dot_fwd(a, b, kind, exact):
    return _raw_dot(a, b, kind, exact), (a, b)


def _dot_bwd(kind, exact, res, g):
    a, b = res
    pre = kind[:-2]
    nn, nt, tn = pre + "nn", pre + "nt", pre + "tn"
    if kind == nn:
        return _dot(g, b, nt, exact), _dot(a, g, tn, exact)
    if kind == nt:
        return _dot(g, b, nn, exact), _dot(g, a, tn, exact)
    return _dot(b, g, nt, exact), _dot(a, g, nn, exact)


_dot.defvjp(_dot_fwd, _dot_bwd)


def _silu(x):
    return x * jax.nn.sigmoid(x)


def _f_rms(x, gain):
    return x * lax.rsqrt(jnp.mean(x * x, axis=-1, keepdims=True) + EPS) * gain


def _f_dn_pre(xs0, xs1, xs2, xs3, ba, cw, alog, dtb):
    rows = xs0.shape[0]
    c = xs0 * cw[0:1] + xs1 * cw[1:2] + xs2 * cw[2:3] + xs3 * cw[3:4]
    qkv = _silu(c)
    qs, ks, bbs, gbs = [], [], [], []
    for h in range(DN_H):
        qh = qkv[:, h * DH:(h + 1) * DH]
        kh = qkv[:, DNW + h * DH:DNW + (h + 1) * DH]
        qs.append(qh * lax.rsqrt(jnp.sum(qh * qh, axis=-1, keepdims=True) + EPS) * (DH ** -0.5))
        ks.append(kh * lax.rsqrt(jnp.sum(kh * kh, axis=-1, keepdims=True) + EPS))
        beta = jax.nn.sigmoid(ba[:, h:h + 1])
        ar = ba[:, DN_H + h:DN_H + h + 1] + dtb[:, h:h + 1]
        softplus = jnp.maximum(ar, 0.0) + jnp.log1p(jnp.exp(-jnp.abs(ar)))
        g = -jnp.exp(alog[:, h:h + 1]) * softplus
        bbs.append(jnp.broadcast_to(beta, (rows, DH)))
        gbs.append(jnp.broadcast_to(g, (rows, DH)))
    return (jnp.concatenate(qs, axis=1), jnp.concatenate(ks, axis=1), qkv[:, 2 * DNW:],
            jnp.concatenate(bbs, axis=1), jnp.concatenate(gbs, axis=1))


def _f_dn_post(o, z, gain):
    ys = []
    for h in range(DN_H):
        oh = o[:, h * DH:(h + 1) * DH]
        zh = z[:, h * DH:(h + 1) * DH]
        ys.append(oh * lax.rsqrt(jnp.mean(oh * oh, axis=-1, keepdims=True) + EPS) * gain * _silu(zh))
    return jnp.concatenate(ys, axis=1)


def _f_merge(pa, pb, ga, gb):
    return jax.nn.sigmoid(ga) * pa + jax.nn.sigmoid(gb) * pb


@jax.custom_vjp
def _f_swiglu(g, u):
    return _silu(g) * u


def _f_swiglu_fwd(g, u):
    return _silu(g) * u, (g, u)


def _f_swiglu_bwd(res, d):
    g, u = res
    s = jax.nn.sigmoid(g)
    act = g * s
    return d * u * (s + act * (1.0 - s)), d * act


_f_swiglu.defvjp(_f_swiglu_fwd, _f_swiglu_bwd)


@jax.custom_vjp
def _unit_lower_inverse(a):
    c = a.shape[-1]
    eye = (lax.broadcasted_iota(jnp.int32, a.shape, 1) == lax.broadcasted_iota(jnp.int32, a.shape, 2)).astype(F32)
    p = -a
    t = eye + p
    for _ in range(max(c.bit_length() - 2, 0)):
        p = _raw_dot(p, p, "bnn", "x3")
        t = t + _raw_dot(t, p, "bnn", "x3")
    return t


def _unit_lower_inverse_fwd(a):
    t = _unit_lower_inverse(a)
    return t, t


def _unit_lower_inverse_bwd(t, g):
    return (-_raw_dot(_raw_dot(t, g, "btn", "x3"), t, "bnt", "x3"),)


_unit_lower_inverse.defvjp(_unit_lower_inverse_fwd, _unit_lower_inverse_bwd)


@jax.custom_vjp
def _known_inverse(a, t):
    return t


def _known_inverse_fwd(a, t):
    return t, t


def _known_inverse_bwd(t, g):
    return _unit_lower_inverse_bwd(t, g)[0], jnp.zeros_like(t)


_known_inverse.defvjp(_known_inverse_fwd, _known_inverse_bwd)


def _f_chunk(q, k, v, gb, bb, s, t_known=None, with_t=False):
    c = CHUNK
    nh = q.shape[0]
    ii = lax.broadcasted_iota(jnp.int32, (nh, c, c), 1)
    jj = lax.broadcasted_iota(jnp.int32, (nh, c, c), 2)
    incl = ii >= jj
    strict = ii > jj
    eye = (ii == jj).astype(F32)
    gcb = _dot(incl.astype(F32), gb, "bnn", "x3")
    lane0 = (lax.broadcasted_iota(jnp.int32, (nh, c, DH), 2) == 0).astype(F32)
    gcol = gcb[:, :, :c]
    grow = _dot(lane0, gcb, "bnt", "x3")
    decay = jnp.where(incl, jnp.exp(jnp.where(incl, gcol - grow, 0.0)), 0.0)
    kb = k * bb
    vb = v * bb
    a = jnp.where(strict, _dot(kb, k, "bnt", False) * decay, 0.0)
    t = _unit_lower_inverse(a) if t_known is None else _known_inverse(a, t_known)
    eg = jnp.exp(gcb)
    u = _dot(t, vb, "bnn", "x3")
    w = _dot(t, kb * eg, "bnn", "x3")
    qk = jnp.where(incl, _dot(q, k, "bnt", False) * decay, 0.0)
    qe = q * eg
    glast = gcb[:, c - 1:c, :]
    k_dec = k * jnp.exp(glast - gcb)
    e_last = jnp.exp(glast)
    outs = []
    for g in range(nh // DN_H):
        sl = slice(g * DN_H, (g + 1) * DN_H)
        v_new = u[sl] - _dot(w[sl], s, "bnn", False)
        outs.append(_dot(qe[sl], s, "bnn", False) + _dot(qk[sl], v_new, "bnn", False))
        s = s * e_last[sl] + _dot(k_dec[sl], v_new, "btn", False)
    o = jnp.concatenate(outs, axis=0)
    return (o, s, t) if with_t else (o, s)


def _f_swa(q8, kp, kc, vp, vc, bias8, qg, kg, sink, mask):
    kb = jnp.concatenate([kp, kc], axis=1)
    vb = jnp.concatenate([vp, vc], axis=1)
    kn = kb * lax.rsqrt(jnp.mean(kb * kb, axis=-1, keepdims=True) + EPS) * kg

    def rows(per_head):
        return jnp.stack([jnp.concatenate([per_head(kv, g) for g in range(SWA_G)], axis=0)
                          for kv in range(SWA_KV)], axis=0)

    qq = rows(lambda kv, g: q8[kv * SWA_G + g])
    qn = qq * lax.rsqrt(jnp.mean(qq * qq, axis=-1, keepdims=True) + EPS) * qg * (SWA_D ** -0.5)
    lg = _dot(qn, kn, "bnt", False) + rows(lambda kv, g: bias8[kv * SWA_G + g])
    lg = jnp.where(rows(lambda kv, g: mask), lg, NEG)
    sk = rows(lambda kv, g: jnp.broadcast_to(sink[kv][:, g:g + 1], (BLK, 1)))
    m = lax.stop_gradient(jnp.maximum(jnp.max(lg, axis=-1, keepdims=True), sk))
    p = jnp.exp(lg - m)
    den = jnp.sum(p, axis=-1, keepdims=True) + jnp.exp(sk - m)
    out = _dot(p * (1.0 / den), vb, "bnn", False)
    return jnp.stack([out[kv, g * BLK:(g + 1) * BLK] for kv in range(SWA_KV) for g in range(SWA_G)], axis=0)


def _bdot(a, b, kind="nn"):
    return lax.dot_general(a.astype(BF16), b.astype(BF16), _DIMS[kind], preferred_element_type=F32)


def _pc(kern, name, grid, in_specs, out_specs, out_shape, scratch=()):
    return pl.pallas_call(
        kern, name=name, grid=grid, in_specs=in_specs, out_specs=out_specs, out_shape=out_shape,
        scratch_shapes=list(scratch), compiler_params=_cparams(dimension_semantics=("arbitrary",) * len(grid)))


def _mm(a, b, kind, out_dtype, tm, tn, name):
    if kind == "tn":
        k, m = a.shape
    else:
        m, k = a.shape
    n = b.shape[0] if kind == "nt" else b.shape[1]
    tm, tn = min(tm, m), min(tn, n)
    assert m % tm == 0 and n % tn == 0, (name, a.shape, b.shape, tm, tn)

    def kern(a_ref, b_ref, o_ref):
        o_ref[...] = _bdot(a_ref[...], b_ref[...], kind).astype(o_ref.dtype)

    a_spec = pl.BlockSpec((k, tm), lambda i, j: (0, i)) if kind == "tn" else pl.BlockSpec((tm, k), lambda i, j: (i, 0))
    b_spec = pl.BlockSpec((tn, k), lambda i, j: (j, 0)) if kind == "nt" else pl.BlockSpec((k, tn), lambda i, j: (0, j))
    return _pc(kern, name, (m // tm, n // tn), [a_spec, b_spec], pl.BlockSpec((tm, tn), lambda i, j: (i, j)),
               SDS((m, n), out_dtype))(a, b)


def _rows(body, name, m, tm, row_ins, full_ins, row_outs, acc_outs=()):
    n_r, n_f, n_o, n_a = len(row_ins), len(full_ins), len(row_outs), len(acc_outs)
    assert m % tm == 0

    def kern(*refs):
        r = refs[:n_r]
        f = refs[n_r:n_r + n_f]
        o = refs[n_r + n_f:n_r + n_f + n_o]
        acc = refs[n_r + n_f + n_o:]
        outs, sums = body([x[...] for x in r], [x[...] for x in f])
        for ref, val in zip(o, outs, strict=True):
            ref[...] = val.astype(ref.dtype)
        if n_a:
            @pl.when(pl.program_id(0) == 0)
            def _():
                for ref in acc:
                    ref[...] = jnp.zeros(ref.shape, F32)

            for ref, val in zip(acc, sums, strict=True):
                ref[...] += val

    in_specs = [pl.BlockSpec((tm, w), functools.partial(lambda i, cb: (i, cb), cb=cb)) for _, w, cb in row_ins]
    in_specs += [pl.BlockSpec(x.shape, lambda i: (0, 0)) for x in full_ins]
    out_specs = [pl.BlockSpec((tm, w), lambda i: (i, 0)) for w, _ in row_outs]
    out_specs += [pl.BlockSpec(s, lambda i: (0, 0)) for s in acc_outs]
    out_shape = [SDS((m, w), dt) for w, dt in row_outs]
    out_shape += [SDS(s, F32) for s in acc_outs]
    return _pc(kern, name, (m // tm,), in_specs, out_specs, out_shape)(*[x for x, _, _ in row_ins], *full_ins)


def _whole(x):
    return (x, x.shape[1], 0)


def _resident(shape):
    return pl.BlockSpec(shape, lambda i: (0,) * len(shape), pipeline_mode=pl.Buffered(1))


def _row_pieces(tm, piece):
    piece = min(piece, tm)
    return [slice(r, r + piece) for r in range(0, tm, piece)]


def _zero_first(refs):
    @pl.when(pl.program_id(0) == 0)
    def _():
        for ref in refs:
            ref[...] = jnp.zeros(ref.shape, F32)


GROUP = 4


def _heads(ref):
    return jnp.stack([ref[g * CHUNK:(g + 1) * CHUNK, h * DH:(h + 1) * DH]
                      for g in range(GROUP) for h in range(DN_H)], axis=0)


def _unheads(ref, val):
    for g in range(GROUP):
        for h in range(DN_H):
            ref[g * CHUNK:(g + 1) * CHUNK, h * DH:(h + 1) * DH] = val[g * DN_H + h]


def _dn_chunks_fwd(q, k, v, gb, bb):
    s_len = q.shape[0]
    ng = s_len // (GROUP * CHUNK)

    def kern(q_ref, k_ref, v_ref, g_ref, b_ref, o_ref, sall_ref, t_ref, state):
        _zero_first([state])
        s = state[...]
        sall_ref[0] = s
        o, s_new, t = _f_chunk(*[_heads(r) for r in (q_ref, k_ref, v_ref, g_ref, b_ref)], s, with_t=True)
        _unheads(o_ref, o)
        t_ref[0] = t
        state[...] = s_new

    blk = pl.BlockSpec((GROUP * CHUNK, DNW), lambda c: (c, 0))
    return _pc(kern, "dn_chunks_fwd", (ng,), [blk] * 5,
               [blk, pl.BlockSpec((1, DN_H, DH, DH), lambda c: (c, 0, 0, 0)),
                pl.BlockSpec((1, GROUP * DN_H, CHUNK, CHUNK), lambda c: (c, 0, 0, 0))],
               [SDS((s_len, DNW), F32), SDS((ng, DN_H, DH, DH), F32), SDS((ng, GROUP * DN_H, CHUNK, CHUNK), F32)],
               scratch=[pltpu.VMEM((DN_H, DH, DH), F32)])(q, k, v, gb, bb)


def _dn_chunks_bwd(q, k, v, gb, bb, s_all, t_all, d_o):
    s_len = q.shape[0]
    ng = s_len // (GROUP * CHUNK)

    def kern(q_ref, k_ref, v_ref, g_ref, b_ref, sall_ref, t_ref, do_ref, dq_ref, dk_ref, dv_ref, dg_ref, db_ref,
             dstate):
        _zero_first([dstate])
        fn = functools.partial(_f_chunk, t_known=t_ref[0])
        _, vjp = jax.vjp(fn, *[_heads(r) for r in (q_ref, k_ref, v_ref, g_ref, b_ref)], sall_ref[0])
        *d_ins, ds = vjp((_heads(do_ref), dstate[...]))
        for ref, val in zip((dq_ref, dk_ref, dv_ref, dg_ref, db_ref), d_ins, strict=True):
            _unheads(ref, val)
        dstate[...] = ds

    blk = pl.BlockSpec((GROUP * CHUNK, DNW), lambda c: (ng - 1 - c, 0))
    return _pc(kern, "dn_chunks_bwd", (ng,),
               [blk] * 5 + [pl.BlockSpec((1, DN_H, DH, DH), lambda c: (ng - 1 - c, 0, 0, 0)),
                            pl.BlockSpec((1, GROUP * DN_H, CHUNK, CHUNK), lambda c: (ng - 1 - c, 0, 0, 0)), blk],
               [blk] * 5, [SDS((s_len, DNW), F32)] * 5,
               scratch=[pltpu.VMEM((DN_H, DH, DH), F32)])(q, k, v, gb, bb, s_all, t_all, d_o)


def _t5_bucket_table():
    qi = np.arange(BLK)[:, None]
    kj = np.arange(2 * BLK)[None, :]
    dist = BLK + qi - kj
    n = np.maximum(dist, 0)
    max_exact = NBUCKET // 2
    nf = np.maximum(n, 1).astype(np.float32)
    large = max_exact + (np.log(nf / np.float32(max_exact)) / np.float32(math.log(MAXDIST / max_exact))
                         * np.float32(NBUCKET - max_exact)).astype(np.int32)
    large = np.minimum(large, NBUCKET - 1)
    return np.where(n < max_exact, n, large)


def _bucket_onehot_t():
    table = _t5_bucket_table().reshape(-1)
    return (np.arange(NBUCKET)[:, None] == table[None, :]).astype(np.float32)


def _swa_mask(first):
    qi = lax.broadcasted_iota(jnp.int32, (BLK, 2 * BLK), 0)
    kj = lax.broadcasted_iota(jnp.int32, (BLK, 2 * BLK), 1)
    dist = BLK + qi - kj
    window = (dist >= 0) & (dist < BLK)
    return window & ((kj >= BLK) | jnp.logical_not(first))


def _bias_expand(rel_bias_t):
    onehot = jnp.asarray(_bucket_onehot_t())

    def kern(r_ref, oh_ref, o_ref):
        o_ref[...] = _raw_dot(r_ref[...], oh_ref[...], "nn", True)

    return pl.pallas_call(
        kern, name="bias_expand", out_shape=SDS((SWA_H, BLK * 2 * BLK), F32), compiler_params=_cparams(),
    )(rel_bias_t, onehot)


def _bias_reduce(d_bias_flat):
    onehot = jnp.asarray(_bucket_onehot_t())

    def kern(d_ref, oh_ref, o_ref):
        o_ref[...] = _raw_dot(d_ref[...], oh_ref[...], "nt", True)

    return pl.pallas_call(
        kern, name="bias_reduce", out_shape=SDS((SWA_H, NBUCKET), F32), compiler_params=_cparams(),
    )(d_bias_flat, onehot)


def _swa_specs(nb, rev):
    def blk(n):
        return (nb - 1 - n) if rev else n

    def before(n):
        return jnp.maximum(blk(n) - 1, 0)

    q_spec = pl.BlockSpec((BLK, SWAW), lambda n: (blk(n), C_SQ // SWAW))
    k_cur = pl.BlockSpec((BLK, SWAKW), lambda n: (blk(n), C_SK // SWAKW))
    k_prev = pl.BlockSpec((BLK, SWAKW), lambda n: (before(n), C_SK // SWAKW))
    v_cur = pl.BlockSpec((BLK, SWAKW), lambda n: (blk(n), C_SV // SWAKW))
    v_prev = pl.BlockSpec((BLK, SWAKW), lambda n: (before(n), C_SV // SWAKW))
    bias = pl.BlockSpec((SWA_H, BLK, 2 * BLK), lambda n: (0, 0, 0))
    gain = pl.BlockSpec((1, SWA_D), lambda n: (0, 0))
    sink = pl.BlockSpec((SWA_KV, 1, SWA_G), lambda n: (0, 0, 0))
    wide = pl.BlockSpec((BLK, SWAW), lambda n: (blk(n), 0))
    narrow = pl.BlockSpec((BLK, SWAKW), lambda n: (blk(n), 0))
    return [q_spec, k_prev, k_cur, v_prev, v_cur, bias, gain, gain, sink], wide, narrow


def _split_heads(x):
    return jnp.stack([x[:, h * SWA_D:(h + 1) * SWA_D] for h in range(x.shape[1] // SWA_D)], axis=0)


def _join_heads(x):
    return jnp.concatenate([x[h] for h in range(x.shape[0])], axis=1)


def _swa_fwd(proj, bias, qg, kg, sinks):
    s_len = proj.shape[0]
    nb = s_len // BLK
    in_specs, wide, _ = _swa_specs(nb, False)

    def kern(q_ref, kp_ref, kc_ref, vp_ref, vc_ref, b_ref, qg_ref, kg_ref, s_ref, o_ref):
        mask = _swa_mask(pl.program_id(0) == 0)
        o8 = _f_swa(*[_split_heads(r[...]) for r in (q_ref, kp_ref, kc_ref, vp_ref, vc_ref)], b_ref[...], qg_ref[...],
                    kg_ref[...], s_ref[...], mask)
        o_ref[...] = _join_heads(o8).astype(BF16)

    return _pc(kern, "swa_fwd", (nb,), in_specs, wide, SDS((s_len, SWAW), BF16))(
        proj, proj, proj, proj, proj, bias, qg, kg, sinks)


def _swa_bwd(proj, bias, qg, kg, sinks, d_out):
    s_len = proj.shape[0]
    nb = s_len // BLK
    in_specs, wide, narrow = _swa_specs(nb, True)

    def kern(q_ref, kp_ref, kc_ref, vp_ref, vc_ref, b_ref, qg_ref, kg_ref, s_ref, do_ref,
             dq_ref, dk_ref, dv_ref, db_ref, dqg_ref, dkg_ref, ds_ref, carry_k, carry_v):
        n = pl.program_id(0)
        mask = _swa_mask(n == nb - 1)
        _zero_first([carry_k, carry_v, db_ref, ds_ref, dqg_ref, dkg_ref])
        fn = functools.partial(_f_swa, mask=mask)
        _, vjp = jax.vjp(fn, *[_split_heads(r[...]) for r in (q_ref, kp_ref, kc_ref, vp_ref, vc_ref)], b_ref[...],
                         qg_ref[...], kg_ref[...], s_ref[...])
        dq, dkp, dkc, dvp, dvc, dbias, dqg, dkg, dsink = vjp(_split_heads(do_ref[...]))
        dq_ref[...] = _join_heads(dq).astype(BF16)
        dk_ref[...] = (_join_heads(dkc) + carry_k[...]).astype(BF16)
        dv_ref[...] = (_join_heads(dvc) + carry_v[...]).astype(BF16)
        carry_k[...] = _join_heads(dkp)
        carry_v[...] = _join_heads(dvp)
        db_ref[...] += dbias
        dqg_ref[...] += dqg
        dkg_ref[...] += dkg
        ds_ref[...] += dsink

    bias_spec, gain, sink = in_specs[5], in_specs[6], in_specs[8]
    return _pc(
        kern, "swa_bwd", (nb,), in_specs + [wide], [wide, narrow, narrow, bias_spec, gain, gain, sink],
        [SDS((s_len, SWAW), BF16), SDS((s_len, SWAKW), BF16), SDS((s_len, SWAKW), BF16),
         SDS((SWA_H, BLK, 2 * BLK), F32), SDS((1, SWA_D), F32), SDS((1, SWA_D), F32), SDS((SWA_KV, 1, SWA_G), F32)],
        scratch=[pltpu.VMEM((BLK, SWAKW), F32), pltpu.VMEM((BLK, SWAKW), F32)],
    )(proj, proj, proj, proj, proj, bias, qg, kg, sinks, d_out)


def _branch_merge(y_dn, y_swa, wa, wb, proj):
    s_len = y_dn.shape[0]
    tm = min(1024, s_len)

    def kern(ya_ref, yb_ref, wa_ref, wb_ref, ga_ref, gb_ref, pa_ref, pb_ref, m_ref):
        for rows in _row_pieces(tm, 128):
            pa = _bdot(ya_ref[rows, :], wa_ref[0])
            pb = _bdot(yb_ref[rows, :], wb_ref[0])
            pa_ref[rows, :] = pa.astype(BF16)
            pb_ref[rows, :] = pb.astype(BF16)
            m_ref[rows, :] = _f_merge(pa, pb, ga_ref[rows, :], gb_ref[rows, :]).astype(BF16)

    y_spec = pl.BlockSpec((tm, DNW), lambda i, s: (i, 0))
    w_spec = pl.BlockSpec((1, DNW, CSH), lambda i, s: (s, 0, 0))
    o_spec = pl.BlockSpec((tm, CSH), lambda i, s: (i, s))
    ga_spec = pl.BlockSpec((tm, CSH), lambda i, s: (i, C_GATE // CSH + s))
    gb_spec = pl.BlockSpec((tm, CSH), lambda i, s: (i, (C_GATE + D) // CSH + s))
    return _pc(kern, "branch_merge", (s_len // tm, N_CHIPS), [y_spec, y_spec, w_spec, w_spec, ga_spec, gb_spec],
               [o_spec] * 3, [SDS((s_len, D), BF16)] * 3,
               )(y_dn, y_swa, wa, wb, proj, proj)


def _in_proj(x, gain, w_in_p):
    s_len = x.shape[0]
    tm = min(512, s_len)

    def kern(x_ref, g_ref, w_ref, h_ref, p_ref):
        h = _f_rms(x_ref[...], g_ref[...]).astype(BF16)
        h_ref[...] = h
        p_ref[...] = _bdot(h, w_ref[...])

    row = pl.BlockSpec((tm, D), lambda i: (i, 0))
    return _pc(kern, "in_proj", (s_len // tm,),
               [row, pl.BlockSpec((1, D), lambda i: (0, 0)), _resident((D, PW))],
               [row, pl.BlockSpec((tm, PW), lambda i: (i, 0))],
               [SDS((s_len, D), BF16), SDS((s_len, PW), F32)])(x, gain, w_in_p)


def _out_proj(merged, w_out, x, gain):
    s_len = x.shape[0]
    tm = min(256, s_len)

    def kern(m_ref, w_ref, x_ref, g_ref, x1_ref, h2_ref):
        x1 = x_ref[...] + _bdot(m_ref[...], w_ref[...])
        x1_ref[...] = x1
        h2_ref[...] = _f_rms(x1, g_ref[...]).astype(BF16)

    row = pl.BlockSpec((tm, D), lambda i: (i, 0))
    return _pc(kern, "out_proj", (s_len // tm,),
               [row, pl.BlockSpec((D, D), lambda i: (0, 0)), row, pl.BlockSpec((1, D), lambda i: (0, 0))],
               [row, row], [SDS((s_len, D), F32), SDS((s_len, D), BF16)])(merged, w_out, x, gain)


def _ffn_up(h2, wg, wu):
    s_len = h2.shape[0]
    tm = min(1024, s_len)

    def kern(h_ref, g_ref, u_ref, gt_ref, up_ref, act_ref):
        for rows in _row_pieces(tm, 256):
            h = h_ref[rows, :]
            g = _bdot(h, g_ref[0], "nt")
            u = _bdot(h, u_ref[0], "nt")
            gt_ref[0, rows, :] = g.astype(BF16)
            up_ref[0, rows, :] = u.astype(BF16)
            act_ref[0, rows, :] = _f_swiglu(g, u).astype(BF16)

    w_spec = pl.BlockSpec((1, FSH, D), lambda s, i: (s, 0, 0))
    o_spec = pl.BlockSpec((1, tm, FSH), lambda s, i: (s, i, 0))
    shape = (N_CHIPS, s_len, FSH)
    return _pc(kern, "ffn_up", (N_CHIPS, s_len // tm), [pl.BlockSpec((tm, D), lambda s, i: (i, 0)), w_spec, w_spec],
               [o_spec] * 3, [SDS(shape, BF16)] * 3)(h2, wg, wu)


def _ffn_down_loss(act, wd, x1, target):
    s_len = x1.shape[0]
    tm = min(256, s_len)

    def kern(a_ref, w_ref, x_ref, t_ref, dy_ref, dyb_ref, loss_ref):
        _zero_first([loss_ref])
        for rows in _row_pieces(tm, 128):
            y = x_ref[rows, :]
            for s in range(N_CHIPS):
                y = y + _bdot(a_ref[s, rows, :], w_ref[s])
            d = y - t_ref[rows, :]
            dy = d * (1.0 / D)
            dy_ref[rows, :] = dy
            dyb_ref[rows, :] = dy.astype(BF16)
            loss_ref[...] += jnp.sum(d * d).reshape(1, 1) * (0.5 / D)

    row = pl.BlockSpec((tm, D), lambda i: (i, 0))
    return _pc(kern, "ffn_down_loss", (s_len // tm,),
               [pl.BlockSpec((N_CHIPS, tm, FSH), lambda i: (0, i, 0)),
                pl.BlockSpec((N_CHIPS, FSH, D), lambda i: (0, 0, 0)), row, row],
               [row, row, pl.BlockSpec((1, 1), lambda i: (0, 0))],
               [SDS((s_len, D), F32), SDS((s_len, D), BF16), SDS((1, 1), F32)])(act, wd, x1, target)


def _ffn_dact(dy_b, wd, gt, up):
    s_len = dy_b.shape[0]
    tm = min(1024, s_len)

    def kern(dy_ref, w_ref, gt_ref, up_ref, dg_ref, du_ref):
        w = w_ref[0]
        for rows in _row_pieces(tm, 256):
            d_act = _bdot(dy_ref[rows, :], w, "nt")
            _, vjp = jax.vjp(_f_swiglu, gt_ref[0, rows, :].astype(F32), up_ref[0, rows, :].astype(F32))
            dg, du = vjp(d_act)
            dg_ref[0, rows, :] = dg.astype(BF16)
            du_ref[0, rows, :] = du.astype(BF16)

    a_spec = pl.BlockSpec((1, tm, FSH), lambda s, i: (s, i, 0))
    shape = (N_CHIPS, s_len, FSH)
    return _pc(kern, "ffn_dact", (N_CHIPS, s_len // tm),
               [pl.BlockSpec((tm, D), lambda s, i: (i, 0)), pl.BlockSpec((1, FSH, D), lambda s, i: (s, 0, 0)),
                a_spec, a_spec],
               [a_spec, a_spec], [SDS(shape, BF16), SDS(shape, BF16)])(dy_b, wd, gt, up)


def _gw_ffn(lhs, rhs, name):
    s_len = rhs.shape[0]
    n = len(lhs)
    tn = 512

    def kern(*refs):
        g = refs[n][...]
        for i in range(n):
            refs[n + 1 + i][0] = _bdot(refs[i][0], g, "tn").astype(BF16)

    a_spec = pl.BlockSpec((1, s_len, FSH), lambda s, j: (s, 0, 0))
    o_spec = pl.BlockSpec((1, FSH, tn), lambda s, j: (s, 0, j))
    return _pc(kern, name, (N_CHIPS, D // tn), [a_spec] * n + [pl.BlockSpec((s_len, tn), lambda s, j: (0, j))],
               [o_spec] * n, [SDS((N_CHIPS, FSH, D), BF16)] * n)(*lhs, rhs)


def _ffn_dh2(d_gt, d_up, wg, wu, x1, dy, gain):
    s_len = x1.shape[0]
    tm = min(512, s_len)

    def kern(dg_ref, du_ref, wg_ref, wu_ref, x_ref, dy_ref, g_ref, dx_ref, dxb_ref, dgain_ref):
        _zero_first([dgain_ref])
        dh2 = jnp.zeros((tm, D), F32)
        for s in range(N_CHIPS):
            dh2 = dh2 + _bdot(dg_ref[s], wg_ref[s]) + _bdot(du_ref[s], wu_ref[s])
        _, vjp = jax.vjp(_f_rms, x_ref[...], g_ref[...])
        dx, dgain = vjp(dh2)
        dx1 = dx + dy_ref[...]
        dx_ref[...] = dx1
        dxb_ref[...] = dx1.astype(BF16)
        dgain_ref[...] += dgain

    row = pl.BlockSpec((tm, D), lambda i: (i, 0))
    d_spec = pl.BlockSpec((N_CHIPS, tm, FSH), lambda i: (0, i, 0))
    w_spec = _resident((N_CHIPS, FSH, D))
    vec = pl.BlockSpec((1, D), lambda i: (0, 0))
    return _pc(kern, "ffn_dh2", (s_len // tm,), [d_spec, d_spec, w_spec, w_spec, row, row, vec],
               [row, row, vec], [SDS((s_len, D), F32), SDS((s_len, D), BF16), SDS((1, D), F32)],
               )(d_gt, d_up, wg, wu, x1, dy, gain)


def _merge_bwd(dx1_b, w_out, pa, pb, proj):
    s_len = dx1_b.shape[0]
    tm = min(256, s_len)

    def kern(dx_ref, w_ref, pa_ref, pb_ref, g_ref, dpa_ref, dpb_ref, dg_ref):
        dm = _bdot(dx_ref[...], w_ref[...], "nt")
        gates = g_ref[...]
        _, vjp = jax.vjp(_f_merge, pa_ref[...].astype(F32), pb_ref[...].astype(F32), gates[:, :D], gates[:, D:])
        dpa, dpb, dga, dgb = vjp(dm)
        dpa_ref[...] = dpa.astype(BF16)
        dpb_ref[...] = dpb.astype(BF16)
        dg_ref[:, :D] = dga.astype(BF16)
        dg_ref[:, D:] = dgb.astype(BF16)

    row = pl.BlockSpec((tm, D), lambda i: (i, 0))
    return _pc(kern, "merge_bwd", (s_len // tm,),
               [row, pl.BlockSpec((D, D), lambda i: (0, 0)), row, row,
                pl.BlockSpec((tm, 2 * D), lambda i: (i, C_GATE // (2 * D)))],
               [row, row, pl.BlockSpec((tm, 2 * D), lambda i: (i, 0))],
               [SDS((s_len, D), BF16), SDS((s_len, D), BF16), SDS((s_len, 2 * D), BF16)],
               )(dx1_b, w_out, pa, pb, proj)


def _d_branch(d_pa, d_pb, wa, wb):
    s_len = d_pa.shape[0]
    tm = min(512, s_len)

    def kern(da_ref, db_ref, wa_ref, wb_ref, oa_ref, ob_ref):
        acc_a = jnp.zeros((tm, DNW), F32)
        acc_b = jnp.zeros((tm, SWAW), F32)
        for s in range(N_CHIPS):
            acc_a = acc_a + _bdot(da_ref[:, s * CSH:(s + 1) * CSH], wa_ref[s], "nt")
            acc_b = acc_b + _bdot(db_ref[:, s * CSH:(s + 1) * CSH], wb_ref[s], "nt")
        oa_ref[...] = acc_a
        ob_ref[...] = acc_b

    row = pl.BlockSpec((tm, D), lambda i: (i, 0))
    w_spec = pl.BlockSpec((N_CHIPS, DNW, CSH), lambda i: (0, 0, 0))
    out = pl.BlockSpec((tm, DNW), lambda i: (i, 0))
    return _pc(kern, "d_branch", (s_len // tm,), [row, row, w_spec, w_spec], [out, out],
               [SDS((s_len, DNW), F32), SDS((s_len, SWAW), F32)])(d_pa, d_pb, wa, wb)


def _gw_branch(y_dn, y_swa, d_pa, d_pb):
    s_len = y_dn.shape[0]

    def kern(ya_ref, yb_ref, da_ref, db_ref, oa_ref, ob_ref):
        oa_ref[0] = _bdot(ya_ref[...], da_ref[...], "tn").astype(BF16)
        ob_ref[0] = _bdot(yb_ref[...], db_ref[...], "tn").astype(BF16)

    y_spec = pl.BlockSpec((s_len, DNW), lambda s: (0, 0))
    d_spec = pl.BlockSpec((s_len, CSH), lambda s: (0, s))
    o_spec = pl.BlockSpec((1, DNW, CSH), lambda s: (s, 0, 0))
    shape = (N_CHIPS, DNW, CSH)
    return _pc(kern, "gw_branch", (N_CHIPS,), [y_spec, y_spec, d_spec, d_spec], [o_spec, o_spec],
               [SDS(shape, BF16), SDS(shape, BF16)])(y_dn, y_swa, d_pa, d_pb)


def _dh_rms(d_proj, w_in_p, x, dx1, gain):
    s_len = x.shape[0]
    tm = min(512, s_len)

    def kern(dp_ref, w_ref, x_ref, r_ref, g_ref, gx_ref, dgain_ref):
        _zero_first([dgain_ref])
        dh = _bdot(dp_ref[...], w_ref[...], "nt")
        _, vjp = jax.vjp(_f_rms, x_ref[...], g_ref[...])
        dx, dgain = vjp(dh)
        gx_ref[...] = dx + r_ref[...]
        dgain_ref[...] += dgain

    row = pl.BlockSpec((tm, D), lambda i: (i, 0))
    vec = pl.BlockSpec((1, D), lambda i: (0, 0))
    return _pc(kern, "dh_rms", (s_len // tm,),
               [pl.BlockSpec((tm, PW), lambda i: (i, 0)), _resident((D, PW)), row, row, vec],
               [row, vec], [SDS((s_len, D), F32), SDS((1, D), F32)])(d_proj, w_in_p, x, dx1, gain)


HALO = 8


def _rows_down(x, n, above):
    tm = x.shape[0]
    r = pltpu.roll(x, n, 0)
    a = pltpu.roll(above, n, 0)
    top = jnp.where(lax.broadcasted_iota(jnp.int32, above.shape, 0) < n, a, r[0:HALO])
    return jnp.concatenate([top, r[HALO:tm]], axis=0)


def _rows_up(x, n, below):
    tm = x.shape[0]
    r = pltpu.roll(x, tm - n, 0)
    b = pltpu.roll(below, HALO - n, 0)
    bottom = jnp.where(lax.broadcasted_iota(jnp.int32, below.shape, 0) >= HALO - n, b, r[tm - HALO:tm])
    return jnp.concatenate([r[0:tm - HALO], bottom], axis=0)


def _conv_taps(cur_ref, prev_ref, first):
    cur = cur_ref[...]
    above = jnp.where(first, 0.0, prev_ref[...])
    return [_rows_down(cur, n, above) for n in range(CONV - 1, 0, -1)] + [cur]


def _dn_pre_specs(s_len, tm, blk):
    cur = pl.BlockSpec((tm, QKVW), lambda i: (blk(i), 0))
    prev = pl.BlockSpec((HALO, QKVW), lambda i: (jnp.maximum(blk(i) * (tm // HALO) - 1, 0), 0))
    ba = pl.BlockSpec((tm, 128), lambda i: (blk(i), C_BA // 128))
    row = pl.BlockSpec((tm, DNW), lambda i: (blk(i), 0))
    full = [pl.BlockSpec((CONV, QKVW), lambda i: (0, 0)), pl.BlockSpec((1, DN_H), lambda i: (0, 0)),
            pl.BlockSpec((1, DN_H), lambda i: (0, 0))]
    return cur, prev, ba, row, full


def _dn_pre_fwd(proj, conv_w, alog, dtb):
    s_len = proj.shape[0]
    tm = min(128, s_len)
    cur, prev, ba, row, full = _dn_pre_specs(s_len, tm, lambda i: i)

    def kern(cur_ref, prev_ref, ba_ref, cw_ref, al_ref, dt_ref, q_ref, k_ref, v_ref, bb_ref, gb_ref):
        xs = _conv_taps(cur_ref, prev_ref, pl.program_id(0) == 0)
        outs = _f_dn_pre(*xs, ba_ref[...], cw_ref[...], al_ref[...], dt_ref[...])
        for ref, val in zip((q_ref, k_ref, v_ref, bb_ref, gb_ref), outs, strict=True):
            ref[...] = val

    return _pc(kern, "dn_pre_fwd", (s_len // tm,), [cur, prev, ba] + full, [row] * 5,
               [SDS((s_len, DNW), F32)] * 5)(proj, proj, proj, conv_w, alog, dtb)


def _dn_pre_bwd(proj, conv_w, alog, dtb, cots, others):
    s_len = proj.shape[0]
    tm = min(128, s_len)
    nb = s_len // tm
    cur, prev, ba, row, full = _dn_pre_specs(s_len, tm, lambda i: nb - 1 - i)
    n_o = len(others)
    assert QKVW + sum(t.shape[1] for t in others) + 128 == C_BA + 128

    def kern(cur_ref, prev_ref, ba_ref, cw_ref, al_ref, dt_ref, dq_ref, dk_ref, dv_ref, dbb_ref, dgb_ref, *rest):
        o_refs = rest[:n_o]
        dproj_ref, dcw_ref, dal_ref, ddt_ref, *tails = rest[n_o:]
        i = pl.program_id(0)
        _zero_first([dcw_ref, dal_ref, ddt_ref] + tails)
        xs = _conv_taps(cur_ref, prev_ref, i == nb - 1)
        _, vjp = jax.vjp(_f_dn_pre, *xs, ba_ref[...], cw_ref[...], al_ref[...], dt_ref[...])
        *dxs, dba, dcw, dal, ddt = vjp((dq_ref[...], dk_ref[...], dv_ref[...], dbb_ref[...], dgb_ref[...]))
        total = dxs[CONV - 1]
        for j, t in enumerate(tails):
            n = CONV - 1 - j
            total = total + _rows_up(dxs[j], n, t[...])
            t[...] = dxs[j][0:HALO, :]
        dproj_ref[...] = jnp.concatenate(
            [total.astype(BF16)] + [r[...] for r in o_refs] + [dba.astype(BF16), jnp.zeros((tm, PW - C_BA - 128), BF16)],
            axis=1)
        dcw_ref[...] += dcw
        dal_ref[...] += dal
        ddt_ref[...] += ddt

    o_specs = [pl.BlockSpec((tm, t.shape[1]), lambda i: (nb - 1 - i, 0)) for t in others]
    return _pc(kern, "dn_pre_bwd", (nb,), [cur, prev, ba] + full + [row] * 5 + o_specs,
               [pl.BlockSpec((tm, PW), lambda i: (nb - 1 - i, 0))] + full,
               [SDS((s_len, PW), BF16), SDS((CONV, QKVW), F32), SDS((1, DN_H), F32), SDS((1, DN_H), F32)],
               scratch=[pltpu.VMEM((HALO, QKVW), F32)] * (CONV - 1))(proj, proj, proj, conv_w, alog, dtb, *cots, *others)


def _w_in_to_padded(w_sh):
    tr = 256

    def kern(w_ref, o_ref):
        full = jnp.concatenate([w_ref[s] for s in range(N_CHIPS)], axis=1)
        pieces = [full[:, o0:o0 + w] for o0, w, _ in sorted(_ORIG_PIECES, key=lambda t: t[2])]
        o_ref[...] = jnp.concatenate(pieces + [jnp.zeros((tr, PW - D_IN), w_ref.dtype)], axis=1)

    return _pc(kern, "w_in_to_padded", (D // tr,), [pl.BlockSpec((N_CHIPS, tr, D_IN // N_CHIPS), lambda i: (0, i, 0))],
               pl.BlockSpec((tr, PW), lambda i: (i, 0)), SDS((D, PW), w_sh.dtype))(w_sh)


def _padded_to_w_in(g):
    tr = 256
    csh = D_IN // N_CHIPS

    def kern(g_ref, o_ref):
        x = g_ref[...]
        full = jnp.concatenate([x[:, p0:p0 + w] for _, w, p0 in _ORIG_PIECES], axis=1)
        for s in range(N_CHIPS):
            o_ref[s] = full[:, s * csh:(s + 1) * csh]

    return _pc(kern, "padded_to_w_in", (D // tr,), [pl.BlockSpec((tr, PW), lambda i: (i, 0))],
               pl.BlockSpec((N_CHIPS, tr, csh), lambda i: (0, i, 0)), SDS((N_CHIPS, D, csh), g.dtype))(g)


def _local_step(x, target, wts):
    s_len = x.shape[0]
    tm = min(256, s_len)
    w_in_p = wts["w_in_p"]
    attn_gain = wts["attn_norm"]
    ffn_gain = wts["ffn_norm"]
    conv_w = wts["dn_conv"]
    alog, dtb, out_gain = wts["dn_a_log"], wts["dn_dt_bias"], wts["dn_out_norm"]
    qg, kg = wts["swa_q_norm"], wts["swa_k_norm"]
    sinks = wts["swa_sinks"].reshape(SWA_KV, 1, SWA_G)

    h, proj = _in_proj(x, attn_gain, w_in_p)
    q_dn, k_dn, v_dn, bb, gb = _dn_pre_fwd(proj, conv_w, alog, dtb)
    o_dn, s_all, t_all = _dn_chunks_fwd(q_dn, k_dn, v_dn, gb, bb)
    post_ins = [_whole(o_dn), (proj, DNW, C_Z // DNW)]
    (y_dn,) = _rows(lambda r, f: ([_f_dn_post(r[0], r[1], f[0])], []), "dn_post_fwd", s_len, tm, post_ins,
                    [out_gain], [(DNW, BF16)])

    bias = _bias_expand(wts["rel_bias"].T).reshape(SWA_H, BLK, 2 * BLK)
    y_swa = _swa_fwd(proj, bias, qg, kg, sinks)

    wts = {**wts, **wts["late"](y_swa)}
    p_a, p_b, merged = _branch_merge(y_dn, y_swa, wts["wa"], wts["wb"], proj)
    x1, h2 = _out_proj(merged, wts["w_out"], x, ffn_gain)
    gt, up, act = _ffn_up(h2, wts["wg"], wts["wu"])
    dy, dy_b, loss = _ffn_down_loss(act, wts["wd"], x1, target)

    grads = {}
    d_gt, d_up = _ffn_dact(dy_b, wts["wd"], gt, up)
    (grads["w_down"],) = _gw_ffn([act], dy_b, "gw_down")
    grads["w_gate"], grads["w_up"] = _gw_ffn([d_gt, d_up], h2, "gw_gate_up")
    token = wts["send_ffn"](grads)
    dx1, dx1_b, grads["ffn_norm"] = _ffn_dh2(d_gt, d_up, wts["wg"], wts["wu"], x1, dy,
                                             ffn_gain + token[0:1, 0:1])
    grads["w_out"] = _mm(merged, dx1_b, "tn", BF16, 512, 512, "gw_out")
    d_pa, d_pb, d_gr = _merge_bwd(dx1_b, wts["w_out"], p_a, p_b, proj)
    d_ydn, d_yswa = _d_branch(d_pa, d_pb, wts["wa"], wts["wb"])
    grads["w_branch_dn"], grads["w_branch_swa"] = _gw_branch(y_dn, y_swa, d_pa, d_pb)
    token = wts["send_early"](grads)
    qg_t = qg + token[0:1, 0:1]
    out_gain_t = out_gain + token[0:1, 0:1]

    d_sq, d_sk, d_sv, d_bias, grads["swa_q_norm"], grads["swa_k_norm"], d_sinks = _swa_bwd(
        proj, bias, qg_t, kg, sinks, d_yswa)
    grads["swa_sinks"] = d_sinks.reshape(1, SWA_H)
    grads["rel_bias"] = _bias_reduce(d_bias.reshape(SWA_H, BLK * 2 * BLK)).T

    def post_bwd(r, f):
        _, vjp = jax.vjp(_f_dn_post, r[0], r[1], f[0])
        d_o, d_z, d_gain = vjp(r[2])
        return [d_o, d_z], [d_gain]

    d_o, d_z, grads["dn_out_norm"] = _rows(post_bwd, "dn_post_bwd", s_len, tm, post_ins + [_whole(d_ydn)], [out_gain_t],
                                           [(DNW, F32), (DNW, BF16)], [(1, DH)])
    d_q, d_k, d_v, d_gb, d_bb = _dn_chunks_bwd(q_dn, k_dn, v_dn, gb, bb, s_all, t_all, d_o)

    d_proj, grads["dn_conv"], grads["dn_a_log"], grads["dn_dt_bias"] = _dn_pre_bwd(
        proj, conv_w, alog, dtb, (d_q, d_k, d_v, d_bb, d_gb), (d_z, d_gr, d_sq, d_sk, d_sv))
    grads["w_in_p"] = _mm(h, d_proj, "tn", BF16, 512, 1024, "gw_in")
    token = wts["send_in"](grads["w_in_p"])
    grad_x, grads["attn_norm"] = _dh_rms(d_proj, w_in_p, x, dx1, attn_gain + token[0:1, 0:1])
    return loss, grad_x, grads


_HBM = pl.BlockSpec(memory_space=pl.ANY)


def _place():
    return lax.axis_index("x"), lax.axis_index("y"), lax.axis_index("c")


def _other_chips(x, y):
    return [(1 - x, y), (x, 1 - y), (1 - x, 1 - y)]


def _rcopy(src, dst, send_sems, recv_sems, k, to):
    return pltpu.make_async_remote_copy(src_ref=src, dst_ref=dst, send_sem=send_sems.at[k], recv_sem=recv_sems.at[k],
                                        device_id=to, device_id_type=MESH)


def _comm_call(body, name, ins, out_shapes, n_remote, landing=0):
    first = len(ins) - landing
    return pl.pallas_call(
        body, name=name, in_specs=[_HBM] * len(ins), out_specs=[_HBM] * len(out_shapes), out_shape=out_shapes,
        scratch_shapes=[pltpu.SemaphoreType.DMA((n_remote,)), pltpu.SemaphoreType.DMA((n_remote,))],
        input_output_aliases={first + i: i for i in range(landing)},
        compiler_params=_cparams(has_side_effects=True),
    )(*ins)


def _own_slot(blocks, chip):
    return [lax.dynamic_update_slice(lax.empty((N_CHIPS,) + b.shape, b.dtype), b[None], (chip, 0, 0)) for b in blocks]


def _gather_weights(ws, chip):
    n = len(ws)
    halves = [w.shape[0] // 2 for w in ws]

    def body(*refs):
        w_refs, o_refs = refs[:n], refs[2 * n:3 * n]
        send_sems, recv_sems = refs[3 * n:]
        x, y, c = _place()
        s = 2 * x + y
        sib = (x, y, 1 - c)
        chips = _other_chips(x, y)

        def rows(i, half):
            return pl.ds(half * halves[i], halves[i])

        first = []
        for j, (cx, cy) in enumerate(chips):
            for i in range(n):
                cp = _rcopy(w_refs[i].at[rows(i, c), :], o_refs[i].at[s, rows(i, c), :], send_sems, recv_sems,
                            j * n + i, (cx, cy, c))
                cp.start()
                first.append(cp)
        passed = []
        for j, (cx, cy) in enumerate(chips):
            sj = 2 * cx + cy
            for i in range(n):
                blk = o_refs[i].at[sj, rows(i, c), :]
                _rcopy(blk, blk, send_sems, recv_sems, j * n + i, (cx, cy, c)).wait_recv()
                cp = _rcopy(blk, blk, send_sems, recv_sems, (3 + j) * n + i, sib)
                cp.start()
                passed.append(cp)
        for j, (cx, cy) in enumerate(chips):
            sj = 2 * cx + cy
            for i in range(n):
                blk = o_refs[i].at[sj, rows(i, 1 - c), :]
                _rcopy(blk, blk, send_sems, recv_sems, (3 + j) * n + i, sib).wait_recv()
        for cp in first + passed:
            cp.wait_send()

    return _comm_call(body, "gather_weights", list(ws) + _own_slot(ws, chip),
                      [SDS((N_CHIPS,) + w.shape, w.dtype) for w in ws], 6 * n, landing=n)


_HBM_ONLY = pl.BlockSpec(memory_space=pltpu.HBM)
_SEM = pl.BlockSpec(memory_space=pltpu.SEMAPHORE)
_DATAFLOW = pltpu.SideEffectType.DATAFLOW_SIDE_EFFECTING


def _in_hbm(a):
    return pltpu.with_memory_space_constraint(a, pltpu.HBM)


def _gather_windows(blocks):
    halves = [b.shape[0] // 2 for b in blocks]

    def src_at(ref, i, c, sj):
        return ref.at[pl.ds(c * halves[i], halves[i]), :]

    def dst_at(ref, i, c, s_from):
        return ref.at[s_from, pl.ds(c * halves[i], halves[i]), :]

    return src_at, dst_at


def _exchange_windows():
    return (lambda ref, i, c, sj: ref.at[sj]), (lambda ref, i, c, s_from: ref.at[s_from])


def _swap_windows(gs):
    halves = [g.shape[1] // 2 for g in gs]
    return ((lambda ref, i, c, tag: ref.at[:, pl.ds((1 - c) * halves[i], halves[i]), :]),
            (lambda ref, i, c, slot: ref))


def _chip_peers(x, y, c):
    return [(2 * cx + cy, (cx, cy, c), 2 * x + y, 2 * cx + cy) for cx, cy in _other_chips(x, y)]


def _sibling_peer(x, y, c):
    return [(0, (x, y, 1 - c), 0, 0)]


def _split_start(name, ws, lands, dep, windows, peers=_chip_peers, n_peers=3):
    n = len(ws)
    src_at, dst_at = windows

    def body(*refs):
        w_refs, l_refs = refs[:n], refs[n:2 * n]
        send_sems, recv_sems = refs[2 * n + 1], refs[2 * n + 2]
        token = refs[-1]
        x, y, c = _place()
        for j, (tag, dev, there, _) in enumerate(peers(x, y, c)):
            for i in range(n):
                _rcopy(src_at(w_refs[i], i, c, tag), dst_at(l_refs[i], i, c, there), send_sems, recv_sems,
                       j * n + i, dev).start()
        token[...] = jnp.zeros_like(token)

    outs = pl.pallas_call(
        body, name=name,
        out_shape=(pltpu.SemaphoreType.DMA((n_peers * n,)), pltpu.SemaphoreType.DMA((n_peers * n,)),
                   *[pltpu.HBM(w.shape, w.dtype) for w in ws], *[pltpu.HBM(t.shape, t.dtype) for t in lands],
                   SDS((8, 128), F32)),
        in_specs=[_HBM_ONLY] * (2 * n) + [pl.BlockSpec(memory_space=pl.ANY)],
        out_specs=(_SEM, _SEM, *[_HBM_ONLY] * (2 * n), pl.BlockSpec(memory_space=pltpu.VMEM)),
        input_output_aliases={i: 2 + i for i in range(2 * n)},
        compiler_params=pltpu.CompilerParams(has_side_effects=_DATAFLOW),
    )(*[_in_hbm(w) for w in ws], *[_in_hbm(t) for t in lands], dep)
    return outs[0], outs[1], outs[2:2 + n], outs[2 + n:2 + 2 * n], outs[-1]


def _split_wait(name, w_thru, l_thru, send_sems, recv_sems, after, windows, peers=_chip_peers, with_sources=False):
    n = len(w_thru)
    src_at, dst_at = windows

    def body(*refs):
        w_refs, l_refs = refs[:n], refs[n:2 * n]
        send_sems, recv_sems = refs[2 * n], refs[2 * n + 1]
        x, y, c = _place()
        for j, (tag, dev, _, here) in enumerate(peers(x, y, c)):
            for i in range(n):
                cp = _rcopy(src_at(w_refs[i], i, c, tag), dst_at(l_refs[i], i, c, here), send_sems, recv_sems,
                            j * n + i, dev)
                cp.wait_send()
                cp.wait_recv()

    outs = pl.pallas_call(
        body, name=name,
        out_shape=[pltpu.HBM(w.shape, w.dtype) for w in w_thru] + [pltpu.HBM(t.shape, t.dtype) for t in l_thru],
        in_specs=[_HBM_ONLY] * (2 * n) + [_SEM, _SEM, pl.BlockSpec(memory_space=pl.ANY)],
        out_specs=[_HBM_ONLY] * (2 * n),
        input_output_aliases={i: i for i in range(2 * n)},
        compiler_params=pltpu.CompilerParams(has_side_effects=_DATAFLOW),
    )(*w_thru, *l_thru, send_sems, recv_sems, after)
    return (outs[:n], outs[n:]) if with_sources else outs[n:]


def _sibling_fill(lands):
    n = len(lands)
    halves = [t.shape[1] // 2 for t in lands]

    def body(*refs):
        o_refs = refs[n:2 * n]
        send_sems, recv_sems = refs[2 * n:]
        x, y, c = _place()
        sib = (x, y, 1 - c)
        chips = _other_chips(x, y)
        sent = []
        for j, (cx, cy) in enumerate(chips):
            for i in range(n):
                blk = o_refs[i].at[2 * cx + cy, pl.ds(c * halves[i], halves[i]), :]
                cp = _rcopy(blk, blk, send_sems, recv_sems, j * n + i, sib)
                cp.start()
                sent.append(cp)
        for j, (cx, cy) in enumerate(chips):
            for i in range(n):
                blk = o_refs[i].at[2 * cx + cy, pl.ds((1 - c) * halves[i], halves[i]), :]
                _rcopy(blk, blk, send_sems, recv_sems, j * n + i, sib).wait_recv()
        for cp in sent:
            cp.wait_send()

    return _comm_call(body, "sibling_fill", list(lands), [SDS(t.shape, t.dtype) for t in lands], 3 * n, landing=n)


def _swap_halves(gs, name):
    n = len(gs)
    halves = [g.shape[1] // 2 for g in gs]

    def body(*refs):
        g_refs, o_refs = refs[:n], refs[n:2 * n]
        send_sems, recv_sems = refs[2 * n:]
        x, y, c = _place()
        cps = [_rcopy(g_refs[i].at[:, pl.ds((1 - c) * halves[i], halves[i]), :], o_refs[i], send_sems, recv_sems, i,
                      (x, y, 1 - c)) for i in range(n)]
        for cp in cps:
            cp.start()
        for cp in cps:
            cp.wait()

    return _comm_call(body, name, gs, [SDS((N_CHIPS, h, g.shape[2]), g.dtype) for g, h in zip(gs, halves)], n)


def _swap_reduced(rs, name):
    n = len(rs)

    def body(*refs):
        r_refs, o_refs = refs[:n], refs[n:2 * n]
        send_sems, recv_sems = refs[2 * n:]
        x, y, c = _place()
        cps = [_rcopy(r_refs[i], o_refs[i], send_sems, recv_sems, i, (x, y, 1 - c)) for i in range(n)]
        for cp in cps:
            cp.start()
        for cp in cps:
            cp.wait()

    return _comm_call(body, name, rs, [SDS(r.shape, r.dtype) for r in rs], n)


def _all_sum_small(vec, name):
    n_dev = 8
    flips = [(bx, by, bc) for bx in (0, 1) for by in (0, 1) for bc in (0, 1)][1:]

    def body(v_ref, out_ref, gath, send_sems, recv_sems):
        x, y, c = _place()
        me = 4 * x + 2 * y + c
        gath[me] = v_ref[...]
        sent = []
        for k, (bx, by, bc) in enumerate(flips):
            peer = (x ^ bx, y ^ by, c ^ bc)
            cp = _rcopy(v_ref, gath.at[me], send_sems, recv_sems, k, peer)
            cp.start()
            sent.append(cp)
        for k, (bx, by, bc) in enumerate(flips):
            peer = (x ^ bx, y ^ by, c ^ bc)
            _rcopy(v_ref, gath.at[4 * peer[0] + 2 * peer[1] + peer[2]], send_sems, recv_sems, k, peer).wait_recv()
        for cp in sent:
            cp.wait_send()
        acc = gath[0]
        for d in range(1, n_dev):
            acc = acc + gath[d]
        out_ref[...] = acc

    vm = pl.BlockSpec(memory_space=pltpu.VMEM)
    return pl.pallas_call(
        body, name=name, in_specs=[vm], out_specs=vm, out_shape=SDS(vec.shape, F32),
        scratch_shapes=[pltpu.VMEM((n_dev,) + vec.shape, F32), pltpu.SemaphoreType.DMA((7,)),
                        pltpu.SemaphoreType.DMA((7,))],
        compiler_params=_cparams(has_side_effects=True),
    )(vec)


def _pack_small(vals, extra=None):
    parts = [vals[n].reshape(-1).astype(F32) for n, _ in _SMALL]
    parts.append(jnp.zeros((1,), F32) if extra is None else extra.reshape(1).astype(F32))
    flat = jnp.concatenate(parts)
    flat = jnp.concatenate([flat, jnp.zeros((_SMALL_ROWS * 128 - flat.shape[0],), F32)])
    return flat.reshape(_SMALL_ROWS, 128)


def _unpack_small(packed, shapes):
    flat = packed.reshape(-1)
    return {n: flat[_SMALL_OFF[n][0]:_SMALL_OFF[n][0] + _SMALL_OFF[n][1]].reshape(shapes[n]) for n, _ in _SMALL}


def _pair_sum(gs, gots, core, name):
    n = len(gs)

    def kern(c_ref, *refs):
        for i in range(n):
            refs[2 * n + i][...] = (refs[i][...].astype(F32) + refs[n + i][...].astype(F32)).astype(BF16)

    in_specs = [pl.BlockSpec((1, t.shape[1], t.shape[2]), lambda s, c_ref: (s, c_ref[0], 0)) for t in gots]
    in_specs += [pl.BlockSpec((1, t.shape[1], t.shape[2]), lambda s, c_ref: (s, 0, 0)) for t in gots]
    out_specs = [pl.BlockSpec((1, t.shape[1], t.shape[2]), lambda s, c_ref: (s, 0, 0)) for t in gots]
    return pl.pallas_call(
        kern, name=name,
        grid_spec=pltpu.PrefetchScalarGridSpec(num_scalar_prefetch=1, grid=(N_CHIPS,), in_specs=in_specs,
                                               out_specs=out_specs),
        out_shape=[SDS(t.shape, BF16) for t in gots],
        compiler_params=_cparams(dimension_semantics=("arbitrary",)),
    )(core.reshape(1).astype(jnp.int32), *gs, *gots)


def _chip_sum(qs, name):
    n = len(qs)

    def kern(*refs):
        for i in range(n):
            acc = refs[i][0].astype(F32)
            for s in range(1, N_CHIPS):
                acc = acc + refs[i][s].astype(F32)
            refs[n + i][...] = acc

    in_specs = [pl.BlockSpec((N_CHIPS, q.shape[1] // 2, q.shape[2]), lambda j: (0, j, 0)) for q in qs]
    out_specs = [pl.BlockSpec((q.shape[1] // 2, q.shape[2]), lambda j: (j, 0)) for q in qs]
    return _pc(kern, name, (2,), in_specs, out_specs, [SDS(q.shape[1:], F32) for q in qs])(*qs)


def _adam_math(w_, g_, m_, v_):
    m_ = ADAM_B1 * m_ + (1.0 - ADAM_B1) * g_
    v_ = ADAM_B2 * v_ + (1.0 - ADAM_B2) * jnp.square(g_)
    m_hat = m_ / (1.0 - ADAM_B1 ** ADAM_STEP)
    v_hat = v_ / (1.0 - ADAM_B2 ** ADAM_STEP)
    return -ADAM_LR * (m_hat / (jnp.sqrt(v_hat) + ADAM_EPS) + ADAM_WD * w_), m_, v_


def _adamw(w, g, m, v, name):
    rows, cols = w.shape
    tr = rows
    for cand in (256, 128, 64, 32, 16, 8):
        if rows % cand == 0 and rows > cand:
            tr = cand
            break

    def kern(w_ref, g_ref, m_ref, v_ref, d_ref, nm_ref, nv_ref):
        d_ref[...], nm_ref[...], nv_ref[...] = _adam_math(w_ref[...], g_ref[...], m_ref[...], v_ref[...])

    spec = pl.BlockSpec((tr, cols), lambda i: (i, 0))
    return _pc(kern, name, (rows // tr,), [spec] * 4, [spec] * 3, [SDS(w.shape, F32)] * 3)(w, g, m, v)


def _adamw_rows1(w, g, m, v, name):
    rows, _, cols = w.shape
    tr = next(t for t in (203, 174, 128, 64, 42, 32, 29, 16, 8, 7, 6, 4, 3, 2, 1) if rows % t == 0)

    def kern(w_ref, g_ref, m_ref, v_ref, go_ref, d_ref, nm_ref, nv_ref):
        g_ = g_ref[...]
        go_ref[...] = g_
        d_ref[...], nm_ref[...], nv_ref[...] = _adam_math(w_ref[...], g_, m_ref[...], v_ref[...])

    spec = pl.BlockSpec((tr, 1, cols), lambda i: (i, 0, 0))
    return _pc(kern, name, (rows // tr,), [spec] * 4, [spec] * 4, [SDS(w.shape, F32)] * 4)(w, g, m, v)


def _adamw_big(w, mine, theirs, m, v, core, name):
    _, rows, cols = w.shape
    half = rows // 2
    tr = next(t for t in (256, 176, 128, 64, 32, 16, 8) if half % t == 0)
    nbh = half // tr

    def kern(c_ref, w_ref, a_ref, b_ref, m_ref, v_ref, g_ref, d_ref, nm_ref, nv_ref):
        g_ = jnp.where(pl.program_id(0) // nbh == c_ref[0], a_ref[...], b_ref[...])
        g_ref[0] = g_
        d_ref[0], nm_ref[0], nv_ref[0] = _adam_math(w_ref[0], g_, m_ref[0], v_ref[0])

    full = pl.BlockSpec((1, tr, cols), lambda i, c_ref: (0, i, 0))
    part = pl.BlockSpec((tr, cols), lambda i, c_ref: (i % nbh, 0))
    return pl.pallas_call(
        kern, name=name,
        grid_spec=pltpu.PrefetchScalarGridSpec(num_scalar_prefetch=1, grid=(rows // tr,),
                                               in_specs=[full, part, part, full, full], out_specs=[full] * 4),
        out_shape=[SDS(w.shape, F32)] * 4,
        compiler_params=_cparams(dimension_semantics=("arbitrary",)),
    )(core.reshape(1).astype(jnp.int32), w, mine, theirs, m, v)


_WEIGHT_NAMES = ("attn_norm", "w_in", "dn_conv", "dn_a_log", "dn_dt_bias", "dn_out_norm", "swa_q_norm", "swa_k_norm",
                 "swa_sinks", "rel_bias", "w_branch_dn", "w_branch_swa", "w_out", "ffn_norm", "w_gate", "w_up",
                 "w_down")
_CONV_SH = QKVW // N_CHIPS


def kernel(x, attn_norm, w_in, dn_conv, dn_a_log, dn_dt_bias, dn_out_norm, swa_q_norm, swa_k_norm, swa_sinks, rel_bias, w_branch_dn, w_branch_swa, w_out, ffn_norm, w_gate, w_up, w_down, loss_target, m_attn_norm, m_w_in, m_dn_conv, m_dn_a_log, m_dn_dt_bias, m_dn_out_norm, m_swa_q_norm, m_swa_k_norm, m_swa_sinks, m_rel_bias, m_w_branch_dn, m_w_branch_swa, m_w_out, m_ffn_norm, m_w_gate, m_w_up, m_w_down, v_attn_norm, v_w_in, v_dn_conv, v_dn_a_log, v_dn_dt_bias, v_dn_out_norm, v_swa_q_norm, v_swa_k_norm, v_swa_sinks, v_rel_bias, v_w_branch_dn, v_w_branch_swa, v_w_out, v_ffn_norm, v_w_gate, v_w_up, v_w_down):
    w = dict(attn_norm=attn_norm, w_in=w_in, dn_conv=dn_conv, dn_a_log=dn_a_log, dn_dt_bias=dn_dt_bias,
             dn_out_norm=dn_out_norm, swa_q_norm=swa_q_norm, swa_k_norm=swa_k_norm, swa_sinks=swa_sinks,
             rel_bias=rel_bias, w_branch_dn=w_branch_dn, w_branch_swa=w_branch_swa, w_out=w_out, ffn_norm=ffn_norm,
             w_gate=w_gate, w_up=w_up, w_down=w_down)
    m = dict(attn_norm=m_attn_norm, w_in=m_w_in, dn_conv=m_dn_conv, dn_a_log=m_dn_a_log, dn_dt_bias=m_dn_dt_bias,
             dn_out_norm=m_dn_out_norm, swa_q_norm=m_swa_q_norm, swa_k_norm=m_swa_k_norm, swa_sinks=m_swa_sinks,
             rel_bias=m_rel_bias, w_branch_dn=m_w_branch_dn, w_branch_swa=m_w_branch_swa, w_out=m_w_out,
             ffn_norm=m_ffn_norm, w_gate=m_w_gate, w_up=m_w_up, w_down=m_w_down)
    v = dict(attn_norm=v_attn_norm, w_in=v_w_in, dn_conv=v_dn_conv, dn_a_log=v_dn_a_log, dn_dt_bias=v_dn_dt_bias,
             dn_out_norm=v_dn_out_norm, swa_q_norm=v_swa_q_norm, swa_k_norm=v_swa_k_norm, swa_sinks=v_swa_sinks,
             rel_bias=v_rel_bias, w_branch_dn=v_w_branch_dn, w_branch_swa=v_w_branch_swa, w_out=v_w_out,
             ffn_norm=v_ffn_norm, w_gate=v_w_gate, w_up=v_w_up, w_down=v_w_down)
    shapes = {n: w[n].shape for n in _WEIGHT_NAMES}

    def two_d(a):
        return a.reshape(a.shape[-2], a.shape[-1]) if a.ndim == 3 else a

    core = lax.axis_index("c")
    chip = 2 * lax.axis_index("x") + lax.axis_index("y")
    small_shapes = {n: two_d(w[n]).shape for n, _ in _SMALL}
    small_shapes["dn_conv"] = (CONV, QKVW)

    conv_loc = two_d(w["dn_conv"])
    conv_part = lax.dynamic_update_slice(jnp.zeros((CONV, QKVW), F32), jnp.where(core == 0, conv_loc, 0.0),
                                         (0, chip * _CONV_SH))
    conv_full = _all_sum_small(conv_part.reshape(CONV * QKVW // 128, 128), "gather_conv").reshape(CONV, QKVW)

    flipped = ("w_gate", "w_up")

    def natural(a, n):
        return a.transpose(0, 2, 1) if n in flipped else a

    w_bf = [two_d(natural(w[n], n).astype(BF16)) for n in _BIG_NAMES]
    (w_in_g,) = _gather_weights(w_bf[:1], chip)
    windows = _gather_windows(w_bf[1:])
    after_sync = w_in_g[0, :8, :128].astype(F32) + conv_full[0:1, :128]
    send_sems, recv_sems, w_thru, l_thru, token = _split_start(
        "gather_start", w_bf[1:], _own_slot(w_bf[1:], chip), after_sync, windows)

    def late(after):
        lands = _split_wait("gather_wait", w_thru, l_thru, send_sems, recv_sems, after, windows)
        g = dict(zip(_BIG_NAMES[1:], _sibling_fill(lands)))
        return dict(wa=g["w_branch_dn"], wb=g["w_branch_swa"], w_out=g["w_out"].reshape(D, D), wg=g["w_gate"],
                    wu=g["w_up"], wd=g["w_down"])

    wts = dict(w_in_p=_w_in_to_padded(w_in_g), dn_conv=conv_full, late=late)
    for n, _ in _SMALL[:-1]:
        wts[n] = two_d(w[n])
    wts["attn_norm"] = wts["attn_norm"] + token[0:1, 0:1]

    early = {}

    ffn = {}

    def send_ffn(grads):
        gs = [grads["w_gate"], grads["w_up"], grads["w_down"]]
        lands = [lax.empty((N_CHIPS, g.shape[1] // 2, g.shape[2]), g.dtype) for g in gs]
        ffn["sems"], ffn["recv"], ffn["src"], ffn["land"], tok = _split_start(
            "swap_ffn_start", gs, lands, gs[0][0, :8, :128], _swap_windows(gs), _sibling_peer, 1)
        return tok

    def send_early(grads):
        small = [grads["w_branch_dn"], grads["w_branch_swa"], grads["w_out"].reshape(N_CHIPS, CSH, D)]
        big = [grads["w_gate"], grads["w_up"], grads["w_down"]]
        big, got_big = _split_wait("swap_ffn_wait", ffn["src"], ffn["land"], ffn["sems"], ffn["recv"], small[0],
                                   _swap_windows(big), _sibling_peer, with_sources=True)
        gots = list(_swap_halves(small, "swap_halves_early")) + list(got_big)
        parts = _pair_sum(small + list(big), gots, core, "pair_sum_early")
        own = [lax.dynamic_index_in_dim(p, chip, axis=0, keepdims=False) for p in parts]
        early["sems"], early["recv"], early["src"], early["land"], tok = _split_start(
            "exchange_start", parts, _own_slot(own, chip), parts[0][0, :8, :128], _exchange_windows())
        return tok

    last = {}

    def send_in(g_in_p):
        g_in = [_padded_to_w_in(g_in_p)]
        parts = _pair_sum(g_in, _swap_halves(g_in, "swap_halves_in"), core, "pair_sum_in")
        own = [lax.dynamic_index_in_dim(p, chip, axis=0, keepdims=False) for p in parts]
        last["sems"], last["recv"], last["src"], last["land"], tok = _split_start(
            "exchange_in_start", parts, _own_slot(own, chip), parts[0][0, :8, :128], _exchange_windows())
        return tok

    wts["send_ffn"] = send_ffn
    wts["send_early"] = send_early
    wts["send_in"] = send_in
    loss_sum, grad_x, grads = _local_step(x[0], loss_target[0], wts)

    small_sum = _all_sum_small(_pack_small(grads, loss_sum), "all_sum_small")
    loss = small_sum.reshape(-1)[_LOSS_OFF]
    g_small = _unpack_small(small_sum, small_shapes)

    q_early = _split_wait("exchange_wait", early["src"], early["land"], early["sems"], early["recv"], small_sum,
                          _exchange_windows())
    red_early = _chip_sum(list(q_early), "chip_sum_early")
    their_early = _swap_reduced(red_early, "swap_reduced_early")
    g_out, d_out, m_out, v_out = {}, {}, {}, {}
    for n, mine, other in zip(_BIG_NAMES[1:], red_early, their_early):
        res = _adamw_big(natural(w[n], n), mine, other, natural(m[n], n), natural(v[n], n), core, "adamw_" + n)
        g_out[n], d_out[n], m_out[n], v_out[n] = (natural(t, n) for t in res)

    q_in = _split_wait("exchange_in_wait", last["src"], last["land"], last["sems"], last["recv"],
                       d_out[_BIG_NAMES[-1]], _exchange_windows())
    reduced = _chip_sum(list(q_in), "chip_sum_in")
    theirs = _swap_reduced(reduced, "swap_reduced_in")

    def rows1(a):
        return a.transpose(2, 0, 1)

    def unrows1(a):
        return a.transpose(1, 2, 0)

    g_in_blk = jnp.concatenate([jnp.where(core == 0, reduced[0], theirs[0]),
                                jnp.where(core == 0, theirs[0], reduced[0])], axis=0)
    g_in_r = rows1(g_in_blk[None])
    res = _adamw_rows1(rows1(w["w_in"]), g_in_r, rows1(m["w_in"]), rows1(v["w_in"]), "adamw_w_in")
    g_out["w_in"], d_out["w_in"], m_out["w_in"], v_out["w_in"] = (unrows1(t) for t in res)
    g_conv = lax.dynamic_slice(g_small["dn_conv"], (0, chip * _CONV_SH), (CONV, _CONV_SH))
    g_out["dn_conv"] = g_conv.reshape(shapes["dn_conv"])
    d_, m_, v_ = _adamw(conv_loc, g_conv, two_d(m["dn_conv"]), two_d(v["dn_conv"]), "adamw_dn_conv")
    d_out["dn_conv"], m_out["dn_conv"], v_out["dn_conv"] = (t.reshape(shapes["dn_conv"]) for t in (d_, m_, v_))

    def packed(src):
        vals = {n: src[n] for n, _ in _SMALL[:-1]}
        vals["dn_conv"] = jnp.zeros((CONV * QKVW,), F32)
        return _pack_small(vals)

    d_s, m_s, v_s = _adamw(packed(w), small_sum, packed(m), packed(v), "adamw_small")
    d_small, m_small, v_small = (_unpack_small(t, small_shapes) for t in (d_s, m_s, v_s))
    for n, _ in _SMALL[:-1]:
        g_out[n] = g_small[n].reshape(shapes[n])
        d_out[n], m_out[n], v_out[n] = (t[n].reshape(shapes[n]) for t in (d_small, m_small, v_small))

    return (loss, grad_x[None], *[g_out[n] for n in _WEIGHT_NAMES], *[d_out[n] for n in _WEIGHT_NAMES],
            *[m_out[n] for n in _WEIGHT_NAMES], *[v_out[n] for n in _WEIGHT_NAMES])
```

```python
import functools
import math

import numpy as np
import jax
import jax.numpy as jnp
from jax import lax
from jax.experimental import pallas as pl
from jax.experimental.pallas import tpu as pltpu

F32 = jnp.float32
BF16 = jnp.bfloat16
SDS = jax.ShapeDtypeStruct

D = 1024
DN_H = 4
DH = 128
DNW = DN_H * DH
QKVW = 3 * DNW
CONV = 4
CHUNK = 64
SWA_H = 8
SWA_KV = 2
SWA_G = SWA_H // SWA_KV
SWA_D = 64
SWAW = SWA_H * SWA_D
SWAKW = SWA_KV * SWA_D
BLK = 128
NBUCKET = 32
MAXDIST = 128
DFF = 2816
D_IN = QKVW + DNW + 2 * DN_H + SWAW + 2 * SWAKW + 2 * D
EPS = 1e-6
NEG = -1e30

ADAM_LR = 0.001
ADAM_B1 = 0.9
ADAM_B2 = 0.999
ADAM_EPS = 1e-08
ADAM_WD = 0.01
ADAM_STEP = 10

C_QKV, C_Z, C_GATE, C_SQ, C_SK, C_SV, C_BA = 0, 1536, 2048, 4096, 4608, 4736, 4864
PW = 5120
_ORIG_PIECES = (
    (0, QKVW, C_QKV),
    (QKVW, DNW, C_Z),
    (QKVW + DNW, 2 * DN_H, C_BA),
    (QKVW + DNW + 2 * DN_H, SWAW, C_SQ),
    (QKVW + DNW + 2 * DN_H + SWAW, SWAKW, C_SK),
    (QKVW + DNW + 2 * DN_H + SWAW + SWAKW, SWAKW, C_SV),
    (QKVW + DNW + 2 * DN_H + SWAW + 2 * SWAKW, 2 * D, C_GATE),
)

N_CHIPS = 4
FSH = DFF // N_CHIPS
CSH = D // N_CHIPS
VMEM_LIMIT = 48 * 1024 * 1024
MESH = pl.DeviceIdType.MESH

_BIG = (
    ("w_in", D, D_IN // N_CHIPS),
    ("w_branch_dn", DNW, CSH),
    ("w_branch_swa", SWAW, CSH),
    ("w_out", CSH, D),
    ("w_gate", FSH, D),
    ("w_up", FSH, D),
    ("w_down", FSH, D),
)
_BIG_NAMES = tuple(n for n, _, _ in _BIG)

_SMALL = (
    ("attn_norm", D), ("ffn_norm", D), ("dn_out_norm", DH), ("swa_q_norm", SWA_D), ("swa_k_norm", SWA_D),
    ("swa_sinks", SWA_H), ("dn_a_log", DN_H), ("dn_dt_bias", DN_H), ("rel_bias", NBUCKET * SWA_H),
    ("dn_conv", CONV * QKVW),
)
_SMALL_OFF = {}
_o = 0
for _n, _s in _SMALL:
    _SMALL_OFF[_n] = (_o, _s)
    _o += _s
_LOSS_OFF = _o
_SMALL_ROWS = -(-(_o + 1) // (8 * 128)) * 8


def _cparams(**kw):
    return pltpu.CompilerParams(vmem_limit_bytes=VMEM_LIMIT, **kw)


_DIMS = {
    "nn": (((1,), (0,)), ((), ())),
    "nt": (((1,), (1,)), ((), ())),
    "tn": (((0,), (0,)), ((), ())),
    "bnn": (((2,), (1,)), ((0,), (0,))),
    "bnt": (((2,), (2,)), ((0,), (0,))),
    "btn": (((1,), (1,)), ((0,), (0,))),
}


def _raw_dot(a, b, kind, exact):
    if exact:
        prec = lax.Precision.HIGH if exact == "x3" else lax.Precision.HIGHEST
        return lax.dot_general(a, b, _DIMS[kind], precision=prec, preferred_element_type=F32)
    return lax.dot_general(a.astype(BF16), b.astype(BF16), _DIMS[kind], preferred_element_type=F32)


@functools.partial(jax.custom_vjp, nondiff_argnums=(2, 3))
def _dot(a, b, kind, exact):
    return _raw_dot(a, b, kind, exact)


def _dot_fwd(a, b, kind, exact):
    return _raw_dot(a, b, kind, exact), (a, b)


def _dot_bwd(kind, exact, res, g):
    a, b = res
    pre = kind[:-2]
    nn, nt, tn = pre + "nn", pre + "nt", pre + "tn"
    if kind == nn:
        return _dot(g, b, nt, exact), _dot(a, g, tn, exact)
    if kind == nt:
        return _dot(g, b, nn, exact), _dot(g, a, tn, exact)
    return _dot(b, g, nt, exact), _dot(a, g, nn, exact)


_dot.defvjp(_dot_fwd, _dot_bwd)


def _silu(x):
    return x * jax.nn.sigmoid(x)


def _f_rms(x, gain):
    return x * lax.rsqrt(jnp.mean(x * x, axis=-1, keepdims=True) + EPS) * gain


def _f_dn_pre(xs0, xs1, xs2, xs3, ba, cw, alog, dtb):
    rows = xs0.shape[0]
    c = xs0 * cw[0:1] + xs1 * cw[1:2] + xs2 * cw[2:3] + xs3 * cw[3:4]
    qkv = _silu(c)
    qs, ks, bbs, gbs = [], [], [], []
    for h in range(DN_H):
        qh = qkv[:, h * DH:(h + 1) * DH]
        kh = qkv[:, DNW + h * DH:DNW + (h + 1) * DH]
        qs.append(qh * lax.rsqrt(jnp.sum(qh * qh, axis=-1, keepdims=True) + EPS) * (DH ** -0.5))
        ks.append(kh * lax.rsqrt(jnp.sum(kh * kh, axis=-1, keepdims=True) + EPS))
        beta = jax.nn.sigmoid(ba[:, h:h + 1])
        ar = ba[:, DN_H + h:DN_H + h + 1] + dtb[:, h:h + 1]
        softplus = jnp.maximum(ar, 0.0) + jnp.log1p(jnp.exp(-jnp.abs(ar)))
        g = -jnp.exp(alog[:, h:h + 1]) * softplus
        bbs.append(jnp.broadcast_to(beta, (rows, DH)))
        gbs.append(jnp.broadcast_to(g, (rows, DH)))
    return (jnp.concatenate(qs, axis=1), jnp.concatenate(ks, axis=1), qkv[:, 2 * DNW:],
            jnp.concatenate(bbs, axis=1), jnp.concatenate(gbs, axis=1))


def _f_dn_post(o, z, gain):
    ys = []
    for h in range(DN_H):
        oh = o[:, h * DH:(h + 1) * DH]
        zh = z[:, h * DH:(h + 1) * DH]
        ys.append(oh * lax.rsqrt(jnp.mean(oh * oh, axis=-1, keepdims=True) + EPS) * gain * _silu(zh))
    return jnp.concatenate(ys, axis=1)


def _f_merge(pa, pb, ga, gb):
    return jax.nn.sigmoid(ga) * pa + jax.nn.sigmoid(gb) * pb


@jax.custom_vjp
def _f_swiglu(g, u):
    return _silu(g) * u


def _f_swiglu_fwd(g, u):
    return _silu(g) * u, (g, u)


def _f_swiglu_bwd(res, d):
    g, u = res
    s = jax.nn.sigmoid(g)
    act = g * s
    return d * u * (s + act * (1.0 - s)), d * act


_f_swiglu.defvjp(_f_swiglu_fwd, _f_swiglu_bwd)


@jax.custom_vjp
def _unit_lower_inverse(a):
    c = a.shape[-1]
    eye = (lax.broadcasted_iota(jnp.int32, a.shape, 1) == lax.broadcasted_iota(jnp.int32, a.shape, 2)).astype(F32)
    p = -a
    t = eye + p
    for _ in range(max(c.bit_length() - 2, 0)):
        p = _raw_dot(p, p, "bnn", "x3")
        t = t + _raw_dot(t, p, "bnn", "x3")
    return t


def _unit_lower_inverse_fwd(a):
    t = _unit_lower_inverse(a)
    return t, t


def _unit_lower_inverse_bwd(t, g):
    return (-_raw_dot(_raw_dot(t, g, "btn", "x3"), t, "bnt", "x3"),)


_unit_lower_inverse.defvjp(_unit_lower_inverse_fwd, _unit_lower_inverse_bwd)


@jax.custom_vjp
def _known_inverse(a, t):
    return t


def _known_inverse_fwd(a, t):
    return t, t


def _known_inverse_bwd(t, g):
    return _unit_lower_inverse_bwd(t, g)[0], jnp.zeros_like(t)


_known_inverse.defvjp(_known_inverse_fwd, _known_inverse_bwd)


def _f_chunk(q, k, v, gb, bb, s, t_known=None, with_t=False):
    c = CHUNK
    nh = q.shape[0]
    ii = lax.broadcasted_iota(jnp.int32, (nh, c, c), 1)
    jj = lax.broadcasted_iota(jnp.int32, (nh, c, c), 2)
    incl = ii >= jj
    strict = ii > jj
    eye = (ii == jj).astype(F32)
    gcb = _dot(incl.astype(F32), gb, "bnn", "x3")
    lane0 = (lax.broadcasted_iota(jnp.int32, (nh, c, DH), 2) == 0).astype(F32)
    gcol = gcb[:, :, :c]
    grow = _dot(lane0, gcb, "bnt", "x3")
    decay = jnp.where(incl, jnp.exp(jnp.where(incl, gcol - grow, 0.0)), 0.0)
    kb = k * bb
    vb = v * bb
    a = jnp.where(strict, _dot(kb, k, "bnt", False) * decay, 0.0)
    t = _unit_lower_inverse(a) if t_known is None else _known_inverse(a, t_known)
    eg = jnp.exp(gcb)
    u = _dot(t, vb, "bnn", "x3")
    w = _dot(t, kb * eg, "bnn", "x3")
    qk = jnp.where(incl, _dot(q, k, "bnt", False) * decay, 0.0)
    qe = q * eg
    glast = gcb[:, c - 1:c, :]
    k_dec = k * jnp.exp(glast - gcb)
    e_last = jnp.exp(glast)
    outs = []
    for g in range(nh // DN_H):
        sl = slice(g * DN_H, (g + 1) * DN_H)
        v_new = u[sl] - _dot(w[sl], s, "bnn", False)
        outs.append(_dot(qe[sl], s, "bnn", False) + _dot(qk[sl], v_new, "bnn", False))
        s = s * e_last[sl] + _dot(k_dec[sl], v_new, "btn", False)
    o = jnp.concatenate(outs, axis=0)
    return (o, s, t) if with_t else (o, s)


def _f_swa(q8, kp, kc, vp, vc, bias8, qg, kg, sink, mask):
    kb = jnp.concatenate([kp, kc], axis=1)
    vb = jnp.concatenate([vp, vc], axis=1)
    kn = kb * lax.rsqrt(jnp.mean(kb * kb, axis=-1, keepdims=True) + EPS) * kg

    def rows(per_head):
        return jnp.stack([jnp.concatenate([per_head(kv, g) for g in range(SWA_G)], axis=0)
                          for kv in range(SWA_KV)], axis=0)

    qq = rows(lambda kv, g: q8[kv * SWA_G + g])
    qn = qq * lax.rsqrt(jnp.mean(qq * qq, axis=-1, keepdims=True) + EPS) * qg * (SWA_D ** -0.5)
    lg = _dot(qn, kn, "bnt", False) + rows(lambda kv, g: bias8[kv * SWA_G + g])
    lg = jnp.where(rows(lambda kv, g: mask), lg, NEG)
    sk = rows(lambda kv, g: jnp.broadcast_to(sink[kv][:, g:g + 1], (BLK, 1)))
    m = lax.stop_gradient(jnp.maximum(jnp.max(lg, axis=-1, keepdims=True), sk))
    p = jnp.exp(lg - m)
    den = jnp.sum(p, axis=-1, keepdims=True) + jnp.exp(sk - m)
    out = _dot(p * (1.0 / den), vb, "bnn", False)
    return jnp.stack([out[kv, g * BLK:(g + 1) * BLK] for kv in range(SWA_KV) for g in range(SWA_G)], axis=0)


def _bdot(a, b, kind="nn"):
    return lax.dot_general(a.astype(BF16), b.astype(BF16), _DIMS[kind], preferred_element_type=F32)


def _pc(kern, name, grid, in_specs, out_specs, out_shape, scratch=()):
    return pl.pallas_call(
        kern, name=name, grid=grid, in_specs=in_specs, out_specs=out_specs, out_shape=out_shape,
        scratch_shapes=list(scratch), compiler_params=_cparams(dimension_semantics=("arbitrary",) * len(grid)))


def _mm(a, b, kind, out_dtype, tm, tn, name):
    if kind == "tn":
        k, m = a.shape
    else:
        m, k = a.shape
    n = b.shape[0] if kind == "nt" else b.shape[1]
    tm, tn = min(tm, m), min(tn, n)
    assert m % tm == 0 and n % tn == 0, (name, a.shape, b.shape, tm, tn)

    def kern(a_ref, b_ref, o_ref):
        o_ref[...] = _bdot(a_ref[...], b_ref[...], kind).astype(o_ref.dtype)

    a_spec = pl.BlockSpec((k, tm), lambda i, j: (0, i)) if kind == "tn" else pl.BlockSpec((tm, k), lambda i, j: (i, 0))
    b_spec = pl.BlockSpec((tn, k), lambda i, j: (j, 0)) if kind == "nt" else pl.BlockSpec((k, tn), lambda i, j: (0, j))
    return _pc(kern, name, (m // tm, n // tn), [a_spec, b_spec], pl.BlockSpec((tm, tn), lambda i, j: (i, j)),
               SDS((m, n), out_dtype))(a, b)


def _rows(body, name, m, tm, row_ins, full_ins, row_outs, acc_outs=()):
    n_r, n_f, n_o, n_a = len(row_ins), len(full_ins), len(row_outs), len(acc_outs)
    assert m % tm == 0

    def kern(*refs):
        r = refs[:n_r]
        f = refs[n_r:n_r + n_f]
        o = refs[n_r + n_f:n_r + n_f + n_o]
        acc = refs[n_r + n_f + n_o:]
        outs, sums = body([x[...] for x in r], [x[...] for x in f])
        for ref, val in zip(o, outs, strict=True):
            ref[...] = val.astype(ref.dtype)
        if n_a:
            @pl.when(pl.program_id(0) == 0)
            def _():
                for ref in acc:
                    ref[...] = jnp.zeros(ref.shape, F32)

            for ref, val in zip(acc, sums, strict=True):
                ref[...] += val

    in_specs = [pl.BlockSpec((tm, w), functools.partial(lambda i, cb: (i, cb), cb=cb)) for _, w, cb in row_ins]
    in_specs += [pl.BlockSpec(x.shape, lambda i: (0, 0)) for x in full_ins]
    out_specs = [pl.BlockSpec((tm, w), lambda i: (i, 0)) for w, _ in row_outs]
    out_specs += [pl.BlockSpec(s, lambda i: (0, 0)) for s in acc_outs]
    out_shape = [SDS((m, w), dt) for w, dt in row_outs]
    out_shape += [SDS(s, F32) for s in acc_outs]
    return _pc(kern, name, (m // tm,), in_specs, out_specs, out_shape)(*[x for x, _, _ in row_ins], *full_ins)


def _whole(x):
    return (x, x.shape[1], 0)


def _resident(shape):
    return pl.BlockSpec(shape, lambda i: (0,) * len(shape), pipeline_mode=pl.Buffered(1))


def _row_pieces(tm, piece):
    piece = min(piece, tm)
    return [slice(r, r + piece) for r in range(0, tm, piece)]


def _zero_first(refs):
    @pl.when(pl.program_id(0) == 0)
    def _():
        for ref in refs:
            ref[...] = jnp.zeros(ref.shape, F32)


GROUP = 4


def _heads(ref):
    return jnp.stack([ref[g * CHUNK:(g + 1) * CHUNK, h * DH:(h + 1) * DH]
                      for g in range(GROUP) for h in range(DN_H)], axis=0)


def _unheads(ref, val):
    for g in range(GROUP):
        for h in range(DN_H):
            ref[g * CHUNK:(g + 1) * CHUNK, h * DH:(h + 1) * DH] = val[g * DN_H + h]


def _dn_chunks_fwd(q, k, v, gb, bb):
    s_len = q.shape[0]
    ng = s_len // (GROUP * CHUNK)

    def kern(q_ref, k_ref, v_ref, g_ref, b_ref, o_ref, sall_ref, t_ref, state):
        _zero_first([state])
        s = state[...]
        sall_ref[0] = s
        o, s_new, t = _f_chunk(*[_heads(r) for r in (q_ref, k_ref, v_ref, g_ref, b_ref)], s, with_t=True)
        _unheads(o_ref, o)
        t_ref[0] = t
        state[...] = s_new

    blk = pl.BlockSpec((GROUP * CHUNK, DNW), lambda c: (c, 0))
    return _pc(kern, "dn_chunks_fwd", (ng,), [blk] * 5,
               [blk, pl.BlockSpec((1, DN_H, DH, DH), lambda c: (c, 0, 0, 0)),
                pl.BlockSpec((1, GROUP * DN_H, CHUNK, CHUNK), lambda c: (c, 0, 0, 0))],
               [SDS((s_len, DNW), F32), SDS((ng, DN_H, DH, DH), F32), SDS((ng, GROUP * DN_H, CHUNK, CHUNK), F32)],
               scratch=[pltpu.VMEM((DN_H, DH, DH), F32)])(q, k, v, gb, bb)


def _dn_chunks_bwd(q, k, v, gb, bb, s_all, t_all, d_o):
    s_len = q.shape[0]
    ng = s_len // (GROUP * CHUNK)

    def kern(q_ref, k_ref, v_ref, g_ref, b_ref, sall_ref, t_ref, do_ref, dq_ref, dk_ref, dv_ref, dg_ref, db_ref,
             dstate):
        _zero_first([dstate])
        fn = functools.partial(_f_chunk, t_known=t_ref[0])
        _, vjp = jax.vjp(fn, *[_heads(r) for r in (q_ref, k_ref, v_ref, g_ref, b_ref)], sall_ref[0])
        *d_ins, ds = vjp((_heads(do_ref), dstate[...]))
        for ref, val in zip((dq_ref, dk_ref, dv_ref, dg_ref, db_ref), d_ins, strict=True):
            _unheads(ref, val)
        dstate[...] = ds

    blk = pl.BlockSpec((GROUP * CHUNK, DNW), lambda c: (ng - 1 - c, 0))
    return _pc(kern, "dn_chunks_bwd", (ng,),
               [blk] * 5 + [pl.BlockSpec((1, DN_H, DH, DH), lambda c: (ng - 1 - c, 0, 0, 0)),
                            pl.BlockSpec((1, GROUP * DN_H, CHUNK, CHUNK), lambda c: (ng - 1 - c, 0, 0, 0)), blk],
               [blk] * 5, [SDS((s_len, DNW), F32)] * 5,
               scratch=[pltpu.VMEM((DN_H, DH, DH), F32)])(q, k, v, gb, bb, s_all, t_all, d_o)


def _t5_bucket_table():
    qi = np.arange(BLK)[:, None]
    kj = np.arange(2 * BLK)[None, :]
    dist = BLK + qi - kj
    n = np.maximum(dist, 0)
    max_exact = NBUCKET // 2
    nf = np.maximum(n, 1).astype(np.float32)
    large = max_exact + (np.log(nf / np.float32(max_exact)) / np.float32(math.log(MAXDIST / max_exact))
                         * np.float32(NBUCKET - max_exact)).astype(np.int32)
    large = np.minimum(large, NBUCKET - 1)
    return np.where(n < max_exact, n, large)


def _bucket_onehot_t():
    table = _t5_bucket_table().reshape(-1)
    return (np.arange(NBUCKET)[:, None] == table[None, :]).astype(np.float32)


def _swa_mask(first):
    qi = lax.broadcasted_iota(jnp.int32, (BLK, 2 * BLK), 0)
    kj = lax.broadcasted_iota(jnp.int32, (BLK, 2 * BLK), 1)
    dist = BLK + qi - kj
    window = (dist >= 0) & (dist < BLK)
    return window & ((kj >= BLK) | jnp.logical_not(first))


def _bias_expand(rel_bias_t):
    onehot = jnp.asarray(_bucket_onehot_t())

    def kern(r_ref, oh_ref, o_ref):
        o_ref[...] = _raw_dot(r_ref[...], oh_ref[...], "nn", True)

    return pl.pallas_call(
        kern, name="bias_expand", out_shape=SDS((SWA_H, BLK * 2 * BLK), F32), compiler_params=_cparams(),
    )(rel_bias_t, onehot)


def _bias_reduce(d_bias_flat):
    onehot = jnp.asarray(_bucket_onehot_t())

    def kern(d_ref, oh_ref, o_ref):
        o_ref[...] = _raw_dot(d_ref[...], oh_ref[...], "nt", True)

    return pl.pallas_call(
        kern, name="bias_reduce", out_shape=SDS((SWA_H, NBUCKET), F32), compiler_params=_cparams(),
    )(d_bias_flat, onehot)


def _swa_specs(nb, rev):
    def blk(n):
        return (nb - 1 - n) if rev else n

    def before(n):
        return jnp.maximum(blk(n) - 1, 0)

    q_spec = pl.BlockSpec((BLK, SWAW), lambda n: (blk(n), C_SQ // SWAW))
    k_cur = pl.BlockSpec((BLK, SWAKW), lambda n: (blk(n), C_SK // SWAKW))
    k_prev = pl.BlockSpec((BLK, SWAKW), lambda n: (before(n), C_SK // SWAKW))
    v_cur = pl.BlockSpec((BLK, SWAKW), lambda n: (blk(n), C_SV // SWAKW))
    v_prev = pl.BlockSpec((BLK, SWAKW), lambda n: (before(n), C_SV // SWAKW))
    bias = pl.BlockSpec((SWA_H, BLK, 2 * BLK), lambda n: (0, 0, 0))
    gain = pl.BlockSpec((1, SWA_D), lambda n: (0, 0))
    sink = pl.BlockSpec((SWA_KV, 1, SWA_G), lambda n: (0, 0, 0))
    wide = pl.BlockSpec((BLK, SWAW), lambda n: (blk(n), 0))
    narrow = pl.BlockSpec((BLK, SWAKW), lambda n: (blk(n), 0))
    return [q_spec, k_prev, k_cur, v_prev, v_cur, bias, gain, gain, sink], wide, narrow


def _split_heads(x):
    return jnp.stack([x[:, h * SWA_D:(h + 1) * SWA_D] for h in range(x.shape[1] // SWA_D)], axis=0)


def _join_heads(x):
    return jnp.concatenate([x[h] for h in range(x.shape[0])], axis=1)


def _swa_fwd(proj, bias, qg, kg, sinks):
    s_len = proj.shape[0]
    nb = s_len // BLK
    in_specs, wide, _ = _swa_specs(nb, False)

    def kern(q_ref, kp_ref, kc_ref, vp_ref, vc_ref, b_ref, qg_ref, kg_ref, s_ref, o_ref):
        mask = _swa_mask(pl.program_id(0) == 0)
        o8 = _f_swa(*[_split_heads(r[...]) for r in (q_ref, kp_ref, kc_ref, vp_ref, vc_ref)], b_ref[...], qg_ref[...],
                    kg_ref[...], s_ref[...], mask)
        o_ref[...] = _join_heads(o8).astype(BF16)

    return _pc(kern, "swa_fwd", (nb,), in_specs, wide, SDS((s_len, SWAW), BF16))(
        proj, proj, proj, proj, proj, bias, qg, kg, sinks)


def _swa_bwd(proj, bias, qg, kg, sinks, d_out):
    s_len = proj.shape[0]
    nb = s_len // BLK
    in_specs, wide, narrow = _swa_specs(nb, True)

    def kern(q_ref, kp_ref, kc_ref, vp_ref, vc_ref, b_ref, qg_ref, kg_ref, s_ref, do_ref,
             dq_ref, dk_ref, dv_ref, db_ref, dqg_ref, dkg_ref, ds_ref, carry_k, carry_v):
        n = pl.program_id(0)
        mask = _swa_mask(n == nb - 1)
        _zero_first([carry_k, carry_v, db_ref, ds_ref, dqg_ref, dkg_ref])
        fn = functools.partial(_f_swa, mask=mask)
        _, vjp = jax.vjp(fn, *[_split_heads(r[...]) for r in (q_ref, kp_ref, kc_ref, vp_ref, vc_ref)], b_ref[...],
                         qg_ref[...], kg_ref[...], s_ref[...])
        dq, dkp, dkc, dvp, dvc, dbias, dqg, dkg, dsink = vjp(_split_heads(do_ref[...]))
        dq_ref[...] = _join_heads(dq).astype(BF16)
        dk_ref[...] = (_join_heads(dkc) + carry_k[...]).astype(BF16)
        dv_ref[...] = (_join_heads(dvc) + carry_v[...]).astype(BF16)
        carry_k[...] = _join_heads(dkp)
        carry_v[...] = _join_heads(dvp)
        db_ref[...] += dbias
        dqg_ref[...] += dqg
        dkg_ref[...] += dkg
        ds_ref[...] += dsink

    bias_spec, gain, sink = in_specs[5], in_specs[6], in_specs[8]
    return _pc(
        kern, "swa_bwd", (nb,), in_specs + [wide], [wide, narrow, narrow, bias_spec, gain, gain, sink],
        [SDS((s_len, SWAW), BF16), SDS((s_len, SWAKW), BF16), SDS((s_len, SWAKW), BF16),
         SDS((SWA_H, BLK, 2 * BLK), F32), SDS((1, SWA_D), F32), SDS((1, SWA_D), F32), SDS((SWA_KV, 1, SWA_G), F32)],
        scratch=[pltpu.VMEM((BLK, SWAKW), F32), pltpu.VMEM((BLK, SWAKW), F32)],
    )(proj, proj, proj, proj, proj, bias, qg, kg, sinks, d_out)


def _branch_merge(y_dn, y_swa, wa, wb, proj):
    s_len = y_dn.shape[0]
    tm = min(1024, s_len)

    def kern(ya_ref, yb_ref, wa_ref, wb_ref, ga_ref, gb_ref, pa_ref, pb_ref, m_ref):
        for rows in _row_pieces(tm, 128):
            pa = _bdot(ya_ref[rows, :], wa_ref[0])
            pb = _bdot(yb_ref[rows, :], wb_ref[0])
            pa_ref[rows, :] = pa.astype(BF16)
            pb_ref[rows, :] = pb.astype(BF16)
            m_ref[rows, :] = _f_merge(pa, pb, ga_ref[rows, :], gb_ref[rows, :]).astype(BF16)

    y_spec = pl.BlockSpec((tm, DNW), lambda i, s: (i, 0))
    w_spec = pl.BlockSpec((1, DNW, CSH), lambda i, s: (s, 0, 0))
    o_spec = pl.BlockSpec((tm, CSH), lambda i, s: (i, s))
    ga_spec = pl.BlockSpec((tm, CSH), lambda i, s: (i, C_GATE // CSH + s))
    gb_spec = pl.BlockSpec((tm, CSH), lambda i, s: (i, (C_GATE + D) // CSH + s))
    return _pc(kern, "branch_merge", (s_len // tm, N_CHIPS), [y_spec, y_spec, w_spec, w_spec, ga_spec, gb_spec],
               [o_spec] * 3, [SDS((s_len, D), BF16)] * 3,
               )(y_dn, y_swa, wa, wb, proj, proj)


def _in_proj(x, gain, w_in_p):
    s_len = x.shape[0]
    tm = min(512, s_len)

    def kern(x_ref, g_ref, w_ref, h_ref, p_ref):
        h = _f_rms(x_ref[...], g_ref[...]).astype(BF16)
        h_ref[...] = h
        p_ref[...] = _bdot(h, w_ref[...])

    row = pl.BlockSpec((tm, D), lambda i: (i, 0))
    return _pc(kern, "in_proj", (s_len // tm,),
               [row, pl.BlockSpec((1, D), lambda i: (0, 0)), _resident((D, PW))],
               [row, pl.BlockSpec((tm, PW), lambda i: (i, 0))],
               [SDS((s_len, D), BF16), SDS((s_len, PW), F32)])(x, gain, w_in_p)


def _out_proj(merged, w_out, x, gain):
    s_len = x.shape[0]
    tm = min(512, s_len)

    def kern(m_ref, w_ref, x_ref, g_ref, x1_ref, h2_ref):
        x1 = x_ref[...] + _bdot(m_ref[...], w_ref[...])
        x1_ref[...] = x1
        h2_ref[...] = _f_rms(x1, g_ref[...]).astype(BF16)

    row = pl.BlockSpec((tm, D), lambda i: (i, 0))
    return _pc(kern, "out_proj", (s_len // tm,),
               [row, _resident((D, D)), row, pl.BlockSpec((1, D), lambda i: (0, 0))],
               [row, row], [SDS((s_len, D), F32), SDS((s_len, D), BF16)])(merged, w_out, x, gain)


def _ffn_up(h2, wg, wu):
    s_len = h2.shape[0]
    tm = min(1024, s_len)

    def kern(h_ref, g_ref, u_ref, gt_ref, up_ref, act_ref):
        for rows in _row_pieces(tm, 256):
            h = h_ref[rows, :]
            g = _bdot(h, g_ref[0], "nt")
            u = _bdot(h, u_ref[0], "nt")
            gt_ref[0, rows, :] = g.astype(BF16)
            up_ref[0, rows, :] = u.astype(BF16)
            act_ref[0, rows, :] = _f_swiglu(g, u).astype(BF16)

    w_spec = pl.BlockSpec((1, FSH, D), lambda s, i: (s, 0, 0))
    o_spec = pl.BlockSpec((1, tm, FSH), lambda s, i: (s, i, 0))
    shape = (N_CHIPS, s_len, FSH)
    return _pc(kern, "ffn_up", (N_CHIPS, s_len // tm), [pl.BlockSpec((tm, D), lambda s, i: (i, 0)), w_spec, w_spec],
               [o_spec] * 3, [SDS(shape, BF16)] * 3)(h2, wg, wu)


def _ffn_down_loss(act, wd, x1, target):
    s_len = x1.shape[0]
    tm = min(512, s_len)

    def kern(a_ref, w_ref, x_ref, t_ref, dy_ref, dyb_ref, loss_ref):
        _zero_first([loss_ref])
        for rows in _row_pieces(tm, 128):
            y = x_ref[rows, :]
            for s in range(N_CHIPS):
                y = y + _bdot(a_ref[s, rows, :], w_ref[s])
            d = y - t_ref[rows, :]
            dy = d * (1.0 / D)
            dy_ref[rows, :] = dy
            dyb_ref[rows, :] = dy.astype(BF16)
            loss_ref[...] += jnp.sum(d * d).reshape(1, 1) * (0.5 / D)

    row = pl.BlockSpec((tm, D), lambda i: (i, 0))
    return _pc(kern, "ffn_down_loss", (s_len // tm,),
               [pl.BlockSpec((N_CHIPS, tm, FSH), lambda i: (0, i, 0)),
                _resident((N_CHIPS, FSH, D)), row, row],
               [row, row, pl.BlockSpec((1, 1), lambda i: (0, 0))],
               [SDS((s_len, D), F32), SDS((s_len, D), BF16), SDS((1, 1), F32)])(act, wd, x1, target)


def _ffn_dact(dy_b, wd, gt, up):
    s_len = dy_b.shape[0]
    tm = min(1024, s_len)

    def kern(dy_ref, w_ref, gt_ref, up_ref, dg_ref, du_ref):
        w = w_ref[0]
        for rows in _row_pieces(tm, 256):
            d_act = _bdot(dy_ref[rows, :], w, "nt")
            _, vjp = jax.vjp(_f_swiglu, gt_ref[0, rows, :].astype(F32), up_ref[0, rows, :].astype(F32))
            dg, du = vjp(d_act)
            dg_ref[0, rows, :] = dg.astype(BF16)
            du_ref[0, rows, :] = du.astype(BF16)

    a_spec = pl.BlockSpec((1, tm, FSH), lambda s, i: (s, i, 0))
    shape = (N_CHIPS, s_len, FSH)
    return _pc(kern, "ffn_dact", (N_CHIPS, s_len // tm),
               [pl.BlockSpec((tm, D), lambda s, i: (i, 0)), pl.BlockSpec((1, FSH, D), lambda s, i: (s, 0, 0)),
                a_spec, a_spec],
               [a_spec, a_spec], [SDS(shape, BF16), SDS(shape, BF16)])(dy_b, wd, gt, up)


def _gw_ffn(lhs, rhs, name):
    s_len = rhs.shape[0]
    n = len(lhs)
    tn = 512

    def kern(*refs):
        g = refs[n][...]
        for i in range(n):
            refs[n + 1 + i][0] = _bdot(refs[i][0], g, "tn").astype(BF16)

    a_spec = pl.BlockSpec((1, s_len, FSH), lambda s, j: (s, 0, 0))
    o_spec = pl.BlockSpec((1, FSH, tn), lambda s, j: (s, 0, j))
    return _pc(kern, name, (N_CHIPS, D // tn), [a_spec] * n + [pl.BlockSpec((s_len, tn), lambda s, j: (0, j))],
               [o_spec] * n, [SDS((N_CHIPS, FSH, D), BF16)] * n)(*lhs, rhs)


def _ffn_dh2(d_gt, d_up, wg, wu, x1, dy, gain):
    s_len = x1.shape[0]
    tm = min(512, s_len)

    def kern(dg_ref, du_ref, wg_ref, wu_ref, x_ref, dy_ref, g_ref, dx_ref, dxb_ref, dgain_ref):
        _zero_first([dgain_ref])
        dh2 = jnp.zeros((tm, D), F32)
        for s in range(N_CHIPS):
            dh2 = dh2 + _bdot(dg_ref[s], wg_ref[s]) + _bdot(du_ref[s], wu_ref[s])
        _, vjp = jax.vjp(_f_rms, x_ref[...], g_ref[...])
        dx, dgain = vjp(dh2)
        dx1 = dx + dy_ref[...]
        dx_ref[...] = dx1
        dxb_ref[...] = dx1.astype(BF16)
        dgain_ref[...] += dgain

    row = pl.BlockSpec((tm, D), lambda i: (i, 0))
    d_spec = pl.BlockSpec((N_CHIPS, tm, FSH), lambda i: (0, i, 0))
    w_spec = _resident((N_CHIPS, FSH, D))
    vec = pl.BlockSpec((1, D), lambda i: (0, 0))
    return _pc(kern, "ffn_dh2", (s_len // tm,), [d_spec, d_spec, w_spec, w_spec, row, row, vec],
               [row, row, vec], [SDS((s_len, D), F32), SDS((s_len, D), BF16), SDS((1, D), F32)],
               )(d_gt, d_up, wg, wu, x1, dy, gain)


def _merge_bwd(dx1_b, w_out, pa, pb, proj):
    s_len = dx1_b.shape[0]
    tm = min(512, s_len)

    def kern(dx_ref, w_ref, pa_ref, pb_ref, g_ref, dpa_ref, dpb_ref, dg_ref):
        dm = _bdot(dx_ref[...], w_ref[...], "nt")
        gates = g_ref[...]
        _, vjp = jax.vjp(_f_merge, pa_ref[...].astype(F32), pb_ref[...].astype(F32), gates[:, :D], gates[:, D:])
        dpa, dpb, dga, dgb = vjp(dm)
        dpa_ref[...] = dpa.astype(BF16)
        dpb_ref[...] = dpb.astype(BF16)
        dg_ref[:, :D] = dga.astype(BF16)
        dg_ref[:, D:] = dgb.astype(BF16)

    row = pl.BlockSpec((tm, D), lambda i: (i, 0))
    return _pc(kern, "merge_bwd", (s_len // tm,),
               [row, _resident((D, D)), row, row,
                pl.BlockSpec((tm, 2 * D), lambda i: (i, C_GATE // (2 * D)))],
               [row, row, pl.BlockSpec((tm, 2 * D), lambda i: (i, 0))],
               [SDS((s_len, D), BF16), SDS((s_len, D), BF16), SDS((s_len, 2 * D), BF16)],
               )(dx1_b, w_out, pa, pb, proj)


def _d_branch(d_pa, d_pb, wa, wb):
    s_len = d_pa.shape[0]
    tm = min(512, s_len)

    def kern(da_ref, db_ref, wa_ref, wb_ref, oa_ref, ob_ref):
        acc_a = jnp.zeros((tm, DNW), F32)
        acc_b = jnp.zeros((tm, SWAW), F32)
        for s in range(N_CHIPS):
            acc_a = acc_a + _bdot(da_ref[:, s * CSH:(s + 1) * CSH], wa_ref[s], "nt")
            acc_b = acc_b + _bdot(db_ref[:, s * CSH:(s + 1) * CSH], wb_ref[s], "nt")
        oa_ref[...] = acc_a
        ob_ref[...] = acc_b

    row = pl.BlockSpec((tm, D), lambda i: (i, 0))
    w_spec = pl.BlockSpec((N_CHIPS, DNW, CSH), lambda i: (0, 0, 0))
    out = pl.BlockSpec((tm, DNW), lambda i: (i, 0))
    return _pc(kern, "d_branch", (s_len // tm,), [row, row, w_spec, w_spec], [out, out],
               [SDS((s_len, DNW), F32), SDS((s_len, SWAW), F32)])(d_pa, d_pb, wa, wb)


def _gw_branch(y_dn, y_swa, d_pa, d_pb):
    s_len = y_dn.shape[0]

    def kern(ya_ref, yb_ref, da_ref, db_ref, oa_ref, ob_ref):
        oa_ref[0] = _bdot(ya_ref[...], da_ref[...], "tn").astype(BF16)
        ob_ref[0] = _bdot(yb_ref[...], db_ref[...], "tn").astype(BF16)

    y_spec = pl.BlockSpec((s_len, DNW), lambda s: (0, 0))
    d_spec = pl.BlockSpec((s_len, CSH), lambda s: (0, s))
    o_spec = pl.BlockSpec((1, DNW, CSH), lambda s: (s, 0, 0))
    shape = (N_CHIPS, DNW, CSH)
    return _pc(kern, "gw_branch", (N_CHIPS,), [y_spec, y_spec, d_spec, d_spec], [o_spec, o_spec],
               [SDS(shape, BF16), SDS(shape, BF16)])(y_dn, y_swa, d_pa, d_pb)


def _dh_rms(d_proj, w_in_p, x, dx1, gain):
    s_len = x.shape[0]
    tm = min(512, s_len)

    def kern(dp_ref, w_ref, x_ref, r_ref, g_ref, gx_ref, dgain_ref):
        _zero_first([dgain_ref])
        dh = _bdot(dp_ref[...], w_ref[...], "nt")
        _, vjp = jax.vjp(_f_rms, x_ref[...], g_ref[...])
        dx, dgain = vjp(dh)
        gx_ref[...] = dx + r_ref[...]
        dgain_ref[...] += dgain

    row = pl.BlockSpec((tm, D), lambda i: (i, 0))
    vec = pl.BlockSpec((1, D), lambda i: (0, 0))
    return _pc(kern, "dh_rms", (s_len // tm,),
               [pl.BlockSpec((tm, PW), lambda i: (i, 0)), _resident((D, PW)), row, row, vec],
               [row, vec], [SDS((s_len, D), F32), SDS((1, D), F32)])(d_proj, w_in_p, x, dx1, gain)


HALO = 8


def _rows_down(x, n, above):
    tm = x.shape[0]
    r = pltpu.roll(x, n, 0)
    a = pltpu.roll(above, n, 0)
    top = jnp.where(lax.broadcasted_iota(jnp.int32, above.shape, 0) < n, a, r[0:HALO])
    return jnp.concatenate([top, r[HALO:tm]], axis=0)


def _rows_up(x, n, below):
    tm = x.shape[0]
    r = pltpu.roll(x, tm - n, 0)
    b = pltpu.roll(below, HALO - n, 0)
    bottom = jnp.where(lax.broadcasted_iota(jnp.int32, below.shape, 0) >= HALO - n, b, r[tm - HALO:tm])
    return jnp.concatenate([r[0:tm - HALO], bottom], axis=0)


def _conv_taps(cur_ref, prev_ref, first):
    cur = cur_ref[...]
    above = jnp.where(first, 0.0, prev_ref[...])
    return [_rows_down(cur, n, above) for n in range(CONV - 1, 0, -1)] + [cur]


def _dn_pre_specs(s_len, tm, blk):
    cur = pl.BlockSpec((tm, QKVW), lambda i: (blk(i), 0))
    prev = pl.BlockSpec((HALO, QKVW), lambda i: (jnp.maximum(blk(i) * (tm // HALO) - 1, 0), 0))
    ba = pl.BlockSpec((tm, 128), lambda i: (blk(i), C_BA // 128))
    row = pl.BlockSpec((tm, DNW), lambda i: (blk(i), 0))
    full = [pl.BlockSpec((CONV, QKVW), lambda i: (0, 0)), pl.BlockSpec((1, DN_H), lambda i: (0, 0)),
            pl.BlockSpec((1, DN_H), lambda i: (0, 0))]
    return cur, prev, ba, row, full


def _dn_pre_fwd(proj, conv_w, alog, dtb):
    s_len = proj.shape[0]
    tm = min(128, s_len)
    cur, prev, ba, row, full = _dn_pre_specs(s_len, tm, lambda i: i)

    def kern(cur_ref, prev_ref, ba_ref, cw_ref, al_ref, dt_ref, q_ref, k_ref, v_ref, bb_ref, gb_ref):
        xs = _conv_taps(cur_ref, prev_ref, pl.program_id(0) == 0)
        outs = _f_dn_pre(*xs, ba_ref[...], cw_ref[...], al_ref[...], dt_ref[...])
        for ref, val in zip((q_ref, k_ref, v_ref, bb_ref, gb_ref), outs, strict=True):
            ref[...] = val

    return _pc(kern, "dn_pre_fwd", (s_len // tm,), [cur, prev, ba] + full, [row] * 5,
               [SDS((s_len, DNW), F32)] * 5)(proj, proj, proj, conv_w, alog, dtb)


def _dn_pre_bwd(proj, conv_w, alog, dtb, cots, others):
    s_len = proj.shape[0]
    tm = min(128, s_len)
    nb = s_len // tm
    cur, prev, ba, row, full = _dn_pre_specs(s_len, tm, lambda i: nb - 1 - i)
    n_o = len(others)
    assert QKVW + sum(t.shape[1] for t in others) + 128 == C_BA + 128

    def kern(cur_ref, prev_ref, ba_ref, cw_ref, al_ref, dt_ref, dq_ref, dk_ref, dv_ref, dbb_ref, dgb_ref, *rest):
        o_refs = rest[:n_o]
        dproj_ref, dcw_ref, dal_ref, ddt_ref, *tails = rest[n_o:]
        i = pl.program_id(0)
        _zero_first([dcw_ref, dal_ref, ddt_ref] + tails)
        xs = _conv_taps(cur_ref, prev_ref, i == nb - 1)
        _, vjp = jax.vjp(_f_dn_pre, *xs, ba_ref[...], cw_ref[...], al_ref[...], dt_ref[...])
        *dxs, dba, dcw, dal, ddt = vjp((dq_ref[...], dk_ref[...], dv_ref[...], dbb_ref[...], dgb_ref[...]))
        total = dxs[CONV - 1]
        for j, t in enumerate(tails):
            n = CONV - 1 - j
            total = total + _rows_up(dxs[j], n, t[...])
            t[...] = dxs[j][0:HALO, :]
        dproj_ref[...] = jnp.concatenate(
            [total.astype(BF16)] + [r[...] for r in o_refs] + [dba.astype(BF16), jnp.zeros((tm, PW - C_BA - 128), BF16)],
            axis=1)
        dcw_ref[...] += dcw
        dal_ref[...] += dal
        ddt_ref[...] += ddt

    o_specs = [pl.BlockSpec((tm, t.shape[1]), lambda i: (nb - 1 - i, 0)) for t in others]
    return _pc(kern, "dn_pre_bwd", (nb,), [cur, prev, ba] + full + [row] * 5 + o_specs,
               [pl.BlockSpec((tm, PW), lambda i: (nb - 1 - i, 0))] + full,
               [SDS((s_len, PW), BF16), SDS((CONV, QKVW), F32), SDS((1, DN_H), F32), SDS((1, DN_H), F32)],
               scratch=[pltpu.VMEM((HALO, QKVW), F32)] * (CONV - 1))(proj, proj, proj, conv_w, alog, dtb, *cots, *others)


def _w_in_to_padded(w_sh):
    tr = 256

    def kern(w_ref, o_ref):
        full = jnp.concatenate([w_ref[s] for s in range(N_CHIPS)], axis=1)
        pieces = [full[:, o0:o0 + w] for o0, w, _ in sorted(_ORIG_PIECES, key=lambda t: t[2])]
        o_ref[...] = jnp.concatenate(pieces + [jnp.zeros((tr, PW - D_IN), w_ref.dtype)], axis=1)

    return _pc(kern, "w_in_to_padded", (D // tr,), [pl.BlockSpec((N_CHIPS, tr, D_IN // N_CHIPS), lambda i: (0, i, 0))],
               pl.BlockSpec((tr, PW), lambda i: (i, 0)), SDS((D, PW), w_sh.dtype))(w_sh)


def _padded_to_w_in(g):
    tr = 256
    csh = D_IN // N_CHIPS

    def kern(g_ref, o_ref):
        x = g_ref[...]
        full = jnp.concatenate([x[:, p0:p0 + w] for _, w, p0 in _ORIG_PIECES], axis=1)
        for s in range(N_CHIPS):
            o_ref[s] = full[:, s * csh:(s + 1) * csh]

    return _pc(kern, "padded_to_w_in", (D // tr,), [pl.BlockSpec((tr, PW), lambda i: (i, 0))],
               pl.BlockSpec((N_CHIPS, tr, csh), lambda i: (0, i, 0)), SDS((N_CHIPS, D, csh), g.dtype))(g)


def _local_step(x, target, wts):
    s_len = x.shape[0]
    tm = min(512, s_len)
    w_in_p = wts["w_in_p"]
    attn_gain = wts["attn_norm"]
    ffn_gain = wts["ffn_norm"]
    conv_w = wts["dn_conv"]
    alog, dtb, out_gain = wts["dn_a_log"], wts["dn_dt_bias"], wts["dn_out_norm"]
    qg, kg = wts["swa_q_norm"], wts["swa_k_norm"]
    sinks = wts["swa_sinks"].reshape(SWA_KV, 1, SWA_G)

    h, proj = _in_proj(x, attn_gain, w_in_p)
    q_dn, k_dn, v_dn, bb, gb = _dn_pre_fwd(proj, conv_w, alog, dtb)
    o_dn, s_all, t_all = _dn_chunks_fwd(q_dn, k_dn, v_dn, gb, bb)
    post_ins = [_whole(o_dn), (proj, DNW, C_Z // DNW)]
    (y_dn,) = _rows(lambda r, f: ([_f_dn_post(r[0], r[1], f[0])], []), "dn_post_fwd", s_len, tm, post_ins,
                    [out_gain], [(DNW, BF16)])

    bias = _bias_expand(wts["rel_bias"].T).reshape(SWA_H, BLK, 2 * BLK)
    y_swa = _swa_fwd(proj, bias, qg, kg, sinks)

    wts = {**wts, **wts["late"](y_swa)}
    p_a, p_b, merged = _branch_merge(y_dn, y_swa, wts["wa"], wts["wb"], proj)
    x1, h2 = _out_proj(merged, wts["w_out"], x, ffn_gain)
    gt, up, act = _ffn_up(h2, wts["wg"], wts["wu"])
    dy, dy_b, loss = _ffn_down_loss(act, wts["wd"], x1, target)

    grads = {}
    d_gt, d_up = _ffn_dact(dy_b, wts["wd"], gt, up)
    (grads["w_down"],) = _gw_ffn([act], dy_b, "gw_down")
    grads["w_gate"], grads["w_up"] = _gw_ffn([d_gt, d_up], h2, "gw_gate_up")
    token = wts["send_ffn"](grads)
    dx1, dx1_b, grads["ffn_norm"] = _ffn_dh2(d_gt, d_up, wts["wg"], wts["wu"], x1, dy,
                                             ffn_gain + token[0:1, 0:1])
    grads["w_out"] = _mm(merged, dx1_b, "tn", BF16, 512, 512, "gw_out")
    d_pa, d_pb, d_gr = _merge_bwd(dx1_b, wts["w_out"], p_a, p_b, proj)
    d_ydn, d_yswa = _d_branch(d_pa, d_pb, wts["wa"], wts["wb"])
    grads["w_branch_dn"], grads["w_branch_swa"] = _gw_branch(y_dn, y_swa, d_pa, d_pb)
    token = wts["send_early"](grads)
    qg_t = qg + token[0:1, 0:1]
    out_gain_t = out_gain + token[0:1, 0:1]

    d_sq, d_sk, d_sv, d_bias, grads["swa_q_norm"], grads["swa_k_norm"], d_sinks = _swa_bwd(
        proj, bias, qg_t, kg, sinks, d_yswa)
    grads["swa_sinks"] = d_sinks.reshape(1, SWA_H)
    grads["rel_bias"] = _bias_reduce(d_bias.reshape(SWA_H, BLK * 2 * BLK)).T

    def post_bwd(r, f):
        _, vjp = jax.vjp(_f_dn_post, r[0], r[1], f[0])
        d_o, d_z, d_gain = vjp(r[2])
        return [d_o, d_z], [d_gain]

    d_o, d_z, grads["dn_out_norm"] = _rows(post_bwd, "dn_post_bwd", s_len, tm, post_ins + [_whole(d_ydn)], [out_gain_t],
                                           [(DNW, F32), (DNW, BF16)], [(1, DH)])
    d_q, d_k, d_v, d_gb, d_bb = _dn_chunks_bwd(q_dn, k_dn, v_dn, gb, bb, s_all, t_all, d_o)

    d_proj, grads["dn_conv"], grads["dn_a_log"], grads["dn_dt_bias"] = _dn_pre_bwd(
        proj, conv_w, alog, dtb, (d_q, d_k, d_v, d_bb, d_gb), (d_z, d_gr, d_sq, d_sk, d_sv))
    grads["w_in_p"] = _mm(h, d_proj, "tn", BF16, 512, 1024, "gw_in")
    token = wts["send_in"](grads["w_in_p"])
    grad_x, grads["attn_norm"] = _dh_rms(d_proj, w_in_p, x, dx1, attn_gain + token[0:1, 0:1])
    return loss, grad_x, grads


_HBM = pl.BlockSpec(memory_space=pl.ANY)


def _place():
    return lax.axis_index("x"), lax.axis_index("y"), lax.axis_index("c")


def _other_chips(x, y):
    return [(1 - x, y), (x, 1 - y), (1 - x, 1 - y)]


def _rcopy(src, dst, send_sems, recv_sems, k, to):
    return pltpu.make_async_remote_copy(src_ref=src, dst_ref=dst, send_sem=send_sems.at[k], recv_sem=recv_sems.at[k],
                                        device_id=to, device_id_type=MESH)


def _comm_call(body, name, ins, out_shapes, n_remote, landing=0):
    first = len(ins) - landing
    return pl.pallas_call(
        body, name=name, in_specs=[_HBM] * len(ins), out_specs=[_HBM] * len(out_shapes), out_shape=out_shapes,
        scratch_shapes=[pltpu.SemaphoreType.DMA((n_remote,)), pltpu.SemaphoreType.DMA((n_remote,))],
        input_output_aliases={first + i: i for i in range(landing)},
        compiler_params=_cparams(has_side_effects=True),
    )(*ins)


def _own_slot(blocks, chip):
    return [lax.dynamic_update_slice(lax.empty((N_CHIPS,) + b.shape, b.dtype), b[None], (chip, 0, 0)) for b in blocks]


def _gather_weights(ws, chip):
    n = len(ws)
    halves = [w.shape[0] // 2 for w in ws]

    def body(*refs):
        w_refs, o_refs = refs[:n], refs[2 * n:3 * n]
        send_sems, recv_sems = refs[3 * n:]
        x, y, c = _place()
        s = 2 * x + y
        sib = (x, y, 1 - c)
        chips = _other_chips(x, y)

        def rows(i, half):
            return pl.ds(half * halves[i], halves[i])

        first = []
        for j, (cx, cy) in enumerate(chips):
            for i in range(n):
                cp = _rcopy(w_refs[i].at[rows(i, c), :], o_refs[i].at[s, rows(i, c), :], send_sems, recv_sems,
                            j * n + i, (cx, cy, c))
                cp.start()
                first.append(cp)
        passed = []
        for j, (cx, cy) in enumerate(chips):
            sj = 2 * cx + cy
            for i in range(n):
                blk = o_refs[i].at[sj, rows(i, c), :]
                _rcopy(blk, blk, send_sems, recv_sems, j * n + i, (cx, cy, c)).wait_recv()
                cp = _rcopy(blk, blk, send_sems, recv_sems, (3 + j) * n + i, sib)
                cp.start()
                passed.append(cp)
        for j, (cx, cy) in enumerate(chips):
            sj = 2 * cx + cy
            for i in range(n):
                blk = o_refs[i].at[sj, rows(i, 1 - c), :]
                _rcopy(blk, blk, send_sems, recv_sems, (3 + j) * n + i, sib).wait_recv()
        for cp in first + passed:
            cp.wait_send()

    return _comm_call(body, "gather_weights", list(ws) + _own_slot(ws, chip),
                      [SDS((N_CHIPS,) + w.shape, w.dtype) for w in ws], 6 * n, landing=n)


_HBM_ONLY = pl.BlockSpec(memory_space=pltpu.HBM)
_SEM = pl.BlockSpec(memory_space=pltpu.SEMAPHORE)
_DATAFLOW = pltpu.SideEffectType.DATAFLOW_SIDE_EFFECTING


def _in_hbm(a):
    return pltpu.with_memory_space_constraint(a, pltpu.HBM)


def _gather_windows(blocks):
    halves = [b.shape[0] // 2 for b in blocks]

    def src_at(ref, i, c, sj):
        return ref.at[pl.ds(c * halves[i], halves[i]), :]

    def dst_at(ref, i, c, s_from):
        return ref.at[s_from, pl.ds(c * halves[i], halves[i]), :]

    return src_at, dst_at


def _exchange_windows():
    return (lambda ref, i, c, sj: ref.at[sj]), (lambda ref, i, c, s_from: ref.at[s_from])


def _swap_windows(gs):
    halves = [g.shape[1] // 2 for g in gs]
    return ((lambda ref, i, c, tag: ref.at[:, pl.ds((1 - c) * halves[i], halves[i]), :]),
            (lambda ref, i, c, slot: ref))


def _chip_peers(x, y, c):
    return [(2 * cx + cy, (cx, cy, c), 2 * x + y, 2 * cx + cy) for cx, cy in _other_chips(x, y)]


def _sibling_peer(x, y, c):
    return [(0, (x, y, 1 - c), 0, 0)]


def _split_start(name, ws, lands, dep, windows, peers=_chip_peers, n_peers=3):
    n = len(ws)
    src_at, dst_at = windows

    def body(*refs):
        w_refs, l_refs = refs[:n], refs[n:2 * n]
        send_sems, recv_sems = refs[2 * n + 1], refs[2 * n + 2]
        token = refs[-1]
        x, y, c = _place()
        for j, (tag, dev, there, _) in enumerate(peers(x, y, c)):
            for i in range(n):
                _rcopy(src_at(w_refs[i], i, c, tag), dst_at(l_refs[i], i, c, there), send_sems, recv_sems,
                       j * n + i, dev).start()
        token[...] = jnp.zeros_like(token)

    outs = pl.pallas_call(
        body, name=name,
        out_shape=(pltpu.SemaphoreType.DMA((n_peers * n,)), pltpu.SemaphoreType.DMA((n_peers * n,)),
                   *[pltpu.HBM(w.shape, w.dtype) for w in ws], *[pltpu.HBM(t.shape, t.dtype) for t in lands],
                   SDS((8, 128), F32)),
        in_specs=[_HBM_ONLY] * (2 * n) + [pl.BlockSpec(memory_space=pl.ANY)],
        out_specs=(_SEM, _SEM, *[_HBM_ONLY] * (2 * n), pl.BlockSpec(memory_space=pltpu.VMEM)),
        input_output_aliases={i: 2 + i for i in range(2 * n)},
        compiler_params=pltpu.CompilerParams(has_side_effects=_DATAFLOW),
    )(*[_in_hbm(w) for w in ws], *[_in_hbm(t) for t in lands], dep)
    return outs[0], outs[1], outs[2:2 + n], outs[2 + n:2 + 2 * n], outs[-1]


def _split_wait(name, w_thru, l_thru, send_sems, recv_sems, after, windows, peers=_chip_peers, with_sources=False):
    n = len(w_thru)
    src_at, dst_at = windows

    def body(*refs):
        w_refs, l_refs = refs[:n], refs[n:2 * n]
        send_sems, recv_sems = refs[2 * n], refs[2 * n + 1]
        x, y, c = _place()
        for j, (tag, dev, _, here) in enumerate(peers(x, y, c)):
            for i in range(n):
                cp = _rcopy(src_at(w_refs[i], i, c, tag), dst_at(l_refs[i], i, c, here), send_sems, recv_sems,
                            j * n + i, dev)
                cp.wait_send()
                cp.wait_recv()

    outs = pl.pallas_call(
        body, name=name,
        out_shape=[pltpu.HBM(w.shape, w.dtype) for w in w_thru] + [pltpu.HBM(t.shape, t.dtype) for t in l_thru],
        in_specs=[_HBM_ONLY] * (2 * n) + [_SEM, _SEM, pl.BlockSpec(memory_space=pl.ANY)],
        out_specs=[_HBM_ONLY] * (2 * n),
        input_output_aliases={i: i for i in range(2 * n)},
        compiler_params=pltpu.CompilerParams(has_side_effects=_DATAFLOW),
    )(*w_thru, *l_thru, send_sems, recv_sems, after)
    return (outs[:n], outs[n:]) if with_sources else outs[n:]


def _sibling_fill(lands):
    n = len(lands)
    halves = [t.shape[1] // 2 for t in lands]

    def body(*refs):
        o_refs = refs[n:2 * n]
        send_sems, recv_sems = refs[2 * n:]
        x, y, c = _place()
        sib = (x, y, 1 - c)
        chips = _other_chips(x, y)
        sent = []
        for j, (cx, cy) in enumerate(chips):
            for i in range(n):
                blk = o_refs[i].at[2 * cx + cy, pl.ds(c * halves[i], halves[i]), :]
                cp = _rcopy(blk, blk, send_sems, recv_sems, j * n + i, sib)
                cp.start()
                sent.append(cp)
        for j, (cx, cy) in enumerate(chips):
            for i in range(n):
                blk = o_refs[i].at[2 * cx + cy, pl.ds((1 - c) * halves[i], halves[i]), :]
                _rcopy(blk, blk, send_sems, recv_sems, j * n + i, sib).wait_recv()
        for cp in sent:
            cp.wait_send()

    return _comm_call(body, "sibling_fill", list(lands), [SDS(t.shape, t.dtype) for t in lands], 3 * n, landing=n)


def _swap_halves(gs, name):
    n = len(gs)
    halves = [g.shape[1] // 2 for g in gs]

    def body(*refs):
        g_refs, o_refs = refs[:n], refs[n:2 * n]
        send_sems, recv_sems = refs[2 * n:]
        x, y, c = _place()
        cps = [_rcopy(g_refs[i].at[:, pl.ds((1 - c) * halves[i], halves[i]), :], o_refs[i], send_sems, recv_sems, i,
                      (x, y, 1 - c)) for i in range(n)]
        for cp in cps:
            cp.start()
        for cp in cps:
            cp.wait()

    return _comm_call(body, name, gs, [SDS((N_CHIPS, h, g.shape[2]), g.dtype) for g, h in zip(gs, halves)], n)


def _swap_reduced(rs, name):
    n = len(rs)

    def body(*refs):
        r_refs, o_refs = refs[:n], refs[n:2 * n]
        send_sems, recv_sems = refs[2 * n:]
        x, y, c = _place()
        cps = [_rcopy(r_refs[i], o_refs[i], send_sems, recv_sems, i, (x, y, 1 - c)) for i in range(n)]
        for cp in cps:
            cp.start()
        for cp in cps:
            cp.wait()

    return _comm_call(body, name, rs, [SDS(r.shape, r.dtype) for r in rs], n)


def _all_sum_small(vec, name):
    n_dev = 8
    flips = [(bx, by, bc) for bx in (0, 1) for by in (0, 1) for bc in (0, 1)][1:]

    def body(v_ref, out_ref, gath, send_sems, recv_sems):
        x, y, c = _place()
        me = 4 * x + 2 * y + c
        gath[me] = v_ref[...]
        sent = []
        for k, (bx, by, bc) in enumerate(flips):
            peer = (x ^ bx, y ^ by, c ^ bc)
            cp = _rcopy(v_ref, gath.at[me], send_sems, recv_sems, k, peer)
            cp.start()
            sent.append(cp)
        for k, (bx, by, bc) in enumerate(flips):
            peer = (x ^ bx, y ^ by, c ^ bc)
            _rcopy(v_ref, gath.at[4 * peer[0] + 2 * peer[1] + peer[2]], send_sems, recv_sems, k, peer).wait_recv()
        for cp in sent:
            cp.wait_send()
        acc = gath[0]
        for d in range(1, n_dev):
            acc = acc + gath[d]
        out_ref[...] = acc

    vm = pl.BlockSpec(memory_space=pltpu.VMEM)
    return pl.pallas_call(
        body, name=name, in_specs=[vm], out_specs=vm, out_shape=SDS(vec.shape, F32),
        scratch_shapes=[pltpu.VMEM((n_dev,) + vec.shape, F32), pltpu.SemaphoreType.DMA((7,)),
                        pltpu.SemaphoreType.DMA((7,))],
        compiler_params=_cparams(has_side_effects=True),
    )(vec)


def _pack_small(vals, extra=None):
    parts = [vals[n].reshape(-1).astype(F32) for n, _ in _SMALL]
    parts.append(jnp.zeros((1,), F32) if extra is None else extra.reshape(1).astype(F32))
    flat = jnp.concatenate(parts)
    flat = jnp.concatenate([flat, jnp.zeros((_SMALL_ROWS * 128 - flat.shape[0],), F32)])
    return flat.reshape(_SMALL_ROWS, 128)


def _unpack_small(packed, shapes):
    flat = packed.reshape(-1)
    return {n: flat[_SMALL_OFF[n][0]:_SMALL_OFF[n][0] + _SMALL_OFF[n][1]].reshape(shapes[n]) for n, _ in _SMALL}


def _pair_sum(gs, gots, core, name):
    n = len(gs)

    def kern(c_ref, *refs):
        for i in range(n):
            refs[2 * n + i][...] = (refs[i][...].astype(F32) + refs[n + i][...].astype(F32)).astype(BF16)

    in_specs = [pl.BlockSpec((1, t.shape[1], t.shape[2]), lambda s, c_ref: (s, c_ref[0], 0)) for t in gots]
    in_specs += [pl.BlockSpec((1, t.shape[1], t.shape[2]), lambda s, c_ref: (s, 0, 0)) for t in gots]
    out_specs = [pl.BlockSpec((1, t.shape[1], t.shape[2]), lambda s, c_ref: (s, 0, 0)) for t in gots]
    return pl.pallas_call(
        kern, name=name,
        grid_spec=pltpu.PrefetchScalarGridSpec(num_scalar_prefetch=1, grid=(N_CHIPS,), in_specs=in_specs,
                                               out_specs=out_specs),
        out_shape=[SDS(t.shape, BF16) for t in gots],
        compiler_params=_cparams(dimension_semantics=("arbitrary",)),
    )(core.reshape(1).astype(jnp.int32), *gs, *gots)


def _chip_sum(qs, name):
    n = len(qs)

    def kern(*refs):
        for i in range(n):
            acc = refs[i][0].astype(F32)
            for s in range(1, N_CHIPS):
                acc = acc + refs[i][s].astype(F32)
            refs[n + i][...] = acc

    in_specs = [pl.BlockSpec((N_CHIPS, q.shape[1] // 2, q.shape[2]), lambda j: (0, j, 0)) for q in qs]
    out_specs = [pl.BlockSpec((q.shape[1] // 2, q.shape[2]), lambda j: (j, 0)) for q in qs]
    return _pc(kern, name, (2,), in_specs, out_specs, [SDS(q.shape[1:], F32) for q in qs])(*qs)


def _adam_math(w_, g_, m_, v_):
    m_ = ADAM_B1 * m_ + (1.0 - ADAM_B1) * g_
    v_ = ADAM_B2 * v_ + (1.0 - ADAM_B2) * jnp.square(g_)
    m_hat = m_ / (1.0 - ADAM_B1 ** ADAM_STEP)
    v_hat = v_ / (1.0 - ADAM_B2 ** ADAM_STEP)
    return -ADAM_LR * (m_hat / (jnp.sqrt(v_hat) + ADAM_EPS) + ADAM_WD * w_), m_, v_


def _adamw(w, g, m, v, name):
    rows, cols = w.shape
    tr = rows
    for cand in (256, 128, 64, 32, 16, 8):
        if rows % cand == 0 and rows > cand:
            tr = cand
            break

    def kern(w_ref, g_ref, m_ref, v_ref, d_ref, nm_ref, nv_ref):
        d_ref[...], nm_ref[...], nv_ref[...] = _adam_math(w_ref[...], g_ref[...], m_ref[...], v_ref[...])

    spec = pl.BlockSpec((tr, cols), lambda i: (i, 0))
    return _pc(kern, name, (rows // tr,), [spec] * 4, [spec] * 3, [SDS(w.shape, F32)] * 3)(w, g, m, v)


def _adamw_rows1(w, g, m, v, name):
    rows, _, cols = w.shape
    tr = next(t for t in (203, 174, 128, 64, 42, 32, 29, 16, 8, 7, 6, 4, 3, 2, 1) if rows % t == 0)

    def kern(w_ref, g_ref, m_ref, v_ref, go_ref, d_ref, nm_ref, nv_ref):
        g_ = g_ref[...]
        go_ref[...] = g_
        d_ref[...], nm_ref[...], nv_ref[...] = _adam_math(w_ref[...], g_, m_ref[...], v_ref[...])

    spec = pl.BlockSpec((tr, 1, cols), lambda i: (i, 0, 0))
    return _pc(kern, name, (rows // tr,), [spec] * 4, [spec] * 4, [SDS(w.shape, F32)] * 4)(w, g, m, v)


def _adamw_big(w, mine, theirs, m, v, core, name):
    _, rows, cols = w.shape
    half = rows // 2
    tr = next(t for t in (256, 176, 128, 64, 32, 16, 8) if half % t == 0)
    nbh = half // tr

    def kern(c_ref, w_ref, a_ref, b_ref, m_ref, v_ref, g_ref, d_ref, nm_ref, nv_ref):
        g_ = jnp.where(pl.program_id(0) // nbh == c_ref[0], a_ref[...], b_ref[...])
        g_ref[0] = g_
        d_ref[0], nm_ref[0], nv_ref[0] = _adam_math(w_ref[0], g_, m_ref[0], v_ref[0])

    full = pl.BlockSpec((1, tr, cols), lambda i, c_ref: (0, i, 0))
    part = pl.BlockSpec((tr, cols), lambda i, c_ref: (i % nbh, 0))
    return pl.pallas_call(
        kern, name=name,
        grid_spec=pltpu.PrefetchScalarGridSpec(num_scalar_prefetch=1, grid=(rows // tr,),
                                               in_specs=[full, part, part, full, full], out_specs=[full] * 4),
        out_shape=[SDS(w.shape, F32)] * 4,
        compiler_params=_cparams(dimension_semantics=("arbitrary",)),
    )(core.reshape(1).astype(jnp.int32), w, mine, theirs, m, v)


_WEIGHT_NAMES = ("attn_norm", "w_in", "dn_conv", "dn_a_log", "dn_dt_bias", "dn_out_norm", "swa_q_norm", "swa_k_norm",
                 "swa_sinks", "rel_bias", "w_branch_dn", "w_branch_swa", "w_out", "ffn_norm", "w_gate", "w_up",
                 "w_down")
_CONV_SH = QKVW // N_CHIPS


def kernel(x, attn_norm, w_in, dn_conv, dn_a_log, dn_dt_bias, dn_out_norm, swa_q_norm, swa_k_norm, swa_sinks, rel_bias, w_branch_dn, w_branch_swa, w_out, ffn_norm, w_gate, w_up, w_down, loss_target, m_attn_norm, m_w_in, m_dn_conv, m_dn_a_log, m_dn_dt_bias, m_dn_out_norm, m_swa_q_norm, m_swa_k_norm, m_swa_sinks, m_rel_bias, m_w_branch_dn, m_w_branch_swa, m_w_out, m_ffn_norm, m_w_gate, m_w_up, m_w_down, v_attn_norm, v_w_in, v_dn_conv, v_dn_a_log, v_dn_dt_bias, v_dn_out_norm, v_swa_q_norm, v_swa_k_norm, v_swa_sinks, v_rel_bias, v_w_branch_dn, v_w_branch_swa, v_w_out, v_ffn_norm, v_w_gate, v_w_up, v_w_down):
    w = dict(attn_norm=attn_norm, w_in=w_in, dn_conv=dn_conv, dn_a_log=dn_a_log, dn_dt_bias=dn_dt_bias,
             dn_out_norm=dn_out_norm, swa_q_norm=swa_q_norm, swa_k_norm=swa_k_norm, swa_sinks=swa_sinks,
             rel_bias=rel_bias, w_branch_dn=w_branch_dn, w_branch_swa=w_branch_swa, w_out=w_out, ffn_norm=ffn_norm,
             w_gate=w_gate, w_up=w_up, w_down=w_down)
    m = dict(attn_norm=m_attn_norm, w_in=m_w_in, dn_conv=m_dn_conv, dn_a_log=m_dn_a_log, dn_dt_bias=m_dn_dt_bias,
             dn_out_norm=m_dn_out_norm, swa_q_norm=m_swa_q_norm, swa_k_norm=m_swa_k_norm, swa_sinks=m_swa_sinks,
             rel_bias=m_rel_bias, w_branch_dn=m_w_branch_dn, w_branch_swa=m_w_branch_swa, w_out=m_w_out,
             ffn_norm=m_ffn_norm, w_gate=m_w_gate, w_up=m_w_up, w_down=m_w_down)
    v = dict(attn_norm=v_attn_norm, w_in=v_w_in, dn_conv=v_dn_conv, dn_a_log=v_dn_a_log, dn_dt_bias=v_dn_dt_bias,
             dn_out_norm=v_dn_out_norm, swa_q_norm=v_swa_q_norm, swa_k_norm=v_swa_k_norm, swa_sinks=v_swa_sinks,
             rel_bias=v_rel_bias, w_branch_dn=v_w_branch_dn, w_branch_swa=v_w_branch_swa, w_out=v_w_out,
             ffn_norm=v_ffn_norm, w_gate=v_w_gate, w_up=v_w_up, w_down=v_w_down)
    shapes = {n: w[n].shape for n in _WEIGHT_NAMES}

    def two_d(a):
        return a.reshape(a.shape[-2], a.shape[-1]) if a.ndim == 3 else a

    core = lax.axis_index("c")
    chip = 2 * lax.axis_index("x") + lax.axis_index("y")
    small_shapes = {n: two_d(w[n]).shape for n, _ in _SMALL}
    small_shapes["dn_conv"] = (CONV, QKVW)

    conv_loc = two_d(w["dn_conv"])
    conv_part = lax.dynamic_update_slice(jnp.zeros((CONV, QKVW), F32), jnp.where(core == 0, conv_loc, 0.0),
                                         (0, chip * _CONV_SH))
    conv_full = _all_sum_small(conv_part.reshape(CONV * QKVW // 128, 128), "gather_conv").reshape(CONV, QKVW)

    flipped = ("w_gate", "w_up")

    def natural(a, n):
        return a.transpose(0, 2, 1) if n in flipped else a

    w_bf = [two_d(natural(w[n], n).astype(BF16)) for n in _BIG_NAMES]
    (w_in_g,) = _gather_weights(w_bf[:1], chip)
    windows = _gather_windows(w_bf[1:])
    after_sync = w_in_g[0, :8, :128].astype(F32) + conv_full[0:1, :128]
    send_sems, recv_sems, w_thru, l_thru, token = _split_start(
        "gather_start", w_bf[1:], _own_slot(w_bf[1:], chip), after_sync, windows)

    def late(after):
        lands = _split_wait("gather_wait", w_thru, l_thru, send_sems, recv_sems, after, windows)
        g = dict(zip(_BIG_NAMES[1:], _sibling_fill(lands)))
        return dict(wa=g["w_branch_dn"], wb=g["w_branch_swa"], w_out=g["w_out"].reshape(D, D), wg=g["w_gate"],
                    wu=g["w_up"], wd=g["w_down"])

    wts = dict(w_in_p=_w_in_to_padded(w_in_g), dn_conv=conv_full, late=late)
    for n, _ in _SMALL[:-1]:
        wts[n] = two_d(w[n])
    wts["attn_norm"] = wts["attn_norm"] + token[0:1, 0:1]

    early = {}

    ffn = {}

    def send_ffn(grads):
        gs = [grads["w_gate"], grads["w_up"], grads["w_down"]]
        lands = [lax.empty((N_CHIPS, g.shape[1] // 2, g.shape[2]), g.dtype) for g in gs]
        ffn["sems"], ffn["recv"], ffn["src"], ffn["land"], tok = _split_start(
            "swap_ffn_start", gs, lands, gs[0][0, :8, :128], _swap_windows(gs), _sibling_peer, 1)
        return tok

    def send_early(grads):
        small = [grads["w_branch_dn"], grads["w_branch_swa"], grads["w_out"].reshape(N_CHIPS, CSH, D)]
        big = [grads["w_gate"], grads["w_up"], grads["w_down"]]
        big, got_big = _split_wait("swap_ffn_wait", ffn["src"], ffn["land"], ffn["sems"], ffn["recv"], small[0],
                                   _swap_windows(big), _sibling_peer, with_sources=True)
        gots = list(_swap_halves(small, "swap_halves_early")) + list(got_big)
        parts = _pair_sum(small + list(big), gots, core, "pair_sum_early")
        own = [lax.dynamic_index_in_dim(p, chip, axis=0, keepdims=False) for p in parts]
        early["sems"], early["recv"], early["src"], early["land"], tok = _split_start(
            "exchange_start", parts, _own_slot(own, chip), parts[0][0, :8, :128], _exchange_windows())
        return tok

    last = {}

    def send_in(g_in_p):
        g_in = [_padded_to_w_in(g_in_p)]
        parts = _pair_sum(g_in, _swap_halves(g_in, "swap_halves_in"), core, "pair_sum_in")
        own = [lax.dynamic_index_in_dim(p, chip, axis=0, keepdims=False) for p in parts]
        last["sems"], last["recv"], last["src"], last["land"], tok = _split_start(
            "exchange_in_start", parts, _own_slot(own, chip), parts[0][0, :8, :128], _exchange_windows())
        return tok

    wts["send_ffn"] = send_ffn
    wts["send_early"] = send_early
    wts["send_in"] = send_in
    loss_sum, grad_x, grads = _local_step(x[0], loss_target[0], wts)

    small_sum = _all_sum_small(_pack_small(grads, loss_sum), "all_sum_small")
    loss = small_sum.reshape(-1)[_LOSS_OFF]
    g_small = _unpack_small(small_sum, small_shapes)

    q_early = _split_wait("exchange_wait", early["src"], early["land"], early["sems"], early["recv"], small_sum,
                          _exchange_windows())
    red_early = _chip_sum(list(q_early), "chip_sum_early")
    their_early = _swap_reduced(red_early, "swap_reduced_early")
    g_out, d_out, m_out, v_out = {}, {}, {}, {}
    for n, mine, other in zip(_BIG_NAMES[1:], red_early, their_early):
        res = _adamw_big(natural(w[n], n), mine, other, natural(m[n], n), natural(v[n], n), core, "adamw_" + n)
        g_out[n], d_out[n], m_out[n], v_out[n] = (natural(t, n) for t in res)

    q_in = _split_wait("exchange_in_wait", last["src"], last["land"], last["sems"], last["recv"],
                       d_out[_BIG_NAMES[-1]], _exchange_windows())
    reduced = _chip_sum(list(q_in), "chip_sum_in")
    theirs = _swap_reduced(reduced, "swap_reduced_in")

    def rows1(a):
        return a.transpose(2, 0, 1)

    def unrows1(a):
        return a.transpose(1, 2, 0)

    g_in_blk = jnp.concatenate([jnp.where(core == 0, reduced[0], theirs[0]),
                                jnp.where(core == 0, theirs[0], reduced[0])], axis=0)
    g_in_r = rows1(g_in_blk[None])
    res = _adamw_rows1(rows1(w["w_in"]), g_in_r, rows1(m["w_in"]), rows1(v["w_in"]), "adamw_w_in")
    g_out["w_in"], d_out["w_in"], m_out["w_in"], v_out["w_in"] = (unrows1(t) for t in res)
    g_conv = lax.dynamic_slice(g_small["dn_conv"], (0, chip * _CONV_SH), (CONV, _CONV_SH))
    g_out["dn_conv"] = g_conv.reshape(shapes["dn_conv"])
    d_, m_, v_ = _adamw(conv_loc, g_conv, two_d(m["dn_conv"]), two_d(v["dn_conv"]), "adamw_dn_conv")
    d_out["dn_conv"], m_out["dn_conv"], v_out["dn_conv"] = (t.reshape(shapes["dn_conv"]) for t in (d_, m_, v_))

    def packed(src):
        vals = {n: src[n] for n, _ in _SMALL[:-1]}
        vals["dn_conv"] = jnp.zeros((CONV * QKVW,), F32)
        return _pack_small(vals)

    d_s, m_s, v_s = _adamw(packed(w), small_sum, packed(m), packed(v), "adamw_small")
    d_small, m_small, v_small = (_unpack_small(t, small_shapes) for t in (d_s, m_s, v_s))
    for n, _ in _SMALL[:-1]:
        g_out[n] = g_small[n].reshape(shapes[n])
        d_out[n], m_out[n], v_out[n] = (t[n].reshape(shapes[n]) for t in (d_small, m_small, v_small))

    return (loss, grad_x[None], *[g_out[n] for n in _WEIGHT_NAMES], *[d_out[n] for n in _WEIGHT_NAMES],
            *[m_out[n] for n in _WEIGHT_NAMES], *[v_out[n] for n in _WEIGHT_NAMES])
```

```python
import functools
import math

import numpy as np
import jax
import jax.numpy as jnp
from jax import lax
from jax.experimental import pallas as pl
from jax.experimental.pallas import tpu as pltpu

F32 = jnp.float32
BF16 = jnp.bfloat16
SDS = jax.ShapeDtypeStruct

D = 1024
DN_H = 4
DH = 128
DNW = DN_H * DH
QKVW = 3 * DNW
CONV = 4
CHUNK = 64
SWA_H = 8
SWA_KV = 2
SWA_G = SWA_H // SWA_KV
SWA_D = 64
SWAW = SWA_H * SWA_D
SWAKW = SWA_KV * SWA_D
BLK = 128
NBUCKET = 32
MAXDIST = 128
DFF = 2816
D_IN = QKVW + DNW + 2 * DN_H + SWAW + 2 * SWAKW + 2 * D
EPS = 1e-6
NEG = -1e30

ADAM_LR = 0.001
ADAM_B1 = 0.9
ADAM_B2 = 0.999
ADAM_EPS = 1e-08
ADAM_WD = 0.01
ADAM_STEP = 10

C_QKV, C_Z, C_GATE, C_SQ, C_SK, C_SV, C_BA = 0, 1536, 2048, 4096, 4608, 4736, 4864
PW = 5120
_ORIG_PIECES = (
    (0, QKVW, C_QKV),
    (QKVW, DNW, C_Z),
    (QKVW + DNW, 2 * DN_H, C_BA),
    (QKVW + DNW + 2 * DN_H, SWAW, C_SQ),
    (QKVW + DNW + 2 * DN_H + SWAW, SWAKW, C_SK),
    (QKVW + DNW + 2 * DN_H + SWAW + SWAKW, SWAKW, C_SV),
    (QKVW + DNW + 2 * DN_H + SWAW + 2 * SWAKW, 2 * D, C_GATE),
)

N_CHIPS = 4
FSH = DFF // N_CHIPS
CSH = D // N_CHIPS
VMEM_LIMIT = 48 * 1024 * 1024
MESH = pl.DeviceIdType.MESH

_BIG = (
    ("w_in", D, D_IN // N_CHIPS),
    ("w_branch_dn", DNW, CSH),
    ("w_branch_swa", SWAW, CSH),
    ("w_out", CSH, D),
    ("w_gate", FSH, D),
    ("w_up", FSH, D),
    ("w_down", FSH, D),
)
_BIG_NAMES = tuple(n for n, _, _ in _BIG)

_SMALL = (
    ("attn_norm", D), ("ffn_norm", D), ("dn_out_norm", DH), ("swa_q_norm", SWA_D), ("swa_k_norm", SWA_D),
    ("swa_sinks", SWA_H), ("dn_a_log", DN_H), ("dn_dt_bias", DN_H), ("rel_bias", NBUCKET * SWA_H),
    ("dn_conv", CONV * QKVW),
)
_SMALL_OFF = {}
_o = 0
for _n, _s in _SMALL:
    _SMALL_OFF[_n] = (_o, _s)
    _o += _s
_LOSS_OFF = _o
_SMALL_ROWS = -(-(_o + 1) // (8 * 128)) * 8


def _cparams(**kw):
    return pltpu.CompilerParams(vmem_limit_bytes=VMEM_LIMIT, **kw)


_DIMS = {
    "nn": (((1,), (0,)), ((), ())),
    "nt": (((1,), (1,)), ((), ())),
    "tn": (((0,), (0,)), ((), ())),
    "bnn": (((2,), (1,)), ((0,), (0,))),
    "bnt": (((2,), (2,)), ((0,), (0,))),
    "btn": (((1,), (1,)), ((0,), (0,))),
}


def _raw_dot(a, b, kind, exact):
    if exact:
        prec = lax.Precision.HIGH if exact == "x3" else lax.Precision.HIGHEST
        return lax.dot_general(a, b, _DIMS[kind], precision=prec, preferred_element_type=F32)
    return lax.dot_general(a.astype(BF16), b.astype(BF16), _DIMS[kind], preferred_element_type=F32)


@functools.partial(jax.custom_vjp, nondiff_argnums=(2, 3))
def _dot(a, b, kind, exact):
    return _raw_dot(a, b, kind, exact)


def _dot_fwd(a, b, kind, exact):
    return _raw_dot(a, b, kind, exact), (a, b)


def _dot_bwd(kind, exact, res, g):
    a, b = res
    pre = kind[:-2]
    nn, nt, tn = pre + "nn", pre + "nt", pre + "tn"
    if kind == nn:
        return _dot(g, b, nt, exact), _dot(a, g, tn, exact)
    if kind == nt:
        return _dot(g, b, nn, exact), _dot(g, a, tn, exact)
    return _dot(b, g, nt, exact), _dot(a, g, nn, exact)


_dot.defvjp(_dot_fwd, _dot_bwd)


def _silu(x):
    return x * jax.nn.sigmoid(x)


def _f_rms(x, gain):
    return x * lax.rsqrt(jnp.mean(x * x, axis=-1, keepdims=True) + EPS) * gain


def _f_dn_pre(xs0, xs1, xs2, xs3, ba, cw, alog, dtb):
    rows = xs0.shape[0]
    c = xs0 * cw[0:1] + xs1 * cw[1:2] + xs2 * cw[2:3] + xs3 * cw[3:4]
    qkv = _silu(c)
    qs, ks, bbs, gbs = [], [], [], []
    for h in range(DN_H):
        qh = qkv[:, h * DH:(h + 1) * DH]
        kh = qkv[:, DNW + h * DH:DNW + (h + 1) * DH]
        qs.append(qh * lax.rsqrt(jnp.sum(qh * qh, axis=-1, keepdims=True) + EPS) * (DH ** -0.5))
        ks.append(kh * lax.rsqrt(jnp.sum(kh * kh, axis=-1, keepdims=True) + EPS))
        beta = jax.nn.sigmoid(ba[:, h:h + 1])
        ar = ba[:, DN_H + h:DN_H + h + 1] + dtb[:, h:h + 1]
        softplus = jnp.maximum(ar, 0.0) + jnp.log1p(jnp.exp(-jnp.abs(ar)))
        g = -jnp.exp(alog[:, h:h + 1]) * softplus
        bbs.append(jnp.broadcast_to(beta, (rows, DH)))
        gbs.append(jnp.broadcast_to(g, (rows, DH)))
    return (jnp.concatenate(qs, axis=1), jnp.concatenate(ks, axis=1), qkv[:, 2 * DNW:],
            jnp.concatenate(bbs, axis=1), jnp.concatenate(gbs, axis=1))


def _f_dn_post(o, z, gain):
    ys = []
    for h in range(DN_H):
        oh = o[:, h * DH:(h + 1) * DH]
        zh = z[:, h * DH:(h + 1) * DH]
        ys.append(oh * lax.rsqrt(jnp.mean(oh * oh, axis=-1, keepdims=True) + EPS) * gain * _silu(zh))
    return jnp.concatenate(ys, axis=1)


def _f_merge(pa, pb, ga, gb):
    return jax.nn.sigmoid(ga) * pa + jax.nn.sigmoid(gb) * pb


@jax.custom_vjp
def _f_swiglu(g, u):
    return _silu(g) * u


def _f_swiglu_fwd(g, u):
    return _silu(g) * u, (g, u)


def _f_swiglu_bwd(res, d):
    g, u = res
    s = jax.nn.sigmoid(g)
    act = g * s
    return d * u * (s + act * (1.0 - s)), d * act


_f_swiglu.defvjp(_f_swiglu_fwd, _f_swiglu_bwd)


@jax.custom_vjp
def _unit_lower_inverse(a):
    c = a.shape[-1]
    eye = (lax.broadcasted_iota(jnp.int32, a.shape, 1) == lax.broadcasted_iota(jnp.int32, a.shape, 2)).astype(F32)
    p = -a
    t = eye + p
    for _ in range(max(c.bit_length() - 2, 0)):
        p = _raw_dot(p, p, "bnn", "x3")
        t = t + _raw_dot(t, p, "bnn", "x3")
    return t


def _unit_lower_inverse_fwd(a):
    t = _unit_lower_inverse(a)
    return t, t


def _unit_lower_inverse_bwd(t, g):
    return (-_raw_dot(_raw_dot(t, g, "btn", "x3"), t, "bnt", "x3"),)


_unit_lower_inverse.defvjp(_unit_lower_inverse_fwd, _unit_lower_inverse_bwd)


@jax.custom_vjp
def _known_inverse(a, t):
    return t


def _known_inverse_fwd(a, t):
    return t, t


def _known_inverse_bwd(t, g):
    return _unit_lower_inverse_bwd(t, g)[0], jnp.zeros_like(t)


_known_inverse.defvjp(_known_inverse_fwd, _known_inverse_bwd)


def _f_chunk(q, k, v, gb, bb, s, t_known=None, with_t=False):
    c = CHUNK
    nh = q.shape[0]
    ii = lax.broadcasted_iota(jnp.int32, (nh, c, c), 1)
    jj = lax.broadcasted_iota(jnp.int32, (nh, c, c), 2)
    incl = ii >= jj
    strict = ii > jj
    eye = (ii == jj).astype(F32)
    gcb = _dot(incl.astype(F32), gb, "bnn", "x3")
    lane0 = (lax.broadcasted_iota(jnp.int32, (nh, c, DH), 2) == 0).astype(F32)
    gcol = gcb[:, :, :c]
    grow = _dot(lane0, gcb, "bnt", "x3")
    decay = jnp.where(incl, jnp.exp(jnp.where(incl, gcol - grow, 0.0)), 0.0)
    kb = k * bb
    vb = v * bb
    a = jnp.where(strict, _dot(kb, k, "bnt", False) * decay, 0.0)
    t = _unit_lower_inverse(a) if t_known is None else _known_inverse(a, t_known)
    eg = jnp.exp(gcb)
    u = _dot(t, vb, "bnn", "x3")
    w = _dot(t, kb * eg, "bnn", "x3")
    qk = jnp.where(incl, _dot(q, k, "bnt", False) * decay, 0.0)
    qe = q * eg
    glast = gcb[:, c - 1:c, :]
    k_dec = k * jnp.exp(glast - gcb)
    e_last = jnp.exp(glast)
    outs = []
    for g in range(nh // DN_H):
        sl = slice(g * DN_H, (g + 1) * DN_H)
        v_new = u[sl] - _dot(w[sl], s, "bnn", False)
        outs.append(_dot(qe[sl], s, "bnn", False) + _dot(qk[sl], v_new, "bnn", False))
        s = s * e_last[sl] + _dot(k_dec[sl], v_new, "btn", False)
    o = jnp.concatenate(outs, axis=0)
    return (o, s, t) if with_t else (o, s)


def _f_swa(q8, kp, kc, vp, vc, bias8, qg, kg, sink, mask):
    kb = jnp.concatenate([kp, kc], axis=1)
    vb = jnp.concatenate([vp, vc], axis=1)
    kn = kb * lax.rsqrt(jnp.mean(kb * kb, axis=-1, keepdims=True) + EPS) * kg

    def rows(per_head):
        return jnp.stack([jnp.concatenate([per_head(kv, g) for g in range(SWA_G)], axis=0)
                          for kv in range(SWA_KV)], axis=0)

    qq = rows(lambda kv, g: q8[kv * SWA_G + g])
    qn = qq * lax.rsqrt(jnp.mean(qq * qq, axis=-1, keepdims=True) + EPS) * qg * (SWA_D ** -0.5)
    lg = _dot(qn, kn, "bnt", False) + rows(lambda kv, g: bias8[kv * SWA_G + g])
    lg = jnp.where(rows(lambda kv, g: mask), lg, NEG)
    sk = rows(lambda kv, g: jnp.broadcast_to(sink[kv][:, g:g + 1], (BLK, 1)))
    m = lax.stop_gradient(jnp.maximum(jnp.max(lg, axis=-1, keepdims=True), sk))
    p = jnp.exp(lg - m)
    den = jnp.sum(p, axis=-1, keepdims=True) + jnp.exp(sk - m)
    out = _dot(p * (1.0 / den), vb, "bnn", False)
    return jnp.stack([out[kv, g * BLK:(g + 1) * BLK] for kv in range(SWA_KV) for g in range(SWA_G)], axis=0)


def _bdot(a, b, kind="nn"):
    return lax.dot_general(a.astype(BF16), b.astype(BF16), _DIMS[kind], preferred_element_type=F32)


def _pc(kern, name, grid, in_specs, out_specs, out_shape, scratch=()):
    return pl.pallas_call(
        kern, name=name, grid=grid, in_specs=in_specs, out_specs=out_specs, out_shape=out_shape,
        scratch_shapes=list(scratch), compiler_params=_cparams(dimension_semantics=("arbitrary",) * len(grid)))


def _mm(a, b, kind, out_dtype, tm, tn, name):
    if kind == "tn":
        k, m = a.shape
    else:
        m, k = a.shape
    n = b.shape[0] if kind == "nt" else b.shape[1]
    tm, tn = min(tm, m), min(tn, n)
    assert m % tm == 0 and n % tn == 0, (name, a.shape, b.shape, tm, tn)

    def kern(a_ref, b_ref, o_ref):
        o_ref[...] = _bdot(a_ref[...], b_ref[...], kind).astype(o_ref.dtype)

    a_spec = pl.BlockSpec((k, tm), lambda i, j: (0, i)) if kind == "tn" else pl.BlockSpec((tm, k), lambda i, j: (i, 0))
    b_spec = pl.BlockSpec((tn, k), lambda i, j: (j, 0)) if kind == "nt" else pl.BlockSpec((k, tn), lambda i, j: (0, j))
    return _pc(kern, name, (m // tm, n // tn), [a_spec, b_spec], pl.BlockSpec((tm, tn), lambda i, j: (i, j)),
               SDS((m, n), out_dtype))(a, b)


def _rows(body, name, m, tm, row_ins, full_ins, row_outs, acc_outs=()):
    n_r, n_f, n_o, n_a = len(row_ins), len(full_ins), len(row_outs), len(acc_outs)
    assert m % tm == 0

    def kern(*refs):
        r = refs[:n_r]
        f = refs[n_r:n_r + n_f]
        o = refs[n_r + n_f:n_r + n_f + n_o]
        acc = refs[n_r + n_f + n_o:]
        outs, sums = body([x[...] for x in r], [x[...] for x in f])
        for ref, val in zip(o, outs, strict=True):
            ref[...] = val.astype(ref.dtype)
        if n_a:
            @pl.when(pl.program_id(0) == 0)
            def _():
                for ref in acc:
                    ref[...] = jnp.zeros(ref.shape, F32)

            for ref, val in zip(acc, sums, strict=True):
                ref[...] += val

    in_specs = [pl.BlockSpec((tm, w), functools.partial(lambda i, cb: (i, cb), cb=cb)) for _, w, cb in row_ins]
    in_specs += [pl.BlockSpec(x.shape, lambda i: (0, 0)) for x in full_ins]
    out_specs = [pl.BlockSpec((tm, w), lambda i: (i, 0)) for w, _ in row_outs]
    out_specs += [pl.BlockSpec(s, lambda i: (0, 0)) for s in acc_outs]
    out_shape = [SDS((m, w), dt) for w, dt in row_outs]
    out_shape += [SDS(s, F32) for s in acc_outs]
    return _pc(kern, name, (m // tm,), in_specs, out_specs, out_shape)(*[x for x, _, _ in row_ins], *full_ins)


def _whole(x):
    return (x, x.shape[1], 0)


def _resident(shape):
    return pl.BlockSpec(shape, lambda i: (0,) * len(shape), pipeline_mode=pl.Buffered(1))


def _row_pieces(tm, piece):
    piece = min(piece, tm)
    return [slice(r, r + piece) for r in range(0, tm, piece)]


def _zero_first(refs):
    @pl.when(pl.program_id(0) == 0)
    def _():
        for ref in refs:
            ref[...] = jnp.zeros(ref.shape, F32)


GROUP = 4


def _heads(ref):
    return jnp.stack([ref[g * CHUNK:(g + 1) * CHUNK, h * DH:(h + 1) * DH]
                      for g in range(GROUP) for h in range(DN_H)], axis=0)


def _unheads(ref, val):
    for g in range(GROUP):
        for h in range(DN_H):
            ref[g * CHUNK:(g + 1) * CHUNK, h * DH:(h + 1) * DH] = val[g * DN_H + h]


def _dn_chunks_fwd(q, k, v, gb, bb):
    s_len = q.shape[0]
    ng = s_len // (GROUP * CHUNK)

    def kern(q_ref, k_ref, v_ref, g_ref, b_ref, o_ref, sall_ref, t_ref, state):
        _zero_first([state])
        s = state[...]
        sall_ref[0] = s
        o, s_new, t = _f_chunk(*[_heads(r) for r in (q_ref, k_ref, v_ref, g_ref, b_ref)], s, with_t=True)
        _unheads(o_ref, o)
        t_ref[0] = t
        state[...] = s_new

    blk = pl.BlockSpec((GROUP * CHUNK, DNW), lambda c: (c, 0))
    return _pc(kern, "dn_chunks_fwd", (ng,), [blk] * 5,
               [blk, pl.BlockSpec((1, DN_H, DH, DH), lambda c: (c, 0, 0, 0)),
                pl.BlockSpec((1, GROUP * DN_H, CHUNK, CHUNK), lambda c: (c, 0, 0, 0))],
               [SDS((s_len, DNW), F32), SDS((ng, DN_H, DH, DH), F32), SDS((ng, GROUP * DN_H, CHUNK, CHUNK), F32)],
               scratch=[pltpu.VMEM((DN_H, DH, DH), F32)])(q, k, v, gb, bb)


def _dn_chunks_bwd(q, k, v, gb, bb, s_all, t_all, d_o):
    s_len = q.shape[0]
    ng = s_len // (GROUP * CHUNK)

    def kern(q_ref, k_ref, v_ref, g_ref, b_ref, sall_ref, t_ref, do_ref, dq_ref, dk_ref, dv_ref, dg_ref, db_ref,
             dstate):
        _zero_first([dstate])
        fn = functools.partial(_f_chunk, t_known=t_ref[0])
        _, vjp = jax.vjp(fn, *[_heads(r) for r in (q_ref, k_ref, v_ref, g_ref, b_ref)], sall_ref[0])
        *d_ins, ds = vjp((_heads(do_ref), dstate[...]))
        for ref, val in zip((dq_ref, dk_ref, dv_ref, dg_ref, db_ref), d_ins, strict=True):
            _unheads(ref, val)
        dstate[...] = ds

    blk = pl.BlockSpec((GROUP * CHUNK, DNW), lambda c: (ng - 1 - c, 0))
    return _pc(kern, "dn_chunks_bwd", (ng,),
               [blk] * 5 + [pl.BlockSpec((1, DN_H, DH, DH), lambda c: (ng - 1 - c, 0, 0, 0)),
                            pl.BlockSpec((1, GROUP * DN_H, CHUNK, CHUNK), lambda c: (ng - 1 - c, 0, 0, 0)), blk],
               [blk] * 5, [SDS((s_len, DNW), F32)] * 5,
               scratch=[pltpu.VMEM((DN_H, DH, DH), F32)])(q, k, v, gb, bb, s_all, t_all, d_o)


def _t5_bucket_table():
    qi = np.arange(BLK)[:, None]
    kj = np.arange(2 * BLK)[None, :]
    dist = BLK + qi - kj
    n = np.maximum(dist, 0)
    max_exact = NBUCKET // 2
    nf = np.maximum(n, 1).astype(np.float32)
    large = max_exact + (np.log(nf / np.float32(max_exact)) / np.float32(math.log(MAXDIST / max_exact))
                         * np.float32(NBUCKET - max_exact)).astype(np.int32)
    large = np.minimum(large, NBUCKET - 1)
    return np.where(n < max_exact, n, large)


def _bucket_onehot_t():
    table = _t5_bucket_table().reshape(-1)
    return (np.arange(NBUCKET)[:, None] == table[None, :]).astype(np.float32)


def _swa_mask(first):
    qi = lax.broadcasted_iota(jnp.int32, (BLK, 2 * BLK), 0)
    kj = lax.broadcasted_iota(jnp.int32, (BLK, 2 * BLK), 1)
    dist = BLK + qi - kj
    window = (dist >= 0) & (dist < BLK)
    return window & ((kj >= BLK) | jnp.logical_not(first))


def _bias_expand(rel_bias_t):
    onehot = jnp.asarray(_bucket_onehot_t())

    def kern(r_ref, oh_ref, o_ref):
        o_ref[...] = _raw_dot(r_ref[...], oh_ref[...], "nn", True)

    return pl.pallas_call(
        kern, name="bias_expand", out_shape=SDS((SWA_H, BLK * 2 * BLK), F32), compiler_params=_cparams(),
    )(rel_bias_t, onehot)


def _bias_reduce(d_bias_flat):
    onehot = jnp.asarray(_bucket_onehot_t())

    def kern(d_ref, oh_ref, o_ref):
        o_ref[...] = _raw_dot(d_ref[...], oh_ref[...], "nt", True)

    return pl.pallas_call(
        kern, name="bias_reduce", out_shape=SDS((SWA_H, NBUCKET), F32), compiler_params=_cparams(),
    )(d_bias_flat, onehot)


def _swa_specs(nb, rev):
    def blk(n):
        return (nb - 1 - n) if rev else n

    def before(n):
        return jnp.maximum(blk(n) - 1, 0)

    q_spec = pl.BlockSpec((BLK, SWAW), lambda n: (blk(n), C_SQ // SWAW))
    k_cur = pl.BlockSpec((BLK, SWAKW), lambda n: (blk(n), C_SK // SWAKW))
    k_prev = pl.BlockSpec((BLK, SWAKW), lambda n: (before(n), C_SK // SWAKW))
    v_cur = pl.BlockSpec((BLK, SWAKW), lambda n: (blk(n), C_SV // SWAKW))
    v_prev = pl.BlockSpec((BLK, SWAKW), lambda n: (before(n), C_SV // SWAKW))
    bias = pl.BlockSpec((SWA_H, BLK, 2 * BLK), lambda n: (0, 0, 0))
    gain = pl.BlockSpec((1, SWA_D), lambda n: (0, 0))
    sink = pl.BlockSpec((SWA_KV, 1, SWA_G), lambda n: (0, 0, 0))
    wide = pl.BlockSpec((BLK, SWAW), lambda n: (blk(n), 0))
    narrow = pl.BlockSpec((BLK, SWAKW), lambda n: (blk(n), 0))
    return [q_spec, k_prev, k_cur, v_prev, v_cur, bias, gain, gain, sink], wide, narrow


def _split_heads(x):
    return jnp.stack([x[:, h * SWA_D:(h + 1) * SWA_D] for h in range(x.shape[1] // SWA_D)], axis=0)


def _join_heads(x):
    return jnp.concatenate([x[h] for h in range(x.shape[0])], axis=1)


def _swa_fwd(proj, bias, qg, kg, sinks):
    s_len = proj.shape[0]
    nb = s_len // BLK
    in_specs, wide, _ = _swa_specs(nb, False)

    def kern(q_ref, kp_ref, kc_ref, vp_ref, vc_ref, b_ref, qg_ref, kg_ref, s_ref, o_ref):
        mask = _swa_mask(pl.program_id(0) == 0)
        o8 = _f_swa(*[_split_heads(r[...]) for r in (q_ref, kp_ref, kc_ref, vp_ref, vc_ref)], b_ref[...], qg_ref[...],
                    kg_ref[...], s_ref[...], mask)
        o_ref[...] = _join_heads(o8).astype(BF16)

    return _pc(kern, "swa_fwd", (nb,), in_specs, wide, SDS((s_len, SWAW), BF16))(
        proj, proj, proj, proj, proj, bias, qg, kg, sinks)


def _swa_bwd(proj, bias, qg, kg, sinks, d_out):
    s_len = proj.shape[0]
    nb = s_len // BLK
    in_specs, wide, narrow = _swa_specs(nb, True)

    def kern(q_ref, kp_ref, kc_ref, vp_ref, vc_ref, b_ref, qg_ref, kg_ref, s_ref, do_ref,
             dq_ref, dk_ref, dv_ref, db_ref, dqg_ref, dkg_ref, ds_ref, carry_k, carry_v):
        n = pl.program_id(0)
        mask = _swa_mask(n == nb - 1)
        _zero_first([carry_k, carry_v, db_ref, ds_ref, dqg_ref, dkg_ref])
        fn = functools.partial(_f_swa, mask=mask)
        _, vjp = jax.vjp(fn, *[_split_heads(r[...]) for r in (q_ref, kp_ref, kc_ref, vp_ref, vc_ref)], b_ref[...],
                         qg_ref[...], kg_ref[...], s_ref[...])
        dq, dkp, dkc, dvp, dvc, dbias, dqg, dkg, dsink = vjp(_split_heads(do_ref[...]))
        dq_ref[...] = _join_heads(dq).astype(BF16)
        dk_ref[...] = (_join_heads(dkc) + carry_k[...]).astype(BF16)
        dv_ref[...] = (_join_heads(dvc) + carry_v[...]).astype(BF16)
        carry_k[...] = _join_heads(dkp)
        carry_v[...] = _join_heads(dvp)
        db_ref[...] += dbias
        dqg_ref[...] += dqg
        dkg_ref[...] += dkg
        ds_ref[...] += dsink

    bias_spec, gain, sink = in_specs[5], in_specs[6], in_specs[8]
    return _pc(
        kern, "swa_bwd", (nb,), in_specs + [wide], [wide, narrow, narrow, bias_spec, gain, gain, sink],
        [SDS((s_len, SWAW), BF16), SDS((s_len, SWAKW), BF16), SDS((s_len, SWAKW), BF16),
         SDS((SWA_H, BLK, 2 * BLK), F32), SDS((1, SWA_D), F32), SDS((1, SWA_D), F32), SDS((SWA_KV, 1, SWA_G), F32)],
        scratch=[pltpu.VMEM((BLK, SWAKW), F32), pltpu.VMEM((BLK, SWAKW), F32)],
    )(proj, proj, proj, proj, proj, bias, qg, kg, sinks, d_out)


def _branch_merge(y_dn, y_swa, wa, wb, proj):
    s_len = y_dn.shape[0]
    tm = min(1024, s_len)

    def kern(ya_ref, yb_ref, wa_ref, wb_ref, ga_ref, gb_ref, pa_ref, pb_ref, m_ref):
        for rows in _row_pieces(tm, 128):
            pa = _bdot(ya_ref[rows, :], wa_ref[0])
            pb = _bdot(yb_ref[rows, :], wb_ref[0])
            pa_ref[rows, :] = pa.astype(BF16)
            pb_ref[rows, :] = pb.astype(BF16)
            m_ref[rows, :] = _f_merge(pa, pb, ga_ref[rows, :], gb_ref[rows, :]).astype(BF16)

    y_spec = pl.BlockSpec((tm, DNW), lambda i, s: (i, 0))
    w_spec = pl.BlockSpec((1, DNW, CSH), lambda i, s: (s, 0, 0))
    o_spec = pl.BlockSpec((tm, CSH), lambda i, s: (i, s))
    ga_spec = pl.BlockSpec((tm, CSH), lambda i, s: (i, C_GATE // CSH + s))
    gb_spec = pl.BlockSpec((tm, CSH), lambda i, s: (i, (C_GATE + D) // CSH + s))
    return _pc(kern, "branch_merge", (s_len // tm, N_CHIPS), [y_spec, y_spec, w_spec, w_spec, ga_spec, gb_spec],
               [o_spec] * 3, [SDS((s_len, D), BF16)] * 3,
               )(y_dn, y_swa, wa, wb, proj, proj)


def _in_proj(x, gain, w_in_p):
    s_len = x.shape[0]
    tm = min(512, s_len)

    def kern(x_ref, g_ref, w_ref, h_ref, p_ref):
        h = _f_rms(x_ref[...], g_ref[...]).astype(BF16)
        h_ref[...] = h
        p_ref[...] = _bdot(h, w_ref[...])

    row = pl.BlockSpec((tm, D), lambda i: (i, 0))
    return _pc(kern, "in_proj", (s_len // tm,),
               [row, pl.BlockSpec((1, D), lambda i: (0, 0)), _resident((D, PW))],
               [row, pl.BlockSpec((tm, PW), lambda i: (i, 0))],
               [SDS((s_len, D), BF16), SDS((s_len, PW), F32)])(x, gain, w_in_p)


def _out_proj(merged, w_out, x, gain):
    s_len = x.shape[0]
    tm = min(512, s_len)

    def kern(m_ref, w_ref, x_ref, g_ref, x1_ref, h2_ref):
        x1 = x_ref[...] + _bdot(m_ref[...], w_ref[...])
        x1_ref[...] = x1
        h2_ref[...] = _f_rms(x1, g_ref[...]).astype(BF16)

    row = pl.BlockSpec((tm, D), lambda i: (i, 0))
    return _pc(kern, "out_proj", (s_len // tm,),
               [row, _resident((D, D)), row, pl.BlockSpec((1, D), lambda i: (0, 0))],
               [row, row], [SDS((s_len, D), F32), SDS((s_len, D), BF16)])(merged, w_out, x, gain)


def _ffn_up(h2, wg, wu):
    s_len = h2.shape[0]
    tm = min(1024, s_len)

    def kern(h_ref, g_ref, u_ref, gt_ref, up_ref, act_ref):
        for rows in _row_pieces(tm, 256):
            h = h_ref[rows, :]
            g = _bdot(h, g_ref[0], "nt")
            u = _bdot(h, u_ref[0], "nt")
            gt_ref[0, rows, :] = g.astype(BF16)
            up_ref[0, rows, :] = u.astype(BF16)
            act_ref[0, rows, :] = _f_swiglu(g, u).astype(BF16)

    w_spec = pl.BlockSpec((1, FSH, D), lambda s, i: (s, 0, 0))
    o_spec = pl.BlockSpec((1, tm, FSH), lambda s, i: (s, i, 0))
    shape = (N_CHIPS, s_len, FSH)
    return _pc(kern, "ffn_up", (N_CHIPS, s_len // tm), [pl.BlockSpec((tm, D), lambda s, i: (i, 0)), w_spec, w_spec],
               [o_spec] * 3, [SDS(shape, BF16)] * 3)(h2, wg, wu)


def _ffn_down_loss(act, wd, x1, target):
    s_len = x1.shape[0]
    tm = min(512, s_len)

    def kern(a_ref, w_ref, x_ref, t_ref, dy_ref, dyb_ref, loss_ref):
        _zero_first([loss_ref])
        for rows in _row_pieces(tm, 128):
            y = x_ref[rows, :]
            for s in range(N_CHIPS):
                y = y + _bdot(a_ref[s, rows, :], w_ref[s])
            d = y - t_ref[rows, :]
            dy = d * (1.0 / D)
            dy_ref[rows, :] = dy
            dyb_ref[rows, :] = dy.astype(BF16)
            loss_ref[...] += jnp.sum(d * d).reshape(1, 1) * (0.5 / D)

    row = pl.BlockSpec((tm, D), lambda i: (i, 0))
    return _pc(kern, "ffn_down_loss", (s_len // tm,),
               [pl.BlockSpec((N_CHIPS, tm, FSH), lambda i: (0, i, 0)),
                _resident((N_CHIPS, FSH, D)), row, row],
               [row, row, pl.BlockSpec((1, 1), lambda i: (0, 0))],
               [SDS((s_len, D), F32), SDS((s_len, D), BF16), SDS((1, 1), F32)])(act, wd, x1, target)


def _ffn_dact(dy_b, wd, gt, up):
    s_len = dy_b.shape[0]
    tm = min(1024, s_len)

    def kern(dy_ref, w_ref, gt_ref, up_ref, dg_ref, du_ref):
        w = w_ref[0]
        for rows in _row_pieces(tm, 256):
            d_act = _bdot(dy_ref[rows, :], w, "nt")
            _, vjp = jax.vjp(_f_swiglu, gt_ref[0, rows, :].astype(F32), up_ref[0, rows, :].astype(F32))
            dg, du = vjp(d_act)
            dg_ref[0, rows, :] = dg.astype(BF16)
            du_ref[0, rows, :] = du.astype(BF16)

    a_spec = pl.BlockSpec((1, tm, FSH), lambda s, i: (s, i, 0))
    shape = (N_CHIPS, s_len, FSH)
    return _pc(kern, "ffn_dact", (N_CHIPS, s_len // tm),
               [pl.BlockSpec((tm, D), lambda s, i: (i, 0)), pl.BlockSpec((1, FSH, D), lambda s, i: (s, 0, 0)),
                a_spec, a_spec],
               [a_spec, a_spec], [SDS(shape, BF16), SDS(shape, BF16)])(dy_b, wd, gt, up)


def _gw_ffn(lhs, rhs, name):
    s_len = rhs.shape[0]
    n = len(lhs)
    tn = 512

    def kern(*refs):
        g = refs[n][...]
        for i in range(n):
            refs[n + 1 + i][0] = _bdot(refs[i][0], g, "tn").astype(BF16)

    a_spec = pl.BlockSpec((1, s_len, FSH), lambda s, j: (s, 0, 0))
    o_spec = pl.BlockSpec((1, FSH, tn), lambda s, j: (s, 0, j))
    return _pc(kern, name, (N_CHIPS, D // tn), [a_spec] * n + [pl.BlockSpec((s_len, tn), lambda s, j: (0, j))],
               [o_spec] * n, [SDS((N_CHIPS, FSH, D), BF16)] * n)(*lhs, rhs)


def _ffn_dh2(d_gt, d_up, wg, wu, x1, dy, gain):
    s_len = x1.shape[0]
    tm = min(512, s_len)

    def kern(dg_ref, du_ref, wg_ref, wu_ref, x_ref, dy_ref, g_ref, dx_ref, dxb_ref, dgain_ref):
        _zero_first([dgain_ref])
        dh2 = jnp.zeros((tm, D), F32)
        for s in range(N_CHIPS):
            dh2 = dh2 + _bdot(dg_ref[s], wg_ref[s]) + _bdot(du_ref[s], wu_ref[s])
        _, vjp = jax.vjp(_f_rms, x_ref[...], g_ref[...])
        dx, dgain = vjp(dh2)
        dx1 = dx + dy_ref[...]
        dx_ref[...] = dx1
        dxb_ref[...] = dx1.astype(BF16)
        dgain_ref[...] += dgain

    row = pl.BlockSpec((tm, D), lambda i: (i, 0))
    d_spec = pl.BlockSpec((N_CHIPS, tm, FSH), lambda i: (0, i, 0))
    w_spec = _resident((N_CHIPS, FSH, D))
    vec = pl.BlockSpec((1, D), lambda i: (0, 0))
    return _pc(kern, "ffn_dh2", (s_len // tm,), [d_spec, d_spec, w_spec, w_spec, row, row, vec],
               [row, row, vec], [SDS((s_len, D), F32), SDS((s_len, D), BF16), SDS((1, D), F32)],
               )(d_gt, d_up, wg, wu, x1, dy, gain)


def _merge_bwd(dx1_b, w_out, pa, pb, proj):
    s_len = dx1_b.shape[0]
    tm = min(512, s_len)

    def kern(dx_ref, w_ref, pa_ref, pb_ref, g_ref, dpa_ref, dpb_ref, dg_ref):
        dm = _bdot(dx_ref[...], w_ref[...], "nt")
        gates = g_ref[...]
        _, vjp = jax.vjp(_f_merge, pa_ref[...].astype(F32), pb_ref[...].astype(F32), gates[:, :D], gates[:, D:])
        dpa, dpb, dga, dgb = vjp(dm)
        dpa_ref[...] = dpa.astype(BF16)
        dpb_ref[...] = dpb.astype(BF16)
        dg_ref[:, :D] = dga.astype(BF16)
        dg_ref[:, D:] = dgb.astype(BF16)

    row = pl.BlockSpec((tm, D), lambda i: (i, 0))
    return _pc(kern, "merge_bwd", (s_len // tm,),
               [row, _resident((D, D)), row, row,
                pl.BlockSpec((tm, 2 * D), lambda i: (i, C_GATE // (2 * D)))],
               [row, row, pl.BlockSpec((tm, 2 * D), lambda i: (i, 0))],
               [SDS((s_len, D), BF16), SDS((s_len, D), BF16), SDS((s_len, 2 * D), BF16)],
               )(dx1_b, w_out, pa, pb, proj)


def _d_branch(d_pa, d_pb, wa, wb):
    s_len = d_pa.shape[0]
    tm = min(512, s_len)

    def kern(da_ref, db_ref, wa_ref, wb_ref, oa_ref, ob_ref):
        acc_a = jnp.zeros((tm, DNW), F32)
        acc_b = jnp.zeros((tm, SWAW), F32)
        for s in range(N_CHIPS):
            acc_a = acc_a + _bdot(da_ref[:, s * CSH:(s + 1) * CSH], wa_ref[s], "nt")
            acc_b = acc_b + _bdot(db_ref[:, s * CSH:(s + 1) * CSH], wb_ref[s], "nt")
        oa_ref[...] = acc_a
        ob_ref[...] = acc_b

    row = pl.BlockSpec((tm, D), lambda i: (i, 0))
    w_spec = pl.BlockSpec((N_CHIPS, DNW, CSH), lambda i: (0, 0, 0))
    out = pl.BlockSpec((tm, DNW), lambda i: (i, 0))
    return _pc(kern, "d_branch", (s_len // tm,), [row, row, w_spec, w_spec], [out, out],
               [SDS((s_len, DNW), F32), SDS((s_len, SWAW), F32)])(d_pa, d_pb, wa, wb)


def _gw_branch(y_dn, y_swa, d_pa, d_pb):
    s_len = y_dn.shape[0]

    def kern(ya_ref, yb_ref, da_ref, db_ref, oa_ref, ob_ref):
        oa_ref[0] = _bdot(ya_ref[...], da_ref[...], "tn").astype(BF16)
        ob_ref[0] = _bdot(yb_ref[...], db_ref[...], "tn").astype(BF16)

    y_spec = pl.BlockSpec((s_len, DNW), lambda s: (0, 0))
    d_spec = pl.BlockSpec((s_len, CSH), lambda s: (0, s))
    o_spec = pl.BlockSpec((1, DNW, CSH), lambda s: (s, 0, 0))
    shape = (N_CHIPS, DNW, CSH)
    return _pc(kern, "gw_branch", (N_CHIPS,), [y_spec, y_spec, d_spec, d_spec], [o_spec, o_spec],
               [SDS(shape, BF16), SDS(shape, BF16)])(y_dn, y_swa, d_pa, d_pb)


def _dh_rms(d_proj, w_in_p, x, dx1, gain):
    s_len = x.shape[0]
    tm = min(512, s_len)

    def kern(dp_ref, w_ref, x_ref, r_ref, g_ref, gx_ref, dgain_ref):
        _zero_first([dgain_ref])
        dh = _bdot(dp_ref[...], w_ref[...], "nt")
        _, vjp = jax.vjp(_f_rms, x_ref[...], g_ref[...])
        dx, dgain = vjp(dh)
        gx_ref[...] = dx + r_ref[...]
        dgain_ref[...] += dgain

    row = pl.BlockSpec((tm, D), lambda i: (i, 0))
    vec = pl.BlockSpec((1, D), lambda i: (0, 0))
    return _pc(kern, "dh_rms", (s_len // tm,),
               [pl.BlockSpec((tm, PW), lambda i: (i, 0)), _resident((D, PW)), row, row, vec],
               [row, vec], [SDS((s_len, D), F32), SDS((1, D), F32)])(d_proj, w_in_p, x, dx1, gain)


HALO = 8


def _rows_down(x, n, above):
    tm = x.shape[0]
    r = pltpu.roll(x, n, 0)
    a = pltpu.roll(above, n, 0)
    top = jnp.where(lax.broadcasted_iota(jnp.int32, above.shape, 0) < n, a, r[0:HALO])
    return jnp.concatenate([top, r[HALO:tm]], axis=0)


def _rows_up(x, n, below):
    tm = x.shape[0]
    r = pltpu.roll(x, tm - n, 0)
    b = pltpu.roll(below, HALO - n, 0)
    bottom = jnp.where(lax.broadcasted_iota(jnp.int32, below.shape, 0) >= HALO - n, b, r[tm - HALO:tm])
    return jnp.concatenate([r[0:tm - HALO], bottom], axis=0)


def _conv_taps(cur_ref, prev_ref, first):
    cur = cur_ref[...]
    above = jnp.where(first, 0.0, prev_ref[...])
    return [_rows_down(cur, n, above) for n in range(CONV - 1, 0, -1)] + [cur]


def _dn_pre_specs(s_len, tm, blk):
    cur = pl.BlockSpec((tm, QKVW), lambda i: (blk(i), 0))
    prev = pl.BlockSpec((HALO, QKVW), lambda i: (jnp.maximum(blk(i) * (tm // HALO) - 1, 0), 0))
    ba = pl.BlockSpec((tm, 128), lambda i: (blk(i), C_BA // 128))
    row = pl.BlockSpec((tm, DNW), lambda i: (blk(i), 0))
    full = [pl.BlockSpec((CONV, QKVW), lambda i: (0, 0)), pl.BlockSpec((1, DN_H), lambda i: (0, 0)),
            pl.BlockSpec((1, DN_H), lambda i: (0, 0))]
    return cur, prev, ba, row, full


def _dn_pre_fwd(proj, conv_w, alog, dtb):
    s_len = proj.shape[0]
    tm = min(256, s_len)
    cur, prev, ba, row, full = _dn_pre_specs(s_len, tm, lambda i: i)

    def kern(cur_ref, prev_ref, ba_ref, cw_ref, al_ref, dt_ref, q_ref, k_ref, v_ref, bb_ref, gb_ref):
        xs = _conv_taps(cur_ref, prev_ref, pl.program_id(0) == 0)
        outs = _f_dn_pre(*xs, ba_ref[...], cw_ref[...], al_ref[...], dt_ref[...])
        for ref, val in zip((q_ref, k_ref, v_ref, bb_ref, gb_ref), outs, strict=True):
            ref[...] = val

    return _pc(kern, "dn_pre_fwd", (s_len // tm,), [cur, prev, ba] + full, [row] * 5,
               [SDS((s_len, DNW), F32)] * 5)(proj, proj, proj, conv_w, alog, dtb)


def _dn_pre_bwd(proj, conv_w, alog, dtb, cots, others):
    s_len = proj.shape[0]
    tm = min(256, s_len)
    nb = s_len // tm
    cur, prev, ba, row, full = _dn_pre_specs(s_len, tm, lambda i: nb - 1 - i)
    n_o = len(others)
    assert QKVW + sum(t.shape[1] for t in others) + 128 == C_BA + 128

    def kern(cur_ref, prev_ref, ba_ref, cw_ref, al_ref, dt_ref, dq_ref, dk_ref, dv_ref, dbb_ref, dgb_ref, *rest):
        o_refs = rest[:n_o]
        dproj_ref, dcw_ref, dal_ref, ddt_ref, *tails = rest[n_o:]
        i = pl.program_id(0)
        _zero_first([dcw_ref, dal_ref, ddt_ref] + tails)
        xs = _conv_taps(cur_ref, prev_ref, i == nb - 1)
        _, vjp = jax.vjp(_f_dn_pre, *xs, ba_ref[...], cw_ref[...], al_ref[...], dt_ref[...])
        *dxs, dba, dcw, dal, ddt = vjp((dq_ref[...], dk_ref[...], dv_ref[...], dbb_ref[...], dgb_ref[...]))
        total = dxs[CONV - 1]
        for j, t in enumerate(tails):
            n = CONV - 1 - j
            total = total + _rows_up(dxs[j], n, t[...])
            t[...] = dxs[j][0:HALO, :]
        dproj_ref[...] = jnp.concatenate(
            [total.astype(BF16)] + [r[...] for r in o_refs] + [dba.astype(BF16), jnp.zeros((tm, PW - C_BA - 128), BF16)],
            axis=1)
        dcw_ref[...] += dcw
        dal_ref[...] += dal
        ddt_ref[...] += ddt

    o_specs = [pl.BlockSpec((tm, t.shape[1]), lambda i: (nb - 1 - i, 0)) for t in others]
    return _pc(kern, "dn_pre_bwd", (nb,), [cur, prev, ba] + full + [row] * 5 + o_specs,
               [pl.BlockSpec((tm, PW), lambda i: (nb - 1 - i, 0))] + full,
               [SDS((s_len, PW), BF16), SDS((CONV, QKVW), F32), SDS((1, DN_H), F32), SDS((1, DN_H), F32)],
               scratch=[pltpu.VMEM((HALO, QKVW), F32)] * (CONV - 1))(proj, proj, proj, conv_w, alog, dtb, *cots, *others)


def _w_in_to_padded(w_sh):
    tr = 256

    def kern(w_ref, o_ref):
        full = jnp.concatenate([w_ref[s] for s in range(N_CHIPS)], axis=1)
        pieces = [full[:, o0:o0 + w] for o0, w, _ in sorted(_ORIG_PIECES, key=lambda t: t[2])]
        o_ref[...] = jnp.concatenate(pieces + [jnp.zeros((tr, PW - D_IN), w_ref.dtype)], axis=1)

    return _pc(kern, "w_in_to_padded", (D // tr,), [pl.BlockSpec((N_CHIPS, tr, D_IN // N_CHIPS), lambda i: (0, i, 0))],
               pl.BlockSpec((tr, PW), lambda i: (i, 0)), SDS((D, PW), w_sh.dtype))(w_sh)


def _padded_to_w_in(g):
    tr = 256
    csh = D_IN // N_CHIPS

    def kern(g_ref, o_ref):
        x = g_ref[...]
        full = jnp.concatenate([x[:, p0:p0 + w] for _, w, p0 in _ORIG_PIECES], axis=1)
        for s in range(N_CHIPS):
            o_ref[s] = full[:, s * csh:(s + 1) * csh]

    return _pc(kern, "padded_to_w_in", (D // tr,), [pl.BlockSpec((tr, PW), lambda i: (i, 0))],
               pl.BlockSpec((N_CHIPS, tr, csh), lambda i: (0, i, 0)), SDS((N_CHIPS, D, csh), g.dtype))(g)


def _local_step(x, target, wts):
    s_len = x.shape[0]
    tm = min(512, s_len)
    w_in_p = wts["w_in_p"]
    attn_gain = wts["attn_norm"]
    ffn_gain = wts["ffn_norm"]
    conv_w = wts["dn_conv"]
    alog, dtb, out_gain = wts["dn_a_log"], wts["dn_dt_bias"], wts["dn_out_norm"]
    qg, kg = wts["swa_q_norm"], wts["swa_k_norm"]
    sinks = wts["swa_sinks"].reshape(SWA_KV, 1, SWA_G)

    h, proj = _in_proj(x, attn_gain, w_in_p)
    q_dn, k_dn, v_dn, bb, gb = _dn_pre_fwd(proj, conv_w, alog, dtb)
    o_dn, s_all, t_all = _dn_chunks_fwd(q_dn, k_dn, v_dn, gb, bb)
    post_ins = [_whole(o_dn), (proj, DNW, C_Z // DNW)]
    (y_dn,) = _rows(lambda r, f: ([_f_dn_post(r[0], r[1], f[0])], []), "dn_post_fwd", s_len, tm, post_ins,
                    [out_gain], [(DNW, BF16)])

    bias = _bias_expand(wts["rel_bias"].T).reshape(SWA_H, BLK, 2 * BLK)
    y_swa = _swa_fwd(proj, bias, qg, kg, sinks)

    wts = {**wts, **wts["late"](y_swa)}
    p_a, p_b, merged = _branch_merge(y_dn, y_swa, wts["wa"], wts["wb"], proj)
    x1, h2 = _out_proj(merged, wts["w_out"], x, ffn_gain)
    gt, up, act = _ffn_up(h2, wts["wg"], wts["wu"])
    dy, dy_b, loss = _ffn_down_loss(act, wts["wd"], x1, target)

    grads = {}
    d_gt, d_up = _ffn_dact(dy_b, wts["wd"], gt, up)
    (grads["w_down"],) = _gw_ffn([act], dy_b, "gw_down")
    grads["w_gate"], grads["w_up"] = _gw_ffn([d_gt, d_up], h2, "gw_gate_up")
    token = wts["send_ffn"](grads)
    dx1, dx1_b, grads["ffn_norm"] = _ffn_dh2(d_gt, d_up, wts["wg"], wts["wu"], x1, dy,
                                             ffn_gain + token[0:1, 0:1])
    grads["w_out"] = _mm(merged, dx1_b, "tn", BF16, 512, 512, "gw_out")
    d_pa, d_pb, d_gr = _merge_bwd(dx1_b, wts["w_out"], p_a, p_b, proj)
    d_ydn, d_yswa = _d_branch(d_pa, d_pb, wts["wa"], wts["wb"])
    grads["w_branch_dn"], grads["w_branch_swa"] = _gw_branch(y_dn, y_swa, d_pa, d_pb)
    token = wts["send_early"](grads)
    qg_t = qg + token[0:1, 0:1]
    out_gain_t = out_gain + token[0:1, 0:1]

    d_sq, d_sk, d_sv, d_bias, grads["swa_q_norm"], grads["swa_k_norm"], d_sinks = _swa_bwd(
        proj, bias, qg_t, kg, sinks, d_yswa)
    grads["swa_sinks"] = d_sinks.reshape(1, SWA_H)
    grads["rel_bias"] = _bias_reduce(d_bias.reshape(SWA_H, BLK * 2 * BLK)).T

    def post_bwd(r, f):
        _, vjp = jax.vjp(_f_dn_post, r[0], r[1], f[0])
        d_o, d_z, d_gain = vjp(r[2])
        return [d_o, d_z], [d_gain]

    d_o, d_z, grads["dn_out_norm"] = _rows(post_bwd, "dn_post_bwd", s_len, tm, post_ins + [_whole(d_ydn)], [out_gain_t],
                                           [(DNW, F32), (DNW, BF16)], [(1, DH)])
    d_q, d_k, d_v, d_gb, d_bb = _dn_chunks_bwd(q_dn, k_dn, v_dn, gb, bb, s_all, t_all, d_o)

    d_proj, grads["dn_conv"], grads["dn_a_log"], grads["dn_dt_bias"] = _dn_pre_bwd(
        proj, conv_w, alog, dtb, (d_q, d_k, d_v, d_bb, d_gb), (d_z, d_gr, d_sq, d_sk, d_sv))
    grads["w_in_p"] = _mm(h, d_proj, "tn", BF16, 512, 1024, "gw_in")
    token = wts["send_in"](grads["w_in_p"])
    grad_x, grads["attn_norm"] = _dh_rms(d_proj, w_in_p, x, dx1, attn_gain + token[0:1, 0:1])
    return loss, grad_x, grads


_HBM = pl.BlockSpec(memory_space=pl.ANY)


def _place():
    return lax.axis_index("x"), lax.axis_index("y"), lax.axis_index("c")


def _other_chips(x, y):
    return [(1 - x, y), (x, 1 - y), (1 - x, 1 - y)]


def _rcopy(src, dst, send_sems, recv_sems, k, to):
    return pltpu.make_async_remote_copy(src_ref=src, dst_ref=dst, send_sem=send_sems.at[k], recv_sem=recv_sems.at[k],
                                        device_id=to, device_id_type=MESH)


def _comm_call(body, name, ins, out_shapes, n_remote, landing=0):
    first = len(ins) - landing
    return pl.pallas_call(
        body, name=name, in_specs=[_HBM] * len(ins), out_specs=[_HBM] * len(out_shapes), out_shape=out_shapes,
        scratch_shapes=[pltpu.SemaphoreType.DMA((n_remote,)), pltpu.SemaphoreType.DMA((n_remote,))],
        input_output_aliases={first + i: i for i in range(landing)},
        compiler_params=_cparams(has_side_effects=True),
    )(*ins)


def _own_slot(blocks, chip):
    return [lax.dynamic_update_slice(lax.empty((N_CHIPS,) + b.shape, b.dtype), b[None], (chip, 0, 0)) for b in blocks]


def _gather_weights(ws, chip):
    n = len(ws)
    halves = [w.shape[0] // 2 for w in ws]

    def body(*refs):
        w_refs, o_refs = refs[:n], refs[2 * n:3 * n]
        send_sems, recv_sems = refs[3 * n:]
        x, y, c = _place()
        s = 2 * x + y
        sib = (x, y, 1 - c)
        chips = _other_chips(x, y)

        def rows(i, half):
            return pl.ds(half * halves[i], halves[i])

        first = []
        for j, (cx, cy) in enumerate(chips):
            for i in range(n):
                cp = _rcopy(w_refs[i].at[rows(i, c), :], o_refs[i].at[s, rows(i, c), :], send_sems, recv_sems,
                            j * n + i, (cx, cy, c))
                cp.start()
                first.append(cp)
        passed = []
        for j, (cx, cy) in enumerate(chips):
            sj = 2 * cx + cy
            for i in range(n):
                blk = o_refs[i].at[sj, rows(i, c), :]
                _rcopy(blk, blk, send_sems, recv_sems, j * n + i, (cx, cy, c)).wait_recv()
                cp = _rcopy(blk, blk, send_sems, recv_sems, (3 + j) * n + i, sib)
                cp.start()
                passed.append(cp)
        for j, (cx, cy) in enumerate(chips):
            sj = 2 * cx + cy
            for i in range(n):
                blk = o_refs[i].at[sj, rows(i, 1 - c), :]
                _rcopy(blk, blk, send_sems, recv_sems, (3 + j) * n + i, sib).wait_recv()
        for cp in first + passed:
            cp.wait_send()

    return _comm_call(body, "gather_weights", list(ws) + _own_slot(ws, chip),
                      [SDS((N_CHIPS,) + w.shape, w.dtype) for w in ws], 6 * n, landing=n)


_HBM_ONLY = pl.BlockSpec(memory_space=pltpu.HBM)
_SEM = pl.BlockSpec(memory_space=pltpu.SEMAPHORE)
_DATAFLOW = pltpu.SideEffectType.DATAFLOW_SIDE_EFFECTING


def _in_hbm(a):
    return pltpu.with_memory_space_constraint(a, pltpu.HBM)


def _gather_windows(blocks):
    halves = [b.shape[0] // 2 for b in blocks]

    def src_at(ref, i, c, sj):
        return ref.at[pl.ds(c * halves[i], halves[i]), :]

    def dst_at(ref, i, c, s_from):
        return ref.at[s_from, pl.ds(c * halves[i], halves[i]), :]

    return src_at, dst_at


def _exchange_windows():
    return (lambda ref, i, c, sj: ref.at[sj]), (lambda ref, i, c, s_from: ref.at[s_from])


def _swap_windows(gs):
    halves = [g.shape[1] // 2 for g in gs]
    return ((lambda ref, i, c, tag: ref.at[:, pl.ds((1 - c) * halves[i], halves[i]), :]),
            (lambda ref, i, c, slot: ref))


def _chip_peers(x, y, c):
    return [(2 * cx + cy, (cx, cy, c), 2 * x + y, 2 * cx + cy) for cx, cy in _other_chips(x, y)]


def _sibling_peer(x, y, c):
    return [(0, (x, y, 1 - c), 0, 0)]


def _split_start(name, ws, lands, dep, windows, peers=_chip_peers, n_peers=3):
    n = len(ws)
    src_at, dst_at = windows

    def body(*refs):
        w_refs, l_refs = refs[:n], refs[n:2 * n]
        send_sems, recv_sems = refs[2 * n + 1], refs[2 * n + 2]
        token = refs[-1]
        x, y, c = _place()
        for j, (tag, dev, there, _) in enumerate(peers(x, y, c)):
            for i in range(n):
                _rcopy(src_at(w_refs[i], i, c, tag), dst_at(l_refs[i], i, c, there), send_sems, recv_sems,
                       j * n + i, dev).start()
        token[...] = jnp.zeros_like(token)

    outs = pl.pallas_call(
        body, name=name,
        out_shape=(pltpu.SemaphoreType.DMA((n_peers * n,)), pltpu.SemaphoreType.DMA((n_peers * n,)),
                   *[pltpu.HBM(w.shape, w.dtype) for w in ws], *[pltpu.HBM(t.shape, t.dtype) for t in lands],
                   SDS((8, 128), F32)),
        in_specs=[_HBM_ONLY] * (2 * n) + [pl.BlockSpec(memory_space=pl.ANY)],
        out_specs=(_SEM, _SEM, *[_HBM_ONLY] * (2 * n), pl.BlockSpec(memory_space=pltpu.VMEM)),
        input_output_aliases={i: 2 + i for i in range(2 * n)},
        compiler_params=pltpu.CompilerParams(has_side_effects=_DATAFLOW),
    )(*[_in_hbm(w) for w in ws], *[_in_hbm(t) for t in lands], dep)
    return outs[0], outs[1], outs[2:2 + n], outs[2 + n:2 + 2 * n], outs[-1]


def _split_wait(name, w_thru, l_thru, send_sems, recv_sems, after, windows, peers=_chip_peers, with_sources=False):
    n = len(w_thru)
    src_at, dst_at = windows

    def body(*refs):
        w_refs, l_refs = refs[:n], refs[n:2 * n]
        send_sems, recv_sems = refs[2 * n], refs[2 * n + 1]
        x, y, c = _place()
        for j, (tag, dev, _, here) in enumerate(peers(x, y, c)):
            for i in range(n):
                cp = _rcopy(src_at(w_refs[i], i, c, tag), dst_at(l_refs[i], i, c, here), send_sems, recv_sems,
                            j * n + i, dev)
                cp.wait_send()
                cp.wait_recv()

    outs = pl.pallas_call(
        body, name=name,
        out_shape=[pltpu.HBM(w.shape, w.dtype) for w in w_thru] + [pltpu.HBM(t.shape, t.dtype) for t in l_thru],
        in_specs=[_HBM_ONLY] * (2 * n) + [_SEM, _SEM, pl.BlockSpec(memory_space=pl.ANY)],
        out_specs=[_HBM_ONLY] * (2 * n),
        input_output_aliases={i: i for i in range(2 * n)},
        compiler_params=pltpu.CompilerParams(has_side_effects=_DATAFLOW),
    )(*w_thru, *l_thru, send_sems, recv_sems, after)
    return (outs[:n], outs[n:]) if with_sources else outs[n:]


def _sibling_fill(lands):
    n = len(lands)
    halves = [t.shape[1] // 2 for t in lands]

    def body(*refs):
        o_refs = refs[n:2 * n]
        send_sems, recv_sems = refs[2 * n:]
        x, y, c = _place()
        sib = (x, y, 1 - c)
        chips = _other_chips(x, y)
        sent = []
        for j, (cx, cy) in enumerate(chips):
            for i in range(n):
                blk = o_refs[i].at[2 * cx + cy, pl.ds(c * halves[i], halves[i]), :]
                cp = _rcopy(blk, blk, send_sems, recv_sems, j * n + i, sib)
                cp.start()
                sent.append(cp)
        for j, (cx, cy) in enumerate(chips):
            for i in range(n):
                blk = o_refs[i].at[2 * cx + cy, pl.ds((1 - c) * halves[i], halves[i]), :]
                _rcopy(blk, blk, send_sems, recv_sems, j * n + i, sib).wait_recv()
        for cp in sent:
            cp.wait_send()

    return _comm_call(body, "sibling_fill", list(lands), [SDS(t.shape, t.dtype) for t in lands], 3 * n, landing=n)


def _swap_halves(gs, name):
    n = len(gs)
    halves = [g.shape[1] // 2 for g in gs]

    def body(*refs):
        g_refs, o_refs = refs[:n], refs[n:2 * n]
        send_sems, recv_sems = refs[2 * n:]
        x, y, c = _place()
        cps = [_rcopy(g_refs[i].at[:, pl.ds((1 - c) * halves[i], halves[i]), :], o_refs[i], send_sems, recv_sems, i,
                      (x, y, 1 - c)) for i in range(n)]
        for cp in cps:
            cp.start()
        for cp in cps:
            cp.wait()

    return _comm_call(body, name, gs, [SDS((N_CHIPS, h, g.shape[2]), g.dtype) for g, h in zip(gs, halves)], n)


def _swap_reduced(rs, name):
    n = len(rs)

    def body(*refs):
        r_refs, o_refs = refs[:n], refs[n:2 * n]
        send_sems, recv_sems = refs[2 * n:]
        x, y, c = _place()
        cps = [_rcopy(r_refs[i], o_refs[i], send_sems, recv_sems, i, (x, y, 1 - c)) for i in range(n)]
        for cp in cps:
            cp.start()
        for cp in cps:
            cp.wait()

    return _comm_call(body, name, rs, [SDS(r.shape, r.dtype) for r in rs], n)


def _all_sum_small(vec, name):
    n_dev = 8
    flips = [(bx, by, bc) for bx in (0, 1) for by in (0, 1) for bc in (0, 1)][1:]

    def body(v_ref, out_ref, gath, send_sems, recv_sems):
        x, y, c = _place()
        me = 4 * x + 2 * y + c
        gath[me] = v_ref[...]
        sent = []
        for k, (bx, by, bc) in enumerate(flips):
            peer = (x ^ bx, y ^ by, c ^ bc)
            cp = _rcopy(v_ref, gath.at[me], send_sems, recv_sems, k, peer)
            cp.start()
            sent.append(cp)
        for k, (bx, by, bc) in enumerate(flips):
            peer = (x ^ bx, y ^ by, c ^ bc)
            _rcopy(v_ref, gath.at[4 * peer[0] + 2 * peer[1] + peer[2]], send_sems, recv_sems, k, peer).wait_recv()
        for cp in sent:
            cp.wait_send()
        acc = gath[0]
        for d in range(1, n_dev):
            acc = acc + gath[d]
        out_ref[...] = acc

    vm = pl.BlockSpec(memory_space=pltpu.VMEM)
    return pl.pallas_call(
        body, name=name, in_specs=[vm], out_specs=vm, out_shape=SDS(vec.shape, F32),
        scratch_shapes=[pltpu.VMEM((n_dev,) + vec.shape, F32), pltpu.SemaphoreType.DMA((7,)),
                        pltpu.SemaphoreType.DMA((7,))],
        compiler_params=_cparams(has_side_effects=True),
    )(vec)


def _pack_small(vals, extra=None):
    parts = [vals[n].reshape(-1).astype(F32) for n, _ in _SMALL]
    parts.append(jnp.zeros((1,), F32) if extra is None else extra.reshape(1).astype(F32))
    flat = jnp.concatenate(parts)
    flat = jnp.concatenate([flat, jnp.zeros((_SMALL_ROWS * 128 - flat.shape[0],), F32)])
    return flat.reshape(_SMALL_ROWS, 128)


def _unpack_small(packed, shapes):
    flat = packed.reshape(-1)
    return {n: flat[_SMALL_OFF[n][0]:_SMALL_OFF[n][0] + _SMALL_OFF[n][1]].reshape(shapes[n]) for n, _ in _SMALL}


def _pair_sum(gs, gots, core, name):
    n = len(gs)

    def kern(c_ref, *refs):
        for i in range(n):
            refs[2 * n + i][...] = (refs[i][...].astype(F32) + refs[n + i][...].astype(F32)).astype(BF16)

    in_specs = [pl.BlockSpec((1, t.shape[1], t.shape[2]), lambda s, c_ref: (s, c_ref[0], 0)) for t in gots]
    in_specs += [pl.BlockSpec((1, t.shape[1], t.shape[2]), lambda s, c_ref: (s, 0, 0)) for t in gots]
    out_specs = [pl.BlockSpec((1, t.shape[1], t.shape[2]), lambda s, c_ref: (s, 0, 0)) for t in gots]
    return pl.pallas_call(
        kern, name=name,
        grid_spec=pltpu.PrefetchScalarGridSpec(num_scalar_prefetch=1, grid=(N_CHIPS,), in_specs=in_specs,
                                               out_specs=out_specs),
        out_shape=[SDS(t.shape, BF16) for t in gots],
        compiler_params=_cparams(dimension_semantics=("arbitrary",)),
    )(core.reshape(1).astype(jnp.int32), *gs, *gots)


def _chip_sum(qs, name):
    n = len(qs)

    def kern(*refs):
        for i in range(n):
            acc = refs[i][0].astype(F32)
            for s in range(1, N_CHIPS):
                acc = acc + refs[i][s].astype(F32)
            refs[n + i][...] = acc

    in_specs = [pl.BlockSpec((N_CHIPS, q.shape[1] // 2, q.shape[2]), lambda j: (0, j, 0)) for q in qs]
    out_specs = [pl.BlockSpec((q.shape[1] // 2, q.shape[2]), lambda j: (j, 0)) for q in qs]
    return _pc(kern, name, (2,), in_specs, out_specs, [SDS(q.shape[1:], F32) for q in qs])(*qs)


def _adam_math(w_, g_, m_, v_):
    m_ = ADAM_B1 * m_ + (1.0 - ADAM_B1) * g_
    v_ = ADAM_B2 * v_ + (1.0 - ADAM_B2) * jnp.square(g_)
    m_hat = m_ / (1.0 - ADAM_B1 ** ADAM_STEP)
    v_hat = v_ / (1.0 - ADAM_B2 ** ADAM_STEP)
    return -ADAM_LR * (m_hat / (jnp.sqrt(v_hat) + ADAM_EPS) + ADAM_WD * w_), m_, v_


def _adamw(w, g, m, v, name):
    rows, cols = w.shape
    tr = rows
    for cand in (256, 128, 64, 32, 16, 8):
        if rows % cand == 0 and rows > cand:
            tr = cand
            break

    def kern(w_ref, g_ref, m_ref, v_ref, d_ref, nm_ref, nv_ref):
        d_ref[...], nm_ref[...], nv_ref[...] = _adam_math(w_ref[...], g_ref[...], m_ref[...], v_ref[...])

    spec = pl.BlockSpec((tr, cols), lambda i: (i, 0))
    return _pc(kern, name, (rows // tr,), [spec] * 4, [spec] * 3, [SDS(w.shape, F32)] * 3)(w, g, m, v)


def _adamw_rows1(w, g, m, v, name):
    rows, _, cols = w.shape
    tr = next(t for t in (203, 174, 128, 64, 42, 32, 29, 16, 8, 7, 6, 4, 3, 2, 1) if rows % t == 0)

    def kern(w_ref, g_ref, m_ref, v_ref, go_ref, d_ref, nm_ref, nv_ref):
        g_ = g_ref[...]
        go_ref[...] = g_
        d_ref[...], nm_ref[...], nv_ref[...] = _adam_math(w_ref[...], g_, m_ref[...], v_ref[...])

    spec = pl.BlockSpec((tr, 1, cols), lambda i: (i, 0, 0))
    return _pc(kern, name, (rows // tr,), [spec] * 4, [spec] * 4, [SDS(w.shape, F32)] * 4)(w, g, m, v)


def _adamw_big(w, mine, theirs, m, v, core, name):
    _, rows, cols = w.shape
    half = rows // 2
    tr = next(t for t in (256, 176, 128, 64, 32, 16, 8) if half % t == 0)
    nbh = half // tr

    def kern(c_ref, w_ref, a_ref, b_ref, m_ref, v_ref, g_ref, d_ref, nm_ref, nv_ref):
        g_ = jnp.where(pl.program_id(0) // nbh == c_ref[0], a_ref[...], b_ref[...])
        g_ref[0] = g_
        d_ref[0], nm_ref[0], nv_ref[0] = _adam_math(w_ref[0], g_, m_ref[0], v_ref[0])

    full = pl.BlockSpec((1, tr, cols), lambda i, c_ref: (0, i, 0))
    part = pl.BlockSpec((tr, cols), lambda i, c_ref: (i % nbh, 0))
    return pl.pallas_call(
        kern, name=name,
        grid_spec=pltpu.PrefetchScalarGridSpec(num_scalar_prefetch=1, grid=(rows // tr,),
                                               in_specs=[full, part, part, full, full], out_specs=[full] * 4),
        out_shape=[SDS(w.shape, F32)] * 4,
        compiler_params=_cparams(dimension_semantics=("arbitrary",)),
    )(core.reshape(1).astype(jnp.int32), w, mine, theirs, m, v)


_WEIGHT_NAMES = ("attn_norm", "w_in", "dn_conv", "dn_a_log", "dn_dt_bias", "dn_out_norm", "swa_q_norm", "swa_k_norm",
                 "swa_sinks", "rel_bias", "w_branch_dn", "w_branch_swa", "w_out", "ffn_norm", "w_gate", "w_up",
                 "w_down")
_CONV_SH = QKVW // N_CHIPS


def kernel(x, attn_norm, w_in, dn_conv, dn_a_log, dn_dt_bias, dn_out_norm, swa_q_norm, swa_k_norm, swa_sinks, rel_bias, w_branch_dn, w_branch_swa, w_out, ffn_norm, w_gate, w_up, w_down, loss_target, m_attn_norm, m_w_in, m_dn_conv, m_dn_a_log, m_dn_dt_bias, m_dn_out_norm, m_swa_q_norm, m_swa_k_norm, m_swa_sinks, m_rel_bias, m_w_branch_dn, m_w_branch_swa, m_w_out, m_ffn_norm, m_w_gate, m_w_up, m_w_down, v_attn_norm, v_w_in, v_dn_conv, v_dn_a_log, v_dn_dt_bias, v_dn_out_norm, v_swa_q_norm, v_swa_k_norm, v_swa_sinks, v_rel_bias, v_w_branch_dn, v_w_branch_swa, v_w_out, v_ffn_norm, v_w_gate, v_w_up, v_w_down):
    w = dict(attn_norm=attn_norm, w_in=w_in, dn_conv=dn_conv, dn_a_log=dn_a_log, dn_dt_bias=dn_dt_bias,
             dn_out_norm=dn_out_norm, swa_q_norm=swa_q_norm, swa_k_norm=swa_k_norm, swa_sinks=swa_sinks,
             rel_bias=rel_bias, w_branch_dn=w_branch_dn, w_branch_swa=w_branch_swa, w_out=w_out, ffn_norm=ffn_norm,
             w_gate=w_gate, w_up=w_up, w_down=w_down)
    m = dict(attn_norm=m_attn_norm, w_in=m_w_in, dn_conv=m_dn_conv, dn_a_log=m_dn_a_log, dn_dt_bias=m_dn_dt_bias,
             dn_out_norm=m_dn_out_norm, swa_q_norm=m_swa_q_norm, swa_k_norm=m_swa_k_norm, swa_sinks=m_swa_sinks,
             rel_bias=m_rel_bias, w_branch_dn=m_w_branch_dn, w_branch_swa=m_w_branch_swa, w_out=m_w_out,
             ffn_norm=m_ffn_norm, w_gate=m_w_gate, w_up=m_w_up, w_down=m_w_down)
    v = dict(attn_norm=v_attn_norm, w_in=v_w_in, dn_conv=v_dn_conv, dn_a_log=v_dn_a_log, dn_dt_bias=v_dn_dt_bias,
             dn_out_norm=v_dn_out_norm, swa_q_norm=v_swa_q_norm, swa_k_norm=v_swa_k_norm, swa_sinks=v_swa_sinks,
             rel_bias=v_rel_bias, w_branch_dn=v_w_branch_dn, w_branch_swa=v_w_branch_swa, w_out=v_w_out,
             ffn_norm=v_ffn_norm, w_gate=v_w_gate, w_up=v_w_up, w_down=v_w_down)
    shapes = {n: w[n].shape for n in _WEIGHT_NAMES}

    def two_d(a):
        return a.reshape(a.shape[-2], a.shape[-1]) if a.ndim == 3 else a

    core = lax.axis_index("c")
    chip = 2 * lax.axis_index("x") + lax.axis_index("y")
    small_shapes = {n: two_d(w[n]).shape for n, _ in _SMALL}
    small_shapes["dn_conv"] = (CONV, QKVW)

    conv_loc = two_d(w["dn_conv"])
    conv_part = lax.dynamic_update_slice(jnp.zeros((CONV, QKVW), F32), jnp.where(core == 0, conv_loc, 0.0),
                                         (0, chip * _CONV_SH))
    conv_full = _all_sum_small(conv_part.reshape(CONV * QKVW // 128, 128), "gather_conv").reshape(CONV, QKVW)

    flipped = ("w_gate", "w_up")

    def natural(a, n):
        return a.transpose(0, 2, 1) if n in flipped else a

    w_bf = [two_d(natural(w[n], n).astype(BF16)) for n in _BIG_NAMES]
    (w_in_g,) = _gather_weights(w_bf[:1], chip)
    windows = _gather_windows(w_bf[1:])
    after_sync = w_in_g[0, :8, :128].astype(F32) + conv_full[0:1, :128]
    send_sems, recv_sems, w_thru, l_thru, token = _split_start(
        "gather_start", w_bf[1:], _own_slot(w_bf[1:], chip), after_sync, windows)

    def late(after):
        lands = _split_wait("gather_wait", w_thru, l_thru, send_sems, recv_sems, after, windows)
        g = dict(zip(_BIG_NAMES[1:], _sibling_fill(lands)))
        return dict(wa=g["w_branch_dn"], wb=g["w_branch_swa"], w_out=g["w_out"].reshape(D, D), wg=g["w_gate"],
                    wu=g["w_up"], wd=g["w_down"])

    wts = dict(w_in_p=_w_in_to_padded(w_in_g), dn_conv=conv_full, late=late)
    for n, _ in _SMALL[:-1]:
        wts[n] = two_d(w[n])
    wts["attn_norm"] = wts["attn_norm"] + token[0:1, 0:1]

    early = {}

    ffn = {}

    def send_ffn(grads):
        gs = [grads["w_gate"], grads["w_up"], grads["w_down"]]
        lands = [lax.empty((N_CHIPS, g.shape[1] // 2, g.shape[2]), g.dtype) for g in gs]
        ffn["sems"], ffn["recv"], ffn["src"], ffn["land"], tok = _split_start(
            "swap_ffn_start", gs, lands, gs[0][0, :8, :128], _swap_windows(gs), _sibling_peer, 1)
        return tok

    def send_early(grads):
        small = [grads["w_branch_dn"], grads["w_branch_swa"], grads["w_out"].reshape(N_CHIPS, CSH, D)]
        big = [grads["w_gate"], grads["w_up"], grads["w_down"]]
        big, got_big = _split_wait("swap_ffn_wait", ffn["src"], ffn["land"], ffn["sems"], ffn["recv"], small[0],
                                   _swap_windows(big), _sibling_peer, with_sources=True)
        gots = list(_swap_halves(small, "swap_halves_early")) + list(got_big)
        parts = _pair_sum(small + list(big), gots, core, "pair_sum_early")
        own = [lax.dynamic_index_in_dim(p, chip, axis=0, keepdims=False) for p in parts]
        early["sems"], early["recv"], early["src"], early["land"], tok = _split_start(
            "exchange_start", parts, _own_slot(own, chip), parts[0][0, :8, :128], _exchange_windows())
        return tok

    last = {}

    def send_in(g_in_p):
        g_in = [_padded_to_w_in(g_in_p)]
        parts = _pair_sum(g_in, _swap_halves(g_in, "swap_halves_in"), core, "pair_sum_in")
        own = [lax.dynamic_index_in_dim(p, chip, axis=0, keepdims=False) for p in parts]
        last["sems"], last["recv"], last["src"], last["land"], tok = _split_start(
            "exchange_in_start", parts, _own_slot(own, chip), parts[0][0, :8, :128], _exchange_windows())
        return tok

    wts["send_ffn"] = send_ffn
    wts["send_early"] = send_early
    wts["send_in"] = send_in
    loss_sum, grad_x, grads = _local_step(x[0], loss_target[0], wts)

    small_sum = _all_sum_small(_pack_small(grads, loss_sum), "all_sum_small")
    loss = small_sum.reshape(-1)[_LOSS_OFF]
    g_small = _unpack_small(small_sum, small_shapes)

    q_early = _split_wait("exchange_wait", early["src"], early["land"], early["sems"], early["recv"], small_sum,
                          _exchange_windows())
    red_early = _chip_sum(list(q_early), "chip_sum_early")
    their_early = _swap_reduced(red_early, "swap_reduced_early")
    g_out, d_out, m_out, v_out = {}, {}, {}, {}
    for n, mine, other in zip(_BIG_NAMES[1:], red_early, their_early):
        res = _adamw_big(natural(w[n], n), mine, other, natural(m[n], n), natural(v[n], n), core, "adamw_" + n)
        g_out[n], d_out[n], m_out[n], v_out[n] = (natural(t, n) for t in res)

    q_in = _split_wait("exchange_in_wait", last["src"], last["land"], last["sems"], last["recv"],
                       d_out[_BIG_NAMES[-1]], _exchange_windows())
    reduced = _chip_sum(list(q_in), "chip_sum_in")
    theirs = _swap_reduced(reduced, "swap_reduced_in")

    def rows1(a):
        return a.transpose(2, 0, 1)

    def unrows1(a):
        return a.transpose(1, 2, 0)

    g_in_blk = jnp.concatenate([jnp.where(core == 0, reduced[0], theirs[0]),
                                jnp.where(core == 0, theirs[0], reduced[0])], axis=0)
    g_in_r = rows1(g_in_blk[None])
    res = _adamw_rows1(rows1(w["w_in"]), g_in_r, rows1(m["w_in"]), rows1(v["w_in"]), "adamw_w_in")
    g_out["w_in"], d_out["w_in"], m_out["w_in"], v_out["w_in"] = (unrows1(t) for t in res)
    g_conv = lax.dynamic_slice(g_small["dn_conv"], (0, chip * _CONV_SH), (CONV, _CONV_SH))
    g_out["dn_conv"] = g_conv.reshape(shapes["dn_conv"])
    d_, m_, v_ = _adamw(conv_loc, g_conv, two_d(m["dn_conv"]), two_d(v["dn_conv"]), "adamw_dn_conv")
    d_out["dn_conv"], m_out["dn_conv"], v_out["dn_conv"] = (t.reshape(shapes["dn_conv"]) for t in (d_, m_, v_))

    def packed(src):
        vals = {n: src[n] for n, _ in _SMALL[:-1]}
        vals["dn_conv"] = jnp.zeros((CONV * QKVW,), F32)
        return _pack_small(vals)

    d_s, m_s, v_s = _adamw(packed(w), small_sum, packed(m), packed(v), "adamw_small")
    d_small, m_small, v_small = (_unpack_small(t, small_shapes) for t in (d_s, m_s, v_s))
    for n, _ in _SMALL[:-1]:
        g_out[n] = g_small[n].reshape(shapes[n])
        d_out[n], m_out[n], v_out[n] = (t[n].reshape(shapes[n]) for t in (d_small, m_small, v_small))

    return (loss, grad_x[None], *[g_out[n] for n in _WEIGHT_NAMES], *[d_out[n] for n in _WEIGHT_NAMES],
            *[m_out[n] for n in _WEIGHT_NAMES], *[v_out[n] for n in _WEIGHT_NAMES])
```

```python
import functools
import math

import numpy as np
import jax
import jax.numpy as jnp
from jax import lax
from jax.experimental import pallas as pl
from jax.experimental.pallas import tpu as pltpu

F32 = jnp.float32
BF16 = jnp.bfloat16
SDS = jax.ShapeDtypeStruct

D = 1024
DN_H = 4
DH = 128
DNW = DN_H * DH
QKVW = 3 * DNW
CONV = 4
CHUNK = 64
SWA_H = 8
SWA_KV = 2
SWA_G = SWA_H // SWA_KV
SWA_D = 64
SWAW = SWA_H * SWA_D
SWAKW = SWA_KV * SWA_D
BLK = 128
NBUCKET = 32
MAXDIST = 128
DFF = 2816
D_IN = QKVW + DNW + 2 * DN_H + SWAW + 2 * SWAKW + 2 * D
EPS = 1e-6
NEG = -1e30

ADAM_LR = 0.001
ADAM_B1 = 0.9
ADAM_B2 = 0.999
ADAM_EPS = 1e-08
ADAM_WD = 0.01
ADAM_STEP = 10

C_QKV, C_Z, C_GATE, C_SQ, C_SK, C_SV, C_BA = 0, 1536, 2048, 4096, 4608, 4736, 4864
PW = 5120
_ORIG_PIECES = (
    (0, QKVW, C_QKV),
    (QKVW, DNW, C_Z),
    (QKVW + DNW, 2 * DN_H, C_BA),
    (QKVW + DNW + 2 * DN_H, SWAW, C_SQ),
    (QKVW + DNW + 2 * DN_H + SWAW, SWAKW, C_SK),
    (QKVW + DNW + 2 * DN_H + SWAW + SWAKW, SWAKW, C_SV),
    (QKVW + DNW + 2 * DN_H + SWAW + 2 * SWAKW, 2 * D, C_GATE),
)

N_CHIPS = 4
FSH = DFF // N_CHIPS
CSH = D // N_CHIPS
VMEM_LIMIT = 48 * 1024 * 1024
MESH = pl.DeviceIdType.MESH

_BIG = (
    ("w_in", D, D_IN // N_CHIPS),
    ("w_branch_dn", DNW, CSH),
    ("w_branch_swa", SWAW, CSH),
    ("w_out", CSH, D),
    ("w_gate", FSH, D),
    ("w_up", FSH, D),
    ("w_down", FSH, D),
)
_BIG_NAMES = tuple(n for n, _, _ in _BIG)

_SMALL = (
    ("attn_norm", D), ("ffn_norm", D), ("dn_out_norm", DH), ("swa_q_norm", SWA_D), ("swa_k_norm", SWA_D),
    ("swa_sinks", SWA_H), ("dn_a_log", DN_H), ("dn_dt_bias", DN_H), ("rel_bias", NBUCKET * SWA_H),
    ("dn_conv", CONV * QKVW),
)
_SMALL_OFF = {}
_o = 0
for _n, _s in _SMALL:
    _SMALL_OFF[_n] = (_o, _s)
    _o += _s
_LOSS_OFF = _o
_SMALL_ROWS = -(-(_o + 1) // (8 * 128)) * 8


def _cparams(**kw):
    return pltpu.CompilerParams(vmem_limit_bytes=VMEM_LIMIT, **kw)


_DIMS = {
    "nn": (((1,), (0,)), ((), ())),
    "nt": (((1,), (1,)), ((), ())),
    "tn": (((0,), (0,)), ((), ())),
    "bnn": (((2,), (1,)), ((0,), (0,))),
    "bnt": (((2,), (2,)), ((0,), (0,))),
    "btn": (((1,), (1,)), ((0,), (0,))),
}


def _raw_dot(a, b, kind, exact):
    if exact:
        prec = lax.Precision.HIGH if exact == "x3" else lax.Precision.HIGHEST
        return lax.dot_general(a, b, _DIMS[kind], precision=prec, preferred_element_type=F32)
    return lax.dot_general(a.astype(BF16), b.astype(BF16), _DIMS[kind], preferred_element_type=F32)


@functools.partial(jax.custom_vjp, nondiff_argnums=(2, 3))
def _dot(a, b, kind, exact):
    return _raw_dot(a, b, kind, exact)


def _dot_fwd(a, b, kind, exact):
    return _raw_dot(a, b, kind, exact), (a, b)


def _dot_bwd(kind, exact, res, g):
    a, b = res
    pre = kind[:-2]
    nn, nt, tn = pre + "nn", pre + "nt", pre + "tn"
    if kind == nn:
        return _dot(g, b, nt, exact), _dot(a, g, tn, exact)
    if kind == nt:
        return _dot(g, b, nn, exact), _dot(g, a, tn, exact)
    return _dot(b, g, nt, exact), _dot(a, g, nn, exact)


_dot.defvjp(_dot_fwd, _dot_bwd)


def _silu(x):
    return x * jax.nn.sigmoid(x)


def _f_rms(x, gain):
    return x * lax.rsqrt(jnp.mean(x * x, axis=-1, keepdims=True) + EPS) * gain


def _f_dn_pre(xs0, xs1, xs2, xs3, ba, cw, alog, dtb):
    rows = xs0.shape[0]
    c = xs0 * cw[0:1] + xs1 * cw[1:2] + xs2 * cw[2:3] + xs3 * cw[3:4]
    qkv = _silu(c)
    qs, ks, bbs, gbs = [], [], [], []
    for h in range(DN_H):
        qh = qkv[:, h * DH:(h + 1) * DH]
        kh = qkv[:, DNW + h * DH:DNW + (h + 1) * DH]
        qs.append(qh * lax.rsqrt(jnp.sum(qh * qh, axis=-1, keepdims=True) + EPS) * (DH ** -0.5))
        ks.append(kh * lax.rsqrt(jnp.sum(kh * kh, axis=-1, keepdims=True) + EPS))
        beta = jax.nn.sigmoid(ba[:, h:h + 1])
        ar = ba[:, DN_H + h:DN_H + h + 1] + dtb[:, h:h + 1]
        softplus = jnp.maximum(ar, 0.0) + jnp.log1p(jnp.exp(-jnp.abs(ar)))
        g = -jnp.exp(alog[:, h:h + 1]) * softplus
        bbs.append(jnp.broadcast_to(beta, (rows, DH)))
        gbs.append(jnp.broadcast_to(g, (rows, DH)))
    return (jnp.concatenate(qs, axis=1), jnp.concatenate(ks, axis=1), qkv[:, 2 * DNW:],
            jnp.concatenate(bbs, axis=1), jnp.concatenate(gbs, axis=1))


def _f_dn_post(o, z, gain):
    ys = []
    for h in range(DN_H):
        oh = o[:, h * DH:(h + 1) * DH]
        zh = z[:, h * DH:(h + 1) * DH]
        ys.append(oh * lax.rsqrt(jnp.mean(oh * oh, axis=-1, keepdims=True) + EPS) * gain * _silu(zh))
    return jnp.concatenate(ys, axis=1)


def _f_merge(pa, pb, ga, gb):
    return jax.nn.sigmoid(ga) * pa + jax.nn.sigmoid(gb) * pb


@jax.custom_vjp
def _f_swiglu(g, u):
    return _silu(g) * u


def _f_swiglu_fwd(g, u):
    return _silu(g) * u, (g, u)


def _f_swiglu_bwd(res, d):
    g, u = res
    s = jax.nn.sigmoid(g)
    act = g * s
    return d * u * (s + act * (1.0 - s)), d * act


_f_swiglu.defvjp(_f_swiglu_fwd, _f_swiglu_bwd)


@jax.custom_vjp
def _unit_lower_inverse(a):
    c = a.shape[-1]
    eye = (lax.broadcasted_iota(jnp.int32, a.shape, 1) == lax.broadcasted_iota(jnp.int32, a.shape, 2)).astype(F32)
    p = -a
    t = eye + p
    for _ in range(max(c.bit_length() - 2, 0)):
        p = _raw_dot(p, p, "bnn", "x3")
        t = t + _raw_dot(t, p, "bnn", "x3")
    return t


def _unit_lower_inverse_fwd(a):
    t = _unit_lower_inverse(a)
    return t, t


def _unit_lower_inverse_bwd(t, g):
    return (-_raw_dot(_raw_dot(t, g, "btn", "x3"), t, "bnt", "x3"),)


_unit_lower_inverse.defvjp(_unit_lower_inverse_fwd, _unit_lower_inverse_bwd)


@jax.custom_vjp
def _known_inverse(a, t):
    return t


def _known_inverse_fwd(a, t):
    return t, t


def _known_inverse_bwd(t, g):
    return _unit_lower_inverse_bwd(t, g)[0], jnp.zeros_like(t)


_known_inverse.defvjp(_known_inverse_fwd, _known_inverse_bwd)


def _f_chunk(q, k, v, gb, bb, s, t_known=None, with_t=False):
    c = CHUNK
    nh = q.shape[0]
    ii = lax.broadcasted_iota(jnp.int32, (nh, c, c), 1)
    jj = lax.broadcasted_iota(jnp.int32, (nh, c, c), 2)
    incl = ii >= jj
    strict = ii > jj
    eye = (ii == jj).astype(F32)
    gcb = _dot(incl.astype(F32), gb, "bnn", "x3")
    lane0 = (lax.broadcasted_iota(jnp.int32, (nh, c, DH), 2) == 0).astype(F32)
    gcol = gcb[:, :, :c]
    grow = _dot(lane0, gcb, "bnt", "x3")
    decay = jnp.where(incl, jnp.exp(jnp.where(incl, gcol - grow, 0.0)), 0.0)
    kb = k * bb
    vb = v * bb
    a = jnp.where(strict, _dot(kb, k, "bnt", False) * decay, 0.0)
    t = _unit_lower_inverse(a) if t_known is None else _known_inverse(a, t_known)
    eg = jnp.exp(gcb)
    u = _dot(t, vb, "bnn", "x3")
    w = _dot(t, kb * eg, "bnn", "x3")
    qk = jnp.where(incl, _dot(q, k, "bnt", False) * decay, 0.0)
    qe = q * eg
    glast = gcb[:, c - 1:c, :]
    k_dec = k * jnp.exp(glast - gcb)
    e_last = jnp.exp(glast)
    outs = []
    for g in range(nh // DN_H):
        sl = slice(g * DN_H, (g + 1) * DN_H)
        v_new = u[sl] - _dot(w[sl], s, "bnn", False)
        outs.append(_dot(qe[sl], s, "bnn", False) + _dot(qk[sl], v_new, "bnn", False))
        s = s * e_last[sl] + _dot(k_dec[sl], v_new, "btn", False)
    o = jnp.concatenate(outs, axis=0)
    return (o, s, t) if with_t else (o, s)


def _f_swa(q8, kp, kc, vp, vc, bias8, qg, kg, sink, mask):
    kb = jnp.concatenate([kp, kc], axis=1)
    vb = jnp.concatenate([vp, vc], axis=1)
    kn = kb * lax.rsqrt(jnp.mean(kb * kb, axis=-1, keepdims=True) + EPS) * kg

    def rows(per_head):
        return jnp.stack([jnp.concatenate([per_head(kv, g) for g in range(SWA_G)], axis=0)
                          for kv in range(SWA_KV)], axis=0)

    qq = rows(lambda kv, g: q8[kv * SWA_G + g])
    qn = qq * lax.rsqrt(jnp.mean(qq * qq, axis=-1, keepdims=True) + EPS) * qg * (SWA_D ** -0.5)
    lg = _dot(qn, kn, "bnt", False) + rows(lambda kv, g: bias8[kv * SWA_G + g])
    lg = jnp.where(rows(lambda kv, g: mask), lg, NEG)
    sk = rows(lambda kv, g: jnp.broadcast_to(sink[kv][:, g:g + 1], (BLK, 1)))
    m = lax.stop_gradient(jnp.maximum(jnp.max(lg, axis=-1, keepdims=True), sk))
    p = jnp.exp(lg - m)
    den = jnp.sum(p, axis=-1, keepdims=True) + jnp.exp(sk - m)
    out = _dot(p * (1.0 / den), vb, "bnn", False)
    return jnp.stack([out[kv, g * BLK:(g + 1) * BLK] for kv in range(SWA_KV) for g in range(SWA_G)], axis=0)


def _bdot(a, b, kind="nn"):
    return lax.dot_general(a.astype(BF16), b.astype(BF16), _DIMS[kind], preferred_element_type=F32)


def _pc(kern, name, grid, in_specs, out_specs, out_shape, scratch=()):
    return pl.pallas_call(
        kern, name=name, grid=grid, in_specs=in_specs, out_specs=out_specs, out_shape=out_shape,
        scratch_shapes=list(scratch), compiler_params=_cparams(dimension_semantics=("arbitrary",) * len(grid)))


def _mm(a, b, kind, out_dtype, tm, tn, name):
    if kind == "tn":
        k, m = a.shape
    else:
        m, k = a.shape
    n = b.shape[0] if kind == "nt" else b.shape[1]
    tm, tn = min(tm, m), min(tn, n)
    assert m % tm == 0 and n % tn == 0, (name, a.shape, b.shape, tm, tn)

    def kern(a_ref, b_ref, o_ref):
        o_ref[...] = _bdot(a_ref[...], b_ref[...], kind).astype(o_ref.dtype)

    a_spec = pl.BlockSpec((k, tm), lambda i, j: (0, i)) if kind == "tn" else pl.BlockSpec((tm, k), lambda i, j: (i, 0))
    b_spec = pl.BlockSpec((tn, k), lambda i, j: (j, 0)) if kind == "nt" else pl.BlockSpec((k, tn), lambda i, j: (0, j))
    return _pc(kern, name, (m // tm, n // tn), [a_spec, b_spec], pl.BlockSpec((tm, tn), lambda i, j: (i, j)),
               SDS((m, n), out_dtype))(a, b)


def _rows(body, name, m, tm, row_ins, full_ins, row_outs, acc_outs=()):
    n_r, n_f, n_o, n_a = len(row_ins), len(full_ins), len(row_outs), len(acc_outs)
    assert m % tm == 0

    def kern(*refs):
        r = refs[:n_r]
        f = refs[n_r:n_r + n_f]
        o = refs[n_r + n_f:n_r + n_f + n_o]
        acc = refs[n_r + n_f + n_o:]
        outs, sums = body([x[...] for x in r], [x[...] for x in f])
        for ref, val in zip(o, outs, strict=True):
            ref[...] = val.astype(ref.dtype)
        if n_a:
            @pl.when(pl.program_id(0) == 0)
            def _():
                for ref in acc:
                    ref[...] = jnp.zeros(ref.shape, F32)

            for ref, val in zip(acc, sums, strict=True):
                ref[...] += val

    in_specs = [pl.BlockSpec((tm, w), functools.partial(lambda i, cb: (i, cb), cb=cb)) for _, w, cb in row_ins]
    in_specs += [pl.BlockSpec(x.shape, lambda i: (0, 0)) for x in full_ins]
    out_specs = [pl.BlockSpec((tm, w), lambda i: (i, 0)) for w, _ in row_outs]
    out_specs += [pl.BlockSpec(s, lambda i: (0, 0)) for s in acc_outs]
    out_shape = [SDS((m, w), dt) for w, dt in row_outs]
    out_shape += [SDS(s, F32) for s in acc_outs]
    return _pc(kern, name, (m // tm,), in_specs, out_specs, out_shape)(*[x for x, _, _ in row_ins], *full_ins)


def _whole(x):
    return (x, x.shape[1], 0)


def _resident(shape):
    return pl.BlockSpec(shape, lambda i: (0,) * len(shape), pipeline_mode=pl.Buffered(1))


def _row_pieces(tm, piece):
    piece = min(piece, tm)
    return [slice(r, r + piece) for r in range(0, tm, piece)]


def _zero_first(refs):
    @pl.when(pl.program_id(0) == 0)
    def _():
        for ref in refs:
            ref[...] = jnp.zeros(ref.shape, F32)


GROUP = 8


def _heads(ref):
    return jnp.stack([ref[g * CHUNK:(g + 1) * CHUNK, h * DH:(h + 1) * DH]
                      for g in range(GROUP) for h in range(DN_H)], axis=0)


def _unheads(ref, val):
    for g in range(GROUP):
        for h in range(DN_H):
            ref[g * CHUNK:(g + 1) * CHUNK, h * DH:(h + 1) * DH] = val[g * DN_H + h]


def _dn_chunks_fwd(q, k, v, gb, bb):
    s_len = q.shape[0]
    ng = s_len // (GROUP * CHUNK)

    def kern(q_ref, k_ref, v_ref, g_ref, b_ref, o_ref, sall_ref, t_ref, state):
        _zero_first([state])
        s = state[...]
        sall_ref[0] = s
        o, s_new, t = _f_chunk(*[_heads(r) for r in (q_ref, k_ref, v_ref, g_ref, b_ref)], s, with_t=True)
        _unheads(o_ref, o)
        t_ref[0] = t
        state[...] = s_new

    blk = pl.BlockSpec((GROUP * CHUNK, DNW), lambda c: (c, 0))
    return _pc(kern, "dn_chunks_fwd", (ng,), [blk] * 5,
               [blk, pl.BlockSpec((1, DN_H, DH, DH), lambda c: (c, 0, 0, 0)),
                pl.BlockSpec((1, GROUP * DN_H, CHUNK, CHUNK), lambda c: (c, 0, 0, 0))],
               [SDS((s_len, DNW), F32), SDS((ng, DN_H, DH, DH), F32), SDS((ng, GROUP * DN_H, CHUNK, CHUNK), F32)],
               scratch=[pltpu.VMEM((DN_H, DH, DH), F32)])(q, k, v, gb, bb)


def _dn_chunks_bwd(q, k, v, gb, bb, s_all, t_all, d_o):
    s_len = q.shape[0]
    ng = s_len // (GROUP * CHUNK)

    def kern(q_ref, k_ref, v_ref, g_ref, b_ref, sall_ref, t_ref, do_ref, dq_ref, dk_ref, dv_ref, dg_ref, db_ref,
             dstate):
        _zero_first([dstate])
        fn = functools.partial(_f_chunk, t_known=t_ref[0])
        _, vjp = jax.vjp(fn, *[_heads(r) for r in (q_ref, k_ref, v_ref, g_ref, b_ref)], sall_ref[0])
        *d_ins, ds = vjp((_heads(do_ref), dstate[...]))
        for ref, val in zip((dq_ref, dk_ref, dv_ref, dg_ref, db_ref), d_ins, strict=True):
            _unheads(ref, val)
        dstate[...] = ds

    blk = pl.BlockSpec((GROUP * CHUNK, DNW), lambda c: (ng - 1 - c, 0))
    return _pc(kern, "dn_chunks_bwd", (ng,),
               [blk] * 5 + [pl.BlockSpec((1, DN_H, DH, DH), lambda c: (ng - 1 - c, 0, 0, 0)),
                            pl.BlockSpec((1, GROUP * DN_H, CHUNK, CHUNK), lambda c: (ng - 1 - c, 0, 0, 0)), blk],
               [blk] * 5, [SDS((s_len, DNW), F32)] * 5,
               scratch=[pltpu.VMEM((DN_H, DH, DH), F32)])(q, k, v, gb, bb, s_all, t_all, d_o)


def _t5_bucket_table():
    qi = np.arange(BLK)[:, None]
    kj = np.arange(2 * BLK)[None, :]
    dist = BLK + qi - kj
    n = np.maximum(dist, 0)
    max_exact = NBUCKET // 2
    nf = np.maximum(n, 1).astype(np.float32)
    large = max_exact + (np.log(nf / np.float32(max_exact)) / np.float32(math.log(MAXDIST / max_exact))
                         * np.float32(NBUCKET - max_exact)).astype(np.int32)
    large = np.minimum(large, NBUCKET - 1)
    return np.where(n < max_exact, n, large)


def _bucket_onehot_t():
    table = _t5_bucket_table().reshape(-1)
    return (np.arange(NBUCKET)[:, None] == table[None, :]).astype(np.float32)


def _swa_mask(first):
    qi = lax.broadcasted_iota(jnp.int32, (BLK, 2 * BLK), 0)
    kj = lax.broadcasted_iota(jnp.int32, (BLK, 2 * BLK), 1)
    dist = BLK + qi - kj
    window = (dist >= 0) & (dist < BLK)
    return window & ((kj >= BLK) | jnp.logical_not(first))


def _bias_expand(rel_bias_t):
    onehot = jnp.asarray(_bucket_onehot_t())

    def kern(r_ref, oh_ref, o_ref):
        o_ref[...] = _raw_dot(r_ref[...], oh_ref[...], "nn", True)

    return pl.pallas_call(
        kern, name="bias_expand", out_shape=SDS((SWA_H, BLK * 2 * BLK), F32), compiler_params=_cparams(),
    )(rel_bias_t, onehot)


def _bias_reduce(d_bias_flat):
    onehot = jnp.asarray(_bucket_onehot_t())

    def kern(d_ref, oh_ref, o_ref):
        o_ref[...] = _raw_dot(d_ref[...], oh_ref[...], "nt", True)

    return pl.pallas_call(
        kern, name="bias_reduce", out_shape=SDS((SWA_H, NBUCKET), F32), compiler_params=_cparams(),
    )(d_bias_flat, onehot)


def _swa_specs(nb, rev):
    def blk(n):
        return (nb - 1 - n) if rev else n

    def before(n):
        return jnp.maximum(blk(n) - 1, 0)

    q_spec = pl.BlockSpec((BLK, SWAW), lambda n: (blk(n), C_SQ // SWAW))
    k_cur = pl.BlockSpec((BLK, SWAKW), lambda n: (blk(n), C_SK // SWAKW))
    k_prev = pl.BlockSpec((BLK, SWAKW), lambda n: (before(n), C_SK // SWAKW))
    v_cur = pl.BlockSpec((BLK, SWAKW), lambda n: (blk(n), C_SV // SWAKW))
    v_prev = pl.BlockSpec((BLK, SWAKW), lambda n: (before(n), C_SV // SWAKW))
    bias = pl.BlockSpec((SWA_H, BLK, 2 * BLK), lambda n: (0, 0, 0))
    gain = pl.BlockSpec((1, SWA_D), lambda n: (0, 0))
    sink = pl.BlockSpec((SWA_KV, 1, SWA_G), lambda n: (0, 0, 0))
    wide = pl.BlockSpec((BLK, SWAW), lambda n: (blk(n), 0))
    narrow = pl.BlockSpec((BLK, SWAKW), lambda n: (blk(n), 0))
    return [q_spec, k_prev, k_cur, v_prev, v_cur, bias, gain, gain, sink], wide, narrow


def _split_heads(x):
    return jnp.stack([x[:, h * SWA_D:(h + 1) * SWA_D] for h in range(x.shape[1] // SWA_D)], axis=0)


def _join_heads(x):
    return jnp.concatenate([x[h] for h in range(x.shape[0])], axis=1)


def _swa_fwd(proj, bias, qg, kg, sinks):
    s_len = proj.shape[0]
    nb = s_len // BLK
    in_specs, wide, _ = _swa_specs(nb, False)

    def kern(q_ref, kp_ref, kc_ref, vp_ref, vc_ref, b_ref, qg_ref, kg_ref, s_ref, o_ref):
        mask = _swa_mask(pl.program_id(0) == 0)
        o8 = _f_swa(*[_split_heads(r[...]) for r in (q_ref, kp_ref, kc_ref, vp_ref, vc_ref)], b_ref[...], qg_ref[...],
                    kg_ref[...], s_ref[...], mask)
        o_ref[...] = _join_heads(o8).astype(BF16)

    return _pc(kern, "swa_fwd", (nb,), in_specs, wide, SDS((s_len, SWAW), BF16))(
        proj, proj, proj, proj, proj, bias, qg, kg, sinks)


def _swa_bwd(proj, bias, qg, kg, sinks, d_out):
    s_len = proj.shape[0]
    nb = s_len // BLK
    in_specs, wide, narrow = _swa_specs(nb, True)

    def kern(q_ref, kp_ref, kc_ref, vp_ref, vc_ref, b_ref, qg_ref, kg_ref, s_ref, do_ref,
             dq_ref, dk_ref, dv_ref, db_ref, dqg_ref, dkg_ref, ds_ref, carry_k, carry_v):
        n = pl.program_id(0)
        mask = _swa_mask(n == nb - 1)
        _zero_first([carry_k, carry_v, db_ref, ds_ref, dqg_ref, dkg_ref])
        fn = functools.partial(_f_swa, mask=mask)
        _, vjp = jax.vjp(fn, *[_split_heads(r[...]) for r in (q_ref, kp_ref, kc_ref, vp_ref, vc_ref)], b_ref[...],
                         qg_ref[...], kg_ref[...], s_ref[...])
        dq, dkp, dkc, dvp, dvc, dbias, dqg, dkg, dsink = vjp(_split_heads(do_ref[...]))
        dq_ref[...] = _join_heads(dq).astype(BF16)
        dk_ref[...] = (_join_heads(dkc) + carry_k[...]).astype(BF16)
        dv_ref[...] = (_join_heads(dvc) + carry_v[...]).astype(BF16)
        carry_k[...] = _join_heads(dkp)
        carry_v[...] = _join_heads(dvp)
        db_ref[...] += dbias
        dqg_ref[...] += dqg
        dkg_ref[...] += dkg
        ds_ref[...] += dsink

    bias_spec, gain, sink = in_specs[5], in_specs[6], in_specs[8]
    return _pc(
        kern, "swa_bwd", (nb,), in_specs + [wide], [wide, narrow, narrow, bias_spec, gain, gain, sink],
        [SDS((s_len, SWAW), BF16), SDS((s_len, SWAKW), BF16), SDS((s_len, SWAKW), BF16),
         SDS((SWA_H, BLK, 2 * BLK), F32), SDS((1, SWA_D), F32), SDS((1, SWA_D), F32), SDS((SWA_KV, 1, SWA_G), F32)],
        scratch=[pltpu.VMEM((BLK, SWAKW), F32), pltpu.VMEM((BLK, SWAKW), F32)],
    )(proj, proj, proj, proj, proj, bias, qg, kg, sinks, d_out)


def _branch_merge(y_dn, y_swa, wa, wb, proj):
    s_len = y_dn.shape[0]
    tm = min(1024, s_len)

    def kern(ya_ref, yb_ref, wa_ref, wb_ref, ga_ref, gb_ref, pa_ref, pb_ref, m_ref):
        for rows in _row_pieces(tm, 128):
            pa = _bdot(ya_ref[rows, :], wa_ref[0])
            pb = _bdot(yb_ref[rows, :], wb_ref[0])
            pa_ref[rows, :] = pa.astype(BF16)
            pb_ref[rows, :] = pb.astype(BF16)
            m_ref[rows, :] = _f_merge(pa, pb, ga_ref[rows, :], gb_ref[rows, :]).astype(BF16)

    y_spec = pl.BlockSpec((tm, DNW), lambda i, s: (i, 0))
    w_spec = pl.BlockSpec((1, DNW, CSH), lambda i, s: (s, 0, 0))
    o_spec = pl.BlockSpec((tm, CSH), lambda i, s: (i, s))
    ga_spec = pl.BlockSpec((tm, CSH), lambda i, s: (i, C_GATE // CSH + s))
    gb_spec = pl.BlockSpec((tm, CSH), lambda i, s: (i, (C_GATE + D) // CSH + s))
    return _pc(kern, "branch_merge", (s_len // tm, N_CHIPS), [y_spec, y_spec, w_spec, w_spec, ga_spec, gb_spec],
               [o_spec] * 3, [SDS((s_len, D), BF16)] * 3,
               )(y_dn, y_swa, wa, wb, proj, proj)


def _in_proj(x, gain, w_in_p):
    s_len = x.shape[0]
    tm = min(512, s_len)

    def kern(x_ref, g_ref, w_ref, h_ref, p_ref):
        h = _f_rms(x_ref[...], g_ref[...]).astype(BF16)
        h_ref[...] = h
        p_ref[...] = _bdot(h, w_ref[...])

    row = pl.BlockSpec((tm, D), lambda i: (i, 0))
    return _pc(kern, "in_proj", (s_len // tm,),
               [row, pl.BlockSpec((1, D), lambda i: (0, 0)), _resident((D, PW))],
               [row, pl.BlockSpec((tm, PW), lambda i: (i, 0))],
               [SDS((s_len, D), BF16), SDS((s_len, PW), F32)])(x, gain, w_in_p)


def _out_proj(merged, w_out, x, gain):
    s_len = x.shape[0]
    tm = min(512, s_len)

    def kern(m_ref, w_ref, x_ref, g_ref, x1_ref, h2_ref):
        x1 = x_ref[...] + _bdot(m_ref[...], w_ref[...])
        x1_ref[...] = x1
        h2_ref[...] = _f_rms(x1, g_ref[...]).astype(BF16)

    row = pl.BlockSpec((tm, D), lambda i: (i, 0))
    return _pc(kern, "out_proj", (s_len // tm,),
               [row, _resident((D, D)), row, pl.BlockSpec((1, D), lambda i: (0, 0))],
               [row, row], [SDS((s_len, D), F32), SDS((s_len, D), BF16)])(merged, w_out, x, gain)


def _ffn_up(h2, wg, wu):
    s_len = h2.shape[0]
    tm = min(2048, s_len)

    def kern(h_ref, g_ref, u_ref, gt_ref, up_ref, act_ref):
        for rows in _row_pieces(tm, 256):
            h = h_ref[rows, :]
            g = _bdot(h, g_ref[0], "nt")
            u = _bdot(h, u_ref[0], "nt")
            gt_ref[0, rows, :] = g.astype(BF16)
            up_ref[0, rows, :] = u.astype(BF16)
            act_ref[0, rows, :] = _f_swiglu(g, u).astype(BF16)

    w_spec = pl.BlockSpec((1, FSH, D), lambda s, i: (s, 0, 0))
    o_spec = pl.BlockSpec((1, tm, FSH), lambda s, i: (s, i, 0))
    shape = (N_CHIPS, s_len, FSH)
    return _pc(kern, "ffn_up", (N_CHIPS, s_len // tm), [pl.BlockSpec((tm, D), lambda s, i: (i, 0)), w_spec, w_spec],
               [o_spec] * 3, [SDS(shape, BF16)] * 3)(h2, wg, wu)


def _ffn_down_loss(act, wd, x1, target):
    s_len = x1.shape[0]
    tm = min(512, s_len)

    def kern(a_ref, w_ref, x_ref, t_ref, dy_ref, dyb_ref, loss_ref):
        _zero_first([loss_ref])
        for rows in _row_pieces(tm, 128):
            y = x_ref[rows, :]
            for s in range(N_CHIPS):
                y = y + _bdot(a_ref[s, rows, :], w_ref[s])
            d = y - t_ref[rows, :]
            dy = d * (1.0 / D)
            dy_ref[rows, :] = dy
            dyb_ref[rows, :] = dy.astype(BF16)
            loss_ref[...] += jnp.sum(d * d).reshape(1, 1) * (0.5 / D)

    row = pl.BlockSpec((tm, D), lambda i: (i, 0))
    return _pc(kern, "ffn_down_loss", (s_len // tm,),
               [pl.BlockSpec((N_CHIPS, tm, FSH), lambda i: (0, i, 0)),
                _resident((N_CHIPS, FSH, D)), row, row],
               [row, row, pl.BlockSpec((1, 1), lambda i: (0, 0))],
               [SDS((s_len, D), F32), SDS((s_len, D), BF16), SDS((1, 1), F32)])(act, wd, x1, target)


def _ffn_dact(dy_b, wd, gt, up):
    s_len = dy_b.shape[0]
    tm = min(2048, s_len)

    def kern(dy_ref, w_ref, gt_ref, up_ref, dg_ref, du_ref):
        w = w_ref[0]
        for rows in _row_pieces(tm, 256):
            d_act = _bdot(dy_ref[rows, :], w, "nt")
            _, vjp = jax.vjp(_f_swiglu, gt_ref[0, rows, :].astype(F32), up_ref[0, rows, :].astype(F32))
            dg, du = vjp(d_act)
            dg_ref[0, rows, :] = dg.astype(BF16)
            du_ref[0, rows, :] = du.astype(BF16)

    a_spec = pl.BlockSpec((1, tm, FSH), lambda s, i: (s, i, 0))
    shape = (N_CHIPS, s_len, FSH)
    return _pc(kern, "ffn_dact", (N_CHIPS, s_len // tm),
               [pl.BlockSpec((tm, D), lambda s, i: (i, 0)), pl.BlockSpec((1, FSH, D), lambda s, i: (s, 0, 0)),
                a_spec, a_spec],
               [a_spec, a_spec], [SDS(shape, BF16), SDS(shape, BF16)])(dy_b, wd, gt, up)


def _gw_ffn(lhs, rhs, name):
    s_len = rhs.shape[0]
    n = len(lhs)
    tn = 512

    def kern(*refs):
        g = refs[n][...]
        for i in range(n):
            refs[n + 1 + i][0] = _bdot(refs[i][0], g, "tn").astype(BF16)

    a_spec = pl.BlockSpec((1, s_len, FSH), lambda s, j: (s, 0, 0))
    o_spec = pl.BlockSpec((1, FSH, tn), lambda s, j: (s, 0, j))
    return _pc(kern, name, (N_CHIPS, D // tn), [a_spec] * n + [pl.BlockSpec((s_len, tn), lambda s, j: (0, j))],
               [o_spec] * n, [SDS((N_CHIPS, FSH, D), BF16)] * n)(*lhs, rhs)


def _ffn_dh2(d_gt, d_up, wg, wu, x1, dy, gain):
    s_len = x1.shape[0]
    tm = min(512, s_len)

    def kern(dg_ref, du_ref, wg_ref, wu_ref, x_ref, dy_ref, g_ref, dx_ref, dxb_ref, dgain_ref):
        _zero_first([dgain_ref])
        dh2 = jnp.zeros((tm, D), F32)
        for s in range(N_CHIPS):
            dh2 = dh2 + _bdot(dg_ref[s], wg_ref[s]) + _bdot(du_ref[s], wu_ref[s])
        _, vjp = jax.vjp(_f_rms, x_ref[...], g_ref[...])
        dx, dgain = vjp(dh2)
        dx1 = dx + dy_ref[...]
        dx_ref[...] = dx1
        dxb_ref[...] = dx1.astype(BF16)
        dgain_ref[...] += dgain

    row = pl.BlockSpec((tm, D), lambda i: (i, 0))
    d_spec = pl.BlockSpec((N_CHIPS, tm, FSH), lambda i: (0, i, 0))
    w_spec = _resident((N_CHIPS, FSH, D))
    vec = pl.BlockSpec((1, D), lambda i: (0, 0))
    return _pc(kern, "ffn_dh2", (s_len // tm,), [d_spec, d_spec, w_spec, w_spec, row, row, vec],
               [row, row, vec], [SDS((s_len, D), F32), SDS((s_len, D), BF16), SDS((1, D), F32)],
               )(d_gt, d_up, wg, wu, x1, dy, gain)


def _merge_bwd(dx1_b, w_out, pa, pb, proj):
    s_len = dx1_b.shape[0]
    tm = min(512, s_len)

    def kern(dx_ref, w_ref, pa_ref, pb_ref, g_ref, dpa_ref, dpb_ref, dg_ref):
        dm = _bdot(dx_ref[...], w_ref[...], "nt")
        gates = g_ref[...]
        _, vjp = jax.vjp(_f_merge, pa_ref[...].astype(F32), pb_ref[...].astype(F32), gates[:, :D], gates[:, D:])
        dpa, dpb, dga, dgb = vjp(dm)
        dpa_ref[...] = dpa.astype(BF16)
        dpb_ref[...] = dpb.astype(BF16)
        dg_ref[:, :D] = dga.astype(BF16)
        dg_ref[:, D:] = dgb.astype(BF16)

    row = pl.BlockSpec((tm, D), lambda i: (i, 0))
    return _pc(kern, "merge_bwd", (s_len // tm,),
               [row, _resident((D, D)), row, row,
                pl.BlockSpec((tm, 2 * D), lambda i: (i, C_GATE // (2 * D)))],
               [row, row, pl.BlockSpec((tm, 2 * D), lambda i: (i, 0))],
               [SDS((s_len, D), BF16), SDS((s_len, D), BF16), SDS((s_len, 2 * D), BF16)],
               )(dx1_b, w_out, pa, pb, proj)


def _d_branch(d_pa, d_pb, wa, wb):
    s_len = d_pa.shape[0]
    tm = min(512, s_len)

    def kern(da_ref, db_ref, wa_ref, wb_ref, oa_ref, ob_ref):
        acc_a = jnp.zeros((tm, DNW), F32)
        acc_b = jnp.zeros((tm, SWAW), F32)
        for s in range(N_CHIPS):
            acc_a = acc_a + _bdot(da_ref[:, s * CSH:(s + 1) * CSH], wa_ref[s], "nt")
            acc_b = acc_b + _bdot(db_ref[:, s * CSH:(s + 1) * CSH], wb_ref[s], "nt")
        oa_ref[...] = acc_a
        ob_ref[...] = acc_b

    row = pl.BlockSpec((tm, D), lambda i: (i, 0))
    w_spec = pl.BlockSpec((N_CHIPS, DNW, CSH), lambda i: (0, 0, 0))
    out = pl.BlockSpec((tm, DNW), lambda i: (i, 0))
    return _pc(kern, "d_branch", (s_len // tm,), [row, row, w_spec, w_spec], [out, out],
               [SDS((s_len, DNW), F32), SDS((s_len, SWAW), F32)])(d_pa, d_pb, wa, wb)


def _gw_branch(y_dn, y_swa, d_pa, d_pb):
    s_len = y_dn.shape[0]

    def kern(ya_ref, yb_ref, da_ref, db_ref, oa_ref, ob_ref):
        oa_ref[0] = _bdot(ya_ref[...], da_ref[...], "tn").astype(BF16)
        ob_ref[0] = _bdot(yb_ref[...], db_ref[...], "tn").astype(BF16)

    y_spec = pl.BlockSpec((s_len, DNW), lambda s: (0, 0))
    d_spec = pl.BlockSpec((s_len, CSH), lambda s: (0, s))
    o_spec = pl.BlockSpec((1, DNW, CSH), lambda s: (s, 0, 0))
    shape = (N_CHIPS, DNW, CSH)
    return _pc(kern, "gw_branch", (N_CHIPS,), [y_spec, y_spec, d_spec, d_spec], [o_spec, o_spec],
               [SDS(shape, BF16), SDS(shape, BF16)])(y_dn, y_swa, d_pa, d_pb)


def _dh_rms(d_proj, w_in_p, x, dx1, gain):
    s_len = x.shape[0]
    tm = min(512, s_len)

    def kern(dp_ref, w_ref, x_ref, r_ref, g_ref, gx_ref, dgain_ref):
        _zero_first([dgain_ref])
        dh = _bdot(dp_ref[...], w_ref[...], "nt")
        _, vjp = jax.vjp(_f_rms, x_ref[...], g_ref[...])
        dx, dgain = vjp(dh)
        gx_ref[...] = dx + r_ref[...]
        dgain_ref[...] += dgain

    row = pl.BlockSpec((tm, D), lambda i: (i, 0))
    vec = pl.BlockSpec((1, D), lambda i: (0, 0))
    return _pc(kern, "dh_rms", (s_len // tm,),
               [pl.BlockSpec((tm, PW), lambda i: (i, 0)), _resident((D, PW)), row, row, vec],
               [row, vec], [SDS((s_len, D), F32), SDS((1, D), F32)])(d_proj, w_in_p, x, dx1, gain)


HALO = 8


def _rows_down(x, n, above):
    tm = x.shape[0]
    r = pltpu.roll(x, n, 0)
    a = pltpu.roll(above, n, 0)
    top = jnp.where(lax.broadcasted_iota(jnp.int32, above.shape, 0) < n, a, r[0:HALO])
    return jnp.concatenate([top, r[HALO:tm]], axis=0)


def _rows_up(x, n, below):
    tm = x.shape[0]
    r = pltpu.roll(x, tm - n, 0)
    b = pltpu.roll(below, HALO - n, 0)
    bottom = jnp.where(lax.broadcasted_iota(jnp.int32, below.shape, 0) >= HALO - n, b, r[tm - HALO:tm])
    return jnp.concatenate([r[0:tm - HALO], bottom], axis=0)


def _conv_taps(cur_ref, prev_ref, first):
    cur = cur_ref[...]
    above = jnp.where(first, 0.0, prev_ref[...])
    return [_rows_down(cur, n, above) for n in range(CONV - 1, 0, -1)] + [cur]


def _dn_pre_specs(s_len, tm, blk):
    cur = pl.BlockSpec((tm, QKVW), lambda i: (blk(i), 0))
    prev = pl.BlockSpec((HALO, QKVW), lambda i: (jnp.maximum(blk(i) * (tm // HALO) - 1, 0), 0))
    ba = pl.BlockSpec((tm, 128), lambda i: (blk(i), C_BA // 128))
    row = pl.BlockSpec((tm, DNW), lambda i: (blk(i), 0))
    full = [pl.BlockSpec((CONV, QKVW), lambda i: (0, 0)), pl.BlockSpec((1, DN_H), lambda i: (0, 0)),
            pl.BlockSpec((1, DN_H), lambda i: (0, 0))]
    return cur, prev, ba, row, full


def _dn_pre_fwd(proj, conv_w, alog, dtb):
    s_len = proj.shape[0]
    tm = min(256, s_len)
    cur, prev, ba, row, full = _dn_pre_specs(s_len, tm, lambda i: i)

    def kern(cur_ref, prev_ref, ba_ref, cw_ref, al_ref, dt_ref, q_ref, k_ref, v_ref, bb_ref, gb_ref):
        xs = _conv_taps(cur_ref, prev_ref, pl.program_id(0) == 0)
        outs = _f_dn_pre(*xs, ba_ref[...], cw_ref[...], al_ref[...], dt_ref[...])
        for ref, val in zip((q_ref, k_ref, v_ref, bb_ref, gb_ref), outs, strict=True):
            ref[...] = val

    return _pc(kern, "dn_pre_fwd", (s_len // tm,), [cur, prev, ba] + full, [row] * 5,
               [SDS((s_len, DNW), F32)] * 5)(proj, proj, proj, conv_w, alog, dtb)


def _dn_pre_bwd(proj, conv_w, alog, dtb, cots, others):
    s_len = proj.shape[0]
    tm = min(256, s_len)
    nb = s_len // tm
    cur, prev, ba, row, full = _dn_pre_specs(s_len, tm, lambda i: nb - 1 - i)
    n_o = len(others)
    assert QKVW + sum(t.shape[1] for t in others) + 128 == C_BA + 128

    def kern(cur_ref, prev_ref, ba_ref, cw_ref, al_ref, dt_ref, dq_ref, dk_ref, dv_ref, dbb_ref, dgb_ref, *rest):
        o_refs = rest[:n_o]
        dproj_ref, dcw_ref, dal_ref, ddt_ref, *tails = rest[n_o:]
        i = pl.program_id(0)
        _zero_first([dcw_ref, dal_ref, ddt_ref] + tails)
        xs = _conv_taps(cur_ref, prev_ref, i == nb - 1)
        _, vjp = jax.vjp(_f_dn_pre, *xs, ba_ref[...], cw_ref[...], al_ref[...], dt_ref[...])
        *dxs, dba, dcw, dal, ddt = vjp((dq_ref[...], dk_ref[...], dv_ref[...], dbb_ref[...], dgb_ref[...]))
        total = dxs[CONV - 1]
        for j, t in enumerate(tails):
            n = CONV - 1 - j
            total = total + _rows_up(dxs[j], n, t[...])
            t[...] = dxs[j][0:HALO, :]
        dproj_ref[...] = jnp.concatenate(
            [total.astype(BF16)] + [r[...] for r in o_refs] + [dba.astype(BF16), jnp.zeros((tm, PW - C_BA - 128), BF16)],
            axis=1)
        dcw_ref[...] += dcw
        dal_ref[...] += dal
        ddt_ref[...] += ddt

    o_specs = [pl.BlockSpec((tm, t.shape[1]), lambda i: (nb - 1 - i, 0)) for t in others]
    return _pc(kern, "dn_pre_bwd", (nb,), [cur, prev, ba] + full + [row] * 5 + o_specs,
               [pl.BlockSpec((tm, PW), lambda i: (nb - 1 - i, 0))] + full,
               [SDS((s_len, PW), BF16), SDS((CONV, QKVW), F32), SDS((1, DN_H), F32), SDS((1, DN_H), F32)],
               scratch=[pltpu.VMEM((HALO, QKVW), F32)] * (CONV - 1))(proj, proj, proj, conv_w, alog, dtb, *cots, *others)


def _w_in_to_padded(w_sh):
    tr = 256

    def kern(w_ref, o_ref):
        full = jnp.concatenate([w_ref[s] for s in range(N_CHIPS)], axis=1)
        pieces = [full[:, o0:o0 + w] for o0, w, _ in sorted(_ORIG_PIECES, key=lambda t: t[2])]
        o_ref[...] = jnp.concatenate(pieces + [jnp.zeros((tr, PW - D_IN), w_ref.dtype)], axis=1)

    return _pc(kern, "w_in_to_padded", (D // tr,), [pl.BlockSpec((N_CHIPS, tr, D_IN // N_CHIPS), lambda i: (0, i, 0))],
               pl.BlockSpec((tr, PW), lambda i: (i, 0)), SDS((D, PW), w_sh.dtype))(w_sh)


def _padded_to_w_in(g):
    tr = 256
    csh = D_IN // N_CHIPS

    def kern(g_ref, o_ref):
        x = g_ref[...]
        full = jnp.concatenate([x[:, p0:p0 + w] for _, w, p0 in _ORIG_PIECES], axis=1)
        for s in range(N_CHIPS):
            o_ref[s] = full[:, s * csh:(s + 1) * csh]

    return _pc(kern, "padded_to_w_in", (D // tr,), [pl.BlockSpec((tr, PW), lambda i: (i, 0))],
               pl.BlockSpec((N_CHIPS, tr, csh), lambda i: (0, i, 0)), SDS((N_CHIPS, D, csh), g.dtype))(g)


def _local_step(x, target, wts):
    s_len = x.shape[0]
    tm = min(512, s_len)
    w_in_p = wts["w_in_p"]
    attn_gain = wts["attn_norm"]
    ffn_gain = wts["ffn_norm"]
    conv_w = wts["dn_conv"]
    alog, dtb, out_gain = wts["dn_a_log"], wts["dn_dt_bias"], wts["dn_out_norm"]
    qg, kg = wts["swa_q_norm"], wts["swa_k_norm"]
    sinks = wts["swa_sinks"].reshape(SWA_KV, 1, SWA_G)

    h, proj = _in_proj(x, attn_gain, w_in_p)
    q_dn, k_dn, v_dn, bb, gb = _dn_pre_fwd(proj, conv_w, alog, dtb)
    o_dn, s_all, t_all = _dn_chunks_fwd(q_dn, k_dn, v_dn, gb, bb)
    post_ins = [_whole(o_dn), (proj, DNW, C_Z // DNW)]
    (y_dn,) = _rows(lambda r, f: ([_f_dn_post(r[0], r[1], f[0])], []), "dn_post_fwd", s_len, tm, post_ins,
                    [out_gain], [(DNW, BF16)])

    bias = _bias_expand(wts["rel_bias"].T).reshape(SWA_H, BLK, 2 * BLK)
    y_swa = _swa_fwd(proj, bias, qg, kg, sinks)

    wts = {**wts, **wts["late"](y_swa)}
    p_a, p_b, merged = _branch_merge(y_dn, y_swa, wts["wa"], wts["wb"], proj)
    x1, h2 = _out_proj(merged, wts["w_out"], x, ffn_gain)
    gt, up, act = _ffn_up(h2, wts["wg"], wts["wu"])
    dy, dy_b, loss = _ffn_down_loss(act, wts["wd"], x1, target)

    grads = {}
    d_gt, d_up = _ffn_dact(dy_b, wts["wd"], gt, up)
    (grads["w_down"],) = _gw_ffn([act], dy_b, "gw_down")
    grads["w_gate"], grads["w_up"] = _gw_ffn([d_gt, d_up], h2, "gw_gate_up")
    token = wts["send_ffn"](grads)
    dx1, dx1_b, grads["ffn_norm"] = _ffn_dh2(d_gt, d_up, wts["wg"], wts["wu"], x1, dy,
                                             ffn_gain + token[0:1, 0:1])
    grads["w_out"] = _mm(merged, dx1_b, "tn", BF16, 512, 512, "gw_out")
    d_pa, d_pb, d_gr = _merge_bwd(dx1_b, wts["w_out"], p_a, p_b, proj)
    d_ydn, d_yswa = _d_branch(d_pa, d_pb, wts["wa"], wts["wb"])
    grads["w_branch_dn"], grads["w_branch_swa"] = _gw_branch(y_dn, y_swa, d_pa, d_pb)
    token = wts["send_early"](grads)
    qg_t = qg + token[0:1, 0:1]
    out_gain_t = out_gain + token[0:1, 0:1]

    d_sq, d_sk, d_sv, d_bias, grads["swa_q_norm"], grads["swa_k_norm"], d_sinks = _swa_bwd(
        proj, bias, qg_t, kg, sinks, d_yswa)
    grads["swa_sinks"] = d_sinks.reshape(1, SWA_H)
    grads["rel_bias"] = _bias_reduce(d_bias.reshape(SWA_H, BLK * 2 * BLK)).T

    def post_bwd(r, f):
        _, vjp = jax.vjp(_f_dn_post, r[0], r[1], f[0])
        d_o, d_z, d_gain = vjp(r[2])
        return [d_o, d_z], [d_gain]

    d_o, d_z, grads["dn_out_norm"] = _rows(post_bwd, "dn_post_bwd", s_len, tm, post_ins + [_whole(d_ydn)], [out_gain_t],
                                           [(DNW, F32), (DNW, BF16)], [(1, DH)])
    d_q, d_k, d_v, d_gb, d_bb = _dn_chunks_bwd(q_dn, k_dn, v_dn, gb, bb, s_all, t_all, d_o)

    d_proj, grads["dn_conv"], grads["dn_a_log"], grads["dn_dt_bias"] = _dn_pre_bwd(
        proj, conv_w, alog, dtb, (d_q, d_k, d_v, d_bb, d_gb), (d_z, d_gr, d_sq, d_sk, d_sv))
    grads["w_in_p"] = _mm(h, d_proj, "tn", BF16, 512, 1024, "gw_in")
    token = wts["send_in"](grads["w_in_p"])
    grad_x, grads["attn_norm"] = _dh_rms(d_proj, w_in_p, x, dx1, attn_gain + token[0:1, 0:1])
    return loss, grad_x, grads


_HBM = pl.BlockSpec(memory_space=pl.ANY)


def _place():
    return lax.axis_index("x"), lax.axis_index("y"), lax.axis_index("c")


def _other_chips(x, y):
    return [(1 - x, y), (x, 1 - y), (1 - x, 1 - y)]


def _rcopy(src, dst, send_sems, recv_sems, k, to):
    return pltpu.make_async_remote_copy(src_ref=src, dst_ref=dst, send_sem=send_sems.at[k], recv_sem=recv_sems.at[k],
                                        device_id=to, device_id_type=MESH)


def _comm_call(body, name, ins, out_shapes, n_remote, landing=0):
    first = len(ins) - landing
    return pl.pallas_call(
        body, name=name, in_specs=[_HBM] * len(ins), out_specs=[_HBM] * len(out_shapes), out_shape=out_shapes,
        scratch_shapes=[pltpu.SemaphoreType.DMA((n_remote,)), pltpu.SemaphoreType.DMA((n_remote,))],
        input_output_aliases={first + i: i for i in range(landing)},
        compiler_params=_cparams(has_side_effects=True),
    )(*ins)


def _own_slot(blocks, chip):
    return [lax.dynamic_update_slice(lax.empty((N_CHIPS,) + b.shape, b.dtype), b[None], (chip, 0, 0)) for b in blocks]


def _gather_weights(ws, chip):
    n = len(ws)
    halves = [w.shape[0] // 2 for w in ws]

    def body(*refs):
        w_refs, o_refs = refs[:n], refs[2 * n:3 * n]
        send_sems, recv_sems = refs[3 * n:]
        x, y, c = _place()
        s = 2 * x + y
        sib = (x, y, 1 - c)
        chips = _other_chips(x, y)

        def rows(i, half):
            return pl.ds(half * halves[i], halves[i])

        first = []
        for j, (cx, cy) in enumerate(chips):
            for i in range(n):
                cp = _rcopy(w_refs[i].at[rows(i, c), :], o_refs[i].at[s, rows(i, c), :], send_sems, recv_sems,
                            j * n + i, (cx, cy, c))
                cp.start()
                first.append(cp)
        passed = []
        for j, (cx, cy) in enumerate(chips):
            sj = 2 * cx + cy
            for i in range(n):
                blk = o_refs[i].at[sj, rows(i, c), :]
                _rcopy(blk, blk, send_sems, recv_sems, j * n + i, (cx, cy, c)).wait_recv()
                cp = _rcopy(blk, blk, send_sems, recv_sems, (3 + j) * n + i, sib)
                cp.start()
                passed.append(cp)
        for j, (cx, cy) in enumerate(chips):
            sj = 2 * cx + cy
            for i in range(n):
                blk = o_refs[i].at[sj, rows(i, 1 - c), :]
                _rcopy(blk, blk, send_sems, recv_sems, (3 + j) * n + i, sib).wait_recv()
        for cp in first + passed:
            cp.wait_send()

    return _comm_call(body, "gather_weights", list(ws) + _own_slot(ws, chip),
                      [SDS((N_CHIPS,) + w.shape, w.dtype) for w in ws], 6 * n, landing=n)


_HBM_ONLY = pl.BlockSpec(memory_space=pltpu.HBM)
_SEM = pl.BlockSpec(memory_space=pltpu.SEMAPHORE)
_DATAFLOW = pltpu.SideEffectType.DATAFLOW_SIDE_EFFECTING


def _in_hbm(a):
    return pltpu.with_memory_space_constraint(a, pltpu.HBM)


def _gather_windows(blocks):
    halves = [b.shape[0] // 2 for b in blocks]

    def src_at(ref, i, c, sj):
        return ref.at[pl.ds(c * halves[i], halves[i]), :]

    def dst_at(ref, i, c, s_from):
        return ref.at[s_from, pl.ds(c * halves[i], halves[i]), :]

    return src_at, dst_at


def _exchange_windows():
    return (lambda ref, i, c, sj: ref.at[sj]), (lambda ref, i, c, s_from: ref.at[s_from])


def _swap_windows(gs):
    halves = [g.shape[1] // 2 for g in gs]
    return ((lambda ref, i, c, tag: ref.at[:, pl.ds((1 - c) * halves[i], halves[i]), :]),
            (lambda ref, i, c, slot: ref))


def _chip_peers(x, y, c):
    return [(2 * cx + cy, (cx, cy, c), 2 * x + y, 2 * cx + cy) for cx, cy in _other_chips(x, y)]


def _sibling_peer(x, y, c):
    return [(0, (x, y, 1 - c), 0, 0)]


def _split_start(name, ws, lands, dep, windows, peers=_chip_peers, n_peers=3):
    n = len(ws)
    src_at, dst_at = windows

    def body(*refs):
        w_refs, l_refs = refs[:n], refs[n:2 * n]
        send_sems, recv_sems = refs[2 * n + 1], refs[2 * n + 2]
        token = refs[-1]
        x, y, c = _place()
        for j, (tag, dev, there, _) in enumerate(peers(x, y, c)):
            for i in range(n):
                _rcopy(src_at(w_refs[i], i, c, tag), dst_at(l_refs[i], i, c, there), send_sems, recv_sems,
                       j * n + i, dev).start()
        token[...] = jnp.zeros_like(token)

    outs = pl.pallas_call(
        body, name=name,
        out_shape=(pltpu.SemaphoreType.DMA((n_peers * n,)), pltpu.SemaphoreType.DMA((n_peers * n,)),
                   *[pltpu.HBM(w.shape, w.dtype) for w in ws], *[pltpu.HBM(t.shape, t.dtype) for t in lands],
                   SDS((8, 128), F32)),
        in_specs=[_HBM_ONLY] * (2 * n) + [pl.BlockSpec(memory_space=pl.ANY)],
        out_specs=(_SEM, _SEM, *[_HBM_ONLY] * (2 * n), pl.BlockSpec(memory_space=pltpu.VMEM)),
        input_output_aliases={i: 2 + i for i in range(2 * n)},
        compiler_params=pltpu.CompilerParams(has_side_effects=_DATAFLOW),
    )(*[_in_hbm(w) for w in ws], *[_in_hbm(t) for t in lands], dep)
    return outs[0], outs[1], outs[2:2 + n], outs[2 + n:2 + 2 * n], outs[-1]


def _split_wait(name, w_thru, l_thru, send_sems, recv_sems, after, windows, peers=_chip_peers, with_sources=False):
    n = len(w_thru)
    src_at, dst_at = windows

    def body(*refs):
        w_refs, l_refs = refs[:n], refs[n:2 * n]
        send_sems, recv_sems = refs[2 * n], refs[2 * n + 1]
        x, y, c = _place()
        for j, (tag, dev, _, here) in enumerate(peers(x, y, c)):
            for i in range(n):
                cp = _rcopy(src_at(w_refs[i], i, c, tag), dst_at(l_refs[i], i, c, here), send_sems, recv_sems,
                            j * n + i, dev)
                cp.wait_send()
                cp.wait_recv()

    outs = pl.pallas_call(
        body, name=name,
        out_shape=[pltpu.HBM(w.shape, w.dtype) for w in w_thru] + [pltpu.HBM(t.shape, t.dtype) for t in l_thru],
        in_specs=[_HBM_ONLY] * (2 * n) + [_SEM, _SEM, pl.BlockSpec(memory_space=pl.ANY)],
        out_specs=[_HBM_ONLY] * (2 * n),
        input_output_aliases={i: i for i in range(2 * n)},
        compiler_params=pltpu.CompilerParams(has_side_effects=_DATAFLOW),
    )(*w_thru, *l_thru, send_sems, recv_sems, after)
    return (outs[:n], outs[n:]) if with_sources else outs[n:]


def _sibling_fill(lands):
    n = len(lands)
    halves = [t.shape[1] // 2 for t in lands]

    def body(*refs):
        o_refs = refs[n:2 * n]
        send_sems, recv_sems = refs[2 * n:]
        x, y, c = _place()
        sib = (x, y, 1 - c)
        chips = _other_chips(x, y)
        sent = []
        for j, (cx, cy) in enumerate(chips):
            for i in range(n):
                blk = o_refs[i].at[2 * cx + cy, pl.ds(c * halves[i], halves[i]), :]
                cp = _rcopy(blk, blk, send_sems, recv_sems, j * n + i, sib)
                cp.start()
                sent.append(cp)
        for j, (cx, cy) in enumerate(chips):
            for i in range(n):
                blk = o_refs[i].at[2 * cx + cy, pl.ds((1 - c) * halves[i], halves[i]), :]
                _rcopy(blk, blk, send_sems, recv_sems, j * n + i, sib).wait_recv()
        for cp in sent:
            cp.wait_send()

    return _comm_call(body, "sibling_fill", list(lands), [SDS(t.shape, t.dtype) for t in lands], 3 * n, landing=n)


def _swap_halves(gs, name):
    n = len(gs)
    halves = [g.shape[1] // 2 for g in gs]

    def body(*refs):
        g_refs, o_refs = refs[:n], refs[n:2 * n]
        send_sems, recv_sems = refs[2 * n:]
        x, y, c = _place()
        cps = [_rcopy(g_refs[i].at[:, pl.ds((1 - c) * halves[i], halves[i]), :], o_refs[i], send_sems, recv_sems, i,
                      (x, y, 1 - c)) for i in range(n)]
        for cp in cps:
            cp.start()
        for cp in cps:
            cp.wait()

    return _comm_call(body, name, gs, [SDS((N_CHIPS, h, g.shape[2]), g.dtype) for g, h in zip(gs, halves)], n)


def _swap_reduced(rs, name):
    n = len(rs)

    def body(*refs):
        r_refs, o_refs = refs[:n], refs[n:2 * n]
        send_sems, recv_sems = refs[2 * n:]
        x, y, c = _place()
        cps = [_rcopy(r_refs[i], o_refs[i], send_sems, recv_sems, i, (x, y, 1 - c)) for i in range(n)]
        for cp in cps:
            cp.start()
        for cp in cps:
            cp.wait()

    return _comm_call(body, name, rs, [SDS(r.shape, r.dtype) for r in rs], n)


def _all_sum_small(vec, name):
    n_dev = 8
    flips = [(bx, by, bc) for bx in (0, 1) for by in (0, 1) for bc in (0, 1)][1:]

    def body(v_ref, out_ref, gath, send_sems, recv_sems):
        x, y, c = _place()
        me = 4 * x + 2 * y + c
        gath[me] = v_ref[...]
        sent = []
        for k, (bx, by, bc) in enumerate(flips):
            peer = (x ^ bx, y ^ by, c ^ bc)
            cp = _rcopy(v_ref, gath.at[me], send_sems, recv_sems, k, peer)
            cp.start()
            sent.append(cp)
        for k, (bx, by, bc) in enumerate(flips):
            peer = (x ^ bx, y ^ by, c ^ bc)
            _rcopy(v_ref, gath.at[4 * peer[0] + 2 * peer[1] + peer[2]], send_sems, recv_sems, k, peer).wait_recv()
        for cp in sent:
            cp.wait_send()
        acc = gath[0]
        for d in range(1, n_dev):
            acc = acc + gath[d]
        out_ref[...] = acc

    vm = pl.BlockSpec(memory_space=pltpu.VMEM)
    return pl.pallas_call(
        body, name=name, in_specs=[vm], out_specs=vm, out_shape=SDS(vec.shape, F32),
        scratch_shapes=[pltpu.VMEM((n_dev,) + vec.shape, F32), pltpu.SemaphoreType.DMA((7,)),
                        pltpu.SemaphoreType.DMA((7,))],
        compiler_params=_cparams(has_side_effects=True),
    )(vec)


def _pack_small(vals, extra=None):
    parts = [vals[n].reshape(-1).astype(F32) for n, _ in _SMALL]
    parts.append(jnp.zeros((1,), F32) if extra is None else extra.reshape(1).astype(F32))
    flat = jnp.concatenate(parts)
    flat = jnp.concatenate([flat, jnp.zeros((_SMALL_ROWS * 128 - flat.shape[0],), F32)])
    return flat.reshape(_SMALL_ROWS, 128)


def _unpack_small(packed, shapes):
    flat = packed.reshape(-1)
    return {n: flat[_SMALL_OFF[n][0]:_SMALL_OFF[n][0] + _SMALL_OFF[n][1]].reshape(shapes[n]) for n, _ in _SMALL}


def _pair_sum(gs, gots, core, name):
    n = len(gs)

    def kern(c_ref, *refs):
        for i in range(n):
            refs[2 * n + i][...] = (refs[i][...].astype(F32) + refs[n + i][...].astype(F32)).astype(BF16)

    in_specs = [pl.BlockSpec((1, t.shape[1], t.shape[2]), lambda s, c_ref: (s, c_ref[0], 0)) for t in gots]
    in_specs += [pl.BlockSpec((1, t.shape[1], t.shape[2]), lambda s, c_ref: (s, 0, 0)) for t in gots]
    out_specs = [pl.BlockSpec((1, t.shape[1], t.shape[2]), lambda s, c_ref: (s, 0, 0)) for t in gots]
    return pl.pallas_call(
        kern, name=name,
        grid_spec=pltpu.PrefetchScalarGridSpec(num_scalar_prefetch=1, grid=(N_CHIPS,), in_specs=in_specs,
                                               out_specs=out_specs),
        out_shape=[SDS(t.shape, BF16) for t in gots],
        compiler_params=_cparams(dimension_semantics=("arbitrary",)),
    )(core.reshape(1).astype(jnp.int32), *gs, *gots)


def _chip_sum(qs, name):
    n = len(qs)

    def kern(*refs):
        for i in range(n):
            acc = refs[i][0].astype(F32)
            for s in range(1, N_CHIPS):
                acc = acc + refs[i][s].astype(F32)
            refs[n + i][...] = acc

    in_specs = [pl.BlockSpec((N_CHIPS, q.shape[1] // 2, q.shape[2]), lambda j: (0, j, 0)) for q in qs]
    out_specs = [pl.BlockSpec((q.shape[1] // 2, q.shape[2]), lambda j: (j, 0)) for q in qs]
    return _pc(kern, name, (2,), in_specs, out_specs, [SDS(q.shape[1:], F32) for q in qs])(*qs)


def _adam_math(w_, g_, m_, v_):
    m_ = ADAM_B1 * m_ + (1.0 - ADAM_B1) * g_
    v_ = ADAM_B2 * v_ + (1.0 - ADAM_B2) * jnp.square(g_)
    m_hat = m_ / (1.0 - ADAM_B1 ** ADAM_STEP)
    v_hat = v_ / (1.0 - ADAM_B2 ** ADAM_STEP)
    return -ADAM_LR * (m_hat / (jnp.sqrt(v_hat) + ADAM_EPS) + ADAM_WD * w_), m_, v_


def _adamw(w, g, m, v, name):
    rows, cols = w.shape
    tr = rows
    for cand in (256, 128, 64, 32, 16, 8):
        if rows % cand == 0 and rows > cand:
            tr = cand
            break

    def kern(w_ref, g_ref, m_ref, v_ref, d_ref, nm_ref, nv_ref):
        d_ref[...], nm_ref[...], nv_ref[...] = _adam_math(w_ref[...], g_ref[...], m_ref[...], v_ref[...])

    spec = pl.BlockSpec((tr, cols), lambda i: (i, 0))
    return _pc(kern, name, (rows // tr,), [spec] * 4, [spec] * 3, [SDS(w.shape, F32)] * 3)(w, g, m, v)


def _adamw_rows1(w, g, m, v, name):
    rows, _, cols = w.shape
    tr = next(t for t in (203, 174, 128, 64, 42, 32, 29, 16, 8, 7, 6, 4, 3, 2, 1) if rows % t == 0)

    def kern(w_ref, g_ref, m_ref, v_ref, go_ref, d_ref, nm_ref, nv_ref):
        g_ = g_ref[...]
        go_ref[...] = g_
        d_ref[...], nm_ref[...], nv_ref[...] = _adam_math(w_ref[...], g_, m_ref[...], v_ref[...])

    spec = pl.BlockSpec((tr, 1, cols), lambda i: (i, 0, 0))
    return _pc(kern, name, (rows // tr,), [spec] * 4, [spec] * 4, [SDS(w.shape, F32)] * 4)(w, g, m, v)


def _adamw_big(w, mine, theirs, m, v, core, name):
    _, rows, cols = w.shape
    half = rows // 2
    tr = next(t for t in (256, 176, 128, 64, 32, 16, 8) if half % t == 0)
    nbh = half // tr

    def kern(c_ref, w_ref, a_ref, b_ref, m_ref, v_ref, g_ref, d_ref, nm_ref, nv_ref):
        g_ = jnp.where(pl.program_id(0) // nbh == c_ref[0], a_ref[...], b_ref[...])
        g_ref[0] = g_
        d_ref[0], nm_ref[0], nv_ref[0] = _adam_math(w_ref[0], g_, m_ref[0], v_ref[0])

    full = pl.BlockSpec((1, tr, cols), lambda i, c_ref: (0, i, 0))
    part = pl.BlockSpec((tr, cols), lambda i, c_ref: (i % nbh, 0))
    return pl.pallas_call(
        kern, name=name,
        grid_spec=pltpu.PrefetchScalarGridSpec(num_scalar_prefetch=1, grid=(rows // tr,),
                                               in_specs=[full, part, part, full, full], out_specs=[full] * 4),
        out_shape=[SDS(w.shape, F32)] * 4,
        compiler_params=_cparams(dimension_semantics=("arbitrary",)),
    )(core.reshape(1).astype(jnp.int32), w, mine, theirs, m, v)


_WEIGHT_NAMES = ("attn_norm", "w_in", "dn_conv", "dn_a_log", "dn_dt_bias", "dn_out_norm", "swa_q_norm", "swa_k_norm",
                 "swa_sinks", "rel_bias", "w_branch_dn", "w_branch_swa", "w_out", "ffn_norm", "w_gate", "w_up",
                 "w_down")
_CONV_SH = QKVW // N_CHIPS


def kernel(x, attn_norm, w_in, dn_conv, dn_a_log, dn_dt_bias, dn_out_norm, swa_q_norm, swa_k_norm, swa_sinks, rel_bias, w_branch_dn, w_branch_swa, w_out, ffn_norm, w_gate, w_up, w_down, loss_target, m_attn_norm, m_w_in, m_dn_conv, m_dn_a_log, m_dn_dt_bias, m_dn_out_norm, m_swa_q_norm, m_swa_k_norm, m_swa_sinks, m_rel_bias, m_w_branch_dn, m_w_branch_swa, m_w_out, m_ffn_norm, m_w_gate, m_w_up, m_w_down, v_attn_norm, v_w_in, v_dn_conv, v_dn_a_log, v_dn_dt_bias, v_dn_out_norm, v_swa_q_norm, v_swa_k_norm, v_swa_sinks, v_rel_bias, v_w_branch_dn, v_w_branch_swa, v_w_out, v_ffn_norm, v_w_gate, v_w_up, v_w_down):
    w = dict(attn_norm=attn_norm, w_in=w_in, dn_conv=dn_conv, dn_a_log=dn_a_log, dn_dt_bias=dn_dt_bias,
             dn_out_norm=dn_out_norm, swa_q_norm=swa_q_norm, swa_k_norm=swa_k_norm, swa_sinks=swa_sinks,
             rel_bias=rel_bias, w_branch_dn=w_branch_dn, w_branch_swa=w_branch_swa, w_out=w_out, ffn_norm=ffn_norm,
             w_gate=w_gate, w_up=w_up, w_down=w_down)
    m = dict(attn_norm=m_attn_norm, w_in=m_w_in, dn_conv=m_dn_conv, dn_a_log=m_dn_a_log, dn_dt_bias=m_dn_dt_bias,
             dn_out_norm=m_dn_out_norm, swa_q_norm=m_swa_q_norm, swa_k_norm=m_swa_k_norm, swa_sinks=m_swa_sinks,
             rel_bias=m_rel_bias, w_branch_dn=m_w_branch_dn, w_branch_swa=m_w_branch_swa, w_out=m_w_out,
             ffn_norm=m_ffn_norm, w_gate=m_w_gate, w_up=m_w_up, w_down=m_w_down)
    v = dict(attn_norm=v_attn_norm, w_in=v_w_in, dn_conv=v_dn_conv, dn_a_log=v_dn_a_log, dn_dt_bias=v_dn_dt_bias,
             dn_out_norm=v_dn_out_norm, swa_q_norm=v_swa_q_norm, swa_k_norm=v_swa_k_norm, swa_sinks=v_swa_sinks,
             rel_bias=v_rel_bias, w_branch_dn=v_w_branch_dn, w_branch_swa=v_w_branch_swa, w_out=v_w_out,
             ffn_norm=v_ffn_norm, w_gate=v_w_gate, w_up=v_w_up, w_down=v_w_down)
    shapes = {n: w[n].shape for n in _WEIGHT_NAMES}

    def two_d(a):
        return a.reshape(a.shape[-2], a.shape[-1]) if a.ndim == 3 else a

    core = lax.axis_index("c")
    chip = 2 * lax.axis_index("x") + lax.axis_index("y")
    small_shapes = {n: two_d(w[n]).shape for n, _ in _SMALL}
    small_shapes["dn_conv"] = (CONV, QKVW)

    conv_loc = two_d(w["dn_conv"])
    conv_part = lax.dynamic_update_slice(jnp.zeros((CONV, QKVW), F32), jnp.where(core == 0, conv_loc, 0.0),
                                         (0, chip * _CONV_SH))
    conv_full = _all_sum_small(conv_part.reshape(CONV * QKVW // 128, 128), "gather_conv").reshape(CONV, QKVW)

    flipped = ("w_gate", "w_up")

    def natural(a, n):
        return a.transpose(0, 2, 1) if n in flipped else a

    w_bf = [two_d(natural(w[n], n).astype(BF16)) for n in _BIG_NAMES]
    (w_in_g,) = _gather_weights(w_bf[:1], chip)
    windows = _gather_windows(w_bf[1:])
    after_sync = w_in_g[0, :8, :128].astype(F32) + conv_full[0:1, :128]
    send_sems, recv_sems, w_thru, l_thru, token = _split_start(
        "gather_start", w_bf[1:], _own_slot(w_bf[1:], chip), after_sync, windows)

    def late(after):
        lands = _split_wait("gather_wait", w_thru, l_thru, send_sems, recv_sems, after, windows)
        g = dict(zip(_BIG_NAMES[1:], _sibling_fill(lands)))
        return dict(wa=g["w_branch_dn"], wb=g["w_branch_swa"], w_out=g["w_out"].reshape(D, D), wg=g["w_gate"],
                    wu=g["w_up"], wd=g["w_down"])

    wts = dict(w_in_p=_w_in_to_padded(w_in_g), dn_conv=conv_full, late=late)
    for n, _ in _SMALL[:-1]:
        wts[n] = two_d(w[n])
    wts["attn_norm"] = wts["attn_norm"] + token[0:1, 0:1]

    early = {}

    ffn = {}

    def send_ffn(grads):
        gs = [grads["w_gate"], grads["w_up"], grads["w_down"]]
        lands = [lax.empty((N_CHIPS, g.shape[1] // 2, g.shape[2]), g.dtype) for g in gs]
        ffn["sems"], ffn["recv"], ffn["src"], ffn["land"], tok = _split_start(
            "swap_ffn_start", gs, lands, gs[0][0, :8, :128], _swap_windows(gs), _sibling_peer, 1)
        return tok

    def send_early(grads):
        small = [grads["w_branch_dn"], grads["w_branch_swa"], grads["w_out"].reshape(N_CHIPS, CSH, D)]
        big = [grads["w_gate"], grads["w_up"], grads["w_down"]]
        big, got_big = _split_wait("swap_ffn_wait", ffn["src"], ffn["land"], ffn["sems"], ffn["recv"], small[0],
                                   _swap_windows(big), _sibling_peer, with_sources=True)
        gots = list(_swap_halves(small, "swap_halves_early")) + list(got_big)
        parts = _pair_sum(small + list(big), gots, core, "pair_sum_early")
        own = [lax.dynamic_index_in_dim(p, chip, axis=0, keepdims=False) for p in parts]
        early["sems"], early["recv"], early["src"], early["land"], tok = _split_start(
            "exchange_start", parts, _own_slot(own, chip), parts[0][0, :8, :128], _exchange_windows())
        return tok

    last = {}

    def send_in(g_in_p):
        g_in = [_padded_to_w_in(g_in_p)]
        parts = _pair_sum(g_in, _swap_halves(g_in, "swap_halves_in"), core, "pair_sum_in")
        own = [lax.dynamic_index_in_dim(p, chip, axis=0, keepdims=False) for p in parts]
        last["sems"], last["recv"], last["src"], last["land"], tok = _split_start(
            "exchange_in_start", parts, _own_slot(own, chip), parts[0][0, :8, :128], _exchange_windows())
        return tok

    wts["send_ffn"] = send_ffn
    wts["send_early"] = send_early
    wts["send_in"] = send_in
    loss_sum, grad_x, grads = _local_step(x[0], loss_target[0], wts)

    small_sum = _all_sum_small(_pack_small(grads, loss_sum), "all_sum_small")
    loss = small_sum.reshape(-1)[_LOSS_OFF]
    g_small = _unpack_small(small_sum, small_shapes)

    q_early = _split_wait("exchange_wait", early["src"], early["land"], early["sems"], early["recv"], small_sum,
                          _exchange_windows())
    red_early = _chip_sum(list(q_early), "chip_sum_early")
    their_early = _swap_reduced(red_early, "swap_reduced_early")
    g_out, d_out, m_out, v_out = {}, {}, {}, {}
    for n, mine, other in zip(_BIG_NAMES[1:], red_early, their_early):
        res = _adamw_big(natural(w[n], n), mine, other, natural(m[n], n), natural(v[n], n), core, "adamw_" + n)
        g_out[n], d_out[n], m_out[n], v_out[n] = (natural(t, n) for t in res)

    q_in = _split_wait("exchange_in_wait", last["src"], last["land"], last["sems"], last["recv"],
                       d_out[_BIG_NAMES[-1]], _exchange_windows())
    reduced = _chip_sum(list(q_in), "chip_sum_in")
    theirs = _swap_reduced(reduced, "swap_reduced_in")

    def rows1(a):
        return a.transpose(2, 0, 1)

    def unrows1(a):
        return a.transpose(1, 2, 0)

    g_in_blk = jnp.concatenate([jnp.where(core == 0, reduced[0], theirs[0]),
                                jnp.where(core == 0, theirs[0], reduced[0])], axis=0)
    g_in_r = rows1(g_in_blk[None])
    res = _adamw_rows1(rows1(w["w_in"]), g_in_r, rows1(m["w_in"]), rows1(v["w_in"]), "adamw_w_in")
    g_out["w_in"], d_out["w_in"], m_out["w_in"], v_out["w_in"] = (unrows1(t) for t in res)
    g_conv = lax.dynamic_slice(g_small["dn_conv"], (0, chip * _CONV_SH), (CONV, _CONV_SH))
    g_out["dn_conv"] = g_conv.reshape(shapes["dn_conv"])
    d_, m_, v_ = _adamw(conv_loc, g_conv, two_d(m["dn_conv"]), two_d(v["dn_conv"]), "adamw_dn_conv")
    d_out["dn_conv"], m_out["dn_conv"], v_out["dn_conv"] = (t.reshape(shapes["dn_conv"]) for t in (d_, m_, v_))

    def packed(src):
        vals = {n: src[n] for n, _ in _SMALL[:-1]}
        vals["dn_conv"] = jnp.zeros((CONV * QKVW,), F32)
        return _pack_small(vals)

    d_s, m_s, v_s = _adamw(packed(w), small_sum, packed(m), packed(v), "adamw_small")
    d_small, m_small, v_small = (_unpack_small(t, small_shapes) for t in (d_s, m_s, v_s))
    for n, _ in _SMALL[:-1]:
        g_out[n] = g_small[n].reshape(shapes[n])
        d_out[n], m_out[n], v_out[n] = (t[n].reshape(shapes[n]) for t in (d_small, m_small, v_small))

    return (loss, grad_x[None], *[g_out[n] for n in _WEIGHT_NAMES], *[d_out[n] for n in _WEIGHT_NAMES],
            *[m_out[n] for n in _WEIGHT_NAMES], *[v_out[n] for n in _WEIGHT_NAMES])
```

```python
import functools
import math

import numpy as np
import jax
import jax.numpy as jnp
from jax import lax
from jax.experimental import pallas as pl
from jax.experimental.pallas import tpu as pltpu

F32 = jnp.float32
BF16 = jnp.bfloat16
SDS = jax.ShapeDtypeStruct

D = 1024
DN_H = 4
DH = 128
DNW = DN_H * DH
QKVW = 3 * DNW
CONV = 4
CHUNK = 64
SWA_H = 8
SWA_KV = 2
SWA_G = SWA_H // SWA_KV
SWA_D = 64
SWAW = SWA_H * SWA_D
SWAKW = SWA_KV * SWA_D
BLK = 128
NBUCKET = 32
MAXDIST = 128
DFF = 2816
D_IN = QKVW + DNW + 2 * DN_H + SWAW + 2 * SWAKW + 2 * D
EPS = 1e-6
NEG = -1e30

ADAM_LR = 0.001
ADAM_B1 = 0.9
ADAM_B2 = 0.999
ADAM_EPS = 1e-08
ADAM_WD = 0.01
ADAM_STEP = 10

C_QKV, C_Z, C_GATE, C_SQ, C_SK, C_SV, C_BA = 0, 1536, 2048, 4096, 4608, 4736, 4864
PW = 5120
_ORIG_PIECES = (
    (0, QKVW, C_QKV),
    (QKVW, DNW, C_Z),
    (QKVW + DNW, 2 * DN_H, C_BA),
    (QKVW + DNW + 2 * DN_H, SWAW, C_SQ),
    (QKVW + DNW + 2 * DN_H + SWAW, SWAKW, C_SK),
    (QKVW + DNW + 2 * DN_H + SWAW + SWAKW, SWAKW, C_SV),
    (QKVW + DNW + 2 * DN_H + SWAW + 2 * SWAKW, 2 * D, C_GATE),
)

N_CHIPS = 4
FSH = DFF // N_CHIPS
CSH = D // N_CHIPS
VMEM_LIMIT = 48 * 1024 * 1024
MESH = pl.DeviceIdType.MESH

_BIG = (
    ("w_in", D, D_IN // N_CHIPS),
    ("w_branch_dn", DNW, CSH),
    ("w_branch_swa", SWAW, CSH),
    ("w_out", CSH, D),
    ("w_gate", FSH, D),
    ("w_up", FSH, D),
    ("w_down", FSH, D),
)
_BIG_NAMES = tuple(n for n, _, _ in _BIG)

_SMALL = (
    ("attn_norm", D), ("ffn_norm", D), ("dn_out_norm", DH), ("swa_q_norm", SWA_D), ("swa_k_norm", SWA_D),
    ("swa_sinks", SWA_H), ("dn_a_log", DN_H), ("dn_dt_bias", DN_H), ("rel_bias", NBUCKET * SWA_H),
    ("dn_conv", CONV * QKVW),
)
_SMALL_OFF = {}
_o = 0
for _n, _s in _SMALL:
    _SMALL_OFF[_n] = (_o, _s)
    _o += _s
_LOSS_OFF = _o
_SMALL_ROWS = -(-(_o + 1) // (8 * 128)) * 8


def _cparams(**kw):
    return pltpu.CompilerParams(vmem_limit_bytes=VMEM_LIMIT, **kw)


_DIMS = {
    "nn": (((1,), (0,)), ((), ())),
    "nt": (((1,), (1,)), ((), ())),
    "tn": (((0,), (0,)), ((), ())),
    "bnn": (((2,), (1,)), ((0,), (0,))),
    "bnt": (((2,), (2,)), ((0,), (0,))),
    "btn": (((1,), (1,)), ((0,), (0,))),
}


def _raw_dot(a, b, kind, exact):
    if exact:
        prec = lax.Precision.HIGH if exact == "x3" else lax.Precision.HIGHEST
        return lax.dot_general(a, b, _DIMS[kind], precision=prec, preferred_element_type=F32)
    return lax.dot_general(a.astype(BF16), b.astype(BF16), _DIMS[kind], preferred_element_type=F32)


@functools.partial(jax.custom_vjp, nondiff_argnums=(2, 3))
def _dot(a, b, kind, exact):
    return _raw_dot(a, b, kind, exact)


def _dot_fwd(a, b, kind, exact):
    return _raw_dot(a, b, kind, exact), (a, b)


def _dot_bwd(kind, exact, res, g):
    a, b = res
    pre = kind[:-2]
    nn, nt, tn = pre + "nn", pre + "nt", pre + "tn"
    if kind == nn:
        return _dot(g, b, nt, exact), _dot(a, g, tn, exact)
    if kind == nt:
        return _dot(g, b, nn, exact), _dot(g, a, tn, exact)
    return _dot(b, g, nt, exact), _dot(a, g, nn, exact)


_dot.defvjp(_dot_fwd, _dot_bwd)


def _silu(x):
    return x * jax.nn.sigmoid(x)


def _f_rms(x, gain):
    return x * lax.rsqrt(jnp.mean(x * x, axis=-1, keepdims=True) + EPS) * gain


def _f_dn_pre(xs0, xs1, xs2, xs3, ba, cw, alog, dtb):
    rows = xs0.shape[0]
    c = xs0 * cw[0:1] + xs1 * cw[1:2] + xs2 * cw[2:3] + xs3 * cw[3:4]
    qkv = _silu(c)
    qs, ks, bbs, gbs = [], [], [], []
    for h in range(DN_H):
        qh = qkv[:, h * DH:(h + 1) * DH]
        kh = qkv[:, DNW + h * DH:DNW + (h + 1) * DH]
        qs.append(qh * lax.rsqrt(jnp.sum(qh * qh, axis=-1, keepdims=True) + EPS) * (DH ** -0.5))
        ks.append(kh * lax.rsqrt(jnp.sum(kh * kh, axis=-1, keepdims=True) + EPS))
        beta = jax.nn.sigmoid(ba[:, h:h + 1])
        ar = ba[:, DN_H + h:DN_H + h + 1] + dtb[:, h:h + 1]
        softplus = jnp.maximum(ar, 0.0) + jnp.log1p(jnp.exp(-jnp.abs(ar)))
        g = -jnp.exp(alog[:, h:h + 1]) * softplus
        bbs.append(jnp.broadcast_to(beta, (rows, DH)))
        gbs.append(jnp.broadcast_to(g, (rows, DH)))
    return (jnp.concatenate(qs, axis=1), jnp.concatenate(ks, axis=1), qkv[:, 2 * DNW:],
            jnp.concatenate(bbs, axis=1), jnp.concatenate(gbs, axis=1))


def _f_dn_post(o, z, gain):
    ys = []
    for h in range(DN_H):
        oh = o[:, h * DH:(h + 1) * DH]
        zh = z[:, h * DH:(h + 1) * DH]
        ys.append(oh * lax.rsqrt(jnp.mean(oh * oh, axis=-1, keepdims=True) + EPS) * gain * _silu(zh))
    return jnp.concatenate(ys, axis=1)


def _f_merge(pa, pb, ga, gb):
    return jax.nn.sigmoid(ga) * pa + jax.nn.sigmoid(gb) * pb


@jax.custom_vjp
def _f_swiglu(g, u):
    return _silu(g) * u


def _f_swiglu_fwd(g, u):
    return _silu(g) * u, (g, u)


def _f_swiglu_bwd(res, d):
    g, u = res
    s = jax.nn.sigmoid(g)
    act = g * s
    return d * u * (s + act * (1.0 - s)), d * act


_f_swiglu.defvjp(_f_swiglu_fwd, _f_swiglu_bwd)


@jax.custom_vjp
def _unit_lower_inverse(a):
    c = a.shape[-1]
    eye = (lax.broadcasted_iota(jnp.int32, a.shape, 1) == lax.broadcasted_iota(jnp.int32, a.shape, 2)).astype(F32)
    p = -a
    t = eye + p
    for _ in range(max(c.bit_length() - 2, 0)):
        p = _raw_dot(p, p, "bnn", "x3")
        t = t + _raw_dot(t, p, "bnn", "x3")
    return t


def _unit_lower_inverse_fwd(a):
    t = _unit_lower_inverse(a)
    return t, t


def _unit_lower_inverse_bwd(t, g):
    return (-_raw_dot(_raw_dot(t, g, "btn", "x3"), t, "bnt", "x3"),)


_unit_lower_inverse.defvjp(_unit_lower_inverse_fwd, _unit_lower_inverse_bwd)


@jax.custom_vjp
def _known_inverse(a, t):
    return t


def _known_inverse_fwd(a, t):
    return t, t


def _known_inverse_bwd(t, g):
    return _unit_lower_inverse_bwd(t, g)[0], jnp.zeros_like(t)


_known_inverse.defvjp(_known_inverse_fwd, _known_inverse_bwd)


def _f_chunk(q, k, v, gb, bb, s, t_known=None, with_t=False):
    c = CHUNK
    nh = q.shape[0]
    ii = lax.broadcasted_iota(jnp.int32, (nh, c, c), 1)
    jj = lax.broadcasted_iota(jnp.int32, (nh, c, c), 2)
    incl = ii >= jj
    strict = ii > jj
    eye = (ii == jj).astype(F32)
    gcb = _dot(incl.astype(F32), gb, "bnn", "x3")
    lane0 = (lax.broadcasted_iota(jnp.int32, (nh, c, DH), 2) == 0).astype(F32)
    gcol = gcb[:, :, :c]
    grow = _dot(lane0, gcb, "bnt", "x3")
    decay = jnp.where(incl, jnp.exp(jnp.where(incl, gcol - grow, 0.0)), 0.0)
    kb = k * bb
    vb = v * bb
    a = jnp.where(strict, _dot(kb, k, "bnt", False) * decay, 0.0)
    t = _unit_lower_inverse(a) if t_known is None else _known_inverse(a, t_known)
    eg = jnp.exp(gcb)
    u = _dot(t, vb, "bnn", "x3")
    w = _dot(t, kb * eg, "bnn", "x3")
    qk = jnp.where(incl, _dot(q, k, "bnt", False) * decay, 0.0)
    qe = q * eg
    glast = gcb[:, c - 1:c, :]
    k_dec = k * jnp.exp(glast - gcb)
    e_last = jnp.exp(glast)
    outs = []
    for g in range(nh // DN_H):
        sl = slice(g * DN_H, (g + 1) * DN_H)
        v_new = u[sl] - _dot(w[sl], s, "bnn", False)
        outs.append(_dot(qe[sl], s, "bnn", False) + _dot(qk[sl], v_new, "bnn", False))
        s = s * e_last[sl] + _dot(k_dec[sl], v_new, "btn", False)
    o = jnp.concatenate(outs, axis=0)
    return (o, s, t) if with_t else (o, s)


def _f_swa(q8, kp, kc, vp, vc, bias8, qg, kg, sink, mask):
    kb = jnp.concatenate([kp, kc], axis=1)
    vb = jnp.concatenate([vp, vc], axis=1)
    kn = kb * lax.rsqrt(jnp.mean(kb * kb, axis=-1, keepdims=True) + EPS) * kg

    def rows(per_head):
        return jnp.stack([jnp.concatenate([per_head(kv, g) for g in range(SWA_G)], axis=0)
                          for kv in range(SWA_KV)], axis=0)

    qq = rows(lambda kv, g: q8[kv * SWA_G + g])
    qn = qq * lax.rsqrt(jnp.mean(qq * qq, axis=-1, keepdims=True) + EPS) * qg * (SWA_D ** -0.5)
    lg = _dot(qn, kn, "bnt", False) + rows(lambda kv, g: bias8[kv * SWA_G + g])
    lg = jnp.where(rows(lambda kv, g: mask), lg, NEG)
    sk = rows(lambda kv, g: jnp.broadcast_to(sink[kv][:, g:g + 1], (BLK, 1)))
    m = lax.stop_gradient(jnp.maximum(jnp.max(lg, axis=-1, keepdims=True), sk))
    p = jnp.exp(lg - m)
    den = jnp.sum(p, axis=-1, keepdims=True) + jnp.exp(sk - m)
    out = _dot(p * (1.0 / den), vb, "bnn", False)
    return jnp.stack([out[kv, g * BLK:(g + 1) * BLK] for kv in range(SWA_KV) for g in range(SWA_G)], axis=0)


def _bdot(a, b, kind="nn"):
    return lax.dot_general(a.astype(BF16), b.astype(BF16), _DIMS[kind], preferred_element_type=F32)


def _pc(kern, name, grid, in_specs, out_specs, out_shape, scratch=()):
    return pl.pallas_call(
        kern, name=name, grid=grid, in_specs=in_specs, out_specs=out_specs, out_shape=out_shape,
        scratch_shapes=list(scratch), compiler_params=_cparams(dimension_semantics=("arbitrary",) * len(grid)))


def _mm(a, b, kind, out_dtype, tm, tn, name):
    if kind == "tn":
        k, m = a.shape
    else:
        m, k = a.shape
    n = b.shape[0] if kind == "nt" else b.shape[1]
    tm, tn = min(tm, m), min(tn, n)
    assert m % tm == 0 and n % tn == 0, (name, a.shape, b.shape, tm, tn)

    def kern(a_ref, b_ref, o_ref):
        o_ref[...] = _bdot(a_ref[...], b_ref[...], kind).astype(o_ref.dtype)

    a_spec = pl.BlockSpec((k, tm), lambda i, j: (0, i)) if kind == "tn" else pl.BlockSpec((tm, k), lambda i, j: (i, 0))
    b_spec = pl.BlockSpec((tn, k), lambda i, j: (j, 0)) if kind == "nt" else pl.BlockSpec((k, tn), lambda i, j: (0, j))
    return _pc(kern, name, (m // tm, n // tn), [a_spec, b_spec], pl.BlockSpec((tm, tn), lambda i, j: (i, j)),
               SDS((m, n), out_dtype))(a, b)


def _rows(body, name, m, tm, row_ins, full_ins, row_outs, acc_outs=()):
    n_r, n_f, n_o, n_a = len(row_ins), len(full_ins), len(row_outs), len(acc_outs)
    assert m % tm == 0

    def kern(*refs):
        r = refs[:n_r]
        f = refs[n_r:n_r + n_f]
        o = refs[n_r + n_f:n_r + n_f + n_o]
        acc = refs[n_r + n_f + n_o:]
        outs, sums = body([x[...] for x in r], [x[...] for x in f])
        for ref, val in zip(o, outs, strict=True):
            ref[...] = val.astype(ref.dtype)
        if n_a:
            @pl.when(pl.program_id(0) == 0)
            def _():
                for ref in acc:
                    ref[...] = jnp.zeros(ref.shape, F32)

            for ref, val in zip(acc, sums, strict=True):
                ref[...] += val

    in_specs = [pl.BlockSpec((tm, w), functools.partial(lambda i, cb: (i, cb), cb=cb)) for _, w, cb in row_ins]
    in_specs += [pl.BlockSpec(x.shape, lambda i: (0, 0)) for x in full_ins]
    out_specs = [pl.BlockSpec((tm, w), lambda i: (i, 0)) for w, _ in row_outs]
    out_specs += [pl.BlockSpec(s, lambda i: (0, 0)) for s in acc_outs]
    out_shape = [SDS((m, w), dt) for w, dt in row_outs]
    out_shape += [SDS(s, F32) for s in acc_outs]
    return _pc(kern, name, (m // tm,), in_specs, out_specs, out_shape)(*[x for x, _, _ in row_ins], *full_ins)


def _whole(x):
    return (x, x.shape[1], 0)


def _resident(shape):
    return pl.BlockSpec(shape, lambda i: (0,) * len(shape), pipeline_mode=pl.Buffered(1))


def _row_pieces(tm, piece):
    piece = min(piece, tm)
    return [slice(r, r + piece) for r in range(0, tm, piece)]


def _zero_first(refs):
    @pl.when(pl.program_id(0) == 0)
    def _():
        for ref in refs:
            ref[...] = jnp.zeros(ref.shape, F32)


GROUP = 4


def _heads(ref):
    return jnp.stack([ref[g * CHUNK:(g + 1) * CHUNK, h * DH:(h + 1) * DH]
                      for g in range(GROUP) for h in range(DN_H)], axis=0)


def _unheads(ref, val):
    for g in range(GROUP):
        for h in range(DN_H):
            ref[g * CHUNK:(g + 1) * CHUNK, h * DH:(h + 1) * DH] = val[g * DN_H + h]


def _dn_chunks_fwd(q, k, v, gb, bb):
    s_len = q.shape[0]
    ng = s_len // (GROUP * CHUNK)

    def kern(q_ref, k_ref, v_ref, g_ref, b_ref, o_ref, sall_ref, t_ref, state):
        _zero_first([state])
        s = state[...]
        sall_ref[0] = s
        o, s_new, t = _f_chunk(*[_heads(r) for r in (q_ref, k_ref, v_ref, g_ref, b_ref)], s, with_t=True)
        _unheads(o_ref, o)
        t_ref[0] = t
        state[...] = s_new

    blk = pl.BlockSpec((GROUP * CHUNK, DNW), lambda c: (c, 0))
    return _pc(kern, "dn_chunks_fwd", (ng,), [blk] * 5,
               [blk, pl.BlockSpec((1, DN_H, DH, DH), lambda c: (c, 0, 0, 0)),
                pl.BlockSpec((1, GROUP * DN_H, CHUNK, CHUNK), lambda c: (c, 0, 0, 0))],
               [SDS((s_len, DNW), F32), SDS((ng, DN_H, DH, DH), F32), SDS((ng, GROUP * DN_H, CHUNK, CHUNK), F32)],
               scratch=[pltpu.VMEM((DN_H, DH, DH), F32)])(q, k, v, gb, bb)


def _dn_chunks_bwd(q, k, v, gb, bb, s_all, t_all, d_o):
    s_len = q.shape[0]
    ng = s_len // (GROUP * CHUNK)

    def kern(q_ref, k_ref, v_ref, g_ref, b_ref, sall_ref, t_ref, do_ref, dq_ref, dk_ref, dv_ref, dg_ref, db_ref,
             dstate):
        _zero_first([dstate])
        fn = functools.partial(_f_chunk, t_known=t_ref[0])
        _, vjp = jax.vjp(fn, *[_heads(r) for r in (q_ref, k_ref, v_ref, g_ref, b_ref)], sall_ref[0])
        *d_ins, ds = vjp((_heads(do_ref), dstate[...]))
        for ref, val in zip((dq_ref, dk_ref, dv_ref, dg_ref, db_ref), d_ins, strict=True):
            _unheads(ref, val)
        dstate[...] = ds

    blk = pl.BlockSpec((GROUP * CHUNK, DNW), lambda c: (ng - 1 - c, 0))
    return _pc(kern, "dn_chunks_bwd", (ng,),
               [blk] * 5 + [pl.BlockSpec((1, DN_H, DH, DH), lambda c: (ng - 1 - c, 0, 0, 0)),
                            pl.BlockSpec((1, GROUP * DN_H, CHUNK, CHUNK), lambda c: (ng - 1 - c, 0, 0, 0)), blk],
               [blk] * 5, [SDS((s_len, DNW), F32)] * 5,
               scratch=[pltpu.VMEM((DN_H, DH, DH), F32)])(q, k, v, gb, bb, s_all, t_all, d_o)


def _t5_bucket_table():
    qi = np.arange(BLK)[:, None]
    kj = np.arange(2 * BLK)[None, :]
    dist = BLK + qi - kj
    n = np.maximum(dist, 0)
    max_exact = NBUCKET // 2
    nf = np.maximum(n, 1).astype(np.float32)
    large = max_exact + (np.log(nf / np.float32(max_exact)) / np.float32(math.log(MAXDIST / max_exact))
                         * np.float32(NBUCKET - max_exact)).astype(np.int32)
    large = np.minimum(large, NBUCKET - 1)
    return np.where(n < max_exact, n, large)


def _bucket_onehot_t():
    table = _t5_bucket_table().reshape(-1)
    return (np.arange(NBUCKET)[:, None] == table[None, :]).astype(np.float32)


def _swa_mask(first):
    qi = lax.broadcasted_iota(jnp.int32, (BLK, 2 * BLK), 0)
    kj = lax.broadcasted_iota(jnp.int32, (BLK, 2 * BLK), 1)
    dist = BLK + qi - kj
    window = (dist >= 0) & (dist < BLK)
    return window & ((kj >= BLK) | jnp.logical_not(first))


def _bias_expand(rel_bias_t):
    onehot = jnp.asarray(_bucket_onehot_t())

    def kern(r_ref, oh_ref, o_ref):
        o_ref[...] = _raw_dot(r_ref[...], oh_ref[...], "nn", True)

    return pl.pallas_call(
        kern, name="bias_expand", out_shape=SDS((SWA_H, BLK * 2 * BLK), F32), compiler_params=_cparams(),
    )(rel_bias_t, onehot)


def _bias_reduce(d_bias_flat):
    onehot = jnp.asarray(_bucket_onehot_t())

    def kern(d_ref, oh_ref, o_ref):
        o_ref[...] = _raw_dot(d_ref[...], oh_ref[...], "nt", True)

    return pl.pallas_call(
        kern, name="bias_reduce", out_shape=SDS((SWA_H, NBUCKET), F32), compiler_params=_cparams(),
    )(d_bias_flat, onehot)


def _swa_specs(nb, rev):
    def blk(n):
        return (nb - 1 - n) if rev else n

    def before(n):
        return jnp.maximum(blk(n) - 1, 0)

    q_spec = pl.BlockSpec((BLK, SWAW), lambda n: (blk(n), C_SQ // SWAW))
    k_cur = pl.BlockSpec((BLK, SWAKW), lambda n: (blk(n), C_SK // SWAKW))
    k_prev = pl.BlockSpec((BLK, SWAKW), lambda n: (before(n), C_SK // SWAKW))
    v_cur = pl.BlockSpec((BLK, SWAKW), lambda n: (blk(n), C_SV // SWAKW))
    v_prev = pl.BlockSpec((BLK, SWAKW), lambda n: (before(n), C_SV // SWAKW))
    bias = pl.BlockSpec((SWA_H, BLK, 2 * BLK), lambda n: (0, 0, 0))
    gain = pl.BlockSpec((1, SWA_D), lambda n: (0, 0))
    sink = pl.BlockSpec((SWA_KV, 1, SWA_G), lambda n: (0, 0, 0))
    wide = pl.BlockSpec((BLK, SWAW), lambda n: (blk(n), 0))
    narrow = pl.BlockSpec((BLK, SWAKW), lambda n: (blk(n), 0))
    return [q_spec, k_prev, k_cur, v_prev, v_cur, bias, gain, gain, sink], wide, narrow


def _split_heads(x):
    return jnp.stack([x[:, h * SWA_D:(h + 1) * SWA_D] for h in range(x.shape[1] // SWA_D)], axis=0)


def _join_heads(x):
    return jnp.concatenate([x[h] for h in range(x.shape[0])], axis=1)


def _swa_fwd(proj, bias, qg, kg, sinks):
    s_len = proj.shape[0]
    nb = s_len // BLK
    in_specs, wide, _ = _swa_specs(nb, False)

    def kern(q_ref, kp_ref, kc_ref, vp_ref, vc_ref, b_ref, qg_ref, kg_ref, s_ref, o_ref):
        mask = _swa_mask(pl.program_id(0) == 0)
        o8 = _f_swa(*[_split_heads(r[...]) for r in (q_ref, kp_ref, kc_ref, vp_ref, vc_ref)], b_ref[...], qg_ref[...],
                    kg_ref[...], s_ref[...], mask)
        o_ref[...] = _join_heads(o8).astype(BF16)

    return _pc(kern, "swa_fwd", (nb,), in_specs, wide, SDS((s_len, SWAW), BF16))(
        proj, proj, proj, proj, proj, bias, qg, kg, sinks)


def _swa_bwd(proj, bias, qg, kg, sinks, d_out):
    s_len = proj.shape[0]
    nb = s_len // BLK
    in_specs, wide, narrow = _swa_specs(nb, True)

    def kern(q_ref, kp_ref, kc_ref, vp_ref, vc_ref, b_ref, qg_ref, kg_ref, s_ref, do_ref,
             dq_ref, dk_ref, dv_ref, db_ref, dqg_ref, dkg_ref, ds_ref, carry_k, carry_v):
        n = pl.program_id(0)
        mask = _swa_mask(n == nb - 1)
        _zero_first([carry_k, carry_v, db_ref, ds_ref, dqg_ref, dkg_ref])
        fn = functools.partial(_f_swa, mask=mask)
        _, vjp = jax.vjp(fn, *[_split_heads(r[...]) for r in (q_ref, kp_ref, kc_ref, vp_ref, vc_ref)], b_ref[...],
                         qg_ref[...], kg_ref[...], s_ref[...])
        dq, dkp, dkc, dvp, dvc, dbias, dqg, dkg, dsink = vjp(_split_heads(do_ref[...]))
        dq_ref[...] = _join_heads(dq).astype(BF16)
        dk_ref[...] = (_join_heads(dkc) + carry_k[...]).astype(BF16)
        dv_ref[...] = (_join_heads(dvc) + carry_v[...]).astype(BF16)
        carry_k[...] = _join_heads(dkp)
        carry_v[...] = _join_heads(dvp)
        db_ref[...] += dbias
        dqg_ref[...] += dqg
        dkg_ref[...] += dkg
        ds_ref[...] += dsink

    bias_spec, gain, sink = in_specs[5], in_specs[6], in_specs[8]
    return _pc(
        kern, "swa_bwd", (nb,), in_specs + [wide], [wide, narrow, narrow, bias_spec, gain, gain, sink],
        [SDS((s_len, SWAW), BF16), SDS((s_len, SWAKW), BF16), SDS((s_len, SWAKW), BF16),
         SDS((SWA_H, BLK, 2 * BLK), F32), SDS((1, SWA_D), F32), SDS((1, SWA_D), F32), SDS((SWA_KV, 1, SWA_G), F32)],
        scratch=[pltpu.VMEM((BLK, SWAKW), F32), pltpu.VMEM((BLK, SWAKW), F32)],
    )(proj, proj, proj, proj, proj, bias, qg, kg, sinks, d_out)


def _branch_merge(y_dn, y_swa, wa, wb, proj):
    s_len = y_dn.shape[0]
    tm = min(1024, s_len)

    def kern(ya_ref, yb_ref, wa_ref, wb_ref, ga_ref, gb_ref, pa_ref, pb_ref, m_ref):
        for rows in _row_pieces(tm, 128):
            pa = _bdot(ya_ref[rows, :], wa_ref[0])
            pb = _bdot(yb_ref[rows, :], wb_ref[0])
            pa_ref[rows, :] = pa.astype(BF16)
            pb_ref[rows, :] = pb.astype(BF16)
            m_ref[rows, :] = _f_merge(pa, pb, ga_ref[rows, :], gb_ref[rows, :]).astype(BF16)

    y_spec = pl.BlockSpec((tm, DNW), lambda i, s: (i, 0))
    w_spec = pl.BlockSpec((1, DNW, CSH), lambda i, s: (s, 0, 0))
    o_spec = pl.BlockSpec((tm, CSH), lambda i, s: (i, s))
    ga_spec = pl.BlockSpec((tm, CSH), lambda i, s: (i, C_GATE // CSH + s))
    gb_spec = pl.BlockSpec((tm, CSH), lambda i, s: (i, (C_GATE + D) // CSH + s))
    return _pc(kern, "branch_merge", (s_len // tm, N_CHIPS), [y_spec, y_spec, w_spec, w_spec, ga_spec, gb_spec],
               [o_spec] * 3, [SDS((s_len, D), BF16)] * 3,
               )(y_dn, y_swa, wa, wb, proj, proj)


def _in_proj(x, gain, w_in_p):
    s_len = x.shape[0]
    tm = min(512, s_len)

    def kern(x_ref, g_ref, w_ref, h_ref, p_ref):
        h = _f_rms(x_ref[...], g_ref[...]).astype(BF16)
        h_ref[...] = h
        p_ref[...] = _bdot(h, w_ref[...])

    row = pl.BlockSpec((tm, D), lambda i: (i, 0))
    return _pc(kern, "in_proj", (s_len // tm,),
               [row, pl.BlockSpec((1, D), lambda i: (0, 0)), _resident((D, PW))],
               [row, pl.BlockSpec((tm, PW), lambda i: (i, 0))],
               [SDS((s_len, D), BF16), SDS((s_len, PW), F32)])(x, gain, w_in_p)


def _out_proj(merged, w_out, x, gain):
    s_len = x.shape[0]
    tm = min(512, s_len)

    def kern(m_ref, w_ref, x_ref, g_ref, x1_ref, h2_ref):
        x1 = x_ref[...] + _bdot(m_ref[...], w_ref[...])
        x1_ref[...] = x1
        h2_ref[...] = _f_rms(x1, g_ref[...]).astype(BF16)

    row = pl.BlockSpec((tm, D), lambda i: (i, 0))
    return _pc(kern, "out_proj", (s_len // tm,),
               [row, _resident((D, D)), row, pl.BlockSpec((1, D), lambda i: (0, 0))],
               [row, row], [SDS((s_len, D), F32), SDS((s_len, D), BF16)])(merged, w_out, x, gain)


def _ffn_up(h2, wg, wu):
    s_len = h2.shape[0]
    tm = min(2048, s_len)

    def kern(h_ref, g_ref, u_ref, gt_ref, up_ref, act_ref):
        for rows in _row_pieces(tm, 256):
            h = h_ref[rows, :]
            g = _bdot(h, g_ref[0], "nt")
            u = _bdot(h, u_ref[0], "nt")
            gt_ref[0, rows, :] = g.astype(BF16)
            up_ref[0, rows, :] = u.astype(BF16)
            act_ref[0, rows, :] = _f_swiglu(g, u).astype(BF16)

    w_spec = pl.BlockSpec((1, FSH, D), lambda s, i: (s, 0, 0))
    o_spec = pl.BlockSpec((1, tm, FSH), lambda s, i: (s, i, 0))
    shape = (N_CHIPS, s_len, FSH)
    return _pc(kern, "ffn_up", (N_CHIPS, s_len // tm), [pl.BlockSpec((tm, D), lambda s, i: (i, 0)), w_spec, w_spec],
               [o_spec] * 3, [SDS(shape, BF16)] * 3)(h2, wg, wu)


def _ffn_down_loss(act, wd, x1, target):
    s_len = x1.shape[0]
    tm = min(512, s_len)

    def kern(a_ref, w_ref, x_ref, t_ref, dy_ref, dyb_ref, loss_ref):
        _zero_first([loss_ref])
        for rows in _row_pieces(tm, 128):
            y = x_ref[rows, :]
            for s in range(N_CHIPS):
                y = y + _bdot(a_ref[s, rows, :], w_ref[s])
            d = y - t_ref[rows, :]
            dy = d * (1.0 / D)
            dy_ref[rows, :] = dy
            dyb_ref[rows, :] = dy.astype(BF16)
            loss_ref[...] += jnp.sum(d * d).reshape(1, 1) * (0.5 / D)

    row = pl.BlockSpec((tm, D), lambda i: (i, 0))
    return _pc(kern, "ffn_down_loss", (s_len // tm,),
               [pl.BlockSpec((N_CHIPS, tm, FSH), lambda i: (0, i, 0)),
                _resident((N_CHIPS, FSH, D)), row, row],
               [row, row, pl.BlockSpec((1, 1), lambda i: (0, 0))],
               [SDS((s_len, D), F32), SDS((s_len, D), BF16), SDS((1, 1), F32)])(act, wd, x1, target)


def _ffn_dact(dy_b, wd, gt, up):
    s_len = dy_b.shape[0]
    tm = min(2048, s_len)

    def kern(dy_ref, w_ref, gt_ref, up_ref, dg_ref, du_ref):
        w = w_ref[0]
        for rows in _row_pieces(tm, 256):
            d_act = _bdot(dy_ref[rows, :], w, "nt")
            _, vjp = jax.vjp(_f_swiglu, gt_ref[0, rows, :].astype(F32), up_ref[0, rows, :].astype(F32))
            dg, du = vjp(d_act)
            dg_ref[0, rows, :] = dg.astype(BF16)
            du_ref[0, rows, :] = du.astype(BF16)

    a_spec = pl.BlockSpec((1, tm, FSH), lambda s, i: (s, i, 0))
    shape = (N_CHIPS, s_len, FSH)
    return _pc(kern, "ffn_dact", (N_CHIPS, s_len // tm),
               [pl.BlockSpec((tm, D), lambda s, i: (i, 0)), pl.BlockSpec((1, FSH, D), lambda s, i: (s, 0, 0)),
                a_spec, a_spec],
               [a_spec, a_spec], [SDS(shape, BF16), SDS(shape, BF16)])(dy_b, wd, gt, up)


def _gw_ffn(lhs, rhs, name):
    s_len = rhs.shape[0]
    n = len(lhs)
    tn = 512

    def kern(*refs):
        g = refs[n][...]
        for i in range(n):
            refs[n + 1 + i][0] = _bdot(refs[i][0], g, "tn").astype(BF16)

    a_spec = pl.BlockSpec((1, s_len, FSH), lambda s, j: (s, 0, 0))
    o_spec = pl.BlockSpec((1, FSH, tn), lambda s, j: (s, 0, j))
    return _pc(kern, name, (N_CHIPS, D // tn), [a_spec] * n + [pl.BlockSpec((s_len, tn), lambda s, j: (0, j))],
               [o_spec] * n, [SDS((N_CHIPS, FSH, D), BF16)] * n)(*lhs, rhs)


def _ffn_dh2(d_gt, d_up, wg, wu, x1, dy, gain):
    s_len = x1.shape[0]
    tm = min(512, s_len)

    def kern(dg_ref, du_ref, wg_ref, wu_ref, x_ref, dy_ref, g_ref, dx_ref, dxb_ref, dgain_ref):
        _zero_first([dgain_ref])
        dh2 = jnp.zeros((tm, D), F32)
        for s in range(N_CHIPS):
            dh2 = dh2 + _bdot(dg_ref[s], wg_ref[s]) + _bdot(du_ref[s], wu_ref[s])
        _, vjp = jax.vjp(_f_rms, x_ref[...], g_ref[...])
        dx, dgain = vjp(dh2)
        dx1 = dx + dy_ref[...]
        dx_ref[...] = dx1
        dxb_ref[...] = dx1.astype(BF16)
        dgain_ref[...] += dgain

    row = pl.BlockSpec((tm, D), lambda i: (i, 0))
    d_spec = pl.BlockSpec((N_CHIPS, tm, FSH), lambda i: (0, i, 0))
    w_spec = _resident((N_CHIPS, FSH, D))
    vec = pl.BlockSpec((1, D), lambda i: (0, 0))
    return _pc(kern, "ffn_dh2", (s_len // tm,), [d_spec, d_spec, w_spec, w_spec, row, row, vec],
               [row, row, vec], [SDS((s_len, D), F32), SDS((s_len, D), BF16), SDS((1, D), F32)],
               )(d_gt, d_up, wg, wu, x1, dy, gain)


def _merge_bwd(dx1_b, w_out, pa, pb, proj):
    s_len = dx1_b.shape[0]
    tm = min(512, s_len)

    def kern(dx_ref, w_ref, pa_ref, pb_ref, g_ref, dpa_ref, dpb_ref, dg_ref):
        dm = _bdot(dx_ref[...], w_ref[...], "nt")
        gates = g_ref[...]
        _, vjp = jax.vjp(_f_merge, pa_ref[...].astype(F32), pb_ref[...].astype(F32), gates[:, :D], gates[:, D:])
        dpa, dpb, dga, dgb = vjp(dm)
        dpa_ref[...] = dpa.astype(BF16)
        dpb_ref[...] = dpb.astype(BF16)
        dg_ref[:, :D] = dga.astype(BF16)
        dg_ref[:, D:] = dgb.astype(BF16)

    row = pl.BlockSpec((tm, D), lambda i: (i, 0))
    return _pc(kern, "merge_bwd", (s_len // tm,),
               [row, _resident((D, D)), row, row,
                pl.BlockSpec((tm, 2 * D), lambda i: (i, C_GATE // (2 * D)))],
               [row, row, pl.BlockSpec((tm, 2 * D), lambda i: (i, 0))],
               [SDS((s_len, D), BF16), SDS((s_len, D), BF16), SDS((s_len, 2 * D), BF16)],
               )(dx1_b, w_out, pa, pb, proj)


def _d_branch(d_pa, d_pb, wa, wb):
    s_len = d_pa.shape[0]
    tm = min(512, s_len)

    def kern(da_ref, db_ref, wa_ref, wb_ref, oa_ref, ob_ref):
        acc_a = jnp.zeros((tm, DNW), F32)
        acc_b = jnp.zeros((tm, SWAW), F32)
        for s in range(N_CHIPS):
            acc_a = acc_a + _bdot(da_ref[:, s * CSH:(s + 1) * CSH], wa_ref[s], "nt")
            acc_b = acc_b + _bdot(db_ref[:, s * CSH:(s + 1) * CSH], wb_ref[s], "nt")
        oa_ref[...] = acc_a
        ob_ref[...] = acc_b

    row = pl.BlockSpec((tm, D), lambda i: (i, 0))
    w_spec = pl.BlockSpec((N_CHIPS, DNW, CSH), lambda i: (0, 0, 0))
    out = pl.BlockSpec((tm, DNW), lambda i: (i, 0))
    return _pc(kern, "d_branch", (s_len // tm,), [row, row, w_spec, w_spec], [out, out],
               [SDS((s_len, DNW), F32), SDS((s_len, SWAW), F32)])(d_pa, d_pb, wa, wb)


def _gw_branch(y_dn, y_swa, d_pa, d_pb):
    s_len = y_dn.shape[0]

    def kern(ya_ref, yb_ref, da_ref, db_ref, oa_ref, ob_ref):
        oa_ref[0] = _bdot(ya_ref[...], da_ref[...], "tn").astype(BF16)
        ob_ref[0] = _bdot(yb_ref[...], db_ref[...], "tn").astype(BF16)

    y_spec = pl.BlockSpec((s_len, DNW), lambda s: (0, 0))
    d_spec = pl.BlockSpec((s_len, CSH), lambda s: (0, s))
    o_spec = pl.BlockSpec((1, DNW, CSH), lambda s: (s, 0, 0))
    shape = (N_CHIPS, DNW, CSH)
    return _pc(kern, "gw_branch", (N_CHIPS,), [y_spec, y_spec, d_spec, d_spec], [o_spec, o_spec],
               [SDS(shape, BF16), SDS(shape, BF16)])(y_dn, y_swa, d_pa, d_pb)


def _dh_rms(d_proj, w_in_p, x, dx1, gain):
    s_len = x.shape[0]
    tm = min(512, s_len)

    def kern(dp_ref, w_ref, x_ref, r_ref, g_ref, gx_ref, dgain_ref):
        _zero_first([dgain_ref])
        dh = _bdot(dp_ref[...], w_ref[...], "nt")
        _, vjp = jax.vjp(_f_rms, x_ref[...], g_ref[...])
        dx, dgain = vjp(dh)
        gx_ref[...] = dx + r_ref[...]
        dgain_ref[...] += dgain

    row = pl.BlockSpec((tm, D), lambda i: (i, 0))
    vec = pl.BlockSpec((1, D), lambda i: (0, 0))
    return _pc(kern, "dh_rms", (s_len // tm,),
               [pl.BlockSpec((tm, PW), lambda i: (i, 0)), _resident((D, PW)), row, row, vec],
               [row, vec], [SDS((s_len, D), F32), SDS((1, D), F32)])(d_proj, w_in_p, x, dx1, gain)


HALO = 8


def _rows_down(x, n, above):
    tm = x.shape[0]
    r = pltpu.roll(x, n, 0)
    a = pltpu.roll(above, n, 0)
    top = jnp.where(lax.broadcasted_iota(jnp.int32, above.shape, 0) < n, a, r[0:HALO])
    return jnp.concatenate([top, r[HALO:tm]], axis=0)


def _rows_up(x, n, below):
    tm = x.shape[0]
    r = pltpu.roll(x, tm - n, 0)
    b = pltpu.roll(below, HALO - n, 0)
    bottom = jnp.where(lax.broadcasted_iota(jnp.int32, below.shape, 0) >= HALO - n, b, r[tm - HALO:tm])
    return jnp.concatenate([r[0:tm - HALO], bottom], axis=0)


def _conv_taps(cur_ref, prev_ref, first):
    cur = cur_ref[...]
    above = jnp.where(first, 0.0, prev_ref[...])
    return [_rows_down(cur, n, above) for n in range(CONV - 1, 0, -1)] + [cur]


def _dn_pre_specs(s_len, tm, blk):
    cur = pl.BlockSpec((tm, QKVW), lambda i: (blk(i), 0))
    prev = pl.BlockSpec((HALO, QKVW), lambda i: (jnp.maximum(blk(i) * (tm // HALO) - 1, 0), 0))
    ba = pl.BlockSpec((tm, 128), lambda i: (blk(i), C_BA // 128))
    row = pl.BlockSpec((tm, DNW), lambda i: (blk(i), 0))
    full = [pl.BlockSpec((CONV, QKVW), lambda i: (0, 0)), pl.BlockSpec((1, DN_H), lambda i: (0, 0)),
            pl.BlockSpec((1, DN_H), lambda i: (0, 0))]
    return cur, prev, ba, row, full


def _dn_pre_fwd(proj, conv_w, alog, dtb):
    s_len = proj.shape[0]
    tm = min(256, s_len)
    cur, prev, ba, row, full = _dn_pre_specs(s_len, tm, lambda i: i)

    def kern(cur_ref, prev_ref, ba_ref, cw_ref, al_ref, dt_ref, q_ref, k_ref, v_ref, bb_ref, gb_ref):
        xs = _conv_taps(cur_ref, prev_ref, pl.program_id(0) == 0)
        outs = _f_dn_pre(*xs, ba_ref[...], cw_ref[...], al_ref[...], dt_ref[...])
        for ref, val in zip((q_ref, k_ref, v_ref, bb_ref, gb_ref), outs, strict=True):
            ref[...] = val

    return _pc(kern, "dn_pre_fwd", (s_len // tm,), [cur, prev, ba] + full, [row] * 5,
               [SDS((s_len, DNW), F32)] * 5)(proj, proj, proj, conv_w, alog, dtb)


def _dn_pre_bwd(proj, conv_w, alog, dtb, cots, others):
    s_len = proj.shape[0]
    tm = min(256, s_len)
    nb = s_len // tm
    cur, prev, ba, row, full = _dn_pre_specs(s_len, tm, lambda i: nb - 1 - i)
    n_o = len(others)
    assert QKVW + sum(t.shape[1] for t in others) + 128 == C_BA + 128

    def kern(cur_ref, prev_ref, ba_ref, cw_ref, al_ref, dt_ref, dq_ref, dk_ref, dv_ref, dbb_ref, dgb_ref, *rest):
        o_refs = rest[:n_o]
        dproj_ref, dcw_ref, dal_ref, ddt_ref, *tails = rest[n_o:]
        i = pl.program_id(0)
        _zero_first([dcw_ref, dal_ref, ddt_ref] + tails)
        xs = _conv_taps(cur_ref, prev_ref, i == nb - 1)
        _, vjp = jax.vjp(_f_dn_pre, *xs, ba_ref[...], cw_ref[...], al_ref[...], dt_ref[...])
        *dxs, dba, dcw, dal, ddt = vjp((dq_ref[...], dk_ref[...], dv_ref[...], dbb_ref[...], dgb_ref[...]))
        total = dxs[CONV - 1]
        for j, t in enumerate(tails):
            n = CONV - 1 - j
            total = total + _rows_up(dxs[j], n, t[...])
            t[...] = dxs[j][0:HALO, :]
        dproj_ref[...] = jnp.concatenate(
            [total.astype(BF16)] + [r[...] for r in o_refs] + [dba.astype(BF16), jnp.zeros((tm, PW - C_BA - 128), BF16)],
            axis=1)
        dcw_ref[...] += dcw
        dal_ref[...] += dal
        ddt_ref[...] += ddt

    o_specs = [pl.BlockSpec((tm, t.shape[1]), lambda i: (nb - 1 - i, 0)) for t in others]
    return _pc(kern, "dn_pre_bwd", (nb,), [cur, prev, ba] + full + [row] * 5 + o_specs,
               [pl.BlockSpec((tm, PW), lambda i: (nb - 1 - i, 0))] + full,
               [SDS((s_len, PW), BF16), SDS((CONV, QKVW), F32), SDS((1, DN_H), F32), SDS((1, DN_H), F32)],
               scratch=[pltpu.VMEM((HALO, QKVW), F32)] * (CONV - 1))(proj, proj, proj, conv_w, alog, dtb, *cots, *others)


def _w_in_to_padded(w_sh):
    tr = 256

    def kern(w_ref, o_ref):
        full = jnp.concatenate([w_ref[s] for s in range(N_CHIPS)], axis=1)
        pieces = [full[:, o0:o0 + w] for o0, w, _ in sorted(_ORIG_PIECES, key=lambda t: t[2])]
        o_ref[...] = jnp.concatenate(pieces + [jnp.zeros((tr, PW - D_IN), w_ref.dtype)], axis=1)

    return _pc(kern, "w_in_to_padded", (D // tr,), [pl.BlockSpec((N_CHIPS, tr, D_IN // N_CHIPS), lambda i: (0, i, 0))],
               pl.BlockSpec((tr, PW), lambda i: (i, 0)), SDS((D, PW), w_sh.dtype))(w_sh)


def _padded_to_w_in(g):
    tr = 256
    csh = D_IN // N_CHIPS

    def kern(g_ref, o_ref):
        x = g_ref[...]
        full = jnp.concatenate([x[:, p0:p0 + w] for _, w, p0 in _ORIG_PIECES], axis=1)
        for s in range(N_CHIPS):
            o_ref[s] = full[:, s * csh:(s + 1) * csh]

    return _pc(kern, "padded_to_w_in", (D // tr,), [pl.BlockSpec((tr, PW), lambda i: (i, 0))],
               pl.BlockSpec((N_CHIPS, tr, csh), lambda i: (0, i, 0)), SDS((N_CHIPS, D, csh), g.dtype))(g)


def _local_step(x, target, wts):
    s_len = x.shape[0]
    tm = min(512, s_len)
    w_in_p = wts["w_in_p"]
    attn_gain = wts["attn_norm"]
    ffn_gain = wts["ffn_norm"]
    conv_w = wts["dn_conv"]
    alog, dtb, out_gain = wts["dn_a_log"], wts["dn_dt_bias"], wts["dn_out_norm"]
    qg, kg = wts["swa_q_norm"], wts["swa_k_norm"]
    sinks = wts["swa_sinks"].reshape(SWA_KV, 1, SWA_G)

    h, proj = _in_proj(x, attn_gain, w_in_p)
    q_dn, k_dn, v_dn, bb, gb = _dn_pre_fwd(proj, conv_w, alog, dtb)
    o_dn, s_all, t_all = _dn_chunks_fwd(q_dn, k_dn, v_dn, gb, bb)
    post_ins = [_whole(o_dn), (proj, DNW, C_Z // DNW)]
    (y_dn,) = _rows(lambda r, f: ([_f_dn_post(r[0], r[1], f[0])], []), "dn_post_fwd", s_len, tm, post_ins,
                    [out_gain], [(DNW, BF16)])

    bias = _bias_expand(wts["rel_bias"].T).reshape(SWA_H, BLK, 2 * BLK)
    y_swa = _swa_fwd(proj, bias, qg, kg, sinks)

    wts = {**wts, **wts["late"](y_swa)}
    p_a, p_b, merged = _branch_merge(y_dn, y_swa, wts["wa"], wts["wb"], proj)
    x1, h2 = _out_proj(merged, wts["w_out"], x, ffn_gain)
    gt, up, act = _ffn_up(h2, wts["wg"], wts["wu"])
    dy, dy_b, loss = _ffn_down_loss(act, wts["wd"], x1, target)

    grads = {}
    d_gt, d_up = _ffn_dact(dy_b, wts["wd"], gt, up)
    (grads["w_down"],) = _gw_ffn([act], dy_b, "gw_down")
    grads["w_gate"], grads["w_up"] = _gw_ffn([d_gt, d_up], h2, "gw_gate_up")
    token = wts["send_ffn"](grads)
    dx1, dx1_b, grads["ffn_norm"] = _ffn_dh2(d_gt, d_up, wts["wg"], wts["wu"], x1, dy,
                                             ffn_gain + token[0:1, 0:1])
    grads["w_out"] = _mm(merged, dx1_b, "tn", BF16, 512, 512, "gw_out")
    d_pa, d_pb, d_gr = _merge_bwd(dx1_b, wts["w_out"], p_a, p_b, proj)
    d_ydn, d_yswa = _d_branch(d_pa, d_pb, wts["wa"], wts["wb"])
    grads["w_branch_dn"], grads["w_branch_swa"] = _gw_branch(y_dn, y_swa, d_pa, d_pb)
    token = wts["send_early"](grads)
    qg_t = qg + token[0:1, 0:1]
    out_gain_t = out_gain + token[0:1, 0:1]

    d_sq, d_sk, d_sv, d_bias, grads["swa_q_norm"], grads["swa_k_norm"], d_sinks = _swa_bwd(
        proj, bias, qg_t, kg, sinks, d_yswa)
    grads["swa_sinks"] = d_sinks.reshape(1, SWA_H)
    grads["rel_bias"] = _bias_reduce(d_bias.reshape(SWA_H, BLK * 2 * BLK)).T

    def post_bwd(r, f):
        _, vjp = jax.vjp(_f_dn_post, r[0], r[1], f[0])
        d_o, d_z, d_gain = vjp(r[2])
        return [d_o, d_z], [d_gain]

    d_o, d_z, grads["dn_out_norm"] = _rows(post_bwd, "dn_post_bwd", s_len, tm, post_ins + [_whole(d_ydn)], [out_gain_t],
                                           [(DNW, F32), (DNW, BF16)], [(1, DH)])
    d_q, d_k, d_v, d_gb, d_bb = _dn_chunks_bwd(q_dn, k_dn, v_dn, gb, bb, s_all, t_all, d_o)

    d_proj, grads["dn_conv"], grads["dn_a_log"], grads["dn_dt_bias"] = _dn_pre_bwd(
        proj, conv_w, alog, dtb, (d_q, d_k, d_v, d_bb, d_gb), (d_z, d_gr, d_sq, d_sk, d_sv))
    grads["w_in_p"] = _mm(h, d_proj, "tn", BF16, 512, 1024, "gw_in")
    token = wts["send_in"](grads["w_in_p"])
    grad_x, grads["attn_norm"] = _dh_rms(d_proj, w_in_p, x, dx1, attn_gain + token[0:1, 0:1])
    return loss, grad_x, grads


_HBM = pl.BlockSpec(memory_space=pl.ANY)


def _place():
    return lax.axis_index("x"), lax.axis_index("y"), lax.axis_index("c")


def _other_chips(x, y):
    return [(1 - x, y), (x, 1 - y), (1 - x, 1 - y)]


def _rcopy(src, dst, send_sems, recv_sems, k, to):
    return pltpu.make_async_remote_copy(src_ref=src, dst_ref=dst, send_sem=send_sems.at[k], recv_sem=recv_sems.at[k],
                                        device_id=to, device_id_type=MESH)


def _comm_call(body, name, ins, out_shapes, n_remote, landing=0):
    first = len(ins) - landing
    return pl.pallas_call(
        body, name=name, in_specs=[_HBM] * len(ins), out_specs=[_HBM] * len(out_shapes), out_shape=out_shapes,
        scratch_shapes=[pltpu.SemaphoreType.DMA((n_remote,)), pltpu.SemaphoreType.DMA((n_remote,))],
        input_output_aliases={first + i: i for i in range(landing)},
        compiler_params=_cparams(has_side_effects=True),
    )(*ins)


def _own_slot(blocks, chip):
    return [lax.dynamic_update_slice(lax.empty((N_CHIPS,) + b.shape, b.dtype), b[None], (chip, 0, 0)) for b in blocks]


def _gather_weights(ws, chip):
    n = len(ws)
    halves = [w.shape[0] // 2 for w in ws]

    def body(*refs):
        w_refs, o_refs = refs[:n], refs[2 * n:3 * n]
        send_sems, recv_sems = refs[3 * n:]
        x, y, c = _place()
        s = 2 * x + y
        sib = (x, y, 1 - c)
        chips = _other_chips(x, y)

        def rows(i, half):
            return pl.ds(half * halves[i], halves[i])

        first = []
        for j, (cx, cy) in enumerate(chips):
            for i in range(n):
                cp = _rcopy(w_refs[i].at[rows(i, c), :], o_refs[i].at[s, rows(i, c), :], send_sems, recv_sems,
                            j * n + i, (cx, cy, c))
                cp.start()
                first.append(cp)
        passed = []
        for j, (cx, cy) in enumerate(chips):
            sj = 2 * cx + cy
            for i in range(n):
                blk = o_refs[i].at[sj, rows(i, c), :]
                _rcopy(blk, blk, send_sems, recv_sems, j * n + i, (cx, cy, c)).wait_recv()
                cp = _rcopy(blk, blk, send_sems, recv_sems, (3 + j) * n + i, sib)
                cp.start()
                passed.append(cp)
        for j, (cx, cy) in enumerate(chips):
            sj = 2 * cx + cy
            for i in range(n):
                blk = o_refs[i].at[sj, rows(i, 1 - c), :]
                _rcopy(blk, blk, send_sems, recv_sems, (3 + j) * n + i, sib).wait_recv()
        for cp in first + passed:
            cp.wait_send()

    return _comm_call(body, "gather_weights", list(ws) + _own_slot(ws, chip),
                      [SDS((N_CHIPS,) + w.shape, w.dtype) for w in ws], 6 * n, landing=n)


_HBM_ONLY = pl.BlockSpec(memory_space=pltpu.HBM)
_SEM = pl.BlockSpec(memory_space=pltpu.SEMAPHORE)
_DATAFLOW = pltpu.SideEffectType.DATAFLOW_SIDE_EFFECTING


def _in_hbm(a):
    return pltpu.with_memory_space_constraint(a, pltpu.HBM)


def _gather_windows(blocks):
    halves = [b.shape[0] // 2 for b in blocks]

    def src_at(ref, i, c, sj):
        return ref.at[pl.ds(c * halves[i], halves[i]), :]

    def dst_at(ref, i, c, s_from):
        return ref.at[s_from, pl.ds(c * halves[i], halves[i]), :]

    return src_at, dst_at


def _exchange_windows():
    return (lambda ref, i, c, sj: ref.at[sj]), (lambda ref, i, c, s_from: ref.at[s_from])


def _swap_windows(gs):
    halves = [g.shape[1] // 2 for g in gs]
    return ((lambda ref, i, c, tag: ref.at[:, pl.ds((1 - c) * halves[i], halves[i]), :]),
            (lambda ref, i, c, slot: ref))


def _chip_peers(x, y, c):
    return [(2 * cx + cy, (cx, cy, c), 2 * x + y, 2 * cx + cy) for cx, cy in _other_chips(x, y)]


def _sibling_peer(x, y, c):
    return [(0, (x, y, 1 - c), 0, 0)]


def _split_start(name, ws, lands, dep, windows, peers=_chip_peers, n_peers=3):
    n = len(ws)
    src_at, dst_at = windows

    def body(*refs):
        w_refs, l_refs = refs[:n], refs[n:2 * n]
        send_sems, recv_sems = refs[2 * n + 1], refs[2 * n + 2]
        token = refs[-1]
        x, y, c = _place()
        for j, (tag, dev, there, _) in enumerate(peers(x, y, c)):
            for i in range(n):
                _rcopy(src_at(w_refs[i], i, c, tag), dst_at(l_refs[i], i, c, there), send_sems, recv_sems,
                       j * n + i, dev).start()
        token[...] = jnp.zeros_like(token)

    outs = pl.pallas_call(
        body, name=name,
        out_shape=(pltpu.SemaphoreType.DMA((n_peers * n,)), pltpu.SemaphoreType.DMA((n_peers * n,)),
                   *[pltpu.HBM(w.shape, w.dtype) for w in ws], *[pltpu.HBM(t.shape, t.dtype) for t in lands],
                   SDS((8, 128), F32)),
        in_specs=[_HBM_ONLY] * (2 * n) + [pl.BlockSpec(memory_space=pl.ANY)],
        out_specs=(_SEM, _SEM, *[_HBM_ONLY] * (2 * n), pl.BlockSpec(memory_space=pltpu.VMEM)),
        input_output_aliases={i: 2 + i for i in range(2 * n)},
        compiler_params=pltpu.CompilerParams(has_side_effects=_DATAFLOW),
    )(*[_in_hbm(w) for w in ws], *[_in_hbm(t) for t in lands], dep)
    return outs[0], outs[1], outs[2:2 + n], outs[2 + n:2 + 2 * n], outs[-1]


def _split_wait(name, w_thru, l_thru, send_sems, recv_sems, after, windows, peers=_chip_peers, with_sources=False):
    n = len(w_thru)
    src_at, dst_at = windows

    def body(*refs):
        w_refs, l_refs = refs[:n], refs[n:2 * n]
        send_sems, recv_sems = refs[2 * n], refs[2 * n + 1]
        x, y, c = _place()
        for j, (tag, dev, _, here) in enumerate(peers(x, y, c)):
            for i in range(n):
                cp = _rcopy(src_at(w_refs[i], i, c, tag), dst_at(l_refs[i], i, c, here), send_sems, recv_sems,
                            j * n + i, dev)
                cp.wait_send()
                cp.wait_recv()

    outs = pl.pallas_call(
        body, name=name,
        out_shape=[pltpu.HBM(w.shape, w.dtype) for w in w_thru] + [pltpu.HBM(t.shape, t.dtype) for t in l_thru],
        in_specs=[_HBM_ONLY] * (2 * n) + [_SEM, _SEM, pl.BlockSpec(memory_space=pl.ANY)],
        out_specs=[_HBM_ONLY] * (2 * n),
        input_output_aliases={i: i for i in range(2 * n)},
        compiler_params=pltpu.CompilerParams(has_side_effects=_DATAFLOW),
    )(*w_thru, *l_thru, send_sems, recv_sems, after)
    return (outs[:n], outs[n:]) if with_sources else outs[n:]


def _sibling_fill(lands):
    n = len(lands)
    halves = [t.shape[1] // 2 for t in lands]

    def body(*refs):
        o_refs = refs[n:2 * n]
        send_sems, recv_sems = refs[2 * n:]
        x, y, c = _place()
        sib = (x, y, 1 - c)
        chips = _other_chips(x, y)
        sent = []
        for j, (cx, cy) in enumerate(chips):
            for i in range(n):
                blk = o_refs[i].at[2 * cx + cy, pl.ds(c * halves[i], halves[i]), :]
                cp = _rcopy(blk, blk, send_sems, recv_sems, j * n + i, sib)
                cp.start()
                sent.append(cp)
        for j, (cx, cy) in enumerate(chips):
            for i in range(n):
                blk = o_refs[i].at[2 * cx + cy, pl.ds((1 - c) * halves[i], halves[i]), :]
                _rcopy(blk, blk, send_sems, recv_sems, j * n + i, sib).wait_recv()
        for cp in sent:
            cp.wait_send()

    return _comm_call(body, "sibling_fill", list(lands), [SDS(t.shape, t.dtype) for t in lands], 3 * n, landing=n)


def _swap_halves(gs, name):
    n = len(gs)
    halves = [g.shape[1] // 2 for g in gs]

    def body(*refs):
        g_refs, o_refs = refs[:n], refs[n:2 * n]
        send_sems, recv_sems = refs[2 * n:]
        x, y, c = _place()
        cps = [_rcopy(g_refs[i].at[:, pl.ds((1 - c) * halves[i], halves[i]), :], o_refs[i], send_sems, recv_sems, i,
                      (x, y, 1 - c)) for i in range(n)]
        for cp in cps:
            cp.start()
        for cp in cps:
            cp.wait()

    return _comm_call(body, name, gs, [SDS((N_CHIPS, h, g.shape[2]), g.dtype) for g, h in zip(gs, halves)], n)


def _swap_reduced(rs, name):
    n = len(rs)

    def body(*refs):
        r_refs, o_refs = refs[:n], refs[n:2 * n]
        send_sems, recv_sems = refs[2 * n:]
        x, y, c = _place()
        cps = [_rcopy(r_refs[i], o_refs[i], send_sems, recv_sems, i, (x, y, 1 - c)) for i in range(n)]
        for cp in cps:
            cp.start()
        for cp in cps:
            cp.wait()

    return _comm_call(body, name, rs, [SDS(r.shape, r.dtype) for r in rs], n)


def _all_sum_small(vec, name):
    n_dev = 8
    flips = [(bx, by, bc) for bx in (0, 1) for by in (0, 1) for bc in (0, 1)][1:]

    def body(v_ref, out_ref, gath, send_sems, recv_sems):
        x, y, c = _place()
        me = 4 * x + 2 * y + c
        gath[me] = v_ref[...]
        sent = []
        for k, (bx, by, bc) in enumerate(flips):
            peer = (x ^ bx, y ^ by, c ^ bc)
            cp = _rcopy(v_ref, gath.at[me], send_sems, recv_sems, k, peer)
            cp.start()
            sent.append(cp)
        for k, (bx, by, bc) in enumerate(flips):
            peer = (x ^ bx, y ^ by, c ^ bc)
            _rcopy(v_ref, gath.at[4 * peer[0] + 2 * peer[1] + peer[2]], send_sems, recv_sems, k, peer).wait_recv()
        for cp in sent:
            cp.wait_send()
        acc = gath[0]
        for d in range(1, n_dev):
            acc = acc + gath[d]
        out_ref[...] = acc

    vm = pl.BlockSpec(memory_space=pltpu.VMEM)
    return pl.pallas_call(
        body, name=name, in_specs=[vm], out_specs=vm, out_shape=SDS(vec.shape, F32),
        scratch_shapes=[pltpu.VMEM((n_dev,) + vec.shape, F32), pltpu.SemaphoreType.DMA((7,)),
                        pltpu.SemaphoreType.DMA((7,))],
        compiler_params=_cparams(has_side_effects=True),
    )(vec)


def _pack_small(vals, extra=None):
    parts = [vals[n].reshape(-1).astype(F32) for n, _ in _SMALL]
    parts.append(jnp.zeros((1,), F32) if extra is None else extra.reshape(1).astype(F32))
    flat = jnp.concatenate(parts)
    flat = jnp.concatenate([flat, jnp.zeros((_SMALL_ROWS * 128 - flat.shape[0],), F32)])
    return flat.reshape(_SMALL_ROWS, 128)


def _unpack_small(packed, shapes):
    flat = packed.reshape(-1)
    return {n: flat[_SMALL_OFF[n][0]:_SMALL_OFF[n][0] + _SMALL_OFF[n][1]].reshape(shapes[n]) for n, _ in _SMALL}


def _pair_sum(gs, gots, core, name):
    n = len(gs)

    def kern(c_ref, *refs):
        for i in range(n):
            refs[2 * n + i][...] = (refs[i][...].astype(F32) + refs[n + i][...].astype(F32)).astype(BF16)

    in_specs = [pl.BlockSpec((1, t.shape[1], t.shape[2]), lambda s, c_ref: (s, c_ref[0], 0)) for t in gots]
    in_specs += [pl.BlockSpec((1, t.shape[1], t.shape[2]), lambda s, c_ref: (s, 0, 0)) for t in gots]
    out_specs = [pl.BlockSpec((1, t.shape[1], t.shape[2]), lambda s, c_ref: (s, 0, 0)) for t in gots]
    return pl.pallas_call(
        kern, name=name,
        grid_spec=pltpu.PrefetchScalarGridSpec(num_scalar_prefetch=1, grid=(N_CHIPS,), in_specs=in_specs,
                                               out_specs=out_specs),
        out_shape=[SDS(t.shape, BF16) for t in gots],
        compiler_params=_cparams(dimension_semantics=("arbitrary",)),
    )(core.reshape(1).astype(jnp.int32), *gs, *gots)


def _chip_sum(qs, name):
    n = len(qs)

    def kern(*refs):
        for i in range(n):
            acc = refs[i][0].astype(F32)
            for s in range(1, N_CHIPS):
                acc = acc + refs[i][s].astype(F32)
            refs[n + i][...] = acc

    in_specs = [pl.BlockSpec((N_CHIPS, q.shape[1] // 2, q.shape[2]), lambda j: (0, j, 0)) for q in qs]
    out_specs = [pl.BlockSpec((q.shape[1] // 2, q.shape[2]), lambda j: (j, 0)) for q in qs]
    return _pc(kern, name, (2,), in_specs, out_specs, [SDS(q.shape[1:], F32) for q in qs])(*qs)


def _adam_math(w_, g_, m_, v_):
    m_ = ADAM_B1 * m_ + (1.0 - ADAM_B1) * g_
    v_ = ADAM_B2 * v_ + (1.0 - ADAM_B2) * jnp.square(g_)
    m_hat = m_ / (1.0 - ADAM_B1 ** ADAM_STEP)
    v_hat = v_ / (1.0 - ADAM_B2 ** ADAM_STEP)
    return -ADAM_LR * (m_hat / (jnp.sqrt(v_hat) + ADAM_EPS) + ADAM_WD * w_), m_, v_


def _adamw(w, g, m, v, name):
    rows, cols = w.shape
    tr = rows
    for cand in (256, 128, 64, 32, 16, 8):
        if rows % cand == 0 and rows > cand:
            tr = cand
            break

    def kern(w_ref, g_ref, m_ref, v_ref, d_ref, nm_ref, nv_ref):
        d_ref[...], nm_ref[...], nv_ref[...] = _adam_math(w_ref[...], g_ref[...], m_ref[...], v_ref[...])

    spec = pl.BlockSpec((tr, cols), lambda i: (i, 0))
    return _pc(kern, name, (rows // tr,), [spec] * 4, [spec] * 3, [SDS(w.shape, F32)] * 3)(w, g, m, v)


def _adamw_rows1(w, g, m, v, name):
    rows, _, cols = w.shape
    tr = next(t for t in (203, 174, 128, 64, 42, 32, 29, 16, 8, 7, 6, 4, 3, 2, 1) if rows % t == 0)

    def kern(w_ref, g_ref, m_ref, v_ref, go_ref, d_ref, nm_ref, nv_ref):
        g_ = g_ref[...]
        go_ref[...] = g_
        d_ref[...], nm_ref[...], nv_ref[...] = _adam_math(w_ref[...], g_, m_ref[...], v_ref[...])

    spec = pl.BlockSpec((tr, 1, cols), lambda i: (i, 0, 0))
    return _pc(kern, name, (rows // tr,), [spec] * 4, [spec] * 4, [SDS(w.shape, F32)] * 4)(w, g, m, v)


def _adamw_big(w, mine, theirs, m, v, core, name):
    _, rows, cols = w.shape
    half = rows // 2
    tr = next(t for t in (256, 176, 128, 64, 32, 16, 8) if half % t == 0)
    nbh = half // tr

    def kern(c_ref, w_ref, a_ref, b_ref, m_ref, v_ref, g_ref, d_ref, nm_ref, nv_ref):
        g_ = jnp.where(pl.program_id(0) // nbh == c_ref[0], a_ref[...], b_ref[...])
        g_ref[0] = g_
        d_ref[0], nm_ref[0], nv_ref[0] = _adam_math(w_ref[0], g_, m_ref[0], v_ref[0])

    full = pl.BlockSpec((1, tr, cols), lambda i, c_ref: (0, i, 0))
    part = pl.BlockSpec((tr, cols), lambda i, c_ref: (i % nbh, 0))
    return pl.pallas_call(
        kern, name=name,
        grid_spec=pltpu.PrefetchScalarGridSpec(num_scalar_prefetch=1, grid=(rows // tr,),
                                               in_specs=[full, part, part, full, full], out_specs=[full] * 4),
        out_shape=[SDS(w.shape, F32)] * 4,
        compiler_params=_cparams(dimension_semantics=("arbitrary",)),
    )(core.reshape(1).astype(jnp.int32), w, mine, theirs, m, v)


_WEIGHT_NAMES = ("attn_norm", "w_in", "dn_conv", "dn_a_log", "dn_dt_bias", "dn_out_norm", "swa_q_norm", "swa_k_norm",
                 "swa_sinks", "rel_bias", "w_branch_dn", "w_branch_swa", "w_out", "ffn_norm", "w_gate", "w_up",
                 "w_down")
_CONV_SH = QKVW // N_CHIPS


def kernel(x, attn_norm, w_in, dn_conv, dn_a_log, dn_dt_bias, dn_out_norm, swa_q_norm, swa_k_norm, swa_sinks, rel_bias, w_branch_dn, w_branch_swa, w_out, ffn_norm, w_gate, w_up, w_down, loss_target, m_attn_norm, m_w_in, m_dn_conv, m_dn_a_log, m_dn_dt_bias, m_dn_out_norm, m_swa_q_norm, m_swa_k_norm, m_swa_sinks, m_rel_bias, m_w_branch_dn, m_w_branch_swa, m_w_out, m_ffn_norm, m_w_gate, m_w_up, m_w_down, v_attn_norm, v_w_in, v_dn_conv, v_dn_a_log, v_dn_dt_bias, v_dn_out_norm, v_swa_q_norm, v_swa_k_norm, v_swa_sinks, v_rel_bias, v_w_branch_dn, v_w_branch_swa, v_w_out, v_ffn_norm, v_w_gate, v_w_up, v_w_down):
    w = dict(attn_norm=attn_norm, w_in=w_in, dn_conv=dn_conv, dn_a_log=dn_a_log, dn_dt_bias=dn_dt_bias,
             dn_out_norm=dn_out_norm, swa_q_norm=swa_q_norm, swa_k_norm=swa_k_norm, swa_sinks=swa_sinks,
             rel_bias=rel_bias, w_branch_dn=w_branch_dn, w_branch_swa=w_branch_swa, w_out=w_out, ffn_norm=ffn_norm,
             w_gate=w_gate, w_up=w_up, w_down=w_down)
    m = dict(attn_norm=m_attn_norm, w_in=m_w_in, dn_conv=m_dn_conv, dn_a_log=m_dn_a_log, dn_dt_bias=m_dn_dt_bias,
             dn_out_norm=m_dn_out_norm, swa_q_norm=m_swa_q_norm, swa_k_norm=m_swa_k_norm, swa_sinks=m_swa_sinks,
             rel_bias=m_rel_bias, w_branch_dn=m_w_branch_dn, w_branch_swa=m_w_branch_swa, w_out=m_w_out,
             ffn_norm=m_ffn_norm, w_gate=m_w_gate, w_up=m_w_up, w_down=m_w_down)
    v = dict(attn_norm=v_attn_norm, w_in=v_w_in, dn_conv=v_dn_conv, dn_a_log=v_dn_a_log, dn_dt_bias=v_dn_dt_bias,
             dn_out_norm=v_dn_out_norm, swa_q_norm=v_swa_q_norm, swa_k_norm=v_swa_k_norm, swa_sinks=v_swa_sinks,
             rel_bias=v_rel_bias, w_branch_dn=v_w_branch_dn, w_branch_swa=v_w_branch_swa, w_out=v_w_out,
             ffn_norm=v_ffn_norm, w_gate=v_w_gate, w_up=v_w_up, w_down=v_w_down)
    shapes = {n: w[n].shape for n in _WEIGHT_NAMES}

    def two_d(a):
        return a.reshape(a.shape[-2], a.shape[-1]) if a.ndim == 3 else a

    core = lax.axis_index("c")
    chip = 2 * lax.axis_index("x") + lax.axis_index("y")
    small_shapes = {n: two_d(w[n]).shape for n, _ in _SMALL}
    small_shapes["dn_conv"] = (CONV, QKVW)

    conv_loc = two_d(w["dn_conv"])
    conv_part = lax.dynamic_update_slice(jnp.zeros((CONV, QKVW), F32), jnp.where(core == 0, conv_loc, 0.0),
                                         (0, chip * _CONV_SH))
    conv_full = _all_sum_small(conv_part.reshape(CONV * QKVW // 128, 128), "gather_conv").reshape(CONV, QKVW)

    flipped = ("w_gate", "w_up")

    def natural(a, n):
        return a.transpose(0, 2, 1) if n in flipped else a

    w_bf = [two_d(natural(w[n], n).astype(BF16)) for n in _BIG_NAMES]
    (w_in_g,) = _gather_weights(w_bf[:1], chip)
    windows = _gather_windows(w_bf[1:])
    after_sync = w_in_g[0, :8, :128].astype(F32) + conv_full[0:1, :128]
    send_sems, recv_sems, w_thru, l_thru, token = _split_start(
        "gather_start", w_bf[1:], _own_slot(w_bf[1:], chip), after_sync, windows)

    def late(after):
        lands = _split_wait("gather_wait", w_thru, l_thru, send_sems, recv_sems, after, windows)
        g = dict(zip(_BIG_NAMES[1:], _sibling_fill(lands)))
        return dict(wa=g["w_branch_dn"], wb=g["w_branch_swa"], w_out=g["w_out"].reshape(D, D), wg=g["w_gate"],
                    wu=g["w_up"], wd=g["w_down"])

    wts = dict(w_in_p=_w_in_to_padded(w_in_g), dn_conv=conv_full, late=late)
    for n, _ in _SMALL[:-1]:
        wts[n] = two_d(w[n])
    wts["attn_norm"] = wts["attn_norm"] + token[0:1, 0:1]

    early = {}

    ffn = {}

    def send_ffn(grads):
        gs = [grads["w_gate"], grads["w_up"], grads["w_down"]]
        lands = [lax.empty((N_CHIPS, g.shape[1] // 2, g.shape[2]), g.dtype) for g in gs]
        ffn["sems"], ffn["recv"], ffn["src"], ffn["land"], tok = _split_start(
            "swap_ffn_start", gs, lands, gs[0][0, :8, :128], _swap_windows(gs), _sibling_peer, 1)
        return tok

    def send_early(grads):
        small = [grads["w_branch_dn"], grads["w_branch_swa"], grads["w_out"].reshape(N_CHIPS, CSH, D)]
        big = [grads["w_gate"], grads["w_up"], grads["w_down"]]
        big, got_big = _split_wait("swap_ffn_wait", ffn["src"], ffn["land"], ffn["sems"], ffn["recv"], small[0],
                                   _swap_windows(big), _sibling_peer, with_sources=True)
        gots = list(_swap_halves(small, "swap_halves_early")) + list(got_big)
        parts = _pair_sum(small + list(big), gots, core, "pair_sum_early")
        own = [lax.dynamic_index_in_dim(p, chip, axis=0, keepdims=False) for p in parts]
        early["sems"], early["recv"], early["src"], early["land"], tok = _split_start(
            "exchange_start", parts, _own_slot(own, chip), parts[0][0, :8, :128], _exchange_windows())
        return tok

    last = {}

    def send_in(g_in_p):
        g_in = [_padded_to_w_in(g_in_p)]
        parts = _pair_sum(g_in, _swap_halves(g_in, "swap_halves_in"), core, "pair_sum_in")
        own = [lax.dynamic_index_in_dim(p, chip, axis=0, keepdims=False) for p in parts]
        last["sems"], last["recv"], last["src"], last["land"], tok = _split_start(
            "exchange_in_start", parts, _own_slot(own, chip), parts[0][0, :8, :128], _exchange_windows())
        return tok

    wts["send_ffn"] = send_ffn
    wts["send_early"] = send_early
    wts["send_in"] = send_in
    loss_sum, grad_x, grads = _local_step(x[0], loss_target[0], wts)

    small_sum = _all_sum_small(_pack_small(grads, loss_sum), "all_sum_small")
    loss = small_sum.reshape(-1)[_LOSS_OFF]
    g_small = _unpack_small(small_sum, small_shapes)

    q_early = _split_wait("exchange_wait", early["src"], early["land"], early["sems"], early["recv"], small_sum,
                          _exchange_windows())
    red_early = _chip_sum(list(q_early), "chip_sum_early")
    their_early = _swap_reduced(red_early, "swap_reduced_early")
    g_out, d_out, m_out, v_out = {}, {}, {}, {}
    for n, mine, other in zip(_BIG_NAMES[1:], red_early, their_early):
        res = _adamw_big(natural(w[n], n), mine, other, natural(m[n], n), natural(v[n], n), core, "adamw_" + n)
        g_out[n], d_out[n], m_out[n], v_out[n] = (natural(t, n) for t in res)

    q_in = _split_wait("exchange_in_wait", last["src"], last["land"], last["sems"], last["recv"],
                       d_out[_BIG_NAMES[-1]], _exchange_windows())
    reduced = _chip_sum(list(q_in), "chip_sum_in")
    theirs = _swap_reduced(reduced, "swap_reduced_in")

    def rows1(a):
        return a.transpose(2, 0, 1)

    def unrows1(a):
        return a.transpose(1, 2, 0)

    g_in_blk = jnp.concatenate([jnp.where(core == 0, reduced[0], theirs[0]),
                                jnp.where(core == 0, theirs[0], reduced[0])], axis=0)
    g_in_r = rows1(g_in_blk[None])
    res = _adamw_rows1(rows1(w["w_in"]), g_in_r, rows1(m["w_in"]), rows1(v["w_in"]), "adamw_w_in")
    g_out["w_in"], d_out["w_in"], m_out["w_in"], v_out["w_in"] = (unrows1(t) for t in res)
    g_conv = lax.dynamic_slice(g_small["dn_conv"], (0, chip * _CONV_SH), (CONV, _CONV_SH))
    g_out["dn_conv"] = g_conv.reshape(shapes["dn_conv"])
    d_, m_, v_ = _adamw(conv_loc, g_conv, two_d(m["dn_conv"]), two_d(v["dn_conv"]), "adamw_dn_conv")
    d_out["dn_conv"], m_out["dn_conv"], v_out["dn_conv"] = (t.reshape(shapes["dn_conv"]) for t in (d_, m_, v_))

    def packed(src):
        vals = {n: src[n] for n, _ in _SMALL[:-1]}
        vals["dn_conv"] = jnp.zeros((CONV * QKVW,), F32)
        return _pack_small(vals)

    d_s, m_s, v_s = _adamw(packed(w), small_sum, packed(m), packed(v), "adamw_small")
    d_small, m_small, v_small = (_unpack_small(t, small_shapes) for t in (d_s, m_s, v_s))
    for n, _ in _SMALL[:-1]:
        g_out[n] = g_small[n].reshape(shapes[n])
        d_out[n], m_out[n], v_out[n] = (t[n].reshape(shapes[n]) for t in (d_small, m_small, v_small))

    return (loss, grad_x[None], *[g_out[n] for n in _WEIGHT_NAMES], *[d_out[n] for n in _WEIGHT_NAMES],
            *[m_out[n] for n in _WEIGHT_NAMES], *[v_out[n] for n in _WEIGHT_NAMES])
```

```python
import functools
import math

import numpy as np
import jax
import jax.numpy as jnp
from jax import lax
from jax.experimental import pallas as pl
from jax.experimental.pallas import tpu as pltpu

F32 = jnp.float32
BF16 = jnp.bfloat16
SDS = jax.ShapeDtypeStruct

D = 1024
DN_H = 4
DH = 128
DNW = DN_H * DH
QKVW = 3 * DNW
CONV = 4
CHUNK = 64
SWA_H = 8
SWA_KV = 2
SWA_G = SWA_H // SWA_KV
SWA_D = 64
SWAW = SWA_H * SWA_D
SWAKW = SWA_KV * SWA_D
BLK = 128
NBUCKET = 32
MAXDIST = 128
DFF = 2816
D_IN = QKVW + DNW + 2 * DN_H + SWAW + 2 * SWAKW + 2 * D
EPS = 1e-6
NEG = -1e30

ADAM_LR = 0.001
ADAM_B1 = 0.9
ADAM_B2 = 0.999
ADAM_EPS = 1e-08
ADAM_WD = 0.01
ADAM_STEP = 10

C_QKV, C_Z, C_GATE, C_SQ, C_SK, C_SV, C_BA = 0, 1536, 2048, 4096, 4608, 4736, 4864
PW = 5120
_ORIG_PIECES = (
    (0, QKVW, C_QKV),
    (QKVW, DNW, C_Z),
    (QKVW + DNW, 2 * DN_H, C_BA),
    (QKVW + DNW + 2 * DN_H, SWAW, C_SQ),
    (QKVW + DNW + 2 * DN_H + SWAW, SWAKW, C_SK),
    (QKVW + DNW + 2 * DN_H + SWAW + SWAKW, SWAKW, C_SV),
    (QKVW + DNW + 2 * DN_H + SWAW + 2 * SWAKW, 2 * D, C_GATE),
)

N_CHIPS = 4
FSH = DFF // N_CHIPS
CSH = D // N_CHIPS
VMEM_LIMIT = 48 * 1024 * 1024
MESH = pl.DeviceIdType.MESH

_BIG = (
    ("w_in", D, D_IN // N_CHIPS),
    ("w_branch_dn", DNW, CSH),
    ("w_branch_swa", SWAW, CSH),
    ("w_out", CSH, D),
    ("w_gate", FSH, D),
    ("w_up", FSH, D),
    ("w_down", FSH, D),
)
_BIG_NAMES = tuple(n for n, _, _ in _BIG)

_SMALL = (
    ("attn_norm", D), ("ffn_norm", D), ("dn_out_norm", DH), ("swa_q_norm", SWA_D), ("swa_k_norm", SWA_D),
    ("swa_sinks", SWA_H), ("dn_a_log", DN_H), ("dn_dt_bias", DN_H), ("rel_bias", NBUCKET * SWA_H),
    ("dn_conv", CONV * QKVW),
)
_SMALL_OFF = {}
_o = 0
for _n, _s in _SMALL:
    _SMALL_OFF[_n] = (_o, _s)
    _o += _s
_LOSS_OFF = _o
_SMALL_ROWS = -(-(_o + 1) // (8 * 128)) * 8


def _cparams(**kw):
    return pltpu.CompilerParams(vmem_limit_bytes=VMEM_LIMIT, **kw)


_DIMS = {
    "nn": (((1,), (0,)), ((), ())),
    "nt": (((1,), (1,)), ((), ())),
    "tn": (((0,), (0,)), ((), ())),
    "bnn": (((2,), (1,)), ((0,), (0,))),
    "bnt": (((2,), (2,)), ((0,), (0,))),
    "btn": (((1,), (1,)), ((0,), (0,))),
}


def _raw_dot(a, b, kind, exact):
    if exact:
        prec = lax.Precision.HIGH if exact == "x3" else lax.Precision.HIGHEST
        return lax.dot_general(a, b, _DIMS[kind], precision=prec, preferred_element_type=F32)
    return lax.dot_general(a.astype(BF16), b.astype(BF16), _DIMS[kind], preferred_element_type=F32)


@functools.partial(jax.custom_vjp, nondiff_argnums=(2, 3))
def _dot(a, b, kind, exact):
    return _raw_dot(a, b, kind, exact)


def _dot_fwd(a, b, kind, exact):
    return _raw_dot(a, b, kind, exact), (a, b)


def _dot_bwd(kind, exact, res, g):
    a, b = res
    pre = kind[:-2]
    nn, nt, tn = pre + "nn", pre + "nt", pre + "tn"
    if kind == nn:
        return _dot(g, b, nt, exact), _dot(a, g, tn, exact)
    if kind == nt:
        return _dot(g, b, nn, exact), _dot(g, a, tn, exact)
    return _dot(b, g, nt, exact), _dot(a, g, nn, exact)


_dot.defvjp(_dot_fwd, _dot_bwd)


def _silu(x):
    return x * jax.nn.sigmoid(x)


def _f_rms(x, gain):
    return x * lax.rsqrt(jnp.mean(x * x, axis=-1, keepdims=True) + EPS) * gain


def _f_dn_pre(xs0, xs1, xs2, xs3, ba, cw, alog, dtb):
    rows = xs0.shape[0]
    c = xs0 * cw[0:1] + xs1 * cw[1:2] + xs2 * cw[2:3] + xs3 * cw[3:4]
    qkv = _silu(c)
    qs, ks, bbs, gbs = [], [], [], []
    for h in range(DN_H):
        qh = qkv[:, h * DH:(h + 1) * DH]
        kh = qkv[:, DNW + h * DH:DNW + (h + 1) * DH]
        qs.append(qh * lax.rsqrt(jnp.sum(qh * qh, axis=-1, keepdims=True) + EPS) * (DH ** -0.5))
        ks.append(kh * lax.rsqrt(jnp.sum(kh * kh, axis=-1, keepdims=True) + EPS))
        beta = jax.nn.sigmoid(ba[:, h:h + 1])
        ar = ba[:, DN_H + h:DN_H + h + 1] + dtb[:, h:h + 1]
        softplus = jnp.maximum(ar, 0.0) + jnp.log1p(jnp.exp(-jnp.abs(ar)))
        g = -jnp.exp(alog[:, h:h + 1]) * softplus
        bbs.append(jnp.broadcast_to(beta, (rows, DH)))
        gbs.append(jnp.broadcast_to(g, (rows, DH)))
    return (jnp.concatenate(qs, axis=1), jnp.concatenate(ks, axis=1), qkv[:, 2 * DNW:],
            jnp.concatenate(bbs, axis=1), jnp.concatenate(gbs, axis=1))


def _f_dn_post(o, z, gain):
    ys = []
    for h in range(DN_H):
        oh = o[:, h * DH:(h + 1) * DH]
        zh = z[:, h * DH:(h + 1) * DH]
        ys.append(oh * lax.rsqrt(jnp.mean(oh * oh, axis=-1, keepdims=True) + EPS) * gain * _silu(zh))
    return jnp.concatenate(ys, axis=1)


def _f_merge(pa, pb, ga, gb):
    return jax.nn.sigmoid(ga) * pa + jax.nn.sigmoid(gb) * pb


@jax.custom_vjp
def _f_swiglu(g, u):
    return _silu(g) * u


def _f_swiglu_fwd(g, u):
    return _silu(g) * u, (g, u)


def _f_swiglu_bwd(res, d):
    g, u = res
    s = jax.nn.sigmoid(g)
    act = g * s
    return d * u * (s + act * (1.0 - s)), d * act


_f_swiglu.defvjp(_f_swiglu_fwd, _f_swiglu_bwd)


@jax.custom_vjp
def _unit_lower_inverse(a):
    c = a.shape[-1]
    eye = (lax.broadcasted_iota(jnp.int32, a.shape, 1) == lax.broadcasted_iota(jnp.int32, a.shape, 2)).astype(F32)
    p = -a
    t = eye + p
    for _ in range(max(c.bit_length() - 2, 0)):
        p = _raw_dot(p, p, "bnn", "x3")
        t = t + _raw_dot(t, p, "bnn", "x3")
    return t


def _unit_lower_inverse_fwd(a):
    t = _unit_lower_inverse(a)
    return t, t


def _unit_lower_inverse_bwd(t, g):
    return (-_raw_dot(_raw_dot(t, g, "btn", "x3"), t, "bnt", "x3"),)


_unit_lower_inverse.defvjp(_unit_lower_inverse_fwd, _unit_lower_inverse_bwd)


@jax.custom_vjp
def _known_inverse(a, t):
    return t


def _known_inverse_fwd(a, t):
    return t, t


def _known_inverse_bwd(t, g):
    return _unit_lower_inverse_bwd(t, g)[0], jnp.zeros_like(t)


_known_inverse.defvjp(_known_inverse_fwd, _known_inverse_bwd)


def _f_chunk(q, k, v, gb, bb, s, t_known=None, with_t=False):
    c = CHUNK
    nh = q.shape[0]
    ii = lax.broadcasted_iota(jnp.int32, (nh, c, c), 1)
    jj = lax.broadcasted_iota(jnp.int32, (nh, c, c), 2)
    incl = ii >= jj
    strict = ii > jj
    eye = (ii == jj).astype(F32)
    gcb = _dot(incl.astype(F32), gb, "bnn", "x3")
    lane0 = (lax.broadcasted_iota(jnp.int32, (nh, c, DH), 2) == 0).astype(F32)
    gcol = gcb[:, :, :c]
    grow = _dot(lane0, gcb, "bnt", "x3")
    decay = jnp.where(incl, jnp.exp(jnp.where(incl, gcol - grow, 0.0)), 0.0)
    kb = k * bb
    vb = v * bb
    a = jnp.where(strict, _dot(kb, k, "bnt", False) * decay, 0.0)
    t = _unit_lower_inverse(a) if t_known is None else _known_inverse(a, t_known)
    eg = jnp.exp(gcb)
    u = _dot(t, vb, "bnn", "x3")
    w = _dot(t, kb * eg, "bnn", "x3")
    qk = jnp.where(incl, _dot(q, k, "bnt", False) * decay, 0.0)
    qe = q * eg
    glast = gcb[:, c - 1:c, :]
    k_dec = k * jnp.exp(glast - gcb)
    e_last = jnp.exp(glast)
    outs = []
    for g in range(nh // DN_H):
        sl = slice(g * DN_H, (g + 1) * DN_H)
        v_new = u[sl] - _dot(w[sl], s, "bnn", False)
        outs.append(_dot(qe[sl], s, "bnn", False) + _dot(qk[sl], v_new, "bnn", False))
        s = s * e_last[sl] + _dot(k_dec[sl], v_new, "btn", False)
    o = jnp.concatenate(outs, axis=0)
    return (o, s, t) if with_t else (o, s)


def _f_swa(q8, kp, kc, vp, vc, bias8, qg, kg, sink, mask):
    kb = jnp.concatenate([kp, kc], axis=1)
    vb = jnp.concatenate([vp, vc], axis=1)
    kn = kb * lax.rsqrt(jnp.mean(kb * kb, axis=-1, keepdims=True) + EPS) * kg

    def rows(per_head):
        return jnp.stack([jnp.concatenate([per_head(kv, g) for g in range(SWA_G)], axis=0)
                          for kv in range(SWA_KV)], axis=0)

    qq = rows(lambda kv, g: q8[kv * SWA_G + g])
    qn = qq * lax.rsqrt(jnp.mean(qq * qq, axis=-1, keepdims=True) + EPS) * qg * (SWA_D ** -0.5)
    lg = _dot(qn, kn, "bnt", False) + rows(lambda kv, g: bias8[kv * SWA_G + g])
    lg = jnp.where(rows(lambda kv, g: mask), lg, NEG)
    sk = rows(lambda kv, g: jnp.broadcast_to(sink[kv][:, g:g + 1], (BLK, 1)))
    m = lax.stop_gradient(jnp.maximum(jnp.max(lg, axis=-1, keepdims=True), sk))
    p = jnp.exp(lg - m)
    den = jnp.sum(p, axis=-1, keepdims=True) + jnp.exp(sk - m)
    out = _dot(p * (1.0 / den), vb, "bnn", False)
    return jnp.stack([out[kv, g * BLK:(g + 1) * BLK] for kv in range(SWA_KV) for g in range(SWA_G)], axis=0)


def _bdot(a, b, kind="nn"):
    return lax.dot_general(a.astype(BF16), b.astype(BF16), _DIMS[kind], preferred_element_type=F32)


def _pc(kern, name, grid, in_specs, out_specs, out_shape, scratch=()):
    return pl.pallas_call(
        kern, name=name, grid=grid, in_specs=in_specs, out_specs=out_specs, out_shape=out_shape,
        scratch_shapes=list(scratch), compiler_params=_cparams(dimension_semantics=("arbitrary",) * len(grid)))


def _mm(a, b, kind, out_dtype, tm, tn, name):
    if kind == "tn":
        k, m = a.shape
    else:
        m, k = a.shape
    n = b.shape[0] if kind == "nt" else b.shape[1]
    tm, tn = min(tm, m), min(tn, n)
    assert m % tm == 0 and n % tn == 0, (name, a.shape, b.shape, tm, tn)

    def kern(a_ref, b_ref, o_ref):
        o_ref[...] = _bdot(a_ref[...], b_ref[...], kind).astype(o_ref.dtype)

    a_spec = pl.BlockSpec((k, tm), lambda i, j: (0, i)) if kind == "tn" else pl.BlockSpec((tm, k), lambda i, j: (i, 0))
    b_spec = pl.BlockSpec((tn, k), lambda i, j: (j, 0)) if kind == "nt" else pl.BlockSpec((k, tn), lambda i, j: (0, j))
    return _pc(kern, name, (m // tm, n // tn), [a_spec, b_spec], pl.BlockSpec((tm, tn), lambda i, j: (i, j)),
               SDS((m, n), out_dtype))(a, b)


def _rows(body, name, m, tm, row_ins, full_ins, row_outs, acc_outs=()):
    n_r, n_f, n_o, n_a = len(row_ins), len(full_ins), len(row_outs), len(acc_outs)
    assert m % tm == 0

    def kern(*refs):
        r = refs[:n_r]
        f = refs[n_r:n_r + n_f]
        o = refs[n_r + n_f:n_r + n_f + n_o]
        acc = refs[n_r + n_f + n_o:]
        outs, sums = body([x[...] for x in r], [x[...] for x in f])
        for ref, val in zip(o, outs, strict=True):
            ref[...] = val.astype(ref.dtype)
        if n_a:
            @pl.when(pl.program_id(0) == 0)
            def _():
                for ref in acc:
                    ref[...] = jnp.zeros(ref.shape, F32)

            for ref, val in zip(acc, sums, strict=True):
                ref[...] += val

    in_specs = [pl.BlockSpec((tm, w), functools.partial(lambda i, cb: (i, cb), cb=cb)) for _, w, cb in row_ins]
    in_specs += [pl.BlockSpec(x.shape, lambda i: (0, 0)) for x in full_ins]
    out_specs = [pl.BlockSpec((tm, w), lambda i: (i, 0)) for w, _ in row_outs]
    out_specs += [pl.BlockSpec(s, lambda i: (0, 0)) for s in acc_outs]
    out_shape = [SDS((m, w), dt) for w, dt in row_outs]
    out_shape += [SDS(s, F32) for s in acc_outs]
    return _pc(kern, name, (m // tm,), in_specs, out_specs, out_shape)(*[x for x, _, _ in row_ins], *full_ins)


def _whole(x):
    return (x, x.shape[1], 0)


def _resident(shape):
    return pl.BlockSpec(shape, lambda i: (0,) * len(shape), pipeline_mode=pl.Buffered(1))


def _row_pieces(tm, piece):
    piece = min(piece, tm)
    return [slice(r, r + piece) for r in range(0, tm, piece)]


def _zero_first(refs):
    @pl.when(pl.program_id(0) == 0)
    def _():
        for ref in refs:
            ref[...] = jnp.zeros(ref.shape, F32)


GROUP = 4


def _heads(ref):
    return jnp.stack([ref[g * CHUNK:(g + 1) * CHUNK, h * DH:(h + 1) * DH]
                      for g in range(GROUP) for h in range(DN_H)], axis=0)


def _unheads(ref, val):
    for g in range(GROUP):
        for h in range(DN_H):
            ref[g * CHUNK:(g + 1) * CHUNK, h * DH:(h + 1) * DH] = val[g * DN_H + h]


def _dn_chunks_fwd(q, k, v, gb, bb):
    s_len = q.shape[0]
    ng = s_len // (GROUP * CHUNK)

    def kern(q_ref, k_ref, v_ref, g_ref, b_ref, o_ref, sall_ref, t_ref, state):
        _zero_first([state])
        s = state[...]
        sall_ref[0] = s
        o, s_new, t = _f_chunk(*[_heads(r) for r in (q_ref, k_ref, v_ref, g_ref, b_ref)], s, with_t=True)
        _unheads(o_ref, o)
        t_ref[0] = t
        state[...] = s_new

    blk = pl.BlockSpec((GROUP * CHUNK, DNW), lambda c: (c, 0))
    return _pc(kern, "dn_chunks_fwd", (ng,), [blk] * 5,
               [blk, pl.BlockSpec((1, DN_H, DH, DH), lambda c: (c, 0, 0, 0)),
                pl.BlockSpec((1, GROUP * DN_H, CHUNK, CHUNK), lambda c: (c, 0, 0, 0))],
               [SDS((s_len, DNW), F32), SDS((ng, DN_H, DH, DH), F32), SDS((ng, GROUP * DN_H, CHUNK, CHUNK), F32)],
               scratch=[pltpu.VMEM((DN_H, DH, DH), F32)])(q, k, v, gb, bb)


def _dn_chunks_bwd(q, k, v, gb, bb, s_all, t_all, d_o):
    s_len = q.shape[0]
    ng = s_len // (GROUP * CHUNK)

    def kern(q_ref, k_ref, v_ref, g_ref, b_ref, sall_ref, t_ref, do_ref, dq_ref, dk_ref, dv_ref, dg_ref, db_ref,
             dstate):
        _zero_first([dstate])
        fn = functools.partial(_f_chunk, t_known=t_ref[0])
        _, vjp = jax.vjp(fn, *[_heads(r) for r in (q_ref, k_ref, v_ref, g_ref, b_ref)], sall_ref[0])
        *d_ins, ds = vjp((_heads(do_ref), dstate[...]))
        for ref, val in zip((dq_ref, dk_ref, dv_ref, dg_ref, db_ref), d_ins, strict=True):
            _unheads(ref, val)
        dstate[...] = ds

    blk = pl.BlockSpec((GROUP * CHUNK, DNW), lambda c: (ng - 1 - c, 0))
    return _pc(kern, "dn_chunks_bwd", (ng,),
               [blk] * 5 + [pl.BlockSpec((1, DN_H, DH, DH), lambda c: (ng - 1 - c, 0, 0, 0)),
                            pl.BlockSpec((1, GROUP * DN_H, CHUNK, CHUNK), lambda c: (ng - 1 - c, 0, 0, 0)), blk],
               [blk] * 5, [SDS((s_len, DNW), F32)] * 5,
               scratch=[pltpu.VMEM((DN_H, DH, DH), F32)])(q, k, v, gb, bb, s_all, t_all, d_o)


def _t5_bucket_table():
    qi = np.arange(BLK)[:, None]
    kj = np.arange(2 * BLK)[None, :]
    dist = BLK + qi - kj
    n = np.maximum(dist, 0)
    max_exact = NBUCKET // 2
    nf = np.maximum(n, 1).astype(np.float32)
    large = max_exact + (np.log(nf / np.float32(max_exact)) / np.float32(math.log(MAXDIST / max_exact))
                         * np.float32(NBUCKET - max_exact)).astype(np.int32)
    large = np.minimum(large, NBUCKET - 1)
    return np.where(n < max_exact, n, large)


def _bucket_onehot_t():
    table = _t5_bucket_table().reshape(-1)
    return (np.arange(NBUCKET)[:, None] == table[None, :]).astype(np.float32)


def _swa_mask(first):
    qi = lax.broadcasted_iota(jnp.int32, (BLK, 2 * BLK), 0)
    kj = lax.broadcasted_iota(jnp.int32, (BLK, 2 * BLK), 1)
    dist = BLK + qi - kj
    window = (dist >= 0) & (dist < BLK)
    return window & ((kj >= BLK) | jnp.logical_not(first))


def _bias_expand(rel_bias_t):
    onehot = jnp.asarray(_bucket_onehot_t())

    def kern(r_ref, oh_ref, o_ref):
        o_ref[...] = _raw_dot(r_ref[...], oh_ref[...], "nn", True)

    return pl.pallas_call(
        kern, name="bias_expand", out_shape=SDS((SWA_H, BLK * 2 * BLK), F32), compiler_params=_cparams(),
    )(rel_bias_t, onehot)


def _bias_reduce(d_bias_flat):
    onehot = jnp.asarray(_bucket_onehot_t())

    def kern(d_ref, oh_ref, o_ref):
        o_ref[...] = _raw_dot(d_ref[...], oh_ref[...], "nt", True)

    return pl.pallas_call(
        kern, name="bias_reduce", out_shape=SDS((SWA_H, NBUCKET), F32), compiler_params=_cparams(),
    )(d_bias_flat, onehot)


def _swa_specs(nb, rev):
    def blk(n):
        return (nb - 1 - n) if rev else n

    def before(n):
        return jnp.maximum(blk(n) - 1, 0)

    q_spec = pl.BlockSpec((BLK, SWAW), lambda n: (blk(n), C_SQ // SWAW))
    k_cur = pl.BlockSpec((BLK, SWAKW), lambda n: (blk(n), C_SK // SWAKW))
    k_prev = pl.BlockSpec((BLK, SWAKW), lambda n: (before(n), C_SK // SWAKW))
    v_cur = pl.BlockSpec((BLK, SWAKW), lambda n: (blk(n), C_SV // SWAKW))
    v_prev = pl.BlockSpec((BLK, SWAKW), lambda n: (before(n), C_SV // SWAKW))
    bias = pl.BlockSpec((SWA_H, BLK, 2 * BLK), lambda n: (0, 0, 0))
    gain = pl.BlockSpec((1, SWA_D), lambda n: (0, 0))
    sink = pl.BlockSpec((SWA_KV, 1, SWA_G), lambda n: (0, 0, 0))
    wide = pl.BlockSpec((BLK, SWAW), lambda n: (blk(n), 0))
    narrow = pl.BlockSpec((BLK, SWAKW), lambda n: (blk(n), 0))
    return [q_spec, k_prev, k_cur, v_prev, v_cur, bias, gain, gain, sink], wide, narrow


def _split_heads(x):
    return jnp.stack([x[:, h * SWA_D:(h + 1) * SWA_D] for h in range(x.shape[1] // SWA_D)], axis=0)


def _join_heads(x):
    return jnp.concatenate([x[h] for h in range(x.shape[0])], axis=1)


def _swa_fwd(proj, bias, qg, kg, sinks):
    s_len = proj.shape[0]
    nb = s_len // BLK
    in_specs, wide, _ = _swa_specs(nb, False)

    def kern(q_ref, kp_ref, kc_ref, vp_ref, vc_ref, b_ref, qg_ref, kg_ref, s_ref, o_ref):
        mask = _swa_mask(pl.program_id(0) == 0)
        o8 = _f_swa(*[_split_heads(r[...]) for r in (q_ref, kp_ref, kc_ref, vp_ref, vc_ref)], b_ref[...], qg_ref[...],
                    kg_ref[...], s_ref[...], mask)
        o_ref[...] = _join_heads(o8).astype(BF16)

    return _pc(kern, "swa_fwd", (nb,), in_specs, wide, SDS((s_len, SWAW), BF16))(
        proj, proj, proj, proj, proj, bias, qg, kg, sinks)


def _swa_bwd(proj, bias, qg, kg, sinks, d_out):
    s_len = proj.shape[0]
    nb = s_len // BLK
    in_specs, wide, narrow = _swa_specs(nb, True)

    def kern(q_ref, kp_ref, kc_ref, vp_ref, vc_ref, b_ref, qg_ref, kg_ref, s_ref, do_ref,
             dq_ref, dk_ref, dv_ref, db_ref, dqg_ref, dkg_ref, ds_ref, carry_k, carry_v):
        n = pl.program_id(0)
        mask = _swa_mask(n == nb - 1)
        _zero_first([carry_k, carry_v, db_ref, ds_ref, dqg_ref, dkg_ref])
        fn = functools.partial(_f_swa, mask=mask)
        _, vjp = jax.vjp(fn, *[_split_heads(r[...]) for r in (q_ref, kp_ref, kc_ref, vp_ref, vc_ref)], b_ref[...],
                         qg_ref[...], kg_ref[...], s_ref[...])
        dq, dkp, dkc, dvp, dvc, dbias, dqg, dkg, dsink = vjp(_split_heads(do_ref[...]))
        dq_ref[...] = _join_heads(dq).astype(BF16)
        dk_ref[...] = (_join_heads(dkc) + carry_k[...]).astype(BF16)
        dv_ref[...] = (_join_heads(dvc) + carry_v[...]).astype(BF16)
        carry_k[...] = _join_heads(dkp)
        carry_v[...] = _join_heads(dvp)
        db_ref[...] += dbias
        dqg_ref[...] += dqg
        dkg_ref[...] += dkg
        ds_ref[...] += dsink

    bias_spec, gain, sink = in_specs[5], in_specs[6], in_specs[8]
    return _pc(
        kern, "swa_bwd", (nb,), in_specs + [wide], [wide, narrow, narrow, bias_spec, gain, gain, sink],
        [SDS((s_len, SWAW), BF16), SDS((s_len, SWAKW), BF16), SDS((s_len, SWAKW), BF16),
         SDS((SWA_H, BLK, 2 * BLK), F32), SDS((1, SWA_D), F32), SDS((1, SWA_D), F32), SDS((SWA_KV, 1, SWA_G), F32)],
        scratch=[pltpu.VMEM((BLK, SWAKW), F32), pltpu.VMEM((BLK, SWAKW), F32)],
    )(proj, proj, proj, proj, proj, bias, qg, kg, sinks, d_out)


def _branch_merge(y_dn, y_swa, wa, wb, proj):
    s_len = y_dn.shape[0]
    tm = min(1024, s_len)

    def kern(ya_ref, yb_ref, wa_ref, wb_ref, ga_ref, gb_ref, pa_ref, pb_ref, m_ref):
        for rows in _row_pieces(tm, 128):
            pa = _bdot(ya_ref[rows, :], wa_ref[0])
            pb = _bdot(yb_ref[rows, :], wb_ref[0])
            pa_ref[rows, :] = pa.astype(BF16)
            pb_ref[rows, :] = pb.astype(BF16)
            m_ref[rows, :] = _f_merge(pa, pb, ga_ref[rows, :], gb_ref[rows, :]).astype(BF16)

    y_spec = pl.BlockSpec((tm, DNW), lambda i, s: (i, 0))
    w_spec = pl.BlockSpec((1, DNW, CSH), lambda i, s: (s, 0, 0))
    o_spec = pl.BlockSpec((tm, CSH), lambda i, s: (i, s))
    ga_spec = pl.BlockSpec((tm, CSH), lambda i, s: (i, C_GATE // CSH + s))
    gb_spec = pl.BlockSpec((tm, CSH), lambda i, s: (i, (C_GATE + D) // CSH + s))
    return _pc(kern, "branch_merge", (s_len // tm, N_CHIPS), [y_spec, y_spec, w_spec, w_spec, ga_spec, gb_spec],
               [o_spec] * 3, [SDS((s_len, D), BF16)] * 3,
               )(y_dn, y_swa, wa, wb, proj, proj)


def _in_proj(x, gain, w_in_p):
    s_len = x.shape[0]
    tm = min(512, s_len)

    def kern(x_ref, g_ref, w_ref, h_ref, p_ref):
        h = _f_rms(x_ref[...], g_ref[...]).astype(BF16)
        h_ref[...] = h
        p_ref[...] = _bdot(h, w_ref[...])

    row = pl.BlockSpec((tm, D), lambda i: (i, 0))
    return _pc(kern, "in_proj", (s_len // tm,),
               [row, pl.BlockSpec((1, D), lambda i: (0, 0)), _resident((D, PW))],
               [row, pl.BlockSpec((tm, PW), lambda i: (i, 0))],
               [SDS((s_len, D), BF16), SDS((s_len, PW), F32)])(x, gain, w_in_p)


def _out_proj(merged, w_out, x, gain):
    s_len = x.shape[0]
    tm = min(512, s_len)

    def kern(m_ref, w_ref, x_ref, g_ref, x1_ref, h2_ref):
        x1 = x_ref[...] + _bdot(m_ref[...], w_ref[...])
        x1_ref[...] = x1
        h2_ref[...] = _f_rms(x1, g_ref[...]).astype(BF16)

    row = pl.BlockSpec((tm, D), lambda i: (i, 0))
    return _pc(kern, "out_proj", (s_len // tm,),
               [row, _resident((D, D)), row, pl.BlockSpec((1, D), lambda i: (0, 0))],
               [row, row], [SDS((s_len, D), F32), SDS((s_len, D), BF16)])(merged, w_out, x, gain)


def _ffn_up(h2, wg, wu):
    s_len = h2.shape[0]
    tm = min(2048, s_len)

    def kern(h_ref, g_ref, u_ref, gt_ref, up_ref, act_ref):
        for rows in _row_pieces(tm, 256):
            h = h_ref[rows, :]
            g = _bdot(h, g_ref[0], "nt")
            u = _bdot(h, u_ref[0], "nt")
            gt_ref[0, rows, :] = g.astype(BF16)
            up_ref[0, rows, :] = u.astype(BF16)
            act_ref[0, rows, :] = _f_swiglu(g, u).astype(BF16)

    w_spec = pl.BlockSpec((1, FSH, D), lambda s, i: (s, 0, 0))
    o_spec = pl.BlockSpec((1, tm, FSH), lambda s, i: (s, i, 0))
    shape = (N_CHIPS, s_len, FSH)
    return _pc(kern, "ffn_up", (N_CHIPS, s_len // tm), [pl.BlockSpec((tm, D), lambda s, i: (i, 0)), w_spec, w_spec],
               [o_spec] * 3, [SDS(shape, BF16)] * 3)(h2, wg, wu)


def _ffn_down_loss(act, wd, x1, target):
    s_len = x1.shape[0]
    tm = min(512, s_len)

    def kern(a_ref, w_ref, x_ref, t_ref, dy_ref, dyb_ref, loss_ref):
        _zero_first([loss_ref])
        for rows in _row_pieces(tm, 128):
            y = x_ref[rows, :]
            for s in range(N_CHIPS):
                y = y + _bdot(a_ref[s, rows, :], w_ref[s])
            d = y - t_ref[rows, :]
            dy = d * (1.0 / D)
            dy_ref[rows, :] = dy
            dyb_ref[rows, :] = dy.astype(BF16)
            loss_ref[...] += jnp.sum(d * d).reshape(1, 1) * (0.5 / D)

    row = pl.BlockSpec((tm, D), lambda i: (i, 0))
    return _pc(kern, "ffn_down_loss", (s_len // tm,),
               [pl.BlockSpec((N_CHIPS, tm, FSH), lambda i: (0, i, 0)),
                _resident((N_CHIPS, FSH, D)), row, row],
               [row, row, pl.BlockSpec((1, 1), lambda i: (0, 0))],
               [SDS((s_len, D), F32), SDS((s_len, D), BF16), SDS((1, 1), F32)])(act, wd, x1, target)


def _ffn_dact(dy_b, wd, gt, up):
    s_len = dy_b.shape[0]
    tm = min(2048, s_len)

    def kern(dy_ref, w_ref, gt_ref, up_ref, dg_ref, du_ref):
        w = w_ref[0]
        for rows in _row_pieces(tm, 256):
            d_act = _bdot(dy_ref[rows, :], w, "nt")
            _, vjp = jax.vjp(_f_swiglu, gt_ref[0, rows, :].astype(F32), up_ref[0, rows, :].astype(F32))
            dg, du = vjp(d_act)
            dg_ref[0, rows, :] = dg.astype(BF16)
            du_ref[0, rows, :] = du.astype(BF16)

    a_spec = pl.BlockSpec((1, tm, FSH), lambda s, i: (s, i, 0))
    shape = (N_CHIPS, s_len, FSH)
    return _pc(kern, "ffn_dact", (N_CHIPS, s_len // tm),
               [pl.BlockSpec((tm, D), lambda s, i: (i, 0)), pl.BlockSpec((1, FSH, D), lambda s, i: (s, 0, 0)),
                a_spec, a_spec],
               [a_spec, a_spec], [SDS(shape, BF16), SDS(shape, BF16)])(dy_b, wd, gt, up)


def _gw_ffn(lhs, rhs, name):
    s_len = rhs.shape[0]
    n = len(lhs)
    tn = 512

    def kern(*refs):
        g = refs[n][...]
        for i in range(n):
            refs[n + 1 + i][0] = _bdot(refs[i][0], g, "tn").astype(BF16)

    a_spec = pl.BlockSpec((1, s_len, FSH), lambda s, j: (s, 0, 0))
    o_spec = pl.BlockSpec((1, FSH, tn), lambda s, j: (s, 0, j))
    return _pc(kern, name, (N_CHIPS, D // tn), [a_spec] * n + [pl.BlockSpec((s_len, tn), lambda s, j: (0, j))],
               [o_spec] * n, [SDS((N_CHIPS, FSH, D), BF16)] * n)(*lhs, rhs)


def _ffn_dh2(d_gt, d_up, wg, wu, x1, dy, gain):
    s_len = x1.shape[0]
    tm = min(512, s_len)

    def kern(dg_ref, du_ref, wg_ref, wu_ref, x_ref, dy_ref, g_ref, dx_ref, dxb_ref, dgain_ref):
        _zero_first([dgain_ref])
        dh2 = jnp.zeros((tm, D), F32)
        for s in range(N_CHIPS):
            dh2 = dh2 + _bdot(dg_ref[s], wg_ref[s]) + _bdot(du_ref[s], wu_ref[s])
        _, vjp = jax.vjp(_f_rms, x_ref[...], g_ref[...])
        dx, dgain = vjp(dh2)
        dx1 = dx + dy_ref[...]
        dx_ref[...] = dx1
        dxb_ref[...] = dx1.astype(BF16)
        dgain_ref[...] += dgain

    row = pl.BlockSpec((tm, D), lambda i: (i, 0))
    d_spec = pl.BlockSpec((N_CHIPS, tm, FSH), lambda i: (0, i, 0))
    w_spec = _resident((N_CHIPS, FSH, D))
    vec = pl.BlockSpec((1, D), lambda i: (0, 0))
    return _pc(kern, "ffn_dh2", (s_len // tm,), [d_spec, d_spec, w_spec, w_spec, row, row, vec],
               [row, row, vec], [SDS((s_len, D), F32), SDS((s_len, D), BF16), SDS((1, D), F32)],
               )(d_gt, d_up, wg, wu, x1, dy, gain)


def _merge_bwd(dx1_b, w_out, pa, pb, proj):
    s_len = dx1_b.shape[0]
    tm = min(512, s_len)

    def kern(dx_ref, w_ref, pa_ref, pb_ref, g_ref, dpa_ref, dpb_ref, dg_ref):
        dm = _bdot(dx_ref[...], w_ref[...], "nt")
        gates = g_ref[...]
        _, vjp = jax.vjp(_f_merge, pa_ref[...].astype(F32), pb_ref[...].astype(F32), gates[:, :D], gates[:, D:])
        dpa, dpb, dga, dgb = vjp(dm)
        dpa_ref[...] = dpa.astype(BF16)
        dpb_ref[...] = dpb.astype(BF16)
        dg_ref[:, :D] = dga.astype(BF16)
        dg_ref[:, D:] = dgb.astype(BF16)

    row = pl.BlockSpec((tm, D), lambda i: (i, 0))
    return _pc(kern, "merge_bwd", (s_len // tm,),
               [row, _resident((D, D)), row, row,
                pl.BlockSpec((tm, 2 * D), lambda i: (i, C_GATE // (2 * D)))],
               [row, row, pl.BlockSpec((tm, 2 * D), lambda i: (i, 0))],
               [SDS((s_len, D), BF16), SDS((s_len, D), BF16), SDS((s_len, 2 * D), BF16)],
               )(dx1_b, w_out, pa, pb, proj)


def _d_branch(d_pa, d_pb, wa, wb):
    s_len = d_pa.shape[0]
    tm = min(512, s_len)

    def kern(da_ref, db_ref, wa_ref, wb_ref, oa_ref, ob_ref):
        acc_a = jnp.zeros((tm, DNW), F32)
        acc_b = jnp.zeros((tm, SWAW), F32)
        for s in range(N_CHIPS):
            acc_a = acc_a + _bdot(da_ref[:, s * CSH:(s + 1) * CSH], wa_ref[s], "nt")
            acc_b = acc_b + _bdot(db_ref[:, s * CSH:(s + 1) * CSH], wb_ref[s], "nt")
        oa_ref[...] = acc_a
        ob_ref[...] = acc_b

    row = pl.BlockSpec((tm, D), lambda i: (i, 0))
    w_spec = pl.BlockSpec((N_CHIPS, DNW, CSH), lambda i: (0, 0, 0))
    out = pl.BlockSpec((tm, DNW), lambda i: (i, 0))
    return _pc(kern, "d_branch", (s_len // tm,), [row, row, w_spec, w_spec], [out, out],
               [SDS((s_len, DNW), F32), SDS((s_len, SWAW), F32)])(d_pa, d_pb, wa, wb)


def _gw_branch(y_dn, y_swa, d_pa, d_pb):
    s_len = y_dn.shape[0]

    def kern(ya_ref, yb_ref, da_ref, db_ref, oa_ref, ob_ref):
        oa_ref[0] = _bdot(ya_ref[...], da_ref[...], "tn").astype(BF16)
        ob_ref[0] = _bdot(yb_ref[...], db_ref[...], "tn").astype(BF16)

    y_spec = pl.BlockSpec((s_len, DNW), lambda s: (0, 0))
    d_spec = pl.BlockSpec((s_len, CSH), lambda s: (0, s))
    o_spec = pl.BlockSpec((1, DNW, CSH), lambda s: (s, 0, 0))
    shape = (N_CHIPS, DNW, CSH)
    return _pc(kern, "gw_branch", (N_CHIPS,), [y_spec, y_spec, d_spec, d_spec], [o_spec, o_spec],
               [SDS(shape, BF16), SDS(shape, BF16)])(y_dn, y_swa, d_pa, d_pb)


def _dh_rms(d_proj, w_in_p, x, dx1, gain):
    s_len = x.shape[0]
    tm = min(512, s_len)

    def kern(dp_ref, w_ref, x_ref, r_ref, g_ref, gx_ref, dgain_ref):
        _zero_first([dgain_ref])
        dh = _bdot(dp_ref[...], w_ref[...], "nt")
        _, vjp = jax.vjp(_f_rms, x_ref[...], g_ref[...])
        dx, dgain = vjp(dh)
        gx_ref[...] = dx + r_ref[...]
        dgain_ref[...] += dgain

    row = pl.BlockSpec((tm, D), lambda i: (i, 0))
    vec = pl.BlockSpec((1, D), lambda i: (0, 0))
    return _pc(kern, "dh_rms", (s_len // tm,),
               [pl.BlockSpec((tm, PW), lambda i: (i, 0)), _resident((D, PW)), row, row, vec],
               [row, vec], [SDS((s_len, D), F32), SDS((1, D), F32)])(d_proj, w_in_p, x, dx1, gain)


HALO = 8


def _rows_down(x, n, above):
    tm = x.shape[0]
    r = pltpu.roll(x, n, 0)
    a = pltpu.roll(above, n, 0)
    top = jnp.where(lax.broadcasted_iota(jnp.int32, above.shape, 0) < n, a, r[0:HALO])
    return jnp.concatenate([top, r[HALO:tm]], axis=0)


def _rows_up(x, n, below):
    tm = x.shape[0]
    r = pltpu.roll(x, tm - n, 0)
    b = pltpu.roll(below, HALO - n, 0)
    bottom = jnp.where(lax.broadcasted_iota(jnp.int32, below.shape, 0) >= HALO - n, b, r[tm - HALO:tm])
    return jnp.concatenate([r[0:tm - HALO], bottom], axis=0)


def _conv_taps(cur_ref, prev_ref, first):
    cur = cur_ref[...]
    above = jnp.where(first, 0.0, prev_ref[...])
    return [_rows_down(cur, n, above) for n in range(CONV - 1, 0, -1)] + [cur]


def _dn_pre_specs(s_len, tm, blk):
    cur = pl.BlockSpec((tm, QKVW), lambda i: (blk(i), 0))
    prev = pl.BlockSpec((HALO, QKVW), lambda i: (jnp.maximum(blk(i) * (tm // HALO) - 1, 0), 0))
    ba = pl.BlockSpec((tm, 128), lambda i: (blk(i), C_BA // 128))
    row = pl.BlockSpec((tm, DNW), lambda i: (blk(i), 0))
    full = [pl.BlockSpec((CONV, QKVW), lambda i: (0, 0)), pl.BlockSpec((1, DN_H), lambda i: (0, 0)),
            pl.BlockSpec((1, DN_H), lambda i: (0, 0))]
    return cur, prev, ba, row, full


def _dn_pre_fwd(proj, conv_w, alog, dtb):
    s_len = proj.shape[0]
    tm = min(256, s_len)
    cur, prev, ba, row, full = _dn_pre_specs(s_len, tm, lambda i: i)

    def kern(cur_ref, prev_ref, ba_ref, cw_ref, al_ref, dt_ref, q_ref, k_ref, v_ref, bb_ref, gb_ref):
        xs = _conv_taps(cur_ref, prev_ref, pl.program_id(0) == 0)
        outs = _f_dn_pre(*xs, ba_ref[...], cw_ref[...], al_ref[...], dt_ref[...])
        for ref, val in zip((q_ref, k_ref, v_ref, bb_ref, gb_ref), outs, strict=True):
            ref[...] = val

    return _pc(kern, "dn_pre_fwd", (s_len // tm,), [cur, prev, ba] + full, [row] * 5,
               [SDS((s_len, DNW), F32)] * 5)(proj, proj, proj, conv_w, alog, dtb)


def _dn_pre_bwd(proj, conv_w, alog, dtb, cots, others):
    s_len = proj.shape[0]
    tm = min(256, s_len)
    nb = s_len // tm
    cur, prev, ba, row, full = _dn_pre_specs(s_len, tm, lambda i: nb - 1 - i)
    n_o = len(others)
    assert QKVW + sum(t.shape[1] for t in others) + 128 == C_BA + 128

    def kern(cur_ref, prev_ref, ba_ref, cw_ref, al_ref, dt_ref, dq_ref, dk_ref, dv_ref, dbb_ref, dgb_ref, *rest):
        o_refs = rest[:n_o]
        dproj_ref, dcw_ref, dal_ref, ddt_ref, *tails = rest[n_o:]
        i = pl.program_id(0)
        _zero_first([dcw_ref, dal_ref, ddt_ref] + tails)
        xs = _conv_taps(cur_ref, prev_ref, i == nb - 1)
        _, vjp = jax.vjp(_f_dn_pre, *xs, ba_ref[...], cw_ref[...], al_ref[...], dt_ref[...])
        *dxs, dba, dcw, dal, ddt = vjp((dq_ref[...], dk_ref[...], dv_ref[...], dbb_ref[...], dgb_ref[...]))
        total = dxs[CONV - 1]
        for j, t in enumerate(tails):
            n = CONV - 1 - j
            total = total + _rows_up(dxs[j], n, t[...])
            t[...] = dxs[j][0:HALO, :]
        dproj_ref[...] = jnp.concatenate(
            [total.astype(BF16)] + [r[...] for r in o_refs] + [dba.astype(BF16), jnp.zeros((tm, PW - C_BA - 128), BF16)],
            axis=1)
        dcw_ref[...] += dcw
        dal_ref[...] += dal
        ddt_ref[...] += ddt

    o_specs = [pl.BlockSpec((tm, t.shape[1]), lambda i: (nb - 1 - i, 0)) for t in others]
    return _pc(kern, "dn_pre_bwd", (nb,), [cur, prev, ba] + full + [row] * 5 + o_specs,
               [pl.BlockSpec((tm, PW), lambda i: (nb - 1 - i, 0))] + full,
               [SDS((s_len, PW), BF16), SDS((CONV, QKVW), F32), SDS((1, DN_H), F32), SDS((1, DN_H), F32)],
               scratch=[pltpu.VMEM((HALO, QKVW), F32)] * (CONV - 1))(proj, proj, proj, conv_w, alog, dtb, *cots, *others)


def _w_in_to_padded(w_sh):
    tr = 256

    def kern(w_ref, o_ref):
        full = jnp.concatenate([w_ref[s] for s in range(N_CHIPS)], axis=1)
        pieces = [full[:, o0:o0 + w] for o0, w, _ in sorted(_ORIG_PIECES, key=lambda t: t[2])]
        o_ref[...] = jnp.concatenate(pieces + [jnp.zeros((tr, PW - D_IN), w_ref.dtype)], axis=1)

    return _pc(kern, "w_in_to_padded", (D // tr,), [pl.BlockSpec((N_CHIPS, tr, D_IN // N_CHIPS), lambda i: (0, i, 0))],
               pl.BlockSpec((tr, PW), lambda i: (i, 0)), SDS((D, PW), w_sh.dtype))(w_sh)


def _padded_to_w_in(g):
    tr = 256
    csh = D_IN // N_CHIPS

    def kern(g_ref, o_ref):
        x = g_ref[...]
        full = jnp.concatenate([x[:, p0:p0 + w] for _, w, p0 in _ORIG_PIECES], axis=1)
        for s in range(N_CHIPS):
            o_ref[s] = full[:, s * csh:(s + 1) * csh]

    return _pc(kern, "padded_to_w_in", (D // tr,), [pl.BlockSpec((tr, PW), lambda i: (i, 0))],
               pl.BlockSpec((N_CHIPS, tr, csh), lambda i: (0, i, 0)), SDS((N_CHIPS, D, csh), g.dtype))(g)


def _local_step(x, target, wts):
    s_len = x.shape[0]
    tm = min(512, s_len)
    w_in_p = wts["w_in_p"]
    attn_gain = wts["attn_norm"]
    ffn_gain = wts["ffn_norm"]
    conv_w = wts["dn_conv"]
    alog, dtb, out_gain = wts["dn_a_log"], wts["dn_dt_bias"], wts["dn_out_norm"]
    qg, kg = wts["swa_q_norm"], wts["swa_k_norm"]
    sinks = wts["swa_sinks"].reshape(SWA_KV, 1, SWA_G)

    h, proj = _in_proj(x, attn_gain, w_in_p)
    q_dn, k_dn, v_dn, bb, gb = _dn_pre_fwd(proj, conv_w, alog, dtb)
    o_dn, s_all, t_all = _dn_chunks_fwd(q_dn, k_dn, v_dn, gb, bb)
    post_ins = [_whole(o_dn), (proj, DNW, C_Z // DNW)]
    (y_dn,) = _rows(lambda r, f: ([_f_dn_post(r[0], r[1], f[0])], []), "dn_post_fwd", s_len, tm, post_ins,
                    [out_gain], [(DNW, BF16)])

    bias = _bias_expand(wts["rel_bias"].T).reshape(SWA_H, BLK, 2 * BLK)
    y_swa = _swa_fwd(proj, bias, qg, kg, sinks)

    wts = {**wts, **wts["late"](y_swa)}
    p_a, p_b, merged = _branch_merge(y_dn, y_swa, wts["wa"], wts["wb"], proj)
    x1, h2 = _out_proj(merged, wts["w_out"], x, ffn_gain)
    gt, up, act = _ffn_up(h2, wts["wg"], wts["wu"])
    dy, dy_b, loss = _ffn_down_loss(act, wts["wd"], x1, target)

    grads = {}
    d_gt, d_up = _ffn_dact(dy_b, wts["wd"], gt, up)
    (grads["w_down"],) = _gw_ffn([act], dy_b, "gw_down")
    grads["w_gate"], grads["w_up"] = _gw_ffn([d_gt, d_up], h2, "gw_gate_up")
    token = wts["send_ffn"](grads)
    dx1, dx1_b, grads["ffn_norm"] = _ffn_dh2(d_gt, d_up, wts["wg"], wts["wu"], x1, dy,
                                             ffn_gain + token[0:1, 0:1])
    grads["w_out"] = _mm(merged, dx1_b, "tn", BF16, 512, 512, "gw_out")
    d_pa, d_pb, d_gr = _merge_bwd(dx1_b, wts["w_out"], p_a, p_b, proj)
    d_ydn, d_yswa = _d_branch(d_pa, d_pb, wts["wa"], wts["wb"])
    grads["w_branch_dn"], grads["w_branch_swa"] = _gw_branch(y_dn, y_swa, d_pa, d_pb)
    token = wts["send_early"](grads)
    qg_t = qg + token[0:1, 0:1]
    out_gain_t = out_gain + token[0:1, 0:1]

    d_sq, d_sk, d_sv, d_bias, grads["swa_q_norm"], grads["swa_k_norm"], d_sinks = _swa_bwd(
        proj, bias, qg_t, kg, sinks, d_yswa)
    grads["swa_sinks"] = d_sinks.reshape(1, SWA_H)
    grads["rel_bias"] = _bias_reduce(d_bias.reshape(SWA_H, BLK * 2 * BLK)).T

    def post_bwd(r, f):
        _, vjp = jax.vjp(_f_dn_post, r[0], r[1], f[0])
        d_o, d_z, d_gain = vjp(r[2])
        return [d_o, d_z], [d_gain]

    d_o, d_z, grads["dn_out_norm"] = _rows(post_bwd, "dn_post_bwd", s_len, tm, post_ins + [_whole(d_ydn)], [out_gain_t],
                                           [(DNW, F32), (DNW, BF16)], [(1, DH)])
    d_q, d_k, d_v, d_gb, d_bb = _dn_chunks_bwd(q_dn, k_dn, v_dn, gb, bb, s_all, t_all, d_o)

    d_proj, grads["dn_conv"], grads["dn_a_log"], grads["dn_dt_bias"] = _dn_pre_bwd(
        proj, conv_w, alog, dtb, (d_q, d_k, d_v, d_bb, d_gb), (d_z, d_gr, d_sq, d_sk, d_sv))
    grads["w_in_p"] = _mm(h, d_proj, "tn", BF16, 1024, 1024, "gw_in")
    token = wts["send_in"](grads["w_in_p"])
    grad_x, grads["attn_norm"] = _dh_rms(d_proj, w_in_p, x, dx1, attn_gain + token[0:1, 0:1])
    return loss, grad_x, grads


_HBM = pl.BlockSpec(memory_space=pl.ANY)


def _place():
    return lax.axis_index("x"), lax.axis_index("y"), lax.axis_index("c")


def _other_chips(x, y):
    return [(1 - x, y), (x, 1 - y), (1 - x, 1 - y)]


def _rcopy(src, dst, send_sems, recv_sems, k, to):
    return pltpu.make_async_remote_copy(src_ref=src, dst_ref=dst, send_sem=send_sems.at[k], recv_sem=recv_sems.at[k],
                                        device_id=to, device_id_type=MESH)


def _comm_call(body, name, ins, out_shapes, n_remote, landing=0):
    first = len(ins) - landing
    return pl.pallas_call(
        body, name=name, in_specs=[_HBM] * len(ins), out_specs=[_HBM] * len(out_shapes), out_shape=out_shapes,
        scratch_shapes=[pltpu.SemaphoreType.DMA((n_remote,)), pltpu.SemaphoreType.DMA((n_remote,))],
        input_output_aliases={first + i: i for i in range(landing)},
        compiler_params=_cparams(has_side_effects=True),
    )(*ins)


def _own_slot(blocks, chip):
    return [lax.dynamic_update_slice(lax.empty((N_CHIPS,) + b.shape, b.dtype), b[None], (chip, 0, 0)) for b in blocks]


def _gather_weights(ws, chip):
    n = len(ws)
    halves = [w.shape[0] // 2 for w in ws]

    def body(*refs):
        w_refs, o_refs = refs[:n], refs[2 * n:3 * n]
        send_sems, recv_sems = refs[3 * n:]
        x, y, c = _place()
        s = 2 * x + y
        sib = (x, y, 1 - c)
        chips = _other_chips(x, y)

        def rows(i, half):
            return pl.ds(half * halves[i], halves[i])

        first = []
        for j, (cx, cy) in enumerate(chips):
            for i in range(n):
                cp = _rcopy(w_refs[i].at[rows(i, c), :], o_refs[i].at[s, rows(i, c), :], send_sems, recv_sems,
                            j * n + i, (cx, cy, c))
                cp.start()
                first.append(cp)
        passed = []
        for j, (cx, cy) in enumerate(chips):
            sj = 2 * cx + cy
            for i in range(n):
                blk = o_refs[i].at[sj, rows(i, c), :]
                _rcopy(blk, blk, send_sems, recv_sems, j * n + i, (cx, cy, c)).wait_recv()
                cp = _rcopy(blk, blk, send_sems, recv_sems, (3 + j) * n + i, sib)
                cp.start()
                passed.append(cp)
        for j, (cx, cy) in enumerate(chips):
            sj = 2 * cx + cy
            for i in range(n):
                blk = o_refs[i].at[sj, rows(i, 1 - c), :]
                _rcopy(blk, blk, send_sems, recv_sems, (3 + j) * n + i, sib).wait_recv()
        for cp in first + passed:
            cp.wait_send()

    return _comm_call(body, "gather_weights", list(ws) + _own_slot(ws, chip),
                      [SDS((N_CHIPS,) + w.shape, w.dtype) for w in ws], 6 * n, landing=n)


_HBM_ONLY = pl.BlockSpec(memory_space=pltpu.HBM)
_SEM = pl.BlockSpec(memory_space=pltpu.SEMAPHORE)
_DATAFLOW = pltpu.SideEffectType.DATAFLOW_SIDE_EFFECTING


def _in_hbm(a):
    return pltpu.with_memory_space_constraint(a, pltpu.HBM)


def _gather_windows(blocks):
    halves = [b.shape[0] // 2 for b in blocks]

    def src_at(ref, i, c, sj):
        return ref.at[pl.ds(c * halves[i], halves[i]), :]

    def dst_at(ref, i, c, s_from):
        return ref.at[s_from, pl.ds(c * halves[i], halves[i]), :]

    return src_at, dst_at


def _exchange_windows():
    return (lambda ref, i, c, sj: ref.at[sj]), (lambda ref, i, c, s_from: ref.at[s_from])


def _swap_windows(gs):
    halves = [g.shape[1] // 2 for g in gs]
    return ((lambda ref, i, c, tag: ref.at[:, pl.ds((1 - c) * halves[i], halves[i]), :]),
            (lambda ref, i, c, slot: ref))


def _chip_peers(x, y, c):
    return [(2 * cx + cy, (cx, cy, c), 2 * x + y, 2 * cx + cy) for cx, cy in _other_chips(x, y)]


def _sibling_peer(x, y, c):
    return [(0, (x, y, 1 - c), 0, 0)]


def _split_start(name, ws, lands, dep, windows, peers=_chip_peers, n_peers=3):
    n = len(ws)
    src_at, dst_at = windows

    def body(*refs):
        w_refs, l_refs = refs[:n], refs[n:2 * n]
        send_sems, recv_sems = refs[2 * n + 1], refs[2 * n + 2]
        token = refs[-1]
        x, y, c = _place()
        for j, (tag, dev, there, _) in enumerate(peers(x, y, c)):
            for i in range(n):
                _rcopy(src_at(w_refs[i], i, c, tag), dst_at(l_refs[i], i, c, there), send_sems, recv_sems,
                       j * n + i, dev).start()
        token[...] = jnp.zeros_like(token)

    outs = pl.pallas_call(
        body, name=name,
        out_shape=(pltpu.SemaphoreType.DMA((n_peers * n,)), pltpu.SemaphoreType.DMA((n_peers * n,)),
                   *[pltpu.HBM(w.shape, w.dtype) for w in ws], *[pltpu.HBM(t.shape, t.dtype) for t in lands],
                   SDS((8, 128), F32)),
        in_specs=[_HBM_ONLY] * (2 * n) + [pl.BlockSpec(memory_space=pl.ANY)],
        out_specs=(_SEM, _SEM, *[_HBM_ONLY] * (2 * n), pl.BlockSpec(memory_space=pltpu.VMEM)),
        input_output_aliases={i: 2 + i for i in range(2 * n)},
        compiler_params=pltpu.CompilerParams(has_side_effects=_DATAFLOW),
    )(*[_in_hbm(w) for w in ws], *[_in_hbm(t) for t in lands], dep)
    return outs[0], outs[1], outs[2:2 + n], outs[2 + n:2 + 2 * n], outs[-1]


def _split_wait(name, w_thru, l_thru, send_sems, recv_sems, after, windows, peers=_chip_peers, with_sources=False):
    n = len(w_thru)
    src_at, dst_at = windows

    def body(*refs):
        w_refs, l_refs = refs[:n], refs[n:2 * n]
        send_sems, recv_sems = refs[2 * n], refs[2 * n + 1]
        x, y, c = _place()
        for j, (tag, dev, _, here) in enumerate(peers(x, y, c)):
            for i in range(n):
                cp = _rcopy(src_at(w_refs[i], i, c, tag), dst_at(l_refs[i], i, c, here), send_sems, recv_sems,
                            j * n + i, dev)
                cp.wait_send()
                cp.wait_recv()

    outs = pl.pallas_call(
        body, name=name,
        out_shape=[pltpu.HBM(w.shape, w.dtype) for w in w_thru] + [pltpu.HBM(t.shape, t.dtype) for t in l_thru],
        in_specs=[_HBM_ONLY] * (2 * n) + [_SEM, _SEM, pl.BlockSpec(memory_space=pl.ANY)],
        out_specs=[_HBM_ONLY] * (2 * n),
        input_output_aliases={i: i for i in range(2 * n)},
        compiler_params=pltpu.CompilerParams(has_side_effects=_DATAFLOW),
    )(*w_thru, *l_thru, send_sems, recv_sems, after)
    return (outs[:n], outs[n:]) if with_sources else outs[n:]


def _sibling_fill(lands):
    n = len(lands)
    halves = [t.shape[1] // 2 for t in lands]

    def body(*refs):
        o_refs = refs[n:2 * n]
        send_sems, recv_sems = refs[2 * n:]
        x, y, c = _place()
        sib = (x, y, 1 - c)
        chips = _other_chips(x, y)
        sent = []
        for j, (cx, cy) in enumerate(chips):
            for i in range(n):
                blk = o_refs[i].at[2 * cx + cy, pl.ds(c * halves[i], halves[i]), :]
                cp = _rcopy(blk, blk, send_sems, recv_sems, j * n + i, sib)
                cp.start()
                sent.append(cp)
        for j, (cx, cy) in enumerate(chips):
            for i in range(n):
                blk = o_refs[i].at[2 * cx + cy, pl.ds((1 - c) * halves[i], halves[i]), :]
                _rcopy(blk, blk, send_sems, recv_sems, j * n + i, sib).wait_recv()
        for cp in sent:
            cp.wait_send()

    return _comm_call(body, "sibling_fill", list(lands), [SDS(t.shape, t.dtype) for t in lands], 3 * n, landing=n)


def _swap_halves(gs, name):
    n = len(gs)
    halves = [g.shape[1] // 2 for g in gs]

    def body(*refs):
        g_refs, o_refs = refs[:n], refs[n:2 * n]
        send_sems, recv_sems = refs[2 * n:]
        x, y, c = _place()
        cps = [_rcopy(g_refs[i].at[:, pl.ds((1 - c) * halves[i], halves[i]), :], o_refs[i], send_sems, recv_sems, i,
                      (x, y, 1 - c)) for i in range(n)]
        for cp in cps:
            cp.start()
        for cp in cps:
            cp.wait()

    return _comm_call(body, name, gs, [SDS((N_CHIPS, h, g.shape[2]), g.dtype) for g, h in zip(gs, halves)], n)


def _swap_reduced(rs, name):
    n = len(rs)

    def body(*refs):
        r_refs, o_refs = refs[:n], refs[n:2 * n]
        send_sems, recv_sems = refs[2 * n:]
        x, y, c = _place()
        cps = [_rcopy(r_refs[i], o_refs[i], send_sems, recv_sems, i, (x, y, 1 - c)) for i in range(n)]
        for cp in cps:
            cp.start()
        for cp in cps:
            cp.wait()

    return _comm_call(body, name, rs, [SDS(r.shape, r.dtype) for r in rs], n)


def _all_sum_small(vec, name):
    n_dev = 8
    flips = [(bx, by, bc) for bx in (0, 1) for by in (0, 1) for bc in (0, 1)][1:]

    def body(v_ref, out_ref, gath, send_sems, recv_sems):
        x, y, c = _place()
        me = 4 * x + 2 * y + c
        gath[me] = v_ref[...]
        sent = []
        for k, (bx, by, bc) in enumerate(flips):
            peer = (x ^ bx, y ^ by, c ^ bc)
            cp = _rcopy(v_ref, gath.at[me], send_sems, recv_sems, k, peer)
            cp.start()
            sent.append(cp)
        for k, (bx, by, bc) in enumerate(flips):
            peer = (x ^ bx, y ^ by, c ^ bc)
            _rcopy(v_ref, gath.at[4 * peer[0] + 2 * peer[1] + peer[2]], send_sems, recv_sems, k, peer).wait_recv()
        for cp in sent:
            cp.wait_send()
        acc = gath[0]
        for d in range(1, n_dev):
            acc = acc + gath[d]
        out_ref[...] = acc

    vm = pl.BlockSpec(memory_space=pltpu.VMEM)
    return pl.pallas_call(
        body, name=name, in_specs=[vm], out_specs=vm, out_shape=SDS(vec.shape, F32),
        scratch_shapes=[pltpu.VMEM((n_dev,) + vec.shape, F32), pltpu.SemaphoreType.DMA((7,)),
                        pltpu.SemaphoreType.DMA((7,))],
        compiler_params=_cparams(has_side_effects=True),
    )(vec)


def _pack_small(vals, extra=None):
    parts = [vals[n].reshape(-1).astype(F32) for n, _ in _SMALL]
    parts.append(jnp.zeros((1,), F32) if extra is None else extra.reshape(1).astype(F32))
    flat = jnp.concatenate(parts)
    flat = jnp.concatenate([flat, jnp.zeros((_SMALL_ROWS * 128 - flat.shape[0],), F32)])
    return flat.reshape(_SMALL_ROWS, 128)


def _unpack_small(packed, shapes):
    flat = packed.reshape(-1)
    return {n: flat[_SMALL_OFF[n][0]:_SMALL_OFF[n][0] + _SMALL_OFF[n][1]].reshape(shapes[n]) for n, _ in _SMALL}


def _pair_sum(gs, gots, core, name):
    n = len(gs)

    def kern(c_ref, *refs):
        for i in range(n):
            refs[2 * n + i][...] = (refs[i][...].astype(F32) + refs[n + i][...].astype(F32)).astype(BF16)

    in_specs = [pl.BlockSpec((1, t.shape[1], t.shape[2]), lambda s, c_ref: (s, c_ref[0], 0)) for t in gots]
    in_specs += [pl.BlockSpec((1, t.shape[1], t.shape[2]), lambda s, c_ref: (s, 0, 0)) for t in gots]
    out_specs = [pl.BlockSpec((1, t.shape[1], t.shape[2]), lambda s, c_ref: (s, 0, 0)) for t in gots]
    return pl.pallas_call(
        kern, name=name,
        grid_spec=pltpu.PrefetchScalarGridSpec(num_scalar_prefetch=1, grid=(N_CHIPS,), in_specs=in_specs,
                                               out_specs=out_specs),
        out_shape=[SDS(t.shape, BF16) for t in gots],
        compiler_params=_cparams(dimension_semantics=("arbitrary",)),
    )(core.reshape(1).astype(jnp.int32), *gs, *gots)


def _chip_sum(qs, name):
    n = len(qs)

    def kern(*refs):
        for i in range(n):
            acc = refs[i][0].astype(F32)
            for s in range(1, N_CHIPS):
                acc = acc + refs[i][s].astype(F32)
            refs[n + i][...] = acc

    in_specs = [pl.BlockSpec((N_CHIPS, q.shape[1] // 2, q.shape[2]), lambda j: (0, j, 0)) for q in qs]
    out_specs = [pl.BlockSpec((q.shape[1] // 2, q.shape[2]), lambda j: (j, 0)) for q in qs]
    return _pc(kern, name, (2,), in_specs, out_specs, [SDS(q.shape[1:], F32) for q in qs])(*qs)


def _adam_math(w_, g_, m_, v_):
    m_ = ADAM_B1 * m_ + (1.0 - ADAM_B1) * g_
    v_ = ADAM_B2 * v_ + (1.0 - ADAM_B2) * jnp.square(g_)
    m_hat = m_ / (1.0 - ADAM_B1 ** ADAM_STEP)
    v_hat = v_ / (1.0 - ADAM_B2 ** ADAM_STEP)
    return -ADAM_LR * (m_hat / (jnp.sqrt(v_hat) + ADAM_EPS) + ADAM_WD * w_), m_, v_


def _adamw(w, g, m, v, name):
    rows, cols = w.shape
    tr = rows
    for cand in (256, 128, 64, 32, 16, 8):
        if rows % cand == 0 and rows > cand:
            tr = cand
            break

    def kern(w_ref, g_ref, m_ref, v_ref, d_ref, nm_ref, nv_ref):
        d_ref[...], nm_ref[...], nv_ref[...] = _adam_math(w_ref[...], g_ref[...], m_ref[...], v_ref[...])

    spec = pl.BlockSpec((tr, cols), lambda i: (i, 0))
    return _pc(kern, name, (rows // tr,), [spec] * 4, [spec] * 3, [SDS(w.shape, F32)] * 3)(w, g, m, v)


def _adamw_rows1(w, g, m, v, name):
    rows, _, cols = w.shape
    tr = next(t for t in (203, 174, 128, 64, 42, 32, 29, 16, 8, 7, 6, 4, 3, 2, 1) if rows % t == 0)

    def kern(w_ref, g_ref, m_ref, v_ref, go_ref, d_ref, nm_ref, nv_ref):
        g_ = g_ref[...]
        go_ref[...] = g_
        d_ref[...], nm_ref[...], nv_ref[...] = _adam_math(w_ref[...], g_, m_ref[...], v_ref[...])

    spec = pl.BlockSpec((tr, 1, cols), lambda i: (i, 0, 0))
    return _pc(kern, name, (rows // tr,), [spec] * 4, [spec] * 4, [SDS(w.shape, F32)] * 4)(w, g, m, v)


def _adamw_big(w, mine, theirs, m, v, core, name):
    _, rows, cols = w.shape
    half = rows // 2
    tr = next(t for t in (256, 176, 128, 64, 32, 16, 8) if half % t == 0)
    nbh = half // tr

    def kern(c_ref, w_ref, a_ref, b_ref, m_ref, v_ref, g_ref, d_ref, nm_ref, nv_ref):
        g_ = jnp.where(pl.program_id(0) // nbh == c_ref[0], a_ref[...], b_ref[...])
        g_ref[0] = g_
        d_ref[0], nm_ref[0], nv_ref[0] = _adam_math(w_ref[0], g_, m_ref[0], v_ref[0])

    full = pl.BlockSpec((1, tr, cols), lambda i, c_ref: (0, i, 0))
    part = pl.BlockSpec((tr, cols), lambda i, c_ref: (i % nbh, 0))
    return pl.pallas_call(
        kern, name=name,
        grid_spec=pltpu.PrefetchScalarGridSpec(num_scalar_prefetch=1, grid=(rows // tr,),
                                               in_specs=[full, part, part, full, full], out_specs=[full] * 4),
        out_shape=[SDS(w.shape, F32)] * 4,
        compiler_params=_cparams(dimension_semantics=("arbitrary",)),
    )(core.reshape(1).astype(jnp.int32), w, mine, theirs, m, v)


_WEIGHT_NAMES = ("attn_norm", "w_in", "dn_conv", "dn_a_log", "dn_dt_bias", "dn_out_norm", "swa_q_norm", "swa_k_norm",
                 "swa_sinks", "rel_bias", "w_branch_dn", "w_branch_swa", "w_out", "ffn_norm", "w_gate", "w_up",
                 "w_down")
_CONV_SH = QKVW // N_CHIPS


def kernel(x, attn_norm, w_in, dn_conv, dn_a_log, dn_dt_bias, dn_out_norm, swa_q_norm, swa_k_norm, swa_sinks, rel_bias, w_branch_dn, w_branch_swa, w_out, ffn_norm, w_gate, w_up, w_down, loss_target, m_attn_norm, m_w_in, m_dn_conv, m_dn_a_log, m_dn_dt_bias, m_dn_out_norm, m_swa_q_norm, m_swa_k_norm, m_swa_sinks, m_rel_bias, m_w_branch_dn, m_w_branch_swa, m_w_out, m_ffn_norm, m_w_gate, m_w_up, m_w_down, v_attn_norm, v_w_in, v_dn_conv, v_dn_a_log, v_dn_dt_bias, v_dn_out_norm, v_swa_q_norm, v_swa_k_norm, v_swa_sinks, v_rel_bias, v_w_branch_dn, v_w_branch_swa, v_w_out, v_ffn_norm, v_w_gate, v_w_up, v_w_down):
    w = dict(attn_norm=attn_norm, w_in=w_in, dn_conv=dn_conv, dn_a_log=dn_a_log, dn_dt_bias=dn_dt_bias,
             dn_out_norm=dn_out_norm, swa_q_norm=swa_q_norm, swa_k_norm=swa_k_norm, swa_sinks=swa_sinks,
             rel_bias=rel_bias, w_branch_dn=w_branch_dn, w_branch_swa=w_branch_swa, w_out=w_out, ffn_norm=ffn_norm,
             w_gate=w_gate, w_up=w_up, w_down=w_down)
    m = dict(attn_norm=m_attn_norm, w_in=m_w_in, dn_conv=m_dn_conv, dn_a_log=m_dn_a_log, dn_dt_bias=m_dn_dt_bias,
             dn_out_norm=m_dn_out_norm, swa_q_norm=m_swa_q_norm, swa_k_norm=m_swa_k_norm, swa_sinks=m_swa_sinks,
             rel_bias=m_rel_bias, w_branch_dn=m_w_branch_dn, w_branch_swa=m_w_branch_swa, w_out=m_w_out,
             ffn_norm=m_ffn_norm, w_gate=m_w_gate, w_up=m_w_up, w_down=m_w_down)
    v = dict(attn_norm=v_attn_norm, w_in=v_w_in, dn_conv=v_dn_conv, dn_a_log=v_dn_a_log, dn_dt_bias=v_dn_dt_bias,
             dn_out_norm=v_dn_out_norm, swa_q_norm=v_swa_q_norm, swa_k_norm=v_swa_k_norm, swa_sinks=v_swa_sinks,
             rel_bias=v_rel_bias, w_branch_dn=v_w_branch_dn, w_branch_swa=v_w_branch_swa, w_out=v_w_out,
             ffn_norm=v_ffn_norm, w_gate=v_w_gate, w_up=v_w_up, w_down=v_w_down)
    shapes = {n: w[n].shape for n in _WEIGHT_NAMES}

    def two_d(a):
        return a.reshape(a.shape[-2], a.shape[-1]) if a.ndim == 3 else a

    core = lax.axis_index("c")
    chip = 2 * lax.axis_index("x") + lax.axis_index("y")
    small_shapes = {n: two_d(w[n]).shape for n, _ in _SMALL}
    small_shapes["dn_conv"] = (CONV, QKVW)

    conv_loc = two_d(w["dn_conv"])
    conv_part = lax.dynamic_update_slice(jnp.zeros((CONV, QKVW), F32), jnp.where(core == 0, conv_loc, 0.0),
                                         (0, chip * _CONV_SH))
    conv_full = _all_sum_small(conv_part.reshape(CONV * QKVW // 128, 128), "gather_conv").reshape(CONV, QKVW)

    flipped = ("w_gate", "w_up")

    def natural(a, n):
        return a.transpose(0, 2, 1) if n in flipped else a

    w_bf = [two_d(natural(w[n], n).astype(BF16)) for n in _BIG_NAMES]
    (w_in_g,) = _gather_weights(w_bf[:1], chip)
    windows = _gather_windows(w_bf[1:])
    after_sync = w_in_g[0, :8, :128].astype(F32) + conv_full[0:1, :128]
    send_sems, recv_sems, w_thru, l_thru, token = _split_start(
        "gather_start", w_bf[1:], _own_slot(w_bf[1:], chip), after_sync, windows)

    def late(after):
        lands = _split_wait("gather_wait", w_thru, l_thru, send_sems, recv_sems, after, windows)
        g = dict(zip(_BIG_NAMES[1:], _sibling_fill(lands)))
        return dict(wa=g["w_branch_dn"], wb=g["w_branch_swa"], w_out=g["w_out"].reshape(D, D), wg=g["w_gate"],
                    wu=g["w_up"], wd=g["w_down"])

    wts = dict(w_in_p=_w_in_to_padded(w_in_g), dn_conv=conv_full, late=late)
    for n, _ in _SMALL[:-1]:
        wts[n] = two_d(w[n])
    wts["attn_norm"] = wts["attn_norm"] + token[0:1, 0:1]

    early = {}

    ffn = {}

    def send_ffn(grads):
        gs = [grads["w_gate"], grads["w_up"], grads["w_down"]]
        lands = [lax.empty((N_CHIPS, g.shape[1] // 2, g.shape[2]), g.dtype) for g in gs]
        ffn["sems"], ffn["recv"], ffn["src"], ffn["land"], tok = _split_start(
            "swap_ffn_start", gs, lands, gs[0][0, :8, :128], _swap_windows(gs), _sibling_peer, 1)
        return tok

    def send_early(grads):
        small = [grads["w_branch_dn"], grads["w_branch_swa"], grads["w_out"].reshape(N_CHIPS, CSH, D)]
        big = [grads["w_gate"], grads["w_up"], grads["w_down"]]
        big, got_big = _split_wait("swap_ffn_wait", ffn["src"], ffn["land"], ffn["sems"], ffn["recv"], small[0],
                                   _swap_windows(big), _sibling_peer, with_sources=True)
        gots = list(_swap_halves(small, "swap_halves_early")) + list(got_big)
        parts = _pair_sum(small + list(big), gots, core, "pair_sum_early")
        own = [lax.dynamic_index_in_dim(p, chip, axis=0, keepdims=False) for p in parts]
        early["sems"], early["recv"], early["src"], early["land"], tok = _split_start(
            "exchange_start", parts, _own_slot(own, chip), parts[0][0, :8, :128], _exchange_windows())
        return tok

    last = {}

    def send_in(g_in_p):
        g_in = [_padded_to_w_in(g_in_p)]
        parts = _pair_sum(g_in, _swap_halves(g_in, "swap_halves_in"), core, "pair_sum_in")
        own = [lax.dynamic_index_in_dim(p, chip, axis=0, keepdims=False) for p in parts]
        last["sems"], last["recv"], last["src"], last["land"], tok = _split_start(
            "exchange_in_start", parts, _own_slot(own, chip), parts[0][0, :8, :128], _exchange_windows())
        return tok

    wts["send_ffn"] = send_ffn
    wts["send_early"] = send_early
    wts["send_in"] = send_in
    loss_sum, grad_x, grads = _local_step(x[0], loss_target[0], wts)

    small_sum = _all_sum_small(_pack_small(grads, loss_sum), "all_sum_small")
    loss = small_sum.reshape(-1)[_LOSS_OFF]
    g_small = _unpack_small(small_sum, small_shapes)

    q_early = _split_wait("exchange_wait", early["src"], early["land"], early["sems"], early["recv"], small_sum,
                          _exchange_windows())
    red_early = _chip_sum(list(q_early), "chip_sum_early")
    their_early = _swap_reduced(red_early, "swap_reduced_early")
    g_out, d_out, m_out, v_out = {}, {}, {}, {}
    for n, mine, other in zip(_BIG_NAMES[1:], red_early, their_early):
        res = _adamw_big(natural(w[n], n), mine, other, natural(m[n], n), natural(v[n], n), core, "adamw_" + n)
        g_out[n], d_out[n], m_out[n], v_out[n] = (natural(t, n) for t in res)

    q_in = _split_wait("exchange_in_wait", last["src"], last["land"], last["sems"], last["recv"],
                       d_out[_BIG_NAMES[-1]], _exchange_windows())
    reduced = _chip_sum(list(q_in), "chip_sum_in")
    theirs = _swap_reduced(reduced, "swap_reduced_in")

    def rows1(a):
        return a.transpose(2, 0, 1)

    def unrows1(a):
        return a.transpose(1, 2, 0)

    g_in_blk = jnp.concatenate([jnp.where(core == 0, reduced[0], theirs[0]),
                                jnp.where(core == 0, theirs[0], reduced[0])], axis=0)
    g_in_r = rows1(g_in_blk[None])
    res = _adamw_rows1(rows1(w["w_in"]), g_in_r, rows1(m["w_in"]), rows1(v["w_in"]), "adamw_w_in")
    g_out["w_in"], d_out["w_in"], m_out["w_in"], v_out["w_in"] = (unrows1(t) for t in res)
    g_conv = lax.dynamic_slice(g_small["dn_conv"], (0, chip * _CONV_SH), (CONV, _CONV_SH))
    g_out["dn_conv"] = g_conv.reshape(shapes["dn_conv"])
    d_, m_, v_ = _adamw(conv_loc, g_conv, two_d(m["dn_conv"]), two_d(v["dn_conv"]), "adamw_dn_conv")
    d_out["dn_conv"], m_out["dn_conv"], v_out["dn_conv"] = (t.reshape(shapes["dn_conv"]) for t in (d_, m_, v_))

    def packed(src):
        vals = {n: src[n] for n, _ in _SMALL[:-1]}
        vals["dn_conv"] = jnp.zeros((CONV * QKVW,), F32)
        return _pack_small(vals)

    d_s, m_s, v_s = _adamw(packed(w), small_sum, packed(m), packed(v), "adamw_small")
    d_small, m_small, v_small = (_unpack_small(t, small_shapes) for t in (d_s, m_s, v_s))
    for n, _ in _SMALL[:-1]:
        g_out[n] = g_small[n].reshape(shapes[n])
        d_out[n], m_out[n], v_out[n] = (t[n].reshape(shapes[n]) for t in (d_small, m_small, v_small))

    return (loss, grad_x[None], *[g_out[n] for n in _WEIGHT_NAMES], *[d_out[n] for n in _WEIGHT_NAMES],
            *[m_out[n] for n in _WEIGHT_NAMES], *[v_out[n] for n in _WEIGHT_NAMES])
```

```python
import functools
import math

import numpy as np
import jax
import jax.numpy as jnp
from jax import lax
from jax.experimental import pallas as pl
from jax.experimental.pallas import tpu as pltpu

F32 = jnp.float32
BF16 = jnp.bfloat16
SDS = jax.ShapeDtypeStruct

D = 1024
DN_H = 4
DH = 128
DNW = DN_H * DH
QKVW = 3 * DNW
CONV = 4
CHUNK = 64
SWA_H = 8
SWA_KV = 2
SWA_G = SWA_H // SWA_KV
SWA_D = 64
SWAW = SWA_H * SWA_D
SWAKW = SWA_KV * SWA_D
BLK = 128
NBUCKET = 32
MAXDIST = 128
DFF = 2816
D_IN = QKVW + DNW + 2 * DN_H + SWAW + 2 * SWAKW + 2 * D
EPS = 1e-6
NEG = -1e30

ADAM_LR = 0.001
ADAM_B1 = 0.9
ADAM_B2 = 0.999
ADAM_EPS = 1e-08
ADAM_WD = 0.01
ADAM_STEP = 10

C_QKV, C_Z, C_GATE, C_SQ, C_SK, C_SV, C_BA = 0, 1536, 2048, 4096, 4608, 4736, 4864
PW = 5120
_ORIG_PIECES = (
    (0, QKVW, C_QKV),
    (QKVW, DNW, C_Z),
    (QKVW + DNW, 2 * DN_H, C_BA),
    (QKVW + DNW + 2 * DN_H, SWAW, C_SQ),
    (QKVW + DNW + 2 * DN_H + SWAW, SWAKW, C_SK),
    (QKVW + DNW + 2 * DN_H + SWAW + SWAKW, SWAKW, C_SV),
    (QKVW + DNW + 2 * DN_H + SWAW + 2 * SWAKW, 2 * D, C_GATE),
)

N_CHIPS = 4
FSH = DFF // N_CHIPS
CSH = D // N_CHIPS
VMEM_LIMIT = 48 * 1024 * 1024
MESH = pl.DeviceIdType.MESH

_BIG = (
    ("w_in", D, D_IN // N_CHIPS),
    ("w_branch_dn", DNW, CSH),
    ("w_branch_swa", SWAW, CSH),
    ("w_out", CSH, D),
    ("w_gate", FSH, D),
    ("w_up", FSH, D),
    ("w_down", FSH, D),
)
_BIG_NAMES = tuple(n for n, _, _ in _BIG)

_SMALL = (
    ("attn_norm", D), ("ffn_norm", D), ("dn_out_norm", DH), ("swa_q_norm", SWA_D), ("swa_k_norm", SWA_D),
    ("swa_sinks", SWA_H), ("dn_a_log", DN_H), ("dn_dt_bias", DN_H), ("rel_bias", NBUCKET * SWA_H),
    ("dn_conv", CONV * QKVW),
)
_SMALL_OFF = {}
_o = 0
for _n, _s in _SMALL:
    _SMALL_OFF[_n] = (_o, _s)
    _o += _s
_LOSS_OFF = _o
_SMALL_ROWS = -(-(_o + 1) // (8 * 128)) * 8


def _cparams(**kw):
    return pltpu.CompilerParams(vmem_limit_bytes=VMEM_LIMIT, **kw)


_DIMS = {
    "nn": (((1,), (0,)), ((), ())),
    "nt": (((1,), (1,)), ((), ())),
    "tn": (((0,), (0,)), ((), ())),
    "bnn": (((2,), (1,)), ((0,), (0,))),
    "bnt": (((2,), (2,)), ((0,), (0,))),
    "btn": (((1,), (1,)), ((0,), (0,))),
}


def _raw_dot(a, b, kind, exact):
    if exact:
        prec = lax.Precision.HIGH if exact == "x3" else lax.Precision.HIGHEST
        return lax.dot_general(a, b, _DIMS[kind], precision=prec, preferred_element_type=F32)
    return lax.dot_general(a.astype(BF16), b.astype(BF16), _DIMS[kind], preferred_element_type=F32)


@functools.partial(jax.custom_vjp, nondiff_argnums=(2, 3))
def _dot(a, b, kind, exact):
    return _raw_dot(a, b, kind, exact)


def _dot_fwd(a, b, kind, exact):
    return _raw_dot(a, b, kind, exact), (a, b)


def _dot_bwd(kind, exact, res, g):
    a, b = res
    pre = kind[:-2]
    nn, nt, tn = pre + "nn", pre + "nt", pre + "tn"
    if kind == nn:
        return _dot(g, b, nt, exact), _dot(a, g, tn, exact)
    if kind == nt:
        return _dot(g, b, nn, exact), _dot(g, a, tn, exact)
    return _dot(b, g, nt, exact), _dot(a, g, nn, exact)


_dot.defvjp(_dot_fwd, _dot_bwd)


def _silu(x):
    return x * jax.nn.sigmoid(x)


def _f_rms(x, gain):
    return x * lax.rsqrt(jnp.mean(x * x, axis=-1, keepdims=True) + EPS) * gain


def _f_dn_pre(xs0, xs1, xs2, xs3, ba, cw, alog, dtb):
    rows = xs0.shape[0]
    c = xs0 * cw[0:1] + xs1 * cw[1:2] + xs2 * cw[2:3] + xs3 * cw[3:4]
    qkv = _silu(c)
    qs, ks, bbs, gbs = [], [], [], []
    for h in range(DN_H):
        qh = qkv[:, h * DH:(h + 1) * DH]
        kh = qkv[:, DNW + h * DH:DNW + (h + 1) * DH]
        qs.append(qh * lax.rsqrt(jnp.sum(qh * qh, axis=-1, keepdims=True) + EPS) * (DH ** -0.5))
        ks.append(kh * lax.rsqrt(jnp.sum(kh * kh, axis=-1, keepdims=True) + EPS))
        beta = jax.nn.sigmoid(ba[:, h:h + 1])
        ar = ba[:, DN_H + h:DN_H + h + 1] + dtb[:, h:h + 1]
        softplus = jnp.maximum(ar, 0.0) + jnp.log1p(jnp.exp(-jnp.abs(ar)))
        g = -jnp.exp(alog[:, h:h + 1]) * softplus
        bbs.append(jnp.broadcast_to(beta, (rows, DH)))
        gbs.append(jnp.broadcast_to(g, (rows, DH)))
    return (jnp.concatenate(qs, axis=1), jnp.concatenate(ks, axis=1), qkv[:, 2 * DNW:],
            jnp.concatenate(bbs, axis=1), jnp.concatenate(gbs, axis=1))


def _f_dn_post(o, z, gain):
    ys = []
    for h in range(DN_H):
        oh = o[:, h * DH:(h + 1) * DH]
        zh = z[:, h * DH:(h + 1) * DH]
        ys.append(oh * lax.rsqrt(jnp.mean(oh * oh, axis=-1, keepdims=True) + EPS) * gain * _silu(zh))
    return jnp.concatenate(ys, axis=1)


def _f_merge(pa, pb, ga, gb):
    return jax.nn.sigmoid(ga) * pa + jax.nn.sigmoid(gb) * pb


@jax.custom_vjp
def _f_swiglu(g, u):
    return _silu(g) * u


def _f_swiglu_fwd(g, u):
    return _silu(g) * u, (g, u)


def _f_swiglu_bwd(res, d):
    g, u = res
    s = jax.nn.sigmoid(g)
    act = g * s
    return d * u * (s + act * (1.0 - s)), d * act


_f_swiglu.defvjp(_f_swiglu_fwd, _f_swiglu_bwd)


@jax.custom_vjp
def _unit_lower_inverse(a):
    c = a.shape[-1]
    eye = (lax.broadcasted_iota(jnp.int32, a.shape, 1) == lax.broadcasted_iota(jnp.int32, a.shape, 2)).astype(F32)
    p = -a
    t = eye + p
    for _ in range(max(c.bit_length() - 2, 0)):
        p = _raw_dot(p, p, "bnn", "x3")
        t = t + _raw_dot(t, p, "bnn", "x3")
    return t


def _unit_lower_inverse_fwd(a):
    t = _unit_lower_inverse(a)
    return t, t


def _unit_lower_inverse_bwd(t, g):
    return (-_raw_dot(_raw_dot(t, g, "btn", "x3"), t, "bnt", "x3"),)


_unit_lower_inverse.defvjp(_unit_lower_inverse_fwd, _unit_lower_inverse_bwd)


@jax.custom_vjp
def _known_inverse(a, t):
    return t


def _known_inverse_fwd(a, t):
    return t, t


def _known_inverse_bwd(t, g):
    return _unit_lower_inverse_bwd(t, g)[0], jnp.zeros_like(t)


_known_inverse.defvjp(_known_inverse_fwd, _known_inverse_bwd)


def _f_chunk(q, k, v, gb, bb, s, t_known=None, with_t=False):
    c = CHUNK
    nh = q.shape[0]
    ii = lax.broadcasted_iota(jnp.int32, (nh, c, c), 1)
    jj = lax.broadcasted_iota(jnp.int32, (nh, c, c), 2)
    incl = ii >= jj
    strict = ii > jj
    eye = (ii == jj).astype(F32)
    gcb = _dot(incl.astype(F32), gb, "bnn", "x3")
    gcol = gcb[:, :, :c]
    grow = jnp.swapaxes(gcol, 1, 2)
    decay = jnp.where(incl, jnp.exp(jnp.where(incl, gcol - grow, 0.0)), 0.0)
    kb = k * bb
    vb = v * bb
    a = jnp.where(strict, _dot(kb, k, "bnt", False) * decay, 0.0)
    t = _unit_lower_inverse(a) if t_known is None else _known_inverse(a, t_known)
    eg = jnp.exp(gcb)
    u = _dot(t, vb, "bnn", "x3")
    w = _dot(t, kb * eg, "bnn", "x3")
    qk = jnp.where(incl, _dot(q, k, "bnt", False) * decay, 0.0)
    qe = q * eg
    glast = gcb[:, c - 1:c, :]
    k_dec = k * jnp.exp(glast - gcb)
    e_last = jnp.exp(glast)
    outs = []
    for g in range(nh // DN_H):
        sl = slice(g * DN_H, (g + 1) * DN_H)
        v_new = u[sl] - _dot(w[sl], s, "bnn", False)
        outs.append(_dot(qe[sl], s, "bnn", False) + _dot(qk[sl], v_new, "bnn", False))
        s = s * e_last[sl] + _dot(k_dec[sl], v_new, "btn", False)
    o = jnp.concatenate(outs, axis=0)
    return (o, s, t) if with_t else (o, s)


def _f_swa(q8, kp, kc, vp, vc, bias8, qg, kg, sink, mask):
    kb = jnp.concatenate([kp, kc], axis=1)
    vb = jnp.concatenate([vp, vc], axis=1)
    kn = kb * lax.rsqrt(jnp.mean(kb * kb, axis=-1, keepdims=True) + EPS) * kg

    def rows(per_head):
        return jnp.stack([jnp.concatenate([per_head(kv, g) for g in range(SWA_G)], axis=0)
                          for kv in range(SWA_KV)], axis=0)

    qq = rows(lambda kv, g: q8[kv * SWA_G + g])
    qn = qq * lax.rsqrt(jnp.mean(qq * qq, axis=-1, keepdims=True) + EPS) * qg * (SWA_D ** -0.5)
    lg = _dot(qn, kn, "bnt", False) + rows(lambda kv, g: bias8[kv * SWA_G + g])
    lg = jnp.where(rows(lambda kv, g: mask), lg, NEG)
    sk = rows(lambda kv, g: jnp.broadcast_to(sink[kv][:, g:g + 1], (BLK, 1)))
    m = lax.stop_gradient(jnp.maximum(jnp.max(lg, axis=-1, keepdims=True), sk))
    p = jnp.exp(lg - m)
    den = jnp.sum(p, axis=-1, keepdims=True) + jnp.exp(sk - m)
    out = _dot(p * (1.0 / den), vb, "bnn", False)
    return jnp.stack([out[kv, g * BLK:(g + 1) * BLK] for kv in range(SWA_KV) for g in range(SWA_G)], axis=0)


def _bdot(a, b, kind="nn"):
    return lax.dot_general(a.astype(BF16), b.astype(BF16), _DIMS[kind], preferred_element_type=F32)


def _pc(kern, name, grid, in_specs, out_specs, out_shape, scratch=()):
    return pl.pallas_call(
        kern, name=name, grid=grid, in_specs=in_specs, out_specs=out_specs, out_shape=out_shape,
        scratch_shapes=list(scratch), compiler_params=_cparams(dimension_semantics=("arbitrary",) * len(grid)))


def _mm(a, b, kind, out_dtype, tm, tn, name):
    if kind == "tn":
        k, m = a.shape
    else:
        m, k = a.shape
    n = b.shape[0] if kind == "nt" else b.shape[1]
    tm, tn = min(tm, m), min(tn, n)
    assert m % tm == 0 and n % tn == 0, (name, a.shape, b.shape, tm, tn)

    def kern(a_ref, b_ref, o_ref):
        o_ref[...] = _bdot(a_ref[...], b_ref[...], kind).astype(o_ref.dtype)

    a_spec = pl.BlockSpec((k, tm), lambda i, j: (0, i)) if kind == "tn" else pl.BlockSpec((tm, k), lambda i, j: (i, 0))
    b_spec = pl.BlockSpec((tn, k), lambda i, j: (j, 0)) if kind == "nt" else pl.BlockSpec((k, tn), lambda i, j: (0, j))
    return _pc(kern, name, (m // tm, n // tn), [a_spec, b_spec], pl.BlockSpec((tm, tn), lambda i, j: (i, j)),
               SDS((m, n), out_dtype))(a, b)


def _rows(body, name, m, tm, row_ins, full_ins, row_outs, acc_outs=()):
    n_r, n_f, n_o, n_a = len(row_ins), len(full_ins), len(row_outs), len(acc_outs)
    assert m % tm == 0

    def kern(*refs):
        r = refs[:n_r]
        f = refs[n_r:n_r + n_f]
        o = refs[n_r + n_f:n_r + n_f + n_o]
        acc = refs[n_r + n_f + n_o:]
        outs, sums = body([x[...] for x in r], [x[...] for x in f])
        for ref, val in zip(o, outs, strict=True):
            ref[...] = val.astype(ref.dtype)
        if n_a:
            @pl.when(pl.program_id(0) == 0)
            def _():
                for ref in acc:
                    ref[...] = jnp.zeros(ref.shape, F32)

            for ref, val in zip(acc, sums, strict=True):
                ref[...] += val

    in_specs = [pl.BlockSpec((tm, w), functools.partial(lambda i, cb: (i, cb), cb=cb)) for _, w, cb in row_ins]
    in_specs += [pl.BlockSpec(x.shape, lambda i: (0, 0)) for x in full_ins]
    out_specs = [pl.BlockSpec((tm, w), lambda i: (i, 0)) for w, _ in row_outs]
    out_specs += [pl.BlockSpec(s, lambda i: (0, 0)) for s in acc_outs]
    out_shape = [SDS((m, w), dt) for w, dt in row_outs]
    out_shape += [SDS(s, F32) for s in acc_outs]
    return _pc(kern, name, (m // tm,), in_specs, out_specs, out_shape)(*[x for x, _, _ in row_ins], *full_ins)


def _whole(x):
    return (x, x.shape[1], 0)


def _resident(shape):
    return pl.BlockSpec(shape, lambda i: (0,) * len(shape), pipeline_mode=pl.Buffered(1))


def _row_pieces(tm, piece):
    piece = min(piece, tm)
    return [slice(r, r + piece) for r in range(0, tm, piece)]


def _zero_first(refs):
    @pl.when(pl.program_id(0) == 0)
    def _():
        for ref in refs:
            ref[...] = jnp.zeros(ref.shape, F32)


GROUP = 4


def _heads(ref):
    return jnp.stack([ref[g * CHUNK:(g + 1) * CHUNK, h * DH:(h + 1) * DH]
                      for g in range(GROUP) for h in range(DN_H)], axis=0)


def _unheads(ref, val):
    for g in range(GROUP):
        for h in range(DN_H):
            ref[g * CHUNK:(g + 1) * CHUNK, h * DH:(h + 1) * DH] = val[g * DN_H + h]


def _dn_chunks_fwd(q, k, v, gb, bb):
    s_len = q.shape[0]
    ng = s_len // (GROUP * CHUNK)

    def kern(q_ref, k_ref, v_ref, g_ref, b_ref, o_ref, sall_ref, t_ref, state):
        _zero_first([state])
        s = state[...]
        sall_ref[0] = s
        o, s_new, t = _f_chunk(*[_heads(r) for r in (q_ref, k_ref, v_ref, g_ref, b_ref)], s, with_t=True)
        _unheads(o_ref, o)
        t_ref[0] = t
        state[...] = s_new

    blk = pl.BlockSpec((GROUP * CHUNK, DNW), lambda c: (c, 0))
    return _pc(kern, "dn_chunks_fwd", (ng,), [blk] * 5,
               [blk, pl.BlockSpec((1, DN_H, DH, DH), lambda c: (c, 0, 0, 0)),
                pl.BlockSpec((1, GROUP * DN_H, CHUNK, CHUNK), lambda c: (c, 0, 0, 0))],
               [SDS((s_len, DNW), F32), SDS((ng, DN_H, DH, DH), F32), SDS((ng, GROUP * DN_H, CHUNK, CHUNK), F32)],
               scratch=[pltpu.VMEM((DN_H, DH, DH), F32)])(q, k, v, gb, bb)


def _dn_chunks_bwd(q, k, v, gb, bb, s_all, t_all, d_o):
    s_len = q.shape[0]
    ng = s_len // (GROUP * CHUNK)

    def kern(q_ref, k_ref, v_ref, g_ref, b_ref, sall_ref, t_ref, do_ref, dq_ref, dk_ref, dv_ref, dg_ref, db_ref,
             dstate):
        _zero_first([dstate])
        fn = functools.partial(_f_chunk, t_known=t_ref[0])
        _, vjp = jax.vjp(fn, *[_heads(r) for r in (q_ref, k_ref, v_ref, g_ref, b_ref)], sall_ref[0])
        *d_ins, ds = vjp((_heads(do_ref), dstate[...]))
        for ref, val in zip((dq_ref, dk_ref, dv_ref, dg_ref, db_ref), d_ins, strict=True):
            _unheads(ref, val)
        dstate[...] = ds

    blk = pl.BlockSpec((GROUP * CHUNK, DNW), lambda c: (ng - 1 - c, 0))
    return _pc(kern, "dn_chunks_bwd", (ng,),
               [blk] * 5 + [pl.BlockSpec((1, DN_H, DH, DH), lambda c: (ng - 1 - c, 0, 0, 0)),
                            pl.BlockSpec((1, GROUP * DN_H, CHUNK, CHUNK), lambda c: (ng - 1 - c, 0, 0, 0)), blk],
               [blk] * 5, [SDS((s_len, DNW), F32)] * 5,
               scratch=[pltpu.VMEM((DN_H, DH, DH), F32)])(q, k, v, gb, bb, s_all, t_all, d_o)


def _t5_bucket_table():
    qi = np.arange(BLK)[:, None]
    kj = np.arange(2 * BLK)[None, :]
    dist = BLK + qi - kj
    n = np.maximum(dist, 0)
    max_exact = NBUCKET // 2
    nf = np.maximum(n, 1).astype(np.float32)
    large = max_exact + (np.log(nf / np.float32(max_exact)) / np.float32(math.log(MAXDIST / max_exact))
                         * np.float32(NBUCKET - max_exact)).astype(np.int32)
    large = np.minimum(large, NBUCKET - 1)
    return np.where(n < max_exact, n, large)


def _bucket_onehot_t():
    table = _t5_bucket_table().reshape(-1)
    return (np.arange(NBUCKET)[:, None] == table[None, :]).astype(np.float32)


def _swa_mask(first):
    qi = lax.broadcasted_iota(jnp.int32, (BLK, 2 * BLK), 0)
    kj = lax.broadcasted_iota(jnp.int32, (BLK, 2 * BLK), 1)
    dist = BLK + qi - kj
    window = (dist >= 0) & (dist < BLK)
    return window & ((kj >= BLK) | jnp.logical_not(first))


def _bias_expand(rel_bias_t):
    onehot = jnp.asarray(_bucket_onehot_t())

    def kern(r_ref, oh_ref, o_ref):
        o_ref[...] = _raw_dot(r_ref[...], oh_ref[...], "nn", True)

    return pl.pallas_call(
        kern, name="bias_expand", out_shape=SDS((SWA_H, BLK * 2 * BLK), F32), compiler_params=_cparams(),
    )(rel_bias_t, onehot)


def _bias_reduce(d_bias_flat):
    onehot = jnp.asarray(_bucket_onehot_t())

    def kern(d_ref, oh_ref, o_ref):
        o_ref[...] = _raw_dot(d_ref[...], oh_ref[...], "nt", True)

    return pl.pallas_call(
        kern, name="bias_reduce", out_shape=SDS((SWA_H, NBUCKET), F32), compiler_params=_cparams(),
    )(d_bias_flat, onehot)


def _swa_specs(nb, rev):
    def blk(n):
        return (nb - 1 - n) if rev else n

    def before(n):
        return jnp.maximum(blk(n) - 1, 0)

    q_spec = pl.BlockSpec((BLK, SWAW), lambda n: (blk(n), C_SQ // SWAW))
    k_cur = pl.BlockSpec((BLK, SWAKW), lambda n: (blk(n), C_SK // SWAKW))
    k_prev = pl.BlockSpec((BLK, SWAKW), lambda n: (before(n), C_SK // SWAKW))
    v_cur = pl.BlockSpec((BLK, SWAKW), lambda n: (blk(n), C_SV // SWAKW))
    v_prev = pl.BlockSpec((BLK, SWAKW), lambda n: (before(n), C_SV // SWAKW))
    bias = pl.BlockSpec((SWA_H, BLK, 2 * BLK), lambda n: (0, 0, 0))
    gain = pl.BlockSpec((1, SWA_D), lambda n: (0, 0))
    sink = pl.BlockSpec((SWA_KV, 1, SWA_G), lambda n: (0, 0, 0))
    wide = pl.BlockSpec((BLK, SWAW), lambda n: (blk(n), 0))
    narrow = pl.BlockSpec((BLK, SWAKW), lambda n: (blk(n), 0))
    return [q_spec, k_prev, k_cur, v_prev, v_cur, bias, gain, gain, sink], wide, narrow


def _split_heads(x):
    return jnp.stack([x[:, h * SWA_D:(h + 1) * SWA_D] for h in range(x.shape[1] // SWA_D)], axis=0)


def _join_heads(x):
    return jnp.concatenate([x[h] for h in range(x.shape[0])], axis=1)


def _swa_fwd(proj, bias, qg, kg, sinks):
    s_len = proj.shape[0]
    nb = s_len // BLK
    in_specs, wide, _ = _swa_specs(nb, False)

    def kern(q_ref, kp_ref, kc_ref, vp_ref, vc_ref, b_ref, qg_ref, kg_ref, s_ref, o_ref):
        mask = _swa_mask(pl.program_id(0) == 0)
        o8 = _f_swa(*[_split_heads(r[...]) for r in (q_ref, kp_ref, kc_ref, vp_ref, vc_ref)], b_ref[...], qg_ref[...],
                    kg_ref[...], s_ref[...], mask)
        o_ref[...] = _join_heads(o8).astype(BF16)

    return _pc(kern, "swa_fwd", (nb,), in_specs, wide, SDS((s_len, SWAW), BF16))(
        proj, proj, proj, proj, proj, bias, qg, kg, sinks)


def _swa_bwd(proj, bias, qg, kg, sinks, d_out):
    s_len = proj.shape[0]
    nb = s_len // BLK
    in_specs, wide, narrow = _swa_specs(nb, True)

    def kern(q_ref, kp_ref, kc_ref, vp_ref, vc_ref, b_ref, qg_ref, kg_ref, s_ref, do_ref,
             dq_ref, dk_ref, dv_ref, db_ref, dqg_ref, dkg_ref, ds_ref, carry_k, carry_v):
        n = pl.program_id(0)
        mask = _swa_mask(n == nb - 1)
        _zero_first([carry_k, carry_v, db_ref, ds_ref, dqg_ref, dkg_ref])
        fn = functools.partial(_f_swa, mask=mask)
        _, vjp = jax.vjp(fn, *[_split_heads(r[...]) for r in (q_ref, kp_ref, kc_ref, vp_ref, vc_ref)], b_ref[...],
                         qg_ref[...], kg_ref[...], s_ref[...])
        dq, dkp, dkc, dvp, dvc, dbias, dqg, dkg, dsink = vjp(_split_heads(do_ref[...]))
        dq_ref[...] = _join_heads(dq).astype(BF16)
        dk_ref[...] = (_join_heads(dkc) + carry_k[...]).astype(BF16)
        dv_ref[...] = (_join_heads(dvc) + carry_v[...]).astype(BF16)
        carry_k[...] = _join_heads(dkp)
        carry_v[...] = _join_heads(dvp)
        db_ref[...] += dbias
        dqg_ref[...] += dqg
        dkg_ref[...] += dkg
        ds_ref[...] += dsink

    bias_spec, gain, sink = in_specs[5], in_specs[6], in_specs[8]
    return _pc(
        kern, "swa_bwd", (nb,), in_specs + [wide], [wide, narrow, narrow, bias_spec, gain, gain, sink],
        [SDS((s_len, SWAW), BF16), SDS((s_len, SWAKW), BF16), SDS((s_len, SWAKW), BF16),
         SDS((SWA_H, BLK, 2 * BLK), F32), SDS((1, SWA_D), F32), SDS((1, SWA_D), F32), SDS((SWA_KV, 1, SWA_G), F32)],
        scratch=[pltpu.VMEM((BLK, SWAKW), F32), pltpu.VMEM((BLK, SWAKW), F32)],
    )(proj, proj, proj, proj, proj, bias, qg, kg, sinks, d_out)


def _branch_merge(y_dn, y_swa, wa, wb, proj):
    s_len = y_dn.shape[0]
    tm = min(1024, s_len)

    def kern(ya_ref, yb_ref, wa_ref, wb_ref, ga_ref, gb_ref, pa_ref, pb_ref, m_ref):
        for rows in _row_pieces(tm, 128):
            pa = _bdot(ya_ref[rows, :], wa_ref[0])
            pb = _bdot(yb_ref[rows, :], wb_ref[0])
            pa_ref[rows, :] = pa.astype(BF16)
            pb_ref[rows, :] = pb.astype(BF16)
            m_ref[rows, :] = _f_merge(pa, pb, ga_ref[rows, :], gb_ref[rows, :]).astype(BF16)

    y_spec = pl.BlockSpec((tm, DNW), lambda i, s: (i, 0))
    w_spec = pl.BlockSpec((1, DNW, CSH), lambda i, s: (s, 0, 0))
    o_spec = pl.BlockSpec((tm, CSH), lambda i, s: (i, s))
    ga_spec = pl.BlockSpec((tm, CSH), lambda i, s: (i, C_GATE // CSH + s))
    gb_spec = pl.BlockSpec((tm, CSH), lambda i, s: (i, (C_GATE + D) // CSH + s))
    return _pc(kern, "branch_merge", (s_len // tm, N_CHIPS), [y_spec, y_spec, w_spec, w_spec, ga_spec, gb_spec],
               [o_spec] * 3, [SDS((s_len, D), BF16)] * 3,
               )(y_dn, y_swa, wa, wb, proj, proj)


def _in_proj(x, gain, w_in_p):
    s_len = x.shape[0]
    tm = min(512, s_len)

    def kern(x_ref, g_ref, w_ref, h_ref, p_ref):
        h = _f_rms(x_ref[...], g_ref[...]).astype(BF16)
        h_ref[...] = h
        p_ref[...] = _bdot(h, w_ref[...])

    row = pl.BlockSpec((tm, D), lambda i: (i, 0))
    return _pc(kern, "in_proj", (s_len // tm,),
               [row, pl.BlockSpec((1, D), lambda i: (0, 0)), _resident((D, PW))],
               [row, pl.BlockSpec((tm, PW), lambda i: (i, 0))],
               [SDS((s_len, D), BF16), SDS((s_len, PW), F32)])(x, gain, w_in_p)


def _out_proj(merged, w_out, x, gain):
    s_len = x.shape[0]
    tm = min(512, s_len)

    def kern(m_ref, w_ref, x_ref, g_ref, x1_ref, h2_ref):
        x1 = x_ref[...] + _bdot(m_ref[...], w_ref[...])
        x1_ref[...] = x1
        h2_ref[...] = _f_rms(x1, g_ref[...]).astype(BF16)

    row = pl.BlockSpec((tm, D), lambda i: (i, 0))
    return _pc(kern, "out_proj", (s_len // tm,),
               [row, _resident((D, D)), row, pl.BlockSpec((1, D), lambda i: (0, 0))],
               [row, row], [SDS((s_len, D), F32), SDS((s_len, D), BF16)])(merged, w_out, x, gain)


def _ffn_up(h2, wg, wu):
    s_len = h2.shape[0]
    tm = min(2048, s_len)

    def kern(h_ref, g_ref, u_ref, gt_ref, up_ref, act_ref):
        for rows in _row_pieces(tm, 256):
            h = h_ref[rows, :]
            g = _bdot(h, g_ref[0], "nt")
            u = _bdot(h, u_ref[0], "nt")
            gt_ref[0, rows, :] = g.astype(BF16)
            up_ref[0, rows, :] = u.astype(BF16)
            act_ref[0, rows, :] = _f_swiglu(g, u).astype(BF16)

    w_spec = pl.BlockSpec((1, FSH, D), lambda s, i: (s, 0, 0))
    o_spec = pl.BlockSpec((1, tm, FSH), lambda s, i: (s, i, 0))
    shape = (N_CHIPS, s_len, FSH)
    return _pc(kern, "ffn_up", (N_CHIPS, s_len // tm), [pl.BlockSpec((tm, D), lambda s, i: (i, 0)), w_spec, w_spec],
               [o_spec] * 3, [SDS(shape, BF16)] * 3)(h2, wg, wu)


def _ffn_down_loss(act, wd, x1, target):
    s_len = x1.shape[0]
    tm = min(512, s_len)

    def kern(a_ref, w_ref, x_ref, t_ref, dy_ref, dyb_ref, loss_ref):
        _zero_first([loss_ref])
        for rows in _row_pieces(tm, 128):
            y = x_ref[rows, :]
            for s in range(N_CHIPS):
                y = y + _bdot(a_ref[s, rows, :], w_ref[s])
            d = y - t_ref[rows, :]
            dy = d * (1.0 / D)
            dy_ref[rows, :] = dy
            dyb_ref[rows, :] = dy.astype(BF16)
            loss_ref[...] += jnp.sum(d * d).reshape(1, 1) * (0.5 / D)

    row = pl.BlockSpec((tm, D), lambda i: (i, 0))
    return _pc(kern, "ffn_down_loss", (s_len // tm,),
               [pl.BlockSpec((N_CHIPS, tm, FSH), lambda i: (0, i, 0)),
                _resident((N_CHIPS, FSH, D)), row, row],
               [row, row, pl.BlockSpec((1, 1), lambda i: (0, 0))],
               [SDS((s_len, D), F32), SDS((s_len, D), BF16), SDS((1, 1), F32)])(act, wd, x1, target)


def _ffn_dact(dy_b, wd, gt, up):
    s_len = dy_b.shape[0]
    tm = min(2048, s_len)

    def kern(dy_ref, w_ref, gt_ref, up_ref, dg_ref, du_ref):
        w = w_ref[0]
        for rows in _row_pieces(tm, 256):
            d_act = _bdot(dy_ref[rows, :], w, "nt")
            _, vjp = jax.vjp(_f_swiglu, gt_ref[0, rows, :].astype(F32), up_ref[0, rows, :].astype(F32))
            dg, du = vjp(d_act)
            dg_ref[0, rows, :] = dg.astype(BF16)
            du_ref[0, rows, :] = du.astype(BF16)

    a_spec = pl.BlockSpec((1, tm, FSH), lambda s, i: (s, i, 0))
    shape = (N_CHIPS, s_len, FSH)
    return _pc(kern, "ffn_dact", (N_CHIPS, s_len // tm),
               [pl.BlockSpec((tm, D), lambda s, i: (i, 0)), pl.BlockSpec((1, FSH, D), lambda s, i: (s, 0, 0)),
                a_spec, a_spec],
               [a_spec, a_spec], [SDS(shape, BF16), SDS(shape, BF16)])(dy_b, wd, gt, up)


def _gw_ffn(lhs, rhs, name):
    s_len = rhs.shape[0]
    n = len(lhs)
    tn = 512

    def kern(*refs):
        g = refs[n][...]
        for i in range(n):
            refs[n + 1 + i][0] = _bdot(refs[i][0], g, "tn").astype(BF16)

    a_spec = pl.BlockSpec((1, s_len, FSH), lambda s, j: (s, 0, 0))
    o_spec = pl.BlockSpec((1, FSH, tn), lambda s, j: (s, 0, j))
    return _pc(kern, name, (N_CHIPS, D // tn), [a_spec] * n + [pl.BlockSpec((s_len, tn), lambda s, j: (0, j))],
               [o_spec] * n, [SDS((N_CHIPS, FSH, D), BF16)] * n)(*lhs, rhs)


def _ffn_dh2(d_gt, d_up, wg, wu, x1, dy, gain):
    s_len = x1.shape[0]
    tm = min(512, s_len)

    def kern(dg_ref, du_ref, wg_ref, wu_ref, x_ref, dy_ref, g_ref, dx_ref, dxb_ref, dgain_ref):
        _zero_first([dgain_ref])
        dh2 = jnp.zeros((tm, D), F32)
        for s in range(N_CHIPS):
            dh2 = dh2 + _bdot(dg_ref[s], wg_ref[s]) + _bdot(du_ref[s], wu_ref[s])
        _, vjp = jax.vjp(_f_rms, x_ref[...], g_ref[...])
        dx, dgain = vjp(dh2)
        dx1 = dx + dy_ref[...]
        dx_ref[...] = dx1
        dxb_ref[...] = dx1.astype(BF16)
        dgain_ref[...] += dgain

    row = pl.BlockSpec((tm, D), lambda i: (i, 0))
    d_spec = pl.BlockSpec((N_CHIPS, tm, FSH), lambda i: (0, i, 0))
    w_spec = _resident((N_CHIPS, FSH, D))
    vec = pl.BlockSpec((1, D), lambda i: (0, 0))
    return _pc(kern, "ffn_dh2", (s_len // tm,), [d_spec, d_spec, w_spec, w_spec, row, row, vec],
               [row, row, vec], [SDS((s_len, D), F32), SDS((s_len, D), BF16), SDS((1, D), F32)],
               )(d_gt, d_up, wg, wu, x1, dy, gain)


def _merge_bwd(dx1_b, w_out, pa, pb, proj):
    s_len = dx1_b.shape[0]
    tm = min(512, s_len)

    def kern(dx_ref, w_ref, pa_ref, pb_ref, g_ref, dpa_ref, dpb_ref, dg_ref):
        dm = _bdot(dx_ref[...], w_ref[...], "nt")
        gates = g_ref[...]
        _, vjp = jax.vjp(_f_merge, pa_ref[...].astype(F32), pb_ref[...].astype(F32), gates[:, :D], gates[:, D:])
        dpa, dpb, dga, dgb = vjp(dm)
        dpa_ref[...] = dpa.astype(BF16)
        dpb_ref[...] = dpb.astype(BF16)
        dg_ref[:, :D] = dga.astype(BF16)
        dg_ref[:, D:] = dgb.astype(BF16)

    row = pl.BlockSpec((tm, D), lambda i: (i, 0))
    return _pc(kern, "merge_bwd", (s_len // tm,),
               [row, _resident((D, D)), row, row,
                pl.BlockSpec((tm, 2 * D), lambda i: (i, C_GATE // (2 * D)))],
               [row, row, pl.BlockSpec((tm, 2 * D), lambda i: (i, 0))],
               [SDS((s_len, D), BF16), SDS((s_len, D), BF16), SDS((s_len, 2 * D), BF16)],
               )(dx1_b, w_out, pa, pb, proj)


def _d_branch(d_pa, d_pb, wa, wb):
    s_len = d_pa.shape[0]
    tm = min(512, s_len)

    def kern(da_ref, db_ref, wa_ref, wb_ref, oa_ref, ob_ref):
        acc_a = jnp.zeros((tm, DNW), F32)
        acc_b = jnp.zeros((tm, SWAW), F32)
        for s in range(N_CHIPS):
            acc_a = acc_a + _bdot(da_ref[:, s * CSH:(s + 1) * CSH], wa_ref[s], "nt")
            acc_b = acc_b + _bdot(db_ref[:, s * CSH:(s + 1) * CSH], wb_ref[s], "nt")
        oa_ref[...] = acc_a
        ob_ref[...] = acc_b

    row = pl.BlockSpec((tm, D), lambda i: (i, 0))
    w_spec = pl.BlockSpec((N_CHIPS, DNW, CSH), lambda i: (0, 0, 0))
    out = pl.BlockSpec((tm, DNW), lambda i: (i, 0))
    return _pc(kern, "d_branch", (s_len // tm,), [row, row, w_spec, w_spec], [out, out],
               [SDS((s_len, DNW), F32), SDS((s_len, SWAW), F32)])(d_pa, d_pb, wa, wb)


def _gw_branch(y_dn, y_swa, d_pa, d_pb):
    s_len = y_dn.shape[0]

    def kern(ya_ref, yb_ref, da_ref, db_ref, oa_ref, ob_ref):
        oa_ref[0] = _bdot(ya_ref[...], da_ref[...], "tn").astype(BF16)
        ob_ref[0] = _bdot(yb_ref[...], db_ref[...], "tn").astype(BF16)

    y_spec = pl.BlockSpec((s_len, DNW), lambda s: (0, 0))
    d_spec = pl.BlockSpec((s_len, CSH), lambda s: (0, s))
    o_spec = pl.BlockSpec((1, DNW, CSH), lambda s: (s, 0, 0))
    shape = (N_CHIPS, DNW, CSH)
    return _pc(kern, "gw_branch", (N_CHIPS,), [y_spec, y_spec, d_spec, d_spec], [o_spec, o_spec],
               [SDS(shape, BF16), SDS(shape, BF16)])(y_dn, y_swa, d_pa, d_pb)


def _dh_rms(d_proj, w_in_p, x, dx1, gain):
    s_len = x.shape[0]
    tm = min(512, s_len)

    def kern(dp_ref, w_ref, x_ref, r_ref, g_ref, gx_ref, dgain_ref):
        _zero_first([dgain_ref])
        dh = _bdot(dp_ref[...], w_ref[...], "nt")
        _, vjp = jax.vjp(_f_rms, x_ref[...], g_ref[...])
        dx, dgain = vjp(dh)
        gx_ref[...] = dx + r_ref[...]
        dgain_ref[...] += dgain

    row = pl.BlockSpec((tm, D), lambda i: (i, 0))
    vec = pl.BlockSpec((1, D), lambda i: (0, 0))
    return _pc(kern, "dh_rms", (s_len // tm,),
               [pl.BlockSpec((tm, PW), lambda i: (i, 0)), _resident((D, PW)), row, row, vec],
               [row, vec], [SDS((s_len, D), F32), SDS((1, D), F32)])(d_proj, w_in_p, x, dx1, gain)


HALO = 8


def _rows_down(x, n, above):
    tm = x.shape[0]
    r = pltpu.roll(x, n, 0)
    a = pltpu.roll(above, n, 0)
    top = jnp.where(lax.broadcasted_iota(jnp.int32, above.shape, 0) < n, a, r[0:HALO])
    return jnp.concatenate([top, r[HALO:tm]], axis=0)


def _rows_up(x, n, below):
    tm = x.shape[0]
    r = pltpu.roll(x, tm - n, 0)
    b = pltpu.roll(below, HALO - n, 0)
    bottom = jnp.where(lax.broadcasted_iota(jnp.int32, below.shape, 0) >= HALO - n, b, r[tm - HALO:tm])
    return jnp.concatenate([r[0:tm - HALO], bottom], axis=0)


def _conv_taps(cur_ref, prev_ref, first):
    cur = cur_ref[...]
    above = jnp.where(first, 0.0, prev_ref[...])
    return [_rows_down(cur, n, above) for n in range(CONV - 1, 0, -1)] + [cur]


def _dn_pre_specs(s_len, tm, blk):
    cur = pl.BlockSpec((tm, QKVW), lambda i: (blk(i), 0))
    prev = pl.BlockSpec((HALO, QKVW), lambda i: (jnp.maximum(blk(i) * (tm // HALO) - 1, 0), 0))
    ba = pl.BlockSpec((tm, 128), lambda i: (blk(i), C_BA // 128))
    row = pl.BlockSpec((tm, DNW), lambda i: (blk(i), 0))
    full = [pl.BlockSpec((CONV, QKVW), lambda i: (0, 0)), pl.BlockSpec((1, DN_H), lambda i: (0, 0)),
            pl.BlockSpec((1, DN_H), lambda i: (0, 0))]
    return cur, prev, ba, row, full


def _dn_pre_fwd(proj, conv_w, alog, dtb):
    s_len = proj.shape[0]
    tm = min(256, s_len)
    cur, prev, ba, row, full = _dn_pre_specs(s_len, tm, lambda i: i)

    def kern(cur_ref, prev_ref, ba_ref, cw_ref, al_ref, dt_ref, q_ref, k_ref, v_ref, bb_ref, gb_ref):
        xs = _conv_taps(cur_ref, prev_ref, pl.program_id(0) == 0)
        outs = _f_dn_pre(*xs, ba_ref[...], cw_ref[...], al_ref[...], dt_ref[...])
        for ref, val in zip((q_ref, k_ref, v_ref, bb_ref, gb_ref), outs, strict=True):
            ref[...] = val

    return _pc(kern, "dn_pre_fwd", (s_len // tm,), [cur, prev, ba] + full, [row] * 5,
               [SDS((s_len, DNW), F32)] * 5)(proj, proj, proj, conv_w, alog, dtb)


def _dn_pre_bwd(proj, conv_w, alog, dtb, cots, others):
    s_len = proj.shape[0]
    tm = min(256, s_len)
    nb = s_len // tm
    cur, prev, ba, row, full = _dn_pre_specs(s_len, tm, lambda i: nb - 1 - i)
    n_o = len(others)
    assert QKVW + sum(t.shape[1] for t in others) + 128 == C_BA + 128

    def kern(cur_ref, prev_ref, ba_ref, cw_ref, al_ref, dt_ref, dq_ref, dk_ref, dv_ref, dbb_ref, dgb_ref, *rest):
        o_refs = rest[:n_o]
        dproj_ref, dcw_ref, dal_ref, ddt_ref, *tails = rest[n_o:]
        i = pl.program_id(0)
        _zero_first([dcw_ref, dal_ref, ddt_ref] + tails)
        xs = _conv_taps(cur_ref, prev_ref, i == nb - 1)
        _, vjp = jax.vjp(_f_dn_pre, *xs, ba_ref[...], cw_ref[...], al_ref[...], dt_ref[...])
        *dxs, dba, dcw, dal, ddt = vjp((dq_ref[...], dk_ref[...], dv_ref[...], dbb_ref[...], dgb_ref[...]))
        total = dxs[CONV - 1]
        for j, t in enumerate(tails):
            n = CONV - 1 - j
            total = total + _rows_up(dxs[j], n, t[...])
            t[...] = dxs[j][0:HALO, :]
        dproj_ref[...] = jnp.concatenate(
            [total.astype(BF16)] + [r[...] for r in o_refs] + [dba.astype(BF16), jnp.zeros((tm, PW - C_BA - 128), BF16)],
            axis=1)
        dcw_ref[...] += dcw
        dal_ref[...] += dal
        ddt_ref[...] += ddt

    o_specs = [pl.BlockSpec((tm, t.shape[1]), lambda i: (nb - 1 - i, 0)) for t in others]
    return _pc(kern, "dn_pre_bwd", (nb,), [cur, prev, ba] + full + [row] * 5 + o_specs,
               [pl.BlockSpec((tm, PW), lambda i: (nb - 1 - i, 0))] + full,
               [SDS((s_len, PW), BF16), SDS((CONV, QKVW), F32), SDS((1, DN_H), F32), SDS((1, DN_H), F32)],
               scratch=[pltpu.VMEM((HALO, QKVW), F32)] * (CONV - 1))(proj, proj, proj, conv_w, alog, dtb, *cots, *others)


def _w_in_to_padded(w_sh):
    tr = 256

    def kern(w_ref, o_ref):
        full = jnp.concatenate([w_ref[s] for s in range(N_CHIPS)], axis=1)
        pieces = [full[:, o0:o0 + w] for o0, w, _ in sorted(_ORIG_PIECES, key=lambda t: t[2])]
        o_ref[...] = jnp.concatenate(pieces + [jnp.zeros((tr, PW - D_IN), w_ref.dtype)], axis=1)

    return _pc(kern, "w_in_to_padded", (D // tr,), [pl.BlockSpec((N_CHIPS, tr, D_IN // N_CHIPS), lambda i: (0, i, 0))],
               pl.BlockSpec((tr, PW), lambda i: (i, 0)), SDS((D, PW), w_sh.dtype))(w_sh)


def _padded_to_w_in(g):
    tr = 256
    csh = D_IN // N_CHIPS

    def kern(g_ref, o_ref):
        x = g_ref[...]
        full = jnp.concatenate([x[:, p0:p0 + w] for _, w, p0 in _ORIG_PIECES], axis=1)
        for s in range(N_CHIPS):
            o_ref[s] = full[:, s * csh:(s + 1) * csh]

    return _pc(kern, "padded_to_w_in", (D // tr,), [pl.BlockSpec((tr, PW), lambda i: (i, 0))],
               pl.BlockSpec((N_CHIPS, tr, csh), lambda i: (0, i, 0)), SDS((N_CHIPS, D, csh), g.dtype))(g)


def _local_step(x, target, wts):
    s_len = x.shape[0]
    tm = min(512, s_len)
    w_in_p = wts["w_in_p"]
    attn_gain = wts["attn_norm"]
    ffn_gain = wts["ffn_norm"]
    conv_w = wts["dn_conv"]
    alog, dtb, out_gain = wts["dn_a_log"], wts["dn_dt_bias"], wts["dn_out_norm"]
    qg, kg = wts["swa_q_norm"], wts["swa_k_norm"]
    sinks = wts["swa_sinks"].reshape(SWA_KV, 1, SWA_G)

    h, proj = _in_proj(x, attn_gain, w_in_p)
    q_dn, k_dn, v_dn, bb, gb = _dn_pre_fwd(proj, conv_w, alog, dtb)
    o_dn, s_all, t_all = _dn_chunks_fwd(q_dn, k_dn, v_dn, gb, bb)
    post_ins = [_whole(o_dn), (proj, DNW, C_Z // DNW)]
    (y_dn,) = _rows(lambda r, f: ([_f_dn_post(r[0], r[1], f[0])], []), "dn_post_fwd", s_len, tm, post_ins,
                    [out_gain], [(DNW, BF16)])

    bias = _bias_expand(wts["rel_bias"].T).reshape(SWA_H, BLK, 2 * BLK)
    y_swa = _swa_fwd(proj, bias, qg, kg, sinks)

    wts = {**wts, **wts["late"](y_swa)}
    p_a, p_b, merged = _branch_merge(y_dn, y_swa, wts["wa"], wts["wb"], proj)
    x1, h2 = _out_proj(merged, wts["w_out"], x, ffn_gain)
    gt, up, act = _ffn_up(h2, wts["wg"], wts["wu"])
    dy, dy_b, loss = _ffn_down_loss(act, wts["wd"], x1, target)

    grads = {}
    d_gt, d_up = _ffn_dact(dy_b, wts["wd"], gt, up)
    (grads["w_down"],) = _gw_ffn([act], dy_b, "gw_down")
    grads["w_gate"], grads["w_up"] = _gw_ffn([d_gt, d_up], h2, "gw_gate_up")
    token = wts["send_ffn"](grads)
    dx1, dx1_b, grads["ffn_norm"] = _ffn_dh2(d_gt, d_up, wts["wg"], wts["wu"], x1, dy,
                                             ffn_gain + token[0:1, 0:1])
    grads["w_out"] = _mm(merged, dx1_b, "tn", BF16, 512, 512, "gw_out")
    d_pa, d_pb, d_gr = _merge_bwd(dx1_b, wts["w_out"], p_a, p_b, proj)
    d_ydn, d_yswa = _d_branch(d_pa, d_pb, wts["wa"], wts["wb"])
    grads["w_branch_dn"], grads["w_branch_swa"] = _gw_branch(y_dn, y_swa, d_pa, d_pb)
    token = wts["send_early"](grads)
    qg_t = qg + token[0:1, 0:1]
    out_gain_t = out_gain + token[0:1, 0:1]

    d_sq, d_sk, d_sv, d_bias, grads["swa_q_norm"], grads["swa_k_norm"], d_sinks = _swa_bwd(
        proj, bias, qg_t, kg, sinks, d_yswa)
    grads["swa_sinks"] = d_sinks.reshape(1, SWA_H)
    grads["rel_bias"] = _bias_reduce(d_bias.reshape(SWA_H, BLK * 2 * BLK)).T

    def post_bwd(r, f):
        _, vjp = jax.vjp(_f_dn_post, r[0], r[1], f[0])
        d_o, d_z, d_gain = vjp(r[2])
        return [d_o, d_z], [d_gain]

    d_o, d_z, grads["dn_out_norm"] = _rows(post_bwd, "dn_post_bwd", s_len, tm, post_ins + [_whole(d_ydn)], [out_gain_t],
                                           [(DNW, F32), (DNW, BF16)], [(1, DH)])
    d_q, d_k, d_v, d_gb, d_bb = _dn_chunks_bwd(q_dn, k_dn, v_dn, gb, bb, s_all, t_all, d_o)

    d_proj, grads["dn_conv"], grads["dn_a_log"], grads["dn_dt_bias"] = _dn_pre_bwd(
        proj, conv_w, alog, dtb, (d_q, d_k, d_v, d_bb, d_gb), (d_z, d_gr, d_sq, d_sk, d_sv))
    grads["w_in_p"] = _mm(h, d_proj, "tn", BF16, 512, 1024, "gw_in")
    token = wts["send_in"](grads["w_in_p"])
    grad_x, grads["attn_norm"] = _dh_rms(d_proj, w_in_p, x, dx1, attn_gain + token[0:1, 0:1])
    return loss, grad_x, grads


_HBM = pl.BlockSpec(memory_space=pl.ANY)


def _place():
    return lax.axis_index("x"), lax.axis_index("y"), lax.axis_index("c")


def _other_chips(x, y):
    return [(1 - x, y), (x, 1 - y), (1 - x, 1 - y)]


def _rcopy(src, dst, send_sems, recv_sems, k, to):
    return pltpu.make_async_remote_copy(src_ref=src, dst_ref=dst, send_sem=send_sems.at[k], recv_sem=recv_sems.at[k],
                                        device_id=to, device_id_type=MESH)


def _comm_call(body, name, ins, out_shapes, n_remote, landing=0):
    first = len(ins) - landing
    return pl.pallas_call(
        body, name=name, in_specs=[_HBM] * len(ins), out_specs=[_HBM] * len(out_shapes), out_shape=out_shapes,
        scratch_shapes=[pltpu.SemaphoreType.DMA((n_remote,)), pltpu.SemaphoreType.DMA((n_remote,))],
        input_output_aliases={first + i: i for i in range(landing)},
        compiler_params=_cparams(has_side_effects=True),
    )(*ins)


def _own_slot(blocks, chip):
    return [lax.dynamic_update_slice(lax.empty((N_CHIPS,) + b.shape, b.dtype), b[None], (chip, 0, 0)) for b in blocks]


def _gather_weights(ws, chip):
    n = len(ws)
    halves = [w.shape[0] // 2 for w in ws]

    def body(*refs):
        w_refs, o_refs = refs[:n], refs[2 * n:3 * n]
        send_sems, recv_sems = refs[3 * n:]
        x, y, c = _place()
        s = 2 * x + y
        sib = (x, y, 1 - c)
        chips = _other_chips(x, y)

        def rows(i, half):
            return pl.ds(half * halves[i], halves[i])

        first = []
        for j, (cx, cy) in enumerate(chips):
            for i in range(n):
                cp = _rcopy(w_refs[i].at[rows(i, c), :], o_refs[i].at[s, rows(i, c), :], send_sems, recv_sems,
                            j * n + i, (cx, cy, c))
                cp.start()
                first.append(cp)
        passed = []
        for j, (cx, cy) in enumerate(chips):
            sj = 2 * cx + cy
            for i in range(n):
                blk = o_refs[i].at[sj, rows(i, c), :]
                _rcopy(blk, blk, send_sems, recv_sems, j * n + i, (cx, cy, c)).wait_recv()
                cp = _rcopy(blk, blk, send_sems, recv_sems, (3 + j) * n + i, sib)
                cp.start()
                passed.append(cp)
        for j, (cx, cy) in enumerate(chips):
            sj = 2 * cx + cy
            for i in range(n):
                blk = o_refs[i].at[sj, rows(i, 1 - c), :]
                _rcopy(blk, blk, send_sems, recv_sems, (3 + j) * n + i, sib).wait_recv()
        for cp in first + passed:
            cp.wait_send()

    return _comm_call(body, "gather_weights", list(ws) + _own_slot(ws, chip),
                      [SDS((N_CHIPS,) + w.shape, w.dtype) for w in ws], 6 * n, landing=n)


_HBM_ONLY = pl.BlockSpec(memory_space=pltpu.HBM)
_SEM = pl.BlockSpec(memory_space=pltpu.SEMAPHORE)
_DATAFLOW = pltpu.SideEffectType.DATAFLOW_SIDE_EFFECTING


def _in_hbm(a):
    return pltpu.with_memory_space_constraint(a, pltpu.HBM)


def _gather_windows(blocks):
    halves = [b.shape[0] // 2 for b in blocks]

    def src_at(ref, i, c, sj):
        return ref.at[pl.ds(c * halves[i], halves[i]), :]

    def dst_at(ref, i, c, s_from):
        return ref.at[s_from, pl.ds(c * halves[i], halves[i]), :]

    return src_at, dst_at


def _exchange_windows():
    return (lambda ref, i, c, sj: ref.at[sj]), (lambda ref, i, c, s_from: ref.at[s_from])


def _swap_windows(gs):
    halves = [g.shape[1] // 2 for g in gs]
    return ((lambda ref, i, c, tag: ref.at[:, pl.ds((1 - c) * halves[i], halves[i]), :]),
            (lambda ref, i, c, slot: ref))


def _chip_peers(x, y, c):
    return [(2 * cx + cy, (cx, cy, c), 2 * x + y, 2 * cx + cy) for cx, cy in _other_chips(x, y)]


def _sibling_peer(x, y, c):
    return [(0, (x, y, 1 - c), 0, 0)]


def _split_start(name, ws, lands, dep, windows, peers=_chip_peers, n_peers=3):
    n = len(ws)
    src_at, dst_at = windows

    def body(*refs):
        w_refs, l_refs = refs[:n], refs[n:2 * n]
        send_sems, recv_sems = refs[2 * n + 1], refs[2 * n + 2]
        token = refs[-1]
        x, y, c = _place()
        for j, (tag, dev, there, _) in enumerate(peers(x, y, c)):
            for i in range(n):
                _rcopy(src_at(w_refs[i], i, c, tag), dst_at(l_refs[i], i, c, there), send_sems, recv_sems,
                       j * n + i, dev).start()
        token[...] = jnp.zeros_like(token)

    outs = pl.pallas_call(
        body, name=name,
        out_shape=(pltpu.SemaphoreType.DMA((n_peers * n,)), pltpu.SemaphoreType.DMA((n_peers * n,)),
                   *[pltpu.HBM(w.shape, w.dtype) for w in ws], *[pltpu.HBM(t.shape, t.dtype) for t in lands],
                   SDS((8, 128), F32)),
        in_specs=[_HBM_ONLY] * (2 * n) + [pl.BlockSpec(memory_space=pl.ANY)],
        out_specs=(_SEM, _SEM, *[_HBM_ONLY] * (2 * n), pl.BlockSpec(memory_space=pltpu.VMEM)),
        input_output_aliases={i: 2 + i for i in range(2 * n)},
        compiler_params=pltpu.CompilerParams(has_side_effects=_DATAFLOW),
    )(*[_in_hbm(w) for w in ws], *[_in_hbm(t) for t in lands], dep)
    return outs[0], outs[1], outs[2:2 + n], outs[2 + n:2 + 2 * n], outs[-1]


def _split_wait(name, w_thru, l_thru, send_sems, recv_sems, after, windows, peers=_chip_peers, with_sources=False):
    n = len(w_thru)
    src_at, dst_at = windows

    def body(*refs):
        w_refs, l_refs = refs[:n], refs[n:2 * n]
        send_sems, recv_sems = refs[2 * n], refs[2 * n + 1]
        x, y, c = _place()
        for j, (tag, dev, _, here) in enumerate(peers(x, y, c)):
            for i in range(n):
                cp = _rcopy(src_at(w_refs[i], i, c, tag), dst_at(l_refs[i], i, c, here), send_sems, recv_sems,
                            j * n + i, dev)
                cp.wait_send()
                cp.wait_recv()

    outs = pl.pallas_call(
        body, name=name,
        out_shape=[pltpu.HBM(w.shape, w.dtype) for w in w_thru] + [pltpu.HBM(t.shape, t.dtype) for t in l_thru],
        in_specs=[_HBM_ONLY] * (2 * n) + [_SEM, _SEM, pl.BlockSpec(memory_space=pl.ANY)],
        out_specs=[_HBM_ONLY] * (2 * n),
        input_output_aliases={i: i for i in range(2 * n)},
        compiler_params=pltpu.CompilerParams(has_side_effects=_DATAFLOW),
    )(*w_thru, *l_thru, send_sems, recv_sems, after)
    return (outs[:n], outs[n:]) if with_sources else outs[n:]


def _sibling_fill(lands):
    n = len(lands)
    halves = [t.shape[1] // 2 for t in lands]

    def body(*refs):
        o_refs = refs[n:2 * n]
        send_sems, recv_sems = refs[2 * n:]
        x, y, c = _place()
        sib = (x, y, 1 - c)
        chips = _other_chips(x, y)
        sent = []
        for j, (cx, cy) in enumerate(chips):
            for i in range(n):
                blk = o_refs[i].at[2 * cx + cy, pl.ds(c * halves[i], halves[i]), :]
                cp = _rcopy(blk, blk, send_sems, recv_sems, j * n + i, sib)
                cp.start()
                sent.append(cp)
        for j, (cx, cy) in enumerate(chips):
            for i in range(n):
                blk = o_refs[i].at[2 * cx + cy, pl.ds((1 - c) * halves[i], halves[i]), :]
                _rcopy(blk, blk, send_sems, recv_sems, j * n + i, sib).wait_recv()
        for cp in sent:
            cp.wait_send()

    return _comm_call(body, "sibling_fill", list(lands), [SDS(t.shape, t.dtype) for t in lands], 3 * n, landing=n)


def _swap_halves(gs, name):
    n = len(gs)
    halves = [g.shape[1] // 2 for g in gs]

    def body(*refs):
        g_refs, o_refs = refs[:n], refs[n:2 * n]
        send_sems, recv_sems = refs[2 * n:]
        x, y, c = _place()
        cps = [_rcopy(g_refs[i].at[:, pl.ds((1 - c) * halves[i], halves[i]), :], o_refs[i], send_sems, recv_sems, i,
                      (x, y, 1 - c)) for i in range(n)]
        for cp in cps:
            cp.start()
        for cp in cps:
            cp.wait()

    return _comm_call(body, name, gs, [SDS((N_CHIPS, h, g.shape[2]), g.dtype) for g, h in zip(gs, halves)], n)


def _swap_reduced(rs, name):
    n = len(rs)

    def body(*refs):
        r_refs, o_refs = refs[:n], refs[n:2 * n]
        send_sems, recv_sems = refs[2 * n:]
        x, y, c = _place()
        cps = [_rcopy(r_refs[i], o_refs[i], send_sems, recv_sems, i, (x, y, 1 - c)) for i in range(n)]
        for cp in cps:
            cp.start()
        for cp in cps:
            cp.wait()

    return _comm_call(body, name, rs, [SDS(r.shape, r.dtype) for r in rs], n)


def _all_sum_small(vec, name):
    n_dev = 8
    flips = [(bx, by, bc) for bx in (0, 1) for by in (0, 1) for bc in (0, 1)][1:]

    def body(v_ref, out_ref, gath, send_sems, recv_sems):
        x, y, c = _place()
        me = 4 * x + 2 * y + c
        gath[me] = v_ref[...]
        sent = []
        for k, (bx, by, bc) in enumerate(flips):
            peer = (x ^ bx, y ^ by, c ^ bc)
            cp = _rcopy(v_ref, gath.at[me], send_sems, recv_sems, k, peer)
            cp.start()
            sent.append(cp)
        for k, (bx, by, bc) in enumerate(flips):
            peer = (x ^ bx, y ^ by, c ^ bc)
            _rcopy(v_ref, gath.at[4 * peer[0] + 2 * peer[1] + peer[2]], send_sems, recv_sems, k, peer).wait_recv()
        for cp in sent:
            cp.wait_send()
        acc = gath[0]
        for d in range(1, n_dev):
            acc = acc + gath[d]
        out_ref[...] = acc

    vm = pl.BlockSpec(memory_space=pltpu.VMEM)
    return pl.pallas_call(
        body, name=name, in_specs=[vm], out_specs=vm, out_shape=SDS(vec.shape, F32),
        scratch_shapes=[pltpu.VMEM((n_dev,) + vec.shape, F32), pltpu.SemaphoreType.DMA((7,)),
                        pltpu.SemaphoreType.DMA((7,))],
        compiler_params=_cparams(has_side_effects=True),
    )(vec)


def _pack_small(vals, extra=None):
    parts = [vals[n].reshape(-1).astype(F32) for n, _ in _SMALL]
    parts.append(jnp.zeros((1,), F32) if extra is None else extra.reshape(1).astype(F32))
    flat = jnp.concatenate(parts)
    flat = jnp.concatenate([flat, jnp.zeros((_SMALL_ROWS * 128 - flat.shape[0],), F32)])
    return flat.reshape(_SMALL_ROWS, 128)


def _unpack_small(packed, shapes):
    flat = packed.reshape(-1)
    return {n: flat[_SMALL_OFF[n][0]:_SMALL_OFF[n][0] + _SMALL_OFF[n][1]].reshape(shapes[n]) for n, _ in _SMALL}


def _pair_sum(gs, gots, core, name):
    n = len(gs)

    def kern(c_ref, *refs):
        for i in range(n):
            refs[2 * n + i][...] = (refs[i][...].astype(F32) + refs[n + i][...].astype(F32)).astype(BF16)

    in_specs = [pl.BlockSpec((1, t.shape[1], t.shape[2]), lambda s, c_ref: (s, c_ref[0], 0)) for t in gots]
    in_specs += [pl.BlockSpec((1, t.shape[1], t.shape[2]), lambda s, c_ref: (s, 0, 0)) for t in gots]
    out_specs = [pl.BlockSpec((1, t.shape[1], t.shape[2]), lambda s, c_ref: (s, 0, 0)) for t in gots]
    return pl.pallas_call(
        kern, name=name,
        grid_spec=pltpu.PrefetchScalarGridSpec(num_scalar_prefetch=1, grid=(N_CHIPS,), in_specs=in_specs,
                                               out_specs=out_specs),
        out_shape=[SDS(t.shape, BF16) for t in gots],
        compiler_params=_cparams(dimension_semantics=("arbitrary",)),
    )(core.reshape(1).astype(jnp.int32), *gs, *gots)


def _chip_sum(qs, name):
    n = len(qs)

    def kern(*refs):
        for i in range(n):
            acc = refs[i][0].astype(F32)
            for s in range(1, N_CHIPS):
                acc = acc + refs[i][s].astype(F32)
            refs[n + i][...] = acc

    in_specs = [pl.BlockSpec((N_CHIPS, q.shape[1] // 2, q.shape[2]), lambda j: (0, j, 0)) for q in qs]
    out_specs = [pl.BlockSpec((q.shape[1] // 2, q.shape[2]), lambda j: (j, 0)) for q in qs]
    return _pc(kern, name, (2,), in_specs, out_specs, [SDS(q.shape[1:], F32) for q in qs])(*qs)


def _adam_math(w_, g_, m_, v_):
    m_ = ADAM_B1 * m_ + (1.0 - ADAM_B1) * g_
    v_ = ADAM_B2 * v_ + (1.0 - ADAM_B2) * jnp.square(g_)
    m_hat = m_ / (1.0 - ADAM_B1 ** ADAM_STEP)
    v_hat = v_ / (1.0 - ADAM_B2 ** ADAM_STEP)
    return -ADAM_LR * (m_hat / (jnp.sqrt(v_hat) + ADAM_EPS) + ADAM_WD * w_), m_, v_


def _adamw(w, g, m, v, name):
    rows, cols = w.shape
    tr = rows
    for cand in (256, 128, 64, 32, 16, 8):
        if rows % cand == 0 and rows > cand:
            tr = cand
            break

    def kern(w_ref, g_ref, m_ref, v_ref, d_ref, nm_ref, nv_ref):
        d_ref[...], nm_ref[...], nv_ref[...] = _adam_math(w_ref[...], g_ref[...], m_ref[...], v_ref[...])

    spec = pl.BlockSpec((tr, cols), lambda i: (i, 0))
    return _pc(kern, name, (rows // tr,), [spec] * 4, [spec] * 3, [SDS(w.shape, F32)] * 3)(w, g, m, v)


def _adamw_rows1(w, g, m, v, name):
    rows, _, cols = w.shape
    tr = next(t for t in (203, 174, 128, 64, 42, 32, 29, 16, 8, 7, 6, 4, 3, 2, 1) if rows % t == 0)

    def kern(w_ref, g_ref, m_ref, v_ref, go_ref, d_ref, nm_ref, nv_ref):
        g_ = g_ref[...]
        go_ref[...] = g_
        d_ref[...], nm_ref[...], nv_ref[...] = _adam_math(w_ref[...], g_, m_ref[...], v_ref[...])

    spec = pl.BlockSpec((tr, 1, cols), lambda i: (i, 0, 0))
    return _pc(kern, name, (rows // tr,), [spec] * 4, [spec] * 4, [SDS(w.shape, F32)] * 4)(w, g, m, v)


def _adamw_big(w, mine, theirs, m, v, core, name):
    _, rows, cols = w.shape
    half = rows // 2
    tr = next(t for t in (256, 176, 128, 64, 32, 16, 8) if half % t == 0)
    nbh = half // tr

    def kern(c_ref, w_ref, a_ref, b_ref, m_ref, v_ref, g_ref, d_ref, nm_ref, nv_ref):
        g_ = jnp.where(pl.program_id(0) // nbh == c_ref[0], a_ref[...], b_ref[...])
        g_ref[0] = g_
        d_ref[0], nm_ref[0], nv_ref[0] = _adam_math(w_ref[0], g_, m_ref[0], v_ref[0])

    full = pl.BlockSpec((1, tr, cols), lambda i, c_ref: (0, i, 0))
    part = pl.BlockSpec((tr, cols), lambda i, c_ref: (i % nbh, 0))
    return pl.pallas_call(
        kern, name=name,
        grid_spec=pltpu.PrefetchScalarGridSpec(num_scalar_prefetch=1, grid=(rows // tr,),
                                               in_specs=[full, part, part, full, full], out_specs=[full] * 4),
        out_shape=[SDS(w.shape, F32)] * 4,
        compiler_params=_cparams(dimension_semantics=("arbitrary",)),
    )(core.reshape(1).astype(jnp.int32), w, mine, theirs, m, v)


_WEIGHT_NAMES = ("attn_norm", "w_in", "dn_conv", "dn_a_log", "dn_dt_bias", "dn_out_norm", "swa_q_norm", "swa_k_norm",
                 "swa_sinks", "rel_bias", "w_branch_dn", "w_branch_swa", "w_out", "ffn_norm", "w_gate", "w_up",
                 "w_down")
_CONV_SH = QKVW // N_CHIPS


def kernel(x, attn_norm, w_in, dn_conv, dn_a_log, dn_dt_bias, dn_out_norm, swa_q_norm, swa_k_norm, swa_sinks, rel_bias, w_branch_dn, w_branch_swa, w_out, ffn_norm, w_gate, w_up, w_down, loss_target, m_attn_norm, m_w_in, m_dn_conv, m_dn_a_log, m_dn_dt_bias, m_dn_out_norm, m_swa_q_norm, m_swa_k_norm, m_swa_sinks, m_rel_bias, m_w_branch_dn, m_w_branch_swa, m_w_out, m_ffn_norm, m_w_gate, m_w_up, m_w_down, v_attn_norm, v_w_in, v_dn_conv, v_dn_a_log, v_dn_dt_bias, v_dn_out_norm, v_swa_q_norm, v_swa_k_norm, v_swa_sinks, v_rel_bias, v_w_branch_dn, v_w_branch_swa, v_w_out, v_ffn_norm, v_w_gate, v_w_up, v_w_down):
    w = dict(attn_norm=attn_norm, w_in=w_in, dn_conv=dn_conv, dn_a_log=dn_a_log, dn_dt_bias=dn_dt_bias,
             dn_out_norm=dn_out_norm, swa_q_norm=swa_q_norm, swa_k_norm=swa_k_norm, swa_sinks=swa_sinks,
             rel_bias=rel_bias, w_branch_dn=w_branch_dn, w_branch_swa=w_branch_swa, w_out=w_out, ffn_norm=ffn_norm,
             w_gate=w_gate, w_up=w_up, w_down=w_down)
    m = dict(attn_norm=m_attn_norm, w_in=m_w_in, dn_conv=m_dn_conv, dn_a_log=m_dn_a_log, dn_dt_bias=m_dn_dt_bias,
             dn_out_norm=m_dn_out_norm, swa_q_norm=m_swa_q_norm, swa_k_norm=m_swa_k_norm, swa_sinks=m_swa_sinks,
             rel_bias=m_rel_bias, w_branch_dn=m_w_branch_dn, w_branch_swa=m_w_branch_swa, w_out=m_w_out,
             ffn_norm=m_ffn_norm, w_gate=m_w_gate, w_up=m_w_up, w_down=m_w_down)
    v = dict(attn_norm=v_attn_norm, w_in=v_w_in, dn_conv=v_dn_conv, dn_a_log=v_dn_a_log, dn_dt_bias=v_dn_dt_bias,
             dn_out_norm=v_dn_out_norm, swa_q_norm=v_swa_q_norm, swa_k_norm=v_swa_k_norm, swa_sinks=v_swa_sinks,
             rel_bias=v_rel_bias, w_branch_dn=v_w_branch_dn, w_branch_swa=v_w_branch_swa, w_out=v_w_out,
             ffn_norm=v_ffn_norm, w_gate=v_w_gate, w_up=v_w_up, w_down=v_w_down)
    shapes = {n: w[n].shape for n in _WEIGHT_NAMES}

    def two_d(a):
        return a.reshape(a.shape[-2], a.shape[-1]) if a.ndim == 3 else a

    core = lax.axis_index("c")
    chip = 2 * lax.axis_index("x") + lax.axis_index("y")
    small_shapes = {n: two_d(w[n]).shape for n, _ in _SMALL}
    small_shapes["dn_conv"] = (CONV, QKVW)

    conv_loc = two_d(w["dn_conv"])
    conv_part = lax.dynamic_update_slice(jnp.zeros((CONV, QKVW), F32), jnp.where(core == 0, conv_loc, 0.0),
                                         (0, chip * _CONV_SH))
    conv_full = _all_sum_small(conv_part.reshape(CONV * QKVW // 128, 128), "gather_conv").reshape(CONV, QKVW)

    flipped = ("w_gate", "w_up")

    def natural(a, n):
        return a.transpose(0, 2, 1) if n in flipped else a

    w_bf = [two_d(natural(w[n], n).astype(BF16)) for n in _BIG_NAMES]
    (w_in_g,) = _gather_weights(w_bf[:1], chip)
    windows = _gather_windows(w_bf[1:])
    after_sync = w_in_g[0, :8, :128].astype(F32) + conv_full[0:1, :128]
    send_sems, recv_sems, w_thru, l_thru, token = _split_start(
        "gather_start", w_bf[1:], _own_slot(w_bf[1:], chip), after_sync, windows)

    def late(after):
        lands = _split_wait("gather_wait", w_thru, l_thru, send_sems, recv_sems, after, windows)
        g = dict(zip(_BIG_NAMES[1:], _sibling_fill(lands)))
        return dict(wa=g["w_branch_dn"], wb=g["w_branch_swa"], w_out=g["w_out"].reshape(D, D), wg=g["w_gate"],
                    wu=g["w_up"], wd=g["w_down"])

    wts = dict(w_in_p=_w_in_to_padded(w_in_g), dn_conv=conv_full, late=late)
    for n, _ in _SMALL[:-1]:
        wts[n] = two_d(w[n])
    wts["attn_norm"] = wts["attn_norm"] + token[0:1, 0:1]

    early = {}

    ffn = {}

    def send_ffn(grads):
        gs = [grads["w_gate"], grads["w_up"], grads["w_down"]]
        lands = [lax.empty((N_CHIPS, g.shape[1] // 2, g.shape[2]), g.dtype) for g in gs]
        ffn["sems"], ffn["recv"], ffn["src"], ffn["land"], tok = _split_start(
            "swap_ffn_start", gs, lands, gs[0][0, :8, :128], _swap_windows(gs), _sibling_peer, 1)
        return tok

    def send_early(grads):
        small = [grads["w_branch_dn"], grads["w_branch_swa"], grads["w_out"].reshape(N_CHIPS, CSH, D)]
        big = [grads["w_gate"], grads["w_up"], grads["w_down"]]
        big, got_big = _split_wait("swap_ffn_wait", ffn["src"], ffn["land"], ffn["sems"], ffn["recv"], small[0],
                                   _swap_windows(big), _sibling_peer, with_sources=True)
        gots = list(_swap_halves(small, "swap_halves_early")) + list(got_big)
        parts = _pair_sum(small + list(big), gots, core, "pair_sum_early")
        own = [lax.dynamic_index_in_dim(p, chip, axis=0, keepdims=False) for p in parts]
        early["sems"], early["recv"], early["src"], early["land"], tok = _split_start(
            "exchange_start", parts, _own_slot(own, chip), parts[0][0, :8, :128], _exchange_windows())
        return tok

    last = {}

    def send_in(g_in_p):
        g_in = [_padded_to_w_in(g_in_p)]
        parts = _pair_sum(g_in, _swap_halves(g_in, "swap_halves_in"), core, "pair_sum_in")
        own = [lax.dynamic_index_in_dim(p, chip, axis=0, keepdims=False) for p in parts]
        last["sems"], last["recv"], last["src"], last["land"], tok = _split_start(
            "exchange_in_start", parts, _own_slot(own, chip), parts[0][0, :8, :128], _exchange_windows())
        return tok

    wts["send_ffn"] = send_ffn
    wts["send_early"] = send_early
    wts["send_in"] = send_in
    loss_sum, grad_x, grads = _local_step(x[0], loss_target[0], wts)

    small_sum = _all_sum_small(_pack_small(grads, loss_sum), "all_sum_small")
    loss = small_sum.reshape(-1)[_LOSS_OFF]
    g_small = _unpack_small(small_sum, small_shapes)

    q_early = _split_wait("exchange_wait", early["src"], early["land"], early["sems"], early["recv"], small_sum,
                          _exchange_windows())
    red_early = _chip_sum(list(q_early), "chip_sum_early")
    their_early = _swap_reduced(red_early, "swap_reduced_early")
    g_out, d_out, m_out, v_out = {}, {}, {}, {}
    for n, mine, other in zip(_BIG_NAMES[1:], red_early, their_early):
        res = _adamw_big(natural(w[n], n), mine, other, natural(m[n], n), natural(v[n], n), core, "adamw_" + n)
        g_out[n], d_out[n], m_out[n], v_out[n] = (natural(t, n) for t in res)

    q_in = _split_wait("exchange_in_wait", last["src"], last["land"], last["sems"], last["recv"],
                       d_out[_BIG_NAMES[-1]], _exchange_windows())
    reduced = _chip_sum(list(q_in), "chip_sum_in")
    theirs = _swap_reduced(reduced, "swap_reduced_in")

    def rows1(a):
        return a.transpose(2, 0, 1)

    def unrows1(a):
        return a.transpose(1, 2, 0)

    g_in_blk = jnp.concatenate([jnp.where(core == 0, reduced[0], theirs[0]),
                                jnp.where(core == 0, theirs[0], reduced[0])], axis=0)
    g_in_r = rows1(g_in_blk[None])
    res = _adamw_rows1(rows1(w["w_in"]), g_in_r, rows1(m["w_in"]), rows1(v["w_in"]), "adamw_w_in")
    g_out["w_in"], d_out["w_in"], m_out["w_in"], v_out["w_in"] = (unrows1(t) for t in res)
    g_conv = lax.dynamic_slice(g_small["dn_conv"], (0, chip * _CONV_SH), (CONV, _CONV_SH))
    g_out["dn_conv"] = g_conv.reshape(shapes["dn_conv"])
    d_, m_, v_ = _adamw(conv_loc, g_conv, two_d(m["dn_conv"]), two_d(v["dn_conv"]), "adamw_dn_conv")
    d_out["dn_conv"], m_out["dn_conv"], v_out["dn_conv"] = (t.reshape(shapes["dn_conv"]) for t in (d_, m_, v_))

    def packed(src):
        vals = {n: src[n] for n, _ in _SMALL[:-1]}
        vals["dn_conv"] = jnp.zeros((CONV * QKVW,), F32)
        return _pack_small(vals)

    d_s, m_s, v_s = _adamw(packed(w), small_sum, packed(m), packed(v), "adamw_small")
    d_small, m_small, v_small = (_unpack_small(t, small_shapes) for t in (d_s, m_s, v_s))
    for n, _ in _SMALL[:-1]:
        g_out[n] = g_small[n].reshape(shapes[n])
        d_out[n], m_out[n], v_out[n] = (t[n].reshape(shapes[n]) for t in (d_small, m_small, v_small))

    return (loss, grad_x[None], *[g_out[n] for n in _WEIGHT_NAMES], *[d_out[n] for n in _WEIGHT_NAMES],
            *[m_out[n] for n in _WEIGHT_NAMES], *[v_out[n] for n in _WEIGHT_NAMES])
```

```python
import functools
import math

import numpy as np
import jax
import jax.numpy as jnp
from jax import lax
from jax.experimental import pallas as pl
from jax.experimental.pallas import tpu as pltpu

F32 = jnp.float32
BF16 = jnp.bfloat16
SDS = jax.ShapeDtypeStruct

D = 1024
DN_H = 4
DH = 128
DNW = DN_H * DH
QKVW = 3 * DNW
CONV = 4
CHUNK = 64
SWA_H = 8
SWA_KV = 2
SWA_G = SWA_H // SWA_KV
SWA_D = 64
SWAW = SWA_H * SWA_D
SWAKW = SWA_KV * SWA_D
BLK = 128
NBUCKET = 32
MAXDIST = 128
DFF = 2816
D_IN = QKVW + DNW + 2 * DN_H + SWAW + 2 * SWAKW + 2 * D
EPS = 1e-6
NEG = -1e30

ADAM_LR = 0.001
ADAM_B1 = 0.9
ADAM_B2 = 0.999
ADAM_EPS = 1e-08
ADAM_WD = 0.01
ADAM_STEP = 10

C_QKV, C_Z, C_GATE, C_SQ, C_SK, C_SV, C_BA = 0, 1536, 2048, 4096, 4608, 4736, 4864
PW = 5120
_ORIG_PIECES = (
    (0, QKVW, C_QKV),
    (QKVW, DNW, C_Z),
    (QKVW + DNW, 2 * DN_H, C_BA),
    (QKVW + DNW + 2 * DN_H, SWAW, C_SQ),
    (QKVW + DNW + 2 * DN_H + SWAW, SWAKW, C_SK),
    (QKVW + DNW + 2 * DN_H + SWAW + SWAKW, SWAKW, C_SV),
    (QKVW + DNW + 2 * DN_H + SWAW + 2 * SWAKW, 2 * D, C_GATE),
)

N_CHIPS = 4
FSH = DFF // N_CHIPS
CSH = D // N_CHIPS
VMEM_LIMIT = 48 * 1024 * 1024
MESH = pl.DeviceIdType.MESH

_BIG = (
    ("w_in", D, D_IN // N_CHIPS),
    ("w_branch_dn", DNW, CSH),
    ("w_branch_swa", SWAW, CSH),
    ("w_out", CSH, D),
    ("w_gate", FSH, D),
    ("w_up", FSH, D),
    ("w_down", FSH, D),
)
_BIG_NAMES = tuple(n for n, _, _ in _BIG)

_SMALL = (
    ("attn_norm", D), ("ffn_norm", D), ("dn_out_norm", DH), ("swa_q_norm", SWA_D), ("swa_k_norm", SWA_D),
    ("swa_sinks", SWA_H), ("dn_a_log", DN_H), ("dn_dt_bias", DN_H), ("rel_bias", NBUCKET * SWA_H),
    ("dn_conv", CONV * QKVW),
)
_SMALL_OFF = {}
_o = 0
for _n, _s in _SMALL:
    _SMALL_OFF[_n] = (_o, _s)
    _o += _s
_LOSS_OFF = _o
_SMALL_ROWS = -(-(_o + 1) // (8 * 128)) * 8


def _cparams(**kw):
    return pltpu.CompilerParams(vmem_limit_bytes=VMEM_LIMIT, **kw)


_DIMS = {
    "nn": (((1,), (0,)), ((), ())),
    "nt": (((1,), (1,)), ((), ())),
    "tn": (((0,), (0,)), ((), ())),
    "bnn": (((2,), (1,)), ((0,), (0,))),
    "bnt": (((2,), (2,)), ((0,), (0,))),
    "btn": (((1,), (1,)), ((0,), (0,))),
}


def _raw_dot(a, b, kind, exact):
    if exact:
        prec = lax.Precision.HIGH if exact == "x3" else lax.Precision.HIGHEST
        return lax.dot_general(a, b, _DIMS[kind], precision=prec, preferred_element_type=F32)
    return lax.dot_general(a.astype(BF16), b.astype(BF16), _DIMS[kind], preferred_element_type=F32)


@functools.partial(jax.custom_vjp, nondiff_argnums=(2, 3))
def _dot(a, b, kind, exact):
    return _raw_dot(a, b, kind, exact)


def _dot_fwd(a, b, kind, exact):
    return _raw_dot(a, b, kind, exact), (a, b)


def _dot_bwd(kind, exact, res, g):
    a, b = res
    pre = kind[:-2]
    nn, nt, tn = pre + "nn", pre + "nt", pre + "tn"
    if kind == nn:
        return _dot(g, b, nt, exact), _dot(a, g, tn, exact)
    if kind == nt:
        return _dot(g, b, nn, exact), _dot(g, a, tn, exact)
    return _dot(b, g, nt, exact), _dot(a, g, nn, exact)


_dot.defvjp(_dot_fwd, _dot_bwd)


def _silu(x):
    return x * jax.nn.sigmoid(x)


def _f_rms(x, gain):
    return x * lax.rsqrt(jnp.mean(x * x, axis=-1, keepdims=True) + EPS) * gain


def _f_dn_pre(xs0, xs1, xs2, xs3, ba, cw, alog, dtb):
    rows = xs0.shape[0]
    c = xs0 * cw[0:1] + xs1 * cw[1:2] + xs2 * cw[2:3] + xs3 * cw[3:4]
    qkv = _silu(c)
    qs, ks, bbs, gbs = [], [], [], []
    for h in range(DN_H):
        qh = qkv[:, h * DH:(h + 1) * DH]
        kh = qkv[:, DNW + h * DH:DNW + (h + 1) * DH]
        qs.append(qh * lax.rsqrt(jnp.sum(qh * qh, axis=-1, keepdims=True) + EPS) * (DH ** -0.5))
        ks.append(kh * lax.rsqrt(jnp.sum(kh * kh, axis=-1, keepdims=True) + EPS))
        beta = jax.nn.sigmoid(ba[:, h:h + 1])
        ar = ba[:, DN_H + h:DN_H + h + 1] + dtb[:, h:h + 1]
        softplus = jnp.maximum(ar, 0.0) + jnp.log1p(jnp.exp(-jnp.abs(ar)))
        g = -jnp.exp(alog[:, h:h + 1]) * softplus
        bbs.append(jnp.broadcast_to(beta, (rows, DH)))
        gbs.append(jnp.broadcast_to(g, (rows, DH)))
    return (jnp.concatenate(qs, axis=1), jnp.concatenate(ks, axis=1), qkv[:, 2 * DNW:],
            jnp.concatenate(bbs, axis=1), jnp.concatenate(gbs, axis=1))


def _f_dn_post(o, z, gain):
    ys = []
    for h in range(DN_H):
        oh = o[:, h * DH:(h + 1) * DH]
        zh = z[:, h * DH:(h + 1) * DH]
        ys.append(oh * lax.rsqrt(jnp.mean(oh * oh, axis=-1, keepdims=True) + EPS) * gain * _silu(zh))
    return jnp.concatenate(ys, axis=1)


def _f_merge(pa, pb, ga, gb):
    return jax.nn.sigmoid(ga) * pa + jax.nn.sigmoid(gb) * pb


@jax.custom_vjp
def _f_swiglu(g, u):
    return _silu(g) * u


def _f_swiglu_fwd(g, u):
    return _silu(g) * u, (g, u)


def _f_swiglu_bwd(res, d):
    g, u = res
    s = jax.nn.sigmoid(g)
    act = g * s
    return d * u * (s + act * (1.0 - s)), d * act


_f_swiglu.defvjp(_f_swiglu_fwd, _f_swiglu_bwd)


@jax.custom_vjp
def _unit_lower_inverse(a):
    c = a.shape[-1]
    eye = (lax.broadcasted_iota(jnp.int32, a.shape, 1) == lax.broadcasted_iota(jnp.int32, a.shape, 2)).astype(F32)
    p = -a
    t = eye + p
    for _ in range(max(c.bit_length() - 2, 0)):
        p = _raw_dot(p, p, "bnn", "x3")
        t = t + _raw_dot(t, p, "bnn", "x3")
    return t


def _unit_lower_inverse_fwd(a):
    t = _unit_lower_inverse(a)
    return t, t


def _unit_lower_inverse_bwd(t, g):
    return (-_raw_dot(_raw_dot(t, g, "btn", "x3"), t, "bnt", "x3"),)


_unit_lower_inverse.defvjp(_unit_lower_inverse_fwd, _unit_lower_inverse_bwd)


def _scan_rows(x, reverse):
    c = x.shape[1]
    row = lax.broadcasted_iota(jnp.int32, x.shape, 1)
    shift = 1
    while shift < c:
        if reverse:
            x = x + jnp.where(row < c - shift, pltpu.roll(x, c - shift, 1), 0.0)
        else:
            x = x + jnp.where(row >= shift, pltpu.roll(x, shift, 1), 0.0)
        shift *= 2
    return x


@jax.custom_vjp
def _cumsum_rows(x):
    return _scan_rows(x, False)


def _cumsum_rows_fwd(x):
    return _scan_rows(x, False), None


def _cumsum_rows_bwd(_, g):
    return (_scan_rows(g, True),)


_cumsum_rows.defvjp(_cumsum_rows_fwd, _cumsum_rows_bwd)


@jax.custom_vjp
def _known_inverse(a, t):
    return t


def _known_inverse_fwd(a, t):
    return t, t


def _known_inverse_bwd(t, g):
    return _unit_lower_inverse_bwd(t, g)[0], jnp.zeros_like(t)


_known_inverse.defvjp(_known_inverse_fwd, _known_inverse_bwd)


def _f_chunk(q, k, v, gb, bb, s, t_known=None, with_t=False):
    c = CHUNK
    nh = q.shape[0]
    ii = lax.broadcasted_iota(jnp.int32, (nh, c, c), 1)
    jj = lax.broadcasted_iota(jnp.int32, (nh, c, c), 2)
    incl = ii >= jj
    strict = ii > jj
    eye = (ii == jj).astype(F32)
    gcb = _cumsum_rows(gb)
    gcol = gcb[:, :, :c]
    grow = jnp.swapaxes(gcol, 1, 2)
    decay = jnp.where(incl, jnp.exp(jnp.where(incl, gcol - grow, 0.0)), 0.0)
    kb = k * bb
    vb = v * bb
    a = jnp.where(strict, _dot(kb, k, "bnt", False) * decay, 0.0)
    t = _unit_lower_inverse(a) if t_known is None else _known_inverse(a, t_known)
    eg = jnp.exp(gcb)
    u = _dot(t, vb, "bnn", "x3")
    w = _dot(t, kb * eg, "bnn", "x3")
    qk = jnp.where(incl, _dot(q, k, "bnt", False) * decay, 0.0)
    qe = q * eg
    glast = gcb[:, c - 1:c, :]
    k_dec = k * jnp.exp(glast - gcb)
    e_last = jnp.exp(glast)
    outs = []
    for g in range(nh // DN_H):
        sl = slice(g * DN_H, (g + 1) * DN_H)
        v_new = u[sl] - _dot(w[sl], s, "bnn", False)
        outs.append(_dot(qe[sl], s, "bnn", False) + _dot(qk[sl], v_new, "bnn", False))
        s = s * e_last[sl] + _dot(k_dec[sl], v_new, "btn", False)
    o = jnp.concatenate(outs, axis=0)
    return (o, s, t) if with_t else (o, s)


def _f_swa(q8, kp, kc, vp, vc, bias8, qg, kg, sink, mask):
    kb = jnp.concatenate([kp, kc], axis=1)
    vb = jnp.concatenate([vp, vc], axis=1)
    kn = kb * lax.rsqrt(jnp.mean(kb * kb, axis=-1, keepdims=True) + EPS) * kg

    def rows(per_head):
        return jnp.stack([jnp.concatenate([per_head(kv, g) for g in range(SWA_G)], axis=0)
                          for kv in range(SWA_KV)], axis=0)

    qq = rows(lambda kv, g: q8[kv * SWA_G + g])
    qn = qq * lax.rsqrt(jnp.mean(qq * qq, axis=-1, keepdims=True) + EPS) * qg * (SWA_D ** -0.5)
    lg = _dot(qn, kn, "bnt", False) + rows(lambda kv, g: bias8[kv * SWA_G + g])
    lg = jnp.where(rows(lambda kv, g: mask), lg, NEG)
    sk = rows(lambda kv, g: jnp.broadcast_to(sink[kv][:, g:g + 1], (BLK, 1)))
    m = lax.stop_gradient(jnp.maximum(jnp.max(lg, axis=-1, keepdims=True), sk))
    p = jnp.exp(lg - m)
    den = jnp.sum(p, axis=-1, keepdims=True) + jnp.exp(sk - m)
    out = _dot(p * (1.0 / den), vb, "bnn", False)
    return jnp.stack([out[kv, g * BLK:(g + 1) * BLK] for kv in range(SWA_KV) for g in range(SWA_G)], axis=0)


def _bdot(a, b, kind="nn"):
    return lax.dot_general(a.astype(BF16), b.astype(BF16), _DIMS[kind], preferred_element_type=F32)


def _pc(kern, name, grid, in_specs, out_specs, out_shape, scratch=()):
    return pl.pallas_call(
        kern, name=name, grid=grid, in_specs=in_specs, out_specs=out_specs, out_shape=out_shape,
        scratch_shapes=list(scratch), compiler_params=_cparams(dimension_semantics=("arbitrary",) * len(grid)))


def _mm(a, b, kind, out_dtype, tm, tn, name):
    if kind == "tn":
        k, m = a.shape
    else:
        m, k = a.shape
    n = b.shape[0] if kind == "nt" else b.shape[1]
    tm, tn = min(tm, m), min(tn, n)
    assert m % tm == 0 and n % tn == 0, (name, a.shape, b.shape, tm, tn)

    def kern(a_ref, b_ref, o_ref):
        o_ref[...] = _bdot(a_ref[...], b_ref[...], kind).astype(o_ref.dtype)

    a_spec = pl.BlockSpec((k, tm), lambda i, j: (0, i)) if kind == "tn" else pl.BlockSpec((tm, k), lambda i, j: (i, 0))
    b_spec = pl.BlockSpec((tn, k), lambda i, j: (j, 0)) if kind == "nt" else pl.BlockSpec((k, tn), lambda i, j: (0, j))
    return _pc(kern, name, (m // tm, n // tn), [a_spec, b_spec], pl.BlockSpec((tm, tn), lambda i, j: (i, j)),
               SDS((m, n), out_dtype))(a, b)


def _rows(body, name, m, tm, row_ins, full_ins, row_outs, acc_outs=()):
    n_r, n_f, n_o, n_a = len(row_ins), len(full_ins), len(row_outs), len(acc_outs)
    assert m % tm == 0

    def kern(*refs):
        r = refs[:n_r]
        f = refs[n_r:n_r + n_f]
        o = refs[n_r + n_f:n_r + n_f + n_o]
        acc = refs[n_r + n_f + n_o:]
        outs, sums = body([x[...] for x in r], [x[...] for x in f])
        for ref, val in zip(o, outs, strict=True):
            ref[...] = val.astype(ref.dtype)
        if n_a:
            @pl.when(pl.program_id(0) == 0)
            def _():
                for ref in acc:
                    ref[...] = jnp.zeros(ref.shape, F32)

            for ref, val in zip(acc, sums, strict=True):
                ref[...] += val

    in_specs = [pl.BlockSpec((tm, w), functools.partial(lambda i, cb: (i, cb), cb=cb)) for _, w, cb in row_ins]
    in_specs += [pl.BlockSpec(x.shape, lambda i: (0, 0)) for x in full_ins]
    out_specs = [pl.BlockSpec((tm, w), lambda i: (i, 0)) for w, _ in row_outs]
    out_specs += [pl.BlockSpec(s, lambda i: (0, 0)) for s in acc_outs]
    out_shape = [SDS((m, w), dt) for w, dt in row_outs]
    out_shape += [SDS(s, F32) for s in acc_outs]
    return _pc(kern, name, (m // tm,), in_specs, out_specs, out_shape)(*[x for x, _, _ in row_ins], *full_ins)


def _whole(x):
    return (x, x.shape[1], 0)


def _resident(shape):
    return pl.BlockSpec(shape, lambda i: (0,) * len(shape), pipeline_mode=pl.Buffered(1))


def _row_pieces(tm, piece):
    piece = min(piece, tm)
    return [slice(r, r + piece) for r in range(0, tm, piece)]


def _zero_first(refs):
    @pl.when(pl.program_id(0) == 0)
    def _():
        for ref in refs:
            ref[...] = jnp.zeros(ref.shape, F32)


GROUP = 4


def _heads(ref):
    return jnp.stack([ref[g * CHUNK:(g + 1) * CHUNK, h * DH:(h + 1) * DH]
                      for g in range(GROUP) for h in range(DN_H)], axis=0)


def _unheads(ref, val):
    for g in range(GROUP):
        for h in range(DN_H):
            ref[g * CHUNK:(g + 1) * CHUNK, h * DH:(h + 1) * DH] = val[g * DN_H + h]


def _dn_chunks_fwd(q, k, v, gb, bb):
    s_len = q.shape[0]
    ng = s_len // (GROUP * CHUNK)

    def kern(q_ref, k_ref, v_ref, g_ref, b_ref, o_ref, sall_ref, t_ref, state):
        _zero_first([state])
        s = state[...]
        sall_ref[0] = s
        o, s_new, t = _f_chunk(*[_heads(r) for r in (q_ref, k_ref, v_ref, g_ref, b_ref)], s, with_t=True)
        _unheads(o_ref, o)
        t_ref[0] = t
        state[...] = s_new

    blk = pl.BlockSpec((GROUP * CHUNK, DNW), lambda c: (c, 0))
    return _pc(kern, "dn_chunks_fwd", (ng,), [blk] * 5,
               [blk, pl.BlockSpec((1, DN_H, DH, DH), lambda c: (c, 0, 0, 0)),
                pl.BlockSpec((1, GROUP * DN_H, CHUNK, CHUNK), lambda c: (c, 0, 0, 0))],
               [SDS((s_len, DNW), F32), SDS((ng, DN_H, DH, DH), F32), SDS((ng, GROUP * DN_H, CHUNK, CHUNK), F32)],
               scratch=[pltpu.VMEM((DN_H, DH, DH), F32)])(q, k, v, gb, bb)


def _dn_chunks_bwd(q, k, v, gb, bb, s_all, t_all, d_o):
    s_len = q.shape[0]
    ng = s_len // (GROUP * CHUNK)

    def kern(q_ref, k_ref, v_ref, g_ref, b_ref, sall_ref, t_ref, do_ref, dq_ref, dk_ref, dv_ref, dg_ref, db_ref,
             dstate):
        _zero_first([dstate])
        fn = functools.partial(_f_chunk, t_known=t_ref[0])
        _, vjp = jax.vjp(fn, *[_heads(r) for r in (q_ref, k_ref, v_ref, g_ref, b_ref)], sall_ref[0])
        *d_ins, ds = vjp((_heads(do_ref), dstate[...]))
        for ref, val in zip((dq_ref, dk_ref, dv_ref, dg_ref, db_ref), d_ins, strict=True):
            _unheads(ref, val)
        dstate[...] = ds

    blk = pl.BlockSpec((GROUP * CHUNK, DNW), lambda c: (ng - 1 - c, 0))
    return _pc(kern, "dn_chunks_bwd", (ng,),
               [blk] * 5 + [pl.BlockSpec((1, DN_H, DH, DH), lambda c: (ng - 1 - c, 0, 0, 0)),
                            pl.BlockSpec((1, GROUP * DN_H, CHUNK, CHUNK), lambda c: (ng - 1 - c, 0, 0, 0)), blk],
               [blk] * 5, [SDS((s_len, DNW), F32)] * 5,
               scratch=[pltpu.VMEM((DN_H, DH, DH), F32)])(q, k, v, gb, bb, s_all, t_all, d_o)


def _t5_bucket_table():
    qi = np.arange(BLK)[:, None]
    kj = np.arange(2 * BLK)[None, :]
    dist = BLK + qi - kj
    n = np.maximum(dist, 0)
    max_exact = NBUCKET // 2
    nf = np.maximum(n, 1).astype(np.float32)
    large = max_exact + (np.log(nf / np.float32(max_exact)) / np.float32(math.log(MAXDIST / max_exact))
                         * np.float32(NBUCKET - max_exact)).astype(np.int32)
    large = np.minimum(large, NBUCKET - 1)
    return np.where(n < max_exact, n, large)


def _bucket_onehot_t():
    table = _t5_bucket_table().reshape(-1)
    return (np.arange(NBUCKET)[:, None] == table[None, :]).astype(np.float32)


def _swa_mask(first):
    qi = lax.broadcasted_iota(jnp.int32, (BLK, 2 * BLK), 0)
    kj = lax.broadcasted_iota(jnp.int32, (BLK, 2 * BLK), 1)
    dist = BLK + qi - kj
    window = (dist >= 0) & (dist < BLK)
    return window & ((kj >= BLK) | jnp.logical_not(first))


def _bias_expand(rel_bias_t):
    onehot = jnp.asarray(_bucket_onehot_t())

    def kern(r_ref, oh_ref, o_ref):
        o_ref[...] = _raw_dot(r_ref[...], oh_ref[...], "nn", True)

    return pl.pallas_call(
        kern, name="bias_expand", out_shape=SDS((SWA_H, BLK * 2 * BLK), F32), compiler_params=_cparams(),
    )(rel_bias_t, onehot)


def _bias_reduce(d_bias_flat):
    onehot = jnp.asarray(_bucket_onehot_t())

    def kern(d_ref, oh_ref, o_ref):
        o_ref[...] = _raw_dot(d_ref[...], oh_ref[...], "nt", True)

    return pl.pallas_call(
        kern, name="bias_reduce", out_shape=SDS((SWA_H, NBUCKET), F32), compiler_params=_cparams(),
    )(d_bias_flat, onehot)


def _swa_specs(nb, rev):
    def blk(n):
        return (nb - 1 - n) if rev else n

    def before(n):
        return jnp.maximum(blk(n) - 1, 0)

    q_spec = pl.BlockSpec((BLK, SWAW), lambda n: (blk(n), C_SQ // SWAW))
    k_cur = pl.BlockSpec((BLK, SWAKW), lambda n: (blk(n), C_SK // SWAKW))
    k_prev = pl.BlockSpec((BLK, SWAKW), lambda n: (before(n), C_SK // SWAKW))
    v_cur = pl.BlockSpec((BLK, SWAKW), lambda n: (blk(n), C_SV // SWAKW))
    v_prev = pl.BlockSpec((BLK, SWAKW), lambda n: (before(n), C_SV // SWAKW))
    bias = pl.BlockSpec((SWA_H, BLK, 2 * BLK), lambda n: (0, 0, 0))
    gain = pl.BlockSpec((1, SWA_D), lambda n: (0, 0))
    sink = pl.BlockSpec((SWA_KV, 1, SWA_G), lambda n: (0, 0, 0))
    wide = pl.BlockSpec((BLK, SWAW), lambda n: (blk(n), 0))
    narrow = pl.BlockSpec((BLK, SWAKW), lambda n: (blk(n), 0))
    return [q_spec, k_prev, k_cur, v_prev, v_cur, bias, gain, gain, sink], wide, narrow


def _split_heads(x):
    return jnp.stack([x[:, h * SWA_D:(h + 1) * SWA_D] for h in range(x.shape[1] // SWA_D)], axis=0)


def _join_heads(x):
    return jnp.concatenate([x[h] for h in range(x.shape[0])], axis=1)


def _swa_fwd(proj, bias, qg, kg, sinks):
    s_len = proj.shape[0]
    nb = s_len // BLK
    in_specs, wide, _ = _swa_specs(nb, False)

    def kern(q_ref, kp_ref, kc_ref, vp_ref, vc_ref, b_ref, qg_ref, kg_ref, s_ref, o_ref):
        mask = _swa_mask(pl.program_id(0) == 0)
        o8 = _f_swa(*[_split_heads(r[...]) for r in (q_ref, kp_ref, kc_ref, vp_ref, vc_ref)], b_ref[...], qg_ref[...],
                    kg_ref[...], s_ref[...], mask)
        o_ref[...] = _join_heads(o8).astype(BF16)

    return _pc(kern, "swa_fwd", (nb,), in_specs, wide, SDS((s_len, SWAW), BF16))(
        proj, proj, proj, proj, proj, bias, qg, kg, sinks)


def _swa_bwd(proj, bias, qg, kg, sinks, d_out):
    s_len = proj.shape[0]
    nb = s_len // BLK
    in_specs, wide, narrow = _swa_specs(nb, True)

    def kern(q_ref, kp_ref, kc_ref, vp_ref, vc_ref, b_ref, qg_ref, kg_ref, s_ref, do_ref,
             dq_ref, dk_ref, dv_ref, db_ref, dqg_ref, dkg_ref, ds_ref, carry_k, carry_v):
        n = pl.program_id(0)
        mask = _swa_mask(n == nb - 1)
        _zero_first([carry_k, carry_v, db_ref, ds_ref, dqg_ref, dkg_ref])
        fn = functools.partial(_f_swa, mask=mask)
        _, vjp = jax.vjp(fn, *[_split_heads(r[...]) for r in (q_ref, kp_ref, kc_ref, vp_ref, vc_ref)], b_ref[...],
                         qg_ref[...], kg_ref[...], s_ref[...])
        dq, dkp, dkc, dvp, dvc, dbias, dqg, dkg, dsink = vjp(_split_heads(do_ref[...]))
        dq_ref[...] = _join_heads(dq).astype(BF16)
        dk_ref[...] = (_join_heads(dkc) + carry_k[...]).astype(BF16)
        dv_ref[...] = (_join_heads(dvc) + carry_v[...]).astype(BF16)
        carry_k[...] = _join_heads(dkp)
        carry_v[...] = _join_heads(dvp)
        db_ref[...] += dbias
        dqg_ref[...] += dqg
        dkg_ref[...] += dkg
        ds_ref[...] += dsink

    bias_spec, gain, sink = in_specs[5], in_specs[6], in_specs[8]
    return _pc(
        kern, "swa_bwd", (nb,), in_specs + [wide], [wide, narrow, narrow, bias_spec, gain, gain, sink],
        [SDS((s_len, SWAW), BF16), SDS((s_len, SWAKW), BF16), SDS((s_len, SWAKW), BF16),
         SDS((SWA_H, BLK, 2 * BLK), F32), SDS((1, SWA_D), F32), SDS((1, SWA_D), F32), SDS((SWA_KV, 1, SWA_G), F32)],
        scratch=[pltpu.VMEM((BLK, SWAKW), F32), pltpu.VMEM((BLK, SWAKW), F32)],
    )(proj, proj, proj, proj, proj, bias, qg, kg, sinks, d_out)


def _branch_merge(y_dn, y_swa, wa, wb, proj):
    s_len = y_dn.shape[0]
    tm = min(1024, s_len)

    def kern(ya_ref, yb_ref, wa_ref, wb_ref, ga_ref, gb_ref, pa_ref, pb_ref, m_ref):
        for rows in _row_pieces(tm, 128):
            pa = _bdot(ya_ref[rows, :], wa_ref[0])
            pb = _bdot(yb_ref[rows, :], wb_ref[0])
            pa_ref[rows, :] = pa.astype(BF16)
            pb_ref[rows, :] = pb.astype(BF16)
            m_ref[rows, :] = _f_merge(pa, pb, ga_ref[rows, :], gb_ref[rows, :]).astype(BF16)

    y_spec = pl.BlockSpec((tm, DNW), lambda i, s: (i, 0))
    w_spec = pl.BlockSpec((1, DNW, CSH), lambda i, s: (s, 0, 0))
    o_spec = pl.BlockSpec((tm, CSH), lambda i, s: (i, s))
    ga_spec = pl.BlockSpec((tm, CSH), lambda i, s: (i, C_GATE // CSH + s))
    gb_spec = pl.BlockSpec((tm, CSH), lambda i, s: (i, (C_GATE + D) // CSH + s))
    return _pc(kern, "branch_merge", (s_len // tm, N_CHIPS), [y_spec, y_spec, w_spec, w_spec, ga_spec, gb_spec],
               [o_spec] * 3, [SDS((s_len, D), BF16)] * 3,
               )(y_dn, y_swa, wa, wb, proj, proj)


def _in_proj(x, gain, w_in_p):
    s_len = x.shape[0]
    tm = min(512, s_len)

    def kern(x_ref, g_ref, w_ref, h_ref, p_ref):
        h = _f_rms(x_ref[...], g_ref[...]).astype(BF16)
        h_ref[...] = h
        p_ref[...] = _bdot(h, w_ref[...])

    row = pl.BlockSpec((tm, D), lambda i: (i, 0))
    return _pc(kern, "in_proj", (s_len // tm,),
               [row, pl.BlockSpec((1, D), lambda i: (0, 0)), _resident((D, PW))],
               [row, pl.BlockSpec((tm, PW), lambda i: (i, 0))],
               [SDS((s_len, D), BF16), SDS((s_len, PW), F32)])(x, gain, w_in_p)


def _out_proj(merged, w_out, x, gain):
    s_len = x.shape[0]
    tm = min(512, s_len)

    def kern(m_ref, w_ref, x_ref, g_ref, x1_ref, h2_ref):
        x1 = x_ref[...] + _bdot(m_ref[...], w_ref[...])
        x1_ref[...] = x1
        h2_ref[...] = _f_rms(x1, g_ref[...]).astype(BF16)

    row = pl.BlockSpec((tm, D), lambda i: (i, 0))
    return _pc(kern, "out_proj", (s_len // tm,),
               [row, _resident((D, D)), row, pl.BlockSpec((1, D), lambda i: (0, 0))],
               [row, row], [SDS((s_len, D), F32), SDS((s_len, D), BF16)])(merged, w_out, x, gain)


def _ffn_up(h2, wg, wu):
    s_len = h2.shape[0]
    tm = min(2048, s_len)

    def kern(h_ref, g_ref, u_ref, gt_ref, up_ref, act_ref):
        for rows in _row_pieces(tm, 256):
            h = h_ref[rows, :]
            g = _bdot(h, g_ref[0], "nt")
            u = _bdot(h, u_ref[0], "nt")
            gt_ref[0, rows, :] = g.astype(BF16)
            up_ref[0, rows, :] = u.astype(BF16)
            act_ref[0, rows, :] = _f_swiglu(g, u).astype(BF16)

    w_spec = pl.BlockSpec((1, FSH, D), lambda s, i: (s, 0, 0))
    o_spec = pl.BlockSpec((1, tm, FSH), lambda s, i: (s, i, 0))
    shape = (N_CHIPS, s_len, FSH)
    return _pc(kern, "ffn_up", (N_CHIPS, s_len // tm), [pl.BlockSpec((tm, D), lambda s, i: (i, 0)), w_spec, w_spec],
               [o_spec] * 3, [SDS(shape, BF16)] * 3)(h2, wg, wu)


def _ffn_down_loss(act, wd, x1, target):
    s_len = x1.shape[0]
    tm = min(512, s_len)

    def kern(a_ref, w_ref, x_ref, t_ref, dy_ref, dyb_ref, loss_ref):
        _zero_first([loss_ref])
        for rows in _row_pieces(tm, 128):
            y = x_ref[rows, :]
            for s in range(N_CHIPS):
                y = y + _bdot(a_ref[s, rows, :], w_ref[s])
            d = y - t_ref[rows, :]
            dy = d * (1.0 / D)
            dy_ref[rows, :] = dy
            dyb_ref[rows, :] = dy.astype(BF16)
            loss_ref[...] += jnp.sum(d * d).reshape(1, 1) * (0.5 / D)

    row = pl.BlockSpec((tm, D), lambda i: (i, 0))
    return _pc(kern, "ffn_down_loss", (s_len // tm,),
               [pl.BlockSpec((N_CHIPS, tm, FSH), lambda i: (0, i, 0)),
                _resident((N_CHIPS, FSH, D)), row, row],
               [row, row, pl.BlockSpec((1, 1), lambda i: (0, 0))],
               [SDS((s_len, D), F32), SDS((s_len, D), BF16), SDS((1, 1), F32)])(act, wd, x1, target)


def _ffn_dact(dy_b, wd, gt, up):
    s_len = dy_b.shape[0]
    tm = min(2048, s_len)

    def kern(dy_ref, w_ref, gt_ref, up_ref, dg_ref, du_ref):
        w = w_ref[0]
        for rows in _row_pieces(tm, 256):
            d_act = _bdot(dy_ref[rows, :], w, "nt")
            _, vjp = jax.vjp(_f_swiglu, gt_ref[0, rows, :].astype(F32), up_ref[0, rows, :].astype(F32))
            dg, du = vjp(d_act)
            dg_ref[0, rows, :] = dg.astype(BF16)
            du_ref[0, rows, :] = du.astype(BF16)

    a_spec = pl.BlockSpec((1, tm, FSH), lambda s, i: (s, i, 0))
    shape = (N_CHIPS, s_len, FSH)
    return _pc(kern, "ffn_dact", (N_CHIPS, s_len // tm),
               [pl.BlockSpec((tm, D), lambda s, i: (i, 0)), pl.BlockSpec((1, FSH, D), lambda s, i: (s, 0, 0)),
                a_spec, a_spec],
               [a_spec, a_spec], [SDS(shape, BF16), SDS(shape, BF16)])(dy_b, wd, gt, up)


def _gw_ffn(lhs, rhs, name):
    s_len = rhs.shape[0]
    n = len(lhs)
    tn = 512

    def kern(*refs):
        g = refs[n][...]
        for i in range(n):
            refs[n + 1 + i][0] = _bdot(refs[i][0], g, "tn").astype(BF16)

    a_spec = pl.BlockSpec((1, s_len, FSH), lambda s, j: (s, 0, 0))
    o_spec = pl.BlockSpec((1, FSH, tn), lambda s, j: (s, 0, j))
    return _pc(kern, name, (N_CHIPS, D // tn), [a_spec] * n + [pl.BlockSpec((s_len, tn), lambda s, j: (0, j))],
               [o_spec] * n, [SDS((N_CHIPS, FSH, D), BF16)] * n)(*lhs, rhs)


def _ffn_dh2(d_gt, d_up, wg, wu, x1, dy, gain):
    s_len = x1.shape[0]
    tm = min(512, s_len)

    def kern(dg_ref, du_ref, wg_ref, wu_ref, x_ref, dy_ref, g_ref, dx_ref, dxb_ref, dgain_ref):
        _zero_first([dgain_ref])
        dh2 = jnp.zeros((tm, D), F32)
        for s in range(N_CHIPS):
            dh2 = dh2 + _bdot(dg_ref[s], wg_ref[s]) + _bdot(du_ref[s], wu_ref[s])
        _, vjp = jax.vjp(_f_rms, x_ref[...], g_ref[...])
        dx, dgain = vjp(dh2)
        dx1 = dx + dy_ref[...]
        dx_ref[...] = dx1
        dxb_ref[...] = dx1.astype(BF16)
        dgain_ref[...] += dgain

    row = pl.BlockSpec((tm, D), lambda i: (i, 0))
    d_spec = pl.BlockSpec((N_CHIPS, tm, FSH), lambda i: (0, i, 0))
    w_spec = _resident((N_CHIPS, FSH, D))
    vec = pl.BlockSpec((1, D), lambda i: (0, 0))
    return _pc(kern, "ffn_dh2", (s_len // tm,), [d_spec, d_spec, w_spec, w_spec, row, row, vec],
               [row, row, vec], [SDS((s_len, D), F32), SDS((s_len, D), BF16), SDS((1, D), F32)],
               )(d_gt, d_up, wg, wu, x1, dy, gain)


def _merge_bwd(dx1_b, w_out, pa, pb, proj):
    s_len = dx1_b.shape[0]
    tm = min(512, s_len)

    def kern(dx_ref, w_ref, pa_ref, pb_ref, g_ref, dpa_ref, dpb_ref, dg_ref):
        dm = _bdot(dx_ref[...], w_ref[...], "nt")
        gates = g_ref[...]
        _, vjp = jax.vjp(_f_merge, pa_ref[...].astype(F32), pb_ref[...].astype(F32), gates[:, :D], gates[:, D:])
        dpa, dpb, dga, dgb = vjp(dm)
        dpa_ref[...] = dpa.astype(BF16)
        dpb_ref[...] = dpb.astype(BF16)
        dg_ref[:, :D] = dga.astype(BF16)
        dg_ref[:, D:] = dgb.astype(BF16)

    row = pl.BlockSpec((tm, D), lambda i: (i, 0))
    return _pc(kern, "merge_bwd", (s_len // tm,),
               [row, _resident((D, D)), row, row,
                pl.BlockSpec((tm, 2 * D), lambda i: (i, C_GATE // (2 * D)))],
               [row, row, pl.BlockSpec((tm, 2 * D), lambda i: (i, 0))],
               [SDS((s_len, D), BF16), SDS((s_len, D), BF16), SDS((s_len, 2 * D), BF16)],
               )(dx1_b, w_out, pa, pb, proj)


def _d_branch(d_pa, d_pb, wa, wb):
    s_len = d_pa.shape[0]
    tm = min(512, s_len)

    def kern(da_ref, db_ref, wa_ref, wb_ref, oa_ref, ob_ref):
        acc_a = jnp.zeros((tm, DNW), F32)
        acc_b = jnp.zeros((tm, SWAW), F32)
        for s in range(N_CHIPS):
            acc_a = acc_a + _bdot(da_ref[:, s * CSH:(s + 1) * CSH], wa_ref[s], "nt")
            acc_b = acc_b + _bdot(db_ref[:, s * CSH:(s + 1) * CSH], wb_ref[s], "nt")
        oa_ref[...] = acc_a
        ob_ref[...] = acc_b

    row = pl.BlockSpec((tm, D), lambda i: (i, 0))
    w_spec = pl.BlockSpec((N_CHIPS, DNW, CSH), lambda i: (0, 0, 0))
    out = pl.BlockSpec((tm, DNW), lambda i: (i, 0))
    return _pc(kern, "d_branch", (s_len // tm,), [row, row, w_spec, w_spec], [out, out],
               [SDS((s_len, DNW), F32), SDS((s_len, SWAW), F32)])(d_pa, d_pb, wa, wb)


def _gw_branch(y_dn, y_swa, d_pa, d_pb):
    s_len = y_dn.shape[0]

    def kern(ya_ref, yb_ref, da_ref, db_ref, oa_ref, ob_ref):
        oa_ref[0] = _bdot(ya_ref[...], da_ref[...], "tn").astype(BF16)
        ob_ref[0] = _bdot(yb_ref[...], db_ref[...], "tn").astype(BF16)

    y_spec = pl.BlockSpec((s_len, DNW), lambda s: (0, 0))
    d_spec = pl.BlockSpec((s_len, CSH), lambda s: (0, s))
    o_spec = pl.BlockSpec((1, DNW, CSH), lambda s: (s, 0, 0))
    shape = (N_CHIPS, DNW, CSH)
    return _pc(kern, "gw_branch", (N_CHIPS,), [y_spec, y_spec, d_spec, d_spec], [o_spec, o_spec],
               [SDS(shape, BF16), SDS(shape, BF16)])(y_dn, y_swa, d_pa, d_pb)


def _dh_rms(d_proj, w_in_p, x, dx1, gain):
    s_len = x.shape[0]
    tm = min(512, s_len)

    def kern(dp_ref, w_ref, x_ref, r_ref, g_ref, gx_ref, dgain_ref):
        _zero_first([dgain_ref])
        dh = _bdot(dp_ref[...], w_ref[...], "nt")
        _, vjp = jax.vjp(_f_rms, x_ref[...], g_ref[...])
        dx, dgain = vjp(dh)
        gx_ref[...] = dx + r_ref[...]
        dgain_ref[...] += dgain

    row = pl.BlockSpec((tm, D), lambda i: (i, 0))
    vec = pl.BlockSpec((1, D), lambda i: (0, 0))
    return _pc(kern, "dh_rms", (s_len // tm,),
               [pl.BlockSpec((tm, PW), lambda i: (i, 0)), _resident((D, PW)), row, row, vec],
               [row, vec], [SDS((s_len, D), F32), SDS((1, D), F32)])(d_proj, w_in_p, x, dx1, gain)


HALO = 8


def _rows_down(x, n, above):
    tm = x.shape[0]
    r = pltpu.roll(x, n, 0)
    a = pltpu.roll(above, n, 0)
    top = jnp.where(lax.broadcasted_iota(jnp.int32, above.shape, 0) < n, a, r[0:HALO])
    return jnp.concatenate([top, r[HALO:tm]], axis=0)


def _rows_up(x, n, below):
    tm = x.shape[0]
    r = pltpu.roll(x, tm - n, 0)
    b = pltpu.roll(below, HALO - n, 0)
    bottom = jnp.where(lax.broadcasted_iota(jnp.int32, below.shape, 0) >= HALO - n, b, r[tm - HALO:tm])
    return jnp.concatenate([r[0:tm - HALO], bottom], axis=0)


def _conv_taps(cur_ref, prev_ref, first):
    cur = cur_ref[...]
    above = jnp.where(first, 0.0, prev_ref[...])
    return [_rows_down(cur, n, above) for n in range(CONV - 1, 0, -1)] + [cur]


def _dn_pre_specs(s_len, tm, blk):
    cur = pl.BlockSpec((tm, QKVW), lambda i: (blk(i), 0))
    prev = pl.BlockSpec((HALO, QKVW), lambda i: (jnp.maximum(blk(i) * (tm // HALO) - 1, 0), 0))
    ba = pl.BlockSpec((tm, 128), lambda i: (blk(i), C_BA // 128))
    row = pl.BlockSpec((tm, DNW), lambda i: (blk(i), 0))
    full = [pl.BlockSpec((CONV, QKVW), lambda i: (0, 0)), pl.BlockSpec((1, DN_H), lambda i: (0, 0)),
            pl.BlockSpec((1, DN_H), lambda i: (0, 0))]
    return cur, prev, ba, row, full


def _dn_pre_fwd(proj, conv_w, alog, dtb):
    s_len = proj.shape[0]
    tm = min(256, s_len)
    cur, prev, ba, row, full = _dn_pre_specs(s_len, tm, lambda i: i)

    def kern(cur_ref, prev_ref, ba_ref, cw_ref, al_ref, dt_ref, q_ref, k_ref, v_ref, bb_ref, gb_ref):
        xs = _conv_taps(cur_ref, prev_ref, pl.program_id(0) == 0)
        outs = _f_dn_pre(*xs, ba_ref[...], cw_ref[...], al_ref[...], dt_ref[...])
        for ref, val in zip((q_ref, k_ref, v_ref, bb_ref, gb_ref), outs, strict=True):
            ref[...] = val

    return _pc(kern, "dn_pre_fwd", (s_len // tm,), [cur, prev, ba] + full, [row] * 5,
               [SDS((s_len, DNW), F32)] * 5)(proj, proj, proj, conv_w, alog, dtb)


def _dn_pre_bwd(proj, conv_w, alog, dtb, cots, others):
    s_len = proj.shape[0]
    tm = min(256, s_len)
    nb = s_len // tm
    cur, prev, ba, row, full = _dn_pre_specs(s_len, tm, lambda i: nb - 1 - i)
    n_o = len(others)
    assert QKVW + sum(t.shape[1] for t in others) + 128 == C_BA + 128

    def kern(cur_ref, prev_ref, ba_ref, cw_ref, al_ref, dt_ref, dq_ref, dk_ref, dv_ref, dbb_ref, dgb_ref, *rest):
        o_refs = rest[:n_o]
        dproj_ref, dcw_ref, dal_ref, ddt_ref, *tails = rest[n_o:]
        i = pl.program_id(0)
        _zero_first([dcw_ref, dal_ref, ddt_ref] + tails)
        xs = _conv_taps(cur_ref, prev_ref, i == nb - 1)
        _, vjp = jax.vjp(_f_dn_pre, *xs, ba_ref[...], cw_ref[...], al_ref[...], dt_ref[...])
        *dxs, dba, dcw, dal, ddt = vjp((dq_ref[...], dk_ref[...], dv_ref[...], dbb_ref[...], dgb_ref[...]))
        total = dxs[CONV - 1]
        for j, t in enumerate(tails):
            n = CONV - 1 - j
            total = total + _rows_up(dxs[j], n, t[...])
            t[...] = dxs[j][0:HALO, :]
        dproj_ref[...] = jnp.concatenate(
            [total.astype(BF16)] + [r[...] for r in o_refs] + [dba.astype(BF16), jnp.zeros((tm, PW - C_BA - 128), BF16)],
            axis=1)
        dcw_ref[...] += dcw
        dal_ref[...] += dal
        ddt_ref[...] += ddt

    o_specs = [pl.BlockSpec((tm, t.shape[1]), lambda i: (nb - 1 - i, 0)) for t in others]
    return _pc(kern, "dn_pre_bwd", (nb,), [cur, prev, ba] + full + [row] * 5 + o_specs,
               [pl.BlockSpec((tm, PW), lambda i: (nb - 1 - i, 0))] + full,
               [SDS((s_len, PW), BF16), SDS((CONV, QKVW), F32), SDS((1, DN_H), F32), SDS((1, DN_H), F32)],
               scratch=[pltpu.VMEM((HALO, QKVW), F32)] * (CONV - 1))(proj, proj, proj, conv_w, alog, dtb, *cots, *others)


def _w_in_to_padded(w_sh):
    tr = 256

    def kern(w_ref, o_ref):
        full = jnp.concatenate([w_ref[s] for s in range(N_CHIPS)], axis=1)
        pieces = [full[:, o0:o0 + w] for o0, w, _ in sorted(_ORIG_PIECES, key=lambda t: t[2])]
        o_ref[...] = jnp.concatenate(pieces + [jnp.zeros((tr, PW - D_IN), w_ref.dtype)], axis=1)

    return _pc(kern, "w_in_to_padded", (D // tr,), [pl.BlockSpec((N_CHIPS, tr, D_IN // N_CHIPS), lambda i: (0, i, 0))],
               pl.BlockSpec((tr, PW), lambda i: (i, 0)), SDS((D, PW), w_sh.dtype))(w_sh)


def _padded_to_w_in(g):
    tr = 256
    csh = D_IN // N_CHIPS

    def kern(g_ref, o_ref):
        x = g_ref[...]
        full = jnp.concatenate([x[:, p0:p0 + w] for _, w, p0 in _ORIG_PIECES], axis=1)
        for s in range(N_CHIPS):
            o_ref[s] = full[:, s * csh:(s + 1) * csh]

    return _pc(kern, "padded_to_w_in", (D // tr,), [pl.BlockSpec((tr, PW), lambda i: (i, 0))],
               pl.BlockSpec((N_CHIPS, tr, csh), lambda i: (0, i, 0)), SDS((N_CHIPS, D, csh), g.dtype))(g)


def _local_step(x, target, wts):
    s_len = x.shape[0]
    tm = min(512, s_len)
    w_in_p = wts["w_in_p"]
    attn_gain = wts["attn_norm"]
    ffn_gain = wts["ffn_norm"]
    conv_w = wts["dn_conv"]
    alog, dtb, out_gain = wts["dn_a_log"], wts["dn_dt_bias"], wts["dn_out_norm"]
    qg, kg = wts["swa_q_norm"], wts["swa_k_norm"]
    sinks = wts["swa_sinks"].reshape(SWA_KV, 1, SWA_G)

    h, proj = _in_proj(x, attn_gain, w_in_p)
    q_dn, k_dn, v_dn, bb, gb = _dn_pre_fwd(proj, conv_w, alog, dtb)
    o_dn, s_all, t_all = _dn_chunks_fwd(q_dn, k_dn, v_dn, gb, bb)
    post_ins = [_whole(o_dn), (proj, DNW, C_Z // DNW)]
    (y_dn,) = _rows(lambda r, f: ([_f_dn_post(r[0], r[1], f[0])], []), "dn_post_fwd", s_len, tm, post_ins,
                    [out_gain], [(DNW, BF16)])

    bias = _bias_expand(wts["rel_bias"].T).reshape(SWA_H, BLK, 2 * BLK)
    y_swa = _swa_fwd(proj, bias, qg, kg, sinks)

    wts = {**wts, **wts["late"](y_swa)}
    p_a, p_b, merged = _branch_merge(y_dn, y_swa, wts["wa"], wts["wb"], proj)
    x1, h2 = _out_proj(merged, wts["w_out"], x, ffn_gain)
    gt, up, act = _ffn_up(h2, wts["wg"], wts["wu"])
    dy, dy_b, loss = _ffn_down_loss(act, wts["wd"], x1, target)

    grads = {}
    d_gt, d_up = _ffn_dact(dy_b, wts["wd"], gt, up)
    (grads["w_down"],) = _gw_ffn([act], dy_b, "gw_down")
    grads["w_gate"], grads["w_up"] = _gw_ffn([d_gt, d_up], h2, "gw_gate_up")
    token = wts["send_ffn"](grads)
    dx1, dx1_b, grads["ffn_norm"] = _ffn_dh2(d_gt, d_up, wts["wg"], wts["wu"], x1, dy,
                                             ffn_gain + token[0:1, 0:1])
    grads["w_out"] = _mm(merged, dx1_b, "tn", BF16, 512, 512, "gw_out")
    d_pa, d_pb, d_gr = _merge_bwd(dx1_b, wts["w_out"], p_a, p_b, proj)
    d_ydn, d_yswa = _d_branch(d_pa, d_pb, wts["wa"], wts["wb"])
    grads["w_branch_dn"], grads["w_branch_swa"] = _gw_branch(y_dn, y_swa, d_pa, d_pb)
    token = wts["send_early"](grads)
    qg_t = qg + token[0:1, 0:1]
    out_gain_t = out_gain + token[0:1, 0:1]

    d_sq, d_sk, d_sv, d_bias, grads["swa_q_norm"], grads["swa_k_norm"], d_sinks = _swa_bwd(
        proj, bias, qg_t, kg, sinks, d_yswa)
    grads["swa_sinks"] = d_sinks.reshape(1, SWA_H)
    grads["rel_bias"] = _bias_reduce(d_bias.reshape(SWA_H, BLK * 2 * BLK)).T

    def post_bwd(r, f):
        _, vjp = jax.vjp(_f_dn_post, r[0], r[1], f[0])
        d_o, d_z, d_gain = vjp(r[2])
        return [d_o, d_z], [d_gain]

    d_o, d_z, grads["dn_out_norm"] = _rows(post_bwd, "dn_post_bwd", s_len, tm, post_ins + [_whole(d_ydn)], [out_gain_t],
                                           [(DNW, F32), (DNW, BF16)], [(1, DH)])
    d_q, d_k, d_v, d_gb, d_bb = _dn_chunks_bwd(q_dn, k_dn, v_dn, gb, bb, s_all, t_all, d_o)

    d_proj, grads["dn_conv"], grads["dn_a_log"], grads["dn_dt_bias"] = _dn_pre_bwd(
        proj, conv_w, alog, dtb, (d_q, d_k, d_v, d_bb, d_gb), (d_z, d_gr, d_sq, d_sk, d_sv))
    grads["w_in_p"] = _mm(h, d_proj, "tn", BF16, 512, 1024, "gw_in")
    token = wts["send_in"](grads["w_in_p"])
    grad_x, grads["attn_norm"] = _dh_rms(d_proj, w_in_p, x, dx1, attn_gain + token[0:1, 0:1])
    return loss, grad_x, grads


_HBM = pl.BlockSpec(memory_space=pl.ANY)


def _place():
    return lax.axis_index("x"), lax.axis_index("y"), lax.axis_index("c")


def _other_chips(x, y):
    return [(1 - x, y), (x, 1 - y), (1 - x, 1 - y)]


def _rcopy(src, dst, send_sems, recv_sems, k, to):
    return pltpu.make_async_remote_copy(src_ref=src, dst_ref=dst, send_sem=send_sems.at[k], recv_sem=recv_sems.at[k],
                                        device_id=to, device_id_type=MESH)


def _comm_call(body, name, ins, out_shapes, n_remote, landing=0):
    first = len(ins) - landing
    return pl.pallas_call(
        body, name=name, in_specs=[_HBM] * len(ins), out_specs=[_HBM] * len(out_shapes), out_shape=out_shapes,
        scratch_shapes=[pltpu.SemaphoreType.DMA((n_remote,)), pltpu.SemaphoreType.DMA((n_remote,))],
        input_output_aliases={first + i: i for i in range(landing)},
        compiler_params=_cparams(has_side_effects=True),
    )(*ins)


def _own_slot(blocks, chip):
    return [lax.dynamic_update_slice(lax.empty((N_CHIPS,) + b.shape, b.dtype), b[None], (chip, 0, 0)) for b in blocks]


def _gather_weights(ws, chip):
    n = len(ws)
    halves = [w.shape[0] // 2 for w in ws]

    def body(*refs):
        w_refs, o_refs = refs[:n], refs[2 * n:3 * n]
        send_sems, recv_sems = refs[3 * n:]
        x, y, c = _place()
        s = 2 * x + y
        sib = (x, y, 1 - c)
        chips = _other_chips(x, y)

        def rows(i, half):
            return pl.ds(half * halves[i], halves[i])

        first = []
        for j, (cx, cy) in enumerate(chips):
            for i in range(n):
                cp = _rcopy(w_refs[i].at[rows(i, c), :], o_refs[i].at[s, rows(i, c), :], send_sems, recv_sems,
                            j * n + i, (cx, cy, c))
                cp.start()
                first.append(cp)
        passed = []
        for j, (cx, cy) in enumerate(chips):
            sj = 2 * cx + cy
            for i in range(n):
                blk = o_refs[i].at[sj, rows(i, c), :]
                _rcopy(blk, blk, send_sems, recv_sems, j * n + i, (cx, cy, c)).wait_recv()
                cp = _rcopy(blk, blk, send_sems, recv_sems, (3 + j) * n + i, sib)
                cp.start()
                passed.append(cp)
        for j, (cx, cy) in enumerate(chips):
            sj = 2 * cx + cy
            for i in range(n):
                blk = o_refs[i].at[sj, rows(i, 1 - c), :]
                _rcopy(blk, blk, send_sems, recv_sems, (3 + j) * n + i, sib).wait_recv()
        for cp in first + passed:
            cp.wait_send()

    return _comm_call(body, "gather_weights", list(ws) + _own_slot(ws, chip),
                      [SDS((N_CHIPS,) + w.shape, w.dtype) for w in ws], 6 * n, landing=n)


_HBM_ONLY = pl.BlockSpec(memory_space=pltpu.HBM)
_SEM = pl.BlockSpec(memory_space=pltpu.SEMAPHORE)
_DATAFLOW = pltpu.SideEffectType.DATAFLOW_SIDE_EFFECTING


def _in_hbm(a):
    return pltpu.with_memory_space_constraint(a, pltpu.HBM)


def _gather_windows(blocks):
    halves = [b.shape[0] // 2 for b in blocks]

    def src_at(ref, i, c, sj):
        return ref.at[pl.ds(c * halves[i], halves[i]), :]

    def dst_at(ref, i, c, s_from):
        return ref.at[s_from, pl.ds(c * halves[i], halves[i]), :]

    return src_at, dst_at


def _exchange_windows():
    return (lambda ref, i, c, sj: ref.at[sj]), (lambda ref, i, c, s_from: ref.at[s_from])


def _swap_windows(gs):
    halves = [g.shape[1] // 2 for g in gs]
    return ((lambda ref, i, c, tag: ref.at[:, pl.ds((1 - c) * halves[i], halves[i]), :]),
            (lambda ref, i, c, slot: ref))


def _chip_peers(x, y, c):
    return [(2 * cx + cy, (cx, cy, c), 2 * x + y, 2 * cx + cy) for cx, cy in _other_chips(x, y)]


def _sibling_peer(x, y, c):
    return [(0, (x, y, 1 - c), 0, 0)]


def _split_start(name, ws, lands, dep, windows, peers=_chip_peers, n_peers=3):
    n = len(ws)
    src_at, dst_at = windows

    def body(*refs):
        w_refs, l_refs = refs[:n], refs[n:2 * n]
        send_sems, recv_sems = refs[2 * n + 1], refs[2 * n + 2]
        token = refs[-1]
        x, y, c = _place()
        for j, (tag, dev, there, _) in enumerate(peers(x, y, c)):
            for i in range(n):
                _rcopy(src_at(w_refs[i], i, c, tag), dst_at(l_refs[i], i, c, there), send_sems, recv_sems,
                       j * n + i, dev).start()
        token[...] = jnp.zeros_like(token)

    outs = pl.pallas_call(
        body, name=name,
        out_shape=(pltpu.SemaphoreType.DMA((n_peers * n,)), pltpu.SemaphoreType.DMA((n_peers * n,)),
                   *[pltpu.HBM(w.shape, w.dtype) for w in ws], *[pltpu.HBM(t.shape, t.dtype) for t in lands],
                   SDS((8, 128), F32)),
        in_specs=[_HBM_ONLY] * (2 * n) + [pl.BlockSpec(memory_space=pl.ANY)],
        out_specs=(_SEM, _SEM, *[_HBM_ONLY] * (2 * n), pl.BlockSpec(memory_space=pltpu.VMEM)),
        input_output_aliases={i: 2 + i for i in range(2 * n)},
        compiler_params=pltpu.CompilerParams(has_side_effects=_DATAFLOW),
    )(*[_in_hbm(w) for w in ws], *[_in_hbm(t) for t in lands], dep)
    return outs[0], outs[1], outs[2:2 + n], outs[2 + n:2 + 2 * n], outs[-1]


def _split_wait(name, w_thru, l_thru, send_sems, recv_sems, after, windows, peers=_chip_peers, with_sources=False):
    n = len(w_thru)
    src_at, dst_at = windows

    def body(*refs):
        w_refs, l_refs = refs[:n], refs[n:2 * n]
        send_sems, recv_sems = refs[2 * n], refs[2 * n + 1]
        x, y, c = _place()
        for j, (tag, dev, _, here) in enumerate(peers(x, y, c)):
            for i in range(n):
                cp = _rcopy(src_at(w_refs[i], i, c, tag), dst_at(l_refs[i], i, c, here), send_sems, recv_sems,
                            j * n + i, dev)
                cp.wait_send()
                cp.wait_recv()

    outs = pl.pallas_call(
        body, name=name,
        out_shape=[pltpu.HBM(w.shape, w.dtype) for w in w_thru] + [pltpu.HBM(t.shape, t.dtype) for t in l_thru],
        in_specs=[_HBM_ONLY] * (2 * n) + [_SEM, _SEM, pl.BlockSpec(memory_space=pl.ANY)],
        out_specs=[_HBM_ONLY] * (2 * n),
        input_output_aliases={i: i for i in range(2 * n)},
        compiler_params=pltpu.CompilerParams(has_side_effects=_DATAFLOW),
    )(*w_thru, *l_thru, send_sems, recv_sems, after)
    return (outs[:n], outs[n:]) if with_sources else outs[n:]


def _sibling_fill(lands):
    n = len(lands)
    halves = [t.shape[1] // 2 for t in lands]

    def body(*refs):
        o_refs = refs[n:2 * n]
        send_sems, recv_sems = refs[2 * n:]
        x, y, c = _place()
        sib = (x, y, 1 - c)
        chips = _other_chips(x, y)
        sent = []
        for j, (cx, cy) in enumerate(chips):
            for i in range(n):
                blk = o_refs[i].at[2 * cx + cy, pl.ds(c * halves[i], halves[i]), :]
                cp = _rcopy(blk, blk, send_sems, recv_sems, j * n + i, sib)
                cp.start()
                sent.append(cp)
        for j, (cx, cy) in enumerate(chips):
            for i in range(n):
                blk = o_refs[i].at[2 * cx + cy, pl.ds((1 - c) * halves[i], halves[i]), :]
                _rcopy(blk, blk, send_sems, recv_sems, j * n + i, sib).wait_recv()
        for cp in sent:
            cp.wait_send()

    return _comm_call(body, "sibling_fill", list(lands), [SDS(t.shape, t.dtype) for t in lands], 3 * n, landing=n)


def _swap_halves(gs, name):
    n = len(gs)
    halves = [g.shape[1] // 2 for g in gs]

    def body(*refs):
        g_refs, o_refs = refs[:n], refs[n:2 * n]
        send_sems, recv_sems = refs[2 * n:]
        x, y, c = _place()
        cps = [_rcopy(g_refs[i].at[:, pl.ds((1 - c) * halves[i], halves[i]), :], o_refs[i], send_sems, recv_sems, i,
                      (x, y, 1 - c)) for i in range(n)]
        for cp in cps:
            cp.start()
        for cp in cps:
            cp.wait()

    return _comm_call(body, name, gs, [SDS((N_CHIPS, h, g.shape[2]), g.dtype) for g, h in zip(gs, halves)], n)


def _swap_reduced(rs, name):
    n = len(rs)

    def body(*refs):
        r_refs, o_refs = refs[:n], refs[n:2 * n]
        send_sems, recv_sems = refs[2 * n:]
        x, y, c = _place()
        cps = [_rcopy(r_refs[i], o_refs[i], send_sems, recv_sems, i, (x, y, 1 - c)) for i in range(n)]
        for cp in cps:
            cp.start()
        for cp in cps:
            cp.wait()

    return _comm_call(body, name, rs, [SDS(r.shape, r.dtype) for r in rs], n)


def _all_sum_small(vec, name):
    n_dev = 8
    flips = [(bx, by, bc) for bx in (0, 1) for by in (0, 1) for bc in (0, 1)][1:]

    def body(v_ref, out_ref, gath, send_sems, recv_sems):
        x, y, c = _place()
        me = 4 * x + 2 * y + c
        gath[me] = v_ref[...]
        sent = []
        for k, (bx, by, bc) in enumerate(flips):
            peer = (x ^ bx, y ^ by, c ^ bc)
            cp = _rcopy(v_ref, gath.at[me], send_sems, recv_sems, k, peer)
            cp.start()
            sent.append(cp)
        for k, (bx, by, bc) in enumerate(flips):
            peer = (x ^ bx, y ^ by, c ^ bc)
            _rcopy(v_ref, gath.at[4 * peer[0] + 2 * peer[1] + peer[2]], send_sems, recv_sems, k, peer).wait_recv()
        for cp in sent:
            cp.wait_send()
        acc = gath[0]
        for d in range(1, n_dev):
            acc = acc + gath[d]
        out_ref[...] = acc

    vm = pl.BlockSpec(memory_space=pltpu.VMEM)
    return pl.pallas_call(
        body, name=name, in_specs=[vm], out_specs=vm, out_shape=SDS(vec.shape, F32),
        scratch_shapes=[pltpu.VMEM((n_dev,) + vec.shape, F32), pltpu.SemaphoreType.DMA((7,)),
                        pltpu.SemaphoreType.DMA((7,))],
        compiler_params=_cparams(has_side_effects=True),
    )(vec)


def _pack_small(vals, extra=None):
    parts = [vals[n].reshape(-1).astype(F32) for n, _ in _SMALL]
    parts.append(jnp.zeros((1,), F32) if extra is None else extra.reshape(1).astype(F32))
    flat = jnp.concatenate(parts)
    flat = jnp.concatenate([flat, jnp.zeros((_SMALL_ROWS * 128 - flat.shape[0],), F32)])
    return flat.reshape(_SMALL_ROWS, 128)


def _unpack_small(packed, shapes):
    flat = packed.reshape(-1)
    return {n: flat[_SMALL_OFF[n][0]:_SMALL_OFF[n][0] + _SMALL_OFF[n][1]].reshape(shapes[n]) for n, _ in _SMALL}


def _pair_sum(gs, gots, core, name):
    n = len(gs)

    def kern(c_ref, *refs):
        for i in range(n):
            refs[2 * n + i][...] = (refs[i][...].astype(F32) + refs[n + i][...].astype(F32)).astype(BF16)

    in_specs = [pl.BlockSpec((1, t.shape[1], t.shape[2]), lambda s, c_ref: (s, c_ref[0], 0)) for t in gots]
    in_specs += [pl.BlockSpec((1, t.shape[1], t.shape[2]), lambda s, c_ref: (s, 0, 0)) for t in gots]
    out_specs = [pl.BlockSpec((1, t.shape[1], t.shape[2]), lambda s, c_ref: (s, 0, 0)) for t in gots]
    return pl.pallas_call(
        kern, name=name,
        grid_spec=pltpu.PrefetchScalarGridSpec(num_scalar_prefetch=1, grid=(N_CHIPS,), in_specs=in_specs,
                                               out_specs=out_specs),
        out_shape=[SDS(t.shape, BF16) for t in gots],
        compiler_params=_cparams(dimension_semantics=("arbitrary",)),
    )(core.reshape(1).astype(jnp.int32), *gs, *gots)


def _chip_sum(qs, name):
    n = len(qs)

    def kern(*refs):
        for i in range(n):
            acc = refs[i][0].astype(F32)
            for s in range(1, N_CHIPS):
                acc = acc + refs[i][s].astype(F32)
            refs[n + i][...] = acc

    in_specs = [pl.BlockSpec((N_CHIPS, q.shape[1] // 2, q.shape[2]), lambda j: (0, j, 0)) for q in qs]
    out_specs = [pl.BlockSpec((q.shape[1] // 2, q.shape[2]), lambda j: (j, 0)) for q in qs]
    return _pc(kern, name, (2,), in_specs, out_specs, [SDS(q.shape[1:], F32) for q in qs])(*qs)


def _adam_math(w_, g_, m_, v_):
    m_ = ADAM_B1 * m_ + (1.0 - ADAM_B1) * g_
    v_ = ADAM_B2 * v_ + (1.0 - ADAM_B2) * jnp.square(g_)
    m_hat = m_ / (1.0 - ADAM_B1 ** ADAM_STEP)
    v_hat = v_ / (1.0 - ADAM_B2 ** ADAM_STEP)
    return -ADAM_LR * (m_hat / (jnp.sqrt(v_hat) + ADAM_EPS) + ADAM_WD * w_), m_, v_


def _adamw(w, g, m, v, name):
    rows, cols = w.shape
    tr = rows
    for cand in (256, 128, 64, 32, 16, 8):
        if rows % cand == 0 and rows > cand:
            tr = cand
            break

    def kern(w_ref, g_ref, m_ref, v_ref, d_ref, nm_ref, nv_ref):
        d_ref[...], nm_ref[...], nv_ref[...] = _adam_math(w_ref[...], g_ref[...], m_ref[...], v_ref[...])

    spec = pl.BlockSpec((tr, cols), lambda i: (i, 0))
    return _pc(kern, name, (rows // tr,), [spec] * 4, [spec] * 3, [SDS(w.shape, F32)] * 3)(w, g, m, v)


def _adamw_rows1(w, g, m, v, name):
    rows, _, cols = w.shape
    tr = next(t for t in (203, 174, 128, 64, 42, 32, 29, 16, 8, 7, 6, 4, 3, 2, 1) if rows % t == 0)

    def kern(w_ref, g_ref, m_ref, v_ref, go_ref, d_ref, nm_ref, nv_ref):
        g_ = g_ref[...]
        go_ref[...] = g_
        d_ref[...], nm_ref[...], nv_ref[...] = _adam_math(w_ref[...], g_, m_ref[...], v_ref[...])

    spec = pl.BlockSpec((tr, 1, cols), lambda i: (i, 0, 0))
    return _pc(kern, name, (rows // tr,), [spec] * 4, [spec] * 4, [SDS(w.shape, F32)] * 4)(w, g, m, v)


def _adamw_big(w, mine, theirs, m, v, core, name):
    _, rows, cols = w.shape
    half = rows // 2
    tr = next(t for t in (256, 176, 128, 64, 32, 16, 8) if half % t == 0)
    nbh = half // tr

    def kern(c_ref, w_ref, a_ref, b_ref, m_ref, v_ref, g_ref, d_ref, nm_ref, nv_ref):
        g_ = jnp.where(pl.program_id(0) // nbh == c_ref[0], a_ref[...], b_ref[...])
        g_ref[0] = g_
        d_ref[0], nm_ref[0], nv_ref[0] = _adam_math(w_ref[0], g_, m_ref[0], v_ref[0])

    full = pl.BlockSpec((1, tr, cols), lambda i, c_ref: (0, i, 0))
    part = pl.BlockSpec((tr, cols), lambda i, c_ref: (i % nbh, 0))
    return pl.pallas_call(
        kern, name=name,
        grid_spec=pltpu.PrefetchScalarGridSpec(num_scalar_prefetch=1, grid=(rows // tr,),
                                               in_specs=[full, part, part, full, full], out_specs=[full] * 4),
        out_shape=[SDS(w.shape, F32)] * 4,
        compiler_params=_cparams(dimension_semantics=("arbitrary",)),
    )(core.reshape(1).astype(jnp.int32), w, mine, theirs, m, v)


_WEIGHT_NAMES = ("attn_norm", "w_in", "dn_conv", "dn_a_log", "dn_dt_bias", "dn_out_norm", "swa_q_norm", "swa_k_norm",
                 "swa_sinks", "rel_bias", "w_branch_dn", "w_branch_swa", "w_out", "ffn_norm", "w_gate", "w_up",
                 "w_down")
_CONV_SH = QKVW // N_CHIPS


def kernel(x, attn_norm, w_in, dn_conv, dn_a_log, dn_dt_bias, dn_out_norm, swa_q_norm, swa_k_norm, swa_sinks, rel_bias, w_branch_dn, w_branch_swa, w_out, ffn_norm, w_gate, w_up, w_down, loss_target, m_attn_norm, m_w_in, m_dn_conv, m_dn_a_log, m_dn_dt_bias, m_dn_out_norm, m_swa_q_norm, m_swa_k_norm, m_swa_sinks, m_rel_bias, m_w_branch_dn, m_w_branch_swa, m_w_out, m_ffn_norm, m_w_gate, m_w_up, m_w_down, v_attn_norm, v_w_in, v_dn_conv, v_dn_a_log, v_dn_dt_bias, v_dn_out_norm, v_swa_q_norm, v_swa_k_norm, v_swa_sinks, v_rel_bias, v_w_branch_dn, v_w_branch_swa, v_w_out, v_ffn_norm, v_w_gate, v_w_up, v_w_down):
    w = dict(attn_norm=attn_norm, w_in=w_in, dn_conv=dn_conv, dn_a_log=dn_a_log, dn_dt_bias=dn_dt_bias,
             dn_out_norm=dn_out_norm, swa_q_norm=swa_q_norm, swa_k_norm=swa_k_norm, swa_sinks=swa_sinks,
             rel_bias=rel_bias, w_branch_dn=w_branch_dn, w_branch_swa=w_branch_swa, w_out=w_out, ffn_norm=ffn_norm,
             w_gate=w_gate, w_up=w_up, w_down=w_down)
    m = dict(attn_norm=m_attn_norm, w_in=m_w_in, dn_conv=m_dn_conv, dn_a_log=m_dn_a_log, dn_dt_bias=m_dn_dt_bias,
             dn_out_norm=m_dn_out_norm, swa_q_norm=m_swa_q_norm, swa_k_norm=m_swa_k_norm, swa_sinks=m_swa_sinks,
             rel_bias=m_rel_bias, w_branch_dn=m_w_branch_dn, w_branch_swa=m_w_branch_swa, w_out=m_w_out,
             ffn_norm=m_ffn_norm, w_gate=m_w_gate, w_up=m_w_up, w_down=m_w_down)
    v = dict(attn_norm=v_attn_norm, w_in=v_w_in, dn_conv=v_dn_conv, dn_a_log=v_dn_a_log, dn_dt_bias=v_dn_dt_bias,
             dn_out_norm=v_dn_out_norm, swa_q_norm=v_swa_q_norm, swa_k_norm=v_swa_k_norm, swa_sinks=v_swa_sinks,
             rel_bias=v_rel_bias, w_branch_dn=v_w_branch_dn, w_branch_swa=v_w_branch_swa, w_out=v_w_out,
             ffn_norm=v_ffn_norm, w_gate=v_w_gate, w_up=v_w_up, w_down=v_w_down)
    shapes = {n: w[n].shape for n in _WEIGHT_NAMES}

    def two_d(a):
        return a.reshape(a.shape[-2], a.shape[-1]) if a.ndim == 3 else a

    core = lax.axis_index("c")
    chip = 2 * lax.axis_index("x") + lax.axis_index("y")
    small_shapes = {n: two_d(w[n]).shape for n, _ in _SMALL}
    small_shapes["dn_conv"] = (CONV, QKVW)

    conv_loc = two_d(w["dn_conv"])
    conv_part = lax.dynamic_update_slice(jnp.zeros((CONV, QKVW), F32), jnp.where(core == 0, conv_loc, 0.0),
                                         (0, chip * _CONV_SH))
    conv_full = _all_sum_small(conv_part.reshape(CONV * QKVW // 128, 128), "gather_conv").reshape(CONV, QKVW)

    flipped = ("w_gate", "w_up")

    def natural(a, n):
        return a.transpose(0, 2, 1) if n in flipped else a

    w_bf = [two_d(natural(w[n], n).astype(BF16)) for n in _BIG_NAMES]
    (w_in_g,) = _gather_weights(w_bf[:1], chip)
    windows = _gather_windows(w_bf[1:])
    after_sync = w_in_g[0, :8, :128].astype(F32) + conv_full[0:1, :128]
    send_sems, recv_sems, w_thru, l_thru, token = _split_start(
        "gather_start", w_bf[1:], _own_slot(w_bf[1:], chip), after_sync, windows)

    def late(after):
        lands = _split_wait("gather_wait", w_thru, l_thru, send_sems, recv_sems, after, windows)
        g = dict(zip(_BIG_NAMES[1:], _sibling_fill(lands)))
        return dict(wa=g["w_branch_dn"], wb=g["w_branch_swa"], w_out=g["w_out"].reshape(D, D), wg=g["w_gate"],
                    wu=g["w_up"], wd=g["w_down"])

    wts = dict(w_in_p=_w_in_to_padded(w_in_g), dn_conv=conv_full, late=late)
    for n, _ in _SMALL[:-1]:
        wts[n] = two_d(w[n])
    wts["attn_norm"] = wts["attn_norm"] + token[0:1, 0:1]

    early = {}

    ffn = {}

    def send_ffn(grads):
        gs = [grads["w_gate"], grads["w_up"], grads["w_down"]]
        lands = [lax.empty((N_CHIPS, g.shape[1] // 2, g.shape[2]), g.dtype) for g in gs]
        ffn["sems"], ffn["recv"], ffn["src"], ffn["land"], tok = _split_start(
            "swap_ffn_start", gs, lands, gs[0][0, :8, :128], _swap_windows(gs), _sibling_peer, 1)
        return tok

    def send_early(grads):
        small = [grads["w_branch_dn"], grads["w_branch_swa"], grads["w_out"].reshape(N_CHIPS, CSH, D)]
        big = [grads["w_gate"], grads["w_up"], grads["w_down"]]
        big, got_big = _split_wait("swap_ffn_wait", ffn["src"], ffn["land"], ffn["sems"], ffn["recv"], small[0],
                                   _swap_windows(big), _sibling_peer, with_sources=True)
        gots = list(_swap_halves(small, "swap_halves_early")) + list(got_big)
        parts = _pair_sum(small + list(big), gots, core, "pair_sum_early")
        own = [lax.dynamic_index_in_dim(p, chip, axis=0, keepdims=False) for p in parts]
        early["sems"], early["recv"], early["src"], early["land"], tok = _split_start(
            "exchange_start", parts, _own_slot(own, chip), parts[0][0, :8, :128], _exchange_windows())
        return tok

    last = {}

    def send_in(g_in_p):
        g_in = [_padded_to_w_in(g_in_p)]
        parts = _pair_sum(g_in, _swap_halves(g_in, "swap_halves_in"), core, "pair_sum_in")
        own = [lax.dynamic_index_in_dim(p, chip, axis=0, keepdims=False) for p in parts]
        last["sems"], last["recv"], last["src"], last["land"], tok = _split_start(
            "exchange_in_start", parts, _own_slot(own, chip), parts[0][0, :8, :128], _exchange_windows())
        return tok

    wts["send_ffn"] = send_ffn
    wts["send_early"] = send_early
    wts["send_in"] = send_in
    loss_sum, grad_x, grads = _local_step(x[0], loss_target[0], wts)

    small_sum = _all_sum_small(_pack_small(grads, loss_sum), "all_sum_small")
    loss = small_sum.reshape(-1)[_LOSS_OFF]
    g_small = _unpack_small(small_sum, small_shapes)

    q_early = _split_wait("exchange_wait", early["src"], early["land"], early["sems"], early["recv"], small_sum,
                          _exchange_windows())
    red_early = _chip_sum(list(q_early), "chip_sum_early")
    their_early = _swap_reduced(red_early, "swap_reduced_early")
    g_out, d_out, m_out, v_out = {}, {}, {}, {}
    for n, mine, other in zip(_BIG_NAMES[1:], red_early, their_early):
        res = _adamw_big(natural(w[n], n), mine, other, natural(m[n], n), natural(v[n], n), core, "adamw_" + n)
        g_out[n], d_out[n], m_out[n], v_out[n] = (natural(t, n) for t in res)

    q_in = _split_wait("exchange_in_wait", last["src"], last["land"], last["sems"], last["recv"],
                       d_out[_BIG_NAMES[-1]], _exchange_windows())
    reduced = _chip_sum(list(q_in), "chip_sum_in")
    theirs = _swap_reduced(reduced, "swap_reduced_in")

    def rows1(a):
        return a.transpose(2, 0, 1)

    def unrows1(a):
        return a.transpose(1, 2, 0)

    g_in_blk = jnp.concatenate([jnp.where(core == 0, reduced[0], theirs[0]),
                                jnp.where(core == 0, theirs[0], reduced[0])], axis=0)
    g_in_r = rows1(g_in_blk[None])
    res = _adamw_rows1(rows1(w["w_in"]), g_in_r, rows1(m["w_in"]), rows1(v["w_in"]), "adamw_w_in")
    g_out["w_in"], d_out["w_in"], m_out["w_in"], v_out["w_in"] = (unrows1(t) for t in res)
    g_conv = lax.dynamic_slice(g_small["dn_conv"], (0, chip * _CONV_SH), (CONV, _CONV_SH))
    g_out["dn_conv"] = g_conv.reshape(shapes["dn_conv"])
    d_, m_, v_ = _adamw(conv_loc, g_conv, two_d(m["dn_conv"]), two_d(v["dn_conv"]), "adamw_dn_conv")
    d_out["dn_conv"], m_out["dn_conv"], v_out["dn_conv"] = (t.reshape(shapes["dn_conv"]) for t in (d_, m_, v_))

    def packed(src):
        vals = {n: src[n] for n, _ in _SMALL[:-1]}
        vals["dn_conv"] = jnp.zeros((CONV * QKVW,), F32)
        return _pack_small(vals)

    d_s, m_s, v_s = _adamw(packed(w), small_sum, packed(m), packed(v), "adamw_small")
    d_small, m_small, v_small = (_unpack_small(t, small_shapes) for t in (d_s, m_s, v_s))
    for n, _ in _SMALL[:-1]:
        g_out[n] = g_small[n].reshape(shapes[n])
        d_out[n], m_out[n], v_out[n] = (t[n].reshape(shapes[n]) for t in (d_small, m_small, v_small))

    return (loss, grad_x[None], *[g_out[n] for n in _WEIGHT_NAMES], *[d_out[n] for n in _WEIGHT_NAMES],
            *[m_out[n] for n in _WEIGHT_NAMES], *[v_out[n] for n in _WEIGHT_NAMES])
```

```python
import functools
import math

import numpy as np
import jax
import jax.numpy as jnp
from jax import lax
from jax.experimental import pallas as pl
from jax.experimental.pallas import tpu as pltpu

F32 = jnp.float32
BF16 = jnp.bfloat16
SDS = jax.ShapeDtypeStruct

D = 1024
DN_H = 4
DH = 128
DNW = DN_H * DH
QKVW = 3 * DNW
CONV = 4
CHUNK = 64
SWA_H = 8
SWA_KV = 2
SWA_G = SWA_H // SWA_KV
SWA_D = 64
SWAW = SWA_H * SWA_D
SWAKW = SWA_KV * SWA_D
BLK = 128
NBUCKET = 32
MAXDIST = 128
DFF = 2816
D_IN = QKVW + DNW + 2 * DN_H + SWAW + 2 * SWAKW + 2 * D
EPS = 1e-6
NEG = -1e30

ADAM_LR = 0.001
ADAM_B1 = 0.9
ADAM_B2 = 0.999
ADAM_EPS = 1e-08
ADAM_WD = 0.01
ADAM_STEP = 10

C_QKV, C_Z, C_GATE, C_SQ, C_SK, C_SV, C_BA = 0, 1536, 2048, 4096, 4608, 4736, 4864
PW = 5120
_ORIG_PIECES = (
    (0, QKVW, C_QKV),
    (QKVW, DNW, C_Z),
    (QKVW + DNW, 2 * DN_H, C_BA),
    (QKVW + DNW + 2 * DN_H, SWAW, C_SQ),
    (QKVW + DNW + 2 * DN_H + SWAW, SWAKW, C_SK),
    (QKVW + DNW + 2 * DN_H + SWAW + SWAKW, SWAKW, C_SV),
    (QKVW + DNW + 2 * DN_H + SWAW + 2 * SWAKW, 2 * D, C_GATE),
)

N_CHIPS = 4
FSH = DFF // N_CHIPS
CSH = D // N_CHIPS
VMEM_LIMIT = 48 * 1024 * 1024
MESH = pl.DeviceIdType.MESH

_BIG = (
    ("w_in", D, D_IN // N_CHIPS),
    ("w_branch_dn", DNW, CSH),
    ("w_branch_swa", SWAW, CSH),
    ("w_out", CSH, D),
    ("w_gate", FSH, D),
    ("w_up", FSH, D),
    ("w_down", FSH, D),
)
_BIG_NAMES = tuple(n for n, _, _ in _BIG)

_SMALL = (
    ("attn_norm", D), ("ffn_norm", D), ("dn_out_norm", DH), ("swa_q_norm", SWA_D), ("swa_k_norm", SWA_D),
    ("swa_sinks", SWA_H), ("dn_a_log", DN_H), ("dn_dt_bias", DN_H), ("rel_bias", NBUCKET * SWA_H),
    ("dn_conv", CONV * QKVW),
)
_SMALL_OFF = {}
_o = 0
for _n, _s in _SMALL:
    _SMALL_OFF[_n] = (_o, _s)
    _o += _s
_LOSS_OFF = _o
_SMALL_ROWS = -(-(_o + 1) // (8 * 128)) * 8


def _cparams(**kw):
    return pltpu.CompilerParams(vmem_limit_bytes=VMEM_LIMIT, **kw)


_DIMS = {
    "nn": (((1,), (0,)), ((), ())),
    "nt": (((1,), (1,)), ((), ())),
    "tn": (((0,), (0,)), ((), ())),
    "bnn": (((2,), (1,)), ((0,), (0,))),
    "bnt": (((2,), (2,)), ((0,), (0,))),
    "btn": (((1,), (1,)), ((0,), (0,))),
}


def _raw_dot(a, b, kind, exact):
    if exact:
        prec = lax.Precision.HIGH if exact == "x3" else lax.Precision.HIGHEST
        return lax.dot_general(a, b, _DIMS[kind], precision=prec, preferred_element_type=F32)
    return lax.dot_general(a.astype(BF16), b.astype(BF16), _DIMS[kind], preferred_element_type=F32)


@functools.partial(jax.custom_vjp, nondiff_argnums=(2, 3))
def _dot(a, b, kind, exact):
    return _raw_dot(a, b, kind, exact)


def _dot_fwd(a, b, kind, exact):
    return _raw_dot(a, b, kind, exact), (a, b)


def _dot_bwd(kind, exact, res, g):
    a, b = res
    pre = kind[:-2]
    nn, nt, tn = pre + "nn", pre + "nt", pre + "tn"
    if kind == nn:
        return _dot(g, b, nt, exact), _dot(a, g, tn, exact)
    if kind == nt:
        return _dot(g, b, nn, exact), _dot(g, a, tn, exact)
    return _dot(b, g, nt, exact), _dot(a, g, nn, exact)


_dot.defvjp(_dot_fwd, _dot_bwd)


def _silu(x):
    return x * jax.nn.sigmoid(x)


def _f_rms(x, gain):
    return x * lax.rsqrt(jnp.mean(x * x, axis=-1, keepdims=True) + EPS) * gain


def _f_dn_pre(xs0, xs1, xs2, xs3, ba, cw, alog, dtb):
    rows = xs0.shape[0]
    c = xs0 * cw[0:1] + xs1 * cw[1:2] + xs2 * cw[2:3] + xs3 * cw[3:4]
    qkv = _silu(c)
    qs, ks, bbs, gbs = [], [], [], []
    for h in range(DN_H):
        qh = qkv[:, h * DH:(h + 1) * DH]
        kh = qkv[:, DNW + h * DH:DNW + (h + 1) * DH]
        qs.append(qh * lax.rsqrt(jnp.sum(qh * qh, axis=-1, keepdims=True) + EPS) * (DH ** -0.5))
        ks.append(kh * lax.rsqrt(jnp.sum(kh * kh, axis=-1, keepdims=True) + EPS))
        beta = jax.nn.sigmoid(ba[:, h:h + 1])
        ar = ba[:, DN_H + h:DN_H + h + 1] + dtb[:, h:h + 1]
        softplus = jnp.maximum(ar, 0.0) + jnp.log1p(jnp.exp(-jnp.abs(ar)))
        g = -jnp.exp(alog[:, h:h + 1]) * softplus
        bbs.append(jnp.broadcast_to(beta, (rows, DH)))
        gbs.append(jnp.broadcast_to(g, (rows, DH)))
    return (jnp.concatenate(qs, axis=1), jnp.concatenate(ks, axis=1), qkv[:, 2 * DNW:],
            jnp.concatenate(bbs, axis=1), jnp.concatenate(gbs, axis=1))


def _f_dn_post(o, z, gain):
    ys = []
    for h in range(DN_H):
        oh = o[:, h * DH:(h + 1) * DH]
        zh = z[:, h * DH:(h + 1) * DH]
        ys.append(oh * lax.rsqrt(jnp.mean(oh * oh, axis=-1, keepdims=True) + EPS) * gain * _silu(zh))
    return jnp.concatenate(ys, axis=1)


def _f_merge(pa, pb, ga, gb):
    return jax.nn.sigmoid(ga) * pa + jax.nn.sigmoid(gb) * pb


@jax.custom_vjp
def _f_swiglu(g, u):
    return _silu(g) * u


def _f_swiglu_fwd(g, u):
    return _silu(g) * u, (g, u)


def _f_swiglu_bwd(res, d):
    g, u = res
    s = jax.nn.sigmoid(g)
    act = g * s
    return d * u * (s + act * (1.0 - s)), d * act


_f_swiglu.defvjp(_f_swiglu_fwd, _f_swiglu_bwd)


@jax.custom_vjp
def _unit_lower_inverse(a):
    c = a.shape[-1]
    eye = (lax.broadcasted_iota(jnp.int32, a.shape, 1) == lax.broadcasted_iota(jnp.int32, a.shape, 2)).astype(F32)
    p = -a
    t = eye + p
    for _ in range(max(c.bit_length() - 2, 0)):
        p = _raw_dot(p, p, "bnn", "x3")
        t = t + _raw_dot(t, p, "bnn", "x3")
    return t


def _unit_lower_inverse_fwd(a):
    t = _unit_lower_inverse(a)
    return t, t


def _unit_lower_inverse_bwd(t, g):
    return (-_raw_dot(_raw_dot(t, g, "btn", "x3"), t, "bnt", "x3"),)


_unit_lower_inverse.defvjp(_unit_lower_inverse_fwd, _unit_lower_inverse_bwd)


def _scan_rows(x, reverse):
    c = x.shape[1]
    row = lax.broadcasted_iota(jnp.int32, x.shape, 1)
    shift = 1
    while shift < c:
        if reverse:
            x = x + jnp.where(row < c - shift, pltpu.roll(x, c - shift, 1), 0.0)
        else:
            x = x + jnp.where(row >= shift, pltpu.roll(x, shift, 1), 0.0)
        shift *= 2
    return x


@jax.custom_vjp
def _cumsum_rows(x):
    return _scan_rows(x, False)


def _cumsum_rows_fwd(x):
    return _scan_rows(x, False), None


def _cumsum_rows_bwd(_, g):
    return (_scan_rows(g, True),)


_cumsum_rows.defvjp(_cumsum_rows_fwd, _cumsum_rows_bwd)


@jax.custom_vjp
def _known_inverse(a, t):
    return t


def _known_inverse_fwd(a, t):
    return t, t


def _known_inverse_bwd(t, g):
    return _unit_lower_inverse_bwd(t, g)[0], jnp.zeros_like(t)


_known_inverse.defvjp(_known_inverse_fwd, _known_inverse_bwd)


def _f_chunk(q, k, v, gb, bb, s, t_known=None, with_t=False):
    c = CHUNK
    nh = q.shape[0]
    ii = lax.broadcasted_iota(jnp.int32, (nh, c, c), 1)
    jj = lax.broadcasted_iota(jnp.int32, (nh, c, c), 2)
    incl = ii >= jj
    strict = ii > jj
    eye = (ii == jj).astype(F32)
    gcb = _cumsum_rows(gb)
    gcol = gcb[:, :, :c]
    grow = jnp.swapaxes(gcol, 1, 2)
    decay = jnp.where(incl, jnp.exp(jnp.where(incl, gcol - grow, 0.0)), 0.0)
    kb = k * bb
    vb = v * bb
    a = jnp.where(strict, _dot(kb, k, "bnt", False) * decay, 0.0)
    t = _unit_lower_inverse(a) if t_known is None else _known_inverse(a, t_known)
    eg = jnp.exp(gcb)
    uw = _dot(t, jnp.concatenate([vb, kb * eg], axis=2), "bnn", "x3")
    u, w = uw[:, :, :DH], uw[:, :, DH:]
    qk = jnp.where(incl, _dot(q, k, "bnt", False) * decay, 0.0)
    qe = q * eg
    glast = gcb[:, c - 1:c, :]
    k_dec = k * jnp.exp(glast - gcb)
    e_last = jnp.exp(glast)
    outs = []
    for g in range(nh // DN_H):
        sl = slice(g * DN_H, (g + 1) * DN_H)
        v_new = u[sl] - _dot(w[sl], s, "bnn", False)
        outs.append(_dot(qe[sl], s, "bnn", False) + _dot(qk[sl], v_new, "bnn", False))
        s = s * e_last[sl] + _dot(k_dec[sl], v_new, "btn", False)
    o = jnp.concatenate(outs, axis=0)
    return (o, s, t) if with_t else (o, s)


def _f_swa(q8, kp, kc, vp, vc, bias8, qg, kg, sink, mask):
    kb = jnp.concatenate([kp, kc], axis=1)
    vb = jnp.concatenate([vp, vc], axis=1)
    kn = kb * lax.rsqrt(jnp.mean(kb * kb, axis=-1, keepdims=True) + EPS) * kg

    def rows(per_head):
        return jnp.stack([jnp.concatenate([per_head(kv, g) for g in range(SWA_G)], axis=0)
                          for kv in range(SWA_KV)], axis=0)

    qq = rows(lambda kv, g: q8[kv * SWA_G + g])
    qn = qq * lax.rsqrt(jnp.mean(qq * qq, axis=-1, keepdims=True) + EPS) * qg * (SWA_D ** -0.5)
    lg = _dot(qn, kn, "bnt", False) + rows(lambda kv, g: bias8[kv * SWA_G + g])
    lg = jnp.where(rows(lambda kv, g: mask), lg, NEG)
    sk = rows(lambda kv, g: jnp.broadcast_to(sink[kv][:, g:g + 1], (BLK, 1)))
    m = lax.stop_gradient(jnp.maximum(jnp.max(lg, axis=-1, keepdims=True), sk))
    p = jnp.exp(lg - m)
    den = jnp.sum(p, axis=-1, keepdims=True) + jnp.exp(sk - m)
    out = _dot(p * (1.0 / den), vb, "bnn", False)
    return jnp.stack([out[kv, g * BLK:(g + 1) * BLK] for kv in range(SWA_KV) for g in range(SWA_G)], axis=0)


def _bdot(a, b, kind="nn"):
    return lax.dot_general(a.astype(BF16), b.astype(BF16), _DIMS[kind], preferred_element_type=F32)


def _pc(kern, name, grid, in_specs, out_specs, out_shape, scratch=()):
    return pl.pallas_call(
        kern, name=name, grid=grid, in_specs=in_specs, out_specs=out_specs, out_shape=out_shape,
        scratch_shapes=list(scratch), compiler_params=_cparams(dimension_semantics=("arbitrary",) * len(grid)))


def _mm(a, b, kind, out_dtype, tm, tn, name):
    if kind == "tn":
        k, m = a.shape
    else:
        m, k = a.shape
    n = b.shape[0] if kind == "nt" else b.shape[1]
    tm, tn = min(tm, m), min(tn, n)
    assert m % tm == 0 and n % tn == 0, (name, a.shape, b.shape, tm, tn)

    def kern(a_ref, b_ref, o_ref):
        o_ref[...] = _bdot(a_ref[...], b_ref[...], kind).astype(o_ref.dtype)

    a_spec = pl.BlockSpec((k, tm), lambda i, j: (0, i)) if kind == "tn" else pl.BlockSpec((tm, k), lambda i, j: (i, 0))
    b_spec = pl.BlockSpec((tn, k), lambda i, j: (j, 0)) if kind == "nt" else pl.BlockSpec((k, tn), lambda i, j: (0, j))
    return _pc(kern, name, (m // tm, n // tn), [a_spec, b_spec], pl.BlockSpec((tm, tn), lambda i, j: (i, j)),
               SDS((m, n), out_dtype))(a, b)


def _rows(body, name, m, tm, row_ins, full_ins, row_outs, acc_outs=()):
    n_r, n_f, n_o, n_a = len(row_ins), len(full_ins), len(row_outs), len(acc_outs)
    assert m % tm == 0

    def kern(*refs):
        r = refs[:n_r]
        f = refs[n_r:n_r + n_f]
        o = refs[n_r + n_f:n_r + n_f + n_o]
        acc = refs[n_r + n_f + n_o:]
        outs, sums = body([x[...] for x in r], [x[...] for x in f])
        for ref, val in zip(o, outs, strict=True):
            ref[...] = val.astype(ref.dtype)
        if n_a:
            @pl.when(pl.program_id(0) == 0)
            def _():
                for ref in acc:
                    ref[...] = jnp.zeros(ref.shape, F32)

            for ref, val in zip(acc, sums, strict=True):
                ref[...] += val

    in_specs = [pl.BlockSpec((tm, w), functools.partial(lambda i, cb: (i, cb), cb=cb)) for _, w, cb in row_ins]
    in_specs += [pl.BlockSpec(x.shape, lambda i: (0, 0)) for x in full_ins]
    out_specs = [pl.BlockSpec((tm, w), lambda i: (i, 0)) for w, _ in row_outs]
    out_specs += [pl.BlockSpec(s, lambda i: (0, 0)) for s in acc_outs]
    out_shape = [SDS((m, w), dt) for w, dt in row_outs]
    out_shape += [SDS(s, F32) for s in acc_outs]
    return _pc(kern, name, (m // tm,), in_specs, out_specs, out_shape)(*[x for x, _, _ in row_ins], *full_ins)


def _whole(x):
    return (x, x.shape[1], 0)


def _resident(shape):
    return pl.BlockSpec(shape, lambda i: (0,) * len(shape), pipeline_mode=pl.Buffered(1))


def _row_pieces(tm, piece):
    piece = min(piece, tm)
    return [slice(r, r + piece) for r in range(0, tm, piece)]


def _zero_first(refs):
    @pl.when(pl.program_id(0) == 0)
    def _():
        for ref in refs:
            ref[...] = jnp.zeros(ref.shape, F32)


GROUP = 4


def _heads(ref):
    return jnp.stack([ref[g * CHUNK:(g + 1) * CHUNK, h * DH:(h + 1) * DH]
                      for g in range(GROUP) for h in range(DN_H)], axis=0)


def _unheads(ref, val):
    for g in range(GROUP):
        for h in range(DN_H):
            ref[g * CHUNK:(g + 1) * CHUNK, h * DH:(h + 1) * DH] = val[g * DN_H + h]


def _dn_chunks_fwd(q, k, v, gb, bb):
    s_len = q.shape[0]
    ng = s_len // (GROUP * CHUNK)

    def kern(q_ref, k_ref, v_ref, g_ref, b_ref, o_ref, sall_ref, t_ref, state):
        _zero_first([state])
        s = state[...]
        sall_ref[0] = s
        o, s_new, t = _f_chunk(*[_heads(r) for r in (q_ref, k_ref, v_ref, g_ref, b_ref)], s, with_t=True)
        _unheads(o_ref, o)
        t_ref[0] = t
        state[...] = s_new

    blk = pl.BlockSpec((GROUP * CHUNK, DNW), lambda c: (c, 0))
    return _pc(kern, "dn_chunks_fwd", (ng,), [blk] * 5,
               [blk, pl.BlockSpec((1, DN_H, DH, DH), lambda c: (c, 0, 0, 0)),
                pl.BlockSpec((1, GROUP * DN_H, CHUNK, CHUNK), lambda c: (c, 0, 0, 0))],
               [SDS((s_len, DNW), F32), SDS((ng, DN_H, DH, DH), F32), SDS((ng, GROUP * DN_H, CHUNK, CHUNK), F32)],
               scratch=[pltpu.VMEM((DN_H, DH, DH), F32)])(q, k, v, gb, bb)


def _dn_chunks_bwd(q, k, v, gb, bb, s_all, t_all, d_o):
    s_len = q.shape[0]
    ng = s_len // (GROUP * CHUNK)

    def kern(q_ref, k_ref, v_ref, g_ref, b_ref, sall_ref, t_ref, do_ref, dq_ref, dk_ref, dv_ref, dg_ref, db_ref,
             dstate):
        _zero_first([dstate])
        fn = functools.partial(_f_chunk, t_known=t_ref[0])
        _, vjp = jax.vjp(fn, *[_heads(r) for r in (q_ref, k_ref, v_ref, g_ref, b_ref)], sall_ref[0])
        *d_ins, ds = vjp((_heads(do_ref), dstate[...]))
        for ref, val in zip((dq_ref, dk_ref, dv_ref, dg_ref, db_ref), d_ins, strict=True):
            _unheads(ref, val)
        dstate[...] = ds

    blk = pl.BlockSpec((GROUP * CHUNK, DNW), lambda c: (ng - 1 - c, 0))
    return _pc(kern, "dn_chunks_bwd", (ng,),
               [blk] * 5 + [pl.BlockSpec((1, DN_H, DH, DH), lambda c: (ng - 1 - c, 0, 0, 0)),
                            pl.BlockSpec((1, GROUP * DN_H, CHUNK, CHUNK), lambda c: (ng - 1 - c, 0, 0, 0)), blk],
               [blk] * 5, [SDS((s_len, DNW), F32)] * 5,
               scratch=[pltpu.VMEM((DN_H, DH, DH), F32)])(q, k, v, gb, bb, s_all, t_all, d_o)


def _t5_bucket_table():
    qi = np.arange(BLK)[:, None]
    kj = np.arange(2 * BLK)[None, :]
    dist = BLK + qi - kj
    n = np.maximum(dist, 0)
    max_exact = NBUCKET // 2
    nf = np.maximum(n, 1).astype(np.float32)
    large = max_exact + (np.log(nf / np.float32(max_exact)) / np.float32(math.log(MAXDIST / max_exact))
                         * np.float32(NBUCKET - max_exact)).astype(np.int32)
    large = np.minimum(large, NBUCKET - 1)
    return np.where(n < max_exact, n, large)


def _bucket_onehot_t():
    table = _t5_bucket_table().reshape(-1)
    return (np.arange(NBUCKET)[:, None] == table[None, :]).astype(np.float32)


def _swa_mask(first):
    qi = lax.broadcasted_iota(jnp.int32, (BLK, 2 * BLK), 0)
    kj = lax.broadcasted_iota(jnp.int32, (BLK, 2 * BLK), 1)
    dist = BLK + qi - kj
    window = (dist >= 0) & (dist < BLK)
    return window & ((kj >= BLK) | jnp.logical_not(first))


def _bias_expand(rel_bias_t):
    onehot = jnp.asarray(_bucket_onehot_t())

    def kern(r_ref, oh_ref, o_ref):
        o_ref[...] = _raw_dot(r_ref[...], oh_ref[...], "nn", True)

    return pl.pallas_call(
        kern, name="bias_expand", out_shape=SDS((SWA_H, BLK * 2 * BLK), F32), compiler_params=_cparams(),
    )(rel_bias_t, onehot)


def _bias_reduce(d_bias_flat):
    onehot = jnp.asarray(_bucket_onehot_t())

    def kern(d_ref, oh_ref, o_ref):
        o_ref[...] = _raw_dot(d_ref[...], oh_ref[...], "nt", True)

    return pl.pallas_call(
        kern, name="bias_reduce", out_shape=SDS((SWA_H, NBUCKET), F32), compiler_params=_cparams(),
    )(d_bias_flat, onehot)


def _swa_specs(nb, rev):
    def blk(n):
        return (nb - 1 - n) if rev else n

    def before(n):
        return jnp.maximum(blk(n) - 1, 0)

    q_spec = pl.BlockSpec((BLK, SWAW), lambda n: (blk(n), C_SQ // SWAW))
    k_cur = pl.BlockSpec((BLK, SWAKW), lambda n: (blk(n), C_SK // SWAKW))
    k_prev = pl.BlockSpec((BLK, SWAKW), lambda n: (before(n), C_SK // SWAKW))
    v_cur = pl.BlockSpec((BLK, SWAKW), lambda n: (blk(n), C_SV // SWAKW))
    v_prev = pl.BlockSpec((BLK, SWAKW), lambda n: (before(n), C_SV // SWAKW))
    bias = pl.BlockSpec((SWA_H, BLK, 2 * BLK), lambda n: (0, 0, 0))
    gain = pl.BlockSpec((1, SWA_D), lambda n: (0, 0))
    sink = pl.BlockSpec((SWA_KV, 1, SWA_G), lambda n: (0, 0, 0))
    wide = pl.BlockSpec((BLK, SWAW), lambda n: (blk(n), 0))
    narrow = pl.BlockSpec((BLK, SWAKW), lambda n: (blk(n), 0))
    return [q_spec, k_prev, k_cur, v_prev, v_cur, bias, gain, gain, sink], wide, narrow


def _split_heads(x):
    return jnp.stack([x[:, h * SWA_D:(h + 1) * SWA_D] for h in range(x.shape[1] // SWA_D)], axis=0)


def _join_heads(x):
    return jnp.concatenate([x[h] for h in range(x.shape[0])], axis=1)


def _swa_fwd(proj, bias, qg, kg, sinks):
    s_len = proj.shape[0]
    nb = s_len // BLK
    in_specs, wide, _ = _swa_specs(nb, False)

    def kern(q_ref, kp_ref, kc_ref, vp_ref, vc_ref, b_ref, qg_ref, kg_ref, s_ref, o_ref):
        mask = _swa_mask(pl.program_id(0) == 0)
        o8 = _f_swa(*[_split_heads(r[...]) for r in (q_ref, kp_ref, kc_ref, vp_ref, vc_ref)], b_ref[...], qg_ref[...],
                    kg_ref[...], s_ref[...], mask)
        o_ref[...] = _join_heads(o8).astype(BF16)

    return _pc(kern, "swa_fwd", (nb,), in_specs, wide, SDS((s_len, SWAW), BF16))(
        proj, proj, proj, proj, proj, bias, qg, kg, sinks)


def _swa_bwd(proj, bias, qg, kg, sinks, d_out):
    s_len = proj.shape[0]
    nb = s_len // BLK
    in_specs, wide, narrow = _swa_specs(nb, True)

    def kern(q_ref, kp_ref, kc_ref, vp_ref, vc_ref, b_ref, qg_ref, kg_ref, s_ref, do_ref,
             dq_ref, dk_ref, dv_ref, db_ref, dqg_ref, dkg_ref, ds_ref, carry_k, carry_v):
        n = pl.program_id(0)
        mask = _swa_mask(n == nb - 1)
        _zero_first([carry_k, carry_v, db_ref, ds_ref, dqg_ref, dkg_ref])
        fn = functools.partial(_f_swa, mask=mask)
        _, vjp = jax.vjp(fn, *[_split_heads(r[...]) for r in (q_ref, kp_ref, kc_ref, vp_ref, vc_ref)], b_ref[...],
                         qg_ref[...], kg_ref[...], s_ref[...])
        dq, dkp, dkc, dvp, dvc, dbias, dqg, dkg, dsink = vjp(_split_heads(do_ref[...]))
        dq_ref[...] = _join_heads(dq).astype(BF16)
        dk_ref[...] = (_join_heads(dkc) + carry_k[...]).astype(BF16)
        dv_ref[...] = (_join_heads(dvc) + carry_v[...]).astype(BF16)
        carry_k[...] = _join_heads(dkp)
        carry_v[...] = _join_heads(dvp)
        db_ref[...] += dbias
        dqg_ref[...] += dqg
        dkg_ref[...] += dkg
        ds_ref[...] += dsink

    bias_spec, gain, sink = in_specs[5], in_specs[6], in_specs[8]
    return _pc(
        kern, "swa_bwd", (nb,), in_specs + [wide], [wide, narrow, narrow, bias_spec, gain, gain, sink],
        [SDS((s_len, SWAW), BF16), SDS((s_len, SWAKW), BF16), SDS((s_len, SWAKW), BF16),
         SDS((SWA_H, BLK, 2 * BLK), F32), SDS((1, SWA_D), F32), SDS((1, SWA_D), F32), SDS((SWA_KV, 1, SWA_G), F32)],
        scratch=[pltpu.VMEM((BLK, SWAKW), F32), pltpu.VMEM((BLK, SWAKW), F32)],
    )(proj, proj, proj, proj, proj, bias, qg, kg, sinks, d_out)


def _branch_merge(y_dn, y_swa, wa, wb, proj):
    s_len = y_dn.shape[0]
    tm = min(1024, s_len)

    def kern(ya_ref, yb_ref, wa_ref, wb_ref, ga_ref, gb_ref, pa_ref, pb_ref, m_ref):
        for rows in _row_pieces(tm, 128):
            pa = _bdot(ya_ref[rows, :], wa_ref[0])
            pb = _bdot(yb_ref[rows, :], wb_ref[0])
            pa_ref[rows, :] = pa.astype(BF16)
            pb_ref[rows, :] = pb.astype(BF16)
            m_ref[rows, :] = _f_merge(pa, pb, ga_ref[rows, :], gb_ref[rows, :]).astype(BF16)

    y_spec = pl.BlockSpec((tm, DNW), lambda i, s: (i, 0))
    w_spec = pl.BlockSpec((1, DNW, CSH), lambda i, s: (s, 0, 0))
    o_spec = pl.BlockSpec((tm, CSH), lambda i, s: (i, s))
    ga_spec = pl.BlockSpec((tm, CSH), lambda i, s: (i, C_GATE // CSH + s))
    gb_spec = pl.BlockSpec((tm, CSH), lambda i, s: (i, (C_GATE + D) // CSH + s))
    return _pc(kern, "branch_merge", (s_len // tm, N_CHIPS), [y_spec, y_spec, w_spec, w_spec, ga_spec, gb_spec],
               [o_spec] * 3, [SDS((s_len, D), BF16)] * 3,
               )(y_dn, y_swa, wa, wb, proj, proj)


def _in_proj(x, gain, w_in_p):
    s_len = x.shape[0]
    tm = min(512, s_len)

    def kern(x_ref, g_ref, w_ref, h_ref, p_ref):
        h = _f_rms(x_ref[...], g_ref[...]).astype(BF16)
        h_ref[...] = h
        p_ref[...] = _bdot(h, w_ref[...])

    row = pl.BlockSpec((tm, D), lambda i: (i, 0))
    return _pc(kern, "in_proj", (s_len // tm,),
               [row, pl.BlockSpec((1, D), lambda i: (0, 0)), _resident((D, PW))],
               [row, pl.BlockSpec((tm, PW), lambda i: (i, 0))],
               [SDS((s_len, D), BF16), SDS((s_len, PW), F32)])(x, gain, w_in_p)


def _out_proj(merged, w_out, x, gain):
    s_len = x.shape[0]
    tm = min(512, s_len)

    def kern(m_ref, w_ref, x_ref, g_ref, x1_ref, h2_ref):
        x1 = x_ref[...] + _bdot(m_ref[...], w_ref[...])
        x1_ref[...] = x1
        h2_ref[...] = _f_rms(x1, g_ref[...]).astype(BF16)

    row = pl.BlockSpec((tm, D), lambda i: (i, 0))
    return _pc(kern, "out_proj", (s_len // tm,),
               [row, _resident((D, D)), row, pl.BlockSpec((1, D), lambda i: (0, 0))],
               [row, row], [SDS((s_len, D), F32), SDS((s_len, D), BF16)])(merged, w_out, x, gain)


def _ffn_up(h2, wg, wu):
    s_len = h2.shape[0]
    tm = min(2048, s_len)

    def kern(h_ref, g_ref, u_ref, gt_ref, up_ref, act_ref):
        for rows in _row_pieces(tm, 256):
            h = h_ref[rows, :]
            g = _bdot(h, g_ref[0], "nt")
            u = _bdot(h, u_ref[0], "nt")
            gt_ref[0, rows, :] = g.astype(BF16)
            up_ref[0, rows, :] = u.astype(BF16)
            act_ref[0, rows, :] = _f_swiglu(g, u).astype(BF16)

    w_spec = pl.BlockSpec((1, FSH, D), lambda s, i: (s, 0, 0))
    o_spec = pl.BlockSpec((1, tm, FSH), lambda s, i: (s, i, 0))
    shape = (N_CHIPS, s_len, FSH)
    return _pc(kern, "ffn_up", (N_CHIPS, s_len // tm), [pl.BlockSpec((tm, D), lambda s, i: (i, 0)), w_spec, w_spec],
               [o_spec] * 3, [SDS(shape, BF16)] * 3)(h2, wg, wu)


def _ffn_down_loss(act, wd, x1, target):
    s_len = x1.shape[0]
    tm = min(512, s_len)

    def kern(a_ref, w_ref, x_ref, t_ref, dy_ref, dyb_ref, loss_ref):
        _zero_first([loss_ref])
        for rows in _row_pieces(tm, 128):
            y = x_ref[rows, :]
            for s in range(N_CHIPS):
                y = y + _bdot(a_ref[s, rows, :], w_ref[s])
            d = y - t_ref[rows, :]
            dy = d * (1.0 / D)
            dy_ref[rows, :] = dy
            dyb_ref[rows, :] = dy.astype(BF16)
            loss_ref[...] += jnp.sum(d * d).reshape(1, 1) * (0.5 / D)

    row = pl.BlockSpec((tm, D), lambda i: (i, 0))
    return _pc(kern, "ffn_down_loss", (s_len // tm,),
               [pl.BlockSpec((N_CHIPS, tm, FSH), lambda i: (0, i, 0)),
                _resident((N_CHIPS, FSH, D)), row, row],
               [row, row, pl.BlockSpec((1, 1), lambda i: (0, 0))],
               [SDS((s_len, D), F32), SDS((s_len, D), BF16), SDS((1, 1), F32)])(act, wd, x1, target)


def _ffn_dact(dy_b, wd, gt, up):
    s_len = dy_b.shape[0]
    tm = min(2048, s_len)

    def kern(dy_ref, w_ref, gt_ref, up_ref, dg_ref, du_ref):
        w = w_ref[0]
        for rows in _row_pieces(tm, 256):
            d_act = _bdot(dy_ref[rows, :], w, "nt")
            _, vjp = jax.vjp(_f_swiglu, gt_ref[0, rows, :].astype(F32), up_ref[0, rows, :].astype(F32))
            dg, du = vjp(d_act)
            dg_ref[0, rows, :] = dg.astype(BF16)
            du_ref[0, rows, :] = du.astype(BF16)

    a_spec = pl.BlockSpec((1, tm, FSH), lambda s, i: (s, i, 0))
    shape = (N_CHIPS, s_len, FSH)
    return _pc(kern, "ffn_dact", (N_CHIPS, s_len // tm),
               [pl.BlockSpec((tm, D), lambda s, i: (i, 0)), pl.BlockSpec((1, FSH, D), lambda s, i: (s, 0, 0)),
                a_spec, a_spec],
               [a_spec, a_spec], [SDS(shape, BF16), SDS(shape, BF16)])(dy_b, wd, gt, up)


def _gw_ffn(lhs, rhs, name):
    s_len = rhs.shape[0]
    n = len(lhs)
    tn = 512

    def kern(*refs):
        g = refs[n][...]
        for i in range(n):
            refs[n + 1 + i][0] = _bdot(refs[i][0], g, "tn").astype(BF16)

    a_spec = pl.BlockSpec((1, s_len, FSH), lambda s, j: (s, 0, 0))
    o_spec = pl.BlockSpec((1, FSH, tn), lambda s, j: (s, 0, j))
    return _pc(kern, name, (N_CHIPS, D // tn), [a_spec] * n + [pl.BlockSpec((s_len, tn), lambda s, j: (0, j))],
               [o_spec] * n, [SDS((N_CHIPS, FSH, D), BF16)] * n)(*lhs, rhs)


def _ffn_dh2(d_gt, d_up, wg, wu, x1, dy, gain):
    s_len = x1.shape[0]
    tm = min(512, s_len)

    def kern(dg_ref, du_ref, wg_ref, wu_ref, x_ref, dy_ref, g_ref, dx_ref, dxb_ref, dgain_ref):
        _zero_first([dgain_ref])
        dh2 = jnp.zeros((tm, D), F32)
        for s in range(N_CHIPS):
            dh2 = dh2 + _bdot(dg_ref[s], wg_ref[s]) + _bdot(du_ref[s], wu_ref[s])
        _, vjp = jax.vjp(_f_rms, x_ref[...], g_ref[...])
        dx, dgain = vjp(dh2)
        dx1 = dx + dy_ref[...]
        dx_ref[...] = dx1
        dxb_ref[...] = dx1.astype(BF16)
        dgain_ref[...] += dgain

    row = pl.BlockSpec((tm, D), lambda i: (i, 0))
    d_spec = pl.BlockSpec((N_CHIPS, tm, FSH), lambda i: (0, i, 0))
    w_spec = _resident((N_CHIPS, FSH, D))
    vec = pl.BlockSpec((1, D), lambda i: (0, 0))
    return _pc(kern, "ffn_dh2", (s_len // tm,), [d_spec, d_spec, w_spec, w_spec, row, row, vec],
               [row, row, vec], [SDS((s_len, D), F32), SDS((s_len, D), BF16), SDS((1, D), F32)],
               )(d_gt, d_up, wg, wu, x1, dy, gain)


def _merge_bwd(dx1_b, w_out, pa, pb, proj):
    s_len = dx1_b.shape[0]
    tm = min(512, s_len)

    def kern(dx_ref, w_ref, pa_ref, pb_ref, g_ref, dpa_ref, dpb_ref, dg_ref):
        dm = _bdot(dx_ref[...], w_ref[...], "nt")
        gates = g_ref[...]
        _, vjp = jax.vjp(_f_merge, pa_ref[...].astype(F32), pb_ref[...].astype(F32), gates[:, :D], gates[:, D:])
        dpa, dpb, dga, dgb = vjp(dm)
        dpa_ref[...] = dpa.astype(BF16)
        dpb_ref[...] = dpb.astype(BF16)
        dg_ref[:, :D] = dga.astype(BF16)
        dg_ref[:, D:] = dgb.astype(BF16)

    row = pl.BlockSpec((tm, D), lambda i: (i, 0))
    return _pc(kern, "merge_bwd", (s_len // tm,),
               [row, _resident((D, D)), row, row,
                pl.BlockSpec((tm, 2 * D), lambda i: (i, C_GATE // (2 * D)))],
               [row, row, pl.BlockSpec((tm, 2 * D), lambda i: (i, 0))],
               [SDS((s_len, D), BF16), SDS((s_len, D), BF16), SDS((s_len, 2 * D), BF16)],
               )(dx1_b, w_out, pa, pb, proj)


def _d_branch(d_pa, d_pb, wa, wb):
    s_len = d_pa.shape[0]
    tm = min(512, s_len)

    def kern(da_ref, db_ref, wa_ref, wb_ref, oa_ref, ob_ref):
        acc_a = jnp.zeros((tm, DNW), F32)
        acc_b = jnp.zeros((tm, SWAW), F32)
        for s in range(N_CHIPS):
            acc_a = acc_a + _bdot(da_ref[:, s * CSH:(s + 1) * CSH], wa_ref[s], "nt")
            acc_b = acc_b + _bdot(db_ref[:, s * CSH:(s + 1) * CSH], wb_ref[s], "nt")
        oa_ref[...] = acc_a
        ob_ref[...] = acc_b

    row = pl.BlockSpec((tm, D), lambda i: (i, 0))
    w_spec = pl.BlockSpec((N_CHIPS, DNW, CSH), lambda i: (0, 0, 0))
    out = pl.BlockSpec((tm, DNW), lambda i: (i, 0))
    return _pc(kern, "d_branch", (s_len // tm,), [row, row, w_spec, w_spec], [out, out],
               [SDS((s_len, DNW), F32), SDS((s_len, SWAW), F32)])(d_pa, d_pb, wa, wb)


def _gw_branch(y_dn, y_swa, d_pa, d_pb):
    s_len = y_dn.shape[0]

    def kern(ya_ref, yb_ref, da_ref, db_ref, oa_ref, ob_ref):
        oa_ref[0] = _bdot(ya_ref[...], da_ref[...], "tn").astype(BF16)
        ob_ref[0] = _bdot(yb_ref[...], db_ref[...], "tn").astype(BF16)

    y_spec = pl.BlockSpec((s_len, DNW), lambda s: (0, 0))
    d_spec = pl.BlockSpec((s_len, CSH), lambda s: (0, s))
    o_spec = pl.BlockSpec((1, DNW, CSH), lambda s: (s, 0, 0))
    shape = (N_CHIPS, DNW, CSH)
    return _pc(kern, "gw_branch", (N_CHIPS,), [y_spec, y_spec, d_spec, d_spec], [o_spec, o_spec],
               [SDS(shape, BF16), SDS(shape, BF16)])(y_dn, y_swa, d_pa, d_pb)


def _dh_rms(d_proj, w_in_p, x, dx1, gain):
    s_len = x.shape[0]
    tm = min(512, s_len)

    def kern(dp_ref, w_ref, x_ref, r_ref, g_ref, gx_ref, dgain_ref):
        _zero_first([dgain_ref])
        dh = _bdot(dp_ref[...], w_ref[...], "nt")
        _, vjp = jax.vjp(_f_rms, x_ref[...], g_ref[...])
        dx, dgain = vjp(dh)
        gx_ref[...] = dx + r_ref[...]
        dgain_ref[...] += dgain

    row = pl.BlockSpec((tm, D), lambda i: (i, 0))
    vec = pl.BlockSpec((1, D), lambda i: (0, 0))
    return _pc(kern, "dh_rms", (s_len // tm,),
               [pl.BlockSpec((tm, PW), lambda i: (i, 0)), _resident((D, PW)), row, row, vec],
               [row, vec], [SDS((s_len, D), F32), SDS((1, D), F32)])(d_proj, w_in_p, x, dx1, gain)


HALO = 8


def _rows_down(x, n, above):
    tm = x.shape[0]
    r = pltpu.roll(x, n, 0)
    a = pltpu.roll(above, n, 0)
    top = jnp.where(lax.broadcasted_iota(jnp.int32, above.shape, 0) < n, a, r[0:HALO])
    return jnp.concatenate([top, r[HALO:tm]], axis=0)


def _rows_up(x, n, below):
    tm = x.shape[0]
    r = pltpu.roll(x, tm - n, 0)
    b = pltpu.roll(below, HALO - n, 0)
    bottom = jnp.where(lax.broadcasted_iota(jnp.int32, below.shape, 0) >= HALO - n, b, r[tm - HALO:tm])
    return jnp.concatenate([r[0:tm - HALO], bottom], axis=0)


def _conv_taps(cur_ref, prev_ref, first):
    cur = cur_ref[...]
    above = jnp.where(first, 0.0, prev_ref[...])
    return [_rows_down(cur, n, above) for n in range(CONV - 1, 0, -1)] + [cur]


def _dn_pre_specs(s_len, tm, blk):
    cur = pl.BlockSpec((tm, QKVW), lambda i: (blk(i), 0))
    prev = pl.BlockSpec((HALO, QKVW), lambda i: (jnp.maximum(blk(i) * (tm // HALO) - 1, 0), 0))
    ba = pl.BlockSpec((tm, 128), lambda i: (blk(i), C_BA // 128))
    row = pl.BlockSpec((tm, DNW), lambda i: (blk(i), 0))
    full = [pl.BlockSpec((CONV, QKVW), lambda i: (0, 0)), pl.BlockSpec((1, DN_H), lambda i: (0, 0)),
            pl.BlockSpec((1, DN_H), lambda i: (0, 0))]
    return cur, prev, ba, row, full


def _dn_pre_fwd(proj, conv_w, alog, dtb):
    s_len = proj.shape[0]
    tm = min(256, s_len)
    cur, prev, ba, row, full = _dn_pre_specs(s_len, tm, lambda i: i)

    def kern(cur_ref, prev_ref, ba_ref, cw_ref, al_ref, dt_ref, q_ref, k_ref, v_ref, bb_ref, gb_ref):
        xs = _conv_taps(cur_ref, prev_ref, pl.program_id(0) == 0)
        outs = _f_dn_pre(*xs, ba_ref[...], cw_ref[...], al_ref[...], dt_ref[...])
        for ref, val in zip((q_ref, k_ref, v_ref, bb_ref, gb_ref), outs, strict=True):
            ref[...] = val

    return _pc(kern, "dn_pre_fwd", (s_len // tm,), [cur, prev, ba] + full, [row] * 5,
               [SDS((s_len, DNW), F32)] * 5)(proj, proj, proj, conv_w, alog, dtb)


def _dn_pre_bwd(proj, conv_w, alog, dtb, cots, others):
    s_len = proj.shape[0]
    tm = min(256, s_len)
    nb = s_len // tm
    cur, prev, ba, row, full = _dn_pre_specs(s_len, tm, lambda i: nb - 1 - i)
    n_o = len(others)
    assert QKVW + sum(t.shape[1] for t in others) + 128 == C_BA + 128

    def kern(cur_ref, prev_ref, ba_ref, cw_ref, al_ref, dt_ref, dq_ref, dk_ref, dv_ref, dbb_ref, dgb_ref, *rest):
        o_refs = rest[:n_o]
        dproj_ref, dcw_ref, dal_ref, ddt_ref, *tails = rest[n_o:]
        i = pl.program_id(0)
        _zero_first([dcw_ref, dal_ref, ddt_ref] + tails)
        xs = _conv_taps(cur_ref, prev_ref, i == nb - 1)
        _, vjp = jax.vjp(_f_dn_pre, *xs, ba_ref[...], cw_ref[...], al_ref[...], dt_ref[...])
        *dxs, dba, dcw, dal, ddt = vjp((dq_ref[...], dk_ref[...], dv_ref[...], dbb_ref[...], dgb_ref[...]))
        total = dxs[CONV - 1]
        for j, t in enumerate(tails):
            n = CONV - 1 - j
            total = total + _rows_up(dxs[j], n, t[...])
            t[...] = dxs[j][0:HALO, :]
        dproj_ref[...] = jnp.concatenate(
            [total.astype(BF16)] + [r[...] for r in o_refs] + [dba.astype(BF16), jnp.zeros((tm, PW - C_BA - 128), BF16)],
            axis=1)
        dcw_ref[...] += dcw
        dal_ref[...] += dal
        ddt_ref[...] += ddt

    o_specs = [pl.BlockSpec((tm, t.shape[1]), lambda i: (nb - 1 - i, 0)) for t in others]
    return _pc(kern, "dn_pre_bwd", (nb,), [cur, prev, ba] + full + [row] * 5 + o_specs,
               [pl.BlockSpec((tm, PW), lambda i: (nb - 1 - i, 0))] + full,
               [SDS((s_len, PW), BF16), SDS((CONV, QKVW), F32), SDS((1, DN_H), F32), SDS((1, DN_H), F32)],
               scratch=[pltpu.VMEM((HALO, QKVW), F32)] * (CONV - 1))(proj, proj, proj, conv_w, alog, dtb, *cots, *others)


def _w_in_to_padded(w_sh):
    tr = 256

    def kern(w_ref, o_ref):
        full = jnp.concatenate([w_ref[s] for s in range(N_CHIPS)], axis=1)
        pieces = [full[:, o0:o0 + w] for o0, w, _ in sorted(_ORIG_PIECES, key=lambda t: t[2])]
        o_ref[...] = jnp.concatenate(pieces + [jnp.zeros((tr, PW - D_IN), w_ref.dtype)], axis=1)

    return _pc(kern, "w_in_to_padded", (D // tr,), [pl.BlockSpec((N_CHIPS, tr, D_IN // N_CHIPS), lambda i: (0, i, 0))],
               pl.BlockSpec((tr, PW), lambda i: (i, 0)), SDS((D, PW), w_sh.dtype))(w_sh)


def _padded_to_w_in(g):
    tr = 256
    csh = D_IN // N_CHIPS

    def kern(g_ref, o_ref):
        x = g_ref[...]
        full = jnp.concatenate([x[:, p0:p0 + w] for _, w, p0 in _ORIG_PIECES], axis=1)
        for s in range(N_CHIPS):
            o_ref[s] = full[:, s * csh:(s + 1) * csh]

    return _pc(kern, "padded_to_w_in", (D // tr,), [pl.BlockSpec((tr, PW), lambda i: (i, 0))],
               pl.BlockSpec((N_CHIPS, tr, csh), lambda i: (0, i, 0)), SDS((N_CHIPS, D, csh), g.dtype))(g)


def _local_step(x, target, wts):
    s_len = x.shape[0]
    tm = min(512, s_len)
    w_in_p = wts["w_in_p"]
    attn_gain = wts["attn_norm"]
    ffn_gain = wts["ffn_norm"]
    conv_w = wts["dn_conv"]
    alog, dtb, out_gain = wts["dn_a_log"], wts["dn_dt_bias"], wts["dn_out_norm"]
    qg, kg = wts["swa_q_norm"], wts["swa_k_norm"]
    sinks = wts["swa_sinks"].reshape(SWA_KV, 1, SWA_G)

    h, proj = _in_proj(x, attn_gain, w_in_p)
    q_dn, k_dn, v_dn, bb, gb = _dn_pre_fwd(proj, conv_w, alog, dtb)
    o_dn, s_all, t_all = _dn_chunks_fwd(q_dn, k_dn, v_dn, gb, bb)
    post_ins = [_whole(o_dn), (proj, DNW, C_Z // DNW)]
    (y_dn,) = _rows(lambda r, f: ([_f_dn_post(r[0], r[1], f[0])], []), "dn_post_fwd", s_len, tm, post_ins,
                    [out_gain], [(DNW, BF16)])

    bias = _bias_expand(wts["rel_bias"].T).reshape(SWA_H, BLK, 2 * BLK)
    y_swa = _swa_fwd(proj, bias, qg, kg, sinks)

    wts = {**wts, **wts["late"](y_swa)}
    p_a, p_b, merged = _branch_merge(y_dn, y_swa, wts["wa"], wts["wb"], proj)
    x1, h2 = _out_proj(merged, wts["w_out"], x, ffn_gain)
    gt, up, act = _ffn_up(h2, wts["wg"], wts["wu"])
    dy, dy_b, loss = _ffn_down_loss(act, wts["wd"], x1, target)

    grads = {}
    d_gt, d_up = _ffn_dact(dy_b, wts["wd"], gt, up)
    (grads["w_down"],) = _gw_ffn([act], dy_b, "gw_down")
    grads["w_gate"], grads["w_up"] = _gw_ffn([d_gt, d_up], h2, "gw_gate_up")
    token = wts["send_ffn"](grads)
    dx1, dx1_b, grads["ffn_norm"] = _ffn_dh2(d_gt, d_up, wts["wg"], wts["wu"], x1, dy,
                                             ffn_gain + token[0:1, 0:1])
    grads["w_out"] = _mm(merged, dx1_b, "tn", BF16, 512, 512, "gw_out")
    d_pa, d_pb, d_gr = _merge_bwd(dx1_b, wts["w_out"], p_a, p_b, proj)
    d_ydn, d_yswa = _d_branch(d_pa, d_pb, wts["wa"], wts["wb"])
    grads["w_branch_dn"], grads["w_branch_swa"] = _gw_branch(y_dn, y_swa, d_pa, d_pb)
    token = wts["send_early"](grads)
    qg_t = qg + token[0:1, 0:1]
    out_gain_t = out_gain + token[0:1, 0:1]

    d_sq, d_sk, d_sv, d_bias, grads["swa_q_norm"], grads["swa_k_norm"], d_sinks = _swa_bwd(
        proj, bias, qg_t, kg, sinks, d_yswa)
    grads["swa_sinks"] = d_sinks.reshape(1, SWA_H)
    grads["rel_bias"] = _bias_reduce(d_bias.reshape(SWA_H, BLK * 2 * BLK)).T

    def post_bwd(r, f):
        _, vjp = jax.vjp(_f_dn_post, r[0], r[1], f[0])
        d_o, d_z, d_gain = vjp(r[2])
        return [d_o, d_z], [d_gain]

    d_o, d_z, grads["dn_out_norm"] = _rows(post_bwd, "dn_post_bwd", s_len, tm, post_ins + [_whole(d_ydn)], [out_gain_t],
                                           [(DNW, F32), (DNW, BF16)], [(1, DH)])
    d_q, d_k, d_v, d_gb, d_bb = _dn_chunks_bwd(q_dn, k_dn, v_dn, gb, bb, s_all, t_all, d_o)

    d_proj, grads["dn_conv"], grads["dn_a_log"], grads["dn_dt_bias"] = _dn_pre_bwd(
        proj, conv_w, alog, dtb, (d_q, d_k, d_v, d_bb, d_gb), (d_z, d_gr, d_sq, d_sk, d_sv))
    grads["w_in_p"] = _mm(h, d_proj, "tn", BF16, 512, 1024, "gw_in")
    token = wts["send_in"](grads["w_in_p"])
    grad_x, grads["attn_norm"] = _dh_rms(d_proj, w_in_p, x, dx1, attn_gain + token[0:1, 0:1])
    return loss, grad_x, grads


_HBM = pl.BlockSpec(memory_space=pl.ANY)


def _place():
    return lax.axis_index("x"), lax.axis_index("y"), lax.axis_index("c")


def _other_chips(x, y):
    return [(1 - x, y), (x, 1 - y), (1 - x, 1 - y)]


def _rcopy(src, dst, send_sems, recv_sems, k, to):
    return pltpu.make_async_remote_copy(src_ref=src, dst_ref=dst, send_sem=send_sems.at[k], recv_sem=recv_sems.at[k],
                                        device_id=to, device_id_type=MESH)


def _comm_call(body, name, ins, out_shapes, n_remote, landing=0):
    first = len(ins) - landing
    return pl.pallas_call(
        body, name=name, in_specs=[_HBM] * len(ins), out_specs=[_HBM] * len(out_shapes), out_shape=out_shapes,
        scratch_shapes=[pltpu.SemaphoreType.DMA((n_remote,)), pltpu.SemaphoreType.DMA((n_remote,))],
        input_output_aliases={first + i: i for i in range(landing)},
        compiler_params=_cparams(has_side_effects=True),
    )(*ins)


def _own_slot(blocks, chip):
    return [lax.dynamic_update_slice(lax.empty((N_CHIPS,) + b.shape, b.dtype), b[None], (chip, 0, 0)) for b in blocks]


def _gather_weights(ws, chip):
    n = len(ws)
    halves = [w.shape[0] // 2 for w in ws]

    def body(*refs):
        w_refs, o_refs = refs[:n], refs[2 * n:3 * n]
        send_sems, recv_sems = refs[3 * n:]
        x, y, c = _place()
        s = 2 * x + y
        sib = (x, y, 1 - c)
        chips = _other_chips(x, y)

        def rows(i, half):
            return pl.ds(half * halves[i], halves[i])

        first = []
        for j, (cx, cy) in enumerate(chips):
            for i in range(n):
                cp = _rcopy(w_refs[i].at[rows(i, c), :], o_refs[i].at[s, rows(i, c), :], send_sems, recv_sems,
                            j * n + i, (cx, cy, c))
                cp.start()
                first.append(cp)
        passed = []
        for j, (cx, cy) in enumerate(chips):
            sj = 2 * cx + cy
            for i in range(n):
                blk = o_refs[i].at[sj, rows(i, c), :]
                _rcopy(blk, blk, send_sems, recv_sems, j * n + i, (cx, cy, c)).wait_recv()
                cp = _rcopy(blk, blk, send_sems, recv_sems, (3 + j) * n + i, sib)
                cp.start()
                passed.append(cp)
        for j, (cx, cy) in enumerate(chips):
            sj = 2 * cx + cy
            for i in range(n):
                blk = o_refs[i].at[sj, rows(i, 1 - c), :]
                _rcopy(blk, blk, send_sems, recv_sems, (3 + j) * n + i, sib).wait_recv()
        for cp in first + passed:
            cp.wait_send()

    return _comm_call(body, "gather_weights", list(ws) + _own_slot(ws, chip),
                      [SDS((N_CHIPS,) + w.shape, w.dtype) for w in ws], 6 * n, landing=n)


_HBM_ONLY = pl.BlockSpec(memory_space=pltpu.HBM)
_SEM = pl.BlockSpec(memory_space=pltpu.SEMAPHORE)
_DATAFLOW = pltpu.SideEffectType.DATAFLOW_SIDE_EFFECTING


def _in_hbm(a):
    return pltpu.with_memory_space_constraint(a, pltpu.HBM)


def _gather_windows(blocks):
    halves = [b.shape[0] // 2 for b in blocks]

    def src_at(ref, i, c, sj):
        return ref.at[pl.ds(c * halves[i], halves[i]), :]

    def dst_at(ref, i, c, s_from):
        return ref.at[s_from, pl.ds(c * halves[i], halves[i]), :]

    return src_at, dst_at


def _exchange_windows():
    return (lambda ref, i, c, sj: ref.at[sj]), (lambda ref, i, c, s_from: ref.at[s_from])


def _swap_windows(gs):
    halves = [g.shape[1] // 2 for g in gs]
    return ((lambda ref, i, c, tag: ref.at[:, pl.ds((1 - c) * halves[i], halves[i]), :]),
            (lambda ref, i, c, slot: ref))


def _chip_peers(x, y, c):
    return [(2 * cx + cy, (cx, cy, c), 2 * x + y, 2 * cx + cy) for cx, cy in _other_chips(x, y)]


def _sibling_peer(x, y, c):
    return [(0, (x, y, 1 - c), 0, 0)]


def _split_start(name, ws, lands, dep, windows, peers=_chip_peers, n_peers=3):
    n = len(ws)
    src_at, dst_at = windows

    def body(*refs):
        w_refs, l_refs = refs[:n], refs[n:2 * n]
        send_sems, recv_sems = refs[2 * n + 1], refs[2 * n + 2]
        token = refs[-1]
        x, y, c = _place()
        for j, (tag, dev, there, _) in enumerate(peers(x, y, c)):
            for i in range(n):
                _rcopy(src_at(w_refs[i], i, c, tag), dst_at(l_refs[i], i, c, there), send_sems, recv_sems,
                       j * n + i, dev).start()
        token[...] = jnp.zeros_like(token)

    outs = pl.pallas_call(
        body, name=name,
        out_shape=(pltpu.SemaphoreType.DMA((n_peers * n,)), pltpu.SemaphoreType.DMA((n_peers * n,)),
                   *[pltpu.HBM(w.shape, w.dtype) for w in ws], *[pltpu.HBM(t.shape, t.dtype) for t in lands],
                   SDS((8, 128), F32)),
        in_specs=[_HBM_ONLY] * (2 * n) + [pl.BlockSpec(memory_space=pl.ANY)],
        out_specs=(_SEM, _SEM, *[_HBM_ONLY] * (2 * n), pl.BlockSpec(memory_space=pltpu.VMEM)),
        input_output_aliases={i: 2 + i for i in range(2 * n)},
        compiler_params=pltpu.CompilerParams(has_side_effects=_DATAFLOW),
    )(*[_in_hbm(w) for w in ws], *[_in_hbm(t) for t in lands], dep)
    return outs[0], outs[1], outs[2:2 + n], outs[2 + n:2 + 2 * n], outs[-1]


def _split_wait(name, w_thru, l_thru, send_sems, recv_sems, after, windows, peers=_chip_peers, with_sources=False):
    n = len(w_thru)
    src_at, dst_at = windows

    def body(*refs):
        w_refs, l_refs = refs[:n], refs[n:2 * n]
        send_sems, recv_sems = refs[2 * n], refs[2 * n + 1]
        x, y, c = _place()
        for j, (tag, dev, _, here) in enumerate(peers(x, y, c)):
            for i in range(n):
                cp = _rcopy(src_at(w_refs[i], i, c, tag), dst_at(l_refs[i], i, c, here), send_sems, recv_sems,
                            j * n + i, dev)
                cp.wait_send()
                cp.wait_recv()

    outs = pl.pallas_call(
        body, name=name,
        out_shape=[pltpu.HBM(w.shape, w.dtype) for w in w_thru] + [pltpu.HBM(t.shape, t.dtype) for t in l_thru],
        in_specs=[_HBM_ONLY] * (2 * n) + [_SEM, _SEM, pl.BlockSpec(memory_space=pl.ANY)],
        out_specs=[_HBM_ONLY] * (2 * n),
        input_output_aliases={i: i for i in range(2 * n)},
        compiler_params=pltpu.CompilerParams(has_side_effects=_DATAFLOW),
    )(*w_thru, *l_thru, send_sems, recv_sems, after)
    return (outs[:n], outs[n:]) if with_sources else outs[n:]


def _sibling_fill(lands):
    n = len(lands)
    halves = [t.shape[1] // 2 for t in lands]

    def body(*refs):
        o_refs = refs[n:2 * n]
        send_sems, recv_sems = refs[2 * n:]
        x, y, c = _place()
        sib = (x, y, 1 - c)
        chips = _other_chips(x, y)
        sent = []
        for j, (cx, cy) in enumerate(chips):
            for i in range(n):
                blk = o_refs[i].at[2 * cx + cy, pl.ds(c * halves[i], halves[i]), :]
                cp = _rcopy(blk, blk, send_sems, recv_sems, j * n + i, sib)
                cp.start()
                sent.append(cp)
        for j, (cx, cy) in enumerate(chips):
            for i in range(n):
                blk = o_refs[i].at[2 * cx + cy, pl.ds((1 - c) * halves[i], halves[i]), :]
                _rcopy(blk, blk, send_sems, recv_sems, j * n + i, sib).wait_recv()
        for cp in sent:
            cp.wait_send()

    return _comm_call(body, "sibling_fill", list(lands), [SDS(t.shape, t.dtype) for t in lands], 3 * n, landing=n)


def _swap_halves(gs, name):
    n = len(gs)
    halves = [g.shape[1] // 2 for g in gs]

    def body(*refs):
        g_refs, o_refs = refs[:n], refs[n:2 * n]
        send_sems, recv_sems = refs[2 * n:]
        x, y, c = _place()
        cps = [_rcopy(g_refs[i].at[:, pl.ds((1 - c) * halves[i], halves[i]), :], o_refs[i], send_sems, recv_sems, i,
                      (x, y, 1 - c)) for i in range(n)]
        for cp in cps:
            cp.start()
        for cp in cps:
            cp.wait()

    return _comm_call(body, name, gs, [SDS((N_CHIPS, h, g.shape[2]), g.dtype) for g, h in zip(gs, halves)], n)


def _swap_reduced(rs, name):
    n = len(rs)

    def body(*refs):
        r_refs, o_refs = refs[:n], refs[n:2 * n]
        send_sems, recv_sems = refs[2 * n:]
        x, y, c = _place()
        cps = [_rcopy(r_refs[i], o_refs[i], send_sems, recv_sems, i, (x, y, 1 - c)) for i in range(n)]
        for cp in cps:
            cp.start()
        for cp in cps:
            cp.wait()

    return _comm_call(body, name, rs, [SDS(r.shape, r.dtype) for r in rs], n)


def _all_sum_small(vec, name):
    n_dev = 8
    flips = [(bx, by, bc) for bx in (0, 1) for by in (0, 1) for bc in (0, 1)][1:]

    def body(v_ref, out_ref, gath, send_sems, recv_sems):
        x, y, c = _place()
        me = 4 * x + 2 * y + c
        gath[me] = v_ref[...]
        sent = []
        for k, (bx, by, bc) in enumerate(flips):
            peer = (x ^ bx, y ^ by, c ^ bc)
            cp = _rcopy(v_ref, gath.at[me], send_sems, recv_sems, k, peer)
            cp.start()
            sent.append(cp)
        for k, (bx, by, bc) in enumerate(flips):
            peer = (x ^ bx, y ^ by, c ^ bc)
            _rcopy(v_ref, gath.at[4 * peer[0] + 2 * peer[1] + peer[2]], send_sems, recv_sems, k, peer).wait_recv()
        for cp in sent:
            cp.wait_send()
        acc = gath[0]
        for d in range(1, n_dev):
            acc = acc + gath[d]
        out_ref[...] = acc

    vm = pl.BlockSpec(memory_space=pltpu.VMEM)
    return pl.pallas_call(
        body, name=name, in_specs=[vm], out_specs=vm, out_shape=SDS(vec.shape, F32),
        scratch_shapes=[pltpu.VMEM((n_dev,) + vec.shape, F32), pltpu.SemaphoreType.DMA((7,)),
                        pltpu.SemaphoreType.DMA((7,))],
        compiler_params=_cparams(has_side_effects=True),
    )(vec)


def _pack_small(vals, extra=None):
    parts = [vals[n].reshape(-1).astype(F32) for n, _ in _SMALL]
    parts.append(jnp.zeros((1,), F32) if extra is None else extra.reshape(1).astype(F32))
    flat = jnp.concatenate(parts)
    flat = jnp.concatenate([flat, jnp.zeros((_SMALL_ROWS * 128 - flat.shape[0],), F32)])
    return flat.reshape(_SMALL_ROWS, 128)


def _unpack_small(packed, shapes):
    flat = packed.reshape(-1)
    return {n: flat[_SMALL_OFF[n][0]:_SMALL_OFF[n][0] + _SMALL_OFF[n][1]].reshape(shapes[n]) for n, _ in _SMALL}


def _pair_sum(gs, gots, core, name):
    n = len(gs)

    def kern(c_ref, *refs):
        for i in range(n):
            refs[2 * n + i][...] = (refs[i][...].astype(F32) + refs[n + i][...].astype(F32)).astype(BF16)

    in_specs = [pl.BlockSpec((1, t.shape[1], t.shape[2]), lambda s, c_ref: (s, c_ref[0], 0)) for t in gots]
    in_specs += [pl.BlockSpec((1, t.shape[1], t.shape[2]), lambda s, c_ref: (s, 0, 0)) for t in gots]
    out_specs = [pl.BlockSpec((1, t.shape[1], t.shape[2]), lambda s, c_ref: (s, 0, 0)) for t in gots]
    return pl.pallas_call(
        kern, name=name,
        grid_spec=pltpu.PrefetchScalarGridSpec(num_scalar_prefetch=1, grid=(N_CHIPS,), in_specs=in_specs,
                                               out_specs=out_specs),
        out_shape=[SDS(t.shape, BF16) for t in gots],
        compiler_params=_cparams(dimension_semantics=("arbitrary",)),
    )(core.reshape(1).astype(jnp.int32), *gs, *gots)


def _chip_sum(qs, name):
    n = len(qs)

    def kern(*refs):
        for i in range(n):
            acc = refs[i][0].astype(F32)
            for s in range(1, N_CHIPS):
                acc = acc + refs[i][s].astype(F32)
            refs[n + i][...] = acc

    in_specs = [pl.BlockSpec((N_CHIPS, q.shape[1] // 2, q.shape[2]), lambda j: (0, j, 0)) for q in qs]
    out_specs = [pl.BlockSpec((q.shape[1] // 2, q.shape[2]), lambda j: (j, 0)) for q in qs]
    return _pc(kern, name, (2,), in_specs, out_specs, [SDS(q.shape[1:], F32) for q in qs])(*qs)


def _adam_math(w_, g_, m_, v_):
    m_ = ADAM_B1 * m_ + (1.0 - ADAM_B1) * g_
    v_ = ADAM_B2 * v_ + (1.0 - ADAM_B2) * jnp.square(g_)
    m_hat = m_ / (1.0 - ADAM_B1 ** ADAM_STEP)
    v_hat = v_ / (1.0 - ADAM_B2 ** ADAM_STEP)
    return -ADAM_LR * (m_hat / (jnp.sqrt(v_hat) + ADAM_EPS) + ADAM_WD * w_), m_, v_


def _adamw(w, g, m, v, name):
    rows, cols = w.shape
    tr = rows
    for cand in (256, 128, 64, 32, 16, 8):
        if rows % cand == 0 and rows > cand:
            tr = cand
            break

    def kern(w_ref, g_ref, m_ref, v_ref, d_ref, nm_ref, nv_ref):
        d_ref[...], nm_ref[...], nv_ref[...] = _adam_math(w_ref[...], g_ref[...], m_ref[...], v_ref[...])

    spec = pl.BlockSpec((tr, cols), lambda i: (i, 0))
    return _pc(kern, name, (rows // tr,), [spec] * 4, [spec] * 3, [SDS(w.shape, F32)] * 3)(w, g, m, v)


def _adamw_rows1(w, g, m, v, name):
    rows, _, cols = w.shape
    tr = next(t for t in (203, 174, 128, 64, 42, 32, 29, 16, 8, 7, 6, 4, 3, 2, 1) if rows % t == 0)

    def kern(w_ref, g_ref, m_ref, v_ref, go_ref, d_ref, nm_ref, nv_ref):
        g_ = g_ref[...]
        go_ref[...] = g_
        d_ref[...], nm_ref[...], nv_ref[...] = _adam_math(w_ref[...], g_, m_ref[...], v_ref[...])

    spec = pl.BlockSpec((tr, 1, cols), lambda i: (i, 0, 0))
    return _pc(kern, name, (rows // tr,), [spec] * 4, [spec] * 4, [SDS(w.shape, F32)] * 4)(w, g, m, v)


def _adamw_big(w, mine, theirs, m, v, core, name):
    _, rows, cols = w.shape
    half = rows // 2
    tr = next(t for t in (256, 176, 128, 64, 32, 16, 8) if half % t == 0)
    nbh = half // tr

    def kern(c_ref, w_ref, a_ref, b_ref, m_ref, v_ref, g_ref, d_ref, nm_ref, nv_ref):
        g_ = jnp.where(pl.program_id(0) // nbh == c_ref[0], a_ref[...], b_ref[...])
        g_ref[0] = g_
        d_ref[0], nm_ref[0], nv_ref[0] = _adam_math(w_ref[0], g_, m_ref[0], v_ref[0])

    full = pl.BlockSpec((1, tr, cols), lambda i, c_ref: (0, i, 0))
    part = pl.BlockSpec((tr, cols), lambda i, c_ref: (i % nbh, 0))
    return pl.pallas_call(
        kern, name=name,
        grid_spec=pltpu.PrefetchScalarGridSpec(num_scalar_prefetch=1, grid=(rows // tr,),
                                               in_specs=[full, part, part, full, full], out_specs=[full] * 4),
        out_shape=[SDS(w.shape, F32)] * 4,
        compiler_params=_cparams(dimension_semantics=("arbitrary",)),
    )(core.reshape(1).astype(jnp.int32), w, mine, theirs, m, v)


_WEIGHT_NAMES = ("attn_norm", "w_in", "dn_conv", "dn_a_log", "dn_dt_bias", "dn_out_norm", "swa_q_norm", "swa_k_norm",
                 "swa_sinks", "rel_bias", "w_branch_dn", "w_branch_swa", "w_out", "ffn_norm", "w_gate", "w_up",
                 "w_down")
_CONV_SH = QKVW // N_CHIPS


def kernel(x, attn_norm, w_in, dn_conv, dn_a_log, dn_dt_bias, dn_out_norm, swa_q_norm, swa_k_norm, swa_sinks, rel_bias, w_branch_dn, w_branch_swa, w_out, ffn_norm, w_gate, w_up, w_down, loss_target, m_attn_norm, m_w_in, m_dn_conv, m_dn_a_log, m_dn_dt_bias, m_dn_out_norm, m_swa_q_norm, m_swa_k_norm, m_swa_sinks, m_rel_bias, m_w_branch_dn, m_w_branch_swa, m_w_out, m_ffn_norm, m_w_gate, m_w_up, m_w_down, v_attn_norm, v_w_in, v_dn_conv, v_dn_a_log, v_dn_dt_bias, v_dn_out_norm, v_swa_q_norm, v_swa_k_norm, v_swa_sinks, v_rel_bias, v_w_branch_dn, v_w_branch_swa, v_w_out, v_ffn_norm, v_w_gate, v_w_up, v_w_down):
    w = dict(attn_norm=attn_norm, w_in=w_in, dn_conv=dn_conv, dn_a_log=dn_a_log, dn_dt_bias=dn_dt_bias,
             dn_out_norm=dn_out_norm, swa_q_norm=swa_q_norm, swa_k_norm=swa_k_norm, swa_sinks=swa_sinks,
             rel_bias=rel_bias, w_branch_dn=w_branch_dn, w_branch_swa=w_branch_swa, w_out=w_out, ffn_norm=ffn_norm,
             w_gate=w_gate, w_up=w_up, w_down=w_down)
    m = dict(attn_norm=m_attn_norm, w_in=m_w_in, dn_conv=m_dn_conv, dn_a_log=m_dn_a_log, dn_dt_bias=m_dn_dt_bias,
             dn_out_norm=m_dn_out_norm, swa_q_norm=m_swa_q_norm, swa_k_norm=m_swa_k_norm, swa_sinks=m_swa_sinks,
             rel_bias=m_rel_bias, w_branch_dn=m_w_branch_dn, w_branch_swa=m_w_branch_swa, w_out=m_w_out,
             ffn_norm=m_ffn_norm, w_gate=m_w_gate, w_up=m_w_up, w_down=m_w_down)
    v = dict(attn_norm=v_attn_norm, w_in=v_w_in, dn_conv=v_dn_conv, dn_a_log=v_dn_a_log, dn_dt_bias=v_dn_dt_bias,
             dn_out_norm=v_dn_out_norm, swa_q_norm=v_swa_q_norm, swa_k_norm=v_swa_k_norm, swa_sinks=v_swa_sinks,
             rel_bias=v_rel_bias, w_branch_dn=v_w_branch_dn, w_branch_swa=v_w_branch_swa, w_out=v_w_out,
             ffn_norm=v_ffn_norm, w_gate=v_w_gate, w_up=v_w_up, w_down=v_w_down)
    shapes = {n: w[n].shape for n in _WEIGHT_NAMES}

    def two_d(a):
        return a.reshape(a.shape[-2], a.shape[-1]) if a.ndim == 3 else a

    core = lax.axis_index("c")
    chip = 2 * lax.axis_index("x") + lax.axis_index("y")
    small_shapes = {n: two_d(w[n]).shape for n, _ in _SMALL}
    small_shapes["dn_conv"] = (CONV, QKVW)

    conv_loc = two_d(w["dn_conv"])
    conv_part = lax.dynamic_update_slice(jnp.zeros((CONV, QKVW), F32), jnp.where(core == 0, conv_loc, 0.0),
                                         (0, chip * _CONV_SH))
    conv_full = _all_sum_small(conv_part.reshape(CONV * QKVW // 128, 128), "gather_conv").reshape(CONV, QKVW)

    flipped = ("w_gate", "w_up")

    def natural(a, n):
        return a.transpose(0, 2, 1) if n in flipped else a

    w_bf = [two_d(natural(w[n], n).astype(BF16)) for n in _BIG_NAMES]
    (w_in_g,) = _gather_weights(w_bf[:1], chip)
    windows = _gather_windows(w_bf[1:])
    after_sync = w_in_g[0, :8, :128].astype(F32) + conv_full[0:1, :128]
    send_sems, recv_sems, w_thru, l_thru, token = _split_start(
        "gather_start", w_bf[1:], _own_slot(w_bf[1:], chip), after_sync, windows)

    def late(after):
        lands = _split_wait("gather_wait", w_thru, l_thru, send_sems, recv_sems, after, windows)
        g = dict(zip(_BIG_NAMES[1:], _sibling_fill(lands)))
        return dict(wa=g["w_branch_dn"], wb=g["w_branch_swa"], w_out=g["w_out"].reshape(D, D), wg=g["w_gate"],
                    wu=g["w_up"], wd=g["w_down"])

    wts = dict(w_in_p=_w_in_to_padded(w_in_g), dn_conv=conv_full, late=late)
    for n, _ in _SMALL[:-1]:
        wts[n] = two_d(w[n])
    wts["attn_norm"] = wts["attn_norm"] + token[0:1, 0:1]

    early = {}

    ffn = {}

    def send_ffn(grads):
        gs = [grads["w_gate"], grads["w_up"], grads["w_down"]]
        lands = [lax.empty((N_CHIPS, g.shape[1] // 2, g.shape[2]), g.dtype) for g in gs]
        ffn["sems"], ffn["recv"], ffn["src"], ffn["land"], tok = _split_start(
            "swap_ffn_start", gs, lands, gs[0][0, :8, :128], _swap_windows(gs), _sibling_peer, 1)
        return tok

    def send_early(grads):
        small = [grads["w_branch_dn"], grads["w_branch_swa"], grads["w_out"].reshape(N_CHIPS, CSH, D)]
        big = [grads["w_gate"], grads["w_up"], grads["w_down"]]
        big, got_big = _split_wait("swap_ffn_wait", ffn["src"], ffn["land"], ffn["sems"], ffn["recv"], small[0],
                                   _swap_windows(big), _sibling_peer, with_sources=True)
        gots = list(_swap_halves(small, "swap_halves_early")) + list(got_big)
        parts = _pair_sum(small + list(big), gots, core, "pair_sum_early")
        own = [lax.dynamic_index_in_dim(p, chip, axis=0, keepdims=False) for p in parts]
        early["sems"], early["recv"], early["src"], early["land"], tok = _split_start(
            "exchange_start", parts, _own_slot(own, chip), parts[0][0, :8, :128], _exchange_windows())
        return tok

    last = {}

    def send_in(g_in_p):
        g_in = [_padded_to_w_in(g_in_p)]
        parts = _pair_sum(g_in, _swap_halves(g_in, "swap_halves_in"), core, "pair_sum_in")
        own = [lax.dynamic_index_in_dim(p, chip, axis=0, keepdims=False) for p in parts]
        last["sems"], last["recv"], last["src"], last["land"], tok = _split_start(
            "exchange_in_start", parts, _own_slot(own, chip), parts[0][0, :8, :128], _exchange_windows())
        return tok

    wts["send_ffn"] = send_ffn
    wts["send_early"] = send_early
    wts["send_in"] = send_in
    loss_sum, grad_x, grads = _local_step(x[0], loss_target[0], wts)

    small_sum = _all_sum_small(_pack_small(grads, loss_sum), "all_sum_small")
    loss = small_sum.reshape(-1)[_LOSS_OFF]
    g_small = _unpack_small(small_sum, small_shapes)

    q_early = _split_wait("exchange_wait", early["src"], early["land"], early["sems"], early["recv"], small_sum,
                          _exchange_windows())
    red_early = _chip_sum(list(q_early), "chip_sum_early")
    their_early = _swap_reduced(red_early, "swap_reduced_early")
    g_out, d_out, m_out, v_out = {}, {}, {}, {}
    for n, mine, other in zip(_BIG_NAMES[1:], red_early, their_early):
        res = _adamw_big(natural(w[n], n), mine, other, natural(m[n], n), natural(v[n], n), core, "adamw_" + n)
        g_out[n], d_out[n], m_out[n], v_out[n] = (natural(t, n) for t in res)

    q_in = _split_wait("exchange_in_wait", last["src"], last["land"], last["sems"], last["recv"],
                       d_out[_BIG_NAMES[-1]], _exchange_windows())
    reduced = _chip_sum(list(q_in), "chip_sum_in")
    theirs = _swap_reduced(reduced, "swap_reduced_in")

    def rows1(a):
        return a.transpose(2, 0, 1)

    def unrows1(a):
        return a.transpose(1, 2, 0)

    g_in_blk = jnp.concatenate([jnp.where(core == 0, reduced[0], theirs[0]),
                                jnp.where(core == 0, theirs[0], reduced[0])], axis=0)
    g_in_r = rows1(g_in_blk[None])
    res = _adamw_rows1(rows1(w["w_in"]), g_in_r, rows1(m["w_in"]), rows1(v["w_in"]), "adamw_w_in")
    g_out["w_in"], d_out["w_in"], m_out["w_in"], v_out["w_in"] = (unrows1(t) for t in res)
    g_conv = lax.dynamic_slice(g_small["dn_conv"], (0, chip * _CONV_SH), (CONV, _CONV_SH))
    g_out["dn_conv"] = g_conv.reshape(shapes["dn_conv"])
    d_, m_, v_ = _adamw(conv_loc, g_conv, two_d(m["dn_conv"]), two_d(v["dn_conv"]), "adamw_dn_conv")
    d_out["dn_conv"], m_out["dn_conv"], v_out["dn_conv"] = (t.reshape(shapes["dn_conv"]) for t in (d_, m_, v_))

    def packed(src):
        vals = {n: src[n] for n, _ in _SMALL[:-1]}
        vals["dn_conv"] = jnp.zeros((CONV * QKVW,), F32)
        return _pack_small(vals)

    d_s, m_s, v_s = _adamw(packed(w), small_sum, packed(m), packed(v), "adamw_small")
    d_small, m_small, v_small = (_unpack_small(t, small_shapes) for t in (d_s, m_s, v_s))
    for n, _ in _SMALL[:-1]:
        g_out[n] = g_small[n].reshape(shapes[n])
        d_out[n], m_out[n], v_out[n] = (t[n].reshape(shapes[n]) for t in (d_small, m_small, v_small))

    return (loss, grad_x[None], *[g_out[n] for n in _WEIGHT_NAMES], *[d_out[n] for n in _WEIGHT_NAMES],
            *[m_out[n] for n in _WEIGHT_NAMES], *[v_out[n] for n in _WEIGHT_NAMES])
```

```python
import functools
import math

import numpy as np
import jax
import jax.numpy as jnp
from jax import lax
from jax.experimental import pallas as pl
from jax.experimental.pallas import tpu as pltpu

F32 = jnp.float32
BF16 = jnp.bfloat16
SDS = jax.ShapeDtypeStruct

D = 1024
DN_H = 4
DH = 128
DNW = DN_H * DH
QKVW = 3 * DNW
CONV = 4
CHUNK = 64
SWA_H = 8
SWA_KV = 2
SWA_G = SWA_H // SWA_KV
SWA_D = 64
SWAW = SWA_H * SWA_D
SWAKW = SWA_KV * SWA_D
BLK = 128
NBUCKET = 32
MAXDIST = 128
DFF = 2816
D_IN = QKVW + DNW + 2 * DN_H + SWAW + 2 * SWAKW + 2 * D
EPS = 1e-6
NEG = -1e30

ADAM_LR = 0.001
ADAM_B1 = 0.9
ADAM_B2 = 0.999
ADAM_EPS = 1e-08
ADAM_WD = 0.01
ADAM_STEP = 10

C_QKV, C_Z, C_GATE, C_SQ, C_SK, C_SV, C_BA = 0, 1536, 2048, 4096, 4608, 4736, 4864
PW = 5120
_ORIG_PIECES = (
    (0, QKVW, C_QKV),
    (QKVW, DNW, C_Z),
    (QKVW + DNW, 2 * DN_H, C_BA),
    (QKVW + DNW + 2 * DN_H, SWAW, C_SQ),
    (QKVW + DNW + 2 * DN_H + SWAW, SWAKW, C_SK),
    (QKVW + DNW + 2 * DN_H + SWAW + SWAKW, SWAKW, C_SV),
    (QKVW + DNW + 2 * DN_H + SWAW + 2 * SWAKW, 2 * D, C_GATE),
)

N_CHIPS = 4
FSH = DFF // N_CHIPS
CSH = D // N_CHIPS
VMEM_LIMIT = 48 * 1024 * 1024
MESH = pl.DeviceIdType.MESH

_BIG = (
    ("w_in", D, D_IN // N_CHIPS),
    ("w_branch_dn", DNW, CSH),
    ("w_branch_swa", SWAW, CSH),
    ("w_out", CSH, D),
    ("w_gate", FSH, D),
    ("w_up", FSH, D),
    ("w_down", FSH, D),
)
_BIG_NAMES = tuple(n for n, _, _ in _BIG)

_SMALL = (
    ("attn_norm", D), ("ffn_norm", D), ("dn_out_norm", DH), ("swa_q_norm", SWA_D), ("swa_k_norm", SWA_D),
    ("swa_sinks", SWA_H), ("dn_a_log", DN_H), ("dn_dt_bias", DN_H), ("rel_bias", NBUCKET * SWA_H),
    ("dn_conv", CONV * QKVW),
)
_SMALL_OFF = {}
_o = 0
for _n, _s in _SMALL:
    _SMALL_OFF[_n] = (_o, _s)
    _o += _s
_LOSS_OFF = _o
_SMALL_ROWS = -(-(_o + 1) // (8 * 128)) * 8


def _cparams(**kw):
    return pltpu.CompilerParams(vmem_limit_bytes=VMEM_LIMIT, **kw)


_DIMS = {
    "nn": (((1,), (0,)), ((), ())),
    "nt": (((1,), (1,)), ((), ())),
    "tn": (((0,), (0,)), ((), ())),
    "bnn": (((2,), (1,)), ((0,), (0,))),
    "bnt": (((2,), (2,)), ((0,), (0,))),
    "btn": (((1,), (1,)), ((0,), (0,))),
}


def _raw_dot(a, b, kind, exact):
    if exact:
        prec = lax.Precision.HIGH if exact == "x3" else lax.Precision.HIGHEST
        return lax.dot_general(a, b, _DIMS[kind], precision=prec, preferred_element_type=F32)
    return lax.dot_general(a.astype(BF16), b.astype(BF16), _DIMS[kind], preferred_element_type=F32)


@functools.partial(jax.custom_vjp, nondiff_argnums=(2, 3))
def _dot(a, b, kind, exact):
    return _raw_dot(a, b, kind, exact)


def _dot_fwd(a, b, kind, exact):
    return _raw_dot(a, b, kind, exact), (a, b)


def _dot_bwd(kind, exact, res, g):
    a, b = res
    pre = kind[:-2]
    nn, nt, tn = pre + "nn", pre + "nt", pre + "tn"
    if kind == nn:
        return _dot(g, b, nt, exact), _dot(a, g, tn, exact)
    if kind == nt:
        return _dot(g, b, nn, exact), _dot(g, a, tn, exact)
    return _dot(b, g, nt, exact), _dot(a, g, nn, exact)


_dot.defvjp(_dot_fwd, _dot_bwd)


def _silu(x):
    return x * jax.nn.sigmoid(x)


def _f_rms(x, gain):
    return x * lax.rsqrt(jnp.mean(x * x, axis=-1, keepdims=True) + EPS) * gain


def _f_dn_pre(xs0, xs1, xs2, xs3, ba, cw, alog, dtb):
    rows = xs0.shape[0]
    c = xs0 * cw[0:1] + xs1 * cw[1:2] + xs2 * cw[2:3] + xs3 * cw[3:4]
    qkv = _silu(c)
    qs, ks, bbs, gbs = [], [], [], []
    for h in range(DN_H):
        qh = qkv[:, h * DH:(h + 1) * DH]
        kh = qkv[:, DNW + h * DH:DNW + (h + 1) * DH]
        qs.append(qh * lax.rsqrt(jnp.sum(qh * qh, axis=-1, keepdims=True) + EPS) * (DH ** -0.5))
        ks.append(kh * lax.rsqrt(jnp.sum(kh * kh, axis=-1, keepdims=True) + EPS))
        beta = jax.nn.sigmoid(ba[:, h:h + 1])
        ar = ba[:, DN_H + h:DN_H + h + 1] + dtb[:, h:h + 1]
        softplus = jnp.maximum(ar, 0.0) + jnp.log1p(jnp.exp(-jnp.abs(ar)))
        g = -jnp.exp(alog[:, h:h + 1]) * softplus
        bbs.append(jnp.broadcast_to(beta, (rows, DH)))
        gbs.append(jnp.broadcast_to(g, (rows, DH)))
    return (jnp.concatenate(qs, axis=1), jnp.concatenate(ks, axis=1), qkv[:, 2 * DNW:],
            jnp.concatenate(bbs, axis=1), jnp.concatenate(gbs, axis=1))


def _f_dn_post(o, z, gain):
    ys = []
    for h in range(DN_H):
        oh = o[:, h * DH:(h + 1) * DH]
        zh = z[:, h * DH:(h + 1) * DH]
        ys.append(oh * lax.rsqrt(jnp.mean(oh * oh, axis=-1, keepdims=True) + EPS) * gain * _silu(zh))
    return jnp.concatenate(ys, axis=1)


def _f_merge(pa, pb, ga, gb):
    return jax.nn.sigmoid(ga) * pa + jax.nn.sigmoid(gb) * pb


@jax.custom_vjp
def _f_swiglu(g, u):
    return _silu(g) * u


def _f_swiglu_fwd(g, u):
    return _silu(g) * u, (g, u)


def _f_swiglu_bwd(res, d):
    g, u = res
    s = jax.nn.sigmoid(g)
    act = g * s
    return d * u * (s + act * (1.0 - s)), d * act


_f_swiglu.defvjp(_f_swiglu_fwd, _f_swiglu_bwd)


@jax.custom_vjp
def _unit_lower_inverse(a):
    c = a.shape[-1]
    eye = (lax.broadcasted_iota(jnp.int32, a.shape, 1) == lax.broadcasted_iota(jnp.int32, a.shape, 2)).astype(F32)
    p = -a
    t = eye + p
    for _ in range(max(c.bit_length() - 2, 0)):
        p = _raw_dot(p, p, "bnn", "x3")
        t = t + _raw_dot(t, p, "bnn", "x3")
    return t


def _unit_lower_inverse_fwd(a):
    t = _unit_lower_inverse(a)
    return t, t


def _unit_lower_inverse_bwd(t, g):
    return (-_raw_dot(_raw_dot(t, g, "btn", "x3"), t, "bnt", "x3"),)


_unit_lower_inverse.defvjp(_unit_lower_inverse_fwd, _unit_lower_inverse_bwd)


def _scan_rows(x, reverse):
    c = x.shape[1]
    row = lax.broadcasted_iota(jnp.int32, x.shape, 1)
    shift = 1
    while shift < c:
        if reverse:
            x = x + jnp.where(row < c - shift, pltpu.roll(x, c - shift, 1), 0.0)
        else:
            x = x + jnp.where(row >= shift, pltpu.roll(x, shift, 1), 0.0)
        shift *= 2
    return x


@jax.custom_vjp
def _cumsum_rows(x):
    return _scan_rows(x, False)


def _cumsum_rows_fwd(x):
    return _scan_rows(x, False), None


def _cumsum_rows_bwd(_, g):
    return (_scan_rows(g, True),)


_cumsum_rows.defvjp(_cumsum_rows_fwd, _cumsum_rows_bwd)


@jax.custom_vjp
def _known_inverse(a, t):
    return t


def _known_inverse_fwd(a, t):
    return t, t


def _known_inverse_bwd(t, g):
    return _unit_lower_inverse_bwd(t, g)[0], jnp.zeros_like(t)


_known_inverse.defvjp(_known_inverse_fwd, _known_inverse_bwd)


def _f_chunk(q, k, v, gb, bb, s, t_known=None, with_t=False):
    c = CHUNK
    nh = q.shape[0]
    ii = lax.broadcasted_iota(jnp.int32, (nh, c, c), 1)
    jj = lax.broadcasted_iota(jnp.int32, (nh, c, c), 2)
    incl = ii >= jj
    strict = ii > jj
    eye = (ii == jj).astype(F32)
    gcb = _cumsum_rows(gb)
    gcol = gcb[:, :, :c]
    grow = jnp.swapaxes(gcol, 1, 2)
    decay = jnp.where(incl, jnp.exp(jnp.where(incl, gcol - grow, 0.0)), 0.0)
    kb = k * bb
    vb = v * bb
    a = jnp.where(strict, _dot(kb, k, "bnt", False) * decay, 0.0)
    t = _unit_lower_inverse(a) if t_known is None else _known_inverse(a, t_known)
    eg = jnp.exp(gcb)
    uw = _dot(t, jnp.concatenate([vb, kb * eg], axis=2), "bnn", "x3")
    u, w = uw[:, :, :DH], uw[:, :, DH:]
    qk = jnp.where(incl, _dot(q, k, "bnt", False) * decay, 0.0)
    qe = q * eg
    glast = gcb[:, c - 1:c, :]
    k_dec = k * jnp.exp(glast - gcb)
    e_last = jnp.exp(glast)
    outs = []
    for g in range(nh // DN_H):
        sl = slice(g * DN_H, (g + 1) * DN_H)
        ws = _dot(jnp.concatenate([w[sl], qe[sl]], axis=1), s, "bnn", False)
        v_new = u[sl] - ws[:, :c]
        outs.append(ws[:, c:] + _dot(qk[sl], v_new, "bnn", False))
        s = s * e_last[sl] + _dot(k_dec[sl], v_new, "btn", False)
    o = jnp.concatenate(outs, axis=0)
    return (o, s, t) if with_t else (o, s)


def _f_swa(q8, kp, kc, vp, vc, bias8, qg, kg, sink, mask):
    kb = jnp.concatenate([kp, kc], axis=1)
    vb = jnp.concatenate([vp, vc], axis=1)
    kn = kb * lax.rsqrt(jnp.mean(kb * kb, axis=-1, keepdims=True) + EPS) * kg

    def rows(per_head):
        return jnp.stack([jnp.concatenate([per_head(kv, g) for g in range(SWA_G)], axis=0)
                          for kv in range(SWA_KV)], axis=0)

    qq = rows(lambda kv, g: q8[kv * SWA_G + g])
    qn = qq * lax.rsqrt(jnp.mean(qq * qq, axis=-1, keepdims=True) + EPS) * qg * (SWA_D ** -0.5)
    lg = _dot(qn, kn, "bnt", False) + rows(lambda kv, g: bias8[kv * SWA_G + g])
    lg = jnp.where(rows(lambda kv, g: mask), lg, NEG)
    sk = rows(lambda kv, g: jnp.broadcast_to(sink[kv][:, g:g + 1], (BLK, 1)))
    m = lax.stop_gradient(jnp.maximum(jnp.max(lg, axis=-1, keepdims=True), sk))
    p = jnp.exp(lg - m)
    den = jnp.sum(p, axis=-1, keepdims=True) + jnp.exp(sk - m)
    out = _dot(p * (1.0 / den), vb, "bnn", False)
    return jnp.stack([out[kv, g * BLK:(g + 1) * BLK] for kv in range(SWA_KV) for g in range(SWA_G)], axis=0)


def _bdot(a, b, kind="nn"):
    return lax.dot_general(a.astype(BF16), b.astype(BF16), _DIMS[kind], preferred_element_type=F32)


def _pc(kern, name, grid, in_specs, out_specs, out_shape, scratch=()):
    return pl.pallas_call(
        kern, name=name, grid=grid, in_specs=in_specs, out_specs=out_specs, out_shape=out_shape,
        scratch_shapes=list(scratch), compiler_params=_cparams(dimension_semantics=("arbitrary",) * len(grid)))


def _mm(a, b, kind, out_dtype, tm, tn, name):
    if kind == "tn":
        k, m = a.shape
    else:
        m, k = a.shape
    n = b.shape[0] if kind == "nt" else b.shape[1]
    tm, tn = min(tm, m), min(tn, n)
    assert m % tm == 0 and n % tn == 0, (name, a.shape, b.shape, tm, tn)

    def kern(a_ref, b_ref, o_ref):
        o_ref[...] = _bdot(a_ref[...], b_ref[...], kind).astype(o_ref.dtype)

    a_spec = pl.BlockSpec((k, tm), lambda i, j: (0, i)) if kind == "tn" else pl.BlockSpec((tm, k), lambda i, j: (i, 0))
    b_spec = pl.BlockSpec((tn, k), lambda i, j: (j, 0)) if kind == "nt" else pl.BlockSpec((k, tn), lambda i, j: (0, j))
    return _pc(kern, name, (m // tm, n // tn), [a_spec, b_spec], pl.BlockSpec((tm, tn), lambda i, j: (i, j)),
               SDS((m, n), out_dtype))(a, b)


def _rows(body, name, m, tm, row_ins, full_ins, row_outs, acc_outs=()):
    n_r, n_f, n_o, n_a = len(row_ins), len(full_ins), len(row_outs), len(acc_outs)
    assert m % tm == 0

    def kern(*refs):
        r = refs[:n_r]
        f = refs[n_r:n_r + n_f]
        o = refs[n_r + n_f:n_r + n_f + n_o]
        acc = refs[n_r + n_f + n_o:]
        outs, sums = body([x[...] for x in r], [x[...] for x in f])
        for ref, val in zip(o, outs, strict=True):
            ref[...] = val.astype(ref.dtype)
        if n_a:
            @pl.when(pl.program_id(0) == 0)
            def _():
                for ref in acc:
                    ref[...] = jnp.zeros(ref.shape, F32)

            for ref, val in zip(acc, sums, strict=True):
                ref[...] += val

    in_specs = [pl.BlockSpec((tm, w), functools.partial(lambda i, cb: (i, cb), cb=cb)) for _, w, cb in row_ins]
    in_specs += [pl.BlockSpec(x.shape, lambda i: (0, 0)) for x in full_ins]
    out_specs = [pl.BlockSpec((tm, w), lambda i: (i, 0)) for w, _ in row_outs]
    out_specs += [pl.BlockSpec(s, lambda i: (0, 0)) for s in acc_outs]
    out_shape = [SDS((m, w), dt) for w, dt in row_outs]
    out_shape += [SDS(s, F32) for s in acc_outs]
    return _pc(kern, name, (m // tm,), in_specs, out_specs, out_shape)(*[x for x, _, _ in row_ins], *full_ins)


def _whole(x):
    return (x, x.shape[1], 0)


def _resident(shape):
    return pl.BlockSpec(shape, lambda i: (0,) * len(shape), pipeline_mode=pl.Buffered(1))


def _row_pieces(tm, piece):
    piece = min(piece, tm)
    return [slice(r, r + piece) for r in range(0, tm, piece)]


def _zero_first(refs):
    @pl.when(pl.program_id(0) == 0)
    def _():
        for ref in refs:
            ref[...] = jnp.zeros(ref.shape, F32)


GROUP = 4


def _heads(ref):
    return jnp.stack([ref[g * CHUNK:(g + 1) * CHUNK, h * DH:(h + 1) * DH]
                      for g in range(GROUP) for h in range(DN_H)], axis=0)


def _unheads(ref, val):
    for g in range(GROUP):
        for h in range(DN_H):
            ref[g * CHUNK:(g + 1) * CHUNK, h * DH:(h + 1) * DH] = val[g * DN_H + h]


def _dn_chunks_fwd(q, k, v, gb, bb):
    s_len = q.shape[0]
    ng = s_len // (GROUP * CHUNK)

    def kern(q_ref, k_ref, v_ref, g_ref, b_ref, o_ref, sall_ref, t_ref, state):
        _zero_first([state])
        s = state[...]
        sall_ref[0] = s
        o, s_new, t = _f_chunk(*[_heads(r) for r in (q_ref, k_ref, v_ref, g_ref, b_ref)], s, with_t=True)
        _unheads(o_ref, o)
        t_ref[0] = t
        state[...] = s_new

    blk = pl.BlockSpec((GROUP * CHUNK, DNW), lambda c: (c, 0))
    return _pc(kern, "dn_chunks_fwd", (ng,), [blk] * 5,
               [blk, pl.BlockSpec((1, DN_H, DH, DH), lambda c: (c, 0, 0, 0)),
                pl.BlockSpec((1, GROUP * DN_H, CHUNK, CHUNK), lambda c: (c, 0, 0, 0))],
               [SDS((s_len, DNW), F32), SDS((ng, DN_H, DH, DH), F32), SDS((ng, GROUP * DN_H, CHUNK, CHUNK), F32)],
               scratch=[pltpu.VMEM((DN_H, DH, DH), F32)])(q, k, v, gb, bb)


def _dn_chunks_bwd(q, k, v, gb, bb, s_all, t_all, d_o):
    s_len = q.shape[0]
    ng = s_len // (GROUP * CHUNK)

    def kern(q_ref, k_ref, v_ref, g_ref, b_ref, sall_ref, t_ref, do_ref, dq_ref, dk_ref, dv_ref, dg_ref, db_ref,
             dstate):
        _zero_first([dstate])
        fn = functools.partial(_f_chunk, t_known=t_ref[0])
        _, vjp = jax.vjp(fn, *[_heads(r) for r in (q_ref, k_ref, v_ref, g_ref, b_ref)], sall_ref[0])
        *d_ins, ds = vjp((_heads(do_ref), dstate[...]))
        for ref, val in zip((dq_ref, dk_ref, dv_ref, dg_ref, db_ref), d_ins, strict=True):
            _unheads(ref, val)
        dstate[...] = ds

    blk = pl.BlockSpec((GROUP * CHUNK, DNW), lambda c: (ng - 1 - c, 0))
    return _pc(kern, "dn_chunks_bwd", (ng,),
               [blk] * 5 + [pl.BlockSpec((1, DN_H, DH, DH), lambda c: (ng - 1 - c, 0, 0, 0)),
                            pl.BlockSpec((1, GROUP * DN_H, CHUNK, CHUNK), lambda c: (ng - 1 - c, 0, 0, 0)), blk],
               [blk] * 5, [SDS((s_len, DNW), F32)] * 5,
               scratch=[pltpu.VMEM((DN_H, DH, DH), F32)])(q, k, v, gb, bb, s_all, t_all, d_o)


def _t5_bucket_table():
    qi = np.arange(BLK)[:, None]
    kj = np.arange(2 * BLK)[None, :]
    dist = BLK + qi - kj
    n = np.maximum(dist, 0)
    max_exact = NBUCKET // 2
    nf = np.maximum(n, 1).astype(np.float32)
    large = max_exact + (np.log(nf / np.float32(max_exact)) / np.float32(math.log(MAXDIST / max_exact))
                         * np.float32(NBUCKET - max_exact)).astype(np.int32)
    large = np.minimum(large, NBUCKET - 1)
    return np.where(n < max_exact, n, large)


def _bucket_onehot_t():
    table = _t5_bucket_table().reshape(-1)
    return (np.arange(NBUCKET)[:, None] == table[None, :]).astype(np.float32)


def _swa_mask(first):
    qi = lax.broadcasted_iota(jnp.int32, (BLK, 2 * BLK), 0)
    kj = lax.broadcasted_iota(jnp.int32, (BLK, 2 * BLK), 1)
    dist = BLK + qi - kj
    window = (dist >= 0) & (dist < BLK)
    return window & ((kj >= BLK) | jnp.logical_not(first))


def _bias_expand(rel_bias_t):
    onehot = jnp.asarray(_bucket_onehot_t())

    def kern(r_ref, oh_ref, o_ref):
        o_ref[...] = _raw_dot(r_ref[...], oh_ref[...], "nn", True)

    return pl.pallas_call(
        kern, name="bias_expand", out_shape=SDS((SWA_H, BLK * 2 * BLK), F32), compiler_params=_cparams(),
    )(rel_bias_t, onehot)


def _bias_reduce(d_bias_flat):
    onehot = jnp.asarray(_bucket_onehot_t())

    def kern(d_ref, oh_ref, o_ref):
        o_ref[...] = _raw_dot(d_ref[...], oh_ref[...], "nt", True)

    return pl.pallas_call(
        kern, name="bias_reduce", out_shape=SDS((SWA_H, NBUCKET), F32), compiler_params=_cparams(),
    )(d_bias_flat, onehot)


def _swa_specs(nb, rev):
    def blk(n):
        return (nb - 1 - n) if rev else n

    def before(n):
        return jnp.maximum(blk(n) - 1, 0)

    q_spec = pl.BlockSpec((BLK, SWAW), lambda n: (blk(n), C_SQ // SWAW))
    k_cur = pl.BlockSpec((BLK, SWAKW), lambda n: (blk(n), C_SK // SWAKW))
    k_prev = pl.BlockSpec((BLK, SWAKW), lambda n: (before(n), C_SK // SWAKW))
    v_cur = pl.BlockSpec((BLK, SWAKW), lambda n: (blk(n), C_SV // SWAKW))
    v_prev = pl.BlockSpec((BLK, SWAKW), lambda n: (before(n), C_SV // SWAKW))
    bias = pl.BlockSpec((SWA_H, BLK, 2 * BLK), lambda n: (0, 0, 0))
    gain = pl.BlockSpec((1, SWA_D), lambda n: (0, 0))
    sink = pl.BlockSpec((SWA_KV, 1, SWA_G), lambda n: (0, 0, 0))
    wide = pl.BlockSpec((BLK, SWAW), lambda n: (blk(n), 0))
    narrow = pl.BlockSpec((BLK, SWAKW), lambda n: (blk(n), 0))
    return [q_spec, k_prev, k_cur, v_prev, v_cur, bias, gain, gain, sink], wide, narrow


def _split_heads(x):
    return jnp.stack([x[:, h * SWA_D:(h + 1) * SWA_D] for h in range(x.shape[1] // SWA_D)], axis=0)


def _join_heads(x):
    return jnp.concatenate([x[h] for h in range(x.shape[0])], axis=1)


def _swa_fwd(proj, bias, qg, kg, sinks):
    s_len = proj.shape[0]
    nb = s_len // BLK
    in_specs, wide, _ = _swa_specs(nb, False)

    def kern(q_ref, kp_ref, kc_ref, vp_ref, vc_ref, b_ref, qg_ref, kg_ref, s_ref, o_ref):
        mask = _swa_mask(pl.program_id(0) == 0)
        o8 = _f_swa(*[_split_heads(r[...]) for r in (q_ref, kp_ref, kc_ref, vp_ref, vc_ref)], b_ref[...], qg_ref[...],
                    kg_ref[...], s_ref[...], mask)
        o_ref[...] = _join_heads(o8).astype(BF16)

    return _pc(kern, "swa_fwd", (nb,), in_specs, wide, SDS((s_len, SWAW), BF16))(
        proj, proj, proj, proj, proj, bias, qg, kg, sinks)


def _swa_bwd(proj, bias, qg, kg, sinks, d_out):
    s_len = proj.shape[0]
    nb = s_len // BLK
    in_specs, wide, narrow = _swa_specs(nb, True)

    def kern(q_ref, kp_ref, kc_ref, vp_ref, vc_ref, b_ref, qg_ref, kg_ref, s_ref, do_ref,
             dq_ref, dk_ref, dv_ref, db_ref, dqg_ref, dkg_ref, ds_ref, carry_k, carry_v):
        n = pl.program_id(0)
        mask = _swa_mask(n == nb - 1)
        _zero_first([carry_k, carry_v, db_ref, ds_ref, dqg_ref, dkg_ref])
        fn = functools.partial(_f_swa, mask=mask)
        _, vjp = jax.vjp(fn, *[_split_heads(r[...]) for r in (q_ref, kp_ref, kc_ref, vp_ref, vc_ref)], b_ref[...],
                         qg_ref[...], kg_ref[...], s_ref[...])
        dq, dkp, dkc, dvp, dvc, dbias, dqg, dkg, dsink = vjp(_split_heads(do_ref[...]))
        dq_ref[...] = _join_heads(dq).astype(BF16)
        dk_ref[...] = (_join_heads(dkc) + carry_k[...]).astype(BF16)
        dv_ref[...] = (_join_heads(dvc) + carry_v[...]).astype(BF16)
        carry_k[...] = _join_heads(dkp)
        carry_v[...] = _join_heads(dvp)
        db_ref[...] += dbias
        dqg_ref[...] += dqg
        dkg_ref[...] += dkg
        ds_ref[...] += dsink

    bias_spec, gain, sink = in_specs[5], in_specs[6], in_specs[8]
    return _pc(
        kern, "swa_bwd", (nb,), in_specs + [wide], [wide, narrow, narrow, bias_spec, gain, gain, sink],
        [SDS((s_len, SWAW), BF16), SDS((s_len, SWAKW), BF16), SDS((s_len, SWAKW), BF16),
         SDS((SWA_H, BLK, 2 * BLK), F32), SDS((1, SWA_D), F32), SDS((1, SWA_D), F32), SDS((SWA_KV, 1, SWA_G), F32)],
        scratch=[pltpu.VMEM((BLK, SWAKW), F32), pltpu.VMEM((BLK, SWAKW), F32)],
    )(proj, proj, proj, proj, proj, bias, qg, kg, sinks, d_out)


def _branch_merge(y_dn, y_swa, wa, wb, proj):
    s_len = y_dn.shape[0]
    tm = min(1024, s_len)

    def kern(ya_ref, yb_ref, wa_ref, wb_ref, ga_ref, gb_ref, pa_ref, pb_ref, m_ref):
        for rows in _row_pieces(tm, 128):
            pa = _bdot(ya_ref[rows, :], wa_ref[0])
            pb = _bdot(yb_ref[rows, :], wb_ref[0])
            pa_ref[rows, :] = pa.astype(BF16)
            pb_ref[rows, :] = pb.astype(BF16)
            m_ref[rows, :] = _f_merge(pa, pb, ga_ref[rows, :], gb_ref[rows, :]).astype(BF16)

    y_spec = pl.BlockSpec((tm, DNW), lambda i, s: (i, 0))
    w_spec = pl.BlockSpec((1, DNW, CSH), lambda i, s: (s, 0, 0))
    o_spec = pl.BlockSpec((tm, CSH), lambda i, s: (i, s))
    ga_spec = pl.BlockSpec((tm, CSH), lambda i, s: (i, C_GATE // CSH + s))
    gb_spec = pl.BlockSpec((tm, CSH), lambda i, s: (i, (C_GATE + D) // CSH + s))
    return _pc(kern, "branch_merge", (s_len // tm, N_CHIPS), [y_spec, y_spec, w_spec, w_spec, ga_spec, gb_spec],
               [o_spec] * 3, [SDS((s_len, D), BF16)] * 3,
               )(y_dn, y_swa, wa, wb, proj, proj)


def _in_proj(x, gain, w_in_p):
    s_len = x.shape[0]
    tm = min(512, s_len)

    def kern(x_ref, g_ref, w_ref, h_ref, p_ref):
        h = _f_rms(x_ref[...], g_ref[...]).astype(BF16)
        h_ref[...] = h
        p_ref[...] = _bdot(h, w_ref[...])

    row = pl.BlockSpec((tm, D), lambda i: (i, 0))
    return _pc(kern, "in_proj", (s_len // tm,),
               [row, pl.BlockSpec((1, D), lambda i: (0, 0)), _resident((D, PW))],
               [row, pl.BlockSpec((tm, PW), lambda i: (i, 0))],
               [SDS((s_len, D), BF16), SDS((s_len, PW), F32)])(x, gain, w_in_p)


def _out_proj(merged, w_out, x, gain):
    s_len = x.shape[0]
    tm = min(512, s_len)

    def kern(m_ref, w_ref, x_ref, g_ref, x1_ref, h2_ref):
        x1 = x_ref[...] + _bdot(m_ref[...], w_ref[...])
        x1_ref[...] = x1
        h2_ref[...] = _f_rms(x1, g_ref[...]).astype(BF16)

    row = pl.BlockSpec((tm, D), lambda i: (i, 0))
    return _pc(kern, "out_proj", (s_len // tm,),
               [row, _resident((D, D)), row, pl.BlockSpec((1, D), lambda i: (0, 0))],
               [row, row], [SDS((s_len, D), F32), SDS((s_len, D), BF16)])(merged, w_out, x, gain)


def _ffn_up(h2, wg, wu):
    s_len = h2.shape[0]
    tm = min(2048, s_len)

    def kern(h_ref, g_ref, u_ref, gt_ref, up_ref, act_ref):
        for rows in _row_pieces(tm, 256):
            h = h_ref[rows, :]
            g = _bdot(h, g_ref[0], "nt")
            u = _bdot(h, u_ref[0], "nt")
            gt_ref[0, rows, :] = g.astype(BF16)
            up_ref[0, rows, :] = u.astype(BF16)
            act_ref[0, rows, :] = _f_swiglu(g, u).astype(BF16)

    w_spec = pl.BlockSpec((1, FSH, D), lambda s, i: (s, 0, 0))
    o_spec = pl.BlockSpec((1, tm, FSH), lambda s, i: (s, i, 0))
    shape = (N_CHIPS, s_len, FSH)
    return _pc(kern, "ffn_up", (N_CHIPS, s_len // tm), [pl.BlockSpec((tm, D), lambda s, i: (i, 0)), w_spec, w_spec],
               [o_spec] * 3, [SDS(shape, BF16)] * 3)(h2, wg, wu)


def _ffn_down_loss(act, wd, x1, target):
    s_len = x1.shape[0]
    tm = min(512, s_len)

    def kern(a_ref, w_ref, x_ref, t_ref, dy_ref, dyb_ref, loss_ref):
        _zero_first([loss_ref])
        for rows in _row_pieces(tm, 128):
            y = x_ref[rows, :]
            for s in range(N_CHIPS):
                y = y + _bdot(a_ref[s, rows, :], w_ref[s])
            d = y - t_ref[rows, :]
            dy = d * (1.0 / D)
            dy_ref[rows, :] = dy
            dyb_ref[rows, :] = dy.astype(BF16)
            loss_ref[...] += jnp.sum(d * d).reshape(1, 1) * (0.5 / D)

    row = pl.BlockSpec((tm, D), lambda i: (i, 0))
    return _pc(kern, "ffn_down_loss", (s_len // tm,),
               [pl.BlockSpec((N_CHIPS, tm, FSH), lambda i: (0, i, 0)),
                _resident((N_CHIPS, FSH, D)), row, row],
               [row, row, pl.BlockSpec((1, 1), lambda i: (0, 0))],
               [SDS((s_len, D), F32), SDS((s_len, D), BF16), SDS((1, 1), F32)])(act, wd, x1, target)


def _ffn_dact(dy_b, wd, gt, up):
    s_len = dy_b.shape[0]
    tm = min(2048, s_len)

    def kern(dy_ref, w_ref, gt_ref, up_ref, dg_ref, du_ref):
        w = w_ref[0]
        for rows in _row_pieces(tm, 256):
            d_act = _bdot(dy_ref[rows, :], w, "nt")
            _, vjp = jax.vjp(_f_swiglu, gt_ref[0, rows, :].astype(F32), up_ref[0, rows, :].astype(F32))
            dg, du = vjp(d_act)
            dg_ref[0, rows, :] = dg.astype(BF16)
            du_ref[0, rows, :] = du.astype(BF16)

    a_spec = pl.BlockSpec((1, tm, FSH), lambda s, i: (s, i, 0))
    shape = (N_CHIPS, s_len, FSH)
    return _pc(kern, "ffn_dact", (N_CHIPS, s_len // tm),
               [pl.BlockSpec((tm, D), lambda s, i: (i, 0)), pl.BlockSpec((1, FSH, D), lambda s, i: (s, 0, 0)),
                a_spec, a_spec],
               [a_spec, a_spec], [SDS(shape, BF16), SDS(shape, BF16)])(dy_b, wd, gt, up)


def _gw_ffn(lhs, rhs, name):
    s_len = rhs.shape[0]
    n = len(lhs)
    tn = 512

    def kern(*refs):
        g = refs[n][...]
        for i in range(n):
            refs[n + 1 + i][0] = _bdot(refs[i][0], g, "tn").astype(BF16)

    a_spec = pl.BlockSpec((1, s_len, FSH), lambda s, j: (s, 0, 0))
    o_spec = pl.BlockSpec((1, FSH, tn), lambda s, j: (s, 0, j))
    return _pc(kern, name, (N_CHIPS, D // tn), [a_spec] * n + [pl.BlockSpec((s_len, tn), lambda s, j: (0, j))],
               [o_spec] * n, [SDS((N_CHIPS, FSH, D), BF16)] * n)(*lhs, rhs)


def _ffn_dh2(d_gt, d_up, wg, wu, x1, dy, gain):
    s_len = x1.shape[0]
    tm = min(512, s_len)

    def kern(dg_ref, du_ref, wg_ref, wu_ref, x_ref, dy_ref, g_ref, dx_ref, dxb_ref, dgain_ref):
        _zero_first([dgain_ref])
        dh2 = jnp.zeros((tm, D), F32)
        for s in range(N_CHIPS):
            dh2 = dh2 + _bdot(dg_ref[s], wg_ref[s]) + _bdot(du_ref[s], wu_ref[s])
        _, vjp = jax.vjp(_f_rms, x_ref[...], g_ref[...])
        dx, dgain = vjp(dh2)
        dx1 = dx + dy_ref[...]
        dx_ref[...] = dx1
        dxb_ref[...] = dx1.astype(BF16)
        dgain_ref[...] += dgain

    row = pl.BlockSpec((tm, D), lambda i: (i, 0))
    d_spec = pl.BlockSpec((N_CHIPS, tm, FSH), lambda i: (0, i, 0))
    w_spec = _resident((N_CHIPS, FSH, D))
    vec = pl.BlockSpec((1, D), lambda i: (0, 0))
    return _pc(kern, "ffn_dh2", (s_len // tm,), [d_spec, d_spec, w_spec, w_spec, row, row, vec],
               [row, row, vec], [SDS((s_len, D), F32), SDS((s_len, D), BF16), SDS((1, D), F32)],
               )(d_gt, d_up, wg, wu, x1, dy, gain)


def _merge_bwd(dx1_b, w_out, pa, pb, proj):
    s_len = dx1_b.shape[0]
    tm = min(512, s_len)

    def kern(dx_ref, w_ref, pa_ref, pb_ref, g_ref, dpa_ref, dpb_ref, dg_ref):
        dm = _bdot(dx_ref[...], w_ref[...], "nt")
        gates = g_ref[...]
        _, vjp = jax.vjp(_f_merge, pa_ref[...].astype(F32), pb_ref[...].astype(F32), gates[:, :D], gates[:, D:])
        dpa, dpb, dga, dgb = vjp(dm)
        dpa_ref[...] = dpa.astype(BF16)
        dpb_ref[...] = dpb.astype(BF16)
        dg_ref[:, :D] = dga.astype(BF16)
        dg_ref[:, D:] = dgb.astype(BF16)

    row = pl.BlockSpec((tm, D), lambda i: (i, 0))
    return _pc(kern, "merge_bwd", (s_len // tm,),
               [row, _resident((D, D)), row, row,
                pl.BlockSpec((tm, 2 * D), lambda i: (i, C_GATE // (2 * D)))],
               [row, row, pl.BlockSpec((tm, 2 * D), lambda i: (i, 0))],
               [SDS((s_len, D), BF16), SDS((s_len, D), BF16), SDS((s_len, 2 * D), BF16)],
               )(dx1_b, w_out, pa, pb, proj)


def _d_branch(d_pa, d_pb, wa, wb):
    s_len = d_pa.shape[0]
    tm = min(512, s_len)

    def kern(da_ref, db_ref, wa_ref, wb_ref, oa_ref, ob_ref):
        acc_a = jnp.zeros((tm, DNW), F32)
        acc_b = jnp.zeros((tm, SWAW), F32)
        for s in range(N_CHIPS):
            acc_a = acc_a + _bdot(da_ref[:, s * CSH:(s + 1) * CSH], wa_ref[s], "nt")
            acc_b = acc_b + _bdot(db_ref[:, s * CSH:(s + 1) * CSH], wb_ref[s], "nt")
        oa_ref[...] = acc_a
        ob_ref[...] = acc_b

    row = pl.BlockSpec((tm, D), lambda i: (i, 0))
    w_spec = pl.BlockSpec((N_CHIPS, DNW, CSH), lambda i: (0, 0, 0))
    out = pl.BlockSpec((tm, DNW), lambda i: (i, 0))
    return _pc(kern, "d_branch", (s_len // tm,), [row, row, w_spec, w_spec], [out, out],
               [SDS((s_len, DNW), F32), SDS((s_len, SWAW), F32)])(d_pa, d_pb, wa, wb)


def _gw_branch(y_dn, y_swa, d_pa, d_pb):
    s_len = y_dn.shape[0]

    def kern(ya_ref, yb_ref, da_ref, db_ref, oa_ref, ob_ref):
        oa_ref[0] = _bdot(ya_ref[...], da_ref[...], "tn").astype(BF16)
        ob_ref[0] = _bdot(yb_ref[...], db_ref[...], "tn").astype(BF16)

    y_spec = pl.BlockSpec((s_len, DNW), lambda s: (0, 0))
    d_spec = pl.BlockSpec((s_len, CSH), lambda s: (0, s))
    o_spec = pl.BlockSpec((1, DNW, CSH), lambda s: (s, 0, 0))
    shape = (N_CHIPS, DNW, CSH)
    return _pc(kern, "gw_branch", (N_CHIPS,), [y_spec, y_spec, d_spec, d_spec], [o_spec, o_spec],
               [SDS(shape, BF16), SDS(shape, BF16)])(y_dn, y_swa, d_pa, d_pb)


def _dh_rms(d_proj, w_in_p, x, dx1, gain):
    s_len = x.shape[0]
    tm = min(512, s_len)

    def kern(dp_ref, w_ref, x_ref, r_ref, g_ref, gx_ref, dgain_ref):
        _zero_first([dgain_ref])
        dh = _bdot(dp_ref[...], w_ref[...], "nt")
        _, vjp = jax.vjp(_f_rms, x_ref[...], g_ref[...])
        dx, dgain = vjp(dh)
        gx_ref[...] = dx + r_ref[...]
        dgain_ref[...] += dgain

    row = pl.BlockSpec((tm, D), lambda i: (i, 0))
    vec = pl.BlockSpec((1, D), lambda i: (0, 0))
    return _pc(kern, "dh_rms", (s_len // tm,),
               [pl.BlockSpec((tm, PW), lambda i: (i, 0)), _resident((D, PW)), row, row, vec],
               [row, vec], [SDS((s_len, D), F32), SDS((1, D), F32)])(d_proj, w_in_p, x, dx1, gain)


HALO = 8


def _rows_down(x, n, above):
    tm = x.shape[0]
    r = pltpu.roll(x, n, 0)
    a = pltpu.roll(above, n, 0)
    top = jnp.where(lax.broadcasted_iota(jnp.int32, above.shape, 0) < n, a, r[0:HALO])
    return jnp.concatenate([top, r[HALO:tm]], axis=0)


def _rows_up(x, n, below):
    tm = x.shape[0]
    r = pltpu.roll(x, tm - n, 0)
    b = pltpu.roll(below, HALO - n, 0)
    bottom = jnp.where(lax.broadcasted_iota(jnp.int32, below.shape, 0) >= HALO - n, b, r[tm - HALO:tm])
    return jnp.concatenate([r[0:tm - HALO], bottom], axis=0)


def _conv_taps(cur_ref, prev_ref, first):
    cur = cur_ref[...]
    above = jnp.where(first, 0.0, prev_ref[...])
    return [_rows_down(cur, n, above) for n in range(CONV - 1, 0, -1)] + [cur]


def _dn_pre_specs(s_len, tm, blk):
    cur = pl.BlockSpec((tm, QKVW), lambda i: (blk(i), 0))
    prev = pl.BlockSpec((HALO, QKVW), lambda i: (jnp.maximum(blk(i) * (tm // HALO) - 1, 0), 0))
    ba = pl.BlockSpec((tm, 128), lambda i: (blk(i), C_BA // 128))
    row = pl.BlockSpec((tm, DNW), lambda i: (blk(i), 0))
    full = [pl.BlockSpec((CONV, QKVW), lambda i: (0, 0)), pl.BlockSpec((1, DN_H), lambda i: (0, 0)),
            pl.BlockSpec((1, DN_H), lambda i: (0, 0))]
    return cur, prev, ba, row, full


def _dn_pre_fwd(proj, conv_w, alog, dtb):
    s_len = proj.shape[0]
    tm = min(256, s_len)
    cur, prev, ba, row, full = _dn_pre_specs(s_len, tm, lambda i: i)

    def kern(cur_ref, prev_ref, ba_ref, cw_ref, al_ref, dt_ref, q_ref, k_ref, v_ref, bb_ref, gb_ref):
        xs = _conv_taps(cur_ref, prev_ref, pl.program_id(0) == 0)
        outs = _f_dn_pre(*xs, ba_ref[...], cw_ref[...], al_ref[...], dt_ref[...])
        for ref, val in zip((q_ref, k_ref, v_ref, bb_ref, gb_ref), outs, strict=True):
            ref[...] = val

    return _pc(kern, "dn_pre_fwd", (s_len // tm,), [cur, prev, ba] + full, [row] * 5,
               [SDS((s_len, DNW), F32)] * 5)(proj, proj, proj, conv_w, alog, dtb)


def _dn_pre_bwd(proj, conv_w, alog, dtb, cots, others):
    s_len = proj.shape[0]
    tm = min(256, s_len)
    nb = s_len // tm
    cur, prev, ba, row, full = _dn_pre_specs(s_len, tm, lambda i: nb - 1 - i)
    n_o = len(others)
    assert QKVW + sum(t.shape[1] for t in others) + 128 == C_BA + 128

    def kern(cur_ref, prev_ref, ba_ref, cw_ref, al_ref, dt_ref, dq_ref, dk_ref, dv_ref, dbb_ref, dgb_ref, *rest):
        o_refs = rest[:n_o]
        dproj_ref, dcw_ref, dal_ref, ddt_ref, *tails = rest[n_o:]
        i = pl.program_id(0)
        _zero_first([dcw_ref, dal_ref, ddt_ref] + tails)
        xs = _conv_taps(cur_ref, prev_ref, i == nb - 1)
        _, vjp = jax.vjp(_f_dn_pre, *xs, ba_ref[...], cw_ref[...], al_ref[...], dt_ref[...])
        *dxs, dba, dcw, dal, ddt = vjp((dq_ref[...], dk_ref[...], dv_ref[...], dbb_ref[...], dgb_ref[...]))
        total = dxs[CONV - 1]
        for j, t in enumerate(tails):
            n = CONV - 1 - j
            total = total + _rows_up(dxs[j], n, t[...])
            t[...] = dxs[j][0:HALO, :]
        dproj_ref[...] = jnp.concatenate(
            [total.astype(BF16)] + [r[...] for r in o_refs] + [dba.astype(BF16), jnp.zeros((tm, PW - C_BA - 128), BF16)],
            axis=1)
        dcw_ref[...] += dcw
        dal_ref[...] += dal
        ddt_ref[...] += ddt

    o_specs = [pl.BlockSpec((tm, t.shape[1]), lambda i: (nb - 1 - i, 0)) for t in others]
    return _pc(kern, "dn_pre_bwd", (nb,), [cur, prev, ba] + full + [row] * 5 + o_specs,
               [pl.BlockSpec((tm, PW), lambda i: (nb - 1 - i, 0))] + full,
               [SDS((s_len, PW), BF16), SDS((CONV, QKVW), F32), SDS((1, DN_H), F32), SDS((1, DN_H), F32)],
               scratch=[pltpu.VMEM((HALO, QKVW), F32)] * (CONV - 1))(proj, proj, proj, conv_w, alog, dtb, *cots, *others)


def _w_in_to_padded(w_sh):
    tr = 256

    def kern(w_ref, o_ref):
        full = jnp.concatenate([w_ref[s] for s in range(N_CHIPS)], axis=1)
        pieces = [full[:, o0:o0 + w] for o0, w, _ in sorted(_ORIG_PIECES, key=lambda t: t[2])]
        o_ref[...] = jnp.concatenate(pieces + [jnp.zeros((tr, PW - D_IN), w_ref.dtype)], axis=1)

    return _pc(kern, "w_in_to_padded", (D // tr,), [pl.BlockSpec((N_CHIPS, tr, D_IN // N_CHIPS), lambda i: (0, i, 0))],
               pl.BlockSpec((tr, PW), lambda i: (i, 0)), SDS((D, PW), w_sh.dtype))(w_sh)


def _padded_to_w_in(g):
    tr = 256
    csh = D_IN // N_CHIPS

    def kern(g_ref, o_ref):
        x = g_ref[...]
        full = jnp.concatenate([x[:, p0:p0 + w] for _, w, p0 in _ORIG_PIECES], axis=1)
        for s in range(N_CHIPS):
            o_ref[s] = full[:, s * csh:(s + 1) * csh]

    return _pc(kern, "padded_to_w_in", (D // tr,), [pl.BlockSpec((tr, PW), lambda i: (i, 0))],
               pl.BlockSpec((N_CHIPS, tr, csh), lambda i: (0, i, 0)), SDS((N_CHIPS, D, csh), g.dtype))(g)


def _local_step(x, target, wts):
    s_len = x.shape[0]
    tm = min(512, s_len)
    w_in_p = wts["w_in_p"]
    attn_gain = wts["attn_norm"]
    ffn_gain = wts["ffn_norm"]
    conv_w = wts["dn_conv"]
    alog, dtb, out_gain = wts["dn_a_log"], wts["dn_dt_bias"], wts["dn_out_norm"]
    qg, kg = wts["swa_q_norm"], wts["swa_k_norm"]
    sinks = wts["swa_sinks"].reshape(SWA_KV, 1, SWA_G)

    h, proj = _in_proj(x, attn_gain, w_in_p)
    q_dn, k_dn, v_dn, bb, gb = _dn_pre_fwd(proj, conv_w, alog, dtb)
    o_dn, s_all, t_all = _dn_chunks_fwd(q_dn, k_dn, v_dn, gb, bb)
    post_ins = [_whole(o_dn), (proj, DNW, C_Z // DNW)]
    (y_dn,) = _rows(lambda r, f: ([_f_dn_post(r[0], r[1], f[0])], []), "dn_post_fwd", s_len, tm, post_ins,
                    [out_gain], [(DNW, BF16)])

    bias = _bias_expand(wts["rel_bias"].T).reshape(SWA_H, BLK, 2 * BLK)
    y_swa = _swa_fwd(proj, bias, qg, kg, sinks)

    wts = {**wts, **wts["late"](y_swa)}
    p_a, p_b, merged = _branch_merge(y_dn, y_swa, wts["wa"], wts["wb"], proj)
    x1, h2 = _out_proj(merged, wts["w_out"], x, ffn_gain)
    gt, up, act = _ffn_up(h2, wts["wg"], wts["wu"])
    dy, dy_b, loss = _ffn_down_loss(act, wts["wd"], x1, target)

    grads = {}
    d_gt, d_up = _ffn_dact(dy_b, wts["wd"], gt, up)
    (grads["w_down"],) = _gw_ffn([act], dy_b, "gw_down")
    grads["w_gate"], grads["w_up"] = _gw_ffn([d_gt, d_up], h2, "gw_gate_up")
    token = wts["send_ffn"](grads)
    dx1, dx1_b, grads["ffn_norm"] = _ffn_dh2(d_gt, d_up, wts["wg"], wts["wu"], x1, dy,
                                             ffn_gain + token[0:1, 0:1])
    grads["w_out"] = _mm(merged, dx1_b, "tn", BF16, 512, 512, "gw_out")
    d_pa, d_pb, d_gr = _merge_bwd(dx1_b, wts["w_out"], p_a, p_b, proj)
    d_ydn, d_yswa = _d_branch(d_pa, d_pb, wts["wa"], wts["wb"])
    grads["w_branch_dn"], grads["w_branch_swa"] = _gw_branch(y_dn, y_swa, d_pa, d_pb)
    token = wts["send_early"](grads)
    qg_t = qg + token[0:1, 0:1]
    out_gain_t = out_gain + token[0:1, 0:1]

    d_sq, d_sk, d_sv, d_bias, grads["swa_q_norm"], grads["swa_k_norm"], d_sinks = _swa_bwd(
        proj, bias, qg_t, kg, sinks, d_yswa)
    grads["swa_sinks"] = d_sinks.reshape(1, SWA_H)
    grads["rel_bias"] = _bias_reduce(d_bias.reshape(SWA_H, BLK * 2 * BLK)).T

    def post_bwd(r, f):
        _, vjp = jax.vjp(_f_dn_post, r[0], r[1], f[0])
        d_o, d_z, d_gain = vjp(r[2])
        return [d_o, d_z], [d_gain]

    d_o, d_z, grads["dn_out_norm"] = _rows(post_bwd, "dn_post_bwd", s_len, tm, post_ins + [_whole(d_ydn)], [out_gain_t],
                                           [(DNW, F32), (DNW, BF16)], [(1, DH)])
    d_q, d_k, d_v, d_gb, d_bb = _dn_chunks_bwd(q_dn, k_dn, v_dn, gb, bb, s_all, t_all, d_o)

    d_proj, grads["dn_conv"], grads["dn_a_log"], grads["dn_dt_bias"] = _dn_pre_bwd(
        proj, conv_w, alog, dtb, (d_q, d_k, d_v, d_bb, d_gb), (d_z, d_gr, d_sq, d_sk, d_sv))
    grads["w_in_p"] = _mm(h, d_proj, "tn", BF16, 512, 1024, "gw_in")
    token = wts["send_in"](grads["w_in_p"])
    grad_x, grads["attn_norm"] = _dh_rms(d_proj, w_in_p, x, dx1, attn_gain + token[0:1, 0:1])
    return loss, grad_x, grads


_HBM = pl.BlockSpec(memory_space=pl.ANY)


def _place():
    return lax.axis_index("x"), lax.axis_index("y"), lax.axis_index("c")


def _other_chips(x, y):
    return [(1 - x, y), (x, 1 - y), (1 - x, 1 - y)]


def _rcopy(src, dst, send_sems, recv_sems, k, to):
    return pltpu.make_async_remote_copy(src_ref=src, dst_ref=dst, send_sem=send_sems.at[k], recv_sem=recv_sems.at[k],
                                        device_id=to, device_id_type=MESH)


def _comm_call(body, name, ins, out_shapes, n_remote, landing=0):
    first = len(ins) - landing
    return pl.pallas_call(
        body, name=name, in_specs=[_HBM] * len(ins), out_specs=[_HBM] * len(out_shapes), out_shape=out_shapes,
        scratch_shapes=[pltpu.SemaphoreType.DMA((n_remote,)), pltpu.SemaphoreType.DMA((n_remote,))],
        input_output_aliases={first + i: i for i in range(landing)},
        compiler_params=_cparams(has_side_effects=True),
    )(*ins)


def _own_slot(blocks, chip):
    return [lax.dynamic_update_slice(lax.empty((N_CHIPS,) + b.shape, b.dtype), b[None], (chip, 0, 0)) for b in blocks]


def _gather_weights(ws, chip):
    n = len(ws)
    halves = [w.shape[0] // 2 for w in ws]

    def body(*refs):
        w_refs, o_refs = refs[:n], refs[2 * n:3 * n]
        send_sems, recv_sems = refs[3 * n:]
        x, y, c = _place()
        s = 2 * x + y
        sib = (x, y, 1 - c)
        chips = _other_chips(x, y)

        def rows(i, half):
            return pl.ds(half * halves[i], halves[i])

        first = []
        for j, (cx, cy) in enumerate(chips):
            for i in range(n):
                cp = _rcopy(w_refs[i].at[rows(i, c), :], o_refs[i].at[s, rows(i, c), :], send_sems, recv_sems,
                            j * n + i, (cx, cy, c))
                cp.start()
                first.append(cp)
        passed = []
        for j, (cx, cy) in enumerate(chips):
            sj = 2 * cx + cy
            for i in range(n):
                blk = o_refs[i].at[sj, rows(i, c), :]
                _rcopy(blk, blk, send_sems, recv_sems, j * n + i, (cx, cy, c)).wait_recv()
                cp = _rcopy(blk, blk, send_sems, recv_sems, (3 + j) * n + i, sib)
                cp.start()
                passed.append(cp)
        for j, (cx, cy) in enumerate(chips):
            sj = 2 * cx + cy
            for i in range(n):
                blk = o_refs[i].at[sj, rows(i, 1 - c), :]
                _rcopy(blk, blk, send_sems, recv_sems, (3 + j) * n + i, sib).wait_recv()
        for cp in first + passed:
            cp.wait_send()

    return _comm_call(body, "gather_weights", list(ws) + _own_slot(ws, chip),
                      [SDS((N_CHIPS,) + w.shape, w.dtype) for w in ws], 6 * n, landing=n)


_HBM_ONLY = pl.BlockSpec(memory_space=pltpu.HBM)
_SEM = pl.BlockSpec(memory_space=pltpu.SEMAPHORE)
_DATAFLOW = pltpu.SideEffectType.DATAFLOW_SIDE_EFFECTING


def _in_hbm(a):
    return pltpu.with_memory_space_constraint(a, pltpu.HBM)


def _gather_windows(blocks):
    halves = [b.shape[0] // 2 for b in blocks]

    def src_at(ref, i, c, sj):
        return ref.at[pl.ds(c * halves[i], halves[i]), :]

    def dst_at(ref, i, c, s_from):
        return ref.at[s_from, pl.ds(c * halves[i], halves[i]), :]

    return src_at, dst_at


def _exchange_windows():
    return (lambda ref, i, c, sj: ref.at[sj]), (lambda ref, i, c, s_from: ref.at[s_from])


def _swap_windows(gs):
    halves = [g.shape[1] // 2 for g in gs]
    return ((lambda ref, i, c, tag: ref.at[:, pl.ds((1 - c) * halves[i], halves[i]), :]),
            (lambda ref, i, c, slot: ref))


def _chip_peers(x, y, c):
    return [(2 * cx + cy, (cx, cy, c), 2 * x + y, 2 * cx + cy) for cx, cy in _other_chips(x, y)]


def _sibling_peer(x, y, c):
    return [(0, (x, y, 1 - c), 0, 0)]


def _split_start(name, ws, lands, dep, windows, peers=_chip_peers, n_peers=3):
    n = len(ws)
    src_at, dst_at = windows

    def body(*refs):
        w_refs, l_refs = refs[:n], refs[n:2 * n]
        send_sems, recv_sems = refs[2 * n + 1], refs[2 * n + 2]
        token = refs[-1]
        x, y, c = _place()
        for j, (tag, dev, there, _) in enumerate(peers(x, y, c)):
            for i in range(n):
                _rcopy(src_at(w_refs[i], i, c, tag), dst_at(l_refs[i], i, c, there), send_sems, recv_sems,
                       j * n + i, dev).start()
        token[...] = jnp.zeros_like(token)

    outs = pl.pallas_call(
        body, name=name,
        out_shape=(pltpu.SemaphoreType.DMA((n_peers * n,)), pltpu.SemaphoreType.DMA((n_peers * n,)),
                   *[pltpu.HBM(w.shape, w.dtype) for w in ws], *[pltpu.HBM(t.shape, t.dtype) for t in lands],
                   SDS((8, 128), F32)),
        in_specs=[_HBM_ONLY] * (2 * n) + [pl.BlockSpec(memory_space=pl.ANY)],
        out_specs=(_SEM, _SEM, *[_HBM_ONLY] * (2 * n), pl.BlockSpec(memory_space=pltpu.VMEM)),
        input_output_aliases={i: 2 + i for i in range(2 * n)},
        compiler_params=pltpu.CompilerParams(has_side_effects=_DATAFLOW),
    )(*[_in_hbm(w) for w in ws], *[_in_hbm(t) for t in lands], dep)
    return outs[0], outs[1], outs[2:2 + n], outs[2 + n:2 + 2 * n], outs[-1]


def _split_wait(name, w_thru, l_thru, send_sems, recv_sems, after, windows, peers=_chip_peers, with_sources=False):
    n = len(w_thru)
    src_at, dst_at = windows

    def body(*refs):
        w_refs, l_refs = refs[:n], refs[n:2 * n]
        send_sems, recv_sems = refs[2 * n], refs[2 * n + 1]
        x, y, c = _place()
        for j, (tag, dev, _, here) in enumerate(peers(x, y, c)):
            for i in range(n):
                cp = _rcopy(src_at(w_refs[i], i, c, tag), dst_at(l_refs[i], i, c, here), send_sems, recv_sems,
                            j * n + i, dev)
                cp.wait_send()
                cp.wait_recv()

    outs = pl.pallas_call(
        body, name=name,
        out_shape=[pltpu.HBM(w.shape, w.dtype) for w in w_thru] + [pltpu.HBM(t.shape, t.dtype) for t in l_thru],
        in_specs=[_HBM_ONLY] * (2 * n) + [_SEM, _SEM, pl.BlockSpec(memory_space=pl.ANY)],
        out_specs=[_HBM_ONLY] * (2 * n),
        input_output_aliases={i: i for i in range(2 * n)},
        compiler_params=pltpu.CompilerParams(has_side_effects=_DATAFLOW),
    )(*w_thru, *l_thru, send_sems, recv_sems, after)
    return (outs[:n], outs[n:]) if with_sources else outs[n:]


def _sibling_fill(lands):
    n = len(lands)
    halves = [t.shape[1] // 2 for t in lands]

    def body(*refs):
        o_refs = refs[n:2 * n]
        send_sems, recv_sems = refs[2 * n:]
        x, y, c = _place()
        sib = (x, y, 1 - c)
        chips = _other_chips(x, y)
        sent = []
        for j, (cx, cy) in enumerate(chips):
            for i in range(n):
                blk = o_refs[i].at[2 * cx + cy, pl.ds(c * halves[i], halves[i]), :]
                cp = _rcopy(blk, blk, send_sems, recv_sems, j * n + i, sib)
                cp.start()
                sent.append(cp)
        for j, (cx, cy) in enumerate(chips):
            for i in range(n):
                blk = o_refs[i].at[2 * cx + cy, pl.ds((1 - c) * halves[i], halves[i]), :]
                _rcopy(blk, blk, send_sems, recv_sems, j * n + i, sib).wait_recv()
        for cp in sent:
            cp.wait_send()

    return _comm_call(body, "sibling_fill", list(lands), [SDS(t.shape, t.dtype) for t in lands], 3 * n, landing=n)


def _swap_halves(gs, name):
    n = len(gs)
    halves = [g.shape[1] // 2 for g in gs]

    def body(*refs):
        g_refs, o_refs = refs[:n], refs[n:2 * n]
        send_sems, recv_sems = refs[2 * n:]
        x, y, c = _place()
        cps = [_rcopy(g_refs[i].at[:, pl.ds((1 - c) * halves[i], halves[i]), :], o_refs[i], send_sems, recv_sems, i,
                      (x, y, 1 - c)) for i in range(n)]
        for cp in cps:
            cp.start()
        for cp in cps:
            cp.wait()

    return _comm_call(body, name, gs, [SDS((N_CHIPS, h, g.shape[2]), g.dtype) for g, h in zip(gs, halves)], n)


def _swap_reduced(rs, name):
    n = len(rs)

    def body(*refs):
        r_refs, o_refs = refs[:n], refs[n:2 * n]
        send_sems, recv_sems = refs[2 * n:]
        x, y, c = _place()
        cps = [_rcopy(r_refs[i], o_refs[i], send_sems, recv_sems, i, (x, y, 1 - c)) for i in range(n)]
        for cp in cps:
            cp.start()
        for cp in cps:
            cp.wait()

    return _comm_call(body, name, rs, [SDS(r.shape, r.dtype) for r in rs], n)


def _all_sum_small(vec, name):
    n_dev = 8
    flips = [(bx, by, bc) for bx in (0, 1) for by in (0, 1) for bc in (0, 1)][1:]

    def body(v_ref, out_ref, gath, send_sems, recv_sems):
        x, y, c = _place()
        me = 4 * x + 2 * y + c
        gath[me] = v_ref[...]
        sent = []
        for k, (bx, by, bc) in enumerate(flips):
            peer = (x ^ bx, y ^ by, c ^ bc)
            cp = _rcopy(v_ref, gath.at[me], send_sems, recv_sems, k, peer)
            cp.start()
            sent.append(cp)
        for k, (bx, by, bc) in enumerate(flips):
            peer = (x ^ bx, y ^ by, c ^ bc)
            _rcopy(v_ref, gath.at[4 * peer[0] + 2 * peer[1] + peer[2]], send_sems, recv_sems, k, peer).wait_recv()
        for cp in sent:
            cp.wait_send()
        acc = gath[0]
        for d in range(1, n_dev):
            acc = acc + gath[d]
        out_ref[...] = acc

    vm = pl.BlockSpec(memory_space=pltpu.VMEM)
    return pl.pallas_call(
        body, name=name, in_specs=[vm], out_specs=vm, out_shape=SDS(vec.shape, F32),
        scratch_shapes=[pltpu.VMEM((n_dev,) + vec.shape, F32), pltpu.SemaphoreType.DMA((7,)),
                        pltpu.SemaphoreType.DMA((7,))],
        compiler_params=_cparams(has_side_effects=True),
    )(vec)


def _pack_small(vals, extra=None):
    parts = [vals[n].reshape(-1).astype(F32) for n, _ in _SMALL]
    parts.append(jnp.zeros((1,), F32) if extra is None else extra.reshape(1).astype(F32))
    flat = jnp.concatenate(parts)
    flat = jnp.concatenate([flat, jnp.zeros((_SMALL_ROWS * 128 - flat.shape[0],), F32)])
    return flat.reshape(_SMALL_ROWS, 128)


def _unpack_small(packed, shapes):
    flat = packed.reshape(-1)
    return {n: flat[_SMALL_OFF[n][0]:_SMALL_OFF[n][0] + _SMALL_OFF[n][1]].reshape(shapes[n]) for n, _ in _SMALL}


def _pair_sum(gs, gots, core, name):
    n = len(gs)

    def kern(c_ref, *refs):
        for i in range(n):
            refs[2 * n + i][...] = (refs[i][...].astype(F32) + refs[n + i][...].astype(F32)).astype(BF16)

    in_specs = [pl.BlockSpec((1, t.shape[1], t.shape[2]), lambda s, c_ref: (s, c_ref[0], 0)) for t in gots]
    in_specs += [pl.BlockSpec((1, t.shape[1], t.shape[2]), lambda s, c_ref: (s, 0, 0)) for t in gots]
    out_specs = [pl.BlockSpec((1, t.shape[1], t.shape[2]), lambda s, c_ref: (s, 0, 0)) for t in gots]
    return pl.pallas_call(
        kern, name=name,
        grid_spec=pltpu.PrefetchScalarGridSpec(num_scalar_prefetch=1, grid=(N_CHIPS,), in_specs=in_specs,
                                               out_specs=out_specs),
        out_shape=[SDS(t.shape, BF16) for t in gots],
        compiler_params=_cparams(dimension_semantics=("arbitrary",)),
    )(core.reshape(1).astype(jnp.int32), *gs, *gots)


def _chip_sum(qs, name):
    n = len(qs)

    def kern(*refs):
        for i in range(n):
            acc = refs[i][0].astype(F32)
            for s in range(1, N_CHIPS):
                acc = acc + refs[i][s].astype(F32)
            refs[n + i][...] = acc

    in_specs = [pl.BlockSpec((N_CHIPS, q.shape[1] // 2, q.shape[2]), lambda j: (0, j, 0)) for q in qs]
    out_specs = [pl.BlockSpec((q.shape[1] // 2, q.shape[2]), lambda j: (j, 0)) for q in qs]
    return _pc(kern, name, (2,), in_specs, out_specs, [SDS(q.shape[1:], F32) for q in qs])(*qs)


def _adam_math(w_, g_, m_, v_):
    m_ = ADAM_B1 * m_ + (1.0 - ADAM_B1) * g_
    v_ = ADAM_B2 * v_ + (1.0 - ADAM_B2) * jnp.square(g_)
    m_hat = m_ / (1.0 - ADAM_B1 ** ADAM_STEP)
    v_hat = v_ / (1.0 - ADAM_B2 ** ADAM_STEP)
    return -ADAM_LR * (m_hat / (jnp.sqrt(v_hat) + ADAM_EPS) + ADAM_WD * w_), m_, v_


def _adamw(w, g, m, v, name):
    rows, cols = w.shape
    tr = rows
    for cand in (256, 128, 64, 32, 16, 8):
        if rows % cand == 0 and rows > cand:
            tr = cand
            break

    def kern(w_ref, g_ref, m_ref, v_ref, d_ref, nm_ref, nv_ref):
        d_ref[...], nm_ref[...], nv_ref[...] = _adam_math(w_ref[...], g_ref[...], m_ref[...], v_ref[...])

    spec = pl.BlockSpec((tr, cols), lambda i: (i, 0))
    return _pc(kern, name, (rows // tr,), [spec] * 4, [spec] * 3, [SDS(w.shape, F32)] * 3)(w, g, m, v)


def _adamw_rows1(w, g, m, v, name):
    rows, _, cols = w.shape
    tr = next(t for t in (203, 174, 128, 64, 42, 32, 29, 16, 8, 7, 6, 4, 3, 2, 1) if rows % t == 0)

    def kern(w_ref, g_ref, m_ref, v_ref, go_ref, d_ref, nm_ref, nv_ref):
        g_ = g_ref[...]
        go_ref[...] = g_
        d_ref[...], nm_ref[...], nv_ref[...] = _adam_math(w_ref[...], g_, m_ref[...], v_ref[...])

    spec = pl.BlockSpec((tr, 1, cols), lambda i: (i, 0, 0))
    return _pc(kern, name, (rows // tr,), [spec] * 4, [spec] * 4, [SDS(w.shape, F32)] * 4)(w, g, m, v)


def _adamw_big(w, mine, theirs, m, v, core, name):
    _, rows, cols = w.shape
    half = rows // 2
    tr = next(t for t in (256, 176, 128, 64, 32, 16, 8) if half % t == 0)
    nbh = half // tr

    def kern(c_ref, w_ref, a_ref, b_ref, m_ref, v_ref, g_ref, d_ref, nm_ref, nv_ref):
        g_ = jnp.where(pl.program_id(0) // nbh == c_ref[0], a_ref[...], b_ref[...])
        g_ref[0] = g_
        d_ref[0], nm_ref[0], nv_ref[0] = _adam_math(w_ref[0], g_, m_ref[0], v_ref[0])

    full = pl.BlockSpec((1, tr, cols), lambda i, c_ref: (0, i, 0))
    part = pl.BlockSpec((tr, cols), lambda i, c_ref: (i % nbh, 0))
    return pl.pallas_call(
        kern, name=name,
        grid_spec=pltpu.PrefetchScalarGridSpec(num_scalar_prefetch=1, grid=(rows // tr,),
                                               in_specs=[full, part, part, full, full], out_specs=[full] * 4),
        out_shape=[SDS(w.shape, F32)] * 4,
        compiler_params=_cparams(dimension_semantics=("arbitrary",)),
    )(core.reshape(1).astype(jnp.int32), w, mine, theirs, m, v)


_WEIGHT_NAMES = ("attn_norm", "w_in", "dn_conv", "dn_a_log", "dn_dt_bias", "dn_out_norm", "swa_q_norm", "swa_k_norm",
                 "swa_sinks", "rel_bias", "w_branch_dn", "w_branch_swa", "w_out", "ffn_norm", "w_gate", "w_up",
                 "w_down")
_CONV_SH = QKVW // N_CHIPS


def kernel(x, attn_norm, w_in, dn_conv, dn_a_log, dn_dt_bias, dn_out_norm, swa_q_norm, swa_k_norm, swa_sinks, rel_bias, w_branch_dn, w_branch_swa, w_out, ffn_norm, w_gate, w_up, w_down, loss_target, m_attn_norm, m_w_in, m_dn_conv, m_dn_a_log, m_dn_dt_bias, m_dn_out_norm, m_swa_q_norm, m_swa_k_norm, m_swa_sinks, m_rel_bias, m_w_branch_dn, m_w_branch_swa, m_w_out, m_ffn_norm, m_w_gate, m_w_up, m_w_down, v_attn_norm, v_w_in, v_dn_conv, v_dn_a_log, v_dn_dt_bias, v_dn_out_norm, v_swa_q_norm, v_swa_k_norm, v_swa_sinks, v_rel_bias, v_w_branch_dn, v_w_branch_swa, v_w_out, v_ffn_norm, v_w_gate, v_w_up, v_w_down):
    w = dict(attn_norm=attn_norm, w_in=w_in, dn_conv=dn_conv, dn_a_log=dn_a_log, dn_dt_bias=dn_dt_bias,
             dn_out_norm=dn_out_norm, swa_q_norm=swa_q_norm, swa_k_norm=swa_k_norm, swa_sinks=swa_sinks,
             rel_bias=rel_bias, w_branch_dn=w_branch_dn, w_branch_swa=w_branch_swa, w_out=w_out, ffn_norm=ffn_norm,
             w_gate=w_gate, w_up=w_up, w_down=w_down)
    m = dict(attn_norm=m_attn_norm, w_in=m_w_in, dn_conv=m_dn_conv, dn_a_log=m_dn_a_log, dn_dt_bias=m_dn_dt_bias,
             dn_out_norm=m_dn_out_norm, swa_q_norm=m_swa_q_norm, swa_k_norm=m_swa_k_norm, swa_sinks=m_swa_sinks,
             rel_bias=m_rel_bias, w_branch_dn=m_w_branch_dn, w_branch_swa=m_w_branch_swa, w_out=m_w_out,
             ffn_norm=m_ffn_norm, w_gate=m_w_gate, w_up=m_w_up, w_down=m_w_down)
    v = dict(attn_norm=v_attn_norm, w_in=v_w_in, dn_conv=v_dn_conv, dn_a_log=v_dn_a_log, dn_dt_bias=v_dn_dt_bias,
             dn_out_norm=v_dn_out_norm, swa_q_norm=v_swa_q_norm, swa_k_norm=v_swa_k_norm, swa_sinks=v_swa_sinks,
             rel_bias=v_rel_bias, w_branch_dn=v_w_branch_dn, w_branch_swa=v_w_branch_swa, w_out=v_w_out,
             ffn_norm=v_ffn_norm, w_gate=v_w_gate, w_up=v_w_up, w_down=v_w_down)
    shapes = {n: w[n].shape for n in _WEIGHT_NAMES}

    def two_d(a):
        return a.reshape(a.shape[-2], a.shape[-1]) if a.ndim == 3 else a

    core = lax.axis_index("c")
    chip = 2 * lax.axis_index("x") + lax.axis_index("y")
    small_shapes = {n: two_d(w[n]).shape for n, _ in _SMALL}
    small_shapes["dn_conv"] = (CONV, QKVW)

    conv_loc = two_d(w["dn_conv"])
    conv_part = lax.dynamic_update_slice(jnp.zeros((CONV, QKVW), F32), jnp.where(core == 0, conv_loc, 0.0),
                                         (0, chip * _CONV_SH))
    conv_full = _all_sum_small(conv_part.reshape(CONV * QKVW // 128, 128), "gather_conv").reshape(CONV, QKVW)

    flipped = ("w_gate", "w_up")

    def natural(a, n):
        return a.transpose(0, 2, 1) if n in flipped else a

    w_bf = [two_d(natural(w[n], n).astype(BF16)) for n in _BIG_NAMES]
    (w_in_g,) = _gather_weights(w_bf[:1], chip)
    windows = _gather_windows(w_bf[1:])
    after_sync = w_in_g[0, :8, :128].astype(F32) + conv_full[0:1, :128]
    send_sems, recv_sems, w_thru, l_thru, token = _split_start(
        "gather_start", w_bf[1:], _own_slot(w_bf[1:], chip), after_sync, windows)

    def late(after):
        lands = _split_wait("gather_wait", w_thru, l_thru, send_sems, recv_sems, after, windows)
        g = dict(zip(_BIG_NAMES[1:], _sibling_fill(lands)))
        return dict(wa=g["w_branch_dn"], wb=g["w_branch_swa"], w_out=g["w_out"].reshape(D, D), wg=g["w_gate"],
                    wu=g["w_up"], wd=g["w_down"])

    wts = dict(w_in_p=_w_in_to_padded(w_in_g), dn_conv=conv_full, late=late)
    for n, _ in _SMALL[:-1]:
        wts[n] = two_d(w[n])
    wts["attn_norm"] = wts["attn_norm"] + token[0:1, 0:1]

    early = {}

    ffn = {}

    def send_ffn(grads):
        gs = [grads["w_gate"], grads["w_up"], grads["w_down"]]
        lands = [lax.empty((N_CHIPS, g.shape[1] // 2, g.shape[2]), g.dtype) for g in gs]
        ffn["sems"], ffn["recv"], ffn["src"], ffn["land"], tok = _split_start(
            "swap_ffn_start", gs, lands, gs[0][0, :8, :128], _swap_windows(gs), _sibling_peer, 1)
        return tok

    def send_early(grads):
        small = [grads["w_branch_dn"], grads["w_branch_swa"], grads["w_out"].reshape(N_CHIPS, CSH, D)]
        big = [grads["w_gate"], grads["w_up"], grads["w_down"]]
        big, got_big = _split_wait("swap_ffn_wait", ffn["src"], ffn["land"], ffn["sems"], ffn["recv"], small[0],
                                   _swap_windows(big), _sibling_peer, with_sources=True)
        gots = list(_swap_halves(small, "swap_halves_early")) + list(got_big)
        parts = _pair_sum(small + list(big), gots, core, "pair_sum_early")
        own = [lax.dynamic_index_in_dim(p, chip, axis=0, keepdims=False) for p in parts]
        early["sems"], early["recv"], early["src"], early["land"], tok = _split_start(
            "exchange_start", parts, _own_slot(own, chip), parts[0][0, :8, :128], _exchange_windows())
        return tok

    last = {}

    def send_in(g_in_p):
        g_in = [_padded_to_w_in(g_in_p)]
        parts = _pair_sum(g_in, _swap_halves(g_in, "swap_halves_in"), core, "pair_sum_in")
        own = [lax.dynamic_index_in_dim(p, chip, axis=0, keepdims=False) for p in parts]
        last["sems"], last["recv"], last["src"], last["land"], tok = _split_start(
            "exchange_in_start", parts, _own_slot(own, chip), parts[0][0, :8, :128], _exchange_windows())
        return tok

    wts["send_ffn"] = send_ffn
    wts["send_early"] = send_early
    wts["send_in"] = send_in
    loss_sum, grad_x, grads = _local_step(x[0], loss_target[0], wts)

    small_sum = _all_sum_small(_pack_small(grads, loss_sum), "all_sum_small")
    loss = small_sum.reshape(-1)[_LOSS_OFF]
    g_small = _unpack_small(small_sum, small_shapes)

    q_early = _split_wait("exchange_wait", early["src"], early["land"], early["sems"], early["recv"], small_sum,
                          _exchange_windows())
    red_early = _chip_sum(list(q_early), "chip_sum_early")
    their_early = _swap_reduced(red_early, "swap_reduced_early")
    g_out, d_out, m_out, v_out = {}, {}, {}, {}
    for n, mine, other in zip(_BIG_NAMES[1:], red_early, their_early):
        res = _adamw_big(natural(w[n], n), mine, other, natural(m[n], n), natural(v[n], n), core, "adamw_" + n)
        g_out[n], d_out[n], m_out[n], v_out[n] = (natural(t, n) for t in res)

    q_in = _split_wait("exchange_in_wait", last["src"], last["land"], last["sems"], last["recv"],
                       d_out[_BIG_NAMES[-1]], _exchange_windows())
    reduced = _chip_sum(list(q_in), "chip_sum_in")
    theirs = _swap_reduced(reduced, "swap_reduced_in")

    def rows1(a):
        return a.transpose(2, 0, 1)

    def unrows1(a):
        return a.transpose(1, 2, 0)

    g_in_blk = jnp.concatenate([jnp.where(core == 0, reduced[0], theirs[0]),
                                jnp.where(core == 0, theirs[0], reduced[0])], axis=0)
    g_in_r = rows1(g_in_blk[None])
    res = _adamw_rows1(rows1(w["w_in"]), g_in_r, rows1(m["w_in"]), rows1(v["w_in"]), "adamw_w_in")
    g_out["w_in"], d_out["w_in"], m_out["w_in"], v_out["w_in"] = (unrows1(t) for t in res)
    g_conv = lax.dynamic_slice(g_small["dn_conv"], (0, chip * _CONV_SH), (CONV, _CONV_SH))
    g_out["dn_conv"] = g_conv.reshape(shapes["dn_conv"])
    d_, m_, v_ = _adamw(conv_loc, g_conv, two_d(m["dn_conv"]), two_d(v["dn_conv"]), "adamw_dn_conv")
    d_out["dn_conv"], m_out["dn_conv"], v_out["dn_conv"] = (t.reshape(shapes["dn_conv"]) for t in (d_, m_, v_))

    def packed(src):
        vals = {n: src[n] for n, _ in _SMALL[:-1]}
        vals["dn_conv"] = jnp.zeros((CONV * QKVW,), F32)
        return _pack_small(vals)

    d_s, m_s, v_s = _adamw(packed(w), small_sum, packed(m), packed(v), "adamw_small")
    d_small, m_small, v_small = (_unpack_small(t, small_shapes) for t in (d_s, m_s, v_s))
    for n, _ in _SMALL[:-1]:
        g_out[n] = g_small[n].reshape(shapes[n])
        d_out[n], m_out[n], v_out[n] = (t[n].reshape(shapes[n]) for t in (d_small, m_small, v_small))

    return (loss, grad_x[None], *[g_out[n] for n in _WEIGHT_NAMES], *[d_out[n] for n in _WEIGHT_NAMES],
            *[m_out[n] for n in _WEIGHT_NAMES], *[v_out[n] for n in _WEIGHT_NAMES])
```
